```python
import math
import jax, jax.numpy as jnp
from jax import lax
import numpy as np

D_MODEL = 1024
BATCH = 8
SEQ = 8192
DEPTH = 1

CHUNK = 64
Q_BLOCK = 128
EPS = 1e-6
D_FF = 2816
N_MOD = 9

GDN_HEADS = 4
GDN_DK = 128
GDN_DV = 128
CONV_K = 4

MLA_HEADS = 4
MLA_NOPE = 128
MLA_ROPE = 64
MLA_V = 128
MLA_Q_LORA = 384
MLA_KV_LORA = 256
ROPE_BASE = 10000.0

GDN_WIDTH = GDN_HEADS * GDN_DV
MLA_WIDTH = MLA_HEADS * MLA_V
MIX_WIDTH = GDN_WIDTH + MLA_WIDTH

IN_SPLITS = (GDN_HEADS * GDN_DK,
             GDN_HEADS * GDN_DK,
             GDN_WIDTH,
             GDN_WIDTH,
             GDN_HEADS,
             GDN_HEADS,
             MLA_Q_LORA,
             MLA_KV_LORA,
             MLA_ROPE)
N_IN = sum(IN_SPLITS)
IN_OFFSETS = tuple(int(o) for o in np.cumsum(IN_SPLITS)[:-1])

kernel_name = "hybrid_gdn_mla_macaron_adaln_block"


def _rms(x, w=None):
    xf = x.astype(jnp.float32)
    y = xf * lax.rsqrt(jnp.mean(xf * xf, axis=-1, keepdims=True) + EPS)
    if w is not None:
        y = y * w.astype(jnp.float32)
    return y.astype(x.dtype)


def _l2n(x):
    return x * lax.rsqrt(jnp.sum(x * x, axis=-1, keepdims=True) + EPS)


def _modulate(x, shift, scale):
    return _rms(x) * (1.0 + scale[:, None, :]) + shift[:, None, :]


def _swiglu(h, w_in, w_out):
    gate, up = jnp.split(h @ w_in, 2, axis=-1)
    return (jax.nn.silu(gate) * up) @ w_out


def _rope(x, cos, sin):
    x1, x2 = jnp.split(x, 2, axis=-1)
    return jnp.concatenate([x1 * cos - x2 * sin, x2 * cos + x1 * sin], axis=-1)


def _causal_conv(x, w):
    return lax.conv_general_dilated(
        x, w[:, None, :].astype(x.dtype), window_strides=(1,),
        padding=[(CONV_K - 1, 0)], dimension_numbers=("NWC", "WIO", "NWC"),
        feature_group_count=x.shape[-1])


def _gated_delta_rule(q, k, v, g, beta):
    B, S, H, _ = q.shape
    nc = S // CHUNK

    def to_chunks(t):
        return t.reshape(B, nc, CHUNK, H, t.shape[-1]).transpose(0, 3, 1, 2, 4)

    q, k, v = to_chunks(q), to_chunks(k), to_chunks(v)
    g = g.reshape(B, nc, CHUNK, H).transpose(0, 3, 1, 2)
    beta = beta.reshape(B, nc, CHUNK, H).transpose(0, 3, 1, 2)

    G = jnp.cumsum(g, axis=-1)
    idx = jnp.arange(CHUNK)
    incl = idx[:, None] >= idx[None, :]
    strict = idx[:, None] > idx[None, :]
    decay = jnp.exp(jnp.where(incl, G[..., :, None] - G[..., None, :], -jnp.inf))

    kk = jnp.einsum('bhncd,bhnsd->bhncs', k, k)
    A = jnp.where(strict, beta[..., :, None] * kk * decay, 0.0)
    M = A + jnp.eye(CHUNK, dtype=A.dtype)
    rhs = jnp.concatenate([v * beta[..., None], k * (beta * jnp.exp(G))[..., None]], axis=-1)
    W = lax.linalg.triangular_solve(M, rhs, left_side=True, lower=True, unit_diagonal=True)
    u, wk = W[..., :GDN_DV], W[..., GDN_DV:]

    qk = jnp.einsum('bhncd,bhnsd->bhncs', q, k) * decay
    q_dec = q * jnp.exp(G)[..., None]
    k_dec = k * jnp.exp(G[..., -1:] - G)[..., None]
    g_last = jnp.exp(G[..., -1])

    xs = tuple(jnp.moveaxis(t, 2, 0) for t in (u, wk, q_dec, k_dec, qk, g_last))

    def step(state, inp):
        u_c, wk_c, qd_c, kd_c, qk_c, gl_c = inp
        v_new = u_c - jnp.einsum('bhck,bhkv->bhcv', wk_c, state)
        o_c = jnp.einsum('bhck,bhkv->bhcv', qd_c, state) + jnp.einsum('bhcs,bhsv->bhcv', qk_c, v_new)
        state = state * gl_c[..., None, None] + jnp.einsum('bhck,bhcv->bhkv', kd_c, v_new)
        return state, o_c

    s0 = jnp.zeros((B, H, GDN_DK, GDN_DV), jnp.float32)
    _, o = lax.scan(step, s0, xs)
    return o.transpose(1, 0, 3, 2, 4).reshape(B, S, H, GDN_DV)


def _hybrid_mixer(h, cos, sin, w_in, gdn_conv_w, gdn_a_log, gdn_dt_bias, gdn_norm_w,
                  mla_q_norm_w, mla_w_uq, mla_kv_norm_w, mla_w_ukv,
                  qkn_q_nope, qkn_q_rope, qkn_k_nope, qkn_k_rope, mla_out_norm_w, w_out):
    B, S, _ = h.shape
    nb = S // Q_BLOCK
    proj = h @ w_in
    gq, gk, gv, gz, ga, gb, cq, ckv, kr = jnp.split(proj, IN_OFFSETS, axis=-1)

    qkv = jax.nn.silu(_causal_conv(jnp.concatenate([gq, gk, gv], axis=-1), gdn_conv_w))
    q_a = qkv[..., :GDN_HEADS * GDN_DK].reshape(B, S, GDN_HEADS, GDN_DK).astype(jnp.float32)
    k_a = qkv[..., GDN_HEADS * GDN_DK:2 * GDN_HEADS * GDN_DK].reshape(B, S, GDN_HEADS, GDN_DK).astype(jnp.float32)
    v_a = qkv[..., 2 * GDN_HEADS * GDN_DK:].reshape(B, S, GDN_HEADS, GDN_DV).astype(jnp.float32)
    q_a = _l2n(q_a) * (GDN_DK ** -0.5)
    k_a = _l2n(k_a)
    beta = jax.nn.sigmoid(gb.astype(jnp.float32))
    g = -jnp.exp(gdn_a_log.astype(jnp.float32)) * jax.nn.softplus(
        ga.astype(jnp.float32) + gdn_dt_bias.astype(jnp.float32))
    o_a = _gated_delta_rule(q_a, k_a, v_a, g, beta).astype(h.dtype)
    o_a = _rms(o_a, gdn_norm_w) * jax.nn.silu(gz.reshape(B, S, GDN_HEADS, GDN_DV))

    qf = (_rms(cq, mla_q_norm_w) @ mla_w_uq).reshape(B, S, MLA_HEADS, MLA_NOPE + MLA_ROPE)
    kvf = (_rms(ckv, mla_kv_norm_w) @ mla_w_ukv).reshape(B, S, MLA_HEADS, MLA_NOPE + MLA_V)
    q_nope, q_rope = qf[..., :MLA_NOPE], qf[..., MLA_NOPE:]
    k_nope, v_b = kvf[..., :MLA_NOPE], kvf[..., MLA_NOPE:]
    scale = (MLA_NOPE + MLA_ROPE) ** -0.5
    q_nope = _rms(q_nope, qkn_q_nope) * scale
    q_rope = _rope(_rms(q_rope, qkn_q_rope), cos[:, :, None], sin[:, :, None]) * scale
    k_nope = _rms(k_nope, qkn_k_nope)
    k_rope = _rope(_rms(kr, qkn_k_rope), cos, sin)

    k_chunk = jnp.arange(S) // CHUNK
    q_chunk_b = k_chunk.reshape(nb, Q_BLOCK)
    qn_b = q_nope.reshape(B, nb, Q_BLOCK, MLA_HEADS, MLA_NOPE).transpose(1, 0, 2, 3, 4)
    qr_b = q_rope.reshape(B, nb, Q_BLOCK, MLA_HEADS, MLA_ROPE).transpose(1, 0, 2, 3, 4)

    def attend(blk):
        qn, qr, qc = blk
        s = (jnp.einsum('bqhd,bkhd->bhqk', qn, k_nope)
             + jnp.einsum('bqhd,bkd->bhqk', qr, k_rope)).astype(jnp.float32)
        s = jnp.where(qc[:, None] >= k_chunk[None, :], s, -jnp.inf)
        p = jax.nn.softmax(s, axis=-1).astype(v_b.dtype)
        return jnp.einsum('bhqk,bkhd->bqhd', p, v_b)

    o_b = lax.map(attend, (qn_b, qr_b, q_chunk_b))
    o_b = o_b.transpose(1, 0, 2, 3, 4).reshape(B, S, MLA_HEADS, MLA_V)
    o_b = _rms(o_b, mla_out_norm_w)

    mixed = jnp.concatenate([o_a.reshape(B, S, GDN_WIDTH), o_b.reshape(B, S, MLA_WIDTH)], axis=-1)
    return mixed @ w_out


def _fwd_setup_inputs(seed: int = 0) -> dict:
    key = jax.random.key(seed)
    ks = jax.random.split(key, 32)
    f32 = jnp.float32
    L = DEPTH

    def nrm(k, shape, fan_in, mult=1.0):
        return jax.random.normal(k, shape, f32) * (mult * fan_in ** -0.5)

    def gain(k, shape):
        return 1.0 + 0.02 * jax.random.normal(k, shape, f32)

    x = jax.random.normal(ks[0], (BATCH, SEQ, D_MODEL), f32)
    c = jax.random.normal(ks[1], (BATCH, D_MODEL), f32)
    offset = jax.random.randint(ks[2], (BATCH, 1), 0, 4096, dtype=jnp.int32)
    positions = (offset + jnp.arange(SEQ, dtype=jnp.int32)[None, :]).astype(jnp.int32)

    dt = jnp.exp(jax.random.uniform(ks[10], (L, GDN_HEADS), f32, math.log(1e-3), math.log(1e-1)))
    return {
        "x": x,
        "c": c,
        "positions": positions,
        "w_ada": nrm(ks[3], (L, D_MODEL, N_MOD * D_MODEL), D_MODEL, 0.5),
        "b_ada": 0.02 * jax.random.normal(ks[4], (L, N_MOD * D_MODEL), f32),
        "ffn1_w_in": nrm(ks[5], (L, D_MODEL, 2 * D_FF), D_MODEL),
        "ffn1_w_out": nrm(ks[6], (L, D_FF, D_MODEL), D_FF),
        "w_in": nrm(ks[7], (L, D_MODEL, N_IN), D_MODEL),
        "gdn_conv_w": nrm(ks[8], (L, CONV_K, 3 * GDN_WIDTH), CONV_K),
        "gdn_a_log": jnp.log(jax.random.uniform(ks[9], (L, GDN_HEADS), f32, 1.0, 16.0)),
        "gdn_dt_bias": dt + jnp.log(-jnp.expm1(-dt)),
        "gdn_norm_w": gain(ks[11], (L, GDN_DV)),
        "mla_q_norm_w": gain(ks[12], (L, MLA_Q_LORA)),
        "mla_w_uq": nrm(ks[13], (L, MLA_Q_LORA, MLA_HEADS * (MLA_NOPE + MLA_ROPE)), MLA_Q_LORA),
        "mla_kv_norm_w": gain(ks[14], (L, MLA_KV_LORA)),
        "mla_w_ukv": nrm(ks[15], (L, MLA_KV_LORA, MLA_HEADS * (MLA_NOPE + MLA_V)), MLA_KV_LORA),
        "qkn_q_nope": gain(ks[16], (L, MLA_NOPE)),
        "qkn_q_rope": gain(ks[17], (L, MLA_ROPE)),
        "qkn_k_nope": gain(ks[18], (L, MLA_NOPE)),
        "qkn_k_rope": gain(ks[19], (L, MLA_ROPE)),
        "mla_out_norm_w": gain(ks[20], (L, MLA_V)),
        "w_out": nrm(ks[21], (L, MIX_WIDTH, D_MODEL), MIX_WIDTH),
        "ffn2_w_in": nrm(ks[22], (L, D_MODEL, 2 * D_FF), D_MODEL),
        "ffn2_w_out": nrm(ks[23], (L, D_FF, D_MODEL), D_FF),
    }


def _fwd_reference(x, c, positions, w_ada, b_ada, ffn1_w_in, ffn1_w_out, w_in, gdn_conv_w,
              gdn_a_log, gdn_dt_bias, gdn_norm_w, mla_q_norm_w, mla_w_uq, mla_kv_norm_w,
              mla_w_ukv, qkn_q_nope, qkn_q_rope, qkn_k_nope, qkn_k_rope, mla_out_norm_w,
              w_out, ffn2_w_in, ffn2_w_out):
    half = MLA_ROPE // 2
    inv_freq = ROPE_BASE ** (-jnp.arange(half, dtype=jnp.float32) / half)
    ang = positions.astype(jnp.float32)[..., None] * inv_freq
    cos = jnp.cos(ang).astype(x.dtype)
    sin = jnp.sin(ang).astype(x.dtype)
    sc = jax.nn.silu(c)

    for l in range(DEPTH):
        mod = sc @ w_ada[l] + b_ada[l]
        sh1, s1, g1, sh2, s2, g2, sh3, s3, g3 = jnp.split(mod, N_MOD, axis=-1)
        h = _modulate(x, sh1, s1)
        x = x + 0.5 * g1[:, None, :] * _swiglu(h, ffn1_w_in[l], ffn1_w_out[l])
        h = _modulate(x, sh2, s2)
        y = _hybrid_mixer(h, cos, sin, w_in[l], gdn_conv_w[l], gdn_a_log[l], gdn_dt_bias[l],
                          gdn_norm_w[l], mla_q_norm_w[l], mla_w_uq[l], mla_kv_norm_w[l],
                          mla_w_ukv[l], qkn_q_nope[l], qkn_q_rope[l], qkn_k_nope[l],
                          qkn_k_rope[l], mla_out_norm_w[l], w_out[l])
        x = x + g2[:, None, :] * y
        h = _modulate(x, sh3, s3)
        x = x + 0.5 * g3[:, None, :] * _swiglu(h, ffn2_w_in[l], ffn2_w_out[l])
    return x


import jax as _jax
import jax.numpy as _jnp

TWIN_FORMAT = 'train_step'
FWD_PARAMS = ['x', 'c', 'positions', 'w_ada', 'b_ada', 'ffn1_w_in', 'ffn1_w_out', 'w_in', 'gdn_conv_w', 'gdn_a_log', 'gdn_dt_bias', 'gdn_norm_w', 'mla_q_norm_w', 'mla_w_uq', 'mla_kv_norm_w', 'mla_w_ukv', 'qkn_q_nope', 'qkn_q_rope', 'qkn_k_nope', 'qkn_k_rope', 'mla_out_norm_w', 'w_out', 'ffn2_w_in', 'ffn2_w_out']
TWIN_WEIGHTS = ['w_ada', 'b_ada', 'ffn1_w_in', 'ffn1_w_out', 'w_in', 'gdn_conv_w', 'gdn_a_log', 'gdn_dt_bias', 'gdn_norm_w', 'mla_q_norm_w', 'mla_w_uq', 'mla_kv_norm_w', 'mla_w_ukv', 'qkn_q_nope', 'qkn_q_rope', 'qkn_k_nope', 'qkn_k_rope', 'mla_out_norm_w', 'w_out', 'ffn2_w_in', 'ffn2_w_out']
TWIN_DIFF_INPUT = 'x'
TWIN_INPUTS = ['x', 'c', 'positions', 'w_ada', 'b_ada', 'ffn1_w_in', 'ffn1_w_out', 'w_in', 'gdn_conv_w', 'gdn_a_log', 'gdn_dt_bias', 'gdn_norm_w', 'mla_q_norm_w', 'mla_w_uq', 'mla_kv_norm_w', 'mla_w_ukv', 'qkn_q_nope', 'qkn_q_rope', 'qkn_k_nope', 'qkn_k_rope', 'mla_out_norm_w', 'w_out', 'ffn2_w_in', 'ffn2_w_out', 'loss_target', 'm_w_ada', 'm_b_ada', 'm_ffn1_w_in', 'm_ffn1_w_out', 'm_w_in', 'm_gdn_conv_w', 'm_gdn_a_log', 'm_gdn_dt_bias', 'm_gdn_norm_w', 'm_mla_q_norm_w', 'm_mla_w_uq', 'm_mla_kv_norm_w', 'm_mla_w_ukv', 'm_qkn_q_nope', 'm_qkn_q_rope', 'm_qkn_k_nope', 'm_qkn_k_rope', 'm_mla_out_norm_w', 'm_w_out', 'm_ffn2_w_in', 'm_ffn2_w_out', 'v_w_ada', 'v_b_ada', 'v_ffn1_w_in', 'v_ffn1_w_out', 'v_w_in', 'v_gdn_conv_w', 'v_gdn_a_log', 'v_gdn_dt_bias', 'v_gdn_norm_w', 'v_mla_q_norm_w', 'v_mla_w_uq', 'v_mla_kv_norm_w', 'v_mla_w_ukv', 'v_qkn_q_nope', 'v_qkn_q_rope', 'v_qkn_k_nope', 'v_qkn_k_rope', 'v_mla_out_norm_w', 'v_w_out', 'v_ffn2_w_in', 'v_ffn2_w_out']
TWIN_OUTPUTS = ['loss', 'grad_x', 'grad_w_ada', 'grad_b_ada', 'grad_ffn1_w_in', 'grad_ffn1_w_out', 'grad_w_in', 'grad_gdn_conv_w', 'grad_gdn_a_log', 'grad_gdn_dt_bias', 'grad_gdn_norm_w', 'grad_mla_q_norm_w', 'grad_mla_w_uq', 'grad_mla_kv_norm_w', 'grad_mla_w_ukv', 'grad_qkn_q_nope', 'grad_qkn_q_rope', 'grad_qkn_k_nope', 'grad_qkn_k_rope', 'grad_mla_out_norm_w', 'grad_w_out', 'grad_ffn2_w_in', 'grad_ffn2_w_out', 'delta_w_ada', 'delta_b_ada', 'delta_ffn1_w_in', 'delta_ffn1_w_out', 'delta_w_in', 'delta_gdn_conv_w', 'delta_gdn_a_log', 'delta_gdn_dt_bias', 'delta_gdn_norm_w', 'delta_mla_q_norm_w', 'delta_mla_w_uq', 'delta_mla_kv_norm_w', 'delta_mla_w_ukv', 'delta_qkn_q_nope', 'delta_qkn_q_rope', 'delta_qkn_k_nope', 'delta_qkn_k_rope', 'delta_mla_out_norm_w', 'delta_w_out', 'delta_ffn2_w_in', 'delta_ffn2_w_out', 'new_m_w_ada', 'new_m_b_ada', 'new_m_ffn1_w_in', 'new_m_ffn1_w_out', 'new_m_w_in', 'new_m_gdn_conv_w', 'new_m_gdn_a_log', 'new_m_gdn_dt_bias', 'new_m_gdn_norm_w', 'new_m_mla_q_norm_w', 'new_m_mla_w_uq', 'new_m_mla_kv_norm_w', 'new_m_mla_w_ukv', 'new_m_qkn_q_nope', 'new_m_qkn_q_rope', 'new_m_qkn_k_nope', 'new_m_qkn_k_rope', 'new_m_mla_out_norm_w', 'new_m_w_out', 'new_m_ffn2_w_in', 'new_m_ffn2_w_out', 'new_v_w_ada', 'new_v_b_ada', 'new_v_ffn1_w_in', 'new_v_ffn1_w_out', 'new_v_w_in', 'new_v_gdn_conv_w', 'new_v_gdn_a_log', 'new_v_gdn_dt_bias', 'new_v_gdn_norm_w', 'new_v_mla_q_norm_w', 'new_v_mla_w_uq', 'new_v_mla_kv_norm_w', 'new_v_mla_w_ukv', 'new_v_qkn_q_nope', 'new_v_qkn_q_rope', 'new_v_qkn_k_nope', 'new_v_qkn_k_rope', 'new_v_mla_out_norm_w', 'new_v_w_out', 'new_v_ffn2_w_in', 'new_v_ffn2_w_out']
TWIN_LEAF_KINDS = {'loss': 'loss', 'grad_x': 'grad_x', 'grad_w_ada': 'grad_w', 'grad_b_ada': 'grad_w', 'grad_ffn1_w_in': 'grad_w', 'grad_ffn1_w_out': 'grad_w', 'grad_w_in': 'grad_w', 'grad_gdn_conv_w': 'grad_w', 'grad_gdn_a_log': 'grad_w', 'grad_gdn_dt_bias': 'grad_w', 'grad_gdn_norm_w': 'grad_w', 'grad_mla_q_norm_w': 'grad_w', 'grad_mla_w_uq': 'grad_w', 'grad_mla_kv_norm_w': 'grad_w', 'grad_mla_w_ukv': 'grad_w', 'grad_qkn_q_nope': 'grad_w', 'grad_qkn_q_rope': 'grad_w', 'grad_qkn_k_nope': 'grad_w', 'grad_qkn_k_rope': 'grad_w', 'grad_mla_out_norm_w': 'grad_w', 'grad_w_out': 'grad_w', 'grad_ffn2_w_in': 'grad_w', 'grad_ffn2_w_out': 'grad_w', 'delta_w_ada': 'delta_w', 'delta_b_ada': 'delta_w', 'delta_ffn1_w_in': 'delta_w', 'delta_ffn1_w_out': 'delta_w', 'delta_w_in': 'delta_w', 'delta_gdn_conv_w': 'delta_w', 'delta_gdn_a_log': 'delta_w', 'delta_gdn_dt_bias': 'delta_w', 'delta_gdn_norm_w': 'delta_w', 'delta_mla_q_norm_w': 'delta_w', 'delta_mla_w_uq': 'delta_w', 'delta_mla_kv_norm_w': 'delta_w', 'delta_mla_w_ukv': 'delta_w', 'delta_qkn_q_nope': 'delta_w', 'delta_qkn_q_rope': 'delta_w', 'delta_qkn_k_nope': 'delta_w', 'delta_qkn_k_rope': 'delta_w', 'delta_mla_out_norm_w': 'delta_w', 'delta_w_out': 'delta_w', 'delta_ffn2_w_in': 'delta_w', 'delta_ffn2_w_out': 'delta_w', 'new_m_w_ada': 'new_m', 'new_m_b_ada': 'new_m', 'new_m_ffn1_w_in': 'new_m', 'new_m_ffn1_w_out': 'new_m', 'new_m_w_in': 'new_m', 'new_m_gdn_conv_w': 'new_m', 'new_m_gdn_a_log': 'new_m', 'new_m_gdn_dt_bias': 'new_m', 'new_m_gdn_norm_w': 'new_m', 'new_m_mla_q_norm_w': 'new_m', 'new_m_mla_w_uq': 'new_m', 'new_m_mla_kv_norm_w': 'new_m', 'new_m_mla_w_ukv': 'new_m', 'new_m_qkn_q_nope': 'new_m', 'new_m_qkn_q_rope': 'new_m', 'new_m_qkn_k_nope': 'new_m', 'new_m_qkn_k_rope': 'new_m', 'new_m_mla_out_norm_w': 'new_m', 'new_m_w_out': 'new_m', 'new_m_ffn2_w_in': 'new_m', 'new_m_ffn2_w_out': 'new_m', 'new_v_w_ada': 'new_v', 'new_v_b_ada': 'new_v', 'new_v_ffn1_w_in': 'new_v', 'new_v_ffn1_w_out': 'new_v', 'new_v_w_in': 'new_v', 'new_v_gdn_conv_w': 'new_v', 'new_v_gdn_a_log': 'new_v', 'new_v_gdn_dt_bias': 'new_v', 'new_v_gdn_norm_w': 'new_v', 'new_v_mla_q_norm_w': 'new_v', 'new_v_mla_w_uq': 'new_v', 'new_v_mla_kv_norm_w': 'new_v', 'new_v_mla_w_ukv': 'new_v', 'new_v_qkn_q_nope': 'new_v', 'new_v_qkn_q_rope': 'new_v', 'new_v_qkn_k_nope': 'new_v', 'new_v_qkn_k_rope': 'new_v', 'new_v_mla_out_norm_w': 'new_v', 'new_v_w_out': 'new_v', 'new_v_ffn2_w_in': 'new_v', 'new_v_ffn2_w_out': 'new_v'}


def _forward(args):
    return _fwd_reference(*[args[k] for k in FWD_PARAMS])


def _output_shape():
    def fwd():
        inp = _fwd_setup_inputs(0)
        return _fwd_reference(*[inp[k] for k in FWD_PARAMS])
    out = _jax.eval_shape(fwd)
    return out.shape, out.dtype

N_MICROBATCH = 1
ADAM_LR = 0.001
ADAM_B1 = 0.9
ADAM_B2 = 0.999
ADAM_EPS = 1e-08
ADAM_WD = 0.01
ADAM_STEP = 10
PER_EXAMPLE_BATCH_AXIS = {'x': 0, 'c': 0, 'positions': 0, 'loss_target': 0}
SHARED_INPUTS = []
_WEIGHT_DTYPES = {'w_ada': _jnp.float32, 'b_ada': _jnp.float32, 'ffn1_w_in': _jnp.float32, 'ffn1_w_out': _jnp.float32, 'w_in': _jnp.float32, 'gdn_conv_w': _jnp.float32, 'gdn_a_log': _jnp.float32, 'gdn_dt_bias': _jnp.float32, 'gdn_norm_w': _jnp.float32, 'mla_q_norm_w': _jnp.float32, 'mla_w_uq': _jnp.float32, 'mla_kv_norm_w': _jnp.float32, 'mla_w_ukv': _jnp.float32, 'qkn_q_nope': _jnp.float32, 'qkn_q_rope': _jnp.float32, 'qkn_k_nope': _jnp.float32, 'qkn_k_rope': _jnp.float32, 'mla_out_norm_w': _jnp.float32, 'w_out': _jnp.float32, 'ffn2_w_in': _jnp.float32, 'ffn2_w_out': _jnp.float32}
MOMENT_SCALE = {'w_ada': 1.780658e+00, 'b_ada': 3.440478e+00, 'ffn1_w_in': 4.303840e-02, 'ffn1_w_out': 7.919969e-02, 'w_in': 9.728215e-01, 'gdn_conv_w': 3.137143e-01, 'gdn_a_log': 4.288824e+00, 'gdn_dt_bias': 3.996743e+00, 'gdn_norm_w': 1.113422e+01, 'mla_q_norm_w': 8.558507e-02, 'mla_w_uq': 6.669243e-02, 'mla_kv_norm_w': 3.319582e+00, 'mla_w_ukv': 1.709840e+00, 'qkn_q_nope': 2.104665e-01, 'qkn_q_rope': 8.094457e-02, 'qkn_k_nope': 2.123059e-01, 'qkn_k_rope': 8.592881e-02, 'mla_out_norm_w': 2.265377e+01, 'w_out': 1.773946e+00, 'ffn2_w_in': 4.028387e-02, 'ffn2_w_out': 6.674234e-02}


def _to_microbatches(a, axis):
    t = _jnp.moveaxis(a, axis, 0)
    t = t.reshape((N_MICROBATCH, t.shape[0] // N_MICROBATCH) + t.shape[1:])
    return _jnp.moveaxis(t, 1, axis + 1)


def setup_inputs(seed: int = 0) -> dict:
    inp = _fwd_setup_inputs(seed)
    key = _jax.random.fold_in(_jax.random.key(seed), 7919)
    shape, _ = _output_shape()
    out = dict(inp)
    out["loss_target"] = _jax.random.normal(_jax.random.fold_in(key, 0), shape, _jnp.float32)
    for i, name in enumerate(TWIN_WEIGHTS):
        w = inp[name].astype(_jnp.float32)
        if MOMENT_SCALE is None:
            s = _jnp.sqrt(_jnp.mean(_jnp.square(w)) + 1e-30)
        else:
            s = MOMENT_SCALE[name]
        km, kv = _jax.random.split(_jax.random.fold_in(key, i + 1))
        out[name] = w
        out["m_" + name] = s * _jax.random.normal(km, w.shape, _jnp.float32)
        out["v_" + name] = (s * s) * _jax.random.uniform(kv, w.shape, _jnp.float32, 0.5, 1.5)
    if N_MICROBATCH > 1:
        for name, axis in PER_EXAMPLE_BATCH_AXIS.items():
            out[name] = _to_microbatches(out[name], axis)
    return {'x': out['x'], 'c': out['c'], 'positions': out['positions'], 'w_ada': out['w_ada'], 'b_ada': out['b_ada'], 'ffn1_w_in': out['ffn1_w_in'], 'ffn1_w_out': out['ffn1_w_out'], 'w_in': out['w_in'], 'gdn_conv_w': out['gdn_conv_w'], 'gdn_a_log': out['gdn_a_log'], 'gdn_dt_bias': out['gdn_dt_bias'], 'gdn_norm_w': out['gdn_norm_w'], 'mla_q_norm_w': out['mla_q_norm_w'], 'mla_w_uq': out['mla_w_uq'], 'mla_kv_norm_w': out['mla_kv_norm_w'], 'mla_w_ukv': out['mla_w_ukv'], 'qkn_q_nope': out['qkn_q_nope'], 'qkn_q_rope': out['qkn_q_rope'], 'qkn_k_nope': out['qkn_k_nope'], 'qkn_k_rope': out['qkn_k_rope'], 'mla_out_norm_w': out['mla_out_norm_w'], 'w_out': out['w_out'], 'ffn2_w_in': out['ffn2_w_in'], 'ffn2_w_out': out['ffn2_w_out'], 'loss_target': out['loss_target'], 'm_w_ada': out['m_w_ada'], 'm_b_ada': out['m_b_ada'], 'm_ffn1_w_in': out['m_ffn1_w_in'], 'm_ffn1_w_out': out['m_ffn1_w_out'], 'm_w_in': out['m_w_in'], 'm_gdn_conv_w': out['m_gdn_conv_w'], 'm_gdn_a_log': out['m_gdn_a_log'], 'm_gdn_dt_bias': out['m_gdn_dt_bias'], 'm_gdn_norm_w': out['m_gdn_norm_w'], 'm_mla_q_norm_w': out['m_mla_q_norm_w'], 'm_mla_w_uq': out['m_mla_w_uq'], 'm_mla_kv_norm_w': out['m_mla_kv_norm_w'], 'm_mla_w_ukv': out['m_mla_w_ukv'], 'm_qkn_q_nope': out['m_qkn_q_nope'], 'm_qkn_q_rope': out['m_qkn_q_rope'], 'm_qkn_k_nope': out['m_qkn_k_nope'], 'm_qkn_k_rope': out['m_qkn_k_rope'], 'm_mla_out_norm_w': out['m_mla_out_norm_w'], 'm_w_out': out['m_w_out'], 'm_ffn2_w_in': out['m_ffn2_w_in'], 'm_ffn2_w_out': out['m_ffn2_w_out'], 'v_w_ada': out['v_w_ada'], 'v_b_ada': out['v_b_ada'], 'v_ffn1_w_in': out['v_ffn1_w_in'], 'v_ffn1_w_out': out['v_ffn1_w_out'], 'v_w_in': out['v_w_in'], 'v_gdn_conv_w': out['v_gdn_conv_w'], 'v_gdn_a_log': out['v_gdn_a_log'], 'v_gdn_dt_bias': out['v_gdn_dt_bias'], 'v_gdn_norm_w': out['v_gdn_norm_w'], 'v_mla_q_norm_w': out['v_mla_q_norm_w'], 'v_mla_w_uq': out['v_mla_w_uq'], 'v_mla_kv_norm_w': out['v_mla_kv_norm_w'], 'v_mla_w_ukv': out['v_mla_w_ukv'], 'v_qkn_q_nope': out['v_qkn_q_nope'], 'v_qkn_q_rope': out['v_qkn_q_rope'], 'v_qkn_k_nope': out['v_qkn_k_nope'], 'v_qkn_k_rope': out['v_qkn_k_rope'], 'v_mla_out_norm_w': out['v_mla_out_norm_w'], 'v_w_out': out['v_w_out'], 'v_ffn2_w_in': out['v_ffn2_w_in'], 'v_ffn2_w_out': out['v_ffn2_w_out']}


def _loss(weights, diff, rest, loss_target):
    with _jax.named_scope("forward"):
        args = {**rest, TWIN_DIFF_INPUT: diff, **{k: w.astype(_WEIGHT_DTYPES[k]) for k, w in weights.items()}}
        y = _forward(args)
    with _jax.named_scope("loss_head"):
        err = _jnp.square(y.astype(_jnp.float32) - loss_target)
        return 0.5 * _jnp.sum(_jnp.mean(err, axis=-1)) if err.ndim else 0.5 * err


def _adamw(w, g, m, v):
    m = ADAM_B1 * m + (1.0 - ADAM_B1) * g
    v = ADAM_B2 * v + (1.0 - ADAM_B2) * _jnp.square(g)
    m_hat = m / (1.0 - ADAM_B1 ** ADAM_STEP)
    v_hat = v / (1.0 - ADAM_B2 ** ADAM_STEP)
    delta = -ADAM_LR * (m_hat / (_jnp.sqrt(v_hat) + ADAM_EPS) + ADAM_WD * w)
    return delta, m, v


def reference(x, c, positions, w_ada, b_ada, ffn1_w_in, ffn1_w_out, w_in, gdn_conv_w, gdn_a_log, gdn_dt_bias, gdn_norm_w, mla_q_norm_w, mla_w_uq, mla_kv_norm_w, mla_w_ukv, qkn_q_nope, qkn_q_rope, qkn_k_nope, qkn_k_rope, mla_out_norm_w, w_out, ffn2_w_in, ffn2_w_out, loss_target, m_w_ada, m_b_ada, m_ffn1_w_in, m_ffn1_w_out, m_w_in, m_gdn_conv_w, m_gdn_a_log, m_gdn_dt_bias, m_gdn_norm_w, m_mla_q_norm_w, m_mla_w_uq, m_mla_kv_norm_w, m_mla_w_ukv, m_qkn_q_nope, m_qkn_q_rope, m_qkn_k_nope, m_qkn_k_rope, m_mla_out_norm_w, m_w_out, m_ffn2_w_in, m_ffn2_w_out, v_w_ada, v_b_ada, v_ffn1_w_in, v_ffn1_w_out, v_w_in, v_gdn_conv_w, v_gdn_a_log, v_gdn_dt_bias, v_gdn_norm_w, v_mla_q_norm_w, v_mla_w_uq, v_mla_kv_norm_w, v_mla_w_ukv, v_qkn_q_nope, v_qkn_q_rope, v_qkn_k_nope, v_qkn_k_rope, v_mla_out_norm_w, v_w_out, v_ffn2_w_in, v_ffn2_w_out):
    given = dict(x=x, c=c, positions=positions, w_ada=w_ada, b_ada=b_ada, ffn1_w_in=ffn1_w_in, ffn1_w_out=ffn1_w_out, w_in=w_in, gdn_conv_w=gdn_conv_w, gdn_a_log=gdn_a_log, gdn_dt_bias=gdn_dt_bias, gdn_norm_w=gdn_norm_w, mla_q_norm_w=mla_q_norm_w, mla_w_uq=mla_w_uq, mla_kv_norm_w=mla_kv_norm_w, mla_w_ukv=mla_w_ukv, qkn_q_nope=qkn_q_nope, qkn_q_rope=qkn_q_rope, qkn_k_nope=qkn_k_nope, qkn_k_rope=qkn_k_rope, mla_out_norm_w=mla_out_norm_w, w_out=w_out, ffn2_w_in=ffn2_w_in, ffn2_w_out=ffn2_w_out, loss_target=loss_target, m_w_ada=m_w_ada, m_b_ada=m_b_ada, m_ffn1_w_in=m_ffn1_w_in, m_ffn1_w_out=m_ffn1_w_out, m_w_in=m_w_in, m_gdn_conv_w=m_gdn_conv_w, m_gdn_a_log=m_gdn_a_log, m_gdn_dt_bias=m_gdn_dt_bias, m_gdn_norm_w=m_gdn_norm_w, m_mla_q_norm_w=m_mla_q_norm_w, m_mla_w_uq=m_mla_w_uq, m_mla_kv_norm_w=m_mla_kv_norm_w, m_mla_w_ukv=m_mla_w_ukv, m_qkn_q_nope=m_qkn_q_nope, m_qkn_q_rope=m_qkn_q_rope, m_qkn_k_nope=m_qkn_k_nope, m_qkn_k_rope=m_qkn_k_rope, m_mla_out_norm_w=m_mla_out_norm_w, m_w_out=m_w_out, m_ffn2_w_in=m_ffn2_w_in, m_ffn2_w_out=m_ffn2_w_out, v_w_ada=v_w_ada, v_b_ada=v_b_ada, v_ffn1_w_in=v_ffn1_w_in, v_ffn1_w_out=v_ffn1_w_out, v_w_in=v_w_in, v_gdn_conv_w=v_gdn_conv_w, v_gdn_a_log=v_gdn_a_log, v_gdn_dt_bias=v_gdn_dt_bias, v_gdn_norm_w=v_gdn_norm_w, v_mla_q_norm_w=v_mla_q_norm_w, v_mla_w_uq=v_mla_w_uq, v_mla_kv_norm_w=v_mla_kv_norm_w, v_mla_w_ukv=v_mla_w_ukv, v_qkn_q_nope=v_qkn_q_nope, v_qkn_q_rope=v_qkn_q_rope, v_qkn_k_nope=v_qkn_k_nope, v_qkn_k_rope=v_qkn_k_rope, v_mla_out_norm_w=v_mla_out_norm_w, v_w_out=v_w_out, v_ffn2_w_in=v_ffn2_w_in, v_ffn2_w_out=v_ffn2_w_out)
    weights = {n: given[n] for n in TWIN_WEIGHTS}
    shared = {n: given[n] for n in SHARED_INPUTS}
    per_example = {n: given[n] for n in ['x', 'c', 'positions']}
    grad_fn = _jax.value_and_grad(_loss, argnums=(0, 1))

    def one_microbatch(ex, loss_target):
        ex = dict(ex)
        diff = ex.pop(TWIN_DIFF_INPUT)
        return grad_fn(weights, diff, {**shared, **ex}, loss_target)

    if N_MICROBATCH == 1:
        loss, (grad_w, grad_x) = one_microbatch(per_example, given["loss_target"])
    else:
        def body(carry, xs):
            loss_sum, grad_sum = carry
            l_k, (gw_k, gx_k) = one_microbatch(xs[0], xs[1])
            with _jax.named_scope("update"):
                return (loss_sum + l_k, _jax.tree.map(_jnp.add, grad_sum, gw_k)), gx_k

        init = (_jnp.zeros((), _jnp.float32), _jax.tree.map(_jnp.zeros_like, weights))
        (loss, grad_w), grad_x = _jax.lax.scan(body, init, (per_example, given["loss_target"]))
    with _jax.named_scope("update"):
        delta_w, new_m, new_v = {}, {}, {}
        for n in TWIN_WEIGHTS:
            delta_w[n], new_m[n], new_v[n] = _adamw(weights[n], grad_w[n], given["m_" + n], given["v_" + n])
    return (loss, grad_x, *[grad_w[n] for n in TWIN_WEIGHTS], *[delta_w[n] for n in TWIN_WEIGHTS],
            *[new_m[n] for n in TWIN_WEIGHTS], *[new_v[n] for n in TWIN_WEIGHTS])
```

```python
import functools

import jax
import jax.numpy as jnp
from jax import lax
from jax.experimental import pallas as pl
from jax.experimental.pallas import tpu as pltpu

F32 = jnp.float32
BF16 = jnp.bfloat16

D_MODEL = 1024
D_FF = 2816
N_MOD = 9
HEADS = 4
HEAD_DIM = 128
CHUNK = 64
EPS = 1e-6
ROPE = 64
Q_LORA = 384
KV_LORA = 256
N_IN = 2760
N_IN_PACKED = 2816
ROPE_BASE = 10000.0
N_DEV = 8

ADAM_LR = 0.001
ADAM_B1 = 0.9
ADAM_B2 = 0.999
ADAM_EPS = 1e-08
ADAM_WD = 0.01
ADAM_STEP = 10

VMEM_LIMIT_BYTES = 56 * 1024 * 1024
MESH = pl.DeviceIdType.MESH


def _params(sem=None):
    return pltpu.CompilerParams(dimension_semantics=sem, vmem_limit_bytes=VMEM_LIMIT_BYTES)


def _pick(dim, prefs):
    for p in prefs:
        if dim % p == 0:
            return p
    return dim


_DIMS = {"nn": (((1,), (0,)), ((), ())), "nt": (((1,), (1,)), ((), ())), "tn": (((0,), (0,)), ((), ()))}


def _dot_raw(a, b, mode):
    return lax.dot_general(a.astype(BF16), b.astype(BF16), _DIMS[mode], preferred_element_type=F32)


def _dot_hi(a, b, mode="nn"):
    return lax.dot_general(a, b, _DIMS[mode], precision=lax.Precision.HIGHEST, preferred_element_type=F32)


@functools.partial(jax.custom_vjp, nondiff_argnums=(2,))
def _bdot(a, b, mode):
    return _dot_raw(a, b, mode)


def _bdot_fwd(a, b, mode):
    return _dot_raw(a, b, mode), (a, b)


def _bdot_bwd(mode, res, g):
    a, b = res
    if mode == "nn":
        return _dot_raw(g, b, "nt"), _dot_raw(a, g, "tn")
    if mode == "nt":
        return _dot_raw(g, b, "nn"), _dot_raw(g, a, "tn")
    return _dot_raw(b, g, "nt"), _dot_raw(a, g, "nn")


_bdot.defvjp(_bdot_fwd, _bdot_bwd)


def _mm(a, b, mode, *, name, out_dtypes=(F32,), epi=None, extras=(), extra_params=(), hi=False,
        tm=None, tn=None, tk=None):
    if mode == "nn":
        (M, K), (_, N) = a.shape, b.shape
    elif mode == "nt":
        (M, K), (N, _) = a.shape, b.shape
    else:
        (K, M), (_, N) = a.shape, b.shape
    tm = tm or _pick(M, (512, 1408, 256, 128) if mode == "tn" else (512, 384, 352, 256, 128))
    tn = tn or _pick(N, (1024, 1408, 768, 512, 384, 256, 128))
    tk = tk or _pick(K, (1024, 1408, 512, 384, 256, 128))
    nk = K // tk
    a_spec = {"nn": pl.BlockSpec((tm, tk), lambda i, j, k: (i, k)), "nt": pl.BlockSpec((tm, tk), lambda i, j, k: (i, k)),
              "tn": pl.BlockSpec((tk, tm), lambda i, j, k: (k, i))}[mode]
    b_spec = {"nn": pl.BlockSpec((tk, tn), lambda i, j, k: (k, j)), "nt": pl.BlockSpec((tn, tk), lambda i, j, k: (j, k)),
              "tn": pl.BlockSpec((tk, tn), lambda i, j, k: (k, j))}[mode]
    n_e, n_p, n_o = len(extras), len(extra_params), len(out_dtypes)

    def body(*refs):
        a_ref, b_ref = refs[:2]
        e_refs = refs[2:2 + n_e]
        p_refs = refs[2 + n_e:2 + n_e + n_p]
        o_refs = refs[2 + n_e + n_p:2 + n_e + n_p + n_o]
        acc_ref = refs[-1]
        k = pl.program_id(2)

        @pl.when(k == 0)
        def _():
            acc_ref[...] = jnp.zeros_like(acc_ref)

        if hi:
            acc_ref[...] += _dot_hi(a_ref[...].astype(F32), b_ref[...].astype(F32), mode)
        else:
            acc_ref[...] += _dot_raw(a_ref[...], b_ref[...], mode)

        @pl.when(k == nk - 1)
        def _():
            acc = acc_ref[...]
            if epi is None:
                outs = (acc,)
            else:
                outs = epi(acc, *[e[...].astype(F32) for e in e_refs], *[p[...].astype(F32) for p in p_refs])
            for o_ref, o in zip(o_refs, outs):
                o_ref[...] = o.astype(o_ref.dtype)

    mn_spec = pl.BlockSpec((tm, tn), lambda i, j, k: (i, j))
    outs = pl.pallas_call(
        body, name=name,
        grid=(M // tm, N // tn, nk),
        in_specs=[a_spec, b_spec] + [mn_spec] * n_e + [pl.BlockSpec((1, tn), lambda i, j, k: (0, j))] * n_p,
        out_specs=[mn_spec] * n_o,
        out_shape=[jax.ShapeDtypeStruct((M, N), dt) for dt in out_dtypes],
        scratch_shapes=[pltpu.VMEM((tm, tn), F32)],
        compiler_params=_params(("parallel", "parallel", "arbitrary")),
    )(a, b, *extras, *extra_params)
    return outs if n_o > 1 else outs[0]


def _row_spec(th, cw, ci):
    return pl.BlockSpec((th, cw), lambda i: (i, ci))


def _full_spec(shape):
    return pl.BlockSpec(shape, lambda i: (0,) * len(shape))


def _rowwise(fn, rows, params, outs, n_steps, name):
    n_r, n_p, n_o = len(rows), len(params), len(outs)

    def body(*refs):
        vals = [r[...].astype(F32) for r in refs[:n_r + n_p]]
        res = fn(*vals)
        for o_ref, o in zip(refs[n_r + n_p:], res):
            o_ref[...] = o.astype(o_ref.dtype)

    res = pl.pallas_call(
        body, name=name, grid=(n_steps,),
        in_specs=[_row_spec(th, cw, ci) for (_, th, cw, ci) in rows] + [_full_spec(p.shape) for p in params],
        out_specs=[_row_spec(th, cw, 0) for (th, cw, _) in outs],
        out_shape=[jax.ShapeDtypeStruct((n_steps * th, cw), dt) for (th, cw, dt) in outs],
        compiler_params=_params(("parallel",)),
    )(*[r[0] for r in rows], *params)
    return res


def _rowwise_bwd(fn, rows, aux, params, douts, n_steps, name, row_dtypes=None, adds=()):
    n_r, n_a, n_p, n_d, n_add = len(rows), len(aux), len(params), len(douts), len(adds)
    row_dtypes = row_dtypes or (F32,) * n_r

    def body(*refs):
        it = iter(refs)
        r_vals = [next(it)[...].astype(F32) for _ in range(n_r)]
        a_vals = [next(it)[...].astype(F32) for _ in range(n_a)]
        p_vals = [next(it)[...].astype(F32) for _ in range(n_p)]
        d_vals = [next(it)[...].astype(F32) for _ in range(n_d)]
        add_vals = [next(it)[...].astype(F32) for _ in range(n_add)]
        dr_refs = [next(it) for _ in range(n_r)]
        dp_refs = [next(it) for _ in range(n_p)]

        def f(*rp):
            return tuple(fn(*rp[:n_r], *a_vals, *rp[n_r:]))

        _, vjp = jax.vjp(f, *r_vals, *p_vals)
        grads = list(vjp(tuple(d_vals)))
        for (ri, _), av in zip(adds, add_vals):
            grads[ri] = grads[ri] + av
        for dr_ref, g in zip(dr_refs, grads[:n_r]):
            dr_ref[...] = g.astype(dr_ref.dtype)

        @pl.when(pl.program_id(0) == 0)
        def _():
            for dp_ref in dp_refs:
                dp_ref[...] = jnp.zeros_like(dp_ref)

        for dp_ref, g in zip(dp_refs, grads[n_r:]):
            dp_ref[...] += g

    all_rows = list(rows) + list(aux) + list(douts) + [(arr,) + tuple(rows[ri][1:3]) + (0,) for ri, arr in adds]
    in_specs = ([_row_spec(th, cw, ci) for (_, th, cw, ci) in list(rows) + list(aux)]
                + [_full_spec(p.shape) for p in params]
                + [_row_spec(th, cw, ci) for (_, th, cw, ci) in all_rows[n_r + n_a:]])
    res = pl.pallas_call(
        body, name=name, grid=(n_steps,),
        in_specs=in_specs,
        out_specs=[_row_spec(th, cw, 0) for (_, th, cw, _) in rows] + [_full_spec(p.shape) for p in params],
        out_shape=[jax.ShapeDtypeStruct((n_steps * th, cw), dt) for (_, th, cw, _), dt in zip(rows, row_dtypes)]
        + [jax.ShapeDtypeStruct(p.shape, F32) for p in params],
        compiler_params=_params(("arbitrary",)),
    )(*[r[0] for r in list(rows) + list(aux)], *params, *[r[0] for r in all_rows[n_r + n_a:]])
    return res[:n_r], res[n_r:]


def _sigmoid(x):
    return lax.logistic(x)


def _silu(x):
    return x * _sigmoid(x)


def _rms(x, w=None, n=None):
    n = n or x.shape[-1]
    y = x * lax.rsqrt(jnp.sum(x * x, axis=-1, keepdims=True) * (1.0 / n) + EPS)
    return y if w is None else y * w


def _modulate(x, scale, shift):
    return _rms(x) * (1.0 + scale) + shift


def _softplus(x):
    return jnp.maximum(x, 0.0) + jnp.log1p(jnp.exp(-jnp.abs(x)))


@jax.custom_vjp
def _rot_half64(x):
    lane = lax.broadcasted_iota(jnp.int32, x.shape, 1)
    up = pltpu.roll(x, 96, 1)
    down = pltpu.roll(x, 32, 1)
    return jnp.where(lane < 32, up, jnp.where(lane < 64, down, 0.0))


_rot_half64.defvjp(lambda x: (_rot_half64(x), None), lambda _, g: (_rot_half64(g),))


def _rope128(x, cos_p, sin_p):
    return x * cos_p + _rot_half64(x) * sin_p


def _gdn_pre_fn(qkvc, kab, alog_p, dt_p):
    a = _silu(qkvc)
    qs, ks = [], []
    for h in range(HEADS):
        qh = a[:, HEAD_DIM * h:HEAD_DIM * (h + 1)]
        kh = a[:, 512 + HEAD_DIM * h:512 + HEAD_DIM * (h + 1)]
        qs.append(qh * lax.rsqrt(jnp.sum(qh * qh, axis=-1, keepdims=True) + EPS) * (HEAD_DIM ** -0.5))
        ks.append(kh * lax.rsqrt(jnp.sum(kh * kh, axis=-1, keepdims=True) + EPS))
    lane = lax.broadcasted_iota(jnp.int32, kab.shape, 1)
    g_full = -jnp.exp(alog_p) * _softplus(kab + dt_p)
    b_full = _sigmoid(kab)
    gb = jnp.where((lane >= 64) & (lane < 68), g_full, jnp.where((lane >= 68) & (lane < 72), b_full, 0.0))
    return jnp.concatenate(qs, axis=1), jnp.concatenate(ks, axis=1), a[:, 1024:1536], gb


def _intra_head(q, k, v, g_col, b_col):
    c = CHUNK
    row = lax.broadcasted_iota(jnp.int32, (c, c), 0)
    col = lax.broadcasted_iota(jnp.int32, (c, c), 1)
    incl, strict, eye = row >= col, row > col, row == col
    tri = jnp.where(incl, 1.0, 0.0).astype(F32)
    ident = jnp.where(eye, 1.0, 0.0).astype(F32)
    g_wide = _dot_hi(tri, jnp.broadcast_to(g_col, (c, HEAD_DIM)))
    g_i = g_wide[:, :c]
    g_j = jnp.sum(jnp.where(eye, g_i, 0.0), axis=0, keepdims=True)
    decay = jnp.where(incl, jnp.exp(jnp.where(incl, g_i - g_j, 0.0)), 0.0)
    kk = _bdot(k, k, "nt")
    a_mat = jnp.where(strict, b_col * kk * decay, 0.0)
    x_pow = -a_mat
    inv = ident + x_pow
    for _ in range(5):
        x_pow = _dot_hi(x_pow, x_pow)
        inv = inv + _dot_hi(inv, x_pow)
    e_wide = jnp.exp(g_wide)
    u = _dot_hi(inv, v * b_col)
    wk = _dot_hi(inv, k * b_col * e_wide)
    qk = _bdot(q, k, "nt") * decay
    last = lax.broadcasted_iota(jnp.int32, (c, HEAD_DIM), 0) == c - 1
    g_last = jnp.sum(jnp.where(last, g_wide, 0.0), axis=0, keepdims=True)
    qd = q * e_wide
    kd = k * jnp.exp(g_last - g_wide)
    gl = jnp.broadcast_to(jnp.exp(g_last), (8, HEAD_DIM))
    return u, wk, qd, kd, qk, gl


def _gdn_intra_fn(q, k, v, gb):
    lane = lax.broadcasted_iota(jnp.int32, gb.shape, 1)
    us, wks, qds, kds, qks, gls = [], [], [], [], [], []
    for h in range(HEADS):
        sl = slice(HEAD_DIM * h, HEAD_DIM * (h + 1))
        g_col = jnp.sum(jnp.where(lane == 64 + h, gb, 0.0), axis=1, keepdims=True)
        b_col = jnp.sum(jnp.where(lane == 68 + h, gb, 0.0), axis=1, keepdims=True)
        u, wk, qd, kd, qk, gl = _intra_head(q[:, sl], k[:, sl], v[:, sl], g_col, b_col)
        us.append(u); wks.append(wk); qds.append(qd); kds.append(kd); qks.append(qk); gls.append(gl)
    cat = lambda xs: jnp.concatenate(xs, axis=1)
    return (cat(us), cat(wks), cat(qds), cat(kds), *qks, cat(gls))


def _scan_step(s0, u, wk, qd, kd, qk, gl):
    v_new = u - _bdot(wk, s0, "nn")
    o = _bdot(qd, s0, "nn") + _bdot(qk, v_new, "nn")
    s1 = s0 * gl[0:1, :] + _bdot(kd, v_new, "tn")
    return o, s1


def _mix_post_fn(o_a, z, o_b, gnw, onw):
    parts = [_rms(o_a[:, HEAD_DIM * h:HEAD_DIM * (h + 1)], gnw) * _silu(z[:, HEAD_DIM * h:HEAD_DIM * (h + 1)])
             for h in range(HEADS)]
    parts += [_rms(o_b[:, HEAD_DIM * h:HEAD_DIM * (h + 1)], onw) for h in range(HEADS)]
    return (jnp.concatenate(parts, axis=1),)


def _mla_pre_fn(ckv, cq, kab, cos_p, sin_p, qnw, kvnw, wuq, wukv, qn_w, qr_w, kn_w, kr_w):
    scale = (HEAD_DIM + ROPE) ** -0.5
    qf = _bdot(_rms(cq, qnw), wuq, "nn")
    kvf = _bdot(_rms(ckv, kvnw), wukv, "nn")
    lane = lax.broadcasted_iota(jnp.int32, kab.shape, 1)
    kr = _rope128(_rms(jnp.where(lane < ROPE, kab, 0.0), kr_w, n=ROPE), cos_p, sin_p)
    qs, ks = [], []
    for h in range(HEADS):
        qn = _rms(qf[:, 256 * h:256 * h + 128], qn_w) * scale
        qr = _rope128(_rms(qf[:, 256 * h + 128:256 * h + 256], qr_w, n=ROPE), cos_p, sin_p) * scale
        qs += [qn, qr]
        ks += [_rms(kvf[:, 128 * h:128 * (h + 1)], kn_w), kr]
    return jnp.concatenate(qs, axis=1), jnp.concatenate(ks, axis=1), kvf[:, 512:]


def _conv_fwd(proj, conv_w, tm, name):
    S = proj.shape[0]
    C = 1536
    nb = tm // 8

    def body(x_ref, prev_ref, w_ref, o_ref, ext_ref):
        i = pl.program_id(0)
        ext_ref[0:8, :] = jnp.where(i > 0, prev_ref[...], 0.0)
        ext_ref[8:, :] = x_ref[...]
        acc = jnp.zeros((tm, C), F32)
        for k in range(4):
            acc = acc + w_ref[k:k + 1, :] * ext_ref[pl.ds(5 + k, tm), :]
        o_ref[...] = acc

    return pl.pallas_call(
        body, name=name, grid=(S // tm,),
        in_specs=[pl.BlockSpec((tm, C), lambda i: (i, 0)),
                  pl.BlockSpec((8, C), lambda i: (jnp.maximum(i * nb - 1, 0), 0)),
                  pl.BlockSpec((4, C), lambda i: (0, 0))],
        out_specs=pl.BlockSpec((tm, C), lambda i: (i, 0)),
        out_shape=jax.ShapeDtypeStruct((S, C), F32),
        scratch_shapes=[pltpu.VMEM((tm + 8, C), F32)],
        compiler_params=_params(("arbitrary",)),
    )(proj, proj, conv_w)


def _conv_bwd(proj, dout, conv_w, tm, name):
    S = proj.shape[0]
    C = 1536
    nb = tm // 8
    n_steps = S // tm

    def body(x_ref, prev_ref, d_ref, next_ref, w_ref, dx_ref, dw_ref, xext_ref, dext_ref):
        i = pl.program_id(0)
        xext_ref[0:8, :] = jnp.where(i > 0, prev_ref[...], 0.0)
        xext_ref[8:, :] = x_ref[...]
        dext_ref[0:tm, :] = d_ref[...]
        dext_ref[tm:, :] = jnp.where(i < n_steps - 1, next_ref[...], 0.0)
        d = d_ref[...]
        acc = jnp.zeros((tm, C), F32)
        dws = []
        for k in range(4):
            acc = acc + w_ref[k:k + 1, :] * dext_ref[pl.ds(3 - k, tm), :]
            dws.append(jnp.sum(d * xext_ref[pl.ds(5 + k, tm), :], axis=0, keepdims=True))
        dx_ref[...] = acc

        @pl.when(i == 0)
        def _():
            dw_ref[...] = jnp.zeros_like(dw_ref)

        dw_ref[...] += jnp.concatenate(dws + [jnp.zeros((4, C), F32)], axis=0)

    return pl.pallas_call(
        body, name=name, grid=(n_steps,),
        in_specs=[pl.BlockSpec((tm, C), lambda i: (i, 0)),
                  pl.BlockSpec((8, C), lambda i: (jnp.maximum(i * nb - 1, 0), 0)),
                  pl.BlockSpec((tm, C), lambda i: (i, 0)),
                  pl.BlockSpec((8, C), lambda i: (jnp.minimum((i + 1) * nb, S // 8 - 1), 0)),
                  pl.BlockSpec((4, C), lambda i: (0, 0))],
        out_specs=[pl.BlockSpec((tm, C), lambda i: (i, 0)), pl.BlockSpec((8, C), lambda i: (0, 0))],
        out_shape=[jax.ShapeDtypeStruct((S, C), F32), jax.ShapeDtypeStruct((8, C), F32)],
        scratch_shapes=[pltpu.VMEM((tm + 8, C), F32), pltpu.VMEM((tm + 8, C), F32)],
        compiler_params=_params(("arbitrary",)),
    )(proj, proj, dout, dout, conv_w)


def _gdn_scan_fwd(u, wk, qd, kd, qks, gl, name):
    S = u.shape[0]
    nc = S // CHUNK
    W = HEADS * HEAD_DIM

    def body(u_ref, wk_ref, qd_ref, kd_ref, qk0, qk1, qk2, qk3, gl_ref, o_ref, sp_ref, s_ref):
        @pl.when(pl.program_id(0) == 0)
        def _():
            s_ref[...] = jnp.zeros_like(s_ref)

        for h, qk_ref in enumerate((qk0, qk1, qk2, qk3)):
            sl = slice(HEAD_DIM * h, HEAD_DIM * (h + 1))
            s0 = s_ref[h]
            sp_ref[0, h] = s0
            o, s1 = _scan_step(s0, u_ref[:, sl], wk_ref[:, sl], qd_ref[:, sl], kd_ref[:, sl], qk_ref[...], gl_ref[:, sl])
            o_ref[:, sl] = o
            s_ref[h] = s1

    row = pl.BlockSpec((CHUNK, W), lambda n: (n, 0))
    qk_spec = pl.BlockSpec((CHUNK, CHUNK), lambda n: (n, 0))
    return pl.pallas_call(
        body, name=name, grid=(nc,),
        in_specs=[row, row, row, row, qk_spec, qk_spec, qk_spec, qk_spec, pl.BlockSpec((8, W), lambda n: (n, 0))],
        out_specs=[row, pl.BlockSpec((1, HEADS, HEAD_DIM, HEAD_DIM), lambda n: (n, 0, 0, 0))],
        out_shape=[jax.ShapeDtypeStruct((S, W), F32), jax.ShapeDtypeStruct((nc, HEADS, HEAD_DIM, HEAD_DIM), F32)],
        scratch_shapes=[pltpu.VMEM((HEADS, HEAD_DIM, HEAD_DIM), F32)],
        compiler_params=_params(("arbitrary",)),
    )(u, wk, qd, kd, *qks, gl)


def _gdn_scan_bwd(u, wk, qd, kd, qks, gl, s_prev, d_o, name):
    S = u.shape[0]
    nc = S // CHUNK
    W = HEADS * HEAD_DIM

    def body(u_ref, wk_ref, qd_ref, kd_ref, qk0, qk1, qk2, qk3, gl_ref, sp_ref, do_ref,
             du_ref, dwk_ref, dqd_ref, dkd_ref, dqk0, dqk1, dqk2, dqk3, dgl_ref, ds_ref):
        @pl.when(pl.program_id(0) == 0)
        def _():
            ds_ref[...] = jnp.zeros_like(ds_ref)

        for h, (qk_ref, dqk_ref) in enumerate(zip((qk0, qk1, qk2, qk3), (dqk0, dqk1, dqk2, dqk3))):
            sl = slice(HEAD_DIM * h, HEAD_DIM * (h + 1))
            _, vjp = jax.vjp(_scan_step, sp_ref[0, h], u_ref[:, sl], wk_ref[:, sl], qd_ref[:, sl], kd_ref[:, sl],
                             qk_ref[...], gl_ref[:, sl])
            ds0, du, dwk, dqd, dkd, dqk, dgl = vjp((do_ref[:, sl], ds_ref[h]))
            ds_ref[h] = ds0
            du_ref[:, sl] = du
            dwk_ref[:, sl] = dwk
            dqd_ref[:, sl] = dqd
            dkd_ref[:, sl] = dkd
            dqk_ref[...] = dqk
            dgl_ref[:, sl] = dgl

    rev = lambda n: (nc - 1 - n, 0)
    row = pl.BlockSpec((CHUNK, W), rev)
    qk_spec = pl.BlockSpec((CHUNK, CHUNK), rev)
    gl_spec = pl.BlockSpec((8, W), rev)
    qk_shape = jax.ShapeDtypeStruct((S, CHUNK), F32)
    row_shape = jax.ShapeDtypeStruct((S, W), F32)
    return pl.pallas_call(
        body, name=name, grid=(nc,),
        in_specs=[row, row, row, row, qk_spec, qk_spec, qk_spec, qk_spec, gl_spec,
                  pl.BlockSpec((1, HEADS, HEAD_DIM, HEAD_DIM), lambda n: (nc - 1 - n, 0, 0, 0)), row],
        out_specs=[row, row, row, row, qk_spec, qk_spec, qk_spec, qk_spec, gl_spec],
        out_shape=[row_shape] * 4 + [qk_shape] * 4 + [jax.ShapeDtypeStruct((nc * 8, W), F32)],
        scratch_shapes=[pltpu.VMEM((HEADS, HEAD_DIM, HEAD_DIM), F32)],
        compiler_params=_params(("arbitrary",)),
    )(u, wk, qd, kd, *qks, gl, s_prev, d_o)


NEG = -1e30


def _chunk_mask(i, j, t):
    r = (i * t + lax.broadcasted_iota(jnp.int32, (t, t), 0)) // CHUNK
    c = (j * t + lax.broadcasted_iota(jnp.int32, (t, t), 1)) // CHUNK
    return c <= r


def _attn_fwd(q, k, v, t, name):
    S = q.shape[0]
    n = S // t

    def body(q_ref, k_ref, v_ref, o_ref, lse_ref, m_ref, l_ref, acc_ref):
        i, j = pl.program_id(1), pl.program_id(2)

        @pl.when(j == 0)
        def _():
            m_ref[...] = jnp.full_like(m_ref, NEG)
            l_ref[...] = jnp.zeros_like(l_ref)
            acc_ref[...] = jnp.zeros_like(acc_ref)

        def update(masked):
            s = _dot_raw(q_ref[...], k_ref[...], "nt")
            if masked:
                s = jnp.where(_chunk_mask(i, j, t), s, NEG)
            m_old = m_ref[...]
            m_new = jnp.maximum(m_old, jnp.max(s, axis=1, keepdims=True))
            p = jnp.exp(s - m_new)
            alpha = jnp.exp(m_old - m_new)
            l_ref[...] = alpha * l_ref[...] + jnp.sum(p, axis=1, keepdims=True)
            acc_ref[...] = alpha * acc_ref[...] + _dot_raw(p, v_ref[...], "nn")
            m_ref[...] = m_new

        @pl.when(j < i)
        def _():
            update(False)

        @pl.when(j == i)
        def _():
            update(True)
            o_ref[...] = acc_ref[...] / l_ref[...]
            lse_ref[...] = jnp.broadcast_to(m_ref[...] + jnp.log(l_ref[...]), (t, HEAD_DIM))

    return pl.pallas_call(
        body, name=name, grid=(HEADS, n, n),
        in_specs=[pl.BlockSpec((t, 256), lambda h, i, j: (i, h)),
                  pl.BlockSpec((t, 256), lambda h, i, j: (jnp.minimum(j, i), h)),
                  pl.BlockSpec((t, HEAD_DIM), lambda h, i, j: (jnp.minimum(j, i), h))],
        out_specs=[pl.BlockSpec((t, HEAD_DIM), lambda h, i, j: (i, h)), pl.BlockSpec((t, HEAD_DIM), lambda h, i, j: (i, h))],
        out_shape=[jax.ShapeDtypeStruct((S, HEADS * HEAD_DIM), F32)] * 2,
        scratch_shapes=[pltpu.VMEM((t, 1), F32), pltpu.VMEM((t, 1), F32), pltpu.VMEM((t, HEAD_DIM), F32)],
        compiler_params=_params(("parallel", "parallel", "arbitrary")),
    )(q, k, v)


def _attn_bwd(q, k, v, o, lse, d_o, t, name):
    S = q.shape[0]
    n = S // t

    def body(q_ref, k_ref, v_ref, o_ref, lse_ref, do_ref, dq_ref, dk_ref, dv_ref):
        j, i = pl.program_id(1), pl.program_id(2)

        @pl.when(i == 0)
        def _():
            dk_ref[...] = jnp.zeros_like(dk_ref)
            dv_ref[...] = jnp.zeros_like(dv_ref)

        def update(masked):
            rows = pl.ds(pl.multiple_of(i * t, t), t)
            d_out = do_ref[...]
            delta = jnp.sum(d_out * o_ref[...], axis=1, keepdims=True)
            s = _dot_raw(q_ref[...], k_ref[...], "nt")
            p = jnp.exp(s - lse_ref[:, 0:1])
            if masked:
                p = jnp.where(_chunk_mask(i, j, t), p, 0.0)
            dv_ref[...] += _dot_raw(p, d_out, "tn")
            dp = _dot_raw(d_out, v_ref[...], "nt")
            ds = p * (dp - delta)
            dk_ref[...] += _dot_raw(ds, q_ref[...], "tn")
            dq_new = _dot_raw(ds, k_ref[...], "nn")

            @pl.when(j == 0)
            def _():
                dq_ref[rows, :] = dq_new

            @pl.when(j > 0)
            def _():
                dq_ref[rows, :] += dq_new

        @pl.when(i > j)
        def _():
            update(False)

        @pl.when(i == j)
        def _():
            update(True)

    qi = lambda h, j, i: (jnp.maximum(i, j), h)
    return pl.pallas_call(
        body, name=name, grid=(HEADS, n, n),
        in_specs=[pl.BlockSpec((t, 256), qi),
                  pl.BlockSpec((t, 256), lambda h, j, i: (j, h)),
                  pl.BlockSpec((t, HEAD_DIM), lambda h, j, i: (j, h)),
                  pl.BlockSpec((t, HEAD_DIM), qi), pl.BlockSpec((t, HEAD_DIM), qi), pl.BlockSpec((t, HEAD_DIM), qi)],
        out_specs=[pl.BlockSpec((S, 256), lambda h, j, i: (0, h)),
                   pl.BlockSpec((t, 256), lambda h, j, i: (j, h)),
                   pl.BlockSpec((t, HEAD_DIM), lambda h, j, i: (j, h))],
        out_shape=[jax.ShapeDtypeStruct((S, HEADS * 256), F32), jax.ShapeDtypeStruct((S, HEADS * 256), F32),
                   jax.ShapeDtypeStruct((S, HEADS * HEAD_DIM), F32)],
        compiler_params=_params(("arbitrary", "arbitrary", "arbitrary")),
    )(q, k, v, o, lse, d_o)


def _ffn_up(h, wg, wu, name):
    S = h.shape[0]
    tm = _pick(S, (512, 256, 128))
    tn = 1408

    def body(h_ref, wg_ref, wu_ref, g_ref, u_ref, hid_ref):
        gate = _dot_raw(h_ref[...], wg_ref[...], "nn")
        up = _dot_raw(h_ref[...], wu_ref[...], "nn")
        g_ref[...] = gate.astype(BF16)
        u_ref[...] = up.astype(BF16)
        hid_ref[...] = (_silu(gate) * up).astype(BF16)

    w_spec = pl.BlockSpec((D_MODEL, tn), lambda i, j: (0, j))
    o_spec = pl.BlockSpec((tm, tn), lambda i, j: (i, j))
    return pl.pallas_call(
        body, name=name, grid=(S // tm, D_FF // tn),
        in_specs=[pl.BlockSpec((tm, D_MODEL), lambda i, j: (i, 0)), w_spec, w_spec],
        out_specs=[o_spec] * 3,
        out_shape=[jax.ShapeDtypeStruct((S, D_FF), BF16)] * 3,
        compiler_params=_params(("parallel", "parallel")),
    )(h, wg, wu)


def _swiglu_bwd(d_hid, gate, up):
    sg = _sigmoid(gate)
    return d_hid * up * (sg * (1.0 + gate * (1.0 - sg))), d_hid * (gate * sg)


def _loss_and_grad(x3, target, name):
    S = x3.shape[0]
    tm = _pick(S, (512, 256, 128))
    n = S // tm

    def body(x_ref, t_ref, dx_ref, l_ref):
        i = pl.program_id(0)
        diff = x_ref[...] - t_ref[...]
        dx_ref[...] = diff * (1.0 / D_MODEL)

        @pl.when(i == 0)
        def _():
            l_ref[...] = jnp.zeros_like(l_ref)

        l_ref[...] += jnp.sum(diff * diff, axis=0, keepdims=True)

        @pl.when(i == n - 1)
        def _():
            l_ref[...] = jnp.full(l_ref.shape, (0.5 / D_MODEL) * jnp.sum(l_ref[...]), F32)

    return pl.pallas_call(
        body, name=name, grid=(n,),
        in_specs=[pl.BlockSpec((tm, D_MODEL), lambda i: (i, 0))] * 2,
        out_specs=[pl.BlockSpec((tm, D_MODEL), lambda i: (i, 0)), pl.BlockSpec((1, D_MODEL), lambda i: (0, 0))],
        out_shape=[jax.ShapeDtypeStruct((S, D_MODEL), F32), jax.ShapeDtypeStruct((1, D_MODEL), F32)],
        compiler_params=_params(("arbitrary",)),
    )(x3, target)


def _adamw(w, g, m, v, name):
    R, C = w.shape
    tr = _pick(R, (256, 176, 128, 64, 32, 16, 8))

    def body(w_ref, g_ref, m_ref, v_ref, d_ref, nm_ref, nv_ref):
        g_ = g_ref[...]
        m_ = ADAM_B1 * m_ref[...] + (1.0 - ADAM_B1) * g_
        v_ = ADAM_B2 * v_ref[...] + (1.0 - ADAM_B2) * (g_ * g_)
        m_hat = m_ / (1.0 - ADAM_B1 ** ADAM_STEP)
        v_hat = v_ / (1.0 - ADAM_B2 ** ADAM_STEP)
        d_ref[...] = -ADAM_LR * (m_hat / (jnp.sqrt(v_hat) + ADAM_EPS) + ADAM_WD * w_ref[...])
        nm_ref[...] = m_
        nv_ref[...] = v_

    spec = pl.BlockSpec((tr, C), lambda i: (i, 0))
    return pl.pallas_call(
        body, name=name, grid=(R // tr,),
        in_specs=[spec] * 4, out_specs=[spec] * 3,
        out_shape=[jax.ShapeDtypeStruct((R, C), F32)] * 3,
        compiler_params=_params(("parallel",)),
    )(w, g, m, v)


def _sum_devices(parts, name):
    _, R, C = parts.shape
    tr = _pick(R, (512, 256, 128, 64, 32, 16, 8))

    def body(p_ref, o_ref):
        acc = p_ref[0].astype(F32)
        for d in range(1, N_DEV):
            acc = acc + p_ref[d].astype(F32)
        o_ref[...] = acc

    return pl.pallas_call(
        body, name=name, grid=(R // tr,),
        in_specs=[pl.BlockSpec((N_DEV, tr, C), lambda i: (0, i, 0))],
        out_specs=pl.BlockSpec((tr, C), lambda i: (i, 0)),
        out_shape=jax.ShapeDtypeStruct((R, C), F32),
        compiler_params=_params(("parallel",)),
    )(parts)


def _my_place():
    return lax.axis_index("x"), lax.axis_index("y"), lax.axis_index("c")


def _all_gather(block, name):
    R, C = block.shape

    def body(x_ref, out_ref, send_sems, recv_sems, local_sem):
        x, y, c = _my_place()
        me, sibling = (x, y, c), (x, y, 1 - c)
        chips = [(1 - x, y), (x, 1 - y), (1 - x, 1 - y)]

        def slot(px, py, pc):
            return out_ref.at[4 * px + 2 * py + pc]

        def copy(k, blk, to, src=None):
            return pltpu.make_async_remote_copy(
                src_ref=slot(*blk) if src is None else src, dst_ref=slot(*blk),
                send_sem=send_sems.at[k], recv_sem=recv_sems.at[k], device_id=to, device_id_type=MESH)

        mine = pltpu.make_async_copy(x_ref, slot(*me), local_sem)
        mine.start()
        first = [copy(0, me, sibling, src=x_ref)]
        first += [copy(1 + j, me, (*chip, c), src=x_ref) for j, chip in enumerate(chips)]
        for cp in first:
            cp.start()
        passed = [copy(4 + j, (*chip, c), sibling) for j, chip in enumerate(chips)]
        for j, chip in enumerate(chips):
            copy(1 + j, (*chip, c), me).wait_recv()
            passed[j].start()
        copy(0, sibling, me).wait_recv()
        for j, chip in enumerate(chips):
            copy(4 + j, (*chip, 1 - c), me).wait_recv()
        for cp in first + passed:
            cp.wait_send()
        mine.wait()

    return pl.pallas_call(
        body, name=name,
        out_shape=jax.ShapeDtypeStruct((N_DEV, R, C), block.dtype),
        in_specs=[pl.BlockSpec(memory_space=pl.ANY)],
        out_specs=pl.BlockSpec(memory_space=pl.ANY),
        scratch_shapes=[pltpu.SemaphoreType.DMA((7,)), pltpu.SemaphoreType.DMA((7,)), pltpu.SemaphoreType.DMA],
    )(block)


def _all_to_all(pieces, name):
    _, R, C = pieces.shape

    def body(x_ref, out_ref, send_sems, recv_sems, local_sem):
        x, y, c = _my_place()
        me = 4 * x + 2 * y + c
        mine = pltpu.make_async_copy(x_ref.at[me], out_ref.at[me], local_sem)
        mine.start()
        copies = []
        for k in range(1, N_DEV):
            px = 1 - x if k & 4 else x
            py = 1 - y if k & 2 else y
            pc = 1 - c if k & 1 else c
            peer = 4 * px + 2 * py + pc
            copies.append(pltpu.make_async_remote_copy(
                src_ref=x_ref.at[peer], dst_ref=out_ref.at[me],
                send_sem=send_sems.at[k - 1], recv_sem=recv_sems.at[k - 1],
                device_id=(px, py, pc), device_id_type=MESH))
        for cp in copies:
            cp.start()
        for cp in copies:
            cp.wait_recv()
        for cp in copies:
            cp.wait_send()
        mine.wait()

    return pl.pallas_call(
        body, name=name,
        out_shape=jax.ShapeDtypeStruct((N_DEV, R, C), pieces.dtype),
        in_specs=[pl.BlockSpec(memory_space=pl.ANY)],
        out_specs=pl.BlockSpec(memory_space=pl.ANY),
        scratch_shapes=[pltpu.SemaphoreType.DMA((7,)), pltpu.SemaphoreType.DMA((7,)), pltpu.SemaphoreType.DMA],
    )(pieces)


def _pad_lanes(v, at=0, width=128):
    return jnp.pad(v, ((0, 0), (at, width - at - v.shape[1])))


def _pack_weights(P):
    W = {}
    for tag, name in (("f1", "ffn1"), ("f2", "ffn2")):
        W[tag + "_wg"] = P[name + "_w_in"][:, :D_FF].astype(BF16)
        W[tag + "_wu"] = P[name + "_w_in"][:, D_FF:].astype(BF16)
        W[tag + "_wo"] = P[name + "_w_out"].astype(BF16)
    w = P["w_in"]
    W["wp"] = jnp.concatenate([w[:, :2048], w[:, 2440:2696], w[:, 2056:2440], w[:, 2696:2760], w[:, 2048:2056],
                               jnp.zeros((D_MODEL, N_IN_PACKED - N_IN), w.dtype)], axis=1).astype(BF16)
    W["conv_w"] = P["gdn_conv_w"].astype(F32)
    W["alog_p"] = _pad_lanes(P["gdn_a_log"], 64)
    W["dt_p"] = _pad_lanes(P["gdn_dt_bias"], 64)
    W["gnw"] = P["gdn_norm_w"]
    W["qnw"] = P["mla_q_norm_w"]
    W["kvnw"] = P["mla_kv_norm_w"]
    uq = P["mla_w_uq"].reshape(Q_LORA, HEADS, HEAD_DIM + ROPE)
    W["wuq"] = jnp.pad(uq, ((0, 0), (0, 0), (0, 256 - HEAD_DIM - ROPE))).reshape(Q_LORA, HEADS * 256).astype(BF16)
    ukv = P["mla_w_ukv"].reshape(KV_LORA, HEADS, 2, HEAD_DIM)
    W["wukv"] = ukv.transpose(0, 2, 1, 3).reshape(KV_LORA, 2 * HEADS * HEAD_DIM).astype(BF16)
    W["qn_w"] = P["qkn_q_nope"]
    W["qr_w"] = _pad_lanes(P["qkn_q_rope"])
    W["kn_w"] = P["qkn_k_nope"]
    W["kr_w"] = _pad_lanes(P["qkn_k_rope"])
    W["onw"] = P["mla_out_norm_w"]
    W["wout"] = P["w_out"].astype(BF16)
    return W


def _unpack_grads(G):
    g = G["wp"]
    uq = G["wuq"].reshape(Q_LORA, HEADS, 256)[:, :, :HEAD_DIM + ROPE].reshape(Q_LORA, HEADS * (HEAD_DIM + ROPE))
    ukv = G["wukv"].reshape(KV_LORA, 2, HEADS, HEAD_DIM).transpose(0, 2, 1, 3).reshape(KV_LORA, 2 * HEADS * HEAD_DIM)
    return {
        "ffn1_w_in": jnp.concatenate([G["f1_wg"], G["f1_wu"]], axis=1), "ffn1_w_out": G["f1_wo"],
        "w_in": jnp.concatenate([g[:, :2048], g[:, 2752:2760], g[:, 2304:2688], g[:, 2048:2304], g[:, 2688:2752]], axis=1),
        "gdn_conv_w": G["conv_w"], "gdn_a_log": G["alog_p"][:, 64:68], "gdn_dt_bias": G["dt_p"][:, 64:68],
        "gdn_norm_w": G["gnw"], "mla_q_norm_w": G["qnw"], "mla_w_uq": uq, "mla_kv_norm_w": G["kvnw"], "mla_w_ukv": ukv,
        "qkn_q_nope": G["qn_w"], "qkn_q_rope": G["qr_w"][:, :ROPE], "qkn_k_nope": G["kn_w"], "qkn_k_rope": G["kr_w"][:, :ROPE],
        "mla_out_norm_w": G["onw"], "w_out": G["wout"],
        "ffn2_w_in": jnp.concatenate([G["f2_wg"], G["f2_wu"]], axis=1), "ffn2_w_out": G["f2_wo"],
    }


def _rope_tables(positions):
    half = ROPE // 2
    inv_freq = ROPE_BASE ** (-jnp.arange(half, dtype=F32) / half)
    ang = positions.astype(F32)[:, None] * inv_freq
    cos, sin = jnp.cos(ang), jnp.sin(ang)
    zeros = jnp.zeros((positions.shape[0], 128 - ROPE), F32)
    return jnp.concatenate([cos, cos, zeros], axis=1), jnp.concatenate([-sin, sin, zeros], axis=1)


def _mod_fn(x, scale, shift):
    return (_modulate(x, scale, shift),)


def _ffn_fwd(x, scale, shift, gate_w, wg, wu, wo, tag):
    S = x.shape[0]
    tm = _pick(S, (512, 256, 128))
    (h,) = _rowwise(_mod_fn, [(x, tm, D_MODEL, 0)], [scale, shift], [(tm, D_MODEL, BF16)], S // tm, tag + "_mod")
    gate, up, hid = _ffn_up(h, wg, wu, tag + "_up")
    f, x_out = _mm(hid, wo, "nn", name=tag + "_down", out_dtypes=(F32, F32), extras=[x], extra_params=[gate_w],
                   epi=lambda acc, x_, g_: (acc, x_ + 0.5 * g_ * acc))
    return x_out, (h, gate, up, hid, f)


def _ffn_bwd(d_out, x, scale, shift, gate_w, wg, wu, wo, saved, tag):
    h, gate, up, hid, f = saved
    S = x.shape[0]
    tm = _pick(S, (512, 256, 128))
    n = S // tm
    (df,), (d_gate_w,) = _rowwise_bwd(lambda f_, g_: (0.5 * g_ * f_,), [(f, tm, D_MODEL, 0)], [], [gate_w],
                                      [(d_out, tm, D_MODEL, 0)], n, tag + "_dres", row_dtypes=(BF16,))
    d_gate, d_up = _mm(df, wo, "nt", name=tag + "_ddown", out_dtypes=(BF16, BF16), extras=[gate, up], epi=_swiglu_bwd)
    g_wo = _mm(hid, df, "tn", name=tag + "_gwo")
    g_wg = _mm(h, d_gate, "tn", name=tag + "_gwg")
    g_wu = _mm(h, d_up, "tn", name=tag + "_gwu")
    dh_gate = _mm(d_gate, wg, "nt", name=tag + "_dhg")
    dh = _mm(d_up, wu, "nt", name=tag + "_dhu", extras=[dh_gate], epi=lambda acc, e: (acc + e,))
    (dx,), (d_scale, d_shift) = _rowwise_bwd(_mod_fn, [(x, tm, D_MODEL, 0)], [], [scale, shift], [(dh, tm, D_MODEL, 0)],
                                             n, tag + "_dmod", adds=[(0, d_out)])
    return dx, d_scale, d_shift, d_gate_w, g_wg, g_wu, g_wo


def _mixer_fwd(x1, scale, shift, gate_w, cos_p, sin_p, W):
    S = x1.shape[0]
    tm = _pick(S, (512, 256, 128))
    tv = _pick(S, (256, 128))
    ta = _pick(S, (512, 256, 128))
    nc = S // CHUNK
    (h2,) = _rowwise(_mod_fn, [(x1, tm, D_MODEL, 0)], [scale, shift], [(tm, D_MODEL, BF16)], S // tm, "mix_mod")
    proj = _mm(h2, W["wp"], "nn", name="mix_proj")
    qkvc = _conv_fwd(proj, W["conv_w"], tv, "gdn_conv")
    kab = (proj, tv, 128, 21)
    q_a, k_a, v_a, gb = _rowwise(_gdn_pre_fn, [(qkvc, tv, 1536, 0), kab], [W["alog_p"], W["dt_p"]],
                                 [(tv, 512, F32)] * 3 + [(tv, 128, F32)], S // tv, "gdn_pre")
    intra = _rowwise(_gdn_intra_fn, [(q_a, CHUNK, 512, 0), (k_a, CHUNK, 512, 0), (v_a, CHUNK, 512, 0), (gb, CHUNK, 128, 0)],
                     [], [(CHUNK, 512, F32)] * 4 + [(CHUNK, CHUNK, F32)] * 4 + [(8, 512, F32)], nc, "gdn_intra")
    u, wk, qd, kd, qks, gl = intra[0], intra[1], intra[2], intra[3], tuple(intra[4:8]), intra[8]
    o_a, s_prev = _gdn_scan_fwd(u, wk, qd, kd, qks, gl, "gdn_scan")
    mla_params = [W["qnw"], W["kvnw"], W["wuq"], W["wukv"], W["qn_w"], W["qr_w"], W["kn_w"], W["kr_w"]]
    q_b, k_b, v_b = _rowwise(_mla_pre_fn,
                             [(proj, tv, 256, 8), (proj, tv, 384, 6), kab, (cos_p, tv, 128, 0), (sin_p, tv, 128, 0)],
                             mla_params, [(tv, 1024, BF16), (tv, 1024, BF16), (tv, 512, BF16)], S // tv, "mla_pre")
    o_b, lse = _attn_fwd(q_b, k_b, v_b, ta, "mla_attn")
    (mixed,) = _rowwise(_mix_post_fn, [(o_a, tv, 512, 0), (proj, tv, 512, 3), (o_b, tv, 512, 0)], [W["gnw"], W["onw"]],
                        [(tv, D_MODEL, BF16)], S // tv, "mix_post")
    y, x2 = _mm(mixed, W["wout"], "nn", name="mix_out", out_dtypes=(F32, F32), extras=[x1], extra_params=[gate_w],
                epi=lambda acc, x_, g_: (acc, x_ + g_ * acc))
    saved = (h2, proj, qkvc, q_a, k_a, v_a, gb, u, wk, qd, kd, qks, gl, s_prev, o_a, q_b, k_b, v_b, o_b, lse, mixed, y)
    return x2, saved


def _mixer_bwd(d_out, x1, scale, shift, gate_w, cos_p, sin_p, W, saved):
    (h2, proj, qkvc, q_a, k_a, v_a, gb, u, wk, qd, kd, qks, gl, s_prev, o_a, q_b, k_b, v_b, o_b, lse, mixed, y) = saved
    S = x1.shape[0]
    tm = _pick(S, (512, 256, 128))
    tv = _pick(S, (256, 128))
    ta = _pick(S, (512, 256, 128))
    nc = S // CHUNK
    G = {}
    (dy,), (G["g2"],) = _rowwise_bwd(lambda y_, g_: (g_ * y_,), [(y, tm, D_MODEL, 0)], [], [gate_w],
                                     [(d_out, tm, D_MODEL, 0)], S // tm, "mix_dres", row_dtypes=(BF16,))
    d_mixed = _mm(dy, W["wout"], "nt", name="mix_dout")
    G["wout"] = _mm(mixed, dy, "tn", name="mix_gwout")
    (do_a, dz, do_b), (G["gnw"], G["onw"]) = _rowwise_bwd(
        _mix_post_fn, [(o_a, tv, 512, 0), (proj, tv, 512, 3), (o_b, tv, 512, 0)], [], [W["gnw"], W["onw"]],
        [(d_mixed, tv, D_MODEL, 0)], S // tv, "mix_dpost")
    dq_b, dk_b, dv_b = _attn_bwd(q_b, k_b, v_b, o_b, lse, do_b, ta, "mla_dattn")
    kab = (proj, tv, 128, 21)
    mla_params = [W["qnw"], W["kvnw"], W["wuq"], W["wukv"], W["qn_w"], W["qr_w"], W["kn_w"], W["kr_w"]]
    (d_ckv, d_cq, d_kab), mla_grads = _rowwise_bwd(
        _mla_pre_fn, [(proj, tv, 256, 8), (proj, tv, 384, 6), kab], [(cos_p, tv, 128, 0), (sin_p, tv, 128, 0)], mla_params,
        [(dq_b, tv, 1024, 0), (dk_b, tv, 1024, 0), (dv_b, tv, 512, 0)], S // tv, "mla_dpre")
    for key, g in zip(("qnw", "kvnw", "wuq", "wukv", "qn_w", "qr_w", "kn_w", "kr_w"), mla_grads):
        G[key] = g
    scan_grads = _gdn_scan_bwd(u, wk, qd, kd, qks, gl, s_prev, do_a, "gdn_dscan")
    intra_douts = [(scan_grads[i], CHUNK, 512, 0) for i in range(4)] + [(scan_grads[4 + i], CHUNK, CHUNK, 0) for i in range(4)]
    intra_douts.append((scan_grads[8], 8, 512, 0))
    (dq_a, dk_a, dv_a, d_gb), _ = _rowwise_bwd(
        _gdn_intra_fn, [(q_a, CHUNK, 512, 0), (k_a, CHUNK, 512, 0), (v_a, CHUNK, 512, 0), (gb, CHUNK, 128, 0)], [], [],
        intra_douts, nc, "gdn_dintra")
    (d_qkvc, d_kab), (G["alog_p"], G["dt_p"]) = _rowwise_bwd(
        _gdn_pre_fn, [(qkvc, tv, 1536, 0), kab], [], [W["alog_p"], W["dt_p"]],
        [(dq_a, tv, 512, 0), (dk_a, tv, 512, 0), (dv_a, tv, 512, 0), (d_gb, tv, 128, 0)], S // tv, "gdn_dpre",
        adds=[(1, d_kab)])
    d_qkv, g_conv = _conv_bwd(proj, d_qkvc, W["conv_w"], tv, "gdn_dconv")
    G["conv_w"] = g_conv[:4]
    d_proj = jnp.concatenate([d_qkv, dz, d_ckv, d_cq, d_kab], axis=1).astype(BF16)
    G["wp"] = _mm(h2, d_proj, "tn", name="mix_gwp")
    dh2 = _mm(d_proj, W["wp"], "nt", name="mix_dproj")
    (dx1,), (G["s2"], G["sh2"]) = _rowwise_bwd(_mod_fn, [(x1, tm, D_MODEL, 0)], [], [scale, shift],
                                               [(dh2, tm, D_MODEL, 0)], S // tm, "mix_dmod", adds=[(0, d_out)])
    return dx1, G


def _local_step(x, target, mod, cos_p, sin_p, W):
    sh1, s1, g1, sh2, s2, g2, sh3, s3, g3 = [mod[:, D_MODEL * i:D_MODEL * (i + 1)] for i in range(N_MOD)]
    x1, saved1 = _ffn_fwd(x, s1, sh1, g1, W["f1_wg"], W["f1_wu"], W["f1_wo"], "ffn1")
    x2, saved2 = _mixer_fwd(x1, s2, sh2, g2, cos_p, sin_p, W)
    x3, saved3 = _ffn_fwd(x2, s3, sh3, g3, W["f2_wg"], W["f2_wu"], W["f2_wo"], "ffn2")
    dx3, loss_row = _loss_and_grad(x3, target, "loss")
    dx2, d_s3, d_sh3, d_g3, g_f2_wg, g_f2_wu, g_f2_wo = _ffn_bwd(dx3, x2, s3, sh3, g3, W["f2_wg"], W["f2_wu"], W["f2_wo"],
                                                                  saved3, "ffn2")
    dx1, G = _mixer_bwd(dx2, x1, s2, sh2, g2, cos_p, sin_p, W, saved2)
    dx, d_s1, d_sh1, d_g1, g_f1_wg, g_f1_wu, g_f1_wo = _ffn_bwd(dx1, x, s1, sh1, g1, W["f1_wg"], W["f1_wu"], W["f1_wo"],
                                                                saved1, "ffn1")
    d_mod = jnp.concatenate([d_sh1, d_s1, d_g1, G.pop("sh2"), G.pop("s2"), G.pop("g2"), d_sh3, d_s3, d_g3], axis=1)
    G.update(f1_wg=g_f1_wg, f1_wu=g_f1_wu, f1_wo=g_f1_wo, f2_wg=g_f2_wg, f2_wu=g_f2_wu, f2_wo=g_f2_wo)
    return loss_row, dx, d_mod, G


WEIGHT_NAMES = ("w_ada", "b_ada", "ffn1_w_in", "ffn1_w_out", "w_in", "gdn_conv_w", "gdn_a_log", "gdn_dt_bias", "gdn_norm_w",
                "mla_q_norm_w", "mla_w_uq", "mla_kv_norm_w", "mla_w_ukv", "qkn_q_nope", "qkn_q_rope", "qkn_k_nope",
                "qkn_k_rope", "mla_out_norm_w", "w_out", "ffn2_w_in", "ffn2_w_out")
SHARDED = (("ffn1_w_in", "col"), ("ffn1_w_out", "row"), ("w_in", "col"), ("gdn_conv_w", "col"), ("mla_w_uq", "col"),
           ("mla_w_ukv", "col"), ("w_out", "row"), ("ffn2_w_in", "col"), ("ffn2_w_out", "row"))
MOD_ROWS = N_MOD * D_MODEL // 128
SMALL = {"gdn_a_log": (MOD_ROWS, 1, 64, 4), "gdn_dt_bias": (MOD_ROWS + 1, 1, 64, 4), "gdn_norm_w": (MOD_ROWS + 2, 1, 0, 128),
         "mla_q_norm_w": (MOD_ROWS + 3, 3, 0, 384), "mla_kv_norm_w": (MOD_ROWS + 6, 2, 0, 256),
         "qkn_q_nope": (MOD_ROWS + 8, 1, 0, 128), "qkn_q_rope": (MOD_ROWS + 9, 1, 0, 64), "qkn_k_nope": (MOD_ROWS + 10, 1, 0, 128),
         "qkn_k_rope": (MOD_ROWS + 11, 1, 0, 64), "mla_out_norm_w": (MOD_ROWS + 12, 1, 0, 128)}
LOSS_ROW = MOD_ROWS + 13
SHEET_ROWS = 88


def _to_sheet(flat, dtype, sublanes):
    n = flat.shape[-1]
    unit = sublanes * 128
    pad = (-n) % unit
    flat = jnp.pad(flat.astype(dtype), [(0, 0)] * (flat.ndim - 1) + [(0, pad)])
    return flat.reshape(flat.shape[:-1] + ((n + pad) // 128, 128))


def _small_sheet(b_like, small):
    sheet = jnp.zeros((SHEET_ROWS, 128), F32).at[:MOD_ROWS].set(b_like.reshape(MOD_ROWS, 128))
    for name, (row, rows, lane, n) in SMALL.items():
        v = small[name].reshape(1, n)
        if rows == 1:
            sheet = sheet.at[row, lane:lane + n].set(v[0])
        else:
            sheet = sheet.at[row:row + rows].set(v.reshape(rows, 128))
    return sheet


def _from_small_sheet(sheet):
    out = {"b_ada": sheet[:MOD_ROWS].reshape(1, N_MOD * D_MODEL)}
    for name, (row, rows, lane, n) in SMALL.items():
        out[name] = sheet[row, lane:lane + n].reshape(1, n) if rows == 1 else sheet[row:row + rows].reshape(1, n)
    return out


def kernel(x, c, positions, w_ada, b_ada, ffn1_w_in, ffn1_w_out, w_in, gdn_conv_w, gdn_a_log, gdn_dt_bias, gdn_norm_w, mla_q_norm_w, mla_w_uq, mla_kv_norm_w, mla_w_ukv, qkn_q_nope, qkn_q_rope, qkn_k_nope, qkn_k_rope, mla_out_norm_w, w_out, ffn2_w_in, ffn2_w_out, loss_target, m_w_ada, m_b_ada, m_ffn1_w_in, m_ffn1_w_out, m_w_in, m_gdn_conv_w, m_gdn_a_log, m_gdn_dt_bias, m_gdn_norm_w, m_mla_q_norm_w, m_mla_w_uq, m_mla_kv_norm_w, m_mla_w_ukv, m_qkn_q_nope, m_qkn_q_rope, m_qkn_k_nope, m_qkn_k_rope, m_mla_out_norm_w, m_w_out, m_ffn2_w_in, m_ffn2_w_out, v_w_ada, v_b_ada, v_ffn1_w_in, v_ffn1_w_out, v_w_in, v_gdn_conv_w, v_gdn_a_log, v_gdn_dt_bias, v_gdn_norm_w, v_mla_q_norm_w, v_mla_w_uq, v_mla_kv_norm_w, v_mla_w_ukv, v_qkn_q_nope, v_qkn_q_rope, v_qkn_k_nope, v_qkn_k_rope, v_mla_out_norm_w, v_w_out, v_ffn2_w_in, v_ffn2_w_out):
    args = locals()
    w = {n: args[n] for n in WEIGHT_NAMES}
    m = {n: args["m_" + n] for n in WEIGHT_NAMES}
    v = {n: args["v_" + n] for n in WEIGHT_NAMES}
    me = 4 * lax.axis_index("x") + 2 * lax.axis_index("y") + lax.axis_index("c")
    cols = N_MOD * D_MODEL // N_DEV
    shard = {n: w[n][0] for n, _ in SHARDED}

    sc = c * _sigmoid(c)
    first = _to_sheet(jnp.concatenate([sc.reshape(-1), shard["gdn_conv_w"].reshape(-1)]), F32, 8)
    first_all = _all_gather(first, "gather_c")
    sc_all = first_all[:, :D_MODEL // 128].reshape(N_DEV, D_MODEL)
    n_taps = shard["gdn_conv_w"].size
    conv_all = first_all.reshape(N_DEV, -1)[:, D_MODEL:D_MODEL + n_taps].reshape(N_DEV, 4, -1)
    b_mine = lax.dynamic_slice(b_ada, (0, me * cols), (1, cols))
    mod_cols = _mm(sc_all, w_ada[0], "nn", name="ada_mod", extra_params=[b_mine], epi=lambda acc, b_: (acc + b_,))
    mod_pieces = _to_sheet(mod_cols, F32, 8)
    mod = _all_to_all(mod_pieces, "scatter_mod").reshape(N_DEV, -1)[:, :cols].reshape(1, N_MOD * D_MODEL)

    travel = [n for n, _ in SHARDED if n != "gdn_conv_w"]
    sizes = [shard[n].size for n in travel]
    flat_w = jnp.concatenate([shard[n].reshape(-1).astype(BF16) for n in travel])
    w_all = _all_gather(_to_sheet(flat_w, BF16, 16), "gather_w").reshape(N_DEV, -1)
    P, off = {}, 0
    for (n, kind), size in zip([s for s in SHARDED if s[0] != "gdn_conv_w"], sizes):
        r, cc = shard[n].shape
        piece = w_all[:, off:off + size].reshape(N_DEV, r, cc)
        P[n] = piece.transpose(1, 0, 2).reshape(r, N_DEV * cc) if kind == "col" else piece.reshape(N_DEV * r, cc)
        off += size
    P["gdn_conv_w"] = conv_all.transpose(1, 0, 2).reshape(4, -1)
    for n in SMALL:
        P[n] = w[n]
    W = _pack_weights(P)
    cos_p, sin_p = _rope_tables(positions[0])

    loss_row, dx, d_mod, G = _local_step(x[0], loss_target[0], mod, cos_p, sin_p, W)
    g_full = _unpack_grads(G)

    sheet = _small_sheet(d_mod, g_full).at[LOSS_ROW].set(loss_row[0, :128])
    sheets = _all_gather(sheet, "gather_small")
    summed = _sum_devices(sheets, "sum_small")
    d_mod_all = sheets[:, :MOD_ROWS].reshape(N_DEV, N_MOD * D_MODEL)
    d_mod_mine = lax.dynamic_slice(d_mod_all, (0, me * cols), (N_DEV, cols))
    grads = _from_small_sheet(summed)
    grads["w_ada"] = _mm(sc_all, d_mod_mine, "tn", name="ada_gw", hi=True)
    loss = summed[LOSS_ROW, 0]

    pieces = []
    for n, kind in SHARDED:
        r, cc = shard[n].shape
        g = g_full[n]
        pieces.append(g.reshape(r, N_DEV, cc).transpose(1, 0, 2).reshape(N_DEV, r * cc) if kind == "col"
                      else g.reshape(N_DEV, r * cc))
    g_sheet = _to_sheet(jnp.concatenate(pieces, axis=1), BF16, 512)
    g_mine = _sum_devices(_all_to_all(g_sheet, "scatter_grads"), "sum_grads").reshape(-1)
    off = 0
    for n, _ in SHARDED:
        grads[n] = g_mine[off:off + shard[n].size].reshape(shard[n].shape)
        off += shard[n].size

    delta, new_m, new_v = {}, {}, {}
    for n in ("w_ada",) + tuple(s[0] for s in SHARDED):
        delta[n], new_m[n], new_v[n] = _adamw(w[n][0], grads[n], m[n][0], v[n][0], "adamw_" + n)
    small_in = [_small_sheet(t["b_ada"], t) for t in (w, grads, m, v)]
    for res, out in zip(_adamw(*small_in, "adamw_small"), (delta, new_m, new_v)):
        out.update(_from_small_sheet(res))

    def shaped(d):
        return [d[n].reshape(w[n].shape) for n in WEIGHT_NAMES]

    return (loss, dx[None], *shaped(grads), *shaped(delta), *shaped(new_m), *shaped(new_v))
```

```python
import functools

import jax
import jax.numpy as jnp
from jax import lax
from jax.experimental import pallas as pl
from jax.experimental.pallas import tpu as pltpu

F32 = jnp.float32
BF16 = jnp.bfloat16

D_MODEL = 1024
D_FF = 2816
N_MOD = 9
HEADS = 4
HEAD_DIM = 128
CHUNK = 64
EPS = 1e-6
ROPE = 64
Q_LORA = 384
KV_LORA = 256
N_IN = 2760
N_IN_PACKED = 2816
ROPE_BASE = 10000.0
N_DEV = 8

ADAM_LR = 0.001
ADAM_B1 = 0.9
ADAM_B2 = 0.999
ADAM_EPS = 1e-08
ADAM_WD = 0.01
ADAM_STEP = 10

VMEM_LIMIT_BYTES = 56 * 1024 * 1024
MESH = pl.DeviceIdType.MESH


def _params(sem=None):
    return pltpu.CompilerParams(dimension_semantics=sem, vmem_limit_bytes=VMEM_LIMIT_BYTES)


def _pick(dim, prefs):
    for p in prefs:
        if dim % p == 0:
            return p
    return dim


_DIMS = {"nn": (((1,), (0,)), ((), ())), "nt": (((1,), (1,)), ((), ())), "tn": (((0,), (0,)), ((), ()))}


def _dot_raw(a, b, mode):
    return lax.dot_general(a.astype(BF16), b.astype(BF16), _DIMS[mode], preferred_element_type=F32)


def _dot_hi(a, b, mode="nn"):
    return lax.dot_general(a, b, _DIMS[mode], precision=lax.Precision.HIGHEST, preferred_element_type=F32)


@functools.partial(jax.custom_vjp, nondiff_argnums=(2,))
def _bdot(a, b, mode):
    return _dot_raw(a, b, mode)


def _bdot_fwd(a, b, mode):
    return _dot_raw(a, b, mode), (a, b)


def _bdot_bwd(mode, res, g):
    a, b = res
    if mode == "nn":
        return _dot_raw(g, b, "nt"), _dot_raw(a, g, "tn")
    if mode == "nt":
        return _dot_raw(g, b, "nn"), _dot_raw(g, a, "tn")
    return _dot_raw(b, g, "nt"), _dot_raw(a, g, "nn")


_bdot.defvjp(_bdot_fwd, _bdot_bwd)


def _mm(a, b, mode, *, name, out_dtypes=(F32,), epi=None, extras=(), extra_params=(), hi=False,
        tm=None, tn=None, tk=None):
    if mode == "nn":
        (M, K), (_, N) = a.shape, b.shape
    elif mode == "nt":
        (M, K), (N, _) = a.shape, b.shape
    else:
        (K, M), (_, N) = a.shape, b.shape
    tm = tm or _pick(M, (512, 1408, 256, 128) if mode == "tn" else (512, 384, 352, 256, 128))
    tn = tn or _pick(N, (1024, 1408, 768, 512, 384, 256, 128))
    tk = tk or _pick(K, (1024, 1408, 512, 384, 256, 128))
    a_spec = {"nn": pl.BlockSpec((tm, tk), lambda i, j, k: (i, k)), "nt": pl.BlockSpec((tm, tk), lambda i, j, k: (i, k)),
              "tn": pl.BlockSpec((tk, tm), lambda i, j, k: (k, i))}[mode]
    b_spec = {"nn": pl.BlockSpec((tk, tn), lambda i, j, k: (k, j)), "nt": pl.BlockSpec((tn, tk), lambda i, j, k: (j, k)),
              "tn": pl.BlockSpec((tk, tn), lambda i, j, k: (k, j))}[mode]
    mn_spec = pl.BlockSpec((tm, tn), lambda i, j, k: (i, j))
    return _mmg(a, b, mode, name=name, grid=(M // tm, N // tn, K // tk), a_spec=a_spec, b_spec=b_spec, out_spec=mn_spec,
                out_shapes=[jax.ShapeDtypeStruct((M, N), dt) for dt in out_dtypes], acc_shape=(tm, tn), epi=epi,
                extras=list(extras) + list(extra_params),
                extra_specs=[mn_spec] * len(extras) + [pl.BlockSpec((1, tn), lambda i, j, k: (0, j))] * len(extra_params),
                hi=hi)


def _mmg(a, b, mode, *, name, grid, a_spec, b_spec, out_spec, out_shapes, acc_shape, epi=None, extras=(),
         extra_specs=(), hi=False):
    nk = grid[2]
    n_e, n_o = len(extras), len(out_shapes)

    def body(*refs):
        a_ref, b_ref = refs[:2]
        e_refs = refs[2:2 + n_e]
        o_refs = refs[2 + n_e:2 + n_e + n_o]
        acc_ref = refs[-1]
        k = pl.program_id(2)

        @pl.when(k == 0)
        def _():
            acc_ref[...] = jnp.zeros_like(acc_ref)

        if hi:
            acc_ref[...] += _dot_hi(a_ref[...].astype(F32), b_ref[...].astype(F32), mode)
        else:
            acc_ref[...] += _dot_raw(a_ref[...], b_ref[...], mode)

        @pl.when(k == nk - 1)
        def _():
            acc = acc_ref[...]
            outs = (acc,) if epi is None else epi(acc, *[e[...].astype(F32) for e in e_refs])
            for o_ref, o in zip(o_refs, outs):
                o_ref[...] = o.astype(o_ref.dtype)

    outs = pl.pallas_call(
        body, name=name, grid=grid,
        in_specs=[a_spec, b_spec] + list(extra_specs),
        out_specs=[out_spec] * n_o,
        out_shape=list(out_shapes),
        scratch_shapes=[pltpu.VMEM(acc_shape, F32)],
        compiler_params=_params(("parallel", "parallel", "arbitrary")),
    )(a, b, *extras)
    return outs if n_o > 1 else outs[0]


def _row_spec(th, cw, ci):
    return pl.BlockSpec((th, cw), lambda i: (i, ci))


def _full_spec(shape):
    return pl.BlockSpec(shape, lambda i: (0,) * len(shape))


def _rowwise(fn, rows, params, outs, n_steps, name):
    n_r, n_p, n_o = len(rows), len(params), len(outs)

    def body(*refs):
        vals = [r[...].astype(F32) for r in refs[:n_r + n_p]]
        res = fn(*vals)
        for o_ref, o in zip(refs[n_r + n_p:], res):
            o_ref[...] = o.astype(o_ref.dtype)

    res = pl.pallas_call(
        body, name=name, grid=(n_steps,),
        in_specs=[_row_spec(th, cw, ci) for (_, th, cw, ci) in rows] + [_full_spec(p.shape) for p in params],
        out_specs=[_row_spec(th, cw, 0) for (th, cw, _) in outs],
        out_shape=[jax.ShapeDtypeStruct((n_steps * th, cw), dt) for (th, cw, dt) in outs],
        compiler_params=_params(("parallel",)),
    )(*[r[0] for r in rows], *params)
    return res


def _rowwise_bwd(fn, rows, aux, params, douts, n_steps, name, row_dtypes=None, adds=()):
    n_r, n_a, n_p, n_d, n_add = len(rows), len(aux), len(params), len(douts), len(adds)
    row_dtypes = row_dtypes or (F32,) * n_r

    def body(*refs):
        it = iter(refs)
        r_vals = [next(it)[...].astype(F32) for _ in range(n_r)]
        a_vals = [next(it)[...].astype(F32) for _ in range(n_a)]
        p_vals = [next(it)[...].astype(F32) for _ in range(n_p)]
        d_vals = [next(it)[...].astype(F32) for _ in range(n_d)]
        add_vals = [next(it)[...].astype(F32) for _ in range(n_add)]
        dr_refs = [next(it) for _ in range(n_r)]
        dp_refs = [next(it) for _ in range(n_p)]

        def f(*rp):
            return tuple(fn(*rp[:n_r], *a_vals, *rp[n_r:]))

        _, vjp = jax.vjp(f, *r_vals, *p_vals)
        grads = list(vjp(tuple(d_vals)))
        for (ri, _), av in zip(adds, add_vals):
            grads[ri] = grads[ri] + av
        for dr_ref, g in zip(dr_refs, grads[:n_r]):
            dr_ref[...] = g.astype(dr_ref.dtype)

        @pl.when(pl.program_id(0) == 0)
        def _():
            for dp_ref in dp_refs:
                dp_ref[...] = jnp.zeros_like(dp_ref)

        for dp_ref, g in zip(dp_refs, grads[n_r:]):
            dp_ref[...] += g

    all_rows = list(rows) + list(aux) + list(douts) + [(arr,) + tuple(rows[ri][1:3]) + (0,) for ri, arr in adds]
    in_specs = ([_row_spec(th, cw, ci) for (_, th, cw, ci) in list(rows) + list(aux)]
                + [_full_spec(p.shape) for p in params]
                + [_row_spec(th, cw, ci) for (_, th, cw, ci) in all_rows[n_r + n_a:]])
    res = pl.pallas_call(
        body, name=name, grid=(n_steps,),
        in_specs=in_specs,
        out_specs=[_row_spec(th, cw, 0) for (_, th, cw, _) in rows] + [_full_spec(p.shape) for p in params],
        out_shape=[jax.ShapeDtypeStruct((n_steps * th, cw), dt) for (_, th, cw, _), dt in zip(rows, row_dtypes)]
        + [jax.ShapeDtypeStruct(p.shape, F32) for p in params],
        compiler_params=_params(("arbitrary",)),
    )(*[r[0] for r in list(rows) + list(aux)], *params, *[r[0] for r in all_rows[n_r + n_a:]])
    return res[:n_r], res[n_r:]


def _sigmoid(x):
    return lax.logistic(x)


def _silu(x):
    return x * _sigmoid(x)


def _rms(x, w=None, n=None):
    n = n or x.shape[-1]
    y = x * lax.rsqrt(jnp.sum(x * x, axis=-1, keepdims=True) * (1.0 / n) + EPS)
    return y if w is None else y * w


def _modulate(x, scale, shift):
    return _rms(x) * (1.0 + scale) + shift


def _softplus(x):
    return jnp.maximum(x, 0.0) + jnp.log1p(jnp.exp(-jnp.abs(x)))


@jax.custom_vjp
def _rot_half64(x):
    lane = lax.broadcasted_iota(jnp.int32, x.shape, 1)
    up = pltpu.roll(x, 96, 1)
    down = pltpu.roll(x, 32, 1)
    return jnp.where(lane < 32, up, jnp.where(lane < 64, down, 0.0))


_rot_half64.defvjp(lambda x: (_rot_half64(x), None), lambda _, g: (_rot_half64(g),))


def _rope128(x, cos_p, sin_p):
    return x * cos_p + _rot_half64(x) * sin_p


def _gdn_pre_fn(qkvc, kab, alog_p, dt_p):
    a = _silu(qkvc)
    qs, ks = [], []
    for h in range(HEADS):
        qh = a[:, HEAD_DIM * h:HEAD_DIM * (h + 1)]
        kh = a[:, 512 + HEAD_DIM * h:512 + HEAD_DIM * (h + 1)]
        qs.append(qh * lax.rsqrt(jnp.sum(qh * qh, axis=-1, keepdims=True) + EPS) * (HEAD_DIM ** -0.5))
        ks.append(kh * lax.rsqrt(jnp.sum(kh * kh, axis=-1, keepdims=True) + EPS))
    lane = lax.broadcasted_iota(jnp.int32, kab.shape, 1)
    g_full = -jnp.exp(alog_p) * _softplus(kab + dt_p)
    b_full = _sigmoid(kab)
    gb = jnp.where((lane >= 64) & (lane < 68), g_full, jnp.where((lane >= 68) & (lane < 72), b_full, 0.0))
    return jnp.concatenate(qs, axis=1), jnp.concatenate(ks, axis=1), a[:, 1024:1536], gb


def _intra_head(q, k, v, g_col, b_col):
    c = CHUNK
    row = lax.broadcasted_iota(jnp.int32, (c, c), 0)
    col = lax.broadcasted_iota(jnp.int32, (c, c), 1)
    incl, strict, eye = row >= col, row > col, row == col
    tri = jnp.where(incl, 1.0, 0.0).astype(F32)
    ident = jnp.where(eye, 1.0, 0.0).astype(F32)
    g_wide = _dot_hi(tri, jnp.broadcast_to(g_col, (c, HEAD_DIM)))
    g_i = g_wide[:, :c]
    g_j = jnp.sum(jnp.where(eye, g_i, 0.0), axis=0, keepdims=True)
    decay = jnp.where(incl, jnp.exp(jnp.where(incl, g_i - g_j, 0.0)), 0.0)
    kk = _bdot(k, k, "nt")
    a_mat = jnp.where(strict, b_col * kk * decay, 0.0)
    x_pow = -a_mat
    inv = ident + x_pow
    for _ in range(5):
        x_pow = _dot_hi(x_pow, x_pow)
        inv = inv + _dot_hi(inv, x_pow)
    e_wide = jnp.exp(g_wide)
    u = _dot_hi(inv, v * b_col)
    wk = _dot_hi(inv, k * b_col * e_wide)
    qk = _bdot(q, k, "nt") * decay
    last = lax.broadcasted_iota(jnp.int32, (c, HEAD_DIM), 0) == c - 1
    g_last = jnp.sum(jnp.where(last, g_wide, 0.0), axis=0, keepdims=True)
    qd = q * e_wide
    kd = k * jnp.exp(g_last - g_wide)
    gl = jnp.broadcast_to(jnp.exp(g_last), (8, HEAD_DIM))
    return u, wk, qd, kd, qk, gl


INTRA_ROWS = (256, 128, 64)

_BNN = (((2,), (1,)), ((0,), (0,)))
_BNT = (((2,), (2,)), ((0,), (0,)))


def _split_bf16(a):
    hi = a.astype(BF16)
    return hi, (a - hi.astype(F32)).astype(BF16)


def _dot3_raw(a, b, dims):
    a_hi, a_lo = _split_bf16(a)
    b_hi, b_lo = _split_bf16(b)
    dot = lambda x_, y_: lax.dot_general(x_, y_, dims, preferred_element_type=F32)
    return dot(a_hi, b_hi) + (dot(a_hi, b_lo) + dot(a_lo, b_hi))


@functools.partial(jax.custom_vjp, nondiff_argnums=(2,))
def _dot3(a, b, nt):
    return _dot3_raw(a, b, _BNT if nt else _BNN)


def _dot3_fwd(a, b, nt):
    return _dot3(a, b, nt), (a, b)


def _dot3_bwd(nt, res, g):
    a, b = res
    g_t = jnp.swapaxes(g, 1, 2)
    if nt:
        return _dot3_raw(g, b, _BNN), _dot3_raw(g_t, a, _BNN)
    return _dot3_raw(g, b, _BNT), _dot3_raw(jnp.swapaxes(a, 1, 2), g, _BNN)


_dot3.defvjp(_dot3_fwd, _dot3_bwd)


@functools.partial(jax.custom_vjp, nondiff_argnums=(2,))
def _bdot_b(a, b, nt):
    return lax.dot_general(a.astype(BF16), b.astype(BF16), _BNT if nt else _BNN, preferred_element_type=F32)


def _bdot_b_fwd(a, b, nt):
    return _bdot_b(a, b, nt), (a, b)


def _bdot_b_bwd(nt, res, g):
    a, b = res
    dot = lambda x_, y_, d_: lax.dot_general(x_.astype(BF16), y_.astype(BF16), d_, preferred_element_type=F32)
    if nt:
        return dot(g, b, _BNN), dot(jnp.swapaxes(g, 1, 2), a, _BNN)
    return dot(g, b, _BNT), dot(jnp.swapaxes(a, 1, 2), g, _BNN)


_bdot_b.defvjp(_bdot_b_fwd, _bdot_b_bwd)


def _intra_batched(q, k, v, g_col, b_col):
    c = CHUNK
    nb = q.shape[0]
    row = lax.broadcasted_iota(jnp.int32, (1, c, c), 1)
    col = lax.broadcasted_iota(jnp.int32, (1, c, c), 2)
    incl, strict, eye = row >= col, row > col, row == col
    tri = jnp.broadcast_to(jnp.where(incl, 1.0, 0.0).astype(F32), (nb, c, c))
    ident = jnp.where(eye, 1.0, 0.0).astype(F32)
    g_wide = _dot3(tri, jnp.broadcast_to(g_col, (nb, c, HEAD_DIM)), False)
    g_i = g_wide[:, :, :c]
    g_j = jnp.sum(jnp.where(eye, g_i, 0.0), axis=1, keepdims=True)
    decay = jnp.where(incl, jnp.exp(jnp.where(incl, g_i - g_j, 0.0)), 0.0)
    kk = _bdot_b(k, k, True)
    a_mat = jnp.where(strict, b_col * kk * decay, 0.0)
    x_pow = -a_mat
    inv = ident + x_pow
    for _ in range(5):
        x_pow = _dot3(x_pow, x_pow, False)
        inv = inv + _dot3(inv, x_pow, False)
    e_wide = jnp.exp(g_wide)
    u = _dot3(inv, v * b_col, False)
    wk = _dot3(inv, k * b_col * e_wide, False)
    qk = _bdot_b(q, k, True) * decay
    last = lax.broadcasted_iota(jnp.int32, (1, c, HEAD_DIM), 1) == c - 1
    g_last = jnp.sum(jnp.where(last, g_wide, 0.0), axis=1, keepdims=True)
    qd = q * e_wide
    kd = k * jnp.exp(g_last - g_wide)
    gl = jnp.broadcast_to(jnp.exp(g_last), (nb, 8, HEAD_DIM))
    return u, wk, qd, kd, qk, gl


def _gdn_intra_fn(q, k, v, gb):
    t = q.shape[0]
    nch = t // CHUNK
    lane = lax.broadcasted_iota(jnp.int32, gb.shape, 1)

    def heads_first(x_):
        return jnp.concatenate([x_[:, HEAD_DIM * h:HEAD_DIM * (h + 1)].reshape(nch, CHUNK, HEAD_DIM) for h in range(HEADS)],
                               axis=0)

    def column(first_lane):
        return jnp.concatenate([jnp.sum(jnp.where(lane == first_lane + h, gb, 0.0), axis=1, keepdims=True)
                                .reshape(nch, CHUNK, 1) for h in range(HEADS)], axis=0)

    u, wk, qd, kd, qk, gl = _intra_batched(heads_first(q), heads_first(k), heads_first(v), column(64), column(68))

    def rows_first(x_):
        r, w_ = x_.shape[1], x_.shape[2]
        return jnp.concatenate([x_[nch * h:nch * (h + 1)].reshape(nch * r, w_) for h in range(HEADS)], axis=1)

    qks = [qk[nch * h:nch * (h + 1)].reshape(t, CHUNK) for h in range(HEADS)]
    return (rows_first(u), rows_first(wk), rows_first(qd), rows_first(kd), *qks, rows_first(gl))


def _scan_step(s0, u, wk, qd, kd, qk, gl):
    v_new = u - _bdot(wk, s0, "nn")
    o = _bdot(qd, s0, "nn") + _bdot(qk, v_new, "nn")
    s1 = s0 * gl[0:1, :] + _bdot(kd, v_new, "tn")
    return o, s1


def _mix_post_fn(o_a, z, o_b, gnw, onw):
    parts = [_rms(o_a[:, HEAD_DIM * h:HEAD_DIM * (h + 1)], gnw) * _silu(z[:, HEAD_DIM * h:HEAD_DIM * (h + 1)])
             for h in range(HEADS)]
    parts += [_rms(o_b[:, HEAD_DIM * h:HEAD_DIM * (h + 1)], onw) for h in range(HEADS)]
    return (jnp.concatenate(parts, axis=1),)


def _mla_pre_fn(ckv, cq, kab, cos_p, sin_p, qnw, kvnw, wuq, wukv, qn_w, qr_w, kn_w, kr_w):
    scale = (HEAD_DIM + ROPE) ** -0.5
    qf = _bdot(_rms(cq, qnw), wuq, "nn")
    kvf = _bdot(_rms(ckv, kvnw), wukv, "nn")
    lane = lax.broadcasted_iota(jnp.int32, kab.shape, 1)
    kr = _rope128(_rms(jnp.where(lane < ROPE, kab, 0.0), kr_w, n=ROPE), cos_p, sin_p)
    qs, ks = [], []
    for h in range(HEADS):
        qn = _rms(qf[:, 256 * h:256 * h + 128], qn_w) * scale
        qr = _rope128(_rms(qf[:, 256 * h + 128:256 * h + 256], qr_w, n=ROPE), cos_p, sin_p) * scale
        qs += [qn, qr]
        ks += [_rms(kvf[:, 128 * h:128 * (h + 1)], kn_w), kr]
    return jnp.concatenate(qs, axis=1), jnp.concatenate(ks, axis=1), kvf[:, 512:]


def _conv_fwd(proj, conv_w, tm, name):
    S = proj.shape[0]
    C = 1536
    nb = tm // 8

    def body(x_ref, prev_ref, w_ref, o_ref, ext_ref):
        i = pl.program_id(0)
        ext_ref[0:8, :] = jnp.where(i > 0, prev_ref[...], 0.0)
        ext_ref[8:, :] = x_ref[...]
        acc = jnp.zeros((tm, C), F32)
        for k in range(4):
            acc = acc + w_ref[k:k + 1, :] * ext_ref[pl.ds(5 + k, tm), :]
        o_ref[...] = acc

    return pl.pallas_call(
        body, name=name, grid=(S // tm,),
        in_specs=[pl.BlockSpec((tm, C), lambda i: (i, 0)),
                  pl.BlockSpec((8, C), lambda i: (jnp.maximum(i * nb - 1, 0), 0)),
                  pl.BlockSpec((4, C), lambda i: (0, 0))],
        out_specs=pl.BlockSpec((tm, C), lambda i: (i, 0)),
        out_shape=jax.ShapeDtypeStruct((S, C), F32),
        scratch_shapes=[pltpu.VMEM((tm + 8, C), F32)],
        compiler_params=_params(("arbitrary",)),
    )(proj, proj, conv_w)


def _conv_bwd(proj, dout, conv_w, tm, name):
    S = proj.shape[0]
    C = 1536
    nb = tm // 8
    n_steps = S // tm

    def body(x_ref, prev_ref, d_ref, next_ref, w_ref, dx_ref, dw_ref, xext_ref, dext_ref):
        i = pl.program_id(0)
        xext_ref[0:8, :] = jnp.where(i > 0, prev_ref[...], 0.0)
        xext_ref[8:, :] = x_ref[...]
        dext_ref[0:tm, :] = d_ref[...]
        dext_ref[tm:, :] = jnp.where(i < n_steps - 1, next_ref[...], 0.0)
        d = d_ref[...]
        acc = jnp.zeros((tm, C), F32)
        dws = []
        for k in range(4):
            acc = acc + w_ref[k:k + 1, :] * dext_ref[pl.ds(3 - k, tm), :]
            dws.append(jnp.sum(d * xext_ref[pl.ds(5 + k, tm), :], axis=0, keepdims=True))
        dx_ref[...] = acc

        @pl.when(i == 0)
        def _():
            dw_ref[...] = jnp.zeros_like(dw_ref)

        dw_ref[...] += jnp.concatenate(dws + [jnp.zeros((4, C), F32)], axis=0)

    return pl.pallas_call(
        body, name=name, grid=(n_steps,),
        in_specs=[pl.BlockSpec((tm, C), lambda i: (i, 0)),
                  pl.BlockSpec((8, C), lambda i: (jnp.maximum(i * nb - 1, 0), 0)),
                  pl.BlockSpec((tm, C), lambda i: (i, 0)),
                  pl.BlockSpec((8, C), lambda i: (jnp.minimum((i + 1) * nb, S // 8 - 1), 0)),
                  pl.BlockSpec((4, C), lambda i: (0, 0))],
        out_specs=[pl.BlockSpec((tm, C), lambda i: (i, 0)), pl.BlockSpec((8, C), lambda i: (0, 0))],
        out_shape=[jax.ShapeDtypeStruct((S, C), F32), jax.ShapeDtypeStruct((8, C), F32)],
        scratch_shapes=[pltpu.VMEM((tm + 8, C), F32), pltpu.VMEM((tm + 8, C), F32)],
        compiler_params=_params(("arbitrary",)),
    )(proj, proj, dout, dout, conv_w)


def _gdn_scan_fwd(u, wk, qd, kd, qks, gl, name):
    S = u.shape[0]
    nc = S // CHUNK
    W = HEADS * HEAD_DIM

    def body(u_ref, wk_ref, qd_ref, kd_ref, qk0, qk1, qk2, qk3, gl_ref, o_ref, sp_ref, s_ref):
        @pl.when(pl.program_id(0) == 0)
        def _():
            s_ref[...] = jnp.zeros_like(s_ref)

        for h, qk_ref in enumerate((qk0, qk1, qk2, qk3)):
            sl = slice(HEAD_DIM * h, HEAD_DIM * (h + 1))
            s0 = s_ref[h]
            sp_ref[0, h] = s0
            o, s1 = _scan_step(s0, u_ref[:, sl], wk_ref[:, sl], qd_ref[:, sl], kd_ref[:, sl], qk_ref[...], gl_ref[:, sl])
            o_ref[:, sl] = o
            s_ref[h] = s1

    row = pl.BlockSpec((CHUNK, W), lambda n: (n, 0))
    qk_spec = pl.BlockSpec((CHUNK, CHUNK), lambda n: (n, 0))
    return pl.pallas_call(
        body, name=name, grid=(nc,),
        in_specs=[row, row, row, row, qk_spec, qk_spec, qk_spec, qk_spec, pl.BlockSpec((8, W), lambda n: (n, 0))],
        out_specs=[row, pl.BlockSpec((1, HEADS, HEAD_DIM, HEAD_DIM), lambda n: (n, 0, 0, 0))],
        out_shape=[jax.ShapeDtypeStruct((S, W), F32), jax.ShapeDtypeStruct((nc, HEADS, HEAD_DIM, HEAD_DIM), F32)],
        scratch_shapes=[pltpu.VMEM((HEADS, HEAD_DIM, HEAD_DIM), F32)],
        compiler_params=_params(("arbitrary",)),
    )(u, wk, qd, kd, *qks, gl)


def _gdn_scan_bwd(u, wk, qd, kd, qks, gl, s_prev, d_o, name):
    S = u.shape[0]
    nc = S // CHUNK
    W = HEADS * HEAD_DIM

    def body(u_ref, wk_ref, qd_ref, kd_ref, qk0, qk1, qk2, qk3, gl_ref, sp_ref, do_ref,
             du_ref, dwk_ref, dqd_ref, dkd_ref, dqk0, dqk1, dqk2, dqk3, dgl_ref, ds_ref):
        @pl.when(pl.program_id(0) == 0)
        def _():
            ds_ref[...] = jnp.zeros_like(ds_ref)

        for h, (qk_ref, dqk_ref) in enumerate(zip((qk0, qk1, qk2, qk3), (dqk0, dqk1, dqk2, dqk3))):
            sl = slice(HEAD_DIM * h, HEAD_DIM * (h + 1))
            _, vjp = jax.vjp(_scan_step, sp_ref[0, h], u_ref[:, sl], wk_ref[:, sl], qd_ref[:, sl], kd_ref[:, sl],
                             qk_ref[...], gl_ref[:, sl])
            ds0, du, dwk, dqd, dkd, dqk, dgl = vjp((do_ref[:, sl], ds_ref[h]))
            ds_ref[h] = ds0
            du_ref[:, sl] = du
            dwk_ref[:, sl] = dwk
            dqd_ref[:, sl] = dqd
            dkd_ref[:, sl] = dkd
            dqk_ref[...] = dqk
            dgl_ref[:, sl] = dgl

    rev = lambda n: (nc - 1 - n, 0)
    row = pl.BlockSpec((CHUNK, W), rev)
    qk_spec = pl.BlockSpec((CHUNK, CHUNK), rev)
    gl_spec = pl.BlockSpec((8, W), rev)
    qk_shape = jax.ShapeDtypeStruct((S, CHUNK), F32)
    row_shape = jax.ShapeDtypeStruct((S, W), F32)
    return pl.pallas_call(
        body, name=name, grid=(nc,),
        in_specs=[row, row, row, row, qk_spec, qk_spec, qk_spec, qk_spec, gl_spec,
                  pl.BlockSpec((1, HEADS, HEAD_DIM, HEAD_DIM), lambda n: (nc - 1 - n, 0, 0, 0)), row],
        out_specs=[row, row, row, row, qk_spec, qk_spec, qk_spec, qk_spec, gl_spec],
        out_shape=[row_shape] * 4 + [qk_shape] * 4 + [jax.ShapeDtypeStruct((nc * 8, W), F32)],
        scratch_shapes=[pltpu.VMEM((HEADS, HEAD_DIM, HEAD_DIM), F32)],
        compiler_params=_params(("arbitrary",)),
    )(u, wk, qd, kd, *qks, gl, s_prev, d_o)


NEG = -1e30


def _chunk_mask(i, j, t):
    r = (i * t + lax.broadcasted_iota(jnp.int32, (t, t), 0)) // CHUNK
    c = (j * t + lax.broadcasted_iota(jnp.int32, (t, t), 1)) // CHUNK
    return c <= r


def _attn_fwd(q, k, v, t, name):
    S = q.shape[0]
    n = S // t

    def body(q_ref, k_ref, v_ref, o_ref, lse_ref, m_ref, l_ref, acc_ref):
        i, j = pl.program_id(1), pl.program_id(2)

        @pl.when(j == 0)
        def _():
            m_ref[...] = jnp.full_like(m_ref, NEG)
            l_ref[...] = jnp.zeros_like(l_ref)
            acc_ref[...] = jnp.zeros_like(acc_ref)

        def update(masked):
            s = _dot_raw(q_ref[...], k_ref[...], "nt")
            if masked:
                s = jnp.where(_chunk_mask(i, j, t), s, NEG)
            m_old = m_ref[...]
            m_new = jnp.maximum(m_old, jnp.max(s, axis=1, keepdims=True))
            p = jnp.exp(s - m_new)
            alpha = jnp.exp(m_old - m_new)
            l_ref[...] = alpha * l_ref[...] + jnp.sum(p, axis=1, keepdims=True)
            acc_ref[...] = alpha * acc_ref[...] + _dot_raw(p, v_ref[...], "nn")
            m_ref[...] = m_new

        @pl.when(j < i)
        def _():
            update(False)

        @pl.when(j == i)
        def _():
            update(True)
            o_ref[...] = acc_ref[...] / l_ref[...]
            lse_ref[...] = jnp.broadcast_to(m_ref[...] + jnp.log(l_ref[...]), (t, HEAD_DIM))

    return pl.pallas_call(
        body, name=name, grid=(HEADS, n, n),
        in_specs=[pl.BlockSpec((t, 256), lambda h, i, j: (i, h)),
                  pl.BlockSpec((t, 256), lambda h, i, j: (jnp.minimum(j, i), h)),
                  pl.BlockSpec((t, HEAD_DIM), lambda h, i, j: (jnp.minimum(j, i), h))],
        out_specs=[pl.BlockSpec((t, HEAD_DIM), lambda h, i, j: (i, h)), pl.BlockSpec((t, HEAD_DIM), lambda h, i, j: (i, h))],
        out_shape=[jax.ShapeDtypeStruct((S, HEADS * HEAD_DIM), F32)] * 2,
        scratch_shapes=[pltpu.VMEM((t, 1), F32), pltpu.VMEM((t, 1), F32), pltpu.VMEM((t, HEAD_DIM), F32)],
        compiler_params=_params(("parallel", "parallel", "arbitrary")),
    )(q, k, v)


def _attn_bwd(q, k, v, o, lse, d_o, t, name):
    S = q.shape[0]
    n = S // t

    def body(q_ref, k_ref, v_ref, o_ref, lse_ref, do_ref, dq_ref, dk_ref, dv_ref):
        j, i = pl.program_id(1), pl.program_id(2)

        @pl.when(i == 0)
        def _():
            dk_ref[...] = jnp.zeros_like(dk_ref)
            dv_ref[...] = jnp.zeros_like(dv_ref)

        def update(masked):
            rows = pl.ds(pl.multiple_of(i * t, t), t)
            d_out = do_ref[...]
            delta = jnp.sum(d_out * o_ref[...], axis=1, keepdims=True)
            s = _dot_raw(q_ref[...], k_ref[...], "nt")
            p = jnp.exp(s - lse_ref[:, 0:1])
            if masked:
                p = jnp.where(_chunk_mask(i, j, t), p, 0.0)
            dv_ref[...] += _dot_raw(p, d_out, "tn")
            dp = _dot_raw(d_out, v_ref[...], "nt")
            ds = p * (dp - delta)
            dk_ref[...] += _dot_raw(ds, q_ref[...], "tn")
            dq_new = _dot_raw(ds, k_ref[...], "nn")

            @pl.when(j == 0)
            def _():
                dq_ref[rows, :] = dq_new

            @pl.when(j > 0)
            def _():
                dq_ref[rows, :] += dq_new

        @pl.when(i > j)
        def _():
            update(False)

        @pl.when(i == j)
        def _():
            update(True)

    qi = lambda h, j, i: (jnp.maximum(i, j), h)
    return pl.pallas_call(
        body, name=name, grid=(HEADS, n, n),
        in_specs=[pl.BlockSpec((t, 256), qi),
                  pl.BlockSpec((t, 256), lambda h, j, i: (j, h)),
                  pl.BlockSpec((t, HEAD_DIM), lambda h, j, i: (j, h)),
                  pl.BlockSpec((t, HEAD_DIM), qi), pl.BlockSpec((t, HEAD_DIM), qi), pl.BlockSpec((t, HEAD_DIM), qi)],
        out_specs=[pl.BlockSpec((S, 256), lambda h, j, i: (0, h)),
                   pl.BlockSpec((t, 256), lambda h, j, i: (j, h)),
                   pl.BlockSpec((t, HEAD_DIM), lambda h, j, i: (j, h))],
        out_shape=[jax.ShapeDtypeStruct((S, HEADS * 256), F32), jax.ShapeDtypeStruct((S, HEADS * 256), F32),
                   jax.ShapeDtypeStruct((S, HEADS * HEAD_DIM), F32)],
        compiler_params=_params(("arbitrary", "arbitrary", "arbitrary")),
    )(q, k, v, o, lse, d_o)


FFN_PIECE = 2 * D_FF // N_DEV
HID_PIECES = D_FF // FFN_PIECE


def _ffn_up(h, w8, name):
    S = h.shape[0]
    tm = _pick(S, (512, 256, 128))

    def body(h_ref, wg_ref, wu_ref, g_ref, u_ref, hid_ref):
        gate = _dot_raw(h_ref[...], wg_ref[...], "nn")
        up = _dot_raw(h_ref[...], wu_ref[...], "nn")
        g_ref[...] = gate.astype(BF16)
        u_ref[...] = up.astype(BF16)
        hid_ref[...] = (_silu(gate) * up).astype(BF16)

    o_spec = pl.BlockSpec((None, tm, FFN_PIECE), lambda i, j: (j, i, 0))
    return pl.pallas_call(
        body, name=name, grid=(S // tm, HID_PIECES),
        in_specs=[pl.BlockSpec((tm, D_MODEL), lambda i, j: (i, 0)),
                  pl.BlockSpec((None, D_MODEL, FFN_PIECE), lambda i, j: (j, 0, 0)),
                  pl.BlockSpec((None, D_MODEL, FFN_PIECE), lambda i, j: (j + HID_PIECES, 0, 0))],
        out_specs=[o_spec] * 3,
        out_shape=[jax.ShapeDtypeStruct((HID_PIECES, S, FFN_PIECE), BF16)] * 3,
        compiler_params=_params(("parallel", "parallel")),
    )(h, w8, w8)


def _ffn_dh(d_gate, d_up, w8, name):
    S = d_gate.shape[1]
    tm = _pick(S, (512, 256, 128))

    def body(dg_ref, du_ref, wg_ref, wu_ref, o_ref):
        @pl.when(pl.program_id(1) == 0)
        def _():
            o_ref[...] = jnp.zeros_like(o_ref)

        o_ref[...] += _dot_raw(dg_ref[...], wg_ref[...], "nt") + _dot_raw(du_ref[...], wu_ref[...], "nt")

    d_spec = pl.BlockSpec((None, tm, FFN_PIECE), lambda i, k: (k, i, 0))
    return pl.pallas_call(
        body, name=name, grid=(S // tm, HID_PIECES),
        in_specs=[d_spec, d_spec,
                  pl.BlockSpec((None, D_MODEL, FFN_PIECE), lambda i, k: (k, 0, 0)),
                  pl.BlockSpec((None, D_MODEL, FFN_PIECE), lambda i, k: (k + HID_PIECES, 0, 0))],
        out_specs=pl.BlockSpec((tm, D_MODEL), lambda i, k: (i, 0)),
        out_shape=jax.ShapeDtypeStruct((S, D_MODEL), F32),
        compiler_params=_params(("parallel", "arbitrary")),
    )(d_gate, d_up, w8, w8)


def _swiglu_bwd(d_hid, gate, up):
    sg = _sigmoid(gate)
    return d_hid * up * (sg * (1.0 + gate * (1.0 - sg))), d_hid * (gate * sg)


def _loss_and_grad(x3, target, name):
    S = x3.shape[0]
    tm = _pick(S, (512, 256, 128))
    n = S // tm

    def body(x_ref, t_ref, dx_ref, l_ref):
        i = pl.program_id(0)
        diff = x_ref[...] - t_ref[...]
        dx_ref[...] = diff * (1.0 / D_MODEL)

        @pl.when(i == 0)
        def _():
            l_ref[...] = jnp.zeros_like(l_ref)

        l_ref[...] += jnp.sum(diff * diff, axis=0, keepdims=True)

        @pl.when(i == n - 1)
        def _():
            l_ref[...] = jnp.full(l_ref.shape, (0.5 / D_MODEL) * jnp.sum(l_ref[...]), F32)

    return pl.pallas_call(
        body, name=name, grid=(n,),
        in_specs=[pl.BlockSpec((tm, D_MODEL), lambda i: (i, 0))] * 2,
        out_specs=[pl.BlockSpec((tm, D_MODEL), lambda i: (i, 0)), pl.BlockSpec((1, D_MODEL), lambda i: (0, 0))],
        out_shape=[jax.ShapeDtypeStruct((S, D_MODEL), F32), jax.ShapeDtypeStruct((1, D_MODEL), F32)],
        compiler_params=_params(("arbitrary",)),
    )(x3, target)


def _adamw(w, g, m, v, name):
    R, C = w.shape
    tr = _pick(R, (256, 176, 128, 64, 32, 16, 8))

    def body(w_ref, g_ref, m_ref, v_ref, d_ref, nm_ref, nv_ref):
        g_ = g_ref[...]
        m_ = ADAM_B1 * m_ref[...] + (1.0 - ADAM_B1) * g_
        v_ = ADAM_B2 * v_ref[...] + (1.0 - ADAM_B2) * (g_ * g_)
        m_hat = m_ / (1.0 - ADAM_B1 ** ADAM_STEP)
        v_hat = v_ / (1.0 - ADAM_B2 ** ADAM_STEP)
        d_ref[...] = -ADAM_LR * (m_hat / (jnp.sqrt(v_hat) + ADAM_EPS) + ADAM_WD * w_ref[...])
        nm_ref[...] = m_
        nv_ref[...] = v_

    spec = pl.BlockSpec((tr, C), lambda i: (i, 0))
    return pl.pallas_call(
        body, name=name, grid=(R // tr,),
        in_specs=[spec] * 4, out_specs=[spec] * 3,
        out_shape=[jax.ShapeDtypeStruct((R, C), F32)] * 3,
        compiler_params=_params(("parallel",)),
    )(w, g, m, v)


def _sum_devices(parts, name):
    _, R, C = parts.shape
    tr = _pick(R, (512, 256, 176, 128, 64, 32, 16, 8))

    def body(p_ref, o_ref):
        acc = p_ref[0].astype(F32)
        for d in range(1, N_DEV):
            acc = acc + p_ref[d].astype(F32)
        o_ref[...] = acc

    return pl.pallas_call(
        body, name=name, grid=(R // tr,),
        in_specs=[pl.BlockSpec((N_DEV, tr, C), lambda i: (0, i, 0))],
        out_specs=pl.BlockSpec((tr, C), lambda i: (i, 0)),
        out_shape=jax.ShapeDtypeStruct((R, C), F32),
        compiler_params=_params(("parallel",)),
    )(parts)


def _my_place():
    return lax.axis_index("x"), lax.axis_index("y"), lax.axis_index("c")


def _all_gather(blocks, name):
    n = len(blocks)

    def body(*refs):
        x_refs, out_refs = refs[:n], refs[n:2 * n]
        send_sems, recv_sems, local_sems = refs[2 * n:]
        x, y, c = _my_place()
        me, sibling = (x, y, c), (x, y, 1 - c)
        chips = [(1 - x, y), (x, 1 - y), (1 - x, 1 - y)]

        def copy(a, k, blk, to, own=False):
            slot = out_refs[a].at[4 * blk[0] + 2 * blk[1] + blk[2]]
            return pltpu.make_async_remote_copy(
                src_ref=x_refs[a] if own else slot, dst_ref=slot,
                send_sem=send_sems.at[7 * a + k], recv_sem=recv_sems.at[7 * a + k], device_id=to, device_id_type=MESH)

        mine = [pltpu.make_async_copy(x_refs[a], out_refs[a].at[4 * x + 2 * y + c], local_sems.at[a]) for a in range(n)]
        for cp in mine:
            cp.start()
        first = []
        for j, chip in enumerate(chips):
            first += [copy(a, 1 + j, me, (*chip, c), own=True) for a in range(n)]
        first += [copy(a, 0, me, sibling, own=True) for a in range(n)]
        for cp in first:
            cp.start()
        passed = []
        for j, chip in enumerate(chips):
            for a in range(n):
                copy(a, 1 + j, (*chip, c), me).wait_recv()
                passed.append(copy(a, 4 + j, (*chip, c), sibling))
                passed[-1].start()
        for a in range(n):
            copy(a, 0, sibling, me).wait_recv()
        for j, chip in enumerate(chips):
            for a in range(n):
                copy(a, 4 + j, (*chip, 1 - c), me).wait_recv()
        for cp in first + passed:
            cp.wait_send()
        for cp in mine:
            cp.wait()

    return pl.pallas_call(
        body, name=name,
        out_shape=[jax.ShapeDtypeStruct((N_DEV,) + b.shape, b.dtype) for b in blocks],
        in_specs=[pl.BlockSpec(memory_space=pl.ANY)] * n,
        out_specs=[pl.BlockSpec(memory_space=pl.ANY)] * n,
        scratch_shapes=[pltpu.SemaphoreType.DMA((7 * n,)), pltpu.SemaphoreType.DMA((7 * n,)), pltpu.SemaphoreType.DMA((n,))],
    )(*blocks)


def _all_to_all(pieces, name):
    n = len(pieces)

    def body(*refs):
        x_refs, out_refs = refs[:n], refs[n:2 * n]
        send_sems, recv_sems, local_sems = refs[2 * n:]
        x, y, c = _my_place()
        me = 4 * x + 2 * y + c
        mine = [pltpu.make_async_copy(x_refs[a].at[me], out_refs[a].at[me], local_sems.at[a]) for a in range(n)]
        for cp in mine:
            cp.start()
        copies = []
        for k in (2, 4, 6, 3, 5, 7, 1):
            px = 1 - x if k & 4 else x
            py = 1 - y if k & 2 else y
            pc = 1 - c if k & 1 else c
            peer = 4 * px + 2 * py + pc
            for a in range(n):
                copies.append(pltpu.make_async_remote_copy(
                    src_ref=x_refs[a].at[peer], dst_ref=out_refs[a].at[me],
                    send_sem=send_sems.at[7 * a + k - 1], recv_sem=recv_sems.at[7 * a + k - 1],
                    device_id=(px, py, pc), device_id_type=MESH))
        for cp in copies:
            cp.start()
        for cp in copies:
            cp.wait_recv()
        for cp in copies:
            cp.wait_send()
        for cp in mine:
            cp.wait()

    return pl.pallas_call(
        body, name=name,
        out_shape=[jax.ShapeDtypeStruct(p.shape, p.dtype) for p in pieces],
        in_specs=[pl.BlockSpec(memory_space=pl.ANY)] * n,
        out_specs=[pl.BlockSpec(memory_space=pl.ANY)] * n,
        scratch_shapes=[pltpu.SemaphoreType.DMA((7 * n,)), pltpu.SemaphoreType.DMA((7 * n,)), pltpu.SemaphoreType.DMA((n,))],
    )(*pieces)


def _pad_lanes(v, at=0, width=128):
    return jnp.pad(v, ((0, 0), (at, width - at - v.shape[1])))


def _pack_weights(P):
    W = {}
    w = P["w_in"]
    W["wp"] = jnp.concatenate([w[:, :2048], w[:, 2440:2696], w[:, 2056:2440], w[:, 2696:2760], w[:, 2048:2056],
                               jnp.zeros((D_MODEL, N_IN_PACKED - N_IN), w.dtype)], axis=1).astype(BF16)
    W["conv_w"] = P["gdn_conv_w"].astype(F32)
    W["alog_p"] = _pad_lanes(P["gdn_a_log"], 64)
    W["dt_p"] = _pad_lanes(P["gdn_dt_bias"], 64)
    W["gnw"] = P["gdn_norm_w"]
    W["qnw"] = P["mla_q_norm_w"]
    W["kvnw"] = P["mla_kv_norm_w"]
    uq = P["mla_w_uq"].reshape(Q_LORA, HEADS, HEAD_DIM + ROPE)
    W["wuq"] = jnp.pad(uq, ((0, 0), (0, 0), (0, 256 - HEAD_DIM - ROPE))).reshape(Q_LORA, HEADS * 256).astype(BF16)
    ukv = P["mla_w_ukv"].reshape(KV_LORA, HEADS, 2, HEAD_DIM)
    W["wukv"] = ukv.transpose(0, 2, 1, 3).reshape(KV_LORA, 2 * HEADS * HEAD_DIM).astype(BF16)
    W["qn_w"] = P["qkn_q_nope"]
    W["qr_w"] = _pad_lanes(P["qkn_q_rope"])
    W["kn_w"] = P["qkn_k_nope"]
    W["kr_w"] = _pad_lanes(P["qkn_k_rope"])
    W["onw"] = P["mla_out_norm_w"]
    W["wout"] = P["w_out"].astype(BF16)
    return W


def _unpack_grads(G):
    g = G["wp"]
    uq = G["wuq"].reshape(Q_LORA, HEADS, 256)[:, :, :HEAD_DIM + ROPE].reshape(Q_LORA, HEADS * (HEAD_DIM + ROPE))
    ukv = G["wukv"].reshape(KV_LORA, 2, HEADS, HEAD_DIM).transpose(0, 2, 1, 3).reshape(KV_LORA, 2 * HEADS * HEAD_DIM)
    return {
        "w_in": jnp.concatenate([g[:, :2048], g[:, 2752:2760], g[:, 2304:2688], g[:, 2048:2304], g[:, 2688:2752]], axis=1),
        "gdn_conv_w": G["conv_w"], "gdn_a_log": G["alog_p"][:, 64:68], "gdn_dt_bias": G["dt_p"][:, 64:68],
        "gdn_norm_w": G["gnw"], "mla_q_norm_w": G["qnw"], "mla_w_uq": uq, "mla_kv_norm_w": G["kvnw"], "mla_w_ukv": ukv,
        "qkn_q_nope": G["qn_w"], "qkn_q_rope": G["qr_w"][:, :ROPE], "qkn_k_nope": G["kn_w"], "qkn_k_rope": G["kr_w"][:, :ROPE],
        "mla_out_norm_w": G["onw"], "w_out": G["wout"],
    }


def _rope_tables(positions):
    half = ROPE // 2
    inv_freq = ROPE_BASE ** (-jnp.arange(half, dtype=F32) / half)
    ang = positions.astype(F32)[:, None] * inv_freq
    cos, sin = jnp.cos(ang), jnp.sin(ang)
    zeros = jnp.zeros((positions.shape[0], 128 - ROPE), F32)
    return jnp.concatenate([cos, cos, zeros], axis=1), jnp.concatenate([-sin, sin, zeros], axis=1)


def _mod_fn(x, scale, shift):
    return (_modulate(x, scale, shift),)


def _ffn_fwd(x, scale, shift, gate_w, w8, wo4, tag):
    S = x.shape[0]
    tm = _pick(S, (512, 256, 128))
    (h,) = _rowwise(_mod_fn, [(x, tm, D_MODEL, 0)], [scale, shift], [(tm, D_MODEL, BF16)], S // tm, tag + "_mod")
    gate, up, hid = _ffn_up(h, w8, tag + "_up")
    mn = pl.BlockSpec((tm, D_MODEL), lambda i, j, k: (i, j))
    f, x_out = _mmg(hid, wo4, "nn", name=tag + "_down", grid=(S // tm, 1, HID_PIECES),
                    a_spec=pl.BlockSpec((None, tm, FFN_PIECE), lambda i, j, k: (k, i, 0)),
                    b_spec=pl.BlockSpec((None, FFN_PIECE, D_MODEL), lambda i, j, k: (k, 0, j)),
                    out_spec=mn, out_shapes=[jax.ShapeDtypeStruct((S, D_MODEL), F32)] * 2, acc_shape=(tm, D_MODEL),
                    extras=[x, gate_w], extra_specs=[mn, pl.BlockSpec((1, D_MODEL), lambda i, j, k: (0, j))],
                    epi=lambda acc, x_, g_: (acc, x_ + 0.5 * g_ * acc))
    return x_out, (h, gate, up, hid, f)


def _ffn_bwd(d_out, x, scale, shift, gate_w, w8, wo4, saved, tag):
    h, gate, up, hid, f = saved
    S = x.shape[0]
    tm = _pick(S, (512, 256, 128))
    tk = _pick(S, (512, 256, 128))
    n = S // tm
    (df,), (d_gate_w,) = _rowwise_bwd(lambda f_, g_: (0.5 * g_ * f_,), [(f, tm, D_MODEL, 0)], [], [gate_w],
                                      [(d_out, tm, D_MODEL, 0)], n, tag + "_dres", row_dtypes=(BF16,))
    piece = pl.BlockSpec((None, tm, FFN_PIECE), lambda i, j, k: (j, i, 0))
    d_gate, d_up = _mmg(df, wo4, "nt", name=tag + "_ddown", grid=(n, HID_PIECES, 1),
                        a_spec=pl.BlockSpec((tm, D_MODEL), lambda i, j, k: (i, 0)),
                        b_spec=pl.BlockSpec((None, FFN_PIECE, D_MODEL), lambda i, j, k: (j, 0, 0)),
                        out_spec=piece, out_shapes=[jax.ShapeDtypeStruct((HID_PIECES, S, FFN_PIECE), BF16)] * 2,
                        acc_shape=(tm, FFN_PIECE), extras=[gate, up], extra_specs=[piece, piece], epi=_swiglu_bwd)
    g_wo4 = _mmg(hid, df, "tn", name=tag + "_gwo", grid=(HID_PIECES, 1, S // tk),
                 a_spec=pl.BlockSpec((None, tk, FFN_PIECE), lambda i, j, k: (i, k, 0)),
                 b_spec=pl.BlockSpec((tk, D_MODEL), lambda i, j, k: (k, j)),
                 out_spec=pl.BlockSpec((None, FFN_PIECE, D_MODEL), lambda i, j, k: (i, 0, j)),
                 out_shapes=[jax.ShapeDtypeStruct((HID_PIECES, FFN_PIECE, D_MODEL), BF16)], acc_shape=(FFN_PIECE, D_MODEL))
    g_halves = []
    for d_pre, sub in ((d_gate, "_gwg"), (d_up, "_gwu")):
        g_halves.append(_mmg(h, d_pre, "tn", name=tag + sub, grid=(D_MODEL // 512, HID_PIECES, S // tk),
                             a_spec=pl.BlockSpec((tk, 512), lambda i, j, k: (k, i)),
                             b_spec=pl.BlockSpec((None, tk, FFN_PIECE), lambda i, j, k: (j, k, 0)),
                             out_spec=pl.BlockSpec((None, 512, FFN_PIECE), lambda i, j, k: (j, i, 0)),
                             out_shapes=[jax.ShapeDtypeStruct((HID_PIECES, D_MODEL, FFN_PIECE), BF16)],
                             acc_shape=(512, FFN_PIECE)))
    g_w8 = jnp.concatenate(g_halves, axis=0)
    dh = _ffn_dh(d_gate, d_up, w8, tag + "_dh")
    (dx,), (d_scale, d_shift) = _rowwise_bwd(_mod_fn, [(x, tm, D_MODEL, 0)], [], [scale, shift], [(dh, tm, D_MODEL, 0)],
                                             n, tag + "_dmod", adds=[(0, d_out)])
    return dx, d_scale, d_shift, d_gate_w, g_w8, g_wo4


def _mixer_fwd(x1, scale, shift, gate_w, cos_p, sin_p, W):
    S = x1.shape[0]
    tm = _pick(S, (512, 256, 128))
    tv = _pick(S, (256, 128))
    ta = _pick(S, (512, 256, 128))
    nc = S // CHUNK
    (h2,) = _rowwise(_mod_fn, [(x1, tm, D_MODEL, 0)], [scale, shift], [(tm, D_MODEL, BF16)], S // tm, "mix_mod")
    proj = _mm(h2, W["wp"], "nn", name="mix_proj")
    qkvc = _conv_fwd(proj, W["conv_w"], tv, "gdn_conv")
    kab = (proj, tv, 128, 21)
    q_a, k_a, v_a, gb = _rowwise(_gdn_pre_fn, [(qkvc, tv, 1536, 0), kab], [W["alog_p"], W["dt_p"]],
                                 [(tv, 512, F32)] * 3 + [(tv, 128, F32)], S // tv, "gdn_pre")
    ti = _pick(S, INTRA_ROWS)
    intra = _rowwise(_gdn_intra_fn, [(q_a, ti, 512, 0), (k_a, ti, 512, 0), (v_a, ti, 512, 0), (gb, ti, 128, 0)],
                     [], [(ti, 512, F32)] * 4 + [(ti, CHUNK, F32)] * 4 + [(ti // 8, 512, F32)], S // ti, "gdn_intra")
    u, wk, qd, kd, qks, gl = intra[0], intra[1], intra[2], intra[3], tuple(intra[4:8]), intra[8]
    o_a, s_prev = _gdn_scan_fwd(u, wk, qd, kd, qks, gl, "gdn_scan")
    mla_params = [W["qnw"], W["kvnw"], W["wuq"], W["wukv"], W["qn_w"], W["qr_w"], W["kn_w"], W["kr_w"]]
    q_b, k_b, v_b = _rowwise(_mla_pre_fn,
                             [(proj, tv, 256, 8), (proj, tv, 384, 6), kab, (cos_p, tv, 128, 0), (sin_p, tv, 128, 0)],
                             mla_params, [(tv, 1024, BF16), (tv, 1024, BF16), (tv, 512, BF16)], S // tv, "mla_pre")
    o_b, lse = _attn_fwd(q_b, k_b, v_b, ta, "mla_attn")
    (mixed,) = _rowwise(_mix_post_fn, [(o_a, tv, 512, 0), (proj, tv, 512, 3), (o_b, tv, 512, 0)], [W["gnw"], W["onw"]],
                        [(tv, D_MODEL, BF16)], S // tv, "mix_post")
    y, x2 = _mm(mixed, W["wout"], "nn", name="mix_out", out_dtypes=(F32, F32), extras=[x1], extra_params=[gate_w],
                epi=lambda acc, x_, g_: (acc, x_ + g_ * acc))
    saved = (h2, proj, qkvc, q_a, k_a, v_a, gb, u, wk, qd, kd, qks, gl, s_prev, o_a, q_b, k_b, v_b, o_b, lse, mixed, y)
    return x2, saved


def _mixer_bwd(d_out, x1, scale, shift, gate_w, cos_p, sin_p, W, saved):
    (h2, proj, qkvc, q_a, k_a, v_a, gb, u, wk, qd, kd, qks, gl, s_prev, o_a, q_b, k_b, v_b, o_b, lse, mixed, y) = saved
    S = x1.shape[0]
    tm = _pick(S, (512, 256, 128))
    tv = _pick(S, (256, 128))
    ta = _pick(S, (512, 256, 128))
    nc = S // CHUNK
    G = {}
    (dy,), (G["g2"],) = _rowwise_bwd(lambda y_, g_: (g_ * y_,), [(y, tm, D_MODEL, 0)], [], [gate_w],
                                     [(d_out, tm, D_MODEL, 0)], S // tm, "mix_dres", row_dtypes=(BF16,))
    d_mixed = _mm(dy, W["wout"], "nt", name="mix_dout")
    G["wout"] = _mm(mixed, dy, "tn", name="mix_gwout")
    (do_a, dz, do_b), (G["gnw"], G["onw"]) = _rowwise_bwd(
        _mix_post_fn, [(o_a, tv, 512, 0), (proj, tv, 512, 3), (o_b, tv, 512, 0)], [], [W["gnw"], W["onw"]],
        [(d_mixed, tv, D_MODEL, 0)], S // tv, "mix_dpost")
    dq_b, dk_b, dv_b = _attn_bwd(q_b, k_b, v_b, o_b, lse, do_b, ta, "mla_dattn")
    kab = (proj, tv, 128, 21)
    mla_params = [W["qnw"], W["kvnw"], W["wuq"], W["wukv"], W["qn_w"], W["qr_w"], W["kn_w"], W["kr_w"]]
    (d_ckv, d_cq, d_kab), mla_grads = _rowwise_bwd(
        _mla_pre_fn, [(proj, tv, 256, 8), (proj, tv, 384, 6), kab], [(cos_p, tv, 128, 0), (sin_p, tv, 128, 0)], mla_params,
        [(dq_b, tv, 1024, 0), (dk_b, tv, 1024, 0), (dv_b, tv, 512, 0)], S // tv, "mla_dpre")
    for key, g in zip(("qnw", "kvnw", "wuq", "wukv", "qn_w", "qr_w", "kn_w", "kr_w"), mla_grads):
        G[key] = g
    scan_grads = _gdn_scan_bwd(u, wk, qd, kd, qks, gl, s_prev, do_a, "gdn_dscan")
    ti = _pick(S, INTRA_ROWS)
    intra_douts = [(scan_grads[i], ti, 512, 0) for i in range(4)] + [(scan_grads[4 + i], ti, CHUNK, 0) for i in range(4)]
    intra_douts.append((scan_grads[8], ti // 8, 512, 0))
    (dq_a, dk_a, dv_a, d_gb), _ = _rowwise_bwd(
        _gdn_intra_fn, [(q_a, ti, 512, 0), (k_a, ti, 512, 0), (v_a, ti, 512, 0), (gb, ti, 128, 0)], [], [],
        intra_douts, S // ti, "gdn_dintra")
    (d_qkvc, d_kab), (G["alog_p"], G["dt_p"]) = _rowwise_bwd(
        _gdn_pre_fn, [(qkvc, tv, 1536, 0), kab], [], [W["alog_p"], W["dt_p"]],
        [(dq_a, tv, 512, 0), (dk_a, tv, 512, 0), (dv_a, tv, 512, 0), (d_gb, tv, 128, 0)], S // tv, "gdn_dpre",
        adds=[(1, d_kab)])
    d_qkv, g_conv = _conv_bwd(proj, d_qkvc, W["conv_w"], tv, "gdn_dconv")
    G["conv_w"] = g_conv[:4]
    d_proj = jnp.concatenate([d_qkv, dz, d_ckv, d_cq, d_kab], axis=1).astype(BF16)
    G["wp"] = _mm(h2, d_proj, "tn", name="mix_gwp")
    dh2 = _mm(d_proj, W["wp"], "nt", name="mix_dproj")
    (dx1,), (G["s2"], G["sh2"]) = _rowwise_bwd(_mod_fn, [(x1, tm, D_MODEL, 0)], [], [scale, shift],
                                               [(dh2, tm, D_MODEL, 0)], S // tm, "mix_dmod", adds=[(0, d_out)])
    return dx1, G


def _local_step(x, target, mod, cos_p, sin_p, W):
    sh1, s1, g1, sh2, s2, g2, sh3, s3, g3 = [mod[:, D_MODEL * i:D_MODEL * (i + 1)] for i in range(N_MOD)]
    x1, saved1 = _ffn_fwd(x, s1, sh1, g1, W["f1_w8"], W["f1_wo4"], "ffn1")
    x2, saved2 = _mixer_fwd(x1, s2, sh2, g2, cos_p, sin_p, W)
    x3, saved3 = _ffn_fwd(x2, s3, sh3, g3, W["f2_w8"], W["f2_wo4"], "ffn2")
    dx3, loss_row = _loss_and_grad(x3, target, "loss")
    dx2, d_s3, d_sh3, d_g3, g_f2_w8, g_f2_wo4 = _ffn_bwd(dx3, x2, s3, sh3, g3, W["f2_w8"], W["f2_wo4"], saved3, "ffn2")
    dx1, G = _mixer_bwd(dx2, x1, s2, sh2, g2, cos_p, sin_p, W, saved2)
    dx, d_s1, d_sh1, d_g1, g_f1_w8, g_f1_wo4 = _ffn_bwd(dx1, x, s1, sh1, g1, W["f1_w8"], W["f1_wo4"], saved1, "ffn1")
    d_mod = jnp.concatenate([d_sh1, d_s1, d_g1, G.pop("sh2"), G.pop("s2"), G.pop("g2"), d_sh3, d_s3, d_g3], axis=1)
    G.update(f1_w8=g_f1_w8, f1_wo4=g_f1_wo4, f2_w8=g_f2_w8, f2_wo4=g_f2_wo4)
    return loss_row, dx, d_mod, G


WEIGHT_NAMES = ("w_ada", "b_ada", "ffn1_w_in", "ffn1_w_out", "w_in", "gdn_conv_w", "gdn_a_log", "gdn_dt_bias", "gdn_norm_w",
                "mla_q_norm_w", "mla_w_uq", "mla_kv_norm_w", "mla_w_ukv", "qkn_q_nope", "qkn_q_rope", "qkn_k_nope",
                "qkn_k_rope", "mla_out_norm_w", "w_out", "ffn2_w_in", "ffn2_w_out")
FFN_SHARDED = ("ffn1_w_in", "ffn1_w_out", "ffn2_w_in", "ffn2_w_out")
SHEETED = (("w_in", "col"), ("gdn_conv_w", "col"), ("mla_w_uq", "col"), ("mla_w_ukv", "col"), ("w_out", "row"))
MOD_ROWS = N_MOD * D_MODEL // 128
SMALL = {"gdn_a_log": (MOD_ROWS, 1, 64, 4), "gdn_dt_bias": (MOD_ROWS + 1, 1, 64, 4), "gdn_norm_w": (MOD_ROWS + 2, 1, 0, 128),
         "mla_q_norm_w": (MOD_ROWS + 3, 3, 0, 384), "mla_kv_norm_w": (MOD_ROWS + 6, 2, 0, 256),
         "qkn_q_nope": (MOD_ROWS + 8, 1, 0, 128), "qkn_q_rope": (MOD_ROWS + 9, 1, 0, 64), "qkn_k_nope": (MOD_ROWS + 10, 1, 0, 128),
         "qkn_k_rope": (MOD_ROWS + 11, 1, 0, 64), "mla_out_norm_w": (MOD_ROWS + 12, 1, 0, 128)}
LOSS_ROW = MOD_ROWS + 13
SHEET_ROWS = 88


def _to_sheet(flat, dtype, sublanes):
    n = flat.shape[-1]
    unit = sublanes * 128
    pad = (-n) % unit
    flat = jnp.pad(flat.astype(dtype), [(0, 0)] * (flat.ndim - 1) + [(0, pad)])
    return flat.reshape(flat.shape[:-1] + ((n + pad) // 128, 128))


def _small_sheet(b_like, small):
    sheet = jnp.zeros((SHEET_ROWS, 128), F32).at[:MOD_ROWS].set(b_like.reshape(MOD_ROWS, 128))
    for name, (row, rows, lane, n) in SMALL.items():
        v = small[name].reshape(1, n)
        if rows == 1:
            sheet = sheet.at[row, lane:lane + n].set(v[0])
        else:
            sheet = sheet.at[row:row + rows].set(v.reshape(rows, 128))
    return sheet


def _from_small_sheet(sheet):
    out = {"b_ada": sheet[:MOD_ROWS].reshape(1, N_MOD * D_MODEL)}
    for name, (row, rows, lane, n) in SMALL.items():
        out[name] = sheet[row, lane:lane + n].reshape(1, n) if rows == 1 else sheet[row:row + rows].reshape(1, n)
    return out


def kernel(x, c, positions, w_ada, b_ada, ffn1_w_in, ffn1_w_out, w_in, gdn_conv_w, gdn_a_log, gdn_dt_bias, gdn_norm_w, mla_q_norm_w, mla_w_uq, mla_kv_norm_w, mla_w_ukv, qkn_q_nope, qkn_q_rope, qkn_k_nope, qkn_k_rope, mla_out_norm_w, w_out, ffn2_w_in, ffn2_w_out, loss_target, m_w_ada, m_b_ada, m_ffn1_w_in, m_ffn1_w_out, m_w_in, m_gdn_conv_w, m_gdn_a_log, m_gdn_dt_bias, m_gdn_norm_w, m_mla_q_norm_w, m_mla_w_uq, m_mla_kv_norm_w, m_mla_w_ukv, m_qkn_q_nope, m_qkn_q_rope, m_qkn_k_nope, m_qkn_k_rope, m_mla_out_norm_w, m_w_out, m_ffn2_w_in, m_ffn2_w_out, v_w_ada, v_b_ada, v_ffn1_w_in, v_ffn1_w_out, v_w_in, v_gdn_conv_w, v_gdn_a_log, v_gdn_dt_bias, v_gdn_norm_w, v_mla_q_norm_w, v_mla_w_uq, v_mla_kv_norm_w, v_mla_w_ukv, v_qkn_q_nope, v_qkn_q_rope, v_qkn_k_nope, v_qkn_k_rope, v_mla_out_norm_w, v_w_out, v_ffn2_w_in, v_ffn2_w_out):
    args = locals()
    w = {n: args[n] for n in WEIGHT_NAMES}
    m = {n: args["m_" + n] for n in WEIGHT_NAMES}
    v = {n: args["v_" + n] for n in WEIGHT_NAMES}
    me = 4 * lax.axis_index("x") + 2 * lax.axis_index("y") + lax.axis_index("c")
    cols = N_MOD * D_MODEL // N_DEV
    shard = {n: w[n][0] for n in FFN_SHARDED + tuple(s[0] for s in SHEETED)}

    sc = c * _sigmoid(c)
    first = _to_sheet(jnp.concatenate([sc.reshape(-1), shard["gdn_conv_w"].reshape(-1)]), F32, 8)
    (first_all,) = _all_gather([first], "gather_c")
    sc_all = first_all[:, :D_MODEL // 128].reshape(N_DEV, D_MODEL)
    n_taps = shard["gdn_conv_w"].size
    conv_all = first_all.reshape(N_DEV, -1)[:, D_MODEL:D_MODEL + n_taps].reshape(N_DEV, 4, -1)
    b_mine = lax.dynamic_slice(b_ada, (0, me * cols), (1, cols))
    mod_cols = _mm(sc_all, w_ada[0], "nn", name="ada_mod", extra_params=[b_mine], epi=lambda acc, b_: (acc + b_,))
    (mod_all,) = _all_to_all([_to_sheet(mod_cols, F32, 8)], "scatter_mod")
    mod = mod_all.reshape(N_DEV, -1)[:, :cols].reshape(1, N_MOD * D_MODEL)

    travel = [s for s in SHEETED if s[0] != "gdn_conv_w"]
    flat_w = jnp.concatenate([shard[n].reshape(-1).astype(BF16) for n, _ in travel])
    gathered = _all_gather([shard[n].astype(BF16) for n in FFN_SHARDED] + [_to_sheet(flat_w, BF16, 16)], "gather_w")
    w_all = gathered[-1].reshape(N_DEV, -1)
    P, off = {}, 0
    for n, kind in travel:
        r, cc = shard[n].shape
        piece = w_all[:, off:off + r * cc].reshape(N_DEV, r, cc)
        P[n] = jnp.concatenate(list(piece), axis=1) if kind == "col" else piece.reshape(N_DEV * r, cc)
        off += r * cc
    P["gdn_conv_w"] = jnp.concatenate(list(conv_all), axis=1)
    for n in SMALL:
        P[n] = w[n]
    W = _pack_weights(P)
    W.update(f1_w8=gathered[0], f1_wo4=gathered[1].reshape(HID_PIECES, FFN_PIECE, D_MODEL),
             f2_w8=gathered[2], f2_wo4=gathered[3].reshape(HID_PIECES, FFN_PIECE, D_MODEL))
    cos_p, sin_p = _rope_tables(positions[0])

    loss_row, dx, d_mod, G = _local_step(x[0], loss_target[0], mod, cos_p, sin_p, W)
    g_full = _unpack_grads(G)

    sheet = _small_sheet(d_mod, g_full).at[LOSS_ROW].set(loss_row[0, :128])
    (sheets,) = _all_gather([sheet], "gather_small")
    summed = _sum_devices(sheets, "sum_small")
    d_mod_all = sheets[:, :MOD_ROWS].reshape(N_DEV, N_MOD * D_MODEL)
    d_mod_mine = lax.dynamic_slice(d_mod_all, (0, me * cols), (N_DEV, cols))
    grads = _from_small_sheet(summed)
    grads["w_ada"] = _mm(sc_all, d_mod_mine, "tn", name="ada_gw", hi=True)
    loss = summed[LOSS_ROW, 0]

    pieces = []
    for n, kind in SHEETED:
        r, cc = shard[n].shape
        g = g_full[n]
        pieces.append(jnp.stack([g[:, cc * p:cc * (p + 1)].reshape(-1) for p in range(N_DEV)]) if kind == "col"
                      else g.reshape(N_DEV, r * cc))
    g_sheet = _to_sheet(jnp.concatenate(pieces, axis=1), BF16, 512)
    ffn_pieces = [G["f1_w8"], G["f1_wo4"].reshape((N_DEV,) + shard["ffn1_w_out"].shape),
                  G["f2_w8"], G["f2_wo4"].reshape((N_DEV,) + shard["ffn2_w_out"].shape)]
    arrived = _all_to_all(ffn_pieces + [g_sheet], "scatter_grads")
    for n, parts in zip(FFN_SHARDED, arrived):
        grads[n] = _sum_devices(parts, "sum_" + n)
    g_mine = _sum_devices(arrived[-1], "sum_grads").reshape(-1)
    off = 0
    for n, _ in SHEETED:
        grads[n] = g_mine[off:off + shard[n].size].reshape(shard[n].shape)
        off += shard[n].size

    delta, new_m, new_v = {}, {}, {}
    for n in ("w_ada",) + FFN_SHARDED + tuple(s[0] for s in SHEETED):
        delta[n], new_m[n], new_v[n] = _adamw(w[n][0], grads[n], m[n][0], v[n][0], "adamw_" + n)
    small_in = [_small_sheet(t["b_ada"], t) for t in (w, grads, m, v)]
    for res, out in zip(_adamw(*small_in, "adamw_small"), (delta, new_m, new_v)):
        out.update(_from_small_sheet(res))

    def shaped(d):
        return [d[n].reshape(w[n].shape) for n in WEIGHT_NAMES]

    return (loss, dx[None], *shaped(grads), *shaped(delta), *shaped(new_m), *shaped(new_v))
```

```python
import functools

import jax
import jax.numpy as jnp
from jax import lax
from jax.experimental import pallas as pl
from jax.experimental.pallas import tpu as pltpu

F32 = jnp.float32
BF16 = jnp.bfloat16

D_MODEL = 1024
D_FF = 2816
N_MOD = 9
HEADS = 4
HEAD_DIM = 128
CHUNK = 64
EPS = 1e-6
ROPE = 64
Q_LORA = 384
KV_LORA = 256
N_IN = 2760
N_IN_PACKED = 2816
ROPE_BASE = 10000.0
N_DEV = 8

ADAM_LR = 0.001
ADAM_B1 = 0.9
ADAM_B2 = 0.999
ADAM_EPS = 1e-08
ADAM_WD = 0.01
ADAM_STEP = 10

VMEM_LIMIT_BYTES = 56 * 1024 * 1024
MESH = pl.DeviceIdType.MESH


def _params(sem=None):
    return pltpu.CompilerParams(dimension_semantics=sem, vmem_limit_bytes=VMEM_LIMIT_BYTES)


def _pick(dim, prefs):
    for p in prefs:
        if dim % p == 0:
            return p
    return dim


_DIMS = {"nn": (((1,), (0,)), ((), ())), "nt": (((1,), (1,)), ((), ())), "tn": (((0,), (0,)), ((), ()))}


def _dot_raw(a, b, mode):
    return lax.dot_general(a.astype(BF16), b.astype(BF16), _DIMS[mode], preferred_element_type=F32)


def _dot_hi(a, b, mode="nn"):
    return lax.dot_general(a, b, _DIMS[mode], precision=lax.Precision.HIGHEST, preferred_element_type=F32)


@functools.partial(jax.custom_vjp, nondiff_argnums=(2,))
def _bdot(a, b, mode):
    return _dot_raw(a, b, mode)


def _bdot_fwd(a, b, mode):
    return _dot_raw(a, b, mode), (a, b)


def _bdot_bwd(mode, res, g):
    a, b = res
    if mode == "nn":
        return _dot_raw(g, b, "nt"), _dot_raw(a, g, "tn")
    if mode == "nt":
        return _dot_raw(g, b, "nn"), _dot_raw(g, a, "tn")
    return _dot_raw(b, g, "nt"), _dot_raw(a, g, "nn")


_bdot.defvjp(_bdot_fwd, _bdot_bwd)


def _mm(a, b, mode, *, name, out_dtypes=(F32,), epi=None, extras=(), extra_params=(), hi=False,
        tm=None, tn=None, tk=None):
    if mode == "nn":
        (M, K), (_, N) = a.shape, b.shape
    elif mode == "nt":
        (M, K), (N, _) = a.shape, b.shape
    else:
        (K, M), (_, N) = a.shape, b.shape
    tm = tm or _pick(M, (512, 1408, 256, 128) if mode == "tn" else (512, 384, 352, 256, 128))
    tn = tn or _pick(N, (1024, 1408, 768, 512, 384, 256, 128))
    tk = tk or _pick(K, (1024, 1408, 512, 384, 256, 128))
    a_spec = {"nn": pl.BlockSpec((tm, tk), lambda i, j, k: (i, k)), "nt": pl.BlockSpec((tm, tk), lambda i, j, k: (i, k)),
              "tn": pl.BlockSpec((tk, tm), lambda i, j, k: (k, i))}[mode]
    b_spec = {"nn": pl.BlockSpec((tk, tn), lambda i, j, k: (k, j)), "nt": pl.BlockSpec((tn, tk), lambda i, j, k: (j, k)),
              "tn": pl.BlockSpec((tk, tn), lambda i, j, k: (k, j))}[mode]
    mn_spec = pl.BlockSpec((tm, tn), lambda i, j, k: (i, j))
    return _mmg(a, b, mode, name=name, grid=(M // tm, N // tn, K // tk), a_spec=a_spec, b_spec=b_spec, out_spec=mn_spec,
                out_shapes=[jax.ShapeDtypeStruct((M, N), dt) for dt in out_dtypes], acc_shape=(tm, tn), epi=epi,
                extras=list(extras) + list(extra_params),
                extra_specs=[mn_spec] * len(extras) + [pl.BlockSpec((1, tn), lambda i, j, k: (0, j))] * len(extra_params),
                hi=hi)


def _mmg(a, b, mode, *, name, grid, a_spec, b_spec, out_spec, out_shapes, acc_shape, epi=None, extras=(),
         extra_specs=(), hi=False):
    nk = grid[2]
    n_e, n_o = len(extras), len(out_shapes)

    def body(*refs):
        a_ref, b_ref = refs[:2]
        e_refs = refs[2:2 + n_e]
        o_refs = refs[2 + n_e:2 + n_e + n_o]
        acc_ref = refs[-1]
        k = pl.program_id(2)

        @pl.when(k == 0)
        def _():
            acc_ref[...] = jnp.zeros_like(acc_ref)

        if hi:
            acc_ref[...] += _dot_hi(a_ref[...].astype(F32), b_ref[...].astype(F32), mode)
        else:
            acc_ref[...] += _dot_raw(a_ref[...], b_ref[...], mode)

        @pl.when(k == nk - 1)
        def _():
            acc = acc_ref[...]
            outs = (acc,) if epi is None else epi(acc, *[e[...].astype(F32) for e in e_refs])
            for o_ref, o in zip(o_refs, outs):
                o_ref[...] = o.astype(o_ref.dtype)

    outs = pl.pallas_call(
        body, name=name, grid=grid,
        in_specs=[a_spec, b_spec] + list(extra_specs),
        out_specs=[out_spec] * n_o,
        out_shape=list(out_shapes),
        scratch_shapes=[pltpu.VMEM(acc_shape, F32)],
        compiler_params=_params(("parallel", "parallel", "arbitrary")),
    )(a, b, *extras)
    return outs if n_o > 1 else outs[0]


def _row_spec(th, cw, ci):
    return pl.BlockSpec((th, cw), lambda i: (i, ci))


def _full_spec(shape):
    return pl.BlockSpec(shape, lambda i: (0,) * len(shape))


def _rowwise(fn, rows, params, outs, n_steps, name):
    n_r, n_p, n_o = len(rows), len(params), len(outs)

    def body(*refs):
        vals = [r[...].astype(F32) for r in refs[:n_r + n_p]]
        res = fn(*vals)
        for o_ref, o in zip(refs[n_r + n_p:], res):
            o_ref[...] = o.astype(o_ref.dtype)

    across = [len(o) == 4 for o in outs]
    res = pl.pallas_call(
        body, name=name, grid=(n_steps,),
        in_specs=[_row_spec(th, cw, ci) for (_, th, cw, ci) in rows] + [_full_spec(p.shape) for p in params],
        out_specs=[pl.BlockSpec((o[0], o[1]), lambda i: (0, i)) if ac else _row_spec(o[0], o[1], 0)
                   for o, ac in zip(outs, across)],
        out_shape=[jax.ShapeDtypeStruct((o[0], n_steps * o[1]) if ac else (n_steps * o[0], o[1]), o[2])
                   for o, ac in zip(outs, across)],
        compiler_params=_params(("parallel",)),
    )(*[r[0] for r in rows], *params)
    return res


def _rowwise_bwd(fn, rows, aux, params, douts, n_steps, name, row_dtypes=None, adds=()):
    n_r, n_a, n_p, n_d, n_add = len(rows), len(aux), len(params), len(douts), len(adds)
    row_dtypes = row_dtypes or (F32,) * n_r

    def body(*refs):
        it = iter(refs)
        r_vals = [next(it)[...].astype(F32) for _ in range(n_r)]
        a_vals = [next(it)[...].astype(F32) for _ in range(n_a)]
        p_vals = [next(it)[...].astype(F32) for _ in range(n_p)]
        d_vals = [next(it)[...].astype(F32) for _ in range(n_d)]
        add_vals = [next(it)[...].astype(F32) for _ in range(n_add)]
        dr_refs = [next(it) for _ in range(n_r)]
        dp_refs = [next(it) for _ in range(n_p)]

        def f(*rp):
            return tuple(fn(*rp[:n_r], *a_vals, *rp[n_r:]))

        _, vjp = jax.vjp(f, *r_vals, *p_vals)
        grads = list(vjp(tuple(d_vals)))
        for (ri, _), av in zip(adds, add_vals):
            grads[ri] = grads[ri] + av
        for dr_ref, g in zip(dr_refs, grads[:n_r]):
            dr_ref[...] = g.astype(dr_ref.dtype)

        @pl.when(pl.program_id(0) == 0)
        def _():
            for dp_ref in dp_refs:
                dp_ref[...] = jnp.zeros_like(dp_ref)

        for dp_ref, g in zip(dp_refs, grads[n_r:]):
            dp_ref[...] += g

    all_rows = list(rows) + list(aux) + list(douts) + [(arr,) + tuple(rows[ri][1:3]) + (0,) for ri, arr in adds]
    in_specs = ([_row_spec(th, cw, ci) for (_, th, cw, ci) in list(rows) + list(aux)]
                + [_full_spec(p.shape) for p in params]
                + [_row_spec(th, cw, ci) for (_, th, cw, ci) in all_rows[n_r + n_a:]])
    res = pl.pallas_call(
        body, name=name, grid=(n_steps,),
        in_specs=in_specs,
        out_specs=[_row_spec(th, cw, 0) for (_, th, cw, _) in rows] + [_full_spec(p.shape) for p in params],
        out_shape=[jax.ShapeDtypeStruct((n_steps * th, cw), dt) for (_, th, cw, _), dt in zip(rows, row_dtypes)]
        + [jax.ShapeDtypeStruct(p.shape, F32) for p in params],
        compiler_params=_params(("arbitrary",)),
    )(*[r[0] for r in list(rows) + list(aux)], *params, *[r[0] for r in all_rows[n_r + n_a:]])
    return res[:n_r], res[n_r:]


def _sigmoid(x):
    return lax.logistic(x)


def _silu(x):
    return x * _sigmoid(x)


def _rms(x, w=None, n=None):
    n = n or x.shape[-1]
    y = x * lax.rsqrt(jnp.sum(x * x, axis=-1, keepdims=True) * (1.0 / n) + EPS)
    return y if w is None else y * w


def _modulate(x, scale, shift):
    return _rms(x) * (1.0 + scale) + shift


def _softplus(x):
    return jnp.maximum(x, 0.0) + jnp.log1p(jnp.exp(-jnp.abs(x)))


@jax.custom_vjp
def _rot_half64(x):
    lane = lax.broadcasted_iota(jnp.int32, x.shape, 1)
    up = pltpu.roll(x, 96, 1)
    down = pltpu.roll(x, 32, 1)
    return jnp.where(lane < 32, up, jnp.where(lane < 64, down, 0.0))


_rot_half64.defvjp(lambda x: (_rot_half64(x), None), lambda _, g: (_rot_half64(g),))


def _rope128(x, cos_p, sin_p):
    return x * cos_p + _rot_half64(x) * sin_p


def _gdn_pre_fn(qkvc, kab, alog_p, dt_p):
    a = _silu(qkvc)
    qs, ks = [], []
    for h in range(HEADS):
        qh = a[:, HEAD_DIM * h:HEAD_DIM * (h + 1)]
        kh = a[:, 512 + HEAD_DIM * h:512 + HEAD_DIM * (h + 1)]
        qs.append(qh * lax.rsqrt(jnp.sum(qh * qh, axis=-1, keepdims=True) + EPS) * (HEAD_DIM ** -0.5))
        ks.append(kh * lax.rsqrt(jnp.sum(kh * kh, axis=-1, keepdims=True) + EPS))
    lane = lax.broadcasted_iota(jnp.int32, kab.shape, 1)
    g_full = -jnp.exp(alog_p) * _softplus(kab + dt_p)
    b_full = _sigmoid(kab)
    gb = jnp.where((lane >= 64) & (lane < 68), g_full, jnp.where((lane >= 68) & (lane < 72), b_full, 0.0))
    return jnp.concatenate(qs, axis=1), jnp.concatenate(ks, axis=1), a[:, 1024:1536], gb


def _intra_head(q, k, v, g_col, b_col):
    c = CHUNK
    row = lax.broadcasted_iota(jnp.int32, (c, c), 0)
    col = lax.broadcasted_iota(jnp.int32, (c, c), 1)
    incl, strict, eye = row >= col, row > col, row == col
    tri = jnp.where(incl, 1.0, 0.0).astype(F32)
    ident = jnp.where(eye, 1.0, 0.0).astype(F32)
    g_wide = _dot_hi(tri, jnp.broadcast_to(g_col, (c, HEAD_DIM)))
    g_i = g_wide[:, :c]
    g_j = jnp.sum(jnp.where(eye, g_i, 0.0), axis=0, keepdims=True)
    decay = jnp.where(incl, jnp.exp(jnp.where(incl, g_i - g_j, 0.0)), 0.0)
    kk = _bdot(k, k, "nt")
    a_mat = jnp.where(strict, b_col * kk * decay, 0.0)
    x_pow = -a_mat
    inv = ident + x_pow
    for _ in range(5):
        x_pow = _dot_hi(x_pow, x_pow)
        inv = inv + _dot_hi(inv, x_pow)
    e_wide = jnp.exp(g_wide)
    u = _dot_hi(inv, v * b_col)
    wk = _dot_hi(inv, k * b_col * e_wide)
    qk = _bdot(q, k, "nt") * decay
    last = lax.broadcasted_iota(jnp.int32, (c, HEAD_DIM), 0) == c - 1
    g_last = jnp.sum(jnp.where(last, g_wide, 0.0), axis=0, keepdims=True)
    qd = q * e_wide
    kd = k * jnp.exp(g_last - g_wide)
    gl = jnp.broadcast_to(jnp.exp(g_last), (8, HEAD_DIM))
    return u, wk, qd, kd, qk, gl


INTRA_ROWS = (256, 128, 64)

_BNN = (((2,), (1,)), ((0,), (0,)))
_BNT = (((2,), (2,)), ((0,), (0,)))


def _split_bf16(a):
    hi = a.astype(BF16)
    return hi, (a - hi.astype(F32)).astype(BF16)


def _dot3_raw(a, b, dims):
    a_hi, a_lo = _split_bf16(a)
    b_hi, b_lo = _split_bf16(b)
    dot = lambda x_, y_: lax.dot_general(x_, y_, dims, preferred_element_type=F32)
    return dot(a_hi, b_hi) + (dot(a_hi, b_lo) + dot(a_lo, b_hi))


@functools.partial(jax.custom_vjp, nondiff_argnums=(2,))
def _dot3(a, b, nt):
    return _dot3_raw(a, b, _BNT if nt else _BNN)


def _dot3_fwd(a, b, nt):
    return _dot3(a, b, nt), (a, b)


def _dot3_bwd(nt, res, g):
    a, b = res
    g_t = jnp.swapaxes(g, 1, 2)
    if nt:
        return _dot3_raw(g, b, _BNN), _dot3_raw(g_t, a, _BNN)
    return _dot3_raw(g, b, _BNT), _dot3_raw(jnp.swapaxes(a, 1, 2), g, _BNN)


_dot3.defvjp(_dot3_fwd, _dot3_bwd)


@functools.partial(jax.custom_vjp, nondiff_argnums=(2,))
def _bdot_b(a, b, nt):
    return lax.dot_general(a.astype(BF16), b.astype(BF16), _BNT if nt else _BNN, preferred_element_type=F32)


def _bdot_b_fwd(a, b, nt):
    return _bdot_b(a, b, nt), (a, b)


def _bdot_b_bwd(nt, res, g):
    a, b = res
    dot = lambda x_, y_, d_: lax.dot_general(x_.astype(BF16), y_.astype(BF16), d_, preferred_element_type=F32)
    if nt:
        return dot(g, b, _BNN), dot(jnp.swapaxes(g, 1, 2), a, _BNN)
    return dot(g, b, _BNT), dot(jnp.swapaxes(a, 1, 2), g, _BNN)


_bdot_b.defvjp(_bdot_b_fwd, _bdot_b_bwd)


def _intra_batched(q, k, v, g_col, b_col):
    c = CHUNK
    nb = q.shape[0]
    row = lax.broadcasted_iota(jnp.int32, (1, c, c), 1)
    col = lax.broadcasted_iota(jnp.int32, (1, c, c), 2)
    incl, strict, eye = row >= col, row > col, row == col
    tri = jnp.broadcast_to(jnp.where(incl, 1.0, 0.0).astype(F32), (nb, c, c))
    ident = jnp.where(eye, 1.0, 0.0).astype(F32)
    g_wide = _dot3(tri, jnp.broadcast_to(g_col, (nb, c, HEAD_DIM)), False)
    g_i = g_wide[:, :, :c]
    g_j = jnp.sum(jnp.where(eye, g_i, 0.0), axis=1, keepdims=True)
    decay = jnp.where(incl, jnp.exp(jnp.where(incl, g_i - g_j, 0.0)), 0.0)
    kk = _bdot_b(k, k, True)
    a_mat = jnp.where(strict, b_col * kk * decay, 0.0)
    x_pow = -a_mat
    inv = ident + x_pow
    for _ in range(5):
        x_pow = _dot3(x_pow, x_pow, False)
        inv = inv + _dot3(inv, x_pow, False)
    e_wide = jnp.exp(g_wide)
    u = _dot3(inv, v * b_col, False)
    wk = _dot3(inv, k * b_col * e_wide, False)
    qk = _bdot_b(q, k, True) * decay
    last = lax.broadcasted_iota(jnp.int32, (1, c, HEAD_DIM), 1) == c - 1
    g_last = jnp.sum(jnp.where(last, g_wide, 0.0), axis=1, keepdims=True)
    qd = q * e_wide
    kd = k * jnp.exp(g_last - g_wide)
    gl = jnp.broadcast_to(jnp.exp(g_last), (nb, 8, HEAD_DIM))
    return u, wk, qd, kd, qk, gl


def _gdn_intra_fn(q, k, v, gb):
    t = q.shape[0]
    nch = t // CHUNK
    lane = lax.broadcasted_iota(jnp.int32, gb.shape, 1)

    def heads_first(x_):
        return jnp.concatenate([x_[:, HEAD_DIM * h:HEAD_DIM * (h + 1)].reshape(nch, CHUNK, HEAD_DIM) for h in range(HEADS)],
                               axis=0)

    def column(first_lane):
        return jnp.concatenate([jnp.sum(jnp.where(lane == first_lane + h, gb, 0.0), axis=1, keepdims=True)
                                .reshape(nch, CHUNK, 1) for h in range(HEADS)], axis=0)

    u, wk, qd, kd, qk, gl = _intra_batched(heads_first(q), heads_first(k), heads_first(v), column(64), column(68))

    def rows_first(x_):
        r, w_ = x_.shape[1], x_.shape[2]
        return jnp.concatenate([x_[nch * h:nch * (h + 1)].reshape(nch * r, w_) for h in range(HEADS)], axis=1)

    qks = [qk[nch * h:nch * (h + 1)].reshape(t, CHUNK) for h in range(HEADS)]
    return (rows_first(u), rows_first(wk), rows_first(qd), rows_first(kd), *qks, rows_first(gl))


def _scan_step(s0, u, wk, qd, kd, qk, gl):
    v_new = u - _bdot(wk, s0, "nn")
    o = _bdot(qd, s0, "nn") + _bdot(qk, v_new, "nn")
    s1 = s0 * gl[0:1, :] + _bdot(kd, v_new, "tn")
    return o, s1


def _mix_post_fn(o_a, z, o_b, gnw, onw):
    parts = [_rms(o_a[:, HEAD_DIM * h:HEAD_DIM * (h + 1)], gnw) * _silu(z[:, HEAD_DIM * h:HEAD_DIM * (h + 1)])
             for h in range(HEADS)]
    parts += [_rms(o_b[:, HEAD_DIM * h:HEAD_DIM * (h + 1)], onw) for h in range(HEADS)]
    return (jnp.concatenate(parts, axis=1),)


def _mla_pre_fn(ckv, cq, kab, cos_p, sin_p, qnw, kvnw, wuq, wukv, qn_w, qr_w, kn_w, kr_w):
    scale = (HEAD_DIM + ROPE) ** -0.5
    qf = _bdot(_rms(cq, qnw), wuq, "nn")
    kvf = _bdot(_rms(ckv, kvnw), wukv, "nn")
    lane = lax.broadcasted_iota(jnp.int32, kab.shape, 1)
    kr = _rope128(_rms(jnp.where(lane < ROPE, kab, 0.0), kr_w, n=ROPE), cos_p, sin_p)
    qs, ks = [], []
    for h in range(HEADS):
        qn = _rms(qf[:, 256 * h:256 * h + 128], qn_w) * scale
        qr = _rope128(_rms(qf[:, 256 * h + 128:256 * h + 256], qr_w, n=ROPE), cos_p, sin_p) * scale
        qs += [qn, qr]
        ks += [_rms(kvf[:, 128 * h:128 * (h + 1)], kn_w), kr]
    return jnp.concatenate(qs, axis=1), jnp.concatenate(ks, axis=1), kvf[:, 512:]


def _conv_fwd(proj, conv_w, tm, name):
    S = proj.shape[0]
    C = 1536
    nb = tm // 8

    def body(x_ref, prev_ref, w_ref, o_ref, ext_ref):
        i = pl.program_id(0)
        ext_ref[0:8, :] = jnp.where(i > 0, prev_ref[...], 0.0)
        ext_ref[8:, :] = x_ref[...]
        acc = jnp.zeros((tm, C), F32)
        for k in range(4):
            acc = acc + w_ref[k:k + 1, :] * ext_ref[pl.ds(5 + k, tm), :]
        o_ref[...] = acc

    return pl.pallas_call(
        body, name=name, grid=(S // tm,),
        in_specs=[pl.BlockSpec((tm, C), lambda i: (i, 0)),
                  pl.BlockSpec((8, C), lambda i: (jnp.maximum(i * nb - 1, 0), 0)),
                  pl.BlockSpec((4, C), lambda i: (0, 0))],
        out_specs=pl.BlockSpec((tm, C), lambda i: (i, 0)),
        out_shape=jax.ShapeDtypeStruct((S, C), F32),
        scratch_shapes=[pltpu.VMEM((tm + 8, C), F32)],
        compiler_params=_params(("arbitrary",)),
    )(proj, proj, conv_w)


def _conv_bwd(proj, dout, conv_w, tm, name):
    S = proj.shape[0]
    C = 1536
    nb = tm // 8
    n_steps = S // tm

    def body(x_ref, prev_ref, d_ref, next_ref, w_ref, dx_ref, dw_ref, xext_ref, dext_ref):
        i = pl.program_id(0)
        xext_ref[0:8, :] = jnp.where(i > 0, prev_ref[...], 0.0)
        xext_ref[8:, :] = x_ref[...]
        dext_ref[0:tm, :] = d_ref[...]
        dext_ref[tm:, :] = jnp.where(i < n_steps - 1, next_ref[...], 0.0)
        d = d_ref[...]
        acc = jnp.zeros((tm, C), F32)
        dws = []
        for k in range(4):
            acc = acc + w_ref[k:k + 1, :] * dext_ref[pl.ds(3 - k, tm), :]
            dws.append(jnp.sum(d * xext_ref[pl.ds(5 + k, tm), :], axis=0, keepdims=True))
        dx_ref[...] = acc

        @pl.when(i == 0)
        def _():
            dw_ref[...] = jnp.zeros_like(dw_ref)

        dw_ref[...] += jnp.concatenate(dws + [jnp.zeros((4, C), F32)], axis=0)

    return pl.pallas_call(
        body, name=name, grid=(n_steps,),
        in_specs=[pl.BlockSpec((tm, C), lambda i: (i, 0)),
                  pl.BlockSpec((8, C), lambda i: (jnp.maximum(i * nb - 1, 0), 0)),
                  pl.BlockSpec((tm, C), lambda i: (i, 0)),
                  pl.BlockSpec((8, C), lambda i: (jnp.minimum((i + 1) * nb, S // 8 - 1), 0)),
                  pl.BlockSpec((4, C), lambda i: (0, 0))],
        out_specs=[pl.BlockSpec((tm, C), lambda i: (i, 0)), pl.BlockSpec((8, C), lambda i: (0, 0))],
        out_shape=[jax.ShapeDtypeStruct((S, C), F32), jax.ShapeDtypeStruct((8, C), F32)],
        scratch_shapes=[pltpu.VMEM((tm + 8, C), F32), pltpu.VMEM((tm + 8, C), F32)],
        compiler_params=_params(("arbitrary",)),
    )(proj, proj, dout, dout, conv_w)


def _gdn_scan_fwd(u, wk, qd, kd, qks, gl, name):
    S = u.shape[0]
    nc = S // CHUNK
    W = HEADS * HEAD_DIM

    def body(u_ref, wk_ref, qd_ref, kd_ref, qk0, qk1, qk2, qk3, gl_ref, o_ref, sp_ref, s_ref):
        @pl.when(pl.program_id(0) == 0)
        def _():
            s_ref[...] = jnp.zeros_like(s_ref)

        for h, qk_ref in enumerate((qk0, qk1, qk2, qk3)):
            sl = slice(HEAD_DIM * h, HEAD_DIM * (h + 1))
            s0 = s_ref[h]
            sp_ref[0, h] = s0
            o, s1 = _scan_step(s0, u_ref[:, sl], wk_ref[:, sl], qd_ref[:, sl], kd_ref[:, sl], qk_ref[...], gl_ref[:, sl])
            o_ref[:, sl] = o
            s_ref[h] = s1

    row = pl.BlockSpec((CHUNK, W), lambda n: (n, 0))
    qk_spec = pl.BlockSpec((CHUNK, CHUNK), lambda n: (n, 0))
    return pl.pallas_call(
        body, name=name, grid=(nc,),
        in_specs=[row, row, row, row, qk_spec, qk_spec, qk_spec, qk_spec, pl.BlockSpec((8, W), lambda n: (n, 0))],
        out_specs=[row, pl.BlockSpec((1, HEADS, HEAD_DIM, HEAD_DIM), lambda n: (n, 0, 0, 0))],
        out_shape=[jax.ShapeDtypeStruct((S, W), F32), jax.ShapeDtypeStruct((nc, HEADS, HEAD_DIM, HEAD_DIM), F32)],
        scratch_shapes=[pltpu.VMEM((HEADS, HEAD_DIM, HEAD_DIM), F32)],
        compiler_params=_params(("arbitrary",)),
    )(u, wk, qd, kd, *qks, gl)


def _gdn_scan_bwd(u, wk, qd, kd, qks, gl, s_prev, d_o, name):
    S = u.shape[0]
    nc = S // CHUNK
    W = HEADS * HEAD_DIM

    def body(u_ref, wk_ref, qd_ref, kd_ref, qk0, qk1, qk2, qk3, gl_ref, sp_ref, do_ref,
             du_ref, dwk_ref, dqd_ref, dkd_ref, dqk0, dqk1, dqk2, dqk3, dgl_ref, ds_ref):
        @pl.when(pl.program_id(0) == 0)
        def _():
            ds_ref[...] = jnp.zeros_like(ds_ref)

        for h, (qk_ref, dqk_ref) in enumerate(zip((qk0, qk1, qk2, qk3), (dqk0, dqk1, dqk2, dqk3))):
            sl = slice(HEAD_DIM * h, HEAD_DIM * (h + 1))
            _, vjp = jax.vjp(_scan_step, sp_ref[0, h], u_ref[:, sl], wk_ref[:, sl], qd_ref[:, sl], kd_ref[:, sl],
                             qk_ref[...], gl_ref[:, sl])
            ds0, du, dwk, dqd, dkd, dqk, dgl = vjp((do_ref[:, sl], ds_ref[h]))
            ds_ref[h] = ds0
            du_ref[:, sl] = du
            dwk_ref[:, sl] = dwk
            dqd_ref[:, sl] = dqd
            dkd_ref[:, sl] = dkd
            dqk_ref[...] = dqk
            dgl_ref[:, sl] = dgl

    rev = lambda n: (nc - 1 - n, 0)
    row = pl.BlockSpec((CHUNK, W), rev)
    qk_spec = pl.BlockSpec((CHUNK, CHUNK), rev)
    gl_spec = pl.BlockSpec((8, W), rev)
    qk_shape = jax.ShapeDtypeStruct((S, CHUNK), F32)
    row_shape = jax.ShapeDtypeStruct((S, W), F32)
    return pl.pallas_call(
        body, name=name, grid=(nc,),
        in_specs=[row, row, row, row, qk_spec, qk_spec, qk_spec, qk_spec, gl_spec,
                  pl.BlockSpec((1, HEADS, HEAD_DIM, HEAD_DIM), lambda n: (nc - 1 - n, 0, 0, 0)), row],
        out_specs=[row, row, row, row, qk_spec, qk_spec, qk_spec, qk_spec, gl_spec],
        out_shape=[row_shape] * 4 + [qk_shape] * 4 + [jax.ShapeDtypeStruct((nc * 8, W), F32)],
        scratch_shapes=[pltpu.VMEM((HEADS, HEAD_DIM, HEAD_DIM), F32)],
        compiler_params=_params(("arbitrary",)),
    )(u, wk, qd, kd, *qks, gl, s_prev, d_o)


NEG = -1e30


def _chunk_mask(i, j, t, transposed=False):
    q_axis, k_axis = (1, 0) if transposed else (0, 1)
    r = (i * t + lax.broadcasted_iota(jnp.int32, (t, t), q_axis)) // CHUNK
    c = (j * t + lax.broadcasted_iota(jnp.int32, (t, t), k_axis)) // CHUNK
    return c <= r


def _heads(ref, width):
    return jnp.stack([ref[:, width * h:width * (h + 1)] for h in range(HEADS)])


def _bmm(a, b, dims):
    return lax.dot_general(a.astype(BF16), b.astype(BF16), dims, preferred_element_type=F32)


def _attn_fwd(q, k, v_t, t, name):
    S = q.shape[0]
    n = S // t

    def body(q_ref, k_ref, vt_ref, o_ref, lse_ref, m_ref, l_ref, acc_ref):
        i, j = pl.program_id(0), pl.program_id(1)

        @pl.when(j == 0)
        def _():
            m_ref[...] = jnp.full_like(m_ref, NEG)
            l_ref[...] = jnp.zeros_like(l_ref)
            acc_ref[...] = jnp.zeros_like(acc_ref)

        def update(masked):
            s_t = _bmm(_heads(k_ref, 256), _heads(q_ref, 256), _BNT)
            if masked:
                s_t = jnp.where(_chunk_mask(i, j, t, transposed=True)[None], s_t, NEG)
            m_old = m_ref[...]
            m_new = jnp.maximum(m_old, jnp.max(s_t, axis=1, keepdims=True))
            p_t = jnp.exp(s_t - m_new)
            alpha = jnp.exp(m_old - m_new)
            l_ref[...] = alpha * l_ref[...] + jnp.sum(p_t, axis=1, keepdims=True)
            v_heads = jnp.stack([vt_ref[HEAD_DIM * h:HEAD_DIM * (h + 1), :] for h in range(HEADS)])
            acc_ref[...] = alpha * acc_ref[...] + _bmm(v_heads, p_t, _BNN)
            m_ref[...] = m_new

        @pl.when(j < i)
        def _():
            update(False)

        @pl.when(j == i)
        def _():
            update(True)
            for h in range(HEADS):
                sl = slice(HEAD_DIM * h, HEAD_DIM * (h + 1))
                o_ref[:, sl] = jnp.transpose(acc_ref[h] / l_ref[h])
                lse_ref[:, sl] = jnp.transpose(jnp.broadcast_to(m_ref[h] + jnp.log(l_ref[h]), (HEAD_DIM, t)))

    row = lambda i, j: (i, 0)
    return pl.pallas_call(
        body, name=name, grid=(n, n),
        in_specs=[pl.BlockSpec((t, HEADS * 256), row), pl.BlockSpec((t, HEADS * 256), lambda i, j: (jnp.minimum(j, i), 0)),
                  pl.BlockSpec((HEADS * HEAD_DIM, t), lambda i, j: (0, jnp.minimum(j, i)))],
        out_specs=[pl.BlockSpec((t, HEADS * HEAD_DIM), row)] * 2,
        out_shape=[jax.ShapeDtypeStruct((S, HEADS * HEAD_DIM), F32)] * 2,
        scratch_shapes=[pltpu.VMEM((HEADS, 1, t), F32), pltpu.VMEM((HEADS, 1, t), F32), pltpu.VMEM((HEADS, HEAD_DIM, t), F32)],
        compiler_params=_params(("parallel", "arbitrary")),
    )(q, k, v_t)


def _attn_dq(q, k, v, o, lse, d_o, t, name):
    S = q.shape[0]
    n = S // t

    def body(q_ref, k_ref, v_ref, o_ref, lse_ref, do_ref, dq_ref, st_ref, acc_ref, delta_ref):
        i, j = pl.program_id(0), pl.program_id(1)

        @pl.when(j == 0)
        def _():
            acc_ref[...] = jnp.zeros_like(acc_ref)
            delta_ref[...] = jnp.sum(_heads(do_ref, HEAD_DIM) * _heads(o_ref, HEAD_DIM), axis=2, keepdims=True)

        def update(masked):
            kh = _heads(k_ref, 256)
            d_out = _heads(do_ref, HEAD_DIM)
            s = _bmm(_heads(q_ref, 256), kh, _BNT)
            p = jnp.exp(s - _heads(lse_ref, HEAD_DIM)[:, :, 0:1])
            if masked:
                p = jnp.where(_chunk_mask(i, j, t)[None], p, 0.0)
            dp = _bmm(d_out, _heads(v_ref, HEAD_DIM), _BNT)
            ds = p * (dp - delta_ref[...])
            acc_ref[...] += _bmm(ds, kh, _BNN)

        @pl.when(j < i)
        def _():
            update(False)

        @pl.when(j == i)
        def _():
            update(True)
            lane = lax.broadcasted_iota(jnp.int32, (t, HEAD_DIM), 1)
            stats = jnp.zeros((t, HEAD_DIM), F32)
            for h in range(HEADS):
                dq_ref[:, 256 * h:256 * (h + 1)] = acc_ref[h]
                stats = stats + jnp.where(lane == h, lse_ref[:, HEAD_DIM * h:HEAD_DIM * (h + 1)], 0.0)
                stats = stats + jnp.where(lane == HEADS + h, delta_ref[h], 0.0)
            st_ref[...] = jnp.transpose(stats)[0:8, :]

    kv = lambda i, j: (jnp.minimum(j, i), 0)
    row = lambda i, j: (i, 0)
    wide, narrow = pl.BlockSpec((t, HEADS * 256), row), pl.BlockSpec((t, HEADS * HEAD_DIM), row)
    return pl.pallas_call(
        body, name=name, grid=(n, n),
        in_specs=[wide, pl.BlockSpec((t, HEADS * 256), kv), pl.BlockSpec((t, HEADS * HEAD_DIM), kv), narrow, narrow, narrow],
        out_specs=[wide, pl.BlockSpec((8, t), lambda i, j: (0, i))],
        out_shape=[jax.ShapeDtypeStruct((S, HEADS * 256), F32), jax.ShapeDtypeStruct((8, S), F32)],
        scratch_shapes=[pltpu.VMEM((HEADS, t, 256), F32), pltpu.VMEM((HEADS, t, 1), F32)],
        compiler_params=_params(("parallel", "arbitrary")),
    )(q, k, v, o, lse, d_o)


def _attn_dkv(q, k, v, d_o, stats, t, name):
    S = q.shape[0]
    n = S // t

    def body(q_ref, k_ref, v_ref, do_ref, st_ref, dk_ref, dv_ref, dk_acc, dv_acc):
        j, step = pl.program_id(0), pl.program_id(1)
        i = j + step

        @pl.when(step == 0)
        def _():
            dk_acc[...] = jnp.zeros_like(dk_acc)
            dv_acc[...] = jnp.zeros_like(dv_acc)

        def update(masked):
            qh = _heads(q_ref, 256)
            d_out = _heads(do_ref, HEAD_DIM)
            st = st_ref[...]
            lse_row = jnp.stack([st[h:h + 1, :] for h in range(HEADS)])
            delta_row = jnp.stack([st[HEADS + h:HEADS + h + 1, :] for h in range(HEADS)])
            s_t = _bmm(_heads(k_ref, 256), qh, _BNT)
            p_t = jnp.exp(s_t - lse_row)
            if masked:
                p_t = jnp.where(_chunk_mask(i, j, t, transposed=True)[None], p_t, 0.0)
            dv_acc[...] += _bmm(p_t, d_out, _BNN)
            dp_t = _bmm(_heads(v_ref, HEAD_DIM), d_out, _BNT)
            ds_t = p_t * (dp_t - delta_row)
            dk_acc[...] += _bmm(ds_t, qh, _BNN)

        @pl.when(step == 0)
        def _():
            update(True)

        @pl.when((step > 0) & (i < n))
        def _():
            update(False)

        @pl.when(step == n - 1)
        def _():
            for h in range(HEADS):
                dk_ref[:, 256 * h:256 * (h + 1)] = dk_acc[h]
                dv_ref[:, HEAD_DIM * h:HEAD_DIM * (h + 1)] = dv_acc[h]

    qi = lambda j, s_: (jnp.minimum(j + s_, n - 1), 0)
    kj = lambda j, s_: (j, 0)
    return pl.pallas_call(
        body, name=name, grid=(n, n),
        in_specs=[pl.BlockSpec((t, HEADS * 256), qi), pl.BlockSpec((t, HEADS * 256), kj), pl.BlockSpec((t, HEADS * HEAD_DIM), kj),
                  pl.BlockSpec((t, HEADS * HEAD_DIM), qi), pl.BlockSpec((8, t), lambda j, s_: (0, jnp.minimum(j + s_, n - 1)))],
        out_specs=[pl.BlockSpec((t, HEADS * 256), kj), pl.BlockSpec((t, HEADS * HEAD_DIM), kj)],
        out_shape=[jax.ShapeDtypeStruct((S, HEADS * 256), F32), jax.ShapeDtypeStruct((S, HEADS * HEAD_DIM), F32)],
        scratch_shapes=[pltpu.VMEM((HEADS, t, 256), F32), pltpu.VMEM((HEADS, t, HEAD_DIM), F32)],
        compiler_params=_params(("parallel", "arbitrary")),
    )(q, k, v, d_o, stats)


FFN_PIECE = 2 * D_FF // N_DEV
HID_PIECES = D_FF // FFN_PIECE


def _ffn_up(h, w8, name):
    S = h.shape[0]
    tm = _pick(S, (512, 256, 128))

    def body(h_ref, wg_ref, wu_ref, g_ref, u_ref, hid_ref):
        gate = _dot_raw(h_ref[...], wg_ref[...], "nn")
        up = _dot_raw(h_ref[...], wu_ref[...], "nn")
        g_ref[...] = gate.astype(BF16)
        u_ref[...] = up.astype(BF16)
        hid_ref[...] = (_silu(gate) * up).astype(BF16)

    o_spec = pl.BlockSpec((None, tm, FFN_PIECE), lambda i, j: (j, i, 0))
    return pl.pallas_call(
        body, name=name, grid=(S // tm, HID_PIECES),
        in_specs=[pl.BlockSpec((tm, D_MODEL), lambda i, j: (i, 0)),
                  pl.BlockSpec((None, D_MODEL, FFN_PIECE), lambda i, j: (j, 0, 0)),
                  pl.BlockSpec((None, D_MODEL, FFN_PIECE), lambda i, j: (j + HID_PIECES, 0, 0))],
        out_specs=[o_spec] * 3,
        out_shape=[jax.ShapeDtypeStruct((HID_PIECES, S, FFN_PIECE), BF16)] * 3,
        compiler_params=_params(("parallel", "parallel")),
    )(h, w8, w8)


def _ffn_gw8(h, d_gate, d_up, name):
    S = h.shape[0]
    tm = 512
    tk = _pick(S, (512, 256, 128))
    nk = S // tk

    def body(h_ref, dg_ref, du_ref, o_ref, acc_ref):
        k = pl.program_id(1)

        @pl.when(k == 0)
        def _():
            acc_ref[...] = jnp.zeros_like(acc_ref)

        h_t = jnp.transpose(h_ref[...])
        for p in range(HID_PIECES):
            acc_ref[p] += _dot_raw(h_t, dg_ref[p], "nn")
            acc_ref[HID_PIECES + p] += _dot_raw(h_t, du_ref[p], "nn")

        @pl.when(k == nk - 1)
        def _():
            o_ref[...] = acc_ref[...].astype(o_ref.dtype)

    d_spec = pl.BlockSpec((HID_PIECES, tk, FFN_PIECE), lambda i, k: (0, k, 0))
    return pl.pallas_call(
        body, name=name, grid=(D_MODEL // tm, nk),
        in_specs=[pl.BlockSpec((tk, tm), lambda i, k: (k, i)), d_spec, d_spec],
        out_specs=pl.BlockSpec((2 * HID_PIECES, tm, FFN_PIECE), lambda i, k: (0, i, 0)),
        out_shape=jax.ShapeDtypeStruct((2 * HID_PIECES, D_MODEL, FFN_PIECE), BF16),
        scratch_shapes=[pltpu.VMEM((2 * HID_PIECES, tm, FFN_PIECE), F32)],
        compiler_params=_params(("parallel", "arbitrary")),
    )(h, d_gate, d_up)


def _ffn_dh(d_gate, d_up, w8, name):
    S = d_gate.shape[1]
    tm = _pick(S, (512, 256, 128))

    def body(dg_ref, du_ref, wg_ref, wu_ref, o_ref):
        @pl.when(pl.program_id(1) == 0)
        def _():
            o_ref[...] = jnp.zeros_like(o_ref)

        o_ref[...] += _dot_raw(dg_ref[...], wg_ref[...], "nt") + _dot_raw(du_ref[...], wu_ref[...], "nt")

    d_spec = pl.BlockSpec((None, tm, FFN_PIECE), lambda i, k: (k, i, 0))
    return pl.pallas_call(
        body, name=name, grid=(S // tm, HID_PIECES),
        in_specs=[d_spec, d_spec,
                  pl.BlockSpec((None, D_MODEL, FFN_PIECE), lambda i, k: (k, 0, 0)),
                  pl.BlockSpec((None, D_MODEL, FFN_PIECE), lambda i, k: (k + HID_PIECES, 0, 0))],
        out_specs=pl.BlockSpec((tm, D_MODEL), lambda i, k: (i, 0)),
        out_shape=jax.ShapeDtypeStruct((S, D_MODEL), F32),
        compiler_params=_params(("parallel", "arbitrary")),
    )(d_gate, d_up, w8, w8)


def _swiglu_bwd(d_hid, gate, up):
    sg = _sigmoid(gate)
    return d_hid * up * (sg * (1.0 + gate * (1.0 - sg))), d_hid * (gate * sg)


def _loss_and_grad(x3, target, name):
    S = x3.shape[0]
    tm = _pick(S, (512, 256, 128))
    n = S // tm

    def body(x_ref, t_ref, dx_ref, l_ref):
        i = pl.program_id(0)
        diff = x_ref[...] - t_ref[...]
        dx_ref[...] = diff * (1.0 / D_MODEL)

        @pl.when(i == 0)
        def _():
            l_ref[...] = jnp.zeros_like(l_ref)

        l_ref[...] += jnp.sum(diff * diff, axis=0, keepdims=True)

        @pl.when(i == n - 1)
        def _():
            l_ref[...] = jnp.full(l_ref.shape, (0.5 / D_MODEL) * jnp.sum(l_ref[...]), F32)

    return pl.pallas_call(
        body, name=name, grid=(n,),
        in_specs=[pl.BlockSpec((tm, D_MODEL), lambda i: (i, 0))] * 2,
        out_specs=[pl.BlockSpec((tm, D_MODEL), lambda i: (i, 0)), pl.BlockSpec((1, D_MODEL), lambda i: (0, 0))],
        out_shape=[jax.ShapeDtypeStruct((S, D_MODEL), F32), jax.ShapeDtypeStruct((1, D_MODEL), F32)],
        compiler_params=_params(("arbitrary",)),
    )(x3, target)


def _adamw(w, g, m, v, name):
    R, C = w.shape
    tr = _pick(R, (256, 176, 128, 64, 32, 16, 8))

    def body(w_ref, g_ref, m_ref, v_ref, d_ref, nm_ref, nv_ref):
        g_ = g_ref[...]
        m_ = ADAM_B1 * m_ref[...] + (1.0 - ADAM_B1) * g_
        v_ = ADAM_B2 * v_ref[...] + (1.0 - ADAM_B2) * (g_ * g_)
        m_hat = m_ / (1.0 - ADAM_B1 ** ADAM_STEP)
        v_hat = v_ / (1.0 - ADAM_B2 ** ADAM_STEP)
        d_ref[...] = -ADAM_LR * (m_hat / (jnp.sqrt(v_hat) + ADAM_EPS) + ADAM_WD * w_ref[...])
        nm_ref[...] = m_
        nv_ref[...] = v_

    spec = pl.BlockSpec((tr, C), lambda i: (i, 0))
    return pl.pallas_call(
        body, name=name, grid=(R // tr,),
        in_specs=[spec] * 4, out_specs=[spec] * 3,
        out_shape=[jax.ShapeDtypeStruct((R, C), F32)] * 3,
        compiler_params=_params(("parallel",)),
    )(w, g, m, v)


def _sum_devices(parts, name):
    _, R, C = parts.shape
    tr = _pick(R, (512, 256, 176, 128, 64, 32, 16, 8))

    def body(p_ref, o_ref):
        acc = p_ref[0].astype(F32)
        for d in range(1, N_DEV):
            acc = acc + p_ref[d].astype(F32)
        o_ref[...] = acc

    return pl.pallas_call(
        body, name=name, grid=(R // tr,),
        in_specs=[pl.BlockSpec((N_DEV, tr, C), lambda i: (0, i, 0))],
        out_specs=pl.BlockSpec((tr, C), lambda i: (i, 0)),
        out_shape=jax.ShapeDtypeStruct((R, C), F32),
        compiler_params=_params(("parallel",)),
    )(parts)


def _my_place():
    return lax.axis_index("x"), lax.axis_index("y"), lax.axis_index("c")


def _all_gather(blocks, name):
    n = len(blocks)

    def body(*refs):
        x_refs, out_refs = refs[:n], refs[n:2 * n]
        send_sems, recv_sems, local_sems = refs[2 * n:]
        x, y, c = _my_place()
        me, sibling = (x, y, c), (x, y, 1 - c)
        chips = [(1 - x, y), (x, 1 - y), (1 - x, 1 - y)]

        def copy(a, k, blk, to, own=False):
            slot = out_refs[a].at[4 * blk[0] + 2 * blk[1] + blk[2]]
            return pltpu.make_async_remote_copy(
                src_ref=x_refs[a] if own else slot, dst_ref=slot,
                send_sem=send_sems.at[7 * a + k], recv_sem=recv_sems.at[7 * a + k], device_id=to, device_id_type=MESH)

        mine = [pltpu.make_async_copy(x_refs[a], out_refs[a].at[4 * x + 2 * y + c], local_sems.at[a]) for a in range(n)]
        for cp in mine:
            cp.start()
        first = []
        for j, chip in enumerate(chips):
            first += [copy(a, 1 + j, me, (*chip, c), own=True) for a in range(n)]
        first += [copy(a, 0, me, sibling, own=True) for a in range(n)]
        for cp in first:
            cp.start()
        passed = []
        for j, chip in enumerate(chips):
            for a in range(n):
                copy(a, 1 + j, (*chip, c), me).wait_recv()
                passed.append(copy(a, 4 + j, (*chip, c), sibling))
                passed[-1].start()
        for a in range(n):
            copy(a, 0, sibling, me).wait_recv()
        for j, chip in enumerate(chips):
            for a in range(n):
                copy(a, 4 + j, (*chip, 1 - c), me).wait_recv()
        for cp in first + passed:
            cp.wait_send()
        for cp in mine:
            cp.wait()

    return pl.pallas_call(
        body, name=name,
        out_shape=[jax.ShapeDtypeStruct((N_DEV,) + b.shape, b.dtype) for b in blocks],
        in_specs=[pl.BlockSpec(memory_space=pl.ANY)] * n,
        out_specs=[pl.BlockSpec(memory_space=pl.ANY)] * n,
        scratch_shapes=[pltpu.SemaphoreType.DMA((7 * n,)), pltpu.SemaphoreType.DMA((7 * n,)), pltpu.SemaphoreType.DMA((n,))],
    )(*blocks)


def _all_to_all(pieces, name):
    n = len(pieces)

    def body(*refs):
        x_refs, out_refs = refs[:n], refs[n:2 * n]
        send_sems, recv_sems, local_sems = refs[2 * n:]
        x, y, c = _my_place()
        me = 4 * x + 2 * y + c
        mine = [pltpu.make_async_copy(x_refs[a].at[me], out_refs[a].at[me], local_sems.at[a]) for a in range(n)]
        for cp in mine:
            cp.start()
        copies = []
        for k in (2, 4, 6, 3, 5, 7, 1):
            px = 1 - x if k & 4 else x
            py = 1 - y if k & 2 else y
            pc = 1 - c if k & 1 else c
            peer = 4 * px + 2 * py + pc
            for a in range(n):
                copies.append(pltpu.make_async_remote_copy(
                    src_ref=x_refs[a].at[peer], dst_ref=out_refs[a].at[me],
                    send_sem=send_sems.at[7 * a + k - 1], recv_sem=recv_sems.at[7 * a + k - 1],
                    device_id=(px, py, pc), device_id_type=MESH))
        for cp in copies:
            cp.start()
        for cp in copies:
            cp.wait_recv()
        for cp in copies:
            cp.wait_send()
        for cp in mine:
            cp.wait()

    return pl.pallas_call(
        body, name=name,
        out_shape=[jax.ShapeDtypeStruct(p.shape, p.dtype) for p in pieces],
        in_specs=[pl.BlockSpec(memory_space=pl.ANY)] * n,
        out_specs=[pl.BlockSpec(memory_space=pl.ANY)] * n,
        scratch_shapes=[pltpu.SemaphoreType.DMA((7 * n,)), pltpu.SemaphoreType.DMA((7 * n,)), pltpu.SemaphoreType.DMA((n,))],
    )(*pieces)


def _pad_lanes(v, at=0, width=128):
    return jnp.pad(v, ((0, 0), (at, width - at - v.shape[1])))


def _pack_weights(P):
    W = {}
    w = P["w_in"]
    W["wp"] = jnp.concatenate([w[:, :2048], w[:, 2440:2696], w[:, 2056:2440], w[:, 2696:2760], w[:, 2048:2056],
                               jnp.zeros((D_MODEL, N_IN_PACKED - N_IN), w.dtype)], axis=1).astype(BF16)
    W["conv_w"] = P["gdn_conv_w"].astype(F32)
    W["alog_p"] = _pad_lanes(P["gdn_a_log"], 64)
    W["dt_p"] = _pad_lanes(P["gdn_dt_bias"], 64)
    W["gnw"] = P["gdn_norm_w"]
    W["qnw"] = P["mla_q_norm_w"]
    W["kvnw"] = P["mla_kv_norm_w"]
    uq = P["mla_w_uq"].reshape(Q_LORA, HEADS, HEAD_DIM + ROPE)
    W["wuq"] = jnp.pad(uq, ((0, 0), (0, 0), (0, 256 - HEAD_DIM - ROPE))).reshape(Q_LORA, HEADS * 256).astype(BF16)
    ukv = P["mla_w_ukv"].reshape(KV_LORA, HEADS, 2, HEAD_DIM)
    W["wukv"] = ukv.transpose(0, 2, 1, 3).reshape(KV_LORA, 2 * HEADS * HEAD_DIM).astype(BF16)
    W["qn_w"] = P["qkn_q_nope"]
    W["qr_w"] = _pad_lanes(P["qkn_q_rope"])
    W["kn_w"] = P["qkn_k_nope"]
    W["kr_w"] = _pad_lanes(P["qkn_k_rope"])
    W["onw"] = P["mla_out_norm_w"]
    W["wout"] = P["w_out"].astype(BF16)
    return W


def _unpack_grads(G):
    g = G["wp"]
    uq = G["wuq"].reshape(Q_LORA, HEADS, 256)[:, :, :HEAD_DIM + ROPE].reshape(Q_LORA, HEADS * (HEAD_DIM + ROPE))
    ukv = G["wukv"].reshape(KV_LORA, 2, HEADS, HEAD_DIM).transpose(0, 2, 1, 3).reshape(KV_LORA, 2 * HEADS * HEAD_DIM)
    return {
        "w_in": jnp.concatenate([g[:, :2048], g[:, 2752:2760], g[:, 2304:2688], g[:, 2048:2304], g[:, 2688:2752]], axis=1),
        "gdn_conv_w": G["conv_w"], "gdn_a_log": G["alog_p"][:, 64:68], "gdn_dt_bias": G["dt_p"][:, 64:68],
        "gdn_norm_w": G["gnw"], "mla_q_norm_w": G["qnw"], "mla_w_uq": uq, "mla_kv_norm_w": G["kvnw"], "mla_w_ukv": ukv,
        "qkn_q_nope": G["qn_w"], "qkn_q_rope": G["qr_w"][:, :ROPE], "qkn_k_nope": G["kn_w"], "qkn_k_rope": G["kr_w"][:, :ROPE],
        "mla_out_norm_w": G["onw"], "w_out": G["wout"],
    }


def _rope_tables(positions):
    half = ROPE // 2
    inv_freq = ROPE_BASE ** (-jnp.arange(half, dtype=F32) / half)
    ang = positions.astype(F32)[:, None] * inv_freq
    cos, sin = jnp.cos(ang), jnp.sin(ang)
    zeros = jnp.zeros((positions.shape[0], 128 - ROPE), F32)
    return jnp.concatenate([cos, cos, zeros], axis=1), jnp.concatenate([-sin, sin, zeros], axis=1)


def _mod_fn(x, scale, shift):
    return (_modulate(x, scale, shift),)


def _ffn_fwd(x, scale, shift, gate_w, w8, wo4, tag):
    S = x.shape[0]
    tm = _pick(S, (512, 256, 128))
    (h,) = _rowwise(_mod_fn, [(x, tm, D_MODEL, 0)], [scale, shift], [(tm, D_MODEL, BF16)], S // tm, tag + "_mod")
    gate, up, hid = _ffn_up(h, w8, tag + "_up")
    mn = pl.BlockSpec((tm, D_MODEL), lambda i, j, k: (i, j))
    f, x_out = _mmg(hid, wo4, "nn", name=tag + "_down", grid=(S // tm, 1, HID_PIECES),
                    a_spec=pl.BlockSpec((None, tm, FFN_PIECE), lambda i, j, k: (k, i, 0)),
                    b_spec=pl.BlockSpec((None, FFN_PIECE, D_MODEL), lambda i, j, k: (k, 0, j)),
                    out_spec=mn, out_shapes=[jax.ShapeDtypeStruct((S, D_MODEL), F32)] * 2, acc_shape=(tm, D_MODEL),
                    extras=[x, gate_w], extra_specs=[mn, pl.BlockSpec((1, D_MODEL), lambda i, j, k: (0, j))],
                    epi=lambda acc, x_, g_: (acc, x_ + 0.5 * g_ * acc))
    return x_out, (h, gate, up, hid, f)


def _ffn_bwd(d_out, x, scale, shift, gate_w, w8, wo4, saved, tag):
    h, gate, up, hid, f = saved
    S = x.shape[0]
    tm = _pick(S, (512, 256, 128))
    tk = _pick(S, (512, 256, 128))
    n = S // tm
    (df,), (d_gate_w,) = _rowwise_bwd(lambda f_, g_: (0.5 * g_ * f_,), [(f, tm, D_MODEL, 0)], [], [gate_w],
                                      [(d_out, tm, D_MODEL, 0)], n, tag + "_dres", row_dtypes=(BF16,))
    piece = pl.BlockSpec((None, tm, FFN_PIECE), lambda i, j, k: (j, i, 0))
    d_gate, d_up = _mmg(df, wo4, "nt", name=tag + "_ddown", grid=(n, HID_PIECES, 1),
                        a_spec=pl.BlockSpec((tm, D_MODEL), lambda i, j, k: (i, 0)),
                        b_spec=pl.BlockSpec((None, FFN_PIECE, D_MODEL), lambda i, j, k: (j, 0, 0)),
                        out_spec=piece, out_shapes=[jax.ShapeDtypeStruct((HID_PIECES, S, FFN_PIECE), BF16)] * 2,
                        acc_shape=(tm, FFN_PIECE), extras=[gate, up], extra_specs=[piece, piece], epi=_swiglu_bwd)
    g_wo4 = _mmg(hid, df, "tn", name=tag + "_gwo", grid=(HID_PIECES, 1, S // tk),
                 a_spec=pl.BlockSpec((None, tk, FFN_PIECE), lambda i, j, k: (i, k, 0)),
                 b_spec=pl.BlockSpec((tk, D_MODEL), lambda i, j, k: (k, j)),
                 out_spec=pl.BlockSpec((None, FFN_PIECE, D_MODEL), lambda i, j, k: (i, 0, j)),
                 out_shapes=[jax.ShapeDtypeStruct((HID_PIECES, FFN_PIECE, D_MODEL), BF16)], acc_shape=(FFN_PIECE, D_MODEL))
    g_w8 = _ffn_gw8(h, d_gate, d_up, tag + "_gw8")
    dh = _ffn_dh(d_gate, d_up, w8, tag + "_dh")
    (dx,), (d_scale, d_shift) = _rowwise_bwd(_mod_fn, [(x, tm, D_MODEL, 0)], [], [scale, shift], [(dh, tm, D_MODEL, 0)],
                                             n, tag + "_dmod", adds=[(0, d_out)])
    return dx, d_scale, d_shift, d_gate_w, g_w8, g_wo4


def _mixer_fwd(x1, scale, shift, gate_w, cos_p, sin_p, W):
    S = x1.shape[0]
    tm = _pick(S, (512, 256, 128))
    tv = _pick(S, (256, 128))
    ta = _pick(S, (512, 256, 128))
    nc = S // CHUNK
    (h2,) = _rowwise(_mod_fn, [(x1, tm, D_MODEL, 0)], [scale, shift], [(tm, D_MODEL, BF16)], S // tm, "mix_mod")
    proj = _mm(h2, W["wp"], "nn", name="mix_proj")
    qkvc = _conv_fwd(proj, W["conv_w"], tv, "gdn_conv")
    kab = (proj, tv, 128, 21)
    q_a, k_a, v_a, gb = _rowwise(_gdn_pre_fn, [(qkvc, tv, 1536, 0), kab], [W["alog_p"], W["dt_p"]],
                                 [(tv, 512, F32)] * 3 + [(tv, 128, F32)], S // tv, "gdn_pre")
    ti = _pick(S, INTRA_ROWS)
    intra = _rowwise(_gdn_intra_fn, [(q_a, ti, 512, 0), (k_a, ti, 512, 0), (v_a, ti, 512, 0), (gb, ti, 128, 0)],
                     [], [(ti, 512, F32)] * 4 + [(ti, CHUNK, F32)] * 4 + [(ti // 8, 512, F32)], S // ti, "gdn_intra")
    u, wk, qd, kd, qks, gl = intra[0], intra[1], intra[2], intra[3], tuple(intra[4:8]), intra[8]
    o_a, s_prev = _gdn_scan_fwd(u, wk, qd, kd, qks, gl, "gdn_scan")
    mla_params = [W["qnw"], W["kvnw"], W["wuq"], W["wukv"], W["qn_w"], W["qr_w"], W["kn_w"], W["kr_w"]]
    def mla_pre_with_vt(*a):
        q_, k_, v_ = _mla_pre_fn(*a)
        return q_, k_, v_, jnp.transpose(v_)

    q_b, k_b, v_b, vt_b = _rowwise(mla_pre_with_vt,
                                   [(proj, tv, 256, 8), (proj, tv, 384, 6), kab, (cos_p, tv, 128, 0), (sin_p, tv, 128, 0)],
                                   mla_params, [(tv, 1024, BF16), (tv, 1024, BF16), (tv, 512, BF16), (512, tv, BF16, "across")],
                                   S // tv, "mla_pre")
    o_b, lse = _attn_fwd(q_b, k_b, vt_b, ta, "mla_attn")
    (mixed,) = _rowwise(_mix_post_fn, [(o_a, tv, 512, 0), (proj, tv, 512, 3), (o_b, tv, 512, 0)], [W["gnw"], W["onw"]],
                        [(tv, D_MODEL, BF16)], S // tv, "mix_post")
    y, x2 = _mm(mixed, W["wout"], "nn", name="mix_out", out_dtypes=(F32, F32), extras=[x1], extra_params=[gate_w],
                epi=lambda acc, x_, g_: (acc, x_ + g_ * acc))
    saved = (h2, proj, qkvc, q_a, k_a, v_a, gb, u, wk, qd, kd, qks, gl, s_prev, o_a, q_b, k_b, v_b, o_b, lse, mixed, y)
    return x2, saved


def _mixer_bwd(d_out, x1, scale, shift, gate_w, cos_p, sin_p, W, saved):
    (h2, proj, qkvc, q_a, k_a, v_a, gb, u, wk, qd, kd, qks, gl, s_prev, o_a, q_b, k_b, v_b, o_b, lse, mixed, y) = saved
    S = x1.shape[0]
    tm = _pick(S, (512, 256, 128))
    tv = _pick(S, (256, 128))
    ta = _pick(S, (512, 256, 128))
    nc = S // CHUNK
    G = {}
    (dy,), (G["g2"],) = _rowwise_bwd(lambda y_, g_: (g_ * y_,), [(y, tm, D_MODEL, 0)], [], [gate_w],
                                     [(d_out, tm, D_MODEL, 0)], S // tm, "mix_dres", row_dtypes=(BF16,))
    d_mixed = _mm(dy, W["wout"], "nt", name="mix_dout")
    G["wout"] = _mm(mixed, dy, "tn", name="mix_gwout")
    (do_a, dz, do_b), (G["gnw"], G["onw"]) = _rowwise_bwd(
        _mix_post_fn, [(o_a, tv, 512, 0), (proj, tv, 512, 3), (o_b, tv, 512, 0)], [], [W["gnw"], W["onw"]],
        [(d_mixed, tv, D_MODEL, 0)], S // tv, "mix_dpost")
    dq_b, stats = _attn_dq(q_b, k_b, v_b, o_b, lse, do_b, ta, "mla_dq")
    dk_b, dv_b = _attn_dkv(q_b, k_b, v_b, do_b, stats, ta, "mla_dkv")
    kab = (proj, tv, 128, 21)
    mla_params = [W["qnw"], W["kvnw"], W["wuq"], W["wukv"], W["qn_w"], W["qr_w"], W["kn_w"], W["kr_w"]]
    (d_ckv, d_cq, d_kab), mla_grads = _rowwise_bwd(
        _mla_pre_fn, [(proj, tv, 256, 8), (proj, tv, 384, 6), kab], [(cos_p, tv, 128, 0), (sin_p, tv, 128, 0)], mla_params,
        [(dq_b, tv, 1024, 0), (dk_b, tv, 1024, 0), (dv_b, tv, 512, 0)], S // tv, "mla_dpre")
    for key, g in zip(("qnw", "kvnw", "wuq", "wukv", "qn_w", "qr_w", "kn_w", "kr_w"), mla_grads):
        G[key] = g
    scan_grads = _gdn_scan_bwd(u, wk, qd, kd, qks, gl, s_prev, do_a, "gdn_dscan")
    ti = _pick(S, INTRA_ROWS)
    intra_douts = [(scan_grads[i], ti, 512, 0) for i in range(4)] + [(scan_grads[4 + i], ti, CHUNK, 0) for i in range(4)]
    intra_douts.append((scan_grads[8], ti // 8, 512, 0))
    (dq_a, dk_a, dv_a, d_gb), _ = _rowwise_bwd(
        _gdn_intra_fn, [(q_a, ti, 512, 0), (k_a, ti, 512, 0), (v_a, ti, 512, 0), (gb, ti, 128, 0)], [], [],
        intra_douts, S // ti, "gdn_dintra")
    (d_qkvc, d_kab), (G["alog_p"], G["dt_p"]) = _rowwise_bwd(
        _gdn_pre_fn, [(qkvc, tv, 1536, 0), kab], [], [W["alog_p"], W["dt_p"]],
        [(dq_a, tv, 512, 0), (dk_a, tv, 512, 0), (dv_a, tv, 512, 0), (d_gb, tv, 128, 0)], S // tv, "gdn_dpre",
        adds=[(1, d_kab)])
    d_qkv, g_conv = _conv_bwd(proj, d_qkvc, W["conv_w"], tv, "gdn_dconv")
    G["conv_w"] = g_conv[:4]
    d_proj = jnp.concatenate([d_qkv, dz, d_ckv, d_cq, d_kab], axis=1).astype(BF16)
    G["wp"] = _mm(h2, d_proj, "tn", name="mix_gwp")
    dh2 = _mm(d_proj, W["wp"], "nt", name="mix_dproj")
    (dx1,), (G["s2"], G["sh2"]) = _rowwise_bwd(_mod_fn, [(x1, tm, D_MODEL, 0)], [], [scale, shift],
                                               [(dh2, tm, D_MODEL, 0)], S // tm, "mix_dmod", adds=[(0, d_out)])
    return dx1, G


def _local_step(x, target, mod, cos_p, sin_p, W):
    sh1, s1, g1, sh2, s2, g2, sh3, s3, g3 = [mod[:, D_MODEL * i:D_MODEL * (i + 1)] for i in range(N_MOD)]
    x1, saved1 = _ffn_fwd(x, s1, sh1, g1, W["f1_w8"], W["f1_wo4"], "ffn1")
    x2, saved2 = _mixer_fwd(x1, s2, sh2, g2, cos_p, sin_p, W)
    x3, saved3 = _ffn_fwd(x2, s3, sh3, g3, W["f2_w8"], W["f2_wo4"], "ffn2")
    dx3, loss_row = _loss_and_grad(x3, target, "loss")
    dx2, d_s3, d_sh3, d_g3, g_f2_w8, g_f2_wo4 = _ffn_bwd(dx3, x2, s3, sh3, g3, W["f2_w8"], W["f2_wo4"], saved3, "ffn2")
    dx1, G = _mixer_bwd(dx2, x1, s2, sh2, g2, cos_p, sin_p, W, saved2)
    dx, d_s1, d_sh1, d_g1, g_f1_w8, g_f1_wo4 = _ffn_bwd(dx1, x, s1, sh1, g1, W["f1_w8"], W["f1_wo4"], saved1, "ffn1")
    d_mod = jnp.concatenate([d_sh1, d_s1, d_g1, G.pop("sh2"), G.pop("s2"), G.pop("g2"), d_sh3, d_s3, d_g3], axis=1)
    G.update(f1_w8=g_f1_w8, f1_wo4=g_f1_wo4, f2_w8=g_f2_w8, f2_wo4=g_f2_wo4)
    return loss_row, dx, d_mod, G


WEIGHT_NAMES = ("w_ada", "b_ada", "ffn1_w_in", "ffn1_w_out", "w_in", "gdn_conv_w", "gdn_a_log", "gdn_dt_bias", "gdn_norm_w",
                "mla_q_norm_w", "mla_w_uq", "mla_kv_norm_w", "mla_w_ukv", "qkn_q_nope", "qkn_q_rope", "qkn_k_nope",
                "qkn_k_rope", "mla_out_norm_w", "w_out", "ffn2_w_in", "ffn2_w_out")
FFN_SHARDED = ("ffn1_w_in", "ffn1_w_out", "ffn2_w_in", "ffn2_w_out")
SHEETED = (("w_in", "col"), ("gdn_conv_w", "col"), ("mla_w_uq", "col"), ("mla_w_ukv", "col"), ("w_out", "row"))
MOD_ROWS = N_MOD * D_MODEL // 128
SMALL = {"gdn_a_log": (MOD_ROWS, 1, 64, 4), "gdn_dt_bias": (MOD_ROWS + 1, 1, 64, 4), "gdn_norm_w": (MOD_ROWS + 2, 1, 0, 128),
         "mla_q_norm_w": (MOD_ROWS + 3, 3, 0, 384), "mla_kv_norm_w": (MOD_ROWS + 6, 2, 0, 256),
         "qkn_q_nope": (MOD_ROWS + 8, 1, 0, 128), "qkn_q_rope": (MOD_ROWS + 9, 1, 0, 64), "qkn_k_nope": (MOD_ROWS + 10, 1, 0, 128),
         "qkn_k_rope": (MOD_ROWS + 11, 1, 0, 64), "mla_out_norm_w": (MOD_ROWS + 12, 1, 0, 128)}
LOSS_ROW = MOD_ROWS + 13
SHEET_ROWS = 88


def _to_sheet(flat, dtype, sublanes):
    n = flat.shape[-1]
    unit = sublanes * 128
    pad = (-n) % unit
    flat = jnp.pad(flat.astype(dtype), [(0, 0)] * (flat.ndim - 1) + [(0, pad)])
    return flat.reshape(flat.shape[:-1] + ((n + pad) // 128, 128))


def _small_sheet(b_like, small):
    sheet = jnp.zeros((SHEET_ROWS, 128), F32).at[:MOD_ROWS].set(b_like.reshape(MOD_ROWS, 128))
    for name, (row, rows, lane, n) in SMALL.items():
        v = small[name].reshape(1, n)
        if rows == 1:
            sheet = sheet.at[row, lane:lane + n].set(v[0])
        else:
            sheet = sheet.at[row:row + rows].set(v.reshape(rows, 128))
    return sheet


def _from_small_sheet(sheet):
    out = {"b_ada": sheet[:MOD_ROWS].reshape(1, N_MOD * D_MODEL)}
    for name, (row, rows, lane, n) in SMALL.items():
        out[name] = sheet[row, lane:lane + n].reshape(1, n) if rows == 1 else sheet[row:row + rows].reshape(1, n)
    return out


def kernel(x, c, positions, w_ada, b_ada, ffn1_w_in, ffn1_w_out, w_in, gdn_conv_w, gdn_a_log, gdn_dt_bias, gdn_norm_w, mla_q_norm_w, mla_w_uq, mla_kv_norm_w, mla_w_ukv, qkn_q_nope, qkn_q_rope, qkn_k_nope, qkn_k_rope, mla_out_norm_w, w_out, ffn2_w_in, ffn2_w_out, loss_target, m_w_ada, m_b_ada, m_ffn1_w_in, m_ffn1_w_out, m_w_in, m_gdn_conv_w, m_gdn_a_log, m_gdn_dt_bias, m_gdn_norm_w, m_mla_q_norm_w, m_mla_w_uq, m_mla_kv_norm_w, m_mla_w_ukv, m_qkn_q_nope, m_qkn_q_rope, m_qkn_k_nope, m_qkn_k_rope, m_mla_out_norm_w, m_w_out, m_ffn2_w_in, m_ffn2_w_out, v_w_ada, v_b_ada, v_ffn1_w_in, v_ffn1_w_out, v_w_in, v_gdn_conv_w, v_gdn_a_log, v_gdn_dt_bias, v_gdn_norm_w, v_mla_q_norm_w, v_mla_w_uq, v_mla_kv_norm_w, v_mla_w_ukv, v_qkn_q_nope, v_qkn_q_rope, v_qkn_k_nope, v_qkn_k_rope, v_mla_out_norm_w, v_w_out, v_ffn2_w_in, v_ffn2_w_out):
    args = locals()
    w = {n: args[n] for n in WEIGHT_NAMES}
    m = {n: args["m_" + n] for n in WEIGHT_NAMES}
    v = {n: args["v_" + n] for n in WEIGHT_NAMES}
    me = 4 * lax.axis_index("x") + 2 * lax.axis_index("y") + lax.axis_index("c")
    cols = N_MOD * D_MODEL // N_DEV
    shard = {n: w[n][0] for n in FFN_SHARDED + tuple(s[0] for s in SHEETED)}

    sc = c * _sigmoid(c)
    first = _to_sheet(jnp.concatenate([sc.reshape(-1), shard["gdn_conv_w"].reshape(-1)]), F32, 8)
    (first_all,) = _all_gather([first], "gather_c")
    sc_all = first_all[:, :D_MODEL // 128].reshape(N_DEV, D_MODEL)
    n_taps = shard["gdn_conv_w"].size
    conv_all = first_all.reshape(N_DEV, -1)[:, D_MODEL:D_MODEL + n_taps].reshape(N_DEV, 4, -1)
    b_mine = lax.dynamic_slice(b_ada, (0, me * cols), (1, cols))
    mod_cols = _mm(sc_all, w_ada[0], "nn", name="ada_mod", extra_params=[b_mine], epi=lambda acc, b_: (acc + b_,))
    (mod_all,) = _all_to_all([_to_sheet(mod_cols, F32, 8)], "scatter_mod")
    mod = mod_all.reshape(N_DEV, -1)[:, :cols].reshape(1, N_MOD * D_MODEL)

    travel = [s for s in SHEETED if s[0] != "gdn_conv_w"]
    flat_w = jnp.concatenate([shard[n].reshape(-1).astype(BF16) for n, _ in travel])
    gathered = _all_gather([shard[n].astype(BF16) for n in FFN_SHARDED] + [_to_sheet(flat_w, BF16, 16)], "gather_w")
    w_all = gathered[-1].reshape(N_DEV, -1)
    P, off = {}, 0
    for n, kind in travel:
        r, cc = shard[n].shape
        piece = w_all[:, off:off + r * cc].reshape(N_DEV, r, cc)
        P[n] = jnp.concatenate(list(piece), axis=1) if kind == "col" else piece.reshape(N_DEV * r, cc)
        off += r * cc
    P["gdn_conv_w"] = jnp.concatenate(list(conv_all), axis=1)
    for n in SMALL:
        P[n] = w[n]
    W = _pack_weights(P)
    W.update(f1_w8=gathered[0], f1_wo4=gathered[1].reshape(HID_PIECES, FFN_PIECE, D_MODEL),
             f2_w8=gathered[2], f2_wo4=gathered[3].reshape(HID_PIECES, FFN_PIECE, D_MODEL))
    cos_p, sin_p = _rope_tables(positions[0])

    loss_row, dx, d_mod, G = _local_step(x[0], loss_target[0], mod, cos_p, sin_p, W)
    g_full = _unpack_grads(G)

    sheet = _small_sheet(d_mod, g_full).at[LOSS_ROW].set(loss_row[0, :128])
    (sheets,) = _all_gather([sheet], "gather_small")
    summed = _sum_devices(sheets, "sum_small")
    d_mod_all = sheets[:, :MOD_ROWS].reshape(N_DEV, N_MOD * D_MODEL)
    d_mod_mine = lax.dynamic_slice(d_mod_all, (0, me * cols), (N_DEV, cols))
    grads = _from_small_sheet(summed)
    grads["w_ada"] = _mm(sc_all, d_mod_mine, "tn", name="ada_gw", hi=True)
    loss = summed[LOSS_ROW, 0]

    pieces = []
    for n, kind in SHEETED:
        r, cc = shard[n].shape
        g = g_full[n]
        pieces.append(jnp.stack([g[:, cc * p:cc * (p + 1)].reshape(-1) for p in range(N_DEV)]) if kind == "col"
                      else g.reshape(N_DEV, r * cc))
    g_sheet = _to_sheet(jnp.concatenate(pieces, axis=1), BF16, 512)
    ffn_pieces = [G["f1_w8"], G["f1_wo4"].reshape((N_DEV,) + shard["ffn1_w_out"].shape),
                  G["f2_w8"], G["f2_wo4"].reshape((N_DEV,) + shard["ffn2_w_out"].shape)]
    arrived = _all_to_all(ffn_pieces + [g_sheet], "scatter_grads")
    for n, parts in zip(FFN_SHARDED, arrived):
        grads[n] = _sum_devices(parts, "sum_" + n)
    g_mine = _sum_devices(arrived[-1], "sum_grads").reshape(-1)
    off = 0
    for n, _ in SHEETED:
        grads[n] = g_mine[off:off + shard[n].size].reshape(shard[n].shape)
        off += shard[n].size

    delta, new_m, new_v = {}, {}, {}
    for n in ("w_ada",) + FFN_SHARDED + tuple(s[0] for s in SHEETED):
        delta[n], new_m[n], new_v[n] = _adamw(w[n][0], grads[n], m[n][0], v[n][0], "adamw_" + n)
    small_in = [_small_sheet(t["b_ada"], t) for t in (w, grads, m, v)]
    for res, out in zip(_adamw(*small_in, "adamw_small"), (delta, new_m, new_v)):
        out.update(_from_small_sheet(res))

    def shaped(d):
        return [d[n].reshape(w[n].shape) for n in WEIGHT_NAMES]

    return (loss, dx[None], *shaped(grads), *shaped(delta), *shaped(new_m), *shaped(new_v))
```

```python
import functools

import jax
import jax.numpy as jnp
from jax import lax
from jax.experimental import pallas as pl
from jax.experimental.pallas import tpu as pltpu

F32 = jnp.float32
BF16 = jnp.bfloat16

D_MODEL = 1024
D_FF = 2816
N_MOD = 9
HEADS = 4
HEAD_DIM = 128
CHUNK = 64
EPS = 1e-6
ROPE = 64
Q_LORA = 384
KV_LORA = 256
N_IN = 2760
N_IN_PACKED = 2816
ROPE_BASE = 10000.0
N_DEV = 8

ADAM_LR = 0.001
ADAM_B1 = 0.9
ADAM_B2 = 0.999
ADAM_EPS = 1e-08
ADAM_WD = 0.01
ADAM_STEP = 10

VMEM_LIMIT_BYTES = 56 * 1024 * 1024
MESH = pl.DeviceIdType.MESH


def _params(sem=None):
    return pltpu.CompilerParams(dimension_semantics=sem, vmem_limit_bytes=VMEM_LIMIT_BYTES)


def _pick(dim, prefs):
    for p in prefs:
        if dim % p == 0:
            return p
    return dim


_DIMS = {"nn": (((1,), (0,)), ((), ())), "nt": (((1,), (1,)), ((), ())), "tn": (((0,), (0,)), ((), ()))}


def _dot_raw(a, b, mode):
    return lax.dot_general(a.astype(BF16), b.astype(BF16), _DIMS[mode], preferred_element_type=F32)


def _dot_hi(a, b, mode="nn"):
    return lax.dot_general(a, b, _DIMS[mode], precision=lax.Precision.HIGHEST, preferred_element_type=F32)


@functools.partial(jax.custom_vjp, nondiff_argnums=(2,))
def _bdot(a, b, mode):
    return _dot_raw(a, b, mode)


def _bdot_fwd(a, b, mode):
    return _dot_raw(a, b, mode), (a, b)


def _bdot_bwd(mode, res, g):
    a, b = res
    if mode == "nn":
        return _dot_raw(g, b, "nt"), _dot_raw(a, g, "tn")
    if mode == "nt":
        return _dot_raw(g, b, "nn"), _dot_raw(g, a, "tn")
    return _dot_raw(b, g, "nt"), _dot_raw(a, g, "nn")


_bdot.defvjp(_bdot_fwd, _bdot_bwd)


def _mm(a, b, mode, *, name, out_dtypes=(F32,), epi=None, extras=(), extra_params=(), hi=False,
        tm=None, tn=None, tk=None):
    if mode == "nn":
        (M, K), (_, N) = a.shape, b.shape
    elif mode == "nt":
        (M, K), (N, _) = a.shape, b.shape
    else:
        (K, M), (_, N) = a.shape, b.shape
    tm = tm or _pick(M, (512, 1408, 256, 128) if mode == "tn" else (512, 384, 352, 256, 128))
    tn = tn or _pick(N, (1024, 1408, 768, 512, 384, 256, 128))
    tk = tk or _pick(K, (1024, 1408, 512, 384, 256, 128))
    a_spec = {"nn": pl.BlockSpec((tm, tk), lambda i, j, k: (i, k)), "nt": pl.BlockSpec((tm, tk), lambda i, j, k: (i, k)),
              "tn": pl.BlockSpec((tk, tm), lambda i, j, k: (k, i))}[mode]
    b_spec = {"nn": pl.BlockSpec((tk, tn), lambda i, j, k: (k, j)), "nt": pl.BlockSpec((tn, tk), lambda i, j, k: (j, k)),
              "tn": pl.BlockSpec((tk, tn), lambda i, j, k: (k, j))}[mode]
    mn_spec = pl.BlockSpec((tm, tn), lambda i, j, k: (i, j))
    return _mmg(a, b, mode, name=name, grid=(M // tm, N // tn, K // tk), a_spec=a_spec, b_spec=b_spec, out_spec=mn_spec,
                out_shapes=[jax.ShapeDtypeStruct((M, N), dt) for dt in out_dtypes], acc_shape=(tm, tn), epi=epi,
                extras=list(extras) + list(extra_params),
                extra_specs=[mn_spec] * len(extras) + [pl.BlockSpec((1, tn), lambda i, j, k: (0, j))] * len(extra_params),
                hi=hi)


def _mmg(a, b, mode, *, name, grid, a_spec, b_spec, out_spec, out_shapes, acc_shape, epi=None, extras=(),
         extra_specs=(), hi=False):
    nk = grid[2]
    n_e, n_o = len(extras), len(out_shapes)

    def body(*refs):
        a_ref, b_ref = refs[:2]
        e_refs = refs[2:2 + n_e]
        o_refs = refs[2 + n_e:2 + n_e + n_o]
        acc_ref = refs[-1]
        k = pl.program_id(2)

        @pl.when(k == 0)
        def _():
            acc_ref[...] = jnp.zeros_like(acc_ref)

        if hi:
            acc_ref[...] += _dot_hi(a_ref[...].astype(F32), b_ref[...].astype(F32), mode)
        else:
            acc_ref[...] += _dot_raw(a_ref[...], b_ref[...], mode)

        @pl.when(k == nk - 1)
        def _():
            acc = acc_ref[...]
            outs = (acc,) if epi is None else epi(acc, *[e[...].astype(F32) for e in e_refs])
            for o_ref, o in zip(o_refs, outs):
                o_ref[...] = o.astype(o_ref.dtype)

    outs = pl.pallas_call(
        body, name=name, grid=grid,
        in_specs=[a_spec, b_spec] + list(extra_specs),
        out_specs=[out_spec] * n_o,
        out_shape=list(out_shapes),
        scratch_shapes=[pltpu.VMEM(acc_shape, F32)],
        compiler_params=_params(("parallel", "parallel", "arbitrary")),
    )(a, b, *extras)
    return outs if n_o > 1 else outs[0]


def _row_spec(th, cw, ci):
    return pl.BlockSpec((th, cw), lambda i: (i, ci))


def _full_spec(shape):
    return pl.BlockSpec(shape, lambda i: (0,) * len(shape))


def _rowwise(fn, rows, params, outs, n_steps, name):
    n_r, n_p, n_o = len(rows), len(params), len(outs)

    def body(*refs):
        vals = [r[...].astype(F32) for r in refs[:n_r + n_p]]
        res = fn(*vals)
        for o_ref, o in zip(refs[n_r + n_p:], res):
            o_ref[...] = o.astype(o_ref.dtype)

    across = [len(o) == 4 for o in outs]
    res = pl.pallas_call(
        body, name=name, grid=(n_steps,),
        in_specs=[_row_spec(th, cw, ci) for (_, th, cw, ci) in rows] + [_full_spec(p.shape) for p in params],
        out_specs=[pl.BlockSpec((o[0], o[1]), lambda i: (0, i)) if ac else _row_spec(o[0], o[1], 0)
                   for o, ac in zip(outs, across)],
        out_shape=[jax.ShapeDtypeStruct((o[0], n_steps * o[1]) if ac else (n_steps * o[0], o[1]), o[2])
                   for o, ac in zip(outs, across)],
        compiler_params=_params(("parallel",)),
    )(*[r[0] for r in rows], *params)
    return res


def _rowwise_bwd(fn, rows, aux, params, douts, n_steps, name, row_dtypes=None, adds=()):
    n_r, n_a, n_p, n_d, n_add = len(rows), len(aux), len(params), len(douts), len(adds)
    row_dtypes = row_dtypes or (F32,) * n_r

    def body(*refs):
        it = iter(refs)
        r_vals = [next(it)[...].astype(F32) for _ in range(n_r)]
        a_vals = [next(it)[...].astype(F32) for _ in range(n_a)]
        p_vals = [next(it)[...].astype(F32) for _ in range(n_p)]
        d_vals = [next(it)[...].astype(F32) for _ in range(n_d)]
        add_vals = [next(it)[...].astype(F32) for _ in range(n_add)]
        dr_refs = [next(it) for _ in range(n_r)]
        dp_refs = [next(it) for _ in range(n_p)]

        def f(*rp):
            return tuple(fn(*rp[:n_r], *a_vals, *rp[n_r:]))

        _, vjp = jax.vjp(f, *r_vals, *p_vals)
        grads = list(vjp(tuple(d_vals)))
        for (ri, _), av in zip(adds, add_vals):
            grads[ri] = grads[ri] + av
        for dr_ref, g in zip(dr_refs, grads[:n_r]):
            dr_ref[...] = g.astype(dr_ref.dtype)

        @pl.when(pl.program_id(0) == 0)
        def _():
            for dp_ref in dp_refs:
                dp_ref[...] = jnp.zeros_like(dp_ref)

        for dp_ref, g in zip(dp_refs, grads[n_r:]):
            dp_ref[...] += g

    all_rows = list(rows) + list(aux) + list(douts) + [(arr,) + tuple(rows[ri][1:3]) + (0,) for ri, arr in adds]
    in_specs = ([_row_spec(th, cw, ci) for (_, th, cw, ci) in list(rows) + list(aux)]
                + [_full_spec(p.shape) for p in params]
                + [_row_spec(th, cw, ci) for (_, th, cw, ci) in all_rows[n_r + n_a:]])
    res = pl.pallas_call(
        body, name=name, grid=(n_steps,),
        in_specs=in_specs,
        out_specs=[_row_spec(th, cw, 0) for (_, th, cw, _) in rows] + [_full_spec(p.shape) for p in params],
        out_shape=[jax.ShapeDtypeStruct((n_steps * th, cw), dt) for (_, th, cw, _), dt in zip(rows, row_dtypes)]
        + [jax.ShapeDtypeStruct(p.shape, F32) for p in params],
        compiler_params=_params(("arbitrary",)),
    )(*[r[0] for r in list(rows) + list(aux)], *params, *[r[0] for r in all_rows[n_r + n_a:]])
    return res[:n_r], res[n_r:]


def _sigmoid(x):
    return lax.logistic(x)


def _silu(x):
    return x * _sigmoid(x)


def _rms(x, w=None, n=None):
    n = n or x.shape[-1]
    y = x * lax.rsqrt(jnp.sum(x * x, axis=-1, keepdims=True) * (1.0 / n) + EPS)
    return y if w is None else y * w


def _modulate(x, scale, shift):
    return _rms(x) * (1.0 + scale) + shift


def _softplus(x):
    return jnp.maximum(x, 0.0) + jnp.log1p(jnp.exp(-jnp.abs(x)))


@jax.custom_vjp
def _rot_half64(x):
    lane = lax.broadcasted_iota(jnp.int32, x.shape, 1)
    up = pltpu.roll(x, 96, 1)
    down = pltpu.roll(x, 32, 1)
    return jnp.where(lane < 32, up, jnp.where(lane < 64, down, 0.0))


_rot_half64.defvjp(lambda x: (_rot_half64(x), None), lambda _, g: (_rot_half64(g),))


def _rope128(x, cos_p, sin_p):
    return x * cos_p + _rot_half64(x) * sin_p


def _gdn_pre_fn(qkvc, kab, alog_p, dt_p):
    a = _silu(qkvc)
    qs, ks = [], []
    for h in range(HEADS):
        qh = a[:, HEAD_DIM * h:HEAD_DIM * (h + 1)]
        kh = a[:, 512 + HEAD_DIM * h:512 + HEAD_DIM * (h + 1)]
        qs.append(qh * lax.rsqrt(jnp.sum(qh * qh, axis=-1, keepdims=True) + EPS) * (HEAD_DIM ** -0.5))
        ks.append(kh * lax.rsqrt(jnp.sum(kh * kh, axis=-1, keepdims=True) + EPS))
    lane = lax.broadcasted_iota(jnp.int32, kab.shape, 1)
    g_full = -jnp.exp(alog_p) * _softplus(kab + dt_p)
    b_full = _sigmoid(kab)
    gb = jnp.where((lane >= 64) & (lane < 68), g_full, jnp.where((lane >= 68) & (lane < 72), b_full, 0.0))
    return jnp.concatenate(qs, axis=1), jnp.concatenate(ks, axis=1), a[:, 1024:1536], gb


def _intra_head(q, k, v, g_col, b_col):
    c = CHUNK
    row = lax.broadcasted_iota(jnp.int32, (c, c), 0)
    col = lax.broadcasted_iota(jnp.int32, (c, c), 1)
    incl, strict, eye = row >= col, row > col, row == col
    tri = jnp.where(incl, 1.0, 0.0).astype(F32)
    ident = jnp.where(eye, 1.0, 0.0).astype(F32)
    g_wide = _dot_hi(tri, jnp.broadcast_to(g_col, (c, HEAD_DIM)))
    g_i = g_wide[:, :c]
    g_j = jnp.sum(jnp.where(eye, g_i, 0.0), axis=0, keepdims=True)
    decay = jnp.where(incl, jnp.exp(jnp.where(incl, g_i - g_j, 0.0)), 0.0)
    kk = _bdot(k, k, "nt")
    a_mat = jnp.where(strict, b_col * kk * decay, 0.0)
    x_pow = -a_mat
    inv = ident + x_pow
    for _ in range(5):
        x_pow = _dot_hi(x_pow, x_pow)
        inv = inv + _dot_hi(inv, x_pow)
    e_wide = jnp.exp(g_wide)
    u = _dot_hi(inv, v * b_col)
    wk = _dot_hi(inv, k * b_col * e_wide)
    qk = _bdot(q, k, "nt") * decay
    last = lax.broadcasted_iota(jnp.int32, (c, HEAD_DIM), 0) == c - 1
    g_last = jnp.sum(jnp.where(last, g_wide, 0.0), axis=0, keepdims=True)
    qd = q * e_wide
    kd = k * jnp.exp(g_last - g_wide)
    gl = jnp.broadcast_to(jnp.exp(g_last), (8, HEAD_DIM))
    return u, wk, qd, kd, qk, gl


INTRA_ROWS = (256, 128, 64)

_BNN = (((2,), (1,)), ((0,), (0,)))
_BNT = (((2,), (2,)), ((0,), (0,)))


def _split_bf16(a):
    hi = a.astype(BF16)
    return hi, (a - hi.astype(F32)).astype(BF16)


def _dot3_raw(a, b, dims):
    a_hi, a_lo = _split_bf16(a)
    b_hi, b_lo = _split_bf16(b)
    dot = lambda x_, y_: lax.dot_general(x_, y_, dims, preferred_element_type=F32)
    return dot(a_hi, b_hi) + (dot(a_hi, b_lo) + dot(a_lo, b_hi))


@functools.partial(jax.custom_vjp, nondiff_argnums=(2, 3))
def _dot3(a, b, nt, exact_bwd=True):
    return _dot3_raw(a, b, _BNT if nt else _BNN)


def _dot3_fwd(a, b, nt, exact_bwd):
    return _dot3_raw(a, b, _BNT if nt else _BNN), (a, b)


def _dot3_bwd(nt, exact_bwd, res, g):
    a, b = res
    if exact_bwd:
        dot = _dot3_raw
    else:
        dot = lambda x_, y_, d_: lax.dot_general(x_.astype(BF16), y_.astype(BF16), d_, preferred_element_type=F32)
    if nt:
        return dot(g, b, _BNN), dot(jnp.swapaxes(g, 1, 2), a, _BNN)
    return dot(g, b, _BNT), dot(jnp.swapaxes(a, 1, 2), g, _BNN)


_dot3.defvjp(_dot3_fwd, _dot3_bwd)


@functools.partial(jax.custom_vjp, nondiff_argnums=(2,))
def _bdot_b(a, b, nt):
    return lax.dot_general(a.astype(BF16), b.astype(BF16), _BNT if nt else _BNN, preferred_element_type=F32)


def _bdot_b_fwd(a, b, nt):
    return _bdot_b(a, b, nt), (a, b)


def _bdot_b_bwd(nt, res, g):
    a, b = res
    dot = lambda x_, y_, d_: lax.dot_general(x_.astype(BF16), y_.astype(BF16), d_, preferred_element_type=F32)
    if nt:
        return dot(g, b, _BNN), dot(jnp.swapaxes(g, 1, 2), a, _BNN)
    return dot(g, b, _BNT), dot(jnp.swapaxes(a, 1, 2), g, _BNN)


_bdot_b.defvjp(_bdot_b_fwd, _bdot_b_bwd)


def _intra_batched(q, k, v, g_col, b_col):
    c = CHUNK
    nb = q.shape[0]
    row = lax.broadcasted_iota(jnp.int32, (1, c, c), 1)
    col = lax.broadcasted_iota(jnp.int32, (1, c, c), 2)
    incl, strict, eye = row >= col, row > col, row == col
    tri = jnp.broadcast_to(jnp.where(incl, 1.0, 0.0).astype(F32), (nb, c, c))
    ident = jnp.where(eye, 1.0, 0.0).astype(F32)
    g_wide = _dot3(tri, jnp.broadcast_to(g_col, (nb, c, HEAD_DIM)), False)
    g_i = g_wide[:, :, :c]
    g_j = jnp.sum(jnp.where(eye, g_i, 0.0), axis=1, keepdims=True)
    decay = jnp.where(incl, jnp.exp(jnp.where(incl, g_i - g_j, 0.0)), 0.0)
    kk = _bdot_b(k, k, True)
    a_mat = jnp.where(strict, b_col * kk * decay, 0.0)
    x_pow = -a_mat
    inv = ident + x_pow
    for _ in range(5):
        x_pow = _dot3(x_pow, x_pow, False, False)
        inv = inv + _dot3(inv, x_pow, False, False)
    e_wide = jnp.exp(g_wide)
    u = _dot3(inv, v * b_col, False)
    wk = _dot3(inv, k * b_col * e_wide, False)
    qk = _bdot_b(q, k, True) * decay
    last = lax.broadcasted_iota(jnp.int32, (1, c, HEAD_DIM), 1) == c - 1
    g_last = jnp.sum(jnp.where(last, g_wide, 0.0), axis=1, keepdims=True)
    qd = q * e_wide
    kd = k * jnp.exp(g_last - g_wide)
    gl = jnp.broadcast_to(jnp.exp(g_last), (nb, 8, HEAD_DIM))
    return u, wk, qd, kd, qk, gl


def _gdn_intra_fn(q, k, v, gb):
    t = q.shape[0]
    nch = t // CHUNK
    lane = lax.broadcasted_iota(jnp.int32, gb.shape, 1)

    def heads_first(x_):
        return jnp.concatenate([x_[:, HEAD_DIM * h:HEAD_DIM * (h + 1)].reshape(nch, CHUNK, HEAD_DIM) for h in range(HEADS)],
                               axis=0)

    def column(first_lane):
        return jnp.concatenate([jnp.sum(jnp.where(lane == first_lane + h, gb, 0.0), axis=1, keepdims=True)
                                .reshape(nch, CHUNK, 1) for h in range(HEADS)], axis=0)

    u, wk, qd, kd, qk, gl = _intra_batched(heads_first(q), heads_first(k), heads_first(v), column(64), column(68))

    def rows_first(x_):
        r, w_ = x_.shape[1], x_.shape[2]
        return jnp.concatenate([x_[nch * h:nch * (h + 1)].reshape(nch * r, w_) for h in range(HEADS)], axis=1)

    qks = [qk[nch * h:nch * (h + 1)].reshape(t, CHUNK) for h in range(HEADS)]
    return (rows_first(u), rows_first(wk), rows_first(qd), rows_first(kd), *qks, rows_first(gl))


def _scan_step(s0, u, wk, qd, kd, qk, gl):
    v_new = u - _bdot_b(wk, s0, False)
    o = _bdot_b(qd, s0, False) + _bdot_b(qk, v_new, False)
    s1 = s0 * gl[:, 0:1, :] + _bdot_b(jnp.swapaxes(kd, 1, 2), v_new, False)
    return o, s1


def _mix_post_fn(o_a, z, o_b, gnw, onw):
    parts = [_rms(o_a[:, HEAD_DIM * h:HEAD_DIM * (h + 1)], gnw) * _silu(z[:, HEAD_DIM * h:HEAD_DIM * (h + 1)])
             for h in range(HEADS)]
    parts += [_rms(o_b[:, HEAD_DIM * h:HEAD_DIM * (h + 1)], onw) for h in range(HEADS)]
    return (jnp.concatenate(parts, axis=1),)


def _mla_pre_fn(ckv, cq, kab, cos_p, sin_p, qnw, kvnw, wuq, wukv, qn_w, qr_w, kn_w, kr_w):
    scale = (HEAD_DIM + ROPE) ** -0.5
    qf = _bdot(_rms(cq, qnw), wuq, "nn")
    kvf = _bdot(_rms(ckv, kvnw), wukv, "nn")
    lane = lax.broadcasted_iota(jnp.int32, kab.shape, 1)
    kr = _rope128(_rms(jnp.where(lane < ROPE, kab, 0.0), kr_w, n=ROPE), cos_p, sin_p)
    qs, ks = [], []
    for h in range(HEADS):
        qn = _rms(qf[:, 256 * h:256 * h + 128], qn_w) * scale
        qr = _rope128(_rms(qf[:, 256 * h + 128:256 * h + 256], qr_w, n=ROPE), cos_p, sin_p) * scale
        qs += [qn, qr]
        ks += [_rms(kvf[:, 128 * h:128 * (h + 1)], kn_w), kr]
    return jnp.concatenate(qs, axis=1), jnp.concatenate(ks, axis=1), kvf[:, 512:]


def _conv_fwd(proj, conv_w, tm, name):
    S = proj.shape[0]
    C = 1536
    nb = tm // 8

    def body(x_ref, prev_ref, w_ref, o_ref, ext_ref):
        i = pl.program_id(0)
        ext_ref[0:8, :] = jnp.where(i > 0, prev_ref[...], 0.0)
        ext_ref[8:, :] = x_ref[...]
        acc = jnp.zeros((tm, C), F32)
        for k in range(4):
            acc = acc + w_ref[k:k + 1, :] * ext_ref[pl.ds(5 + k, tm), :]
        o_ref[...] = acc

    return pl.pallas_call(
        body, name=name, grid=(S // tm,),
        in_specs=[pl.BlockSpec((tm, C), lambda i: (i, 0)),
                  pl.BlockSpec((8, C), lambda i: (jnp.maximum(i * nb - 1, 0), 0)),
                  pl.BlockSpec((4, C), lambda i: (0, 0))],
        out_specs=pl.BlockSpec((tm, C), lambda i: (i, 0)),
        out_shape=jax.ShapeDtypeStruct((S, C), F32),
        scratch_shapes=[pltpu.VMEM((tm + 8, C), F32)],
        compiler_params=_params(("arbitrary",)),
    )(proj, proj, conv_w)


def _conv_bwd(proj, dout, conv_w, tm, name):
    S = proj.shape[0]
    C = 1536
    nb = tm // 8
    n_steps = S // tm

    def body(x_ref, prev_ref, d_ref, next_ref, w_ref, dx_ref, dw_ref, xext_ref, dext_ref):
        i = pl.program_id(0)
        xext_ref[0:8, :] = jnp.where(i > 0, prev_ref[...], 0.0)
        xext_ref[8:, :] = x_ref[...]
        dext_ref[0:tm, :] = d_ref[...]
        dext_ref[tm:, :] = jnp.where(i < n_steps - 1, next_ref[...], 0.0)
        d = d_ref[...]
        acc = jnp.zeros((tm, C), F32)
        dws = []
        for k in range(4):
            acc = acc + w_ref[k:k + 1, :] * dext_ref[pl.ds(3 - k, tm), :]
            dws.append(jnp.sum(d * xext_ref[pl.ds(5 + k, tm), :], axis=0, keepdims=True))
        dx_ref[...] = acc

        @pl.when(i == 0)
        def _():
            dw_ref[...] = jnp.zeros_like(dw_ref)

        dw_ref[...] += jnp.concatenate(dws + [jnp.zeros((4, C), F32)], axis=0)

    return pl.pallas_call(
        body, name=name, grid=(n_steps,),
        in_specs=[pl.BlockSpec((tm, C), lambda i: (i, 0)),
                  pl.BlockSpec((8, C), lambda i: (jnp.maximum(i * nb - 1, 0), 0)),
                  pl.BlockSpec((tm, C), lambda i: (i, 0)),
                  pl.BlockSpec((8, C), lambda i: (jnp.minimum((i + 1) * nb, S // 8 - 1), 0)),
                  pl.BlockSpec((4, C), lambda i: (0, 0))],
        out_specs=[pl.BlockSpec((tm, C), lambda i: (i, 0)), pl.BlockSpec((8, C), lambda i: (0, 0))],
        out_shape=[jax.ShapeDtypeStruct((S, C), F32), jax.ShapeDtypeStruct((8, C), F32)],
        scratch_shapes=[pltpu.VMEM((tm + 8, C), F32), pltpu.VMEM((tm + 8, C), F32)],
        compiler_params=_params(("arbitrary",)),
    )(proj, proj, dout, dout, conv_w)


def _gdn_scan_fwd(u, wk, qd, kd, qks, gl, name):
    S = u.shape[0]
    nc = S // CHUNK
    W = HEADS * HEAD_DIM

    def body(u_ref, wk_ref, qd_ref, kd_ref, qk0, qk1, qk2, qk3, gl_ref, o_ref, sp_ref, s_ref):
        @pl.when(pl.program_id(0) == 0)
        def _():
            s_ref[...] = jnp.zeros_like(s_ref)

        s0 = s_ref[...]
        sp_ref[0] = s0
        o, s1 = _scan_step(s0, _heads(u_ref, HEAD_DIM), _heads(wk_ref, HEAD_DIM), _heads(qd_ref, HEAD_DIM),
                           _heads(kd_ref, HEAD_DIM), jnp.stack([r[...] for r in (qk0, qk1, qk2, qk3)]),
                           _heads(gl_ref, HEAD_DIM))
        s_ref[...] = s1
        for h in range(HEADS):
            o_ref[:, HEAD_DIM * h:HEAD_DIM * (h + 1)] = o[h]

    row = pl.BlockSpec((CHUNK, W), lambda n: (n, 0))
    qk_spec = pl.BlockSpec((CHUNK, CHUNK), lambda n: (n, 0))
    return pl.pallas_call(
        body, name=name, grid=(nc,),
        in_specs=[row, row, row, row, qk_spec, qk_spec, qk_spec, qk_spec, pl.BlockSpec((8, W), lambda n: (n, 0))],
        out_specs=[row, pl.BlockSpec((1, HEADS, HEAD_DIM, HEAD_DIM), lambda n: (n, 0, 0, 0))],
        out_shape=[jax.ShapeDtypeStruct((S, W), F32), jax.ShapeDtypeStruct((nc, HEADS, HEAD_DIM, HEAD_DIM), F32)],
        scratch_shapes=[pltpu.VMEM((HEADS, HEAD_DIM, HEAD_DIM), F32)],
        compiler_params=_params(("arbitrary",)),
    )(u, wk, qd, kd, *qks, gl)


def _gdn_scan_bwd(u, wk, qd, kd, qks, gl, s_prev, d_o, name):
    S = u.shape[0]
    nc = S // CHUNK
    W = HEADS * HEAD_DIM

    def body(u_ref, wk_ref, qd_ref, kd_ref, qk0, qk1, qk2, qk3, gl_ref, sp_ref, do_ref,
             du_ref, dwk_ref, dqd_ref, dkd_ref, dqk0, dqk1, dqk2, dqk3, dgl_ref, ds_ref):
        @pl.when(pl.program_id(0) == 0)
        def _():
            ds_ref[...] = jnp.zeros_like(ds_ref)

        _, vjp = jax.vjp(_scan_step, sp_ref[0], _heads(u_ref, HEAD_DIM), _heads(wk_ref, HEAD_DIM), _heads(qd_ref, HEAD_DIM),
                         _heads(kd_ref, HEAD_DIM), jnp.stack([r[...] for r in (qk0, qk1, qk2, qk3)]),
                         _heads(gl_ref, HEAD_DIM))
        ds0, du, dwk, dqd, dkd, dqk, dgl = vjp((_heads(do_ref, HEAD_DIM), ds_ref[...]))
        ds_ref[...] = ds0
        for h, dqk_ref in enumerate((dqk0, dqk1, dqk2, dqk3)):
            sl = slice(HEAD_DIM * h, HEAD_DIM * (h + 1))
            du_ref[:, sl] = du[h]
            dwk_ref[:, sl] = dwk[h]
            dqd_ref[:, sl] = dqd[h]
            dkd_ref[:, sl] = dkd[h]
            dqk_ref[...] = dqk[h]
            dgl_ref[:, sl] = dgl[h]

    rev = lambda n: (nc - 1 - n, 0)
    row = pl.BlockSpec((CHUNK, W), rev)
    qk_spec = pl.BlockSpec((CHUNK, CHUNK), rev)
    gl_spec = pl.BlockSpec((8, W), rev)
    qk_shape = jax.ShapeDtypeStruct((S, CHUNK), F32)
    row_shape = jax.ShapeDtypeStruct((S, W), F32)
    return pl.pallas_call(
        body, name=name, grid=(nc,),
        in_specs=[row, row, row, row, qk_spec, qk_spec, qk_spec, qk_spec, gl_spec,
                  pl.BlockSpec((1, HEADS, HEAD_DIM, HEAD_DIM), lambda n: (nc - 1 - n, 0, 0, 0)), row],
        out_specs=[row, row, row, row, qk_spec, qk_spec, qk_spec, qk_spec, gl_spec],
        out_shape=[row_shape] * 4 + [qk_shape] * 4 + [jax.ShapeDtypeStruct((nc * 8, W), F32)],
        scratch_shapes=[pltpu.VMEM((HEADS, HEAD_DIM, HEAD_DIM), F32)],
        compiler_params=_params(("arbitrary",)),
    )(u, wk, qd, kd, *qks, gl, s_prev, d_o)


NEG = -1e30


def _chunk_mask(i, j, t, transposed=False):
    q_axis, k_axis = (1, 0) if transposed else (0, 1)
    r = (i * t + lax.broadcasted_iota(jnp.int32, (t, t), q_axis)) // CHUNK
    c = (j * t + lax.broadcasted_iota(jnp.int32, (t, t), k_axis)) // CHUNK
    return c <= r


def _heads(ref, width):
    return jnp.stack([ref[:, width * h:width * (h + 1)] for h in range(HEADS)])


def _bmm(a, b, dims):
    return lax.dot_general(a.astype(BF16), b.astype(BF16), dims, preferred_element_type=F32)


def _attn_fwd(q, k, v_t, t, name):
    S = q.shape[0]
    n = S // t

    def body(q_ref, k_ref, vt_ref, o_ref, lse_ref, m_ref, l_ref, acc_ref):
        i, j = pl.program_id(0), pl.program_id(1)

        @pl.when(j == 0)
        def _():
            m_ref[...] = jnp.full_like(m_ref, NEG)
            l_ref[...] = jnp.zeros_like(l_ref)
            acc_ref[...] = jnp.zeros_like(acc_ref)

        def update(masked):
            s_t = _bmm(_heads(k_ref, 256), _heads(q_ref, 256), _BNT)
            if masked:
                s_t = jnp.where(_chunk_mask(i, j, t, transposed=True)[None], s_t, NEG)
            m_old = m_ref[...]
            m_new = jnp.maximum(m_old, jnp.max(s_t, axis=1, keepdims=True))
            p_t = jnp.exp(s_t - m_new)
            alpha = jnp.exp(m_old - m_new)
            l_ref[...] = alpha * l_ref[...] + jnp.sum(p_t, axis=1, keepdims=True)
            v_heads = jnp.stack([vt_ref[HEAD_DIM * h:HEAD_DIM * (h + 1), :] for h in range(HEADS)])
            acc_ref[...] = alpha * acc_ref[...] + _bmm(v_heads, p_t, _BNN)
            m_ref[...] = m_new

        @pl.when(j < i)
        def _():
            update(False)

        @pl.when(j == i)
        def _():
            update(True)
            for h in range(HEADS):
                sl = slice(HEAD_DIM * h, HEAD_DIM * (h + 1))
                o_ref[:, sl] = jnp.transpose(acc_ref[h] / l_ref[h])
                lse_ref[:, sl] = jnp.transpose(jnp.broadcast_to(m_ref[h] + jnp.log(l_ref[h]), (HEAD_DIM, t)))

    row = lambda i, j: (i, 0)
    return pl.pallas_call(
        body, name=name, grid=(n, n),
        in_specs=[pl.BlockSpec((t, HEADS * 256), row), pl.BlockSpec((t, HEADS * 256), lambda i, j: (jnp.minimum(j, i), 0)),
                  pl.BlockSpec((HEADS * HEAD_DIM, t), lambda i, j: (0, jnp.minimum(j, i)))],
        out_specs=[pl.BlockSpec((t, HEADS * HEAD_DIM), row)] * 2,
        out_shape=[jax.ShapeDtypeStruct((S, HEADS * HEAD_DIM), F32)] * 2,
        scratch_shapes=[pltpu.VMEM((HEADS, 1, t), F32), pltpu.VMEM((HEADS, 1, t), F32), pltpu.VMEM((HEADS, HEAD_DIM, t), F32)],
        compiler_params=_params(("parallel", "arbitrary")),
    )(q, k, v_t)


def _attn_dq(q, k, v, o, lse, d_o, t, name):
    S = q.shape[0]
    n = S // t

    def body(q_ref, k_ref, v_ref, o_ref, lse_ref, do_ref, dq_ref, st_ref, acc_ref, delta_ref):
        i, j = pl.program_id(0), pl.program_id(1)

        @pl.when(j == 0)
        def _():
            acc_ref[...] = jnp.zeros_like(acc_ref)
            delta_ref[...] = jnp.sum(_heads(do_ref, HEAD_DIM) * _heads(o_ref, HEAD_DIM), axis=2, keepdims=True)

        def update(masked):
            kh = _heads(k_ref, 256)
            d_out = _heads(do_ref, HEAD_DIM)
            s = _bmm(_heads(q_ref, 256), kh, _BNT)
            p = jnp.exp(s - _heads(lse_ref, HEAD_DIM)[:, :, 0:1])
            if masked:
                p = jnp.where(_chunk_mask(i, j, t)[None], p, 0.0)
            dp = _bmm(d_out, _heads(v_ref, HEAD_DIM), _BNT)
            ds = p * (dp - delta_ref[...])
            acc_ref[...] += _bmm(ds, kh, _BNN)

        @pl.when(j < i)
        def _():
            update(False)

        @pl.when(j == i)
        def _():
            update(True)
            lane = lax.broadcasted_iota(jnp.int32, (t, HEAD_DIM), 1)
            stats = jnp.zeros((t, HEAD_DIM), F32)
            for h in range(HEADS):
                dq_ref[:, 256 * h:256 * (h + 1)] = acc_ref[h]
                stats = stats + jnp.where(lane == h, lse_ref[:, HEAD_DIM * h:HEAD_DIM * (h + 1)], 0.0)
                stats = stats + jnp.where(lane == HEADS + h, delta_ref[h], 0.0)
            st_ref[...] = jnp.transpose(stats)[0:8, :]

    kv = lambda i, j: (jnp.minimum(j, i), 0)
    row = lambda i, j: (i, 0)
    wide, narrow = pl.BlockSpec((t, HEADS * 256), row), pl.BlockSpec((t, HEADS * HEAD_DIM), row)
    return pl.pallas_call(
        body, name=name, grid=(n, n),
        in_specs=[wide, pl.BlockSpec((t, HEADS * 256), kv), pl.BlockSpec((t, HEADS * HEAD_DIM), kv), narrow, narrow, narrow],
        out_specs=[wide, pl.BlockSpec((8, t), lambda i, j: (0, i))],
        out_shape=[jax.ShapeDtypeStruct((S, HEADS * 256), F32), jax.ShapeDtypeStruct((8, S), F32)],
        scratch_shapes=[pltpu.VMEM((HEADS, t, 256), F32), pltpu.VMEM((HEADS, t, 1), F32)],
        compiler_params=_params(("parallel", "arbitrary")),
    )(q, k, v, o, lse, d_o)


def _attn_dkv(q, k, v, d_o, stats, t, name):
    S = q.shape[0]
    n = S // t

    def body(q_ref, k_ref, v_ref, do_ref, st_ref, dk_ref, dv_ref, dk_acc, dv_acc):
        j, step = pl.program_id(0), pl.program_id(1)
        i = j + step

        @pl.when(step == 0)
        def _():
            dk_acc[...] = jnp.zeros_like(dk_acc)
            dv_acc[...] = jnp.zeros_like(dv_acc)

        def update(masked):
            qh = _heads(q_ref, 256)
            d_out = _heads(do_ref, HEAD_DIM)
            st = st_ref[...]
            lse_row = jnp.stack([st[h:h + 1, :] for h in range(HEADS)])
            delta_row = jnp.stack([st[HEADS + h:HEADS + h + 1, :] for h in range(HEADS)])
            s_t = _bmm(_heads(k_ref, 256), qh, _BNT)
            p_t = jnp.exp(s_t - lse_row)
            if masked:
                p_t = jnp.where(_chunk_mask(i, j, t, transposed=True)[None], p_t, 0.0)
            dv_acc[...] += _bmm(p_t, d_out, _BNN)
            dp_t = _bmm(_heads(v_ref, HEAD_DIM), d_out, _BNT)
            ds_t = p_t * (dp_t - delta_row)
            dk_acc[...] += _bmm(ds_t, qh, _BNN)

        @pl.when(step == 0)
        def _():
            update(True)

        @pl.when((step > 0) & (i < n))
        def _():
            update(False)

        @pl.when(step == n - 1)
        def _():
            for h in range(HEADS):
                dk_ref[:, 256 * h:256 * (h + 1)] = dk_acc[h]
                dv_ref[:, HEAD_DIM * h:HEAD_DIM * (h + 1)] = dv_acc[h]

    qi = lambda j, s_: (jnp.minimum(j + s_, n - 1), 0)
    kj = lambda j, s_: (j, 0)
    return pl.pallas_call(
        body, name=name, grid=(n, n),
        in_specs=[pl.BlockSpec((t, HEADS * 256), qi), pl.BlockSpec((t, HEADS * 256), kj), pl.BlockSpec((t, HEADS * HEAD_DIM), kj),
                  pl.BlockSpec((t, HEADS * HEAD_DIM), qi), pl.BlockSpec((8, t), lambda j, s_: (0, jnp.minimum(j + s_, n - 1)))],
        out_specs=[pl.BlockSpec((t, HEADS * 256), kj), pl.BlockSpec((t, HEADS * HEAD_DIM), kj)],
        out_shape=[jax.ShapeDtypeStruct((S, HEADS * 256), F32), jax.ShapeDtypeStruct((S, HEADS * HEAD_DIM), F32)],
        scratch_shapes=[pltpu.VMEM((HEADS, t, 256), F32), pltpu.VMEM((HEADS, t, HEAD_DIM), F32)],
        compiler_params=_params(("parallel", "arbitrary")),
    )(q, k, v, d_o, stats)


FFN_PIECE = 2 * D_FF // N_DEV
HID_PIECES = D_FF // FFN_PIECE


def _ffn_up(h, w8, name):
    S = h.shape[0]
    tm = _pick(S, (512, 256, 128))

    def body(h_ref, wg_ref, wu_ref, g_ref, u_ref, hid_ref):
        gate = _dot_raw(h_ref[...], wg_ref[...], "nn")
        up = _dot_raw(h_ref[...], wu_ref[...], "nn")
        g_ref[...] = gate.astype(BF16)
        u_ref[...] = up.astype(BF16)
        hid_ref[...] = (_silu(gate) * up).astype(BF16)

    o_spec = pl.BlockSpec((None, tm, FFN_PIECE), lambda i, j: (j, i, 0))
    return pl.pallas_call(
        body, name=name, grid=(S // tm, HID_PIECES),
        in_specs=[pl.BlockSpec((tm, D_MODEL), lambda i, j: (i, 0)),
                  pl.BlockSpec((None, D_MODEL, FFN_PIECE), lambda i, j: (j, 0, 0)),
                  pl.BlockSpec((None, D_MODEL, FFN_PIECE), lambda i, j: (j + HID_PIECES, 0, 0))],
        out_specs=[o_spec] * 3,
        out_shape=[jax.ShapeDtypeStruct((HID_PIECES, S, FFN_PIECE), BF16)] * 3,
        compiler_params=_params(("parallel", "parallel")),
    )(h, w8, w8)


def _ffn_gw8(h, d_gate, d_up, name):
    S = h.shape[0]
    tm = 512
    tk = _pick(S, (512, 256, 128))
    nk = S // tk

    def body(h_ref, dg_ref, du_ref, o_ref, acc_ref):
        k = pl.program_id(1)

        @pl.when(k == 0)
        def _():
            acc_ref[...] = jnp.zeros_like(acc_ref)

        h_t = jnp.transpose(h_ref[...])
        for p in range(HID_PIECES):
            acc_ref[p] += _dot_raw(h_t, dg_ref[p], "nn")
            acc_ref[HID_PIECES + p] += _dot_raw(h_t, du_ref[p], "nn")

        @pl.when(k == nk - 1)
        def _():
            o_ref[...] = acc_ref[...].astype(o_ref.dtype)

    d_spec = pl.BlockSpec((HID_PIECES, tk, FFN_PIECE), lambda i, k: (0, k, 0))
    return pl.pallas_call(
        body, name=name, grid=(D_MODEL // tm, nk),
        in_specs=[pl.BlockSpec((tk, tm), lambda i, k: (k, i)), d_spec, d_spec],
        out_specs=pl.BlockSpec((2 * HID_PIECES, tm, FFN_PIECE), lambda i, k: (0, i, 0)),
        out_shape=jax.ShapeDtypeStruct((2 * HID_PIECES, D_MODEL, FFN_PIECE), BF16),
        scratch_shapes=[pltpu.VMEM((2 * HID_PIECES, tm, FFN_PIECE), F32)],
        compiler_params=_params(("parallel", "arbitrary")),
    )(h, d_gate, d_up)


def _ffn_dh(d_gate, d_up, w8, name):
    S = d_gate.shape[1]
    tm = _pick(S, (512, 256, 128))

    def body(dg_ref, du_ref, wg_ref, wu_ref, o_ref):
        @pl.when(pl.program_id(1) == 0)
        def _():
            o_ref[...] = jnp.zeros_like(o_ref)

        o_ref[...] += _dot_raw(dg_ref[...], wg_ref[...], "nt") + _dot_raw(du_ref[...], wu_ref[...], "nt")

    d_spec = pl.BlockSpec((None, tm, FFN_PIECE), lambda i, k: (k, i, 0))
    return pl.pallas_call(
        body, name=name, grid=(S // tm, HID_PIECES),
        in_specs=[d_spec, d_spec,
                  pl.BlockSpec((None, D_MODEL, FFN_PIECE), lambda i, k: (k, 0, 0)),
                  pl.BlockSpec((None, D_MODEL, FFN_PIECE), lambda i, k: (k + HID_PIECES, 0, 0))],
        out_specs=pl.BlockSpec((tm, D_MODEL), lambda i, k: (i, 0)),
        out_shape=jax.ShapeDtypeStruct((S, D_MODEL), F32),
        compiler_params=_params(("parallel", "arbitrary")),
    )(d_gate, d_up, w8, w8)


def _swiglu_bwd(d_hid, gate, up):
    sg = _sigmoid(gate)
    return d_hid * up * (sg * (1.0 + gate * (1.0 - sg))), d_hid * (gate * sg)


def _loss_and_grad(x3, target, name):
    S = x3.shape[0]
    tm = _pick(S, (512, 256, 128))
    n = S // tm

    def body(x_ref, t_ref, dx_ref, l_ref):
        i = pl.program_id(0)
        diff = x_ref[...] - t_ref[...]
        dx_ref[...] = diff * (1.0 / D_MODEL)

        @pl.when(i == 0)
        def _():
            l_ref[...] = jnp.zeros_like(l_ref)

        l_ref[...] += jnp.sum(diff * diff, axis=0, keepdims=True)

        @pl.when(i == n - 1)
        def _():
            l_ref[...] = jnp.full(l_ref.shape, (0.5 / D_MODEL) * jnp.sum(l_ref[...]), F32)

    return pl.pallas_call(
        body, name=name, grid=(n,),
        in_specs=[pl.BlockSpec((tm, D_MODEL), lambda i: (i, 0))] * 2,
        out_specs=[pl.BlockSpec((tm, D_MODEL), lambda i: (i, 0)), pl.BlockSpec((1, D_MODEL), lambda i: (0, 0))],
        out_shape=[jax.ShapeDtypeStruct((S, D_MODEL), F32), jax.ShapeDtypeStruct((1, D_MODEL), F32)],
        compiler_params=_params(("arbitrary",)),
    )(x3, target)


def _adamw(w, g, m, v, name):
    R, C = w.shape
    tr = _pick(R, (256, 176, 128, 64, 32, 16, 8))

    def body(w_ref, g_ref, m_ref, v_ref, d_ref, nm_ref, nv_ref):
        g_ = g_ref[...]
        m_ = ADAM_B1 * m_ref[...] + (1.0 - ADAM_B1) * g_
        v_ = ADAM_B2 * v_ref[...] + (1.0 - ADAM_B2) * (g_ * g_)
        m_hat = m_ / (1.0 - ADAM_B1 ** ADAM_STEP)
        v_hat = v_ / (1.0 - ADAM_B2 ** ADAM_STEP)
        d_ref[...] = -ADAM_LR * (m_hat / (jnp.sqrt(v_hat) + ADAM_EPS) + ADAM_WD * w_ref[...])
        nm_ref[...] = m_
        nv_ref[...] = v_

    spec = pl.BlockSpec((tr, C), lambda i: (i, 0))
    return pl.pallas_call(
        body, name=name, grid=(R // tr,),
        in_specs=[spec] * 4, out_specs=[spec] * 3,
        out_shape=[jax.ShapeDtypeStruct((R, C), F32)] * 3,
        compiler_params=_params(("parallel",)),
    )(w, g, m, v)


def _sum_devices(parts, name):
    _, R, C = parts.shape
    tr = _pick(R, (512, 256, 176, 128, 64, 32, 16, 8))

    def body(p_ref, o_ref):
        acc = p_ref[0].astype(F32)
        for d in range(1, N_DEV):
            acc = acc + p_ref[d].astype(F32)
        o_ref[...] = acc

    return pl.pallas_call(
        body, name=name, grid=(R // tr,),
        in_specs=[pl.BlockSpec((N_DEV, tr, C), lambda i: (0, i, 0))],
        out_specs=pl.BlockSpec((tr, C), lambda i: (i, 0)),
        out_shape=jax.ShapeDtypeStruct((R, C), F32),
        compiler_params=_params(("parallel",)),
    )(parts)


def _my_place():
    return lax.axis_index("x"), lax.axis_index("y"), lax.axis_index("c")


def _all_gather(blocks, name):
    n = len(blocks)

    def body(*refs):
        x_refs, out_refs = refs[:n], refs[n:2 * n]
        send_sems, recv_sems, local_sems = refs[2 * n:]
        x, y, c = _my_place()
        me, sibling = (x, y, c), (x, y, 1 - c)
        chips = [(1 - x, y), (x, 1 - y), (1 - x, 1 - y)]

        def copy(a, k, blk, to, own=False):
            slot = out_refs[a].at[4 * blk[0] + 2 * blk[1] + blk[2]]
            return pltpu.make_async_remote_copy(
                src_ref=x_refs[a] if own else slot, dst_ref=slot,
                send_sem=send_sems.at[7 * a + k], recv_sem=recv_sems.at[7 * a + k], device_id=to, device_id_type=MESH)

        mine = [pltpu.make_async_copy(x_refs[a], out_refs[a].at[4 * x + 2 * y + c], local_sems.at[a]) for a in range(n)]
        for cp in mine:
            cp.start()
        first = []
        for j, chip in enumerate(chips):
            first += [copy(a, 1 + j, me, (*chip, c), own=True) for a in range(n)]
        first += [copy(a, 0, me, sibling, own=True) for a in range(n)]
        for cp in first:
            cp.start()
        passed = []
        for j, chip in enumerate(chips):
            for a in range(n):
                copy(a, 1 + j, (*chip, c), me).wait_recv()
                passed.append(copy(a, 4 + j, (*chip, c), sibling))
                passed[-1].start()
        for a in range(n):
            copy(a, 0, sibling, me).wait_recv()
        for j, chip in enumerate(chips):
            for a in range(n):
                copy(a, 4 + j, (*chip, 1 - c), me).wait_recv()
        for cp in first + passed:
            cp.wait_send()
        for cp in mine:
            cp.wait()

    return pl.pallas_call(
        body, name=name,
        out_shape=[jax.ShapeDtypeStruct((N_DEV,) + b.shape, b.dtype) for b in blocks],
        in_specs=[pl.BlockSpec(memory_space=pl.ANY)] * n,
        out_specs=[pl.BlockSpec(memory_space=pl.ANY)] * n,
        scratch_shapes=[pltpu.SemaphoreType.DMA((7 * n,)), pltpu.SemaphoreType.DMA((7 * n,)), pltpu.SemaphoreType.DMA((n,))],
    )(*blocks)


def _all_to_all(pieces, name):
    n = len(pieces)

    def body(*refs):
        x_refs, out_refs = refs[:n], refs[n:2 * n]
        send_sems, recv_sems, local_sems = refs[2 * n:]
        x, y, c = _my_place()
        me = 4 * x + 2 * y + c
        mine = [pltpu.make_async_copy(x_refs[a].at[me], out_refs[a].at[me], local_sems.at[a]) for a in range(n)]
        for cp in mine:
            cp.start()
        copies = []
        for k in (2, 4, 6, 3, 5, 7, 1):
            px = 1 - x if k & 4 else x
            py = 1 - y if k & 2 else y
            pc = 1 - c if k & 1 else c
            peer = 4 * px + 2 * py + pc
            for a in range(n):
                copies.append(pltpu.make_async_remote_copy(
                    src_ref=x_refs[a].at[peer], dst_ref=out_refs[a].at[me],
                    send_sem=send_sems.at[7 * a + k - 1], recv_sem=recv_sems.at[7 * a + k - 1],
                    device_id=(px, py, pc), device_id_type=MESH))
        for cp in copies:
            cp.start()
        for cp in copies:
            cp.wait_recv()
        for cp in copies:
            cp.wait_send()
        for cp in mine:
            cp.wait()

    return pl.pallas_call(
        body, name=name,
        out_shape=[jax.ShapeDtypeStruct(p.shape, p.dtype) for p in pieces],
        in_specs=[pl.BlockSpec(memory_space=pl.ANY)] * n,
        out_specs=[pl.BlockSpec(memory_space=pl.ANY)] * n,
        scratch_shapes=[pltpu.SemaphoreType.DMA((7 * n,)), pltpu.SemaphoreType.DMA((7 * n,)), pltpu.SemaphoreType.DMA((n,))],
    )(*pieces)


def _peers():
    x, y, c = _my_place()
    out = []
    for k in (2, 4, 6, 3, 5, 7, 1):
        px = 1 - x if k & 4 else x
        py = 1 - y if k & 2 else y
        pc = 1 - c if k & 1 else c
        out.append((k, (px, py, pc), 4 * px + 2 * py + pc))
    return out


def _exchange_copies(x_refs, land_refs, send_sems, recv_sems, scatter):
    x, y, c = _my_place()
    me = 4 * x + 2 * y + c
    starts, arrivals = [], []
    for k, place, peer in _peers():
        for a, (x_ref, land_ref) in enumerate(zip(x_refs, land_refs)):
            sems = dict(send_sem=send_sems.at[7 * a + k - 1], recv_sem=recv_sems.at[7 * a + k - 1],
                        device_id=place, device_id_type=MESH)
            src = x_ref.at[peer] if scatter else x_ref
            starts.append(pltpu.make_async_remote_copy(src_ref=src, dst_ref=land_ref.at[me], **sems))
            arrivals.append(pltpu.make_async_remote_copy(src_ref=src, dst_ref=land_ref.at[peer], **sems))
    return starts, arrivals


def _exchange_start(arrays, scatter, name):
    n = len(arrays)
    hbm = pl.BlockSpec(memory_space=pltpu.HBM)
    sem = pl.BlockSpec(memory_space=pltpu.SEMAPHORE)
    lands = [lax.empty(a.shape if scatter else (N_DEV,) + a.shape, a.dtype) for a in arrays]

    def body(*refs):
        x_refs, land_refs = refs[:n], refs[n:2 * n]
        send_sems, recv_sems = refs[2 * n], refs[2 * n + 1]
        token = refs[-1]
        starts, _ = _exchange_copies(x_refs, land_refs, send_sems, recv_sems, scatter)
        for cp in starts:
            cp.start()
        token[...] = jnp.zeros_like(token)

    res = pl.pallas_call(
        body, name=name,
        out_shape=(pltpu.SemaphoreType.DMA((7 * n,)), pltpu.SemaphoreType.DMA((7 * n,)),
                   *[pltpu.HBM(a.shape, a.dtype) for a in arrays], *[pltpu.HBM(l.shape, l.dtype) for l in lands],
                   jax.ShapeDtypeStruct((8, 128), F32)),
        in_specs=[hbm] * (2 * n),
        out_specs=(sem, sem, *[hbm] * (2 * n), pl.BlockSpec(memory_space=pltpu.VMEM)),
        input_output_aliases={i: 2 + i for i in range(2 * n)},
        compiler_params=pltpu.CompilerParams(has_side_effects=pltpu.SideEffectType.DATAFLOW_SIDE_EFFECTING),
    )(*[pltpu.with_memory_space_constraint(a, pltpu.HBM) for a in arrays],
      *[pltpu.with_memory_space_constraint(l, pltpu.HBM) for l in lands])
    return res[0], res[1], list(res[2:2 + n]), list(res[2 + n:2 + 2 * n]), res[-1]


def _exchange_wait(handles, scatter, after, name):
    send_sems, recv_sems, arrays, lands, _ = handles
    n = len(arrays)
    hbm = pl.BlockSpec(memory_space=pltpu.HBM)
    sem = pl.BlockSpec(memory_space=pltpu.SEMAPHORE)

    def body(*refs):
        x_refs, land_refs = refs[:n], refs[n:2 * n]
        send_s, recv_s = refs[2 * n], refs[2 * n + 1]
        starts, arrivals = _exchange_copies(x_refs, land_refs, send_s, recv_s, scatter)
        for cp in arrivals:
            cp.wait_recv()
        for cp in starts:
            cp.wait_send()

    res = pl.pallas_call(
        body, name=name,
        out_shape=(*[pltpu.HBM(a.shape, a.dtype) for a in arrays], *[pltpu.HBM(l.shape, l.dtype) for l in lands]),
        in_specs=[hbm] * (2 * n) + [sem, sem, pl.BlockSpec(memory_space=pl.ANY)],
        out_specs=tuple([hbm] * (2 * n)),
        input_output_aliases={i: i for i in range(2 * n)},
        compiler_params=pltpu.CompilerParams(has_side_effects=pltpu.SideEffectType.DATAFLOW_SIDE_EFFECTING),
    )(*arrays, *lands, send_sems, recv_sems, after)
    me = 4 * lax.axis_index("x") + 2 * lax.axis_index("y") + lax.axis_index("c")
    out = []
    for src, got in zip(res[:n], res[n:]):
        zeros = (0,) * (got.ndim - 1)
        own = lax.dynamic_slice(src, (me,) + zeros, (1,) + src.shape[1:]) if scatter else src[None]
        out.append(lax.dynamic_update_slice(got, own, (me,) + zeros))
    return out


def _pad_lanes(v, at=0, width=128):
    return jnp.pad(v, ((0, 0), (at, width - at - v.shape[1])))


def _pack_weights(P):
    W = {}
    w = P["w_in"]
    W["wp"] = jnp.concatenate([w[:, :2048], w[:, 2440:2696], w[:, 2056:2440], w[:, 2696:2760], w[:, 2048:2056],
                               jnp.zeros((D_MODEL, N_IN_PACKED - N_IN), w.dtype)], axis=1).astype(BF16)
    W["conv_w"] = P["gdn_conv_w"].astype(F32)
    W["alog_p"] = _pad_lanes(P["gdn_a_log"], 64)
    W["dt_p"] = _pad_lanes(P["gdn_dt_bias"], 64)
    W["gnw"] = P["gdn_norm_w"]
    W["qnw"] = P["mla_q_norm_w"]
    W["kvnw"] = P["mla_kv_norm_w"]
    uq = P["mla_w_uq"].reshape(Q_LORA, HEADS, HEAD_DIM + ROPE)
    W["wuq"] = jnp.pad(uq, ((0, 0), (0, 0), (0, 256 - HEAD_DIM - ROPE))).reshape(Q_LORA, HEADS * 256).astype(BF16)
    ukv = P["mla_w_ukv"].reshape(KV_LORA, HEADS, 2, HEAD_DIM)
    W["wukv"] = ukv.transpose(0, 2, 1, 3).reshape(KV_LORA, 2 * HEADS * HEAD_DIM).astype(BF16)
    W["qn_w"] = P["qkn_q_nope"]
    W["qr_w"] = _pad_lanes(P["qkn_q_rope"])
    W["kn_w"] = P["qkn_k_nope"]
    W["kr_w"] = _pad_lanes(P["qkn_k_rope"])
    W["onw"] = P["mla_out_norm_w"]
    W["wout"] = P["w_out"].astype(BF16)
    return W


def _unpack_grads(G):
    g = G["wp"]
    uq = G["wuq"].reshape(Q_LORA, HEADS, 256)[:, :, :HEAD_DIM + ROPE].reshape(Q_LORA, HEADS * (HEAD_DIM + ROPE))
    ukv = G["wukv"].reshape(KV_LORA, 2, HEADS, HEAD_DIM).transpose(0, 2, 1, 3).reshape(KV_LORA, 2 * HEADS * HEAD_DIM)
    return {
        "w_in": jnp.concatenate([g[:, :2048], g[:, 2752:2760], g[:, 2304:2688], g[:, 2048:2304], g[:, 2688:2752]], axis=1),
        "gdn_conv_w": G["conv_w"], "gdn_a_log": G["alog_p"][:, 64:68], "gdn_dt_bias": G["dt_p"][:, 64:68],
        "gdn_norm_w": G["gnw"], "mla_q_norm_w": G["qnw"], "mla_w_uq": uq, "mla_kv_norm_w": G["kvnw"], "mla_w_ukv": ukv,
        "qkn_q_nope": G["qn_w"], "qkn_q_rope": G["qr_w"][:, :ROPE], "qkn_k_nope": G["kn_w"], "qkn_k_rope": G["kr_w"][:, :ROPE],
        "mla_out_norm_w": G["onw"], "w_out": G["wout"],
    }


def _rope_tables(positions):
    half = ROPE // 2
    inv_freq = ROPE_BASE ** (-jnp.arange(half, dtype=F32) / half)
    ang = positions.astype(F32)[:, None] * inv_freq
    cos, sin = jnp.cos(ang), jnp.sin(ang)
    zeros = jnp.zeros((positions.shape[0], 128 - ROPE), F32)
    return jnp.concatenate([cos, cos, zeros], axis=1), jnp.concatenate([-sin, sin, zeros], axis=1)


def _mod_fn(x, scale, shift):
    return (_modulate(x, scale, shift),)


def _ffn_fwd(x, scale, shift, gate_w, w8, wo4, tag):
    S = x.shape[0]
    tm = _pick(S, (512, 256, 128))
    (h,) = _rowwise(_mod_fn, [(x, tm, D_MODEL, 0)], [scale, shift], [(tm, D_MODEL, BF16)], S // tm, tag + "_mod")
    gate, up, hid = _ffn_up(h, w8, tag + "_up")
    mn = pl.BlockSpec((tm, D_MODEL), lambda i, j, k: (i, j))
    f, x_out = _mmg(hid, wo4, "nn", name=tag + "_down", grid=(S // tm, 1, HID_PIECES),
                    a_spec=pl.BlockSpec((None, tm, FFN_PIECE), lambda i, j, k: (k, i, 0)),
                    b_spec=pl.BlockSpec((None, FFN_PIECE, D_MODEL), lambda i, j, k: (k, 0, j)),
                    out_spec=mn, out_shapes=[jax.ShapeDtypeStruct((S, D_MODEL), F32)] * 2, acc_shape=(tm, D_MODEL),
                    extras=[x, gate_w], extra_specs=[mn, pl.BlockSpec((1, D_MODEL), lambda i, j, k: (0, j))],
                    epi=lambda acc, x_, g_: (acc, x_ + 0.5 * g_ * acc))
    return x_out, (h, gate, up, hid, f)


def _ffn_bwd(d_out, x, scale, shift, gate_w, w8, wo4, saved, tag):
    h, gate, up, hid, f = saved
    S = x.shape[0]
    tm = _pick(S, (512, 256, 128))
    tk = _pick(S, (512, 256, 128))
    n = S // tm
    (df,), (d_gate_w,) = _rowwise_bwd(lambda f_, g_: (0.5 * g_ * f_,), [(f, tm, D_MODEL, 0)], [], [gate_w],
                                      [(d_out, tm, D_MODEL, 0)], n, tag + "_dres", row_dtypes=(BF16,))
    piece = pl.BlockSpec((None, tm, FFN_PIECE), lambda i, j, k: (j, i, 0))
    d_gate, d_up = _mmg(df, wo4, "nt", name=tag + "_ddown", grid=(n, HID_PIECES, 1),
                        a_spec=pl.BlockSpec((tm, D_MODEL), lambda i, j, k: (i, 0)),
                        b_spec=pl.BlockSpec((None, FFN_PIECE, D_MODEL), lambda i, j, k: (j, 0, 0)),
                        out_spec=piece, out_shapes=[jax.ShapeDtypeStruct((HID_PIECES, S, FFN_PIECE), BF16)] * 2,
                        acc_shape=(tm, FFN_PIECE), extras=[gate, up], extra_specs=[piece, piece], epi=_swiglu_bwd)
    g_wo4 = _mmg(hid, df, "tn", name=tag + "_gwo", grid=(HID_PIECES, 1, S // tk),
                 a_spec=pl.BlockSpec((None, tk, FFN_PIECE), lambda i, j, k: (i, k, 0)),
                 b_spec=pl.BlockSpec((tk, D_MODEL), lambda i, j, k: (k, j)),
                 out_spec=pl.BlockSpec((None, FFN_PIECE, D_MODEL), lambda i, j, k: (i, 0, j)),
                 out_shapes=[jax.ShapeDtypeStruct((HID_PIECES, FFN_PIECE, D_MODEL), BF16)], acc_shape=(FFN_PIECE, D_MODEL))
    g_w8 = _ffn_gw8(h, d_gate, d_up, tag + "_gw8")
    dh = _ffn_dh(d_gate, d_up, w8, tag + "_dh")
    (dx,), (d_scale, d_shift) = _rowwise_bwd(_mod_fn, [(x, tm, D_MODEL, 0)], [], [scale, shift], [(dh, tm, D_MODEL, 0)],
                                             n, tag + "_dmod", adds=[(0, d_out)])
    return dx, d_scale, d_shift, d_gate_w, g_w8, g_wo4


def _mixer_fwd(x1, scale, shift, gate_w, cos_p, sin_p, W):
    S = x1.shape[0]
    tm = _pick(S, (512, 256, 128))
    tv = _pick(S, (256, 128))
    ta = _pick(S, (512, 256, 128))
    nc = S // CHUNK
    (h2,) = _rowwise(_mod_fn, [(x1, tm, D_MODEL, 0)], [scale, shift], [(tm, D_MODEL, BF16)], S // tm, "mix_mod")
    proj = _mm(h2, W["wp"], "nn", name="mix_proj")
    qkvc = _conv_fwd(proj, W["conv_w"], tv, "gdn_conv")
    kab = (proj, tv, 128, 21)
    q_a, k_a, v_a, gb = _rowwise(_gdn_pre_fn, [(qkvc, tv, 1536, 0), kab], [W["alog_p"], W["dt_p"]],
                                 [(tv, 512, F32)] * 3 + [(tv, 128, F32)], S // tv, "gdn_pre")
    ti = _pick(S, INTRA_ROWS)
    intra = _rowwise(_gdn_intra_fn, [(q_a, ti, 512, 0), (k_a, ti, 512, 0), (v_a, ti, 512, 0), (gb, ti, 128, 0)],
                     [], [(ti, 512, F32)] * 4 + [(ti, CHUNK, F32)] * 4 + [(ti // 8, 512, F32)], S // ti, "gdn_intra")
    u, wk, qd, kd, qks, gl = intra[0], intra[1], intra[2], intra[3], tuple(intra[4:8]), intra[8]
    o_a, s_prev = _gdn_scan_fwd(u, wk, qd, kd, qks, gl, "gdn_scan")
    mla_params = [W["qnw"], W["kvnw"], W["wuq"], W["wukv"], W["qn_w"], W["qr_w"], W["kn_w"], W["kr_w"]]
    def mla_pre_with_vt(*a):
        q_, k_, v_ = _mla_pre_fn(*a)
        return q_, k_, v_, jnp.transpose(v_)

    q_b, k_b, v_b, vt_b = _rowwise(mla_pre_with_vt,
                                   [(proj, tv, 256, 8), (proj, tv, 384, 6), kab, (cos_p, tv, 128, 0), (sin_p, tv, 128, 0)],
                                   mla_params, [(tv, 1024, BF16), (tv, 1024, BF16), (tv, 512, BF16), (512, tv, BF16, "across")],
                                   S // tv, "mla_pre")
    o_b, lse = _attn_fwd(q_b, k_b, vt_b, ta, "mla_attn")
    (mixed,) = _rowwise(_mix_post_fn, [(o_a, tv, 512, 0), (proj, tv, 512, 3), (o_b, tv, 512, 0)], [W["gnw"], W["onw"]],
                        [(tv, D_MODEL, BF16)], S // tv, "mix_post")
    y, x2 = _mm(mixed, W["wout"], "nn", name="mix_out", out_dtypes=(F32, F32), extras=[x1], extra_params=[gate_w],
                epi=lambda acc, x_, g_: (acc, x_ + g_ * acc))
    saved = (h2, proj, qkvc, q_a, k_a, v_a, gb, u, wk, qd, kd, qks, gl, s_prev, o_a, q_b, k_b, v_b, o_b, lse, mixed, y)
    return x2, saved


def _mixer_bwd(d_out, x1, scale, shift, gate_w, cos_p, sin_p, W, saved):
    (h2, proj, qkvc, q_a, k_a, v_a, gb, u, wk, qd, kd, qks, gl, s_prev, o_a, q_b, k_b, v_b, o_b, lse, mixed, y) = saved
    S = x1.shape[0]
    tm = _pick(S, (512, 256, 128))
    tv = _pick(S, (256, 128))
    ta = _pick(S, (512, 256, 128))
    nc = S // CHUNK
    G = {}
    (dy,), (G["g2"],) = _rowwise_bwd(lambda y_, g_: (g_ * y_,), [(y, tm, D_MODEL, 0)], [], [gate_w],
                                     [(d_out, tm, D_MODEL, 0)], S // tm, "mix_dres", row_dtypes=(BF16,))
    d_mixed = _mm(dy, W["wout"], "nt", name="mix_dout")
    G["wout"] = _mm(mixed, dy, "tn", name="mix_gwout")
    (do_a, dz, do_b), (G["gnw"], G["onw"]) = _rowwise_bwd(
        _mix_post_fn, [(o_a, tv, 512, 0), (proj, tv, 512, 3), (o_b, tv, 512, 0)], [], [W["gnw"], W["onw"]],
        [(d_mixed, tv, D_MODEL, 0)], S // tv, "mix_dpost")
    dq_b, stats = _attn_dq(q_b, k_b, v_b, o_b, lse, do_b, ta, "mla_dq")
    dk_b, dv_b = _attn_dkv(q_b, k_b, v_b, do_b, stats, ta, "mla_dkv")
    kab = (proj, tv, 128, 21)
    mla_params = [W["qnw"], W["kvnw"], W["wuq"], W["wukv"], W["qn_w"], W["qr_w"], W["kn_w"], W["kr_w"]]
    (d_ckv, d_cq, d_kab), mla_grads = _rowwise_bwd(
        _mla_pre_fn, [(proj, tv, 256, 8), (proj, tv, 384, 6), kab], [(cos_p, tv, 128, 0), (sin_p, tv, 128, 0)], mla_params,
        [(dq_b, tv, 1024, 0), (dk_b, tv, 1024, 0), (dv_b, tv, 512, 0)], S // tv, "mla_dpre")
    for key, g in zip(("qnw", "kvnw", "wuq", "wukv", "qn_w", "qr_w", "kn_w", "kr_w"), mla_grads):
        G[key] = g
    scan_grads = _gdn_scan_bwd(u, wk, qd, kd, qks, gl, s_prev, do_a, "gdn_dscan")
    ti = _pick(S, INTRA_ROWS)
    intra_douts = [(scan_grads[i], ti, 512, 0) for i in range(4)] + [(scan_grads[4 + i], ti, CHUNK, 0) for i in range(4)]
    intra_douts.append((scan_grads[8], ti // 8, 512, 0))
    (dq_a, dk_a, dv_a, d_gb), _ = _rowwise_bwd(
        _gdn_intra_fn, [(q_a, ti, 512, 0), (k_a, ti, 512, 0), (v_a, ti, 512, 0), (gb, ti, 128, 0)], [], [],
        intra_douts, S // ti, "gdn_dintra")
    (d_qkvc, d_kab), (G["alog_p"], G["dt_p"]) = _rowwise_bwd(
        _gdn_pre_fn, [(qkvc, tv, 1536, 0), kab], [], [W["alog_p"], W["dt_p"]],
        [(dq_a, tv, 512, 0), (dk_a, tv, 512, 0), (dv_a, tv, 512, 0), (d_gb, tv, 128, 0)], S // tv, "gdn_dpre",
        adds=[(1, d_kab)])
    d_qkv, g_conv = _conv_bwd(proj, d_qkvc, W["conv_w"], tv, "gdn_dconv")
    G["conv_w"] = g_conv[:4]
    d_proj = jnp.concatenate([d_qkv, dz, d_ckv, d_cq, d_kab], axis=1).astype(BF16)
    G["wp"] = _mm(h2, d_proj, "tn", name="mix_gwp")
    dh2 = _mm(d_proj, W["wp"], "nt", name="mix_dproj")
    (dx1,), (G["s2"], G["sh2"]) = _rowwise_bwd(_mod_fn, [(x1, tm, D_MODEL, 0)], [], [scale, shift],
                                               [(dh2, tm, D_MODEL, 0)], S // tm, "mix_dmod", adds=[(0, d_out)])
    return dx1, G


def _local_step(x, target, mod, cos_p, sin_p, W1, later_weights, ffn2_grads_ready, mixer_grads_ready):
    sh1, s1, g1, sh2, s2, g2, sh3, s3, g3 = [mod[:, D_MODEL * i:D_MODEL * (i + 1)] for i in range(N_MOD)]
    x1, saved1 = _ffn_fwd(x, s1, sh1, g1, W1["f1_w8"], W1["f1_wo4"], "ffn1")
    W = later_weights(x1)
    x2, saved2 = _mixer_fwd(x1, s2, sh2, g2, cos_p, sin_p, W)
    x3, saved3 = _ffn_fwd(x2, s3, sh3, g3, W["f2_w8"], W["f2_wo4"], "ffn2")
    dx3, loss_row = _loss_and_grad(x3, target, "loss")
    dx2, d_s3, d_sh3, d_g3, g_f2_w8, g_f2_wo4 = _ffn_bwd(dx3, x2, s3, sh3, g3, W["f2_w8"], W["f2_wo4"], saved3, "ffn2")
    g2 = g2 + ffn2_grads_ready(g_f2_w8, g_f2_wo4)
    dx1, G = _mixer_bwd(dx2, x1, s2, sh2, g2, cos_p, sin_p, W, saved2)
    d_sh2, d_s2, d_g2 = G.pop("sh2"), G.pop("s2"), G.pop("g2")
    g1 = g1 + mixer_grads_ready(G)
    dx, d_s1, d_sh1, d_g1, g_f1_w8, g_f1_wo4 = _ffn_bwd(dx1, x, s1, sh1, g1, W1["f1_w8"], W1["f1_wo4"], saved1, "ffn1")
    d_mod = jnp.concatenate([d_sh1, d_s1, d_g1, d_sh2, d_s2, d_g2, d_sh3, d_s3, d_g3], axis=1)
    return loss_row, dx, d_mod, g_f1_w8, g_f1_wo4


WEIGHT_NAMES = ("w_ada", "b_ada", "ffn1_w_in", "ffn1_w_out", "w_in", "gdn_conv_w", "gdn_a_log", "gdn_dt_bias", "gdn_norm_w",
                "mla_q_norm_w", "mla_w_uq", "mla_kv_norm_w", "mla_w_ukv", "qkn_q_nope", "qkn_q_rope", "qkn_k_nope",
                "qkn_k_rope", "mla_out_norm_w", "w_out", "ffn2_w_in", "ffn2_w_out")
FFN_SHARDED = ("ffn1_w_in", "ffn1_w_out", "ffn2_w_in", "ffn2_w_out")
SHEETED = (("w_in", "col"), ("gdn_conv_w", "col"), ("mla_w_uq", "col"), ("mla_w_ukv", "col"), ("w_out", "row"))
MOD_ROWS = N_MOD * D_MODEL // 128
SMALL = {"gdn_a_log": (MOD_ROWS, 1, 64, 4), "gdn_dt_bias": (MOD_ROWS + 1, 1, 64, 4), "gdn_norm_w": (MOD_ROWS + 2, 1, 0, 128),
         "mla_q_norm_w": (MOD_ROWS + 3, 3, 0, 384), "mla_kv_norm_w": (MOD_ROWS + 6, 2, 0, 256),
         "qkn_q_nope": (MOD_ROWS + 8, 1, 0, 128), "qkn_q_rope": (MOD_ROWS + 9, 1, 0, 64), "qkn_k_nope": (MOD_ROWS + 10, 1, 0, 128),
         "qkn_k_rope": (MOD_ROWS + 11, 1, 0, 64), "mla_out_norm_w": (MOD_ROWS + 12, 1, 0, 128)}
LOSS_ROW = MOD_ROWS + 13
SHEET_ROWS = 88


def _to_sheet(flat, dtype, sublanes):
    n = flat.shape[-1]
    unit = sublanes * 128
    pad = (-n) % unit
    flat = jnp.pad(flat.astype(dtype), [(0, 0)] * (flat.ndim - 1) + [(0, pad)])
    return flat.reshape(flat.shape[:-1] + ((n + pad) // 128, 128))


def _small_sheet(b_like, small):
    sheet = jnp.zeros((SHEET_ROWS, 128), F32).at[:MOD_ROWS].set(b_like.reshape(MOD_ROWS, 128))
    for name, (row, rows, lane, n) in SMALL.items():
        v = small[name].reshape(1, n)
        if rows == 1:
            sheet = sheet.at[row, lane:lane + n].set(v[0])
        else:
            sheet = sheet.at[row:row + rows].set(v.reshape(rows, 128))
    return sheet


def _from_small_sheet(sheet):
    out = {"b_ada": sheet[:MOD_ROWS].reshape(1, N_MOD * D_MODEL)}
    for name, (row, rows, lane, n) in SMALL.items():
        out[name] = sheet[row, lane:lane + n].reshape(1, n) if rows == 1 else sheet[row:row + rows].reshape(1, n)
    return out


def kernel(x, c, positions, w_ada, b_ada, ffn1_w_in, ffn1_w_out, w_in, gdn_conv_w, gdn_a_log, gdn_dt_bias, gdn_norm_w, mla_q_norm_w, mla_w_uq, mla_kv_norm_w, mla_w_ukv, qkn_q_nope, qkn_q_rope, qkn_k_nope, qkn_k_rope, mla_out_norm_w, w_out, ffn2_w_in, ffn2_w_out, loss_target, m_w_ada, m_b_ada, m_ffn1_w_in, m_ffn1_w_out, m_w_in, m_gdn_conv_w, m_gdn_a_log, m_gdn_dt_bias, m_gdn_norm_w, m_mla_q_norm_w, m_mla_w_uq, m_mla_kv_norm_w, m_mla_w_ukv, m_qkn_q_nope, m_qkn_q_rope, m_qkn_k_nope, m_qkn_k_rope, m_mla_out_norm_w, m_w_out, m_ffn2_w_in, m_ffn2_w_out, v_w_ada, v_b_ada, v_ffn1_w_in, v_ffn1_w_out, v_w_in, v_gdn_conv_w, v_gdn_a_log, v_gdn_dt_bias, v_gdn_norm_w, v_mla_q_norm_w, v_mla_w_uq, v_mla_kv_norm_w, v_mla_w_ukv, v_qkn_q_nope, v_qkn_q_rope, v_qkn_k_nope, v_qkn_k_rope, v_mla_out_norm_w, v_w_out, v_ffn2_w_in, v_ffn2_w_out):
    args = locals()
    w = {n: args[n] for n in WEIGHT_NAMES}
    m = {n: args["m_" + n] for n in WEIGHT_NAMES}
    v = {n: args["v_" + n] for n in WEIGHT_NAMES}
    me = 4 * lax.axis_index("x") + 2 * lax.axis_index("y") + lax.axis_index("c")
    cols = N_MOD * D_MODEL // N_DEV
    shard = {n: w[n][0] for n in FFN_SHARDED + tuple(s[0] for s in SHEETED)}

    sc = c * _sigmoid(c)
    first = _to_sheet(jnp.concatenate([sc.reshape(-1), shard["gdn_conv_w"].reshape(-1)]), F32, 8)
    (first_all,) = _all_gather([first], "gather_c")
    sc_all = first_all[:, :D_MODEL // 128].reshape(N_DEV, D_MODEL)
    n_taps = shard["gdn_conv_w"].size
    conv_all = first_all.reshape(N_DEV, -1)[:, D_MODEL:D_MODEL + n_taps].reshape(N_DEV, 4, -1)
    b_mine = lax.dynamic_slice(b_ada, (0, me * cols), (1, cols))
    mod_cols = _mm(sc_all, w_ada[0], "nn", name="ada_mod", extra_params=[b_mine], epi=lambda acc, b_: (acc + b_,))
    (mod_all,) = _all_to_all([_to_sheet(mod_cols, F32, 8)], "scatter_mod")
    mod = mod_all.reshape(N_DEV, -1)[:, :cols].reshape(1, N_MOD * D_MODEL)

    f1_w8, f1_out = _all_gather([shard["ffn1_w_in"].astype(BF16), shard["ffn1_w_out"].astype(BF16)], "gather_w1")
    travel = [s for s in SHEETED if s[0] != "gdn_conv_w"]
    flat_w = jnp.concatenate([shard[n].reshape(-1).astype(BF16) for n, _ in travel])
    later = _exchange_start([shard["ffn2_w_in"].astype(BF16), shard["ffn2_w_out"].astype(BF16), _to_sheet(flat_w, BF16, 16)],
                            False, "gather_w2_start")
    mod = mod + later[4][0:1, 0:1]
    W1 = dict(f1_w8=f1_w8, f1_wo4=f1_out.reshape(HID_PIECES, FFN_PIECE, D_MODEL))

    def later_weights(after):
        f2_w8, f2_out, sheet_all = _exchange_wait(later, False, after, "gather_w2_wait")
        w_all = sheet_all.reshape(N_DEV, -1)
        P, off = {}, 0
        for n, kind in travel:
            r, cc = shard[n].shape
            piece = w_all[:, off:off + r * cc].reshape(N_DEV, r, cc)
            P[n] = jnp.concatenate(list(piece), axis=1) if kind == "col" else piece.reshape(N_DEV * r, cc)
            off += r * cc
        P["gdn_conv_w"] = jnp.concatenate(list(conv_all), axis=1)
        for n in SMALL:
            P[n] = w[n]
        W = _pack_weights(P)
        W.update(f2_w8=f2_w8, f2_wo4=f2_out.reshape(HID_PIECES, FFN_PIECE, D_MODEL))
        return W

    pending, small_grads = {}, {}

    def ffn2_grads_ready(g_w8, g_wo4):
        pending["ffn2"] = _exchange_start([g_w8, g_wo4.reshape((N_DEV,) + shard["ffn2_w_out"].shape)], True, "scatter_f2_start")
        return pending["ffn2"][4][0:1, 0:1]

    def mixer_grads_ready(G):
        g_full = _unpack_grads(G)
        small_grads.update({n: g_full[n] for n in SMALL})
        pieces = []
        for n, kind in SHEETED:
            r, cc = shard[n].shape
            g = g_full[n]
            pieces.append(jnp.stack([g[:, cc * p:cc * (p + 1)].reshape(-1) for p in range(N_DEV)]) if kind == "col"
                          else g.reshape(N_DEV, r * cc))
        pending["mixer"] = _exchange_start([_to_sheet(jnp.concatenate(pieces, axis=1), BF16, 512)], True, "scatter_mx_start")
        return pending["mixer"][4][0:1, 0:1]

    cos_p, sin_p = _rope_tables(positions[0])
    loss_row, dx, d_mod, g_f1_w8, g_f1_wo4 = _local_step(x[0], loss_target[0], mod, cos_p, sin_p, W1, later_weights,
                                                         ffn2_grads_ready, mixer_grads_ready)

    sheet = _small_sheet(d_mod, small_grads).at[LOSS_ROW].set(loss_row[0, :128])
    (sheets,) = _all_gather([sheet], "gather_small")
    summed = _sum_devices(sheets, "sum_small")
    d_mod_all = sheets[:, :MOD_ROWS].reshape(N_DEV, N_MOD * D_MODEL)
    d_mod_mine = lax.dynamic_slice(d_mod_all, (0, me * cols), (N_DEV, cols))
    grads = _from_small_sheet(summed)
    grads["w_ada"] = _mm(sc_all, d_mod_mine, "tn", name="ada_gw", hi=True)
    loss = summed[LOSS_ROW, 0]

    arrived = list(_all_to_all([g_f1_w8, g_f1_wo4.reshape((N_DEV,) + shard["ffn1_w_out"].shape)], "scatter_f1"))
    arrived += _exchange_wait(pending["ffn2"], True, arrived[0], "scatter_f2_wait")
    for n, parts in zip(FFN_SHARDED, arrived):
        grads[n] = _sum_devices(parts, "sum_" + n)
    (mixer_parts,) = _exchange_wait(pending["mixer"], True, arrived[0], "scatter_mx_wait")
    g_mine = _sum_devices(mixer_parts, "sum_grads").reshape(-1)
    off = 0
    for n, _ in SHEETED:
        grads[n] = g_mine[off:off + shard[n].size].reshape(shard[n].shape)
        off += shard[n].size

    delta, new_m, new_v = {}, {}, {}
    for n in ("w_ada",) + FFN_SHARDED + tuple(s[0] for s in SHEETED):
        delta[n], new_m[n], new_v[n] = _adamw(w[n][0], grads[n], m[n][0], v[n][0], "adamw_" + n)
    small_in = [_small_sheet(t["b_ada"], t) for t in (w, grads, m, v)]
    for res, out in zip(_adamw(*small_in, "adamw_small"), (delta, new_m, new_v)):
        out.update(_from_small_sheet(res))

    def shaped(d):
        return [d[n].reshape(w[n].shape) for n in WEIGHT_NAMES]

    return (loss, dx[None], *shaped(grads), *shaped(delta), *shaped(new_m), *shaped(new_v))
```

```python
import functools

import jax
import jax.numpy as jnp
from jax import lax
from jax.experimental import pallas as pl
from jax.experimental.pallas import tpu as pltpu

F32 = jnp.float32
BF16 = jnp.bfloat16

D_MODEL = 1024
D_FF = 2816
N_MOD = 9
HEADS = 4
HEAD_DIM = 128
CHUNK = 64
EPS = 1e-6
ROPE = 64
Q_LORA = 384
KV_LORA = 256
N_IN = 2760
N_IN_PACKED = 2816
ROPE_BASE = 10000.0
N_DEV = 8

ADAM_LR = 0.001
ADAM_B1 = 0.9
ADAM_B2 = 0.999
ADAM_EPS = 1e-08
ADAM_WD = 0.01
ADAM_STEP = 10

VMEM_LIMIT_BYTES = 56 * 1024 * 1024
MESH = pl.DeviceIdType.MESH


def _params(sem=None):
    return pltpu.CompilerParams(dimension_semantics=sem, vmem_limit_bytes=VMEM_LIMIT_BYTES)


def _pick(dim, prefs):
    for p in prefs:
        if dim % p == 0:
            return p
    return dim


_DIMS = {"nn": (((1,), (0,)), ((), ())), "nt": (((1,), (1,)), ((), ())), "tn": (((0,), (0,)), ((), ()))}


def _dot_raw(a, b, mode):
    return lax.dot_general(a.astype(BF16), b.astype(BF16), _DIMS[mode], preferred_element_type=F32)


def _dot_hi(a, b, mode="nn"):
    return lax.dot_general(a, b, _DIMS[mode], precision=lax.Precision.HIGHEST, preferred_element_type=F32)


@functools.partial(jax.custom_vjp, nondiff_argnums=(2,))
def _bdot(a, b, mode):
    return _dot_raw(a, b, mode)


def _bdot_fwd(a, b, mode):
    return _dot_raw(a, b, mode), (a, b)


def _bdot_bwd(mode, res, g):
    a, b = res
    if mode == "nn":
        return _dot_raw(g, b, "nt"), _dot_raw(a, g, "tn")
    if mode == "nt":
        return _dot_raw(g, b, "nn"), _dot_raw(g, a, "tn")
    return _dot_raw(b, g, "nt"), _dot_raw(a, g, "nn")


_bdot.defvjp(_bdot_fwd, _bdot_bwd)


def _mm(a, b, mode, *, name, out_dtypes=(F32,), epi=None, extras=(), extra_params=(), hi=False,
        tm=None, tn=None, tk=None):
    if mode == "nn":
        (M, K), (_, N) = a.shape, b.shape
    elif mode == "nt":
        (M, K), (N, _) = a.shape, b.shape
    else:
        (K, M), (_, N) = a.shape, b.shape
    tm = tm or _pick(M, (512, 1408, 256, 128) if mode == "tn" else (512, 384, 352, 256, 128))
    tn = tn or _pick(N, (1024, 1408, 768, 512, 384, 256, 128))
    tk = tk or _pick(K, (1024, 1408, 512, 384, 256, 128))
    a_spec = {"nn": pl.BlockSpec((tm, tk), lambda i, j, k: (i, k)), "nt": pl.BlockSpec((tm, tk), lambda i, j, k: (i, k)),
              "tn": pl.BlockSpec((tk, tm), lambda i, j, k: (k, i))}[mode]
    b_spec = {"nn": pl.BlockSpec((tk, tn), lambda i, j, k: (k, j)), "nt": pl.BlockSpec((tn, tk), lambda i, j, k: (j, k)),
              "tn": pl.BlockSpec((tk, tn), lambda i, j, k: (k, j))}[mode]
    mn_spec = pl.BlockSpec((tm, tn), lambda i, j, k: (i, j))
    return _mmg(a, b, mode, name=name, grid=(M // tm, N // tn, K // tk), a_spec=a_spec, b_spec=b_spec, out_spec=mn_spec,
                out_shapes=[jax.ShapeDtypeStruct((M, N), dt) for dt in out_dtypes], acc_shape=(tm, tn), epi=epi,
                extras=list(extras) + list(extra_params),
                extra_specs=[mn_spec] * len(extras) + [pl.BlockSpec((1, tn), lambda i, j, k: (0, j))] * len(extra_params),
                hi=hi)


def _mmg(a, b, mode, *, name, grid, a_spec, b_spec, out_spec, out_shapes, acc_shape, epi=None, extras=(),
         extra_specs=(), hi=False):
    nk = grid[2]
    n_e, n_o = len(extras), len(out_shapes)

    def body(*refs):
        a_ref, b_ref = refs[:2]
        e_refs = refs[2:2 + n_e]
        o_refs = refs[2 + n_e:2 + n_e + n_o]
        acc_ref = refs[-1]
        k = pl.program_id(2)

        @pl.when(k == 0)
        def _():
            acc_ref[...] = jnp.zeros_like(acc_ref)

        if hi:
            acc_ref[...] += _dot_hi(a_ref[...].astype(F32), b_ref[...].astype(F32), mode)
        else:
            acc_ref[...] += _dot_raw(a_ref[...], b_ref[...], mode)

        @pl.when(k == nk - 1)
        def _():
            acc = acc_ref[...]
            outs = (acc,) if epi is None else epi(acc, *[e[...].astype(F32) for e in e_refs])
            for o_ref, o in zip(o_refs, outs):
                o_ref[...] = o.astype(o_ref.dtype)

    outs = pl.pallas_call(
        body, name=name, grid=grid,
        in_specs=[a_spec, b_spec] + list(extra_specs),
        out_specs=[out_spec] * n_o,
        out_shape=list(out_shapes),
        scratch_shapes=[pltpu.VMEM(acc_shape, F32)],
        compiler_params=_params(("parallel", "parallel", "arbitrary")),
    )(a, b, *extras)
    return outs if n_o > 1 else outs[0]


def _row_spec(th, cw, ci):
    return pl.BlockSpec((th, cw), lambda i: (i, ci))


def _full_spec(shape):
    return pl.BlockSpec(shape, lambda i: (0,) * len(shape))


def _rowwise(fn, rows, params, outs, n_steps, name):
    n_r, n_p, n_o = len(rows), len(params), len(outs)

    def body(*refs):
        vals = [r[...].astype(F32) for r in refs[:n_r + n_p]]
        res = fn(*vals)
        for o_ref, o in zip(refs[n_r + n_p:], res):
            o_ref[...] = o.astype(o_ref.dtype)

    across = [len(o) == 4 for o in outs]
    res = pl.pallas_call(
        body, name=name, grid=(n_steps,),
        in_specs=[_row_spec(th, cw, ci) for (_, th, cw, ci) in rows] + [_full_spec(p.shape) for p in params],
        out_specs=[pl.BlockSpec((o[0], o[1]), lambda i: (0, i)) if ac else _row_spec(o[0], o[1], 0)
                   for o, ac in zip(outs, across)],
        out_shape=[jax.ShapeDtypeStruct((o[0], n_steps * o[1]) if ac else (n_steps * o[0], o[1]), o[2])
                   for o, ac in zip(outs, across)],
        compiler_params=_params(("parallel",)),
    )(*[r[0] for r in rows], *params)
    return res


def _rowwise_bwd(fn, rows, aux, params, douts, n_steps, name, row_dtypes=None, adds=()):
    n_r, n_a, n_p, n_d, n_add = len(rows), len(aux), len(params), len(douts), len(adds)
    row_dtypes = row_dtypes or (F32,) * n_r

    def body(*refs):
        it = iter(refs)
        r_vals = [next(it)[...].astype(F32) for _ in range(n_r)]
        a_vals = [next(it)[...].astype(F32) for _ in range(n_a)]
        p_vals = [next(it)[...].astype(F32) for _ in range(n_p)]
        d_vals = [next(it)[...].astype(F32) for _ in range(n_d)]
        add_vals = [next(it)[...].astype(F32) for _ in range(n_add)]
        dr_refs = [next(it) for _ in range(n_r)]
        dp_refs = [next(it) for _ in range(n_p)]

        def f(*rp):
            return tuple(fn(*rp[:n_r], *a_vals, *rp[n_r:]))

        _, vjp = jax.vjp(f, *r_vals, *p_vals)
        grads = list(vjp(tuple(d_vals)))
        for (ri, _), av in zip(adds, add_vals):
            grads[ri] = grads[ri] + av
        for dr_ref, g in zip(dr_refs, grads[:n_r]):
            dr_ref[...] = g.astype(dr_ref.dtype)

        @pl.when(pl.program_id(0) == 0)
        def _():
            for dp_ref in dp_refs:
                dp_ref[...] = jnp.zeros_like(dp_ref)

        for dp_ref, g in zip(dp_refs, grads[n_r:]):
            dp_ref[...] += g

    all_rows = list(rows) + list(aux) + list(douts) + [(arr,) + tuple(rows[ri][1:3]) + (0,) for ri, arr in adds]
    in_specs = ([_row_spec(th, cw, ci) for (_, th, cw, ci) in list(rows) + list(aux)]
                + [_full_spec(p.shape) for p in params]
                + [_row_spec(th, cw, ci) for (_, th, cw, ci) in all_rows[n_r + n_a:]])
    res = pl.pallas_call(
        body, name=name, grid=(n_steps,),
        in_specs=in_specs,
        out_specs=[_row_spec(th, cw, 0) for (_, th, cw, _) in rows] + [_full_spec(p.shape) for p in params],
        out_shape=[jax.ShapeDtypeStruct((n_steps * th, cw), dt) for (_, th, cw, _), dt in zip(rows, row_dtypes)]
        + [jax.ShapeDtypeStruct(p.shape, F32) for p in params],
        compiler_params=_params(("arbitrary",)),
    )(*[r[0] for r in list(rows) + list(aux)], *params, *[r[0] for r in all_rows[n_r + n_a:]])
    return res[:n_r], res[n_r:]


def _sigmoid(x):
    return lax.logistic(x)


def _silu(x):
    return x * _sigmoid(x)


def _rms(x, w=None, n=None):
    n = n or x.shape[-1]
    y = x * lax.rsqrt(jnp.sum(x * x, axis=-1, keepdims=True) * (1.0 / n) + EPS)
    return y if w is None else y * w


def _modulate(x, scale, shift):
    return _rms(x) * (1.0 + scale) + shift


def _softplus(x):
    return jnp.maximum(x, 0.0) + jnp.log1p(jnp.exp(-jnp.abs(x)))


@jax.custom_vjp
def _rot_half64(x):
    lane = lax.broadcasted_iota(jnp.int32, x.shape, 1)
    up = pltpu.roll(x, 96, 1)
    down = pltpu.roll(x, 32, 1)
    return jnp.where(lane < 32, up, jnp.where(lane < 64, down, 0.0))


_rot_half64.defvjp(lambda x: (_rot_half64(x), None), lambda _, g: (_rot_half64(g),))


def _rope128(x, cos_p, sin_p):
    return x * cos_p + _rot_half64(x) * sin_p


def _gdn_pre_fn(qkvc, kab, alog_p, dt_p):
    a = _silu(qkvc)
    qs, ks = [], []
    for h in range(HEADS):
        qh = a[:, HEAD_DIM * h:HEAD_DIM * (h + 1)]
        kh = a[:, 512 + HEAD_DIM * h:512 + HEAD_DIM * (h + 1)]
        qs.append(qh * lax.rsqrt(jnp.sum(qh * qh, axis=-1, keepdims=True) + EPS) * (HEAD_DIM ** -0.5))
        ks.append(kh * lax.rsqrt(jnp.sum(kh * kh, axis=-1, keepdims=True) + EPS))
    lane = lax.broadcasted_iota(jnp.int32, kab.shape, 1)
    g_full = -jnp.exp(alog_p) * _softplus(kab + dt_p)
    b_full = _sigmoid(kab)
    gb = jnp.where((lane >= 64) & (lane < 68), g_full, jnp.where((lane >= 68) & (lane < 72), b_full, 0.0))
    return jnp.concatenate(qs, axis=1), jnp.concatenate(ks, axis=1), a[:, 1024:1536], gb


def _intra_head(q, k, v, g_col, b_col):
    c = CHUNK
    row = lax.broadcasted_iota(jnp.int32, (c, c), 0)
    col = lax.broadcasted_iota(jnp.int32, (c, c), 1)
    incl, strict, eye = row >= col, row > col, row == col
    tri = jnp.where(incl, 1.0, 0.0).astype(F32)
    ident = jnp.where(eye, 1.0, 0.0).astype(F32)
    g_wide = _dot_hi(tri, jnp.broadcast_to(g_col, (c, HEAD_DIM)))
    g_i = g_wide[:, :c]
    g_j = jnp.sum(jnp.where(eye, g_i, 0.0), axis=0, keepdims=True)
    decay = jnp.where(incl, jnp.exp(jnp.where(incl, g_i - g_j, 0.0)), 0.0)
    kk = _bdot(k, k, "nt")
    a_mat = jnp.where(strict, b_col * kk * decay, 0.0)
    x_pow = -a_mat
    inv = ident + x_pow
    for _ in range(5):
        x_pow = _dot_hi(x_pow, x_pow)
        inv = inv + _dot_hi(inv, x_pow)
    e_wide = jnp.exp(g_wide)
    u = _dot_hi(inv, v * b_col)
    wk = _dot_hi(inv, k * b_col * e_wide)
    qk = _bdot(q, k, "nt") * decay
    last = lax.broadcasted_iota(jnp.int32, (c, HEAD_DIM), 0) == c - 1
    g_last = jnp.sum(jnp.where(last, g_wide, 0.0), axis=0, keepdims=True)
    qd = q * e_wide
    kd = k * jnp.exp(g_last - g_wide)
    gl = jnp.broadcast_to(jnp.exp(g_last), (8, HEAD_DIM))
    return u, wk, qd, kd, qk, gl


INTRA_ROWS = (256, 128, 64)

_BNN = (((2,), (1,)), ((0,), (0,)))
_BNT = (((2,), (2,)), ((0,), (0,)))


def _split_bf16(a):
    hi = a.astype(BF16)
    return hi, (a - hi.astype(F32)).astype(BF16)


def _dot3_raw(a, b, dims):
    a_hi, a_lo = _split_bf16(a)
    b_hi, b_lo = _split_bf16(b)
    dot = lambda x_, y_: lax.dot_general(x_, y_, dims, preferred_element_type=F32)
    return dot(a_hi, b_hi) + (dot(a_hi, b_lo) + dot(a_lo, b_hi))


@functools.partial(jax.custom_vjp, nondiff_argnums=(2, 3))
def _dot3(a, b, nt, exact_bwd=True):
    return _dot3_raw(a, b, _BNT if nt else _BNN)


def _dot3_fwd(a, b, nt, exact_bwd):
    return _dot3_raw(a, b, _BNT if nt else _BNN), (a, b)


def _dot3_bwd(nt, exact_bwd, res, g):
    a, b = res
    if exact_bwd:
        dot = _dot3_raw
    else:
        dot = lambda x_, y_, d_: lax.dot_general(x_.astype(BF16), y_.astype(BF16), d_, preferred_element_type=F32)
    if nt:
        return dot(g, b, _BNN), dot(jnp.swapaxes(g, 1, 2), a, _BNN)
    return dot(g, b, _BNT), dot(jnp.swapaxes(a, 1, 2), g, _BNN)


_dot3.defvjp(_dot3_fwd, _dot3_bwd)


@functools.partial(jax.custom_vjp, nondiff_argnums=(2,))
def _bdot_b(a, b, nt):
    return lax.dot_general(a.astype(BF16), b.astype(BF16), _BNT if nt else _BNN, preferred_element_type=F32)


def _bdot_b_fwd(a, b, nt):
    return _bdot_b(a, b, nt), (a, b)


def _bdot_b_bwd(nt, res, g):
    a, b = res
    dot = lambda x_, y_, d_: lax.dot_general(x_.astype(BF16), y_.astype(BF16), d_, preferred_element_type=F32)
    if nt:
        return dot(g, b, _BNN), dot(jnp.swapaxes(g, 1, 2), a, _BNN)
    return dot(g, b, _BNT), dot(jnp.swapaxes(a, 1, 2), g, _BNN)


_bdot_b.defvjp(_bdot_b_fwd, _bdot_b_bwd)


def _intra_batched(q, k, v, g_col, b_col):
    c = CHUNK
    nb = q.shape[0]
    row = lax.broadcasted_iota(jnp.int32, (1, c, c), 1)
    col = lax.broadcasted_iota(jnp.int32, (1, c, c), 2)
    incl, strict, eye = row >= col, row > col, row == col
    tri = jnp.broadcast_to(jnp.where(incl, 1.0, 0.0).astype(F32), (nb, c, c))
    ident = jnp.where(eye, 1.0, 0.0).astype(F32)
    g_wide = _dot3(tri, jnp.broadcast_to(g_col, (nb, c, HEAD_DIM)), False)
    g_i = g_wide[:, :, :c]
    g_j = jnp.sum(jnp.where(eye, g_i, 0.0), axis=1, keepdims=True)
    decay = jnp.where(incl, jnp.exp(jnp.where(incl, g_i - g_j, 0.0)), 0.0)
    kk = _bdot_b(k, k, True)
    a_mat = jnp.where(strict, b_col * kk * decay, 0.0)
    x_pow = -a_mat
    inv = ident + x_pow
    for _ in range(5):
        x_pow = _dot3(x_pow, x_pow, False, False)
        inv = inv + _dot3(inv, x_pow, False, False)
    e_wide = jnp.exp(g_wide)
    u = _dot3(inv, v * b_col, False)
    wk = _dot3(inv, k * b_col * e_wide, False)
    qk = _bdot_b(q, k, True) * decay
    last = lax.broadcasted_iota(jnp.int32, (1, c, HEAD_DIM), 1) == c - 1
    g_last = jnp.sum(jnp.where(last, g_wide, 0.0), axis=1, keepdims=True)
    qd = q * e_wide
    kd = k * jnp.exp(g_last - g_wide)
    gl = jnp.broadcast_to(jnp.exp(g_last), (nb, 8, HEAD_DIM))
    return u, wk, qd, kd, qk, gl


def _gdn_intra_fn(q, k, v, gb):
    t = q.shape[0]
    nch = t // CHUNK
    lane = lax.broadcasted_iota(jnp.int32, gb.shape, 1)

    def heads_first(x_):
        return jnp.concatenate([x_[:, HEAD_DIM * h:HEAD_DIM * (h + 1)].reshape(nch, CHUNK, HEAD_DIM) for h in range(HEADS)],
                               axis=0)

    def column(first_lane):
        return jnp.concatenate([jnp.sum(jnp.where(lane == first_lane + h, gb, 0.0), axis=1, keepdims=True)
                                .reshape(nch, CHUNK, 1) for h in range(HEADS)], axis=0)

    u, wk, qd, kd, qk, gl = _intra_batched(heads_first(q), heads_first(k), heads_first(v), column(64), column(68))

    def rows_first(x_):
        r, w_ = x_.shape[1], x_.shape[2]
        return jnp.concatenate([x_[nch * h:nch * (h + 1)].reshape(nch * r, w_) for h in range(HEADS)], axis=1)

    qks = [qk[nch * h:nch * (h + 1)].reshape(t, CHUNK) for h in range(HEADS)]
    return (rows_first(u), rows_first(wk), rows_first(qd), rows_first(kd), *qks, rows_first(gl))


def _scan_step(s0, u, wk, qd, kd, qk, gl):
    v_new = u - _bdot_b(wk, s0, False)
    o = _bdot_b(qd, s0, False) + _bdot_b(qk, v_new, False)
    s1 = s0 * gl[:, 0:1, :] + _bdot_b(jnp.swapaxes(kd, 1, 2), v_new, False)
    return o, s1


def _mix_post_fn(o_a, z, o_b, gnw, onw):
    parts = [_rms(o_a[:, HEAD_DIM * h:HEAD_DIM * (h + 1)], gnw) * _silu(z[:, HEAD_DIM * h:HEAD_DIM * (h + 1)])
             for h in range(HEADS)]
    parts += [_rms(o_b[:, HEAD_DIM * h:HEAD_DIM * (h + 1)], onw) for h in range(HEADS)]
    return (jnp.concatenate(parts, axis=1),)


def _mla_pre_fn(ckv, cq, kab, cos_p, sin_p, qnw, kvnw, wuq, wukv, qn_w, qr_w, kn_w, kr_w):
    scale = (HEAD_DIM + ROPE) ** -0.5
    qf = _bdot(_rms(cq, qnw), wuq, "nn")
    kvf = _bdot(_rms(ckv, kvnw), wukv, "nn")
    lane = lax.broadcasted_iota(jnp.int32, kab.shape, 1)
    kr = _rope128(_rms(jnp.where(lane < ROPE, kab, 0.0), kr_w, n=ROPE), cos_p, sin_p)
    qs, ks = [], []
    for h in range(HEADS):
        qn = _rms(qf[:, 256 * h:256 * h + 128], qn_w) * scale
        qr = _rope128(_rms(qf[:, 256 * h + 128:256 * h + 256], qr_w, n=ROPE), cos_p, sin_p) * scale
        qs += [qn, qr]
        ks += [_rms(kvf[:, 128 * h:128 * (h + 1)], kn_w), kr]
    return jnp.concatenate(qs, axis=1), jnp.concatenate(ks, axis=1), kvf[:, 512:]


def _conv_fwd(proj, conv_w, tm, name):
    S = proj.shape[0]
    C = 1536
    nb = tm // 8

    def body(x_ref, prev_ref, w_ref, o_ref, ext_ref):
        i = pl.program_id(0)
        ext_ref[0:8, :] = jnp.where(i > 0, prev_ref[...], 0.0)
        ext_ref[8:, :] = x_ref[...]
        acc = jnp.zeros((tm, C), F32)
        for k in range(4):
            acc = acc + w_ref[k:k + 1, :] * ext_ref[pl.ds(5 + k, tm), :]
        o_ref[...] = acc

    return pl.pallas_call(
        body, name=name, grid=(S // tm,),
        in_specs=[pl.BlockSpec((tm, C), lambda i: (i, 0)),
                  pl.BlockSpec((8, C), lambda i: (jnp.maximum(i * nb - 1, 0), 0)),
                  pl.BlockSpec((4, C), lambda i: (0, 0))],
        out_specs=pl.BlockSpec((tm, C), lambda i: (i, 0)),
        out_shape=jax.ShapeDtypeStruct((S, C), F32),
        scratch_shapes=[pltpu.VMEM((tm + 8, C), F32)],
        compiler_params=_params(("arbitrary",)),
    )(proj, proj, conv_w)


def _conv_bwd(proj, dout, conv_w, tm, name):
    S = proj.shape[0]
    C = 1536
    nb = tm // 8
    n_steps = S // tm

    def body(x_ref, prev_ref, d_ref, next_ref, w_ref, dx_ref, dw_ref, xext_ref, dext_ref):
        i = pl.program_id(0)
        xext_ref[0:8, :] = jnp.where(i > 0, prev_ref[...], 0.0)
        xext_ref[8:, :] = x_ref[...]
        dext_ref[0:tm, :] = d_ref[...]
        dext_ref[tm:, :] = jnp.where(i < n_steps - 1, next_ref[...], 0.0)
        d = d_ref[...]
        acc = jnp.zeros((tm, C), F32)
        dws = []
        for k in range(4):
            acc = acc + w_ref[k:k + 1, :] * dext_ref[pl.ds(3 - k, tm), :]
            dws.append(jnp.sum(d * xext_ref[pl.ds(5 + k, tm), :], axis=0, keepdims=True))
        dx_ref[...] = acc

        @pl.when(i == 0)
        def _():
            dw_ref[...] = jnp.zeros_like(dw_ref)

        dw_ref[...] += jnp.concatenate(dws + [jnp.zeros((4, C), F32)], axis=0)

    return pl.pallas_call(
        body, name=name, grid=(n_steps,),
        in_specs=[pl.BlockSpec((tm, C), lambda i: (i, 0)),
                  pl.BlockSpec((8, C), lambda i: (jnp.maximum(i * nb - 1, 0), 0)),
                  pl.BlockSpec((tm, C), lambda i: (i, 0)),
                  pl.BlockSpec((8, C), lambda i: (jnp.minimum((i + 1) * nb, S // 8 - 1), 0)),
                  pl.BlockSpec((4, C), lambda i: (0, 0))],
        out_specs=[pl.BlockSpec((tm, C), lambda i: (i, 0)), pl.BlockSpec((8, C), lambda i: (0, 0))],
        out_shape=[jax.ShapeDtypeStruct((S, C), F32), jax.ShapeDtypeStruct((8, C), F32)],
        scratch_shapes=[pltpu.VMEM((tm + 8, C), F32), pltpu.VMEM((tm + 8, C), F32)],
        compiler_params=_params(("arbitrary",)),
    )(proj, proj, dout, dout, conv_w)


def _gdn_scan_fwd(u, wk, qd, kd, qks, gl, name):
    S = u.shape[0]
    nc = S // CHUNK
    W = HEADS * HEAD_DIM

    def body(u_ref, wk_ref, qd_ref, kd_ref, qk0, qk1, qk2, qk3, gl_ref, o_ref, sp_ref, s_ref):
        @pl.when(pl.program_id(0) == 0)
        def _():
            s_ref[...] = jnp.zeros_like(s_ref)

        s0 = s_ref[...]
        sp_ref[0] = s0
        o, s1 = _scan_step(s0, _heads(u_ref, HEAD_DIM), _heads(wk_ref, HEAD_DIM), _heads(qd_ref, HEAD_DIM),
                           _heads(kd_ref, HEAD_DIM), jnp.stack([r[...] for r in (qk0, qk1, qk2, qk3)]),
                           _heads(gl_ref, HEAD_DIM))
        s_ref[...] = s1
        for h in range(HEADS):
            o_ref[:, HEAD_DIM * h:HEAD_DIM * (h + 1)] = o[h]

    row = pl.BlockSpec((CHUNK, W), lambda n: (n, 0))
    qk_spec = pl.BlockSpec((CHUNK, CHUNK), lambda n: (n, 0))
    return pl.pallas_call(
        body, name=name, grid=(nc,),
        in_specs=[row, row, row, row, qk_spec, qk_spec, qk_spec, qk_spec, pl.BlockSpec((8, W), lambda n: (n, 0))],
        out_specs=[row, pl.BlockSpec((1, HEADS, HEAD_DIM, HEAD_DIM), lambda n: (n, 0, 0, 0))],
        out_shape=[jax.ShapeDtypeStruct((S, W), F32), jax.ShapeDtypeStruct((nc, HEADS, HEAD_DIM, HEAD_DIM), F32)],
        scratch_shapes=[pltpu.VMEM((HEADS, HEAD_DIM, HEAD_DIM), F32)],
        compiler_params=_params(("arbitrary",)),
    )(u, wk, qd, kd, *qks, gl)


def _gdn_scan_bwd(u, wk, qd, kd, qks, gl, s_prev, d_o, name):
    S = u.shape[0]
    nc = S // CHUNK
    W = HEADS * HEAD_DIM

    def body(u_ref, wk_ref, qd_ref, kd_ref, qk0, qk1, qk2, qk3, gl_ref, sp_ref, do_ref,
             du_ref, dwk_ref, dqd_ref, dkd_ref, dqk0, dqk1, dqk2, dqk3, dgl_ref, ds_ref):
        @pl.when(pl.program_id(0) == 0)
        def _():
            ds_ref[...] = jnp.zeros_like(ds_ref)

        _, vjp = jax.vjp(_scan_step, sp_ref[0], _heads(u_ref, HEAD_DIM), _heads(wk_ref, HEAD_DIM), _heads(qd_ref, HEAD_DIM),
                         _heads(kd_ref, HEAD_DIM), jnp.stack([r[...] for r in (qk0, qk1, qk2, qk3)]),
                         _heads(gl_ref, HEAD_DIM))
        ds0, du, dwk, dqd, dkd, dqk, dgl = vjp((_heads(do_ref, HEAD_DIM), ds_ref[...]))
        ds_ref[...] = ds0
        for h, dqk_ref in enumerate((dqk0, dqk1, dqk2, dqk3)):
            sl = slice(HEAD_DIM * h, HEAD_DIM * (h + 1))
            du_ref[:, sl] = du[h]
            dwk_ref[:, sl] = dwk[h]
            dqd_ref[:, sl] = dqd[h]
            dkd_ref[:, sl] = dkd[h]
            dqk_ref[...] = dqk[h]
            dgl_ref[:, sl] = dgl[h]

    rev = lambda n: (nc - 1 - n, 0)
    row = pl.BlockSpec((CHUNK, W), rev)
    qk_spec = pl.BlockSpec((CHUNK, CHUNK), rev)
    gl_spec = pl.BlockSpec((8, W), rev)
    qk_shape = jax.ShapeDtypeStruct((S, CHUNK), F32)
    row_shape = jax.ShapeDtypeStruct((S, W), F32)
    return pl.pallas_call(
        body, name=name, grid=(nc,),
        in_specs=[row, row, row, row, qk_spec, qk_spec, qk_spec, qk_spec, gl_spec,
                  pl.BlockSpec((1, HEADS, HEAD_DIM, HEAD_DIM), lambda n: (nc - 1 - n, 0, 0, 0)), row],
        out_specs=[row, row, row, row, qk_spec, qk_spec, qk_spec, qk_spec, gl_spec],
        out_shape=[row_shape] * 4 + [qk_shape] * 4 + [jax.ShapeDtypeStruct((nc * 8, W), F32)],
        scratch_shapes=[pltpu.VMEM((HEADS, HEAD_DIM, HEAD_DIM), F32)],
        compiler_params=_params(("arbitrary",)),
    )(u, wk, qd, kd, *qks, gl, s_prev, d_o)


NEG = -1e30


def _chunk_mask(i, j, t, transposed=False):
    q_axis, k_axis = (1, 0) if transposed else (0, 1)
    r = (i * t + lax.broadcasted_iota(jnp.int32, (t, t), q_axis)) // CHUNK
    c = (j * t + lax.broadcasted_iota(jnp.int32, (t, t), k_axis)) // CHUNK
    return c <= r


def _heads(ref, width):
    return jnp.stack([ref[:, width * h:width * (h + 1)] for h in range(HEADS)])


def _bmm(a, b, dims):
    return lax.dot_general(a.astype(BF16), b.astype(BF16), dims, preferred_element_type=F32)


def _attn_fwd(q, k, v_t, t, name):
    S = q.shape[0]
    n = S // t

    def body(q_ref, k_ref, vt_ref, o_ref, lse_ref, m_ref, l_ref, acc_ref):
        i, j = pl.program_id(0), pl.program_id(1)

        @pl.when(j == 0)
        def _():
            m_ref[...] = jnp.full_like(m_ref, NEG)
            l_ref[...] = jnp.zeros_like(l_ref)
            acc_ref[...] = jnp.zeros_like(acc_ref)

        def update(masked):
            s_t = _bmm(_heads(k_ref, 256), _heads(q_ref, 256), _BNT)
            if masked:
                s_t = jnp.where(_chunk_mask(i, j, t, transposed=True)[None], s_t, NEG)
            m_old = m_ref[...]
            m_new = jnp.maximum(m_old, jnp.max(s_t, axis=1, keepdims=True))
            p_t = jnp.exp(s_t - m_new)
            alpha = jnp.exp(m_old - m_new)
            l_ref[...] = alpha * l_ref[...] + jnp.sum(p_t, axis=1, keepdims=True)
            v_heads = jnp.stack([vt_ref[HEAD_DIM * h:HEAD_DIM * (h + 1), :] for h in range(HEADS)])
            acc_ref[...] = alpha * acc_ref[...] + _bmm(v_heads, p_t, _BNN)
            m_ref[...] = m_new

        @pl.when(j < i)
        def _():
            update(False)

        @pl.when(j == i)
        def _():
            update(True)
            for h in range(HEADS):
                sl = slice(HEAD_DIM * h, HEAD_DIM * (h + 1))
                o_ref[:, sl] = jnp.transpose(acc_ref[h] / l_ref[h])
                lse_ref[:, sl] = jnp.transpose(jnp.broadcast_to(m_ref[h] + jnp.log(l_ref[h]), (HEAD_DIM, t)))

    row = lambda i, j: (i, 0)
    return pl.pallas_call(
        body, name=name, grid=(n, n),
        in_specs=[pl.BlockSpec((t, HEADS * 256), row), pl.BlockSpec((t, HEADS * 256), lambda i, j: (jnp.minimum(j, i), 0)),
                  pl.BlockSpec((HEADS * HEAD_DIM, t), lambda i, j: (0, jnp.minimum(j, i)))],
        out_specs=[pl.BlockSpec((t, HEADS * HEAD_DIM), row)] * 2,
        out_shape=[jax.ShapeDtypeStruct((S, HEADS * HEAD_DIM), F32)] * 2,
        scratch_shapes=[pltpu.VMEM((HEADS, 1, t), F32), pltpu.VMEM((HEADS, 1, t), F32), pltpu.VMEM((HEADS, HEAD_DIM, t), F32)],
        compiler_params=_params(("parallel", "arbitrary")),
    )(q, k, v_t)


def _attn_dq(q, k, v, o, lse, d_o, t, name):
    S = q.shape[0]
    n = S // t

    def body(q_ref, k_ref, v_ref, o_ref, lse_ref, do_ref, dq_ref, st_ref, acc_ref, delta_ref):
        i, j = pl.program_id(0), pl.program_id(1)

        @pl.when(j == 0)
        def _():
            acc_ref[...] = jnp.zeros_like(acc_ref)
            delta_ref[...] = jnp.sum(_heads(do_ref, HEAD_DIM) * _heads(o_ref, HEAD_DIM), axis=2, keepdims=True)

        def update(masked):
            kh = _heads(k_ref, 256)
            d_out = _heads(do_ref, HEAD_DIM)
            s = _bmm(_heads(q_ref, 256), kh, _BNT)
            p = jnp.exp(s - _heads(lse_ref, HEAD_DIM)[:, :, 0:1])
            if masked:
                p = jnp.where(_chunk_mask(i, j, t)[None], p, 0.0)
            dp = _bmm(d_out, _heads(v_ref, HEAD_DIM), _BNT)
            ds = p * (dp - delta_ref[...])
            acc_ref[...] += _bmm(ds, kh, _BNN)

        @pl.when(j < i)
        def _():
            update(False)

        @pl.when(j == i)
        def _():
            update(True)
            lane = lax.broadcasted_iota(jnp.int32, (t, HEAD_DIM), 1)
            stats = jnp.zeros((t, HEAD_DIM), F32)
            for h in range(HEADS):
                dq_ref[:, 256 * h:256 * (h + 1)] = acc_ref[h]
                stats = stats + jnp.where(lane == h, lse_ref[:, HEAD_DIM * h:HEAD_DIM * (h + 1)], 0.0)
                stats = stats + jnp.where(lane == HEADS + h, delta_ref[h], 0.0)
            st_ref[...] = jnp.transpose(stats)[0:8, :]

    kv = lambda i, j: (jnp.minimum(j, i), 0)
    row = lambda i, j: (i, 0)
    wide, narrow = pl.BlockSpec((t, HEADS * 256), row), pl.BlockSpec((t, HEADS * HEAD_DIM), row)
    return pl.pallas_call(
        body, name=name, grid=(n, n),
        in_specs=[wide, pl.BlockSpec((t, HEADS * 256), kv), pl.BlockSpec((t, HEADS * HEAD_DIM), kv), narrow, narrow, narrow],
        out_specs=[wide, pl.BlockSpec((8, t), lambda i, j: (0, i))],
        out_shape=[jax.ShapeDtypeStruct((S, HEADS * 256), F32), jax.ShapeDtypeStruct((8, S), F32)],
        scratch_shapes=[pltpu.VMEM((HEADS, t, 256), F32), pltpu.VMEM((HEADS, t, 1), F32)],
        compiler_params=_params(("parallel", "arbitrary")),
    )(q, k, v, o, lse, d_o)


def _attn_dkv(q, k, v, d_o, stats, t, name):
    S = q.shape[0]
    n = S // t

    def body(q_ref, k_ref, v_ref, do_ref, st_ref, dk_ref, dv_ref, dk_acc, dv_acc):
        j, step = pl.program_id(0), pl.program_id(1)
        i = j + step

        @pl.when(step == 0)
        def _():
            dk_acc[...] = jnp.zeros_like(dk_acc)
            dv_acc[...] = jnp.zeros_like(dv_acc)

        def update(masked):
            qh = _heads(q_ref, 256)
            d_out = _heads(do_ref, HEAD_DIM)
            st = st_ref[...]
            lse_row = jnp.stack([st[h:h + 1, :] for h in range(HEADS)])
            delta_row = jnp.stack([st[HEADS + h:HEADS + h + 1, :] for h in range(HEADS)])
            s_t = _bmm(_heads(k_ref, 256), qh, _BNT)
            p_t = jnp.exp(s_t - lse_row)
            if masked:
                p_t = jnp.where(_chunk_mask(i, j, t, transposed=True)[None], p_t, 0.0)
            dv_acc[...] += _bmm(p_t, d_out, _BNN)
            dp_t = _bmm(_heads(v_ref, HEAD_DIM), d_out, _BNT)
            ds_t = p_t * (dp_t - delta_row)
            dk_acc[...] += _bmm(ds_t, qh, _BNN)

        @pl.when(step == 0)
        def _():
            update(True)

        @pl.when((step > 0) & (i < n))
        def _():
            update(False)

        @pl.when(step == n - 1)
        def _():
            for h in range(HEADS):
                dk_ref[:, 256 * h:256 * (h + 1)] = dk_acc[h]
                dv_ref[:, HEAD_DIM * h:HEAD_DIM * (h + 1)] = dv_acc[h]

    qi = lambda j, s_: (jnp.minimum(j + s_, n - 1), 0)
    kj = lambda j, s_: (j, 0)
    return pl.pallas_call(
        body, name=name, grid=(n, n),
        in_specs=[pl.BlockSpec((t, HEADS * 256), qi), pl.BlockSpec((t, HEADS * 256), kj), pl.BlockSpec((t, HEADS * HEAD_DIM), kj),
                  pl.BlockSpec((t, HEADS * HEAD_DIM), qi), pl.BlockSpec((8, t), lambda j, s_: (0, jnp.minimum(j + s_, n - 1)))],
        out_specs=[pl.BlockSpec((t, HEADS * 256), kj), pl.BlockSpec((t, HEADS * HEAD_DIM), kj)],
        out_shape=[jax.ShapeDtypeStruct((S, HEADS * 256), F32), jax.ShapeDtypeStruct((S, HEADS * HEAD_DIM), F32)],
        scratch_shapes=[pltpu.VMEM((HEADS, t, 256), F32), pltpu.VMEM((HEADS, t, HEAD_DIM), F32)],
        compiler_params=_params(("parallel", "arbitrary")),
    )(q, k, v, d_o, stats)


FFN_PIECE = 2 * D_FF // N_DEV
HID_PIECES = D_FF // FFN_PIECE


def _ffn_up(h, w8, name):
    S = h.shape[0]
    tm = _pick(S, (512, 256, 128))

    def body(h_ref, wg_ref, wu_ref, g_ref, u_ref, hid_ref):
        gate = _dot_raw(h_ref[...], wg_ref[...], "nn")
        up = _dot_raw(h_ref[...], wu_ref[...], "nn")
        g_ref[...] = gate.astype(BF16)
        u_ref[...] = up.astype(BF16)
        hid_ref[...] = (_silu(gate) * up).astype(BF16)

    o_spec = pl.BlockSpec((None, tm, FFN_PIECE), lambda i, j: (j, i, 0))
    return pl.pallas_call(
        body, name=name, grid=(S // tm, HID_PIECES),
        in_specs=[pl.BlockSpec((tm, D_MODEL), lambda i, j: (i, 0)),
                  pl.BlockSpec((None, D_MODEL, FFN_PIECE), lambda i, j: (j, 0, 0)),
                  pl.BlockSpec((None, D_MODEL, FFN_PIECE), lambda i, j: (j + HID_PIECES, 0, 0))],
        out_specs=[o_spec] * 3,
        out_shape=[jax.ShapeDtypeStruct((HID_PIECES, S, FFN_PIECE), BF16)] * 3,
        compiler_params=_params(("parallel", "parallel")),
    )(h, w8, w8)


def _after_specs(after):
    return [] if after is None else [pl.BlockSpec(memory_space=pl.ANY)]


def _after_args(after):
    return [] if after is None else [after]


def _ffn_gw8(h, d_gate, d_up, name, after=None):
    S = h.shape[0]
    tm = 512
    tk = _pick(S, (512, 256, 128))
    nk = S // tk

    def body(h_ref, dg_ref, du_ref, *rest):
        o_ref, acc_ref = rest[-2:]
        k = pl.program_id(1)

        @pl.when(k == 0)
        def _():
            acc_ref[...] = jnp.zeros_like(acc_ref)

        h_t = jnp.transpose(h_ref[...])
        for p in range(HID_PIECES):
            acc_ref[p] += _dot_raw(h_t, dg_ref[p], "nn")
            acc_ref[HID_PIECES + p] += _dot_raw(h_t, du_ref[p], "nn")

        @pl.when(k == nk - 1)
        def _():
            o_ref[...] = acc_ref[...].astype(o_ref.dtype)

    d_spec = pl.BlockSpec((HID_PIECES, tk, FFN_PIECE), lambda i, k: (0, k, 0))
    return pl.pallas_call(
        body, name=name, grid=(D_MODEL // tm, nk),
        in_specs=[pl.BlockSpec((tk, tm), lambda i, k: (k, i)), d_spec, d_spec] + _after_specs(after),
        out_specs=pl.BlockSpec((2 * HID_PIECES, tm, FFN_PIECE), lambda i, k: (0, i, 0)),
        out_shape=jax.ShapeDtypeStruct((2 * HID_PIECES, D_MODEL, FFN_PIECE), BF16),
        scratch_shapes=[pltpu.VMEM((2 * HID_PIECES, tm, FFN_PIECE), F32)],
        compiler_params=_params(("parallel", "arbitrary")),
    )(h, d_gate, d_up, *_after_args(after))


def _ffn_dh(d_gate, d_up, w8, name, after=None):
    S = d_gate.shape[1]
    tm = _pick(S, (512, 256, 128))

    def body(dg_ref, du_ref, wg_ref, wu_ref, *rest):
        o_ref = rest[-1]

        @pl.when(pl.program_id(1) == 0)
        def _():
            o_ref[...] = jnp.zeros_like(o_ref)

        o_ref[...] += _dot_raw(dg_ref[...], wg_ref[...], "nt") + _dot_raw(du_ref[...], wu_ref[...], "nt")

    d_spec = pl.BlockSpec((None, tm, FFN_PIECE), lambda i, k: (k, i, 0))
    return pl.pallas_call(
        body, name=name, grid=(S // tm, HID_PIECES),
        in_specs=[d_spec, d_spec,
                  pl.BlockSpec((None, D_MODEL, FFN_PIECE), lambda i, k: (k, 0, 0)),
                  pl.BlockSpec((None, D_MODEL, FFN_PIECE), lambda i, k: (k + HID_PIECES, 0, 0))] + _after_specs(after),
        out_specs=pl.BlockSpec((tm, D_MODEL), lambda i, k: (i, 0)),
        out_shape=jax.ShapeDtypeStruct((S, D_MODEL), F32),
        compiler_params=_params(("parallel", "arbitrary")),
    )(d_gate, d_up, w8, w8, *_after_args(after))


def _swiglu_bwd(d_hid, gate, up):
    sg = _sigmoid(gate)
    return d_hid * up * (sg * (1.0 + gate * (1.0 - sg))), d_hid * (gate * sg)


def _loss_and_grad(x3, target, name):
    S = x3.shape[0]
    tm = _pick(S, (512, 256, 128))
    n = S // tm

    def body(x_ref, t_ref, dx_ref, l_ref):
        i = pl.program_id(0)
        diff = x_ref[...] - t_ref[...]
        dx_ref[...] = diff * (1.0 / D_MODEL)

        @pl.when(i == 0)
        def _():
            l_ref[...] = jnp.zeros_like(l_ref)

        l_ref[...] += jnp.sum(diff * diff, axis=0, keepdims=True)

        @pl.when(i == n - 1)
        def _():
            l_ref[...] = jnp.full(l_ref.shape, (0.5 / D_MODEL) * jnp.sum(l_ref[...]), F32)

    return pl.pallas_call(
        body, name=name, grid=(n,),
        in_specs=[pl.BlockSpec((tm, D_MODEL), lambda i: (i, 0))] * 2,
        out_specs=[pl.BlockSpec((tm, D_MODEL), lambda i: (i, 0)), pl.BlockSpec((1, D_MODEL), lambda i: (0, 0))],
        out_shape=[jax.ShapeDtypeStruct((S, D_MODEL), F32), jax.ShapeDtypeStruct((1, D_MODEL), F32)],
        compiler_params=_params(("arbitrary",)),
    )(x3, target)


def _adamw(w, g, m, v, name):
    R, C = w.shape
    tr = _pick(R, (256, 176, 128, 64, 32, 16, 8))

    def body(w_ref, g_ref, m_ref, v_ref, d_ref, nm_ref, nv_ref):
        g_ = g_ref[...]
        m_ = ADAM_B1 * m_ref[...] + (1.0 - ADAM_B1) * g_
        v_ = ADAM_B2 * v_ref[...] + (1.0 - ADAM_B2) * (g_ * g_)
        m_hat = m_ / (1.0 - ADAM_B1 ** ADAM_STEP)
        v_hat = v_ / (1.0 - ADAM_B2 ** ADAM_STEP)
        d_ref[...] = -ADAM_LR * (m_hat / (jnp.sqrt(v_hat) + ADAM_EPS) + ADAM_WD * w_ref[...])
        nm_ref[...] = m_
        nv_ref[...] = v_

    spec = pl.BlockSpec((tr, C), lambda i: (i, 0))
    return pl.pallas_call(
        body, name=name, grid=(R // tr,),
        in_specs=[spec] * 4, out_specs=[spec] * 3,
        out_shape=[jax.ShapeDtypeStruct((R, C), F32)] * 3,
        compiler_params=_params(("parallel",)),
    )(w, g, m, v)


def _sum_devices(parts, name):
    _, R, C = parts.shape
    tr = _pick(R, (512, 256, 176, 128, 64, 32, 16, 8))

    def body(p_ref, o_ref):
        acc = p_ref[0].astype(F32)
        for d in range(1, N_DEV):
            acc = acc + p_ref[d].astype(F32)
        o_ref[...] = acc

    return pl.pallas_call(
        body, name=name, grid=(R // tr,),
        in_specs=[pl.BlockSpec((N_DEV, tr, C), lambda i: (0, i, 0))],
        out_specs=pl.BlockSpec((tr, C), lambda i: (i, 0)),
        out_shape=jax.ShapeDtypeStruct((R, C), F32),
        compiler_params=_params(("parallel",)),
    )(parts)


def _my_place():
    return lax.axis_index("x"), lax.axis_index("y"), lax.axis_index("c")


def _all_gather(blocks, name):
    n = len(blocks)

    def body(*refs):
        x_refs, out_refs = refs[:n], refs[n:2 * n]
        send_sems, recv_sems, local_sems = refs[2 * n:]
        x, y, c = _my_place()
        me, sibling = (x, y, c), (x, y, 1 - c)
        chips = [(1 - x, y), (x, 1 - y), (1 - x, 1 - y)]

        def copy(a, k, blk, to, own=False):
            slot = out_refs[a].at[4 * blk[0] + 2 * blk[1] + blk[2]]
            return pltpu.make_async_remote_copy(
                src_ref=x_refs[a] if own else slot, dst_ref=slot,
                send_sem=send_sems.at[7 * a + k], recv_sem=recv_sems.at[7 * a + k], device_id=to, device_id_type=MESH)

        mine = [pltpu.make_async_copy(x_refs[a], out_refs[a].at[4 * x + 2 * y + c], local_sems.at[a]) for a in range(n)]
        for cp in mine:
            cp.start()
        first = []
        for j, chip in enumerate(chips):
            first += [copy(a, 1 + j, me, (*chip, c), own=True) for a in range(n)]
        first += [copy(a, 0, me, sibling, own=True) for a in range(n)]
        for cp in first:
            cp.start()
        passed = []
        for j, chip in enumerate(chips):
            for a in range(n):
                copy(a, 1 + j, (*chip, c), me).wait_recv()
                passed.append(copy(a, 4 + j, (*chip, c), sibling))
                passed[-1].start()
        for a in range(n):
            copy(a, 0, sibling, me).wait_recv()
        for j, chip in enumerate(chips):
            for a in range(n):
                copy(a, 4 + j, (*chip, 1 - c), me).wait_recv()
        for cp in first + passed:
            cp.wait_send()
        for cp in mine:
            cp.wait()

    return pl.pallas_call(
        body, name=name,
        out_shape=[jax.ShapeDtypeStruct((N_DEV,) + b.shape, b.dtype) for b in blocks],
        in_specs=[pl.BlockSpec(memory_space=pl.ANY)] * n,
        out_specs=[pl.BlockSpec(memory_space=pl.ANY)] * n,
        scratch_shapes=[pltpu.SemaphoreType.DMA((7 * n,)), pltpu.SemaphoreType.DMA((7 * n,)), pltpu.SemaphoreType.DMA((n,))],
    )(*blocks)


def _all_to_all(pieces, name):
    n = len(pieces)

    def body(*refs):
        x_refs, out_refs = refs[:n], refs[n:2 * n]
        send_sems, recv_sems, local_sems = refs[2 * n:]
        x, y, c = _my_place()
        me = 4 * x + 2 * y + c
        mine = [pltpu.make_async_copy(x_refs[a].at[me], out_refs[a].at[me], local_sems.at[a]) for a in range(n)]
        for cp in mine:
            cp.start()
        copies = []
        for k in (2, 4, 6, 3, 5, 7, 1):
            px = 1 - x if k & 4 else x
            py = 1 - y if k & 2 else y
            pc = 1 - c if k & 1 else c
            peer = 4 * px + 2 * py + pc
            for a in range(n):
                copies.append(pltpu.make_async_remote_copy(
                    src_ref=x_refs[a].at[peer], dst_ref=out_refs[a].at[me],
                    send_sem=send_sems.at[7 * a + k - 1], recv_sem=recv_sems.at[7 * a + k - 1],
                    device_id=(px, py, pc), device_id_type=MESH))
        for cp in copies:
            cp.start()
        for cp in copies:
            cp.wait_recv()
        for cp in copies:
            cp.wait_send()
        for cp in mine:
            cp.wait()

    return pl.pallas_call(
        body, name=name,
        out_shape=[jax.ShapeDtypeStruct(p.shape, p.dtype) for p in pieces],
        in_specs=[pl.BlockSpec(memory_space=pl.ANY)] * n,
        out_specs=[pl.BlockSpec(memory_space=pl.ANY)] * n,
        scratch_shapes=[pltpu.SemaphoreType.DMA((7 * n,)), pltpu.SemaphoreType.DMA((7 * n,)), pltpu.SemaphoreType.DMA((n,))],
    )(*pieces)


def _peers():
    x, y, c = _my_place()
    out = []
    for k in (2, 4, 6, 3, 5, 7, 1):
        px = 1 - x if k & 4 else x
        py = 1 - y if k & 2 else y
        pc = 1 - c if k & 1 else c
        out.append((k, (px, py, pc), 4 * px + 2 * py + pc))
    return out


def _exchange_copies(x_refs, land_refs, send_sems, recv_sems, scatter):
    x, y, c = _my_place()
    me = 4 * x + 2 * y + c
    starts, arrivals = [], []
    for k, place, peer in _peers():
        for a, (x_ref, land_ref) in enumerate(zip(x_refs, land_refs)):
            sems = dict(send_sem=send_sems.at[7 * a + k - 1], recv_sem=recv_sems.at[7 * a + k - 1],
                        device_id=place, device_id_type=MESH)
            src = x_ref.at[peer] if scatter else x_ref
            starts.append(pltpu.make_async_remote_copy(src_ref=src, dst_ref=land_ref.at[me], **sems))
            arrivals.append(pltpu.make_async_remote_copy(src_ref=src, dst_ref=land_ref.at[peer], **sems))
    return starts, arrivals


def _exchange_start(arrays, scatter, name):
    n = len(arrays)
    hbm = pl.BlockSpec(memory_space=pltpu.HBM)
    sem = pl.BlockSpec(memory_space=pltpu.SEMAPHORE)
    lands = [lax.empty(a.shape if scatter else (N_DEV,) + a.shape, a.dtype) for a in arrays]

    def body(*refs):
        x_refs, land_refs = refs[:n], refs[n:2 * n]
        send_sems, recv_sems = refs[2 * n], refs[2 * n + 1]
        token = refs[-1]
        starts, _ = _exchange_copies(x_refs, land_refs, send_sems, recv_sems, scatter)
        for cp in starts:
            cp.start()
        token[...] = jnp.zeros_like(token)

    res = pl.pallas_call(
        body, name=name,
        out_shape=(pltpu.SemaphoreType.DMA((7 * n,)), pltpu.SemaphoreType.DMA((7 * n,)),
                   *[pltpu.HBM(a.shape, a.dtype) for a in arrays], *[pltpu.HBM(l.shape, l.dtype) for l in lands],
                   jax.ShapeDtypeStruct((8, 128), F32)),
        in_specs=[hbm] * (2 * n),
        out_specs=(sem, sem, *[hbm] * (2 * n), pl.BlockSpec(memory_space=pltpu.VMEM)),
        input_output_aliases={i: 2 + i for i in range(2 * n)},
        compiler_params=pltpu.CompilerParams(has_side_effects=pltpu.SideEffectType.DATAFLOW_SIDE_EFFECTING),
    )(*[pltpu.with_memory_space_constraint(a, pltpu.HBM) for a in arrays],
      *[pltpu.with_memory_space_constraint(l, pltpu.HBM) for l in lands])
    return res[0], res[1], list(res[2:2 + n]), list(res[2 + n:2 + 2 * n]), res[-1]


def _exchange_wait(handles, scatter, after, name):
    send_sems, recv_sems, arrays, lands, _ = handles
    n = len(arrays)
    hbm = pl.BlockSpec(memory_space=pltpu.HBM)
    sem = pl.BlockSpec(memory_space=pltpu.SEMAPHORE)

    def body(*refs):
        x_refs, land_refs = refs[:n], refs[n:2 * n]
        send_s, recv_s = refs[2 * n], refs[2 * n + 1]
        starts, arrivals = _exchange_copies(x_refs, land_refs, send_s, recv_s, scatter)
        for cp in arrivals:
            cp.wait_recv()
        for cp in starts:
            cp.wait_send()

    res = pl.pallas_call(
        body, name=name,
        out_shape=(*[pltpu.HBM(a.shape, a.dtype) for a in arrays], *[pltpu.HBM(l.shape, l.dtype) for l in lands]),
        in_specs=[hbm] * (2 * n) + [sem, sem, pl.BlockSpec(memory_space=pl.ANY)],
        out_specs=tuple([hbm] * (2 * n)),
        input_output_aliases={i: i for i in range(2 * n)},
        compiler_params=pltpu.CompilerParams(has_side_effects=pltpu.SideEffectType.DATAFLOW_SIDE_EFFECTING),
    )(*arrays, *lands, send_sems, recv_sems, after)
    me = 4 * lax.axis_index("x") + 2 * lax.axis_index("y") + lax.axis_index("c")
    out = []
    for src, got in zip(res[:n], res[n:]):
        zeros = (0,) * (got.ndim - 1)
        own = lax.dynamic_slice(src, (me,) + zeros, (1,) + src.shape[1:]) if scatter else src[None]
        out.append(lax.dynamic_update_slice(got, own, (me,) + zeros))
    return out


def _pad_lanes(v, at=0, width=128):
    return jnp.pad(v, ((0, 0), (at, width - at - v.shape[1])))


def _pack_weights(P):
    W = {}
    w = P["w_in"]
    W["wp"] = jnp.concatenate([w[:, :2048], w[:, 2440:2696], w[:, 2056:2440], w[:, 2696:2760], w[:, 2048:2056],
                               jnp.zeros((D_MODEL, N_IN_PACKED - N_IN), w.dtype)], axis=1).astype(BF16)
    W["conv_w"] = P["gdn_conv_w"].astype(F32)
    W["alog_p"] = _pad_lanes(P["gdn_a_log"], 64)
    W["dt_p"] = _pad_lanes(P["gdn_dt_bias"], 64)
    W["gnw"] = P["gdn_norm_w"]
    W["qnw"] = P["mla_q_norm_w"]
    W["kvnw"] = P["mla_kv_norm_w"]
    uq = P["mla_w_uq"].reshape(Q_LORA, HEADS, HEAD_DIM + ROPE)
    W["wuq"] = jnp.pad(uq, ((0, 0), (0, 0), (0, 256 - HEAD_DIM - ROPE))).reshape(Q_LORA, HEADS * 256).astype(BF16)
    ukv = P["mla_w_ukv"].reshape(KV_LORA, HEADS, 2, HEAD_DIM)
    W["wukv"] = ukv.transpose(0, 2, 1, 3).reshape(KV_LORA, 2 * HEADS * HEAD_DIM).astype(BF16)
    W["qn_w"] = P["qkn_q_nope"]
    W["qr_w"] = _pad_lanes(P["qkn_q_rope"])
    W["kn_w"] = P["qkn_k_nope"]
    W["kr_w"] = _pad_lanes(P["qkn_k_rope"])
    W["onw"] = P["mla_out_norm_w"]
    W["wout"] = P["w_out"].astype(BF16)
    return W


def _unpack_grads(G):
    g = G["wp"]
    uq = G["wuq"].reshape(Q_LORA, HEADS, 256)[:, :, :HEAD_DIM + ROPE].reshape(Q_LORA, HEADS * (HEAD_DIM + ROPE))
    ukv = G["wukv"].reshape(KV_LORA, 2, HEADS, HEAD_DIM).transpose(0, 2, 1, 3).reshape(KV_LORA, 2 * HEADS * HEAD_DIM)
    return {
        "w_in": jnp.concatenate([g[:, :2048], g[:, 2752:2760], g[:, 2304:2688], g[:, 2048:2304], g[:, 2688:2752]], axis=1),
        "gdn_conv_w": G["conv_w"], "gdn_a_log": G["alog_p"][:, 64:68], "gdn_dt_bias": G["dt_p"][:, 64:68],
        "gdn_norm_w": G["gnw"], "mla_q_norm_w": G["qnw"], "mla_w_uq": uq, "mla_kv_norm_w": G["kvnw"], "mla_w_ukv": ukv,
        "qkn_q_nope": G["qn_w"], "qkn_q_rope": G["qr_w"][:, :ROPE], "qkn_k_nope": G["kn_w"], "qkn_k_rope": G["kr_w"][:, :ROPE],
        "mla_out_norm_w": G["onw"], "w_out": G["wout"],
    }


def _rope_tables(positions):
    half = ROPE // 2
    inv_freq = ROPE_BASE ** (-jnp.arange(half, dtype=F32) / half)
    ang = positions.astype(F32)[:, None] * inv_freq
    cos, sin = jnp.cos(ang), jnp.sin(ang)
    zeros = jnp.zeros((positions.shape[0], 128 - ROPE), F32)
    return jnp.concatenate([cos, cos, zeros], axis=1), jnp.concatenate([-sin, sin, zeros], axis=1)


def _mod_fn(x, scale, shift):
    return (_modulate(x, scale, shift),)


def _ffn_fwd(x, scale, shift, gate_w, w8, wo4, tag):
    S = x.shape[0]
    tm = _pick(S, (512, 256, 128))
    (h,) = _rowwise(_mod_fn, [(x, tm, D_MODEL, 0)], [scale, shift], [(tm, D_MODEL, BF16)], S // tm, tag + "_mod")
    gate, up, hid = _ffn_up(h, w8, tag + "_up")
    mn = pl.BlockSpec((tm, D_MODEL), lambda i, j, k: (i, j))
    f, x_out = _mmg(hid, wo4, "nn", name=tag + "_down", grid=(S // tm, 1, HID_PIECES),
                    a_spec=pl.BlockSpec((None, tm, FFN_PIECE), lambda i, j, k: (k, i, 0)),
                    b_spec=pl.BlockSpec((None, FFN_PIECE, D_MODEL), lambda i, j, k: (k, 0, j)),
                    out_spec=mn, out_shapes=[jax.ShapeDtypeStruct((S, D_MODEL), F32)] * 2, acc_shape=(tm, D_MODEL),
                    extras=[x, gate_w], extra_specs=[mn, pl.BlockSpec((1, D_MODEL), lambda i, j, k: (0, j))],
                    epi=lambda acc, x_, g_: (acc, x_ + 0.5 * g_ * acc))
    return x_out, (h, gate, up, hid, f)


def _ffn_bwd(d_out, x, scale, shift, gate_w, w8, wo4, saved, tag, grad_ready):
    h, gate, up, hid, f = saved
    S = x.shape[0]
    tm = _pick(S, (512, 256, 128))
    tk = _pick(S, (512, 256, 128))
    n = S // tm
    (df,), (d_gate_w,) = _rowwise_bwd(lambda f_, g_: (0.5 * g_ * f_,), [(f, tm, D_MODEL, 0)], [], [gate_w],
                                      [(d_out, tm, D_MODEL, 0)], n, tag + "_dres", row_dtypes=(BF16,))
    piece = pl.BlockSpec((None, tm, FFN_PIECE), lambda i, j, k: (j, i, 0))
    d_gate, d_up = _mmg(df, wo4, "nt", name=tag + "_ddown", grid=(n, HID_PIECES, 1),
                        a_spec=pl.BlockSpec((tm, D_MODEL), lambda i, j, k: (i, 0)),
                        b_spec=pl.BlockSpec((None, FFN_PIECE, D_MODEL), lambda i, j, k: (j, 0, 0)),
                        out_spec=piece, out_shapes=[jax.ShapeDtypeStruct((HID_PIECES, S, FFN_PIECE), BF16)] * 2,
                        acc_shape=(tm, FFN_PIECE), extras=[gate, up], extra_specs=[piece, piece], epi=_swiglu_bwd)
    g_wo4 = _mmg(hid, df, "tn", name=tag + "_gwo", grid=(HID_PIECES, 1, S // tk),
                 a_spec=pl.BlockSpec((None, tk, FFN_PIECE), lambda i, j, k: (i, k, 0)),
                 b_spec=pl.BlockSpec((tk, D_MODEL), lambda i, j, k: (k, j)),
                 out_spec=pl.BlockSpec((None, FFN_PIECE, D_MODEL), lambda i, j, k: (i, 0, j)),
                 out_shapes=[jax.ShapeDtypeStruct((HID_PIECES, FFN_PIECE, D_MODEL), BF16)], acc_shape=(FFN_PIECE, D_MODEL))
    g_w8 = _ffn_gw8(h, d_gate, d_up, tag + "_gw8", after=grad_ready("wo4", g_wo4))
    dh = _ffn_dh(d_gate, d_up, w8, tag + "_dh", after=grad_ready("w8", g_w8))
    (dx,), (d_scale, d_shift) = _rowwise_bwd(_mod_fn, [(x, tm, D_MODEL, 0)], [], [scale, shift], [(dh, tm, D_MODEL, 0)],
                                             n, tag + "_dmod", adds=[(0, d_out)])
    return dx, d_scale, d_shift, d_gate_w


def _mixer_fwd(x1, scale, shift, gate_w, cos_p, sin_p, W):
    S = x1.shape[0]
    tm = _pick(S, (512, 256, 128))
    tv = _pick(S, (256, 128))
    ta = _pick(S, (512, 256, 128))
    nc = S // CHUNK
    (h2,) = _rowwise(_mod_fn, [(x1, tm, D_MODEL, 0)], [scale, shift], [(tm, D_MODEL, BF16)], S // tm, "mix_mod")
    proj = _mm(h2, W["wp"], "nn", name="mix_proj")
    qkvc = _conv_fwd(proj, W["conv_w"], tv, "gdn_conv")
    kab = (proj, tv, 128, 21)
    q_a, k_a, v_a, gb = _rowwise(_gdn_pre_fn, [(qkvc, tv, 1536, 0), kab], [W["alog_p"], W["dt_p"]],
                                 [(tv, 512, F32)] * 3 + [(tv, 128, F32)], S // tv, "gdn_pre")
    ti = _pick(S, INTRA_ROWS)
    intra = _rowwise(_gdn_intra_fn, [(q_a, ti, 512, 0), (k_a, ti, 512, 0), (v_a, ti, 512, 0), (gb, ti, 128, 0)],
                     [], [(ti, 512, F32)] * 4 + [(ti, CHUNK, F32)] * 4 + [(ti // 8, 512, F32)], S // ti, "gdn_intra")
    u, wk, qd, kd, qks, gl = intra[0], intra[1], intra[2], intra[3], tuple(intra[4:8]), intra[8]
    o_a, s_prev = _gdn_scan_fwd(u, wk, qd, kd, qks, gl, "gdn_scan")
    mla_params = [W["qnw"], W["kvnw"], W["wuq"], W["wukv"], W["qn_w"], W["qr_w"], W["kn_w"], W["kr_w"]]
    def mla_pre_with_vt(*a):
        q_, k_, v_ = _mla_pre_fn(*a)
        return q_, k_, v_, jnp.transpose(v_)

    q_b, k_b, v_b, vt_b = _rowwise(mla_pre_with_vt,
                                   [(proj, tv, 256, 8), (proj, tv, 384, 6), kab, (cos_p, tv, 128, 0), (sin_p, tv, 128, 0)],
                                   mla_params, [(tv, 1024, BF16), (tv, 1024, BF16), (tv, 512, BF16), (512, tv, BF16, "across")],
                                   S // tv, "mla_pre")
    o_b, lse = _attn_fwd(q_b, k_b, vt_b, ta, "mla_attn")
    (mixed,) = _rowwise(_mix_post_fn, [(o_a, tv, 512, 0), (proj, tv, 512, 3), (o_b, tv, 512, 0)], [W["gnw"], W["onw"]],
                        [(tv, D_MODEL, BF16)], S // tv, "mix_post")
    y, x2 = _mm(mixed, W["wout"], "nn", name="mix_out", out_dtypes=(F32, F32), extras=[x1], extra_params=[gate_w],
                epi=lambda acc, x_, g_: (acc, x_ + g_ * acc))
    saved = (h2, proj, qkvc, q_a, k_a, v_a, gb, u, wk, qd, kd, qks, gl, s_prev, o_a, q_b, k_b, v_b, o_b, lse, mixed, y)
    return x2, saved


def _mixer_bwd(d_out, x1, scale, shift, gate_w, cos_p, sin_p, W, saved):
    (h2, proj, qkvc, q_a, k_a, v_a, gb, u, wk, qd, kd, qks, gl, s_prev, o_a, q_b, k_b, v_b, o_b, lse, mixed, y) = saved
    S = x1.shape[0]
    tm = _pick(S, (512, 256, 128))
    tv = _pick(S, (256, 128))
    ta = _pick(S, (512, 256, 128))
    nc = S // CHUNK
    G = {}
    (dy,), (G["g2"],) = _rowwise_bwd(lambda y_, g_: (g_ * y_,), [(y, tm, D_MODEL, 0)], [], [gate_w],
                                     [(d_out, tm, D_MODEL, 0)], S // tm, "mix_dres", row_dtypes=(BF16,))
    d_mixed = _mm(dy, W["wout"], "nt", name="mix_dout")
    G["wout"] = _mm(mixed, dy, "tn", name="mix_gwout")
    (do_a, dz, do_b), (G["gnw"], G["onw"]) = _rowwise_bwd(
        _mix_post_fn, [(o_a, tv, 512, 0), (proj, tv, 512, 3), (o_b, tv, 512, 0)], [], [W["gnw"], W["onw"]],
        [(d_mixed, tv, D_MODEL, 0)], S // tv, "mix_dpost")
    dq_b, stats = _attn_dq(q_b, k_b, v_b, o_b, lse, do_b, ta, "mla_dq")
    dk_b, dv_b = _attn_dkv(q_b, k_b, v_b, do_b, stats, ta, "mla_dkv")
    kab = (proj, tv, 128, 21)
    mla_params = [W["qnw"], W["kvnw"], W["wuq"], W["wukv"], W["qn_w"], W["qr_w"], W["kn_w"], W["kr_w"]]
    (d_ckv, d_cq, d_kab), mla_grads = _rowwise_bwd(
        _mla_pre_fn, [(proj, tv, 256, 8), (proj, tv, 384, 6), kab], [(cos_p, tv, 128, 0), (sin_p, tv, 128, 0)], mla_params,
        [(dq_b, tv, 1024, 0), (dk_b, tv, 1024, 0), (dv_b, tv, 512, 0)], S // tv, "mla_dpre")
    for key, g in zip(("qnw", "kvnw", "wuq", "wukv", "qn_w", "qr_w", "kn_w", "kr_w"), mla_grads):
        G[key] = g
    scan_grads = _gdn_scan_bwd(u, wk, qd, kd, qks, gl, s_prev, do_a, "gdn_dscan")
    ti = _pick(S, INTRA_ROWS)
    intra_douts = [(scan_grads[i], ti, 512, 0) for i in range(4)] + [(scan_grads[4 + i], ti, CHUNK, 0) for i in range(4)]
    intra_douts.append((scan_grads[8], ti // 8, 512, 0))
    (dq_a, dk_a, dv_a, d_gb), _ = _rowwise_bwd(
        _gdn_intra_fn, [(q_a, ti, 512, 0), (k_a, ti, 512, 0), (v_a, ti, 512, 0), (gb, ti, 128, 0)], [], [],
        intra_douts, S // ti, "gdn_dintra")
    (d_qkvc, d_kab), (G["alog_p"], G["dt_p"]) = _rowwise_bwd(
        _gdn_pre_fn, [(qkvc, tv, 1536, 0), kab], [], [W["alog_p"], W["dt_p"]],
        [(dq_a, tv, 512, 0), (dk_a, tv, 512, 0), (dv_a, tv, 512, 0), (d_gb, tv, 128, 0)], S // tv, "gdn_dpre",
        adds=[(1, d_kab)])
    d_qkv, g_conv = _conv_bwd(proj, d_qkvc, W["conv_w"], tv, "gdn_dconv")
    G["conv_w"] = g_conv[:4]
    d_proj = jnp.concatenate([d_qkv, dz, d_ckv, d_cq, d_kab], axis=1).astype(BF16)
    G["wp"] = _mm(h2, d_proj, "tn", name="mix_gwp")
    dh2 = _mm(d_proj, W["wp"], "nt", name="mix_dproj")
    (dx1,), (G["s2"], G["sh2"]) = _rowwise_bwd(_mod_fn, [(x1, tm, D_MODEL, 0)], [], [scale, shift],
                                               [(dh2, tm, D_MODEL, 0)], S // tm, "mix_dmod", adds=[(0, d_out)])
    return dx1, G


def _local_step(x, target, mod, cos_p, sin_p, W1, later_weights, ffn_grad_ready, mixer_grads_ready):
    sh1, s1, g1, sh2, s2, g2, sh3, s3, g3 = [mod[:, D_MODEL * i:D_MODEL * (i + 1)] for i in range(N_MOD)]
    x1, saved1 = _ffn_fwd(x, s1, sh1, g1, W1["f1_w8"], W1["f1_wo4"], "ffn1")
    W = later_weights(x1)
    x2, saved2 = _mixer_fwd(x1, s2, sh2, g2, cos_p, sin_p, W)
    x3, saved3 = _ffn_fwd(x2, s3, sh3, g3, W["f2_w8"], W["f2_wo4"], "ffn2")
    dx3, loss_row = _loss_and_grad(x3, target, "loss")
    dx2, d_s3, d_sh3, d_g3 = _ffn_bwd(dx3, x2, s3, sh3, g3, W["f2_w8"], W["f2_wo4"], saved3, "ffn2", ffn_grad_ready("f2"))
    dx1, G = _mixer_bwd(dx2, x1, s2, sh2, g2, cos_p, sin_p, W, saved2)
    d_sh2, d_s2, d_g2 = G.pop("sh2"), G.pop("s2"), G.pop("g2")
    g1 = g1 + mixer_grads_ready(G)
    dx, d_s1, d_sh1, d_g1 = _ffn_bwd(dx1, x, s1, sh1, g1, W1["f1_w8"], W1["f1_wo4"], saved1, "ffn1", ffn_grad_ready("f1"))
    d_mod = jnp.concatenate([d_sh1, d_s1, d_g1, d_sh2, d_s2, d_g2, d_sh3, d_s3, d_g3], axis=1)
    return loss_row, dx, d_mod


WEIGHT_NAMES = ("w_ada", "b_ada", "ffn1_w_in", "ffn1_w_out", "w_in", "gdn_conv_w", "gdn_a_log", "gdn_dt_bias", "gdn_norm_w",
                "mla_q_norm_w", "mla_w_uq", "mla_kv_norm_w", "mla_w_ukv", "qkn_q_nope", "qkn_q_rope", "qkn_k_nope",
                "qkn_k_rope", "mla_out_norm_w", "w_out", "ffn2_w_in", "ffn2_w_out")
FFN_SHARDED = ("ffn1_w_in", "ffn1_w_out", "ffn2_w_in", "ffn2_w_out")
SHEETED = (("w_in", "col"), ("gdn_conv_w", "col"), ("mla_w_uq", "col"), ("mla_w_ukv", "col"), ("w_out", "row"))
MOD_ROWS = N_MOD * D_MODEL // 128
SMALL = {"gdn_a_log": (MOD_ROWS, 1, 64, 4), "gdn_dt_bias": (MOD_ROWS + 1, 1, 64, 4), "gdn_norm_w": (MOD_ROWS + 2, 1, 0, 128),
         "mla_q_norm_w": (MOD_ROWS + 3, 3, 0, 384), "mla_kv_norm_w": (MOD_ROWS + 6, 2, 0, 256),
         "qkn_q_nope": (MOD_ROWS + 8, 1, 0, 128), "qkn_q_rope": (MOD_ROWS + 9, 1, 0, 64), "qkn_k_nope": (MOD_ROWS + 10, 1, 0, 128),
         "qkn_k_rope": (MOD_ROWS + 11, 1, 0, 64), "mla_out_norm_w": (MOD_ROWS + 12, 1, 0, 128)}
LOSS_ROW = MOD_ROWS + 13
SHEET_ROWS = 88


def _to_sheet(flat, dtype, sublanes):
    n = flat.shape[-1]
    unit = sublanes * 128
    pad = (-n) % unit
    flat = jnp.pad(flat.astype(dtype), [(0, 0)] * (flat.ndim - 1) + [(0, pad)])
    return flat.reshape(flat.shape[:-1] + ((n + pad) // 128, 128))


def _small_sheet(b_like, small):
    sheet = jnp.zeros((SHEET_ROWS, 128), F32).at[:MOD_ROWS].set(b_like.reshape(MOD_ROWS, 128))
    for name, (row, rows, lane, n) in SMALL.items():
        v = small[name].reshape(1, n)
        if rows == 1:
            sheet = sheet.at[row, lane:lane + n].set(v[0])
        else:
            sheet = sheet.at[row:row + rows].set(v.reshape(rows, 128))
    return sheet


def _from_small_sheet(sheet):
    out = {"b_ada": sheet[:MOD_ROWS].reshape(1, N_MOD * D_MODEL)}
    for name, (row, rows, lane, n) in SMALL.items():
        out[name] = sheet[row, lane:lane + n].reshape(1, n) if rows == 1 else sheet[row:row + rows].reshape(1, n)
    return out


def kernel(x, c, positions, w_ada, b_ada, ffn1_w_in, ffn1_w_out, w_in, gdn_conv_w, gdn_a_log, gdn_dt_bias, gdn_norm_w, mla_q_norm_w, mla_w_uq, mla_kv_norm_w, mla_w_ukv, qkn_q_nope, qkn_q_rope, qkn_k_nope, qkn_k_rope, mla_out_norm_w, w_out, ffn2_w_in, ffn2_w_out, loss_target, m_w_ada, m_b_ada, m_ffn1_w_in, m_ffn1_w_out, m_w_in, m_gdn_conv_w, m_gdn_a_log, m_gdn_dt_bias, m_gdn_norm_w, m_mla_q_norm_w, m_mla_w_uq, m_mla_kv_norm_w, m_mla_w_ukv, m_qkn_q_nope, m_qkn_q_rope, m_qkn_k_nope, m_qkn_k_rope, m_mla_out_norm_w, m_w_out, m_ffn2_w_in, m_ffn2_w_out, v_w_ada, v_b_ada, v_ffn1_w_in, v_ffn1_w_out, v_w_in, v_gdn_conv_w, v_gdn_a_log, v_gdn_dt_bias, v_gdn_norm_w, v_mla_q_norm_w, v_mla_w_uq, v_mla_kv_norm_w, v_mla_w_ukv, v_qkn_q_nope, v_qkn_q_rope, v_qkn_k_nope, v_qkn_k_rope, v_mla_out_norm_w, v_w_out, v_ffn2_w_in, v_ffn2_w_out):
    args = locals()
    w = {n: args[n] for n in WEIGHT_NAMES}
    m = {n: args["m_" + n] for n in WEIGHT_NAMES}
    v = {n: args["v_" + n] for n in WEIGHT_NAMES}
    me = 4 * lax.axis_index("x") + 2 * lax.axis_index("y") + lax.axis_index("c")
    cols = N_MOD * D_MODEL // N_DEV
    shard = {n: w[n][0] for n in FFN_SHARDED + tuple(s[0] for s in SHEETED)}

    sc = c * _sigmoid(c)
    first = _to_sheet(jnp.concatenate([sc.reshape(-1), shard["gdn_conv_w"].reshape(-1)]), F32, 8)
    (first_all,) = _all_gather([first], "gather_c")
    sc_all = first_all[:, :D_MODEL // 128].reshape(N_DEV, D_MODEL)
    n_taps = shard["gdn_conv_w"].size
    conv_all = first_all.reshape(N_DEV, -1)[:, D_MODEL:D_MODEL + n_taps].reshape(N_DEV, 4, -1)
    b_mine = lax.dynamic_slice(b_ada, (0, me * cols), (1, cols))
    mod_cols = _mm(sc_all, w_ada[0], "nn", name="ada_mod", extra_params=[b_mine], epi=lambda acc, b_: (acc + b_,))
    (mod_all,) = _all_to_all([_to_sheet(mod_cols, F32, 8)], "scatter_mod")
    mod = mod_all.reshape(N_DEV, -1)[:, :cols].reshape(1, N_MOD * D_MODEL)

    f1_w8, f1_out = _all_gather([shard["ffn1_w_in"].astype(BF16), shard["ffn1_w_out"].astype(BF16)], "gather_w1")
    travel = [s for s in SHEETED if s[0] != "gdn_conv_w"]
    flat_w = jnp.concatenate([shard[n].reshape(-1).astype(BF16) for n, _ in travel])
    w_sheet, f1_w8 = lax.optimization_barrier((_to_sheet(flat_w, BF16, 16), f1_w8))
    later = _exchange_start([shard["ffn2_w_in"].astype(BF16), shard["ffn2_w_out"].astype(BF16), w_sheet],
                            False, "gather_w2_start")
    mod = mod + later[4][0:1, 0:1]
    W1 = dict(f1_w8=f1_w8, f1_wo4=f1_out.reshape(HID_PIECES, FFN_PIECE, D_MODEL))

    def later_weights(after):
        f2_w8, f2_out, sheet_all = _exchange_wait(later, False, after, "gather_w2_wait")
        w_all = sheet_all.reshape(N_DEV, -1)
        P, off = {}, 0
        for n, kind in travel:
            r, cc = shard[n].shape
            piece = w_all[:, off:off + r * cc].reshape(N_DEV, r, cc)
            P[n] = jnp.concatenate(list(piece), axis=1) if kind == "col" else piece.reshape(N_DEV * r, cc)
            off += r * cc
        P["gdn_conv_w"] = jnp.concatenate(list(conv_all), axis=1)
        for n in SMALL:
            P[n] = w[n]
        W = _pack_weights(P)
        W.update(f2_w8=f2_w8, f2_wo4=f2_out.reshape(HID_PIECES, FFN_PIECE, D_MODEL))
        return W

    pending, small_grads = {}, {}

    def ffn_grad_ready(tag):
        def ready(which, g):
            pieces = g if which == "w8" else g.reshape((N_DEV,) + shard["ffn1_w_out"].shape)
            pending[tag + which] = _exchange_start([pieces], True, "scatter_%s_%s_start" % (tag, which))
            return pending[tag + which][4]
        return ready

    def mixer_grads_ready(G):
        g_full = _unpack_grads(G)
        small_grads.update({n: g_full[n] for n in SMALL})
        pieces = []
        for n, kind in SHEETED:
            r, cc = shard[n].shape
            g = g_full[n]
            pieces.append(jnp.stack([g[:, cc * p:cc * (p + 1)].reshape(-1) for p in range(N_DEV)]) if kind == "col"
                          else g.reshape(N_DEV, r * cc))
        pending["mixer"] = _exchange_start([_to_sheet(jnp.concatenate(pieces, axis=1), BF16, 512)], True, "scatter_mx_start")
        return pending["mixer"][4][0:1, 0:1]

    cos_p, sin_p = _rope_tables(positions[0])
    loss_row, dx, d_mod = _local_step(x[0], loss_target[0], mod, cos_p, sin_p, W1, later_weights, ffn_grad_ready,
                                      mixer_grads_ready)

    sheet = _small_sheet(d_mod, small_grads).at[LOSS_ROW].set(loss_row[0, :128])
    (sheets,) = _all_gather([sheet], "gather_small")
    summed = _sum_devices(sheets, "sum_small")
    d_mod_all = sheets[:, :MOD_ROWS].reshape(N_DEV, N_MOD * D_MODEL)
    d_mod_mine = lax.dynamic_slice(d_mod_all, (0, me * cols), (N_DEV, cols))
    grads = _from_small_sheet(summed)
    grads["w_ada"] = _mm(sc_all, d_mod_mine, "tn", name="ada_gw", hi=True)
    loss = summed[LOSS_ROW, 0]

    for n, key in zip(FFN_SHARDED, ("f1w8", "f1wo4", "f2w8", "f2wo4")):
        (parts,) = _exchange_wait(pending[key], True, summed, "scatter_%s_wait" % key)
        grads[n] = _sum_devices(parts, "sum_" + n)
    (mixer_parts,) = _exchange_wait(pending["mixer"], True, summed, "scatter_mx_wait")
    g_mine = _sum_devices(mixer_parts, "sum_grads").reshape(-1)
    off = 0
    for n, _ in SHEETED:
        grads[n] = g_mine[off:off + shard[n].size].reshape(shard[n].shape)
        off += shard[n].size

    delta, new_m, new_v = {}, {}, {}
    for n in ("w_ada",) + FFN_SHARDED + tuple(s[0] for s in SHEETED):
        delta[n], new_m[n], new_v[n] = _adamw(w[n][0], grads[n], m[n][0], v[n][0], "adamw_" + n)
    small_in = [_small_sheet(t["b_ada"], t) for t in (w, grads, m, v)]
    for res, out in zip(_adamw(*small_in, "adamw_small"), (delta, new_m, new_v)):
        out.update(_from_small_sheet(res))

    def shaped(d):
        return [d[n].reshape(w[n].shape) for n in WEIGHT_NAMES]

    return (loss, dx[None], *shaped(grads), *shaped(delta), *shaped(new_m), *shaped(new_v))
```

```python
import functools

import jax
import jax.numpy as jnp
import numpy as np
from jax import lax
from jax.experimental import pallas as pl
from jax.experimental.pallas import tpu as pltpu

F32 = jnp.float32
BF16 = jnp.bfloat16

D_MODEL = 1024
D_FF = 2816
N_MOD = 9
HEADS = 4
HEAD_DIM = 128
CHUNK = 64
EPS = 1e-6
ROPE = 64
Q_LORA = 384
KV_LORA = 256
N_IN = 2760
N_IN_PACKED = 2816
ROPE_BASE = 10000.0
N_DEV = 8

ADAM_LR = 0.001
ADAM_B1 = 0.9
ADAM_B2 = 0.999
ADAM_EPS = 1e-08
ADAM_WD = 0.01
ADAM_STEP = 10

VMEM_LIMIT_BYTES = 56 * 1024 * 1024
MATMUL_ROWS = (1024, 512, 256, 128)
MESH = pl.DeviceIdType.MESH


def _params(sem=None):
    return pltpu.CompilerParams(dimension_semantics=sem, vmem_limit_bytes=VMEM_LIMIT_BYTES)


def _pick(dim, prefs):
    for p in prefs:
        if dim % p == 0:
            return p
    return dim


_DIMS = {"nn": (((1,), (0,)), ((), ())), "nt": (((1,), (1,)), ((), ())), "tn": (((0,), (0,)), ((), ()))}


def _dot_raw(a, b, mode):
    return lax.dot_general(a.astype(BF16), b.astype(BF16), _DIMS[mode], preferred_element_type=F32)


def _dot_hi(a, b, mode="nn"):
    return lax.dot_general(a, b, _DIMS[mode], precision=lax.Precision.HIGHEST, preferred_element_type=F32)


@functools.partial(jax.custom_vjp, nondiff_argnums=(2,))
def _bdot(a, b, mode):
    return _dot_raw(a, b, mode)


def _bdot_fwd(a, b, mode):
    return _dot_raw(a, b, mode), (a, b)


def _bdot_bwd(mode, res, g):
    a, b = res
    if mode == "nn":
        return _dot_raw(g, b, "nt"), _dot_raw(a, g, "tn")
    if mode == "nt":
        return _dot_raw(g, b, "nn"), _dot_raw(g, a, "tn")
    return _dot_raw(b, g, "nt"), _dot_raw(a, g, "nn")


_bdot.defvjp(_bdot_fwd, _bdot_bwd)


def _mm(a, b, mode, *, name, out_dtypes=(F32,), epi=None, extras=(), extra_params=(), hi=False,
        tm=None, tn=None, tk=None):
    if mode == "nn":
        (M, K), (_, N) = a.shape, b.shape
    elif mode == "nt":
        (M, K), (N, _) = a.shape, b.shape
    else:
        (K, M), (_, N) = a.shape, b.shape
    tm = tm or _pick(M, (512, 1408, 256, 128) if mode == "tn" else MATMUL_ROWS + (384, 352))
    tn = tn or _pick(N, (1024, 1408, 768, 512, 384, 256, 128))
    tk = tk or _pick(K, (1024, 1408, 512, 384, 256, 128))
    a_spec = {"nn": pl.BlockSpec((tm, tk), lambda i, j, k: (i, k)), "nt": pl.BlockSpec((tm, tk), lambda i, j, k: (i, k)),
              "tn": pl.BlockSpec((tk, tm), lambda i, j, k: (k, i))}[mode]
    b_spec = {"nn": pl.BlockSpec((tk, tn), lambda i, j, k: (k, j)), "nt": pl.BlockSpec((tn, tk), lambda i, j, k: (j, k)),
              "tn": pl.BlockSpec((tk, tn), lambda i, j, k: (k, j))}[mode]
    mn_spec = pl.BlockSpec((tm, tn), lambda i, j, k: (i, j))
    return _mmg(a, b, mode, name=name, grid=(M // tm, N // tn, K // tk), a_spec=a_spec, b_spec=b_spec, out_spec=mn_spec,
                out_shapes=[jax.ShapeDtypeStruct((M, N), dt) for dt in out_dtypes], acc_shape=(tm, tn), epi=epi,
                extras=list(extras) + list(extra_params),
                extra_specs=[mn_spec] * len(extras) + [pl.BlockSpec((1, tn), lambda i, j, k: (0, j))] * len(extra_params),
                hi=hi)


def _mmg(a, b, mode, *, name, grid, a_spec, b_spec, out_spec, out_shapes, acc_shape, epi=None, extras=(),
         extra_specs=(), hi=False):
    nk = grid[2]
    n_e, n_o = len(extras), len(out_shapes)

    def body(*refs):
        a_ref, b_ref = refs[:2]
        e_refs = refs[2:2 + n_e]
        o_refs = refs[2 + n_e:2 + n_e + n_o]
        acc_ref = refs[-1]
        k = pl.program_id(2)

        @pl.when(k == 0)
        def _():
            acc_ref[...] = jnp.zeros_like(acc_ref)

        if hi:
            acc_ref[...] += _dot_hi(a_ref[...].astype(F32), b_ref[...].astype(F32), mode)
        else:
            acc_ref[...] += _dot_raw(a_ref[...], b_ref[...], mode)

        @pl.when(k == nk - 1)
        def _():
            acc = acc_ref[...]
            outs = (acc,) if epi is None else epi(acc, *[e[...].astype(F32) for e in e_refs])
            for o_ref, o in zip(o_refs, outs):
                o_ref[...] = o.astype(o_ref.dtype)

    outs = pl.pallas_call(
        body, name=name, grid=grid,
        in_specs=[a_spec, b_spec] + list(extra_specs),
        out_specs=[out_spec] * n_o,
        out_shape=list(out_shapes),
        scratch_shapes=[pltpu.VMEM(acc_shape, F32)],
        compiler_params=_params(("parallel", "parallel", "arbitrary")),
    )(a, b, *extras)
    return outs if n_o > 1 else outs[0]


def _row_spec(th, cw, ci):
    return pl.BlockSpec((th, cw), lambda i: (i, ci))


def _full_spec(shape):
    return pl.BlockSpec(shape, lambda i: (0,) * len(shape))


def _rowwise(fn, rows, params, outs, n_steps, name):
    n_r, n_p, n_o = len(rows), len(params), len(outs)

    def body(*refs):
        vals = [r[...].astype(F32) for r in refs[:n_r + n_p]]
        res = fn(*vals)
        for o_ref, o in zip(refs[n_r + n_p:], res):
            o_ref[...] = o.astype(o_ref.dtype)

    across = [len(o) == 4 for o in outs]
    res = pl.pallas_call(
        body, name=name, grid=(n_steps,),
        in_specs=[_row_spec(th, cw, ci) for (_, th, cw, ci) in rows] + [_full_spec(p.shape) for p in params],
        out_specs=[pl.BlockSpec((o[0], o[1]), lambda i: (0, i)) if ac else _row_spec(o[0], o[1], 0)
                   for o, ac in zip(outs, across)],
        out_shape=[jax.ShapeDtypeStruct((o[0], n_steps * o[1]) if ac else (n_steps * o[0], o[1]), o[2])
                   for o, ac in zip(outs, across)],
        compiler_params=_params(("parallel",)),
    )(*[r[0] for r in rows], *params)
    return res


def _rowwise_bwd(fn, rows, aux, params, douts, n_steps, name, row_dtypes=None, adds=()):
    n_r, n_a, n_p, n_d, n_add = len(rows), len(aux), len(params), len(douts), len(adds)
    row_dtypes = row_dtypes or (F32,) * n_r

    def body(*refs):
        it = iter(refs)
        r_vals = [next(it)[...].astype(F32) for _ in range(n_r)]
        a_vals = [next(it)[...].astype(F32) for _ in range(n_a)]
        p_vals = [next(it)[...].astype(F32) for _ in range(n_p)]
        d_vals = [next(it)[...].astype(F32) for _ in range(n_d)]
        add_vals = [next(it)[...].astype(F32) for _ in range(n_add)]
        dr_refs = [next(it) for _ in range(n_r)]
        dp_refs = [next(it) for _ in range(n_p)]

        def f(*rp):
            return tuple(fn(*rp[:n_r], *a_vals, *rp[n_r:]))

        _, vjp = jax.vjp(f, *r_vals, *p_vals)
        grads = list(vjp(tuple(d_vals)))
        for (ri, _), av in zip(adds, add_vals):
            grads[ri] = grads[ri] + av
        for dr_ref, g in zip(dr_refs, grads[:n_r]):
            dr_ref[...] = g.astype(dr_ref.dtype)

        @pl.when(pl.program_id(0) == 0)
        def _():
            for dp_ref in dp_refs:
                dp_ref[...] = jnp.zeros_like(dp_ref)

        for dp_ref, g in zip(dp_refs, grads[n_r:]):
            dp_ref[...] += g

    all_rows = list(rows) + list(aux) + list(douts) + [(arr,) + tuple(rows[ri][1:3]) + (0,) for ri, arr in adds]
    in_specs = ([_row_spec(th, cw, ci) for (_, th, cw, ci) in list(rows) + list(aux)]
                + [_full_spec(p.shape) for p in params]
                + [_row_spec(th, cw, ci) for (_, th, cw, ci) in all_rows[n_r + n_a:]])
    res = pl.pallas_call(
        body, name=name, grid=(n_steps,),
        in_specs=in_specs,
        out_specs=[_row_spec(th, cw, 0) for (_, th, cw, _) in rows] + [_full_spec(p.shape) for p in params],
        out_shape=[jax.ShapeDtypeStruct((n_steps * th, cw), dt) for (_, th, cw, _), dt in zip(rows, row_dtypes)]
        + [jax.ShapeDtypeStruct(p.shape, F32) for p in params],
        compiler_params=_params(("arbitrary",)),
    )(*[r[0] for r in list(rows) + list(aux)], *params, *[r[0] for r in all_rows[n_r + n_a:]])
    return res[:n_r], res[n_r:]


def _sigmoid(x):
    return lax.logistic(x)


def _silu(x):
    return x * _sigmoid(x)


def _rms(x, w=None, n=None):
    n = n or x.shape[-1]
    y = x * lax.rsqrt(jnp.sum(x * x, axis=-1, keepdims=True) * (1.0 / n) + EPS)
    return y if w is None else y * w


def _modulate(x, scale, shift):
    return _rms(x) * (1.0 + scale) + shift


def _softplus(x):
    return jnp.maximum(x, 0.0) + jnp.log1p(jnp.exp(-jnp.abs(x)))


@jax.custom_vjp
def _rot_half64(x):
    lane = lax.broadcasted_iota(jnp.int32, x.shape, 1)
    up = pltpu.roll(x, 96, 1)
    down = pltpu.roll(x, 32, 1)
    return jnp.where(lane < 32, up, jnp.where(lane < 64, down, 0.0))


_rot_half64.defvjp(lambda x: (_rot_half64(x), None), lambda _, g: (_rot_half64(g),))


def _rope128(x, cos_p, sin_p):
    return x * cos_p + _rot_half64(x) * sin_p


def _gdn_pre_fn(qkvc, kab, alog_p, dt_p):
    a = _silu(qkvc)
    qs, ks = [], []
    for h in range(HEADS):
        qh = a[:, HEAD_DIM * h:HEAD_DIM * (h + 1)]
        kh = a[:, 512 + HEAD_DIM * h:512 + HEAD_DIM * (h + 1)]
        qs.append(qh * lax.rsqrt(jnp.sum(qh * qh, axis=-1, keepdims=True) + EPS) * (HEAD_DIM ** -0.5))
        ks.append(kh * lax.rsqrt(jnp.sum(kh * kh, axis=-1, keepdims=True) + EPS))
    lane = lax.broadcasted_iota(jnp.int32, kab.shape, 1)
    g_full = -jnp.exp(alog_p) * _softplus(kab + dt_p)
    b_full = _sigmoid(kab)
    gb = jnp.where((lane >= 64) & (lane < 68), g_full, jnp.where((lane >= 68) & (lane < 72), b_full, 0.0))
    return jnp.concatenate(qs, axis=1), jnp.concatenate(ks, axis=1), a[:, 1024:1536], gb


def _intra_head(q, k, v, g_col, b_col):
    c = CHUNK
    row = lax.broadcasted_iota(jnp.int32, (c, c), 0)
    col = lax.broadcasted_iota(jnp.int32, (c, c), 1)
    incl, strict, eye = row >= col, row > col, row == col
    tri = jnp.where(incl, 1.0, 0.0).astype(F32)
    ident = jnp.where(eye, 1.0, 0.0).astype(F32)
    g_wide = _dot_hi(tri, jnp.broadcast_to(g_col, (c, HEAD_DIM)))
    g_i = g_wide[:, :c]
    g_j = jnp.sum(jnp.where(eye, g_i, 0.0), axis=0, keepdims=True)
    decay = jnp.where(incl, jnp.exp(jnp.where(incl, g_i - g_j, 0.0)), 0.0)
    kk = _bdot(k, k, "nt")
    a_mat = jnp.where(strict, b_col * kk * decay, 0.0)
    x_pow = -a_mat
    inv = ident + x_pow
    for _ in range(5):
        x_pow = _dot_hi(x_pow, x_pow)
        inv = inv + _dot_hi(inv, x_pow)
    e_wide = jnp.exp(g_wide)
    u = _dot_hi(inv, v * b_col)
    wk = _dot_hi(inv, k * b_col * e_wide)
    qk = _bdot(q, k, "nt") * decay
    last = lax.broadcasted_iota(jnp.int32, (c, HEAD_DIM), 0) == c - 1
    g_last = jnp.sum(jnp.where(last, g_wide, 0.0), axis=0, keepdims=True)
    qd = q * e_wide
    kd = k * jnp.exp(g_last - g_wide)
    gl = jnp.broadcast_to(jnp.exp(g_last), (8, HEAD_DIM))
    return u, wk, qd, kd, qk, gl


INTRA_ROWS = (256, 128, 64)

_BNN = (((2,), (1,)), ((0,), (0,)))
_BNT = (((2,), (2,)), ((0,), (0,)))


def _split_bf16(a):
    hi = a.astype(BF16)
    return hi, (a - hi.astype(F32)).astype(BF16)


def _dot3_raw(a, b, dims):
    a_hi, a_lo = _split_bf16(a)
    b_hi, b_lo = _split_bf16(b)
    dot = lambda x_, y_: lax.dot_general(x_, y_, dims, preferred_element_type=F32)
    return dot(a_hi, b_hi) + (dot(a_hi, b_lo) + dot(a_lo, b_hi))


@functools.partial(jax.custom_vjp, nondiff_argnums=(2, 3))
def _dot3(a, b, nt, exact_bwd=True):
    return _dot3_raw(a, b, _BNT if nt else _BNN)


def _dot3_fwd(a, b, nt, exact_bwd):
    return _dot3_raw(a, b, _BNT if nt else _BNN), (a, b)


def _dot3_bwd(nt, exact_bwd, res, g):
    a, b = res
    if exact_bwd:
        dot = _dot3_raw
    else:
        dot = lambda x_, y_, d_: lax.dot_general(x_.astype(BF16), y_.astype(BF16), d_, preferred_element_type=F32)
    if nt:
        return dot(g, b, _BNN), dot(jnp.swapaxes(g, 1, 2), a, _BNN)
    return dot(g, b, _BNT), dot(jnp.swapaxes(a, 1, 2), g, _BNN)


_dot3.defvjp(_dot3_fwd, _dot3_bwd)


@functools.partial(jax.custom_vjp, nondiff_argnums=(2,))
def _bdot_b(a, b, nt):
    return lax.dot_general(a.astype(BF16), b.astype(BF16), _BNT if nt else _BNN, preferred_element_type=F32)


def _bdot_b_fwd(a, b, nt):
    return _bdot_b(a, b, nt), (a, b)


def _bdot_b_bwd(nt, res, g):
    a, b = res
    dot = lambda x_, y_, d_: lax.dot_general(x_.astype(BF16), y_.astype(BF16), d_, preferred_element_type=F32)
    if nt:
        return dot(g, b, _BNN), dot(jnp.swapaxes(g, 1, 2), a, _BNN)
    return dot(g, b, _BNT), dot(jnp.swapaxes(a, 1, 2), g, _BNN)


_bdot_b.defvjp(_bdot_b_fwd, _bdot_b_bwd)


def _intra_batched(q, k, v, g_col, b_col):
    c = CHUNK
    nb = q.shape[0]
    row = lax.broadcasted_iota(jnp.int32, (1, c, c), 1)
    col = lax.broadcasted_iota(jnp.int32, (1, c, c), 2)
    incl, strict, eye = row >= col, row > col, row == col
    tri = jnp.broadcast_to(jnp.where(incl, 1.0, 0.0).astype(F32), (nb, c, c))
    ident = jnp.where(eye, 1.0, 0.0).astype(F32)
    g_wide = _dot3(tri, jnp.broadcast_to(g_col, (nb, c, HEAD_DIM)), False)
    g_i = g_wide[:, :, :c]
    g_j = jnp.sum(jnp.where(eye, g_i, 0.0), axis=1, keepdims=True)
    decay = jnp.where(incl, jnp.exp(jnp.where(incl, g_i - g_j, 0.0)), 0.0)
    kk = _bdot_b(k, k, True)
    a_mat = jnp.where(strict, b_col * kk * decay, 0.0)
    x_pow = -a_mat
    inv = ident + x_pow
    for _ in range(5):
        x_pow = _dot3(x_pow, x_pow, False, False)
        inv = inv + _dot3(inv, x_pow, False, False)
    e_wide = jnp.exp(g_wide)
    u = _dot3(inv, v * b_col, False)
    wk = _dot3(inv, k * b_col * e_wide, False)
    qk = _bdot_b(q, k, True) * decay
    last = lax.broadcasted_iota(jnp.int32, (1, c, HEAD_DIM), 1) == c - 1
    g_last = jnp.sum(jnp.where(last, g_wide, 0.0), axis=1, keepdims=True)
    qd = q * e_wide
    kd = k * jnp.exp(g_last - g_wide)
    gl = jnp.broadcast_to(jnp.exp(g_last), (nb, 8, HEAD_DIM))
    return u, wk, qd, kd, qk, gl


def _gdn_intra_fn(q, k, v, gb):
    t = q.shape[0]
    nch = t // CHUNK
    lane = lax.broadcasted_iota(jnp.int32, gb.shape, 1)

    def heads_first(x_):
        return jnp.concatenate([x_[:, HEAD_DIM * h:HEAD_DIM * (h + 1)].reshape(nch, CHUNK, HEAD_DIM) for h in range(HEADS)],
                               axis=0)

    def column(first_lane):
        return jnp.concatenate([jnp.sum(jnp.where(lane == first_lane + h, gb, 0.0), axis=1, keepdims=True)
                                .reshape(nch, CHUNK, 1) for h in range(HEADS)], axis=0)

    u, wk, qd, kd, qk, gl = _intra_batched(heads_first(q), heads_first(k), heads_first(v), column(64), column(68))

    def rows_first(x_):
        r, w_ = x_.shape[1], x_.shape[2]
        return jnp.concatenate([x_[nch * h:nch * (h + 1)].reshape(nch * r, w_) for h in range(HEADS)], axis=1)

    qks = [qk[nch * h:nch * (h + 1)].reshape(t, CHUNK) for h in range(HEADS)]
    return (rows_first(u), rows_first(wk), rows_first(qd), rows_first(kd), *qks, rows_first(gl))


def _scan_step(s0, u, wk, qd, kd, qk, gl):
    v_new = u - _bdot_b(wk, s0, False)
    o = _bdot_b(qd, s0, False) + _bdot_b(qk, v_new, False)
    s1 = s0 * gl[:, 0:1, :] + _bdot_b(jnp.swapaxes(kd, 1, 2), v_new, False)
    return o, s1


def _mix_post_fn(o_a, z, o_b, gnw, onw):
    parts = [_rms(o_a[:, HEAD_DIM * h:HEAD_DIM * (h + 1)], gnw) * _silu(z[:, HEAD_DIM * h:HEAD_DIM * (h + 1)])
             for h in range(HEADS)]
    parts += [_rms(o_b[:, HEAD_DIM * h:HEAD_DIM * (h + 1)], onw) for h in range(HEADS)]
    return (jnp.concatenate(parts, axis=1),)


def _mla_pre_fn(ckv, cq, kab, cos_p, sin_p, qnw, kvnw, wuq, wukv, qn_w, qr_w, kn_w, kr_w):
    scale = (HEAD_DIM + ROPE) ** -0.5
    qf = _bdot(_rms(cq, qnw), wuq, "nn")
    kvf = _bdot(_rms(ckv, kvnw), wukv, "nn")
    lane = lax.broadcasted_iota(jnp.int32, kab.shape, 1)
    kr = _rope128(_rms(jnp.where(lane < ROPE, kab, 0.0), kr_w, n=ROPE), cos_p, sin_p)
    qs, ks = [], []
    for h in range(HEADS):
        qn = _rms(qf[:, 256 * h:256 * h + 128], qn_w) * scale
        qr = _rope128(_rms(qf[:, 256 * h + 128:256 * h + 256], qr_w, n=ROPE), cos_p, sin_p) * scale
        qs += [qn, qr]
        ks += [_rms(kvf[:, 128 * h:128 * (h + 1)], kn_w), kr]
    return jnp.concatenate(qs, axis=1), jnp.concatenate(ks, axis=1), kvf[:, 512:]


def _conv_fwd(proj, conv_w, tm, name):
    S = proj.shape[0]
    C = 1536
    nb = tm // 8

    def body(x_ref, prev_ref, w_ref, o_ref, ext_ref):
        i = pl.program_id(0)
        ext_ref[0:8, :] = jnp.where(i > 0, prev_ref[...], 0.0)
        ext_ref[8:, :] = x_ref[...]
        acc = jnp.zeros((tm, C), F32)
        for k in range(4):
            acc = acc + w_ref[k:k + 1, :] * ext_ref[pl.ds(5 + k, tm), :]
        o_ref[...] = acc

    return pl.pallas_call(
        body, name=name, grid=(S // tm,),
        in_specs=[pl.BlockSpec((tm, C), lambda i: (i, 0)),
                  pl.BlockSpec((8, C), lambda i: (jnp.maximum(i * nb - 1, 0), 0)),
                  pl.BlockSpec((4, C), lambda i: (0, 0))],
        out_specs=pl.BlockSpec((tm, C), lambda i: (i, 0)),
        out_shape=jax.ShapeDtypeStruct((S, C), F32),
        scratch_shapes=[pltpu.VMEM((tm + 8, C), F32)],
        compiler_params=_params(("arbitrary",)),
    )(proj, proj, conv_w)


def _conv_bwd(proj, dout, conv_w, tm, name):
    S = proj.shape[0]
    C = 1536
    nb = tm // 8
    n_steps = S // tm

    def body(x_ref, prev_ref, d_ref, next_ref, w_ref, dx_ref, dw_ref, xext_ref, dext_ref):
        i = pl.program_id(0)
        xext_ref[0:8, :] = jnp.where(i > 0, prev_ref[...], 0.0)
        xext_ref[8:, :] = x_ref[...]
        dext_ref[0:tm, :] = d_ref[...]
        dext_ref[tm:, :] = jnp.where(i < n_steps - 1, next_ref[...], 0.0)
        d = d_ref[...]
        acc = jnp.zeros((tm, C), F32)
        dws = []
        for k in range(4):
            acc = acc + w_ref[k:k + 1, :] * dext_ref[pl.ds(3 - k, tm), :]
            dws.append(jnp.sum(d * xext_ref[pl.ds(5 + k, tm), :], axis=0, keepdims=True))
        dx_ref[...] = acc

        @pl.when(i == 0)
        def _():
            dw_ref[...] = jnp.zeros_like(dw_ref)

        dw_ref[...] += jnp.concatenate(dws + [jnp.zeros((4, C), F32)], axis=0)

    return pl.pallas_call(
        body, name=name, grid=(n_steps,),
        in_specs=[pl.BlockSpec((tm, C), lambda i: (i, 0)),
                  pl.BlockSpec((8, C), lambda i: (jnp.maximum(i * nb - 1, 0), 0)),
                  pl.BlockSpec((tm, C), lambda i: (i, 0)),
                  pl.BlockSpec((8, C), lambda i: (jnp.minimum((i + 1) * nb, S // 8 - 1), 0)),
                  pl.BlockSpec((4, C), lambda i: (0, 0))],
        out_specs=[pl.BlockSpec((tm, C), lambda i: (i, 0)), pl.BlockSpec((8, C), lambda i: (0, 0))],
        out_shape=[jax.ShapeDtypeStruct((S, C), F32), jax.ShapeDtypeStruct((8, C), F32)],
        scratch_shapes=[pltpu.VMEM((tm + 8, C), F32), pltpu.VMEM((tm + 8, C), F32)],
        compiler_params=_params(("arbitrary",)),
    )(proj, proj, dout, dout, conv_w)


def _gdn_scan_fwd(u, wk, qd, kd, qks, gl, name):
    S = u.shape[0]
    nc = S // CHUNK
    W = HEADS * HEAD_DIM

    def body(u_ref, wk_ref, qd_ref, kd_ref, qk0, qk1, qk2, qk3, gl_ref, o_ref, sp_ref, s_ref):
        @pl.when(pl.program_id(0) == 0)
        def _():
            s_ref[...] = jnp.zeros_like(s_ref)

        s0 = s_ref[...]
        sp_ref[0] = s0
        o, s1 = _scan_step(s0, _heads(u_ref, HEAD_DIM), _heads(wk_ref, HEAD_DIM), _heads(qd_ref, HEAD_DIM),
                           _heads(kd_ref, HEAD_DIM), jnp.stack([r[...] for r in (qk0, qk1, qk2, qk3)]),
                           _heads(gl_ref, HEAD_DIM))
        s_ref[...] = s1
        for h in range(HEADS):
            o_ref[:, HEAD_DIM * h:HEAD_DIM * (h + 1)] = o[h]

    row = pl.BlockSpec((CHUNK, W), lambda n: (n, 0))
    qk_spec = pl.BlockSpec((CHUNK, CHUNK), lambda n: (n, 0))
    return pl.pallas_call(
        body, name=name, grid=(nc,),
        in_specs=[row, row, row, row, qk_spec, qk_spec, qk_spec, qk_spec, pl.BlockSpec((8, W), lambda n: (n, 0))],
        out_specs=[row, pl.BlockSpec((1, HEADS, HEAD_DIM, HEAD_DIM), lambda n: (n, 0, 0, 0))],
        out_shape=[jax.ShapeDtypeStruct((S, W), F32), jax.ShapeDtypeStruct((nc, HEADS, HEAD_DIM, HEAD_DIM), F32)],
        scratch_shapes=[pltpu.VMEM((HEADS, HEAD_DIM, HEAD_DIM), F32)],
        compiler_params=_params(("arbitrary",)),
    )(u, wk, qd, kd, *qks, gl)


def _gdn_scan_bwd(u, wk, qd, kd, qks, gl, s_prev, d_o, name):
    S = u.shape[0]
    nc = S // CHUNK
    W = HEADS * HEAD_DIM

    def body(u_ref, wk_ref, qd_ref, kd_ref, qk0, qk1, qk2, qk3, gl_ref, sp_ref, do_ref,
             du_ref, dwk_ref, dqd_ref, dkd_ref, dqk0, dqk1, dqk2, dqk3, dgl_ref, ds_ref):
        @pl.when(pl.program_id(0) == 0)
        def _():
            ds_ref[...] = jnp.zeros_like(ds_ref)

        _, vjp = jax.vjp(_scan_step, sp_ref[0], _heads(u_ref, HEAD_DIM), _heads(wk_ref, HEAD_DIM), _heads(qd_ref, HEAD_DIM),
                         _heads(kd_ref, HEAD_DIM), jnp.stack([r[...] for r in (qk0, qk1, qk2, qk3)]),
                         _heads(gl_ref, HEAD_DIM))
        ds0, du, dwk, dqd, dkd, dqk, dgl = vjp((_heads(do_ref, HEAD_DIM), ds_ref[...]))
        ds_ref[...] = ds0
        for h, dqk_ref in enumerate((dqk0, dqk1, dqk2, dqk3)):
            sl = slice(HEAD_DIM * h, HEAD_DIM * (h + 1))
            du_ref[:, sl] = du[h]
            dwk_ref[:, sl] = dwk[h]
            dqd_ref[:, sl] = dqd[h]
            dkd_ref[:, sl] = dkd[h]
            dqk_ref[...] = dqk[h]
            dgl_ref[:, sl] = dgl[h]

    rev = lambda n: (nc - 1 - n, 0)
    row = pl.BlockSpec((CHUNK, W), rev)
    qk_spec = pl.BlockSpec((CHUNK, CHUNK), rev)
    gl_spec = pl.BlockSpec((8, W), rev)
    qk_shape = jax.ShapeDtypeStruct((S, CHUNK), F32)
    row_shape = jax.ShapeDtypeStruct((S, W), F32)
    return pl.pallas_call(
        body, name=name, grid=(nc,),
        in_specs=[row, row, row, row, qk_spec, qk_spec, qk_spec, qk_spec, gl_spec,
                  pl.BlockSpec((1, HEADS, HEAD_DIM, HEAD_DIM), lambda n: (nc - 1 - n, 0, 0, 0)), row],
        out_specs=[row, row, row, row, qk_spec, qk_spec, qk_spec, qk_spec, gl_spec],
        out_shape=[row_shape] * 4 + [qk_shape] * 4 + [jax.ShapeDtypeStruct((nc * 8, W), F32)],
        scratch_shapes=[pltpu.VMEM((HEADS, HEAD_DIM, HEAD_DIM), F32)],
        compiler_params=_params(("arbitrary",)),
    )(u, wk, qd, kd, *qks, gl, s_prev, d_o)


NEG = -1e30


def _chunk_mask(i, j, t, transposed=False):
    q_axis, k_axis = (1, 0) if transposed else (0, 1)
    r = (i * t + lax.broadcasted_iota(jnp.int32, (t, t), q_axis)) // CHUNK
    c = (j * t + lax.broadcasted_iota(jnp.int32, (t, t), k_axis)) // CHUNK
    return c <= r


def _tile_pairs(n, by_key):
    pairs = [(i, j) for j in range(n) for i in range(j, n)] if by_key else [(i, j) for i in range(n) for j in range(i + 1)]
    return jnp.asarray(np.array([p[0] for p in pairs], np.int32)), jnp.asarray(np.array([p[1] for p in pairs], np.int32))


def _heads(ref, width):
    return jnp.stack([ref[:, width * h:width * (h + 1)] for h in range(HEADS)])


def _bmm(a, b, dims):
    return lax.dot_general(a.astype(BF16), b.astype(BF16), dims, preferred_element_type=F32)


def _attn_fwd(q, k, v_t, t, name):
    S = q.shape[0]
    n = S // t
    qi, kj = _tile_pairs(n, by_key=False)

    def body(qi_ref, kj_ref, q_ref, k_ref, vt_ref, o_ref, lse_ref, m_ref, l_ref, acc_ref):
        i, j = qi_ref[pl.program_id(0)], kj_ref[pl.program_id(0)]

        @pl.when(j == 0)
        def _():
            m_ref[...] = jnp.full_like(m_ref, NEG)
            l_ref[...] = jnp.zeros_like(l_ref)
            acc_ref[...] = jnp.zeros_like(acc_ref)

        def update(masked):
            s_t = _bmm(_heads(k_ref, 256), _heads(q_ref, 256), _BNT)
            if masked:
                s_t = jnp.where(_chunk_mask(i, j, t, transposed=True)[None], s_t, NEG)
            m_old = m_ref[...]
            m_new = jnp.maximum(m_old, jnp.max(s_t, axis=1, keepdims=True))
            p_t = jnp.exp(s_t - m_new)
            alpha = jnp.exp(m_old - m_new)
            l_ref[...] = alpha * l_ref[...] + jnp.sum(p_t, axis=1, keepdims=True)
            v_heads = jnp.stack([vt_ref[HEAD_DIM * h:HEAD_DIM * (h + 1), :] for h in range(HEADS)])
            acc_ref[...] = alpha * acc_ref[...] + _bmm(v_heads, p_t, _BNN)
            m_ref[...] = m_new

        @pl.when(j < i)
        def _():
            update(False)

        @pl.when(j == i)
        def _():
            update(True)
            for h in range(HEADS):
                sl = slice(HEAD_DIM * h, HEAD_DIM * (h + 1))
                o_ref[:, sl] = jnp.transpose(acc_ref[h] / l_ref[h])
                lse_ref[:, sl] = jnp.transpose(jnp.broadcast_to(m_ref[h] + jnp.log(l_ref[h]), (HEAD_DIM, t)))

    row = lambda p, qi_, kj_: (qi_[p], 0)
    return pl.pallas_call(
        body, name=name,
        grid_spec=pltpu.PrefetchScalarGridSpec(
            num_scalar_prefetch=2, grid=(qi.shape[0],),
            in_specs=[pl.BlockSpec((t, HEADS * 256), row), pl.BlockSpec((t, HEADS * 256), lambda p, qi_, kj_: (kj_[p], 0)),
                      pl.BlockSpec((HEADS * HEAD_DIM, t), lambda p, qi_, kj_: (0, kj_[p]))],
            out_specs=[pl.BlockSpec((t, HEADS * HEAD_DIM), row)] * 2,
            scratch_shapes=[pltpu.VMEM((HEADS, 1, t), F32), pltpu.VMEM((HEADS, 1, t), F32),
                            pltpu.VMEM((HEADS, HEAD_DIM, t), F32)]),
        out_shape=[jax.ShapeDtypeStruct((S, HEADS * HEAD_DIM), F32)] * 2,
        compiler_params=_params(("arbitrary",)),
    )(qi, kj, q, k, v_t)


def _attn_dq(q, k, v, o, lse, d_o, t, name):
    S = q.shape[0]
    n = S // t
    qi, kj = _tile_pairs(n, by_key=False)

    def body(qi_ref, kj_ref, q_ref, k_ref, v_ref, o_ref, lse_ref, do_ref, dq_ref, st_ref, acc_ref, delta_ref):
        i, j = qi_ref[pl.program_id(0)], kj_ref[pl.program_id(0)]

        @pl.when(j == 0)
        def _():
            acc_ref[...] = jnp.zeros_like(acc_ref)
            delta_ref[...] = jnp.sum(_heads(do_ref, HEAD_DIM) * _heads(o_ref, HEAD_DIM), axis=2, keepdims=True)

        def update(masked):
            kh = _heads(k_ref, 256)
            d_out = _heads(do_ref, HEAD_DIM)
            s = _bmm(_heads(q_ref, 256), kh, _BNT)
            p = jnp.exp(s - _heads(lse_ref, HEAD_DIM)[:, :, 0:1])
            if masked:
                p = jnp.where(_chunk_mask(i, j, t)[None], p, 0.0)
            dp = _bmm(d_out, _heads(v_ref, HEAD_DIM), _BNT)
            ds = p * (dp - delta_ref[...])
            acc_ref[...] += _bmm(ds, kh, _BNN)

        @pl.when(j < i)
        def _():
            update(False)

        @pl.when(j == i)
        def _():
            update(True)
            lane = lax.broadcasted_iota(jnp.int32, (t, HEAD_DIM), 1)
            stats = jnp.zeros((t, HEAD_DIM), F32)
            for h in range(HEADS):
                dq_ref[:, 256 * h:256 * (h + 1)] = acc_ref[h]
                stats = stats + jnp.where(lane == h, lse_ref[:, HEAD_DIM * h:HEAD_DIM * (h + 1)], 0.0)
                stats = stats + jnp.where(lane == HEADS + h, delta_ref[h], 0.0)
            st_ref[...] = jnp.transpose(stats)[0:8, :]

    kv = lambda p, qi_, kj_: (kj_[p], 0)
    row = lambda p, qi_, kj_: (qi_[p], 0)
    wide, narrow = pl.BlockSpec((t, HEADS * 256), row), pl.BlockSpec((t, HEADS * HEAD_DIM), row)
    return pl.pallas_call(
        body, name=name,
        grid_spec=pltpu.PrefetchScalarGridSpec(
            num_scalar_prefetch=2, grid=(qi.shape[0],),
            in_specs=[wide, pl.BlockSpec((t, HEADS * 256), kv), pl.BlockSpec((t, HEADS * HEAD_DIM), kv), narrow, narrow, narrow],
            out_specs=[wide, pl.BlockSpec((8, t), lambda p, qi_, kj_: (0, qi_[p]))],
            scratch_shapes=[pltpu.VMEM((HEADS, t, 256), F32), pltpu.VMEM((HEADS, t, 1), F32)]),
        out_shape=[jax.ShapeDtypeStruct((S, HEADS * 256), F32), jax.ShapeDtypeStruct((8, S), F32)],
        compiler_params=_params(("arbitrary",)),
    )(qi, kj, q, k, v, o, lse, d_o)


def _attn_dkv(q, k, v, d_o, stats, t, name):
    S = q.shape[0]
    n = S // t
    qi, kj = _tile_pairs(n, by_key=True)

    def body(qi_ref, kj_ref, q_ref, k_ref, v_ref, do_ref, st_ref, dk_ref, dv_ref, dk_acc, dv_acc):
        i, j = qi_ref[pl.program_id(0)], kj_ref[pl.program_id(0)]

        @pl.when(i == j)
        def _():
            dk_acc[...] = jnp.zeros_like(dk_acc)
            dv_acc[...] = jnp.zeros_like(dv_acc)

        def update(masked):
            qh = _heads(q_ref, 256)
            d_out = _heads(do_ref, HEAD_DIM)
            st = st_ref[...]
            lse_row = jnp.stack([st[h:h + 1, :] for h in range(HEADS)])
            delta_row = jnp.stack([st[HEADS + h:HEADS + h + 1, :] for h in range(HEADS)])
            s_t = _bmm(_heads(k_ref, 256), qh, _BNT)
            p_t = jnp.exp(s_t - lse_row)
            if masked:
                p_t = jnp.where(_chunk_mask(i, j, t, transposed=True)[None], p_t, 0.0)
            dv_acc[...] += _bmm(p_t, d_out, _BNN)
            dp_t = _bmm(_heads(v_ref, HEAD_DIM), d_out, _BNT)
            ds_t = p_t * (dp_t - delta_row)
            dk_acc[...] += _bmm(ds_t, qh, _BNN)

        @pl.when(i == j)
        def _():
            update(True)

        @pl.when(i > j)
        def _():
            update(False)

        @pl.when(i == n - 1)
        def _():
            for h in range(HEADS):
                dk_ref[:, 256 * h:256 * (h + 1)] = dk_acc[h]
                dv_ref[:, HEAD_DIM * h:HEAD_DIM * (h + 1)] = dv_acc[h]

    q_row = lambda p, qi_, kj_: (qi_[p], 0)
    k_row = lambda p, qi_, kj_: (kj_[p], 0)
    return pl.pallas_call(
        body, name=name,
        grid_spec=pltpu.PrefetchScalarGridSpec(
            num_scalar_prefetch=2, grid=(qi.shape[0],),
            in_specs=[pl.BlockSpec((t, HEADS * 256), q_row), pl.BlockSpec((t, HEADS * 256), k_row),
                      pl.BlockSpec((t, HEADS * HEAD_DIM), k_row), pl.BlockSpec((t, HEADS * HEAD_DIM), q_row),
                      pl.BlockSpec((8, t), lambda p, qi_, kj_: (0, qi_[p]))],
            out_specs=[pl.BlockSpec((t, HEADS * 256), k_row), pl.BlockSpec((t, HEADS * HEAD_DIM), k_row)],
            scratch_shapes=[pltpu.VMEM((HEADS, t, 256), F32), pltpu.VMEM((HEADS, t, HEAD_DIM), F32)]),
        out_shape=[jax.ShapeDtypeStruct((S, HEADS * 256), F32), jax.ShapeDtypeStruct((S, HEADS * HEAD_DIM), F32)],
        compiler_params=_params(("arbitrary",)),
    )(qi, kj, q, k, v, d_o, stats)


FFN_PIECE = 2 * D_FF // N_DEV
HID_PIECES = D_FF // FFN_PIECE


def _ffn_up(h, w8, name):
    S = h.shape[0]
    tm = _pick(S, MATMUL_ROWS)

    def body(h_ref, wg_ref, wu_ref, g_ref, u_ref, hid_ref):
        gate = _dot_raw(h_ref[...], wg_ref[...], "nn")
        up = _dot_raw(h_ref[...], wu_ref[...], "nn")
        g_ref[...] = gate.astype(BF16)
        u_ref[...] = up.astype(BF16)
        hid_ref[...] = (_silu(gate) * up).astype(BF16)

    o_spec = pl.BlockSpec((None, tm, FFN_PIECE), lambda i, j: (j, i, 0))
    return pl.pallas_call(
        body, name=name, grid=(S // tm, HID_PIECES),
        in_specs=[pl.BlockSpec((tm, D_MODEL), lambda i, j: (i, 0)),
                  pl.BlockSpec((None, D_MODEL, FFN_PIECE), lambda i, j: (j, 0, 0)),
                  pl.BlockSpec((None, D_MODEL, FFN_PIECE), lambda i, j: (j + HID_PIECES, 0, 0))],
        out_specs=[o_spec] * 3,
        out_shape=[jax.ShapeDtypeStruct((HID_PIECES, S, FFN_PIECE), BF16)] * 3,
        compiler_params=_params(("parallel", "parallel")),
    )(h, w8, w8)


def _after_specs(after):
    return [] if after is None else [pl.BlockSpec(memory_space=pl.ANY)]


def _after_args(after):
    return [] if after is None else [after]


def _ffn_gw8(h, d_gate, d_up, name, after=None):
    S = h.shape[0]
    tm = 512
    tk = _pick(S, (512, 256, 128))
    nk = S // tk

    def body(h_ref, dg_ref, du_ref, *rest):
        o_ref, acc_ref = rest[-2:]
        k = pl.program_id(1)

        @pl.when(k == 0)
        def _():
            acc_ref[...] = jnp.zeros_like(acc_ref)

        h_t = jnp.transpose(h_ref[...])
        for p in range(HID_PIECES):
            acc_ref[p] += _dot_raw(h_t, dg_ref[p], "nn")
            acc_ref[HID_PIECES + p] += _dot_raw(h_t, du_ref[p], "nn")

        @pl.when(k == nk - 1)
        def _():
            o_ref[...] = acc_ref[...].astype(o_ref.dtype)

    d_spec = pl.BlockSpec((HID_PIECES, tk, FFN_PIECE), lambda i, k: (0, k, 0))
    return pl.pallas_call(
        body, name=name, grid=(D_MODEL // tm, nk),
        in_specs=[pl.BlockSpec((tk, tm), lambda i, k: (k, i)), d_spec, d_spec] + _after_specs(after),
        out_specs=pl.BlockSpec((2 * HID_PIECES, tm, FFN_PIECE), lambda i, k: (0, i, 0)),
        out_shape=jax.ShapeDtypeStruct((2 * HID_PIECES, D_MODEL, FFN_PIECE), BF16),
        scratch_shapes=[pltpu.VMEM((2 * HID_PIECES, tm, FFN_PIECE), F32)],
        compiler_params=_params(("parallel", "arbitrary")),
    )(h, d_gate, d_up, *_after_args(after))


def _ffn_dh(d_gate, d_up, w8, x, d_out, scale, shift, name, after=None):
    S = d_gate.shape[1]
    tm = _pick(S, (512, 256, 128))

    def body(dg_ref, du_ref, wg_ref, wu_ref, x_ref, do_ref, sc_ref, sh_ref, *rest):
        dx_ref, dsc_ref, dsh_ref, acc_ref = rest[-4:]
        i, k = pl.program_id(0), pl.program_id(1)

        @pl.when(k == 0)
        def _():
            acc_ref[...] = jnp.zeros_like(acc_ref)

        acc_ref[...] += _dot_raw(dg_ref[...], wg_ref[...], "nt") + _dot_raw(du_ref[...], wu_ref[...], "nt")

        @pl.when((k == 0) & (i == 0))
        def _():
            dsc_ref[...] = jnp.zeros_like(dsc_ref)
            dsh_ref[...] = jnp.zeros_like(dsh_ref)

        @pl.when(k == HID_PIECES - 1)
        def _():
            _, vjp = jax.vjp(_modulate, x_ref[...], sc_ref[...], sh_ref[...])
            dx, dsc, dsh = vjp(acc_ref[...])
            dx_ref[...] = dx + do_ref[...]
            dsc_ref[...] += dsc
            dsh_ref[...] += dsh

    d_spec = pl.BlockSpec((None, tm, FFN_PIECE), lambda i, k: (k, i, 0))
    row = pl.BlockSpec((tm, D_MODEL), lambda i, k: (i, 0))
    par = pl.BlockSpec((1, D_MODEL), lambda i, k: (0, 0))
    return pl.pallas_call(
        body, name=name, grid=(S // tm, HID_PIECES),
        in_specs=[d_spec, d_spec,
                  pl.BlockSpec((None, D_MODEL, FFN_PIECE), lambda i, k: (k, 0, 0)),
                  pl.BlockSpec((None, D_MODEL, FFN_PIECE), lambda i, k: (k + HID_PIECES, 0, 0)),
                  row, row, par, par] + _after_specs(after),
        out_specs=[row, par, par],
        out_shape=[jax.ShapeDtypeStruct((S, D_MODEL), F32), jax.ShapeDtypeStruct((1, D_MODEL), F32),
                   jax.ShapeDtypeStruct((1, D_MODEL), F32)],
        scratch_shapes=[pltpu.VMEM((tm, D_MODEL), F32)],
        compiler_params=_params(("arbitrary", "arbitrary")),
    )(d_gate, d_up, w8, w8, x, d_out, scale, shift, *_after_args(after))


def _swiglu_bwd(d_hid, gate, up):
    sg = _sigmoid(gate)
    return d_hid * up * (sg * (1.0 + gate * (1.0 - sg))), d_hid * (gate * sg)


def _loss_and_grad(x3, target, name):
    S = x3.shape[0]
    tm = _pick(S, (512, 256, 128))
    n = S // tm

    def body(x_ref, t_ref, dx_ref, l_ref):
        i = pl.program_id(0)
        diff = x_ref[...] - t_ref[...]
        dx_ref[...] = diff * (1.0 / D_MODEL)

        @pl.when(i == 0)
        def _():
            l_ref[...] = jnp.zeros_like(l_ref)

        l_ref[...] += jnp.sum(diff * diff, axis=0, keepdims=True)

        @pl.when(i == n - 1)
        def _():
            l_ref[...] = jnp.full(l_ref.shape, (0.5 / D_MODEL) * jnp.sum(l_ref[...]), F32)

    return pl.pallas_call(
        body, name=name, grid=(n,),
        in_specs=[pl.BlockSpec((tm, D_MODEL), lambda i: (i, 0))] * 2,
        out_specs=[pl.BlockSpec((tm, D_MODEL), lambda i: (i, 0)), pl.BlockSpec((1, D_MODEL), lambda i: (0, 0))],
        out_shape=[jax.ShapeDtypeStruct((S, D_MODEL), F32), jax.ShapeDtypeStruct((1, D_MODEL), F32)],
        compiler_params=_params(("arbitrary",)),
    )(x3, target)


def _adamw(w, g, m, v, name):
    R, C = w.shape
    tr = _pick(R, (256, 176, 128, 64, 32, 16, 8))

    def body(w_ref, g_ref, m_ref, v_ref, d_ref, nm_ref, nv_ref):
        g_ = g_ref[...]
        m_ = ADAM_B1 * m_ref[...] + (1.0 - ADAM_B1) * g_
        v_ = ADAM_B2 * v_ref[...] + (1.0 - ADAM_B2) * (g_ * g_)
        m_hat = m_ / (1.0 - ADAM_B1 ** ADAM_STEP)
        v_hat = v_ / (1.0 - ADAM_B2 ** ADAM_STEP)
        d_ref[...] = -ADAM_LR * (m_hat / (jnp.sqrt(v_hat) + ADAM_EPS) + ADAM_WD * w_ref[...])
        nm_ref[...] = m_
        nv_ref[...] = v_

    spec = pl.BlockSpec((tr, C), lambda i: (i, 0))
    return pl.pallas_call(
        body, name=name, grid=(R // tr,),
        in_specs=[spec] * 4, out_specs=[spec] * 3,
        out_shape=[jax.ShapeDtypeStruct((R, C), F32)] * 3,
        compiler_params=_params(("parallel",)),
    )(w, g, m, v)


def _sum_devices(parts, name):
    _, R, C = parts.shape
    tr = _pick(R, (512, 256, 176, 128, 64, 32, 16, 8))

    def body(p_ref, o_ref):
        acc = p_ref[0].astype(F32)
        for d in range(1, N_DEV):
            acc = acc + p_ref[d].astype(F32)
        o_ref[...] = acc

    return pl.pallas_call(
        body, name=name, grid=(R // tr,),
        in_specs=[pl.BlockSpec((N_DEV, tr, C), lambda i: (0, i, 0))],
        out_specs=pl.BlockSpec((tr, C), lambda i: (i, 0)),
        out_shape=jax.ShapeDtypeStruct((R, C), F32),
        compiler_params=_params(("parallel",)),
    )(parts)


def _my_place():
    return lax.axis_index("x"), lax.axis_index("y"), lax.axis_index("c")


def _all_gather(blocks, name):
    n = len(blocks)

    def body(*refs):
        x_refs, out_refs = refs[:n], refs[n:2 * n]
        send_sems, recv_sems, local_sems = refs[2 * n:]
        x, y, c = _my_place()
        me, sibling = (x, y, c), (x, y, 1 - c)
        chips = [(1 - x, y), (x, 1 - y), (1 - x, 1 - y)]

        def copy(a, k, blk, to, own=False):
            slot = out_refs[a].at[4 * blk[0] + 2 * blk[1] + blk[2]]
            return pltpu.make_async_remote_copy(
                src_ref=x_refs[a] if own else slot, dst_ref=slot,
                send_sem=send_sems.at[7 * a + k], recv_sem=recv_sems.at[7 * a + k], device_id=to, device_id_type=MESH)

        mine = [pltpu.make_async_copy(x_refs[a], out_refs[a].at[4 * x + 2 * y + c], local_sems.at[a]) for a in range(n)]
        for cp in mine:
            cp.start()
        first = []
        for j, chip in enumerate(chips):
            first += [copy(a, 1 + j, me, (*chip, c), own=True) for a in range(n)]
        first += [copy(a, 0, me, sibling, own=True) for a in range(n)]
        for cp in first:
            cp.start()
        passed = []
        for j, chip in enumerate(chips):
            for a in range(n):
                copy(a, 1 + j, (*chip, c), me).wait_recv()
                passed.append(copy(a, 4 + j, (*chip, c), sibling))
                passed[-1].start()
        for a in range(n):
            copy(a, 0, sibling, me).wait_recv()
        for j, chip in enumerate(chips):
            for a in range(n):
                copy(a, 4 + j, (*chip, 1 - c), me).wait_recv()
        for cp in first + passed:
            cp.wait_send()
        for cp in mine:
            cp.wait()

    return pl.pallas_call(
        body, name=name,
        out_shape=[jax.ShapeDtypeStruct((N_DEV,) + b.shape, b.dtype) for b in blocks],
        in_specs=[pl.BlockSpec(memory_space=pl.ANY)] * n,
        out_specs=[pl.BlockSpec(memory_space=pl.ANY)] * n,
        scratch_shapes=[pltpu.SemaphoreType.DMA((7 * n,)), pltpu.SemaphoreType.DMA((7 * n,)), pltpu.SemaphoreType.DMA((n,))],
    )(*blocks)


def _all_to_all(pieces, name):
    n = len(pieces)

    def body(*refs):
        x_refs, out_refs = refs[:n], refs[n:2 * n]
        send_sems, recv_sems, local_sems = refs[2 * n:]
        x, y, c = _my_place()
        me = 4 * x + 2 * y + c
        mine = [pltpu.make_async_copy(x_refs[a].at[me], out_refs[a].at[me], local_sems.at[a]) for a in range(n)]
        for cp in mine:
            cp.start()
        copies = []
        for k in (2, 4, 6, 3, 5, 7, 1):
            px = 1 - x if k & 4 else x
            py = 1 - y if k & 2 else y
            pc = 1 - c if k & 1 else c
            peer = 4 * px + 2 * py + pc
            for a in range(n):
                copies.append(pltpu.make_async_remote_copy(
                    src_ref=x_refs[a].at[peer], dst_ref=out_refs[a].at[me],
                    send_sem=send_sems.at[7 * a + k - 1], recv_sem=recv_sems.at[7 * a + k - 1],
                    device_id=(px, py, pc), device_id_type=MESH))
        for cp in copies:
            cp.start()
        for cp in copies:
            cp.wait_recv()
        for cp in copies:
            cp.wait_send()
        for cp in mine:
            cp.wait()

    return pl.pallas_call(
        body, name=name,
        out_shape=[jax.ShapeDtypeStruct(p.shape, p.dtype) for p in pieces],
        in_specs=[pl.BlockSpec(memory_space=pl.ANY)] * n,
        out_specs=[pl.BlockSpec(memory_space=pl.ANY)] * n,
        scratch_shapes=[pltpu.SemaphoreType.DMA((7 * n,)), pltpu.SemaphoreType.DMA((7 * n,)), pltpu.SemaphoreType.DMA((n,))],
    )(*pieces)


def _peers():
    x, y, c = _my_place()
    out = []
    for k in (2, 4, 6, 3, 5, 7, 1):
        px = 1 - x if k & 4 else x
        py = 1 - y if k & 2 else y
        pc = 1 - c if k & 1 else c
        out.append((k, (px, py, pc), 4 * px + 2 * py + pc))
    return out


def _exchange_copies(x_refs, land_refs, send_sems, recv_sems, scatter):
    x, y, c = _my_place()
    me = 4 * x + 2 * y + c
    starts, arrivals = [], []
    for k, place, peer in _peers():
        for a, (x_ref, land_ref) in enumerate(zip(x_refs, land_refs)):
            sems = dict(send_sem=send_sems.at[7 * a + k - 1], recv_sem=recv_sems.at[7 * a + k - 1],
                        device_id=place, device_id_type=MESH)
            src = x_ref.at[peer] if scatter else x_ref
            starts.append(pltpu.make_async_remote_copy(src_ref=src, dst_ref=land_ref.at[me], **sems))
            arrivals.append(pltpu.make_async_remote_copy(src_ref=src, dst_ref=land_ref.at[peer], **sems))
    return starts, arrivals


def _exchange_start(arrays, scatter, name):
    n = len(arrays)
    hbm = pl.BlockSpec(memory_space=pltpu.HBM)
    sem = pl.BlockSpec(memory_space=pltpu.SEMAPHORE)
    lands = [lax.empty(a.shape if scatter else (N_DEV,) + a.shape, a.dtype) for a in arrays]

    def body(*refs):
        x_refs, land_refs = refs[:n], refs[n:2 * n]
        send_sems, recv_sems = refs[2 * n], refs[2 * n + 1]
        token = refs[-1]
        starts, _ = _exchange_copies(x_refs, land_refs, send_sems, recv_sems, scatter)
        for cp in starts:
            cp.start()
        token[...] = jnp.zeros_like(token)

    res = pl.pallas_call(
        body, name=name,
        out_shape=(pltpu.SemaphoreType.DMA((7 * n,)), pltpu.SemaphoreType.DMA((7 * n,)),
                   *[pltpu.HBM(a.shape, a.dtype) for a in arrays], *[pltpu.HBM(l.shape, l.dtype) for l in lands],
                   jax.ShapeDtypeStruct((8, 128), F32)),
        in_specs=[hbm] * (2 * n),
        out_specs=(sem, sem, *[hbm] * (2 * n), pl.BlockSpec(memory_space=pltpu.VMEM)),
        input_output_aliases={i: 2 + i for i in range(2 * n)},
        compiler_params=pltpu.CompilerParams(has_side_effects=pltpu.SideEffectType.DATAFLOW_SIDE_EFFECTING),
    )(*[pltpu.with_memory_space_constraint(a, pltpu.HBM) for a in arrays],
      *[pltpu.with_memory_space_constraint(l, pltpu.HBM) for l in lands])
    return res[0], res[1], list(res[2:2 + n]), list(res[2 + n:2 + 2 * n]), res[-1]


def _exchange_wait(handles, scatter, after, name):
    send_sems, recv_sems, arrays, lands, _ = handles
    n = len(arrays)
    hbm = pl.BlockSpec(memory_space=pltpu.HBM)
    sem = pl.BlockSpec(memory_space=pltpu.SEMAPHORE)

    def body(*refs):
        x_refs, land_refs = refs[:n], refs[n:2 * n]
        send_s, recv_s = refs[2 * n], refs[2 * n + 1]
        starts, arrivals = _exchange_copies(x_refs, land_refs, send_s, recv_s, scatter)
        for cp in arrivals:
            cp.wait_recv()
        for cp in starts:
            cp.wait_send()

    res = pl.pallas_call(
        body, name=name,
        out_shape=(*[pltpu.HBM(a.shape, a.dtype) for a in arrays], *[pltpu.HBM(l.shape, l.dtype) for l in lands]),
        in_specs=[hbm] * (2 * n) + [sem, sem, pl.BlockSpec(memory_space=pl.ANY)],
        out_specs=tuple([hbm] * (2 * n)),
        input_output_aliases={i: i for i in range(2 * n)},
        compiler_params=pltpu.CompilerParams(has_side_effects=pltpu.SideEffectType.DATAFLOW_SIDE_EFFECTING),
    )(*arrays, *lands, send_sems, recv_sems, after)
    me = 4 * lax.axis_index("x") + 2 * lax.axis_index("y") + lax.axis_index("c")
    out = []
    for src, got in zip(res[:n], res[n:]):
        zeros = (0,) * (got.ndim - 1)
        own = lax.dynamic_slice(src, (me,) + zeros, (1,) + src.shape[1:]) if scatter else src[None]
        out.append(lax.dynamic_update_slice(got, own, (me,) + zeros))
    return out


def _pad_lanes(v, at=0, width=128):
    return jnp.pad(v, ((0, 0), (at, width - at - v.shape[1])))


def _pack_weights(P):
    W = {}
    w = P["w_in"]
    W["wp"] = jnp.concatenate([w[:, :2048], w[:, 2440:2696], w[:, 2056:2440], w[:, 2696:2760], w[:, 2048:2056],
                               jnp.zeros((D_MODEL, N_IN_PACKED - N_IN), w.dtype)], axis=1).astype(BF16)
    W["conv_w"] = P["gdn_conv_w"].astype(F32)
    W["alog_p"] = _pad_lanes(P["gdn_a_log"], 64)
    W["dt_p"] = _pad_lanes(P["gdn_dt_bias"], 64)
    W["gnw"] = P["gdn_norm_w"]
    W["qnw"] = P["mla_q_norm_w"]
    W["kvnw"] = P["mla_kv_norm_w"]
    uq = P["mla_w_uq"].reshape(Q_LORA, HEADS, HEAD_DIM + ROPE)
    W["wuq"] = jnp.pad(uq, ((0, 0), (0, 0), (0, 256 - HEAD_DIM - ROPE))).reshape(Q_LORA, HEADS * 256).astype(BF16)
    ukv = P["mla_w_ukv"].reshape(KV_LORA, HEADS, 2, HEAD_DIM)
    W["wukv"] = ukv.transpose(0, 2, 1, 3).reshape(KV_LORA, 2 * HEADS * HEAD_DIM).astype(BF16)
    W["qn_w"] = P["qkn_q_nope"]
    W["qr_w"] = _pad_lanes(P["qkn_q_rope"])
    W["kn_w"] = P["qkn_k_nope"]
    W["kr_w"] = _pad_lanes(P["qkn_k_rope"])
    W["onw"] = P["mla_out_norm_w"]
    W["wout"] = P["w_out"].astype(BF16)
    return W


def _unpack_grads(G):
    g = G["wp"]
    uq = G["wuq"].reshape(Q_LORA, HEADS, 256)[:, :, :HEAD_DIM + ROPE].reshape(Q_LORA, HEADS * (HEAD_DIM + ROPE))
    ukv = G["wukv"].reshape(KV_LORA, 2, HEADS, HEAD_DIM).transpose(0, 2, 1, 3).reshape(KV_LORA, 2 * HEADS * HEAD_DIM)
    return {
        "w_in": jnp.concatenate([g[:, :2048], g[:, 2752:2760], g[:, 2304:2688], g[:, 2048:2304], g[:, 2688:2752]], axis=1),
        "gdn_conv_w": G["conv_w"], "gdn_a_log": G["alog_p"][:, 64:68], "gdn_dt_bias": G["dt_p"][:, 64:68],
        "gdn_norm_w": G["gnw"], "mla_q_norm_w": G["qnw"], "mla_w_uq": uq, "mla_kv_norm_w": G["kvnw"], "mla_w_ukv": ukv,
        "qkn_q_nope": G["qn_w"], "qkn_q_rope": G["qr_w"][:, :ROPE], "qkn_k_nope": G["kn_w"], "qkn_k_rope": G["kr_w"][:, :ROPE],
        "mla_out_norm_w": G["onw"], "w_out": G["wout"],
    }


def _rope_tables(positions):
    half = ROPE // 2
    inv_freq = ROPE_BASE ** (-jnp.arange(half, dtype=F32) / half)
    ang = positions.astype(F32)[:, None] * inv_freq
    cos, sin = jnp.cos(ang), jnp.sin(ang)
    zeros = jnp.zeros((positions.shape[0], 128 - ROPE), F32)
    return jnp.concatenate([cos, cos, zeros], axis=1), jnp.concatenate([-sin, sin, zeros], axis=1)


def _mod_fn(x, scale, shift):
    return (_modulate(x, scale, shift),)


def _ffn_fwd(x, scale, shift, gate_w, w8, wo4, tag):
    S = x.shape[0]
    tm = _pick(S, (512, 256, 128))
    (h,) = _rowwise(_mod_fn, [(x, tm, D_MODEL, 0)], [scale, shift], [(tm, D_MODEL, BF16)], S // tm, tag + "_mod")
    gate, up, hid = _ffn_up(h, w8, tag + "_up")
    tb = _pick(S, MATMUL_ROWS)
    mn = pl.BlockSpec((tb, D_MODEL), lambda i, j, k: (i, j))
    f, x_out = _mmg(hid, wo4, "nn", name=tag + "_down", grid=(S // tb, 1, HID_PIECES),
                    a_spec=pl.BlockSpec((None, tb, FFN_PIECE), lambda i, j, k: (k, i, 0)),
                    b_spec=pl.BlockSpec((None, FFN_PIECE, D_MODEL), lambda i, j, k: (k, 0, j)),
                    out_spec=mn, out_shapes=[jax.ShapeDtypeStruct((S, D_MODEL), F32)] * 2, acc_shape=(tb, D_MODEL),
                    extras=[x, gate_w], extra_specs=[mn, pl.BlockSpec((1, D_MODEL), lambda i, j, k: (0, j))],
                    epi=lambda acc, x_, g_: (acc, x_ + 0.5 * g_ * acc))
    return x_out, (h, gate, up, hid, f)


def _ffn_bwd(d_out, x, scale, shift, gate_w, w8, wo4, saved, tag, grad_ready):
    h, gate, up, hid, f = saved
    S = x.shape[0]
    tm = _pick(S, (512, 256, 128))
    tk = _pick(S, (512, 256, 128))
    n = S // tm
    (df,), (d_gate_w,) = _rowwise_bwd(lambda f_, g_: (0.5 * g_ * f_,), [(f, tm, D_MODEL, 0)], [], [gate_w],
                                      [(d_out, tm, D_MODEL, 0)], n, tag + "_dres", row_dtypes=(BF16,))
    tb = _pick(S, MATMUL_ROWS)
    piece = pl.BlockSpec((None, tb, FFN_PIECE), lambda i, j, k: (j, i, 0))
    d_gate, d_up = _mmg(df, wo4, "nt", name=tag + "_ddown", grid=(S // tb, HID_PIECES, 1),
                        a_spec=pl.BlockSpec((tb, D_MODEL), lambda i, j, k: (i, 0)),
                        b_spec=pl.BlockSpec((None, FFN_PIECE, D_MODEL), lambda i, j, k: (j, 0, 0)),
                        out_spec=piece, out_shapes=[jax.ShapeDtypeStruct((HID_PIECES, S, FFN_PIECE), BF16)] * 2,
                        acc_shape=(tb, FFN_PIECE), extras=[gate, up], extra_specs=[piece, piece], epi=_swiglu_bwd)
    g_wo4 = _mmg(hid, df, "tn", name=tag + "_gwo", grid=(HID_PIECES, 1, S // tk),
                 a_spec=pl.BlockSpec((None, tk, FFN_PIECE), lambda i, j, k: (i, k, 0)),
                 b_spec=pl.BlockSpec((tk, D_MODEL), lambda i, j, k: (k, j)),
                 out_spec=pl.BlockSpec((None, FFN_PIECE, D_MODEL), lambda i, j, k: (i, 0, j)),
                 out_shapes=[jax.ShapeDtypeStruct((HID_PIECES, FFN_PIECE, D_MODEL), BF16)], acc_shape=(FFN_PIECE, D_MODEL))
    g_w8 = _ffn_gw8(h, d_gate, d_up, tag + "_gw8", after=grad_ready("wo4", g_wo4))
    dx, d_scale, d_shift = _ffn_dh(d_gate, d_up, w8, x, d_out, scale, shift, tag + "_dh", after=grad_ready("w8", g_w8))
    return dx, d_scale, d_shift, d_gate_w


def _mixer_fwd(x1, scale, shift, gate_w, cos_p, sin_p, W):
    S = x1.shape[0]
    tm = _pick(S, (512, 256, 128))
    tv = _pick(S, (256, 128))
    ta = _pick(S, (512, 256, 128))
    nc = S // CHUNK
    (h2,) = _rowwise(_mod_fn, [(x1, tm, D_MODEL, 0)], [scale, shift], [(tm, D_MODEL, BF16)], S // tm, "mix_mod")
    proj = _mm(h2, W["wp"], "nn", name="mix_proj")
    qkvc = _conv_fwd(proj, W["conv_w"], tv, "gdn_conv")
    kab = (proj, tv, 128, 21)
    q_a, k_a, v_a, gb = _rowwise(_gdn_pre_fn, [(qkvc, tv, 1536, 0), kab], [W["alog_p"], W["dt_p"]],
                                 [(tv, 512, F32)] * 3 + [(tv, 128, F32)], S // tv, "gdn_pre")
    ti = _pick(S, INTRA_ROWS)
    intra = _rowwise(_gdn_intra_fn, [(q_a, ti, 512, 0), (k_a, ti, 512, 0), (v_a, ti, 512, 0), (gb, ti, 128, 0)],
                     [], [(ti, 512, F32)] * 4 + [(ti, CHUNK, F32)] * 4 + [(ti // 8, 512, F32)], S // ti, "gdn_intra")
    u, wk, qd, kd, qks, gl = intra[0], intra[1], intra[2], intra[3], tuple(intra[4:8]), intra[8]
    o_a, s_prev = _gdn_scan_fwd(u, wk, qd, kd, qks, gl, "gdn_scan")
    mla_params = [W["qnw"], W["kvnw"], W["wuq"], W["wukv"], W["qn_w"], W["qr_w"], W["kn_w"], W["kr_w"]]
    def mla_pre_with_vt(*a):
        q_, k_, v_ = _mla_pre_fn(*a)
        return q_, k_, v_, jnp.transpose(v_)

    q_b, k_b, v_b, vt_b = _rowwise(mla_pre_with_vt,
                                   [(proj, tv, 256, 8), (proj, tv, 384, 6), kab, (cos_p, tv, 128, 0), (sin_p, tv, 128, 0)],
                                   mla_params, [(tv, 1024, BF16), (tv, 1024, BF16), (tv, 512, BF16), (512, tv, BF16, "across")],
                                   S // tv, "mla_pre")
    o_b, lse = _attn_fwd(q_b, k_b, vt_b, ta, "mla_attn")
    (mixed,) = _rowwise(_mix_post_fn, [(o_a, tv, 512, 0), (proj, tv, 512, 3), (o_b, tv, 512, 0)], [W["gnw"], W["onw"]],
                        [(tv, D_MODEL, BF16)], S // tv, "mix_post")
    y, x2 = _mm(mixed, W["wout"], "nn", name="mix_out", out_dtypes=(F32, F32), extras=[x1], extra_params=[gate_w],
                epi=lambda acc, x_, g_: (acc, x_ + g_ * acc))
    saved = (h2, proj, qkvc, q_a, k_a, v_a, gb, u, wk, qd, kd, qks, gl, s_prev, o_a, q_b, k_b, v_b, o_b, lse, mixed, y)
    return x2, saved


def _mixer_bwd(d_out, x1, scale, shift, gate_w, cos_p, sin_p, W, saved):
    (h2, proj, qkvc, q_a, k_a, v_a, gb, u, wk, qd, kd, qks, gl, s_prev, o_a, q_b, k_b, v_b, o_b, lse, mixed, y) = saved
    S = x1.shape[0]
    tm = _pick(S, (512, 256, 128))
    tv = _pick(S, (256, 128))
    ta = _pick(S, (512, 256, 128))
    nc = S // CHUNK
    G = {}
    (dy,), (G["g2"],) = _rowwise_bwd(lambda y_, g_: (g_ * y_,), [(y, tm, D_MODEL, 0)], [], [gate_w],
                                     [(d_out, tm, D_MODEL, 0)], S // tm, "mix_dres", row_dtypes=(BF16,))
    d_mixed = _mm(dy, W["wout"], "nt", name="mix_dout")
    G["wout"] = _mm(mixed, dy, "tn", name="mix_gwout")
    (do_a, dz, do_b), (G["gnw"], G["onw"]) = _rowwise_bwd(
        _mix_post_fn, [(o_a, tv, 512, 0), (proj, tv, 512, 3), (o_b, tv, 512, 0)], [], [W["gnw"], W["onw"]],
        [(d_mixed, tv, D_MODEL, 0)], S // tv, "mix_dpost")
    dq_b, stats = _attn_dq(q_b, k_b, v_b, o_b, lse, do_b, ta, "mla_dq")
    dk_b, dv_b = _attn_dkv(q_b, k_b, v_b, do_b, stats, ta, "mla_dkv")
    kab = (proj, tv, 128, 21)
    mla_params = [W["qnw"], W["kvnw"], W["wuq"], W["wukv"], W["qn_w"], W["qr_w"], W["kn_w"], W["kr_w"]]
    (d_ckv, d_cq, d_kab), mla_grads = _rowwise_bwd(
        _mla_pre_fn, [(proj, tv, 256, 8), (proj, tv, 384, 6), kab], [(cos_p, tv, 128, 0), (sin_p, tv, 128, 0)], mla_params,
        [(dq_b, tv, 1024, 0), (dk_b, tv, 1024, 0), (dv_b, tv, 512, 0)], S // tv, "mla_dpre")
    for key, g in zip(("qnw", "kvnw", "wuq", "wukv", "qn_w", "qr_w", "kn_w", "kr_w"), mla_grads):
        G[key] = g
    scan_grads = _gdn_scan_bwd(u, wk, qd, kd, qks, gl, s_prev, do_a, "gdn_dscan")
    ti = _pick(S, INTRA_ROWS)
    intra_douts = [(scan_grads[i], ti, 512, 0) for i in range(4)] + [(scan_grads[4 + i], ti, CHUNK, 0) for i in range(4)]
    intra_douts.append((scan_grads[8], ti // 8, 512, 0))
    (dq_a, dk_a, dv_a, d_gb), _ = _rowwise_bwd(
        _gdn_intra_fn, [(q_a, ti, 512, 0), (k_a, ti, 512, 0), (v_a, ti, 512, 0), (gb, ti, 128, 0)], [], [],
        intra_douts, S // ti, "gdn_dintra")
    (d_qkvc, d_kab), (G["alog_p"], G["dt_p"]) = _rowwise_bwd(
        _gdn_pre_fn, [(qkvc, tv, 1536, 0), kab], [], [W["alog_p"], W["dt_p"]],
        [(dq_a, tv, 512, 0), (dk_a, tv, 512, 0), (dv_a, tv, 512, 0), (d_gb, tv, 128, 0)], S // tv, "gdn_dpre",
        adds=[(1, d_kab)])
    d_qkv, g_conv = _conv_bwd(proj, d_qkvc, W["conv_w"], tv, "gdn_dconv")
    G["conv_w"] = g_conv[:4]
    d_proj = jnp.concatenate([d_qkv, dz, d_ckv, d_cq, d_kab], axis=1).astype(BF16)
    G["wp"] = _mm(h2, d_proj, "tn", name="mix_gwp")
    dh2 = _mm(d_proj, W["wp"], "nt", name="mix_dproj")
    (dx1,), (G["s2"], G["sh2"]) = _rowwise_bwd(_mod_fn, [(x1, tm, D_MODEL, 0)], [], [scale, shift],
                                               [(dh2, tm, D_MODEL, 0)], S // tm, "mix_dmod", adds=[(0, d_out)])
    return dx1, G


def _local_step(x, target, mod, cos_p, sin_p, W1, later_weights, ffn_grad_ready, mixer_grads_ready):
    sh1, s1, g1, sh2, s2, g2, sh3, s3, g3 = [mod[:, D_MODEL * i:D_MODEL * (i + 1)] for i in range(N_MOD)]
    x1, saved1 = _ffn_fwd(x, s1, sh1, g1, W1["f1_w8"], W1["f1_wo4"], "ffn1")
    W = later_weights(x1)
    x2, saved2 = _mixer_fwd(x1, s2, sh2, g2, cos_p, sin_p, W)
    x3, saved3 = _ffn_fwd(x2, s3, sh3, g3, W["f2_w8"], W["f2_wo4"], "ffn2")
    dx3, loss_row = _loss_and_grad(x3, target, "loss")
    dx2, d_s3, d_sh3, d_g3 = _ffn_bwd(dx3, x2, s3, sh3, g3, W["f2_w8"], W["f2_wo4"], saved3, "ffn2", ffn_grad_ready("f2"))
    dx1, G = _mixer_bwd(dx2, x1, s2, sh2, g2, cos_p, sin_p, W, saved2)
    d_sh2, d_s2, d_g2 = G.pop("sh2"), G.pop("s2"), G.pop("g2")
    g1 = g1 + mixer_grads_ready(G)
    dx, d_s1, d_sh1, d_g1 = _ffn_bwd(dx1, x, s1, sh1, g1, W1["f1_w8"], W1["f1_wo4"], saved1, "ffn1", ffn_grad_ready("f1"))
    d_mod = jnp.concatenate([d_sh1, d_s1, d_g1, d_sh2, d_s2, d_g2, d_sh3, d_s3, d_g3], axis=1)
    return loss_row, dx, d_mod


WEIGHT_NAMES = ("w_ada", "b_ada", "ffn1_w_in", "ffn1_w_out", "w_in", "gdn_conv_w", "gdn_a_log", "gdn_dt_bias", "gdn_norm_w",
                "mla_q_norm_w", "mla_w_uq", "mla_kv_norm_w", "mla_w_ukv", "qkn_q_nope", "qkn_q_rope", "qkn_k_nope",
                "qkn_k_rope", "mla_out_norm_w", "w_out", "ffn2_w_in", "ffn2_w_out")
FFN_SHARDED = ("ffn1_w_in", "ffn1_w_out", "ffn2_w_in", "ffn2_w_out")
SHEETED = (("w_in", "col"), ("gdn_conv_w", "col"), ("mla_w_uq", "col"), ("mla_w_ukv", "col"), ("w_out", "row"))
MOD_ROWS = N_MOD * D_MODEL // 128
SMALL = {"gdn_a_log": (MOD_ROWS, 1, 64, 4), "gdn_dt_bias": (MOD_ROWS + 1, 1, 64, 4), "gdn_norm_w": (MOD_ROWS + 2, 1, 0, 128),
         "mla_q_norm_w": (MOD_ROWS + 3, 3, 0, 384), "mla_kv_norm_w": (MOD_ROWS + 6, 2, 0, 256),
         "qkn_q_nope": (MOD_ROWS + 8, 1, 0, 128), "qkn_q_rope": (MOD_ROWS + 9, 1, 0, 64), "qkn_k_nope": (MOD_ROWS + 10, 1, 0, 128),
         "qkn_k_rope": (MOD_ROWS + 11, 1, 0, 64), "mla_out_norm_w": (MOD_ROWS + 12, 1, 0, 128)}
LOSS_ROW = MOD_ROWS + 13
SHEET_ROWS = 88


def _to_sheet(flat, dtype, sublanes):
    n = flat.shape[-1]
    unit = sublanes * 128
    pad = (-n) % unit
    flat = jnp.pad(flat.astype(dtype), [(0, 0)] * (flat.ndim - 1) + [(0, pad)])
    return flat.reshape(flat.shape[:-1] + ((n + pad) // 128, 128))


def _small_sheet(b_like, small):
    sheet = jnp.zeros((SHEET_ROWS, 128), F32).at[:MOD_ROWS].set(b_like.reshape(MOD_ROWS, 128))
    for name, (row, rows, lane, n) in SMALL.items():
        v = small[name].reshape(1, n)
        if rows == 1:
            sheet = sheet.at[row, lane:lane + n].set(v[0])
        else:
            sheet = sheet.at[row:row + rows].set(v.reshape(rows, 128))
    return sheet


def _from_small_sheet(sheet):
    out = {"b_ada": sheet[:MOD_ROWS].reshape(1, N_MOD * D_MODEL)}
    for name, (row, rows, lane, n) in SMALL.items():
        out[name] = sheet[row, lane:lane + n].reshape(1, n) if rows == 1 else sheet[row:row + rows].reshape(1, n)
    return out


def kernel(x, c, positions, w_ada, b_ada, ffn1_w_in, ffn1_w_out, w_in, gdn_conv_w, gdn_a_log, gdn_dt_bias, gdn_norm_w, mla_q_norm_w, mla_w_uq, mla_kv_norm_w, mla_w_ukv, qkn_q_nope, qkn_q_rope, qkn_k_nope, qkn_k_rope, mla_out_norm_w, w_out, ffn2_w_in, ffn2_w_out, loss_target, m_w_ada, m_b_ada, m_ffn1_w_in, m_ffn1_w_out, m_w_in, m_gdn_conv_w, m_gdn_a_log, m_gdn_dt_bias, m_gdn_norm_w, m_mla_q_norm_w, m_mla_w_uq, m_mla_kv_norm_w, m_mla_w_ukv, m_qkn_q_nope, m_qkn_q_rope, m_qkn_k_nope, m_qkn_k_rope, m_mla_out_norm_w, m_w_out, m_ffn2_w_in, m_ffn2_w_out, v_w_ada, v_b_ada, v_ffn1_w_in, v_ffn1_w_out, v_w_in, v_gdn_conv_w, v_gdn_a_log, v_gdn_dt_bias, v_gdn_norm_w, v_mla_q_norm_w, v_mla_w_uq, v_mla_kv_norm_w, v_mla_w_ukv, v_qkn_q_nope, v_qkn_q_rope, v_qkn_k_nope, v_qkn_k_rope, v_mla_out_norm_w, v_w_out, v_ffn2_w_in, v_ffn2_w_out):
    args = locals()
    w = {n: args[n] for n in WEIGHT_NAMES}
    m = {n: args["m_" + n] for n in WEIGHT_NAMES}
    v = {n: args["v_" + n] for n in WEIGHT_NAMES}
    me = 4 * lax.axis_index("x") + 2 * lax.axis_index("y") + lax.axis_index("c")
    cols = N_MOD * D_MODEL // N_DEV
    shard = {n: w[n][0] for n in FFN_SHARDED + tuple(s[0] for s in SHEETED)}

    sc = c * _sigmoid(c)
    first = _to_sheet(jnp.concatenate([sc.reshape(-1), shard["gdn_conv_w"].reshape(-1)]), F32, 8)
    (first_all,) = _all_gather([first], "gather_c")
    sc_all = first_all[:, :D_MODEL // 128].reshape(N_DEV, D_MODEL)
    n_taps = shard["gdn_conv_w"].size
    conv_all = first_all.reshape(N_DEV, -1)[:, D_MODEL:D_MODEL + n_taps].reshape(N_DEV, 4, -1)
    b_mine = lax.dynamic_slice(b_ada, (0, me * cols), (1, cols))
    mod_cols = _mm(sc_all, w_ada[0], "nn", name="ada_mod", extra_params=[b_mine], epi=lambda acc, b_: (acc + b_,))
    (mod_all,) = _all_to_all([_to_sheet(mod_cols, F32, 8)], "scatter_mod")
    mod = mod_all.reshape(N_DEV, -1)[:, :cols].reshape(1, N_MOD * D_MODEL)

    f1_w8, f1_out = _all_gather([shard["ffn1_w_in"].astype(BF16), shard["ffn1_w_out"].astype(BF16)], "gather_w1")
    travel = [s for s in SHEETED if s[0] != "gdn_conv_w"]
    flat_w = jnp.concatenate([shard[n].reshape(-1).astype(BF16) for n, _ in travel])
    w_sheet, f1_w8 = lax.optimization_barrier((_to_sheet(flat_w, BF16, 16), f1_w8))
    later = _exchange_start([shard["ffn2_w_in"].astype(BF16), shard["ffn2_w_out"].astype(BF16), w_sheet],
                            False, "gather_w2_start")
    mod = mod + later[4][0:1, 0:1]
    W1 = dict(f1_w8=f1_w8, f1_wo4=f1_out.reshape(HID_PIECES, FFN_PIECE, D_MODEL))

    def later_weights(after):
        f2_w8, f2_out, sheet_all = _exchange_wait(later, False, after, "gather_w2_wait")
        w_all = sheet_all.reshape(N_DEV, -1)
        P, off = {}, 0
        for n, kind in travel:
            r, cc = shard[n].shape
            piece = w_all[:, off:off + r * cc].reshape(N_DEV, r, cc)
            P[n] = jnp.concatenate(list(piece), axis=1) if kind == "col" else piece.reshape(N_DEV * r, cc)
            off += r * cc
        P["gdn_conv_w"] = jnp.concatenate(list(conv_all), axis=1)
        for n in SMALL:
            P[n] = w[n]
        W = _pack_weights(P)
        W.update(f2_w8=f2_w8, f2_wo4=f2_out.reshape(HID_PIECES, FFN_PIECE, D_MODEL))
        return W

    pending, small_grads = {}, {}

    def ffn_grad_ready(tag):
        def ready(which, g):
            pieces = g if which == "w8" else g.reshape((N_DEV,) + shard["ffn1_w_out"].shape)
            pending[tag + which] = _exchange_start([pieces], True, "scatter_%s_%s_start" % (tag, which))
            return pending[tag + which][4]
        return ready

    def mixer_grads_ready(G):
        g_full = _unpack_grads(G)
        small_grads.update({n: g_full[n] for n in SMALL})
        pieces = []
        for n, kind in SHEETED:
            r, cc = shard[n].shape
            g = g_full[n]
            pieces.append(jnp.stack([g[:, cc * p:cc * (p + 1)].reshape(-1) for p in range(N_DEV)]) if kind == "col"
                          else g.reshape(N_DEV, r * cc))
        pending["mixer"] = _exchange_start([_to_sheet(jnp.concatenate(pieces, axis=1), BF16, 512)], True, "scatter_mx_start")
        return pending["mixer"][4][0:1, 0:1]

    cos_p, sin_p = _rope_tables(positions[0])
    loss_row, dx, d_mod = _local_step(x[0], loss_target[0], mod, cos_p, sin_p, W1, later_weights, ffn_grad_ready,
                                      mixer_grads_ready)

    sheet = _small_sheet(d_mod, small_grads).at[LOSS_ROW].set(loss_row[0, :128])
    (sheets,) = _all_gather([sheet], "gather_small")
    summed = _sum_devices(sheets, "sum_small")
    d_mod_all = sheets[:, :MOD_ROWS].reshape(N_DEV, N_MOD * D_MODEL)
    d_mod_mine = lax.dynamic_slice(d_mod_all, (0, me * cols), (N_DEV, cols))
    grads = _from_small_sheet(summed)
    grads["w_ada"] = _mm(sc_all, d_mod_mine, "tn", name="ada_gw", hi=True)
    loss = summed[LOSS_ROW, 0]

    for n, key in zip(FFN_SHARDED, ("f1w8", "f1wo4", "f2w8", "f2wo4")):
        (parts,) = _exchange_wait(pending[key], True, summed, "scatter_%s_wait" % key)
        grads[n] = _sum_devices(parts, "sum_" + n)
    (mixer_parts,) = _exchange_wait(pending["mixer"], True, summed, "scatter_mx_wait")
    g_mine = _sum_devices(mixer_parts, "sum_grads").reshape(-1)
    off = 0
    for n, _ in SHEETED:
        grads[n] = g_mine[off:off + shard[n].size].reshape(shard[n].shape)
        off += shard[n].size

    delta, new_m, new_v = {}, {}, {}
    for n in ("w_ada",) + FFN_SHARDED + tuple(s[0] for s in SHEETED):
        delta[n], new_m[n], new_v[n] = _adamw(w[n][0], grads[n], m[n][0], v[n][0], "adamw_" + n)
    small_in = [_small_sheet(t["b_ada"], t) for t in (w, grads, m, v)]
    for res, out in zip(_adamw(*small_in, "adamw_small"), (delta, new_m, new_v)):
        out.update(_from_small_sheet(res))

    def shaped(d):
        return [d[n].reshape(w[n].shape) for n in WEIGHT_NAMES]

    return (loss, dx[None], *shaped(grads), *shaped(delta), *shaped(new_m), *shaped(new_v))
```

```python
import functools

import jax
import jax.numpy as jnp
import numpy as np
from jax import lax
from jax.experimental import pallas as pl
from jax.experimental.pallas import tpu as pltpu

F32 = jnp.float32
BF16 = jnp.bfloat16

D_MODEL = 1024
D_FF = 2816
N_MOD = 9
HEADS = 4
HEAD_DIM = 128
CHUNK = 64
EPS = 1e-6
ROPE = 64
Q_LORA = 384
KV_LORA = 256
N_IN = 2760
N_IN_PACKED = 2816
ROPE_BASE = 10000.0
N_DEV = 8

ADAM_LR = 0.001
ADAM_B1 = 0.9
ADAM_B2 = 0.999
ADAM_EPS = 1e-08
ADAM_WD = 0.01
ADAM_STEP = 10

VMEM_LIMIT_BYTES = 56 * 1024 * 1024
MATMUL_ROWS = (1024, 512, 256, 128)
MESH = pl.DeviceIdType.MESH


def _params(sem=None):
    return pltpu.CompilerParams(dimension_semantics=sem, vmem_limit_bytes=VMEM_LIMIT_BYTES)


def _pick(dim, prefs):
    for p in prefs:
        if dim % p == 0:
            return p
    return dim


_DIMS = {"nn": (((1,), (0,)), ((), ())), "nt": (((1,), (1,)), ((), ())), "tn": (((0,), (0,)), ((), ()))}


def _dot_raw(a, b, mode):
    return lax.dot_general(a.astype(BF16), b.astype(BF16), _DIMS[mode], preferred_element_type=F32)


def _dot_hi(a, b, mode="nn"):
    return lax.dot_general(a, b, _DIMS[mode], precision=lax.Precision.HIGHEST, preferred_element_type=F32)


@functools.partial(jax.custom_vjp, nondiff_argnums=(2,))
def _bdot(a, b, mode):
    return _dot_raw(a, b, mode)


def _bdot_fwd(a, b, mode):
    return _dot_raw(a, b, mode), (a, b)


def _bdot_bwd(mode, res, g):
    a, b = res
    if mode == "nn":
        return _dot_raw(g, b, "nt"), _dot_raw(a, g, "tn")
    if mode == "nt":
        return _dot_raw(g, b, "nn"), _dot_raw(g, a, "tn")
    return _dot_raw(b, g, "nt"), _dot_raw(a, g, "nn")


_bdot.defvjp(_bdot_fwd, _bdot_bwd)


def _mm(a, b, mode, *, name, out_dtypes=(F32,), epi=None, extras=(), extra_params=(), hi=False,
        tm=None, tn=None, tk=None):
    if mode == "nn":
        (M, K), (_, N) = a.shape, b.shape
    elif mode == "nt":
        (M, K), (N, _) = a.shape, b.shape
    else:
        (K, M), (_, N) = a.shape, b.shape
    tm = tm or _pick(M, (512, 1408, 256, 128) if mode == "tn" else MATMUL_ROWS + (384, 352))
    tn = tn or _pick(N, (1024, 1408, 768, 512, 384, 256, 128))
    tk = tk or _pick(K, (1024, 1408, 512, 384, 256, 128))
    a_spec = {"nn": pl.BlockSpec((tm, tk), lambda i, j, k: (i, k)), "nt": pl.BlockSpec((tm, tk), lambda i, j, k: (i, k)),
              "tn": pl.BlockSpec((tk, tm), lambda i, j, k: (k, i))}[mode]
    b_spec = {"nn": pl.BlockSpec((tk, tn), lambda i, j, k: (k, j)), "nt": pl.BlockSpec((tn, tk), lambda i, j, k: (j, k)),
              "tn": pl.BlockSpec((tk, tn), lambda i, j, k: (k, j))}[mode]
    mn_spec = pl.BlockSpec((tm, tn), lambda i, j, k: (i, j))
    return _mmg(a, b, mode, name=name, grid=(M // tm, N // tn, K // tk), a_spec=a_spec, b_spec=b_spec, out_spec=mn_spec,
                out_shapes=[jax.ShapeDtypeStruct((M, N), dt) for dt in out_dtypes], acc_shape=(tm, tn), epi=epi,
                extras=list(extras) + list(extra_params),
                extra_specs=[mn_spec] * len(extras) + [pl.BlockSpec((1, tn), lambda i, j, k: (0, j))] * len(extra_params),
                hi=hi)


def _mmg(a, b, mode, *, name, grid, a_spec, b_spec, out_spec, out_shapes, acc_shape, epi=None, extras=(),
         extra_specs=(), hi=False):
    nk = grid[2]
    n_e, n_o = len(extras), len(out_shapes)

    def body(*refs):
        a_ref, b_ref = refs[:2]
        e_refs = refs[2:2 + n_e]
        o_refs = refs[2 + n_e:2 + n_e + n_o]
        acc_ref = refs[-1]
        k = pl.program_id(2)

        @pl.when(k == 0)
        def _():
            acc_ref[...] = jnp.zeros_like(acc_ref)

        if hi:
            acc_ref[...] += _dot_hi(a_ref[...].astype(F32), b_ref[...].astype(F32), mode)
        else:
            acc_ref[...] += _dot_raw(a_ref[...], b_ref[...], mode)

        @pl.when(k == nk - 1)
        def _():
            acc = acc_ref[...]
            outs = (acc,) if epi is None else epi(acc, *[e[...].astype(F32) for e in e_refs])
            for o_ref, o in zip(o_refs, outs):
                o_ref[...] = o.astype(o_ref.dtype)

    outs = pl.pallas_call(
        body, name=name, grid=grid,
        in_specs=[a_spec, b_spec] + list(extra_specs),
        out_specs=[out_spec] * n_o,
        out_shape=list(out_shapes),
        scratch_shapes=[pltpu.VMEM(acc_shape, F32)],
        compiler_params=_params(("parallel", "parallel", "arbitrary")),
    )(a, b, *extras)
    return outs if n_o > 1 else outs[0]


def _row_spec(th, cw, ci):
    return pl.BlockSpec((th, cw), lambda i: (i, ci))


def _full_spec(shape):
    return pl.BlockSpec(shape, lambda i: (0,) * len(shape))


def _rowwise(fn, rows, params, outs, n_steps, name):
    n_r, n_p, n_o = len(rows), len(params), len(outs)

    def body(*refs):
        vals = [r[...].astype(F32) for r in refs[:n_r + n_p]]
        res = fn(*vals)
        for o_ref, o in zip(refs[n_r + n_p:], res):
            o_ref[...] = o.astype(o_ref.dtype)

    across = [len(o) == 4 for o in outs]
    res = pl.pallas_call(
        body, name=name, grid=(n_steps,),
        in_specs=[_row_spec(th, cw, ci) for (_, th, cw, ci) in rows] + [_full_spec(p.shape) for p in params],
        out_specs=[pl.BlockSpec((o[0], o[1]), lambda i: (0, i)) if ac else _row_spec(o[0], o[1], 0)
                   for o, ac in zip(outs, across)],
        out_shape=[jax.ShapeDtypeStruct((o[0], n_steps * o[1]) if ac else (n_steps * o[0], o[1]), o[2])
                   for o, ac in zip(outs, across)],
        compiler_params=_params(("parallel",)),
    )(*[r[0] for r in rows], *params)
    return res


def _rowwise_bwd(fn, rows, aux, params, douts, n_steps, name, row_dtypes=None, adds=()):
    n_r, n_a, n_p, n_d, n_add = len(rows), len(aux), len(params), len(douts), len(adds)
    row_dtypes = row_dtypes or (F32,) * n_r

    def body(*refs):
        it = iter(refs)
        r_vals = [next(it)[...].astype(F32) for _ in range(n_r)]
        a_vals = [next(it)[...].astype(F32) for _ in range(n_a)]
        p_vals = [next(it)[...].astype(F32) for _ in range(n_p)]
        d_vals = [next(it)[...].astype(F32) for _ in range(n_d)]
        add_vals = [next(it)[...].astype(F32) for _ in range(n_add)]
        dr_refs = [next(it) for _ in range(n_r)]
        dp_refs = [next(it) for _ in range(n_p)]

        def f(*rp):
            return tuple(fn(*rp[:n_r], *a_vals, *rp[n_r:]))

        _, vjp = jax.vjp(f, *r_vals, *p_vals)
        grads = list(vjp(tuple(d_vals)))
        for (ri, _), av in zip(adds, add_vals):
            grads[ri] = grads[ri] + av
        for dr_ref, g in zip(dr_refs, grads[:n_r]):
            dr_ref[...] = g.astype(dr_ref.dtype)

        @pl.when(pl.program_id(0) == 0)
        def _():
            for dp_ref in dp_refs:
                dp_ref[...] = jnp.zeros_like(dp_ref)

        for dp_ref, g in zip(dp_refs, grads[n_r:]):
            dp_ref[...] += g

    all_rows = list(rows) + list(aux) + list(douts) + [(arr,) + tuple(rows[ri][1:3]) + (0,) for ri, arr in adds]
    in_specs = ([_row_spec(th, cw, ci) for (_, th, cw, ci) in list(rows) + list(aux)]
                + [_full_spec(p.shape) for p in params]
                + [_row_spec(th, cw, ci) for (_, th, cw, ci) in all_rows[n_r + n_a:]])
    res = pl.pallas_call(
        body, name=name, grid=(n_steps,),
        in_specs=in_specs,
        out_specs=[_row_spec(th, cw, 0) for (_, th, cw, _) in rows] + [_full_spec(p.shape) for p in params],
        out_shape=[jax.ShapeDtypeStruct((n_steps * th, cw), dt) for (_, th, cw, _), dt in zip(rows, row_dtypes)]
        + [jax.ShapeDtypeStruct(p.shape, F32) for p in params],
        compiler_params=_params(("arbitrary",)),
    )(*[r[0] for r in list(rows) + list(aux)], *params, *[r[0] for r in all_rows[n_r + n_a:]])
    return res[:n_r], res[n_r:]


def _sigmoid(x):
    return lax.logistic(x)


def _silu(x):
    return x * _sigmoid(x)


def _rms(x, w=None, n=None):
    n = n or x.shape[-1]
    y = x * lax.rsqrt(jnp.sum(x * x, axis=-1, keepdims=True) * (1.0 / n) + EPS)
    return y if w is None else y * w


def _modulate(x, scale, shift):
    return _rms(x) * (1.0 + scale) + shift


def _softplus(x):
    return jnp.maximum(x, 0.0) + jnp.log1p(jnp.exp(-jnp.abs(x)))


@jax.custom_vjp
def _rot_half64(x):
    lane = lax.broadcasted_iota(jnp.int32, x.shape, 1)
    up = pltpu.roll(x, 96, 1)
    down = pltpu.roll(x, 32, 1)
    return jnp.where(lane < 32, up, jnp.where(lane < 64, down, 0.0))


_rot_half64.defvjp(lambda x: (_rot_half64(x), None), lambda _, g: (_rot_half64(g),))


def _rope128(x, cos_p, sin_p):
    return x * cos_p + _rot_half64(x) * sin_p


def _gdn_pre_fn(qkvc, kab, alog_p, dt_p):
    a = _silu(qkvc)
    qs, ks = [], []
    for h in range(HEADS):
        qh = a[:, HEAD_DIM * h:HEAD_DIM * (h + 1)]
        kh = a[:, 512 + HEAD_DIM * h:512 + HEAD_DIM * (h + 1)]
        qs.append(qh * lax.rsqrt(jnp.sum(qh * qh, axis=-1, keepdims=True) + EPS) * (HEAD_DIM ** -0.5))
        ks.append(kh * lax.rsqrt(jnp.sum(kh * kh, axis=-1, keepdims=True) + EPS))
    lane = lax.broadcasted_iota(jnp.int32, kab.shape, 1)
    g_full = -jnp.exp(alog_p) * _softplus(kab + dt_p)
    b_full = _sigmoid(kab)
    gb = jnp.where((lane >= 64) & (lane < 68), g_full, jnp.where((lane >= 68) & (lane < 72), b_full, 0.0))
    return jnp.concatenate(qs, axis=1), jnp.concatenate(ks, axis=1), a[:, 1024:1536], gb


def _intra_head(q, k, v, g_col, b_col):
    c = CHUNK
    row = lax.broadcasted_iota(jnp.int32, (c, c), 0)
    col = lax.broadcasted_iota(jnp.int32, (c, c), 1)
    incl, strict, eye = row >= col, row > col, row == col
    tri = jnp.where(incl, 1.0, 0.0).astype(F32)
    ident = jnp.where(eye, 1.0, 0.0).astype(F32)
    g_wide = _dot_hi(tri, jnp.broadcast_to(g_col, (c, HEAD_DIM)))
    g_i = g_wide[:, :c]
    g_j = jnp.sum(jnp.where(eye, g_i, 0.0), axis=0, keepdims=True)
    decay = jnp.where(incl, jnp.exp(jnp.where(incl, g_i - g_j, 0.0)), 0.0)
    kk = _bdot(k, k, "nt")
    a_mat = jnp.where(strict, b_col * kk * decay, 0.0)
    x_pow = -a_mat
    inv = ident + x_pow
    for _ in range(5):
        x_pow = _dot_hi(x_pow, x_pow)
        inv = inv + _dot_hi(inv, x_pow)
    e_wide = jnp.exp(g_wide)
    u = _dot_hi(inv, v * b_col)
    wk = _dot_hi(inv, k * b_col * e_wide)
    qk = _bdot(q, k, "nt") * decay
    last = lax.broadcasted_iota(jnp.int32, (c, HEAD_DIM), 0) == c - 1
    g_last = jnp.sum(jnp.where(last, g_wide, 0.0), axis=0, keepdims=True)
    qd = q * e_wide
    kd = k * jnp.exp(g_last - g_wide)
    gl = jnp.broadcast_to(jnp.exp(g_last), (8, HEAD_DIM))
    return u, wk, qd, kd, qk, gl


INTRA_ROWS = (256, 128, 64)

_BNN = (((2,), (1,)), ((0,), (0,)))
_BNT = (((2,), (2,)), ((0,), (0,)))


def _split_bf16(a):
    hi = a.astype(BF16)
    return hi, (a - hi.astype(F32)).astype(BF16)


def _dot3_raw(a, b, dims):
    a_hi, a_lo = _split_bf16(a)
    b_hi, b_lo = _split_bf16(b)
    dot = lambda x_, y_: lax.dot_general(x_, y_, dims, preferred_element_type=F32)
    return dot(a_hi, b_hi) + (dot(a_hi, b_lo) + dot(a_lo, b_hi))


@functools.partial(jax.custom_vjp, nondiff_argnums=(2, 3))
def _dot3(a, b, nt, exact_bwd=True):
    return _dot3_raw(a, b, _BNT if nt else _BNN)


def _dot3_fwd(a, b, nt, exact_bwd):
    return _dot3_raw(a, b, _BNT if nt else _BNN), (a, b)


def _dot3_bwd(nt, exact_bwd, res, g):
    a, b = res
    if exact_bwd:
        dot = _dot3_raw
    else:
        dot = lambda x_, y_, d_: lax.dot_general(x_.astype(BF16), y_.astype(BF16), d_, preferred_element_type=F32)
    if nt:
        return dot(g, b, _BNN), dot(jnp.swapaxes(g, 1, 2), a, _BNN)
    return dot(g, b, _BNT), dot(jnp.swapaxes(a, 1, 2), g, _BNN)


_dot3.defvjp(_dot3_fwd, _dot3_bwd)


@functools.partial(jax.custom_vjp, nondiff_argnums=(2,))
def _bdot_b(a, b, nt):
    return lax.dot_general(a.astype(BF16), b.astype(BF16), _BNT if nt else _BNN, preferred_element_type=F32)


def _bdot_b_fwd(a, b, nt):
    return _bdot_b(a, b, nt), (a, b)


def _bdot_b_bwd(nt, res, g):
    a, b = res
    dot = lambda x_, y_, d_: lax.dot_general(x_.astype(BF16), y_.astype(BF16), d_, preferred_element_type=F32)
    if nt:
        return dot(g, b, _BNN), dot(jnp.swapaxes(g, 1, 2), a, _BNN)
    return dot(g, b, _BNT), dot(jnp.swapaxes(a, 1, 2), g, _BNN)


_bdot_b.defvjp(_bdot_b_fwd, _bdot_b_bwd)


def _intra_batched(q, k, v, g_col, b_col):
    c = CHUNK
    nb = q.shape[0]
    row = lax.broadcasted_iota(jnp.int32, (1, c, c), 1)
    col = lax.broadcasted_iota(jnp.int32, (1, c, c), 2)
    incl, strict, eye = row >= col, row > col, row == col
    tri = jnp.broadcast_to(jnp.where(incl, 1.0, 0.0).astype(F32), (nb, c, c))
    ident = jnp.where(eye, 1.0, 0.0).astype(F32)
    g_wide = _dot3(tri, jnp.broadcast_to(g_col, (nb, c, HEAD_DIM)), False)
    g_i = g_wide[:, :, :c]
    g_j = jnp.sum(jnp.where(eye, g_i, 0.0), axis=1, keepdims=True)
    decay = jnp.where(incl, jnp.exp(jnp.where(incl, g_i - g_j, 0.0)), 0.0)
    kk = _bdot_b(k, k, True)
    a_mat = jnp.where(strict, b_col * kk * decay, 0.0)
    x_pow = -a_mat
    inv = ident + x_pow
    for _ in range(5):
        x_pow = _dot3(x_pow, x_pow, False, False)
        inv = inv + _dot3(inv, x_pow, False, False)
    e_wide = jnp.exp(g_wide)
    u = _dot3(inv, v * b_col, False)
    wk = _dot3(inv, k * b_col * e_wide, False)
    qk = _bdot_b(q, k, True) * decay
    last = lax.broadcasted_iota(jnp.int32, (1, c, HEAD_DIM), 1) == c - 1
    g_last = jnp.sum(jnp.where(last, g_wide, 0.0), axis=1, keepdims=True)
    qd = q * e_wide
    kd = k * jnp.exp(g_last - g_wide)
    gl = jnp.broadcast_to(jnp.exp(g_last), (nb, 8, HEAD_DIM))
    return u, wk, qd, kd, qk, gl


def _gdn_intra_fn(q, k, v, gb):
    t = q.shape[0]
    nch = t // CHUNK
    lane = lax.broadcasted_iota(jnp.int32, gb.shape, 1)

    def heads_first(x_):
        return jnp.concatenate([x_[:, HEAD_DIM * h:HEAD_DIM * (h + 1)].reshape(nch, CHUNK, HEAD_DIM) for h in range(HEADS)],
                               axis=0)

    def column(first_lane):
        return jnp.concatenate([jnp.sum(jnp.where(lane == first_lane + h, gb, 0.0), axis=1, keepdims=True)
                                .reshape(nch, CHUNK, 1) for h in range(HEADS)], axis=0)

    u, wk, qd, kd, qk, gl = _intra_batched(heads_first(q), heads_first(k), heads_first(v), column(64), column(68))

    def rows_first(x_):
        r, w_ = x_.shape[1], x_.shape[2]
        return jnp.concatenate([x_[nch * h:nch * (h + 1)].reshape(nch * r, w_) for h in range(HEADS)], axis=1)

    qks = [qk[nch * h:nch * (h + 1)].reshape(t, CHUNK) for h in range(HEADS)]
    return (rows_first(u), rows_first(wk), rows_first(qd), rows_first(kd), *qks, rows_first(gl))


def _scan_step(s0, u, wk, qd, kd, qk, gl):
    v_new = u - _bdot_b(wk, s0, False)
    o = _bdot_b(qd, s0, False) + _bdot_b(qk, v_new, False)
    s1 = s0 * gl[:, 0:1, :] + _bdot_b(jnp.swapaxes(kd, 1, 2), v_new, False)
    return o, s1


def _mix_post_fn(o_a, z, o_b, gnw, onw):
    parts = [_rms(o_a[:, HEAD_DIM * h:HEAD_DIM * (h + 1)], gnw) * _silu(z[:, HEAD_DIM * h:HEAD_DIM * (h + 1)])
             for h in range(HEADS)]
    parts += [_rms(o_b[:, HEAD_DIM * h:HEAD_DIM * (h + 1)], onw) for h in range(HEADS)]
    return (jnp.concatenate(parts, axis=1),)


def _mla_pre_fn(ckv, cq, kab, cos_p, sin_p, qnw, kvnw, wuq, wukv, qn_w, qr_w, kn_w, kr_w):
    scale = (HEAD_DIM + ROPE) ** -0.5
    qf = _bdot(_rms(cq, qnw), wuq, "nn")
    kvf = _bdot(_rms(ckv, kvnw), wukv, "nn")
    lane = lax.broadcasted_iota(jnp.int32, kab.shape, 1)
    kr = _rope128(_rms(jnp.where(lane < ROPE, kab, 0.0), kr_w, n=ROPE), cos_p, sin_p)
    qs, ks = [], []
    for h in range(HEADS):
        qn = _rms(qf[:, 256 * h:256 * h + 128], qn_w) * scale
        qr = _rope128(_rms(qf[:, 256 * h + 128:256 * h + 256], qr_w, n=ROPE), cos_p, sin_p) * scale
        qs += [qn, qr]
        ks += [_rms(kvf[:, 128 * h:128 * (h + 1)], kn_w), kr]
    return jnp.concatenate(qs, axis=1), jnp.concatenate(ks, axis=1), kvf[:, 512:]


def _conv_fwd(proj, conv_w, tm, name):
    S = proj.shape[0]
    C = 1536
    nb = tm // 8

    def body(x_ref, prev_ref, w_ref, o_ref, ext_ref):
        i = pl.program_id(0)
        ext_ref[0:8, :] = jnp.where(i > 0, prev_ref[...], 0.0)
        ext_ref[8:, :] = x_ref[...]
        acc = jnp.zeros((tm, C), F32)
        for k in range(4):
            acc = acc + w_ref[k:k + 1, :] * ext_ref[pl.ds(5 + k, tm), :]
        o_ref[...] = acc

    return pl.pallas_call(
        body, name=name, grid=(S // tm,),
        in_specs=[pl.BlockSpec((tm, C), lambda i: (i, 0)),
                  pl.BlockSpec((8, C), lambda i: (jnp.maximum(i * nb - 1, 0), 0)),
                  pl.BlockSpec((4, C), lambda i: (0, 0))],
        out_specs=pl.BlockSpec((tm, C), lambda i: (i, 0)),
        out_shape=jax.ShapeDtypeStruct((S, C), F32),
        scratch_shapes=[pltpu.VMEM((tm + 8, C), F32)],
        compiler_params=_params(("arbitrary",)),
    )(proj, proj, conv_w)


def _conv_bwd(proj, dout, conv_w, tm, name):
    S = proj.shape[0]
    C = 1536
    nb = tm // 8
    n_steps = S // tm

    def body(x_ref, prev_ref, d_ref, next_ref, w_ref, dx_ref, dw_ref, xext_ref, dext_ref):
        i = pl.program_id(0)
        xext_ref[0:8, :] = jnp.where(i > 0, prev_ref[...], 0.0)
        xext_ref[8:, :] = x_ref[...]
        dext_ref[0:tm, :] = d_ref[...]
        dext_ref[tm:, :] = jnp.where(i < n_steps - 1, next_ref[...], 0.0)
        d = d_ref[...]
        acc = jnp.zeros((tm, C), F32)
        dws = []
        for k in range(4):
            acc = acc + w_ref[k:k + 1, :] * dext_ref[pl.ds(3 - k, tm), :]
            dws.append(jnp.sum(d * xext_ref[pl.ds(5 + k, tm), :], axis=0, keepdims=True))
        dx_ref[...] = acc

        @pl.when(i == 0)
        def _():
            dw_ref[...] = jnp.zeros_like(dw_ref)

        dw_ref[...] += jnp.concatenate(dws + [jnp.zeros((4, C), F32)], axis=0)

    return pl.pallas_call(
        body, name=name, grid=(n_steps,),
        in_specs=[pl.BlockSpec((tm, C), lambda i: (i, 0)),
                  pl.BlockSpec((8, C), lambda i: (jnp.maximum(i * nb - 1, 0), 0)),
                  pl.BlockSpec((tm, C), lambda i: (i, 0)),
                  pl.BlockSpec((8, C), lambda i: (jnp.minimum((i + 1) * nb, S // 8 - 1), 0)),
                  pl.BlockSpec((4, C), lambda i: (0, 0))],
        out_specs=[pl.BlockSpec((tm, C), lambda i: (i, 0)), pl.BlockSpec((8, C), lambda i: (0, 0))],
        out_shape=[jax.ShapeDtypeStruct((S, C), F32), jax.ShapeDtypeStruct((8, C), F32)],
        scratch_shapes=[pltpu.VMEM((tm + 8, C), F32), pltpu.VMEM((tm + 8, C), F32)],
        compiler_params=_params(("arbitrary",)),
    )(proj, proj, dout, dout, conv_w)


def _gdn_scan_fwd(u, wk, qd, kd, qks, gl, name):
    S = u.shape[0]
    nc = S // CHUNK
    W = HEADS * HEAD_DIM

    def body(u_ref, wk_ref, qd_ref, kd_ref, qk0, qk1, qk2, qk3, gl_ref, o_ref, sp_ref, s_ref):
        @pl.when(pl.program_id(0) == 0)
        def _():
            s_ref[...] = jnp.zeros_like(s_ref)

        s0 = s_ref[...]
        sp_ref[0] = s0
        o, s1 = _scan_step(s0, _heads(u_ref, HEAD_DIM), _heads(wk_ref, HEAD_DIM), _heads(qd_ref, HEAD_DIM),
                           _heads(kd_ref, HEAD_DIM), jnp.stack([r[...] for r in (qk0, qk1, qk2, qk3)]),
                           _heads(gl_ref, HEAD_DIM))
        s_ref[...] = s1
        for h in range(HEADS):
            o_ref[:, HEAD_DIM * h:HEAD_DIM * (h + 1)] = o[h]

    row = pl.BlockSpec((CHUNK, W), lambda n: (n, 0))
    qk_spec = pl.BlockSpec((CHUNK, CHUNK), lambda n: (n, 0))
    return pl.pallas_call(
        body, name=name, grid=(nc,),
        in_specs=[row, row, row, row, qk_spec, qk_spec, qk_spec, qk_spec, pl.BlockSpec((8, W), lambda n: (n, 0))],
        out_specs=[row, pl.BlockSpec((1, HEADS, HEAD_DIM, HEAD_DIM), lambda n: (n, 0, 0, 0))],
        out_shape=[jax.ShapeDtypeStruct((S, W), F32), jax.ShapeDtypeStruct((nc, HEADS, HEAD_DIM, HEAD_DIM), F32)],
        scratch_shapes=[pltpu.VMEM((HEADS, HEAD_DIM, HEAD_DIM), F32)],
        compiler_params=_params(("arbitrary",)),
    )(u, wk, qd, kd, *qks, gl)


def _gdn_scan_bwd(u, wk, qd, kd, qks, gl, s_prev, d_o, name):
    S = u.shape[0]
    nc = S // CHUNK
    W = HEADS * HEAD_DIM

    def body(u_ref, wk_ref, qd_ref, kd_ref, qk0, qk1, qk2, qk3, gl_ref, sp_ref, do_ref,
             du_ref, dwk_ref, dqd_ref, dkd_ref, dqk0, dqk1, dqk2, dqk3, dgl_ref, ds_ref):
        @pl.when(pl.program_id(0) == 0)
        def _():
            ds_ref[...] = jnp.zeros_like(ds_ref)

        _, vjp = jax.vjp(_scan_step, sp_ref[0], _heads(u_ref, HEAD_DIM), _heads(wk_ref, HEAD_DIM), _heads(qd_ref, HEAD_DIM),
                         _heads(kd_ref, HEAD_DIM), jnp.stack([r[...] for r in (qk0, qk1, qk2, qk3)]),
                         _heads(gl_ref, HEAD_DIM))
        ds0, du, dwk, dqd, dkd, dqk, dgl = vjp((_heads(do_ref, HEAD_DIM), ds_ref[...]))
        ds_ref[...] = ds0
        for h, dqk_ref in enumerate((dqk0, dqk1, dqk2, dqk3)):
            sl = slice(HEAD_DIM * h, HEAD_DIM * (h + 1))
            du_ref[:, sl] = du[h]
            dwk_ref[:, sl] = dwk[h]
            dqd_ref[:, sl] = dqd[h]
            dkd_ref[:, sl] = dkd[h]
            dqk_ref[...] = dqk[h]
            dgl_ref[:, sl] = dgl[h]

    rev = lambda n: (nc - 1 - n, 0)
    row = pl.BlockSpec((CHUNK, W), rev)
    qk_spec = pl.BlockSpec((CHUNK, CHUNK), rev)
    gl_spec = pl.BlockSpec((8, W), rev)
    qk_shape = jax.ShapeDtypeStruct((S, CHUNK), F32)
    row_shape = jax.ShapeDtypeStruct((S, W), F32)
    return pl.pallas_call(
        body, name=name, grid=(nc,),
        in_specs=[row, row, row, row, qk_spec, qk_spec, qk_spec, qk_spec, gl_spec,
                  pl.BlockSpec((1, HEADS, HEAD_DIM, HEAD_DIM), lambda n: (nc - 1 - n, 0, 0, 0)), row],
        out_specs=[row, row, row, row, qk_spec, qk_spec, qk_spec, qk_spec, gl_spec],
        out_shape=[row_shape] * 4 + [qk_shape] * 4 + [jax.ShapeDtypeStruct((nc * 8, W), F32)],
        scratch_shapes=[pltpu.VMEM((HEADS, HEAD_DIM, HEAD_DIM), F32)],
        compiler_params=_params(("arbitrary",)),
    )(u, wk, qd, kd, *qks, gl, s_prev, d_o)


NEG = -1e30


def _chunk_mask(i, j, t, transposed=False):
    q_axis, k_axis = (1, 0) if transposed else (0, 1)
    r = (i * t + lax.broadcasted_iota(jnp.int32, (t, t), q_axis)) // CHUNK
    c = (j * t + lax.broadcasted_iota(jnp.int32, (t, t), k_axis)) // CHUNK
    return c <= r


def _tile_pairs(n, by_key):
    pairs = [(i, j) for j in range(n) for i in range(j, n)] if by_key else [(i, j) for i in range(n) for j in range(i + 1)]
    return jnp.asarray(np.array([p[0] for p in pairs], np.int32)), jnp.asarray(np.array([p[1] for p in pairs], np.int32))


def _heads(ref, width):
    return jnp.stack([ref[:, width * h:width * (h + 1)] for h in range(HEADS)])


def _bmm(a, b, dims):
    return lax.dot_general(a.astype(BF16), b.astype(BF16), dims, preferred_element_type=F32)


def _attn_fwd(q, k, v_t, t, name):
    S = q.shape[0]
    n = S // t
    qi, kj = _tile_pairs(n, by_key=False)

    def body(qi_ref, kj_ref, q_ref, k_ref, vt_ref, o_ref, lse_ref, m_ref, l_ref, acc_ref):
        i, j = qi_ref[pl.program_id(0)], kj_ref[pl.program_id(0)]

        @pl.when(j == 0)
        def _():
            m_ref[...] = jnp.full_like(m_ref, NEG)
            l_ref[...] = jnp.zeros_like(l_ref)
            acc_ref[...] = jnp.zeros_like(acc_ref)

        def update(masked):
            s_t = _bmm(_heads(k_ref, 256), _heads(q_ref, 256), _BNT)
            if masked:
                s_t = jnp.where(_chunk_mask(i, j, t, transposed=True)[None], s_t, NEG)
            m_old = m_ref[...]
            m_new = jnp.maximum(m_old, jnp.max(s_t, axis=1, keepdims=True))
            p_t = jnp.exp(s_t - m_new)
            alpha = jnp.exp(m_old - m_new)
            l_ref[...] = alpha * l_ref[...] + jnp.sum(p_t, axis=1, keepdims=True)
            v_heads = jnp.stack([vt_ref[HEAD_DIM * h:HEAD_DIM * (h + 1), :] for h in range(HEADS)])
            acc_ref[...] = alpha * acc_ref[...] + _bmm(v_heads, p_t, _BNN)
            m_ref[...] = m_new

        @pl.when(j < i)
        def _():
            update(False)

        @pl.when(j == i)
        def _():
            update(True)
            for h in range(HEADS):
                sl = slice(HEAD_DIM * h, HEAD_DIM * (h + 1))
                o_ref[:, sl] = jnp.transpose(acc_ref[h] / l_ref[h])
                lse_ref[:, sl] = jnp.transpose(jnp.broadcast_to(m_ref[h] + jnp.log(l_ref[h]), (HEAD_DIM, t)))

    row = lambda p, qi_, kj_: (qi_[p], 0)
    return pl.pallas_call(
        body, name=name,
        grid_spec=pltpu.PrefetchScalarGridSpec(
            num_scalar_prefetch=2, grid=(qi.shape[0],),
            in_specs=[pl.BlockSpec((t, HEADS * 256), row), pl.BlockSpec((t, HEADS * 256), lambda p, qi_, kj_: (kj_[p], 0)),
                      pl.BlockSpec((HEADS * HEAD_DIM, t), lambda p, qi_, kj_: (0, kj_[p]))],
            out_specs=[pl.BlockSpec((t, HEADS * HEAD_DIM), row)] * 2,
            scratch_shapes=[pltpu.VMEM((HEADS, 1, t), F32), pltpu.VMEM((HEADS, 1, t), F32),
                            pltpu.VMEM((HEADS, HEAD_DIM, t), F32)]),
        out_shape=[jax.ShapeDtypeStruct((S, HEADS * HEAD_DIM), F32)] * 2,
        compiler_params=_params(("arbitrary",)),
    )(qi, kj, q, k, v_t)


def _attn_dq(q, k, v, o, lse, d_o, t, name):
    S = q.shape[0]
    n = S // t
    qi, kj = _tile_pairs(n, by_key=False)

    def body(qi_ref, kj_ref, q_ref, k_ref, v_ref, o_ref, lse_ref, do_ref, dq_ref, st_ref, acc_ref, delta_ref):
        i, j = qi_ref[pl.program_id(0)], kj_ref[pl.program_id(0)]

        @pl.when(j == 0)
        def _():
            acc_ref[...] = jnp.zeros_like(acc_ref)
            delta_ref[...] = jnp.sum(_heads(do_ref, HEAD_DIM) * _heads(o_ref, HEAD_DIM), axis=2, keepdims=True)

        def update(masked):
            kh = _heads(k_ref, 256)
            d_out = _heads(do_ref, HEAD_DIM)
            s = _bmm(_heads(q_ref, 256), kh, _BNT)
            p = jnp.exp(s - _heads(lse_ref, HEAD_DIM)[:, :, 0:1])
            if masked:
                p = jnp.where(_chunk_mask(i, j, t)[None], p, 0.0)
            dp = _bmm(d_out, _heads(v_ref, HEAD_DIM), _BNT)
            ds = p * (dp - delta_ref[...])
            acc_ref[...] += _bmm(ds, kh, _BNN)

        @pl.when(j < i)
        def _():
            update(False)

        @pl.when(j == i)
        def _():
            update(True)
            lane = lax.broadcasted_iota(jnp.int32, (t, HEAD_DIM), 1)
            stats = jnp.zeros((t, HEAD_DIM), F32)
            for h in range(HEADS):
                dq_ref[:, 256 * h:256 * (h + 1)] = acc_ref[h]
                stats = stats + jnp.where(lane == h, lse_ref[:, HEAD_DIM * h:HEAD_DIM * (h + 1)], 0.0)
                stats = stats + jnp.where(lane == HEADS + h, delta_ref[h], 0.0)
            st_ref[...] = jnp.transpose(stats)[0:8, :]

    kv = lambda p, qi_, kj_: (kj_[p], 0)
    row = lambda p, qi_, kj_: (qi_[p], 0)
    wide, narrow = pl.BlockSpec((t, HEADS * 256), row), pl.BlockSpec((t, HEADS * HEAD_DIM), row)
    return pl.pallas_call(
        body, name=name,
        grid_spec=pltpu.PrefetchScalarGridSpec(
            num_scalar_prefetch=2, grid=(qi.shape[0],),
            in_specs=[wide, pl.BlockSpec((t, HEADS * 256), kv), pl.BlockSpec((t, HEADS * HEAD_DIM), kv), narrow, narrow, narrow],
            out_specs=[wide, pl.BlockSpec((8, t), lambda p, qi_, kj_: (0, qi_[p]))],
            scratch_shapes=[pltpu.VMEM((HEADS, t, 256), F32), pltpu.VMEM((HEADS, t, 1), F32)]),
        out_shape=[jax.ShapeDtypeStruct((S, HEADS * 256), F32), jax.ShapeDtypeStruct((8, S), F32)],
        compiler_params=_params(("arbitrary",)),
    )(qi, kj, q, k, v, o, lse, d_o)


def _attn_dkv(q, k, v, d_o, stats, t, name):
    S = q.shape[0]
    n = S // t
    qi, kj = _tile_pairs(n, by_key=True)

    def body(qi_ref, kj_ref, q_ref, k_ref, v_ref, do_ref, st_ref, dk_ref, dv_ref, dk_acc, dv_acc):
        i, j = qi_ref[pl.program_id(0)], kj_ref[pl.program_id(0)]

        @pl.when(i == j)
        def _():
            dk_acc[...] = jnp.zeros_like(dk_acc)
            dv_acc[...] = jnp.zeros_like(dv_acc)

        def update(masked):
            qh = _heads(q_ref, 256)
            d_out = _heads(do_ref, HEAD_DIM)
            st = st_ref[...]
            lse_row = jnp.stack([st[h:h + 1, :] for h in range(HEADS)])
            delta_row = jnp.stack([st[HEADS + h:HEADS + h + 1, :] for h in range(HEADS)])
            s_t = _bmm(_heads(k_ref, 256), qh, _BNT)
            p_t = jnp.exp(s_t - lse_row)
            if masked:
                p_t = jnp.where(_chunk_mask(i, j, t, transposed=True)[None], p_t, 0.0)
            dv_acc[...] += _bmm(p_t, d_out, _BNN)
            dp_t = _bmm(_heads(v_ref, HEAD_DIM), d_out, _BNT)
            ds_t = p_t * (dp_t - delta_row)
            dk_acc[...] += _bmm(ds_t, qh, _BNN)

        @pl.when(i == j)
        def _():
            update(True)

        @pl.when(i > j)
        def _():
            update(False)

        @pl.when(i == n - 1)
        def _():
            for h in range(HEADS):
                dk_ref[:, 256 * h:256 * (h + 1)] = dk_acc[h]
                dv_ref[:, HEAD_DIM * h:HEAD_DIM * (h + 1)] = dv_acc[h]

    q_row = lambda p, qi_, kj_: (qi_[p], 0)
    k_row = lambda p, qi_, kj_: (kj_[p], 0)
    return pl.pallas_call(
        body, name=name,
        grid_spec=pltpu.PrefetchScalarGridSpec(
            num_scalar_prefetch=2, grid=(qi.shape[0],),
            in_specs=[pl.BlockSpec((t, HEADS * 256), q_row), pl.BlockSpec((t, HEADS * 256), k_row),
                      pl.BlockSpec((t, HEADS * HEAD_DIM), k_row), pl.BlockSpec((t, HEADS * HEAD_DIM), q_row),
                      pl.BlockSpec((8, t), lambda p, qi_, kj_: (0, qi_[p]))],
            out_specs=[pl.BlockSpec((t, HEADS * 256), k_row), pl.BlockSpec((t, HEADS * HEAD_DIM), k_row)],
            scratch_shapes=[pltpu.VMEM((HEADS, t, 256), F32), pltpu.VMEM((HEADS, t, HEAD_DIM), F32)]),
        out_shape=[jax.ShapeDtypeStruct((S, HEADS * 256), F32), jax.ShapeDtypeStruct((S, HEADS * HEAD_DIM), F32)],
        compiler_params=_params(("arbitrary",)),
    )(qi, kj, q, k, v, d_o, stats)


FFN_PIECE = 2 * D_FF // N_DEV
HID_PIECES = D_FF // FFN_PIECE


def _ffn_up(h, w8, name):
    S = h.shape[0]
    tm = _pick(S, MATMUL_ROWS)

    def body(h_ref, wg_ref, wu_ref, g_ref, u_ref, hid_ref):
        gate = _dot_raw(h_ref[...], wg_ref[...], "nn")
        up = _dot_raw(h_ref[...], wu_ref[...], "nn")
        g_ref[...] = gate.astype(BF16)
        u_ref[...] = up.astype(BF16)
        hid_ref[...] = (_silu(gate) * up).astype(BF16)

    o_spec = pl.BlockSpec((None, tm, FFN_PIECE), lambda i, j: (j, i, 0))
    return pl.pallas_call(
        body, name=name, grid=(S // tm, HID_PIECES),
        in_specs=[pl.BlockSpec((tm, D_MODEL), lambda i, j: (i, 0)),
                  pl.BlockSpec((None, D_MODEL, FFN_PIECE), lambda i, j: (j, 0, 0)),
                  pl.BlockSpec((None, D_MODEL, FFN_PIECE), lambda i, j: (j + HID_PIECES, 0, 0))],
        out_specs=[o_spec] * 3,
        out_shape=[jax.ShapeDtypeStruct((HID_PIECES, S, FFN_PIECE), BF16)] * 3,
        compiler_params=_params(("parallel", "parallel")),
    )(h, w8, w8)


def _after_specs(after):
    return [] if after is None else [pl.BlockSpec(memory_space=pl.ANY)]


def _after_args(after):
    return [] if after is None else [after]


def _ffn_gw8(h, d_gate, d_up, name, after=None):
    S = h.shape[0]
    tm = 512
    tk = _pick(S, (512, 256, 128))
    nk = S // tk

    def body(h_ref, dg_ref, du_ref, *rest):
        o_ref, acc_ref = rest[-2:]
        k = pl.program_id(1)

        @pl.when(k == 0)
        def _():
            acc_ref[...] = jnp.zeros_like(acc_ref)

        h_t = jnp.transpose(h_ref[...])
        for p in range(HID_PIECES):
            acc_ref[p] += _dot_raw(h_t, dg_ref[p], "nn")
            acc_ref[HID_PIECES + p] += _dot_raw(h_t, du_ref[p], "nn")

        @pl.when(k == nk - 1)
        def _():
            o_ref[...] = acc_ref[...].astype(o_ref.dtype)

    d_spec = pl.BlockSpec((HID_PIECES, tk, FFN_PIECE), lambda i, k: (0, k, 0))
    return pl.pallas_call(
        body, name=name, grid=(D_MODEL // tm, nk),
        in_specs=[pl.BlockSpec((tk, tm), lambda i, k: (k, i)), d_spec, d_spec] + _after_specs(after),
        out_specs=pl.BlockSpec((2 * HID_PIECES, tm, FFN_PIECE), lambda i, k: (0, i, 0)),
        out_shape=jax.ShapeDtypeStruct((2 * HID_PIECES, D_MODEL, FFN_PIECE), BF16),
        scratch_shapes=[pltpu.VMEM((2 * HID_PIECES, tm, FFN_PIECE), F32)],
        compiler_params=_params(("parallel", "arbitrary")),
    )(h, d_gate, d_up, *_after_args(after))


def _ffn_dh(d_gate, d_up, w8, x, d_out, scale, shift, name, after=None):
    S = d_gate.shape[1]
    tm = _pick(S, (512, 256, 128))

    def body(dg_ref, du_ref, wg_ref, wu_ref, x_ref, do_ref, sc_ref, sh_ref, *rest):
        dx_ref, dsc_ref, dsh_ref, acc_ref = rest[-4:]
        i, k = pl.program_id(0), pl.program_id(1)

        @pl.when(k == 0)
        def _():
            acc_ref[...] = jnp.zeros_like(acc_ref)

        acc_ref[...] += _dot_raw(dg_ref[...], wg_ref[...], "nt") + _dot_raw(du_ref[...], wu_ref[...], "nt")

        @pl.when((k == 0) & (i == 0))
        def _():
            dsc_ref[...] = jnp.zeros_like(dsc_ref)
            dsh_ref[...] = jnp.zeros_like(dsh_ref)

        @pl.when(k == HID_PIECES - 1)
        def _():
            _, vjp = jax.vjp(_modulate, x_ref[...], sc_ref[...], sh_ref[...])
            dx, dsc, dsh = vjp(acc_ref[...])
            dx_ref[...] = dx + do_ref[...]
            dsc_ref[...] += dsc
            dsh_ref[...] += dsh

    d_spec = pl.BlockSpec((None, tm, FFN_PIECE), lambda i, k: (k, i, 0))
    row = pl.BlockSpec((tm, D_MODEL), lambda i, k: (i, 0))
    par = pl.BlockSpec((1, D_MODEL), lambda i, k: (0, 0))
    return pl.pallas_call(
        body, name=name, grid=(S // tm, HID_PIECES),
        in_specs=[d_spec, d_spec,
                  pl.BlockSpec((None, D_MODEL, FFN_PIECE), lambda i, k: (k, 0, 0)),
                  pl.BlockSpec((None, D_MODEL, FFN_PIECE), lambda i, k: (k + HID_PIECES, 0, 0)),
                  row, row, par, par] + _after_specs(after),
        out_specs=[row, par, par],
        out_shape=[jax.ShapeDtypeStruct((S, D_MODEL), F32), jax.ShapeDtypeStruct((1, D_MODEL), F32),
                   jax.ShapeDtypeStruct((1, D_MODEL), F32)],
        scratch_shapes=[pltpu.VMEM((tm, D_MODEL), F32)],
        compiler_params=_params(("arbitrary", "arbitrary")),
    )(d_gate, d_up, w8, w8, x, d_out, scale, shift, *_after_args(after))


def _swiglu_bwd(d_hid, gate, up):
    sg = _sigmoid(gate)
    return d_hid * up * (sg * (1.0 + gate * (1.0 - sg))), d_hid * (gate * sg)


def _loss_and_grad(x3, target, name):
    S = x3.shape[0]
    tm = _pick(S, (512, 256, 128))
    n = S // tm

    def body(x_ref, t_ref, dx_ref, l_ref):
        i = pl.program_id(0)
        diff = x_ref[...] - t_ref[...]
        dx_ref[...] = diff * (1.0 / D_MODEL)

        @pl.when(i == 0)
        def _():
            l_ref[...] = jnp.zeros_like(l_ref)

        l_ref[...] += jnp.sum(diff * diff, axis=0, keepdims=True)

        @pl.when(i == n - 1)
        def _():
            l_ref[...] = jnp.full(l_ref.shape, (0.5 / D_MODEL) * jnp.sum(l_ref[...]), F32)

    return pl.pallas_call(
        body, name=name, grid=(n,),
        in_specs=[pl.BlockSpec((tm, D_MODEL), lambda i: (i, 0))] * 2,
        out_specs=[pl.BlockSpec((tm, D_MODEL), lambda i: (i, 0)), pl.BlockSpec((1, D_MODEL), lambda i: (0, 0))],
        out_shape=[jax.ShapeDtypeStruct((S, D_MODEL), F32), jax.ShapeDtypeStruct((1, D_MODEL), F32)],
        compiler_params=_params(("arbitrary",)),
    )(x3, target)


def _adamw(w, g, m, v, name):
    R, C = w.shape
    tr = _pick(R, (256, 176, 128, 64, 32, 16, 8))

    def body(w_ref, g_ref, m_ref, v_ref, d_ref, nm_ref, nv_ref):
        g_ = g_ref[...]
        m_ = ADAM_B1 * m_ref[...] + (1.0 - ADAM_B1) * g_
        v_ = ADAM_B2 * v_ref[...] + (1.0 - ADAM_B2) * (g_ * g_)
        m_hat = m_ / (1.0 - ADAM_B1 ** ADAM_STEP)
        v_hat = v_ / (1.0 - ADAM_B2 ** ADAM_STEP)
        d_ref[...] = -ADAM_LR * (m_hat / (jnp.sqrt(v_hat) + ADAM_EPS) + ADAM_WD * w_ref[...])
        nm_ref[...] = m_
        nv_ref[...] = v_

    spec = pl.BlockSpec((tr, C), lambda i: (i, 0))
    return pl.pallas_call(
        body, name=name, grid=(R // tr,),
        in_specs=[spec] * 4, out_specs=[spec] * 3,
        out_shape=[jax.ShapeDtypeStruct((R, C), F32)] * 3,
        compiler_params=_params(("parallel",)),
    )(w, g, m, v)


def _sum_devices(parts, name):
    _, R, C = parts.shape
    tr = _pick(R, (512, 256, 176, 128, 64, 32, 16, 8))

    def body(p_ref, o_ref):
        acc = p_ref[0].astype(F32)
        for d in range(1, N_DEV):
            acc = acc + p_ref[d].astype(F32)
        o_ref[...] = acc

    return pl.pallas_call(
        body, name=name, grid=(R // tr,),
        in_specs=[pl.BlockSpec((N_DEV, tr, C), lambda i: (0, i, 0))],
        out_specs=pl.BlockSpec((tr, C), lambda i: (i, 0)),
        out_shape=jax.ShapeDtypeStruct((R, C), F32),
        compiler_params=_params(("parallel",)),
    )(parts)


def _my_place():
    return lax.axis_index("x"), lax.axis_index("y"), lax.axis_index("c")


def _all_gather(blocks, name):
    n = len(blocks)

    def body(*refs):
        x_refs, out_refs = refs[:n], refs[n:2 * n]
        send_sems, recv_sems, local_sems = refs[2 * n:]
        x, y, c = _my_place()
        me, sibling = (x, y, c), (x, y, 1 - c)
        chips = [(1 - x, y), (x, 1 - y), (1 - x, 1 - y)]

        def copy(a, k, blk, to, own=False):
            slot = out_refs[a].at[4 * blk[0] + 2 * blk[1] + blk[2]]
            return pltpu.make_async_remote_copy(
                src_ref=x_refs[a] if own else slot, dst_ref=slot,
                send_sem=send_sems.at[7 * a + k], recv_sem=recv_sems.at[7 * a + k], device_id=to, device_id_type=MESH)

        mine = [pltpu.make_async_copy(x_refs[a], out_refs[a].at[4 * x + 2 * y + c], local_sems.at[a]) for a in range(n)]
        for cp in mine:
            cp.start()
        first = []
        for j, chip in enumerate(chips):
            first += [copy(a, 1 + j, me, (*chip, c), own=True) for a in range(n)]
        first += [copy(a, 0, me, sibling, own=True) for a in range(n)]
        for cp in first:
            cp.start()
        passed = []
        for j, chip in enumerate(chips):
            for a in range(n):
                copy(a, 1 + j, (*chip, c), me).wait_recv()
                passed.append(copy(a, 4 + j, (*chip, c), sibling))
                passed[-1].start()
        for a in range(n):
            copy(a, 0, sibling, me).wait_recv()
        for j, chip in enumerate(chips):
            for a in range(n):
                copy(a, 4 + j, (*chip, 1 - c), me).wait_recv()
        for cp in first + passed:
            cp.wait_send()
        for cp in mine:
            cp.wait()

    return pl.pallas_call(
        body, name=name,
        out_shape=[jax.ShapeDtypeStruct((N_DEV,) + b.shape, b.dtype) for b in blocks],
        in_specs=[pl.BlockSpec(memory_space=pl.ANY)] * n,
        out_specs=[pl.BlockSpec(memory_space=pl.ANY)] * n,
        scratch_shapes=[pltpu.SemaphoreType.DMA((7 * n,)), pltpu.SemaphoreType.DMA((7 * n,)), pltpu.SemaphoreType.DMA((n,))],
    )(*blocks)


def _all_to_all(pieces, name):
    n = len(pieces)

    def body(*refs):
        x_refs, out_refs = refs[:n], refs[n:2 * n]
        send_sems, recv_sems, local_sems = refs[2 * n:]
        x, y, c = _my_place()
        me = 4 * x + 2 * y + c
        mine = [pltpu.make_async_copy(x_refs[a].at[me], out_refs[a].at[me], local_sems.at[a]) for a in range(n)]
        for cp in mine:
            cp.start()
        copies = []
        for k in (2, 4, 6, 3, 5, 7, 1):
            px = 1 - x if k & 4 else x
            py = 1 - y if k & 2 else y
            pc = 1 - c if k & 1 else c
            peer = 4 * px + 2 * py + pc
            for a in range(n):
                copies.append(pltpu.make_async_remote_copy(
                    src_ref=x_refs[a].at[peer], dst_ref=out_refs[a].at[me],
                    send_sem=send_sems.at[7 * a + k - 1], recv_sem=recv_sems.at[7 * a + k - 1],
                    device_id=(px, py, pc), device_id_type=MESH))
        for cp in copies:
            cp.start()
        for cp in copies:
            cp.wait_recv()
        for cp in copies:
            cp.wait_send()
        for cp in mine:
            cp.wait()

    return pl.pallas_call(
        body, name=name,
        out_shape=[jax.ShapeDtypeStruct(p.shape, p.dtype) for p in pieces],
        in_specs=[pl.BlockSpec(memory_space=pl.ANY)] * n,
        out_specs=[pl.BlockSpec(memory_space=pl.ANY)] * n,
        scratch_shapes=[pltpu.SemaphoreType.DMA((7 * n,)), pltpu.SemaphoreType.DMA((7 * n,)), pltpu.SemaphoreType.DMA((n,))],
    )(*pieces)


def _peers():
    x, y, c = _my_place()
    out = []
    for k in (2, 4, 6, 3, 5, 7, 1):
        px = 1 - x if k & 4 else x
        py = 1 - y if k & 2 else y
        pc = 1 - c if k & 1 else c
        out.append((k, (px, py, pc), 4 * px + 2 * py + pc))
    return out


def _exchange_copies(x_refs, land_refs, send_sems, recv_sems, scatter):
    x, y, c = _my_place()
    me = 4 * x + 2 * y + c
    starts, arrivals = [], []
    for k, place, peer in _peers():
        for a, (x_ref, land_ref) in enumerate(zip(x_refs, land_refs)):
            sems = dict(send_sem=send_sems.at[7 * a + k - 1], recv_sem=recv_sems.at[7 * a + k - 1],
                        device_id=place, device_id_type=MESH)
            src = x_ref.at[peer] if scatter else x_ref
            starts.append(pltpu.make_async_remote_copy(src_ref=src, dst_ref=land_ref.at[me], **sems))
            arrivals.append(pltpu.make_async_remote_copy(src_ref=src, dst_ref=land_ref.at[peer], **sems))
    return starts, arrivals


def _exchange_start(arrays, scatter, name):
    n = len(arrays)
    hbm = pl.BlockSpec(memory_space=pltpu.HBM)
    sem = pl.BlockSpec(memory_space=pltpu.SEMAPHORE)
    lands = [lax.empty(a.shape if scatter else (N_DEV,) + a.shape, a.dtype) for a in arrays]

    def body(*refs):
        x_refs, land_refs = refs[:n], refs[n:2 * n]
        send_sems, recv_sems = refs[2 * n], refs[2 * n + 1]
        token = refs[-1]
        starts, _ = _exchange_copies(x_refs, land_refs, send_sems, recv_sems, scatter)
        for cp in starts:
            cp.start()
        token[...] = jnp.zeros_like(token)

    res = pl.pallas_call(
        body, name=name,
        out_shape=(pltpu.SemaphoreType.DMA((7 * n,)), pltpu.SemaphoreType.DMA((7 * n,)),
                   *[pltpu.HBM(a.shape, a.dtype) for a in arrays], *[pltpu.HBM(l.shape, l.dtype) for l in lands],
                   jax.ShapeDtypeStruct((8, 128), F32)),
        in_specs=[hbm] * (2 * n),
        out_specs=(sem, sem, *[hbm] * (2 * n), pl.BlockSpec(memory_space=pltpu.VMEM)),
        input_output_aliases={i: 2 + i for i in range(2 * n)},
        compiler_params=pltpu.CompilerParams(has_side_effects=pltpu.SideEffectType.DATAFLOW_SIDE_EFFECTING),
    )(*[pltpu.with_memory_space_constraint(a, pltpu.HBM) for a in arrays],
      *[pltpu.with_memory_space_constraint(l, pltpu.HBM) for l in lands])
    return res[0], res[1], list(res[2:2 + n]), list(res[2 + n:2 + 2 * n]), res[-1]


def _exchange_wait(handles, scatter, after, name):
    send_sems, recv_sems, arrays, lands, _ = handles
    n = len(arrays)
    hbm = pl.BlockSpec(memory_space=pltpu.HBM)
    sem = pl.BlockSpec(memory_space=pltpu.SEMAPHORE)

    def body(*refs):
        x_refs, land_refs = refs[:n], refs[n:2 * n]
        send_s, recv_s = refs[2 * n], refs[2 * n + 1]
        starts, arrivals = _exchange_copies(x_refs, land_refs, send_s, recv_s, scatter)
        for cp in arrivals:
            cp.wait_recv()
        for cp in starts:
            cp.wait_send()

    res = pl.pallas_call(
        body, name=name,
        out_shape=(*[pltpu.HBM(a.shape, a.dtype) for a in arrays], *[pltpu.HBM(l.shape, l.dtype) for l in lands]),
        in_specs=[hbm] * (2 * n) + [sem, sem, pl.BlockSpec(memory_space=pl.ANY)],
        out_specs=tuple([hbm] * (2 * n)),
        input_output_aliases={i: i for i in range(2 * n)},
        compiler_params=pltpu.CompilerParams(has_side_effects=pltpu.SideEffectType.DATAFLOW_SIDE_EFFECTING),
    )(*arrays, *lands, send_sems, recv_sems, after)
    me = 4 * lax.axis_index("x") + 2 * lax.axis_index("y") + lax.axis_index("c")
    out = []
    for src, got in zip(res[:n], res[n:]):
        zeros = (0,) * (got.ndim - 1)
        own = lax.dynamic_slice(src, (me,) + zeros, (1,) + src.shape[1:]) if scatter else src[None]
        out.append(lax.dynamic_update_slice(got, own, (me,) + zeros))
    return out


def _pad_lanes(v, at=0, width=128):
    return jnp.pad(v, ((0, 0), (at, width - at - v.shape[1])))


def _pack_weights(P):
    W = {}
    w = P["w_in"]
    W["wp"] = jnp.concatenate([w[:, :2048], w[:, 2440:2696], w[:, 2056:2440], w[:, 2696:2760], w[:, 2048:2056],
                               jnp.zeros((D_MODEL, N_IN_PACKED - N_IN), w.dtype)], axis=1).astype(BF16)
    W["conv_w"] = P["gdn_conv_w"].astype(F32)
    W["alog_p"] = _pad_lanes(P["gdn_a_log"], 64)
    W["dt_p"] = _pad_lanes(P["gdn_dt_bias"], 64)
    W["gnw"] = P["gdn_norm_w"]
    W["qnw"] = P["mla_q_norm_w"]
    W["kvnw"] = P["mla_kv_norm_w"]
    uq = P["mla_w_uq"].reshape(Q_LORA, HEADS, HEAD_DIM + ROPE)
    W["wuq"] = jnp.pad(uq, ((0, 0), (0, 0), (0, 256 - HEAD_DIM - ROPE))).reshape(Q_LORA, HEADS * 256).astype(BF16)
    ukv = P["mla_w_ukv"].reshape(KV_LORA, HEADS, 2, HEAD_DIM)
    W["wukv"] = ukv.transpose(0, 2, 1, 3).reshape(KV_LORA, 2 * HEADS * HEAD_DIM).astype(BF16)
    W["qn_w"] = P["qkn_q_nope"]
    W["qr_w"] = _pad_lanes(P["qkn_q_rope"])
    W["kn_w"] = P["qkn_k_nope"]
    W["kr_w"] = _pad_lanes(P["qkn_k_rope"])
    W["onw"] = P["mla_out_norm_w"]
    W["wout"] = P["w_out"].astype(BF16)
    return W


def _unpack_grads(G):
    g = G["wp"]
    uq = G["wuq"].reshape(Q_LORA, HEADS, 256)[:, :, :HEAD_DIM + ROPE].reshape(Q_LORA, HEADS * (HEAD_DIM + ROPE))
    ukv = G["wukv"].reshape(KV_LORA, 2, HEADS, HEAD_DIM).transpose(0, 2, 1, 3).reshape(KV_LORA, 2 * HEADS * HEAD_DIM)
    return {
        "w_in": jnp.concatenate([g[:, :2048], g[:, 2752:2760], g[:, 2304:2688], g[:, 2048:2304], g[:, 2688:2752]], axis=1),
        "gdn_conv_w": G["conv_w"], "gdn_a_log": G["alog_p"][:, 64:68], "gdn_dt_bias": G["dt_p"][:, 64:68],
        "gdn_norm_w": G["gnw"], "mla_q_norm_w": G["qnw"], "mla_w_uq": uq, "mla_kv_norm_w": G["kvnw"], "mla_w_ukv": ukv,
        "qkn_q_nope": G["qn_w"], "qkn_q_rope": G["qr_w"][:, :ROPE], "qkn_k_nope": G["kn_w"], "qkn_k_rope": G["kr_w"][:, :ROPE],
        "mla_out_norm_w": G["onw"], "w_out": G["wout"],
    }


def _rope_tables(positions):
    half = ROPE // 2
    inv_freq = ROPE_BASE ** (-jnp.arange(half, dtype=F32) / half)
    ang = positions.astype(F32)[:, None] * inv_freq
    cos, sin = jnp.cos(ang), jnp.sin(ang)
    zeros = jnp.zeros((positions.shape[0], 128 - ROPE), F32)
    return jnp.concatenate([cos, cos, zeros], axis=1), jnp.concatenate([-sin, sin, zeros], axis=1)


def _mod_fn(x, scale, shift):
    return (_modulate(x, scale, shift),)


def _ffn_fwd(x, scale, shift, gate_w, w8, wo4, tag):
    S = x.shape[0]
    tm = _pick(S, (512, 256, 128))
    (h,) = _rowwise(_mod_fn, [(x, tm, D_MODEL, 0)], [scale, shift], [(tm, D_MODEL, BF16)], S // tm, tag + "_mod")
    gate, up, hid = _ffn_up(h, w8, tag + "_up")
    tb = _pick(S, MATMUL_ROWS)
    mn = pl.BlockSpec((tb, D_MODEL), lambda i, j, k: (i, j))
    f, x_out = _mmg(hid, wo4, "nn", name=tag + "_down", grid=(S // tb, 1, HID_PIECES),
                    a_spec=pl.BlockSpec((None, tb, FFN_PIECE), lambda i, j, k: (k, i, 0)),
                    b_spec=pl.BlockSpec((None, FFN_PIECE, D_MODEL), lambda i, j, k: (k, 0, j)),
                    out_spec=mn, out_shapes=[jax.ShapeDtypeStruct((S, D_MODEL), F32)] * 2, acc_shape=(tb, D_MODEL),
                    extras=[x, gate_w], extra_specs=[mn, pl.BlockSpec((1, D_MODEL), lambda i, j, k: (0, j))],
                    epi=lambda acc, x_, g_: (acc, x_ + 0.5 * g_ * acc))
    return x_out, (h, gate, up, hid, f)


def _ffn_bwd(d_out, x, scale, shift, gate_w, w8, wo4, saved, tag, grad_ready):
    h, gate, up, hid, f = saved
    S = x.shape[0]
    tm = _pick(S, (512, 256, 128))
    tk = _pick(S, (512, 256, 128))
    n = S // tm
    (df,), (d_gate_w,) = _rowwise_bwd(lambda f_, g_: (0.5 * g_ * f_,), [(f, tm, D_MODEL, 0)], [], [gate_w],
                                      [(d_out, tm, D_MODEL, 0)], n, tag + "_dres", row_dtypes=(BF16,))
    tb = _pick(S, MATMUL_ROWS)
    piece = pl.BlockSpec((None, tb, FFN_PIECE), lambda i, j, k: (j, i, 0))
    d_gate, d_up = _mmg(df, wo4, "nt", name=tag + "_ddown", grid=(S // tb, HID_PIECES, 1),
                        a_spec=pl.BlockSpec((tb, D_MODEL), lambda i, j, k: (i, 0)),
                        b_spec=pl.BlockSpec((None, FFN_PIECE, D_MODEL), lambda i, j, k: (j, 0, 0)),
                        out_spec=piece, out_shapes=[jax.ShapeDtypeStruct((HID_PIECES, S, FFN_PIECE), BF16)] * 2,
                        acc_shape=(tb, FFN_PIECE), extras=[gate, up], extra_specs=[piece, piece], epi=_swiglu_bwd)
    g_wo4 = _mmg(hid, df, "tn", name=tag + "_gwo", grid=(HID_PIECES, 1, S // tk),
                 a_spec=pl.BlockSpec((None, tk, FFN_PIECE), lambda i, j, k: (i, k, 0)),
                 b_spec=pl.BlockSpec((tk, D_MODEL), lambda i, j, k: (k, j)),
                 out_spec=pl.BlockSpec((None, FFN_PIECE, D_MODEL), lambda i, j, k: (i, 0, j)),
                 out_shapes=[jax.ShapeDtypeStruct((HID_PIECES, FFN_PIECE, D_MODEL), BF16)], acc_shape=(FFN_PIECE, D_MODEL))
    g_w8 = _ffn_gw8(h, d_gate, d_up, tag + "_gw8", after=grad_ready("wo4", g_wo4))
    dx, d_scale, d_shift = _ffn_dh(d_gate, d_up, w8, x, d_out, scale, shift, tag + "_dh", after=grad_ready("w8", g_w8))
    return dx, d_scale, d_shift, d_gate_w


def _mixer_fwd(x1, scale, shift, gate_w, cos_p, sin_p, W):
    S = x1.shape[0]
    tm = _pick(S, (512, 256, 128))
    tv = _pick(S, (256, 128))
    ta = _pick(S, (512, 256, 128))
    nc = S // CHUNK
    (h2,) = _rowwise(_mod_fn, [(x1, tm, D_MODEL, 0)], [scale, shift], [(tm, D_MODEL, BF16)], S // tm, "mix_mod")
    proj = _mm(h2, W["wp"], "nn", name="mix_proj")
    qkvc = _conv_fwd(proj, W["conv_w"], tv, "gdn_conv")
    kab = (proj, tv, 128, 21)
    q_a, k_a, v_a, gb = _rowwise(_gdn_pre_fn, [(qkvc, tv, 1536, 0), kab], [W["alog_p"], W["dt_p"]],
                                 [(tv, 512, F32)] * 3 + [(tv, 128, F32)], S // tv, "gdn_pre")
    ti = _pick(S, INTRA_ROWS)
    intra = _rowwise(_gdn_intra_fn, [(q_a, ti, 512, 0), (k_a, ti, 512, 0), (v_a, ti, 512, 0), (gb, ti, 128, 0)],
                     [], [(ti, 512, F32)] * 4 + [(ti, CHUNK, F32)] * 4 + [(ti // 8, 512, F32)], S // ti, "gdn_intra")
    u, wk, qd, kd, qks, gl = intra[0], intra[1], intra[2], intra[3], tuple(intra[4:8]), intra[8]
    o_a, s_prev = _gdn_scan_fwd(u, wk, qd, kd, qks, gl, "gdn_scan")
    mla_params = [W["qnw"], W["kvnw"], W["wuq"], W["wukv"], W["qn_w"], W["qr_w"], W["kn_w"], W["kr_w"]]
    def mla_pre_with_vt(*a):
        q_, k_, v_ = _mla_pre_fn(*a)
        return q_, k_, v_, jnp.transpose(v_)

    q_b, k_b, v_b, vt_b = _rowwise(mla_pre_with_vt,
                                   [(proj, tv, 256, 8), (proj, tv, 384, 6), kab, (cos_p, tv, 128, 0), (sin_p, tv, 128, 0)],
                                   mla_params, [(tv, 1024, BF16), (tv, 1024, BF16), (tv, 512, BF16), (512, tv, BF16, "across")],
                                   S // tv, "mla_pre")
    o_b, lse = _attn_fwd(q_b, k_b, vt_b, ta, "mla_attn")
    (mixed,) = _rowwise(_mix_post_fn, [(o_a, tv, 512, 0), (proj, tv, 512, 3), (o_b, tv, 512, 0)], [W["gnw"], W["onw"]],
                        [(tv, D_MODEL, BF16)], S // tv, "mix_post")
    y, x2 = _mm(mixed, W["wout"], "nn", name="mix_out", out_dtypes=(F32, F32), extras=[x1], extra_params=[gate_w],
                epi=lambda acc, x_, g_: (acc, x_ + g_ * acc))
    saved = (h2, proj, qkvc, q_a, k_a, v_a, gb, u, wk, qd, kd, qks, gl, s_prev, o_a, q_b, k_b, v_b, o_b, lse, mixed, y)
    return x2, saved


def _mixer_bwd(d_out, x1, scale, shift, gate_w, cos_p, sin_p, W, saved):
    (h2, proj, qkvc, q_a, k_a, v_a, gb, u, wk, qd, kd, qks, gl, s_prev, o_a, q_b, k_b, v_b, o_b, lse, mixed, y) = saved
    S = x1.shape[0]
    tm = _pick(S, (512, 256, 128))
    tv = _pick(S, (256, 128))
    ta = _pick(S, (512, 256, 128))
    nc = S // CHUNK
    G = {}
    (dy,), (G["g2"],) = _rowwise_bwd(lambda y_, g_: (g_ * y_,), [(y, tm, D_MODEL, 0)], [], [gate_w],
                                     [(d_out, tm, D_MODEL, 0)], S // tm, "mix_dres", row_dtypes=(BF16,))
    d_mixed = _mm(dy, W["wout"], "nt", name="mix_dout")
    G["wout"] = _mm(mixed, dy, "tn", name="mix_gwout")
    (do_a, dz, do_b), (G["gnw"], G["onw"]) = _rowwise_bwd(
        _mix_post_fn, [(o_a, tv, 512, 0), (proj, tv, 512, 3), (o_b, tv, 512, 0)], [], [W["gnw"], W["onw"]],
        [(d_mixed, tv, D_MODEL, 0)], S // tv, "mix_dpost")
    dq_b, stats = _attn_dq(q_b, k_b, v_b, o_b, lse, do_b, ta, "mla_dq")
    dk_b, dv_b = _attn_dkv(q_b, k_b, v_b, do_b, stats, ta, "mla_dkv")
    kab = (proj, tv, 128, 21)
    mla_params = [W["qnw"], W["kvnw"], W["wuq"], W["wukv"], W["qn_w"], W["qr_w"], W["kn_w"], W["kr_w"]]
    (d_ckv, d_cq, d_kab), mla_grads = _rowwise_bwd(
        _mla_pre_fn, [(proj, tv, 256, 8), (proj, tv, 384, 6), kab], [(cos_p, tv, 128, 0), (sin_p, tv, 128, 0)], mla_params,
        [(dq_b, tv, 1024, 0), (dk_b, tv, 1024, 0), (dv_b, tv, 512, 0)], S // tv, "mla_dpre")
    for key, g in zip(("qnw", "kvnw", "wuq", "wukv", "qn_w", "qr_w", "kn_w", "kr_w"), mla_grads):
        G[key] = g
    scan_grads = _gdn_scan_bwd(u, wk, qd, kd, qks, gl, s_prev, do_a, "gdn_dscan")
    ti = _pick(S, INTRA_ROWS)
    intra_douts = [(scan_grads[i], ti, 512, 0) for i in range(4)] + [(scan_grads[4 + i], ti, CHUNK, 0) for i in range(4)]
    intra_douts.append((scan_grads[8], ti // 8, 512, 0))
    (dq_a, dk_a, dv_a, d_gb), _ = _rowwise_bwd(
        _gdn_intra_fn, [(q_a, ti, 512, 0), (k_a, ti, 512, 0), (v_a, ti, 512, 0), (gb, ti, 128, 0)], [], [],
        intra_douts, S // ti, "gdn_dintra")
    (d_qkvc, d_kab), (G["alog_p"], G["dt_p"]) = _rowwise_bwd(
        _gdn_pre_fn, [(qkvc, tv, 1536, 0), kab], [], [W["alog_p"], W["dt_p"]],
        [(dq_a, tv, 512, 0), (dk_a, tv, 512, 0), (dv_a, tv, 512, 0), (d_gb, tv, 128, 0)], S // tv, "gdn_dpre",
        adds=[(1, d_kab)])
    d_qkv, g_conv = _conv_bwd(proj, d_qkvc, W["conv_w"], tv, "gdn_dconv")
    G["conv_w"] = g_conv[:4]
    d_proj = jnp.concatenate([d_qkv, dz, d_ckv, d_cq, d_kab], axis=1).astype(BF16)
    G["wp"] = _mm(h2, d_proj, "tn", name="mix_gwp")
    dh2 = _mm(d_proj, W["wp"], "nt", name="mix_dproj")
    (dx1,), (G["s2"], G["sh2"]) = _rowwise_bwd(_mod_fn, [(x1, tm, D_MODEL, 0)], [], [scale, shift],
                                               [(dh2, tm, D_MODEL, 0)], S // tm, "mix_dmod", adds=[(0, d_out)])
    return dx1, G


def _local_step(x, target, mod, cos_p, sin_p, W1, mixer_weights, ffn2_weights, ffn_grad_ready, mixer_grads_ready):
    sh1, s1, g1, sh2, s2, g2, sh3, s3, g3 = [mod[:, D_MODEL * i:D_MODEL * (i + 1)] for i in range(N_MOD)]
    x1, saved1 = _ffn_fwd(x, s1, sh1, g1, W1["f1_w8"], W1["f1_wo4"], "ffn1")
    W = mixer_weights(x1)
    x2, saved2 = _mixer_fwd(x1, s2, sh2, g2, cos_p, sin_p, W)
    W.update(ffn2_weights(x2))
    x3, saved3 = _ffn_fwd(x2, s3, sh3, g3, W["f2_w8"], W["f2_wo4"], "ffn2")
    dx3, loss_row = _loss_and_grad(x3, target, "loss")
    dx2, d_s3, d_sh3, d_g3 = _ffn_bwd(dx3, x2, s3, sh3, g3, W["f2_w8"], W["f2_wo4"], saved3, "ffn2", ffn_grad_ready("f2"))
    dx1, G = _mixer_bwd(dx2, x1, s2, sh2, g2, cos_p, sin_p, W, saved2)
    d_sh2, d_s2, d_g2 = G.pop("sh2"), G.pop("s2"), G.pop("g2")
    g1 = g1 + mixer_grads_ready(G)
    dx, d_s1, d_sh1, d_g1 = _ffn_bwd(dx1, x, s1, sh1, g1, W1["f1_w8"], W1["f1_wo4"], saved1, "ffn1", ffn_grad_ready("f1"))
    d_mod = jnp.concatenate([d_sh1, d_s1, d_g1, d_sh2, d_s2, d_g2, d_sh3, d_s3, d_g3], axis=1)
    return loss_row, dx, d_mod


WEIGHT_NAMES = ("w_ada", "b_ada", "ffn1_w_in", "ffn1_w_out", "w_in", "gdn_conv_w", "gdn_a_log", "gdn_dt_bias", "gdn_norm_w",
                "mla_q_norm_w", "mla_w_uq", "mla_kv_norm_w", "mla_w_ukv", "qkn_q_nope", "qkn_q_rope", "qkn_k_nope",
                "qkn_k_rope", "mla_out_norm_w", "w_out", "ffn2_w_in", "ffn2_w_out")
FFN_SHARDED = ("ffn1_w_in", "ffn1_w_out", "ffn2_w_in", "ffn2_w_out")
SHEETED = (("w_in", "col"), ("gdn_conv_w", "col"), ("mla_w_uq", "col"), ("mla_w_ukv", "col"), ("w_out", "row"))
MOD_ROWS = N_MOD * D_MODEL // 128
SMALL = {"gdn_a_log": (MOD_ROWS, 1, 64, 4), "gdn_dt_bias": (MOD_ROWS + 1, 1, 64, 4), "gdn_norm_w": (MOD_ROWS + 2, 1, 0, 128),
         "mla_q_norm_w": (MOD_ROWS + 3, 3, 0, 384), "mla_kv_norm_w": (MOD_ROWS + 6, 2, 0, 256),
         "qkn_q_nope": (MOD_ROWS + 8, 1, 0, 128), "qkn_q_rope": (MOD_ROWS + 9, 1, 0, 64), "qkn_k_nope": (MOD_ROWS + 10, 1, 0, 128),
         "qkn_k_rope": (MOD_ROWS + 11, 1, 0, 64), "mla_out_norm_w": (MOD_ROWS + 12, 1, 0, 128)}
LOSS_ROW = MOD_ROWS + 13
CONV_ROW, CONV_ROWS = 88, 4 * 1536 // 128
SHEET_ROWS = CONV_ROW + CONV_ROWS


def _to_sheet(flat, dtype, sublanes):
    n = flat.shape[-1]
    unit = sublanes * 128
    pad = (-n) % unit
    flat = jnp.pad(flat.astype(dtype), [(0, 0)] * (flat.ndim - 1) + [(0, pad)])
    return flat.reshape(flat.shape[:-1] + ((n + pad) // 128, 128))


def _small_sheet(b_like, small):
    sheet = jnp.zeros((SHEET_ROWS, 128), F32).at[:MOD_ROWS].set(b_like.reshape(MOD_ROWS, 128))
    for name, (row, rows, lane, n) in SMALL.items():
        v = small[name].reshape(1, n)
        if rows == 1:
            sheet = sheet.at[row, lane:lane + n].set(v[0])
        else:
            sheet = sheet.at[row:row + rows].set(v.reshape(rows, 128))
    return sheet


def _from_small_sheet(sheet):
    out = {"b_ada": sheet[:MOD_ROWS].reshape(1, N_MOD * D_MODEL)}
    for name, (row, rows, lane, n) in SMALL.items():
        out[name] = sheet[row, lane:lane + n].reshape(1, n) if rows == 1 else sheet[row:row + rows].reshape(1, n)
    return out


def kernel(x, c, positions, w_ada, b_ada, ffn1_w_in, ffn1_w_out, w_in, gdn_conv_w, gdn_a_log, gdn_dt_bias, gdn_norm_w, mla_q_norm_w, mla_w_uq, mla_kv_norm_w, mla_w_ukv, qkn_q_nope, qkn_q_rope, qkn_k_nope, qkn_k_rope, mla_out_norm_w, w_out, ffn2_w_in, ffn2_w_out, loss_target, m_w_ada, m_b_ada, m_ffn1_w_in, m_ffn1_w_out, m_w_in, m_gdn_conv_w, m_gdn_a_log, m_gdn_dt_bias, m_gdn_norm_w, m_mla_q_norm_w, m_mla_w_uq, m_mla_kv_norm_w, m_mla_w_ukv, m_qkn_q_nope, m_qkn_q_rope, m_qkn_k_nope, m_qkn_k_rope, m_mla_out_norm_w, m_w_out, m_ffn2_w_in, m_ffn2_w_out, v_w_ada, v_b_ada, v_ffn1_w_in, v_ffn1_w_out, v_w_in, v_gdn_conv_w, v_gdn_a_log, v_gdn_dt_bias, v_gdn_norm_w, v_mla_q_norm_w, v_mla_w_uq, v_mla_kv_norm_w, v_mla_w_ukv, v_qkn_q_nope, v_qkn_q_rope, v_qkn_k_nope, v_qkn_k_rope, v_mla_out_norm_w, v_w_out, v_ffn2_w_in, v_ffn2_w_out):
    args = locals()
    w = {n: args[n] for n in WEIGHT_NAMES}
    m = {n: args["m_" + n] for n in WEIGHT_NAMES}
    v = {n: args["v_" + n] for n in WEIGHT_NAMES}
    me = 4 * lax.axis_index("x") + 2 * lax.axis_index("y") + lax.axis_index("c")
    cols = N_MOD * D_MODEL // N_DEV
    shard = {n: w[n][0] for n in FFN_SHARDED + tuple(s[0] for s in SHEETED)}

    sc = c * _sigmoid(c)
    first = _to_sheet(jnp.concatenate([sc.reshape(-1), shard["gdn_conv_w"].reshape(-1)]), F32, 8)
    (first_all,) = _all_gather([first], "gather_c")
    sc_all = first_all[:, :D_MODEL // 128].reshape(N_DEV, D_MODEL)
    n_taps = shard["gdn_conv_w"].size
    conv_all = first_all.reshape(N_DEV, -1)[:, D_MODEL:D_MODEL + n_taps].reshape(N_DEV, 4, -1)
    b_mine = lax.dynamic_slice(b_ada, (0, me * cols), (1, cols))
    mod_cols = _mm(sc_all, w_ada[0], "nn", name="ada_mod", extra_params=[b_mine], epi=lambda acc, b_: (acc + b_,))
    (mod_all,) = _all_to_all([_to_sheet(mod_cols, F32, 8)], "scatter_mod")
    mod = mod_all.reshape(N_DEV, -1)[:, :cols].reshape(1, N_MOD * D_MODEL)

    f1_w8, f1_out = _all_gather([shard["ffn1_w_in"].astype(BF16), shard["ffn1_w_out"].astype(BF16)], "gather_w1")
    travel = [s for s in SHEETED if s[0] != "gdn_conv_w"]
    tied = lax.optimization_barrier(([shard[n].astype(BF16) for n, _ in travel], f1_w8))
    f1_w8 = tied[1]
    mixer_w = _exchange_start(tied[0], False, "gather_wm_start")
    ffn2_w = _exchange_start([shard["ffn2_w_in"].astype(BF16) + mixer_w[4][0:1, 0:1].astype(BF16),
                              shard["ffn2_w_out"].astype(BF16)], False, "gather_w2_start")
    mod = mod + ffn2_w[4][0:1, 0:1]
    W1 = dict(f1_w8=f1_w8, f1_wo4=f1_out.reshape(HID_PIECES, FFN_PIECE, D_MODEL))

    def mixer_weights(after):
        got = _exchange_wait(mixer_w, False, after, "gather_wm_wait")
        P = {n: jnp.concatenate(list(g), axis=1) if kind == "col" else g.reshape(-1, g.shape[-1])
             for (n, kind), g in zip(travel, got)}
        P["gdn_conv_w"] = jnp.concatenate(list(conv_all), axis=1)
        for n in SMALL:
            P[n] = w[n]
        return _pack_weights(P)

    def ffn2_weights(after):
        f2_w8, f2_out = _exchange_wait(ffn2_w, False, after, "gather_w2_wait")
        return dict(f2_w8=f2_w8, f2_wo4=f2_out.reshape(HID_PIECES, FFN_PIECE, D_MODEL))

    pending, small_grads = {}, {}

    def ffn_grad_ready(tag):
        def ready(which, g):
            pieces = g if which == "w8" else g.reshape((N_DEV,) + shard["ffn1_w_out"].shape)
            pending[tag + which] = _exchange_start([pieces], True, "scatter_%s_%s_start" % (tag, which))
            return pending[tag + which][4]
        return ready

    def mixer_grads_ready(G):
        g_full = _unpack_grads(G)
        small_grads.update({n: g_full[n] for n in SMALL})
        small_grads["gdn_conv_w"] = g_full["gdn_conv_w"]
        pieces = []
        for n, kind in travel:
            r, cc = shard[n].shape
            g = g_full[n].astype(BF16)
            pieces.append(jnp.stack([g[:, cc * p:cc * (p + 1)] for p in range(N_DEV)]) if kind == "col"
                          else g.reshape(N_DEV, r, cc))
        pending["mixer"] = _exchange_start(pieces, True, "scatter_mx_start")
        return pending["mixer"][4][0:1, 0:1]

    cos_p, sin_p = _rope_tables(positions[0])
    loss_row, dx, d_mod = _local_step(x[0], loss_target[0], mod, cos_p, sin_p, W1, mixer_weights, ffn2_weights,
                                      ffn_grad_ready, mixer_grads_ready)

    sheet = _small_sheet(d_mod, small_grads).at[LOSS_ROW].set(loss_row[0, :128])
    sheet = sheet.at[CONV_ROW:CONV_ROW + CONV_ROWS].set(small_grads["gdn_conv_w"].reshape(CONV_ROWS, 128))
    (sheets,) = _all_gather([sheet], "gather_small")
    summed = _sum_devices(sheets, "sum_small")
    d_mod_all = sheets[:, :MOD_ROWS].reshape(N_DEV, N_MOD * D_MODEL)
    d_mod_mine = lax.dynamic_slice(d_mod_all, (0, me * cols), (N_DEV, cols))
    grads = _from_small_sheet(summed)
    grads["w_ada"] = _mm(sc_all, d_mod_mine, "tn", name="ada_gw", hi=True)
    conv_taps = shard["gdn_conv_w"].shape[1]
    grads["gdn_conv_w"] = lax.dynamic_slice(summed[CONV_ROW:CONV_ROW + CONV_ROWS].reshape(4, -1), (0, me * conv_taps),
                                            (4, conv_taps))
    loss = summed[LOSS_ROW, 0]

    for n, key in zip(FFN_SHARDED, ("f1w8", "f1wo4", "f2w8", "f2wo4")):
        (parts,) = _exchange_wait(pending[key], True, summed, "scatter_%s_wait" % key)
        grads[n] = _sum_devices(parts, "sum_" + n)
    for (n, _), parts in zip(travel, _exchange_wait(pending["mixer"], True, summed, "scatter_mx_wait")):
        grads[n] = _sum_devices(parts, "sum_" + n)

    delta, new_m, new_v = {}, {}, {}
    for n in ("w_ada",) + FFN_SHARDED + tuple(s[0] for s in SHEETED):
        delta[n], new_m[n], new_v[n] = _adamw(w[n][0], grads[n], m[n][0], v[n][0], "adamw_" + n)
    small_in = [_small_sheet(t["b_ada"], t) for t in (w, grads, m, v)]
    for res, out in zip(_adamw(*small_in, "adamw_small"), (delta, new_m, new_v)):
        out.update(_from_small_sheet(res))

    def shaped(d):
        return [d[n].reshape(w[n].shape) for n in WEIGHT_NAMES]

    return (loss, dx[None], *shaped(grads), *shaped(delta), *shaped(new_m), *shaped(new_v))
```

```python
import functools

import jax
import jax.numpy as jnp
import numpy as np
from jax import lax
from jax.experimental import pallas as pl
from jax.experimental.pallas import tpu as pltpu

F32 = jnp.float32
BF16 = jnp.bfloat16

D_MODEL = 1024
D_FF = 2816
N_MOD = 9
HEADS = 4
HEAD_DIM = 128
CHUNK = 64
EPS = 1e-6
ROPE = 64
Q_LORA = 384
KV_LORA = 256
N_IN = 2760
N_IN_PACKED = 2816
ROPE_BASE = 10000.0
N_DEV = 8

ADAM_LR = 0.001
ADAM_B1 = 0.9
ADAM_B2 = 0.999
ADAM_EPS = 1e-08
ADAM_WD = 0.01
ADAM_STEP = 10

VMEM_LIMIT_BYTES = 56 * 1024 * 1024
MATMUL_ROWS = (1024, 512, 256, 128)
MESH = pl.DeviceIdType.MESH


def _params(sem=None):
    return pltpu.CompilerParams(dimension_semantics=sem, vmem_limit_bytes=VMEM_LIMIT_BYTES)


def _pick(dim, prefs):
    for p in prefs:
        if dim % p == 0:
            return p
    return dim


_DIMS = {"nn": (((1,), (0,)), ((), ())), "nt": (((1,), (1,)), ((), ())), "tn": (((0,), (0,)), ((), ()))}


def _dot_raw(a, b, mode):
    return lax.dot_general(a.astype(BF16), b.astype(BF16), _DIMS[mode], preferred_element_type=F32)


def _dot_hi(a, b, mode="nn"):
    return lax.dot_general(a, b, _DIMS[mode], precision=lax.Precision.HIGHEST, preferred_element_type=F32)


@functools.partial(jax.custom_vjp, nondiff_argnums=(2,))
def _bdot(a, b, mode):
    return _dot_raw(a, b, mode)


def _bdot_fwd(a, b, mode):
    return _dot_raw(a, b, mode), (a, b)


def _bdot_bwd(mode, res, g):
    a, b = res
    if mode == "nn":
        return _dot_raw(g, b, "nt"), _dot_raw(a, g, "tn")
    if mode == "nt":
        return _dot_raw(g, b, "nn"), _dot_raw(g, a, "tn")
    return _dot_raw(b, g, "nt"), _dot_raw(a, g, "nn")


_bdot.defvjp(_bdot_fwd, _bdot_bwd)


def _mm(a, b, mode, *, name, out_dtypes=(F32,), epi=None, extras=(), extra_params=(), hi=False,
        tm=None, tn=None, tk=None):
    if mode == "nn":
        (M, K), (_, N) = a.shape, b.shape
    elif mode == "nt":
        (M, K), (N, _) = a.shape, b.shape
    else:
        (K, M), (_, N) = a.shape, b.shape
    tm = tm or _pick(M, (512, 1408, 256, 128) if mode == "tn" else MATMUL_ROWS + (384, 352))
    tn = tn or _pick(N, (1024, 1408, 768, 512, 384, 256, 128))
    tk = tk or _pick(K, (1024, 1408, 512, 384, 256, 128))
    a_spec = {"nn": pl.BlockSpec((tm, tk), lambda i, j, k: (i, k)), "nt": pl.BlockSpec((tm, tk), lambda i, j, k: (i, k)),
              "tn": pl.BlockSpec((tk, tm), lambda i, j, k: (k, i))}[mode]
    b_spec = {"nn": pl.BlockSpec((tk, tn), lambda i, j, k: (k, j)), "nt": pl.BlockSpec((tn, tk), lambda i, j, k: (j, k)),
              "tn": pl.BlockSpec((tk, tn), lambda i, j, k: (k, j))}[mode]
    mn_spec = pl.BlockSpec((tm, tn), lambda i, j, k: (i, j))
    return _mmg(a, b, mode, name=name, grid=(M // tm, N // tn, K // tk), a_spec=a_spec, b_spec=b_spec, out_spec=mn_spec,
                out_shapes=[jax.ShapeDtypeStruct((M, N), dt) for dt in out_dtypes], acc_shape=(tm, tn), epi=epi,
                extras=list(extras) + list(extra_params),
                extra_specs=[mn_spec] * len(extras) + [pl.BlockSpec((1, tn), lambda i, j, k: (0, j))] * len(extra_params),
                hi=hi)


def _mmg(a, b, mode, *, name, grid, a_spec, b_spec, out_spec, out_shapes, acc_shape, epi=None, extras=(),
         extra_specs=(), hi=False):
    nk = grid[2]
    n_e, n_o = len(extras), len(out_shapes)

    def body(*refs):
        a_ref, b_ref = refs[:2]
        e_refs = refs[2:2 + n_e]
        o_refs = refs[2 + n_e:2 + n_e + n_o]
        acc_ref = refs[-1]
        k = pl.program_id(2)

        @pl.when(k == 0)
        def _():
            acc_ref[...] = jnp.zeros_like(acc_ref)

        if hi:
            acc_ref[...] += _dot_hi(a_ref[...].astype(F32), b_ref[...].astype(F32), mode)
        else:
            acc_ref[...] += _dot_raw(a_ref[...], b_ref[...], mode)

        @pl.when(k == nk - 1)
        def _():
            acc = acc_ref[...]
            outs = (acc,) if epi is None else epi(acc, *[e[...].astype(F32) for e in e_refs])
            for o_ref, o in zip(o_refs, outs):
                o_ref[...] = o.astype(o_ref.dtype)

    outs = pl.pallas_call(
        body, name=name, grid=grid,
        in_specs=[a_spec, b_spec] + list(extra_specs),
        out_specs=[out_spec] * n_o,
        out_shape=list(out_shapes),
        scratch_shapes=[pltpu.VMEM(acc_shape, F32)],
        compiler_params=_params(("parallel", "parallel", "arbitrary")),
    )(a, b, *extras)
    return outs if n_o > 1 else outs[0]


def _row_spec(th, cw, ci):
    return pl.BlockSpec((th, cw), lambda i: (i, ci))


def _full_spec(shape):
    return pl.BlockSpec(shape, lambda i: (0,) * len(shape))


def _rowwise(fn, rows, params, outs, n_steps, name):
    n_r, n_p, n_o = len(rows), len(params), len(outs)

    def body(*refs):
        vals = [r[...].astype(F32) for r in refs[:n_r + n_p]]
        res = fn(*vals)
        for o_ref, o in zip(refs[n_r + n_p:], res):
            o_ref[...] = o.astype(o_ref.dtype)

    across = [len(o) == 4 for o in outs]
    res = pl.pallas_call(
        body, name=name, grid=(n_steps,),
        in_specs=[_row_spec(th, cw, ci) for (_, th, cw, ci) in rows] + [_full_spec(p.shape) for p in params],
        out_specs=[pl.BlockSpec((o[0], o[1]), lambda i: (0, i)) if ac else _row_spec(o[0], o[1], 0)
                   for o, ac in zip(outs, across)],
        out_shape=[jax.ShapeDtypeStruct((o[0], n_steps * o[1]) if ac else (n_steps * o[0], o[1]), o[2])
                   for o, ac in zip(outs, across)],
        compiler_params=_params(("parallel",)),
    )(*[r[0] for r in rows], *params)
    return res


def _rowwise_bwd(fn, rows, aux, params, douts, n_steps, name, row_dtypes=None, adds=()):
    n_r, n_a, n_p, n_d, n_add = len(rows), len(aux), len(params), len(douts), len(adds)
    row_dtypes = row_dtypes or (F32,) * n_r

    def body(*refs):
        it = iter(refs)
        r_vals = [next(it)[...].astype(F32) for _ in range(n_r)]
        a_vals = [next(it)[...].astype(F32) for _ in range(n_a)]
        p_vals = [next(it)[...].astype(F32) for _ in range(n_p)]
        d_vals = [next(it)[...].astype(F32) for _ in range(n_d)]
        add_vals = [next(it)[...].astype(F32) for _ in range(n_add)]
        dr_refs = [next(it) for _ in range(n_r)]
        dp_refs = [next(it) for _ in range(n_p)]

        def f(*rp):
            return tuple(fn(*rp[:n_r], *a_vals, *rp[n_r:]))

        _, vjp = jax.vjp(f, *r_vals, *p_vals)
        grads = list(vjp(tuple(d_vals)))
        for (ri, _), av in zip(adds, add_vals):
            grads[ri] = grads[ri] + av
        for dr_ref, g in zip(dr_refs, grads[:n_r]):
            dr_ref[...] = g.astype(dr_ref.dtype)

        @pl.when(pl.program_id(0) == 0)
        def _():
            for dp_ref in dp_refs:
                dp_ref[...] = jnp.zeros_like(dp_ref)

        for dp_ref, g in zip(dp_refs, grads[n_r:]):
            dp_ref[...] += g

    all_rows = list(rows) + list(aux) + list(douts) + [(arr,) + tuple(rows[ri][1:3]) + (0,) for ri, arr in adds]
    in_specs = ([_row_spec(th, cw, ci) for (_, th, cw, ci) in list(rows) + list(aux)]
                + [_full_spec(p.shape) for p in params]
                + [_row_spec(th, cw, ci) for (_, th, cw, ci) in all_rows[n_r + n_a:]])
    res = pl.pallas_call(
        body, name=name, grid=(n_steps,),
        in_specs=in_specs,
        out_specs=[_row_spec(th, cw, 0) for (_, th, cw, _) in rows] + [_full_spec(p.shape) for p in params],
        out_shape=[jax.ShapeDtypeStruct((n_steps * th, cw), dt) for (_, th, cw, _), dt in zip(rows, row_dtypes)]
        + [jax.ShapeDtypeStruct(p.shape, F32) for p in params],
        compiler_params=_params(("arbitrary",)),
    )(*[r[0] for r in list(rows) + list(aux)], *params, *[r[0] for r in all_rows[n_r + n_a:]])
    return res[:n_r], res[n_r:]


def _sigmoid(x):
    return lax.logistic(x)


def _silu(x):
    return x * _sigmoid(x)


def _rms(x, w=None, n=None):
    n = n or x.shape[-1]
    y = x * lax.rsqrt(jnp.sum(x * x, axis=-1, keepdims=True) * (1.0 / n) + EPS)
    return y if w is None else y * w


def _modulate(x, scale, shift):
    return _rms(x) * (1.0 + scale) + shift


def _softplus(x):
    return jnp.maximum(x, 0.0) + jnp.log1p(jnp.exp(-jnp.abs(x)))


@jax.custom_vjp
def _rot_half64(x):
    lane = lax.broadcasted_iota(jnp.int32, x.shape, 1)
    up = pltpu.roll(x, 96, 1)
    down = pltpu.roll(x, 32, 1)
    return jnp.where(lane < 32, up, jnp.where(lane < 64, down, 0.0))


_rot_half64.defvjp(lambda x: (_rot_half64(x), None), lambda _, g: (_rot_half64(g),))


def _rope128(x, cos_p, sin_p):
    return x * cos_p + _rot_half64(x) * sin_p


def _gdn_pre_fn(qkvc, kab, alog_p, dt_p):
    a = _silu(qkvc)
    qs, ks = [], []
    for h in range(HEADS):
        qh = a[:, HEAD_DIM * h:HEAD_DIM * (h + 1)]
        kh = a[:, 512 + HEAD_DIM * h:512 + HEAD_DIM * (h + 1)]
        qs.append(qh * lax.rsqrt(jnp.sum(qh * qh, axis=-1, keepdims=True) + EPS) * (HEAD_DIM ** -0.5))
        ks.append(kh * lax.rsqrt(jnp.sum(kh * kh, axis=-1, keepdims=True) + EPS))
    lane = lax.broadcasted_iota(jnp.int32, kab.shape, 1)
    g_full = -jnp.exp(alog_p) * _softplus(kab + dt_p)
    b_full = _sigmoid(kab)
    gb = jnp.where((lane >= 64) & (lane < 68), g_full, jnp.where((lane >= 68) & (lane < 72), b_full, 0.0))
    return jnp.concatenate(qs, axis=1), jnp.concatenate(ks, axis=1), a[:, 1024:1536], gb


def _intra_head(q, k, v, g_col, b_col):
    c = CHUNK
    row = lax.broadcasted_iota(jnp.int32, (c, c), 0)
    col = lax.broadcasted_iota(jnp.int32, (c, c), 1)
    incl, strict, eye = row >= col, row > col, row == col
    tri = jnp.where(incl, 1.0, 0.0).astype(F32)
    ident = jnp.where(eye, 1.0, 0.0).astype(F32)
    g_wide = _dot_hi(tri, jnp.broadcast_to(g_col, (c, HEAD_DIM)))
    g_i = g_wide[:, :c]
    g_j = jnp.sum(jnp.where(eye, g_i, 0.0), axis=0, keepdims=True)
    decay = jnp.where(incl, jnp.exp(jnp.where(incl, g_i - g_j, 0.0)), 0.0)
    kk = _bdot(k, k, "nt")
    a_mat = jnp.where(strict, b_col * kk * decay, 0.0)
    x_pow = -a_mat
    inv = ident + x_pow
    for _ in range(5):
        x_pow = _dot_hi(x_pow, x_pow)
        inv = inv + _dot_hi(inv, x_pow)
    e_wide = jnp.exp(g_wide)
    u = _dot_hi(inv, v * b_col)
    wk = _dot_hi(inv, k * b_col * e_wide)
    qk = _bdot(q, k, "nt") * decay
    last = lax.broadcasted_iota(jnp.int32, (c, HEAD_DIM), 0) == c - 1
    g_last = jnp.sum(jnp.where(last, g_wide, 0.0), axis=0, keepdims=True)
    qd = q * e_wide
    kd = k * jnp.exp(g_last - g_wide)
    gl = jnp.broadcast_to(jnp.exp(g_last), (8, HEAD_DIM))
    return u, wk, qd, kd, qk, gl


INTRA_ROWS = (256, 128, 64)

_BNN = (((2,), (1,)), ((0,), (0,)))
_BNT = (((2,), (2,)), ((0,), (0,)))


def _split_bf16(a):
    hi = a.astype(BF16)
    return hi, (a - hi.astype(F32)).astype(BF16)


def _dot3_raw(a, b, dims):
    a_hi, a_lo = _split_bf16(a)
    b_hi, b_lo = _split_bf16(b)
    dot = lambda x_, y_: lax.dot_general(x_, y_, dims, preferred_element_type=F32)
    return dot(a_hi, b_hi) + (dot(a_hi, b_lo) + dot(a_lo, b_hi))


@functools.partial(jax.custom_vjp, nondiff_argnums=(2, 3))
def _dot3(a, b, nt, exact_bwd=True):
    return _dot3_raw(a, b, _BNT if nt else _BNN)


def _dot3_fwd(a, b, nt, exact_bwd):
    return _dot3_raw(a, b, _BNT if nt else _BNN), (a, b)


def _dot3_bwd(nt, exact_bwd, res, g):
    a, b = res
    if exact_bwd:
        dot = _dot3_raw
    else:
        dot = lambda x_, y_, d_: lax.dot_general(x_.astype(BF16), y_.astype(BF16), d_, preferred_element_type=F32)
    if nt:
        return dot(g, b, _BNN), dot(jnp.swapaxes(g, 1, 2), a, _BNN)
    return dot(g, b, _BNT), dot(jnp.swapaxes(a, 1, 2), g, _BNN)


_dot3.defvjp(_dot3_fwd, _dot3_bwd)


@functools.partial(jax.custom_vjp, nondiff_argnums=(2,))
def _bdot_b(a, b, nt):
    return lax.dot_general(a.astype(BF16), b.astype(BF16), _BNT if nt else _BNN, preferred_element_type=F32)


def _bdot_b_fwd(a, b, nt):
    return _bdot_b(a, b, nt), (a, b)


def _bdot_b_bwd(nt, res, g):
    a, b = res
    dot = lambda x_, y_, d_: lax.dot_general(x_.astype(BF16), y_.astype(BF16), d_, preferred_element_type=F32)
    if nt:
        return dot(g, b, _BNN), dot(jnp.swapaxes(g, 1, 2), a, _BNN)
    return dot(g, b, _BNT), dot(jnp.swapaxes(a, 1, 2), g, _BNN)


_bdot_b.defvjp(_bdot_b_fwd, _bdot_b_bwd)


def _intra_batched(q, k, v, g_col, b_col):
    c = CHUNK
    nb = q.shape[0]
    row = lax.broadcasted_iota(jnp.int32, (1, c, c), 1)
    col = lax.broadcasted_iota(jnp.int32, (1, c, c), 2)
    incl, strict, eye = row >= col, row > col, row == col
    tri = jnp.broadcast_to(jnp.where(incl, 1.0, 0.0).astype(F32), (nb, c, c))
    ident = jnp.where(eye, 1.0, 0.0).astype(F32)
    g_wide = _dot3(tri, jnp.broadcast_to(g_col, (nb, c, HEAD_DIM)), False)
    g_i = g_wide[:, :, :c]
    g_j = jnp.sum(jnp.where(eye, g_i, 0.0), axis=1, keepdims=True)
    decay = jnp.where(incl, jnp.exp(jnp.where(incl, g_i - g_j, 0.0)), 0.0)
    kk = _bdot_b(k, k, True)
    a_mat = jnp.where(strict, b_col * kk * decay, 0.0)
    x_pow = -a_mat
    inv = ident + x_pow
    for _ in range(5):
        x_pow = _dot3(x_pow, x_pow, False, False)
        inv = inv + _dot3(inv, x_pow, False, False)
    e_wide = jnp.exp(g_wide)
    u = _dot3(inv, v * b_col, False)
    wk = _dot3(inv, k * b_col * e_wide, False)
    qk = _bdot_b(q, k, True) * decay
    last = lax.broadcasted_iota(jnp.int32, (1, c, HEAD_DIM), 1) == c - 1
    g_last = jnp.sum(jnp.where(last, g_wide, 0.0), axis=1, keepdims=True)
    qd = q * e_wide
    kd = k * jnp.exp(g_last - g_wide)
    gl = jnp.broadcast_to(jnp.exp(g_last), (nb, 8, HEAD_DIM))
    return u, wk, qd, kd, qk, gl


def _gdn_intra_fn(q, k, v, gb):
    t = q.shape[0]
    nch = t // CHUNK
    lane = lax.broadcasted_iota(jnp.int32, gb.shape, 1)

    def heads_first(x_):
        return jnp.concatenate([x_[:, HEAD_DIM * h:HEAD_DIM * (h + 1)].reshape(nch, CHUNK, HEAD_DIM) for h in range(HEADS)],
                               axis=0)

    def column(first_lane):
        return jnp.concatenate([jnp.sum(jnp.where(lane == first_lane + h, gb, 0.0), axis=1, keepdims=True)
                                .reshape(nch, CHUNK, 1) for h in range(HEADS)], axis=0)

    u, wk, qd, kd, qk, gl = _intra_batched(heads_first(q), heads_first(k), heads_first(v), column(64), column(68))

    def rows_first(x_):
        r, w_ = x_.shape[1], x_.shape[2]
        return jnp.concatenate([x_[nch * h:nch * (h + 1)].reshape(nch * r, w_) for h in range(HEADS)], axis=1)

    qks = [qk[nch * h:nch * (h + 1)].reshape(t, CHUNK) for h in range(HEADS)]
    return (rows_first(u), rows_first(wk), rows_first(qd), rows_first(kd), *qks, rows_first(gl))


def _scan_step(s0, u, wk, qd, kd, qk, gl):
    v_new = u - _bdot_b(wk, s0, False)
    o = _bdot_b(qd, s0, False) + _bdot_b(qk, v_new, False)
    s1 = s0 * gl[:, 0:1, :] + _bdot_b(jnp.swapaxes(kd, 1, 2), v_new, False)
    return o, s1


def _mix_post_fn(o_a, z, o_b, gnw, onw):
    parts = [_rms(o_a[:, HEAD_DIM * h:HEAD_DIM * (h + 1)], gnw) * _silu(z[:, HEAD_DIM * h:HEAD_DIM * (h + 1)])
             for h in range(HEADS)]
    parts += [_rms(o_b[:, HEAD_DIM * h:HEAD_DIM * (h + 1)], onw) for h in range(HEADS)]
    return (jnp.concatenate(parts, axis=1),)


def _mla_pre_fn(ckv, cq, kab, cos_p, sin_p, qnw, kvnw, wuq, wukv, qn_w, qr_w, kn_w, kr_w):
    scale = (HEAD_DIM + ROPE) ** -0.5
    qf = _bdot(_rms(cq, qnw), wuq, "nn")
    kvf = _bdot(_rms(ckv, kvnw), wukv, "nn")
    lane = lax.broadcasted_iota(jnp.int32, kab.shape, 1)
    kr = _rope128(_rms(jnp.where(lane < ROPE, kab, 0.0), kr_w, n=ROPE), cos_p, sin_p)
    qs, ks = [], []
    for h in range(HEADS):
        qn = _rms(qf[:, 256 * h:256 * h + 128], qn_w) * scale
        qr = _rope128(_rms(qf[:, 256 * h + 128:256 * h + 256], qr_w, n=ROPE), cos_p, sin_p) * scale
        qs += [qn, qr]
        ks += [_rms(kvf[:, 128 * h:128 * (h + 1)], kn_w), kr]
    return jnp.concatenate(qs, axis=1), jnp.concatenate(ks, axis=1), kvf[:, 512:]


def _conv_fwd(proj, conv_w, tm, name):
    S = proj.shape[0]
    C = 1536
    nb = tm // 8

    def body(x_ref, prev_ref, w_ref, o_ref, ext_ref):
        i = pl.program_id(0)
        ext_ref[0:8, :] = jnp.where(i > 0, prev_ref[...], 0.0)
        ext_ref[8:, :] = x_ref[...]
        acc = jnp.zeros((tm, C), F32)
        for k in range(4):
            acc = acc + w_ref[k:k + 1, :] * ext_ref[pl.ds(5 + k, tm), :]
        o_ref[...] = acc

    return pl.pallas_call(
        body, name=name, grid=(S // tm,),
        in_specs=[pl.BlockSpec((tm, C), lambda i: (i, 0)),
                  pl.BlockSpec((8, C), lambda i: (jnp.maximum(i * nb - 1, 0), 0)),
                  pl.BlockSpec((4, C), lambda i: (0, 0))],
        out_specs=pl.BlockSpec((tm, C), lambda i: (i, 0)),
        out_shape=jax.ShapeDtypeStruct((S, C), F32),
        scratch_shapes=[pltpu.VMEM((tm + 8, C), F32)],
        compiler_params=_params(("arbitrary",)),
    )(proj, proj, conv_w)


def _conv_bwd(proj, dout, conv_w, tm, name):
    S = proj.shape[0]
    C = 1536
    nb = tm // 8
    n_steps = S // tm

    def body(x_ref, prev_ref, d_ref, next_ref, w_ref, dx_ref, dw_ref, xext_ref, dext_ref):
        i = pl.program_id(0)
        xext_ref[0:8, :] = jnp.where(i > 0, prev_ref[...], 0.0)
        xext_ref[8:, :] = x_ref[...]
        dext_ref[0:tm, :] = d_ref[...]
        dext_ref[tm:, :] = jnp.where(i < n_steps - 1, next_ref[...], 0.0)
        d = d_ref[...]
        acc = jnp.zeros((tm, C), F32)
        dws = []
        for k in range(4):
            acc = acc + w_ref[k:k + 1, :] * dext_ref[pl.ds(3 - k, tm), :]
            dws.append(jnp.sum(d * xext_ref[pl.ds(5 + k, tm), :], axis=0, keepdims=True))
        dx_ref[...] = acc

        @pl.when(i == 0)
        def _():
            dw_ref[...] = jnp.zeros_like(dw_ref)

        dw_ref[...] += jnp.concatenate(dws + [jnp.zeros((4, C), F32)], axis=0)

    return pl.pallas_call(
        body, name=name, grid=(n_steps,),
        in_specs=[pl.BlockSpec((tm, C), lambda i: (i, 0)),
                  pl.BlockSpec((8, C), lambda i: (jnp.maximum(i * nb - 1, 0), 0)),
                  pl.BlockSpec((tm, C), lambda i: (i, 0)),
                  pl.BlockSpec((8, C), lambda i: (jnp.minimum((i + 1) * nb, S // 8 - 1), 0)),
                  pl.BlockSpec((4, C), lambda i: (0, 0))],
        out_specs=[pl.BlockSpec((tm, C), lambda i: (i, 0)), pl.BlockSpec((8, C), lambda i: (0, 0))],
        out_shape=[jax.ShapeDtypeStruct((S, C), F32), jax.ShapeDtypeStruct((8, C), F32)],
        scratch_shapes=[pltpu.VMEM((tm + 8, C), F32), pltpu.VMEM((tm + 8, C), F32)],
        compiler_params=_params(("arbitrary",)),
    )(proj, proj, dout, dout, conv_w)


def _gdn_scan_fwd(u, wk, qd, kd, qks, gl, name):
    S = u.shape[0]
    nc = S // CHUNK
    W = HEADS * HEAD_DIM

    def body(u_ref, wk_ref, qd_ref, kd_ref, qk0, qk1, qk2, qk3, gl_ref, o_ref, sp_ref, s_ref):
        @pl.when(pl.program_id(0) == 0)
        def _():
            s_ref[...] = jnp.zeros_like(s_ref)

        s0 = s_ref[...]
        sp_ref[0] = s0
        o, s1 = _scan_step(s0, _heads(u_ref, HEAD_DIM), _heads(wk_ref, HEAD_DIM), _heads(qd_ref, HEAD_DIM),
                           _heads(kd_ref, HEAD_DIM), jnp.stack([r[...] for r in (qk0, qk1, qk2, qk3)]),
                           _heads(gl_ref, HEAD_DIM))
        s_ref[...] = s1
        for h in range(HEADS):
            o_ref[:, HEAD_DIM * h:HEAD_DIM * (h + 1)] = o[h]

    row = pl.BlockSpec((CHUNK, W), lambda n: (n, 0))
    qk_spec = pl.BlockSpec((CHUNK, CHUNK), lambda n: (n, 0))
    return pl.pallas_call(
        body, name=name, grid=(nc,),
        in_specs=[row, row, row, row, qk_spec, qk_spec, qk_spec, qk_spec, pl.BlockSpec((8, W), lambda n: (n, 0))],
        out_specs=[row, pl.BlockSpec((1, HEADS, HEAD_DIM, HEAD_DIM), lambda n: (n, 0, 0, 0))],
        out_shape=[jax.ShapeDtypeStruct((S, W), F32), jax.ShapeDtypeStruct((nc, HEADS, HEAD_DIM, HEAD_DIM), F32)],
        scratch_shapes=[pltpu.VMEM((HEADS, HEAD_DIM, HEAD_DIM), F32)],
        compiler_params=_params(("arbitrary",)),
    )(u, wk, qd, kd, *qks, gl)


def _gdn_scan_bwd(u, wk, qd, kd, qks, gl, s_prev, d_o, name):
    S = u.shape[0]
    nc = S // CHUNK
    W = HEADS * HEAD_DIM

    def body(u_ref, wk_ref, qd_ref, kd_ref, qk0, qk1, qk2, qk3, gl_ref, sp_ref, do_ref,
             du_ref, dwk_ref, dqd_ref, dkd_ref, dqk0, dqk1, dqk2, dqk3, dgl_ref, ds_ref):
        @pl.when(pl.program_id(0) == 0)
        def _():
            ds_ref[...] = jnp.zeros_like(ds_ref)

        _, vjp = jax.vjp(_scan_step, sp_ref[0], _heads(u_ref, HEAD_DIM), _heads(wk_ref, HEAD_DIM), _heads(qd_ref, HEAD_DIM),
                         _heads(kd_ref, HEAD_DIM), jnp.stack([r[...] for r in (qk0, qk1, qk2, qk3)]),
                         _heads(gl_ref, HEAD_DIM))
        ds0, du, dwk, dqd, dkd, dqk, dgl = vjp((_heads(do_ref, HEAD_DIM), ds_ref[...]))
        ds_ref[...] = ds0
        for h, dqk_ref in enumerate((dqk0, dqk1, dqk2, dqk3)):
            sl = slice(HEAD_DIM * h, HEAD_DIM * (h + 1))
            du_ref[:, sl] = du[h]
            dwk_ref[:, sl] = dwk[h]
            dqd_ref[:, sl] = dqd[h]
            dkd_ref[:, sl] = dkd[h]
            dqk_ref[...] = dqk[h]
            dgl_ref[:, sl] = dgl[h]

    rev = lambda n: (nc - 1 - n, 0)
    row = pl.BlockSpec((CHUNK, W), rev)
    qk_spec = pl.BlockSpec((CHUNK, CHUNK), rev)
    gl_spec = pl.BlockSpec((8, W), rev)
    qk_shape = jax.ShapeDtypeStruct((S, CHUNK), F32)
    row_shape = jax.ShapeDtypeStruct((S, W), F32)
    return pl.pallas_call(
        body, name=name, grid=(nc,),
        in_specs=[row, row, row, row, qk_spec, qk_spec, qk_spec, qk_spec, gl_spec,
                  pl.BlockSpec((1, HEADS, HEAD_DIM, HEAD_DIM), lambda n: (nc - 1 - n, 0, 0, 0)), row],
        out_specs=[row, row, row, row, qk_spec, qk_spec, qk_spec, qk_spec, gl_spec],
        out_shape=[row_shape] * 4 + [qk_shape] * 4 + [jax.ShapeDtypeStruct((nc * 8, W), F32)],
        scratch_shapes=[pltpu.VMEM((HEADS, HEAD_DIM, HEAD_DIM), F32)],
        compiler_params=_params(("arbitrary",)),
    )(u, wk, qd, kd, *qks, gl, s_prev, d_o)


NEG = -1e30


def _chunk_mask(i, j, t, transposed=False):
    q_axis, k_axis = (1, 0) if transposed else (0, 1)
    r = (i * t + lax.broadcasted_iota(jnp.int32, (t, t), q_axis)) // CHUNK
    c = (j * t + lax.broadcasted_iota(jnp.int32, (t, t), k_axis)) // CHUNK
    return c <= r


def _tile_pairs(n, by_key):
    pairs = [(i, j) for j in range(n) for i in range(j, n)] if by_key else [(i, j) for i in range(n) for j in range(i + 1)]
    return jnp.asarray(np.array([p[0] for p in pairs], np.int32)), jnp.asarray(np.array([p[1] for p in pairs], np.int32))


def _heads(ref, width):
    return jnp.stack([ref[:, width * h:width * (h + 1)] for h in range(HEADS)])


def _bmm(a, b, dims):
    return lax.dot_general(a.astype(BF16), b.astype(BF16), dims, preferred_element_type=F32)


def _attn_fwd(q, k, v_t, t, name):
    S = q.shape[0]
    n = S // t
    qi, kj = _tile_pairs(n, by_key=False)

    def body(qi_ref, kj_ref, q_ref, k_ref, vt_ref, o_ref, lse_ref, m_ref, l_ref, acc_ref):
        i, j = qi_ref[pl.program_id(0)], kj_ref[pl.program_id(0)]

        @pl.when(j == 0)
        def _():
            m_ref[...] = jnp.full_like(m_ref, NEG)
            l_ref[...] = jnp.zeros_like(l_ref)
            acc_ref[...] = jnp.zeros_like(acc_ref)

        def update(masked):
            s_t = _bmm(_heads(k_ref, 256), _heads(q_ref, 256), _BNT)
            if masked:
                s_t = jnp.where(_chunk_mask(i, j, t, transposed=True)[None], s_t, NEG)
            m_old = m_ref[...]
            m_new = jnp.maximum(m_old, jnp.max(s_t, axis=1, keepdims=True))
            p_t = jnp.exp(s_t - m_new)
            alpha = jnp.exp(m_old - m_new)
            l_ref[...] = alpha * l_ref[...] + jnp.sum(p_t, axis=1, keepdims=True)
            v_heads = jnp.stack([vt_ref[HEAD_DIM * h:HEAD_DIM * (h + 1), :] for h in range(HEADS)])
            acc_ref[...] = alpha * acc_ref[...] + _bmm(v_heads, p_t, _BNN)
            m_ref[...] = m_new

        @pl.when(j < i)
        def _():
            update(False)

        @pl.when(j == i)
        def _():
            update(True)
            for h in range(HEADS):
                sl = slice(HEAD_DIM * h, HEAD_DIM * (h + 1))
                o_ref[:, sl] = jnp.transpose(acc_ref[h] / l_ref[h])
                lse_ref[:, sl] = jnp.transpose(jnp.broadcast_to(m_ref[h] + jnp.log(l_ref[h]), (HEAD_DIM, t)))

    row = lambda p, qi_, kj_: (qi_[p], 0)
    return pl.pallas_call(
        body, name=name,
        grid_spec=pltpu.PrefetchScalarGridSpec(
            num_scalar_prefetch=2, grid=(qi.shape[0],),
            in_specs=[pl.BlockSpec((t, HEADS * 256), row), pl.BlockSpec((t, HEADS * 256), lambda p, qi_, kj_: (kj_[p], 0)),
                      pl.BlockSpec((HEADS * HEAD_DIM, t), lambda p, qi_, kj_: (0, kj_[p]))],
            out_specs=[pl.BlockSpec((t, HEADS * HEAD_DIM), row)] * 2,
            scratch_shapes=[pltpu.VMEM((HEADS, 1, t), F32), pltpu.VMEM((HEADS, 1, t), F32),
                            pltpu.VMEM((HEADS, HEAD_DIM, t), F32)]),
        out_shape=[jax.ShapeDtypeStruct((S, HEADS * HEAD_DIM), F32)] * 2,
        compiler_params=_params(("arbitrary",)),
    )(qi, kj, q, k, v_t)


def _attn_dq(q, k, v, o, lse, d_o, t, name):
    S = q.shape[0]
    n = S // t
    qi, kj = _tile_pairs(n, by_key=False)

    def body(qi_ref, kj_ref, q_ref, k_ref, v_ref, o_ref, lse_ref, do_ref, dq_ref, st_ref, acc_ref, delta_ref):
        i, j = qi_ref[pl.program_id(0)], kj_ref[pl.program_id(0)]

        @pl.when(j == 0)
        def _():
            acc_ref[...] = jnp.zeros_like(acc_ref)
            delta_ref[...] = jnp.sum(_heads(do_ref, HEAD_DIM) * _heads(o_ref, HEAD_DIM), axis=2, keepdims=True)

        def update(masked):
            kh = _heads(k_ref, 256)
            d_out = _heads(do_ref, HEAD_DIM)
            s = _bmm(_heads(q_ref, 256), kh, _BNT)
            p = jnp.exp(s - _heads(lse_ref, HEAD_DIM)[:, :, 0:1])
            if masked:
                p = jnp.where(_chunk_mask(i, j, t)[None], p, 0.0)
            dp = _bmm(d_out, _heads(v_ref, HEAD_DIM), _BNT)
            ds = p * (dp - delta_ref[...])
            acc_ref[...] += _bmm(ds, kh, _BNN)

        @pl.when(j < i)
        def _():
            update(False)

        @pl.when(j == i)
        def _():
            update(True)
            lane = lax.broadcasted_iota(jnp.int32, (t, HEAD_DIM), 1)
            stats = jnp.zeros((t, HEAD_DIM), F32)
            for h in range(HEADS):
                dq_ref[:, 256 * h:256 * (h + 1)] = acc_ref[h]
                stats = stats + jnp.where(lane == h, lse_ref[:, HEAD_DIM * h:HEAD_DIM * (h + 1)], 0.0)
                stats = stats + jnp.where(lane == HEADS + h, delta_ref[h], 0.0)
            st_ref[...] = jnp.transpose(stats)[0:8, :]

    kv = lambda p, qi_, kj_: (kj_[p], 0)
    row = lambda p, qi_, kj_: (qi_[p], 0)
    wide, narrow = pl.BlockSpec((t, HEADS * 256), row), pl.BlockSpec((t, HEADS * HEAD_DIM), row)
    return pl.pallas_call(
        body, name=name,
        grid_spec=pltpu.PrefetchScalarGridSpec(
            num_scalar_prefetch=2, grid=(qi.shape[0],),
            in_specs=[wide, pl.BlockSpec((t, HEADS * 256), kv), pl.BlockSpec((t, HEADS * HEAD_DIM), kv), narrow, narrow, narrow],
            out_specs=[wide, pl.BlockSpec((8, t), lambda p, qi_, kj_: (0, qi_[p]))],
            scratch_shapes=[pltpu.VMEM((HEADS, t, 256), F32), pltpu.VMEM((HEADS, t, 1), F32)]),
        out_shape=[jax.ShapeDtypeStruct((S, HEADS * 256), F32), jax.ShapeDtypeStruct((8, S), F32)],
        compiler_params=_params(("arbitrary",)),
    )(qi, kj, q, k, v, o, lse, d_o)


def _attn_dkv(q, k, v, d_o, stats, t, name):
    S = q.shape[0]
    n = S // t
    qi, kj = _tile_pairs(n, by_key=True)

    def body(qi_ref, kj_ref, q_ref, k_ref, v_ref, do_ref, st_ref, dk_ref, dv_ref, dk_acc, dv_acc):
        i, j = qi_ref[pl.program_id(0)], kj_ref[pl.program_id(0)]

        @pl.when(i == j)
        def _():
            dk_acc[...] = jnp.zeros_like(dk_acc)
            dv_acc[...] = jnp.zeros_like(dv_acc)

        def update(masked):
            qh = _heads(q_ref, 256)
            d_out = _heads(do_ref, HEAD_DIM)
            st = st_ref[...]
            lse_row = jnp.stack([st[h:h + 1, :] for h in range(HEADS)])
            delta_row = jnp.stack([st[HEADS + h:HEADS + h + 1, :] for h in range(HEADS)])
            s_t = _bmm(_heads(k_ref, 256), qh, _BNT)
            p_t = jnp.exp(s_t - lse_row)
            if masked:
                p_t = jnp.where(_chunk_mask(i, j, t, transposed=True)[None], p_t, 0.0)
            dv_acc[...] += _bmm(p_t, d_out, _BNN)
            dp_t = _bmm(_heads(v_ref, HEAD_DIM), d_out, _BNT)
            ds_t = p_t * (dp_t - delta_row)
            dk_acc[...] += _bmm(ds_t, qh, _BNN)

        @pl.when(i == j)
        def _():
            update(True)

        @pl.when(i > j)
        def _():
            update(False)

        @pl.when(i == n - 1)
        def _():
            for h in range(HEADS):
                dk_ref[:, 256 * h:256 * (h + 1)] = dk_acc[h]
                dv_ref[:, HEAD_DIM * h:HEAD_DIM * (h + 1)] = dv_acc[h]

    q_row = lambda p, qi_, kj_: (qi_[p], 0)
    k_row = lambda p, qi_, kj_: (kj_[p], 0)
    return pl.pallas_call(
        body, name=name,
        grid_spec=pltpu.PrefetchScalarGridSpec(
            num_scalar_prefetch=2, grid=(qi.shape[0],),
            in_specs=[pl.BlockSpec((t, HEADS * 256), q_row), pl.BlockSpec((t, HEADS * 256), k_row),
                      pl.BlockSpec((t, HEADS * HEAD_DIM), k_row), pl.BlockSpec((t, HEADS * HEAD_DIM), q_row),
                      pl.BlockSpec((8, t), lambda p, qi_, kj_: (0, qi_[p]))],
            out_specs=[pl.BlockSpec((t, HEADS * 256), k_row), pl.BlockSpec((t, HEADS * HEAD_DIM), k_row)],
            scratch_shapes=[pltpu.VMEM((HEADS, t, 256), F32), pltpu.VMEM((HEADS, t, HEAD_DIM), F32)]),
        out_shape=[jax.ShapeDtypeStruct((S, HEADS * 256), F32), jax.ShapeDtypeStruct((S, HEADS * HEAD_DIM), F32)],
        compiler_params=_params(("arbitrary",)),
    )(qi, kj, q, k, v, d_o, stats)


FFN_PIECE = 2 * D_FF // N_DEV
HID_PIECES = D_FF // FFN_PIECE


def _ffn_up(h, w8, name):
    S = h.shape[0]
    tm = _pick(S, MATMUL_ROWS)

    def body(h_ref, wg_ref, wu_ref, g_ref, u_ref, hid_ref):
        gate = _dot_raw(h_ref[...], wg_ref[...], "nn")
        up = _dot_raw(h_ref[...], wu_ref[...], "nn")
        g_ref[...] = gate.astype(BF16)
        u_ref[...] = up.astype(BF16)
        hid_ref[...] = (_silu(gate) * up).astype(BF16)

    o_spec = pl.BlockSpec((None, tm, FFN_PIECE), lambda i, j: (j, i, 0))
    return pl.pallas_call(
        body, name=name, grid=(S // tm, HID_PIECES),
        in_specs=[pl.BlockSpec((tm, D_MODEL), lambda i, j: (i, 0)),
                  pl.BlockSpec((None, D_MODEL, FFN_PIECE), lambda i, j: (j, 0, 0)),
                  pl.BlockSpec((None, D_MODEL, FFN_PIECE), lambda i, j: (j + HID_PIECES, 0, 0))],
        out_specs=[o_spec] * 3,
        out_shape=[jax.ShapeDtypeStruct((HID_PIECES, S, FFN_PIECE), BF16)] * 3,
        compiler_params=_params(("parallel", "parallel")),
    )(h, w8, w8)


def _after_specs(after):
    return [] if after is None else [pl.BlockSpec(memory_space=pl.ANY)]


def _after_args(after):
    return [] if after is None else [after]


def _ffn_gw8(h, d_gate, d_up, name, after=None):
    S = h.shape[0]
    tm = 512
    tk = _pick(S, (512, 256, 128))
    nk = S // tk

    def body(h_ref, dg_ref, du_ref, *rest):
        o_ref, acc_ref = rest[-2:]
        k = pl.program_id(1)

        @pl.when(k == 0)
        def _():
            acc_ref[...] = jnp.zeros_like(acc_ref)

        h_t = jnp.transpose(h_ref[...])
        for p in range(HID_PIECES):
            acc_ref[p] += _dot_raw(h_t, dg_ref[p], "nn")
            acc_ref[HID_PIECES + p] += _dot_raw(h_t, du_ref[p], "nn")

        @pl.when(k == nk - 1)
        def _():
            o_ref[...] = acc_ref[...].astype(o_ref.dtype)

    d_spec = pl.BlockSpec((HID_PIECES, tk, FFN_PIECE), lambda i, k: (0, k, 0))
    return pl.pallas_call(
        body, name=name, grid=(D_MODEL // tm, nk),
        in_specs=[pl.BlockSpec((tk, tm), lambda i, k: (k, i)), d_spec, d_spec] + _after_specs(after),
        out_specs=pl.BlockSpec((2 * HID_PIECES, tm, FFN_PIECE), lambda i, k: (0, i, 0)),
        out_shape=jax.ShapeDtypeStruct((2 * HID_PIECES, D_MODEL, FFN_PIECE), BF16),
        scratch_shapes=[pltpu.VMEM((2 * HID_PIECES, tm, FFN_PIECE), F32)],
        compiler_params=_params(("parallel", "arbitrary")),
    )(h, d_gate, d_up, *_after_args(after))


def _ffn_dh(d_gate, d_up, w8, x, d_out, scale, shift, name, after=None):
    S = d_gate.shape[1]
    tm = _pick(S, (512, 256, 128))

    def body(dg_ref, du_ref, wg_ref, wu_ref, x_ref, do_ref, sc_ref, sh_ref, *rest):
        dx_ref, dsc_ref, dsh_ref, acc_ref = rest[-4:]
        i, k = pl.program_id(0), pl.program_id(1)

        @pl.when(k == 0)
        def _():
            acc_ref[...] = jnp.zeros_like(acc_ref)

        acc_ref[...] += _dot_raw(dg_ref[...], wg_ref[...], "nt") + _dot_raw(du_ref[...], wu_ref[...], "nt")

        @pl.when((k == 0) & (i == 0))
        def _():
            dsc_ref[...] = jnp.zeros_like(dsc_ref)
            dsh_ref[...] = jnp.zeros_like(dsh_ref)

        @pl.when(k == HID_PIECES - 1)
        def _():
            _, vjp = jax.vjp(_modulate, x_ref[...], sc_ref[...], sh_ref[...])
            dx, dsc, dsh = vjp(acc_ref[...])
            dx_ref[...] = dx + do_ref[...]
            dsc_ref[...] += dsc
            dsh_ref[...] += dsh

    d_spec = pl.BlockSpec((None, tm, FFN_PIECE), lambda i, k: (k, i, 0))
    row = pl.BlockSpec((tm, D_MODEL), lambda i, k: (i, 0))
    par = pl.BlockSpec((1, D_MODEL), lambda i, k: (0, 0))
    return pl.pallas_call(
        body, name=name, grid=(S // tm, HID_PIECES),
        in_specs=[d_spec, d_spec,
                  pl.BlockSpec((None, D_MODEL, FFN_PIECE), lambda i, k: (k, 0, 0)),
                  pl.BlockSpec((None, D_MODEL, FFN_PIECE), lambda i, k: (k + HID_PIECES, 0, 0)),
                  row, row, par, par] + _after_specs(after),
        out_specs=[row, par, par],
        out_shape=[jax.ShapeDtypeStruct((S, D_MODEL), F32), jax.ShapeDtypeStruct((1, D_MODEL), F32),
                   jax.ShapeDtypeStruct((1, D_MODEL), F32)],
        scratch_shapes=[pltpu.VMEM((tm, D_MODEL), F32)],
        compiler_params=_params(("arbitrary", "arbitrary")),
    )(d_gate, d_up, w8, w8, x, d_out, scale, shift, *_after_args(after))


def _swiglu_bwd(d_hid, gate, up):
    sg = _sigmoid(gate)
    return d_hid * up * (sg * (1.0 + gate * (1.0 - sg))), d_hid * (gate * sg)


def _loss_and_grad(x3, target, name):
    S = x3.shape[0]
    tm = _pick(S, (512, 256, 128))
    n = S // tm

    def body(x_ref, t_ref, dx_ref, l_ref):
        i = pl.program_id(0)
        diff = x_ref[...] - t_ref[...]
        dx_ref[...] = diff * (1.0 / D_MODEL)

        @pl.when(i == 0)
        def _():
            l_ref[...] = jnp.zeros_like(l_ref)

        l_ref[...] += jnp.sum(diff * diff, axis=0, keepdims=True)

        @pl.when(i == n - 1)
        def _():
            l_ref[...] = jnp.full(l_ref.shape, (0.5 / D_MODEL) * jnp.sum(l_ref[...]), F32)

    return pl.pallas_call(
        body, name=name, grid=(n,),
        in_specs=[pl.BlockSpec((tm, D_MODEL), lambda i: (i, 0))] * 2,
        out_specs=[pl.BlockSpec((tm, D_MODEL), lambda i: (i, 0)), pl.BlockSpec((1, D_MODEL), lambda i: (0, 0))],
        out_shape=[jax.ShapeDtypeStruct((S, D_MODEL), F32), jax.ShapeDtypeStruct((1, D_MODEL), F32)],
        compiler_params=_params(("arbitrary",)),
    )(x3, target)


def _adamw(w, g, m, v, name):
    R, C = w.shape
    tr = _pick(R, (256, 176, 128, 64, 32, 16, 8))

    def body(w_ref, g_ref, m_ref, v_ref, d_ref, nm_ref, nv_ref):
        g_ = g_ref[...]
        m_ = ADAM_B1 * m_ref[...] + (1.0 - ADAM_B1) * g_
        v_ = ADAM_B2 * v_ref[...] + (1.0 - ADAM_B2) * (g_ * g_)
        m_hat = m_ / (1.0 - ADAM_B1 ** ADAM_STEP)
        v_hat = v_ / (1.0 - ADAM_B2 ** ADAM_STEP)
        d_ref[...] = -ADAM_LR * (m_hat / (jnp.sqrt(v_hat) + ADAM_EPS) + ADAM_WD * w_ref[...])
        nm_ref[...] = m_
        nv_ref[...] = v_

    spec = pl.BlockSpec((tr, C), lambda i: (i, 0))
    return pl.pallas_call(
        body, name=name, grid=(R // tr,),
        in_specs=[spec] * 4, out_specs=[spec] * 3,
        out_shape=[jax.ShapeDtypeStruct((R, C), F32)] * 3,
        compiler_params=_params(("parallel",)),
    )(w, g, m, v)


def _sum_devices(parts, name):
    _, R, C = parts.shape
    tr = _pick(R, (512, 256, 176, 128, 64, 32, 16, 8))

    def body(p_ref, o_ref):
        acc = p_ref[0].astype(F32)
        for d in range(1, N_DEV):
            acc = acc + p_ref[d].astype(F32)
        o_ref[...] = acc

    return pl.pallas_call(
        body, name=name, grid=(R // tr,),
        in_specs=[pl.BlockSpec((N_DEV, tr, C), lambda i: (0, i, 0))],
        out_specs=pl.BlockSpec((tr, C), lambda i: (i, 0)),
        out_shape=jax.ShapeDtypeStruct((R, C), F32),
        compiler_params=_params(("parallel",)),
    )(parts)


def _my_place():
    return lax.axis_index("x"), lax.axis_index("y"), lax.axis_index("c")


def _all_gather(blocks, name):
    n = len(blocks)

    def body(*refs):
        x_refs, out_refs = refs[:n], refs[n:2 * n]
        send_sems, recv_sems, local_sems = refs[2 * n:]
        x, y, c = _my_place()
        me, sibling = (x, y, c), (x, y, 1 - c)
        chips = [(1 - x, y), (x, 1 - y), (1 - x, 1 - y)]

        def copy(a, k, blk, to, own=False):
            slot = out_refs[a].at[4 * blk[0] + 2 * blk[1] + blk[2]]
            return pltpu.make_async_remote_copy(
                src_ref=x_refs[a] if own else slot, dst_ref=slot,
                send_sem=send_sems.at[7 * a + k], recv_sem=recv_sems.at[7 * a + k], device_id=to, device_id_type=MESH)

        mine = [pltpu.make_async_copy(x_refs[a], out_refs[a].at[4 * x + 2 * y + c], local_sems.at[a]) for a in range(n)]
        for cp in mine:
            cp.start()
        first = []
        for j, chip in enumerate(chips):
            first += [copy(a, 1 + j, me, (*chip, c), own=True) for a in range(n)]
        first += [copy(a, 0, me, sibling, own=True) for a in range(n)]
        for cp in first:
            cp.start()
        passed = []
        for j, chip in enumerate(chips):
            for a in range(n):
                copy(a, 1 + j, (*chip, c), me).wait_recv()
                passed.append(copy(a, 4 + j, (*chip, c), sibling))
                passed[-1].start()
        for a in range(n):
            copy(a, 0, sibling, me).wait_recv()
        for j, chip in enumerate(chips):
            for a in range(n):
                copy(a, 4 + j, (*chip, 1 - c), me).wait_recv()
        for cp in first + passed:
            cp.wait_send()
        for cp in mine:
            cp.wait()

    return pl.pallas_call(
        body, name=name,
        out_shape=[jax.ShapeDtypeStruct((N_DEV,) + b.shape, b.dtype) for b in blocks],
        in_specs=[pl.BlockSpec(memory_space=pl.ANY)] * n,
        out_specs=[pl.BlockSpec(memory_space=pl.ANY)] * n,
        scratch_shapes=[pltpu.SemaphoreType.DMA((7 * n,)), pltpu.SemaphoreType.DMA((7 * n,)), pltpu.SemaphoreType.DMA((n,))],
    )(*blocks)


def _all_to_all(pieces, name):
    n = len(pieces)

    def body(*refs):
        x_refs, out_refs = refs[:n], refs[n:2 * n]
        send_sems, recv_sems, local_sems = refs[2 * n:]
        x, y, c = _my_place()
        me = 4 * x + 2 * y + c
        mine = [pltpu.make_async_copy(x_refs[a].at[me], out_refs[a].at[me], local_sems.at[a]) for a in range(n)]
        for cp in mine:
            cp.start()
        copies = []
        for k in (2, 4, 6, 3, 5, 7, 1):
            px = 1 - x if k & 4 else x
            py = 1 - y if k & 2 else y
            pc = 1 - c if k & 1 else c
            peer = 4 * px + 2 * py + pc
            for a in range(n):
                copies.append(pltpu.make_async_remote_copy(
                    src_ref=x_refs[a].at[peer], dst_ref=out_refs[a].at[me],
                    send_sem=send_sems.at[7 * a + k - 1], recv_sem=recv_sems.at[7 * a + k - 1],
                    device_id=(px, py, pc), device_id_type=MESH))
        for cp in copies:
            cp.start()
        for cp in copies:
            cp.wait_recv()
        for cp in copies:
            cp.wait_send()
        for cp in mine:
            cp.wait()

    return pl.pallas_call(
        body, name=name,
        out_shape=[jax.ShapeDtypeStruct(p.shape, p.dtype) for p in pieces],
        in_specs=[pl.BlockSpec(memory_space=pl.ANY)] * n,
        out_specs=[pl.BlockSpec(memory_space=pl.ANY)] * n,
        scratch_shapes=[pltpu.SemaphoreType.DMA((7 * n,)), pltpu.SemaphoreType.DMA((7 * n,)), pltpu.SemaphoreType.DMA((n,))],
    )(*pieces)


def _peers():
    x, y, c = _my_place()
    out = []
    for k in (2, 4, 6, 3, 5, 7, 1):
        px = 1 - x if k & 4 else x
        py = 1 - y if k & 2 else y
        pc = 1 - c if k & 1 else c
        out.append((k, (px, py, pc), 4 * px + 2 * py + pc))
    return out


def _exchange_copies(x_refs, land_refs, send_sems, recv_sems, scatter):
    x, y, c = _my_place()
    me = 4 * x + 2 * y + c
    starts, arrivals = [], []
    for k, place, peer in _peers():
        for a, (x_ref, land_ref) in enumerate(zip(x_refs, land_refs)):
            sems = dict(send_sem=send_sems.at[7 * a + k - 1], recv_sem=recv_sems.at[7 * a + k - 1],
                        device_id=place, device_id_type=MESH)
            src = x_ref.at[peer] if scatter else x_ref
            starts.append(pltpu.make_async_remote_copy(src_ref=src, dst_ref=land_ref.at[me], **sems))
            arrivals.append(pltpu.make_async_remote_copy(src_ref=src, dst_ref=land_ref.at[peer], **sems))
    return starts, arrivals


def _exchange_start(arrays, scatter, name):
    n = len(arrays)
    hbm = pl.BlockSpec(memory_space=pltpu.HBM)
    sem = pl.BlockSpec(memory_space=pltpu.SEMAPHORE)
    lands = [lax.empty(a.shape if scatter else (N_DEV,) + a.shape, a.dtype) for a in arrays]

    def body(*refs):
        x_refs, land_refs = refs[:n], refs[n:2 * n]
        send_sems, recv_sems = refs[2 * n], refs[2 * n + 1]
        token = refs[-1]
        starts, _ = _exchange_copies(x_refs, land_refs, send_sems, recv_sems, scatter)
        for cp in starts:
            cp.start()
        token[...] = jnp.zeros_like(token)

    res = pl.pallas_call(
        body, name=name,
        out_shape=(pltpu.SemaphoreType.DMA((7 * n,)), pltpu.SemaphoreType.DMA((7 * n,)),
                   *[pltpu.HBM(a.shape, a.dtype) for a in arrays], *[pltpu.HBM(l.shape, l.dtype) for l in lands],
                   jax.ShapeDtypeStruct((8, 128), F32)),
        in_specs=[hbm] * (2 * n),
        out_specs=(sem, sem, *[hbm] * (2 * n), pl.BlockSpec(memory_space=pltpu.VMEM)),
        input_output_aliases={i: 2 + i for i in range(2 * n)},
        compiler_params=pltpu.CompilerParams(has_side_effects=pltpu.SideEffectType.DATAFLOW_SIDE_EFFECTING),
    )(*[pltpu.with_memory_space_constraint(a, pltpu.HBM) for a in arrays],
      *[pltpu.with_memory_space_constraint(l, pltpu.HBM) for l in lands])
    return res[0], res[1], list(res[2:2 + n]), list(res[2 + n:2 + 2 * n]), res[-1]


def _exchange_wait(handles, scatter, after, name):
    send_sems, recv_sems, arrays, lands, _ = handles
    n = len(arrays)
    hbm = pl.BlockSpec(memory_space=pltpu.HBM)
    sem = pl.BlockSpec(memory_space=pltpu.SEMAPHORE)

    def body(*refs):
        x_refs, land_refs = refs[:n], refs[n:2 * n]
        send_s, recv_s = refs[2 * n], refs[2 * n + 1]
        starts, arrivals = _exchange_copies(x_refs, land_refs, send_s, recv_s, scatter)
        for cp in arrivals:
            cp.wait_recv()
        for cp in starts:
            cp.wait_send()

    res = pl.pallas_call(
        body, name=name,
        out_shape=(*[pltpu.HBM(a.shape, a.dtype) for a in arrays], *[pltpu.HBM(l.shape, l.dtype) for l in lands]),
        in_specs=[hbm] * (2 * n) + [sem, sem, pl.BlockSpec(memory_space=pl.ANY)],
        out_specs=tuple([hbm] * (2 * n)),
        input_output_aliases={i: i for i in range(2 * n)},
        compiler_params=pltpu.CompilerParams(has_side_effects=pltpu.SideEffectType.DATAFLOW_SIDE_EFFECTING),
    )(*arrays, *lands, send_sems, recv_sems, after)
    me = 4 * lax.axis_index("x") + 2 * lax.axis_index("y") + lax.axis_index("c")
    out = []
    for src, got in zip(res[:n], res[n:]):
        zeros = (0,) * (got.ndim - 1)
        own = lax.dynamic_slice(src, (me,) + zeros, (1,) + src.shape[1:]) if scatter else src[None]
        out.append(lax.dynamic_update_slice(got, own, (me,) + zeros))
    return out


def _pad_lanes(v, at=0, width=128):
    return jnp.pad(v, ((0, 0), (at, width - at - v.shape[1])))


def _pack_weights(P):
    W = {}
    w = P["w_in"]
    W["wp"] = jnp.concatenate([w[:, :2048], w[:, 2440:2696], w[:, 2056:2440], w[:, 2696:2760], w[:, 2048:2056],
                               jnp.zeros((D_MODEL, N_IN_PACKED - N_IN), w.dtype)], axis=1).astype(BF16)
    W["conv_w"] = P["gdn_conv_w"].astype(F32)
    W["alog_p"] = _pad_lanes(P["gdn_a_log"], 64)
    W["dt_p"] = _pad_lanes(P["gdn_dt_bias"], 64)
    W["gnw"] = P["gdn_norm_w"]
    W["qnw"] = P["mla_q_norm_w"]
    W["kvnw"] = P["mla_kv_norm_w"]
    uq = P["mla_w_uq"].reshape(Q_LORA, HEADS, HEAD_DIM + ROPE)
    W["wuq"] = jnp.pad(uq, ((0, 0), (0, 0), (0, 256 - HEAD_DIM - ROPE))).reshape(Q_LORA, HEADS * 256).astype(BF16)
    ukv = P["mla_w_ukv"].reshape(KV_LORA, HEADS, 2, HEAD_DIM)
    W["wukv"] = ukv.transpose(0, 2, 1, 3).reshape(KV_LORA, 2 * HEADS * HEAD_DIM).astype(BF16)
    W["qn_w"] = P["qkn_q_nope"]
    W["qr_w"] = _pad_lanes(P["qkn_q_rope"])
    W["kn_w"] = P["qkn_k_nope"]
    W["kr_w"] = _pad_lanes(P["qkn_k_rope"])
    W["onw"] = P["mla_out_norm_w"]
    W["wout"] = P["w_out"].astype(BF16)
    return W


def _unpack_grads(G):
    g = G["wp"]
    uq = G["wuq"].reshape(Q_LORA, HEADS, 256)[:, :, :HEAD_DIM + ROPE].reshape(Q_LORA, HEADS * (HEAD_DIM + ROPE))
    ukv = G["wukv"].reshape(KV_LORA, 2, HEADS, HEAD_DIM).transpose(0, 2, 1, 3).reshape(KV_LORA, 2 * HEADS * HEAD_DIM)
    return {
        "w_in": jnp.concatenate([g[:, :2048], g[:, 2752:2760], g[:, 2304:2688], g[:, 2048:2304], g[:, 2688:2752]], axis=1),
        "gdn_conv_w": G["conv_w"], "gdn_a_log": G["alog_p"][:, 64:68], "gdn_dt_bias": G["dt_p"][:, 64:68],
        "gdn_norm_w": G["gnw"], "mla_q_norm_w": G["qnw"], "mla_w_uq": uq, "mla_kv_norm_w": G["kvnw"], "mla_w_ukv": ukv,
        "qkn_q_nope": G["qn_w"], "qkn_q_rope": G["qr_w"][:, :ROPE], "qkn_k_nope": G["kn_w"], "qkn_k_rope": G["kr_w"][:, :ROPE],
        "mla_out_norm_w": G["onw"], "w_out": G["wout"],
    }


def _rope_tables(positions):
    half = ROPE // 2
    inv_freq = ROPE_BASE ** (-jnp.arange(half, dtype=F32) / half)
    ang = positions.astype(F32)[:, None] * inv_freq
    cos, sin = jnp.cos(ang), jnp.sin(ang)
    zeros = jnp.zeros((positions.shape[0], 128 - ROPE), F32)
    return jnp.concatenate([cos, cos, zeros], axis=1), jnp.concatenate([-sin, sin, zeros], axis=1)


def _mod_fn(x, scale, shift):
    return (_modulate(x, scale, shift),)


def _ffn_fwd(x, scale, shift, gate_w, w8, wo4, tag):
    S = x.shape[0]
    tm = _pick(S, (512, 256, 128))
    (h,) = _rowwise(_mod_fn, [(x, tm, D_MODEL, 0)], [scale, shift], [(tm, D_MODEL, BF16)], S // tm, tag + "_mod")
    gate, up, hid = _ffn_up(h, w8, tag + "_up")
    tb = _pick(S, MATMUL_ROWS)
    mn = pl.BlockSpec((tb, D_MODEL), lambda i, j, k: (i, j))
    f, x_out = _mmg(hid, wo4, "nn", name=tag + "_down", grid=(S // tb, 1, HID_PIECES),
                    a_spec=pl.BlockSpec((None, tb, FFN_PIECE), lambda i, j, k: (k, i, 0)),
                    b_spec=pl.BlockSpec((None, FFN_PIECE, D_MODEL), lambda i, j, k: (k, 0, j)),
                    out_spec=mn, out_shapes=[jax.ShapeDtypeStruct((S, D_MODEL), F32)] * 2, acc_shape=(tb, D_MODEL),
                    extras=[x, gate_w], extra_specs=[mn, pl.BlockSpec((1, D_MODEL), lambda i, j, k: (0, j))],
                    epi=lambda acc, x_, g_: (acc, x_ + 0.5 * g_ * acc))
    return x_out, (h, gate, up, hid, f)


def _ffn_bwd(d_out, x, scale, shift, gate_w, w8, wo4, saved, tag, grad_ready):
    h, gate, up, hid, f = saved
    S = x.shape[0]
    tm = _pick(S, (512, 256, 128))
    tk = _pick(S, (512, 256, 128))
    n = S // tm
    (df,), (d_gate_w,) = _rowwise_bwd(lambda f_, g_: (0.5 * g_ * f_,), [(f, tm, D_MODEL, 0)], [], [gate_w],
                                      [(d_out, tm, D_MODEL, 0)], n, tag + "_dres", row_dtypes=(BF16,))
    tb = _pick(S, MATMUL_ROWS)
    piece = pl.BlockSpec((None, tb, FFN_PIECE), lambda i, j, k: (j, i, 0))
    d_gate, d_up = _mmg(df, wo4, "nt", name=tag + "_ddown", grid=(S // tb, HID_PIECES, 1),
                        a_spec=pl.BlockSpec((tb, D_MODEL), lambda i, j, k: (i, 0)),
                        b_spec=pl.BlockSpec((None, FFN_PIECE, D_MODEL), lambda i, j, k: (j, 0, 0)),
                        out_spec=piece, out_shapes=[jax.ShapeDtypeStruct((HID_PIECES, S, FFN_PIECE), BF16)] * 2,
                        acc_shape=(tb, FFN_PIECE), extras=[gate, up], extra_specs=[piece, piece], epi=_swiglu_bwd)
    g_wo4 = _mmg(hid, df, "tn", name=tag + "_gwo", grid=(HID_PIECES, 1, S // tk),
                 a_spec=pl.BlockSpec((None, tk, FFN_PIECE), lambda i, j, k: (i, k, 0)),
                 b_spec=pl.BlockSpec((tk, D_MODEL), lambda i, j, k: (k, j)),
                 out_spec=pl.BlockSpec((None, FFN_PIECE, D_MODEL), lambda i, j, k: (i, 0, j)),
                 out_shapes=[jax.ShapeDtypeStruct((HID_PIECES, FFN_PIECE, D_MODEL), BF16)], acc_shape=(FFN_PIECE, D_MODEL))
    g_w8 = _ffn_gw8(h, d_gate, d_up, tag + "_gw8", after=grad_ready("wo4", g_wo4))
    dx, d_scale, d_shift = _ffn_dh(d_gate, d_up, w8, x, d_out, scale, shift, tag + "_dh", after=grad_ready("w8", g_w8))
    return dx, d_scale, d_shift, d_gate_w


def _dproj_dmod(d_proj, wp, x, d_out, scale, shift, name):
    S, K = d_proj.shape
    tm = _pick(S, (512, 256, 128))
    tk = _pick(K, (1408, 512, 256, 128))
    nk = K // tk

    def body(dp_ref, w_ref, x_ref, do_ref, sc_ref, sh_ref, dx_ref, dsc_ref, dsh_ref, acc_ref):
        i, k = pl.program_id(0), pl.program_id(1)

        @pl.when(k == 0)
        def _():
            acc_ref[...] = jnp.zeros_like(acc_ref)

        acc_ref[...] += _dot_raw(dp_ref[...], w_ref[...], "nt")

        @pl.when((k == 0) & (i == 0))
        def _():
            dsc_ref[...] = jnp.zeros_like(dsc_ref)
            dsh_ref[...] = jnp.zeros_like(dsh_ref)

        @pl.when(k == nk - 1)
        def _():
            _, vjp = jax.vjp(_modulate, x_ref[...], sc_ref[...], sh_ref[...])
            dx, dsc, dsh = vjp(acc_ref[...])
            dx_ref[...] = dx + do_ref[...]
            dsc_ref[...] += dsc
            dsh_ref[...] += dsh

    row = pl.BlockSpec((tm, D_MODEL), lambda i, k: (i, 0))
    par = pl.BlockSpec((1, D_MODEL), lambda i, k: (0, 0))
    return pl.pallas_call(
        body, name=name, grid=(S // tm, nk),
        in_specs=[pl.BlockSpec((tm, tk), lambda i, k: (i, k)), pl.BlockSpec((D_MODEL, tk), lambda i, k: (0, k)),
                  row, row, par, par],
        out_specs=[row, par, par],
        out_shape=[jax.ShapeDtypeStruct((S, D_MODEL), F32), jax.ShapeDtypeStruct((1, D_MODEL), F32),
                   jax.ShapeDtypeStruct((1, D_MODEL), F32)],
        scratch_shapes=[pltpu.VMEM((tm, D_MODEL), F32)],
        compiler_params=_params(("arbitrary", "arbitrary")),
    )(d_proj, wp, x, d_out, scale, shift)


def _mixer_fwd(x1, scale, shift, gate_w, cos_p, sin_p, W):
    S = x1.shape[0]
    tm = _pick(S, (512, 256, 128))
    tv = _pick(S, (256, 128))
    ta = _pick(S, (512, 256, 128))
    nc = S // CHUNK
    (h2,) = _rowwise(_mod_fn, [(x1, tm, D_MODEL, 0)], [scale, shift], [(tm, D_MODEL, BF16)], S // tm, "mix_mod")
    proj = _mm(h2, W["wp"], "nn", name="mix_proj")
    qkvc = _conv_fwd(proj, W["conv_w"], tv, "gdn_conv")
    kab = (proj, tv, 128, 21)
    q_a, k_a, v_a, gb = _rowwise(_gdn_pre_fn, [(qkvc, tv, 1536, 0), kab], [W["alog_p"], W["dt_p"]],
                                 [(tv, 512, F32)] * 3 + [(tv, 128, F32)], S // tv, "gdn_pre")
    ti = _pick(S, INTRA_ROWS)
    intra = _rowwise(_gdn_intra_fn, [(q_a, ti, 512, 0), (k_a, ti, 512, 0), (v_a, ti, 512, 0), (gb, ti, 128, 0)],
                     [], [(ti, 512, F32)] * 4 + [(ti, CHUNK, F32)] * 4 + [(ti // 8, 512, F32)], S // ti, "gdn_intra")
    u, wk, qd, kd, qks, gl = intra[0], intra[1], intra[2], intra[3], tuple(intra[4:8]), intra[8]
    o_a, s_prev = _gdn_scan_fwd(u, wk, qd, kd, qks, gl, "gdn_scan")
    mla_params = [W["qnw"], W["kvnw"], W["wuq"], W["wukv"], W["qn_w"], W["qr_w"], W["kn_w"], W["kr_w"]]
    def mla_pre_with_vt(*a):
        q_, k_, v_ = _mla_pre_fn(*a)
        return q_, k_, v_, jnp.transpose(v_)

    q_b, k_b, v_b, vt_b = _rowwise(mla_pre_with_vt,
                                   [(proj, tv, 256, 8), (proj, tv, 384, 6), kab, (cos_p, tv, 128, 0), (sin_p, tv, 128, 0)],
                                   mla_params, [(tv, 1024, BF16), (tv, 1024, BF16), (tv, 512, BF16), (512, tv, BF16, "across")],
                                   S // tv, "mla_pre")
    o_b, lse = _attn_fwd(q_b, k_b, vt_b, ta, "mla_attn")
    (mixed,) = _rowwise(_mix_post_fn, [(o_a, tv, 512, 0), (proj, tv, 512, 3), (o_b, tv, 512, 0)], [W["gnw"], W["onw"]],
                        [(tv, D_MODEL, BF16)], S // tv, "mix_post")
    y, x2 = _mm(mixed, W["wout"], "nn", name="mix_out", out_dtypes=(F32, F32), extras=[x1], extra_params=[gate_w],
                epi=lambda acc, x_, g_: (acc, x_ + g_ * acc))
    saved = (h2, proj, qkvc, q_a, k_a, v_a, gb, u, wk, qd, kd, qks, gl, s_prev, o_a, q_b, k_b, v_b, o_b, lse, mixed, y)
    return x2, saved


def _mixer_bwd(d_out, x1, scale, shift, gate_w, cos_p, sin_p, W, saved):
    (h2, proj, qkvc, q_a, k_a, v_a, gb, u, wk, qd, kd, qks, gl, s_prev, o_a, q_b, k_b, v_b, o_b, lse, mixed, y) = saved
    S = x1.shape[0]
    tm = _pick(S, (512, 256, 128))
    tv = _pick(S, (256, 128))
    ta = _pick(S, (512, 256, 128))
    nc = S // CHUNK
    G = {}
    (dy,), (G["g2"],) = _rowwise_bwd(lambda y_, g_: (g_ * y_,), [(y, tm, D_MODEL, 0)], [], [gate_w],
                                     [(d_out, tm, D_MODEL, 0)], S // tm, "mix_dres", row_dtypes=(BF16,))
    d_mixed = _mm(dy, W["wout"], "nt", name="mix_dout")
    G["wout"] = _mm(mixed, dy, "tn", name="mix_gwout")
    (do_a, dz, do_b), (G["gnw"], G["onw"]) = _rowwise_bwd(
        _mix_post_fn, [(o_a, tv, 512, 0), (proj, tv, 512, 3), (o_b, tv, 512, 0)], [], [W["gnw"], W["onw"]],
        [(d_mixed, tv, D_MODEL, 0)], S // tv, "mix_dpost")
    dq_b, stats = _attn_dq(q_b, k_b, v_b, o_b, lse, do_b, ta, "mla_dq")
    dk_b, dv_b = _attn_dkv(q_b, k_b, v_b, do_b, stats, ta, "mla_dkv")
    kab = (proj, tv, 128, 21)
    mla_params = [W["qnw"], W["kvnw"], W["wuq"], W["wukv"], W["qn_w"], W["qr_w"], W["kn_w"], W["kr_w"]]
    (d_ckv, d_cq, d_kab), mla_grads = _rowwise_bwd(
        _mla_pre_fn, [(proj, tv, 256, 8), (proj, tv, 384, 6), kab], [(cos_p, tv, 128, 0), (sin_p, tv, 128, 0)], mla_params,
        [(dq_b, tv, 1024, 0), (dk_b, tv, 1024, 0), (dv_b, tv, 512, 0)], S // tv, "mla_dpre")
    for key, g in zip(("qnw", "kvnw", "wuq", "wukv", "qn_w", "qr_w", "kn_w", "kr_w"), mla_grads):
        G[key] = g
    scan_grads = _gdn_scan_bwd(u, wk, qd, kd, qks, gl, s_prev, do_a, "gdn_dscan")
    ti = _pick(S, INTRA_ROWS)
    intra_douts = [(scan_grads[i], ti, 512, 0) for i in range(4)] + [(scan_grads[4 + i], ti, CHUNK, 0) for i in range(4)]
    intra_douts.append((scan_grads[8], ti // 8, 512, 0))
    (dq_a, dk_a, dv_a, d_gb), _ = _rowwise_bwd(
        _gdn_intra_fn, [(q_a, ti, 512, 0), (k_a, ti, 512, 0), (v_a, ti, 512, 0), (gb, ti, 128, 0)], [], [],
        intra_douts, S // ti, "gdn_dintra")
    (d_qkvc, d_kab), (G["alog_p"], G["dt_p"]) = _rowwise_bwd(
        _gdn_pre_fn, [(qkvc, tv, 1536, 0), kab], [], [W["alog_p"], W["dt_p"]],
        [(dq_a, tv, 512, 0), (dk_a, tv, 512, 0), (dv_a, tv, 512, 0), (d_gb, tv, 128, 0)], S // tv, "gdn_dpre",
        adds=[(1, d_kab)])
    d_qkv, g_conv = _conv_bwd(proj, d_qkvc, W["conv_w"], tv, "gdn_dconv")
    G["conv_w"] = g_conv[:4]
    d_proj = jnp.concatenate([d_qkv, dz, d_ckv, d_cq, d_kab], axis=1).astype(BF16)
    G["wp"] = _mm(h2, d_proj, "tn", name="mix_gwp")
    dx1, G["s2"], G["sh2"] = _dproj_dmod(d_proj, W["wp"], x1, d_out, scale, shift, "mix_dproj")
    return dx1, G


def _local_step(x, target, mod, cos_p, sin_p, W1, mixer_weights, ffn2_weights, ffn_grad_ready, mixer_grads_ready):
    sh1, s1, g1, sh2, s2, g2, sh3, s3, g3 = [mod[:, D_MODEL * i:D_MODEL * (i + 1)] for i in range(N_MOD)]
    x1, saved1 = _ffn_fwd(x, s1, sh1, g1, W1["f1_w8"], W1["f1_wo4"], "ffn1")
    W = mixer_weights(x1)
    x2, saved2 = _mixer_fwd(x1, s2, sh2, g2, cos_p, sin_p, W)
    W.update(ffn2_weights(x2))
    x3, saved3 = _ffn_fwd(x2, s3, sh3, g3, W["f2_w8"], W["f2_wo4"], "ffn2")
    dx3, loss_row = _loss_and_grad(x3, target, "loss")
    dx2, d_s3, d_sh3, d_g3 = _ffn_bwd(dx3, x2, s3, sh3, g3, W["f2_w8"], W["f2_wo4"], saved3, "ffn2", ffn_grad_ready("f2"))
    dx1, G = _mixer_bwd(dx2, x1, s2, sh2, g2, cos_p, sin_p, W, saved2)
    d_sh2, d_s2, d_g2 = G.pop("sh2"), G.pop("s2"), G.pop("g2")
    g1 = g1 + mixer_grads_ready(G)
    dx, d_s1, d_sh1, d_g1 = _ffn_bwd(dx1, x, s1, sh1, g1, W1["f1_w8"], W1["f1_wo4"], saved1, "ffn1", ffn_grad_ready("f1"))
    d_mod = jnp.concatenate([d_sh1, d_s1, d_g1, d_sh2, d_s2, d_g2, d_sh3, d_s3, d_g3], axis=1)
    return loss_row, dx, d_mod


WEIGHT_NAMES = ("w_ada", "b_ada", "ffn1_w_in", "ffn1_w_out", "w_in", "gdn_conv_w", "gdn_a_log", "gdn_dt_bias", "gdn_norm_w",
                "mla_q_norm_w", "mla_w_uq", "mla_kv_norm_w", "mla_w_ukv", "qkn_q_nope", "qkn_q_rope", "qkn_k_nope",
                "qkn_k_rope", "mla_out_norm_w", "w_out", "ffn2_w_in", "ffn2_w_out")
FFN_SHARDED = ("ffn1_w_in", "ffn1_w_out", "ffn2_w_in", "ffn2_w_out")
SHEETED = (("w_in", "col"), ("gdn_conv_w", "col"), ("mla_w_uq", "col"), ("mla_w_ukv", "col"), ("w_out", "row"))
MOD_ROWS = N_MOD * D_MODEL // 128
SMALL = {"gdn_a_log": (MOD_ROWS, 1, 64, 4), "gdn_dt_bias": (MOD_ROWS + 1, 1, 64, 4), "gdn_norm_w": (MOD_ROWS + 2, 1, 0, 128),
         "mla_q_norm_w": (MOD_ROWS + 3, 3, 0, 384), "mla_kv_norm_w": (MOD_ROWS + 6, 2, 0, 256),
         "qkn_q_nope": (MOD_ROWS + 8, 1, 0, 128), "qkn_q_rope": (MOD_ROWS + 9, 1, 0, 64), "qkn_k_nope": (MOD_ROWS + 10, 1, 0, 128),
         "qkn_k_rope": (MOD_ROWS + 11, 1, 0, 64), "mla_out_norm_w": (MOD_ROWS + 12, 1, 0, 128)}
LOSS_ROW = MOD_ROWS + 13
CONV_ROW, CONV_ROWS = 88, 4 * 1536 // 128
SHEET_ROWS = CONV_ROW + CONV_ROWS


def _to_sheet(flat, dtype, sublanes):
    n = flat.shape[-1]
    unit = sublanes * 128
    pad = (-n) % unit
    flat = jnp.pad(flat.astype(dtype), [(0, 0)] * (flat.ndim - 1) + [(0, pad)])
    return flat.reshape(flat.shape[:-1] + ((n + pad) // 128, 128))


def _small_sheet(b_like, small):
    sheet = jnp.zeros((SHEET_ROWS, 128), F32).at[:MOD_ROWS].set(b_like.reshape(MOD_ROWS, 128))
    for name, (row, rows, lane, n) in SMALL.items():
        v = small[name].reshape(1, n)
        if rows == 1:
            sheet = sheet.at[row, lane:lane + n].set(v[0])
        else:
            sheet = sheet.at[row:row + rows].set(v.reshape(rows, 128))
    return sheet


def _from_small_sheet(sheet):
    out = {"b_ada": sheet[:MOD_ROWS].reshape(1, N_MOD * D_MODEL)}
    for name, (row, rows, lane, n) in SMALL.items():
        out[name] = sheet[row, lane:lane + n].reshape(1, n) if rows == 1 else sheet[row:row + rows].reshape(1, n)
    return out


def kernel(x, c, positions, w_ada, b_ada, ffn1_w_in, ffn1_w_out, w_in, gdn_conv_w, gdn_a_log, gdn_dt_bias, gdn_norm_w, mla_q_norm_w, mla_w_uq, mla_kv_norm_w, mla_w_ukv, qkn_q_nope, qkn_q_rope, qkn_k_nope, qkn_k_rope, mla_out_norm_w, w_out, ffn2_w_in, ffn2_w_out, loss_target, m_w_ada, m_b_ada, m_ffn1_w_in, m_ffn1_w_out, m_w_in, m_gdn_conv_w, m_gdn_a_log, m_gdn_dt_bias, m_gdn_norm_w, m_mla_q_norm_w, m_mla_w_uq, m_mla_kv_norm_w, m_mla_w_ukv, m_qkn_q_nope, m_qkn_q_rope, m_qkn_k_nope, m_qkn_k_rope, m_mla_out_norm_w, m_w_out, m_ffn2_w_in, m_ffn2_w_out, v_w_ada, v_b_ada, v_ffn1_w_in, v_ffn1_w_out, v_w_in, v_gdn_conv_w, v_gdn_a_log, v_gdn_dt_bias, v_gdn_norm_w, v_mla_q_norm_w, v_mla_w_uq, v_mla_kv_norm_w, v_mla_w_ukv, v_qkn_q_nope, v_qkn_q_rope, v_qkn_k_nope, v_qkn_k_rope, v_mla_out_norm_w, v_w_out, v_ffn2_w_in, v_ffn2_w_out):
    args = locals()
    w = {n: args[n] for n in WEIGHT_NAMES}
    m = {n: args["m_" + n] for n in WEIGHT_NAMES}
    v = {n: args["v_" + n] for n in WEIGHT_NAMES}
    me = 4 * lax.axis_index("x") + 2 * lax.axis_index("y") + lax.axis_index("c")
    cols = N_MOD * D_MODEL // N_DEV
    shard = {n: w[n][0] for n in FFN_SHARDED + tuple(s[0] for s in SHEETED)}

    sc = c * _sigmoid(c)
    first = _to_sheet(jnp.concatenate([sc.reshape(-1), shard["gdn_conv_w"].reshape(-1)]), F32, 8)
    (first_all,) = _all_gather([first], "gather_c")
    sc_all = first_all[:, :D_MODEL // 128].reshape(N_DEV, D_MODEL)
    n_taps = shard["gdn_conv_w"].size
    conv_all = first_all.reshape(N_DEV, -1)[:, D_MODEL:D_MODEL + n_taps].reshape(N_DEV, 4, -1)
    b_mine = lax.dynamic_slice(b_ada, (0, me * cols), (1, cols))
    mod_cols = _mm(sc_all, w_ada[0], "nn", name="ada_mod", extra_params=[b_mine], epi=lambda acc, b_: (acc + b_,))
    (mod_all,) = _all_to_all([_to_sheet(mod_cols, F32, 8)], "scatter_mod")
    mod = mod_all.reshape(N_DEV, -1)[:, :cols].reshape(1, N_MOD * D_MODEL)

    f1_shards, mod = lax.optimization_barrier(([shard["ffn1_w_in"].astype(BF16), shard["ffn1_w_out"].astype(BF16)], mod))
    f1_w8, f1_out = _all_gather(f1_shards, "gather_w1")
    travel = [s for s in SHEETED if s[0] != "gdn_conv_w"]
    tied = lax.optimization_barrier(([shard[n].astype(BF16) for n, _ in travel], f1_w8))
    f1_w8 = tied[1]
    mixer_w = _exchange_start(tied[0], False, "gather_wm_start")
    ffn2_w = _exchange_start([shard["ffn2_w_in"].astype(BF16) + mixer_w[4][0:1, 0:1].astype(BF16),
                              shard["ffn2_w_out"].astype(BF16)], False, "gather_w2_start")
    mod = mod + ffn2_w[4][0:1, 0:1]
    W1 = dict(f1_w8=f1_w8, f1_wo4=f1_out.reshape(HID_PIECES, FFN_PIECE, D_MODEL))

    def mixer_weights(after):
        got = _exchange_wait(mixer_w, False, after, "gather_wm_wait")
        P = {n: jnp.concatenate(list(g), axis=1) if kind == "col" else g.reshape(-1, g.shape[-1])
             for (n, kind), g in zip(travel, got)}
        P["gdn_conv_w"] = jnp.concatenate(list(conv_all), axis=1)
        for n in SMALL:
            P[n] = w[n]
        return _pack_weights(P)

    def ffn2_weights(after):
        f2_w8, f2_out = _exchange_wait(ffn2_w, False, after, "gather_w2_wait")
        return dict(f2_w8=f2_w8, f2_wo4=f2_out.reshape(HID_PIECES, FFN_PIECE, D_MODEL))

    pending, small_grads = {}, {}

    def ffn_grad_ready(tag):
        def ready(which, g):
            pieces = g if which == "w8" else g.reshape((N_DEV,) + shard["ffn1_w_out"].shape)
            pending[tag + which] = _exchange_start([pieces], True, "scatter_%s_%s_start" % (tag, which))
            return pending[tag + which][4]
        return ready

    def mixer_grads_ready(G):
        g_full = _unpack_grads(G)
        small_grads.update({n: g_full[n] for n in SMALL})
        small_grads["gdn_conv_w"] = g_full["gdn_conv_w"]
        pieces = []
        for n, kind in travel:
            r, cc = shard[n].shape
            g = g_full[n].astype(BF16)
            pieces.append(jnp.stack([g[:, cc * p:cc * (p + 1)] for p in range(N_DEV)]) if kind == "col"
                          else g.reshape(N_DEV, r, cc))
        pending["mixer"] = _exchange_start(pieces, True, "scatter_mx_start")
        return pending["mixer"][4][0:1, 0:1]

    cos_p, sin_p = _rope_tables(positions[0])
    loss_row, dx, d_mod = _local_step(x[0], loss_target[0], mod, cos_p, sin_p, W1, mixer_weights, ffn2_weights,
                                      ffn_grad_ready, mixer_grads_ready)

    sheet = _small_sheet(d_mod, small_grads).at[LOSS_ROW].set(loss_row[0, :128])
    sheet = sheet.at[CONV_ROW:CONV_ROW + CONV_ROWS].set(small_grads["gdn_conv_w"].reshape(CONV_ROWS, 128))
    (sheets,) = _all_gather([sheet], "gather_small")
    summed = _sum_devices(sheets, "sum_small")
    d_mod_all = sheets[:, :MOD_ROWS].reshape(N_DEV, N_MOD * D_MODEL)
    d_mod_mine = lax.dynamic_slice(d_mod_all, (0, me * cols), (N_DEV, cols))
    grads = _from_small_sheet(summed)
    grads["w_ada"] = _mm(sc_all, d_mod_mine, "tn", name="ada_gw", hi=True)
    conv_taps = shard["gdn_conv_w"].shape[1]
    grads["gdn_conv_w"] = lax.dynamic_slice(summed[CONV_ROW:CONV_ROW + CONV_ROWS].reshape(4, -1), (0, me * conv_taps),
                                            (4, conv_taps))
    loss = summed[LOSS_ROW, 0]

    for n, key in zip(FFN_SHARDED, ("f1w8", "f1wo4", "f2w8", "f2wo4")):
        (parts,) = _exchange_wait(pending[key], True, summed, "scatter_%s_wait" % key)
        grads[n] = _sum_devices(parts, "sum_" + n)
    for (n, _), parts in zip(travel, _exchange_wait(pending["mixer"], True, summed, "scatter_mx_wait")):
        grads[n] = _sum_devices(parts, "sum_" + n)

    delta, new_m, new_v = {}, {}, {}
    for n in ("w_ada",) + FFN_SHARDED + tuple(s[0] for s in SHEETED):
        delta[n], new_m[n], new_v[n] = _adamw(w[n][0], grads[n], m[n][0], v[n][0], "adamw_" + n)
    small_in = [_small_sheet(t["b_ada"], t) for t in (w, grads, m, v)]
    for res, out in zip(_adamw(*small_in, "adamw_small"), (delta, new_m, new_v)):
        out.update(_from_small_sheet(res))

    def shaped(d):
        return [d[n].reshape(w[n].shape) for n in WEIGHT_NAMES]

    return (loss, dx[None], *shaped(grads), *shaped(delta), *shaped(new_m), *shaped(new_v))
```

```python
import functools

import jax
import jax.numpy as jnp
import numpy as np
from jax import lax
from jax.experimental import pallas as pl
from jax.experimental.pallas import tpu as pltpu

F32 = jnp.float32
BF16 = jnp.bfloat16

D_MODEL = 1024
D_FF = 2816
N_MOD = 9
HEADS = 4
HEAD_DIM = 128
CHUNK = 64
EPS = 1e-6
ROPE = 64
Q_LORA = 384
KV_LORA = 256
N_IN = 2760
N_IN_PACKED = 2816
ROPE_BASE = 10000.0
N_DEV = 8

ADAM_LR = 0.001
ADAM_B1 = 0.9
ADAM_B2 = 0.999
ADAM_EPS = 1e-08
ADAM_WD = 0.01
ADAM_STEP = 10

VMEM_LIMIT_BYTES = 56 * 1024 * 1024
MATMUL_ROWS = (1024, 512, 256, 128)
MESH = pl.DeviceIdType.MESH


def _params(sem=None):
    return pltpu.CompilerParams(dimension_semantics=sem, vmem_limit_bytes=VMEM_LIMIT_BYTES)


def _pick(dim, prefs):
    for p in prefs:
        if dim % p == 0:
            return p
    return dim


_DIMS = {"nn": (((1,), (0,)), ((), ())), "nt": (((1,), (1,)), ((), ())), "tn": (((0,), (0,)), ((), ()))}


def _dot_raw(a, b, mode):
    return lax.dot_general(a.astype(BF16), b.astype(BF16), _DIMS[mode], preferred_element_type=F32)


def _dot_hi(a, b, mode="nn"):
    return lax.dot_general(a, b, _DIMS[mode], precision=lax.Precision.HIGHEST, preferred_element_type=F32)


@functools.partial(jax.custom_vjp, nondiff_argnums=(2,))
def _bdot(a, b, mode):
    return _dot_raw(a, b, mode)


def _bdot_fwd(a, b, mode):
    return _dot_raw(a, b, mode), (a, b)


def _bdot_bwd(mode, res, g):
    a, b = res
    if mode == "nn":
        return _dot_raw(g, b, "nt"), _dot_raw(a, g, "tn")
    if mode == "nt":
        return _dot_raw(g, b, "nn"), _dot_raw(g, a, "tn")
    return _dot_raw(b, g, "nt"), _dot_raw(a, g, "nn")


_bdot.defvjp(_bdot_fwd, _bdot_bwd)


def _mm(a, b, mode, *, name, out_dtypes=(F32,), epi=None, extras=(), extra_params=(), hi=False,
        tm=None, tn=None, tk=None):
    if mode == "nn":
        (M, K), (_, N) = a.shape, b.shape
    elif mode == "nt":
        (M, K), (N, _) = a.shape, b.shape
    else:
        (K, M), (_, N) = a.shape, b.shape
    tm = tm or _pick(M, (512, 1408, 256, 128) if mode == "tn" else MATMUL_ROWS + (384, 352))
    tn = tn or _pick(N, (1024, 1408, 768, 512, 384, 256, 128))
    tk = tk or _pick(K, (1024, 1408, 512, 384, 256, 128))
    a_spec = {"nn": pl.BlockSpec((tm, tk), lambda i, j, k: (i, k)), "nt": pl.BlockSpec((tm, tk), lambda i, j, k: (i, k)),
              "tn": pl.BlockSpec((tk, tm), lambda i, j, k: (k, i))}[mode]
    b_spec = {"nn": pl.BlockSpec((tk, tn), lambda i, j, k: (k, j)), "nt": pl.BlockSpec((tn, tk), lambda i, j, k: (j, k)),
              "tn": pl.BlockSpec((tk, tn), lambda i, j, k: (k, j))}[mode]
    mn_spec = pl.BlockSpec((tm, tn), lambda i, j, k: (i, j))
    return _mmg(a, b, mode, name=name, grid=(M // tm, N // tn, K // tk), a_spec=a_spec, b_spec=b_spec, out_spec=mn_spec,
                out_shapes=[jax.ShapeDtypeStruct((M, N), dt) for dt in out_dtypes], acc_shape=(tm, tn), epi=epi,
                extras=list(extras) + list(extra_params),
                extra_specs=[mn_spec] * len(extras) + [pl.BlockSpec((1, tn), lambda i, j, k: (0, j))] * len(extra_params),
                hi=hi)


def _mmg(a, b, mode, *, name, grid, a_spec, b_spec, out_spec, out_shapes, acc_shape, epi=None, extras=(),
         extra_specs=(), hi=False):
    nk = grid[2]
    n_e, n_o = len(extras), len(out_shapes)

    def body(*refs):
        a_ref, b_ref = refs[:2]
        e_refs = refs[2:2 + n_e]
        o_refs = refs[2 + n_e:2 + n_e + n_o]
        acc_ref = refs[-1]
        k = pl.program_id(2)

        @pl.when(k == 0)
        def _():
            acc_ref[...] = jnp.zeros_like(acc_ref)

        if hi:
            acc_ref[...] += _dot_hi(a_ref[...].astype(F32), b_ref[...].astype(F32), mode)
        else:
            acc_ref[...] += _dot_raw(a_ref[...], b_ref[...], mode)

        @pl.when(k == nk - 1)
        def _():
            acc = acc_ref[...]
            outs = (acc,) if epi is None else epi(acc, *[e[...].astype(F32) for e in e_refs])
            for o_ref, o in zip(o_refs, outs):
                o_ref[...] = o.astype(o_ref.dtype)

    outs = pl.pallas_call(
        body, name=name, grid=grid,
        in_specs=[a_spec, b_spec] + list(extra_specs),
        out_specs=[out_spec] * n_o,
        out_shape=list(out_shapes),
        scratch_shapes=[pltpu.VMEM(acc_shape, F32)],
        compiler_params=_params(("parallel", "parallel", "arbitrary")),
    )(a, b, *extras)
    return outs if n_o > 1 else outs[0]


def _row_spec(th, cw, ci):
    return pl.BlockSpec((th, cw), lambda i: (i, ci))


def _full_spec(shape):
    return pl.BlockSpec(shape, lambda i: (0,) * len(shape))


def _rowwise(fn, rows, params, outs, n_steps, name):
    n_r, n_p, n_o = len(rows), len(params), len(outs)

    def body(*refs):
        vals = [r[...].astype(F32) for r in refs[:n_r + n_p]]
        res = fn(*vals)
        for o_ref, o in zip(refs[n_r + n_p:], res):
            o_ref[...] = o.astype(o_ref.dtype)

    across = [len(o) == 4 for o in outs]
    res = pl.pallas_call(
        body, name=name, grid=(n_steps,),
        in_specs=[_row_spec(th, cw, ci) for (_, th, cw, ci) in rows] + [_full_spec(p.shape) for p in params],
        out_specs=[pl.BlockSpec((o[0], o[1]), lambda i: (0, i)) if ac else _row_spec(o[0], o[1], 0)
                   for o, ac in zip(outs, across)],
        out_shape=[jax.ShapeDtypeStruct((o[0], n_steps * o[1]) if ac else (n_steps * o[0], o[1]), o[2])
                   for o, ac in zip(outs, across)],
        compiler_params=_params(("parallel",)),
    )(*[r[0] for r in rows], *params)
    return res


def _rowwise_bwd(fn, rows, aux, params, douts, n_steps, name, row_dtypes=None, adds=()):
    n_r, n_a, n_p, n_d, n_add = len(rows), len(aux), len(params), len(douts), len(adds)
    row_dtypes = row_dtypes or (F32,) * n_r

    def body(*refs):
        it = iter(refs)
        r_vals = [next(it)[...].astype(F32) for _ in range(n_r)]
        a_vals = [next(it)[...].astype(F32) for _ in range(n_a)]
        p_vals = [next(it)[...].astype(F32) for _ in range(n_p)]
        d_vals = [next(it)[...].astype(F32) for _ in range(n_d)]
        add_vals = [next(it)[...].astype(F32) for _ in range(n_add)]
        dr_refs = [next(it) for _ in range(n_r)]
        dp_refs = [next(it) for _ in range(n_p)]

        def f(*rp):
            return tuple(fn(*rp[:n_r], *a_vals, *rp[n_r:]))

        _, vjp = jax.vjp(f, *r_vals, *p_vals)
        grads = list(vjp(tuple(d_vals)))
        for (ri, _), av in zip(adds, add_vals):
            grads[ri] = grads[ri] + av
        for dr_ref, g in zip(dr_refs, grads[:n_r]):
            dr_ref[...] = g.astype(dr_ref.dtype)

        @pl.when(pl.program_id(0) == 0)
        def _():
            for dp_ref in dp_refs:
                dp_ref[...] = jnp.zeros_like(dp_ref)

        for dp_ref, g in zip(dp_refs, grads[n_r:]):
            dp_ref[...] += g

    all_rows = list(rows) + list(aux) + list(douts) + [(arr,) + tuple(rows[ri][1:3]) + (0,) for ri, arr in adds]
    in_specs = ([_row_spec(th, cw, ci) for (_, th, cw, ci) in list(rows) + list(aux)]
                + [_full_spec(p.shape) for p in params]
                + [_row_spec(th, cw, ci) for (_, th, cw, ci) in all_rows[n_r + n_a:]])
    res = pl.pallas_call(
        body, name=name, grid=(n_steps,),
        in_specs=in_specs,
        out_specs=[_row_spec(th, cw, 0) for (_, th, cw, _) in rows] + [_full_spec(p.shape) for p in params],
        out_shape=[jax.ShapeDtypeStruct((n_steps * th, cw), dt) for (_, th, cw, _), dt in zip(rows, row_dtypes)]
        + [jax.ShapeDtypeStruct(p.shape, F32) for p in params],
        compiler_params=_params(("arbitrary",)),
    )(*[r[0] for r in list(rows) + list(aux)], *params, *[r[0] for r in all_rows[n_r + n_a:]])
    return res[:n_r], res[n_r:]


def _sigmoid(x):
    return lax.logistic(x)


def _silu(x):
    return x * _sigmoid(x)


def _rms(x, w=None, n=None):
    n = n or x.shape[-1]
    y = x * lax.rsqrt(jnp.sum(x * x, axis=-1, keepdims=True) * (1.0 / n) + EPS)
    return y if w is None else y * w


def _modulate(x, scale, shift):
    return _rms(x) * (1.0 + scale) + shift


def _softplus(x):
    return jnp.maximum(x, 0.0) + jnp.log1p(jnp.exp(-jnp.abs(x)))


@jax.custom_vjp
def _rot_half64(x):
    lane = lax.broadcasted_iota(jnp.int32, x.shape, 1)
    up = pltpu.roll(x, 96, 1)
    down = pltpu.roll(x, 32, 1)
    return jnp.where(lane < 32, up, jnp.where(lane < 64, down, 0.0))


_rot_half64.defvjp(lambda x: (_rot_half64(x), None), lambda _, g: (_rot_half64(g),))


def _rope128(x, cos_p, sin_p):
    return x * cos_p + _rot_half64(x) * sin_p


def _gdn_pre_fn(qkvc, kab, alog_p, dt_p):
    a = _silu(qkvc)
    qs, ks = [], []
    for h in range(HEADS):
        qh = a[:, HEAD_DIM * h:HEAD_DIM * (h + 1)]
        kh = a[:, 512 + HEAD_DIM * h:512 + HEAD_DIM * (h + 1)]
        qs.append(qh * lax.rsqrt(jnp.sum(qh * qh, axis=-1, keepdims=True) + EPS) * (HEAD_DIM ** -0.5))
        ks.append(kh * lax.rsqrt(jnp.sum(kh * kh, axis=-1, keepdims=True) + EPS))
    lane = lax.broadcasted_iota(jnp.int32, kab.shape, 1)
    g_full = -jnp.exp(alog_p) * _softplus(kab + dt_p)
    b_full = _sigmoid(kab)
    gb = jnp.where((lane >= 64) & (lane < 68), g_full, jnp.where((lane >= 68) & (lane < 72), b_full, 0.0))
    return jnp.concatenate(qs, axis=1), jnp.concatenate(ks, axis=1), a[:, 1024:1536], gb


INTRA_ROWS = (256, 128, 64)

_BNN = (((2,), (1,)), ((0,), (0,)))
_BNT = (((2,), (2,)), ((0,), (0,)))


def _split_bf16(a):
    hi = a.astype(BF16)
    return hi, (a - hi.astype(F32)).astype(BF16)


def _dot3_raw(a, b, dims):
    a_hi, a_lo = _split_bf16(a)
    b_hi, b_lo = _split_bf16(b)
    dot = lambda x_, y_: lax.dot_general(x_, y_, dims, preferred_element_type=F32)
    return dot(a_hi, b_hi) + (dot(a_hi, b_lo) + dot(a_lo, b_hi))


@functools.partial(jax.custom_vjp, nondiff_argnums=(2, 3))
def _dot3(a, b, nt, exact_bwd=True):
    return _dot3_raw(a, b, _BNT if nt else _BNN)


def _dot3_fwd(a, b, nt, exact_bwd):
    return _dot3_raw(a, b, _BNT if nt else _BNN), (a, b)


def _dot3_bwd(nt, exact_bwd, res, g):
    a, b = res
    if exact_bwd:
        dot = _dot3_raw
    else:
        dot = lambda x_, y_, d_: lax.dot_general(x_.astype(BF16), y_.astype(BF16), d_, preferred_element_type=F32)
    if nt:
        return dot(g, b, _BNN), dot(jnp.swapaxes(g, 1, 2), a, _BNN)
    return dot(g, b, _BNT), dot(jnp.swapaxes(a, 1, 2), g, _BNN)


_dot3.defvjp(_dot3_fwd, _dot3_bwd)


@functools.partial(jax.custom_vjp, nondiff_argnums=(2,))
def _bdot_b(a, b, nt):
    return lax.dot_general(a.astype(BF16), b.astype(BF16), _BNT if nt else _BNN, preferred_element_type=F32)


def _bdot_b_fwd(a, b, nt):
    return _bdot_b(a, b, nt), (a, b)


def _bdot_b_bwd(nt, res, g):
    a, b = res
    dot = lambda x_, y_, d_: lax.dot_general(x_.astype(BF16), y_.astype(BF16), d_, preferred_element_type=F32)
    if nt:
        return dot(g, b, _BNN), dot(jnp.swapaxes(g, 1, 2), a, _BNN)
    return dot(g, b, _BNT), dot(jnp.swapaxes(a, 1, 2), g, _BNN)


_bdot_b.defvjp(_bdot_b_fwd, _bdot_b_bwd)


def _intra_batched(q, k, v, g_col, b_col):
    c = CHUNK
    nb = q.shape[0]
    row = lax.broadcasted_iota(jnp.int32, (1, c, c), 1)
    col = lax.broadcasted_iota(jnp.int32, (1, c, c), 2)
    incl, strict, eye = row >= col, row > col, row == col
    tri = jnp.broadcast_to(jnp.where(incl, 1.0, 0.0).astype(F32), (nb, c, c))
    ident = jnp.where(eye, 1.0, 0.0).astype(F32)
    g_wide = _dot3(tri, jnp.broadcast_to(g_col, (nb, c, HEAD_DIM)), False)
    g_i = g_wide[:, :, :c]
    g_j = jnp.sum(jnp.where(eye, g_i, 0.0), axis=1, keepdims=True)
    decay = jnp.where(incl, jnp.exp(jnp.where(incl, g_i - g_j, 0.0)), 0.0)
    kk = _bdot_b(k, k, True)
    a_mat = jnp.where(strict, b_col * kk * decay, 0.0)
    x_pow = -a_mat
    inv = ident + x_pow
    for _ in range(5):
        x_pow = _dot3(x_pow, x_pow, False, False)
        inv = inv + _dot3(inv, x_pow, False, False)
    e_wide = jnp.exp(g_wide)
    u = _dot3(inv, v * b_col, False)
    wk = _dot3(inv, k * b_col * e_wide, False)
    qk = _bdot_b(q, k, True) * decay
    last = lax.broadcasted_iota(jnp.int32, (1, c, HEAD_DIM), 1) == c - 1
    g_last = jnp.sum(jnp.where(last, g_wide, 0.0), axis=1, keepdims=True)
    qd = q * e_wide
    kd = k * jnp.exp(g_last - g_wide)
    gl = jnp.broadcast_to(jnp.exp(g_last), (nb, 8, HEAD_DIM))
    return u, wk, qd, kd, qk, gl


def _gdn_intra_fn(q, k, v, gb):
    t = q.shape[0]
    nch = t // CHUNK
    lane = lax.broadcasted_iota(jnp.int32, gb.shape, 1)

    def heads_first(x_):
        return jnp.concatenate([x_[:, HEAD_DIM * h:HEAD_DIM * (h + 1)].reshape(nch, CHUNK, HEAD_DIM) for h in range(HEADS)],
                               axis=0)

    def column(first_lane):
        return jnp.concatenate([jnp.sum(jnp.where(lane == first_lane + h, gb, 0.0), axis=1, keepdims=True)
                                .reshape(nch, CHUNK, 1) for h in range(HEADS)], axis=0)

    u, wk, qd, kd, qk, gl = _intra_batched(heads_first(q), heads_first(k), heads_first(v), column(64), column(68))

    def rows_first(x_):
        r, w_ = x_.shape[1], x_.shape[2]
        return jnp.concatenate([x_[nch * h:nch * (h + 1)].reshape(nch * r, w_) for h in range(HEADS)], axis=1)

    qks = [qk[nch * h:nch * (h + 1)].reshape(t, CHUNK) for h in range(HEADS)]
    return (rows_first(u), rows_first(wk), rows_first(qd), rows_first(kd), *qks, rows_first(gl))


def _scan_step(s0, u, wk, qd, kd, qk, gl):
    v_new = u - _bdot_b(wk, s0, False)
    o = _bdot_b(qd, s0, False) + _bdot_b(qk, v_new, False)
    s1 = s0 * gl[:, 0:1, :] + _bdot_b(jnp.swapaxes(kd, 1, 2), v_new, False)
    return o, s1


def _mix_post_fn(o_a, z, o_b, gnw, onw):
    parts = [_rms(o_a[:, HEAD_DIM * h:HEAD_DIM * (h + 1)], gnw) * _silu(z[:, HEAD_DIM * h:HEAD_DIM * (h + 1)])
             for h in range(HEADS)]
    parts += [_rms(o_b[:, HEAD_DIM * h:HEAD_DIM * (h + 1)], onw) for h in range(HEADS)]
    return (jnp.concatenate(parts, axis=1),)


def _mla_pre_fn(ckv, cq, kab, cos_p, sin_p, qnw, kvnw, wuq, wukv, qn_w, qr_w, kn_w, kr_w):
    scale = (HEAD_DIM + ROPE) ** -0.5
    qf = _bdot(_rms(cq, qnw), wuq, "nn")
    kvf = _bdot(_rms(ckv, kvnw), wukv, "nn")
    lane = lax.broadcasted_iota(jnp.int32, kab.shape, 1)
    kr = _rope128(_rms(jnp.where(lane < ROPE, kab, 0.0), kr_w, n=ROPE), cos_p, sin_p)
    qs, ks = [], []
    for h in range(HEADS):
        qn = _rms(qf[:, 256 * h:256 * h + 128], qn_w) * scale
        qr = _rope128(_rms(qf[:, 256 * h + 128:256 * h + 256], qr_w, n=ROPE), cos_p, sin_p) * scale
        qs += [qn, qr]
        ks += [_rms(kvf[:, 128 * h:128 * (h + 1)], kn_w), kr]
    return jnp.concatenate(qs, axis=1), jnp.concatenate(ks, axis=1), kvf[:, 512:]


def _conv_fwd(proj, conv_w, tm, name):
    S = proj.shape[0]
    C = 1536
    nb = tm // 8

    def body(x_ref, prev_ref, w_ref, o_ref, ext_ref):
        i = pl.program_id(0)
        ext_ref[0:8, :] = jnp.where(i > 0, prev_ref[...], 0.0)
        ext_ref[8:, :] = x_ref[...]
        acc = jnp.zeros((tm, C), F32)
        for k in range(4):
            acc = acc + w_ref[k:k + 1, :] * ext_ref[pl.ds(5 + k, tm), :]
        o_ref[...] = acc

    return pl.pallas_call(
        body, name=name, grid=(S // tm,),
        in_specs=[pl.BlockSpec((tm, C), lambda i: (i, 0)),
                  pl.BlockSpec((8, C), lambda i: (jnp.maximum(i * nb - 1, 0), 0)),
                  pl.BlockSpec((4, C), lambda i: (0, 0))],
        out_specs=pl.BlockSpec((tm, C), lambda i: (i, 0)),
        out_shape=jax.ShapeDtypeStruct((S, C), F32),
        scratch_shapes=[pltpu.VMEM((tm + 8, C), F32)],
        compiler_params=_params(("arbitrary",)),
    )(proj, proj, conv_w)


def _conv_bwd(proj, dout, conv_w, tm, name):
    S = proj.shape[0]
    C = 1536
    nb = tm // 8
    n_steps = S // tm

    def body(x_ref, prev_ref, d_ref, next_ref, w_ref, dx_ref, dw_ref, xext_ref, dext_ref):
        i = pl.program_id(0)
        xext_ref[0:8, :] = jnp.where(i > 0, prev_ref[...], 0.0)
        xext_ref[8:, :] = x_ref[...]
        dext_ref[0:tm, :] = d_ref[...]
        dext_ref[tm:, :] = jnp.where(i < n_steps - 1, next_ref[...], 0.0)
        d = d_ref[...]
        acc = jnp.zeros((tm, C), F32)
        dws = []
        for k in range(4):
            acc = acc + w_ref[k:k + 1, :] * dext_ref[pl.ds(3 - k, tm), :]
            dws.append(jnp.sum(d * xext_ref[pl.ds(5 + k, tm), :], axis=0, keepdims=True))
        dx_ref[...] = acc

        @pl.when(i == 0)
        def _():
            dw_ref[...] = jnp.zeros_like(dw_ref)

        dw_ref[...] += jnp.concatenate(dws + [jnp.zeros((4, C), F32)], axis=0)

    return pl.pallas_call(
        body, name=name, grid=(n_steps,),
        in_specs=[pl.BlockSpec((tm, C), lambda i: (i, 0)),
                  pl.BlockSpec((8, C), lambda i: (jnp.maximum(i * nb - 1, 0), 0)),
                  pl.BlockSpec((tm, C), lambda i: (i, 0)),
                  pl.BlockSpec((8, C), lambda i: (jnp.minimum((i + 1) * nb, S // 8 - 1), 0)),
                  pl.BlockSpec((4, C), lambda i: (0, 0))],
        out_specs=[pl.BlockSpec((tm, C), lambda i: (i, 0)), pl.BlockSpec((8, C), lambda i: (0, 0))],
        out_shape=[jax.ShapeDtypeStruct((S, C), F32), jax.ShapeDtypeStruct((8, C), F32)],
        scratch_shapes=[pltpu.VMEM((tm + 8, C), F32), pltpu.VMEM((tm + 8, C), F32)],
        compiler_params=_params(("arbitrary",)),
    )(proj, proj, dout, dout, conv_w)


def _gdn_scan_fwd(u, wk, qd, kd, qks, gl, name):
    S = u.shape[0]
    nc = S // CHUNK
    W = HEADS * HEAD_DIM

    def body(u_ref, wk_ref, qd_ref, kd_ref, qk0, qk1, qk2, qk3, gl_ref, o_ref, sp_ref, s_ref):
        @pl.when(pl.program_id(0) == 0)
        def _():
            s_ref[...] = jnp.zeros_like(s_ref)

        s0 = s_ref[...]
        sp_ref[0] = s0
        o, s1 = _scan_step(s0, _heads(u_ref, HEAD_DIM), _heads(wk_ref, HEAD_DIM), _heads(qd_ref, HEAD_DIM),
                           _heads(kd_ref, HEAD_DIM), jnp.stack([r[...] for r in (qk0, qk1, qk2, qk3)]),
                           _heads(gl_ref, HEAD_DIM))
        s_ref[...] = s1
        for h in range(HEADS):
            o_ref[:, HEAD_DIM * h:HEAD_DIM * (h + 1)] = o[h]

    row = pl.BlockSpec((CHUNK, W), lambda n: (n, 0))
    qk_spec = pl.BlockSpec((CHUNK, CHUNK), lambda n: (n, 0))
    return pl.pallas_call(
        body, name=name, grid=(nc,),
        in_specs=[row, row, row, row, qk_spec, qk_spec, qk_spec, qk_spec, pl.BlockSpec((8, W), lambda n: (n, 0))],
        out_specs=[row, pl.BlockSpec((1, HEADS, HEAD_DIM, HEAD_DIM), lambda n: (n, 0, 0, 0))],
        out_shape=[jax.ShapeDtypeStruct((S, W), F32), jax.ShapeDtypeStruct((nc, HEADS, HEAD_DIM, HEAD_DIM), F32)],
        scratch_shapes=[pltpu.VMEM((HEADS, HEAD_DIM, HEAD_DIM), F32)],
        compiler_params=_params(("arbitrary",)),
    )(u, wk, qd, kd, *qks, gl)


def _gdn_scan_bwd(u, wk, qd, kd, qks, gl, s_prev, d_o, name):
    S = u.shape[0]
    nc = S // CHUNK
    W = HEADS * HEAD_DIM

    def body(u_ref, wk_ref, qd_ref, kd_ref, qk0, qk1, qk2, qk3, gl_ref, sp_ref, do_ref,
             du_ref, dwk_ref, dqd_ref, dkd_ref, dqk0, dqk1, dqk2, dqk3, dgl_ref, ds_ref):
        @pl.when(pl.program_id(0) == 0)
        def _():
            ds_ref[...] = jnp.zeros_like(ds_ref)

        _, vjp = jax.vjp(_scan_step, sp_ref[0], _heads(u_ref, HEAD_DIM), _heads(wk_ref, HEAD_DIM), _heads(qd_ref, HEAD_DIM),
                         _heads(kd_ref, HEAD_DIM), jnp.stack([r[...] for r in (qk0, qk1, qk2, qk3)]),
                         _heads(gl_ref, HEAD_DIM))
        ds0, du, dwk, dqd, dkd, dqk, dgl = vjp((_heads(do_ref, HEAD_DIM), ds_ref[...]))
        ds_ref[...] = ds0
        for h, dqk_ref in enumerate((dqk0, dqk1, dqk2, dqk3)):
            sl = slice(HEAD_DIM * h, HEAD_DIM * (h + 1))
            du_ref[:, sl] = du[h]
            dwk_ref[:, sl] = dwk[h]
            dqd_ref[:, sl] = dqd[h]
            dkd_ref[:, sl] = dkd[h]
            dqk_ref[...] = dqk[h]
            dgl_ref[:, sl] = dgl[h]

    rev = lambda n: (nc - 1 - n, 0)
    row = pl.BlockSpec((CHUNK, W), rev)
    qk_spec = pl.BlockSpec((CHUNK, CHUNK), rev)
    gl_spec = pl.BlockSpec((8, W), rev)
    qk_shape = jax.ShapeDtypeStruct((S, CHUNK), F32)
    row_shape = jax.ShapeDtypeStruct((S, W), F32)
    return pl.pallas_call(
        body, name=name, grid=(nc,),
        in_specs=[row, row, row, row, qk_spec, qk_spec, qk_spec, qk_spec, gl_spec,
                  pl.BlockSpec((1, HEADS, HEAD_DIM, HEAD_DIM), lambda n: (nc - 1 - n, 0, 0, 0)), row],
        out_specs=[row, row, row, row, qk_spec, qk_spec, qk_spec, qk_spec, gl_spec],
        out_shape=[row_shape] * 4 + [qk_shape] * 4 + [jax.ShapeDtypeStruct((nc * 8, W), F32)],
        scratch_shapes=[pltpu.VMEM((HEADS, HEAD_DIM, HEAD_DIM), F32)],
        compiler_params=_params(("arbitrary",)),
    )(u, wk, qd, kd, *qks, gl, s_prev, d_o)


NEG = -1e30


def _chunk_mask(i, j, t, transposed=False):
    q_axis, k_axis = (1, 0) if transposed else (0, 1)
    r = (i * t + lax.broadcasted_iota(jnp.int32, (t, t), q_axis)) // CHUNK
    c = (j * t + lax.broadcasted_iota(jnp.int32, (t, t), k_axis)) // CHUNK
    return c <= r


def _tile_pairs(n, by_key):
    pairs = [(i, j) for j in range(n) for i in range(j, n)] if by_key else [(i, j) for i in range(n) for j in range(i + 1)]
    return jnp.asarray(np.array([p[0] for p in pairs], np.int32)), jnp.asarray(np.array([p[1] for p in pairs], np.int32))


def _heads(ref, width):
    return jnp.stack([ref[:, width * h:width * (h + 1)] for h in range(HEADS)])


def _bmm(a, b, dims):
    return lax.dot_general(a.astype(BF16), b.astype(BF16), dims, preferred_element_type=F32)


def _attn_fwd(q, k, v_t, t, name):
    S = q.shape[0]
    n = S // t
    qi, kj = _tile_pairs(n, by_key=False)

    def body(qi_ref, kj_ref, q_ref, k_ref, vt_ref, o_ref, lse_ref, m_ref, l_ref, acc_ref):
        i, j = qi_ref[pl.program_id(0)], kj_ref[pl.program_id(0)]

        @pl.when(j == 0)
        def _():
            m_ref[...] = jnp.full_like(m_ref, NEG)
            l_ref[...] = jnp.zeros_like(l_ref)
            acc_ref[...] = jnp.zeros_like(acc_ref)

        def update(masked):
            s_t = _bmm(_heads(k_ref, 256), _heads(q_ref, 256), _BNT)
            if masked:
                s_t = jnp.where(_chunk_mask(i, j, t, transposed=True)[None], s_t, NEG)
            m_old = m_ref[...]
            m_new = jnp.maximum(m_old, jnp.max(s_t, axis=1, keepdims=True))
            p_t = jnp.exp(s_t - m_new)
            alpha = jnp.exp(m_old - m_new)
            l_ref[...] = alpha * l_ref[...] + jnp.sum(p_t, axis=1, keepdims=True)
            v_heads = jnp.stack([vt_ref[HEAD_DIM * h:HEAD_DIM * (h + 1), :] for h in range(HEADS)])
            acc_ref[...] = alpha * acc_ref[...] + _bmm(v_heads, p_t, _BNN)
            m_ref[...] = m_new

        @pl.when(j < i)
        def _():
            update(False)

        @pl.when(j == i)
        def _():
            update(True)
            for h in range(HEADS):
                sl = slice(HEAD_DIM * h, HEAD_DIM * (h + 1))
                o_ref[:, sl] = jnp.transpose(acc_ref[h] / l_ref[h])
                lse_ref[:, sl] = jnp.transpose(jnp.broadcast_to(m_ref[h] + jnp.log(l_ref[h]), (HEAD_DIM, t)))

    row = lambda p, qi_, kj_: (qi_[p], 0)
    return pl.pallas_call(
        body, name=name,
        grid_spec=pltpu.PrefetchScalarGridSpec(
            num_scalar_prefetch=2, grid=(qi.shape[0],),
            in_specs=[pl.BlockSpec((t, HEADS * 256), row), pl.BlockSpec((t, HEADS * 256), lambda p, qi_, kj_: (kj_[p], 0)),
                      pl.BlockSpec((HEADS * HEAD_DIM, t), lambda p, qi_, kj_: (0, kj_[p]))],
            out_specs=[pl.BlockSpec((t, HEADS * HEAD_DIM), row)] * 2,
            scratch_shapes=[pltpu.VMEM((HEADS, 1, t), F32), pltpu.VMEM((HEADS, 1, t), F32),
                            pltpu.VMEM((HEADS, HEAD_DIM, t), F32)]),
        out_shape=[jax.ShapeDtypeStruct((S, HEADS * HEAD_DIM), F32)] * 2,
        compiler_params=_params(("arbitrary",)),
    )(qi, kj, q, k, v_t)


def _attn_dq(q, k, v, o, lse, d_o, t, name):
    S = q.shape[0]
    n = S // t
    qi, kj = _tile_pairs(n, by_key=False)

    def body(qi_ref, kj_ref, q_ref, k_ref, v_ref, o_ref, lse_ref, do_ref, dq_ref, st_ref, acc_ref, delta_ref):
        i, j = qi_ref[pl.program_id(0)], kj_ref[pl.program_id(0)]

        @pl.when(j == 0)
        def _():
            acc_ref[...] = jnp.zeros_like(acc_ref)
            delta_ref[...] = jnp.sum(_heads(do_ref, HEAD_DIM) * _heads(o_ref, HEAD_DIM), axis=2, keepdims=True)

        def update(masked):
            kh = _heads(k_ref, 256)
            d_out = _heads(do_ref, HEAD_DIM)
            s = _bmm(_heads(q_ref, 256), kh, _BNT)
            p = jnp.exp(s - _heads(lse_ref, HEAD_DIM)[:, :, 0:1])
            if masked:
                p = jnp.where(_chunk_mask(i, j, t)[None], p, 0.0)
            dp = _bmm(d_out, _heads(v_ref, HEAD_DIM), _BNT)
            ds = p * (dp - delta_ref[...])
            acc_ref[...] += _bmm(ds, kh, _BNN)

        @pl.when(j < i)
        def _():
            update(False)

        @pl.when(j == i)
        def _():
            update(True)
            lane = lax.broadcasted_iota(jnp.int32, (t, HEAD_DIM), 1)
            stats = jnp.zeros((t, HEAD_DIM), F32)
            for h in range(HEADS):
                dq_ref[:, 256 * h:256 * (h + 1)] = acc_ref[h]
                stats = stats + jnp.where(lane == h, lse_ref[:, HEAD_DIM * h:HEAD_DIM * (h + 1)], 0.0)
                stats = stats + jnp.where(lane == HEADS + h, delta_ref[h], 0.0)
            st_ref[...] = jnp.transpose(stats)[0:8, :]

    kv = lambda p, qi_, kj_: (kj_[p], 0)
    row = lambda p, qi_, kj_: (qi_[p], 0)
    wide, narrow = pl.BlockSpec((t, HEADS * 256), row), pl.BlockSpec((t, HEADS * HEAD_DIM), row)
    return pl.pallas_call(
        body, name=name,
        grid_spec=pltpu.PrefetchScalarGridSpec(
            num_scalar_prefetch=2, grid=(qi.shape[0],),
            in_specs=[wide, pl.BlockSpec((t, HEADS * 256), kv), pl.BlockSpec((t, HEADS * HEAD_DIM), kv), narrow, narrow, narrow],
            out_specs=[wide, pl.BlockSpec((8, t), lambda p, qi_, kj_: (0, qi_[p]))],
            scratch_shapes=[pltpu.VMEM((HEADS, t, 256), F32), pltpu.VMEM((HEADS, t, 1), F32)]),
        out_shape=[jax.ShapeDtypeStruct((S, HEADS * 256), F32), jax.ShapeDtypeStruct((8, S), F32)],
        compiler_params=_params(("arbitrary",)),
    )(qi, kj, q, k, v, o, lse, d_o)


def _attn_dkv(q, k, v, d_o, stats, t, name):
    S = q.shape[0]
    n = S // t
    qi, kj = _tile_pairs(n, by_key=True)

    def body(qi_ref, kj_ref, q_ref, k_ref, v_ref, do_ref, st_ref, dk_ref, dv_ref, dk_acc, dv_acc):
        i, j = qi_ref[pl.program_id(0)], kj_ref[pl.program_id(0)]

        @pl.when(i == j)
        def _():
            dk_acc[...] = jnp.zeros_like(dk_acc)
            dv_acc[...] = jnp.zeros_like(dv_acc)

        def update(masked):
            qh = _heads(q_ref, 256)
            d_out = _heads(do_ref, HEAD_DIM)
            st = st_ref[...]
            lse_row = jnp.stack([st[h:h + 1, :] for h in range(HEADS)])
            delta_row = jnp.stack([st[HEADS + h:HEADS + h + 1, :] for h in range(HEADS)])
            s_t = _bmm(_heads(k_ref, 256), qh, _BNT)
            p_t = jnp.exp(s_t - lse_row)
            if masked:
                p_t = jnp.where(_chunk_mask(i, j, t, transposed=True)[None], p_t, 0.0)
            dv_acc[...] += _bmm(p_t, d_out, _BNN)
            dp_t = _bmm(_heads(v_ref, HEAD_DIM), d_out, _BNT)
            ds_t = p_t * (dp_t - delta_row)
            dk_acc[...] += _bmm(ds_t, qh, _BNN)

        @pl.when(i == j)
        def _():
            update(True)

        @pl.when(i > j)
        def _():
            update(False)

        @pl.when(i == n - 1)
        def _():
            for h in range(HEADS):
                dk_ref[:, 256 * h:256 * (h + 1)] = dk_acc[h]
                dv_ref[:, HEAD_DIM * h:HEAD_DIM * (h + 1)] = dv_acc[h]

    q_row = lambda p, qi_, kj_: (qi_[p], 0)
    k_row = lambda p, qi_, kj_: (kj_[p], 0)
    return pl.pallas_call(
        body, name=name,
        grid_spec=pltpu.PrefetchScalarGridSpec(
            num_scalar_prefetch=2, grid=(qi.shape[0],),
            in_specs=[pl.BlockSpec((t, HEADS * 256), q_row), pl.BlockSpec((t, HEADS * 256), k_row),
                      pl.BlockSpec((t, HEADS * HEAD_DIM), k_row), pl.BlockSpec((t, HEADS * HEAD_DIM), q_row),
                      pl.BlockSpec((8, t), lambda p, qi_, kj_: (0, qi_[p]))],
            out_specs=[pl.BlockSpec((t, HEADS * 256), k_row), pl.BlockSpec((t, HEADS * HEAD_DIM), k_row)],
            scratch_shapes=[pltpu.VMEM((HEADS, t, 256), F32), pltpu.VMEM((HEADS, t, HEAD_DIM), F32)]),
        out_shape=[jax.ShapeDtypeStruct((S, HEADS * 256), F32), jax.ShapeDtypeStruct((S, HEADS * HEAD_DIM), F32)],
        compiler_params=_params(("arbitrary",)),
    )(qi, kj, q, k, v, d_o, stats)


FFN_PIECE = 2 * D_FF // N_DEV
HID_PIECES = D_FF // FFN_PIECE


def _ffn_up(h, w8, name):
    S = h.shape[0]
    tm = _pick(S, MATMUL_ROWS)

    def body(h_ref, wg_ref, wu_ref, g_ref, u_ref, hid_ref):
        gate = _dot_raw(h_ref[...], wg_ref[...], "nn")
        up = _dot_raw(h_ref[...], wu_ref[...], "nn")
        sg = _sigmoid(gate)
        act = gate * sg
        g_ref[...] = (up * (sg * (1.0 + gate * (1.0 - sg)))).astype(BF16)
        u_ref[...] = act.astype(BF16)
        hid_ref[...] = (act * up).astype(BF16)

    o_spec = pl.BlockSpec((None, tm, FFN_PIECE), lambda i, j: (j, i, 0))
    return pl.pallas_call(
        body, name=name, grid=(S // tm, HID_PIECES),
        in_specs=[pl.BlockSpec((tm, D_MODEL), lambda i, j: (i, 0)),
                  pl.BlockSpec((None, D_MODEL, FFN_PIECE), lambda i, j: (j, 0, 0)),
                  pl.BlockSpec((None, D_MODEL, FFN_PIECE), lambda i, j: (j + HID_PIECES, 0, 0))],
        out_specs=[o_spec] * 3,
        out_shape=[jax.ShapeDtypeStruct((HID_PIECES, S, FFN_PIECE), BF16)] * 3,
        compiler_params=_params(("parallel", "parallel")),
    )(h, w8, w8)


def _after_specs(after):
    return [] if after is None else [pl.BlockSpec(memory_space=pl.ANY)]


def _after_args(after):
    return [] if after is None else [after]


def _ffn_gw8(h, d_gate, d_up, name, after=None):
    S = h.shape[0]
    tm = 512
    tk = _pick(S, (512, 256, 128))
    nk = S // tk

    def body(h_ref, dg_ref, du_ref, *rest):
        o_ref, acc_ref = rest[-2:]
        k = pl.program_id(1)

        @pl.when(k == 0)
        def _():
            acc_ref[...] = jnp.zeros_like(acc_ref)

        h_t = jnp.transpose(h_ref[...])
        for p in range(HID_PIECES):
            acc_ref[p] += _dot_raw(h_t, dg_ref[p], "nn")
            acc_ref[HID_PIECES + p] += _dot_raw(h_t, du_ref[p], "nn")

        @pl.when(k == nk - 1)
        def _():
            o_ref[...] = acc_ref[...].astype(o_ref.dtype)

    d_spec = pl.BlockSpec((HID_PIECES, tk, FFN_PIECE), lambda i, k: (0, k, 0))
    return pl.pallas_call(
        body, name=name, grid=(D_MODEL // tm, nk),
        in_specs=[pl.BlockSpec((tk, tm), lambda i, k: (k, i)), d_spec, d_spec] + _after_specs(after),
        out_specs=pl.BlockSpec((2 * HID_PIECES, tm, FFN_PIECE), lambda i, k: (0, i, 0)),
        out_shape=jax.ShapeDtypeStruct((2 * HID_PIECES, D_MODEL, FFN_PIECE), BF16),
        scratch_shapes=[pltpu.VMEM((2 * HID_PIECES, tm, FFN_PIECE), F32)],
        compiler_params=_params(("parallel", "arbitrary")),
    )(h, d_gate, d_up, *_after_args(after))


def _ffn_dh(d_gate, d_up, w8, x, d_out, scale, shift, name, after=None):
    S = d_gate.shape[1]
    tm = _pick(S, (512, 256, 128))

    def body(dg_ref, du_ref, wg_ref, wu_ref, x_ref, do_ref, sc_ref, sh_ref, *rest):
        dx_ref, dsc_ref, dsh_ref, acc_ref = rest[-4:]
        i, k = pl.program_id(0), pl.program_id(1)

        @pl.when(k == 0)
        def _():
            acc_ref[...] = jnp.zeros_like(acc_ref)

        acc_ref[...] += _dot_raw(dg_ref[...], wg_ref[...], "nt") + _dot_raw(du_ref[...], wu_ref[...], "nt")

        @pl.when((k == 0) & (i == 0))
        def _():
            dsc_ref[...] = jnp.zeros_like(dsc_ref)
            dsh_ref[...] = jnp.zeros_like(dsh_ref)

        @pl.when(k == HID_PIECES - 1)
        def _():
            _, vjp = jax.vjp(_modulate, x_ref[...], sc_ref[...], sh_ref[...])
            dx, dsc, dsh = vjp(acc_ref[...])
            dx_ref[...] = dx + do_ref[...]
            dsc_ref[...] += dsc
            dsh_ref[...] += dsh

    d_spec = pl.BlockSpec((None, tm, FFN_PIECE), lambda i, k: (k, i, 0))
    row = pl.BlockSpec((tm, D_MODEL), lambda i, k: (i, 0))
    par = pl.BlockSpec((1, D_MODEL), lambda i, k: (0, 0))
    return pl.pallas_call(
        body, name=name, grid=(S // tm, HID_PIECES),
        in_specs=[d_spec, d_spec,
                  pl.BlockSpec((None, D_MODEL, FFN_PIECE), lambda i, k: (k, 0, 0)),
                  pl.BlockSpec((None, D_MODEL, FFN_PIECE), lambda i, k: (k + HID_PIECES, 0, 0)),
                  row, row, par, par] + _after_specs(after),
        out_specs=[row, par, par],
        out_shape=[jax.ShapeDtypeStruct((S, D_MODEL), F32), jax.ShapeDtypeStruct((1, D_MODEL), F32),
                   jax.ShapeDtypeStruct((1, D_MODEL), F32)],
        scratch_shapes=[pltpu.VMEM((tm, D_MODEL), F32)],
        compiler_params=_params(("arbitrary", "arbitrary")),
    )(d_gate, d_up, w8, w8, x, d_out, scale, shift, *_after_args(after))


def _swiglu_bwd(d_hid, hid_by_gate, hid_by_up):
    return d_hid * hid_by_gate, d_hid * hid_by_up


def _adamw(w, g, m, v, name):
    R, C = w.shape
    tr = _pick(R, (256, 176, 128, 64, 32, 16, 8))

    def body(w_ref, g_ref, m_ref, v_ref, d_ref, nm_ref, nv_ref):
        g_ = g_ref[...]
        m_ = ADAM_B1 * m_ref[...] + (1.0 - ADAM_B1) * g_
        v_ = ADAM_B2 * v_ref[...] + (1.0 - ADAM_B2) * (g_ * g_)
        m_hat = m_ / (1.0 - ADAM_B1 ** ADAM_STEP)
        v_hat = v_ / (1.0 - ADAM_B2 ** ADAM_STEP)
        d_ref[...] = -ADAM_LR * (m_hat / (jnp.sqrt(v_hat) + ADAM_EPS) + ADAM_WD * w_ref[...])
        nm_ref[...] = m_
        nv_ref[...] = v_

    spec = pl.BlockSpec((tr, C), lambda i: (i, 0))
    return pl.pallas_call(
        body, name=name, grid=(R // tr,),
        in_specs=[spec] * 4, out_specs=[spec] * 3,
        out_shape=[jax.ShapeDtypeStruct((R, C), F32)] * 3,
        compiler_params=_params(("parallel",)),
    )(w, g, m, v)


def _sum_devices(parts, name):
    _, R, C = parts.shape
    tr = _pick(R, (512, 256, 176, 128, 64, 32, 16, 8))

    def body(p_ref, o_ref):
        acc = p_ref[0].astype(F32)
        for d in range(1, N_DEV):
            acc = acc + p_ref[d].astype(F32)
        o_ref[...] = acc

    return pl.pallas_call(
        body, name=name, grid=(R // tr,),
        in_specs=[pl.BlockSpec((N_DEV, tr, C), lambda i: (0, i, 0))],
        out_specs=pl.BlockSpec((tr, C), lambda i: (i, 0)),
        out_shape=jax.ShapeDtypeStruct((R, C), F32),
        compiler_params=_params(("parallel",)),
    )(parts)


def _my_place():
    return lax.axis_index("x"), lax.axis_index("y"), lax.axis_index("c")


def _all_gather(blocks, name):
    n = len(blocks)

    def body(*refs):
        x_refs, out_refs = refs[:n], refs[n:2 * n]
        send_sems, recv_sems, local_sems = refs[2 * n:]
        x, y, c = _my_place()
        me, sibling = (x, y, c), (x, y, 1 - c)
        chips = [(1 - x, y), (x, 1 - y), (1 - x, 1 - y)]

        def copy(a, k, blk, to, own=False):
            slot = out_refs[a].at[4 * blk[0] + 2 * blk[1] + blk[2]]
            return pltpu.make_async_remote_copy(
                src_ref=x_refs[a] if own else slot, dst_ref=slot,
                send_sem=send_sems.at[7 * a + k], recv_sem=recv_sems.at[7 * a + k], device_id=to, device_id_type=MESH)

        mine = [pltpu.make_async_copy(x_refs[a], out_refs[a].at[4 * x + 2 * y + c], local_sems.at[a]) for a in range(n)]
        for cp in mine:
            cp.start()
        first = []
        for j, chip in enumerate(chips):
            first += [copy(a, 1 + j, me, (*chip, c), own=True) for a in range(n)]
        first += [copy(a, 0, me, sibling, own=True) for a in range(n)]
        for cp in first:
            cp.start()
        passed = []
        for j, chip in enumerate(chips):
            for a in range(n):
                copy(a, 1 + j, (*chip, c), me).wait_recv()
                passed.append(copy(a, 4 + j, (*chip, c), sibling))
                passed[-1].start()
        for a in range(n):
            copy(a, 0, sibling, me).wait_recv()
        for j, chip in enumerate(chips):
            for a in range(n):
                copy(a, 4 + j, (*chip, 1 - c), me).wait_recv()
        for cp in first + passed:
            cp.wait_send()
        for cp in mine:
            cp.wait()

    return pl.pallas_call(
        body, name=name,
        out_shape=[jax.ShapeDtypeStruct((N_DEV,) + b.shape, b.dtype) for b in blocks],
        in_specs=[pl.BlockSpec(memory_space=pl.ANY)] * n,
        out_specs=[pl.BlockSpec(memory_space=pl.ANY)] * n,
        scratch_shapes=[pltpu.SemaphoreType.DMA((7 * n,)), pltpu.SemaphoreType.DMA((7 * n,)), pltpu.SemaphoreType.DMA((n,))],
    )(*blocks)


def _all_to_all(pieces, name):
    n = len(pieces)

    def body(*refs):
        x_refs, out_refs = refs[:n], refs[n:2 * n]
        send_sems, recv_sems, local_sems = refs[2 * n:]
        x, y, c = _my_place()
        me = 4 * x + 2 * y + c
        mine = [pltpu.make_async_copy(x_refs[a].at[me], out_refs[a].at[me], local_sems.at[a]) for a in range(n)]
        for cp in mine:
            cp.start()
        copies = []
        for k in (2, 4, 6, 3, 5, 7, 1):
            px = 1 - x if k & 4 else x
            py = 1 - y if k & 2 else y
            pc = 1 - c if k & 1 else c
            peer = 4 * px + 2 * py + pc
            for a in range(n):
                copies.append(pltpu.make_async_remote_copy(
                    src_ref=x_refs[a].at[peer], dst_ref=out_refs[a].at[me],
                    send_sem=send_sems.at[7 * a + k - 1], recv_sem=recv_sems.at[7 * a + k - 1],
                    device_id=(px, py, pc), device_id_type=MESH))
        for cp in copies:
            cp.start()
        for cp in copies:
            cp.wait_recv()
        for cp in copies:
            cp.wait_send()
        for cp in mine:
            cp.wait()

    return pl.pallas_call(
        body, name=name,
        out_shape=[jax.ShapeDtypeStruct(p.shape, p.dtype) for p in pieces],
        in_specs=[pl.BlockSpec(memory_space=pl.ANY)] * n,
        out_specs=[pl.BlockSpec(memory_space=pl.ANY)] * n,
        scratch_shapes=[pltpu.SemaphoreType.DMA((7 * n,)), pltpu.SemaphoreType.DMA((7 * n,)), pltpu.SemaphoreType.DMA((n,))],
    )(*pieces)


def _peers():
    x, y, c = _my_place()
    out = []
    for k in (2, 4, 6, 3, 5, 7, 1):
        px = 1 - x if k & 4 else x
        py = 1 - y if k & 2 else y
        pc = 1 - c if k & 1 else c
        out.append((k, (px, py, pc), 4 * px + 2 * py + pc))
    return out


def _exchange_copies(x_refs, land_refs, send_sems, recv_sems, scatter):
    x, y, c = _my_place()
    me = 4 * x + 2 * y + c
    starts, arrivals = [], []
    for k, place, peer in _peers():
        for a, (x_ref, land_ref) in enumerate(zip(x_refs, land_refs)):
            sems = dict(send_sem=send_sems.at[7 * a + k - 1], recv_sem=recv_sems.at[7 * a + k - 1],
                        device_id=place, device_id_type=MESH)
            src = x_ref.at[peer] if scatter else x_ref
            starts.append(pltpu.make_async_remote_copy(src_ref=src, dst_ref=land_ref.at[me], **sems))
            arrivals.append(pltpu.make_async_remote_copy(src_ref=src, dst_ref=land_ref.at[peer], **sems))
    return starts, arrivals


def _exchange_start(arrays, scatter, name):
    n = len(arrays)
    hbm = pl.BlockSpec(memory_space=pltpu.HBM)
    sem = pl.BlockSpec(memory_space=pltpu.SEMAPHORE)
    lands = [lax.empty(a.shape if scatter else (N_DEV,) + a.shape, a.dtype) for a in arrays]

    def body(*refs):
        x_refs, land_refs = refs[:n], refs[n:2 * n]
        send_sems, recv_sems = refs[2 * n], refs[2 * n + 1]
        token = refs[-1]
        starts, _ = _exchange_copies(x_refs, land_refs, send_sems, recv_sems, scatter)
        for cp in starts:
            cp.start()
        token[...] = jnp.zeros_like(token)

    res = pl.pallas_call(
        body, name=name,
        out_shape=(pltpu.SemaphoreType.DMA((7 * n,)), pltpu.SemaphoreType.DMA((7 * n,)),
                   *[pltpu.HBM(a.shape, a.dtype) for a in arrays], *[pltpu.HBM(l.shape, l.dtype) for l in lands],
                   jax.ShapeDtypeStruct((8, 128), F32)),
        in_specs=[hbm] * (2 * n),
        out_specs=(sem, sem, *[hbm] * (2 * n), pl.BlockSpec(memory_space=pltpu.VMEM)),
        input_output_aliases={i: 2 + i for i in range(2 * n)},
        compiler_params=pltpu.CompilerParams(has_side_effects=pltpu.SideEffectType.DATAFLOW_SIDE_EFFECTING),
    )(*[pltpu.with_memory_space_constraint(a, pltpu.HBM) for a in arrays],
      *[pltpu.with_memory_space_constraint(l, pltpu.HBM) for l in lands])
    return res[0], res[1], list(res[2:2 + n]), list(res[2 + n:2 + 2 * n]), res[-1]


def _exchange_wait(handles, scatter, after, name):
    send_sems, recv_sems, arrays, lands, _ = handles
    n = len(arrays)
    hbm = pl.BlockSpec(memory_space=pltpu.HBM)
    sem = pl.BlockSpec(memory_space=pltpu.SEMAPHORE)

    def body(*refs):
        x_refs, land_refs = refs[:n], refs[n:2 * n]
        send_s, recv_s = refs[2 * n], refs[2 * n + 1]
        starts, arrivals = _exchange_copies(x_refs, land_refs, send_s, recv_s, scatter)
        for cp in arrivals:
            cp.wait_recv()
        for cp in starts:
            cp.wait_send()

    res = pl.pallas_call(
        body, name=name,
        out_shape=(*[pltpu.HBM(a.shape, a.dtype) for a in arrays], *[pltpu.HBM(l.shape, l.dtype) for l in lands]),
        in_specs=[hbm] * (2 * n) + [sem, sem, pl.BlockSpec(memory_space=pl.ANY)],
        out_specs=tuple([hbm] * (2 * n)),
        input_output_aliases={i: i for i in range(2 * n)},
        compiler_params=pltpu.CompilerParams(has_side_effects=pltpu.SideEffectType.DATAFLOW_SIDE_EFFECTING),
    )(*arrays, *lands, send_sems, recv_sems, after)
    me = 4 * lax.axis_index("x") + 2 * lax.axis_index("y") + lax.axis_index("c")
    out = []
    for src, got in zip(res[:n], res[n:]):
        zeros = (0,) * (got.ndim - 1)
        own = lax.dynamic_slice(src, (me,) + zeros, (1,) + src.shape[1:]) if scatter else src[None]
        out.append(lax.dynamic_update_slice(got, own, (me,) + zeros))
    return out


def _pad_lanes(v, at=0, width=128):
    return jnp.pad(v, ((0, 0), (at, width - at - v.shape[1])))


def _pack_weights(P):
    W = {}
    w = P["w_in"]
    W["wp"] = jnp.concatenate([w[:, :2048], w[:, 2440:2696], w[:, 2056:2440], w[:, 2696:2760], w[:, 2048:2056],
                               jnp.zeros((D_MODEL, N_IN_PACKED - N_IN), w.dtype)], axis=1).astype(BF16)
    W["conv_w"] = P["gdn_conv_w"].astype(F32)
    W["alog_p"] = _pad_lanes(P["gdn_a_log"], 64)
    W["dt_p"] = _pad_lanes(P["gdn_dt_bias"], 64)
    W["gnw"] = P["gdn_norm_w"]
    W["qnw"] = P["mla_q_norm_w"]
    W["kvnw"] = P["mla_kv_norm_w"]
    uq = P["mla_w_uq"].reshape(Q_LORA, HEADS, HEAD_DIM + ROPE)
    W["wuq"] = jnp.pad(uq, ((0, 0), (0, 0), (0, 256 - HEAD_DIM - ROPE))).reshape(Q_LORA, HEADS * 256).astype(BF16)
    ukv = P["mla_w_ukv"].reshape(KV_LORA, HEADS, 2, HEAD_DIM)
    W["wukv"] = ukv.transpose(0, 2, 1, 3).reshape(KV_LORA, 2 * HEADS * HEAD_DIM).astype(BF16)
    W["qn_w"] = P["qkn_q_nope"]
    W["qr_w"] = _pad_lanes(P["qkn_q_rope"])
    W["kn_w"] = P["qkn_k_nope"]
    W["kr_w"] = _pad_lanes(P["qkn_k_rope"])
    W["onw"] = P["mla_out_norm_w"]
    W["wout"] = P["w_out"].astype(BF16)
    return W


def _unpack_grads(G):
    g = G["wp"]
    uq = G["wuq"].reshape(Q_LORA, HEADS, 256)[:, :, :HEAD_DIM + ROPE].reshape(Q_LORA, HEADS * (HEAD_DIM + ROPE))
    ukv = G["wukv"].reshape(KV_LORA, 2, HEADS, HEAD_DIM).transpose(0, 2, 1, 3).reshape(KV_LORA, 2 * HEADS * HEAD_DIM)
    return {
        "w_in": jnp.concatenate([g[:, :2048], g[:, 2752:2760], g[:, 2304:2688], g[:, 2048:2304], g[:, 2688:2752]], axis=1),
        "gdn_conv_w": G["conv_w"], "gdn_a_log": G["alog_p"][:, 64:68], "gdn_dt_bias": G["dt_p"][:, 64:68],
        "gdn_norm_w": G["gnw"], "mla_q_norm_w": G["qnw"], "mla_w_uq": uq, "mla_kv_norm_w": G["kvnw"], "mla_w_ukv": ukv,
        "qkn_q_nope": G["qn_w"], "qkn_q_rope": G["qr_w"][:, :ROPE], "qkn_k_nope": G["kn_w"], "qkn_k_rope": G["kr_w"][:, :ROPE],
        "mla_out_norm_w": G["onw"], "w_out": G["wout"],
    }


def _rope_tables(positions):
    half = ROPE // 2
    inv_freq = ROPE_BASE ** (-jnp.arange(half, dtype=F32) / half)
    ang = positions.astype(F32)[:, None] * inv_freq
    cos, sin = jnp.cos(ang), jnp.sin(ang)
    zeros = jnp.zeros((positions.shape[0], 128 - ROPE), F32)
    return jnp.concatenate([cos, cos, zeros], axis=1), jnp.concatenate([-sin, sin, zeros], axis=1)


def _mod_fn(x, scale, shift):
    return (_modulate(x, scale, shift),)


def _ffn_down_loss(hid, wo4, x, gate_w, target, name):
    S = x.shape[0]
    tb = _pick(S, (512, 256, 128))
    n = S // tb

    def body(hid_ref, wo_ref, x_ref, g_ref, t_ref, dx_ref, df_ref, dg_ref, l_ref, acc_ref):
        i, k = pl.program_id(0), pl.program_id(1)

        @pl.when(k == 0)
        def _():
            acc_ref[...] = jnp.zeros_like(acc_ref)

        acc_ref[...] += _dot_raw(hid_ref[...], wo_ref[...], "nn")

        @pl.when((k == 0) & (i == 0))
        def _():
            dg_ref[...] = jnp.zeros_like(dg_ref)
            l_ref[...] = jnp.zeros_like(l_ref)

        @pl.when(k == HID_PIECES - 1)
        def _():
            f = acc_ref[...]
            diff = x_ref[...] + 0.5 * g_ref[...] * f - t_ref[...]
            dx = diff * (1.0 / D_MODEL)
            dx_ref[...] = dx
            df_ref[...] = (0.5 * g_ref[...] * dx).astype(df_ref.dtype)
            dg_ref[...] += jnp.sum(0.5 * f * dx, axis=0, keepdims=True)
            l_ref[...] += jnp.sum(diff * diff, axis=0, keepdims=True)

        @pl.when((k == HID_PIECES - 1) & (i == n - 1))
        def _():
            l_ref[...] = jnp.full(l_ref.shape, (0.5 / D_MODEL) * jnp.sum(l_ref[...]), F32)

    row = pl.BlockSpec((tb, D_MODEL), lambda i, k: (i, 0))
    par = pl.BlockSpec((1, D_MODEL), lambda i, k: (0, 0))
    return pl.pallas_call(
        body, name=name, grid=(n, HID_PIECES),
        in_specs=[pl.BlockSpec((None, tb, FFN_PIECE), lambda i, k: (k, i, 0)),
                  pl.BlockSpec((None, FFN_PIECE, D_MODEL), lambda i, k: (k, 0, 0)), row, par, row],
        out_specs=[row, row, par, par],
        out_shape=[jax.ShapeDtypeStruct((S, D_MODEL), F32), jax.ShapeDtypeStruct((S, D_MODEL), BF16),
                   jax.ShapeDtypeStruct((1, D_MODEL), F32), jax.ShapeDtypeStruct((1, D_MODEL), F32)],
        scratch_shapes=[pltpu.VMEM((tb, D_MODEL), F32)],
        compiler_params=_params(("arbitrary", "arbitrary")),
    )(hid, wo4, x, gate_w, target)


def _ffn_fwd(x, scale, shift, gate_w, w8, wo4, tag, target=None):
    S = x.shape[0]
    tm = _pick(S, (512, 256, 128))
    (h,) = _rowwise(_mod_fn, [(x, tm, D_MODEL, 0)], [scale, shift], [(tm, D_MODEL, BF16)], S // tm, tag + "_mod")
    by_gate, by_up, hid = _ffn_up(h, w8, tag + "_up")
    if target is not None:
        dx_out, df, d_gate_w, loss_row = _ffn_down_loss(hid, wo4, x, gate_w, target, tag + "_down")
        return (dx_out, loss_row), (h, by_gate, by_up, hid, None, df, d_gate_w)
    tb = _pick(S, MATMUL_ROWS)
    mn = pl.BlockSpec((tb, D_MODEL), lambda i, j, k: (i, j))
    f, x_out = _mmg(hid, wo4, "nn", name=tag + "_down", grid=(S // tb, 1, HID_PIECES),
                    a_spec=pl.BlockSpec((None, tb, FFN_PIECE), lambda i, j, k: (k, i, 0)),
                    b_spec=pl.BlockSpec((None, FFN_PIECE, D_MODEL), lambda i, j, k: (k, 0, j)),
                    out_spec=mn, out_shapes=[jax.ShapeDtypeStruct((S, D_MODEL), F32)] * 2, acc_shape=(tb, D_MODEL),
                    extras=[x, gate_w], extra_specs=[mn, pl.BlockSpec((1, D_MODEL), lambda i, j, k: (0, j))],
                    epi=lambda acc, x_, g_: (acc, x_ + 0.5 * g_ * acc))
    return x_out, (h, by_gate, by_up, hid, f, None, None)


def _ffn_bwd(d_out, x, scale, shift, gate_w, w8, wo4, saved, tag, grad_ready):
    h, gate, up, hid, f, df, d_gate_w = saved
    S = x.shape[0]
    tm = _pick(S, (512, 256, 128))
    tk = _pick(S, (512, 256, 128))
    n = S // tm
    if df is None:
        (df,), (d_gate_w,) = _rowwise_bwd(lambda f_, g_: (0.5 * g_ * f_,), [(f, tm, D_MODEL, 0)], [], [gate_w],
                                          [(d_out, tm, D_MODEL, 0)], n, tag + "_dres", row_dtypes=(BF16,))
    tb = _pick(S, MATMUL_ROWS)
    piece = pl.BlockSpec((None, tb, FFN_PIECE), lambda i, j, k: (j, i, 0))
    d_gate, d_up = _mmg(df, wo4, "nt", name=tag + "_ddown", grid=(S // tb, HID_PIECES, 1),
                        a_spec=pl.BlockSpec((tb, D_MODEL), lambda i, j, k: (i, 0)),
                        b_spec=pl.BlockSpec((None, FFN_PIECE, D_MODEL), lambda i, j, k: (j, 0, 0)),
                        out_spec=piece, out_shapes=[jax.ShapeDtypeStruct((HID_PIECES, S, FFN_PIECE), BF16)] * 2,
                        acc_shape=(tb, FFN_PIECE), extras=[gate, up], extra_specs=[piece, piece], epi=_swiglu_bwd)
    g_wo4 = _mmg(hid, df, "tn", name=tag + "_gwo", grid=(HID_PIECES, 1, S // tk),
                 a_spec=pl.BlockSpec((None, tk, FFN_PIECE), lambda i, j, k: (i, k, 0)),
                 b_spec=pl.BlockSpec((tk, D_MODEL), lambda i, j, k: (k, j)),
                 out_spec=pl.BlockSpec((None, FFN_PIECE, D_MODEL), lambda i, j, k: (i, 0, j)),
                 out_shapes=[jax.ShapeDtypeStruct((HID_PIECES, FFN_PIECE, D_MODEL), BF16)], acc_shape=(FFN_PIECE, D_MODEL))
    g_w8 = _ffn_gw8(h, d_gate, d_up, tag + "_gw8", after=grad_ready("wo4", g_wo4))
    dx, d_scale, d_shift = _ffn_dh(d_gate, d_up, w8, x, d_out, scale, shift, tag + "_dh", after=grad_ready("w8", g_w8))
    return dx, d_scale, d_shift, d_gate_w


def _dproj_dmod(d_proj, wp, x, d_out, scale, shift, name):
    S, K = d_proj.shape
    tm = _pick(S, (512, 256, 128))
    tk = _pick(K, (1408, 512, 256, 128))
    nk = K // tk

    def body(dp_ref, w_ref, x_ref, do_ref, sc_ref, sh_ref, dx_ref, dsc_ref, dsh_ref, acc_ref):
        i, k = pl.program_id(0), pl.program_id(1)

        @pl.when(k == 0)
        def _():
            acc_ref[...] = jnp.zeros_like(acc_ref)

        acc_ref[...] += _dot_raw(dp_ref[...], w_ref[...], "nt")

        @pl.when((k == 0) & (i == 0))
        def _():
            dsc_ref[...] = jnp.zeros_like(dsc_ref)
            dsh_ref[...] = jnp.zeros_like(dsh_ref)

        @pl.when(k == nk - 1)
        def _():
            _, vjp = jax.vjp(_modulate, x_ref[...], sc_ref[...], sh_ref[...])
            dx, dsc, dsh = vjp(acc_ref[...])
            dx_ref[...] = dx + do_ref[...]
            dsc_ref[...] += dsc
            dsh_ref[...] += dsh

    row = pl.BlockSpec((tm, D_MODEL), lambda i, k: (i, 0))
    par = pl.BlockSpec((1, D_MODEL), lambda i, k: (0, 0))
    return pl.pallas_call(
        body, name=name, grid=(S // tm, nk),
        in_specs=[pl.BlockSpec((tm, tk), lambda i, k: (i, k)), pl.BlockSpec((D_MODEL, tk), lambda i, k: (0, k)),
                  row, row, par, par],
        out_specs=[row, par, par],
        out_shape=[jax.ShapeDtypeStruct((S, D_MODEL), F32), jax.ShapeDtypeStruct((1, D_MODEL), F32),
                   jax.ShapeDtypeStruct((1, D_MODEL), F32)],
        scratch_shapes=[pltpu.VMEM((tm, D_MODEL), F32)],
        compiler_params=_params(("arbitrary", "arbitrary")),
    )(d_proj, wp, x, d_out, scale, shift)


def _mixer_fwd(x1, scale, shift, gate_w, cos_p, sin_p, W):
    S = x1.shape[0]
    tm = _pick(S, (512, 256, 128))
    tv = _pick(S, (256, 128))
    ta = _pick(S, (512, 256, 128))
    nc = S // CHUNK
    (h2,) = _rowwise(_mod_fn, [(x1, tm, D_MODEL, 0)], [scale, shift], [(tm, D_MODEL, BF16)], S // tm, "mix_mod")
    proj = _mm(h2, W["wp"], "nn", name="mix_proj")
    qkvc = _conv_fwd(proj, W["conv_w"], tv, "gdn_conv")
    kab = (proj, tv, 128, 21)
    q_a, k_a, v_a, gb = _rowwise(_gdn_pre_fn, [(qkvc, tv, 1536, 0), kab], [W["alog_p"], W["dt_p"]],
                                 [(tv, 512, F32)] * 3 + [(tv, 128, F32)], S // tv, "gdn_pre")
    ti = _pick(S, INTRA_ROWS)
    intra = _rowwise(_gdn_intra_fn, [(q_a, ti, 512, 0), (k_a, ti, 512, 0), (v_a, ti, 512, 0), (gb, ti, 128, 0)],
                     [], [(ti, 512, F32)] * 4 + [(ti, CHUNK, F32)] * 4 + [(ti // 8, 512, F32)], S // ti, "gdn_intra")
    u, wk, qd, kd, qks, gl = intra[0], intra[1], intra[2], intra[3], tuple(intra[4:8]), intra[8]
    o_a, s_prev = _gdn_scan_fwd(u, wk, qd, kd, qks, gl, "gdn_scan")
    mla_params = [W["qnw"], W["kvnw"], W["wuq"], W["wukv"], W["qn_w"], W["qr_w"], W["kn_w"], W["kr_w"]]
    def mla_pre_with_vt(*a):
        q_, k_, v_ = _mla_pre_fn(*a)
        return q_, k_, v_, jnp.transpose(v_)

    q_b, k_b, v_b, vt_b = _rowwise(mla_pre_with_vt,
                                   [(proj, tv, 256, 8), (proj, tv, 384, 6), kab, (cos_p, tv, 128, 0), (sin_p, tv, 128, 0)],
                                   mla_params, [(tv, 1024, BF16), (tv, 1024, BF16), (tv, 512, BF16), (512, tv, BF16, "across")],
                                   S // tv, "mla_pre")
    o_b, lse = _attn_fwd(q_b, k_b, vt_b, ta, "mla_attn")
    (mixed,) = _rowwise(_mix_post_fn, [(o_a, tv, 512, 0), (proj, tv, 512, 3), (o_b, tv, 512, 0)], [W["gnw"], W["onw"]],
                        [(tv, D_MODEL, BF16)], S // tv, "mix_post")
    y, x2 = _mm(mixed, W["wout"], "nn", name="mix_out", out_dtypes=(F32, F32), extras=[x1], extra_params=[gate_w],
                epi=lambda acc, x_, g_: (acc, x_ + g_ * acc))
    saved = (h2, proj, qkvc, q_a, k_a, v_a, gb, u, wk, qd, kd, qks, gl, s_prev, o_a, q_b, k_b, v_b, o_b, lse, mixed, y)
    return x2, saved


def _mixer_bwd(d_out, x1, scale, shift, gate_w, cos_p, sin_p, W, saved):
    (h2, proj, qkvc, q_a, k_a, v_a, gb, u, wk, qd, kd, qks, gl, s_prev, o_a, q_b, k_b, v_b, o_b, lse, mixed, y) = saved
    S = x1.shape[0]
    tm = _pick(S, (512, 256, 128))
    tv = _pick(S, (256, 128))
    ta = _pick(S, (512, 256, 128))
    nc = S // CHUNK
    G = {}
    (dy,), (G["g2"],) = _rowwise_bwd(lambda y_, g_: (g_ * y_,), [(y, tm, D_MODEL, 0)], [], [gate_w],
                                     [(d_out, tm, D_MODEL, 0)], S // tm, "mix_dres", row_dtypes=(BF16,))
    d_mixed = _mm(dy, W["wout"], "nt", name="mix_dout")
    G["wout"] = _mm(mixed, dy, "tn", name="mix_gwout")
    (do_a, dz, do_b), (G["gnw"], G["onw"]) = _rowwise_bwd(
        _mix_post_fn, [(o_a, tv, 512, 0), (proj, tv, 512, 3), (o_b, tv, 512, 0)], [], [W["gnw"], W["onw"]],
        [(d_mixed, tv, D_MODEL, 0)], S // tv, "mix_dpost")
    dq_b, stats = _attn_dq(q_b, k_b, v_b, o_b, lse, do_b, ta, "mla_dq")
    dk_b, dv_b = _attn_dkv(q_b, k_b, v_b, do_b, stats, ta, "mla_dkv")
    kab = (proj, tv, 128, 21)
    mla_params = [W["qnw"], W["kvnw"], W["wuq"], W["wukv"], W["qn_w"], W["qr_w"], W["kn_w"], W["kr_w"]]
    (d_ckv, d_cq, d_kab), mla_grads = _rowwise_bwd(
        _mla_pre_fn, [(proj, tv, 256, 8), (proj, tv, 384, 6), kab], [(cos_p, tv, 128, 0), (sin_p, tv, 128, 0)], mla_params,
        [(dq_b, tv, 1024, 0), (dk_b, tv, 1024, 0), (dv_b, tv, 512, 0)], S // tv, "mla_dpre")
    for key, g in zip(("qnw", "kvnw", "wuq", "wukv", "qn_w", "qr_w", "kn_w", "kr_w"), mla_grads):
        G[key] = g
    scan_grads = _gdn_scan_bwd(u, wk, qd, kd, qks, gl, s_prev, do_a, "gdn_dscan")
    ti = _pick(S, INTRA_ROWS)
    intra_douts = [(scan_grads[i], ti, 512, 0) for i in range(4)] + [(scan_grads[4 + i], ti, CHUNK, 0) for i in range(4)]
    intra_douts.append((scan_grads[8], ti // 8, 512, 0))
    (dq_a, dk_a, dv_a, d_gb), _ = _rowwise_bwd(
        _gdn_intra_fn, [(q_a, ti, 512, 0), (k_a, ti, 512, 0), (v_a, ti, 512, 0), (gb, ti, 128, 0)], [], [],
        intra_douts, S // ti, "gdn_dintra")
    (d_qkvc, d_kab), (G["alog_p"], G["dt_p"]) = _rowwise_bwd(
        _gdn_pre_fn, [(qkvc, tv, 1536, 0), kab], [], [W["alog_p"], W["dt_p"]],
        [(dq_a, tv, 512, 0), (dk_a, tv, 512, 0), (dv_a, tv, 512, 0), (d_gb, tv, 128, 0)], S // tv, "gdn_dpre",
        adds=[(1, d_kab)])
    d_qkv, g_conv = _conv_bwd(proj, d_qkvc, W["conv_w"], tv, "gdn_dconv")
    G["conv_w"] = g_conv[:4]
    d_proj = jnp.concatenate([d_qkv, dz, d_ckv, d_cq, d_kab], axis=1).astype(BF16)
    G["wp"] = _mm(h2, d_proj, "tn", name="mix_gwp")
    dx1, G["s2"], G["sh2"] = _dproj_dmod(d_proj, W["wp"], x1, d_out, scale, shift, "mix_dproj")
    return dx1, G


def _local_step(x, target, mod, cos_p, sin_p, W1, mixer_weights, ffn2_weights, ffn_grad_ready, mixer_grads_ready):
    sh1, s1, g1, sh2, s2, g2, sh3, s3, g3 = [mod[:, D_MODEL * i:D_MODEL * (i + 1)] for i in range(N_MOD)]
    x1, saved1 = _ffn_fwd(x, s1, sh1, g1, W1["f1_w8"], W1["f1_wo4"], "ffn1")
    W = mixer_weights(x1)
    x2, saved2 = _mixer_fwd(x1, s2, sh2, g2, cos_p, sin_p, W)
    W.update(ffn2_weights(x2))
    (dx3, loss_row), saved3 = _ffn_fwd(x2, s3, sh3, g3, W["f2_w8"], W["f2_wo4"], "ffn2", target=target)
    dx2, d_s3, d_sh3, d_g3 = _ffn_bwd(dx3, x2, s3, sh3, g3, W["f2_w8"], W["f2_wo4"], saved3, "ffn2", ffn_grad_ready("f2"))
    dx1, G = _mixer_bwd(dx2, x1, s2, sh2, g2, cos_p, sin_p, W, saved2)
    d_sh2, d_s2, d_g2 = G.pop("sh2"), G.pop("s2"), G.pop("g2")
    g1 = g1 + mixer_grads_ready(G)
    dx, d_s1, d_sh1, d_g1 = _ffn_bwd(dx1, x, s1, sh1, g1, W1["f1_w8"], W1["f1_wo4"], saved1, "ffn1", ffn_grad_ready("f1"))
    d_mod = jnp.concatenate([d_sh1, d_s1, d_g1, d_sh2, d_s2, d_g2, d_sh3, d_s3, d_g3], axis=1)
    return loss_row, dx, d_mod


WEIGHT_NAMES = ("w_ada", "b_ada", "ffn1_w_in", "ffn1_w_out", "w_in", "gdn_conv_w", "gdn_a_log", "gdn_dt_bias", "gdn_norm_w",
                "mla_q_norm_w", "mla_w_uq", "mla_kv_norm_w", "mla_w_ukv", "qkn_q_nope", "qkn_q_rope", "qkn_k_nope",
                "qkn_k_rope", "mla_out_norm_w", "w_out", "ffn2_w_in", "ffn2_w_out")
FFN_SHARDED = ("ffn1_w_in", "ffn1_w_out", "ffn2_w_in", "ffn2_w_out")
SHEETED = (("w_in", "col"), ("gdn_conv_w", "col"), ("mla_w_uq", "col"), ("mla_w_ukv", "col"), ("w_out", "row"))
MOD_ROWS = N_MOD * D_MODEL // 128
SMALL = {"gdn_a_log": (MOD_ROWS, 1, 64, 4), "gdn_dt_bias": (MOD_ROWS + 1, 1, 64, 4), "gdn_norm_w": (MOD_ROWS + 2, 1, 0, 128),
         "mla_q_norm_w": (MOD_ROWS + 3, 3, 0, 384), "mla_kv_norm_w": (MOD_ROWS + 6, 2, 0, 256),
         "qkn_q_nope": (MOD_ROWS + 8, 1, 0, 128), "qkn_q_rope": (MOD_ROWS + 9, 1, 0, 64), "qkn_k_nope": (MOD_ROWS + 10, 1, 0, 128),
         "qkn_k_rope": (MOD_ROWS + 11, 1, 0, 64), "mla_out_norm_w": (MOD_ROWS + 12, 1, 0, 128)}
LOSS_ROW = MOD_ROWS + 13
CONV_ROW, CONV_ROWS = 88, 4 * 1536 // 128
SHEET_ROWS = CONV_ROW + CONV_ROWS


def _to_sheet(flat, dtype, sublanes):
    n = flat.shape[-1]
    unit = sublanes * 128
    pad = (-n) % unit
    flat = jnp.pad(flat.astype(dtype), [(0, 0)] * (flat.ndim - 1) + [(0, pad)])
    return flat.reshape(flat.shape[:-1] + ((n + pad) // 128, 128))


def _small_sheet(b_like, small):
    sheet = jnp.zeros((SHEET_ROWS, 128), F32).at[:MOD_ROWS].set(b_like.reshape(MOD_ROWS, 128))
    for name, (row, rows, lane, n) in SMALL.items():
        v = small[name].reshape(1, n)
        if rows == 1:
            sheet = sheet.at[row, lane:lane + n].set(v[0])
        else:
            sheet = sheet.at[row:row + rows].set(v.reshape(rows, 128))
    return sheet


def _from_small_sheet(sheet):
    out = {"b_ada": sheet[:MOD_ROWS].reshape(1, N_MOD * D_MODEL)}
    for name, (row, rows, lane, n) in SMALL.items():
        out[name] = sheet[row, lane:lane + n].reshape(1, n) if rows == 1 else sheet[row:row + rows].reshape(1, n)
    return out


def kernel(x, c, positions, w_ada, b_ada, ffn1_w_in, ffn1_w_out, w_in, gdn_conv_w, gdn_a_log, gdn_dt_bias, gdn_norm_w, mla_q_norm_w, mla_w_uq, mla_kv_norm_w, mla_w_ukv, qkn_q_nope, qkn_q_rope, qkn_k_nope, qkn_k_rope, mla_out_norm_w, w_out, ffn2_w_in, ffn2_w_out, loss_target, m_w_ada, m_b_ada, m_ffn1_w_in, m_ffn1_w_out, m_w_in, m_gdn_conv_w, m_gdn_a_log, m_gdn_dt_bias, m_gdn_norm_w, m_mla_q_norm_w, m_mla_w_uq, m_mla_kv_norm_w, m_mla_w_ukv, m_qkn_q_nope, m_qkn_q_rope, m_qkn_k_nope, m_qkn_k_rope, m_mla_out_norm_w, m_w_out, m_ffn2_w_in, m_ffn2_w_out, v_w_ada, v_b_ada, v_ffn1_w_in, v_ffn1_w_out, v_w_in, v_gdn_conv_w, v_gdn_a_log, v_gdn_dt_bias, v_gdn_norm_w, v_mla_q_norm_w, v_mla_w_uq, v_mla_kv_norm_w, v_mla_w_ukv, v_qkn_q_nope, v_qkn_q_rope, v_qkn_k_nope, v_qkn_k_rope, v_mla_out_norm_w, v_w_out, v_ffn2_w_in, v_ffn2_w_out):
    args = locals()
    w = {n: args[n] for n in WEIGHT_NAMES}
    m = {n: args["m_" + n] for n in WEIGHT_NAMES}
    v = {n: args["v_" + n] for n in WEIGHT_NAMES}
    me = 4 * lax.axis_index("x") + 2 * lax.axis_index("y") + lax.axis_index("c")
    cols = N_MOD * D_MODEL // N_DEV
    shard = {n: w[n][0] for n in FFN_SHARDED + tuple(s[0] for s in SHEETED)}

    sc = c * _sigmoid(c)
    first = _to_sheet(jnp.concatenate([sc.reshape(-1), shard["gdn_conv_w"].reshape(-1)]), F32, 8)
    (first_all,) = _all_gather([first], "gather_c")
    sc_all = first_all[:, :D_MODEL // 128].reshape(N_DEV, D_MODEL)
    n_taps = shard["gdn_conv_w"].size
    conv_all = first_all.reshape(N_DEV, -1)[:, D_MODEL:D_MODEL + n_taps].reshape(N_DEV, 4, -1)
    b_mine = lax.dynamic_slice(b_ada, (0, me * cols), (1, cols))
    mod_cols = _mm(sc_all, w_ada[0], "nn", name="ada_mod", extra_params=[b_mine], epi=lambda acc, b_: (acc + b_,))
    (mod_all,) = _all_to_all([_to_sheet(mod_cols, F32, 8)], "scatter_mod")
    mod = mod_all.reshape(N_DEV, -1)[:, :cols].reshape(1, N_MOD * D_MODEL)

    f1_shards, mod = lax.optimization_barrier(([shard["ffn1_w_in"].astype(BF16), shard["ffn1_w_out"].astype(BF16)], mod))
    f1_w8, f1_out = _all_gather(f1_shards, "gather_w1")
    travel = [s for s in SHEETED if s[0] != "gdn_conv_w"]
    tied = lax.optimization_barrier(([shard[n].astype(BF16) for n, _ in travel], f1_w8))
    f1_w8 = tied[1]
    mixer_w = _exchange_start(tied[0], False, "gather_wm_start")
    ffn2_w = _exchange_start([shard["ffn2_w_in"].astype(BF16) + mixer_w[4][0:1, 0:1].astype(BF16),
                              shard["ffn2_w_out"].astype(BF16)], False, "gather_w2_start")
    mod = mod + ffn2_w[4][0:1, 0:1]
    W1 = dict(f1_w8=f1_w8, f1_wo4=f1_out.reshape(HID_PIECES, FFN_PIECE, D_MODEL))

    def mixer_weights(after):
        got = _exchange_wait(mixer_w, False, after, "gather_wm_wait")
        P = {n: jnp.concatenate(list(g), axis=1) if kind == "col" else g.reshape(-1, g.shape[-1])
             for (n, kind), g in zip(travel, got)}
        P["gdn_conv_w"] = jnp.concatenate(list(conv_all), axis=1)
        for n in SMALL:
            P[n] = w[n]
        return _pack_weights(P)

    def ffn2_weights(after):
        f2_w8, f2_out = _exchange_wait(ffn2_w, False, after, "gather_w2_wait")
        return dict(f2_w8=f2_w8, f2_wo4=f2_out.reshape(HID_PIECES, FFN_PIECE, D_MODEL))

    pending, small_grads = {}, {}

    def ffn_grad_ready(tag):
        def ready(which, g):
            pieces = g if which == "w8" else g.reshape((N_DEV,) + shard["ffn1_w_out"].shape)
            pending[tag + which] = _exchange_start([pieces], True, "scatter_%s_%s_start" % (tag, which))
            return pending[tag + which][4]
        return ready

    def mixer_grads_ready(G):
        g_full = _unpack_grads(G)
        small_grads.update({n: g_full[n] for n in SMALL})
        small_grads["gdn_conv_w"] = g_full["gdn_conv_w"]
        pieces = []
        for n, kind in travel:
            r, cc = shard[n].shape
            g = g_full[n].astype(BF16)
            pieces.append(jnp.stack([g[:, cc * p:cc * (p + 1)] for p in range(N_DEV)]) if kind == "col"
                          else g.reshape(N_DEV, r, cc))
        pending["mixer"] = _exchange_start(pieces, True, "scatter_mx_start")
        return pending["mixer"][4][0:1, 0:1]

    cos_p, sin_p = _rope_tables(positions[0])
    loss_row, dx, d_mod = _local_step(x[0], loss_target[0], mod, cos_p, sin_p, W1, mixer_weights, ffn2_weights,
                                      ffn_grad_ready, mixer_grads_ready)

    sheet = _small_sheet(d_mod, small_grads).at[LOSS_ROW].set(loss_row[0, :128])
    sheet = sheet.at[CONV_ROW:CONV_ROW + CONV_ROWS].set(small_grads["gdn_conv_w"].reshape(CONV_ROWS, 128))
    (sheets,) = _all_gather([sheet], "gather_small")
    summed = _sum_devices(sheets, "sum_small")
    d_mod_all = sheets[:, :MOD_ROWS].reshape(N_DEV, N_MOD * D_MODEL)
    d_mod_mine = lax.dynamic_slice(d_mod_all, (0, me * cols), (N_DEV, cols))
    grads = _from_small_sheet(summed)
    grads["w_ada"] = _mm(sc_all, d_mod_mine, "tn", name="ada_gw", hi=True)
    conv_taps = shard["gdn_conv_w"].shape[1]
    grads["gdn_conv_w"] = lax.dynamic_slice(summed[CONV_ROW:CONV_ROW + CONV_ROWS].reshape(4, -1), (0, me * conv_taps),
                                            (4, conv_taps))
    loss = summed[LOSS_ROW, 0]

    for n, key in zip(FFN_SHARDED, ("f1w8", "f1wo4", "f2w8", "f2wo4")):
        (parts,) = _exchange_wait(pending[key], True, summed, "scatter_%s_wait" % key)
        grads[n] = _sum_devices(parts, "sum_" + n)
    for (n, _), parts in zip(travel, _exchange_wait(pending["mixer"], True, summed, "scatter_mx_wait")):
        grads[n] = _sum_devices(parts, "sum_" + n)

    delta, new_m, new_v = {}, {}, {}
    for n in ("w_ada",) + FFN_SHARDED + tuple(s[0] for s in SHEETED):
        delta[n], new_m[n], new_v[n] = _adamw(w[n][0], grads[n], m[n][0], v[n][0], "adamw_" + n)
    small_in = [_small_sheet(t["b_ada"], t) for t in (w, grads, m, v)]
    for res, out in zip(_adamw(*small_in, "adamw_small"), (delta, new_m, new_v)):
        out.update(_from_small_sheet(res))

    def shaped(d):
        return [d[n].reshape(w[n].shape) for n in WEIGHT_NAMES]

    return (loss, dx[None], *shaped(grads), *shaped(delta), *shaped(new_m), *shaped(new_v))
```

```python
import functools

import jax
import jax.numpy as jnp
import numpy as np
from jax import lax
from jax.experimental import pallas as pl
from jax.experimental.pallas import tpu as pltpu

F32 = jnp.float32
BF16 = jnp.bfloat16

D_MODEL = 1024
D_FF = 2816
N_MOD = 9
HEADS = 4
HEAD_DIM = 128
CHUNK = 64
EPS = 1e-6
ROPE = 64
Q_LORA = 384
KV_LORA = 256
N_IN = 2760
N_IN_PACKED = 2816
ROPE_BASE = 10000.0
N_DEV = 8

ADAM_LR = 0.001
ADAM_B1 = 0.9
ADAM_B2 = 0.999
ADAM_EPS = 1e-08
ADAM_WD = 0.01
ADAM_STEP = 10

VMEM_LIMIT_BYTES = 56 * 1024 * 1024
MATMUL_ROWS = (1024, 512, 256, 128)
MESH = pl.DeviceIdType.MESH


def _params(sem=None):
    return pltpu.CompilerParams(dimension_semantics=sem, vmem_limit_bytes=VMEM_LIMIT_BYTES)


def _pick(dim, prefs):
    for p in prefs:
        if dim % p == 0:
            return p
    return dim


_DIMS = {"nn": (((1,), (0,)), ((), ())), "nt": (((1,), (1,)), ((), ())), "tn": (((0,), (0,)), ((), ()))}


def _dot_raw(a, b, mode):
    return lax.dot_general(a.astype(BF16), b.astype(BF16), _DIMS[mode], preferred_element_type=F32)


def _dot_hi(a, b, mode="nn"):
    return lax.dot_general(a, b, _DIMS[mode], precision=lax.Precision.HIGHEST, preferred_element_type=F32)


@functools.partial(jax.custom_vjp, nondiff_argnums=(2,))
def _bdot(a, b, mode):
    return _dot_raw(a, b, mode)


def _bdot_fwd(a, b, mode):
    return _dot_raw(a, b, mode), (a, b)


def _bdot_bwd(mode, res, g):
    a, b = res
    if mode == "nn":
        return _dot_raw(g, b, "nt"), _dot_raw(a, g, "tn")
    if mode == "nt":
        return _dot_raw(g, b, "nn"), _dot_raw(g, a, "tn")
    return _dot_raw(b, g, "nt"), _dot_raw(a, g, "nn")


_bdot.defvjp(_bdot_fwd, _bdot_bwd)


def _mm(a, b, mode, *, name, out_dtypes=(F32,), epi=None, extras=(), extra_params=(), hi=False,
        tm=None, tn=None, tk=None):
    if mode == "nn":
        (M, K), (_, N) = a.shape, b.shape
    elif mode == "nt":
        (M, K), (N, _) = a.shape, b.shape
    else:
        (K, M), (_, N) = a.shape, b.shape
    tm = tm or _pick(M, (512, 1408, 256, 128) if mode == "tn" else MATMUL_ROWS + (384, 352))
    tn = tn or _pick(N, (1024, 1408, 768, 512, 384, 256, 128))
    tk = tk or _pick(K, (1024, 1408, 512, 384, 256, 128))
    a_spec = {"nn": pl.BlockSpec((tm, tk), lambda i, j, k: (i, k)), "nt": pl.BlockSpec((tm, tk), lambda i, j, k: (i, k)),
              "tn": pl.BlockSpec((tk, tm), lambda i, j, k: (k, i))}[mode]
    b_spec = {"nn": pl.BlockSpec((tk, tn), lambda i, j, k: (k, j)), "nt": pl.BlockSpec((tn, tk), lambda i, j, k: (j, k)),
              "tn": pl.BlockSpec((tk, tn), lambda i, j, k: (k, j))}[mode]
    mn_spec = pl.BlockSpec((tm, tn), lambda i, j, k: (i, j))
    return _mmg(a, b, mode, name=name, grid=(M // tm, N // tn, K // tk), a_spec=a_spec, b_spec=b_spec, out_spec=mn_spec,
                out_shapes=[jax.ShapeDtypeStruct((M, N), dt) for dt in out_dtypes], acc_shape=(tm, tn), epi=epi,
                extras=list(extras) + list(extra_params),
                extra_specs=[mn_spec] * len(extras) + [pl.BlockSpec((1, tn), lambda i, j, k: (0, j))] * len(extra_params),
                hi=hi)


def _mmg(a, b, mode, *, name, grid, a_spec, b_spec, out_spec, out_shapes, acc_shape, epi=None, extras=(),
         extra_specs=(), hi=False):
    nk = grid[2]
    n_e, n_o = len(extras), len(out_shapes)

    def body(*refs):
        a_ref, b_ref = refs[:2]
        e_refs = refs[2:2 + n_e]
        o_refs = refs[2 + n_e:2 + n_e + n_o]
        acc_ref = refs[-1]
        k = pl.program_id(2)

        @pl.when(k == 0)
        def _():
            acc_ref[...] = jnp.zeros_like(acc_ref)

        if hi:
            acc_ref[...] += _dot_hi(a_ref[...].astype(F32), b_ref[...].astype(F32), mode)
        else:
            acc_ref[...] += _dot_raw(a_ref[...], b_ref[...], mode)

        @pl.when(k == nk - 1)
        def _():
            acc = acc_ref[...]
            outs = (acc,) if epi is None else epi(acc, *[e[...].astype(F32) for e in e_refs])
            for o_ref, o in zip(o_refs, outs):
                o_ref[...] = o.astype(o_ref.dtype)

    outs = pl.pallas_call(
        body, name=name, grid=grid,
        in_specs=[a_spec, b_spec] + list(extra_specs),
        out_specs=[out_spec] * n_o,
        out_shape=list(out_shapes),
        scratch_shapes=[pltpu.VMEM(acc_shape, F32)],
        compiler_params=_params(("parallel", "parallel", "arbitrary")),
    )(a, b, *extras)
    return outs if n_o > 1 else outs[0]


def _row_spec(th, cw, ci):
    return pl.BlockSpec((th, cw), lambda i: (i, ci))


def _full_spec(shape):
    return pl.BlockSpec(shape, lambda i: (0,) * len(shape))


def _rowwise(fn, rows, params, outs, n_steps, name):
    n_r, n_p, n_o = len(rows), len(params), len(outs)

    def body(*refs):
        vals = [r[...].astype(F32) for r in refs[:n_r + n_p]]
        res = fn(*vals)
        for o_ref, o in zip(refs[n_r + n_p:], res):
            o_ref[...] = o.astype(o_ref.dtype)

    across = [len(o) == 4 for o in outs]
    res = pl.pallas_call(
        body, name=name, grid=(n_steps,),
        in_specs=[_row_spec(th, cw, ci) for (_, th, cw, ci) in rows] + [_full_spec(p.shape) for p in params],
        out_specs=[pl.BlockSpec((o[0], o[1]), lambda i: (0, i)) if ac else _row_spec(o[0], o[1], 0)
                   for o, ac in zip(outs, across)],
        out_shape=[jax.ShapeDtypeStruct((o[0], n_steps * o[1]) if ac else (n_steps * o[0], o[1]), o[2])
                   for o, ac in zip(outs, across)],
        compiler_params=_params(("parallel",)),
    )(*[r[0] for r in rows], *params)
    return res


def _rowwise_bwd(fn, rows, aux, params, douts, n_steps, name, row_dtypes=None, adds=()):
    n_r, n_a, n_p, n_d, n_add = len(rows), len(aux), len(params), len(douts), len(adds)
    row_dtypes = row_dtypes or (F32,) * n_r

    def body(*refs):
        it = iter(refs)
        r_vals = [next(it)[...].astype(F32) for _ in range(n_r)]
        a_vals = [next(it)[...].astype(F32) for _ in range(n_a)]
        p_vals = [next(it)[...].astype(F32) for _ in range(n_p)]
        d_vals = [next(it)[...].astype(F32) for _ in range(n_d)]
        add_vals = [next(it)[...].astype(F32) for _ in range(n_add)]
        dr_refs = [next(it) for _ in range(n_r)]
        dp_refs = [next(it) for _ in range(n_p)]

        def f(*rp):
            return tuple(fn(*rp[:n_r], *a_vals, *rp[n_r:]))

        _, vjp = jax.vjp(f, *r_vals, *p_vals)
        grads = list(vjp(tuple(d_vals)))
        for (ri, _), av in zip(adds, add_vals):
            grads[ri] = grads[ri] + av
        for dr_ref, g in zip(dr_refs, grads[:n_r]):
            dr_ref[...] = g.astype(dr_ref.dtype)

        @pl.when(pl.program_id(0) == 0)
        def _():
            for dp_ref in dp_refs:
                dp_ref[...] = jnp.zeros_like(dp_ref)

        for dp_ref, g in zip(dp_refs, grads[n_r:]):
            dp_ref[...] += g

    all_rows = list(rows) + list(aux) + list(douts) + [(arr,) + tuple(rows[ri][1:3]) + (0,) for ri, arr in adds]
    in_specs = ([_row_spec(th, cw, ci) for (_, th, cw, ci) in list(rows) + list(aux)]
                + [_full_spec(p.shape) for p in params]
                + [_row_spec(th, cw, ci) for (_, th, cw, ci) in all_rows[n_r + n_a:]])
    res = pl.pallas_call(
        body, name=name, grid=(n_steps,),
        in_specs=in_specs,
        out_specs=[_row_spec(th, cw, 0) for (_, th, cw, _) in rows] + [_full_spec(p.shape) for p in params],
        out_shape=[jax.ShapeDtypeStruct((n_steps * th, cw), dt) for (_, th, cw, _), dt in zip(rows, row_dtypes)]
        + [jax.ShapeDtypeStruct(p.shape, F32) for p in params],
        compiler_params=_params(("arbitrary",)),
    )(*[r[0] for r in list(rows) + list(aux)], *params, *[r[0] for r in all_rows[n_r + n_a:]])
    return res[:n_r], res[n_r:]


def _sigmoid(x):
    return lax.logistic(x)


def _silu(x):
    return x * _sigmoid(x)


def _rms(x, w=None, n=None):
    n = n or x.shape[-1]
    y = x * lax.rsqrt(jnp.sum(x * x, axis=-1, keepdims=True) * (1.0 / n) + EPS)
    return y if w is None else y * w


def _modulate(x, scale, shift):
    return _rms(x) * (1.0 + scale) + shift


def _softplus(x):
    return jnp.maximum(x, 0.0) + jnp.log1p(jnp.exp(-jnp.abs(x)))


@jax.custom_vjp
def _rot_half64(x):
    lane = lax.broadcasted_iota(jnp.int32, x.shape, 1)
    up = pltpu.roll(x, 96, 1)
    down = pltpu.roll(x, 32, 1)
    return jnp.where(lane < 32, up, jnp.where(lane < 64, down, 0.0))


_rot_half64.defvjp(lambda x: (_rot_half64(x), None), lambda _, g: (_rot_half64(g),))


def _rope128(x, cos_p, sin_p):
    return x * cos_p + _rot_half64(x) * sin_p


def _gdn_pre_fn(qkvc, kab, alog_p, dt_p):
    a = _silu(qkvc)
    qs, ks = [], []
    for h in range(HEADS):
        qh = a[:, HEAD_DIM * h:HEAD_DIM * (h + 1)]
        kh = a[:, 512 + HEAD_DIM * h:512 + HEAD_DIM * (h + 1)]
        qs.append(qh * lax.rsqrt(jnp.sum(qh * qh, axis=-1, keepdims=True) + EPS) * (HEAD_DIM ** -0.5))
        ks.append(kh * lax.rsqrt(jnp.sum(kh * kh, axis=-1, keepdims=True) + EPS))
    lane = lax.broadcasted_iota(jnp.int32, kab.shape, 1)
    g_full = -jnp.exp(alog_p) * _softplus(kab + dt_p)
    b_full = _sigmoid(kab)
    gb = jnp.where((lane >= 64) & (lane < 68), g_full, jnp.where((lane >= 68) & (lane < 72), b_full, 0.0))
    return jnp.concatenate(qs, axis=1), jnp.concatenate(ks, axis=1), a[:, 1024:1536], gb


INTRA_ROWS = (256, 128, 64)

_BNN = (((2,), (1,)), ((0,), (0,)))
_BNT = (((2,), (2,)), ((0,), (0,)))


def _split_bf16(a):
    hi = a.astype(BF16)
    return hi, (a - hi.astype(F32)).astype(BF16)


def _dot3_raw(a, b, dims):
    a_hi, a_lo = _split_bf16(a)
    b_hi, b_lo = _split_bf16(b)
    dot = lambda x_, y_: lax.dot_general(x_, y_, dims, preferred_element_type=F32)
    return dot(a_hi, b_hi) + (dot(a_hi, b_lo) + dot(a_lo, b_hi))


@functools.partial(jax.custom_vjp, nondiff_argnums=(2, 3))
def _dot3(a, b, nt, exact_bwd=True):
    return _dot3_raw(a, b, _BNT if nt else _BNN)


def _dot3_fwd(a, b, nt, exact_bwd):
    return _dot3_raw(a, b, _BNT if nt else _BNN), (a, b)


def _dot3_bwd(nt, exact_bwd, res, g):
    a, b = res
    if exact_bwd:
        dot = _dot3_raw
    else:
        dot = lambda x_, y_, d_: lax.dot_general(x_.astype(BF16), y_.astype(BF16), d_, preferred_element_type=F32)
    if nt:
        return dot(g, b, _BNN), dot(jnp.swapaxes(g, 1, 2), a, _BNN)
    return dot(g, b, _BNT), dot(jnp.swapaxes(a, 1, 2), g, _BNN)


_dot3.defvjp(_dot3_fwd, _dot3_bwd)


@functools.partial(jax.custom_vjp, nondiff_argnums=(2,))
def _bdot_b(a, b, nt):
    return lax.dot_general(a.astype(BF16), b.astype(BF16), _BNT if nt else _BNN, preferred_element_type=F32)


def _bdot_b_fwd(a, b, nt):
    return _bdot_b(a, b, nt), (a, b)


def _bdot_b_bwd(nt, res, g):
    a, b = res
    dot = lambda x_, y_, d_: lax.dot_general(x_.astype(BF16), y_.astype(BF16), d_, preferred_element_type=F32)
    if nt:
        return dot(g, b, _BNN), dot(jnp.swapaxes(g, 1, 2), a, _BNN)
    return dot(g, b, _BNT), dot(jnp.swapaxes(a, 1, 2), g, _BNN)


_bdot_b.defvjp(_bdot_b_fwd, _bdot_b_bwd)


def _intra_batched(q, k, v, g_col, b_col):
    c = CHUNK
    nb = q.shape[0]
    row = lax.broadcasted_iota(jnp.int32, (1, c, c), 1)
    col = lax.broadcasted_iota(jnp.int32, (1, c, c), 2)
    incl, strict, eye = row >= col, row > col, row == col
    tri = jnp.broadcast_to(jnp.where(incl, 1.0, 0.0).astype(F32), (nb, c, c))
    ident = jnp.where(eye, 1.0, 0.0).astype(F32)
    g_wide = _dot3(tri, jnp.broadcast_to(g_col, (nb, c, HEAD_DIM)), False)
    g_i = g_wide[:, :, :c]
    g_j = jnp.sum(jnp.where(eye, g_i, 0.0), axis=1, keepdims=True)
    decay = jnp.where(incl, jnp.exp(jnp.where(incl, g_i - g_j, 0.0)), 0.0)
    kk = _bdot_b(k, k, True)
    a_mat = jnp.where(strict, b_col * kk * decay, 0.0)
    x_pow = -a_mat
    inv = ident + x_pow
    for _ in range(5):
        x_pow = _dot3(x_pow, x_pow, False, False)
        inv = inv + _dot3(inv, x_pow, False, False)
    e_wide = jnp.exp(g_wide)
    u = _dot3(inv, v * b_col, False)
    wk = _dot3(inv, k * b_col * e_wide, False)
    qk = _bdot_b(q, k, True) * decay
    last = lax.broadcasted_iota(jnp.int32, (1, c, HEAD_DIM), 1) == c - 1
    g_last = jnp.sum(jnp.where(last, g_wide, 0.0), axis=1, keepdims=True)
    qd = q * e_wide
    kd = k * jnp.exp(g_last - g_wide)
    gl = jnp.broadcast_to(jnp.exp(g_last), (nb, 8, HEAD_DIM))
    return u, wk, qd, kd, qk, gl


def _gdn_intra_fn(q, k, v, gb):
    t = q.shape[0]
    nch = t // CHUNK
    lane = lax.broadcasted_iota(jnp.int32, gb.shape, 1)

    def heads_first(x_):
        return jnp.concatenate([x_[:, HEAD_DIM * h:HEAD_DIM * (h + 1)].reshape(nch, CHUNK, HEAD_DIM) for h in range(HEADS)],
                               axis=0)

    def column(first_lane):
        return jnp.concatenate([jnp.sum(jnp.where(lane == first_lane + h, gb, 0.0), axis=1, keepdims=True)
                                .reshape(nch, CHUNK, 1) for h in range(HEADS)], axis=0)

    u, wk, qd, kd, qk, gl = _intra_batched(heads_first(q), heads_first(k), heads_first(v), column(64), column(68))

    def rows_first(x_):
        r, w_ = x_.shape[1], x_.shape[2]
        return jnp.concatenate([x_[nch * h:nch * (h + 1)].reshape(nch * r, w_) for h in range(HEADS)], axis=1)

    qks = [qk[nch * h:nch * (h + 1)].reshape(t, CHUNK) for h in range(HEADS)]
    return (rows_first(u), rows_first(wk), rows_first(qd), rows_first(kd), *qks, rows_first(gl))


def _scan_step(s0, u, wk, qd, kd, qk, gl):
    v_new = u - _bdot_b(wk, s0, False)
    o = _bdot_b(qd, s0, False) + _bdot_b(qk, v_new, False)
    s1 = s0 * gl[:, 0:1, :] + _bdot_b(jnp.swapaxes(kd, 1, 2), v_new, False)
    return o, s1


def _mix_post_fn(o_a, z, o_b, gnw, onw):
    parts = [_rms(o_a[:, HEAD_DIM * h:HEAD_DIM * (h + 1)], gnw) * _silu(z[:, HEAD_DIM * h:HEAD_DIM * (h + 1)])
             for h in range(HEADS)]
    parts += [_rms(o_b[:, HEAD_DIM * h:HEAD_DIM * (h + 1)], onw) for h in range(HEADS)]
    return (jnp.concatenate(parts, axis=1),)


def _mla_pre_fn(ckv, cq, kab, cos_p, sin_p, qnw, kvnw, wuq, wukv, qn_w, qr_w, kn_w, kr_w):
    scale = (HEAD_DIM + ROPE) ** -0.5
    qf = _bdot(_rms(cq, qnw), wuq, "nn")
    kvf = _bdot(_rms(ckv, kvnw), wukv, "nn")
    lane = lax.broadcasted_iota(jnp.int32, kab.shape, 1)
    kr = _rope128(_rms(jnp.where(lane < ROPE, kab, 0.0), kr_w, n=ROPE), cos_p, sin_p)
    qs, ks = [], []
    for h in range(HEADS):
        qn = _rms(qf[:, 256 * h:256 * h + 128], qn_w) * scale
        qr = _rope128(_rms(qf[:, 256 * h + 128:256 * h + 256], qr_w, n=ROPE), cos_p, sin_p) * scale
        qs += [qn, qr]
        ks += [_rms(kvf[:, 128 * h:128 * (h + 1)], kn_w), kr]
    return jnp.concatenate(qs, axis=1), jnp.concatenate(ks, axis=1), kvf[:, 512:]


def _conv_fwd(proj, conv_w, tm, name):
    S = proj.shape[0]
    C = 1536
    nb = tm // 8

    def body(x_ref, prev_ref, w_ref, o_ref, ext_ref):
        i = pl.program_id(0)
        ext_ref[0:8, :] = jnp.where(i > 0, prev_ref[...], 0.0)
        ext_ref[8:, :] = x_ref[...]
        acc = jnp.zeros((tm, C), F32)
        for k in range(4):
            acc = acc + w_ref[k:k + 1, :] * ext_ref[pl.ds(5 + k, tm), :]
        o_ref[...] = acc

    return pl.pallas_call(
        body, name=name, grid=(S // tm,),
        in_specs=[pl.BlockSpec((tm, C), lambda i: (i, 0)),
                  pl.BlockSpec((8, C), lambda i: (jnp.maximum(i * nb - 1, 0), 0)),
                  pl.BlockSpec((4, C), lambda i: (0, 0))],
        out_specs=pl.BlockSpec((tm, C), lambda i: (i, 0)),
        out_shape=jax.ShapeDtypeStruct((S, C), F32),
        scratch_shapes=[pltpu.VMEM((tm + 8, C), F32)],
        compiler_params=_params(("arbitrary",)),
    )(proj, proj, conv_w)


def _conv_bwd(proj, dout, conv_w, tm, name):
    S = proj.shape[0]
    C = 1536
    nb = tm // 8
    n_steps = S // tm

    def body(x_ref, prev_ref, d_ref, next_ref, w_ref, dx_ref, dw_ref, xext_ref, dext_ref):
        i = pl.program_id(0)
        xext_ref[0:8, :] = jnp.where(i > 0, prev_ref[...], 0.0)
        xext_ref[8:, :] = x_ref[...]
        dext_ref[0:tm, :] = d_ref[...]
        dext_ref[tm:, :] = jnp.where(i < n_steps - 1, next_ref[...], 0.0)
        d = d_ref[...]
        acc = jnp.zeros((tm, C), F32)
        dws = []
        for k in range(4):
            acc = acc + w_ref[k:k + 1, :] * dext_ref[pl.ds(3 - k, tm), :]
            dws.append(jnp.sum(d * xext_ref[pl.ds(5 + k, tm), :], axis=0, keepdims=True))
        dx_ref[...] = acc

        @pl.when(i == 0)
        def _():
            dw_ref[...] = jnp.zeros_like(dw_ref)

        dw_ref[...] += jnp.concatenate(dws + [jnp.zeros((4, C), F32)], axis=0)

    return pl.pallas_call(
        body, name=name, grid=(n_steps,),
        in_specs=[pl.BlockSpec((tm, C), lambda i: (i, 0)),
                  pl.BlockSpec((8, C), lambda i: (jnp.maximum(i * nb - 1, 0), 0)),
                  pl.BlockSpec((tm, C), lambda i: (i, 0)),
                  pl.BlockSpec((8, C), lambda i: (jnp.minimum((i + 1) * nb, S // 8 - 1), 0)),
                  pl.BlockSpec((4, C), lambda i: (0, 0))],
        out_specs=[pl.BlockSpec((tm, C), lambda i: (i, 0)), pl.BlockSpec((8, C), lambda i: (0, 0))],
        out_shape=[jax.ShapeDtypeStruct((S, C), F32), jax.ShapeDtypeStruct((8, C), F32)],
        scratch_shapes=[pltpu.VMEM((tm + 8, C), F32), pltpu.VMEM((tm + 8, C), F32)],
        compiler_params=_params(("arbitrary",)),
    )(proj, proj, dout, dout, conv_w)


SCAN_CHUNKS = (4, 2, 1)


def _gdn_scan_fwd(u, wk, qd, kd, qks, gl, name):
    S = u.shape[0]
    nc = S // CHUNK
    cs = _pick(nc, SCAN_CHUNKS)
    W = HEADS * HEAD_DIM

    def body(u_ref, wk_ref, qd_ref, kd_ref, qk0, qk1, qk2, qk3, gl_ref, o_ref, sp_ref, s_ref):
        @pl.when(pl.program_id(0) == 0)
        def _():
            s_ref[...] = jnp.zeros_like(s_ref)

        state = s_ref[...]
        for c in range(cs):
            rows, gl_rows = slice(CHUNK * c, CHUNK * (c + 1)), slice(8 * c, 8 * (c + 1))
            sp_ref[c] = state
            o, state = _scan_step(state, _heads(u_ref, HEAD_DIM, rows), _heads(wk_ref, HEAD_DIM, rows),
                                  _heads(qd_ref, HEAD_DIM, rows), _heads(kd_ref, HEAD_DIM, rows),
                                  jnp.stack([r[rows, :] for r in (qk0, qk1, qk2, qk3)]), _heads(gl_ref, HEAD_DIM, gl_rows))
            for h in range(HEADS):
                o_ref[rows, HEAD_DIM * h:HEAD_DIM * (h + 1)] = o[h]
        s_ref[...] = state

    row = pl.BlockSpec((cs * CHUNK, W), lambda n: (n, 0))
    qk_spec = pl.BlockSpec((cs * CHUNK, CHUNK), lambda n: (n, 0))
    return pl.pallas_call(
        body, name=name, grid=(nc // cs,),
        in_specs=[row, row, row, row, qk_spec, qk_spec, qk_spec, qk_spec, pl.BlockSpec((cs * 8, W), lambda n: (n, 0))],
        out_specs=[row, pl.BlockSpec((cs, HEADS, HEAD_DIM, HEAD_DIM), lambda n: (n, 0, 0, 0))],
        out_shape=[jax.ShapeDtypeStruct((S, W), F32), jax.ShapeDtypeStruct((nc, HEADS, HEAD_DIM, HEAD_DIM), F32)],
        scratch_shapes=[pltpu.VMEM((HEADS, HEAD_DIM, HEAD_DIM), F32)],
        compiler_params=_params(("arbitrary",)),
    )(u, wk, qd, kd, *qks, gl)


def _gdn_scan_bwd(u, wk, qd, kd, qks, gl, s_prev, d_o, name):
    S = u.shape[0]
    nc = S // CHUNK
    cs = _pick(nc, SCAN_CHUNKS)
    nb = nc // cs
    W = HEADS * HEAD_DIM

    def body(u_ref, wk_ref, qd_ref, kd_ref, qk0, qk1, qk2, qk3, gl_ref, sp_ref, do_ref,
             du_ref, dwk_ref, dqd_ref, dkd_ref, dqk0, dqk1, dqk2, dqk3, dgl_ref, ds_ref):
        @pl.when(pl.program_id(0) == 0)
        def _():
            ds_ref[...] = jnp.zeros_like(ds_ref)

        d_state = ds_ref[...]
        for c in reversed(range(cs)):
            rows, gl_rows = slice(CHUNK * c, CHUNK * (c + 1)), slice(8 * c, 8 * (c + 1))
            _, vjp = jax.vjp(_scan_step, sp_ref[c], _heads(u_ref, HEAD_DIM, rows), _heads(wk_ref, HEAD_DIM, rows),
                             _heads(qd_ref, HEAD_DIM, rows), _heads(kd_ref, HEAD_DIM, rows),
                             jnp.stack([r[rows, :] for r in (qk0, qk1, qk2, qk3)]), _heads(gl_ref, HEAD_DIM, gl_rows))
            d_state, du, dwk, dqd, dkd, dqk, dgl = vjp((_heads(do_ref, HEAD_DIM, rows), d_state))
            for h, dqk_ref in enumerate((dqk0, dqk1, dqk2, dqk3)):
                sl = slice(HEAD_DIM * h, HEAD_DIM * (h + 1))
                du_ref[rows, sl] = du[h]
                dwk_ref[rows, sl] = dwk[h]
                dqd_ref[rows, sl] = dqd[h]
                dkd_ref[rows, sl] = dkd[h]
                dqk_ref[rows, :] = dqk[h]
                dgl_ref[gl_rows, sl] = dgl[h]
        ds_ref[...] = d_state

    rev = lambda n: (nb - 1 - n, 0)
    row = pl.BlockSpec((cs * CHUNK, W), rev)
    qk_spec = pl.BlockSpec((cs * CHUNK, CHUNK), rev)
    gl_spec = pl.BlockSpec((cs * 8, W), rev)
    qk_shape = jax.ShapeDtypeStruct((S, CHUNK), F32)
    row_shape = jax.ShapeDtypeStruct((S, W), F32)
    return pl.pallas_call(
        body, name=name, grid=(nb,),
        in_specs=[row, row, row, row, qk_spec, qk_spec, qk_spec, qk_spec, gl_spec,
                  pl.BlockSpec((cs, HEADS, HEAD_DIM, HEAD_DIM), lambda n: (nb - 1 - n, 0, 0, 0)), row],
        out_specs=[row, row, row, row, qk_spec, qk_spec, qk_spec, qk_spec, gl_spec],
        out_shape=[row_shape] * 4 + [qk_shape] * 4 + [jax.ShapeDtypeStruct((nc * 8, W), F32)],
        scratch_shapes=[pltpu.VMEM((HEADS, HEAD_DIM, HEAD_DIM), F32)],
        compiler_params=_params(("arbitrary",)),
    )(u, wk, qd, kd, *qks, gl, s_prev, d_o)


NEG = -1e30


def _chunk_mask(i, j, t, transposed=False):
    q_axis, k_axis = (1, 0) if transposed else (0, 1)
    r = (i * t + lax.broadcasted_iota(jnp.int32, (t, t), q_axis)) // CHUNK
    c = (j * t + lax.broadcasted_iota(jnp.int32, (t, t), k_axis)) // CHUNK
    return c <= r


def _tile_pairs(n, by_key):
    pairs = [(i, j) for j in range(n) for i in range(j, n)] if by_key else [(i, j) for i in range(n) for j in range(i + 1)]
    return jnp.asarray(np.array([p[0] for p in pairs], np.int32)), jnp.asarray(np.array([p[1] for p in pairs], np.int32))


def _heads(ref, width, rows=slice(None)):
    return jnp.stack([ref[rows, width * h:width * (h + 1)] for h in range(HEADS)])


def _bmm(a, b, dims):
    return lax.dot_general(a.astype(BF16), b.astype(BF16), dims, preferred_element_type=F32)


def _attn_fwd(q, k, v_t, t, name):
    S = q.shape[0]
    n = S // t
    qi, kj = _tile_pairs(n, by_key=False)

    def body(qi_ref, kj_ref, q_ref, k_ref, vt_ref, o_ref, lse_ref, m_ref, l_ref, acc_ref):
        i, j = qi_ref[pl.program_id(0)], kj_ref[pl.program_id(0)]

        @pl.when(j == 0)
        def _():
            m_ref[...] = jnp.full_like(m_ref, NEG)
            l_ref[...] = jnp.zeros_like(l_ref)
            acc_ref[...] = jnp.zeros_like(acc_ref)

        def update(masked):
            s_t = _bmm(_heads(k_ref, 256), _heads(q_ref, 256), _BNT)
            if masked:
                s_t = jnp.where(_chunk_mask(i, j, t, transposed=True)[None], s_t, NEG)
            m_old = m_ref[...]
            m_new = jnp.maximum(m_old, jnp.max(s_t, axis=1, keepdims=True))
            p_t = jnp.exp(s_t - m_new)
            alpha = jnp.exp(m_old - m_new)
            l_ref[...] = alpha * l_ref[...] + jnp.sum(p_t, axis=1, keepdims=True)
            v_heads = jnp.stack([vt_ref[HEAD_DIM * h:HEAD_DIM * (h + 1), :] for h in range(HEADS)])
            acc_ref[...] = alpha * acc_ref[...] + _bmm(v_heads, p_t, _BNN)
            m_ref[...] = m_new

        @pl.when(j < i)
        def _():
            update(False)

        @pl.when(j == i)
        def _():
            update(True)
            for h in range(HEADS):
                sl = slice(HEAD_DIM * h, HEAD_DIM * (h + 1))
                o_ref[:, sl] = jnp.transpose(acc_ref[h] / l_ref[h])
                lse_ref[:, sl] = jnp.transpose(jnp.broadcast_to(m_ref[h] + jnp.log(l_ref[h]), (HEAD_DIM, t)))

    row = lambda p, qi_, kj_: (qi_[p], 0)
    return pl.pallas_call(
        body, name=name,
        grid_spec=pltpu.PrefetchScalarGridSpec(
            num_scalar_prefetch=2, grid=(qi.shape[0],),
            in_specs=[pl.BlockSpec((t, HEADS * 256), row), pl.BlockSpec((t, HEADS * 256), lambda p, qi_, kj_: (kj_[p], 0)),
                      pl.BlockSpec((HEADS * HEAD_DIM, t), lambda p, qi_, kj_: (0, kj_[p]))],
            out_specs=[pl.BlockSpec((t, HEADS * HEAD_DIM), row)] * 2,
            scratch_shapes=[pltpu.VMEM((HEADS, 1, t), F32), pltpu.VMEM((HEADS, 1, t), F32),
                            pltpu.VMEM((HEADS, HEAD_DIM, t), F32)]),
        out_shape=[jax.ShapeDtypeStruct((S, HEADS * HEAD_DIM), F32)] * 2,
        compiler_params=_params(("arbitrary",)),
    )(qi, kj, q, k, v_t)


def _attn_stats(o, lse, d_o, t, name):
    S = o.shape[0]

    def body(o_ref, lse_ref, do_ref, st_ref):
        lane = lax.broadcasted_iota(jnp.int32, (t, HEAD_DIM), 1)
        stats = jnp.zeros((t, HEAD_DIM), F32)
        for h in range(HEADS):
            sl = slice(HEAD_DIM * h, HEAD_DIM * (h + 1))
            delta = jnp.sum(do_ref[:, sl] * o_ref[:, sl], axis=1, keepdims=True)
            stats = stats + jnp.where(lane == h, lse_ref[:, sl], 0.0) + jnp.where(lane == HEADS + h, delta, 0.0)
        st_ref[...] = jnp.transpose(stats)[0:8, :]

    row = pl.BlockSpec((t, HEADS * HEAD_DIM), lambda i: (i, 0))
    return pl.pallas_call(
        body, name=name, grid=(S // t,),
        in_specs=[row, row, row], out_specs=pl.BlockSpec((8, t), lambda i: (0, i)),
        out_shape=jax.ShapeDtypeStruct((8, S), F32),
        compiler_params=_params(("parallel",)),
    )(o, lse, d_o)


BWD_GROUP = 2


def _attn_bwd(q, k, v, d_o, stats, t, name):
    S = q.shape[0]
    n = S // t
    groups = HEADS // BWD_GROUP
    gq, gv = BWD_GROUP * 256, BWD_GROUP * HEAD_DIM
    qi, kj = _tile_pairs(n, by_key=True)
    n_pairs = qi.shape[0]
    st = stats.reshape(2, groups, BWD_GROUP, S).transpose(1, 0, 2, 3).reshape(groups, 2 * BWD_GROUP, S)
    st = jnp.pad(st, ((0, 0), (0, 8 - 2 * BWD_GROUP), (0, 0)))

    def heads(ref, width, rows=slice(None)):
        return jnp.stack([ref[rows, width * h:width * (h + 1)] for h in range(BWD_GROUP)])

    def body(qi_ref, kj_ref, q_ref, k_ref, v_ref, do_ref, st_ref, dq_hbm, dk_ref, dv_ref, dq_acc, sem):
        g, p = pl.program_id(0), pl.program_id(1)
        i, j = qi_ref[p], kj_ref[p]

        @pl.when(i == j)
        def _():
            dk_ref[...] = jnp.zeros_like(dk_ref)
            dv_ref[...] = jnp.zeros_like(dv_ref)

        def update(masked):
            qh, kh = heads(q_ref, 256), heads(k_ref, 256)
            d_out = heads(do_ref, HEAD_DIM)
            stv = st_ref[...]
            lse_row = jnp.stack([stv[h:h + 1, :] for h in range(BWD_GROUP)])
            delta_row = jnp.stack([stv[BWD_GROUP + h:BWD_GROUP + h + 1, :] for h in range(BWD_GROUP)])
            s_t = _bmm(kh, qh, _BNT)
            p_t = jnp.exp(s_t - lse_row)
            if masked:
                p_t = jnp.where(_chunk_mask(i, j, t, transposed=True)[None], p_t, 0.0)
            dv = _bmm(p_t, d_out, _BNN)
            dp_t = _bmm(heads(v_ref, HEAD_DIM), d_out, _BNT)
            ds_t = p_t * (dp_t - delta_row)
            dk = _bmm(ds_t, qh, _BNN)
            dq = _bmm(jnp.swapaxes(ds_t, 1, 2), kh, _BNN)
            rows = pl.ds(pl.multiple_of(i * t, t), t)
            for h in range(BWD_GROUP):
                dk_ref[:, 256 * h:256 * (h + 1)] += dk[h]
                dv_ref[:, HEAD_DIM * h:HEAD_DIM * (h + 1)] += dv[h]

            @pl.when(j == 0)
            def _():
                for h in range(BWD_GROUP):
                    dq_acc[rows, 256 * h:256 * (h + 1)] = dq[h]

            @pl.when(j > 0)
            def _():
                for h in range(BWD_GROUP):
                    dq_acc[rows, 256 * h:256 * (h + 1)] += dq[h]

        @pl.when(i == j)
        def _():
            update(True)

        @pl.when(i > j)
        def _():
            update(False)

        @pl.when(p == n_pairs - 1)
        def _():
            for gg in range(groups):
                @pl.when(g == gg)
                def _():
                    cp = pltpu.make_async_copy(dq_acc, dq_hbm.at[:, gq * gg:gq * (gg + 1)], sem)
                    cp.start()
                    cp.wait()

    q_blk = lambda g, p, qi_, kj_: (qi_[p], g)
    k_blk = lambda g, p, qi_, kj_: (kj_[p], g)
    return pl.pallas_call(
        body, name=name,
        grid_spec=pltpu.PrefetchScalarGridSpec(
            num_scalar_prefetch=2, grid=(groups, n_pairs),
            in_specs=[pl.BlockSpec((t, gq), q_blk), pl.BlockSpec((t, gq), k_blk), pl.BlockSpec((t, gv), k_blk),
                      pl.BlockSpec((t, gv), q_blk), pl.BlockSpec((None, 8, t), lambda g, p, qi_, kj_: (g, 0, qi_[p]))],
            out_specs=[pl.BlockSpec(memory_space=pl.ANY), pl.BlockSpec((t, gq), k_blk), pl.BlockSpec((t, gv), k_blk)],
            scratch_shapes=[pltpu.VMEM((S, gq), F32), pltpu.SemaphoreType.DMA]),
        out_shape=[jax.ShapeDtypeStruct((S, HEADS * 256), F32), jax.ShapeDtypeStruct((S, HEADS * 256), F32),
                   jax.ShapeDtypeStruct((S, HEADS * HEAD_DIM), F32)],
        compiler_params=_params(("arbitrary", "arbitrary")),
    )(qi, kj, q, k, v, d_o, st)


FFN_PIECE = 2 * D_FF // N_DEV
HID_PIECES = D_FF // FFN_PIECE


def _ffn_up(h, w8, name):
    S = h.shape[0]
    tm = _pick(S, MATMUL_ROWS)

    def body(h_ref, wg_ref, wu_ref, g_ref, u_ref, hid_ref):
        gate = _dot_raw(h_ref[...], wg_ref[...], "nn")
        up = _dot_raw(h_ref[...], wu_ref[...], "nn")
        sg = _sigmoid(gate)
        act = gate * sg
        g_ref[...] = (up * (sg * (1.0 + gate * (1.0 - sg)))).astype(BF16)
        u_ref[...] = act.astype(BF16)
        hid_ref[...] = (act * up).astype(BF16)

    o_spec = pl.BlockSpec((None, tm, FFN_PIECE), lambda i, j: (j, i, 0))
    return pl.pallas_call(
        body, name=name, grid=(S // tm, HID_PIECES),
        in_specs=[pl.BlockSpec((tm, D_MODEL), lambda i, j: (i, 0)),
                  pl.BlockSpec((None, D_MODEL, FFN_PIECE), lambda i, j: (j, 0, 0)),
                  pl.BlockSpec((None, D_MODEL, FFN_PIECE), lambda i, j: (j + HID_PIECES, 0, 0))],
        out_specs=[o_spec] * 3,
        out_shape=[jax.ShapeDtypeStruct((HID_PIECES, S, FFN_PIECE), BF16)] * 3,
        compiler_params=_params(("parallel", "parallel")),
    )(h, w8, w8)


def _after_specs(after):
    return [] if after is None else [pl.BlockSpec(memory_space=pl.ANY)]


def _after_args(after):
    return [] if after is None else [after]


def _ffn_gw8(h, d_gate, d_up, name, after=None):
    S = h.shape[0]
    tm = 512
    tk = _pick(S, (512, 256, 128))
    nk = S // tk

    def body(h_ref, dg_ref, du_ref, *rest):
        o_ref, acc_ref = rest[-2:]
        k = pl.program_id(1)

        @pl.when(k == 0)
        def _():
            acc_ref[...] = jnp.zeros_like(acc_ref)

        h_t = jnp.transpose(h_ref[...])
        for p in range(HID_PIECES):
            acc_ref[p] += _dot_raw(h_t, dg_ref[p], "nn")
            acc_ref[HID_PIECES + p] += _dot_raw(h_t, du_ref[p], "nn")

        @pl.when(k == nk - 1)
        def _():
            o_ref[...] = acc_ref[...].astype(o_ref.dtype)

    d_spec = pl.BlockSpec((HID_PIECES, tk, FFN_PIECE), lambda i, k: (0, k, 0))
    return pl.pallas_call(
        body, name=name, grid=(D_MODEL // tm, nk),
        in_specs=[pl.BlockSpec((tk, tm), lambda i, k: (k, i)), d_spec, d_spec] + _after_specs(after),
        out_specs=pl.BlockSpec((2 * HID_PIECES, tm, FFN_PIECE), lambda i, k: (0, i, 0)),
        out_shape=jax.ShapeDtypeStruct((2 * HID_PIECES, D_MODEL, FFN_PIECE), BF16),
        scratch_shapes=[pltpu.VMEM((2 * HID_PIECES, tm, FFN_PIECE), F32)],
        compiler_params=_params(("parallel", "arbitrary")),
    )(h, d_gate, d_up, *_after_args(after))


def _ffn_dh(d_gate, d_up, w8, x, d_out, scale, shift, name, after=None):
    S = d_gate.shape[1]
    tm = _pick(S, (512, 256, 128))

    def body(dg_ref, du_ref, wg_ref, wu_ref, x_ref, do_ref, sc_ref, sh_ref, *rest):
        dx_ref, dsc_ref, dsh_ref, acc_ref = rest[-4:]
        i, k = pl.program_id(0), pl.program_id(1)

        @pl.when(k == 0)
        def _():
            acc_ref[...] = jnp.zeros_like(acc_ref)

        acc_ref[...] += _dot_raw(dg_ref[...], wg_ref[...], "nt") + _dot_raw(du_ref[...], wu_ref[...], "nt")

        @pl.when((k == 0) & (i == 0))
        def _():
            dsc_ref[...] = jnp.zeros_like(dsc_ref)
            dsh_ref[...] = jnp.zeros_like(dsh_ref)

        @pl.when(k == HID_PIECES - 1)
        def _():
            _, vjp = jax.vjp(_modulate, x_ref[...], sc_ref[...], sh_ref[...])
            dx, dsc, dsh = vjp(acc_ref[...])
            dx_ref[...] = dx + do_ref[...]
            dsc_ref[...] += dsc
            dsh_ref[...] += dsh

    d_spec = pl.BlockSpec((None, tm, FFN_PIECE), lambda i, k: (k, i, 0))
    row = pl.BlockSpec((tm, D_MODEL), lambda i, k: (i, 0))
    par = pl.BlockSpec((1, D_MODEL), lambda i, k: (0, 0))
    return pl.pallas_call(
        body, name=name, grid=(S // tm, HID_PIECES),
        in_specs=[d_spec, d_spec,
                  pl.BlockSpec((None, D_MODEL, FFN_PIECE), lambda i, k: (k, 0, 0)),
                  pl.BlockSpec((None, D_MODEL, FFN_PIECE), lambda i, k: (k + HID_PIECES, 0, 0)),
                  row, row, par, par] + _after_specs(after),
        out_specs=[row, par, par],
        out_shape=[jax.ShapeDtypeStruct((S, D_MODEL), F32), jax.ShapeDtypeStruct((1, D_MODEL), F32),
                   jax.ShapeDtypeStruct((1, D_MODEL), F32)],
        scratch_shapes=[pltpu.VMEM((tm, D_MODEL), F32)],
        compiler_params=_params(("arbitrary", "arbitrary")),
    )(d_gate, d_up, w8, w8, x, d_out, scale, shift, *_after_args(after))


def _swiglu_bwd(d_hid, hid_by_gate, hid_by_up):
    return d_hid * hid_by_gate, d_hid * hid_by_up


def _adamw(w, g, m, v, name):
    R, C = w.shape
    tr = _pick(R, (256, 176, 128, 64, 32, 16, 8))

    def body(w_ref, g_ref, m_ref, v_ref, d_ref, nm_ref, nv_ref):
        g_ = g_ref[...]
        m_ = ADAM_B1 * m_ref[...] + (1.0 - ADAM_B1) * g_
        v_ = ADAM_B2 * v_ref[...] + (1.0 - ADAM_B2) * (g_ * g_)
        m_hat = m_ / (1.0 - ADAM_B1 ** ADAM_STEP)
        v_hat = v_ / (1.0 - ADAM_B2 ** ADAM_STEP)
        d_ref[...] = -ADAM_LR * (m_hat / (jnp.sqrt(v_hat) + ADAM_EPS) + ADAM_WD * w_ref[...])
        nm_ref[...] = m_
        nv_ref[...] = v_

    spec = pl.BlockSpec((tr, C), lambda i: (i, 0))
    return pl.pallas_call(
        body, name=name, grid=(R // tr,),
        in_specs=[spec] * 4, out_specs=[spec] * 3,
        out_shape=[jax.ShapeDtypeStruct((R, C), F32)] * 3,
        compiler_params=_params(("parallel",)),
    )(w, g, m, v)


def _sum_devices(parts, name):
    _, R, C = parts.shape
    tr = _pick(R, (512, 256, 176, 128, 64, 32, 16, 8))

    def body(p_ref, o_ref):
        acc = p_ref[0].astype(F32)
        for d in range(1, N_DEV):
            acc = acc + p_ref[d].astype(F32)
        o_ref[...] = acc

    return pl.pallas_call(
        body, name=name, grid=(R // tr,),
        in_specs=[pl.BlockSpec((N_DEV, tr, C), lambda i: (0, i, 0))],
        out_specs=pl.BlockSpec((tr, C), lambda i: (i, 0)),
        out_shape=jax.ShapeDtypeStruct((R, C), F32),
        compiler_params=_params(("parallel",)),
    )(parts)


def _my_place():
    return lax.axis_index("x"), lax.axis_index("y"), lax.axis_index("c")


def _all_gather(blocks, name):
    n = len(blocks)

    def body(*refs):
        x_refs, out_refs = refs[:n], refs[n:2 * n]
        send_sems, recv_sems, local_sems = refs[2 * n:]
        x, y, c = _my_place()
        me, sibling = (x, y, c), (x, y, 1 - c)
        chips = [(1 - x, y), (x, 1 - y), (1 - x, 1 - y)]

        def copy(a, k, blk, to, own=False):
            slot = out_refs[a].at[4 * blk[0] + 2 * blk[1] + blk[2]]
            return pltpu.make_async_remote_copy(
                src_ref=x_refs[a] if own else slot, dst_ref=slot,
                send_sem=send_sems.at[7 * a + k], recv_sem=recv_sems.at[7 * a + k], device_id=to, device_id_type=MESH)

        mine = [pltpu.make_async_copy(x_refs[a], out_refs[a].at[4 * x + 2 * y + c], local_sems.at[a]) for a in range(n)]
        for cp in mine:
            cp.start()
        first = []
        for j, chip in enumerate(chips):
            first += [copy(a, 1 + j, me, (*chip, c), own=True) for a in range(n)]
        first += [copy(a, 0, me, sibling, own=True) for a in range(n)]
        for cp in first:
            cp.start()
        passed = []
        for j, chip in enumerate(chips):
            for a in range(n):
                copy(a, 1 + j, (*chip, c), me).wait_recv()
                passed.append(copy(a, 4 + j, (*chip, c), sibling))
                passed[-1].start()
        for a in range(n):
            copy(a, 0, sibling, me).wait_recv()
        for j, chip in enumerate(chips):
            for a in range(n):
                copy(a, 4 + j, (*chip, 1 - c), me).wait_recv()
        for cp in first + passed:
            cp.wait_send()
        for cp in mine:
            cp.wait()

    return pl.pallas_call(
        body, name=name,
        out_shape=[jax.ShapeDtypeStruct((N_DEV,) + b.shape, b.dtype) for b in blocks],
        in_specs=[pl.BlockSpec(memory_space=pl.ANY)] * n,
        out_specs=[pl.BlockSpec(memory_space=pl.ANY)] * n,
        scratch_shapes=[pltpu.SemaphoreType.DMA((7 * n,)), pltpu.SemaphoreType.DMA((7 * n,)), pltpu.SemaphoreType.DMA((n,))],
    )(*blocks)


def _all_to_all(pieces, name):
    n = len(pieces)

    def body(*refs):
        x_refs, out_refs = refs[:n], refs[n:2 * n]
        send_sems, recv_sems, local_sems = refs[2 * n:]
        x, y, c = _my_place()
        me = 4 * x + 2 * y + c
        mine = [pltpu.make_async_copy(x_refs[a].at[me], out_refs[a].at[me], local_sems.at[a]) for a in range(n)]
        for cp in mine:
            cp.start()
        copies = []
        for k in (2, 4, 6, 3, 5, 7, 1):
            px = 1 - x if k & 4 else x
            py = 1 - y if k & 2 else y
            pc = 1 - c if k & 1 else c
            peer = 4 * px + 2 * py + pc
            for a in range(n):
                copies.append(pltpu.make_async_remote_copy(
                    src_ref=x_refs[a].at[peer], dst_ref=out_refs[a].at[me],
                    send_sem=send_sems.at[7 * a + k - 1], recv_sem=recv_sems.at[7 * a + k - 1],
                    device_id=(px, py, pc), device_id_type=MESH))
        for cp in copies:
            cp.start()
        for cp in copies:
            cp.wait_recv()
        for cp in copies:
            cp.wait_send()
        for cp in mine:
            cp.wait()

    return pl.pallas_call(
        body, name=name,
        out_shape=[jax.ShapeDtypeStruct(p.shape, p.dtype) for p in pieces],
        in_specs=[pl.BlockSpec(memory_space=pl.ANY)] * n,
        out_specs=[pl.BlockSpec(memory_space=pl.ANY)] * n,
        scratch_shapes=[pltpu.SemaphoreType.DMA((7 * n,)), pltpu.SemaphoreType.DMA((7 * n,)), pltpu.SemaphoreType.DMA((n,))],
    )(*pieces)


def _peers():
    x, y, c = _my_place()
    out = []
    for k in (2, 4, 6, 3, 5, 7, 1):
        px = 1 - x if k & 4 else x
        py = 1 - y if k & 2 else y
        pc = 1 - c if k & 1 else c
        out.append((k, (px, py, pc), 4 * px + 2 * py + pc))
    return out


def _exchange_copies(x_refs, land_refs, send_sems, recv_sems, scatter):
    x, y, c = _my_place()
    me = 4 * x + 2 * y + c
    starts, arrivals = [], []
    for k, place, peer in _peers():
        for a, (x_ref, land_ref) in enumerate(zip(x_refs, land_refs)):
            sems = dict(send_sem=send_sems.at[7 * a + k - 1], recv_sem=recv_sems.at[7 * a + k - 1],
                        device_id=place, device_id_type=MESH)
            src = x_ref.at[peer] if scatter else x_ref
            starts.append(pltpu.make_async_remote_copy(src_ref=src, dst_ref=land_ref.at[me], **sems))
            arrivals.append(pltpu.make_async_remote_copy(src_ref=src, dst_ref=land_ref.at[peer], **sems))
    return starts, arrivals


def _exchange_start(arrays, scatter, name):
    n = len(arrays)
    hbm = pl.BlockSpec(memory_space=pltpu.HBM)
    sem = pl.BlockSpec(memory_space=pltpu.SEMAPHORE)
    lands = [lax.empty(a.shape if scatter else (N_DEV,) + a.shape, a.dtype) for a in arrays]

    def body(*refs):
        x_refs, land_refs = refs[:n], refs[n:2 * n]
        send_sems, recv_sems = refs[2 * n], refs[2 * n + 1]
        token = refs[-1]
        starts, _ = _exchange_copies(x_refs, land_refs, send_sems, recv_sems, scatter)
        for cp in starts:
            cp.start()
        token[...] = jnp.zeros_like(token)

    res = pl.pallas_call(
        body, name=name,
        out_shape=(pltpu.SemaphoreType.DMA((7 * n,)), pltpu.SemaphoreType.DMA((7 * n,)),
                   *[pltpu.HBM(a.shape, a.dtype) for a in arrays], *[pltpu.HBM(l.shape, l.dtype) for l in lands],
                   jax.ShapeDtypeStruct((8, 128), F32)),
        in_specs=[hbm] * (2 * n),
        out_specs=(sem, sem, *[hbm] * (2 * n), pl.BlockSpec(memory_space=pltpu.VMEM)),
        input_output_aliases={i: 2 + i for i in range(2 * n)},
        compiler_params=pltpu.CompilerParams(has_side_effects=pltpu.SideEffectType.DATAFLOW_SIDE_EFFECTING),
    )(*[pltpu.with_memory_space_constraint(a, pltpu.HBM) for a in arrays],
      *[pltpu.with_memory_space_constraint(l, pltpu.HBM) for l in lands])
    return res[0], res[1], list(res[2:2 + n]), list(res[2 + n:2 + 2 * n]), res[-1]


def _exchange_wait(handles, scatter, after, name):
    send_sems, recv_sems, arrays, lands, _ = handles
    n = len(arrays)
    hbm = pl.BlockSpec(memory_space=pltpu.HBM)
    sem = pl.BlockSpec(memory_space=pltpu.SEMAPHORE)

    def body(*refs):
        x_refs, land_refs = refs[:n], refs[n:2 * n]
        send_s, recv_s = refs[2 * n], refs[2 * n + 1]
        starts, arrivals = _exchange_copies(x_refs, land_refs, send_s, recv_s, scatter)
        for cp in arrivals:
            cp.wait_recv()
        for cp in starts:
            cp.wait_send()

    res = pl.pallas_call(
        body, name=name,
        out_shape=(*[pltpu.HBM(a.shape, a.dtype) for a in arrays], *[pltpu.HBM(l.shape, l.dtype) for l in lands]),
        in_specs=[hbm] * (2 * n) + [sem, sem, pl.BlockSpec(memory_space=pl.ANY)],
        out_specs=tuple([hbm] * (2 * n)),
        input_output_aliases={i: i for i in range(2 * n)},
        compiler_params=pltpu.CompilerParams(has_side_effects=pltpu.SideEffectType.DATAFLOW_SIDE_EFFECTING),
    )(*arrays, *lands, send_sems, recv_sems, after)
    me = 4 * lax.axis_index("x") + 2 * lax.axis_index("y") + lax.axis_index("c")
    out = []
    for src, got in zip(res[:n], res[n:]):
        zeros = (0,) * (got.ndim - 1)
        own = lax.dynamic_slice(src, (me,) + zeros, (1,) + src.shape[1:]) if scatter else src[None]
        out.append(lax.dynamic_update_slice(got, own, (me,) + zeros))
    return out


def _pad_lanes(v, at=0, width=128):
    return jnp.pad(v, ((0, 0), (at, width - at - v.shape[1])))


def _pack_weights(P):
    W = {}
    w = P["w_in"]
    W["wp"] = jnp.concatenate([w[:, :2048], w[:, 2440:2696], w[:, 2056:2440], w[:, 2696:2760], w[:, 2048:2056],
                               jnp.zeros((D_MODEL, N_IN_PACKED - N_IN), w.dtype)], axis=1).astype(BF16)
    W["conv_w"] = P["gdn_conv_w"].astype(F32)
    W["alog_p"] = _pad_lanes(P["gdn_a_log"], 64)
    W["dt_p"] = _pad_lanes(P["gdn_dt_bias"], 64)
    W["gnw"] = P["gdn_norm_w"]
    W["qnw"] = P["mla_q_norm_w"]
    W["kvnw"] = P["mla_kv_norm_w"]
    uq = P["mla_w_uq"].reshape(Q_LORA, HEADS, HEAD_DIM + ROPE)
    W["wuq"] = jnp.pad(uq, ((0, 0), (0, 0), (0, 256 - HEAD_DIM - ROPE))).reshape(Q_LORA, HEADS * 256).astype(BF16)
    ukv = P["mla_w_ukv"].reshape(KV_LORA, HEADS, 2, HEAD_DIM)
    W["wukv"] = ukv.transpose(0, 2, 1, 3).reshape(KV_LORA, 2 * HEADS * HEAD_DIM).astype(BF16)
    W["qn_w"] = P["qkn_q_nope"]
    W["qr_w"] = _pad_lanes(P["qkn_q_rope"])
    W["kn_w"] = P["qkn_k_nope"]
    W["kr_w"] = _pad_lanes(P["qkn_k_rope"])
    W["onw"] = P["mla_out_norm_w"]
    W["wout"] = P["w_out"].astype(BF16)
    return W


def _unpack_grads(G):
    g = G["wp"]
    uq = G["wuq"].reshape(Q_LORA, HEADS, 256)[:, :, :HEAD_DIM + ROPE].reshape(Q_LORA, HEADS * (HEAD_DIM + ROPE))
    ukv = G["wukv"].reshape(KV_LORA, 2, HEADS, HEAD_DIM).transpose(0, 2, 1, 3).reshape(KV_LORA, 2 * HEADS * HEAD_DIM)
    return {
        "w_in": jnp.concatenate([g[:, :2048], g[:, 2752:2760], g[:, 2304:2688], g[:, 2048:2304], g[:, 2688:2752]], axis=1),
        "gdn_conv_w": G["conv_w"], "gdn_a_log": G["alog_p"][:, 64:68], "gdn_dt_bias": G["dt_p"][:, 64:68],
        "gdn_norm_w": G["gnw"], "mla_q_norm_w": G["qnw"], "mla_w_uq": uq, "mla_kv_norm_w": G["kvnw"], "mla_w_ukv": ukv,
        "qkn_q_nope": G["qn_w"], "qkn_q_rope": G["qr_w"][:, :ROPE], "qkn_k_nope": G["kn_w"], "qkn_k_rope": G["kr_w"][:, :ROPE],
        "mla_out_norm_w": G["onw"], "w_out": G["wout"],
    }


def _rope_tables(positions):
    half = ROPE // 2
    inv_freq = ROPE_BASE ** (-jnp.arange(half, dtype=F32) / half)
    ang = positions.astype(F32)[:, None] * inv_freq
    cos, sin = jnp.cos(ang), jnp.sin(ang)
    zeros = jnp.zeros((positions.shape[0], 128 - ROPE), F32)
    return jnp.concatenate([cos, cos, zeros], axis=1), jnp.concatenate([-sin, sin, zeros], axis=1)


def _mod_fn(x, scale, shift):
    return (_modulate(x, scale, shift),)


def _ffn_down_loss(hid, wo4, x, gate_w, target, name):
    S = x.shape[0]
    tb = _pick(S, (512, 256, 128))
    n = S // tb

    def body(hid_ref, wo_ref, x_ref, g_ref, t_ref, dx_ref, df_ref, dg_ref, l_ref, acc_ref):
        i, k = pl.program_id(0), pl.program_id(1)

        @pl.when(k == 0)
        def _():
            acc_ref[...] = jnp.zeros_like(acc_ref)

        acc_ref[...] += _dot_raw(hid_ref[...], wo_ref[...], "nn")

        @pl.when((k == 0) & (i == 0))
        def _():
            dg_ref[...] = jnp.zeros_like(dg_ref)
            l_ref[...] = jnp.zeros_like(l_ref)

        @pl.when(k == HID_PIECES - 1)
        def _():
            f = acc_ref[...]
            diff = x_ref[...] + 0.5 * g_ref[...] * f - t_ref[...]
            dx = diff * (1.0 / D_MODEL)
            dx_ref[...] = dx
            df_ref[...] = (0.5 * g_ref[...] * dx).astype(df_ref.dtype)
            dg_ref[...] += jnp.sum(0.5 * f * dx, axis=0, keepdims=True)
            l_ref[...] += jnp.sum(diff * diff, axis=0, keepdims=True)

        @pl.when((k == HID_PIECES - 1) & (i == n - 1))
        def _():
            l_ref[...] = jnp.full(l_ref.shape, (0.5 / D_MODEL) * jnp.sum(l_ref[...]), F32)

    row = pl.BlockSpec((tb, D_MODEL), lambda i, k: (i, 0))
    par = pl.BlockSpec((1, D_MODEL), lambda i, k: (0, 0))
    return pl.pallas_call(
        body, name=name, grid=(n, HID_PIECES),
        in_specs=[pl.BlockSpec((None, tb, FFN_PIECE), lambda i, k: (k, i, 0)),
                  pl.BlockSpec((None, FFN_PIECE, D_MODEL), lambda i, k: (k, 0, 0)), row, par, row],
        out_specs=[row, row, par, par],
        out_shape=[jax.ShapeDtypeStruct((S, D_MODEL), F32), jax.ShapeDtypeStruct((S, D_MODEL), BF16),
                   jax.ShapeDtypeStruct((1, D_MODEL), F32), jax.ShapeDtypeStruct((1, D_MODEL), F32)],
        scratch_shapes=[pltpu.VMEM((tb, D_MODEL), F32)],
        compiler_params=_params(("arbitrary", "arbitrary")),
    )(hid, wo4, x, gate_w, target)


def _ffn_fwd(x, scale, shift, gate_w, w8, wo4, tag, target=None):
    S = x.shape[0]
    tm = _pick(S, (512, 256, 128))
    (h,) = _rowwise(_mod_fn, [(x, tm, D_MODEL, 0)], [scale, shift], [(tm, D_MODEL, BF16)], S // tm, tag + "_mod")
    by_gate, by_up, hid = _ffn_up(h, w8, tag + "_up")
    if target is not None:
        dx_out, df, d_gate_w, loss_row = _ffn_down_loss(hid, wo4, x, gate_w, target, tag + "_down")
        return (dx_out, loss_row), (h, by_gate, by_up, hid, None, df, d_gate_w)
    tb = _pick(S, MATMUL_ROWS)
    mn = pl.BlockSpec((tb, D_MODEL), lambda i, j, k: (i, j))
    f, x_out = _mmg(hid, wo4, "nn", name=tag + "_down", grid=(S // tb, 1, HID_PIECES),
                    a_spec=pl.BlockSpec((None, tb, FFN_PIECE), lambda i, j, k: (k, i, 0)),
                    b_spec=pl.BlockSpec((None, FFN_PIECE, D_MODEL), lambda i, j, k: (k, 0, j)),
                    out_spec=mn, out_shapes=[jax.ShapeDtypeStruct((S, D_MODEL), F32)] * 2, acc_shape=(tb, D_MODEL),
                    extras=[x, gate_w], extra_specs=[mn, pl.BlockSpec((1, D_MODEL), lambda i, j, k: (0, j))],
                    epi=lambda acc, x_, g_: (acc, x_ + 0.5 * g_ * acc))
    return x_out, (h, by_gate, by_up, hid, f, None, None)


def _ffn_bwd(d_out, x, scale, shift, gate_w, w8, wo4, saved, tag, grad_ready):
    h, gate, up, hid, f, df, d_gate_w = saved
    S = x.shape[0]
    tm = _pick(S, (512, 256, 128))
    tk = _pick(S, (512, 256, 128))
    n = S // tm
    if df is None:
        (df,), (d_gate_w,) = _rowwise_bwd(lambda f_, g_: (0.5 * g_ * f_,), [(f, tm, D_MODEL, 0)], [], [gate_w],
                                          [(d_out, tm, D_MODEL, 0)], n, tag + "_dres", row_dtypes=(BF16,))
    tb = _pick(S, MATMUL_ROWS)
    piece = pl.BlockSpec((None, tb, FFN_PIECE), lambda i, j, k: (j, i, 0))
    d_gate, d_up = _mmg(df, wo4, "nt", name=tag + "_ddown", grid=(S // tb, HID_PIECES, 1),
                        a_spec=pl.BlockSpec((tb, D_MODEL), lambda i, j, k: (i, 0)),
                        b_spec=pl.BlockSpec((None, FFN_PIECE, D_MODEL), lambda i, j, k: (j, 0, 0)),
                        out_spec=piece, out_shapes=[jax.ShapeDtypeStruct((HID_PIECES, S, FFN_PIECE), BF16)] * 2,
                        acc_shape=(tb, FFN_PIECE), extras=[gate, up], extra_specs=[piece, piece], epi=_swiglu_bwd)
    g_wo4 = _mmg(hid, df, "tn", name=tag + "_gwo", grid=(HID_PIECES, 1, S // tk),
                 a_spec=pl.BlockSpec((None, tk, FFN_PIECE), lambda i, j, k: (i, k, 0)),
                 b_spec=pl.BlockSpec((tk, D_MODEL), lambda i, j, k: (k, j)),
                 out_spec=pl.BlockSpec((None, FFN_PIECE, D_MODEL), lambda i, j, k: (i, 0, j)),
                 out_shapes=[jax.ShapeDtypeStruct((HID_PIECES, FFN_PIECE, D_MODEL), BF16)], acc_shape=(FFN_PIECE, D_MODEL))
    g_w8 = _ffn_gw8(h, d_gate, d_up, tag + "_gw8", after=grad_ready("wo4", g_wo4))
    dx, d_scale, d_shift = _ffn_dh(d_gate, d_up, w8, x, d_out, scale, shift, tag + "_dh", after=grad_ready("w8", g_w8))
    return dx, d_scale, d_shift, d_gate_w


def _dproj_dmod(d_proj, wp, x, d_out, scale, shift, name):
    S, K = d_proj.shape
    tm = _pick(S, (512, 256, 128))
    tk = _pick(K, (1408, 512, 256, 128))
    nk = K // tk

    def body(dp_ref, w_ref, x_ref, do_ref, sc_ref, sh_ref, dx_ref, dsc_ref, dsh_ref, acc_ref):
        i, k = pl.program_id(0), pl.program_id(1)

        @pl.when(k == 0)
        def _():
            acc_ref[...] = jnp.zeros_like(acc_ref)

        acc_ref[...] += _dot_raw(dp_ref[...], w_ref[...], "nt")

        @pl.when((k == 0) & (i == 0))
        def _():
            dsc_ref[...] = jnp.zeros_like(dsc_ref)
            dsh_ref[...] = jnp.zeros_like(dsh_ref)

        @pl.when(k == nk - 1)
        def _():
            _, vjp = jax.vjp(_modulate, x_ref[...], sc_ref[...], sh_ref[...])
            dx, dsc, dsh = vjp(acc_ref[...])
            dx_ref[...] = dx + do_ref[...]
            dsc_ref[...] += dsc
            dsh_ref[...] += dsh

    row = pl.BlockSpec((tm, D_MODEL), lambda i, k: (i, 0))
    par = pl.BlockSpec((1, D_MODEL), lambda i, k: (0, 0))
    return pl.pallas_call(
        body, name=name, grid=(S // tm, nk),
        in_specs=[pl.BlockSpec((tm, tk), lambda i, k: (i, k)), pl.BlockSpec((D_MODEL, tk), lambda i, k: (0, k)),
                  row, row, par, par],
        out_specs=[row, par, par],
        out_shape=[jax.ShapeDtypeStruct((S, D_MODEL), F32), jax.ShapeDtypeStruct((1, D_MODEL), F32),
                   jax.ShapeDtypeStruct((1, D_MODEL), F32)],
        scratch_shapes=[pltpu.VMEM((tm, D_MODEL), F32)],
        compiler_params=_params(("arbitrary", "arbitrary")),
    )(d_proj, wp, x, d_out, scale, shift)


def _mixer_fwd(x1, scale, shift, gate_w, cos_p, sin_p, W):
    S = x1.shape[0]
    tm = _pick(S, (512, 256, 128))
    tv = _pick(S, (256, 128))
    ta = _pick(S, (512, 256, 128))
    nc = S // CHUNK
    (h2,) = _rowwise(_mod_fn, [(x1, tm, D_MODEL, 0)], [scale, shift], [(tm, D_MODEL, BF16)], S // tm, "mix_mod")
    proj = _mm(h2, W["wp"], "nn", name="mix_proj")
    qkvc = _conv_fwd(proj, W["conv_w"], tv, "gdn_conv")
    kab = (proj, tv, 128, 21)
    q_a, k_a, v_a, gb = _rowwise(_gdn_pre_fn, [(qkvc, tv, 1536, 0), kab], [W["alog_p"], W["dt_p"]],
                                 [(tv, 512, F32)] * 3 + [(tv, 128, F32)], S // tv, "gdn_pre")
    ti = _pick(S, INTRA_ROWS)
    intra = _rowwise(_gdn_intra_fn, [(q_a, ti, 512, 0), (k_a, ti, 512, 0), (v_a, ti, 512, 0), (gb, ti, 128, 0)],
                     [], [(ti, 512, F32)] * 4 + [(ti, CHUNK, F32)] * 4 + [(ti // 8, 512, F32)], S // ti, "gdn_intra")
    u, wk, qd, kd, qks, gl = intra[0], intra[1], intra[2], intra[3], tuple(intra[4:8]), intra[8]
    o_a, s_prev = _gdn_scan_fwd(u, wk, qd, kd, qks, gl, "gdn_scan")
    mla_params = [W["qnw"], W["kvnw"], W["wuq"], W["wukv"], W["qn_w"], W["qr_w"], W["kn_w"], W["kr_w"]]
    def mla_pre_with_vt(*a):
        q_, k_, v_ = _mla_pre_fn(*a)
        return q_, k_, v_, jnp.transpose(v_)

    q_b, k_b, v_b, vt_b = _rowwise(mla_pre_with_vt,
                                   [(proj, tv, 256, 8), (proj, tv, 384, 6), kab, (cos_p, tv, 128, 0), (sin_p, tv, 128, 0)],
                                   mla_params, [(tv, 1024, BF16), (tv, 1024, BF16), (tv, 512, BF16), (512, tv, BF16, "across")],
                                   S // tv, "mla_pre")
    o_b, lse = _attn_fwd(q_b, k_b, vt_b, ta, "mla_attn")
    (mixed,) = _rowwise(_mix_post_fn, [(o_a, tv, 512, 0), (proj, tv, 512, 3), (o_b, tv, 512, 0)], [W["gnw"], W["onw"]],
                        [(tv, D_MODEL, BF16)], S // tv, "mix_post")
    y, x2 = _mm(mixed, W["wout"], "nn", name="mix_out", out_dtypes=(F32, F32), extras=[x1], extra_params=[gate_w],
                epi=lambda acc, x_, g_: (acc, x_ + g_ * acc))
    saved = (h2, proj, qkvc, q_a, k_a, v_a, gb, u, wk, qd, kd, qks, gl, s_prev, o_a, q_b, k_b, v_b, o_b, lse, mixed, y)
    return x2, saved


def _mixer_bwd(d_out, x1, scale, shift, gate_w, cos_p, sin_p, W, saved):
    (h2, proj, qkvc, q_a, k_a, v_a, gb, u, wk, qd, kd, qks, gl, s_prev, o_a, q_b, k_b, v_b, o_b, lse, mixed, y) = saved
    S = x1.shape[0]
    tm = _pick(S, (512, 256, 128))
    tv = _pick(S, (256, 128))
    ta = _pick(S, (512, 256, 128))
    nc = S // CHUNK
    G = {}
    (dy,), (G["g2"],) = _rowwise_bwd(lambda y_, g_: (g_ * y_,), [(y, tm, D_MODEL, 0)], [], [gate_w],
                                     [(d_out, tm, D_MODEL, 0)], S // tm, "mix_dres", row_dtypes=(BF16,))
    d_mixed = _mm(dy, W["wout"], "nt", name="mix_dout")
    G["wout"] = _mm(mixed, dy, "tn", name="mix_gwout")
    (do_a, dz, do_b), (G["gnw"], G["onw"]) = _rowwise_bwd(
        _mix_post_fn, [(o_a, tv, 512, 0), (proj, tv, 512, 3), (o_b, tv, 512, 0)], [], [W["gnw"], W["onw"]],
        [(d_mixed, tv, D_MODEL, 0)], S // tv, "mix_dpost")
    stats = _attn_stats(o_b, lse, do_b, ta, "mla_stats")
    dq_b, dk_b, dv_b = _attn_bwd(q_b, k_b, v_b, do_b, stats, ta, "mla_dattn")
    kab = (proj, tv, 128, 21)
    mla_params = [W["qnw"], W["kvnw"], W["wuq"], W["wukv"], W["qn_w"], W["qr_w"], W["kn_w"], W["kr_w"]]
    (d_ckv, d_cq, d_kab), mla_grads = _rowwise_bwd(
        _mla_pre_fn, [(proj, tv, 256, 8), (proj, tv, 384, 6), kab], [(cos_p, tv, 128, 0), (sin_p, tv, 128, 0)], mla_params,
        [(dq_b, tv, 1024, 0), (dk_b, tv, 1024, 0), (dv_b, tv, 512, 0)], S // tv, "mla_dpre")
    for key, g in zip(("qnw", "kvnw", "wuq", "wukv", "qn_w", "qr_w", "kn_w", "kr_w"), mla_grads):
        G[key] = g
    scan_grads = _gdn_scan_bwd(u, wk, qd, kd, qks, gl, s_prev, do_a, "gdn_dscan")
    ti = _pick(S, INTRA_ROWS)
    intra_douts = [(scan_grads[i], ti, 512, 0) for i in range(4)] + [(scan_grads[4 + i], ti, CHUNK, 0) for i in range(4)]
    intra_douts.append((scan_grads[8], ti // 8, 512, 0))
    (dq_a, dk_a, dv_a, d_gb), _ = _rowwise_bwd(
        _gdn_intra_fn, [(q_a, ti, 512, 0), (k_a, ti, 512, 0), (v_a, ti, 512, 0), (gb, ti, 128, 0)], [], [],
        intra_douts, S // ti, "gdn_dintra")
    (d_qkvc, d_kab), (G["alog_p"], G["dt_p"]) = _rowwise_bwd(
        _gdn_pre_fn, [(qkvc, tv, 1536, 0), kab], [], [W["alog_p"], W["dt_p"]],
        [(dq_a, tv, 512, 0), (dk_a, tv, 512, 0), (dv_a, tv, 512, 0), (d_gb, tv, 128, 0)], S // tv, "gdn_dpre",
        adds=[(1, d_kab)])
    d_qkv, g_conv = _conv_bwd(proj, d_qkvc, W["conv_w"], tv, "gdn_dconv")
    G["conv_w"] = g_conv[:4]
    d_proj = jnp.concatenate([d_qkv, dz, d_ckv, d_cq, d_kab], axis=1).astype(BF16)
    G["wp"] = _mm(h2, d_proj, "tn", name="mix_gwp")
    dx1, G["s2"], G["sh2"] = _dproj_dmod(d_proj, W["wp"], x1, d_out, scale, shift, "mix_dproj")
    return dx1, G


def _local_step(x, target, mod, cos_p, sin_p, W1, mixer_weights, ffn2_weights, ffn_grad_ready, mixer_grads_ready):
    sh1, s1, g1, sh2, s2, g2, sh3, s3, g3 = [mod[:, D_MODEL * i:D_MODEL * (i + 1)] for i in range(N_MOD)]
    x1, saved1 = _ffn_fwd(x, s1, sh1, g1, W1["f1_w8"], W1["f1_wo4"], "ffn1")
    W = mixer_weights(x1)
    x2, saved2 = _mixer_fwd(x1, s2, sh2, g2, cos_p, sin_p, W)
    W.update(ffn2_weights(x2))
    (dx3, loss_row), saved3 = _ffn_fwd(x2, s3, sh3, g3, W["f2_w8"], W["f2_wo4"], "ffn2", target=target)
    dx2, d_s3, d_sh3, d_g3 = _ffn_bwd(dx3, x2, s3, sh3, g3, W["f2_w8"], W["f2_wo4"], saved3, "ffn2", ffn_grad_ready("f2"))
    dx1, G = _mixer_bwd(dx2, x1, s2, sh2, g2, cos_p, sin_p, W, saved2)
    d_sh2, d_s2, d_g2 = G.pop("sh2"), G.pop("s2"), G.pop("g2")
    g1 = g1 + mixer_grads_ready(G)
    dx, d_s1, d_sh1, d_g1 = _ffn_bwd(dx1, x, s1, sh1, g1, W1["f1_w8"], W1["f1_wo4"], saved1, "ffn1", ffn_grad_ready("f1"))
    d_mod = jnp.concatenate([d_sh1, d_s1, d_g1, d_sh2, d_s2, d_g2, d_sh3, d_s3, d_g3], axis=1)
    return loss_row, dx, d_mod


WEIGHT_NAMES = ("w_ada", "b_ada", "ffn1_w_in", "ffn1_w_out", "w_in", "gdn_conv_w", "gdn_a_log", "gdn_dt_bias", "gdn_norm_w",
                "mla_q_norm_w", "mla_w_uq", "mla_kv_norm_w", "mla_w_ukv", "qkn_q_nope", "qkn_q_rope", "qkn_k_nope",
                "qkn_k_rope", "mla_out_norm_w", "w_out", "ffn2_w_in", "ffn2_w_out")
FFN_SHARDED = ("ffn1_w_in", "ffn1_w_out", "ffn2_w_in", "ffn2_w_out")
SHEETED = (("w_in", "col"), ("gdn_conv_w", "col"), ("mla_w_uq", "col"), ("mla_w_ukv", "col"), ("w_out", "row"))
MOD_ROWS = N_MOD * D_MODEL // 128
SMALL = {"gdn_a_log": (MOD_ROWS, 1, 64, 4), "gdn_dt_bias": (MOD_ROWS + 1, 1, 64, 4), "gdn_norm_w": (MOD_ROWS + 2, 1, 0, 128),
         "mla_q_norm_w": (MOD_ROWS + 3, 3, 0, 384), "mla_kv_norm_w": (MOD_ROWS + 6, 2, 0, 256),
         "qkn_q_nope": (MOD_ROWS + 8, 1, 0, 128), "qkn_q_rope": (MOD_ROWS + 9, 1, 0, 64), "qkn_k_nope": (MOD_ROWS + 10, 1, 0, 128),
         "qkn_k_rope": (MOD_ROWS + 11, 1, 0, 64), "mla_out_norm_w": (MOD_ROWS + 12, 1, 0, 128)}
LOSS_ROW = MOD_ROWS + 13
CONV_ROW, CONV_ROWS = 88, 4 * 1536 // 128
SHEET_ROWS = CONV_ROW + CONV_ROWS


def _to_sheet(flat, dtype, sublanes):
    n = flat.shape[-1]
    unit = sublanes * 128
    pad = (-n) % unit
    flat = jnp.pad(flat.astype(dtype), [(0, 0)] * (flat.ndim - 1) + [(0, pad)])
    return flat.reshape(flat.shape[:-1] + ((n + pad) // 128, 128))


def _small_sheet(b_like, small):
    sheet = jnp.zeros((SHEET_ROWS, 128), F32).at[:MOD_ROWS].set(b_like.reshape(MOD_ROWS, 128))
    for name, (row, rows, lane, n) in SMALL.items():
        v = small[name].reshape(1, n)
        if rows == 1:
            sheet = sheet.at[row, lane:lane + n].set(v[0])
        else:
            sheet = sheet.at[row:row + rows].set(v.reshape(rows, 128))
    return sheet


def _from_small_sheet(sheet):
    out = {"b_ada": sheet[:MOD_ROWS].reshape(1, N_MOD * D_MODEL)}
    for name, (row, rows, lane, n) in SMALL.items():
        out[name] = sheet[row, lane:lane + n].reshape(1, n) if rows == 1 else sheet[row:row + rows].reshape(1, n)
    return out


def kernel(x, c, positions, w_ada, b_ada, ffn1_w_in, ffn1_w_out, w_in, gdn_conv_w, gdn_a_log, gdn_dt_bias, gdn_norm_w, mla_q_norm_w, mla_w_uq, mla_kv_norm_w, mla_w_ukv, qkn_q_nope, qkn_q_rope, qkn_k_nope, qkn_k_rope, mla_out_norm_w, w_out, ffn2_w_in, ffn2_w_out, loss_target, m_w_ada, m_b_ada, m_ffn1_w_in, m_ffn1_w_out, m_w_in, m_gdn_conv_w, m_gdn_a_log, m_gdn_dt_bias, m_gdn_norm_w, m_mla_q_norm_w, m_mla_w_uq, m_mla_kv_norm_w, m_mla_w_ukv, m_qkn_q_nope, m_qkn_q_rope, m_qkn_k_nope, m_qkn_k_rope, m_mla_out_norm_w, m_w_out, m_ffn2_w_in, m_ffn2_w_out, v_w_ada, v_b_ada, v_ffn1_w_in, v_ffn1_w_out, v_w_in, v_gdn_conv_w, v_gdn_a_log, v_gdn_dt_bias, v_gdn_norm_w, v_mla_q_norm_w, v_mla_w_uq, v_mla_kv_norm_w, v_mla_w_ukv, v_qkn_q_nope, v_qkn_q_rope, v_qkn_k_nope, v_qkn_k_rope, v_mla_out_norm_w, v_w_out, v_ffn2_w_in, v_ffn2_w_out):
    args = locals()
    w = {n: args[n] for n in WEIGHT_NAMES}
    m = {n: args["m_" + n] for n in WEIGHT_NAMES}
    v = {n: args["v_" + n] for n in WEIGHT_NAMES}
    me = 4 * lax.axis_index("x") + 2 * lax.axis_index("y") + lax.axis_index("c")
    cols = N_MOD * D_MODEL // N_DEV
    shard = {n: w[n][0] for n in FFN_SHARDED + tuple(s[0] for s in SHEETED)}

    sc = c * _sigmoid(c)
    first = _to_sheet(jnp.concatenate([sc.reshape(-1), shard["gdn_conv_w"].reshape(-1)]), F32, 8)
    (first_all,) = _all_gather([first], "gather_c")
    sc_all = first_all[:, :D_MODEL // 128].reshape(N_DEV, D_MODEL)
    n_taps = shard["gdn_conv_w"].size
    conv_all = first_all.reshape(N_DEV, -1)[:, D_MODEL:D_MODEL + n_taps].reshape(N_DEV, 4, -1)
    b_mine = lax.dynamic_slice(b_ada, (0, me * cols), (1, cols))
    mod_cols = _mm(sc_all, w_ada[0], "nn", name="ada_mod", extra_params=[b_mine], epi=lambda acc, b_: (acc + b_,))
    (mod_all,) = _all_to_all([_to_sheet(mod_cols, F32, 8)], "scatter_mod")
    mod = mod_all.reshape(N_DEV, -1)[:, :cols].reshape(1, N_MOD * D_MODEL)

    f1_shards, mod = lax.optimization_barrier(([shard["ffn1_w_in"].astype(BF16), shard["ffn1_w_out"].astype(BF16)], mod))
    f1_w8, f1_out = _all_gather(f1_shards, "gather_w1")
    travel = [s for s in SHEETED if s[0] != "gdn_conv_w"]
    tied = lax.optimization_barrier(([shard[n].astype(BF16) for n, _ in travel], f1_w8))
    f1_w8 = tied[1]
    mixer_w = _exchange_start(tied[0], False, "gather_wm_start")
    ffn2_w = _exchange_start([shard["ffn2_w_in"].astype(BF16) + mixer_w[4][0:1, 0:1].astype(BF16),
                              shard["ffn2_w_out"].astype(BF16)], False, "gather_w2_start")
    mod = mod + ffn2_w[4][0:1, 0:1]
    W1 = dict(f1_w8=f1_w8, f1_wo4=f1_out.reshape(HID_PIECES, FFN_PIECE, D_MODEL))

    def mixer_weights(after):
        got = _exchange_wait(mixer_w, False, after, "gather_wm_wait")
        P = {n: jnp.concatenate(list(g), axis=1) if kind == "col" else g.reshape(-1, g.shape[-1])
             for (n, kind), g in zip(travel, got)}
        P["gdn_conv_w"] = jnp.concatenate(list(conv_all), axis=1)
        for n in SMALL:
            P[n] = w[n]
        return _pack_weights(P)

    def ffn2_weights(after):
        f2_w8, f2_out = _exchange_wait(ffn2_w, False, after, "gather_w2_wait")
        return dict(f2_w8=f2_w8, f2_wo4=f2_out.reshape(HID_PIECES, FFN_PIECE, D_MODEL))

    pending, small_grads = {}, {}

    def ffn_grad_ready(tag):
        def ready(which, g):
            pieces = g if which == "w8" else g.reshape((N_DEV,) + shard["ffn1_w_out"].shape)
            pending[tag + which] = _exchange_start([pieces], True, "scatter_%s_%s_start" % (tag, which))
            return pending[tag + which][4]
        return ready

    def mixer_grads_ready(G):
        g_full = _unpack_grads(G)
        small_grads.update({n: g_full[n] for n in SMALL})
        small_grads["gdn_conv_w"] = g_full["gdn_conv_w"]
        pieces = []
        for n, kind in travel:
            r, cc = shard[n].shape
            g = g_full[n].astype(BF16)
            pieces.append(jnp.stack([g[:, cc * p:cc * (p + 1)] for p in range(N_DEV)]) if kind == "col"
                          else g.reshape(N_DEV, r, cc))
        pending["mixer"] = _exchange_start(pieces, True, "scatter_mx_start")
        return pending["mixer"][4][0:1, 0:1]

    cos_p, sin_p = _rope_tables(positions[0])
    loss_row, dx, d_mod = _local_step(x[0], loss_target[0], mod, cos_p, sin_p, W1, mixer_weights, ffn2_weights,
                                      ffn_grad_ready, mixer_grads_ready)

    sheet = _small_sheet(d_mod, small_grads).at[LOSS_ROW].set(loss_row[0, :128])
    sheet = sheet.at[CONV_ROW:CONV_ROW + CONV_ROWS].set(small_grads["gdn_conv_w"].reshape(CONV_ROWS, 128))
    (sheets,) = _all_gather([sheet], "gather_small")
    summed = _sum_devices(sheets, "sum_small")
    d_mod_all = sheets[:, :MOD_ROWS].reshape(N_DEV, N_MOD * D_MODEL)
    d_mod_mine = lax.dynamic_slice(d_mod_all, (0, me * cols), (N_DEV, cols))
    grads = _from_small_sheet(summed)
    grads["w_ada"] = _mm(sc_all, d_mod_mine, "tn", name="ada_gw", hi=True)
    conv_taps = shard["gdn_conv_w"].shape[1]
    grads["gdn_conv_w"] = lax.dynamic_slice(summed[CONV_ROW:CONV_ROW + CONV_ROWS].reshape(4, -1), (0, me * conv_taps),
                                            (4, conv_taps))
    loss = summed[LOSS_ROW, 0]

    for n, key in zip(FFN_SHARDED, ("f1w8", "f1wo4", "f2w8", "f2wo4")):
        (parts,) = _exchange_wait(pending[key], True, summed, "scatter_%s_wait" % key)
        grads[n] = _sum_devices(parts, "sum_" + n)
    for (n, _), parts in zip(travel, _exchange_wait(pending["mixer"], True, summed, "scatter_mx_wait")):
        grads[n] = _sum_devices(parts, "sum_" + n)

    delta, new_m, new_v = {}, {}, {}
    for n in ("w_ada",) + FFN_SHARDED + tuple(s[0] for s in SHEETED):
        delta[n], new_m[n], new_v[n] = _adamw(w[n][0], grads[n], m[n][0], v[n][0], "adamw_" + n)
    small_in = [_small_sheet(t["b_ada"], t) for t in (w, grads, m, v)]
    for res, out in zip(_adamw(*small_in, "adamw_small"), (delta, new_m, new_v)):
        out.update(_from_small_sheet(res))

    def shaped(d):
        return [d[n].reshape(w[n].shape) for n in WEIGHT_NAMES]

    return (loss, dx[None], *shaped(grads), *shaped(delta), *shaped(new_m), *shaped(new_v))
```

```python
import functools

import jax
import jax.numpy as jnp
import numpy as np
from jax import lax
from jax.experimental import pallas as pl
from jax.experimental.pallas import tpu as pltpu

F32 = jnp.float32
BF16 = jnp.bfloat16

D_MODEL = 1024
D_FF = 2816
N_MOD = 9
HEADS = 4
HEAD_DIM = 128
CHUNK = 64
EPS = 1e-6
ROPE = 64
Q_LORA = 384
KV_LORA = 256
N_IN = 2760
N_IN_PACKED = 2816
ROPE_BASE = 10000.0
N_DEV = 8

ADAM_LR = 0.001
ADAM_B1 = 0.9
ADAM_B2 = 0.999
ADAM_EPS = 1e-08
ADAM_WD = 0.01
ADAM_STEP = 10

VMEM_LIMIT_BYTES = 56 * 1024 * 1024
MATMUL_ROWS = (1024, 512, 256, 128)
MESH = pl.DeviceIdType.MESH


def _params(sem=None):
    return pltpu.CompilerParams(dimension_semantics=sem, vmem_limit_bytes=VMEM_LIMIT_BYTES)


def _pick(dim, prefs):
    for p in prefs:
        if dim % p == 0:
            return p
    return dim


_DIMS = {"nn": (((1,), (0,)), ((), ())), "nt": (((1,), (1,)), ((), ())), "tn": (((0,), (0,)), ((), ()))}


def _dot_raw(a, b, mode):
    return lax.dot_general(a.astype(BF16), b.astype(BF16), _DIMS[mode], preferred_element_type=F32)


def _dot_hi(a, b, mode="nn"):
    return lax.dot_general(a, b, _DIMS[mode], precision=lax.Precision.HIGHEST, preferred_element_type=F32)


@functools.partial(jax.custom_vjp, nondiff_argnums=(2,))
def _bdot(a, b, mode):
    return _dot_raw(a, b, mode)


def _bdot_fwd(a, b, mode):
    return _dot_raw(a, b, mode), (a, b)


def _bdot_bwd(mode, res, g):
    a, b = res
    if mode == "nn":
        return _dot_raw(g, b, "nt"), _dot_raw(a, g, "tn")
    if mode == "nt":
        return _dot_raw(g, b, "nn"), _dot_raw(g, a, "tn")
    return _dot_raw(b, g, "nt"), _dot_raw(a, g, "nn")


_bdot.defvjp(_bdot_fwd, _bdot_bwd)


def _mm(a, b, mode, *, name, out_dtypes=(F32,), epi=None, extras=(), extra_params=(), hi=False,
        tm=None, tn=None, tk=None):
    if mode == "nn":
        (M, K), (_, N) = a.shape, b.shape
    elif mode == "nt":
        (M, K), (N, _) = a.shape, b.shape
    else:
        (K, M), (_, N) = a.shape, b.shape
    tm = tm or _pick(M, (512, 1408, 256, 128) if mode == "tn" else MATMUL_ROWS + (384, 352))
    tn = tn or _pick(N, (1024, 1408, 768, 512, 384, 256, 128))
    tk = tk or _pick(K, (1024, 1408, 512, 384, 256, 128))
    a_spec = {"nn": pl.BlockSpec((tm, tk), lambda i, j, k: (i, k)), "nt": pl.BlockSpec((tm, tk), lambda i, j, k: (i, k)),
              "tn": pl.BlockSpec((tk, tm), lambda i, j, k: (k, i))}[mode]
    b_spec = {"nn": pl.BlockSpec((tk, tn), lambda i, j, k: (k, j)), "nt": pl.BlockSpec((tn, tk), lambda i, j, k: (j, k)),
              "tn": pl.BlockSpec((tk, tn), lambda i, j, k: (k, j))}[mode]
    mn_spec = pl.BlockSpec((tm, tn), lambda i, j, k: (i, j))
    return _mmg(a, b, mode, name=name, grid=(M // tm, N // tn, K // tk), a_spec=a_spec, b_spec=b_spec, out_spec=mn_spec,
                out_shapes=[jax.ShapeDtypeStruct((M, N), dt) for dt in out_dtypes], acc_shape=(tm, tn), epi=epi,
                extras=list(extras) + list(extra_params),
                extra_specs=[mn_spec] * len(extras) + [pl.BlockSpec((1, tn), lambda i, j, k: (0, j))] * len(extra_params),
                hi=hi)


def _mmg(a, b, mode, *, name, grid, a_spec, b_spec, out_spec, out_shapes, acc_shape, epi=None, extras=(),
         extra_specs=(), hi=False):
    nk = grid[2]
    n_e, n_o = len(extras), len(out_shapes)

    def body(*refs):
        a_ref, b_ref = refs[:2]
        e_refs = refs[2:2 + n_e]
        o_refs = refs[2 + n_e:2 + n_e + n_o]
        acc_ref = refs[-1]
        k = pl.program_id(2)

        @pl.when(k == 0)
        def _():
            acc_ref[...] = jnp.zeros_like(acc_ref)

        if hi:
            acc_ref[...] += _dot_hi(a_ref[...].astype(F32), b_ref[...].astype(F32), mode)
        else:
            acc_ref[...] += _dot_raw(a_ref[...], b_ref[...], mode)

        @pl.when(k == nk - 1)
        def _():
            acc = acc_ref[...]
            outs = (acc,) if epi is None else epi(acc, *[e[...].astype(F32) for e in e_refs])
            for o_ref, o in zip(o_refs, outs):
                o_ref[...] = o.astype(o_ref.dtype)

    outs = pl.pallas_call(
        body, name=name, grid=grid,
        in_specs=[a_spec, b_spec] + list(extra_specs),
        out_specs=[out_spec] * n_o,
        out_shape=list(out_shapes),
        scratch_shapes=[pltpu.VMEM(acc_shape, F32)],
        compiler_params=_params(("parallel", "parallel", "arbitrary")),
    )(a, b, *extras)
    return outs if n_o > 1 else outs[0]


def _row_spec(th, cw, ci):
    return pl.BlockSpec((th, cw), lambda i: (i, ci))


def _full_spec(shape):
    return pl.BlockSpec(shape, lambda i: (0,) * len(shape))


def _rowwise(fn, rows, params, outs, n_steps, name):
    n_r, n_p, n_o = len(rows), len(params), len(outs)

    def body(*refs):
        vals = [r[...].astype(F32) for r in refs[:n_r + n_p]]
        res = fn(*vals)
        for o_ref, o in zip(refs[n_r + n_p:], res):
            o_ref[...] = o.astype(o_ref.dtype)

    across = [len(o) == 4 for o in outs]
    res = pl.pallas_call(
        body, name=name, grid=(n_steps,),
        in_specs=[_row_spec(th, cw, ci) for (_, th, cw, ci) in rows] + [_full_spec(p.shape) for p in params],
        out_specs=[pl.BlockSpec((o[0], o[1]), lambda i: (0, i)) if ac else _row_spec(o[0], o[1], 0)
                   for o, ac in zip(outs, across)],
        out_shape=[jax.ShapeDtypeStruct((o[0], n_steps * o[1]) if ac else (n_steps * o[0], o[1]), o[2])
                   for o, ac in zip(outs, across)],
        compiler_params=_params(("parallel",)),
    )(*[r[0] for r in rows], *params)
    return res


def _rowwise_bwd(fn, rows, aux, params, douts, n_steps, name, row_dtypes=None, adds=()):
    n_r, n_a, n_p, n_d, n_add = len(rows), len(aux), len(params), len(douts), len(adds)
    row_dtypes = row_dtypes or (F32,) * n_r

    def body(*refs):
        it = iter(refs)
        r_vals = [next(it)[...].astype(F32) for _ in range(n_r)]
        a_vals = [next(it)[...].astype(F32) for _ in range(n_a)]
        p_vals = [next(it)[...].astype(F32) for _ in range(n_p)]
        d_vals = [next(it)[...].astype(F32) for _ in range(n_d)]
        add_vals = [next(it)[...].astype(F32) for _ in range(n_add)]
        dr_refs = [next(it) for _ in range(n_r)]
        dp_refs = [next(it) for _ in range(n_p)]

        def f(*rp):
            return tuple(fn(*rp[:n_r], *a_vals, *rp[n_r:]))

        _, vjp = jax.vjp(f, *r_vals, *p_vals)
        grads = list(vjp(tuple(d_vals)))
        for (ri, _), av in zip(adds, add_vals):
            grads[ri] = grads[ri] + av
        for dr_ref, g in zip(dr_refs, grads[:n_r]):
            dr_ref[...] = g.astype(dr_ref.dtype)

        @pl.when(pl.program_id(0) == 0)
        def _():
            for dp_ref in dp_refs:
                dp_ref[...] = jnp.zeros_like(dp_ref)

        for dp_ref, g in zip(dp_refs, grads[n_r:]):
            dp_ref[...] += g

    all_rows = list(rows) + list(aux) + list(douts) + [(arr,) + tuple(rows[ri][1:3]) + (0,) for ri, arr in adds]
    in_specs = ([_row_spec(th, cw, ci) for (_, th, cw, ci) in list(rows) + list(aux)]
                + [_full_spec(p.shape) for p in params]
                + [_row_spec(th, cw, ci) for (_, th, cw, ci) in all_rows[n_r + n_a:]])
    res = pl.pallas_call(
        body, name=name, grid=(n_steps,),
        in_specs=in_specs,
        out_specs=[_row_spec(th, cw, 0) for (_, th, cw, _) in rows] + [_full_spec(p.shape) for p in params],
        out_shape=[jax.ShapeDtypeStruct((n_steps * th, cw), dt) for (_, th, cw, _), dt in zip(rows, row_dtypes)]
        + [jax.ShapeDtypeStruct(p.shape, F32) for p in params],
        compiler_params=_params(("arbitrary",)),
    )(*[r[0] for r in list(rows) + list(aux)], *params, *[r[0] for r in all_rows[n_r + n_a:]])
    return res[:n_r], res[n_r:]


def _sigmoid(x):
    return lax.logistic(x)


def _silu(x):
    return x * _sigmoid(x)


def _rms(x, w=None, n=None):
    n = n or x.shape[-1]
    y = x * lax.rsqrt(jnp.sum(x * x, axis=-1, keepdims=True) * (1.0 / n) + EPS)
    return y if w is None else y * w


def _modulate(x, scale, shift):
    return _rms(x) * (1.0 + scale) + shift


def _softplus(x):
    return jnp.maximum(x, 0.0) + jnp.log1p(jnp.exp(-jnp.abs(x)))


@jax.custom_vjp
def _rot_half64(x):
    lane = lax.broadcasted_iota(jnp.int32, x.shape, 1)
    up = pltpu.roll(x, 96, 1)
    down = pltpu.roll(x, 32, 1)
    return jnp.where(lane < 32, up, jnp.where(lane < 64, down, 0.0))


_rot_half64.defvjp(lambda x: (_rot_half64(x), None), lambda _, g: (_rot_half64(g),))


def _rope128(x, cos_p, sin_p):
    return x * cos_p + _rot_half64(x) * sin_p


def _gdn_pre_fn(qkvc, kab, alog_p, dt_p):
    a = _silu(qkvc)
    qs, ks = [], []
    for h in range(HEADS):
        qh = a[:, HEAD_DIM * h:HEAD_DIM * (h + 1)]
        kh = a[:, 512 + HEAD_DIM * h:512 + HEAD_DIM * (h + 1)]
        qs.append(qh * lax.rsqrt(jnp.sum(qh * qh, axis=-1, keepdims=True) + EPS) * (HEAD_DIM ** -0.5))
        ks.append(kh * lax.rsqrt(jnp.sum(kh * kh, axis=-1, keepdims=True) + EPS))
    lane = lax.broadcasted_iota(jnp.int32, kab.shape, 1)
    g_full = -jnp.exp(alog_p) * _softplus(kab + dt_p)
    b_full = _sigmoid(kab)
    gb = jnp.where((lane >= 64) & (lane < 68), g_full, jnp.where((lane >= 68) & (lane < 72), b_full, 0.0))
    return jnp.concatenate(qs, axis=1), jnp.concatenate(ks, axis=1), a[:, 1024:1536], gb


INTRA_ROWS = (256, 128, 64)

_BNN = (((2,), (1,)), ((0,), (0,)))
_BNT = (((2,), (2,)), ((0,), (0,)))


def _split_bf16(a):
    hi = a.astype(BF16)
    return hi, (a - hi.astype(F32)).astype(BF16)


def _dot3_raw(a, b, dims):
    a_hi, a_lo = _split_bf16(a)
    b_hi, b_lo = _split_bf16(b)
    dot = lambda x_, y_: lax.dot_general(x_, y_, dims, preferred_element_type=F32)
    return dot(a_hi, b_hi) + (dot(a_hi, b_lo) + dot(a_lo, b_hi))


@functools.partial(jax.custom_vjp, nondiff_argnums=(2, 3))
def _dot3(a, b, nt, exact_bwd=True):
    return _dot3_raw(a, b, _BNT if nt else _BNN)


def _dot3_fwd(a, b, nt, exact_bwd):
    return _dot3_raw(a, b, _BNT if nt else _BNN), (a, b)


def _dot3_bwd(nt, exact_bwd, res, g):
    a, b = res
    if exact_bwd:
        dot = _dot3_raw
    else:
        dot = lambda x_, y_, d_: lax.dot_general(x_.astype(BF16), y_.astype(BF16), d_, preferred_element_type=F32)
    if nt:
        return dot(g, b, _BNN), dot(jnp.swapaxes(g, 1, 2), a, _BNN)
    return dot(g, b, _BNT), dot(jnp.swapaxes(a, 1, 2), g, _BNN)


_dot3.defvjp(_dot3_fwd, _dot3_bwd)


@functools.partial(jax.custom_vjp, nondiff_argnums=(2,))
def _bdot_b(a, b, nt):
    return lax.dot_general(a.astype(BF16), b.astype(BF16), _BNT if nt else _BNN, preferred_element_type=F32)


def _bdot_b_fwd(a, b, nt):
    return _bdot_b(a, b, nt), (a, b)


def _bdot_b_bwd(nt, res, g):
    a, b = res
    dot = lambda x_, y_, d_: lax.dot_general(x_.astype(BF16), y_.astype(BF16), d_, preferred_element_type=F32)
    if nt:
        return dot(g, b, _BNN), dot(jnp.swapaxes(g, 1, 2), a, _BNN)
    return dot(g, b, _BNT), dot(jnp.swapaxes(a, 1, 2), g, _BNN)


_bdot_b.defvjp(_bdot_b_fwd, _bdot_b_bwd)


def _intra_batched(q, k, v, g_col, b_col):
    c = CHUNK
    nb = q.shape[0]
    row = lax.broadcasted_iota(jnp.int32, (1, c, c), 1)
    col = lax.broadcasted_iota(jnp.int32, (1, c, c), 2)
    incl, strict, eye = row >= col, row > col, row == col
    tri = jnp.broadcast_to(jnp.where(incl, 1.0, 0.0).astype(F32), (nb, c, c))
    ident = jnp.where(eye, 1.0, 0.0).astype(F32)
    g_wide = _dot3(tri, jnp.broadcast_to(g_col, (nb, c, HEAD_DIM)), False)
    g_i = g_wide[:, :, :c]
    g_j = jnp.sum(jnp.where(eye, g_i, 0.0), axis=1, keepdims=True)
    decay = jnp.where(incl, jnp.exp(jnp.where(incl, g_i - g_j, 0.0)), 0.0)
    kk = _bdot_b(k, k, True)
    a_mat = jnp.where(strict, b_col * kk * decay, 0.0)
    x_pow = -a_mat
    inv = ident + x_pow
    for _ in range(5):
        x_pow = _dot3(x_pow, x_pow, False, False)
        inv = inv + _dot3(inv, x_pow, False, False)
    e_wide = jnp.exp(g_wide)
    u = _dot3(inv, v * b_col, False)
    wk = _dot3(inv, k * b_col * e_wide, False)
    qk = _bdot_b(q, k, True) * decay
    last = lax.broadcasted_iota(jnp.int32, (1, c, HEAD_DIM), 1) == c - 1
    g_last = jnp.sum(jnp.where(last, g_wide, 0.0), axis=1, keepdims=True)
    qd = q * e_wide
    kd = k * jnp.exp(g_last - g_wide)
    gl = jnp.broadcast_to(jnp.exp(g_last), (nb, 8, HEAD_DIM))
    return u, wk, qd, kd, qk, gl


def _gdn_intra_fn(q, k, v, gb):
    t = q.shape[0]
    nch = t // CHUNK
    lane = lax.broadcasted_iota(jnp.int32, gb.shape, 1)

    def heads_first(x_):
        return jnp.concatenate([x_[:, HEAD_DIM * h:HEAD_DIM * (h + 1)].reshape(nch, CHUNK, HEAD_DIM) for h in range(HEADS)],
                               axis=0)

    def column(first_lane):
        return jnp.concatenate([jnp.sum(jnp.where(lane == first_lane + h, gb, 0.0), axis=1, keepdims=True)
                                .reshape(nch, CHUNK, 1) for h in range(HEADS)], axis=0)

    u, wk, qd, kd, qk, gl = _intra_batched(heads_first(q), heads_first(k), heads_first(v), column(64), column(68))

    def rows_first(x_):
        r, w_ = x_.shape[1], x_.shape[2]
        return jnp.concatenate([x_[nch * h:nch * (h + 1)].reshape(nch * r, w_) for h in range(HEADS)], axis=1)

    qks = [qk[nch * h:nch * (h + 1)].reshape(t, CHUNK) for h in range(HEADS)]
    return (rows_first(u), rows_first(wk), rows_first(qd), rows_first(kd), *qks, rows_first(gl))


def _scan_step(s0, u, wk, qd, kd, qk, gl):
    v_new = u - _bdot_b(wk, s0, False)
    o = _bdot_b(qd, s0, False) + _bdot_b(qk, v_new, False)
    s1 = s0 * gl[:, 0:1, :] + _bdot_b(jnp.swapaxes(kd, 1, 2), v_new, False)
    return o, s1


def _mix_post_fn(o_a, z, o_b, gnw, onw):
    parts = [_rms(o_a[:, HEAD_DIM * h:HEAD_DIM * (h + 1)], gnw) * _silu(z[:, HEAD_DIM * h:HEAD_DIM * (h + 1)])
             for h in range(HEADS)]
    parts += [_rms(o_b[:, HEAD_DIM * h:HEAD_DIM * (h + 1)], onw) for h in range(HEADS)]
    return (jnp.concatenate(parts, axis=1),)


def _mla_pre_fn(ckv, cq, kab, cos_p, sin_p, qnw, kvnw, wuq, wukv, qn_w, qr_w, kn_w, kr_w):
    scale = (HEAD_DIM + ROPE) ** -0.5
    qf = _bdot(_rms(cq, qnw), wuq, "nn")
    kvf = _bdot(_rms(ckv, kvnw), wukv, "nn")
    lane = lax.broadcasted_iota(jnp.int32, kab.shape, 1)
    kr = _rope128(_rms(jnp.where(lane < ROPE, kab, 0.0), kr_w, n=ROPE), cos_p, sin_p)
    qs, ks = [], []
    for h in range(HEADS):
        qn = _rms(qf[:, 256 * h:256 * h + 128], qn_w) * scale
        qr = _rope128(_rms(qf[:, 256 * h + 128:256 * h + 256], qr_w, n=ROPE), cos_p, sin_p) * scale
        qs += [qn, qr]
        ks += [_rms(kvf[:, 128 * h:128 * (h + 1)], kn_w), kr]
    return jnp.concatenate(qs, axis=1), jnp.concatenate(ks, axis=1), kvf[:, 512:]


def _conv_fwd(proj, conv_w, tm, name):
    S = proj.shape[0]
    C = 1536
    nb = tm // 8

    def body(x_ref, prev_ref, w_ref, o_ref, ext_ref):
        i = pl.program_id(0)
        ext_ref[0:8, :] = jnp.where(i > 0, prev_ref[...], 0.0)
        ext_ref[8:, :] = x_ref[...]
        acc = jnp.zeros((tm, C), F32)
        for k in range(4):
            acc = acc + w_ref[k:k + 1, :] * ext_ref[pl.ds(5 + k, tm), :]
        o_ref[...] = acc

    return pl.pallas_call(
        body, name=name, grid=(S // tm,),
        in_specs=[pl.BlockSpec((tm, C), lambda i: (i, 0)),
                  pl.BlockSpec((8, C), lambda i: (jnp.maximum(i * nb - 1, 0), 0)),
                  pl.BlockSpec((4, C), lambda i: (0, 0))],
        out_specs=pl.BlockSpec((tm, C), lambda i: (i, 0)),
        out_shape=jax.ShapeDtypeStruct((S, C), F32),
        scratch_shapes=[pltpu.VMEM((tm + 8, C), F32)],
        compiler_params=_params(("arbitrary",)),
    )(proj, proj, conv_w)


def _conv_bwd(proj, dout, conv_w, tm, name):
    S = proj.shape[0]
    C = 1536
    nb = tm // 8
    n_steps = S // tm

    def body(x_ref, prev_ref, d_ref, next_ref, w_ref, dx_ref, dw_ref, xext_ref, dext_ref):
        i = pl.program_id(0)
        xext_ref[0:8, :] = jnp.where(i > 0, prev_ref[...], 0.0)
        xext_ref[8:, :] = x_ref[...]
        dext_ref[0:tm, :] = d_ref[...]
        dext_ref[tm:, :] = jnp.where(i < n_steps - 1, next_ref[...], 0.0)
        d = d_ref[...]
        acc = jnp.zeros((tm, C), F32)
        dws = []
        for k in range(4):
            acc = acc + w_ref[k:k + 1, :] * dext_ref[pl.ds(3 - k, tm), :]
            dws.append(jnp.sum(d * xext_ref[pl.ds(5 + k, tm), :], axis=0, keepdims=True))
        dx_ref[...] = acc

        @pl.when(i == 0)
        def _():
            dw_ref[...] = jnp.zeros_like(dw_ref)

        dw_ref[...] += jnp.concatenate(dws + [jnp.zeros((4, C), F32)], axis=0)

    return pl.pallas_call(
        body, name=name, grid=(n_steps,),
        in_specs=[pl.BlockSpec((tm, C), lambda i: (i, 0)),
                  pl.BlockSpec((8, C), lambda i: (jnp.maximum(i * nb - 1, 0), 0)),
                  pl.BlockSpec((tm, C), lambda i: (i, 0)),
                  pl.BlockSpec((8, C), lambda i: (jnp.minimum((i + 1) * nb, S // 8 - 1), 0)),
                  pl.BlockSpec((4, C), lambda i: (0, 0))],
        out_specs=[pl.BlockSpec((tm, C), lambda i: (i, 0)), pl.BlockSpec((8, C), lambda i: (0, 0))],
        out_shape=[jax.ShapeDtypeStruct((S, C), F32), jax.ShapeDtypeStruct((8, C), F32)],
        scratch_shapes=[pltpu.VMEM((tm + 8, C), F32), pltpu.VMEM((tm + 8, C), F32)],
        compiler_params=_params(("arbitrary",)),
    )(proj, proj, dout, dout, conv_w)


SCAN_CHUNKS = (4, 2, 1)


def _gdn_scan_fwd(u, wk, qd, kd, qks, gl, name):
    S = u.shape[0]
    nc = S // CHUNK
    cs = _pick(nc, SCAN_CHUNKS)
    W = HEADS * HEAD_DIM

    def body(u_ref, wk_ref, qd_ref, kd_ref, qk0, qk1, qk2, qk3, gl_ref, o_ref, sp_ref, s_ref):
        @pl.when(pl.program_id(0) == 0)
        def _():
            s_ref[...] = jnp.zeros_like(s_ref)

        state = s_ref[...]
        for c in range(cs):
            rows, gl_rows = slice(CHUNK * c, CHUNK * (c + 1)), slice(8 * c, 8 * (c + 1))
            sp_ref[c] = state
            o, state = _scan_step(state, _heads(u_ref, HEAD_DIM, rows), _heads(wk_ref, HEAD_DIM, rows),
                                  _heads(qd_ref, HEAD_DIM, rows), _heads(kd_ref, HEAD_DIM, rows),
                                  jnp.stack([r[rows, :] for r in (qk0, qk1, qk2, qk3)]), _heads(gl_ref, HEAD_DIM, gl_rows))
            for h in range(HEADS):
                o_ref[rows, HEAD_DIM * h:HEAD_DIM * (h + 1)] = o[h]
        s_ref[...] = state

    row = pl.BlockSpec((cs * CHUNK, W), lambda n: (n, 0))
    qk_spec = pl.BlockSpec((cs * CHUNK, CHUNK), lambda n: (n, 0))
    return pl.pallas_call(
        body, name=name, grid=(nc // cs,),
        in_specs=[row, row, row, row, qk_spec, qk_spec, qk_spec, qk_spec, pl.BlockSpec((cs * 8, W), lambda n: (n, 0))],
        out_specs=[row, pl.BlockSpec((cs, HEADS, HEAD_DIM, HEAD_DIM), lambda n: (n, 0, 0, 0))],
        out_shape=[jax.ShapeDtypeStruct((S, W), F32), jax.ShapeDtypeStruct((nc, HEADS, HEAD_DIM, HEAD_DIM), F32)],
        scratch_shapes=[pltpu.VMEM((HEADS, HEAD_DIM, HEAD_DIM), F32)],
        compiler_params=_params(("arbitrary",)),
    )(u, wk, qd, kd, *qks, gl)


def _gdn_scan_bwd(u, wk, qd, kd, qks, gl, s_prev, d_o, name):
    S = u.shape[0]
    nc = S // CHUNK
    cs = _pick(nc, SCAN_CHUNKS)
    nb = nc // cs
    W = HEADS * HEAD_DIM

    def body(u_ref, wk_ref, qd_ref, kd_ref, qk0, qk1, qk2, qk3, gl_ref, sp_ref, do_ref,
             du_ref, dwk_ref, dqd_ref, dkd_ref, dqk0, dqk1, dqk2, dqk3, dgl_ref, ds_ref):
        @pl.when(pl.program_id(0) == 0)
        def _():
            ds_ref[...] = jnp.zeros_like(ds_ref)

        d_state = ds_ref[...]
        for c in reversed(range(cs)):
            rows, gl_rows = slice(CHUNK * c, CHUNK * (c + 1)), slice(8 * c, 8 * (c + 1))
            _, vjp = jax.vjp(_scan_step, sp_ref[c], _heads(u_ref, HEAD_DIM, rows), _heads(wk_ref, HEAD_DIM, rows),
                             _heads(qd_ref, HEAD_DIM, rows), _heads(kd_ref, HEAD_DIM, rows),
                             jnp.stack([r[rows, :] for r in (qk0, qk1, qk2, qk3)]), _heads(gl_ref, HEAD_DIM, gl_rows))
            d_state, du, dwk, dqd, dkd, dqk, dgl = vjp((_heads(do_ref, HEAD_DIM, rows), d_state))
            for h, dqk_ref in enumerate((dqk0, dqk1, dqk2, dqk3)):
                sl = slice(HEAD_DIM * h, HEAD_DIM * (h + 1))
                du_ref[rows, sl] = du[h]
                dwk_ref[rows, sl] = dwk[h]
                dqd_ref[rows, sl] = dqd[h]
                dkd_ref[rows, sl] = dkd[h]
                dqk_ref[rows, :] = dqk[h]
                dgl_ref[gl_rows, sl] = dgl[h]
        ds_ref[...] = d_state

    rev = lambda n: (nb - 1 - n, 0)
    row = pl.BlockSpec((cs * CHUNK, W), rev)
    qk_spec = pl.BlockSpec((cs * CHUNK, CHUNK), rev)
    gl_spec = pl.BlockSpec((cs * 8, W), rev)
    qk_shape = jax.ShapeDtypeStruct((S, CHUNK), F32)
    row_shape = jax.ShapeDtypeStruct((S, W), F32)
    return pl.pallas_call(
        body, name=name, grid=(nb,),
        in_specs=[row, row, row, row, qk_spec, qk_spec, qk_spec, qk_spec, gl_spec,
                  pl.BlockSpec((cs, HEADS, HEAD_DIM, HEAD_DIM), lambda n: (nb - 1 - n, 0, 0, 0)), row],
        out_specs=[row, row, row, row, qk_spec, qk_spec, qk_spec, qk_spec, gl_spec],
        out_shape=[row_shape] * 4 + [qk_shape] * 4 + [jax.ShapeDtypeStruct((nc * 8, W), F32)],
        scratch_shapes=[pltpu.VMEM((HEADS, HEAD_DIM, HEAD_DIM), F32)],
        compiler_params=_params(("arbitrary",)),
    )(u, wk, qd, kd, *qks, gl, s_prev, d_o)


NEG = -1e30


def _chunk_mask(i, j, t, transposed=False):
    q_axis, k_axis = (1, 0) if transposed else (0, 1)
    r = (i * t + lax.broadcasted_iota(jnp.int32, (t, t), q_axis)) // CHUNK
    c = (j * t + lax.broadcasted_iota(jnp.int32, (t, t), k_axis)) // CHUNK
    return c <= r


def _tile_pairs(n, by_key):
    pairs = [(i, j) for j in range(n) for i in range(j, n)] if by_key else [(i, j) for i in range(n) for j in range(i + 1)]
    return jnp.asarray(np.array([p[0] for p in pairs], np.int32)), jnp.asarray(np.array([p[1] for p in pairs], np.int32))


def _heads(ref, width, rows=slice(None)):
    return jnp.stack([ref[rows, width * h:width * (h + 1)] for h in range(HEADS)])


def _bmm(a, b, dims):
    return lax.dot_general(a.astype(BF16), b.astype(BF16), dims, preferred_element_type=F32)


def _attn_fwd(q, k, v_t, t, name):
    S = q.shape[0]
    n = S // t
    qi, kj = _tile_pairs(n, by_key=False)

    def body(qi_ref, kj_ref, q_ref, k_ref, vt_ref, o_ref, lse_ref, m_ref, l_ref, acc_ref):
        i, j = qi_ref[pl.program_id(0)], kj_ref[pl.program_id(0)]

        @pl.when(j == 0)
        def _():
            m_ref[...] = jnp.full_like(m_ref, NEG)
            l_ref[...] = jnp.zeros_like(l_ref)
            acc_ref[...] = jnp.zeros_like(acc_ref)

        def update(masked):
            s_t = _bmm(_heads(k_ref, 256), _heads(q_ref, 256), _BNT)
            if masked:
                s_t = jnp.where(_chunk_mask(i, j, t, transposed=True)[None], s_t, NEG)
            m_old = m_ref[...]
            m_new = jnp.maximum(m_old, jnp.max(s_t, axis=1, keepdims=True))
            p_t = jnp.exp(s_t - m_new)
            alpha = jnp.exp(m_old - m_new)
            l_ref[...] = alpha * l_ref[...] + jnp.sum(p_t, axis=1, keepdims=True)
            v_heads = jnp.stack([vt_ref[HEAD_DIM * h:HEAD_DIM * (h + 1), :] for h in range(HEADS)])
            acc_ref[...] = alpha * acc_ref[...] + _bmm(v_heads, p_t, _BNN)
            m_ref[...] = m_new

        @pl.when(j < i)
        def _():
            update(False)

        @pl.when(j == i)
        def _():
            update(True)
            for h in range(HEADS):
                sl = slice(HEAD_DIM * h, HEAD_DIM * (h + 1))
                o_ref[:, sl] = jnp.transpose(acc_ref[h] / l_ref[h])
                lse_ref[:, sl] = jnp.transpose(jnp.broadcast_to(m_ref[h] + jnp.log(l_ref[h]), (HEAD_DIM, t)))

    row = lambda p, qi_, kj_: (qi_[p], 0)
    return pl.pallas_call(
        body, name=name,
        grid_spec=pltpu.PrefetchScalarGridSpec(
            num_scalar_prefetch=2, grid=(qi.shape[0],),
            in_specs=[pl.BlockSpec((t, HEADS * 256), row), pl.BlockSpec((t, HEADS * 256), lambda p, qi_, kj_: (kj_[p], 0)),
                      pl.BlockSpec((HEADS * HEAD_DIM, t), lambda p, qi_, kj_: (0, kj_[p]))],
            out_specs=[pl.BlockSpec((t, HEADS * HEAD_DIM), row)] * 2,
            scratch_shapes=[pltpu.VMEM((HEADS, 1, t), F32), pltpu.VMEM((HEADS, 1, t), F32),
                            pltpu.VMEM((HEADS, HEAD_DIM, t), F32)]),
        out_shape=[jax.ShapeDtypeStruct((S, HEADS * HEAD_DIM), F32)] * 2,
        compiler_params=_params(("arbitrary",)),
    )(qi, kj, q, k, v_t)


def _attn_stats(o, lse, d_o, t, name):
    S = o.shape[0]

    def body(o_ref, lse_ref, do_ref, st_ref):
        lane = lax.broadcasted_iota(jnp.int32, (t, HEAD_DIM), 1)
        stats = jnp.zeros((t, HEAD_DIM), F32)
        for h in range(HEADS):
            sl = slice(HEAD_DIM * h, HEAD_DIM * (h + 1))
            delta = jnp.sum(do_ref[:, sl] * o_ref[:, sl], axis=1, keepdims=True)
            stats = stats + jnp.where(lane == h, lse_ref[:, sl], 0.0) + jnp.where(lane == HEADS + h, delta, 0.0)
        st_ref[...] = jnp.transpose(stats)[0:8, :]

    row = pl.BlockSpec((t, HEADS * HEAD_DIM), lambda i: (i, 0))
    return pl.pallas_call(
        body, name=name, grid=(S // t,),
        in_specs=[row, row, row], out_specs=pl.BlockSpec((8, t), lambda i: (0, i)),
        out_shape=jax.ShapeDtypeStruct((8, S), F32),
        compiler_params=_params(("parallel",)),
    )(o, lse, d_o)


BWD_GROUP = 2


def _attn_bwd(q, k, v, d_o, stats, t, name):
    S = q.shape[0]
    n = S // t
    groups = HEADS // BWD_GROUP
    gq, gv = BWD_GROUP * 256, BWD_GROUP * HEAD_DIM
    qi, kj = _tile_pairs(n, by_key=True)
    n_pairs = qi.shape[0]
    st = stats.reshape(2, groups, BWD_GROUP, S).transpose(1, 0, 2, 3).reshape(groups, 2 * BWD_GROUP, S)
    st = jnp.pad(st, ((0, 0), (0, 8 - 2 * BWD_GROUP), (0, 0)))

    def heads(ref, width, rows=slice(None)):
        return jnp.stack([ref[rows, width * h:width * (h + 1)] for h in range(BWD_GROUP)])

    def body(qi_ref, kj_ref, q_ref, k_ref, v_ref, do_ref, st_ref, dq_hbm, dk_ref, dv_ref, dq_acc, sem):
        g, p = pl.program_id(0), pl.program_id(1)
        i, j = qi_ref[p], kj_ref[p]

        @pl.when(i == j)
        def _():
            dk_ref[...] = jnp.zeros_like(dk_ref)
            dv_ref[...] = jnp.zeros_like(dv_ref)

        def update(masked):
            qh, kh = heads(q_ref, 256), heads(k_ref, 256)
            d_out = heads(do_ref, HEAD_DIM)
            stv = st_ref[...]
            lse_row = jnp.stack([stv[h:h + 1, :] for h in range(BWD_GROUP)])
            delta_row = jnp.stack([stv[BWD_GROUP + h:BWD_GROUP + h + 1, :] for h in range(BWD_GROUP)])
            s_t = _bmm(kh, qh, _BNT)
            p_t = jnp.exp(s_t - lse_row)
            if masked:
                p_t = jnp.where(_chunk_mask(i, j, t, transposed=True)[None], p_t, 0.0)
            dv = _bmm(p_t, d_out, _BNN)
            dp_t = _bmm(heads(v_ref, HEAD_DIM), d_out, _BNT)
            ds_t = p_t * (dp_t - delta_row)
            dk = _bmm(ds_t, qh, _BNN)
            dq = _bmm(jnp.swapaxes(ds_t, 1, 2), kh, _BNN)
            rows = pl.ds(pl.multiple_of(i * t, t), t)
            for h in range(BWD_GROUP):
                dk_ref[:, 256 * h:256 * (h + 1)] += dk[h]
                dv_ref[:, HEAD_DIM * h:HEAD_DIM * (h + 1)] += dv[h]

            @pl.when(j == 0)
            def _():
                for h in range(BWD_GROUP):
                    dq_acc[rows, 256 * h:256 * (h + 1)] = dq[h]

            @pl.when(j > 0)
            def _():
                for h in range(BWD_GROUP):
                    dq_acc[rows, 256 * h:256 * (h + 1)] += dq[h]

        @pl.when(i == j)
        def _():
            update(True)

        @pl.when(i > j)
        def _():
            update(False)

        @pl.when(p == n_pairs - 1)
        def _():
            for gg in range(groups):
                @pl.when(g == gg)
                def _():
                    cp = pltpu.make_async_copy(dq_acc, dq_hbm.at[:, gq * gg:gq * (gg + 1)], sem)
                    cp.start()
                    cp.wait()

    q_blk = lambda g, p, qi_, kj_: (qi_[p], g)
    k_blk = lambda g, p, qi_, kj_: (kj_[p], g)
    return pl.pallas_call(
        body, name=name,
        grid_spec=pltpu.PrefetchScalarGridSpec(
            num_scalar_prefetch=2, grid=(groups, n_pairs),
            in_specs=[pl.BlockSpec((t, gq), q_blk), pl.BlockSpec((t, gq), k_blk), pl.BlockSpec((t, gv), k_blk),
                      pl.BlockSpec((t, gv), q_blk), pl.BlockSpec((None, 8, t), lambda g, p, qi_, kj_: (g, 0, qi_[p]))],
            out_specs=[pl.BlockSpec(memory_space=pl.ANY), pl.BlockSpec((t, gq), k_blk), pl.BlockSpec((t, gv), k_blk)],
            scratch_shapes=[pltpu.VMEM((S, gq), F32), pltpu.SemaphoreType.DMA]),
        out_shape=[jax.ShapeDtypeStruct((S, HEADS * 256), F32), jax.ShapeDtypeStruct((S, HEADS * 256), F32),
                   jax.ShapeDtypeStruct((S, HEADS * HEAD_DIM), F32)],
        compiler_params=_params(("arbitrary", "arbitrary")),
    )(qi, kj, q, k, v, d_o, st)


FFN_PIECE = 2 * D_FF // N_DEV
HID_PIECES = D_FF // FFN_PIECE


def _ffn_up(h, w8, name):
    S = h.shape[0]
    tm = _pick(S, MATMUL_ROWS)

    def body(h_ref, wg_ref, wu_ref, g_ref, u_ref, hid_ref):
        gate = _dot_raw(h_ref[...], wg_ref[...], "nn")
        up = _dot_raw(h_ref[...], wu_ref[...], "nn")
        sg = _sigmoid(gate)
        act = gate * sg
        g_ref[...] = (up * (sg * (1.0 + gate * (1.0 - sg)))).astype(BF16)
        u_ref[...] = act.astype(BF16)
        hid_ref[...] = (act * up).astype(BF16)

    o_spec = pl.BlockSpec((None, tm, FFN_PIECE), lambda i, j: (j, i, 0))
    return pl.pallas_call(
        body, name=name, grid=(S // tm, HID_PIECES),
        in_specs=[pl.BlockSpec((tm, D_MODEL), lambda i, j: (i, 0)),
                  pl.BlockSpec((None, D_MODEL, FFN_PIECE), lambda i, j: (j, 0, 0)),
                  pl.BlockSpec((None, D_MODEL, FFN_PIECE), lambda i, j: (j + HID_PIECES, 0, 0))],
        out_specs=[o_spec] * 3,
        out_shape=[jax.ShapeDtypeStruct((HID_PIECES, S, FFN_PIECE), BF16)] * 3,
        compiler_params=_params(("parallel", "parallel")),
    )(h, w8, w8)


def _after_specs(after):
    return [] if after is None else [pl.BlockSpec(memory_space=pl.ANY)]


def _after_args(after):
    return [] if after is None else [after]


def _ffn_gw8(h, d_gate, d_up, name, after=None):
    S = h.shape[0]
    tm = 512
    tk = _pick(S, (512, 256, 128))
    nk = S // tk

    def body(h_ref, dg_ref, du_ref, *rest):
        o_ref, acc_ref = rest[-2:]
        k = pl.program_id(1)

        @pl.when(k == 0)
        def _():
            acc_ref[...] = jnp.zeros_like(acc_ref)

        h_t = jnp.transpose(h_ref[...])
        for p in range(HID_PIECES):
            acc_ref[p] += _dot_raw(h_t, dg_ref[p], "nn")
            acc_ref[HID_PIECES + p] += _dot_raw(h_t, du_ref[p], "nn")

        @pl.when(k == nk - 1)
        def _():
            o_ref[...] = acc_ref[...].astype(o_ref.dtype)

    d_spec = pl.BlockSpec((HID_PIECES, tk, FFN_PIECE), lambda i, k: (0, k, 0))
    return pl.pallas_call(
        body, name=name, grid=(D_MODEL // tm, nk),
        in_specs=[pl.BlockSpec((tk, tm), lambda i, k: (k, i)), d_spec, d_spec] + _after_specs(after),
        out_specs=pl.BlockSpec((2 * HID_PIECES, tm, FFN_PIECE), lambda i, k: (0, i, 0)),
        out_shape=jax.ShapeDtypeStruct((2 * HID_PIECES, D_MODEL, FFN_PIECE), BF16),
        scratch_shapes=[pltpu.VMEM((2 * HID_PIECES, tm, FFN_PIECE), F32)],
        compiler_params=_params(("parallel", "arbitrary")),
    )(h, d_gate, d_up, *_after_args(after))


def _gate_below(dx, y_ref, g_ref, coef, dy_ref, dg_ref):
    dy_ref[...] = (coef * g_ref[...] * dx).astype(dy_ref.dtype)
    dg_ref[...] += jnp.sum(coef * y_ref[...] * dx, axis=0, keepdims=True)


def _ffn_dh(d_gate, d_up, w8, x, d_out, scale, shift, name, after=None, below=None):
    S = d_gate.shape[1]
    tm = _pick(S, (512, 256, 128))
    n_below = 0 if below is None else 2

    def body(dg_ref, du_ref, wg_ref, wu_ref, x_ref, do_ref, sc_ref, sh_ref, *rest):
        below_in = rest[:n_below]
        outs = rest[len(rest) - 4 - n_below:]
        dx_ref, dsc_ref, dsh_ref = outs[:3]
        below_out, acc_ref = outs[3:3 + n_below], outs[-1]
        i, k = pl.program_id(0), pl.program_id(1)

        @pl.when(k == 0)
        def _():
            acc_ref[...] = jnp.zeros_like(acc_ref)

        acc_ref[...] += _dot_raw(dg_ref[...], wg_ref[...], "nt") + _dot_raw(du_ref[...], wu_ref[...], "nt")

        @pl.when((k == 0) & (i == 0))
        def _():
            for r in (dsc_ref, dsh_ref) + tuple(below_out[1:]):
                r[...] = jnp.zeros_like(r)

        @pl.when(k == HID_PIECES - 1)
        def _():
            _, vjp = jax.vjp(_modulate, x_ref[...], sc_ref[...], sh_ref[...])
            dx, dsc, dsh = vjp(acc_ref[...])
            dx = dx + do_ref[...]
            dx_ref[...] = dx
            dsc_ref[...] += dsc
            dsh_ref[...] += dsh
            if below is not None:
                _gate_below(dx, below_in[0], below_in[1], below[2], below_out[0], below_out[1])

    d_spec = pl.BlockSpec((None, tm, FFN_PIECE), lambda i, k: (k, i, 0))
    row = pl.BlockSpec((tm, D_MODEL), lambda i, k: (i, 0))
    par = pl.BlockSpec((1, D_MODEL), lambda i, k: (0, 0))
    row_shape, par_shape = jax.ShapeDtypeStruct((S, D_MODEL), F32), jax.ShapeDtypeStruct((1, D_MODEL), F32)
    return pl.pallas_call(
        body, name=name, grid=(S // tm, HID_PIECES),
        in_specs=[d_spec, d_spec,
                  pl.BlockSpec((None, D_MODEL, FFN_PIECE), lambda i, k: (k, 0, 0)),
                  pl.BlockSpec((None, D_MODEL, FFN_PIECE), lambda i, k: (k + HID_PIECES, 0, 0)),
                  row, row, par, par] + [row, par][:n_below] + _after_specs(after),
        out_specs=[row, par, par] + [row, par][:n_below],
        out_shape=[row_shape, par_shape, par_shape] + [jax.ShapeDtypeStruct((S, D_MODEL), BF16), par_shape][:n_below],
        scratch_shapes=[pltpu.VMEM((tm, D_MODEL), F32)],
        compiler_params=_params(("arbitrary", "arbitrary")),
    )(d_gate, d_up, w8, w8, x, d_out, scale, shift, *(below[:2] if below is not None else ()), *_after_args(after))


def _swiglu_bwd(d_hid, hid_by_gate, hid_by_up):
    return d_hid * hid_by_gate, d_hid * hid_by_up


def _adamw(w, g, m, v, name):
    R, C = w.shape
    tr = _pick(R, (256, 176, 128, 64, 32, 16, 8))

    def body(w_ref, g_ref, m_ref, v_ref, d_ref, nm_ref, nv_ref):
        g_ = g_ref[...]
        m_ = ADAM_B1 * m_ref[...] + (1.0 - ADAM_B1) * g_
        v_ = ADAM_B2 * v_ref[...] + (1.0 - ADAM_B2) * (g_ * g_)
        m_hat = m_ / (1.0 - ADAM_B1 ** ADAM_STEP)
        v_hat = v_ / (1.0 - ADAM_B2 ** ADAM_STEP)
        d_ref[...] = -ADAM_LR * (m_hat / (jnp.sqrt(v_hat) + ADAM_EPS) + ADAM_WD * w_ref[...])
        nm_ref[...] = m_
        nv_ref[...] = v_

    spec = pl.BlockSpec((tr, C), lambda i: (i, 0))
    return pl.pallas_call(
        body, name=name, grid=(R // tr,),
        in_specs=[spec] * 4, out_specs=[spec] * 3,
        out_shape=[jax.ShapeDtypeStruct((R, C), F32)] * 3,
        compiler_params=_params(("parallel",)),
    )(w, g, m, v)


def _sum_devices(parts, name):
    _, R, C = parts.shape
    tr = _pick(R, (512, 256, 176, 128, 64, 32, 16, 8))

    def body(p_ref, o_ref):
        acc = p_ref[0].astype(F32)
        for d in range(1, N_DEV):
            acc = acc + p_ref[d].astype(F32)
        o_ref[...] = acc

    return pl.pallas_call(
        body, name=name, grid=(R // tr,),
        in_specs=[pl.BlockSpec((N_DEV, tr, C), lambda i: (0, i, 0))],
        out_specs=pl.BlockSpec((tr, C), lambda i: (i, 0)),
        out_shape=jax.ShapeDtypeStruct((R, C), F32),
        compiler_params=_params(("parallel",)),
    )(parts)


def _my_place():
    return lax.axis_index("x"), lax.axis_index("y"), lax.axis_index("c")


def _all_gather(blocks, name):
    n = len(blocks)

    def body(*refs):
        x_refs, out_refs = refs[:n], refs[n:2 * n]
        send_sems, recv_sems, local_sems = refs[2 * n:]
        x, y, c = _my_place()
        me, sibling = (x, y, c), (x, y, 1 - c)
        chips = [(1 - x, y), (x, 1 - y), (1 - x, 1 - y)]

        def copy(a, k, blk, to, own=False):
            slot = out_refs[a].at[4 * blk[0] + 2 * blk[1] + blk[2]]
            return pltpu.make_async_remote_copy(
                src_ref=x_refs[a] if own else slot, dst_ref=slot,
                send_sem=send_sems.at[7 * a + k], recv_sem=recv_sems.at[7 * a + k], device_id=to, device_id_type=MESH)

        mine = [pltpu.make_async_copy(x_refs[a], out_refs[a].at[4 * x + 2 * y + c], local_sems.at[a]) for a in range(n)]
        for cp in mine:
            cp.start()
        first = []
        for j, chip in enumerate(chips):
            first += [copy(a, 1 + j, me, (*chip, c), own=True) for a in range(n)]
        first += [copy(a, 0, me, sibling, own=True) for a in range(n)]
        for cp in first:
            cp.start()
        passed = []
        for j, chip in enumerate(chips):
            for a in range(n):
                copy(a, 1 + j, (*chip, c), me).wait_recv()
                passed.append(copy(a, 4 + j, (*chip, c), sibling))
                passed[-1].start()
        for a in range(n):
            copy(a, 0, sibling, me).wait_recv()
        for j, chip in enumerate(chips):
            for a in range(n):
                copy(a, 4 + j, (*chip, 1 - c), me).wait_recv()
        for cp in first + passed:
            cp.wait_send()
        for cp in mine:
            cp.wait()

    return pl.pallas_call(
        body, name=name,
        out_shape=[jax.ShapeDtypeStruct((N_DEV,) + b.shape, b.dtype) for b in blocks],
        in_specs=[pl.BlockSpec(memory_space=pl.ANY)] * n,
        out_specs=[pl.BlockSpec(memory_space=pl.ANY)] * n,
        scratch_shapes=[pltpu.SemaphoreType.DMA((7 * n,)), pltpu.SemaphoreType.DMA((7 * n,)), pltpu.SemaphoreType.DMA((n,))],
    )(*blocks)


def _all_to_all(pieces, name):
    n = len(pieces)

    def body(*refs):
        x_refs, out_refs = refs[:n], refs[n:2 * n]
        send_sems, recv_sems, local_sems = refs[2 * n:]
        x, y, c = _my_place()
        me = 4 * x + 2 * y + c
        mine = [pltpu.make_async_copy(x_refs[a].at[me], out_refs[a].at[me], local_sems.at[a]) for a in range(n)]
        for cp in mine:
            cp.start()
        copies = []
        for k in (2, 4, 6, 3, 5, 7, 1):
            px = 1 - x if k & 4 else x
            py = 1 - y if k & 2 else y
            pc = 1 - c if k & 1 else c
            peer = 4 * px + 2 * py + pc
            for a in range(n):
                copies.append(pltpu.make_async_remote_copy(
                    src_ref=x_refs[a].at[peer], dst_ref=out_refs[a].at[me],
                    send_sem=send_sems.at[7 * a + k - 1], recv_sem=recv_sems.at[7 * a + k - 1],
                    device_id=(px, py, pc), device_id_type=MESH))
        for cp in copies:
            cp.start()
        for cp in copies:
            cp.wait_recv()
        for cp in copies:
            cp.wait_send()
        for cp in mine:
            cp.wait()

    return pl.pallas_call(
        body, name=name,
        out_shape=[jax.ShapeDtypeStruct(p.shape, p.dtype) for p in pieces],
        in_specs=[pl.BlockSpec(memory_space=pl.ANY)] * n,
        out_specs=[pl.BlockSpec(memory_space=pl.ANY)] * n,
        scratch_shapes=[pltpu.SemaphoreType.DMA((7 * n,)), pltpu.SemaphoreType.DMA((7 * n,)), pltpu.SemaphoreType.DMA((n,))],
    )(*pieces)


def _peers():
    x, y, c = _my_place()
    out = []
    for k in (2, 4, 6, 3, 5, 7, 1):
        px = 1 - x if k & 4 else x
        py = 1 - y if k & 2 else y
        pc = 1 - c if k & 1 else c
        out.append((k, (px, py, pc), 4 * px + 2 * py + pc))
    return out


def _exchange_copies(x_refs, land_refs, send_sems, recv_sems, scatter):
    x, y, c = _my_place()
    me = 4 * x + 2 * y + c
    starts, arrivals = [], []
    for k, place, peer in _peers():
        for a, (x_ref, land_ref) in enumerate(zip(x_refs, land_refs)):
            sems = dict(send_sem=send_sems.at[7 * a + k - 1], recv_sem=recv_sems.at[7 * a + k - 1],
                        device_id=place, device_id_type=MESH)
            src = x_ref.at[peer] if scatter else x_ref
            starts.append(pltpu.make_async_remote_copy(src_ref=src, dst_ref=land_ref.at[me], **sems))
            arrivals.append(pltpu.make_async_remote_copy(src_ref=src, dst_ref=land_ref.at[peer], **sems))
    return starts, arrivals


def _exchange_start(arrays, scatter, name):
    n = len(arrays)
    hbm = pl.BlockSpec(memory_space=pltpu.HBM)
    sem = pl.BlockSpec(memory_space=pltpu.SEMAPHORE)
    lands = [lax.empty(a.shape if scatter else (N_DEV,) + a.shape, a.dtype) for a in arrays]

    def body(*refs):
        x_refs, land_refs = refs[:n], refs[n:2 * n]
        send_sems, recv_sems = refs[2 * n], refs[2 * n + 1]
        token = refs[-1]
        starts, _ = _exchange_copies(x_refs, land_refs, send_sems, recv_sems, scatter)
        for cp in starts:
            cp.start()
        token[...] = jnp.zeros_like(token)

    res = pl.pallas_call(
        body, name=name,
        out_shape=(pltpu.SemaphoreType.DMA((7 * n,)), pltpu.SemaphoreType.DMA((7 * n,)),
                   *[pltpu.HBM(a.shape, a.dtype) for a in arrays], *[pltpu.HBM(l.shape, l.dtype) for l in lands],
                   jax.ShapeDtypeStruct((8, 128), F32)),
        in_specs=[hbm] * (2 * n),
        out_specs=(sem, sem, *[hbm] * (2 * n), pl.BlockSpec(memory_space=pltpu.VMEM)),
        input_output_aliases={i: 2 + i for i in range(2 * n)},
        compiler_params=pltpu.CompilerParams(has_side_effects=pltpu.SideEffectType.DATAFLOW_SIDE_EFFECTING),
    )(*[pltpu.with_memory_space_constraint(a, pltpu.HBM) for a in arrays],
      *[pltpu.with_memory_space_constraint(l, pltpu.HBM) for l in lands])
    return res[0], res[1], list(res[2:2 + n]), list(res[2 + n:2 + 2 * n]), res[-1]


def _exchange_wait(handles, scatter, after, name):
    send_sems, recv_sems, arrays, lands, _ = handles
    n = len(arrays)
    hbm = pl.BlockSpec(memory_space=pltpu.HBM)
    sem = pl.BlockSpec(memory_space=pltpu.SEMAPHORE)

    def body(*refs):
        x_refs, land_refs = refs[:n], refs[n:2 * n]
        send_s, recv_s = refs[2 * n], refs[2 * n + 1]
        starts, arrivals = _exchange_copies(x_refs, land_refs, send_s, recv_s, scatter)
        for cp in arrivals:
            cp.wait_recv()
        for cp in starts:
            cp.wait_send()

    res = pl.pallas_call(
        body, name=name,
        out_shape=(*[pltpu.HBM(a.shape, a.dtype) for a in arrays], *[pltpu.HBM(l.shape, l.dtype) for l in lands]),
        in_specs=[hbm] * (2 * n) + [sem, sem, pl.BlockSpec(memory_space=pl.ANY)],
        out_specs=tuple([hbm] * (2 * n)),
        input_output_aliases={i: i for i in range(2 * n)},
        compiler_params=pltpu.CompilerParams(has_side_effects=pltpu.SideEffectType.DATAFLOW_SIDE_EFFECTING),
    )(*arrays, *lands, send_sems, recv_sems, after)
    me = 4 * lax.axis_index("x") + 2 * lax.axis_index("y") + lax.axis_index("c")
    out = []
    for src, got in zip(res[:n], res[n:]):
        zeros = (0,) * (got.ndim - 1)
        own = lax.dynamic_slice(src, (me,) + zeros, (1,) + src.shape[1:]) if scatter else src[None]
        out.append(lax.dynamic_update_slice(got, own, (me,) + zeros))
    return out


def _pad_lanes(v, at=0, width=128):
    return jnp.pad(v, ((0, 0), (at, width - at - v.shape[1])))


def _pack_weights(P):
    W = {}
    w = P["w_in"]
    W["wp"] = jnp.concatenate([w[:, :2048], w[:, 2440:2696], w[:, 2056:2440], w[:, 2696:2760], w[:, 2048:2056],
                               jnp.zeros((D_MODEL, N_IN_PACKED - N_IN), w.dtype)], axis=1).astype(BF16)
    W["conv_w"] = P["gdn_conv_w"].astype(F32)
    W["alog_p"] = _pad_lanes(P["gdn_a_log"], 64)
    W["dt_p"] = _pad_lanes(P["gdn_dt_bias"], 64)
    W["gnw"] = P["gdn_norm_w"]
    W["qnw"] = P["mla_q_norm_w"]
    W["kvnw"] = P["mla_kv_norm_w"]
    uq = P["mla_w_uq"].reshape(Q_LORA, HEADS, HEAD_DIM + ROPE)
    W["wuq"] = jnp.pad(uq, ((0, 0), (0, 0), (0, 256 - HEAD_DIM - ROPE))).reshape(Q_LORA, HEADS * 256).astype(BF16)
    ukv = P["mla_w_ukv"].reshape(KV_LORA, HEADS, 2, HEAD_DIM)
    W["wukv"] = ukv.transpose(0, 2, 1, 3).reshape(KV_LORA, 2 * HEADS * HEAD_DIM).astype(BF16)
    W["qn_w"] = P["qkn_q_nope"]
    W["qr_w"] = _pad_lanes(P["qkn_q_rope"])
    W["kn_w"] = P["qkn_k_nope"]
    W["kr_w"] = _pad_lanes(P["qkn_k_rope"])
    W["onw"] = P["mla_out_norm_w"]
    W["wout"] = P["w_out"].astype(BF16)
    return W


def _unpack_grads(G):
    g = G["wp"]
    uq = G["wuq"].reshape(Q_LORA, HEADS, 256)[:, :, :HEAD_DIM + ROPE].reshape(Q_LORA, HEADS * (HEAD_DIM + ROPE))
    ukv = G["wukv"].reshape(KV_LORA, 2, HEADS, HEAD_DIM).transpose(0, 2, 1, 3).reshape(KV_LORA, 2 * HEADS * HEAD_DIM)
    return {
        "w_in": jnp.concatenate([g[:, :2048], g[:, 2752:2760], g[:, 2304:2688], g[:, 2048:2304], g[:, 2688:2752]], axis=1),
        "gdn_conv_w": G["conv_w"], "gdn_a_log": G["alog_p"][:, 64:68], "gdn_dt_bias": G["dt_p"][:, 64:68],
        "gdn_norm_w": G["gnw"], "mla_q_norm_w": G["qnw"], "mla_w_uq": uq, "mla_kv_norm_w": G["kvnw"], "mla_w_ukv": ukv,
        "qkn_q_nope": G["qn_w"], "qkn_q_rope": G["qr_w"][:, :ROPE], "qkn_k_nope": G["kn_w"], "qkn_k_rope": G["kr_w"][:, :ROPE],
        "mla_out_norm_w": G["onw"], "w_out": G["wout"],
    }


def _rope_tables(positions):
    half = ROPE // 2
    inv_freq = ROPE_BASE ** (-jnp.arange(half, dtype=F32) / half)
    ang = positions.astype(F32)[:, None] * inv_freq
    cos, sin = jnp.cos(ang), jnp.sin(ang)
    zeros = jnp.zeros((positions.shape[0], 128 - ROPE), F32)
    return jnp.concatenate([cos, cos, zeros], axis=1), jnp.concatenate([-sin, sin, zeros], axis=1)


def _mod_fn(x, scale, shift):
    return (_modulate(x, scale, shift),)


def _ffn_down_loss(hid, wo4, x, gate_w, target, name):
    S = x.shape[0]
    tb = _pick(S, (512, 256, 128))
    n = S // tb

    def body(hid_ref, wo_ref, x_ref, g_ref, t_ref, dx_ref, df_ref, dg_ref, l_ref, acc_ref):
        i, k = pl.program_id(0), pl.program_id(1)

        @pl.when(k == 0)
        def _():
            acc_ref[...] = jnp.zeros_like(acc_ref)

        acc_ref[...] += _dot_raw(hid_ref[...], wo_ref[...], "nn")

        @pl.when((k == 0) & (i == 0))
        def _():
            dg_ref[...] = jnp.zeros_like(dg_ref)
            l_ref[...] = jnp.zeros_like(l_ref)

        @pl.when(k == HID_PIECES - 1)
        def _():
            f = acc_ref[...]
            diff = x_ref[...] + 0.5 * g_ref[...] * f - t_ref[...]
            dx = diff * (1.0 / D_MODEL)
            dx_ref[...] = dx
            df_ref[...] = (0.5 * g_ref[...] * dx).astype(df_ref.dtype)
            dg_ref[...] += jnp.sum(0.5 * f * dx, axis=0, keepdims=True)
            l_ref[...] += jnp.sum(diff * diff, axis=0, keepdims=True)

        @pl.when((k == HID_PIECES - 1) & (i == n - 1))
        def _():
            l_ref[...] = jnp.full(l_ref.shape, (0.5 / D_MODEL) * jnp.sum(l_ref[...]), F32)

    row = pl.BlockSpec((tb, D_MODEL), lambda i, k: (i, 0))
    par = pl.BlockSpec((1, D_MODEL), lambda i, k: (0, 0))
    return pl.pallas_call(
        body, name=name, grid=(n, HID_PIECES),
        in_specs=[pl.BlockSpec((None, tb, FFN_PIECE), lambda i, k: (k, i, 0)),
                  pl.BlockSpec((None, FFN_PIECE, D_MODEL), lambda i, k: (k, 0, 0)), row, par, row],
        out_specs=[row, row, par, par],
        out_shape=[jax.ShapeDtypeStruct((S, D_MODEL), F32), jax.ShapeDtypeStruct((S, D_MODEL), BF16),
                   jax.ShapeDtypeStruct((1, D_MODEL), F32), jax.ShapeDtypeStruct((1, D_MODEL), F32)],
        scratch_shapes=[pltpu.VMEM((tb, D_MODEL), F32)],
        compiler_params=_params(("arbitrary", "arbitrary")),
    )(hid, wo4, x, gate_w, target)


def _ffn_fwd(x, scale, shift, gate_w, w8, wo4, tag, target=None):
    S = x.shape[0]
    tm = _pick(S, (512, 256, 128))
    (h,) = _rowwise(_mod_fn, [(x, tm, D_MODEL, 0)], [scale, shift], [(tm, D_MODEL, BF16)], S // tm, tag + "_mod")
    by_gate, by_up, hid = _ffn_up(h, w8, tag + "_up")
    if target is not None:
        dx_out, df, d_gate_w, loss_row = _ffn_down_loss(hid, wo4, x, gate_w, target, tag + "_down")
        return (dx_out, loss_row), (h, by_gate, by_up, hid, None, df, d_gate_w)
    tb = _pick(S, MATMUL_ROWS)
    mn = pl.BlockSpec((tb, D_MODEL), lambda i, j, k: (i, j))
    f, x_out = _mmg(hid, wo4, "nn", name=tag + "_down", grid=(S // tb, 1, HID_PIECES),
                    a_spec=pl.BlockSpec((None, tb, FFN_PIECE), lambda i, j, k: (k, i, 0)),
                    b_spec=pl.BlockSpec((None, FFN_PIECE, D_MODEL), lambda i, j, k: (k, 0, j)),
                    out_spec=mn, out_shapes=[jax.ShapeDtypeStruct((S, D_MODEL), F32)] * 2, acc_shape=(tb, D_MODEL),
                    extras=[x, gate_w], extra_specs=[mn, pl.BlockSpec((1, D_MODEL), lambda i, j, k: (0, j))],
                    epi=lambda acc, x_, g_: (acc, x_ + 0.5 * g_ * acc))
    return x_out, (h, by_gate, by_up, hid, f, None, None)


def _ffn_bwd(d_out, x, scale, shift, gate_w, w8, wo4, saved, tag, grad_ready, below=None):
    h, gate, up, hid, f, df, d_gate_w = saved
    S = x.shape[0]
    tm = _pick(S, (512, 256, 128))
    tk = _pick(S, (512, 256, 128))
    n = S // tm
    if df is None:
        (df,), (d_gate_w,) = _rowwise_bwd(lambda f_, g_: (0.5 * g_ * f_,), [(f, tm, D_MODEL, 0)], [], [gate_w],
                                          [(d_out, tm, D_MODEL, 0)], n, tag + "_dres", row_dtypes=(BF16,))
    tb = _pick(S, MATMUL_ROWS)
    piece = pl.BlockSpec((None, tb, FFN_PIECE), lambda i, j, k: (j, i, 0))
    d_gate, d_up = _mmg(df, wo4, "nt", name=tag + "_ddown", grid=(S // tb, HID_PIECES, 1),
                        a_spec=pl.BlockSpec((tb, D_MODEL), lambda i, j, k: (i, 0)),
                        b_spec=pl.BlockSpec((None, FFN_PIECE, D_MODEL), lambda i, j, k: (j, 0, 0)),
                        out_spec=piece, out_shapes=[jax.ShapeDtypeStruct((HID_PIECES, S, FFN_PIECE), BF16)] * 2,
                        acc_shape=(tb, FFN_PIECE), extras=[gate, up], extra_specs=[piece, piece], epi=_swiglu_bwd)
    g_wo4 = _mmg(hid, df, "tn", name=tag + "_gwo", grid=(HID_PIECES, 1, S // tk),
                 a_spec=pl.BlockSpec((None, tk, FFN_PIECE), lambda i, j, k: (i, k, 0)),
                 b_spec=pl.BlockSpec((tk, D_MODEL), lambda i, j, k: (k, j)),
                 out_spec=pl.BlockSpec((None, FFN_PIECE, D_MODEL), lambda i, j, k: (i, 0, j)),
                 out_shapes=[jax.ShapeDtypeStruct((HID_PIECES, FFN_PIECE, D_MODEL), BF16)], acc_shape=(FFN_PIECE, D_MODEL))
    g_w8 = _ffn_gw8(h, d_gate, d_up, tag + "_gw8", after=grad_ready("wo4", g_wo4))
    res = _ffn_dh(d_gate, d_up, w8, x, d_out, scale, shift, tag + "_dh", after=grad_ready("w8", g_w8), below=below)
    return (res[0], res[1], res[2], d_gate_w) + tuple(res[3:])


def _dproj_dmod(d_proj, wp, x, d_out, scale, shift, name, below=None):
    S, K = d_proj.shape
    tm = _pick(S, (512, 256, 128))
    tk = _pick(K, (1408, 512, 256, 128))
    nk = K // tk
    n_below = 0 if below is None else 2

    def body(dp_ref, w_ref, x_ref, do_ref, sc_ref, sh_ref, *rest):
        below_in = rest[:n_below]
        dx_ref, dsc_ref, dsh_ref = rest[n_below:n_below + 3]
        below_out, acc_ref = rest[n_below + 3:n_below + 3 + n_below], rest[-1]
        i, k = pl.program_id(0), pl.program_id(1)

        @pl.when(k == 0)
        def _():
            acc_ref[...] = jnp.zeros_like(acc_ref)

        acc_ref[...] += _dot_raw(dp_ref[...], w_ref[...], "nt")

        @pl.when((k == 0) & (i == 0))
        def _():
            for r in (dsc_ref, dsh_ref) + tuple(below_out[1:]):
                r[...] = jnp.zeros_like(r)

        @pl.when(k == nk - 1)
        def _():
            _, vjp = jax.vjp(_modulate, x_ref[...], sc_ref[...], sh_ref[...])
            dx, dsc, dsh = vjp(acc_ref[...])
            dx = dx + do_ref[...]
            dx_ref[...] = dx
            dsc_ref[...] += dsc
            dsh_ref[...] += dsh
            if below is not None:
                _gate_below(dx, below_in[0], below_in[1], below[2], below_out[0], below_out[1])

    row = pl.BlockSpec((tm, D_MODEL), lambda i, k: (i, 0))
    par = pl.BlockSpec((1, D_MODEL), lambda i, k: (0, 0))
    row_shape, par_shape = jax.ShapeDtypeStruct((S, D_MODEL), F32), jax.ShapeDtypeStruct((1, D_MODEL), F32)
    return pl.pallas_call(
        body, name=name, grid=(S // tm, nk),
        in_specs=[pl.BlockSpec((tm, tk), lambda i, k: (i, k)), pl.BlockSpec((D_MODEL, tk), lambda i, k: (0, k)),
                  row, row, par, par] + [row, par][:n_below],
        out_specs=[row, par, par] + [row, par][:n_below],
        out_shape=[row_shape, par_shape, par_shape] + [jax.ShapeDtypeStruct((S, D_MODEL), BF16), par_shape][:n_below],
        scratch_shapes=[pltpu.VMEM((tm, D_MODEL), F32)],
        compiler_params=_params(("arbitrary", "arbitrary")),
    )(d_proj, wp, x, d_out, scale, shift, *(below[:2] if below is not None else ()))


def _mixer_fwd(x1, scale, shift, gate_w, cos_p, sin_p, W):
    S = x1.shape[0]
    tm = _pick(S, (512, 256, 128))
    tv = _pick(S, (256, 128))
    ta = _pick(S, (512, 256, 128))
    nc = S // CHUNK
    (h2,) = _rowwise(_mod_fn, [(x1, tm, D_MODEL, 0)], [scale, shift], [(tm, D_MODEL, BF16)], S // tm, "mix_mod")
    proj = _mm(h2, W["wp"], "nn", name="mix_proj")
    qkvc = _conv_fwd(proj, W["conv_w"], tv, "gdn_conv")
    kab = (proj, tv, 128, 21)
    q_a, k_a, v_a, gb = _rowwise(_gdn_pre_fn, [(qkvc, tv, 1536, 0), kab], [W["alog_p"], W["dt_p"]],
                                 [(tv, 512, F32)] * 3 + [(tv, 128, F32)], S // tv, "gdn_pre")
    ti = _pick(S, INTRA_ROWS)
    intra = _rowwise(_gdn_intra_fn, [(q_a, ti, 512, 0), (k_a, ti, 512, 0), (v_a, ti, 512, 0), (gb, ti, 128, 0)],
                     [], [(ti, 512, F32)] * 4 + [(ti, CHUNK, F32)] * 4 + [(ti // 8, 512, F32)], S // ti, "gdn_intra")
    u, wk, qd, kd, qks, gl = intra[0], intra[1], intra[2], intra[3], tuple(intra[4:8]), intra[8]
    o_a, s_prev = _gdn_scan_fwd(u, wk, qd, kd, qks, gl, "gdn_scan")
    mla_params = [W["qnw"], W["kvnw"], W["wuq"], W["wukv"], W["qn_w"], W["qr_w"], W["kn_w"], W["kr_w"]]
    def mla_pre_with_vt(*a):
        q_, k_, v_ = _mla_pre_fn(*a)
        return q_, k_, v_, jnp.transpose(v_)

    q_b, k_b, v_b, vt_b = _rowwise(mla_pre_with_vt,
                                   [(proj, tv, 256, 8), (proj, tv, 384, 6), kab, (cos_p, tv, 128, 0), (sin_p, tv, 128, 0)],
                                   mla_params, [(tv, 1024, BF16), (tv, 1024, BF16), (tv, 512, BF16), (512, tv, BF16, "across")],
                                   S // tv, "mla_pre")
    o_b, lse = _attn_fwd(q_b, k_b, vt_b, ta, "mla_attn")
    (mixed,) = _rowwise(_mix_post_fn, [(o_a, tv, 512, 0), (proj, tv, 512, 3), (o_b, tv, 512, 0)], [W["gnw"], W["onw"]],
                        [(tv, D_MODEL, BF16)], S // tv, "mix_post")
    y, x2 = _mm(mixed, W["wout"], "nn", name="mix_out", out_dtypes=(F32, F32), extras=[x1], extra_params=[gate_w],
                epi=lambda acc, x_, g_: (acc, x_ + g_ * acc))
    saved = (h2, proj, qkvc, q_a, k_a, v_a, gb, u, wk, qd, kd, qks, gl, s_prev, o_a, q_b, k_b, v_b, o_b, lse, mixed, y)
    return x2, saved


def _mixer_bwd(d_out, dy, x1, scale, shift, cos_p, sin_p, W, saved, below):
    (h2, proj, qkvc, q_a, k_a, v_a, gb, u, wk, qd, kd, qks, gl, s_prev, o_a, q_b, k_b, v_b, o_b, lse, mixed, y) = saved
    S = x1.shape[0]
    tm = _pick(S, (512, 256, 128))
    tv = _pick(S, (256, 128))
    ta = _pick(S, (512, 256, 128))
    nc = S // CHUNK
    G = {}
    d_mixed = _mm(dy, W["wout"], "nt", name="mix_dout")
    G["wout"] = _mm(mixed, dy, "tn", name="mix_gwout")
    (do_a, dz, do_b), (G["gnw"], G["onw"]) = _rowwise_bwd(
        _mix_post_fn, [(o_a, tv, 512, 0), (proj, tv, 512, 3), (o_b, tv, 512, 0)], [], [W["gnw"], W["onw"]],
        [(d_mixed, tv, D_MODEL, 0)], S // tv, "mix_dpost")
    stats = _attn_stats(o_b, lse, do_b, ta, "mla_stats")
    dq_b, dk_b, dv_b = _attn_bwd(q_b, k_b, v_b, do_b, stats, ta, "mla_dattn")
    kab = (proj, tv, 128, 21)
    mla_params = [W["qnw"], W["kvnw"], W["wuq"], W["wukv"], W["qn_w"], W["qr_w"], W["kn_w"], W["kr_w"]]
    (d_ckv, d_cq, d_kab), mla_grads = _rowwise_bwd(
        _mla_pre_fn, [(proj, tv, 256, 8), (proj, tv, 384, 6), kab], [(cos_p, tv, 128, 0), (sin_p, tv, 128, 0)], mla_params,
        [(dq_b, tv, 1024, 0), (dk_b, tv, 1024, 0), (dv_b, tv, 512, 0)], S // tv, "mla_dpre")
    for key, g in zip(("qnw", "kvnw", "wuq", "wukv", "qn_w", "qr_w", "kn_w", "kr_w"), mla_grads):
        G[key] = g
    scan_grads = _gdn_scan_bwd(u, wk, qd, kd, qks, gl, s_prev, do_a, "gdn_dscan")
    ti = _pick(S, INTRA_ROWS)
    intra_douts = [(scan_grads[i], ti, 512, 0) for i in range(4)] + [(scan_grads[4 + i], ti, CHUNK, 0) for i in range(4)]
    intra_douts.append((scan_grads[8], ti // 8, 512, 0))
    (dq_a, dk_a, dv_a, d_gb), _ = _rowwise_bwd(
        _gdn_intra_fn, [(q_a, ti, 512, 0), (k_a, ti, 512, 0), (v_a, ti, 512, 0), (gb, ti, 128, 0)], [], [],
        intra_douts, S // ti, "gdn_dintra")
    (d_qkvc, d_kab), (G["alog_p"], G["dt_p"]) = _rowwise_bwd(
        _gdn_pre_fn, [(qkvc, tv, 1536, 0), kab], [], [W["alog_p"], W["dt_p"]],
        [(dq_a, tv, 512, 0), (dk_a, tv, 512, 0), (dv_a, tv, 512, 0), (d_gb, tv, 128, 0)], S // tv, "gdn_dpre",
        adds=[(1, d_kab)])
    d_qkv, g_conv = _conv_bwd(proj, d_qkvc, W["conv_w"], tv, "gdn_dconv")
    G["conv_w"] = g_conv[:4]
    d_proj = jnp.concatenate([d_qkv, dz, d_ckv, d_cq, d_kab], axis=1).astype(BF16)
    G["wp"] = _mm(h2, d_proj, "tn", name="mix_gwp")
    dx1, G["s2"], G["sh2"], d_below, dg_below = _dproj_dmod(d_proj, W["wp"], x1, d_out, scale, shift, "mix_dproj", below=below)
    return dx1, d_below, dg_below, G


def _local_step(x, target, mod, cos_p, sin_p, W1, mixer_weights, ffn2_weights, ffn_grad_ready, mixer_grads_ready):
    sh1, s1, g1, sh2, s2, g2, sh3, s3, g3 = [mod[:, D_MODEL * i:D_MODEL * (i + 1)] for i in range(N_MOD)]
    x1, saved1 = _ffn_fwd(x, s1, sh1, g1, W1["f1_w8"], W1["f1_wo4"], "ffn1")
    W = mixer_weights(x1)
    x2, saved2 = _mixer_fwd(x1, s2, sh2, g2, cos_p, sin_p, W)
    W.update(ffn2_weights(x2))
    (dx3, loss_row), saved3 = _ffn_fwd(x2, s3, sh3, g3, W["f2_w8"], W["f2_wo4"], "ffn2", target=target)
    dx2, d_s3, d_sh3, d_g3, dy, d_g2 = _ffn_bwd(dx3, x2, s3, sh3, g3, W["f2_w8"], W["f2_wo4"], saved3, "ffn2",
                                                ffn_grad_ready("f2"), below=(saved2[-1], g2, 1.0))
    dx1, df1, d_g1, G = _mixer_bwd(dx2, dy, x1, s2, sh2, cos_p, sin_p, W, saved2, below=(saved1[4], g1, 0.5))
    d_sh2, d_s2 = G.pop("sh2"), G.pop("s2")
    saved1 = saved1[:5] + (df1, d_g1 + mixer_grads_ready(G))
    dx, d_s1, d_sh1, d_g1 = _ffn_bwd(dx1, x, s1, sh1, g1, W1["f1_w8"], W1["f1_wo4"], saved1, "ffn1", ffn_grad_ready("f1"))
    d_mod = jnp.concatenate([d_sh1, d_s1, d_g1, d_sh2, d_s2, d_g2, d_sh3, d_s3, d_g3], axis=1)
    return loss_row, dx, d_mod


WEIGHT_NAMES = ("w_ada", "b_ada", "ffn1_w_in", "ffn1_w_out", "w_in", "gdn_conv_w", "gdn_a_log", "gdn_dt_bias", "gdn_norm_w",
                "mla_q_norm_w", "mla_w_uq", "mla_kv_norm_w", "mla_w_ukv", "qkn_q_nope", "qkn_q_rope", "qkn_k_nope",
                "qkn_k_rope", "mla_out_norm_w", "w_out", "ffn2_w_in", "ffn2_w_out")
FFN_SHARDED = ("ffn1_w_in", "ffn1_w_out", "ffn2_w_in", "ffn2_w_out")
SHEETED = (("w_in", "col"), ("gdn_conv_w", "col"), ("mla_w_uq", "col"), ("mla_w_ukv", "col"), ("w_out", "row"))
MOD_ROWS = N_MOD * D_MODEL // 128
SMALL = {"gdn_a_log": (MOD_ROWS, 1, 64, 4), "gdn_dt_bias": (MOD_ROWS + 1, 1, 64, 4), "gdn_norm_w": (MOD_ROWS + 2, 1, 0, 128),
         "mla_q_norm_w": (MOD_ROWS + 3, 3, 0, 384), "mla_kv_norm_w": (MOD_ROWS + 6, 2, 0, 256),
         "qkn_q_nope": (MOD_ROWS + 8, 1, 0, 128), "qkn_q_rope": (MOD_ROWS + 9, 1, 0, 64), "qkn_k_nope": (MOD_ROWS + 10, 1, 0, 128),
         "qkn_k_rope": (MOD_ROWS + 11, 1, 0, 64), "mla_out_norm_w": (MOD_ROWS + 12, 1, 0, 128)}
LOSS_ROW = MOD_ROWS + 13
CONV_ROW, CONV_ROWS = 88, 4 * 1536 // 128
SHEET_ROWS = CONV_ROW + CONV_ROWS


def _to_sheet(flat, dtype, sublanes):
    n = flat.shape[-1]
    unit = sublanes * 128
    pad = (-n) % unit
    flat = jnp.pad(flat.astype(dtype), [(0, 0)] * (flat.ndim - 1) + [(0, pad)])
    return flat.reshape(flat.shape[:-1] + ((n + pad) // 128, 128))


def _small_sheet(b_like, small):
    sheet = jnp.zeros((SHEET_ROWS, 128), F32).at[:MOD_ROWS].set(b_like.reshape(MOD_ROWS, 128))
    for name, (row, rows, lane, n) in SMALL.items():
        v = small[name].reshape(1, n)
        if rows == 1:
            sheet = sheet.at[row, lane:lane + n].set(v[0])
        else:
            sheet = sheet.at[row:row + rows].set(v.reshape(rows, 128))
    return sheet


def _from_small_sheet(sheet):
    out = {"b_ada": sheet[:MOD_ROWS].reshape(1, N_MOD * D_MODEL)}
    for name, (row, rows, lane, n) in SMALL.items():
        out[name] = sheet[row, lane:lane + n].reshape(1, n) if rows == 1 else sheet[row:row + rows].reshape(1, n)
    return out


def kernel(x, c, positions, w_ada, b_ada, ffn1_w_in, ffn1_w_out, w_in, gdn_conv_w, gdn_a_log, gdn_dt_bias, gdn_norm_w, mla_q_norm_w, mla_w_uq, mla_kv_norm_w, mla_w_ukv, qkn_q_nope, qkn_q_rope, qkn_k_nope, qkn_k_rope, mla_out_norm_w, w_out, ffn2_w_in, ffn2_w_out, loss_target, m_w_ada, m_b_ada, m_ffn1_w_in, m_ffn1_w_out, m_w_in, m_gdn_conv_w, m_gdn_a_log, m_gdn_dt_bias, m_gdn_norm_w, m_mla_q_norm_w, m_mla_w_uq, m_mla_kv_norm_w, m_mla_w_ukv, m_qkn_q_nope, m_qkn_q_rope, m_qkn_k_nope, m_qkn_k_rope, m_mla_out_norm_w, m_w_out, m_ffn2_w_in, m_ffn2_w_out, v_w_ada, v_b_ada, v_ffn1_w_in, v_ffn1_w_out, v_w_in, v_gdn_conv_w, v_gdn_a_log, v_gdn_dt_bias, v_gdn_norm_w, v_mla_q_norm_w, v_mla_w_uq, v_mla_kv_norm_w, v_mla_w_ukv, v_qkn_q_nope, v_qkn_q_rope, v_qkn_k_nope, v_qkn_k_rope, v_mla_out_norm_w, v_w_out, v_ffn2_w_in, v_ffn2_w_out):
    args = locals()
    w = {n: args[n] for n in WEIGHT_NAMES}
    m = {n: args["m_" + n] for n in WEIGHT_NAMES}
    v = {n: args["v_" + n] for n in WEIGHT_NAMES}
    me = 4 * lax.axis_index("x") + 2 * lax.axis_index("y") + lax.axis_index("c")
    cols = N_MOD * D_MODEL // N_DEV
    shard = {n: w[n][0] for n in FFN_SHARDED + tuple(s[0] for s in SHEETED)}

    sc = c * _sigmoid(c)
    first = _to_sheet(jnp.concatenate([sc.reshape(-1), shard["gdn_conv_w"].reshape(-1)]), F32, 8)
    (first_all,) = _all_gather([first], "gather_c")
    sc_all = first_all[:, :D_MODEL // 128].reshape(N_DEV, D_MODEL)
    n_taps = shard["gdn_conv_w"].size
    conv_all = first_all.reshape(N_DEV, -1)[:, D_MODEL:D_MODEL + n_taps].reshape(N_DEV, 4, -1)
    b_mine = lax.dynamic_slice(b_ada, (0, me * cols), (1, cols))
    mod_cols = _mm(sc_all, w_ada[0], "nn", name="ada_mod", extra_params=[b_mine], epi=lambda acc, b_: (acc + b_,))
    (mod_all,) = _all_to_all([_to_sheet(mod_cols, F32, 8)], "scatter_mod")
    mod = mod_all.reshape(N_DEV, -1)[:, :cols].reshape(1, N_MOD * D_MODEL)

    f1_shards, mod = lax.optimization_barrier(([shard["ffn1_w_in"].astype(BF16), shard["ffn1_w_out"].astype(BF16)], mod))
    f1_w8, f1_out = _all_gather(f1_shards, "gather_w1")
    travel = [s for s in SHEETED if s[0] != "gdn_conv_w"]
    tied = lax.optimization_barrier(([shard[n].astype(BF16) for n, _ in travel], f1_w8))
    f1_w8 = tied[1]
    mixer_w = _exchange_start(tied[0], False, "gather_wm_start")
    ffn2_w = _exchange_start([shard["ffn2_w_in"].astype(BF16) + mixer_w[4][0:1, 0:1].astype(BF16),
                              shard["ffn2_w_out"].astype(BF16)], False, "gather_w2_start")
    mod = mod + ffn2_w[4][0:1, 0:1]
    W1 = dict(f1_w8=f1_w8, f1_wo4=f1_out.reshape(HID_PIECES, FFN_PIECE, D_MODEL))

    def mixer_weights(after):
        got = _exchange_wait(mixer_w, False, after, "gather_wm_wait")
        P = {n: jnp.concatenate(list(g), axis=1) if kind == "col" else g.reshape(-1, g.shape[-1])
             for (n, kind), g in zip(travel, got)}
        P["gdn_conv_w"] = jnp.concatenate(list(conv_all), axis=1)
        for n in SMALL:
            P[n] = w[n]
        return _pack_weights(P)

    def ffn2_weights(after):
        f2_w8, f2_out = _exchange_wait(ffn2_w, False, after, "gather_w2_wait")
        return dict(f2_w8=f2_w8, f2_wo4=f2_out.reshape(HID_PIECES, FFN_PIECE, D_MODEL))

    pending, small_grads = {}, {}

    def ffn_grad_ready(tag):
        def ready(which, g):
            pieces = g if which == "w8" else g.reshape((N_DEV,) + shard["ffn1_w_out"].shape)
            pending[tag + which] = _exchange_start([pieces], True, "scatter_%s_%s_start" % (tag, which))
            return pending[tag + which][4]
        return ready

    def mixer_grads_ready(G):
        g_full = _unpack_grads(G)
        small_grads.update({n: g_full[n] for n in SMALL})
        small_grads["gdn_conv_w"] = g_full["gdn_conv_w"]
        pieces = []
        for n, kind in travel:
            r, cc = shard[n].shape
            g = g_full[n].astype(BF16)
            pieces.append(jnp.stack([g[:, cc * p:cc * (p + 1)] for p in range(N_DEV)]) if kind == "col"
                          else g.reshape(N_DEV, r, cc))
        pending["mixer"] = _exchange_start(pieces, True, "scatter_mx_start")
        return pending["mixer"][4][0:1, 0:1]

    cos_p, sin_p = _rope_tables(positions[0])
    loss_row, dx, d_mod = _local_step(x[0], loss_target[0], mod, cos_p, sin_p, W1, mixer_weights, ffn2_weights,
                                      ffn_grad_ready, mixer_grads_ready)

    sheet = _small_sheet(d_mod, small_grads).at[LOSS_ROW].set(loss_row[0, :128])
    sheet = sheet.at[CONV_ROW:CONV_ROW + CONV_ROWS].set(small_grads["gdn_conv_w"].reshape(CONV_ROWS, 128))
    (sheets,) = _all_gather([sheet], "gather_small")
    summed = _sum_devices(sheets, "sum_small")
    d_mod_all = sheets[:, :MOD_ROWS].reshape(N_DEV, N_MOD * D_MODEL)
    d_mod_mine = lax.dynamic_slice(d_mod_all, (0, me * cols), (N_DEV, cols))
    grads = _from_small_sheet(summed)
    grads["w_ada"] = _mm(sc_all, d_mod_mine, "tn", name="ada_gw", hi=True)
    conv_taps = shard["gdn_conv_w"].shape[1]
    grads["gdn_conv_w"] = lax.dynamic_slice(summed[CONV_ROW:CONV_ROW + CONV_ROWS].reshape(4, -1), (0, me * conv_taps),
                                            (4, conv_taps))
    loss = summed[LOSS_ROW, 0]

    for n, key in zip(FFN_SHARDED, ("f1w8", "f1wo4", "f2w8", "f2wo4")):
        (parts,) = _exchange_wait(pending[key], True, summed, "scatter_%s_wait" % key)
        grads[n] = _sum_devices(parts, "sum_" + n)
    for (n, _), parts in zip(travel, _exchange_wait(pending["mixer"], True, summed, "scatter_mx_wait")):
        grads[n] = _sum_devices(parts, "sum_" + n)

    delta, new_m, new_v = {}, {}, {}
    for n in ("w_ada",) + FFN_SHARDED + tuple(s[0] for s in SHEETED):
        delta[n], new_m[n], new_v[n] = _adamw(w[n][0], grads[n], m[n][0], v[n][0], "adamw_" + n)
    small_in = [_small_sheet(t["b_ada"], t) for t in (w, grads, m, v)]
    for res, out in zip(_adamw(*small_in, "adamw_small"), (delta, new_m, new_v)):
        out.update(_from_small_sheet(res))

    def shaped(d):
        return [d[n].reshape(w[n].shape) for n in WEIGHT_NAMES]

    return (loss, dx[None], *shaped(grads), *shaped(delta), *shaped(new_m), *shaped(new_v))
```

```python
import functools

import jax
import jax.numpy as jnp
import numpy as np
from jax import lax
from jax.experimental import pallas as pl
from jax.experimental.pallas import tpu as pltpu

F32 = jnp.float32
BF16 = jnp.bfloat16

D_MODEL = 1024
D_FF = 2816
N_MOD = 9
HEADS = 4
HEAD_DIM = 128
CHUNK = 64
EPS = 1e-6
ROPE = 64
Q_LORA = 384
KV_LORA = 256
N_IN = 2760
N_IN_PACKED = 2816
ROPE_BASE = 10000.0
N_DEV = 8

ADAM_LR = 0.001
ADAM_B1 = 0.9
ADAM_B2 = 0.999
ADAM_EPS = 1e-08
ADAM_WD = 0.01
ADAM_STEP = 10

VMEM_LIMIT_BYTES = 56 * 1024 * 1024
MATMUL_ROWS = (1024, 512, 256, 128)
MESH = pl.DeviceIdType.MESH


def _params(sem=None):
    return pltpu.CompilerParams(dimension_semantics=sem, vmem_limit_bytes=VMEM_LIMIT_BYTES)


def _pick(dim, prefs):
    for p in prefs:
        if dim % p == 0:
            return p
    return dim


_DIMS = {"nn": (((1,), (0,)), ((), ())), "nt": (((1,), (1,)), ((), ())), "tn": (((0,), (0,)), ((), ()))}


def _dot_raw(a, b, mode):
    return lax.dot_general(a.astype(BF16), b.astype(BF16), _DIMS[mode], preferred_element_type=F32)


def _dot_hi(a, b, mode="nn"):
    return lax.dot_general(a, b, _DIMS[mode], precision=lax.Precision.HIGHEST, preferred_element_type=F32)


@functools.partial(jax.custom_vjp, nondiff_argnums=(2,))
def _bdot(a, b, mode):
    return _dot_raw(a, b, mode)


def _bdot_fwd(a, b, mode):
    return _dot_raw(a, b, mode), (a, b)


def _bdot_bwd(mode, res, g):
    a, b = res
    if mode == "nn":
        return _dot_raw(g, b, "nt"), _dot_raw(a, g, "tn")
    if mode == "nt":
        return _dot_raw(g, b, "nn"), _dot_raw(g, a, "tn")
    return _dot_raw(b, g, "nt"), _dot_raw(a, g, "nn")


_bdot.defvjp(_bdot_fwd, _bdot_bwd)


def _mm(a, b, mode, *, name, out_dtypes=(F32,), epi=None, extras=(), extra_params=(), hi=False,
        tm=None, tn=None, tk=None):
    if mode == "nn":
        (M, K), (_, N) = a.shape, b.shape
    elif mode == "nt":
        (M, K), (N, _) = a.shape, b.shape
    else:
        (K, M), (_, N) = a.shape, b.shape
    tm = tm or _pick(M, (512, 1408, 256, 128) if mode == "tn" else MATMUL_ROWS + (384, 352))
    tn = tn or _pick(N, (1024, 1408, 768, 512, 384, 256, 128))
    tk = tk or _pick(K, (1024, 1408, 512, 384, 256, 128))
    a_spec = {"nn": pl.BlockSpec((tm, tk), lambda i, j, k: (i, k)), "nt": pl.BlockSpec((tm, tk), lambda i, j, k: (i, k)),
              "tn": pl.BlockSpec((tk, tm), lambda i, j, k: (k, i))}[mode]
    b_spec = {"nn": pl.BlockSpec((tk, tn), lambda i, j, k: (k, j)), "nt": pl.BlockSpec((tn, tk), lambda i, j, k: (j, k)),
              "tn": pl.BlockSpec((tk, tn), lambda i, j, k: (k, j))}[mode]
    mn_spec = pl.BlockSpec((tm, tn), lambda i, j, k: (i, j))
    return _mmg(a, b, mode, name=name, grid=(M // tm, N // tn, K // tk), a_spec=a_spec, b_spec=b_spec, out_spec=mn_spec,
                out_shapes=[jax.ShapeDtypeStruct((M, N), dt) for dt in out_dtypes], acc_shape=(tm, tn), epi=epi,
                extras=list(extras) + list(extra_params),
                extra_specs=[mn_spec] * len(extras) + [pl.BlockSpec((1, tn), lambda i, j, k: (0, j))] * len(extra_params),
                hi=hi)


def _mmg(a, b, mode, *, name, grid, a_spec, b_spec, out_spec, out_shapes, acc_shape, epi=None, extras=(),
         extra_specs=(), hi=False):
    nk = grid[2]
    n_e, n_o = len(extras), len(out_shapes)

    def body(*refs):
        a_ref, b_ref = refs[:2]
        e_refs = refs[2:2 + n_e]
        o_refs = refs[2 + n_e:2 + n_e + n_o]
        acc_ref = refs[-1]
        k = pl.program_id(2)

        @pl.when(k == 0)
        def _():
            acc_ref[...] = jnp.zeros_like(acc_ref)

        if hi:
            acc_ref[...] += _dot_hi(a_ref[...].astype(F32), b_ref[...].astype(F32), mode)
        else:
            acc_ref[...] += _dot_raw(a_ref[...], b_ref[...], mode)

        @pl.when(k == nk - 1)
        def _():
            acc = acc_ref[...]
            outs = (acc,) if epi is None else epi(acc, *[e[...].astype(F32) for e in e_refs])
            for o_ref, o in zip(o_refs, outs):
                o_ref[...] = o.astype(o_ref.dtype)

    outs = pl.pallas_call(
        body, name=name, grid=grid,
        in_specs=[a_spec, b_spec] + list(extra_specs),
        out_specs=[out_spec] * n_o,
        out_shape=list(out_shapes),
        scratch_shapes=[pltpu.VMEM(acc_shape, F32)],
        compiler_params=_params(("parallel", "parallel", "arbitrary")),
    )(a, b, *extras)
    return outs if n_o > 1 else outs[0]


def _row_spec(th, cw, ci):
    return pl.BlockSpec((th, cw), lambda i: (i, ci))


def _full_spec(shape):
    return pl.BlockSpec(shape, lambda i: (0,) * len(shape))


def _rowwise(fn, rows, params, outs, n_steps, name):
    n_r, n_p, n_o = len(rows), len(params), len(outs)

    def body(*refs):
        vals = [r[...].astype(F32) for r in refs[:n_r + n_p]]
        res = fn(*vals)
        for o_ref, o in zip(refs[n_r + n_p:], res):
            o_ref[...] = o.astype(o_ref.dtype)

    across = [len(o) == 4 for o in outs]
    res = pl.pallas_call(
        body, name=name, grid=(n_steps,),
        in_specs=[_row_spec(th, cw, ci) for (_, th, cw, ci) in rows] + [_full_spec(p.shape) for p in params],
        out_specs=[pl.BlockSpec((o[0], o[1]), lambda i: (0, i)) if ac else _row_spec(o[0], o[1], 0)
                   for o, ac in zip(outs, across)],
        out_shape=[jax.ShapeDtypeStruct((o[0], n_steps * o[1]) if ac else (n_steps * o[0], o[1]), o[2])
                   for o, ac in zip(outs, across)],
        compiler_params=_params(("parallel",)),
    )(*[r[0] for r in rows], *params)
    return res


def _rowwise_bwd(fn, rows, aux, params, douts, n_steps, name, row_dtypes=None, adds=()):
    n_r, n_a, n_p, n_d, n_add = len(rows), len(aux), len(params), len(douts), len(adds)
    row_dtypes = row_dtypes or (F32,) * n_r

    def body(*refs):
        it = iter(refs)
        r_vals = [next(it)[...].astype(F32) for _ in range(n_r)]
        a_vals = [next(it)[...].astype(F32) for _ in range(n_a)]
        p_vals = [next(it)[...].astype(F32) for _ in range(n_p)]
        d_vals = [next(it)[...].astype(F32) for _ in range(n_d)]
        add_vals = [next(it)[...].astype(F32) for _ in range(n_add)]
        dr_refs = [next(it) for _ in range(n_r)]
        dp_refs = [next(it) for _ in range(n_p)]

        def f(*rp):
            return tuple(fn(*rp[:n_r], *a_vals, *rp[n_r:]))

        _, vjp = jax.vjp(f, *r_vals, *p_vals)
        grads = list(vjp(tuple(d_vals)))
        for (ri, _), av in zip(adds, add_vals):
            grads[ri] = grads[ri] + av
        for dr_ref, g in zip(dr_refs, grads[:n_r]):
            dr_ref[...] = g.astype(dr_ref.dtype)

        @pl.when(pl.program_id(0) == 0)
        def _():
            for dp_ref in dp_refs:
                dp_ref[...] = jnp.zeros_like(dp_ref)

        for dp_ref, g in zip(dp_refs, grads[n_r:]):
            dp_ref[...] += g

    all_rows = list(rows) + list(aux) + list(douts) + [(arr,) + tuple(rows[ri][1:3]) + (0,) for ri, arr in adds]
    in_specs = ([_row_spec(th, cw, ci) for (_, th, cw, ci) in list(rows) + list(aux)]
                + [_full_spec(p.shape) for p in params]
                + [_row_spec(th, cw, ci) for (_, th, cw, ci) in all_rows[n_r + n_a:]])
    res = pl.pallas_call(
        body, name=name, grid=(n_steps,),
        in_specs=in_specs,
        out_specs=[_row_spec(th, cw, 0) for (_, th, cw, _) in rows] + [_full_spec(p.shape) for p in params],
        out_shape=[jax.ShapeDtypeStruct((n_steps * th, cw), dt) for (_, th, cw, _), dt in zip(rows, row_dtypes)]
        + [jax.ShapeDtypeStruct(p.shape, F32) for p in params],
        compiler_params=_params(("arbitrary",)),
    )(*[r[0] for r in list(rows) + list(aux)], *params, *[r[0] for r in all_rows[n_r + n_a:]])
    return res[:n_r], res[n_r:]


def _sigmoid(x):
    return lax.logistic(x)


def _silu(x):
    return x * _sigmoid(x)


def _rms(x, w=None, n=None):
    n = n or x.shape[-1]
    y = x * lax.rsqrt(jnp.sum(x * x, axis=-1, keepdims=True) * (1.0 / n) + EPS)
    return y if w is None else y * w


def _modulate(x, scale, shift):
    return _rms(x) * (1.0 + scale) + shift


def _softplus(x):
    return jnp.maximum(x, 0.0) + jnp.log1p(jnp.exp(-jnp.abs(x)))


@jax.custom_vjp
def _rot_half64(x):
    lane = lax.broadcasted_iota(jnp.int32, x.shape, 1)
    up = pltpu.roll(x, 96, 1)
    down = pltpu.roll(x, 32, 1)
    return jnp.where(lane < 32, up, jnp.where(lane < 64, down, 0.0))


_rot_half64.defvjp(lambda x: (_rot_half64(x), None), lambda _, g: (_rot_half64(g),))


def _rope128(x, cos_p, sin_p):
    return x * cos_p + _rot_half64(x) * sin_p


def _gdn_pre_fn(qkvc, kab, alog_p, dt_p):
    a = _silu(qkvc)
    qs, ks = [], []
    for h in range(HEADS):
        qh = a[:, HEAD_DIM * h:HEAD_DIM * (h + 1)]
        kh = a[:, 512 + HEAD_DIM * h:512 + HEAD_DIM * (h + 1)]
        qs.append(qh * lax.rsqrt(jnp.sum(qh * qh, axis=-1, keepdims=True) + EPS) * (HEAD_DIM ** -0.5))
        ks.append(kh * lax.rsqrt(jnp.sum(kh * kh, axis=-1, keepdims=True) + EPS))
    lane = lax.broadcasted_iota(jnp.int32, kab.shape, 1)
    g_full = -jnp.exp(alog_p) * _softplus(kab + dt_p)
    b_full = _sigmoid(kab)
    gb = jnp.where((lane >= 64) & (lane < 68), g_full, jnp.where((lane >= 68) & (lane < 72), b_full, 0.0))
    return jnp.concatenate(qs, axis=1), jnp.concatenate(ks, axis=1), a[:, 1024:1536], gb


INTRA_ROWS = (256, 128, 64)

_BNN = (((2,), (1,)), ((0,), (0,)))
_BNT = (((2,), (2,)), ((0,), (0,)))


def _split_bf16(a):
    hi = a.astype(BF16)
    return hi, (a - hi.astype(F32)).astype(BF16)


def _dot3_raw(a, b, dims):
    a_hi, a_lo = _split_bf16(a)
    b_hi, b_lo = _split_bf16(b)
    dot = lambda x_, y_: lax.dot_general(x_, y_, dims, preferred_element_type=F32)
    return dot(a_hi, b_hi) + (dot(a_hi, b_lo) + dot(a_lo, b_hi))


@functools.partial(jax.custom_vjp, nondiff_argnums=(2, 3))
def _dot3(a, b, nt, exact_bwd=True):
    return _dot3_raw(a, b, _BNT if nt else _BNN)


def _dot3_fwd(a, b, nt, exact_bwd):
    return _dot3_raw(a, b, _BNT if nt else _BNN), (a, b)


def _dot3_bwd(nt, exact_bwd, res, g):
    a, b = res
    if exact_bwd:
        dot = _dot3_raw
    else:
        dot = lambda x_, y_, d_: lax.dot_general(x_.astype(BF16), y_.astype(BF16), d_, preferred_element_type=F32)
    if nt:
        return dot(g, b, _BNN), dot(jnp.swapaxes(g, 1, 2), a, _BNN)
    return dot(g, b, _BNT), dot(jnp.swapaxes(a, 1, 2), g, _BNN)


_dot3.defvjp(_dot3_fwd, _dot3_bwd)


@functools.partial(jax.custom_vjp, nondiff_argnums=(2,))
def _bdot_b(a, b, nt):
    return lax.dot_general(a.astype(BF16), b.astype(BF16), _BNT if nt else _BNN, preferred_element_type=F32)


def _bdot_b_fwd(a, b, nt):
    return _bdot_b(a, b, nt), (a, b)


def _bdot_b_bwd(nt, res, g):
    a, b = res
    dot = lambda x_, y_, d_: lax.dot_general(x_.astype(BF16), y_.astype(BF16), d_, preferred_element_type=F32)
    if nt:
        return dot(g, b, _BNN), dot(jnp.swapaxes(g, 1, 2), a, _BNN)
    return dot(g, b, _BNT), dot(jnp.swapaxes(a, 1, 2), g, _BNN)


_bdot_b.defvjp(_bdot_b_fwd, _bdot_b_bwd)


def _intra_batched(q, k, v, g_col, b_col):
    c = CHUNK
    nb = q.shape[0]
    row = lax.broadcasted_iota(jnp.int32, (1, c, c), 1)
    col = lax.broadcasted_iota(jnp.int32, (1, c, c), 2)
    incl, strict, eye = row >= col, row > col, row == col
    tri = jnp.broadcast_to(jnp.where(incl, 1.0, 0.0).astype(F32), (nb, c, c))
    ident = jnp.where(eye, 1.0, 0.0).astype(F32)
    g_wide = _dot3(tri, jnp.broadcast_to(g_col, (nb, c, HEAD_DIM)), False)
    g_i = g_wide[:, :, :c]
    g_j = jnp.sum(jnp.where(eye, g_i, 0.0), axis=1, keepdims=True)
    decay = jnp.where(incl, jnp.exp(jnp.where(incl, g_i - g_j, 0.0)), 0.0)
    kk = _bdot_b(k, k, True)
    a_mat = jnp.where(strict, b_col * kk * decay, 0.0)
    x_pow = -a_mat
    inv = ident + x_pow
    for _ in range(5):
        x_pow = _dot3(x_pow, x_pow, False, False)
        inv = inv + _dot3(inv, x_pow, False, False)
    e_wide = jnp.exp(g_wide)
    u = _dot3(inv, v * b_col, False)
    wk = _dot3(inv, k * b_col * e_wide, False)
    qk = _bdot_b(q, k, True) * decay
    last = lax.broadcasted_iota(jnp.int32, (1, c, HEAD_DIM), 1) == c - 1
    g_last = jnp.sum(jnp.where(last, g_wide, 0.0), axis=1, keepdims=True)
    qd = q * e_wide
    kd = k * jnp.exp(g_last - g_wide)
    gl = jnp.broadcast_to(jnp.exp(g_last), (nb, 8, HEAD_DIM))
    return u, wk, qd, kd, qk, gl


def _gdn_intra_fn(q, k, v, gb):
    t = q.shape[0]
    nch = t // CHUNK
    lane = lax.broadcasted_iota(jnp.int32, gb.shape, 1)

    def heads_first(x_):
        return jnp.concatenate([x_[:, HEAD_DIM * h:HEAD_DIM * (h + 1)].reshape(nch, CHUNK, HEAD_DIM) for h in range(HEADS)],
                               axis=0)

    def column(first_lane):
        return jnp.concatenate([jnp.sum(jnp.where(lane == first_lane + h, gb, 0.0), axis=1, keepdims=True)
                                .reshape(nch, CHUNK, 1) for h in range(HEADS)], axis=0)

    u, wk, qd, kd, qk, gl = _intra_batched(heads_first(q), heads_first(k), heads_first(v), column(64), column(68))

    def rows_first(x_):
        r, w_ = x_.shape[1], x_.shape[2]
        return jnp.concatenate([x_[nch * h:nch * (h + 1)].reshape(nch * r, w_) for h in range(HEADS)], axis=1)

    qks = [qk[nch * h:nch * (h + 1)].reshape(t, CHUNK) for h in range(HEADS)]
    return (rows_first(u), rows_first(wk), rows_first(qd), rows_first(kd), *qks, rows_first(gl))


def _scan_step(s0, u, wk, qd, kd, qk, gl):
    v_new = u - _bdot_b(wk, s0, False)
    o = _bdot_b(qd, s0, False) + _bdot_b(qk, v_new, False)
    s1 = s0 * gl[:, 0:1, :] + _bdot_b(jnp.swapaxes(kd, 1, 2), v_new, False)
    return o, s1


def _mix_post_fn(o_a, z, o_b, gnw, onw):
    parts = [_rms(o_a[:, HEAD_DIM * h:HEAD_DIM * (h + 1)], gnw) * _silu(z[:, HEAD_DIM * h:HEAD_DIM * (h + 1)])
             for h in range(HEADS)]
    parts += [_rms(o_b[:, HEAD_DIM * h:HEAD_DIM * (h + 1)], onw) for h in range(HEADS)]
    return (jnp.concatenate(parts, axis=1),)


def _mla_pre_fn(ckv, cq, kab, cos_p, sin_p, qnw, kvnw, wuq, wukv, qn_w, qr_w, kn_w, kr_w):
    scale = (HEAD_DIM + ROPE) ** -0.5
    qf = _bdot(_rms(cq, qnw), wuq, "nn")
    kvf = _bdot(_rms(ckv, kvnw), wukv, "nn")
    lane = lax.broadcasted_iota(jnp.int32, kab.shape, 1)
    kr = _rope128(_rms(jnp.where(lane < ROPE, kab, 0.0), kr_w, n=ROPE), cos_p, sin_p)
    qs, ks = [], []
    for h in range(HEADS):
        qn = _rms(qf[:, 256 * h:256 * h + 128], qn_w) * scale
        qr = _rope128(_rms(qf[:, 256 * h + 128:256 * h + 256], qr_w, n=ROPE), cos_p, sin_p) * scale
        qs += [qn, qr]
        ks += [_rms(kvf[:, 128 * h:128 * (h + 1)], kn_w), kr]
    return jnp.concatenate(qs, axis=1), jnp.concatenate(ks, axis=1), kvf[:, 512:]


def _conv_fwd(proj, conv_w, tm, name):
    S = proj.shape[0]
    C = 1536
    nb = tm // 8

    def body(x_ref, prev_ref, w_ref, o_ref, ext_ref):
        i = pl.program_id(0)
        ext_ref[0:8, :] = jnp.where(i > 0, prev_ref[...], 0.0)
        ext_ref[8:, :] = x_ref[...]
        acc = jnp.zeros((tm, C), F32)
        for k in range(4):
            acc = acc + w_ref[k:k + 1, :] * ext_ref[pl.ds(5 + k, tm), :]
        o_ref[...] = acc

    return pl.pallas_call(
        body, name=name, grid=(S // tm,),
        in_specs=[pl.BlockSpec((tm, C), lambda i: (i, 0)),
                  pl.BlockSpec((8, C), lambda i: (jnp.maximum(i * nb - 1, 0), 0)),
                  pl.BlockSpec((4, C), lambda i: (0, 0))],
        out_specs=pl.BlockSpec((tm, C), lambda i: (i, 0)),
        out_shape=jax.ShapeDtypeStruct((S, C), F32),
        scratch_shapes=[pltpu.VMEM((tm + 8, C), F32)],
        compiler_params=_params(("arbitrary",)),
    )(proj, proj, conv_w)


def _conv_bwd(proj, dout, conv_w, tm, name):
    S = proj.shape[0]
    C = 1536
    nb = tm // 8
    n_steps = S // tm

    def body(x_ref, prev_ref, d_ref, next_ref, w_ref, dx_ref, dw_ref, xext_ref, dext_ref):
        i = pl.program_id(0)
        xext_ref[0:8, :] = jnp.where(i > 0, prev_ref[...], 0.0)
        xext_ref[8:, :] = x_ref[...]
        dext_ref[0:tm, :] = d_ref[...]
        dext_ref[tm:, :] = jnp.where(i < n_steps - 1, next_ref[...], 0.0)
        d = d_ref[...]
        acc = jnp.zeros((tm, C), F32)
        dws = []
        for k in range(4):
            acc = acc + w_ref[k:k + 1, :] * dext_ref[pl.ds(3 - k, tm), :]
            dws.append(jnp.sum(d * xext_ref[pl.ds(5 + k, tm), :], axis=0, keepdims=True))
        dx_ref[...] = acc

        @pl.when(i == 0)
        def _():
            dw_ref[...] = jnp.zeros_like(dw_ref)

        dw_ref[...] += jnp.concatenate(dws + [jnp.zeros((4, C), F32)], axis=0)

    return pl.pallas_call(
        body, name=name, grid=(n_steps,),
        in_specs=[pl.BlockSpec((tm, C), lambda i: (i, 0)),
                  pl.BlockSpec((8, C), lambda i: (jnp.maximum(i * nb - 1, 0), 0)),
                  pl.BlockSpec((tm, C), lambda i: (i, 0)),
                  pl.BlockSpec((8, C), lambda i: (jnp.minimum((i + 1) * nb, S // 8 - 1), 0)),
                  pl.BlockSpec((4, C), lambda i: (0, 0))],
        out_specs=[pl.BlockSpec((tm, C), lambda i: (i, 0)), pl.BlockSpec((8, C), lambda i: (0, 0))],
        out_shape=[jax.ShapeDtypeStruct((S, C), F32), jax.ShapeDtypeStruct((8, C), F32)],
        scratch_shapes=[pltpu.VMEM((tm + 8, C), F32), pltpu.VMEM((tm + 8, C), F32)],
        compiler_params=_params(("arbitrary",)),
    )(proj, proj, dout, dout, conv_w)


SCAN_CHUNKS = (4, 2, 1)


def _gdn_scan_fwd(u, wk, qd, kd, qks, gl, name):
    S = u.shape[0]
    nc = S // CHUNK
    cs = _pick(nc, SCAN_CHUNKS)
    W = HEADS * HEAD_DIM

    def body(u_ref, wk_ref, qd_ref, kd_ref, qk0, qk1, qk2, qk3, gl_ref, o_ref, sp_ref, s_ref):
        @pl.when(pl.program_id(0) == 0)
        def _():
            s_ref[...] = jnp.zeros_like(s_ref)

        state = s_ref[...]
        for c in range(cs):
            rows, gl_rows = slice(CHUNK * c, CHUNK * (c + 1)), slice(8 * c, 8 * (c + 1))
            sp_ref[c] = state
            o, state = _scan_step(state, _heads(u_ref, HEAD_DIM, rows), _heads(wk_ref, HEAD_DIM, rows),
                                  _heads(qd_ref, HEAD_DIM, rows), _heads(kd_ref, HEAD_DIM, rows),
                                  jnp.stack([r[rows, :] for r in (qk0, qk1, qk2, qk3)]), _heads(gl_ref, HEAD_DIM, gl_rows))
            for h in range(HEADS):
                o_ref[rows, HEAD_DIM * h:HEAD_DIM * (h + 1)] = o[h]
        s_ref[...] = state

    row = pl.BlockSpec((cs * CHUNK, W), lambda n: (n, 0))
    qk_spec = pl.BlockSpec((cs * CHUNK, CHUNK), lambda n: (n, 0))
    return pl.pallas_call(
        body, name=name, grid=(nc // cs,),
        in_specs=[row, row, row, row, qk_spec, qk_spec, qk_spec, qk_spec, pl.BlockSpec((cs * 8, W), lambda n: (n, 0))],
        out_specs=[row, pl.BlockSpec((cs, HEADS, HEAD_DIM, HEAD_DIM), lambda n: (n, 0, 0, 0))],
        out_shape=[jax.ShapeDtypeStruct((S, W), F32), jax.ShapeDtypeStruct((nc, HEADS, HEAD_DIM, HEAD_DIM), F32)],
        scratch_shapes=[pltpu.VMEM((HEADS, HEAD_DIM, HEAD_DIM), F32)],
        compiler_params=_params(("arbitrary",)),
    )(u, wk, qd, kd, *qks, gl)


def _gdn_scan_bwd(u, wk, qd, kd, qks, gl, s_prev, d_o, name):
    S = u.shape[0]
    nc = S // CHUNK
    cs = _pick(nc, SCAN_CHUNKS)
    nb = nc // cs
    W = HEADS * HEAD_DIM

    def body(u_ref, wk_ref, qd_ref, kd_ref, qk0, qk1, qk2, qk3, gl_ref, sp_ref, do_ref,
             du_ref, dwk_ref, dqd_ref, dkd_ref, dqk0, dqk1, dqk2, dqk3, dgl_ref, ds_ref):
        @pl.when(pl.program_id(0) == 0)
        def _():
            ds_ref[...] = jnp.zeros_like(ds_ref)

        d_state = ds_ref[...]
        for c in reversed(range(cs)):
            rows, gl_rows = slice(CHUNK * c, CHUNK * (c + 1)), slice(8 * c, 8 * (c + 1))
            _, vjp = jax.vjp(_scan_step, sp_ref[c], _heads(u_ref, HEAD_DIM, rows), _heads(wk_ref, HEAD_DIM, rows),
                             _heads(qd_ref, HEAD_DIM, rows), _heads(kd_ref, HEAD_DIM, rows),
                             jnp.stack([r[rows, :] for r in (qk0, qk1, qk2, qk3)]), _heads(gl_ref, HEAD_DIM, gl_rows))
            d_state, du, dwk, dqd, dkd, dqk, dgl = vjp((_heads(do_ref, HEAD_DIM, rows), d_state))
            for h, dqk_ref in enumerate((dqk0, dqk1, dqk2, dqk3)):
                sl = slice(HEAD_DIM * h, HEAD_DIM * (h + 1))
                du_ref[rows, sl] = du[h]
                dwk_ref[rows, sl] = dwk[h]
                dqd_ref[rows, sl] = dqd[h]
                dkd_ref[rows, sl] = dkd[h]
                dqk_ref[rows, :] = dqk[h]
                dgl_ref[gl_rows, sl] = dgl[h]
        ds_ref[...] = d_state

    rev = lambda n: (nb - 1 - n, 0)
    row = pl.BlockSpec((cs * CHUNK, W), rev)
    qk_spec = pl.BlockSpec((cs * CHUNK, CHUNK), rev)
    gl_spec = pl.BlockSpec((cs * 8, W), rev)
    qk_shape = jax.ShapeDtypeStruct((S, CHUNK), F32)
    row_shape = jax.ShapeDtypeStruct((S, W), F32)
    return pl.pallas_call(
        body, name=name, grid=(nb,),
        in_specs=[row, row, row, row, qk_spec, qk_spec, qk_spec, qk_spec, gl_spec,
                  pl.BlockSpec((cs, HEADS, HEAD_DIM, HEAD_DIM), lambda n: (nb - 1 - n, 0, 0, 0)), row],
        out_specs=[row, row, row, row, qk_spec, qk_spec, qk_spec, qk_spec, gl_spec],
        out_shape=[row_shape] * 4 + [qk_shape] * 4 + [jax.ShapeDtypeStruct((nc * 8, W), F32)],
        scratch_shapes=[pltpu.VMEM((HEADS, HEAD_DIM, HEAD_DIM), F32)],
        compiler_params=_params(("arbitrary",)),
    )(u, wk, qd, kd, *qks, gl, s_prev, d_o)


NEG = -1e30


def _chunk_mask(i, j, t, transposed=False):
    q_axis, k_axis = (1, 0) if transposed else (0, 1)
    r = (i * t + lax.broadcasted_iota(jnp.int32, (t, t), q_axis)) // CHUNK
    c = (j * t + lax.broadcasted_iota(jnp.int32, (t, t), k_axis)) // CHUNK
    return c <= r


def _tile_pairs(n, by_key):
    pairs = [(i, j) for j in range(n) for i in range(j, n)] if by_key else [(i, j) for i in range(n) for j in range(i + 1)]
    return jnp.asarray(np.array([p[0] for p in pairs], np.int32)), jnp.asarray(np.array([p[1] for p in pairs], np.int32))


def _heads(ref, width, rows=slice(None)):
    return jnp.stack([ref[rows, width * h:width * (h + 1)] for h in range(HEADS)])


def _bmm(a, b, dims):
    return lax.dot_general(a.astype(BF16), b.astype(BF16), dims, preferred_element_type=F32)


def _attn_fwd(q, k, v_t, t, name):
    S = q.shape[0]
    n = S // t
    qi, kj = _tile_pairs(n, by_key=False)

    def body(qi_ref, kj_ref, q_ref, k_ref, vt_ref, o_ref, lse_ref, m_ref, l_ref, acc_ref):
        i, j = qi_ref[pl.program_id(0)], kj_ref[pl.program_id(0)]

        @pl.when(j == 0)
        def _():
            m_ref[...] = jnp.full_like(m_ref, NEG)
            l_ref[...] = jnp.zeros_like(l_ref)
            acc_ref[...] = jnp.zeros_like(acc_ref)

        def update(masked):
            s_t = _bmm(_heads(k_ref, 256), _heads(q_ref, 256), _BNT)
            if masked:
                s_t = jnp.where(_chunk_mask(i, j, t, transposed=True)[None], s_t, NEG)
            m_old = m_ref[...]
            m_new = jnp.maximum(m_old, jnp.max(s_t, axis=1, keepdims=True))
            p_t = jnp.exp(s_t - m_new)
            alpha = jnp.exp(m_old - m_new)
            l_ref[...] = alpha * l_ref[...] + jnp.sum(p_t, axis=1, keepdims=True)
            v_heads = jnp.stack([vt_ref[HEAD_DIM * h:HEAD_DIM * (h + 1), :] for h in range(HEADS)])
            acc_ref[...] = alpha * acc_ref[...] + _bmm(v_heads, p_t, _BNN)
            m_ref[...] = m_new

        @pl.when(j < i)
        def _():
            update(False)

        @pl.when(j == i)
        def _():
            update(True)
            for h in range(HEADS):
                sl = slice(HEAD_DIM * h, HEAD_DIM * (h + 1))
                o_ref[:, sl] = jnp.transpose(acc_ref[h] / l_ref[h])
                lse_ref[:, sl] = jnp.transpose(jnp.broadcast_to(m_ref[h] + jnp.log(l_ref[h]), (HEAD_DIM, t)))

    row = lambda p, qi_, kj_: (qi_[p], 0)
    return pl.pallas_call(
        body, name=name,
        grid_spec=pltpu.PrefetchScalarGridSpec(
            num_scalar_prefetch=2, grid=(qi.shape[0],),
            in_specs=[pl.BlockSpec((t, HEADS * 256), row), pl.BlockSpec((t, HEADS * 256), lambda p, qi_, kj_: (kj_[p], 0)),
                      pl.BlockSpec((HEADS * HEAD_DIM, t), lambda p, qi_, kj_: (0, kj_[p]))],
            out_specs=[pl.BlockSpec((t, HEADS * HEAD_DIM), row)] * 2,
            scratch_shapes=[pltpu.VMEM((HEADS, 1, t), F32), pltpu.VMEM((HEADS, 1, t), F32),
                            pltpu.VMEM((HEADS, HEAD_DIM, t), F32)]),
        out_shape=[jax.ShapeDtypeStruct((S, HEADS * HEAD_DIM), F32)] * 2,
        compiler_params=_params(("arbitrary",)),
    )(qi, kj, q, k, v_t)


def _attn_stats(o, lse, d_o, t, name):
    S = o.shape[0]

    def body(o_ref, lse_ref, do_ref, st_ref):
        lane = lax.broadcasted_iota(jnp.int32, (t, HEAD_DIM), 1)
        stats = jnp.zeros((t, HEAD_DIM), F32)
        for h in range(HEADS):
            sl = slice(HEAD_DIM * h, HEAD_DIM * (h + 1))
            delta = jnp.sum(do_ref[:, sl] * o_ref[:, sl], axis=1, keepdims=True)
            stats = stats + jnp.where(lane == h, lse_ref[:, sl], 0.0) + jnp.where(lane == HEADS + h, delta, 0.0)
        st_ref[...] = jnp.transpose(stats)[0:8, :]

    row = pl.BlockSpec((t, HEADS * HEAD_DIM), lambda i: (i, 0))
    return pl.pallas_call(
        body, name=name, grid=(S // t,),
        in_specs=[row, row, row], out_specs=pl.BlockSpec((8, t), lambda i: (0, i)),
        out_shape=jax.ShapeDtypeStruct((8, S), F32),
        compiler_params=_params(("parallel",)),
    )(o, lse, d_o)


BWD_GROUP = 2


def _attn_bwd(q, k, v, d_o, stats, t, name):
    S = q.shape[0]
    n = S // t
    groups = HEADS // BWD_GROUP
    gq, gv = BWD_GROUP * 256, BWD_GROUP * HEAD_DIM
    qi, kj = _tile_pairs(n, by_key=True)
    n_pairs = qi.shape[0]
    st = stats.reshape(2, groups, BWD_GROUP, S).transpose(1, 0, 2, 3).reshape(groups, 2 * BWD_GROUP, S)
    st = jnp.pad(st, ((0, 0), (0, 8 - 2 * BWD_GROUP), (0, 0)))

    def heads(ref, width, rows=slice(None)):
        return jnp.stack([ref[rows, width * h:width * (h + 1)] for h in range(BWD_GROUP)])

    def body(qi_ref, kj_ref, q_ref, k_ref, v_ref, do_ref, st_ref, dq_hbm, dk_ref, dv_ref, dq_acc, sem):
        g, p = pl.program_id(0), pl.program_id(1)
        i, j = qi_ref[p], kj_ref[p]

        @pl.when(i == j)
        def _():
            dk_ref[...] = jnp.zeros_like(dk_ref)
            dv_ref[...] = jnp.zeros_like(dv_ref)

        def update(masked):
            qh, kh = heads(q_ref, 256), heads(k_ref, 256)
            d_out = heads(do_ref, HEAD_DIM)
            stv = st_ref[...]
            lse_row = jnp.stack([stv[h:h + 1, :] for h in range(BWD_GROUP)])
            delta_row = jnp.stack([stv[BWD_GROUP + h:BWD_GROUP + h + 1, :] for h in range(BWD_GROUP)])
            s_t = _bmm(kh, qh, _BNT)
            p_t = jnp.exp(s_t - lse_row)
            if masked:
                p_t = jnp.where(_chunk_mask(i, j, t, transposed=True)[None], p_t, 0.0)
            dv = _bmm(p_t, d_out, _BNN)
            dp_t = _bmm(heads(v_ref, HEAD_DIM), d_out, _BNT)
            ds_t = p_t * (dp_t - delta_row)
            dk = _bmm(ds_t, qh, _BNN)
            dq = _bmm(jnp.swapaxes(ds_t, 1, 2), kh, _BNN)
            rows = pl.ds(pl.multiple_of(i * t, t), t)
            for h in range(BWD_GROUP):
                dk_ref[:, 256 * h:256 * (h + 1)] += dk[h]
                dv_ref[:, HEAD_DIM * h:HEAD_DIM * (h + 1)] += dv[h]

            @pl.when(j == 0)
            def _():
                for h in range(BWD_GROUP):
                    dq_acc[rows, 256 * h:256 * (h + 1)] = dq[h]

            @pl.when(j > 0)
            def _():
                for h in range(BWD_GROUP):
                    dq_acc[rows, 256 * h:256 * (h + 1)] += dq[h]

        @pl.when(i == j)
        def _():
            update(True)

        @pl.when(i > j)
        def _():
            update(False)

        @pl.when(p == n_pairs - 1)
        def _():
            for gg in range(groups):
                @pl.when(g == gg)
                def _():
                    cp = pltpu.make_async_copy(dq_acc, dq_hbm.at[:, gq * gg:gq * (gg + 1)], sem)
                    cp.start()
                    cp.wait()

    q_blk = lambda g, p, qi_, kj_: (qi_[p], g)
    k_blk = lambda g, p, qi_, kj_: (kj_[p], g)
    return pl.pallas_call(
        body, name=name,
        grid_spec=pltpu.PrefetchScalarGridSpec(
            num_scalar_prefetch=2, grid=(groups, n_pairs),
            in_specs=[pl.BlockSpec((t, gq), q_blk), pl.BlockSpec((t, gq), k_blk), pl.BlockSpec((t, gv), k_blk),
                      pl.BlockSpec((t, gv), q_blk), pl.BlockSpec((None, 8, t), lambda g, p, qi_, kj_: (g, 0, qi_[p]))],
            out_specs=[pl.BlockSpec(memory_space=pl.ANY), pl.BlockSpec((t, gq), k_blk), pl.BlockSpec((t, gv), k_blk)],
            scratch_shapes=[pltpu.VMEM((S, gq), F32), pltpu.SemaphoreType.DMA]),
        out_shape=[jax.ShapeDtypeStruct((S, HEADS * 256), F32), jax.ShapeDtypeStruct((S, HEADS * 256), F32),
                   jax.ShapeDtypeStruct((S, HEADS * HEAD_DIM), F32)],
        compiler_params=_params(("arbitrary", "arbitrary")),
    )(qi, kj, q, k, v, d_o, st)


FFN_PIECE = 2 * D_FF // N_DEV
HID_PIECES = D_FF // FFN_PIECE


def _ffn_up(h, w8, name):
    S = h.shape[0]
    tm = _pick(S, MATMUL_ROWS)

    def body(h_ref, wg_ref, wu_ref, g_ref, u_ref, hid_ref, hid_t_ref):
        gate = _dot_raw(h_ref[...], wg_ref[...], "nn")
        up = _dot_raw(h_ref[...], wu_ref[...], "nn")
        sg = _sigmoid(gate)
        act = gate * sg
        hid = act * up
        g_ref[...] = (up * (sg * (1.0 + gate * (1.0 - sg)))).astype(BF16)
        u_ref[...] = act.astype(BF16)
        hid_ref[...] = hid.astype(BF16)
        hid_t_ref[...] = jnp.transpose(hid).astype(BF16)

    o_spec = pl.BlockSpec((None, tm, FFN_PIECE), lambda i, j: (j, i, 0))
    return pl.pallas_call(
        body, name=name, grid=(S // tm, HID_PIECES),
        in_specs=[pl.BlockSpec((tm, D_MODEL), lambda i, j: (i, 0)),
                  pl.BlockSpec((None, D_MODEL, FFN_PIECE), lambda i, j: (j, 0, 0)),
                  pl.BlockSpec((None, D_MODEL, FFN_PIECE), lambda i, j: (j + HID_PIECES, 0, 0))],
        out_specs=[o_spec] * 3 + [pl.BlockSpec((None, FFN_PIECE, tm), lambda i, j: (j, 0, i))],
        out_shape=[jax.ShapeDtypeStruct((HID_PIECES, S, FFN_PIECE), BF16)] * 3
        + [jax.ShapeDtypeStruct((HID_PIECES, FFN_PIECE, S), BF16)],
        compiler_params=_params(("parallel", "parallel")),
    )(h, w8, w8)


def _after_specs(after):
    return [] if after is None else [pl.BlockSpec(memory_space=pl.ANY)]


def _after_args(after):
    return [] if after is None else [after]


def _ffn_gw8(h, d_gate, d_up, name, after=None):
    S = h.shape[0]
    tm = 512
    tk = _pick(S, (512, 256, 128))
    nk = S // tk

    def body(h_ref, dg_ref, du_ref, *rest):
        o_ref, acc_ref = rest[-2:]
        k = pl.program_id(1)

        @pl.when(k == 0)
        def _():
            acc_ref[...] = jnp.zeros_like(acc_ref)

        h_t = jnp.transpose(h_ref[...])
        for p in range(HID_PIECES):
            acc_ref[p] += _dot_raw(h_t, dg_ref[p], "nn")
            acc_ref[HID_PIECES + p] += _dot_raw(h_t, du_ref[p], "nn")

        @pl.when(k == nk - 1)
        def _():
            o_ref[...] = acc_ref[...].astype(o_ref.dtype)

    d_spec = pl.BlockSpec((HID_PIECES, tk, FFN_PIECE), lambda i, k: (0, k, 0))
    return pl.pallas_call(
        body, name=name, grid=(D_MODEL // tm, nk),
        in_specs=[pl.BlockSpec((tk, tm), lambda i, k: (k, i)), d_spec, d_spec] + _after_specs(after),
        out_specs=pl.BlockSpec((2 * HID_PIECES, tm, FFN_PIECE), lambda i, k: (0, i, 0)),
        out_shape=jax.ShapeDtypeStruct((2 * HID_PIECES, D_MODEL, FFN_PIECE), BF16),
        scratch_shapes=[pltpu.VMEM((2 * HID_PIECES, tm, FFN_PIECE), F32)],
        compiler_params=_params(("parallel", "arbitrary")),
    )(h, d_gate, d_up, *_after_args(after))


def _gate_below(dx, y_ref, g_ref, coef, dy_ref, dg_ref):
    dy_ref[...] = (coef * g_ref[...] * dx).astype(dy_ref.dtype)
    dg_ref[...] += jnp.sum(coef * y_ref[...] * dx, axis=0, keepdims=True)


def _ffn_dh(d_gate, d_up, w8, x, d_out, scale, shift, name, after=None, below=None):
    S = d_gate.shape[1]
    tm = _pick(S, (512, 256, 128))
    n_below = 0 if below is None else 2

    def body(dg_ref, du_ref, wg_ref, wu_ref, x_ref, do_ref, sc_ref, sh_ref, *rest):
        below_in = rest[:n_below]
        outs = rest[len(rest) - 4 - n_below:]
        dx_ref, dsc_ref, dsh_ref = outs[:3]
        below_out, acc_ref = outs[3:3 + n_below], outs[-1]
        i, k = pl.program_id(0), pl.program_id(1)

        @pl.when(k == 0)
        def _():
            acc_ref[...] = jnp.zeros_like(acc_ref)

        acc_ref[...] += _dot_raw(dg_ref[...], wg_ref[...], "nt") + _dot_raw(du_ref[...], wu_ref[...], "nt")

        @pl.when((k == 0) & (i == 0))
        def _():
            for r in (dsc_ref, dsh_ref) + tuple(below_out[1:]):
                r[...] = jnp.zeros_like(r)

        @pl.when(k == HID_PIECES - 1)
        def _():
            _, vjp = jax.vjp(_modulate, x_ref[...], sc_ref[...], sh_ref[...])
            dx, dsc, dsh = vjp(acc_ref[...])
            dx = dx + do_ref[...]
            dx_ref[...] = dx
            dsc_ref[...] += dsc
            dsh_ref[...] += dsh
            if below is not None:
                _gate_below(dx, below_in[0], below_in[1], below[2], below_out[0], below_out[1])

    d_spec = pl.BlockSpec((None, tm, FFN_PIECE), lambda i, k: (k, i, 0))
    row = pl.BlockSpec((tm, D_MODEL), lambda i, k: (i, 0))
    par = pl.BlockSpec((1, D_MODEL), lambda i, k: (0, 0))
    row_shape, par_shape = jax.ShapeDtypeStruct((S, D_MODEL), F32), jax.ShapeDtypeStruct((1, D_MODEL), F32)
    return pl.pallas_call(
        body, name=name, grid=(S // tm, HID_PIECES),
        in_specs=[d_spec, d_spec,
                  pl.BlockSpec((None, D_MODEL, FFN_PIECE), lambda i, k: (k, 0, 0)),
                  pl.BlockSpec((None, D_MODEL, FFN_PIECE), lambda i, k: (k + HID_PIECES, 0, 0)),
                  row, row, par, par] + [row, par][:n_below] + _after_specs(after),
        out_specs=[row, par, par] + [row, par][:n_below],
        out_shape=[row_shape, par_shape, par_shape] + [jax.ShapeDtypeStruct((S, D_MODEL), BF16), par_shape][:n_below],
        scratch_shapes=[pltpu.VMEM((tm, D_MODEL), F32)],
        compiler_params=_params(("arbitrary", "arbitrary")),
    )(d_gate, d_up, w8, w8, x, d_out, scale, shift, *(below[:2] if below is not None else ()), *_after_args(after))


def _swiglu_bwd(d_hid, hid_by_gate, hid_by_up):
    return d_hid * hid_by_gate, d_hid * hid_by_up


def _adamw_math(w_, g_, m_, v_):
    m_ = ADAM_B1 * m_ + (1.0 - ADAM_B1) * g_
    v_ = ADAM_B2 * v_ + (1.0 - ADAM_B2) * (g_ * g_)
    m_hat = m_ / (1.0 - ADAM_B1 ** ADAM_STEP)
    v_hat = v_ / (1.0 - ADAM_B2 ** ADAM_STEP)
    return -ADAM_LR * (m_hat / (jnp.sqrt(v_hat) + ADAM_EPS) + ADAM_WD * w_), m_, v_


def _adamw(w, g, m, v, name):
    R, C = w.shape
    tr = _pick(R, (256, 176, 128, 64, 32, 16, 8))

    def body(w_ref, g_ref, m_ref, v_ref, d_ref, nm_ref, nv_ref):
        d_ref[...], nm_ref[...], nv_ref[...] = _adamw_math(w_ref[...], g_ref[...], m_ref[...], v_ref[...])

    spec = pl.BlockSpec((tr, C), lambda i: (i, 0))
    return pl.pallas_call(
        body, name=name, grid=(R // tr,),
        in_specs=[spec] * 4, out_specs=[spec] * 3,
        out_shape=[jax.ShapeDtypeStruct((R, C), F32)] * 3,
        compiler_params=_params(("parallel",)),
    )(w, g, m, v)


def _sum_adamw(parts, w, m, v, name):
    R, C = w.shape
    tr = _pick(R, (256, 176, 128, 64, 32, 16, 8))

    def body(p_ref, w_ref, m_ref, v_ref, g_ref, d_ref, nm_ref, nv_ref):
        g_ = p_ref[0].astype(F32)
        for d in range(1, N_DEV):
            g_ = g_ + p_ref[d].astype(F32)
        g_ref[...] = g_
        d_ref[...], nm_ref[...], nv_ref[...] = _adamw_math(w_ref[...], g_, m_ref[...], v_ref[...])

    spec = pl.BlockSpec((tr, C), lambda i: (i, 0))
    return pl.pallas_call(
        body, name=name, grid=(R // tr,),
        in_specs=[pl.BlockSpec((N_DEV, tr, C), lambda i: (0, i, 0)), spec, spec, spec], out_specs=[spec] * 4,
        out_shape=[jax.ShapeDtypeStruct((R, C), F32)] * 4,
        compiler_params=_params(("parallel",)),
    )(parts, w, m, v)


def _sum_devices(parts, name):
    _, R, C = parts.shape
    tr = _pick(R, (512, 256, 176, 128, 64, 32, 16, 8))

    def body(p_ref, o_ref):
        acc = p_ref[0].astype(F32)
        for d in range(1, N_DEV):
            acc = acc + p_ref[d].astype(F32)
        o_ref[...] = acc

    return pl.pallas_call(
        body, name=name, grid=(R // tr,),
        in_specs=[pl.BlockSpec((N_DEV, tr, C), lambda i: (0, i, 0))],
        out_specs=pl.BlockSpec((tr, C), lambda i: (i, 0)),
        out_shape=jax.ShapeDtypeStruct((R, C), F32),
        compiler_params=_params(("parallel",)),
    )(parts)


def _my_place():
    return lax.axis_index("x"), lax.axis_index("y"), lax.axis_index("c")


def _all_gather(blocks, name):
    n = len(blocks)

    def body(*refs):
        x_refs, out_refs = refs[:n], refs[n:2 * n]
        send_sems, recv_sems, local_sems = refs[2 * n:]
        x, y, c = _my_place()
        me, sibling = (x, y, c), (x, y, 1 - c)
        chips = [(1 - x, y), (x, 1 - y), (1 - x, 1 - y)]

        def copy(a, k, blk, to, own=False):
            slot = out_refs[a].at[4 * blk[0] + 2 * blk[1] + blk[2]]
            return pltpu.make_async_remote_copy(
                src_ref=x_refs[a] if own else slot, dst_ref=slot,
                send_sem=send_sems.at[7 * a + k], recv_sem=recv_sems.at[7 * a + k], device_id=to, device_id_type=MESH)

        mine = [pltpu.make_async_copy(x_refs[a], out_refs[a].at[4 * x + 2 * y + c], local_sems.at[a]) for a in range(n)]
        for cp in mine:
            cp.start()
        first = []
        for j, chip in enumerate(chips):
            first += [copy(a, 1 + j, me, (*chip, c), own=True) for a in range(n)]
        first += [copy(a, 0, me, sibling, own=True) for a in range(n)]
        for cp in first:
            cp.start()
        passed = []
        for j, chip in enumerate(chips):
            for a in range(n):
                copy(a, 1 + j, (*chip, c), me).wait_recv()
                passed.append(copy(a, 4 + j, (*chip, c), sibling))
                passed[-1].start()
        for a in range(n):
            copy(a, 0, sibling, me).wait_recv()
        for j, chip in enumerate(chips):
            for a in range(n):
                copy(a, 4 + j, (*chip, 1 - c), me).wait_recv()
        for cp in first + passed:
            cp.wait_send()
        for cp in mine:
            cp.wait()

    return pl.pallas_call(
        body, name=name,
        out_shape=[jax.ShapeDtypeStruct((N_DEV,) + b.shape, b.dtype) for b in blocks],
        in_specs=[pl.BlockSpec(memory_space=pl.ANY)] * n,
        out_specs=[pl.BlockSpec(memory_space=pl.ANY)] * n,
        scratch_shapes=[pltpu.SemaphoreType.DMA((7 * n,)), pltpu.SemaphoreType.DMA((7 * n,)), pltpu.SemaphoreType.DMA((n,))],
    )(*blocks)


def _all_to_all(pieces, name):
    n = len(pieces)

    def body(*refs):
        x_refs, out_refs = refs[:n], refs[n:2 * n]
        send_sems, recv_sems, local_sems = refs[2 * n:]
        x, y, c = _my_place()
        me = 4 * x + 2 * y + c
        mine = [pltpu.make_async_copy(x_refs[a].at[me], out_refs[a].at[me], local_sems.at[a]) for a in range(n)]
        for cp in mine:
            cp.start()
        copies = []
        for k in (2, 4, 6, 3, 5, 7, 1):
            px = 1 - x if k & 4 else x
            py = 1 - y if k & 2 else y
            pc = 1 - c if k & 1 else c
            peer = 4 * px + 2 * py + pc
            for a in range(n):
                copies.append(pltpu.make_async_remote_copy(
                    src_ref=x_refs[a].at[peer], dst_ref=out_refs[a].at[me],
                    send_sem=send_sems.at[7 * a + k - 1], recv_sem=recv_sems.at[7 * a + k - 1],
                    device_id=(px, py, pc), device_id_type=MESH))
        for cp in copies:
            cp.start()
        for cp in copies:
            cp.wait_recv()
        for cp in copies:
            cp.wait_send()
        for cp in mine:
            cp.wait()

    return pl.pallas_call(
        body, name=name,
        out_shape=[jax.ShapeDtypeStruct(p.shape, p.dtype) for p in pieces],
        in_specs=[pl.BlockSpec(memory_space=pl.ANY)] * n,
        out_specs=[pl.BlockSpec(memory_space=pl.ANY)] * n,
        scratch_shapes=[pltpu.SemaphoreType.DMA((7 * n,)), pltpu.SemaphoreType.DMA((7 * n,)), pltpu.SemaphoreType.DMA((n,))],
    )(*pieces)


def _peers():
    x, y, c = _my_place()
    out = []
    for k in (2, 4, 6, 3, 5, 7, 1):
        px = 1 - x if k & 4 else x
        py = 1 - y if k & 2 else y
        pc = 1 - c if k & 1 else c
        out.append((k, (px, py, pc), 4 * px + 2 * py + pc))
    return out


def _exchange_copies(x_refs, land_refs, send_sems, recv_sems, scatter):
    x, y, c = _my_place()
    me = 4 * x + 2 * y + c
    starts, arrivals = [], []
    for k, place, peer in _peers():
        for a, (x_ref, land_ref) in enumerate(zip(x_refs, land_refs)):
            sems = dict(send_sem=send_sems.at[7 * a + k - 1], recv_sem=recv_sems.at[7 * a + k - 1],
                        device_id=place, device_id_type=MESH)
            src = x_ref.at[peer] if scatter else x_ref
            starts.append(pltpu.make_async_remote_copy(src_ref=src, dst_ref=land_ref.at[me], **sems))
            arrivals.append(pltpu.make_async_remote_copy(src_ref=src, dst_ref=land_ref.at[peer], **sems))
    return starts, arrivals


def _exchange_start(arrays, scatter, name):
    n = len(arrays)
    hbm = pl.BlockSpec(memory_space=pltpu.HBM)
    sem = pl.BlockSpec(memory_space=pltpu.SEMAPHORE)
    lands = [lax.empty(a.shape if scatter else (N_DEV,) + a.shape, a.dtype) for a in arrays]

    def body(*refs):
        x_refs, land_refs = refs[:n], refs[n:2 * n]
        send_sems, recv_sems = refs[2 * n], refs[2 * n + 1]
        token = refs[-1]
        starts, _ = _exchange_copies(x_refs, land_refs, send_sems, recv_sems, scatter)
        for cp in starts:
            cp.start()
        token[...] = jnp.zeros_like(token)

    res = pl.pallas_call(
        body, name=name,
        out_shape=(pltpu.SemaphoreType.DMA((7 * n,)), pltpu.SemaphoreType.DMA((7 * n,)),
                   *[pltpu.HBM(a.shape, a.dtype) for a in arrays], *[pltpu.HBM(l.shape, l.dtype) for l in lands],
                   jax.ShapeDtypeStruct((8, 128), F32)),
        in_specs=[hbm] * (2 * n),
        out_specs=(sem, sem, *[hbm] * (2 * n), pl.BlockSpec(memory_space=pltpu.VMEM)),
        input_output_aliases={i: 2 + i for i in range(2 * n)},
        compiler_params=pltpu.CompilerParams(has_side_effects=pltpu.SideEffectType.DATAFLOW_SIDE_EFFECTING),
    )(*[pltpu.with_memory_space_constraint(a, pltpu.HBM) for a in arrays],
      *[pltpu.with_memory_space_constraint(l, pltpu.HBM) for l in lands])
    return res[0], res[1], list(res[2:2 + n]), list(res[2 + n:2 + 2 * n]), res[-1]


def _exchange_wait(handles, scatter, after, name):
    send_sems, recv_sems, arrays, lands, _ = handles
    n = len(arrays)
    hbm = pl.BlockSpec(memory_space=pltpu.HBM)
    sem = pl.BlockSpec(memory_space=pltpu.SEMAPHORE)

    def body(*refs):
        x_refs, land_refs = refs[:n], refs[n:2 * n]
        send_s, recv_s = refs[2 * n], refs[2 * n + 1]
        starts, arrivals = _exchange_copies(x_refs, land_refs, send_s, recv_s, scatter)
        for cp in arrivals:
            cp.wait_recv()
        for cp in starts:
            cp.wait_send()

    res = pl.pallas_call(
        body, name=name,
        out_shape=(*[pltpu.HBM(a.shape, a.dtype) for a in arrays], *[pltpu.HBM(l.shape, l.dtype) for l in lands]),
        in_specs=[hbm] * (2 * n) + [sem, sem, pl.BlockSpec(memory_space=pl.ANY)],
        out_specs=tuple([hbm] * (2 * n)),
        input_output_aliases={i: i for i in range(2 * n)},
        compiler_params=pltpu.CompilerParams(has_side_effects=pltpu.SideEffectType.DATAFLOW_SIDE_EFFECTING),
    )(*arrays, *lands, send_sems, recv_sems, after)
    me = 4 * lax.axis_index("x") + 2 * lax.axis_index("y") + lax.axis_index("c")
    out = []
    for src, got in zip(res[:n], res[n:]):
        zeros = (0,) * (got.ndim - 1)
        own = lax.dynamic_slice(src, (me,) + zeros, (1,) + src.shape[1:]) if scatter else src[None]
        out.append(lax.dynamic_update_slice(got, own, (me,) + zeros))
    return out


def _pad_lanes(v, at=0, width=128):
    return jnp.pad(v, ((0, 0), (at, width - at - v.shape[1])))


def _pack_weights(P):
    W = {}
    w = P["w_in"]
    W["wp"] = jnp.concatenate([w[:, :2048], w[:, 2440:2696], w[:, 2056:2440], w[:, 2696:2760], w[:, 2048:2056],
                               jnp.zeros((D_MODEL, N_IN_PACKED - N_IN), w.dtype)], axis=1).astype(BF16)
    W["conv_w"] = P["gdn_conv_w"].astype(F32)
    W["alog_p"] = _pad_lanes(P["gdn_a_log"], 64)
    W["dt_p"] = _pad_lanes(P["gdn_dt_bias"], 64)
    W["gnw"] = P["gdn_norm_w"]
    W["qnw"] = P["mla_q_norm_w"]
    W["kvnw"] = P["mla_kv_norm_w"]
    uq = P["mla_w_uq"].reshape(Q_LORA, HEADS, HEAD_DIM + ROPE)
    W["wuq"] = jnp.pad(uq, ((0, 0), (0, 0), (0, 256 - HEAD_DIM - ROPE))).reshape(Q_LORA, HEADS * 256).astype(BF16)
    ukv = P["mla_w_ukv"].reshape(KV_LORA, HEADS, 2, HEAD_DIM)
    W["wukv"] = ukv.transpose(0, 2, 1, 3).reshape(KV_LORA, 2 * HEADS * HEAD_DIM).astype(BF16)
    W["qn_w"] = P["qkn_q_nope"]
    W["qr_w"] = _pad_lanes(P["qkn_q_rope"])
    W["kn_w"] = P["qkn_k_nope"]
    W["kr_w"] = _pad_lanes(P["qkn_k_rope"])
    W["onw"] = P["mla_out_norm_w"]
    W["wout"] = P["w_out"].astype(BF16)
    return W


def _unpack_grads(G):
    g = G["wp"]
    uq = G["wuq"].reshape(Q_LORA, HEADS, 256)[:, :, :HEAD_DIM + ROPE].reshape(Q_LORA, HEADS * (HEAD_DIM + ROPE))
    ukv = G["wukv"].reshape(KV_LORA, 2, HEADS, HEAD_DIM).transpose(0, 2, 1, 3).reshape(KV_LORA, 2 * HEADS * HEAD_DIM)
    return {
        "w_in": jnp.concatenate([g[:, :2048], g[:, 2752:2760], g[:, 2304:2688], g[:, 2048:2304], g[:, 2688:2752]], axis=1),
        "gdn_conv_w": G["conv_w"], "gdn_a_log": G["alog_p"][:, 64:68], "gdn_dt_bias": G["dt_p"][:, 64:68],
        "gdn_norm_w": G["gnw"], "mla_q_norm_w": G["qnw"], "mla_w_uq": uq, "mla_kv_norm_w": G["kvnw"], "mla_w_ukv": ukv,
        "qkn_q_nope": G["qn_w"], "qkn_q_rope": G["qr_w"][:, :ROPE], "qkn_k_nope": G["kn_w"], "qkn_k_rope": G["kr_w"][:, :ROPE],
        "mla_out_norm_w": G["onw"], "w_out": G["wout"],
    }


def _rope_tables(positions):
    half = ROPE // 2
    inv_freq = ROPE_BASE ** (-jnp.arange(half, dtype=F32) / half)
    ang = positions.astype(F32)[:, None] * inv_freq
    cos, sin = jnp.cos(ang), jnp.sin(ang)
    zeros = jnp.zeros((positions.shape[0], 128 - ROPE), F32)
    return jnp.concatenate([cos, cos, zeros], axis=1), jnp.concatenate([-sin, sin, zeros], axis=1)


def _mod_fn(x, scale, shift):
    return (_modulate(x, scale, shift),)


def _ffn_down_loss(hid, wo4, x, gate_w, target, name):
    S = x.shape[0]
    tb = _pick(S, (512, 256, 128))
    n = S // tb

    def body(hid_ref, wo_ref, x_ref, g_ref, t_ref, dx_ref, df_ref, dg_ref, l_ref, acc_ref):
        i, k = pl.program_id(0), pl.program_id(1)

        @pl.when(k == 0)
        def _():
            acc_ref[...] = jnp.zeros_like(acc_ref)

        acc_ref[...] += _dot_raw(hid_ref[...], wo_ref[...], "nn")

        @pl.when((k == 0) & (i == 0))
        def _():
            dg_ref[...] = jnp.zeros_like(dg_ref)
            l_ref[...] = jnp.zeros_like(l_ref)

        @pl.when(k == HID_PIECES - 1)
        def _():
            f = acc_ref[...]
            diff = x_ref[...] + 0.5 * g_ref[...] * f - t_ref[...]
            dx = diff * (1.0 / D_MODEL)
            dx_ref[...] = dx
            df_ref[...] = (0.5 * g_ref[...] * dx).astype(df_ref.dtype)
            dg_ref[...] += jnp.sum(0.5 * f * dx, axis=0, keepdims=True)
            l_ref[...] += jnp.sum(diff * diff, axis=0, keepdims=True)

        @pl.when((k == HID_PIECES - 1) & (i == n - 1))
        def _():
            l_ref[...] = jnp.full(l_ref.shape, (0.5 / D_MODEL) * jnp.sum(l_ref[...]), F32)

    row = pl.BlockSpec((tb, D_MODEL), lambda i, k: (i, 0))
    par = pl.BlockSpec((1, D_MODEL), lambda i, k: (0, 0))
    return pl.pallas_call(
        body, name=name, grid=(n, HID_PIECES),
        in_specs=[pl.BlockSpec((None, tb, FFN_PIECE), lambda i, k: (k, i, 0)),
                  pl.BlockSpec((None, FFN_PIECE, D_MODEL), lambda i, k: (k, 0, 0)), row, par, row],
        out_specs=[row, row, par, par],
        out_shape=[jax.ShapeDtypeStruct((S, D_MODEL), F32), jax.ShapeDtypeStruct((S, D_MODEL), BF16),
                   jax.ShapeDtypeStruct((1, D_MODEL), F32), jax.ShapeDtypeStruct((1, D_MODEL), F32)],
        scratch_shapes=[pltpu.VMEM((tb, D_MODEL), F32)],
        compiler_params=_params(("arbitrary", "arbitrary")),
    )(hid, wo4, x, gate_w, target)


def _ffn_fwd(x, scale, shift, gate_w, w8, wo4, tag, target=None):
    S = x.shape[0]
    tm = _pick(S, (512, 256, 128))
    (h,) = _rowwise(_mod_fn, [(x, tm, D_MODEL, 0)], [scale, shift], [(tm, D_MODEL, BF16)], S // tm, tag + "_mod")
    by_gate, by_up, hid, hid_t = _ffn_up(h, w8, tag + "_up")
    if target is not None:
        dx_out, df, d_gate_w, loss_row = _ffn_down_loss(hid, wo4, x, gate_w, target, tag + "_down")
        return (dx_out, loss_row), (h, by_gate, by_up, hid_t, None, df, d_gate_w)
    tb = _pick(S, MATMUL_ROWS)
    mn = pl.BlockSpec((tb, D_MODEL), lambda i, j, k: (i, j))
    f, x_out = _mmg(hid, wo4, "nn", name=tag + "_down", grid=(S // tb, 1, HID_PIECES),
                    a_spec=pl.BlockSpec((None, tb, FFN_PIECE), lambda i, j, k: (k, i, 0)),
                    b_spec=pl.BlockSpec((None, FFN_PIECE, D_MODEL), lambda i, j, k: (k, 0, j)),
                    out_spec=mn, out_shapes=[jax.ShapeDtypeStruct((S, D_MODEL), F32)] * 2, acc_shape=(tb, D_MODEL),
                    extras=[x, gate_w], extra_specs=[mn, pl.BlockSpec((1, D_MODEL), lambda i, j, k: (0, j))],
                    epi=lambda acc, x_, g_: (acc, x_ + 0.5 * g_ * acc))
    return x_out, (h, by_gate, by_up, hid_t, f, None, None)


def _ffn_bwd(d_out, x, scale, shift, gate_w, w8, wo4, saved, tag, grad_ready, below=None):
    h, gate, up, hid_t, f, df, d_gate_w = saved
    S = x.shape[0]
    tm = _pick(S, (512, 256, 128))
    tk = _pick(S, (512, 256, 128))
    n = S // tm
    if df is None:
        (df,), (d_gate_w,) = _rowwise_bwd(lambda f_, g_: (0.5 * g_ * f_,), [(f, tm, D_MODEL, 0)], [], [gate_w],
                                          [(d_out, tm, D_MODEL, 0)], n, tag + "_dres", row_dtypes=(BF16,))
    tb = _pick(S, MATMUL_ROWS)
    piece = pl.BlockSpec((None, tb, FFN_PIECE), lambda i, j, k: (j, i, 0))
    d_gate, d_up = _mmg(df, wo4, "nt", name=tag + "_ddown", grid=(S // tb, HID_PIECES, 1),
                        a_spec=pl.BlockSpec((tb, D_MODEL), lambda i, j, k: (i, 0)),
                        b_spec=pl.BlockSpec((None, FFN_PIECE, D_MODEL), lambda i, j, k: (j, 0, 0)),
                        out_spec=piece, out_shapes=[jax.ShapeDtypeStruct((HID_PIECES, S, FFN_PIECE), BF16)] * 2,
                        acc_shape=(tb, FFN_PIECE), extras=[gate, up], extra_specs=[piece, piece], epi=_swiglu_bwd)
    tk = _pick(S, MATMUL_ROWS)
    g_wo4 = _mmg(hid_t, df, "nn", name=tag + "_gwo", grid=(HID_PIECES, 1, S // tk),
                 a_spec=pl.BlockSpec((None, FFN_PIECE, tk), lambda i, j, k: (i, 0, k)),
                 b_spec=pl.BlockSpec((tk, D_MODEL), lambda i, j, k: (k, j)),
                 out_spec=pl.BlockSpec((None, FFN_PIECE, D_MODEL), lambda i, j, k: (i, 0, j)),
                 out_shapes=[jax.ShapeDtypeStruct((HID_PIECES, FFN_PIECE, D_MODEL), BF16)], acc_shape=(FFN_PIECE, D_MODEL))
    g_w8 = _ffn_gw8(h, d_gate, d_up, tag + "_gw8", after=grad_ready("wo4", g_wo4))
    res = _ffn_dh(d_gate, d_up, w8, x, d_out, scale, shift, tag + "_dh", after=grad_ready("w8", g_w8), below=below)
    return (res[0], res[1], res[2], d_gate_w) + tuple(res[3:])


def _dproj_dmod(d_proj, wp, x, d_out, scale, shift, name, below=None):
    S, K = d_proj.shape
    tm = _pick(S, (512, 256, 128))
    tk = _pick(K, (1408, 512, 256, 128))
    nk = K // tk
    n_below = 0 if below is None else 2

    def body(dp_ref, w_ref, x_ref, do_ref, sc_ref, sh_ref, *rest):
        below_in = rest[:n_below]
        dx_ref, dsc_ref, dsh_ref = rest[n_below:n_below + 3]
        below_out, acc_ref = rest[n_below + 3:n_below + 3 + n_below], rest[-1]
        i, k = pl.program_id(0), pl.program_id(1)

        @pl.when(k == 0)
        def _():
            acc_ref[...] = jnp.zeros_like(acc_ref)

        acc_ref[...] += _dot_raw(dp_ref[...], w_ref[...], "nt")

        @pl.when((k == 0) & (i == 0))
        def _():
            for r in (dsc_ref, dsh_ref) + tuple(below_out[1:]):
                r[...] = jnp.zeros_like(r)

        @pl.when(k == nk - 1)
        def _():
            _, vjp = jax.vjp(_modulate, x_ref[...], sc_ref[...], sh_ref[...])
            dx, dsc, dsh = vjp(acc_ref[...])
            dx = dx + do_ref[...]
            dx_ref[...] = dx
            dsc_ref[...] += dsc
            dsh_ref[...] += dsh
            if below is not None:
                _gate_below(dx, below_in[0], below_in[1], below[2], below_out[0], below_out[1])

    row = pl.BlockSpec((tm, D_MODEL), lambda i, k: (i, 0))
    par = pl.BlockSpec((1, D_MODEL), lambda i, k: (0, 0))
    row_shape, par_shape = jax.ShapeDtypeStruct((S, D_MODEL), F32), jax.ShapeDtypeStruct((1, D_MODEL), F32)
    return pl.pallas_call(
        body, name=name, grid=(S // tm, nk),
        in_specs=[pl.BlockSpec((tm, tk), lambda i, k: (i, k)), pl.BlockSpec((D_MODEL, tk), lambda i, k: (0, k)),
                  row, row, par, par] + [row, par][:n_below],
        out_specs=[row, par, par] + [row, par][:n_below],
        out_shape=[row_shape, par_shape, par_shape] + [jax.ShapeDtypeStruct((S, D_MODEL), BF16), par_shape][:n_below],
        scratch_shapes=[pltpu.VMEM((tm, D_MODEL), F32)],
        compiler_params=_params(("arbitrary", "arbitrary")),
    )(d_proj, wp, x, d_out, scale, shift, *(below[:2] if below is not None else ()))


def _mixer_fwd(x1, scale, shift, gate_w, cos_p, sin_p, W):
    S = x1.shape[0]
    tm = _pick(S, (512, 256, 128))
    tv = _pick(S, (256, 128))
    ta = _pick(S, (512, 256, 128))
    nc = S // CHUNK
    (h2,) = _rowwise(_mod_fn, [(x1, tm, D_MODEL, 0)], [scale, shift], [(tm, D_MODEL, BF16)], S // tm, "mix_mod")
    proj = _mm(h2, W["wp"], "nn", name="mix_proj")
    qkvc = _conv_fwd(proj, W["conv_w"], tv, "gdn_conv")
    kab = (proj, tv, 128, 21)
    q_a, k_a, v_a, gb = _rowwise(_gdn_pre_fn, [(qkvc, tv, 1536, 0), kab], [W["alog_p"], W["dt_p"]],
                                 [(tv, 512, F32)] * 3 + [(tv, 128, F32)], S // tv, "gdn_pre")
    ti = _pick(S, INTRA_ROWS)
    intra = _rowwise(_gdn_intra_fn, [(q_a, ti, 512, 0), (k_a, ti, 512, 0), (v_a, ti, 512, 0), (gb, ti, 128, 0)],
                     [], [(ti, 512, F32)] * 4 + [(ti, CHUNK, F32)] * 4 + [(ti // 8, 512, F32)], S // ti, "gdn_intra")
    u, wk, qd, kd, qks, gl = intra[0], intra[1], intra[2], intra[3], tuple(intra[4:8]), intra[8]
    o_a, s_prev = _gdn_scan_fwd(u, wk, qd, kd, qks, gl, "gdn_scan")
    mla_params = [W["qnw"], W["kvnw"], W["wuq"], W["wukv"], W["qn_w"], W["qr_w"], W["kn_w"], W["kr_w"]]
    def mla_pre_with_vt(*a):
        q_, k_, v_ = _mla_pre_fn(*a)
        return q_, k_, v_, jnp.transpose(v_)

    q_b, k_b, v_b, vt_b = _rowwise(mla_pre_with_vt,
                                   [(proj, tv, 256, 8), (proj, tv, 384, 6), kab, (cos_p, tv, 128, 0), (sin_p, tv, 128, 0)],
                                   mla_params, [(tv, 1024, BF16), (tv, 1024, BF16), (tv, 512, BF16), (512, tv, BF16, "across")],
                                   S // tv, "mla_pre")
    o_b, lse = _attn_fwd(q_b, k_b, vt_b, ta, "mla_attn")
    (mixed,) = _rowwise(_mix_post_fn, [(o_a, tv, 512, 0), (proj, tv, 512, 3), (o_b, tv, 512, 0)], [W["gnw"], W["onw"]],
                        [(tv, D_MODEL, BF16)], S // tv, "mix_post")
    y, x2 = _mm(mixed, W["wout"], "nn", name="mix_out", out_dtypes=(F32, F32), extras=[x1], extra_params=[gate_w],
                epi=lambda acc, x_, g_: (acc, x_ + g_ * acc))
    saved = (h2, proj, qkvc, q_a, k_a, v_a, gb, u, wk, qd, kd, qks, gl, s_prev, o_a, q_b, k_b, v_b, o_b, lse, mixed, y)
    return x2, saved


def _mixer_bwd(d_out, dy, x1, scale, shift, cos_p, sin_p, W, saved, below):
    (h2, proj, qkvc, q_a, k_a, v_a, gb, u, wk, qd, kd, qks, gl, s_prev, o_a, q_b, k_b, v_b, o_b, lse, mixed, y) = saved
    S = x1.shape[0]
    tm = _pick(S, (512, 256, 128))
    tv = _pick(S, (256, 128))
    ta = _pick(S, (512, 256, 128))
    nc = S // CHUNK
    G = {}
    d_mixed = _mm(dy, W["wout"], "nt", name="mix_dout")
    G["wout"] = _mm(mixed, dy, "tn", name="mix_gwout")
    (do_a, dz, do_b), (G["gnw"], G["onw"]) = _rowwise_bwd(
        _mix_post_fn, [(o_a, tv, 512, 0), (proj, tv, 512, 3), (o_b, tv, 512, 0)], [], [W["gnw"], W["onw"]],
        [(d_mixed, tv, D_MODEL, 0)], S // tv, "mix_dpost")
    stats = _attn_stats(o_b, lse, do_b, ta, "mla_stats")
    dq_b, dk_b, dv_b = _attn_bwd(q_b, k_b, v_b, do_b, stats, ta, "mla_dattn")
    kab = (proj, tv, 128, 21)
    mla_params = [W["qnw"], W["kvnw"], W["wuq"], W["wukv"], W["qn_w"], W["qr_w"], W["kn_w"], W["kr_w"]]
    (d_ckv, d_cq, d_kab), mla_grads = _rowwise_bwd(
        _mla_pre_fn, [(proj, tv, 256, 8), (proj, tv, 384, 6), kab], [(cos_p, tv, 128, 0), (sin_p, tv, 128, 0)], mla_params,
        [(dq_b, tv, 1024, 0), (dk_b, tv, 1024, 0), (dv_b, tv, 512, 0)], S // tv, "mla_dpre")
    for key, g in zip(("qnw", "kvnw", "wuq", "wukv", "qn_w", "qr_w", "kn_w", "kr_w"), mla_grads):
        G[key] = g
    scan_grads = _gdn_scan_bwd(u, wk, qd, kd, qks, gl, s_prev, do_a, "gdn_dscan")
    ti = _pick(S, INTRA_ROWS)
    intra_douts = [(scan_grads[i], ti, 512, 0) for i in range(4)] + [(scan_grads[4 + i], ti, CHUNK, 0) for i in range(4)]
    intra_douts.append((scan_grads[8], ti // 8, 512, 0))
    (dq_a, dk_a, dv_a, d_gb), _ = _rowwise_bwd(
        _gdn_intra_fn, [(q_a, ti, 512, 0), (k_a, ti, 512, 0), (v_a, ti, 512, 0), (gb, ti, 128, 0)], [], [],
        intra_douts, S // ti, "gdn_dintra")
    (d_qkvc, d_kab), (G["alog_p"], G["dt_p"]) = _rowwise_bwd(
        _gdn_pre_fn, [(qkvc, tv, 1536, 0), kab], [], [W["alog_p"], W["dt_p"]],
        [(dq_a, tv, 512, 0), (dk_a, tv, 512, 0), (dv_a, tv, 512, 0), (d_gb, tv, 128, 0)], S // tv, "gdn_dpre",
        adds=[(1, d_kab)])
    d_qkv, g_conv = _conv_bwd(proj, d_qkvc, W["conv_w"], tv, "gdn_dconv")
    G["conv_w"] = g_conv[:4]
    d_proj = jnp.concatenate([d_qkv, dz, d_ckv, d_cq, d_kab], axis=1).astype(BF16)
    G["wp"] = _mm(h2, d_proj, "tn", name="mix_gwp")
    dx1, G["s2"], G["sh2"], d_below, dg_below = _dproj_dmod(d_proj, W["wp"], x1, d_out, scale, shift, "mix_dproj", below=below)
    return dx1, d_below, dg_below, G


def _local_step(x, target, mod, cos_p, sin_p, W1, mixer_weights, ffn2_weights, ffn_grad_ready, mixer_grads_ready):
    sh1, s1, g1, sh2, s2, g2, sh3, s3, g3 = [mod[:, D_MODEL * i:D_MODEL * (i + 1)] for i in range(N_MOD)]
    x1, saved1 = _ffn_fwd(x, s1, sh1, g1, W1["f1_w8"], W1["f1_wo4"], "ffn1")
    W = mixer_weights(x1)
    x2, saved2 = _mixer_fwd(x1, s2, sh2, g2, cos_p, sin_p, W)
    W.update(ffn2_weights(x2))
    (dx3, loss_row), saved3 = _ffn_fwd(x2, s3, sh3, g3, W["f2_w8"], W["f2_wo4"], "ffn2", target=target)
    dx2, d_s3, d_sh3, d_g3, dy, d_g2 = _ffn_bwd(dx3, x2, s3, sh3, g3, W["f2_w8"], W["f2_wo4"], saved3, "ffn2",
                                                ffn_grad_ready("f2"), below=(saved2[-1], g2, 1.0))
    dx1, df1, d_g1, G = _mixer_bwd(dx2, dy, x1, s2, sh2, cos_p, sin_p, W, saved2, below=(saved1[4], g1, 0.5))
    d_sh2, d_s2 = G.pop("sh2"), G.pop("s2")
    saved1 = saved1[:5] + (df1, d_g1 + mixer_grads_ready(G))
    dx, d_s1, d_sh1, d_g1 = _ffn_bwd(dx1, x, s1, sh1, g1, W1["f1_w8"], W1["f1_wo4"], saved1, "ffn1", ffn_grad_ready("f1"))
    d_mod = jnp.concatenate([d_sh1, d_s1, d_g1, d_sh2, d_s2, d_g2, d_sh3, d_s3, d_g3], axis=1)
    return loss_row, dx, d_mod


WEIGHT_NAMES = ("w_ada", "b_ada", "ffn1_w_in", "ffn1_w_out", "w_in", "gdn_conv_w", "gdn_a_log", "gdn_dt_bias", "gdn_norm_w",
                "mla_q_norm_w", "mla_w_uq", "mla_kv_norm_w", "mla_w_ukv", "qkn_q_nope", "qkn_q_rope", "qkn_k_nope",
                "qkn_k_rope", "mla_out_norm_w", "w_out", "ffn2_w_in", "ffn2_w_out")
FFN_SHARDED = ("ffn1_w_in", "ffn1_w_out", "ffn2_w_in", "ffn2_w_out")
SHEETED = (("w_in", "col"), ("gdn_conv_w", "col"), ("mla_w_uq", "col"), ("mla_w_ukv", "col"), ("w_out", "row"))
MOD_ROWS = N_MOD * D_MODEL // 128
SMALL = {"gdn_a_log": (MOD_ROWS, 1, 64, 4), "gdn_dt_bias": (MOD_ROWS + 1, 1, 64, 4), "gdn_norm_w": (MOD_ROWS + 2, 1, 0, 128),
         "mla_q_norm_w": (MOD_ROWS + 3, 3, 0, 384), "mla_kv_norm_w": (MOD_ROWS + 6, 2, 0, 256),
         "qkn_q_nope": (MOD_ROWS + 8, 1, 0, 128), "qkn_q_rope": (MOD_ROWS + 9, 1, 0, 64), "qkn_k_nope": (MOD_ROWS + 10, 1, 0, 128),
         "qkn_k_rope": (MOD_ROWS + 11, 1, 0, 64), "mla_out_norm_w": (MOD_ROWS + 12, 1, 0, 128)}
LOSS_ROW = MOD_ROWS + 13
CONV_ROW, CONV_ROWS = 88, 4 * 1536 // 128
SHEET_ROWS = CONV_ROW + CONV_ROWS


def _to_sheet(flat, dtype, sublanes):
    n = flat.shape[-1]
    unit = sublanes * 128
    pad = (-n) % unit
    flat = jnp.pad(flat.astype(dtype), [(0, 0)] * (flat.ndim - 1) + [(0, pad)])
    return flat.reshape(flat.shape[:-1] + ((n + pad) // 128, 128))


def _small_sheet(b_like, small):
    sheet = jnp.zeros((SHEET_ROWS, 128), F32).at[:MOD_ROWS].set(b_like.reshape(MOD_ROWS, 128))
    for name, (row, rows, lane, n) in SMALL.items():
        v = small[name].reshape(1, n)
        if rows == 1:
            sheet = sheet.at[row, lane:lane + n].set(v[0])
        else:
            sheet = sheet.at[row:row + rows].set(v.reshape(rows, 128))
    return sheet


def _from_small_sheet(sheet):
    out = {"b_ada": sheet[:MOD_ROWS].reshape(1, N_MOD * D_MODEL)}
    for name, (row, rows, lane, n) in SMALL.items():
        out[name] = sheet[row, lane:lane + n].reshape(1, n) if rows == 1 else sheet[row:row + rows].reshape(1, n)
    return out


def kernel(x, c, positions, w_ada, b_ada, ffn1_w_in, ffn1_w_out, w_in, gdn_conv_w, gdn_a_log, gdn_dt_bias, gdn_norm_w, mla_q_norm_w, mla_w_uq, mla_kv_norm_w, mla_w_ukv, qkn_q_nope, qkn_q_rope, qkn_k_nope, qkn_k_rope, mla_out_norm_w, w_out, ffn2_w_in, ffn2_w_out, loss_target, m_w_ada, m_b_ada, m_ffn1_w_in, m_ffn1_w_out, m_w_in, m_gdn_conv_w, m_gdn_a_log, m_gdn_dt_bias, m_gdn_norm_w, m_mla_q_norm_w, m_mla_w_uq, m_mla_kv_norm_w, m_mla_w_ukv, m_qkn_q_nope, m_qkn_q_rope, m_qkn_k_nope, m_qkn_k_rope, m_mla_out_norm_w, m_w_out, m_ffn2_w_in, m_ffn2_w_out, v_w_ada, v_b_ada, v_ffn1_w_in, v_ffn1_w_out, v_w_in, v_gdn_conv_w, v_gdn_a_log, v_gdn_dt_bias, v_gdn_norm_w, v_mla_q_norm_w, v_mla_w_uq, v_mla_kv_norm_w, v_mla_w_ukv, v_qkn_q_nope, v_qkn_q_rope, v_qkn_k_nope, v_qkn_k_rope, v_mla_out_norm_w, v_w_out, v_ffn2_w_in, v_ffn2_w_out):
    args = locals()
    w = {n: args[n] for n in WEIGHT_NAMES}
    m = {n: args["m_" + n] for n in WEIGHT_NAMES}
    v = {n: args["v_" + n] for n in WEIGHT_NAMES}
    me = 4 * lax.axis_index("x") + 2 * lax.axis_index("y") + lax.axis_index("c")
    cols = N_MOD * D_MODEL // N_DEV
    shard = {n: w[n][0] for n in FFN_SHARDED + tuple(s[0] for s in SHEETED)}

    sc = c * _sigmoid(c)
    first = _to_sheet(jnp.concatenate([sc.reshape(-1), shard["gdn_conv_w"].reshape(-1)]), F32, 8)
    (first_all,) = _all_gather([first], "gather_c")
    sc_all = first_all[:, :D_MODEL // 128].reshape(N_DEV, D_MODEL)
    n_taps = shard["gdn_conv_w"].size
    conv_all = first_all.reshape(N_DEV, -1)[:, D_MODEL:D_MODEL + n_taps].reshape(N_DEV, 4, -1)
    b_mine = lax.dynamic_slice(b_ada, (0, me * cols), (1, cols))
    mod_cols = _mm(sc_all, w_ada[0], "nn", name="ada_mod", extra_params=[b_mine], epi=lambda acc, b_: (acc + b_,))
    (mod_all,) = _all_to_all([_to_sheet(mod_cols, F32, 8)], "scatter_mod")
    mod = mod_all.reshape(N_DEV, -1)[:, :cols].reshape(1, N_MOD * D_MODEL)

    f1_shards, mod = lax.optimization_barrier(([shard["ffn1_w_in"].astype(BF16), shard["ffn1_w_out"].astype(BF16)], mod))
    f1_w8, f1_out = _all_gather(f1_shards, "gather_w1")
    travel = [s for s in SHEETED if s[0] != "gdn_conv_w"]
    tied = lax.optimization_barrier(([shard[n].astype(BF16) for n, _ in travel], f1_w8))
    f1_w8 = tied[1]
    mixer_w = _exchange_start(tied[0], False, "gather_wm_start")
    ffn2_w = _exchange_start([shard["ffn2_w_in"].astype(BF16) + mixer_w[4][0:1, 0:1].astype(BF16),
                              shard["ffn2_w_out"].astype(BF16)], False, "gather_w2_start")
    mod = mod + ffn2_w[4][0:1, 0:1]
    W1 = dict(f1_w8=f1_w8, f1_wo4=f1_out.reshape(HID_PIECES, FFN_PIECE, D_MODEL))

    def mixer_weights(after):
        got = _exchange_wait(mixer_w, False, after, "gather_wm_wait")
        P = {n: jnp.concatenate(list(g), axis=1) if kind == "col" else g.reshape(-1, g.shape[-1])
             for (n, kind), g in zip(travel, got)}
        P["gdn_conv_w"] = jnp.concatenate(list(conv_all), axis=1)
        for n in SMALL:
            P[n] = w[n]
        return _pack_weights(P)

    def ffn2_weights(after):
        f2_w8, f2_out = _exchange_wait(ffn2_w, False, after, "gather_w2_wait")
        return dict(f2_w8=f2_w8, f2_wo4=f2_out.reshape(HID_PIECES, FFN_PIECE, D_MODEL))

    pending, small_grads = {}, {}

    def ffn_grad_ready(tag):
        def ready(which, g):
            pieces = g if which == "w8" else g.reshape((N_DEV,) + shard["ffn1_w_out"].shape)
            pending[tag + which] = _exchange_start([pieces], True, "scatter_%s_%s_start" % (tag, which))
            return pending[tag + which][4]
        return ready

    def mixer_grads_ready(G):
        g_full = _unpack_grads(G)
        small_grads.update({n: g_full[n] for n in SMALL})
        small_grads["gdn_conv_w"] = g_full["gdn_conv_w"]
        pieces = []
        for n, kind in travel:
            r, cc = shard[n].shape
            g = g_full[n].astype(BF16)
            pieces.append(jnp.stack([g[:, cc * p:cc * (p + 1)] for p in range(N_DEV)]) if kind == "col"
                          else g.reshape(N_DEV, r, cc))
        pending["mixer"] = _exchange_start(pieces, True, "scatter_mx_start")
        return pending["mixer"][4][0:1, 0:1]

    cos_p, sin_p = _rope_tables(positions[0])
    loss_row, dx, d_mod = _local_step(x[0], loss_target[0], mod, cos_p, sin_p, W1, mixer_weights, ffn2_weights,
                                      ffn_grad_ready, mixer_grads_ready)

    sheet = _small_sheet(d_mod, small_grads).at[LOSS_ROW].set(loss_row[0, :128])
    sheet = sheet.at[CONV_ROW:CONV_ROW + CONV_ROWS].set(small_grads["gdn_conv_w"].reshape(CONV_ROWS, 128))
    (sheets,) = _all_gather([sheet], "gather_small")
    summed = _sum_devices(sheets, "sum_small")
    d_mod_all = sheets[:, :MOD_ROWS].reshape(N_DEV, N_MOD * D_MODEL)
    d_mod_mine = lax.dynamic_slice(d_mod_all, (0, me * cols), (N_DEV, cols))
    grads = _from_small_sheet(summed)
    grads["w_ada"] = _mm(sc_all, d_mod_mine, "tn", name="ada_gw", hi=True)
    conv_taps = shard["gdn_conv_w"].shape[1]
    grads["gdn_conv_w"] = lax.dynamic_slice(summed[CONV_ROW:CONV_ROW + CONV_ROWS].reshape(4, -1), (0, me * conv_taps),
                                            (4, conv_taps))
    loss = summed[LOSS_ROW, 0]

    delta, new_m, new_v = {}, {}, {}
    arrived = {}
    for n, key in zip(FFN_SHARDED, ("f1w8", "f1wo4", "f2w8", "f2wo4")):
        (arrived[n],) = _exchange_wait(pending[key], True, summed, "scatter_%s_wait" % key)
    arrived.update(zip([n for n, _ in travel], _exchange_wait(pending["mixer"], True, summed, "scatter_mx_wait")))
    for n, parts in arrived.items():
        grads[n], delta[n], new_m[n], new_v[n] = _sum_adamw(parts, w[n][0], m[n][0], v[n][0], "adamw_" + n)
    for n in ("w_ada", "gdn_conv_w"):
        delta[n], new_m[n], new_v[n] = _adamw(w[n][0], grads[n], m[n][0], v[n][0], "adamw_" + n)
    small_in = [_small_sheet(t["b_ada"], t) for t in (w, grads, m, v)]
    for res, out in zip(_adamw(*small_in, "adamw_small"), (delta, new_m, new_v)):
        out.update(_from_small_sheet(res))

    def shaped(d):
        return [d[n].reshape(w[n].shape) for n in WEIGHT_NAMES]

    return (loss, dx[None], *shaped(grads), *shaped(delta), *shaped(new_m), *shaped(new_v))
```

```python
import functools

import jax
import jax.numpy as jnp
import numpy as np
from jax import lax
from jax.experimental import pallas as pl
from jax.experimental.pallas import tpu as pltpu

F32 = jnp.float32
BF16 = jnp.bfloat16

D_MODEL = 1024
D_FF = 2816
N_MOD = 9
HEADS = 4
HEAD_DIM = 128
CHUNK = 64
EPS = 1e-6
ROPE = 64
Q_LORA = 384
KV_LORA = 256
N_IN = 2760
N_IN_PACKED = 2816
ROPE_BASE = 10000.0
N_DEV = 8

ADAM_LR = 0.001
ADAM_B1 = 0.9
ADAM_B2 = 0.999
ADAM_EPS = 1e-08
ADAM_WD = 0.01
ADAM_STEP = 10

VMEM_LIMIT_BYTES = 56 * 1024 * 1024
MATMUL_ROWS = (1024, 512, 256, 128)
MESH = pl.DeviceIdType.MESH


def _params(sem=None):
    return pltpu.CompilerParams(dimension_semantics=sem, vmem_limit_bytes=VMEM_LIMIT_BYTES)


def _pick(dim, prefs):
    for p in prefs:
        if dim % p == 0:
            return p
    return dim


_DIMS = {"nn": (((1,), (0,)), ((), ())), "nt": (((1,), (1,)), ((), ())), "tn": (((0,), (0,)), ((), ()))}


def _dot_raw(a, b, mode):
    return lax.dot_general(a.astype(BF16), b.astype(BF16), _DIMS[mode], preferred_element_type=F32)


def _dot_hi(a, b, mode="nn"):
    return lax.dot_general(a, b, _DIMS[mode], precision=lax.Precision.HIGHEST, preferred_element_type=F32)


@functools.partial(jax.custom_vjp, nondiff_argnums=(2,))
def _bdot(a, b, mode):
    return _dot_raw(a, b, mode)


def _bdot_fwd(a, b, mode):
    return _dot_raw(a, b, mode), (a, b)


def _bdot_bwd(mode, res, g):
    a, b = res
    if mode == "nn":
        return _dot_raw(g, b, "nt"), _dot_raw(a, g, "tn")
    if mode == "nt":
        return _dot_raw(g, b, "nn"), _dot_raw(g, a, "tn")
    return _dot_raw(b, g, "nt"), _dot_raw(a, g, "nn")


_bdot.defvjp(_bdot_fwd, _bdot_bwd)


def _mm(a, b, mode, *, name, out_dtypes=(F32,), epi=None, extras=(), extra_params=(), hi=False,
        tm=None, tn=None, tk=None):
    if mode == "nn":
        (M, K), (_, N) = a.shape, b.shape
    elif mode == "nt":
        (M, K), (N, _) = a.shape, b.shape
    else:
        (K, M), (_, N) = a.shape, b.shape
    tm = tm or _pick(M, (512, 1408, 256, 128) if mode == "tn" else MATMUL_ROWS + (384, 352))
    tn = tn or _pick(N, (1024, 1408, 768, 512, 384, 256, 128))
    tk = tk or _pick(K, (1024, 1408, 512, 384, 256, 128))
    a_spec = {"nn": pl.BlockSpec((tm, tk), lambda i, j, k: (i, k)), "nt": pl.BlockSpec((tm, tk), lambda i, j, k: (i, k)),
              "tn": pl.BlockSpec((tk, tm), lambda i, j, k: (k, i))}[mode]
    b_spec = {"nn": pl.BlockSpec((tk, tn), lambda i, j, k: (k, j)), "nt": pl.BlockSpec((tn, tk), lambda i, j, k: (j, k)),
              "tn": pl.BlockSpec((tk, tn), lambda i, j, k: (k, j))}[mode]
    mn_spec = pl.BlockSpec((tm, tn), lambda i, j, k: (i, j))
    return _mmg(a, b, mode, name=name, grid=(M // tm, N // tn, K // tk), a_spec=a_spec, b_spec=b_spec, out_spec=mn_spec,
                out_shapes=[jax.ShapeDtypeStruct((M, N), dt) for dt in out_dtypes], acc_shape=(tm, tn), epi=epi,
                extras=list(extras) + list(extra_params),
                extra_specs=[mn_spec] * len(extras) + [pl.BlockSpec((1, tn), lambda i, j, k: (0, j))] * len(extra_params),
                hi=hi)


def _mmg(a, b, mode, *, name, grid, a_spec, b_spec, out_spec, out_shapes, acc_shape, epi=None, extras=(),
         extra_specs=(), hi=False):
    nk = grid[2]
    n_e, n_o = len(extras), len(out_shapes)

    def body(*refs):
        a_ref, b_ref = refs[:2]
        e_refs = refs[2:2 + n_e]
        o_refs = refs[2 + n_e:2 + n_e + n_o]
        acc_ref = refs[-1]
        k = pl.program_id(2)

        @pl.when(k == 0)
        def _():
            acc_ref[...] = jnp.zeros_like(acc_ref)

        if hi:
            acc_ref[...] += _dot_hi(a_ref[...].astype(F32), b_ref[...].astype(F32), mode)
        else:
            acc_ref[...] += _dot_raw(a_ref[...], b_ref[...], mode)

        @pl.when(k == nk - 1)
        def _():
            acc = acc_ref[...]
            outs = (acc,) if epi is None else epi(acc, *[e[...].astype(F32) for e in e_refs])
            for o_ref, o in zip(o_refs, outs):
                o_ref[...] = o.astype(o_ref.dtype)

    outs = pl.pallas_call(
        body, name=name, grid=grid,
        in_specs=[a_spec, b_spec] + list(extra_specs),
        out_specs=[out_spec] * n_o,
        out_shape=list(out_shapes),
        scratch_shapes=[pltpu.VMEM(acc_shape, F32)],
        compiler_params=_params(("parallel", "parallel", "arbitrary")),
    )(a, b, *extras)
    return outs if n_o > 1 else outs[0]


def _row_spec(th, cw, ci):
    return pl.BlockSpec((th, cw), lambda i: (i, ci))


def _full_spec(shape):
    return pl.BlockSpec(shape, lambda i: (0,) * len(shape))


def _rowwise(fn, rows, params, outs, n_steps, name):
    n_r, n_p, n_o = len(rows), len(params), len(outs)

    def body(*refs):
        vals = [r[...].astype(F32) for r in refs[:n_r + n_p]]
        res = fn(*vals)
        for o_ref, o in zip(refs[n_r + n_p:], res):
            o_ref[...] = o.astype(o_ref.dtype)

    across = [len(o) == 4 for o in outs]
    res = pl.pallas_call(
        body, name=name, grid=(n_steps,),
        in_specs=[_row_spec(th, cw, ci) for (_, th, cw, ci) in rows] + [_full_spec(p.shape) for p in params],
        out_specs=[pl.BlockSpec((o[0], o[1]), lambda i: (0, i)) if ac else _row_spec(o[0], o[1], 0)
                   for o, ac in zip(outs, across)],
        out_shape=[jax.ShapeDtypeStruct((o[0], n_steps * o[1]) if ac else (n_steps * o[0], o[1]), o[2])
                   for o, ac in zip(outs, across)],
        compiler_params=_params(("parallel",)),
    )(*[r[0] for r in rows], *params)
    return res


def _rowwise_bwd(fn, rows, aux, params, douts, n_steps, name, row_dtypes=None, adds=()):
    n_r, n_a, n_p, n_d, n_add = len(rows), len(aux), len(params), len(douts), len(adds)
    row_dtypes = row_dtypes or (F32,) * n_r

    def body(*refs):
        it = iter(refs)
        r_vals = [next(it)[...].astype(F32) for _ in range(n_r)]
        a_vals = [next(it)[...].astype(F32) for _ in range(n_a)]
        p_vals = [next(it)[...].astype(F32) for _ in range(n_p)]
        d_vals = [next(it)[...].astype(F32) for _ in range(n_d)]
        add_vals = [next(it)[...].astype(F32) for _ in range(n_add)]
        dr_refs = [next(it) for _ in range(n_r)]
        dp_refs = [next(it) for _ in range(n_p)]

        def f(*rp):
            return tuple(fn(*rp[:n_r], *a_vals, *rp[n_r:]))

        _, vjp = jax.vjp(f, *r_vals, *p_vals)
        grads = list(vjp(tuple(d_vals)))
        for (ri, _), av in zip(adds, add_vals):
            grads[ri] = grads[ri] + av
        for dr_ref, g in zip(dr_refs, grads[:n_r]):
            dr_ref[...] = g.astype(dr_ref.dtype)

        @pl.when(pl.program_id(0) == 0)
        def _():
            for dp_ref in dp_refs:
                dp_ref[...] = jnp.zeros_like(dp_ref)

        for dp_ref, g in zip(dp_refs, grads[n_r:]):
            dp_ref[...] += g

    all_rows = list(rows) + list(aux) + list(douts) + [(arr,) + tuple(rows[ri][1:3]) + (0,) for ri, arr in adds]
    in_specs = ([_row_spec(th, cw, ci) for (_, th, cw, ci) in list(rows) + list(aux)]
                + [_full_spec(p.shape) for p in params]
                + [_row_spec(th, cw, ci) for (_, th, cw, ci) in all_rows[n_r + n_a:]])
    res = pl.pallas_call(
        body, name=name, grid=(n_steps,),
        in_specs=in_specs,
        out_specs=[_row_spec(th, cw, 0) for (_, th, cw, _) in rows] + [_full_spec(p.shape) for p in params],
        out_shape=[jax.ShapeDtypeStruct((n_steps * th, cw), dt) for (_, th, cw, _), dt in zip(rows, row_dtypes)]
        + [jax.ShapeDtypeStruct(p.shape, F32) for p in params],
        compiler_params=_params(("arbitrary",)),
    )(*[r[0] for r in list(rows) + list(aux)], *params, *[r[0] for r in all_rows[n_r + n_a:]])
    return res[:n_r], res[n_r:]


def _sigmoid(x):
    return lax.logistic(x)


def _silu(x):
    return x * _sigmoid(x)


def _rms(x, w=None, n=None):
    n = n or x.shape[-1]
    y = x * lax.rsqrt(jnp.sum(x * x, axis=-1, keepdims=True) * (1.0 / n) + EPS)
    return y if w is None else y * w


def _modulate(x, scale, shift):
    return _rms(x) * (1.0 + scale) + shift


def _softplus(x):
    return jnp.maximum(x, 0.0) + jnp.log1p(jnp.exp(-jnp.abs(x)))


@jax.custom_vjp
def _rot_half64(x):
    lane = lax.broadcasted_iota(jnp.int32, x.shape, 1)
    up = pltpu.roll(x, 96, 1)
    down = pltpu.roll(x, 32, 1)
    return jnp.where(lane < 32, up, jnp.where(lane < 64, down, 0.0))


_rot_half64.defvjp(lambda x: (_rot_half64(x), None), lambda _, g: (_rot_half64(g),))


def _rope128(x, cos_p, sin_p):
    return x * cos_p + _rot_half64(x) * sin_p


def _gdn_pre_fn(qkvc, kab, alog_p, dt_p):
    a = _silu(qkvc)
    qs, ks = [], []
    for h in range(HEADS):
        qh = a[:, HEAD_DIM * h:HEAD_DIM * (h + 1)]
        kh = a[:, 512 + HEAD_DIM * h:512 + HEAD_DIM * (h + 1)]
        qs.append(qh * lax.rsqrt(jnp.sum(qh * qh, axis=-1, keepdims=True) + EPS) * (HEAD_DIM ** -0.5))
        ks.append(kh * lax.rsqrt(jnp.sum(kh * kh, axis=-1, keepdims=True) + EPS))
    lane = lax.broadcasted_iota(jnp.int32, kab.shape, 1)
    g_full = -jnp.exp(alog_p) * _softplus(kab + dt_p)
    b_full = _sigmoid(kab)
    gb = jnp.where((lane >= 64) & (lane < 68), g_full, jnp.where((lane >= 68) & (lane < 72), b_full, 0.0))
    return jnp.concatenate(qs, axis=1), jnp.concatenate(ks, axis=1), a[:, 1024:1536], gb


INTRA_ROWS = (256, 128, 64)

_BNN = (((2,), (1,)), ((0,), (0,)))
_BNT = (((2,), (2,)), ((0,), (0,)))


def _split_bf16(a):
    hi = a.astype(BF16)
    return hi, (a - hi.astype(F32)).astype(BF16)


def _dot3_raw(a, b, dims):
    a_hi, a_lo = _split_bf16(a)
    b_hi, b_lo = _split_bf16(b)
    dot = lambda x_, y_: lax.dot_general(x_, y_, dims, preferred_element_type=F32)
    return dot(a_hi, b_hi) + (dot(a_hi, b_lo) + dot(a_lo, b_hi))


@functools.partial(jax.custom_vjp, nondiff_argnums=(2, 3))
def _dot3(a, b, nt, exact_bwd=True):
    return _dot3_raw(a, b, _BNT if nt else _BNN)


def _dot3_fwd(a, b, nt, exact_bwd):
    return _dot3_raw(a, b, _BNT if nt else _BNN), (a, b)


def _dot3_bwd(nt, exact_bwd, res, g):
    a, b = res
    if exact_bwd:
        dot = _dot3_raw
    else:
        dot = lambda x_, y_, d_: lax.dot_general(x_.astype(BF16), y_.astype(BF16), d_, preferred_element_type=F32)
    if nt:
        return dot(g, b, _BNN), dot(jnp.swapaxes(g, 1, 2), a, _BNN)
    return dot(g, b, _BNT), dot(jnp.swapaxes(a, 1, 2), g, _BNN)


_dot3.defvjp(_dot3_fwd, _dot3_bwd)


@functools.partial(jax.custom_vjp, nondiff_argnums=(2,))
def _bdot_b(a, b, nt):
    return lax.dot_general(a.astype(BF16), b.astype(BF16), _BNT if nt else _BNN, preferred_element_type=F32)


def _bdot_b_fwd(a, b, nt):
    return _bdot_b(a, b, nt), (a, b)


def _bdot_b_bwd(nt, res, g):
    a, b = res
    dot = lambda x_, y_, d_: lax.dot_general(x_.astype(BF16), y_.astype(BF16), d_, preferred_element_type=F32)
    if nt:
        return dot(g, b, _BNN), dot(jnp.swapaxes(g, 1, 2), a, _BNN)
    return dot(g, b, _BNT), dot(jnp.swapaxes(a, 1, 2), g, _BNN)


_bdot_b.defvjp(_bdot_b_fwd, _bdot_b_bwd)


@jax.custom_vjp
def _inverse_given(a_mat, inv):
    return inv


def _inverse_given_bwd(inv, g):
    inv_t = jnp.swapaxes(inv, 1, 2)
    return -_dot3_raw(_dot3_raw(inv_t, g, _BNN), inv_t, _BNN), jnp.zeros_like(inv)


_inverse_given.defvjp(lambda a_mat, inv: (inv, inv), _inverse_given_bwd)


def _intra_batched(q, k, v, g_col, b_col, inv_known=None):
    c = CHUNK
    nb = q.shape[0]
    row = lax.broadcasted_iota(jnp.int32, (1, c, c), 1)
    col = lax.broadcasted_iota(jnp.int32, (1, c, c), 2)
    incl, strict, eye = row >= col, row > col, row == col
    tri = jnp.broadcast_to(jnp.where(incl, 1.0, 0.0).astype(F32), (nb, c, c))
    ident = jnp.where(eye, 1.0, 0.0).astype(F32)
    g_wide = _dot3(tri, jnp.broadcast_to(g_col, (nb, c, HEAD_DIM)), False)
    g_i = g_wide[:, :, :c]
    g_j = jnp.sum(jnp.where(eye, g_i, 0.0), axis=1, keepdims=True)
    decay = jnp.where(incl, jnp.exp(jnp.where(incl, g_i - g_j, 0.0)), 0.0)
    kk = _bdot_b(k, k, True)
    a_mat = jnp.where(strict, b_col * kk * decay, 0.0)
    if inv_known is None:
        x_pow = -a_mat
        inv = ident + x_pow
        for _ in range(5):
            x_pow = _dot3(x_pow, x_pow, False, False)
            inv = inv + _dot3(inv, x_pow, False, False)
    else:
        inv = _inverse_given(a_mat, inv_known)
    e_wide = jnp.exp(g_wide)
    u = _dot3(inv, v * b_col, False)
    wk = _dot3(inv, k * b_col * e_wide, False)
    qk = _bdot_b(q, k, True) * decay
    last = lax.broadcasted_iota(jnp.int32, (1, c, HEAD_DIM), 1) == c - 1
    g_last = jnp.sum(jnp.where(last, g_wide, 0.0), axis=1, keepdims=True)
    qd = q * e_wide
    kd = k * jnp.exp(g_last - g_wide)
    gl = jnp.broadcast_to(jnp.exp(g_last), (nb, 8, HEAD_DIM))
    return u, wk, qd, kd, qk, gl, inv


def _gdn_intra_fn(q, k, v, gb, *inv_known):
    t = q.shape[0]
    nch = t // CHUNK
    lane = lax.broadcasted_iota(jnp.int32, gb.shape, 1)

    def heads_first(x_):
        return jnp.concatenate([x_[:, HEAD_DIM * h:HEAD_DIM * (h + 1)].reshape(nch, CHUNK, HEAD_DIM) for h in range(HEADS)],
                               axis=0)

    def column(first_lane):
        return jnp.concatenate([jnp.sum(jnp.where(lane == first_lane + h, gb, 0.0), axis=1, keepdims=True)
                                .reshape(nch, CHUNK, 1) for h in range(HEADS)], axis=0)

    known = jnp.concatenate([x_.reshape(nch, CHUNK, CHUNK) for x_ in inv_known], axis=0) if inv_known else None
    u, wk, qd, kd, qk, gl, inv = _intra_batched(heads_first(q), heads_first(k), heads_first(v), column(64), column(68), known)

    def rows_first(x_):
        r, w_ = x_.shape[1], x_.shape[2]
        return jnp.concatenate([x_[nch * h:nch * (h + 1)].reshape(nch * r, w_) for h in range(HEADS)], axis=1)

    per_head = lambda x_: [x_[nch * h:nch * (h + 1)].reshape(t, CHUNK) for h in range(HEADS)]
    outs = (rows_first(u), rows_first(wk), rows_first(qd), rows_first(kd), *per_head(qk), rows_first(gl))
    return outs if inv_known else outs + tuple(per_head(inv))


def _scan_step(s0, u, wk, qd, kd, qk, gl):
    v_new = u - _bdot_b(wk, s0, False)
    o = _bdot_b(qd, s0, False) + _bdot_b(qk, v_new, False)
    s1 = s0 * gl[:, 0:1, :] + _bdot_b(jnp.swapaxes(kd, 1, 2), v_new, False)
    return o, s1


def _mix_post_fn(o_a, z, o_b, gnw, onw):
    parts = [_rms(o_a[:, HEAD_DIM * h:HEAD_DIM * (h + 1)], gnw) * _silu(z[:, HEAD_DIM * h:HEAD_DIM * (h + 1)])
             for h in range(HEADS)]
    parts += [_rms(o_b[:, HEAD_DIM * h:HEAD_DIM * (h + 1)], onw) for h in range(HEADS)]
    return (jnp.concatenate(parts, axis=1),)


def _mla_pre_fn(ckv, cq, kab, cos_p, sin_p, qnw, kvnw, wuq, wukv, qn_w, qr_w, kn_w, kr_w):
    scale = (HEAD_DIM + ROPE) ** -0.5
    qf = _bdot(_rms(cq, qnw), wuq, "nn")
    kvf = _bdot(_rms(ckv, kvnw), wukv, "nn")
    lane = lax.broadcasted_iota(jnp.int32, kab.shape, 1)
    kr = _rope128(_rms(jnp.where(lane < ROPE, kab, 0.0), kr_w, n=ROPE), cos_p, sin_p)
    qs, ks = [], []
    for h in range(HEADS):
        qn = _rms(qf[:, 256 * h:256 * h + 128], qn_w) * scale
        qr = _rope128(_rms(qf[:, 256 * h + 128:256 * h + 256], qr_w, n=ROPE), cos_p, sin_p) * scale
        qs += [qn, qr]
        ks += [_rms(kvf[:, 128 * h:128 * (h + 1)], kn_w), kr]
    return jnp.concatenate(qs, axis=1), jnp.concatenate(ks, axis=1), kvf[:, 512:]


def _conv_fwd(proj, conv_w, tm, name):
    S = proj.shape[0]
    C = 1536
    nb = tm // 8

    def body(x_ref, prev_ref, w_ref, o_ref, ext_ref):
        i = pl.program_id(0)
        ext_ref[0:8, :] = jnp.where(i > 0, prev_ref[...], 0.0)
        ext_ref[8:, :] = x_ref[...]
        acc = jnp.zeros((tm, C), F32)
        for k in range(4):
            acc = acc + w_ref[k:k + 1, :] * ext_ref[pl.ds(5 + k, tm), :]
        o_ref[...] = acc

    return pl.pallas_call(
        body, name=name, grid=(S // tm,),
        in_specs=[pl.BlockSpec((tm, C), lambda i: (i, 0)),
                  pl.BlockSpec((8, C), lambda i: (jnp.maximum(i * nb - 1, 0), 0)),
                  pl.BlockSpec((4, C), lambda i: (0, 0))],
        out_specs=pl.BlockSpec((tm, C), lambda i: (i, 0)),
        out_shape=jax.ShapeDtypeStruct((S, C), F32),
        scratch_shapes=[pltpu.VMEM((tm + 8, C), F32)],
        compiler_params=_params(("arbitrary",)),
    )(proj, proj, conv_w)


def _conv_bwd(proj, dout, conv_w, tm, name):
    S = proj.shape[0]
    C = 1536
    nb = tm // 8
    n_steps = S // tm

    def body(x_ref, prev_ref, d_ref, next_ref, w_ref, dx_ref, dw_ref, xext_ref, dext_ref):
        i = pl.program_id(0)
        xext_ref[0:8, :] = jnp.where(i > 0, prev_ref[...], 0.0)
        xext_ref[8:, :] = x_ref[...]
        dext_ref[0:tm, :] = d_ref[...]
        dext_ref[tm:, :] = jnp.where(i < n_steps - 1, next_ref[...], 0.0)
        d = d_ref[...]
        acc = jnp.zeros((tm, C), F32)
        dws = []
        for k in range(4):
            acc = acc + w_ref[k:k + 1, :] * dext_ref[pl.ds(3 - k, tm), :]
            dws.append(jnp.sum(d * xext_ref[pl.ds(5 + k, tm), :], axis=0, keepdims=True))
        dx_ref[...] = acc

        @pl.when(i == 0)
        def _():
            dw_ref[...] = jnp.zeros_like(dw_ref)

        dw_ref[...] += jnp.concatenate(dws + [jnp.zeros((4, C), F32)], axis=0)

    return pl.pallas_call(
        body, name=name, grid=(n_steps,),
        in_specs=[pl.BlockSpec((tm, C), lambda i: (i, 0)),
                  pl.BlockSpec((8, C), lambda i: (jnp.maximum(i * nb - 1, 0), 0)),
                  pl.BlockSpec((tm, C), lambda i: (i, 0)),
                  pl.BlockSpec((8, C), lambda i: (jnp.minimum((i + 1) * nb, S // 8 - 1), 0)),
                  pl.BlockSpec((4, C), lambda i: (0, 0))],
        out_specs=[pl.BlockSpec((tm, C), lambda i: (i, 0)), pl.BlockSpec((8, C), lambda i: (0, 0))],
        out_shape=[jax.ShapeDtypeStruct((S, C), F32), jax.ShapeDtypeStruct((8, C), F32)],
        scratch_shapes=[pltpu.VMEM((tm + 8, C), F32), pltpu.VMEM((tm + 8, C), F32)],
        compiler_params=_params(("arbitrary",)),
    )(proj, proj, dout, dout, conv_w)


SCAN_CHUNKS = (4, 2, 1)


def _gdn_scan_fwd(u, wk, qd, kd, qks, gl, name):
    S = u.shape[0]
    nc = S // CHUNK
    cs = _pick(nc, SCAN_CHUNKS)
    W = HEADS * HEAD_DIM

    def body(u_ref, wk_ref, qd_ref, kd_ref, qk0, qk1, qk2, qk3, gl_ref, o_ref, sp_ref, s_ref):
        @pl.when(pl.program_id(0) == 0)
        def _():
            s_ref[...] = jnp.zeros_like(s_ref)

        state = s_ref[...]
        for c in range(cs):
            rows, gl_rows = slice(CHUNK * c, CHUNK * (c + 1)), slice(8 * c, 8 * (c + 1))
            sp_ref[c] = state
            o, state = _scan_step(state, _heads(u_ref, HEAD_DIM, rows), _heads(wk_ref, HEAD_DIM, rows),
                                  _heads(qd_ref, HEAD_DIM, rows), _heads(kd_ref, HEAD_DIM, rows),
                                  jnp.stack([r[rows, :] for r in (qk0, qk1, qk2, qk3)]), _heads(gl_ref, HEAD_DIM, gl_rows))
            for h in range(HEADS):
                o_ref[rows, HEAD_DIM * h:HEAD_DIM * (h + 1)] = o[h]
        s_ref[...] = state

    row = pl.BlockSpec((cs * CHUNK, W), lambda n: (n, 0))
    qk_spec = pl.BlockSpec((cs * CHUNK, CHUNK), lambda n: (n, 0))
    return pl.pallas_call(
        body, name=name, grid=(nc // cs,),
        in_specs=[row, row, row, row, qk_spec, qk_spec, qk_spec, qk_spec, pl.BlockSpec((cs * 8, W), lambda n: (n, 0))],
        out_specs=[row, pl.BlockSpec((cs, HEADS, HEAD_DIM, HEAD_DIM), lambda n: (n, 0, 0, 0))],
        out_shape=[jax.ShapeDtypeStruct((S, W), F32), jax.ShapeDtypeStruct((nc, HEADS, HEAD_DIM, HEAD_DIM), F32)],
        scratch_shapes=[pltpu.VMEM((HEADS, HEAD_DIM, HEAD_DIM), F32)],
        compiler_params=_params(("arbitrary",)),
    )(u, wk, qd, kd, *qks, gl)


def _gdn_scan_bwd(u, wk, qd, kd, qks, gl, s_prev, d_o, name):
    S = u.shape[0]
    nc = S // CHUNK
    cs = _pick(nc, SCAN_CHUNKS)
    nb = nc // cs
    W = HEADS * HEAD_DIM

    def body(u_ref, wk_ref, qd_ref, kd_ref, qk0, qk1, qk2, qk3, gl_ref, sp_ref, do_ref,
             du_ref, dwk_ref, dqd_ref, dkd_ref, dqk0, dqk1, dqk2, dqk3, dgl_ref, ds_ref):
        @pl.when(pl.program_id(0) == 0)
        def _():
            ds_ref[...] = jnp.zeros_like(ds_ref)

        d_state = ds_ref[...]
        for c in reversed(range(cs)):
            rows, gl_rows = slice(CHUNK * c, CHUNK * (c + 1)), slice(8 * c, 8 * (c + 1))
            _, vjp = jax.vjp(_scan_step, sp_ref[c], _heads(u_ref, HEAD_DIM, rows), _heads(wk_ref, HEAD_DIM, rows),
                             _heads(qd_ref, HEAD_DIM, rows), _heads(kd_ref, HEAD_DIM, rows),
                             jnp.stack([r[rows, :] for r in (qk0, qk1, qk2, qk3)]), _heads(gl_ref, HEAD_DIM, gl_rows))
            d_state, du, dwk, dqd, dkd, dqk, dgl = vjp((_heads(do_ref, HEAD_DIM, rows), d_state))
            for h, dqk_ref in enumerate((dqk0, dqk1, dqk2, dqk3)):
                sl = slice(HEAD_DIM * h, HEAD_DIM * (h + 1))
                du_ref[rows, sl] = du[h]
                dwk_ref[rows, sl] = dwk[h]
                dqd_ref[rows, sl] = dqd[h]
                dkd_ref[rows, sl] = dkd[h]
                dqk_ref[rows, :] = dqk[h]
                dgl_ref[gl_rows, sl] = dgl[h]
        ds_ref[...] = d_state

    rev = lambda n: (nb - 1 - n, 0)
    row = pl.BlockSpec((cs * CHUNK, W), rev)
    qk_spec = pl.BlockSpec((cs * CHUNK, CHUNK), rev)
    gl_spec = pl.BlockSpec((cs * 8, W), rev)
    qk_shape = jax.ShapeDtypeStruct((S, CHUNK), F32)
    row_shape = jax.ShapeDtypeStruct((S, W), F32)
    return pl.pallas_call(
        body, name=name, grid=(nb,),
        in_specs=[row, row, row, row, qk_spec, qk_spec, qk_spec, qk_spec, gl_spec,
                  pl.BlockSpec((cs, HEADS, HEAD_DIM, HEAD_DIM), lambda n: (nb - 1 - n, 0, 0, 0)), row],
        out_specs=[row, row, row, row, qk_spec, qk_spec, qk_spec, qk_spec, gl_spec],
        out_shape=[row_shape] * 4 + [qk_shape] * 4 + [jax.ShapeDtypeStruct((nc * 8, W), F32)],
        scratch_shapes=[pltpu.VMEM((HEADS, HEAD_DIM, HEAD_DIM), F32)],
        compiler_params=_params(("arbitrary",)),
    )(u, wk, qd, kd, *qks, gl, s_prev, d_o)


NEG = -1e30


def _chunk_mask(i, j, t, transposed=False):
    q_axis, k_axis = (1, 0) if transposed else (0, 1)
    r = (i * t + lax.broadcasted_iota(jnp.int32, (t, t), q_axis)) // CHUNK
    c = (j * t + lax.broadcasted_iota(jnp.int32, (t, t), k_axis)) // CHUNK
    return c <= r


def _tile_pairs(n, by_key):
    pairs = [(i, j) for j in range(n) for i in range(j, n)] if by_key else [(i, j) for i in range(n) for j in range(i + 1)]
    return jnp.asarray(np.array([p[0] for p in pairs], np.int32)), jnp.asarray(np.array([p[1] for p in pairs], np.int32))


def _heads(ref, width, rows=slice(None)):
    return jnp.stack([ref[rows, width * h:width * (h + 1)] for h in range(HEADS)])


def _bmm(a, b, dims):
    return lax.dot_general(a.astype(BF16), b.astype(BF16), dims, preferred_element_type=F32)


def _attn_fwd(q, k, v_t, t, name):
    S = q.shape[0]
    n = S // t
    qi, kj = _tile_pairs(n, by_key=False)

    def body(qi_ref, kj_ref, q_ref, k_ref, vt_ref, o_ref, lse_ref, m_ref, l_ref, acc_ref):
        i, j = qi_ref[pl.program_id(0)], kj_ref[pl.program_id(0)]

        @pl.when(j == 0)
        def _():
            m_ref[...] = jnp.full_like(m_ref, NEG)
            l_ref[...] = jnp.zeros_like(l_ref)
            acc_ref[...] = jnp.zeros_like(acc_ref)

        def update(masked):
            s_t = _bmm(_heads(k_ref, 256), _heads(q_ref, 256), _BNT)
            if masked:
                s_t = jnp.where(_chunk_mask(i, j, t, transposed=True)[None], s_t, NEG)
            m_old = m_ref[...]
            m_new = jnp.maximum(m_old, jnp.max(s_t, axis=1, keepdims=True))
            p_t = jnp.exp(s_t - m_new)
            alpha = jnp.exp(m_old - m_new)
            l_ref[...] = alpha * l_ref[...] + jnp.sum(p_t, axis=1, keepdims=True)
            v_heads = jnp.stack([vt_ref[HEAD_DIM * h:HEAD_DIM * (h + 1), :] for h in range(HEADS)])
            acc_ref[...] = alpha * acc_ref[...] + _bmm(v_heads, p_t, _BNN)
            m_ref[...] = m_new

        @pl.when(j < i)
        def _():
            update(False)

        @pl.when(j == i)
        def _():
            update(True)
            for h in range(HEADS):
                sl = slice(HEAD_DIM * h, HEAD_DIM * (h + 1))
                o_ref[:, sl] = jnp.transpose(acc_ref[h] / l_ref[h])
                lse_ref[:, sl] = jnp.transpose(jnp.broadcast_to(m_ref[h] + jnp.log(l_ref[h]), (HEAD_DIM, t)))

    row = lambda p, qi_, kj_: (qi_[p], 0)
    return pl.pallas_call(
        body, name=name,
        grid_spec=pltpu.PrefetchScalarGridSpec(
            num_scalar_prefetch=2, grid=(qi.shape[0],),
            in_specs=[pl.BlockSpec((t, HEADS * 256), row), pl.BlockSpec((t, HEADS * 256), lambda p, qi_, kj_: (kj_[p], 0)),
                      pl.BlockSpec((HEADS * HEAD_DIM, t), lambda p, qi_, kj_: (0, kj_[p]))],
            out_specs=[pl.BlockSpec((t, HEADS * HEAD_DIM), row)] * 2,
            scratch_shapes=[pltpu.VMEM((HEADS, 1, t), F32), pltpu.VMEM((HEADS, 1, t), F32),
                            pltpu.VMEM((HEADS, HEAD_DIM, t), F32)]),
        out_shape=[jax.ShapeDtypeStruct((S, HEADS * HEAD_DIM), F32)] * 2,
        compiler_params=_params(("arbitrary",)),
    )(qi, kj, q, k, v_t)


def _attn_stats(o, lse, d_o, t, name):
    S = o.shape[0]

    def body(o_ref, lse_ref, do_ref, st_ref):
        lane = lax.broadcasted_iota(jnp.int32, (t, HEAD_DIM), 1)
        stats = jnp.zeros((t, HEAD_DIM), F32)
        for h in range(HEADS):
            sl = slice(HEAD_DIM * h, HEAD_DIM * (h + 1))
            delta = jnp.sum(do_ref[:, sl] * o_ref[:, sl], axis=1, keepdims=True)
            stats = stats + jnp.where(lane == h, lse_ref[:, sl], 0.0) + jnp.where(lane == HEADS + h, delta, 0.0)
        st_ref[...] = jnp.transpose(stats)[0:8, :]

    row = pl.BlockSpec((t, HEADS * HEAD_DIM), lambda i: (i, 0))
    return pl.pallas_call(
        body, name=name, grid=(S // t,),
        in_specs=[row, row, row], out_specs=pl.BlockSpec((8, t), lambda i: (0, i)),
        out_shape=jax.ShapeDtypeStruct((8, S), F32),
        compiler_params=_params(("parallel",)),
    )(o, lse, d_o)


BWD_GROUP = 2


def _attn_bwd(q, k, v, d_o, stats, t, name):
    S = q.shape[0]
    n = S // t
    groups = HEADS // BWD_GROUP
    gq, gv = BWD_GROUP * 256, BWD_GROUP * HEAD_DIM
    qi, kj = _tile_pairs(n, by_key=True)
    n_pairs = qi.shape[0]
    st = stats.reshape(2, groups, BWD_GROUP, S).transpose(1, 0, 2, 3).reshape(groups, 2 * BWD_GROUP, S)
    st = jnp.pad(st, ((0, 0), (0, 8 - 2 * BWD_GROUP), (0, 0)))

    def heads(ref, width, rows=slice(None)):
        return jnp.stack([ref[rows, width * h:width * (h + 1)] for h in range(BWD_GROUP)])

    def body(qi_ref, kj_ref, q_ref, k_ref, v_ref, do_ref, st_ref, dq_hbm, dk_ref, dv_ref, dq_acc, sem):
        g, p = pl.program_id(0), pl.program_id(1)
        i, j = qi_ref[p], kj_ref[p]

        @pl.when(i == j)
        def _():
            dk_ref[...] = jnp.zeros_like(dk_ref)
            dv_ref[...] = jnp.zeros_like(dv_ref)

        def update(masked):
            qh, kh = heads(q_ref, 256), heads(k_ref, 256)
            d_out = heads(do_ref, HEAD_DIM)
            stv = st_ref[...]
            lse_row = jnp.stack([stv[h:h + 1, :] for h in range(BWD_GROUP)])
            delta_row = jnp.stack([stv[BWD_GROUP + h:BWD_GROUP + h + 1, :] for h in range(BWD_GROUP)])
            s_t = _bmm(kh, qh, _BNT)
            p_t = jnp.exp(s_t - lse_row)
            if masked:
                p_t = jnp.where(_chunk_mask(i, j, t, transposed=True)[None], p_t, 0.0)
            dv = _bmm(p_t, d_out, _BNN)
            dp_t = _bmm(heads(v_ref, HEAD_DIM), d_out, _BNT)
            ds_t = p_t * (dp_t - delta_row)
            dk = _bmm(ds_t, qh, _BNN)
            dq = _bmm(jnp.swapaxes(ds_t, 1, 2), kh, _BNN)
            rows = pl.ds(pl.multiple_of(i * t, t), t)
            for h in range(BWD_GROUP):
                dk_ref[:, 256 * h:256 * (h + 1)] += dk[h]
                dv_ref[:, HEAD_DIM * h:HEAD_DIM * (h + 1)] += dv[h]

            @pl.when(j == 0)
            def _():
                for h in range(BWD_GROUP):
                    dq_acc[rows, 256 * h:256 * (h + 1)] = dq[h]

            @pl.when(j > 0)
            def _():
                for h in range(BWD_GROUP):
                    dq_acc[rows, 256 * h:256 * (h + 1)] += dq[h]

        @pl.when(i == j)
        def _():
            update(True)

        @pl.when(i > j)
        def _():
            update(False)

        @pl.when(p == n_pairs - 1)
        def _():
            for gg in range(groups):
                @pl.when(g == gg)
                def _():
                    cp = pltpu.make_async_copy(dq_acc, dq_hbm.at[:, gq * gg:gq * (gg + 1)], sem)
                    cp.start()
                    cp.wait()

    q_blk = lambda g, p, qi_, kj_: (qi_[p], g)
    k_blk = lambda g, p, qi_, kj_: (kj_[p], g)
    return pl.pallas_call(
        body, name=name,
        grid_spec=pltpu.PrefetchScalarGridSpec(
            num_scalar_prefetch=2, grid=(groups, n_pairs),
            in_specs=[pl.BlockSpec((t, gq), q_blk), pl.BlockSpec((t, gq), k_blk), pl.BlockSpec((t, gv), k_blk),
                      pl.BlockSpec((t, gv), q_blk), pl.BlockSpec((None, 8, t), lambda g, p, qi_, kj_: (g, 0, qi_[p]))],
            out_specs=[pl.BlockSpec(memory_space=pl.ANY), pl.BlockSpec((t, gq), k_blk), pl.BlockSpec((t, gv), k_blk)],
            scratch_shapes=[pltpu.VMEM((S, gq), F32), pltpu.SemaphoreType.DMA]),
        out_shape=[jax.ShapeDtypeStruct((S, HEADS * 256), F32), jax.ShapeDtypeStruct((S, HEADS * 256), F32),
                   jax.ShapeDtypeStruct((S, HEADS * HEAD_DIM), F32)],
        compiler_params=_params(("arbitrary", "arbitrary")),
    )(qi, kj, q, k, v, d_o, st)


FFN_PIECE = 2 * D_FF // N_DEV
HID_PIECES = D_FF // FFN_PIECE


def _ffn_up(h, w8, name):
    S = h.shape[0]
    tm = _pick(S, MATMUL_ROWS)

    def body(h_ref, wg_ref, wu_ref, g_ref, u_ref, hid_ref, hid_t_ref):
        gate = _dot_raw(h_ref[...], wg_ref[...], "nn")
        up = _dot_raw(h_ref[...], wu_ref[...], "nn")
        sg = _sigmoid(gate)
        act = gate * sg
        hid = act * up
        g_ref[...] = (up * (sg * (1.0 + gate * (1.0 - sg)))).astype(BF16)
        u_ref[...] = act.astype(BF16)
        hid_ref[...] = hid.astype(BF16)
        hid_t_ref[...] = jnp.transpose(hid).astype(BF16)

    o_spec = pl.BlockSpec((None, tm, FFN_PIECE), lambda i, j: (j, i, 0))
    return pl.pallas_call(
        body, name=name, grid=(S // tm, HID_PIECES),
        in_specs=[pl.BlockSpec((tm, D_MODEL), lambda i, j: (i, 0)),
                  pl.BlockSpec((None, D_MODEL, FFN_PIECE), lambda i, j: (j, 0, 0)),
                  pl.BlockSpec((None, D_MODEL, FFN_PIECE), lambda i, j: (j + HID_PIECES, 0, 0))],
        out_specs=[o_spec] * 3 + [pl.BlockSpec((None, FFN_PIECE, tm), lambda i, j: (j, 0, i))],
        out_shape=[jax.ShapeDtypeStruct((HID_PIECES, S, FFN_PIECE), BF16)] * 3
        + [jax.ShapeDtypeStruct((HID_PIECES, FFN_PIECE, S), BF16)],
        compiler_params=_params(("parallel", "parallel")),
    )(h, w8, w8)


def _after_specs(after):
    return [] if after is None else [pl.BlockSpec(memory_space=pl.ANY)]


def _after_args(after):
    return [] if after is None else [after]


def _ffn_gw8(h, d_gate, d_up, name, after=None):
    S = h.shape[0]
    tm = 512
    tk = _pick(S, (512, 256, 128))
    nk = S // tk

    def body(h_ref, dg_ref, du_ref, *rest):
        o_ref, acc_ref = rest[-2:]
        k = pl.program_id(1)

        @pl.when(k == 0)
        def _():
            acc_ref[...] = jnp.zeros_like(acc_ref)

        h_t = jnp.transpose(h_ref[...])
        for p in range(HID_PIECES):
            acc_ref[p] += _dot_raw(h_t, dg_ref[p], "nn")
            acc_ref[HID_PIECES + p] += _dot_raw(h_t, du_ref[p], "nn")

        @pl.when(k == nk - 1)
        def _():
            o_ref[...] = acc_ref[...].astype(o_ref.dtype)

    d_spec = pl.BlockSpec((HID_PIECES, tk, FFN_PIECE), lambda i, k: (0, k, 0))
    return pl.pallas_call(
        body, name=name, grid=(D_MODEL // tm, nk),
        in_specs=[pl.BlockSpec((tk, tm), lambda i, k: (k, i)), d_spec, d_spec] + _after_specs(after),
        out_specs=pl.BlockSpec((2 * HID_PIECES, tm, FFN_PIECE), lambda i, k: (0, i, 0)),
        out_shape=jax.ShapeDtypeStruct((2 * HID_PIECES, D_MODEL, FFN_PIECE), BF16),
        scratch_shapes=[pltpu.VMEM((2 * HID_PIECES, tm, FFN_PIECE), F32)],
        compiler_params=_params(("parallel", "arbitrary")),
    )(h, d_gate, d_up, *_after_args(after))


def _gate_below(dx, y_ref, g_ref, coef, dy_ref, dg_ref):
    dy_ref[...] = (coef * g_ref[...] * dx).astype(dy_ref.dtype)
    dg_ref[...] += jnp.sum(coef * y_ref[...] * dx, axis=0, keepdims=True)


def _ffn_dh(d_gate, d_up, w8, x, d_out, scale, shift, name, after=None, below=None):
    S = d_gate.shape[1]
    tm = _pick(S, (512, 256, 128))
    n_below = 0 if below is None else 2

    def body(dg_ref, du_ref, wg_ref, wu_ref, x_ref, do_ref, sc_ref, sh_ref, *rest):
        below_in = rest[:n_below]
        outs = rest[len(rest) - 4 - n_below:]
        dx_ref, dsc_ref, dsh_ref = outs[:3]
        below_out, acc_ref = outs[3:3 + n_below], outs[-1]
        i, k = pl.program_id(0), pl.program_id(1)

        @pl.when(k == 0)
        def _():
            acc_ref[...] = jnp.zeros_like(acc_ref)

        acc_ref[...] += _dot_raw(dg_ref[...], wg_ref[...], "nt") + _dot_raw(du_ref[...], wu_ref[...], "nt")

        @pl.when((k == 0) & (i == 0))
        def _():
            for r in (dsc_ref, dsh_ref) + tuple(below_out[1:]):
                r[...] = jnp.zeros_like(r)

        @pl.when(k == HID_PIECES - 1)
        def _():
            _, vjp = jax.vjp(_modulate, x_ref[...], sc_ref[...], sh_ref[...])
            dx, dsc, dsh = vjp(acc_ref[...])
            dx = dx + do_ref[...]
            dx_ref[...] = dx
            dsc_ref[...] += dsc
            dsh_ref[...] += dsh
            if below is not None:
                _gate_below(dx, below_in[0], below_in[1], below[2], below_out[0], below_out[1])

    d_spec = pl.BlockSpec((None, tm, FFN_PIECE), lambda i, k: (k, i, 0))
    row = pl.BlockSpec((tm, D_MODEL), lambda i, k: (i, 0))
    par = pl.BlockSpec((1, D_MODEL), lambda i, k: (0, 0))
    row_shape, par_shape = jax.ShapeDtypeStruct((S, D_MODEL), F32), jax.ShapeDtypeStruct((1, D_MODEL), F32)
    return pl.pallas_call(
        body, name=name, grid=(S // tm, HID_PIECES),
        in_specs=[d_spec, d_spec,
                  pl.BlockSpec((None, D_MODEL, FFN_PIECE), lambda i, k: (k, 0, 0)),
                  pl.BlockSpec((None, D_MODEL, FFN_PIECE), lambda i, k: (k + HID_PIECES, 0, 0)),
                  row, row, par, par] + [row, par][:n_below] + _after_specs(after),
        out_specs=[row, par, par] + [row, par][:n_below],
        out_shape=[row_shape, par_shape, par_shape] + [jax.ShapeDtypeStruct((S, D_MODEL), BF16), par_shape][:n_below],
        scratch_shapes=[pltpu.VMEM((tm, D_MODEL), F32)],
        compiler_params=_params(("arbitrary", "arbitrary")),
    )(d_gate, d_up, w8, w8, x, d_out, scale, shift, *(below[:2] if below is not None else ()), *_after_args(after))


def _swiglu_bwd(d_hid, hid_by_gate, hid_by_up):
    return d_hid * hid_by_gate, d_hid * hid_by_up


def _adamw_math(w_, g_, m_, v_):
    m_ = ADAM_B1 * m_ + (1.0 - ADAM_B1) * g_
    v_ = ADAM_B2 * v_ + (1.0 - ADAM_B2) * (g_ * g_)
    m_hat = m_ / (1.0 - ADAM_B1 ** ADAM_STEP)
    v_hat = v_ / (1.0 - ADAM_B2 ** ADAM_STEP)
    return -ADAM_LR * (m_hat / (jnp.sqrt(v_hat) + ADAM_EPS) + ADAM_WD * w_), m_, v_


def _adamw(w, g, m, v, name):
    R, C = w.shape
    tr = _pick(R, (256, 176, 128, 64, 32, 16, 8))

    def body(w_ref, g_ref, m_ref, v_ref, d_ref, nm_ref, nv_ref):
        d_ref[...], nm_ref[...], nv_ref[...] = _adamw_math(w_ref[...], g_ref[...], m_ref[...], v_ref[...])

    spec = pl.BlockSpec((tr, C), lambda i: (i, 0))
    return pl.pallas_call(
        body, name=name, grid=(R // tr,),
        in_specs=[spec] * 4, out_specs=[spec] * 3,
        out_shape=[jax.ShapeDtypeStruct((R, C), F32)] * 3,
        compiler_params=_params(("parallel",)),
    )(w, g, m, v)


def _sum_adamw(parts, w, m, v, name):
    R, C = w.shape
    tr = _pick(R, (256, 176, 128, 64, 32, 16, 8))

    def body(p_ref, w_ref, m_ref, v_ref, g_ref, d_ref, nm_ref, nv_ref):
        g_ = p_ref[0].astype(F32)
        for d in range(1, N_DEV):
            g_ = g_ + p_ref[d].astype(F32)
        g_ref[...] = g_
        d_ref[...], nm_ref[...], nv_ref[...] = _adamw_math(w_ref[...], g_, m_ref[...], v_ref[...])

    spec = pl.BlockSpec((tr, C), lambda i: (i, 0))
    return pl.pallas_call(
        body, name=name, grid=(R // tr,),
        in_specs=[pl.BlockSpec((N_DEV, tr, C), lambda i: (0, i, 0)), spec, spec, spec], out_specs=[spec] * 4,
        out_shape=[jax.ShapeDtypeStruct((R, C), F32)] * 4,
        compiler_params=_params(("parallel",)),
    )(parts, w, m, v)


def _sum_devices(parts, name):
    _, R, C = parts.shape
    tr = _pick(R, (512, 256, 176, 128, 64, 32, 16, 8))

    def body(p_ref, o_ref):
        acc = p_ref[0].astype(F32)
        for d in range(1, N_DEV):
            acc = acc + p_ref[d].astype(F32)
        o_ref[...] = acc

    return pl.pallas_call(
        body, name=name, grid=(R // tr,),
        in_specs=[pl.BlockSpec((N_DEV, tr, C), lambda i: (0, i, 0))],
        out_specs=pl.BlockSpec((tr, C), lambda i: (i, 0)),
        out_shape=jax.ShapeDtypeStruct((R, C), F32),
        compiler_params=_params(("parallel",)),
    )(parts)


def _my_place():
    return lax.axis_index("x"), lax.axis_index("y"), lax.axis_index("c")


def _all_gather(blocks, name):
    n = len(blocks)

    def body(*refs):
        x_refs, out_refs = refs[:n], refs[n:2 * n]
        send_sems, recv_sems, local_sems = refs[2 * n:]
        x, y, c = _my_place()
        me, sibling = (x, y, c), (x, y, 1 - c)
        chips = [(1 - x, y), (x, 1 - y), (1 - x, 1 - y)]

        def copy(a, k, blk, to, own=False):
            slot = out_refs[a].at[4 * blk[0] + 2 * blk[1] + blk[2]]
            return pltpu.make_async_remote_copy(
                src_ref=x_refs[a] if own else slot, dst_ref=slot,
                send_sem=send_sems.at[7 * a + k], recv_sem=recv_sems.at[7 * a + k], device_id=to, device_id_type=MESH)

        mine = [pltpu.make_async_copy(x_refs[a], out_refs[a].at[4 * x + 2 * y + c], local_sems.at[a]) for a in range(n)]
        for cp in mine:
            cp.start()
        first = []
        for j, chip in enumerate(chips):
            first += [copy(a, 1 + j, me, (*chip, c), own=True) for a in range(n)]
        first += [copy(a, 0, me, sibling, own=True) for a in range(n)]
        for cp in first:
            cp.start()
        passed = []
        for j, chip in enumerate(chips):
            for a in range(n):
                copy(a, 1 + j, (*chip, c), me).wait_recv()
                passed.append(copy(a, 4 + j, (*chip, c), sibling))
                passed[-1].start()
        for a in range(n):
            copy(a, 0, sibling, me).wait_recv()
        for j, chip in enumerate(chips):
            for a in range(n):
                copy(a, 4 + j, (*chip, 1 - c), me).wait_recv()
        for cp in first + passed:
            cp.wait_send()
        for cp in mine:
            cp.wait()

    return pl.pallas_call(
        body, name=name,
        out_shape=[jax.ShapeDtypeStruct((N_DEV,) + b.shape, b.dtype) for b in blocks],
        in_specs=[pl.BlockSpec(memory_space=pl.ANY)] * n,
        out_specs=[pl.BlockSpec(memory_space=pl.ANY)] * n,
        scratch_shapes=[pltpu.SemaphoreType.DMA((7 * n,)), pltpu.SemaphoreType.DMA((7 * n,)), pltpu.SemaphoreType.DMA((n,))],
    )(*blocks)


def _all_to_all(pieces, name):
    n = len(pieces)

    def body(*refs):
        x_refs, out_refs = refs[:n], refs[n:2 * n]
        send_sems, recv_sems, local_sems = refs[2 * n:]
        x, y, c = _my_place()
        me = 4 * x + 2 * y + c
        mine = [pltpu.make_async_copy(x_refs[a].at[me], out_refs[a].at[me], local_sems.at[a]) for a in range(n)]
        for cp in mine:
            cp.start()
        copies = []
        for k in (2, 4, 6, 3, 5, 7, 1):
            px = 1 - x if k & 4 else x
            py = 1 - y if k & 2 else y
            pc = 1 - c if k & 1 else c
            peer = 4 * px + 2 * py + pc
            for a in range(n):
                copies.append(pltpu.make_async_remote_copy(
                    src_ref=x_refs[a].at[peer], dst_ref=out_refs[a].at[me],
                    send_sem=send_sems.at[7 * a + k - 1], recv_sem=recv_sems.at[7 * a + k - 1],
                    device_id=(px, py, pc), device_id_type=MESH))
        for cp in copies:
            cp.start()
        for cp in copies:
            cp.wait_recv()
        for cp in copies:
            cp.wait_send()
        for cp in mine:
            cp.wait()

    return pl.pallas_call(
        body, name=name,
        out_shape=[jax.ShapeDtypeStruct(p.shape, p.dtype) for p in pieces],
        in_specs=[pl.BlockSpec(memory_space=pl.ANY)] * n,
        out_specs=[pl.BlockSpec(memory_space=pl.ANY)] * n,
        scratch_shapes=[pltpu.SemaphoreType.DMA((7 * n,)), pltpu.SemaphoreType.DMA((7 * n,)), pltpu.SemaphoreType.DMA((n,))],
    )(*pieces)


def _peers():
    x, y, c = _my_place()
    out = []
    for k in (2, 4, 6, 3, 5, 7, 1):
        px = 1 - x if k & 4 else x
        py = 1 - y if k & 2 else y
        pc = 1 - c if k & 1 else c
        out.append((k, (px, py, pc), 4 * px + 2 * py + pc))
    return out


def _exchange_copies(x_refs, land_refs, send_sems, recv_sems, scatter):
    x, y, c = _my_place()
    me = 4 * x + 2 * y + c
    starts, arrivals = [], []
    for k, place, peer in _peers():
        for a, (x_ref, land_ref) in enumerate(zip(x_refs, land_refs)):
            sems = dict(send_sem=send_sems.at[7 * a + k - 1], recv_sem=recv_sems.at[7 * a + k - 1],
                        device_id=place, device_id_type=MESH)
            src = x_ref.at[peer] if scatter else x_ref
            starts.append(pltpu.make_async_remote_copy(src_ref=src, dst_ref=land_ref.at[me], **sems))
            arrivals.append(pltpu.make_async_remote_copy(src_ref=src, dst_ref=land_ref.at[peer], **sems))
    return starts, arrivals


def _exchange_start(arrays, scatter, name):
    n = len(arrays)
    hbm = pl.BlockSpec(memory_space=pltpu.HBM)
    sem = pl.BlockSpec(memory_space=pltpu.SEMAPHORE)
    lands = [lax.empty(a.shape if scatter else (N_DEV,) + a.shape, a.dtype) for a in arrays]

    def body(*refs):
        x_refs, land_refs = refs[:n], refs[n:2 * n]
        send_sems, recv_sems = refs[2 * n], refs[2 * n + 1]
        token = refs[-1]
        starts, _ = _exchange_copies(x_refs, land_refs, send_sems, recv_sems, scatter)
        for cp in starts:
            cp.start()
        token[...] = jnp.zeros_like(token)

    res = pl.pallas_call(
        body, name=name,
        out_shape=(pltpu.SemaphoreType.DMA((7 * n,)), pltpu.SemaphoreType.DMA((7 * n,)),
                   *[pltpu.HBM(a.shape, a.dtype) for a in arrays], *[pltpu.HBM(l.shape, l.dtype) for l in lands],
                   jax.ShapeDtypeStruct((8, 128), F32)),
        in_specs=[hbm] * (2 * n),
        out_specs=(sem, sem, *[hbm] * (2 * n), pl.BlockSpec(memory_space=pltpu.VMEM)),
        input_output_aliases={i: 2 + i for i in range(2 * n)},
        compiler_params=pltpu.CompilerParams(has_side_effects=pltpu.SideEffectType.DATAFLOW_SIDE_EFFECTING),
    )(*[pltpu.with_memory_space_constraint(a, pltpu.HBM) for a in arrays],
      *[pltpu.with_memory_space_constraint(l, pltpu.HBM) for l in lands])
    return res[0], res[1], list(res[2:2 + n]), list(res[2 + n:2 + 2 * n]), res[-1]


def _exchange_wait(handles, scatter, after, name):
    send_sems, recv_sems, arrays, lands, _ = handles
    n = len(arrays)
    hbm = pl.BlockSpec(memory_space=pltpu.HBM)
    sem = pl.BlockSpec(memory_space=pltpu.SEMAPHORE)

    def body(*refs):
        x_refs, land_refs = refs[:n], refs[n:2 * n]
        send_s, recv_s = refs[2 * n], refs[2 * n + 1]
        starts, arrivals = _exchange_copies(x_refs, land_refs, send_s, recv_s, scatter)
        for cp in arrivals:
            cp.wait_recv()
        for cp in starts:
            cp.wait_send()

    res = pl.pallas_call(
        body, name=name,
        out_shape=(*[pltpu.HBM(a.shape, a.dtype) for a in arrays], *[pltpu.HBM(l.shape, l.dtype) for l in lands]),
        in_specs=[hbm] * (2 * n) + [sem, sem, pl.BlockSpec(memory_space=pl.ANY)],
        out_specs=tuple([hbm] * (2 * n)),
        input_output_aliases={i: i for i in range(2 * n)},
        compiler_params=pltpu.CompilerParams(has_side_effects=pltpu.SideEffectType.DATAFLOW_SIDE_EFFECTING),
    )(*arrays, *lands, send_sems, recv_sems, after)
    me = 4 * lax.axis_index("x") + 2 * lax.axis_index("y") + lax.axis_index("c")
    out = []
    for src, got in zip(res[:n], res[n:]):
        zeros = (0,) * (got.ndim - 1)
        own = lax.dynamic_slice(src, (me,) + zeros, (1,) + src.shape[1:]) if scatter else src[None]
        out.append(lax.dynamic_update_slice(got, own, (me,) + zeros))
    return out


def _pad_lanes(v, at=0, width=128):
    return jnp.pad(v, ((0, 0), (at, width - at - v.shape[1])))


def _pack_weights(P):
    W = {}
    w = P["w_in"]
    W["wp"] = jnp.concatenate([w[:, :2048], w[:, 2440:2696], w[:, 2056:2440], w[:, 2696:2760], w[:, 2048:2056],
                               jnp.zeros((D_MODEL, N_IN_PACKED - N_IN), w.dtype)], axis=1).astype(BF16)
    W["conv_w"] = P["gdn_conv_w"].astype(F32)
    W["alog_p"] = _pad_lanes(P["gdn_a_log"], 64)
    W["dt_p"] = _pad_lanes(P["gdn_dt_bias"], 64)
    W["gnw"] = P["gdn_norm_w"]
    W["qnw"] = P["mla_q_norm_w"]
    W["kvnw"] = P["mla_kv_norm_w"]
    uq = P["mla_w_uq"].reshape(Q_LORA, HEADS, HEAD_DIM + ROPE)
    W["wuq"] = jnp.pad(uq, ((0, 0), (0, 0), (0, 256 - HEAD_DIM - ROPE))).reshape(Q_LORA, HEADS * 256).astype(BF16)
    ukv = P["mla_w_ukv"].reshape(KV_LORA, HEADS, 2, HEAD_DIM)
    W["wukv"] = ukv.transpose(0, 2, 1, 3).reshape(KV_LORA, 2 * HEADS * HEAD_DIM).astype(BF16)
    W["qn_w"] = P["qkn_q_nope"]
    W["qr_w"] = _pad_lanes(P["qkn_q_rope"])
    W["kn_w"] = P["qkn_k_nope"]
    W["kr_w"] = _pad_lanes(P["qkn_k_rope"])
    W["onw"] = P["mla_out_norm_w"]
    W["wout"] = P["w_out"].astype(BF16)
    return W


def _unpack_grads(G):
    g = G["wp"]
    uq = G["wuq"].reshape(Q_LORA, HEADS, 256)[:, :, :HEAD_DIM + ROPE].reshape(Q_LORA, HEADS * (HEAD_DIM + ROPE))
    ukv = G["wukv"].reshape(KV_LORA, 2, HEADS, HEAD_DIM).transpose(0, 2, 1, 3).reshape(KV_LORA, 2 * HEADS * HEAD_DIM)
    return {
        "w_in": jnp.concatenate([g[:, :2048], g[:, 2752:2760], g[:, 2304:2688], g[:, 2048:2304], g[:, 2688:2752]], axis=1),
        "gdn_conv_w": G["conv_w"], "gdn_a_log": G["alog_p"][:, 64:68], "gdn_dt_bias": G["dt_p"][:, 64:68],
        "gdn_norm_w": G["gnw"], "mla_q_norm_w": G["qnw"], "mla_w_uq": uq, "mla_kv_norm_w": G["kvnw"], "mla_w_ukv": ukv,
        "qkn_q_nope": G["qn_w"], "qkn_q_rope": G["qr_w"][:, :ROPE], "qkn_k_nope": G["kn_w"], "qkn_k_rope": G["kr_w"][:, :ROPE],
        "mla_out_norm_w": G["onw"], "w_out": G["wout"],
    }


def _rope_tables(positions):
    half = ROPE // 2
    inv_freq = ROPE_BASE ** (-jnp.arange(half, dtype=F32) / half)
    ang = positions.astype(F32)[:, None] * inv_freq
    cos, sin = jnp.cos(ang), jnp.sin(ang)
    zeros = jnp.zeros((positions.shape[0], 128 - ROPE), F32)
    return jnp.concatenate([cos, cos, zeros], axis=1), jnp.concatenate([-sin, sin, zeros], axis=1)


def _mod_fn(x, scale, shift):
    return (_modulate(x, scale, shift),)


def _ffn_down_loss(hid, wo4, x, gate_w, target, name):
    S = x.shape[0]
    tb = _pick(S, (512, 256, 128))
    n = S // tb

    def body(hid_ref, wo_ref, x_ref, g_ref, t_ref, dx_ref, df_ref, dg_ref, l_ref, acc_ref):
        i, k = pl.program_id(0), pl.program_id(1)

        @pl.when(k == 0)
        def _():
            acc_ref[...] = jnp.zeros_like(acc_ref)

        acc_ref[...] += _dot_raw(hid_ref[...], wo_ref[...], "nn")

        @pl.when((k == 0) & (i == 0))
        def _():
            dg_ref[...] = jnp.zeros_like(dg_ref)
            l_ref[...] = jnp.zeros_like(l_ref)

        @pl.when(k == HID_PIECES - 1)
        def _():
            f = acc_ref[...]
            diff = x_ref[...] + 0.5 * g_ref[...] * f - t_ref[...]
            dx = diff * (1.0 / D_MODEL)
            dx_ref[...] = dx
            df_ref[...] = (0.5 * g_ref[...] * dx).astype(df_ref.dtype)
            dg_ref[...] += jnp.sum(0.5 * f * dx, axis=0, keepdims=True)
            l_ref[...] += jnp.sum(diff * diff, axis=0, keepdims=True)

        @pl.when((k == HID_PIECES - 1) & (i == n - 1))
        def _():
            l_ref[...] = jnp.full(l_ref.shape, (0.5 / D_MODEL) * jnp.sum(l_ref[...]), F32)

    row = pl.BlockSpec((tb, D_MODEL), lambda i, k: (i, 0))
    par = pl.BlockSpec((1, D_MODEL), lambda i, k: (0, 0))
    return pl.pallas_call(
        body, name=name, grid=(n, HID_PIECES),
        in_specs=[pl.BlockSpec((None, tb, FFN_PIECE), lambda i, k: (k, i, 0)),
                  pl.BlockSpec((None, FFN_PIECE, D_MODEL), lambda i, k: (k, 0, 0)), row, par, row],
        out_specs=[row, row, par, par],
        out_shape=[jax.ShapeDtypeStruct((S, D_MODEL), F32), jax.ShapeDtypeStruct((S, D_MODEL), BF16),
                   jax.ShapeDtypeStruct((1, D_MODEL), F32), jax.ShapeDtypeStruct((1, D_MODEL), F32)],
        scratch_shapes=[pltpu.VMEM((tb, D_MODEL), F32)],
        compiler_params=_params(("arbitrary", "arbitrary")),
    )(hid, wo4, x, gate_w, target)


def _ffn_fwd(x, scale, shift, gate_w, w8, wo4, tag, target=None):
    S = x.shape[0]
    tm = _pick(S, (512, 256, 128))
    (h,) = _rowwise(_mod_fn, [(x, tm, D_MODEL, 0)], [scale, shift], [(tm, D_MODEL, BF16)], S // tm, tag + "_mod")
    by_gate, by_up, hid, hid_t = _ffn_up(h, w8, tag + "_up")
    if target is not None:
        dx_out, df, d_gate_w, loss_row = _ffn_down_loss(hid, wo4, x, gate_w, target, tag + "_down")
        return (dx_out, loss_row), (h, by_gate, by_up, hid_t, None, df, d_gate_w)
    tb = _pick(S, MATMUL_ROWS)
    mn = pl.BlockSpec((tb, D_MODEL), lambda i, j, k: (i, j))
    f, x_out = _mmg(hid, wo4, "nn", name=tag + "_down", grid=(S // tb, 1, HID_PIECES),
                    a_spec=pl.BlockSpec((None, tb, FFN_PIECE), lambda i, j, k: (k, i, 0)),
                    b_spec=pl.BlockSpec((None, FFN_PIECE, D_MODEL), lambda i, j, k: (k, 0, j)),
                    out_spec=mn, out_shapes=[jax.ShapeDtypeStruct((S, D_MODEL), F32)] * 2, acc_shape=(tb, D_MODEL),
                    extras=[x, gate_w], extra_specs=[mn, pl.BlockSpec((1, D_MODEL), lambda i, j, k: (0, j))],
                    epi=lambda acc, x_, g_: (acc, x_ + 0.5 * g_ * acc))
    return x_out, (h, by_gate, by_up, hid_t, f, None, None)


def _ffn_bwd(d_out, x, scale, shift, gate_w, w8, wo4, saved, tag, grad_ready, below=None):
    h, gate, up, hid_t, f, df, d_gate_w = saved
    S = x.shape[0]
    tm = _pick(S, (512, 256, 128))
    tk = _pick(S, (512, 256, 128))
    n = S // tm
    if df is None:
        (df,), (d_gate_w,) = _rowwise_bwd(lambda f_, g_: (0.5 * g_ * f_,), [(f, tm, D_MODEL, 0)], [], [gate_w],
                                          [(d_out, tm, D_MODEL, 0)], n, tag + "_dres", row_dtypes=(BF16,))
    tb = _pick(S, MATMUL_ROWS)
    piece = pl.BlockSpec((None, tb, FFN_PIECE), lambda i, j, k: (j, i, 0))
    d_gate, d_up = _mmg(df, wo4, "nt", name=tag + "_ddown", grid=(S // tb, HID_PIECES, 1),
                        a_spec=pl.BlockSpec((tb, D_MODEL), lambda i, j, k: (i, 0)),
                        b_spec=pl.BlockSpec((None, FFN_PIECE, D_MODEL), lambda i, j, k: (j, 0, 0)),
                        out_spec=piece, out_shapes=[jax.ShapeDtypeStruct((HID_PIECES, S, FFN_PIECE), BF16)] * 2,
                        acc_shape=(tb, FFN_PIECE), extras=[gate, up], extra_specs=[piece, piece], epi=_swiglu_bwd)
    tk = _pick(S, MATMUL_ROWS)
    g_wo4 = _mmg(hid_t, df, "nn", name=tag + "_gwo", grid=(HID_PIECES, 1, S // tk),
                 a_spec=pl.BlockSpec((None, FFN_PIECE, tk), lambda i, j, k: (i, 0, k)),
                 b_spec=pl.BlockSpec((tk, D_MODEL), lambda i, j, k: (k, j)),
                 out_spec=pl.BlockSpec((None, FFN_PIECE, D_MODEL), lambda i, j, k: (i, 0, j)),
                 out_shapes=[jax.ShapeDtypeStruct((HID_PIECES, FFN_PIECE, D_MODEL), BF16)], acc_shape=(FFN_PIECE, D_MODEL))
    g_w8 = _ffn_gw8(h, d_gate, d_up, tag + "_gw8", after=grad_ready("wo4", g_wo4))
    res = _ffn_dh(d_gate, d_up, w8, x, d_out, scale, shift, tag + "_dh", after=grad_ready("w8", g_w8), below=below)
    return (res[0], res[1], res[2], d_gate_w) + tuple(res[3:])


def _dproj_dmod(d_proj, wp, x, d_out, scale, shift, name, below=None):
    S, K = d_proj.shape
    tm = _pick(S, (512, 256, 128))
    tk = _pick(K, (1408, 512, 256, 128))
    nk = K // tk
    n_below = 0 if below is None else 2

    def body(dp_ref, w_ref, x_ref, do_ref, sc_ref, sh_ref, *rest):
        below_in = rest[:n_below]
        dx_ref, dsc_ref, dsh_ref = rest[n_below:n_below + 3]
        below_out, acc_ref = rest[n_below + 3:n_below + 3 + n_below], rest[-1]
        i, k = pl.program_id(0), pl.program_id(1)

        @pl.when(k == 0)
        def _():
            acc_ref[...] = jnp.zeros_like(acc_ref)

        acc_ref[...] += _dot_raw(dp_ref[...], w_ref[...], "nt")

        @pl.when((k == 0) & (i == 0))
        def _():
            for r in (dsc_ref, dsh_ref) + tuple(below_out[1:]):
                r[...] = jnp.zeros_like(r)

        @pl.when(k == nk - 1)
        def _():
            _, vjp = jax.vjp(_modulate, x_ref[...], sc_ref[...], sh_ref[...])
            dx, dsc, dsh = vjp(acc_ref[...])
            dx = dx + do_ref[...]
            dx_ref[...] = dx
            dsc_ref[...] += dsc
            dsh_ref[...] += dsh
            if below is not None:
                _gate_below(dx, below_in[0], below_in[1], below[2], below_out[0], below_out[1])

    row = pl.BlockSpec((tm, D_MODEL), lambda i, k: (i, 0))
    par = pl.BlockSpec((1, D_MODEL), lambda i, k: (0, 0))
    row_shape, par_shape = jax.ShapeDtypeStruct((S, D_MODEL), F32), jax.ShapeDtypeStruct((1, D_MODEL), F32)
    return pl.pallas_call(
        body, name=name, grid=(S // tm, nk),
        in_specs=[pl.BlockSpec((tm, tk), lambda i, k: (i, k)), pl.BlockSpec((D_MODEL, tk), lambda i, k: (0, k)),
                  row, row, par, par] + [row, par][:n_below],
        out_specs=[row, par, par] + [row, par][:n_below],
        out_shape=[row_shape, par_shape, par_shape] + [jax.ShapeDtypeStruct((S, D_MODEL), BF16), par_shape][:n_below],
        scratch_shapes=[pltpu.VMEM((tm, D_MODEL), F32)],
        compiler_params=_params(("arbitrary", "arbitrary")),
    )(d_proj, wp, x, d_out, scale, shift, *(below[:2] if below is not None else ()))


def _mixer_fwd(x1, scale, shift, gate_w, cos_p, sin_p, W):
    S = x1.shape[0]
    tm = _pick(S, (512, 256, 128))
    tv = _pick(S, (256, 128))
    ta = _pick(S, (512, 256, 128))
    nc = S // CHUNK
    (h2,) = _rowwise(_mod_fn, [(x1, tm, D_MODEL, 0)], [scale, shift], [(tm, D_MODEL, BF16)], S // tm, "mix_mod")
    proj = _mm(h2, W["wp"], "nn", name="mix_proj")
    qkvc = _conv_fwd(proj, W["conv_w"], tv, "gdn_conv")
    kab = (proj, tv, 128, 21)
    q_a, k_a, v_a, gb = _rowwise(_gdn_pre_fn, [(qkvc, tv, 1536, 0), kab], [W["alog_p"], W["dt_p"]],
                                 [(tv, 512, F32)] * 3 + [(tv, 128, F32)], S // tv, "gdn_pre")
    ti = _pick(S, INTRA_ROWS)
    intra = _rowwise(_gdn_intra_fn, [(q_a, ti, 512, 0), (k_a, ti, 512, 0), (v_a, ti, 512, 0), (gb, ti, 128, 0)],
                     [], [(ti, 512, F32)] * 4 + [(ti, CHUNK, F32)] * 4 + [(ti // 8, 512, F32)] + [(ti, CHUNK, F32)] * 4,
                     S // ti, "gdn_intra")
    u, wk, qd, kd, qks, gl, invs = intra[0], intra[1], intra[2], intra[3], tuple(intra[4:8]), intra[8], tuple(intra[9:])
    o_a, s_prev = _gdn_scan_fwd(u, wk, qd, kd, qks, gl, "gdn_scan")
    mla_params = [W["qnw"], W["kvnw"], W["wuq"], W["wukv"], W["qn_w"], W["qr_w"], W["kn_w"], W["kr_w"]]
    def mla_pre_with_vt(*a):
        q_, k_, v_ = _mla_pre_fn(*a)
        return q_, k_, v_, jnp.transpose(v_)

    q_b, k_b, v_b, vt_b = _rowwise(mla_pre_with_vt,
                                   [(proj, tv, 256, 8), (proj, tv, 384, 6), kab, (cos_p, tv, 128, 0), (sin_p, tv, 128, 0)],
                                   mla_params, [(tv, 1024, BF16), (tv, 1024, BF16), (tv, 512, BF16), (512, tv, BF16, "across")],
                                   S // tv, "mla_pre")
    o_b, lse = _attn_fwd(q_b, k_b, vt_b, ta, "mla_attn")
    (mixed,) = _rowwise(_mix_post_fn, [(o_a, tv, 512, 0), (proj, tv, 512, 3), (o_b, tv, 512, 0)], [W["gnw"], W["onw"]],
                        [(tv, D_MODEL, BF16)], S // tv, "mix_post")
    y, x2 = _mm(mixed, W["wout"], "nn", name="mix_out", out_dtypes=(F32, F32), extras=[x1], extra_params=[gate_w],
                epi=lambda acc, x_, g_: (acc, x_ + g_ * acc))
    saved = (h2, proj, qkvc, q_a, k_a, v_a, gb, u, wk, qd, kd, qks, gl, invs, s_prev, o_a, q_b, k_b, v_b, o_b, lse, mixed, y)
    return x2, saved


def _mixer_bwd(d_out, dy, x1, scale, shift, cos_p, sin_p, W, saved, below):
    (h2, proj, qkvc, q_a, k_a, v_a, gb, u, wk, qd, kd, qks, gl, invs, s_prev, o_a, q_b, k_b, v_b, o_b, lse, mixed, y) = saved
    S = x1.shape[0]
    tm = _pick(S, (512, 256, 128))
    tv = _pick(S, (256, 128))
    ta = _pick(S, (512, 256, 128))
    nc = S // CHUNK
    G = {}
    d_mixed = _mm(dy, W["wout"], "nt", name="mix_dout")
    G["wout"] = _mm(mixed, dy, "tn", name="mix_gwout")
    (do_a, dz, do_b), (G["gnw"], G["onw"]) = _rowwise_bwd(
        _mix_post_fn, [(o_a, tv, 512, 0), (proj, tv, 512, 3), (o_b, tv, 512, 0)], [], [W["gnw"], W["onw"]],
        [(d_mixed, tv, D_MODEL, 0)], S // tv, "mix_dpost")
    stats = _attn_stats(o_b, lse, do_b, ta, "mla_stats")
    dq_b, dk_b, dv_b = _attn_bwd(q_b, k_b, v_b, do_b, stats, ta, "mla_dattn")
    kab = (proj, tv, 128, 21)
    mla_params = [W["qnw"], W["kvnw"], W["wuq"], W["wukv"], W["qn_w"], W["qr_w"], W["kn_w"], W["kr_w"]]
    (d_ckv, d_cq, d_kab), mla_grads = _rowwise_bwd(
        _mla_pre_fn, [(proj, tv, 256, 8), (proj, tv, 384, 6), kab], [(cos_p, tv, 128, 0), (sin_p, tv, 128, 0)], mla_params,
        [(dq_b, tv, 1024, 0), (dk_b, tv, 1024, 0), (dv_b, tv, 512, 0)], S // tv, "mla_dpre")
    for key, g in zip(("qnw", "kvnw", "wuq", "wukv", "qn_w", "qr_w", "kn_w", "kr_w"), mla_grads):
        G[key] = g
    scan_grads = _gdn_scan_bwd(u, wk, qd, kd, qks, gl, s_prev, do_a, "gdn_dscan")
    ti = _pick(S, INTRA_ROWS)
    intra_douts = [(scan_grads[i], ti, 512, 0) for i in range(4)] + [(scan_grads[4 + i], ti, CHUNK, 0) for i in range(4)]
    intra_douts.append((scan_grads[8], ti // 8, 512, 0))
    (dq_a, dk_a, dv_a, d_gb), _ = _rowwise_bwd(
        _gdn_intra_fn, [(q_a, ti, 512, 0), (k_a, ti, 512, 0), (v_a, ti, 512, 0), (gb, ti, 128, 0)],
        [(x_, ti, CHUNK, 0) for x_ in invs], [], intra_douts, S // ti, "gdn_dintra")
    (d_qkvc, d_kab), (G["alog_p"], G["dt_p"]) = _rowwise_bwd(
        _gdn_pre_fn, [(qkvc, tv, 1536, 0), kab], [], [W["alog_p"], W["dt_p"]],
        [(dq_a, tv, 512, 0), (dk_a, tv, 512, 0), (dv_a, tv, 512, 0), (d_gb, tv, 128, 0)], S // tv, "gdn_dpre",
        adds=[(1, d_kab)])
    d_qkv, g_conv = _conv_bwd(proj, d_qkvc, W["conv_w"], tv, "gdn_dconv")
    G["conv_w"] = g_conv[:4]
    d_proj = jnp.concatenate([d_qkv, dz, d_ckv, d_cq, d_kab], axis=1).astype(BF16)
    G["wp"] = _mm(h2, d_proj, "tn", name="mix_gwp")
    dx1, G["s2"], G["sh2"], d_below, dg_below = _dproj_dmod(d_proj, W["wp"], x1, d_out, scale, shift, "mix_dproj", below=below)
    return dx1, d_below, dg_below, G


def _local_step(x, target, mod, cos_p, sin_p, W1, mixer_weights, ffn2_weights, ffn_grad_ready, mixer_grads_ready):
    sh1, s1, g1, sh2, s2, g2, sh3, s3, g3 = [mod[:, D_MODEL * i:D_MODEL * (i + 1)] for i in range(N_MOD)]
    x1, saved1 = _ffn_fwd(x, s1, sh1, g1, W1["f1_w8"], W1["f1_wo4"], "ffn1")
    W = mixer_weights(x1)
    x2, saved2 = _mixer_fwd(x1, s2, sh2, g2, cos_p, sin_p, W)
    W.update(ffn2_weights(x2))
    (dx3, loss_row), saved3 = _ffn_fwd(x2, s3, sh3, g3, W["f2_w8"], W["f2_wo4"], "ffn2", target=target)
    dx2, d_s3, d_sh3, d_g3, dy, d_g2 = _ffn_bwd(dx3, x2, s3, sh3, g3, W["f2_w8"], W["f2_wo4"], saved3, "ffn2",
                                                ffn_grad_ready("f2"), below=(saved2[-1], g2, 1.0))
    dx1, df1, d_g1, G = _mixer_bwd(dx2, dy, x1, s2, sh2, cos_p, sin_p, W, saved2, below=(saved1[4], g1, 0.5))
    d_sh2, d_s2 = G.pop("sh2"), G.pop("s2")
    saved1 = saved1[:5] + (df1, d_g1 + mixer_grads_ready(G))
    dx, d_s1, d_sh1, d_g1 = _ffn_bwd(dx1, x, s1, sh1, g1, W1["f1_w8"], W1["f1_wo4"], saved1, "ffn1", ffn_grad_ready("f1"))
    d_mod = jnp.concatenate([d_sh1, d_s1, d_g1, d_sh2, d_s2, d_g2, d_sh3, d_s3, d_g3], axis=1)
    return loss_row, dx, d_mod


WEIGHT_NAMES = ("w_ada", "b_ada", "ffn1_w_in", "ffn1_w_out", "w_in", "gdn_conv_w", "gdn_a_log", "gdn_dt_bias", "gdn_norm_w",
                "mla_q_norm_w", "mla_w_uq", "mla_kv_norm_w", "mla_w_ukv", "qkn_q_nope", "qkn_q_rope", "qkn_k_nope",
                "qkn_k_rope", "mla_out_norm_w", "w_out", "ffn2_w_in", "ffn2_w_out")
FFN_SHARDED = ("ffn1_w_in", "ffn1_w_out", "ffn2_w_in", "ffn2_w_out")
SHEETED = (("w_in", "col"), ("gdn_conv_w", "col"), ("mla_w_uq", "col"), ("mla_w_ukv", "col"), ("w_out", "row"))
MOD_ROWS = N_MOD * D_MODEL // 128
SMALL = {"gdn_a_log": (MOD_ROWS, 1, 64, 4), "gdn_dt_bias": (MOD_ROWS + 1, 1, 64, 4), "gdn_norm_w": (MOD_ROWS + 2, 1, 0, 128),
         "mla_q_norm_w": (MOD_ROWS + 3, 3, 0, 384), "mla_kv_norm_w": (MOD_ROWS + 6, 2, 0, 256),
         "qkn_q_nope": (MOD_ROWS + 8, 1, 0, 128), "qkn_q_rope": (MOD_ROWS + 9, 1, 0, 64), "qkn_k_nope": (MOD_ROWS + 10, 1, 0, 128),
         "qkn_k_rope": (MOD_ROWS + 11, 1, 0, 64), "mla_out_norm_w": (MOD_ROWS + 12, 1, 0, 128)}
LOSS_ROW = MOD_ROWS + 13
CONV_ROW, CONV_ROWS = 88, 4 * 1536 // 128
SHEET_ROWS = CONV_ROW + CONV_ROWS


def _to_sheet(flat, dtype, sublanes):
    n = flat.shape[-1]
    unit = sublanes * 128
    pad = (-n) % unit
    flat = jnp.pad(flat.astype(dtype), [(0, 0)] * (flat.ndim - 1) + [(0, pad)])
    return flat.reshape(flat.shape[:-1] + ((n + pad) // 128, 128))


def _small_sheet(b_like, small):
    sheet = jnp.zeros((SHEET_ROWS, 128), F32).at[:MOD_ROWS].set(b_like.reshape(MOD_ROWS, 128))
    for name, (row, rows, lane, n) in SMALL.items():
        v = small[name].reshape(1, n)
        if rows == 1:
            sheet = sheet.at[row, lane:lane + n].set(v[0])
        else:
            sheet = sheet.at[row:row + rows].set(v.reshape(rows, 128))
    return sheet


def _from_small_sheet(sheet):
    out = {"b_ada": sheet[:MOD_ROWS].reshape(1, N_MOD * D_MODEL)}
    for name, (row, rows, lane, n) in SMALL.items():
        out[name] = sheet[row, lane:lane + n].reshape(1, n) if rows == 1 else sheet[row:row + rows].reshape(1, n)
    return out


def kernel(x, c, positions, w_ada, b_ada, ffn1_w_in, ffn1_w_out, w_in, gdn_conv_w, gdn_a_log, gdn_dt_bias, gdn_norm_w, mla_q_norm_w, mla_w_uq, mla_kv_norm_w, mla_w_ukv, qkn_q_nope, qkn_q_rope, qkn_k_nope, qkn_k_rope, mla_out_norm_w, w_out, ffn2_w_in, ffn2_w_out, loss_target, m_w_ada, m_b_ada, m_ffn1_w_in, m_ffn1_w_out, m_w_in, m_gdn_conv_w, m_gdn_a_log, m_gdn_dt_bias, m_gdn_norm_w, m_mla_q_norm_w, m_mla_w_uq, m_mla_kv_norm_w, m_mla_w_ukv, m_qkn_q_nope, m_qkn_q_rope, m_qkn_k_nope, m_qkn_k_rope, m_mla_out_norm_w, m_w_out, m_ffn2_w_in, m_ffn2_w_out, v_w_ada, v_b_ada, v_ffn1_w_in, v_ffn1_w_out, v_w_in, v_gdn_conv_w, v_gdn_a_log, v_gdn_dt_bias, v_gdn_norm_w, v_mla_q_norm_w, v_mla_w_uq, v_mla_kv_norm_w, v_mla_w_ukv, v_qkn_q_nope, v_qkn_q_rope, v_qkn_k_nope, v_qkn_k_rope, v_mla_out_norm_w, v_w_out, v_ffn2_w_in, v_ffn2_w_out):
    args = locals()
    w = {n: args[n] for n in WEIGHT_NAMES}
    m = {n: args["m_" + n] for n in WEIGHT_NAMES}
    v = {n: args["v_" + n] for n in WEIGHT_NAMES}
    me = 4 * lax.axis_index("x") + 2 * lax.axis_index("y") + lax.axis_index("c")
    cols = N_MOD * D_MODEL // N_DEV
    shard = {n: w[n][0] for n in FFN_SHARDED + tuple(s[0] for s in SHEETED)}

    sc = c * _sigmoid(c)
    first = _to_sheet(jnp.concatenate([sc.reshape(-1), shard["gdn_conv_w"].reshape(-1)]), F32, 8)
    (first_all,) = _all_gather([first], "gather_c")
    sc_all = first_all[:, :D_MODEL // 128].reshape(N_DEV, D_MODEL)
    n_taps = shard["gdn_conv_w"].size
    conv_all = first_all.reshape(N_DEV, -1)[:, D_MODEL:D_MODEL + n_taps].reshape(N_DEV, 4, -1)
    b_mine = lax.dynamic_slice(b_ada, (0, me * cols), (1, cols))
    mod_cols = _mm(sc_all, w_ada[0], "nn", name="ada_mod", extra_params=[b_mine], epi=lambda acc, b_: (acc + b_,))
    (mod_all,) = _all_to_all([_to_sheet(mod_cols, F32, 8)], "scatter_mod")
    mod = mod_all.reshape(N_DEV, -1)[:, :cols].reshape(1, N_MOD * D_MODEL)

    f1_shards, mod = lax.optimization_barrier(([shard["ffn1_w_in"].astype(BF16), shard["ffn1_w_out"].astype(BF16)], mod))
    f1_w8, f1_out = _all_gather(f1_shards, "gather_w1")
    travel = [s for s in SHEETED if s[0] != "gdn_conv_w"]
    tied = lax.optimization_barrier(([shard[n].astype(BF16) for n, _ in travel], f1_w8))
    f1_w8 = tied[1]
    mixer_w = _exchange_start(tied[0], False, "gather_wm_start")
    ffn2_w = _exchange_start([shard["ffn2_w_in"].astype(BF16) + mixer_w[4][0:1, 0:1].astype(BF16),
                              shard["ffn2_w_out"].astype(BF16)], False, "gather_w2_start")
    mod = mod + ffn2_w[4][0:1, 0:1]
    W1 = dict(f1_w8=f1_w8, f1_wo4=f1_out.reshape(HID_PIECES, FFN_PIECE, D_MODEL))

    def mixer_weights(after):
        got = _exchange_wait(mixer_w, False, after, "gather_wm_wait")
        P = {n: jnp.concatenate(list(g), axis=1) if kind == "col" else g.reshape(-1, g.shape[-1])
             for (n, kind), g in zip(travel, got)}
        P["gdn_conv_w"] = jnp.concatenate(list(conv_all), axis=1)
        for n in SMALL:
            P[n] = w[n]
        return _pack_weights(P)

    def ffn2_weights(after):
        f2_w8, f2_out = _exchange_wait(ffn2_w, False, after, "gather_w2_wait")
        return dict(f2_w8=f2_w8, f2_wo4=f2_out.reshape(HID_PIECES, FFN_PIECE, D_MODEL))

    pending, small_grads = {}, {}

    def ffn_grad_ready(tag):
        def ready(which, g):
            pieces = g if which == "w8" else g.reshape((N_DEV,) + shard["ffn1_w_out"].shape)
            pending[tag + which] = _exchange_start([pieces], True, "scatter_%s_%s_start" % (tag, which))
            return pending[tag + which][4]
        return ready

    def mixer_grads_ready(G):
        g_full = _unpack_grads(G)
        small_grads.update({n: g_full[n] for n in SMALL})
        small_grads["gdn_conv_w"] = g_full["gdn_conv_w"]
        pieces = []
        for n, kind in travel:
            r, cc = shard[n].shape
            g = g_full[n].astype(BF16)
            pieces.append(jnp.stack([g[:, cc * p:cc * (p + 1)] for p in range(N_DEV)]) if kind == "col"
                          else g.reshape(N_DEV, r, cc))
        pending["mixer"] = _exchange_start(pieces, True, "scatter_mx_start")
        return pending["mixer"][4][0:1, 0:1]

    cos_p, sin_p = _rope_tables(positions[0])
    loss_row, dx, d_mod = _local_step(x[0], loss_target[0], mod, cos_p, sin_p, W1, mixer_weights, ffn2_weights,
                                      ffn_grad_ready, mixer_grads_ready)

    sheet = _small_sheet(d_mod, small_grads).at[LOSS_ROW].set(loss_row[0, :128])
    sheet = sheet.at[CONV_ROW:CONV_ROW + CONV_ROWS].set(small_grads["gdn_conv_w"].reshape(CONV_ROWS, 128))
    (sheets,) = _all_gather([sheet], "gather_small")
    summed = _sum_devices(sheets, "sum_small")
    d_mod_all = sheets[:, :MOD_ROWS].reshape(N_DEV, N_MOD * D_MODEL)
    d_mod_mine = lax.dynamic_slice(d_mod_all, (0, me * cols), (N_DEV, cols))
    grads = _from_small_sheet(summed)
    grads["w_ada"] = _mm(sc_all, d_mod_mine, "tn", name="ada_gw", hi=True)
    conv_taps = shard["gdn_conv_w"].shape[1]
    grads["gdn_conv_w"] = lax.dynamic_slice(summed[CONV_ROW:CONV_ROW + CONV_ROWS].reshape(4, -1), (0, me * conv_taps),
                                            (4, conv_taps))
    loss = summed[LOSS_ROW, 0]

    delta, new_m, new_v = {}, {}, {}
    arrived = {}
    for n, key in zip(FFN_SHARDED, ("f1w8", "f1wo4", "f2w8", "f2wo4")):
        (arrived[n],) = _exchange_wait(pending[key], True, summed, "scatter_%s_wait" % key)
    arrived.update(zip([n for n, _ in travel], _exchange_wait(pending["mixer"], True, summed, "scatter_mx_wait")))
    for n, parts in arrived.items():
        grads[n], delta[n], new_m[n], new_v[n] = _sum_adamw(parts, w[n][0], m[n][0], v[n][0], "adamw_" + n)
    for n in ("w_ada", "gdn_conv_w"):
        delta[n], new_m[n], new_v[n] = _adamw(w[n][0], grads[n], m[n][0], v[n][0], "adamw_" + n)
    small_in = [_small_sheet(t["b_ada"], t) for t in (w, grads, m, v)]
    for res, out in zip(_adamw(*small_in, "adamw_small"), (delta, new_m, new_v)):
        out.update(_from_small_sheet(res))

    def shaped(d):
        return [d[n].reshape(w[n].shape) for n in WEIGHT_NAMES]

    return (loss, dx[None], *shaped(grads), *shaped(delta), *shaped(new_m), *shaped(new_v))
```

```python
import functools

import jax
import jax.numpy as jnp
import numpy as np
from jax import lax
from jax.experimental import pallas as pl
from jax.experimental.pallas import tpu as pltpu

F32 = jnp.float32
BF16 = jnp.bfloat16

D_MODEL = 1024
D_FF = 2816
N_MOD = 9
HEADS = 4
HEAD_DIM = 128
CHUNK = 64
EPS = 1e-6
ROPE = 64
Q_LORA = 384
KV_LORA = 256
N_IN = 2760
N_IN_PACKED = 2816
ROPE_BASE = 10000.0
LOG2_E = 1.4426950408889634
N_DEV = 8

ADAM_LR = 0.001
ADAM_B1 = 0.9
ADAM_B2 = 0.999
ADAM_EPS = 1e-08
ADAM_WD = 0.01
ADAM_STEP = 10

VMEM_LIMIT_BYTES = 56 * 1024 * 1024
MATMUL_ROWS = (1024, 512, 256, 128)
MESH = pl.DeviceIdType.MESH


def _params(sem=None):
    return pltpu.CompilerParams(dimension_semantics=sem, vmem_limit_bytes=VMEM_LIMIT_BYTES)


def _pick(dim, prefs):
    for p in prefs:
        if dim % p == 0:
            return p
    return dim


_DIMS = {"nn": (((1,), (0,)), ((), ())), "nt": (((1,), (1,)), ((), ())), "tn": (((0,), (0,)), ((), ()))}


def _dot_raw(a, b, mode):
    return lax.dot_general(a.astype(BF16), b.astype(BF16), _DIMS[mode], preferred_element_type=F32)


def _dot_hi(a, b, mode="nn"):
    return lax.dot_general(a, b, _DIMS[mode], precision=lax.Precision.HIGHEST, preferred_element_type=F32)


@functools.partial(jax.custom_vjp, nondiff_argnums=(2,))
def _bdot(a, b, mode):
    return _dot_raw(a, b, mode)


def _bdot_fwd(a, b, mode):
    return _dot_raw(a, b, mode), (a, b)


def _bdot_bwd(mode, res, g):
    a, b = res
    if mode == "nn":
        return _dot_raw(g, b, "nt"), _dot_raw(a, g, "tn")
    if mode == "nt":
        return _dot_raw(g, b, "nn"), _dot_raw(g, a, "tn")
    return _dot_raw(b, g, "nt"), _dot_raw(a, g, "nn")


_bdot.defvjp(_bdot_fwd, _bdot_bwd)


def _mm(a, b, mode, *, name, out_dtypes=(F32,), epi=None, extras=(), extra_params=(), hi=False,
        tm=None, tn=None, tk=None):
    if mode == "nn":
        (M, K), (_, N) = a.shape, b.shape
    elif mode == "nt":
        (M, K), (N, _) = a.shape, b.shape
    else:
        (K, M), (_, N) = a.shape, b.shape
    tm = tm or _pick(M, (512, 1408, 256, 128) if mode == "tn" else MATMUL_ROWS + (384, 352))
    tn = tn or _pick(N, (1024, 1408, 768, 512, 384, 256, 128))
    tk = tk or _pick(K, (1024, 1408, 512, 384, 256, 128))
    a_spec = {"nn": pl.BlockSpec((tm, tk), lambda i, j, k: (i, k)), "nt": pl.BlockSpec((tm, tk), lambda i, j, k: (i, k)),
              "tn": pl.BlockSpec((tk, tm), lambda i, j, k: (k, i))}[mode]
    b_spec = {"nn": pl.BlockSpec((tk, tn), lambda i, j, k: (k, j)), "nt": pl.BlockSpec((tn, tk), lambda i, j, k: (j, k)),
              "tn": pl.BlockSpec((tk, tn), lambda i, j, k: (k, j))}[mode]
    mn_spec = pl.BlockSpec((tm, tn), lambda i, j, k: (i, j))
    return _mmg(a, b, mode, name=name, grid=(M // tm, N // tn, K // tk), a_spec=a_spec, b_spec=b_spec, out_spec=mn_spec,
                out_shapes=[jax.ShapeDtypeStruct((M, N), dt) for dt in out_dtypes], acc_shape=(tm, tn), epi=epi,
                extras=list(extras) + list(extra_params),
                extra_specs=[mn_spec] * len(extras) + [pl.BlockSpec((1, tn), lambda i, j, k: (0, j))] * len(extra_params),
                hi=hi)


def _mmg(a, b, mode, *, name, grid, a_spec, b_spec, out_spec, out_shapes, acc_shape, epi=None, extras=(),
         extra_specs=(), hi=False):
    nk = grid[2]
    n_e, n_o = len(extras), len(out_shapes)

    def body(*refs):
        a_ref, b_ref = refs[:2]
        e_refs = refs[2:2 + n_e]
        o_refs = refs[2 + n_e:2 + n_e + n_o]
        acc_ref = refs[-1]
        k = pl.program_id(2)

        @pl.when(k == 0)
        def _():
            acc_ref[...] = jnp.zeros_like(acc_ref)

        if hi:
            acc_ref[...] += _dot_hi(a_ref[...].astype(F32), b_ref[...].astype(F32), mode)
        else:
            acc_ref[...] += _dot_raw(a_ref[...], b_ref[...], mode)

        @pl.when(k == nk - 1)
        def _():
            acc = acc_ref[...]
            outs = (acc,) if epi is None else epi(acc, *[e[...].astype(F32) for e in e_refs])
            for o_ref, o in zip(o_refs, outs):
                o_ref[...] = o.astype(o_ref.dtype)

    outs = pl.pallas_call(
        body, name=name, grid=grid,
        in_specs=[a_spec, b_spec] + list(extra_specs),
        out_specs=[out_spec] * n_o,
        out_shape=list(out_shapes),
        scratch_shapes=[pltpu.VMEM(acc_shape, F32)],
        compiler_params=_params(("parallel", "parallel", "arbitrary")),
    )(a, b, *extras)
    return outs if n_o > 1 else outs[0]


def _row_spec(th, cw, ci):
    return pl.BlockSpec((th, cw), lambda i: (i, ci))


def _full_spec(shape):
    return pl.BlockSpec(shape, lambda i: (0,) * len(shape))


def _rowwise(fn, rows, params, outs, n_steps, name):
    n_r, n_p, n_o = len(rows), len(params), len(outs)

    def body(*refs):
        vals = [r[...].astype(F32) for r in refs[:n_r + n_p]]
        res = fn(*vals)
        for o_ref, o in zip(refs[n_r + n_p:], res):
            o_ref[...] = o.astype(o_ref.dtype)

    across = [len(o) == 4 for o in outs]
    res = pl.pallas_call(
        body, name=name, grid=(n_steps,),
        in_specs=[_row_spec(th, cw, ci) for (_, th, cw, ci) in rows] + [_full_spec(p.shape) for p in params],
        out_specs=[pl.BlockSpec((o[0], o[1]), lambda i: (0, i)) if ac else _row_spec(o[0], o[1], 0)
                   for o, ac in zip(outs, across)],
        out_shape=[jax.ShapeDtypeStruct((o[0], n_steps * o[1]) if ac else (n_steps * o[0], o[1]), o[2])
                   for o, ac in zip(outs, across)],
        compiler_params=_params(("parallel",)),
    )(*[r[0] for r in rows], *params)
    return res


def _rowwise_bwd(fn, rows, aux, params, douts, n_steps, name, row_dtypes=None, adds=()):
    n_r, n_a, n_p, n_d, n_add = len(rows), len(aux), len(params), len(douts), len(adds)
    row_dtypes = row_dtypes or (F32,) * n_r

    def body(*refs):
        it = iter(refs)
        r_vals = [next(it)[...].astype(F32) for _ in range(n_r)]
        a_vals = [next(it)[...].astype(F32) for _ in range(n_a)]
        p_vals = [next(it)[...].astype(F32) for _ in range(n_p)]
        d_vals = [next(it)[...].astype(F32) for _ in range(n_d)]
        add_vals = [next(it)[...].astype(F32) for _ in range(n_add)]
        dr_refs = [next(it) for _ in range(n_r)]
        dp_refs = [next(it) for _ in range(n_p)]

        def f(*rp):
            return tuple(fn(*rp[:n_r], *a_vals, *rp[n_r:]))

        _, vjp = jax.vjp(f, *r_vals, *p_vals)
        grads = list(vjp(tuple(d_vals)))
        for (ri, _), av in zip(adds, add_vals):
            grads[ri] = grads[ri] + av
        for dr_ref, g in zip(dr_refs, grads[:n_r]):
            dr_ref[...] = g.astype(dr_ref.dtype)

        @pl.when(pl.program_id(0) == 0)
        def _():
            for dp_ref in dp_refs:
                dp_ref[...] = jnp.zeros_like(dp_ref)

        for dp_ref, g in zip(dp_refs, grads[n_r:]):
            dp_ref[...] += g

    all_rows = list(rows) + list(aux) + list(douts) + [(arr,) + tuple(rows[ri][1:3]) + (0,) for ri, arr in adds]
    in_specs = ([_row_spec(th, cw, ci) for (_, th, cw, ci) in list(rows) + list(aux)]
                + [_full_spec(p.shape) for p in params]
                + [_row_spec(th, cw, ci) for (_, th, cw, ci) in all_rows[n_r + n_a:]])
    res = pl.pallas_call(
        body, name=name, grid=(n_steps,),
        in_specs=in_specs,
        out_specs=[_row_spec(th, cw, 0) for (_, th, cw, _) in rows] + [_full_spec(p.shape) for p in params],
        out_shape=[jax.ShapeDtypeStruct((n_steps * th, cw), dt) for (_, th, cw, _), dt in zip(rows, row_dtypes)]
        + [jax.ShapeDtypeStruct(p.shape, F32) for p in params],
        compiler_params=_params(("arbitrary",)),
    )(*[r[0] for r in list(rows) + list(aux)], *params, *[r[0] for r in all_rows[n_r + n_a:]])
    return res[:n_r], res[n_r:]


def _sigmoid(x):
    return lax.logistic(x)


def _silu(x):
    return x * _sigmoid(x)


def _rms(x, w=None, n=None):
    n = n or x.shape[-1]
    y = x * lax.rsqrt(jnp.sum(x * x, axis=-1, keepdims=True) * (1.0 / n) + EPS)
    return y if w is None else y * w


def _modulate(x, scale, shift):
    return _rms(x) * (1.0 + scale) + shift


def _softplus(x):
    return jnp.maximum(x, 0.0) + jnp.log1p(jnp.exp(-jnp.abs(x)))


@jax.custom_vjp
def _rot_half64(x):
    lane = lax.broadcasted_iota(jnp.int32, x.shape, 1)
    up = pltpu.roll(x, 96, 1)
    down = pltpu.roll(x, 32, 1)
    return jnp.where(lane < 32, up, jnp.where(lane < 64, down, 0.0))


_rot_half64.defvjp(lambda x: (_rot_half64(x), None), lambda _, g: (_rot_half64(g),))


def _rope128(x, cos_p, sin_p):
    return x * cos_p + _rot_half64(x) * sin_p


def _gdn_pre_fn(qkvc, kab, alog_p, dt_p):
    a = _silu(qkvc)
    qs, ks = [], []
    for h in range(HEADS):
        qh = a[:, HEAD_DIM * h:HEAD_DIM * (h + 1)]
        kh = a[:, 512 + HEAD_DIM * h:512 + HEAD_DIM * (h + 1)]
        qs.append(qh * lax.rsqrt(jnp.sum(qh * qh, axis=-1, keepdims=True) + EPS) * (HEAD_DIM ** -0.5))
        ks.append(kh * lax.rsqrt(jnp.sum(kh * kh, axis=-1, keepdims=True) + EPS))
    lane = lax.broadcasted_iota(jnp.int32, kab.shape, 1)
    g_full = -jnp.exp(alog_p) * _softplus(kab + dt_p)
    b_full = _sigmoid(kab)
    gb = jnp.where((lane >= 64) & (lane < 68), g_full, jnp.where((lane >= 68) & (lane < 72), b_full, 0.0))
    return jnp.concatenate(qs, axis=1), jnp.concatenate(ks, axis=1), a[:, 1024:1536], gb


INTRA_ROWS = (256, 128, 64)

_BNN = (((2,), (1,)), ((0,), (0,)))
_BNT = (((2,), (2,)), ((0,), (0,)))


def _split_bf16(a):
    hi = a.astype(BF16)
    return hi, (a - hi.astype(F32)).astype(BF16)


def _dot3_raw(a, b, dims):
    a_hi, a_lo = _split_bf16(a)
    b_hi, b_lo = _split_bf16(b)
    dot = lambda x_, y_: lax.dot_general(x_, y_, dims, preferred_element_type=F32)
    return dot(a_hi, b_hi) + (dot(a_hi, b_lo) + dot(a_lo, b_hi))


@functools.partial(jax.custom_vjp, nondiff_argnums=(2, 3))
def _dot3(a, b, nt, exact_bwd=True):
    return _dot3_raw(a, b, _BNT if nt else _BNN)


def _dot3_fwd(a, b, nt, exact_bwd):
    return _dot3_raw(a, b, _BNT if nt else _BNN), (a, b)


def _dot3_bwd(nt, exact_bwd, res, g):
    a, b = res
    if exact_bwd:
        dot = _dot3_raw
    else:
        dot = lambda x_, y_, d_: lax.dot_general(x_.astype(BF16), y_.astype(BF16), d_, preferred_element_type=F32)
    if nt:
        return dot(g, b, _BNN), dot(jnp.swapaxes(g, 1, 2), a, _BNN)
    return dot(g, b, _BNT), dot(jnp.swapaxes(a, 1, 2), g, _BNN)


_dot3.defvjp(_dot3_fwd, _dot3_bwd)


@functools.partial(jax.custom_vjp, nondiff_argnums=(2,))
def _bdot_b(a, b, nt):
    return lax.dot_general(a.astype(BF16), b.astype(BF16), _BNT if nt else _BNN, preferred_element_type=F32)


def _bdot_b_fwd(a, b, nt):
    return _bdot_b(a, b, nt), (a, b)


def _bdot_b_bwd(nt, res, g):
    a, b = res
    dot = lambda x_, y_, d_: lax.dot_general(x_.astype(BF16), y_.astype(BF16), d_, preferred_element_type=F32)
    if nt:
        return dot(g, b, _BNN), dot(jnp.swapaxes(g, 1, 2), a, _BNN)
    return dot(g, b, _BNT), dot(jnp.swapaxes(a, 1, 2), g, _BNN)


_bdot_b.defvjp(_bdot_b_fwd, _bdot_b_bwd)


@jax.custom_vjp
def _inverse_given(a_mat, inv):
    return inv


def _inverse_given_bwd(inv, g):
    inv_t = jnp.swapaxes(inv, 1, 2)
    return -_dot3_raw(_dot3_raw(inv_t, g, _BNN), inv_t, _BNN), jnp.zeros_like(inv)


_inverse_given.defvjp(lambda a_mat, inv: (inv, inv), _inverse_given_bwd)


def _intra_batched(q, k, v, g_col, b_col, inv_known=None):
    c = CHUNK
    nb = q.shape[0]
    row = lax.broadcasted_iota(jnp.int32, (1, c, c), 1)
    col = lax.broadcasted_iota(jnp.int32, (1, c, c), 2)
    incl, strict, eye = row >= col, row > col, row == col
    tri = jnp.broadcast_to(jnp.where(incl, 1.0, 0.0).astype(F32), (nb, c, c))
    ident = jnp.where(eye, 1.0, 0.0).astype(F32)
    g_wide = _dot3(tri, jnp.broadcast_to(g_col, (nb, c, HEAD_DIM)), False)
    g_i = g_wide[:, :, :c]
    g_j = jnp.sum(jnp.where(eye, g_i, 0.0), axis=1, keepdims=True)
    decay = jnp.where(incl, jnp.exp(jnp.where(incl, g_i - g_j, 0.0)), 0.0)
    kk = _bdot_b(k, k, True)
    a_mat = jnp.where(strict, b_col * kk * decay, 0.0)
    if inv_known is None:
        x_pow = -a_mat
        inv = ident + x_pow
        for _ in range(5):
            x_pow = _dot3(x_pow, x_pow, False, False)
            inv = inv + _dot3(inv, x_pow, False, False)
    else:
        inv = _inverse_given(a_mat, inv_known)
    e_wide = jnp.exp(g_wide)
    u = _dot3(inv, v * b_col, False)
    wk = _dot3(inv, k * b_col * e_wide, False)
    qk = _bdot_b(q, k, True) * decay
    last = lax.broadcasted_iota(jnp.int32, (1, c, HEAD_DIM), 1) == c - 1
    g_last = jnp.sum(jnp.where(last, g_wide, 0.0), axis=1, keepdims=True)
    qd = q * e_wide
    kd = k * jnp.exp(g_last - g_wide)
    gl = jnp.broadcast_to(jnp.exp(g_last), (nb, 8, HEAD_DIM))
    return u, wk, qd, kd, qk, gl, inv


def _gdn_intra_fn(q, k, v, gb, *inv_known):
    t = q.shape[0]
    nch = t // CHUNK
    lane = lax.broadcasted_iota(jnp.int32, gb.shape, 1)

    def heads_first(x_):
        return jnp.concatenate([x_[:, HEAD_DIM * h:HEAD_DIM * (h + 1)].reshape(nch, CHUNK, HEAD_DIM) for h in range(HEADS)],
                               axis=0)

    def column(first_lane):
        return jnp.concatenate([jnp.sum(jnp.where(lane == first_lane + h, gb, 0.0), axis=1, keepdims=True)
                                .reshape(nch, CHUNK, 1) for h in range(HEADS)], axis=0)

    known = jnp.concatenate([x_.reshape(nch, CHUNK, CHUNK) for x_ in inv_known], axis=0) if inv_known else None
    u, wk, qd, kd, qk, gl, inv = _intra_batched(heads_first(q), heads_first(k), heads_first(v), column(64), column(68), known)

    def rows_first(x_):
        r, w_ = x_.shape[1], x_.shape[2]
        return jnp.concatenate([x_[nch * h:nch * (h + 1)].reshape(nch * r, w_) for h in range(HEADS)], axis=1)

    per_head = lambda x_: [x_[nch * h:nch * (h + 1)].reshape(t, CHUNK) for h in range(HEADS)]
    outs = (rows_first(u), rows_first(wk), rows_first(qd), rows_first(kd), *per_head(qk), rows_first(gl))
    return outs if inv_known else outs + tuple(per_head(inv))


def _scan_step(s0, u, wk, qd, kd, qk, gl):
    v_new = u - _bdot_b(wk, s0, False)
    o = _bdot_b(qd, s0, False) + _bdot_b(qk, v_new, False)
    s1 = s0 * gl[:, 0:1, :] + _bdot_b(jnp.swapaxes(kd, 1, 2), v_new, False)
    return o, s1


def _mix_post_fn(o_a, z, o_b, gnw, onw):
    parts = [_rms(o_a[:, HEAD_DIM * h:HEAD_DIM * (h + 1)], gnw) * _silu(z[:, HEAD_DIM * h:HEAD_DIM * (h + 1)])
             for h in range(HEADS)]
    parts += [_rms(o_b[:, HEAD_DIM * h:HEAD_DIM * (h + 1)], onw) for h in range(HEADS)]
    return (jnp.concatenate(parts, axis=1),)


def _mla_pre_fn(ckv, cq, kab, cos_p, sin_p, qnw, kvnw, wuq, wukv, qn_w, qr_w, kn_w, kr_w):
    scale = (HEAD_DIM + ROPE) ** -0.5 * LOG2_E
    qf = _bdot(_rms(cq, qnw), wuq, "nn")
    kvf = _bdot(_rms(ckv, kvnw), wukv, "nn")
    lane = lax.broadcasted_iota(jnp.int32, kab.shape, 1)
    kr = _rope128(_rms(jnp.where(lane < ROPE, kab, 0.0), kr_w, n=ROPE), cos_p, sin_p)
    qs, ks = [], []
    for h in range(HEADS):
        qn = _rms(qf[:, 256 * h:256 * h + 128], qn_w) * scale
        qr = _rope128(_rms(qf[:, 256 * h + 128:256 * h + 256], qr_w, n=ROPE), cos_p, sin_p) * scale
        qs += [qn, qr]
        ks += [_rms(kvf[:, 128 * h:128 * (h + 1)], kn_w), kr]
    return jnp.concatenate(qs, axis=1), jnp.concatenate(ks, axis=1), kvf[:, 512:]


def _conv_fwd(proj, conv_w, tm, name):
    S = proj.shape[0]
    C = 1536
    nb = tm // 8

    def body(x_ref, prev_ref, w_ref, o_ref, ext_ref):
        i = pl.program_id(0)
        ext_ref[0:8, :] = jnp.where(i > 0, prev_ref[...], 0.0)
        ext_ref[8:, :] = x_ref[...]
        acc = jnp.zeros((tm, C), F32)
        for k in range(4):
            acc = acc + w_ref[k:k + 1, :] * ext_ref[pl.ds(5 + k, tm), :]
        o_ref[...] = acc

    return pl.pallas_call(
        body, name=name, grid=(S // tm,),
        in_specs=[pl.BlockSpec((tm, C), lambda i: (i, 0)),
                  pl.BlockSpec((8, C), lambda i: (jnp.maximum(i * nb - 1, 0), 0)),
                  pl.BlockSpec((4, C), lambda i: (0, 0))],
        out_specs=pl.BlockSpec((tm, C), lambda i: (i, 0)),
        out_shape=jax.ShapeDtypeStruct((S, C), F32),
        scratch_shapes=[pltpu.VMEM((tm + 8, C), F32)],
        compiler_params=_params(("arbitrary",)),
    )(proj, proj, conv_w)


def _conv_bwd(proj, dout, conv_w, tm, name):
    S = proj.shape[0]
    C = 1536
    nb = tm // 8
    n_steps = S // tm

    def body(x_ref, prev_ref, d_ref, next_ref, w_ref, dx_ref, dw_ref, xext_ref, dext_ref):
        i = pl.program_id(0)
        xext_ref[0:8, :] = jnp.where(i > 0, prev_ref[...], 0.0)
        xext_ref[8:, :] = x_ref[...]
        dext_ref[0:tm, :] = d_ref[...]
        dext_ref[tm:, :] = jnp.where(i < n_steps - 1, next_ref[...], 0.0)
        d = d_ref[...]
        acc = jnp.zeros((tm, C), F32)
        dws = []
        for k in range(4):
            acc = acc + w_ref[k:k + 1, :] * dext_ref[pl.ds(3 - k, tm), :]
            dws.append(jnp.sum(d * xext_ref[pl.ds(5 + k, tm), :], axis=0, keepdims=True))
        dx_ref[...] = acc

        @pl.when(i == 0)
        def _():
            dw_ref[...] = jnp.zeros_like(dw_ref)

        dw_ref[...] += jnp.concatenate(dws + [jnp.zeros((4, C), F32)], axis=0)

    return pl.pallas_call(
        body, name=name, grid=(n_steps,),
        in_specs=[pl.BlockSpec((tm, C), lambda i: (i, 0)),
                  pl.BlockSpec((8, C), lambda i: (jnp.maximum(i * nb - 1, 0), 0)),
                  pl.BlockSpec((tm, C), lambda i: (i, 0)),
                  pl.BlockSpec((8, C), lambda i: (jnp.minimum((i + 1) * nb, S // 8 - 1), 0)),
                  pl.BlockSpec((4, C), lambda i: (0, 0))],
        out_specs=[pl.BlockSpec((tm, C), lambda i: (i, 0)), pl.BlockSpec((8, C), lambda i: (0, 0))],
        out_shape=[jax.ShapeDtypeStruct((S, C), F32), jax.ShapeDtypeStruct((8, C), F32)],
        scratch_shapes=[pltpu.VMEM((tm + 8, C), F32), pltpu.VMEM((tm + 8, C), F32)],
        compiler_params=_params(("arbitrary",)),
    )(proj, proj, dout, dout, conv_w)


SCAN_CHUNKS = (4, 2, 1)


def _gdn_scan_fwd(u, wk, qd, kd, qks, gl, name):
    S = u.shape[0]
    nc = S // CHUNK
    cs = _pick(nc, SCAN_CHUNKS)
    W = HEADS * HEAD_DIM

    def body(u_ref, wk_ref, qd_ref, kd_ref, qk0, qk1, qk2, qk3, gl_ref, o_ref, sp_ref, s_ref):
        @pl.when(pl.program_id(0) == 0)
        def _():
            s_ref[...] = jnp.zeros_like(s_ref)

        state = s_ref[...]
        for c in range(cs):
            rows, gl_rows = slice(CHUNK * c, CHUNK * (c + 1)), slice(8 * c, 8 * (c + 1))
            sp_ref[c] = state
            o, state = _scan_step(state, _heads(u_ref, HEAD_DIM, rows), _heads(wk_ref, HEAD_DIM, rows),
                                  _heads(qd_ref, HEAD_DIM, rows), _heads(kd_ref, HEAD_DIM, rows),
                                  jnp.stack([r[rows, :] for r in (qk0, qk1, qk2, qk3)]), _heads(gl_ref, HEAD_DIM, gl_rows))
            for h in range(HEADS):
                o_ref[rows, HEAD_DIM * h:HEAD_DIM * (h + 1)] = o[h]
        s_ref[...] = state

    row = pl.BlockSpec((cs * CHUNK, W), lambda n: (n, 0))
    qk_spec = pl.BlockSpec((cs * CHUNK, CHUNK), lambda n: (n, 0))
    return pl.pallas_call(
        body, name=name, grid=(nc // cs,),
        in_specs=[row, row, row, row, qk_spec, qk_spec, qk_spec, qk_spec, pl.BlockSpec((cs * 8, W), lambda n: (n, 0))],
        out_specs=[row, pl.BlockSpec((cs, HEADS, HEAD_DIM, HEAD_DIM), lambda n: (n, 0, 0, 0))],
        out_shape=[jax.ShapeDtypeStruct((S, W), F32), jax.ShapeDtypeStruct((nc, HEADS, HEAD_DIM, HEAD_DIM), F32)],
        scratch_shapes=[pltpu.VMEM((HEADS, HEAD_DIM, HEAD_DIM), F32)],
        compiler_params=_params(("arbitrary",)),
    )(u, wk, qd, kd, *qks, gl)


def _gdn_scan_bwd(u, wk, qd, kd, qks, gl, s_prev, d_o, name):
    S = u.shape[0]
    nc = S // CHUNK
    cs = _pick(nc, SCAN_CHUNKS)
    nb = nc // cs
    W = HEADS * HEAD_DIM

    def body(u_ref, wk_ref, qd_ref, kd_ref, qk0, qk1, qk2, qk3, gl_ref, sp_ref, do_ref,
             du_ref, dwk_ref, dqd_ref, dkd_ref, dqk0, dqk1, dqk2, dqk3, dgl_ref, ds_ref):
        @pl.when(pl.program_id(0) == 0)
        def _():
            ds_ref[...] = jnp.zeros_like(ds_ref)

        d_state = ds_ref[...]
        for c in reversed(range(cs)):
            rows, gl_rows = slice(CHUNK * c, CHUNK * (c + 1)), slice(8 * c, 8 * (c + 1))
            _, vjp = jax.vjp(_scan_step, sp_ref[c], _heads(u_ref, HEAD_DIM, rows), _heads(wk_ref, HEAD_DIM, rows),
                             _heads(qd_ref, HEAD_DIM, rows), _heads(kd_ref, HEAD_DIM, rows),
                             jnp.stack([r[rows, :] for r in (qk0, qk1, qk2, qk3)]), _heads(gl_ref, HEAD_DIM, gl_rows))
            d_state, du, dwk, dqd, dkd, dqk, dgl = vjp((_heads(do_ref, HEAD_DIM, rows), d_state))
            for h, dqk_ref in enumerate((dqk0, dqk1, dqk2, dqk3)):
                sl = slice(HEAD_DIM * h, HEAD_DIM * (h + 1))
                du_ref[rows, sl] = du[h]
                dwk_ref[rows, sl] = dwk[h]
                dqd_ref[rows, sl] = dqd[h]
                dkd_ref[rows, sl] = dkd[h]
                dqk_ref[rows, :] = dqk[h]
                dgl_ref[gl_rows, sl] = dgl[h]
        ds_ref[...] = d_state

    rev = lambda n: (nb - 1 - n, 0)
    row = pl.BlockSpec((cs * CHUNK, W), rev)
    qk_spec = pl.BlockSpec((cs * CHUNK, CHUNK), rev)
    gl_spec = pl.BlockSpec((cs * 8, W), rev)
    qk_shape = jax.ShapeDtypeStruct((S, CHUNK), F32)
    row_shape = jax.ShapeDtypeStruct((S, W), F32)
    return pl.pallas_call(
        body, name=name, grid=(nb,),
        in_specs=[row, row, row, row, qk_spec, qk_spec, qk_spec, qk_spec, gl_spec,
                  pl.BlockSpec((cs, HEADS, HEAD_DIM, HEAD_DIM), lambda n: (nb - 1 - n, 0, 0, 0)), row],
        out_specs=[row, row, row, row, qk_spec, qk_spec, qk_spec, qk_spec, gl_spec],
        out_shape=[row_shape] * 4 + [qk_shape] * 4 + [jax.ShapeDtypeStruct((nc * 8, W), F32)],
        scratch_shapes=[pltpu.VMEM((HEADS, HEAD_DIM, HEAD_DIM), F32)],
        compiler_params=_params(("arbitrary",)),
    )(u, wk, qd, kd, *qks, gl, s_prev, d_o)


NEG = -1e30


def _chunk_mask(i, j, t, transposed=False):
    q_axis, k_axis = (1, 0) if transposed else (0, 1)
    r = (i * t + lax.broadcasted_iota(jnp.int32, (t, t), q_axis)) // CHUNK
    c = (j * t + lax.broadcasted_iota(jnp.int32, (t, t), k_axis)) // CHUNK
    return c <= r


def _tile_pairs(n, by_key):
    pairs = [(i, j) for j in range(n) for i in range(j, n)] if by_key else [(i, j) for i in range(n) for j in range(i + 1)]
    return jnp.asarray(np.array([p[0] for p in pairs], np.int32)), jnp.asarray(np.array([p[1] for p in pairs], np.int32))


def _heads(ref, width, rows=slice(None)):
    return jnp.stack([ref[rows, width * h:width * (h + 1)] for h in range(HEADS)])


def _bmm(a, b, dims):
    return lax.dot_general(a.astype(BF16), b.astype(BF16), dims, preferred_element_type=F32)


def _attn_fwd(q, k, v_t, t, name):
    S = q.shape[0]
    n = S // t
    qi, kj = _tile_pairs(n, by_key=False)

    def body(qi_ref, kj_ref, q_ref, k_ref, vt_ref, o_ref, lse_ref, m_ref, l_ref, acc_ref):
        i, j = qi_ref[pl.program_id(0)], kj_ref[pl.program_id(0)]

        @pl.when(j == 0)
        def _():
            m_ref[...] = jnp.full_like(m_ref, NEG)
            l_ref[...] = jnp.zeros_like(l_ref)
            acc_ref[...] = jnp.zeros_like(acc_ref)

        def update(masked):
            s_t = _bmm(_heads(k_ref, 256), _heads(q_ref, 256), _BNT)
            if masked:
                s_t = jnp.where(_chunk_mask(i, j, t, transposed=True)[None], s_t, NEG)
            m_old = m_ref[...]
            m_new = jnp.maximum(m_old, jnp.max(s_t, axis=1, keepdims=True))
            p_t = jnp.exp2(s_t - m_new)
            alpha = jnp.exp2(m_old - m_new)
            l_ref[...] = alpha * l_ref[...] + jnp.sum(p_t, axis=1, keepdims=True)
            v_heads = jnp.stack([vt_ref[HEAD_DIM * h:HEAD_DIM * (h + 1), :] for h in range(HEADS)])
            acc_ref[...] = alpha * acc_ref[...] + _bmm(v_heads, p_t, _BNN)
            m_ref[...] = m_new

        @pl.when(j < i)
        def _():
            update(False)

        @pl.when(j == i)
        def _():
            update(True)
            for h in range(HEADS):
                sl = slice(HEAD_DIM * h, HEAD_DIM * (h + 1))
                o_ref[:, sl] = jnp.transpose(acc_ref[h] / l_ref[h])
                lse_ref[:, sl] = jnp.transpose(jnp.broadcast_to(m_ref[h] + jnp.log(l_ref[h]) * LOG2_E, (HEAD_DIM, t)))

    row = lambda p, qi_, kj_: (qi_[p], 0)
    return pl.pallas_call(
        body, name=name,
        grid_spec=pltpu.PrefetchScalarGridSpec(
            num_scalar_prefetch=2, grid=(qi.shape[0],),
            in_specs=[pl.BlockSpec((t, HEADS * 256), row), pl.BlockSpec((t, HEADS * 256), lambda p, qi_, kj_: (kj_[p], 0)),
                      pl.BlockSpec((HEADS * HEAD_DIM, t), lambda p, qi_, kj_: (0, kj_[p]))],
            out_specs=[pl.BlockSpec((t, HEADS * HEAD_DIM), row)] * 2,
            scratch_shapes=[pltpu.VMEM((HEADS, 1, t), F32), pltpu.VMEM((HEADS, 1, t), F32),
                            pltpu.VMEM((HEADS, HEAD_DIM, t), F32)]),
        out_shape=[jax.ShapeDtypeStruct((S, HEADS * HEAD_DIM), F32)] * 2,
        compiler_params=_params(("arbitrary",)),
    )(qi, kj, q, k, v_t)


def _attn_stats(o, lse, d_o, t, name):
    S = o.shape[0]

    def body(o_ref, lse_ref, do_ref, st_ref):
        lane = lax.broadcasted_iota(jnp.int32, (t, HEAD_DIM), 1)
        stats = jnp.zeros((t, HEAD_DIM), F32)
        for h in range(HEADS):
            sl = slice(HEAD_DIM * h, HEAD_DIM * (h + 1))
            delta = jnp.sum(do_ref[:, sl] * o_ref[:, sl], axis=1, keepdims=True)
            stats = stats + jnp.where(lane == h, lse_ref[:, sl], 0.0) + jnp.where(lane == HEADS + h, delta, 0.0)
        st_ref[...] = jnp.transpose(stats)[0:8, :]

    row = pl.BlockSpec((t, HEADS * HEAD_DIM), lambda i: (i, 0))
    return pl.pallas_call(
        body, name=name, grid=(S // t,),
        in_specs=[row, row, row], out_specs=pl.BlockSpec((8, t), lambda i: (0, i)),
        out_shape=jax.ShapeDtypeStruct((8, S), F32),
        compiler_params=_params(("parallel",)),
    )(o, lse, d_o)


BWD_GROUP = 2


def _attn_bwd(q, k, v, d_o, stats, t, name):
    S = q.shape[0]
    n = S // t
    groups = HEADS // BWD_GROUP
    gq, gv = BWD_GROUP * 256, BWD_GROUP * HEAD_DIM
    qi, kj = _tile_pairs(n, by_key=True)
    n_pairs = qi.shape[0]
    st = stats.reshape(2, groups, BWD_GROUP, S).transpose(1, 0, 2, 3).reshape(groups, 2 * BWD_GROUP, S)
    st = jnp.pad(st, ((0, 0), (0, 8 - 2 * BWD_GROUP), (0, 0)))

    def heads(ref, width, rows=slice(None)):
        return jnp.stack([ref[rows, width * h:width * (h + 1)] for h in range(BWD_GROUP)])

    def body(qi_ref, kj_ref, q_ref, k_ref, v_ref, do_ref, st_ref, dq_hbm, dk_ref, dv_ref, dq_acc, sem):
        g, p = pl.program_id(0), pl.program_id(1)
        i, j = qi_ref[p], kj_ref[p]

        @pl.when(i == j)
        def _():
            dk_ref[...] = jnp.zeros_like(dk_ref)
            dv_ref[...] = jnp.zeros_like(dv_ref)

        def update(masked):
            qh, kh = heads(q_ref, 256), heads(k_ref, 256)
            d_out = heads(do_ref, HEAD_DIM)
            stv = st_ref[...]
            lse_row = jnp.stack([stv[h:h + 1, :] for h in range(BWD_GROUP)])
            delta_row = jnp.stack([stv[BWD_GROUP + h:BWD_GROUP + h + 1, :] for h in range(BWD_GROUP)])
            s_t = _bmm(kh, qh, _BNT)
            p_t = jnp.exp2(s_t - lse_row)
            if masked:
                p_t = jnp.where(_chunk_mask(i, j, t, transposed=True)[None], p_t, 0.0)
            dv = _bmm(p_t, d_out, _BNN)
            dp_t = _bmm(heads(v_ref, HEAD_DIM), d_out, _BNT)
            ds_t = p_t * (dp_t - delta_row)
            dk = _bmm(ds_t, qh, _BNN)
            dq = _bmm(jnp.swapaxes(ds_t, 1, 2), kh, _BNN)
            rows = pl.ds(pl.multiple_of(i * t, t), t)
            for h in range(BWD_GROUP):
                dk_ref[:, 256 * h:256 * (h + 1)] += dk[h]
                dv_ref[:, HEAD_DIM * h:HEAD_DIM * (h + 1)] += dv[h]

            @pl.when(j == 0)
            def _():
                for h in range(BWD_GROUP):
                    dq_acc[rows, 256 * h:256 * (h + 1)] = dq[h]

            @pl.when(j > 0)
            def _():
                for h in range(BWD_GROUP):
                    dq_acc[rows, 256 * h:256 * (h + 1)] += dq[h]

        @pl.when(i == j)
        def _():
            update(True)

        @pl.when(i > j)
        def _():
            update(False)

        @pl.when(i == n - 1)
        def _():
            dk_ref[...] *= 1.0 / LOG2_E

        @pl.when(p == n_pairs - 1)
        def _():
            dq_acc[...] *= 1.0 / LOG2_E
            for gg in range(groups):
                @pl.when(g == gg)
                def _():
                    cp = pltpu.make_async_copy(dq_acc, dq_hbm.at[:, gq * gg:gq * (gg + 1)], sem)
                    cp.start()
                    cp.wait()

    q_blk = lambda g, p, qi_, kj_: (qi_[p], g)
    k_blk = lambda g, p, qi_, kj_: (kj_[p], g)
    return pl.pallas_call(
        body, name=name,
        grid_spec=pltpu.PrefetchScalarGridSpec(
            num_scalar_prefetch=2, grid=(groups, n_pairs),
            in_specs=[pl.BlockSpec((t, gq), q_blk), pl.BlockSpec((t, gq), k_blk), pl.BlockSpec((t, gv), k_blk),
                      pl.BlockSpec((t, gv), q_blk), pl.BlockSpec((None, 8, t), lambda g, p, qi_, kj_: (g, 0, qi_[p]))],
            out_specs=[pl.BlockSpec(memory_space=pl.ANY), pl.BlockSpec((t, gq), k_blk), pl.BlockSpec((t, gv), k_blk)],
            scratch_shapes=[pltpu.VMEM((S, gq), F32), pltpu.SemaphoreType.DMA]),
        out_shape=[jax.ShapeDtypeStruct((S, HEADS * 256), F32), jax.ShapeDtypeStruct((S, HEADS * 256), F32),
                   jax.ShapeDtypeStruct((S, HEADS * HEAD_DIM), F32)],
        compiler_params=_params(("arbitrary", "arbitrary")),
    )(qi, kj, q, k, v, d_o, st)


FFN_PIECE = 2 * D_FF // N_DEV
HID_PIECES = D_FF // FFN_PIECE


def _ffn_up(h, w8, name):
    S = h.shape[0]
    tm = _pick(S, MATMUL_ROWS)

    def body(h_ref, wg_ref, wu_ref, g_ref, u_ref, hid_ref, hid_t_ref):
        gate = _dot_raw(h_ref[...], wg_ref[...], "nn")
        up = _dot_raw(h_ref[...], wu_ref[...], "nn")
        sg = _sigmoid(gate)
        act = gate * sg
        hid = act * up
        g_ref[...] = (up * (sg * (1.0 + gate * (1.0 - sg)))).astype(BF16)
        u_ref[...] = act.astype(BF16)
        hid_ref[...] = hid.astype(BF16)
        hid_t_ref[...] = jnp.transpose(hid).astype(BF16)

    o_spec = pl.BlockSpec((None, tm, FFN_PIECE), lambda i, j: (j, i, 0))
    return pl.pallas_call(
        body, name=name, grid=(S // tm, HID_PIECES),
        in_specs=[pl.BlockSpec((tm, D_MODEL), lambda i, j: (i, 0)),
                  pl.BlockSpec((None, D_MODEL, FFN_PIECE), lambda i, j: (j, 0, 0)),
                  pl.BlockSpec((None, D_MODEL, FFN_PIECE), lambda i, j: (j + HID_PIECES, 0, 0))],
        out_specs=[o_spec] * 3 + [pl.BlockSpec((None, FFN_PIECE, tm), lambda i, j: (j, 0, i))],
        out_shape=[jax.ShapeDtypeStruct((HID_PIECES, S, FFN_PIECE), BF16)] * 3
        + [jax.ShapeDtypeStruct((HID_PIECES, FFN_PIECE, S), BF16)],
        compiler_params=_params(("parallel", "parallel")),
    )(h, w8, w8)


def _after_specs(after):
    return [] if after is None else [pl.BlockSpec(memory_space=pl.ANY)]


def _after_args(after):
    return [] if after is None else [after]


def _ffn_gw8(h, d_gate, d_up, name, after=None):
    S = h.shape[0]
    tm = 512
    tk = _pick(S, (512, 256, 128))
    nk = S // tk

    def body(h_ref, dg_ref, du_ref, *rest):
        o_ref, acc_ref = rest[-2:]
        k = pl.program_id(1)

        @pl.when(k == 0)
        def _():
            acc_ref[...] = jnp.zeros_like(acc_ref)

        h_t = jnp.transpose(h_ref[...])
        for p in range(HID_PIECES):
            acc_ref[p] += _dot_raw(h_t, dg_ref[p], "nn")
            acc_ref[HID_PIECES + p] += _dot_raw(h_t, du_ref[p], "nn")

        @pl.when(k == nk - 1)
        def _():
            o_ref[...] = acc_ref[...].astype(o_ref.dtype)

    d_spec = pl.BlockSpec((HID_PIECES, tk, FFN_PIECE), lambda i, k: (0, k, 0))
    return pl.pallas_call(
        body, name=name, grid=(D_MODEL // tm, nk),
        in_specs=[pl.BlockSpec((tk, tm), lambda i, k: (k, i)), d_spec, d_spec] + _after_specs(after),
        out_specs=pl.BlockSpec((2 * HID_PIECES, tm, FFN_PIECE), lambda i, k: (0, i, 0)),
        out_shape=jax.ShapeDtypeStruct((2 * HID_PIECES, D_MODEL, FFN_PIECE), BF16),
        scratch_shapes=[pltpu.VMEM((2 * HID_PIECES, tm, FFN_PIECE), F32)],
        compiler_params=_params(("parallel", "arbitrary")),
    )(h, d_gate, d_up, *_after_args(after))


def _gate_below(dx, y_ref, g_ref, coef, dy_ref, dg_ref):
    dy_ref[...] = (coef * g_ref[...] * dx).astype(dy_ref.dtype)
    dg_ref[...] += jnp.sum(coef * y_ref[...] * dx, axis=0, keepdims=True)


def _ffn_dh(d_gate, d_up, w8, x, d_out, scale, shift, name, after=None, below=None):
    S = d_gate.shape[1]
    tm = _pick(S, (512, 256, 128))
    n_below = 0 if below is None else 2

    def body(dg_ref, du_ref, wg_ref, wu_ref, x_ref, do_ref, sc_ref, sh_ref, *rest):
        below_in = rest[:n_below]
        outs = rest[len(rest) - 4 - n_below:]
        dx_ref, dsc_ref, dsh_ref = outs[:3]
        below_out, acc_ref = outs[3:3 + n_below], outs[-1]
        i, k = pl.program_id(0), pl.program_id(1)

        @pl.when(k == 0)
        def _():
            acc_ref[...] = jnp.zeros_like(acc_ref)

        acc_ref[...] += _dot_raw(dg_ref[...], wg_ref[...], "nt") + _dot_raw(du_ref[...], wu_ref[...], "nt")

        @pl.when((k == 0) & (i == 0))
        def _():
            for r in (dsc_ref, dsh_ref) + tuple(below_out[1:]):
                r[...] = jnp.zeros_like(r)

        @pl.when(k == HID_PIECES - 1)
        def _():
            _, vjp = jax.vjp(_modulate, x_ref[...], sc_ref[...], sh_ref[...])
            dx, dsc, dsh = vjp(acc_ref[...])
            dx = dx + do_ref[...]
            dx_ref[...] = dx
            dsc_ref[...] += dsc
            dsh_ref[...] += dsh
            if below is not None:
                _gate_below(dx, below_in[0], below_in[1], below[2], below_out[0], below_out[1])

    d_spec = pl.BlockSpec((None, tm, FFN_PIECE), lambda i, k: (k, i, 0))
    row = pl.BlockSpec((tm, D_MODEL), lambda i, k: (i, 0))
    par = pl.BlockSpec((1, D_MODEL), lambda i, k: (0, 0))
    row_shape, par_shape = jax.ShapeDtypeStruct((S, D_MODEL), F32), jax.ShapeDtypeStruct((1, D_MODEL), F32)
    return pl.pallas_call(
        body, name=name, grid=(S // tm, HID_PIECES),
        in_specs=[d_spec, d_spec,
                  pl.BlockSpec((None, D_MODEL, FFN_PIECE), lambda i, k: (k, 0, 0)),
                  pl.BlockSpec((None, D_MODEL, FFN_PIECE), lambda i, k: (k + HID_PIECES, 0, 0)),
                  row, row, par, par] + [row, par][:n_below] + _after_specs(after),
        out_specs=[row, par, par] + [row, par][:n_below],
        out_shape=[row_shape, par_shape, par_shape] + [jax.ShapeDtypeStruct((S, D_MODEL), BF16), par_shape][:n_below],
        scratch_shapes=[pltpu.VMEM((tm, D_MODEL), F32)],
        compiler_params=_params(("arbitrary", "arbitrary")),
    )(d_gate, d_up, w8, w8, x, d_out, scale, shift, *(below[:2] if below is not None else ()), *_after_args(after))


def _swiglu_bwd(d_hid, hid_by_gate, hid_by_up):
    return d_hid * hid_by_gate, d_hid * hid_by_up


def _adamw_math(w_, g_, m_, v_):
    m_ = ADAM_B1 * m_ + (1.0 - ADAM_B1) * g_
    v_ = ADAM_B2 * v_ + (1.0 - ADAM_B2) * (g_ * g_)
    m_hat = m_ / (1.0 - ADAM_B1 ** ADAM_STEP)
    v_hat = v_ / (1.0 - ADAM_B2 ** ADAM_STEP)
    return -ADAM_LR * (m_hat / (jnp.sqrt(v_hat) + ADAM_EPS) + ADAM_WD * w_), m_, v_


def _adamw(w, g, m, v, name):
    R, C = w.shape
    tr = _pick(R, (256, 176, 128, 64, 32, 16, 8))

    def body(w_ref, g_ref, m_ref, v_ref, d_ref, nm_ref, nv_ref):
        d_ref[...], nm_ref[...], nv_ref[...] = _adamw_math(w_ref[...], g_ref[...], m_ref[...], v_ref[...])

    spec = pl.BlockSpec((tr, C), lambda i: (i, 0))
    return pl.pallas_call(
        body, name=name, grid=(R // tr,),
        in_specs=[spec] * 4, out_specs=[spec] * 3,
        out_shape=[jax.ShapeDtypeStruct((R, C), F32)] * 3,
        compiler_params=_params(("parallel",)),
    )(w, g, m, v)


def _sum_adamw(parts, w, m, v, name):
    R, C = w.shape
    tr = _pick(R, (256, 176, 128, 64, 32, 16, 8))

    def body(p_ref, w_ref, m_ref, v_ref, g_ref, d_ref, nm_ref, nv_ref):
        g_ = p_ref[0].astype(F32)
        for d in range(1, N_DEV):
            g_ = g_ + p_ref[d].astype(F32)
        g_ref[...] = g_
        d_ref[...], nm_ref[...], nv_ref[...] = _adamw_math(w_ref[...], g_, m_ref[...], v_ref[...])

    spec = pl.BlockSpec((tr, C), lambda i: (i, 0))
    return pl.pallas_call(
        body, name=name, grid=(R // tr,),
        in_specs=[pl.BlockSpec((N_DEV, tr, C), lambda i: (0, i, 0)), spec, spec, spec], out_specs=[spec] * 4,
        out_shape=[jax.ShapeDtypeStruct((R, C), F32)] * 4,
        compiler_params=_params(("parallel",)),
    )(parts, w, m, v)


def _sum_devices(parts, name):
    _, R, C = parts.shape
    tr = _pick(R, (512, 256, 176, 128, 64, 32, 16, 8))

    def body(p_ref, o_ref):
        acc = p_ref[0].astype(F32)
        for d in range(1, N_DEV):
            acc = acc + p_ref[d].astype(F32)
        o_ref[...] = acc

    return pl.pallas_call(
        body, name=name, grid=(R // tr,),
        in_specs=[pl.BlockSpec((N_DEV, tr, C), lambda i: (0, i, 0))],
        out_specs=pl.BlockSpec((tr, C), lambda i: (i, 0)),
        out_shape=jax.ShapeDtypeStruct((R, C), F32),
        compiler_params=_params(("parallel",)),
    )(parts)


def _my_place():
    return lax.axis_index("x"), lax.axis_index("y"), lax.axis_index("c")


def _all_gather(blocks, name):
    n = len(blocks)

    def body(*refs):
        x_refs, out_refs = refs[:n], refs[n:2 * n]
        send_sems, recv_sems, local_sems = refs[2 * n:]
        x, y, c = _my_place()
        me, sibling = (x, y, c), (x, y, 1 - c)
        chips = [(1 - x, y), (x, 1 - y), (1 - x, 1 - y)]

        def copy(a, k, blk, to, own=False):
            slot = out_refs[a].at[4 * blk[0] + 2 * blk[1] + blk[2]]
            return pltpu.make_async_remote_copy(
                src_ref=x_refs[a] if own else slot, dst_ref=slot,
                send_sem=send_sems.at[7 * a + k], recv_sem=recv_sems.at[7 * a + k], device_id=to, device_id_type=MESH)

        mine = [pltpu.make_async_copy(x_refs[a], out_refs[a].at[4 * x + 2 * y + c], local_sems.at[a]) for a in range(n)]
        for cp in mine:
            cp.start()
        first = []
        for j, chip in enumerate(chips):
            first += [copy(a, 1 + j, me, (*chip, c), own=True) for a in range(n)]
        first += [copy(a, 0, me, sibling, own=True) for a in range(n)]
        for cp in first:
            cp.start()
        passed = []
        for j, chip in enumerate(chips):
            for a in range(n):
                copy(a, 1 + j, (*chip, c), me).wait_recv()
                passed.append(copy(a, 4 + j, (*chip, c), sibling))
                passed[-1].start()
        for a in range(n):
            copy(a, 0, sibling, me).wait_recv()
        for j, chip in enumerate(chips):
            for a in range(n):
                copy(a, 4 + j, (*chip, 1 - c), me).wait_recv()
        for cp in first + passed:
            cp.wait_send()
        for cp in mine:
            cp.wait()

    return pl.pallas_call(
        body, name=name,
        out_shape=[jax.ShapeDtypeStruct((N_DEV,) + b.shape, b.dtype) for b in blocks],
        in_specs=[pl.BlockSpec(memory_space=pl.ANY)] * n,
        out_specs=[pl.BlockSpec(memory_space=pl.ANY)] * n,
        scratch_shapes=[pltpu.SemaphoreType.DMA((7 * n,)), pltpu.SemaphoreType.DMA((7 * n,)), pltpu.SemaphoreType.DMA((n,))],
    )(*blocks)


def _all_to_all(pieces, name):
    n = len(pieces)

    def body(*refs):
        x_refs, out_refs = refs[:n], refs[n:2 * n]
        send_sems, recv_sems, local_sems = refs[2 * n:]
        x, y, c = _my_place()
        me = 4 * x + 2 * y + c
        mine = [pltpu.make_async_copy(x_refs[a].at[me], out_refs[a].at[me], local_sems.at[a]) for a in range(n)]
        for cp in mine:
            cp.start()
        copies = []
        for k in (2, 4, 6, 3, 5, 7, 1):
            px = 1 - x if k & 4 else x
            py = 1 - y if k & 2 else y
            pc = 1 - c if k & 1 else c
            peer = 4 * px + 2 * py + pc
            for a in range(n):
                copies.append(pltpu.make_async_remote_copy(
                    src_ref=x_refs[a].at[peer], dst_ref=out_refs[a].at[me],
                    send_sem=send_sems.at[7 * a + k - 1], recv_sem=recv_sems.at[7 * a + k - 1],
                    device_id=(px, py, pc), device_id_type=MESH))
        for cp in copies:
            cp.start()
        for cp in copies:
            cp.wait_recv()
        for cp in copies:
            cp.wait_send()
        for cp in mine:
            cp.wait()

    return pl.pallas_call(
        body, name=name,
        out_shape=[jax.ShapeDtypeStruct(p.shape, p.dtype) for p in pieces],
        in_specs=[pl.BlockSpec(memory_space=pl.ANY)] * n,
        out_specs=[pl.BlockSpec(memory_space=pl.ANY)] * n,
        scratch_shapes=[pltpu.SemaphoreType.DMA((7 * n,)), pltpu.SemaphoreType.DMA((7 * n,)), pltpu.SemaphoreType.DMA((n,))],
    )(*pieces)


def _peers():
    x, y, c = _my_place()
    out = []
    for k in (2, 4, 6, 3, 5, 7, 1):
        px = 1 - x if k & 4 else x
        py = 1 - y if k & 2 else y
        pc = 1 - c if k & 1 else c
        out.append((k, (px, py, pc), 4 * px + 2 * py + pc))
    return out


def _exchange_copies(x_refs, land_refs, send_sems, recv_sems, scatter):
    x, y, c = _my_place()
    me = 4 * x + 2 * y + c
    starts, arrivals = [], []
    for k, place, peer in _peers():
        for a, (x_ref, land_ref) in enumerate(zip(x_refs, land_refs)):
            sems = dict(send_sem=send_sems.at[7 * a + k - 1], recv_sem=recv_sems.at[7 * a + k - 1],
                        device_id=place, device_id_type=MESH)
            src = x_ref.at[peer] if scatter else x_ref
            starts.append(pltpu.make_async_remote_copy(src_ref=src, dst_ref=land_ref.at[me], **sems))
            arrivals.append(pltpu.make_async_remote_copy(src_ref=src, dst_ref=land_ref.at[peer], **sems))
    return starts, arrivals


def _exchange_start(arrays, scatter, name):
    n = len(arrays)
    hbm = pl.BlockSpec(memory_space=pltpu.HBM)
    sem = pl.BlockSpec(memory_space=pltpu.SEMAPHORE)
    lands = [lax.empty(a.shape if scatter else (N_DEV,) + a.shape, a.dtype) for a in arrays]

    def body(*refs):
        x_refs, land_refs = refs[:n], refs[n:2 * n]
        send_sems, recv_sems = refs[2 * n], refs[2 * n + 1]
        token = refs[-1]
        starts, _ = _exchange_copies(x_refs, land_refs, send_sems, recv_sems, scatter)
        for cp in starts:
            cp.start()
        token[...] = jnp.zeros_like(token)

    res = pl.pallas_call(
        body, name=name,
        out_shape=(pltpu.SemaphoreType.DMA((7 * n,)), pltpu.SemaphoreType.DMA((7 * n,)),
                   *[pltpu.HBM(a.shape, a.dtype) for a in arrays], *[pltpu.HBM(l.shape, l.dtype) for l in lands],
                   jax.ShapeDtypeStruct((8, 128), F32)),
        in_specs=[hbm] * (2 * n),
        out_specs=(sem, sem, *[hbm] * (2 * n), pl.BlockSpec(memory_space=pltpu.VMEM)),
        input_output_aliases={i: 2 + i for i in range(2 * n)},
        compiler_params=pltpu.CompilerParams(has_side_effects=pltpu.SideEffectType.DATAFLOW_SIDE_EFFECTING),
    )(*[pltpu.with_memory_space_constraint(a, pltpu.HBM) for a in arrays],
      *[pltpu.with_memory_space_constraint(l, pltpu.HBM) for l in lands])
    return res[0], res[1], list(res[2:2 + n]), list(res[2 + n:2 + 2 * n]), res[-1]


def _exchange_wait(handles, scatter, after, name):
    send_sems, recv_sems, arrays, lands, _ = handles
    n = len(arrays)
    hbm = pl.BlockSpec(memory_space=pltpu.HBM)
    sem = pl.BlockSpec(memory_space=pltpu.SEMAPHORE)

    def body(*refs):
        x_refs, land_refs = refs[:n], refs[n:2 * n]
        send_s, recv_s = refs[2 * n], refs[2 * n + 1]
        starts, arrivals = _exchange_copies(x_refs, land_refs, send_s, recv_s, scatter)
        for cp in arrivals:
            cp.wait_recv()
        for cp in starts:
            cp.wait_send()

    res = pl.pallas_call(
        body, name=name,
        out_shape=(*[pltpu.HBM(a.shape, a.dtype) for a in arrays], *[pltpu.HBM(l.shape, l.dtype) for l in lands]),
        in_specs=[hbm] * (2 * n) + [sem, sem, pl.BlockSpec(memory_space=pl.ANY)],
        out_specs=tuple([hbm] * (2 * n)),
        input_output_aliases={i: i for i in range(2 * n)},
        compiler_params=pltpu.CompilerParams(has_side_effects=pltpu.SideEffectType.DATAFLOW_SIDE_EFFECTING),
    )(*arrays, *lands, send_sems, recv_sems, after)
    me = 4 * lax.axis_index("x") + 2 * lax.axis_index("y") + lax.axis_index("c")
    out = []
    for src, got in zip(res[:n], res[n:]):
        zeros = (0,) * (got.ndim - 1)
        own = lax.dynamic_slice(src, (me,) + zeros, (1,) + src.shape[1:]) if scatter else src[None]
        out.append(lax.dynamic_update_slice(got, own, (me,) + zeros))
    return out


def _pad_lanes(v, at=0, width=128):
    return jnp.pad(v, ((0, 0), (at, width - at - v.shape[1])))


def _pack_weights(P):
    W = {}
    w = P["w_in"]
    W["wp"] = jnp.concatenate([w[:, :2048], w[:, 2440:2696], w[:, 2056:2440], w[:, 2696:2760], w[:, 2048:2056],
                               jnp.zeros((D_MODEL, N_IN_PACKED - N_IN), w.dtype)], axis=1).astype(BF16)
    W["conv_w"] = P["gdn_conv_w"].astype(F32)
    W["alog_p"] = _pad_lanes(P["gdn_a_log"], 64)
    W["dt_p"] = _pad_lanes(P["gdn_dt_bias"], 64)
    W["gnw"] = P["gdn_norm_w"]
    W["qnw"] = P["mla_q_norm_w"]
    W["kvnw"] = P["mla_kv_norm_w"]
    uq = P["mla_w_uq"].reshape(Q_LORA, HEADS, HEAD_DIM + ROPE)
    W["wuq"] = jnp.pad(uq, ((0, 0), (0, 0), (0, 256 - HEAD_DIM - ROPE))).reshape(Q_LORA, HEADS * 256).astype(BF16)
    ukv = P["mla_w_ukv"].reshape(KV_LORA, HEADS, 2, HEAD_DIM)
    W["wukv"] = ukv.transpose(0, 2, 1, 3).reshape(KV_LORA, 2 * HEADS * HEAD_DIM).astype(BF16)
    W["qn_w"] = P["qkn_q_nope"]
    W["qr_w"] = _pad_lanes(P["qkn_q_rope"])
    W["kn_w"] = P["qkn_k_nope"]
    W["kr_w"] = _pad_lanes(P["qkn_k_rope"])
    W["onw"] = P["mla_out_norm_w"]
    W["wout"] = P["w_out"].astype(BF16)
    return W


def _unpack_grads(G):
    g = G["wp"]
    uq = G["wuq"].reshape(Q_LORA, HEADS, 256)[:, :, :HEAD_DIM + ROPE].reshape(Q_LORA, HEADS * (HEAD_DIM + ROPE))
    ukv = G["wukv"].reshape(KV_LORA, 2, HEADS, HEAD_DIM).transpose(0, 2, 1, 3).reshape(KV_LORA, 2 * HEADS * HEAD_DIM)
    return {
        "w_in": jnp.concatenate([g[:, :2048], g[:, 2752:2760], g[:, 2304:2688], g[:, 2048:2304], g[:, 2688:2752]], axis=1),
        "gdn_conv_w": G["conv_w"], "gdn_a_log": G["alog_p"][:, 64:68], "gdn_dt_bias": G["dt_p"][:, 64:68],
        "gdn_norm_w": G["gnw"], "mla_q_norm_w": G["qnw"], "mla_w_uq": uq, "mla_kv_norm_w": G["kvnw"], "mla_w_ukv": ukv,
        "qkn_q_nope": G["qn_w"], "qkn_q_rope": G["qr_w"][:, :ROPE], "qkn_k_nope": G["kn_w"], "qkn_k_rope": G["kr_w"][:, :ROPE],
        "mla_out_norm_w": G["onw"], "w_out": G["wout"],
    }


def _rope_tables(positions):
    half = ROPE // 2
    inv_freq = ROPE_BASE ** (-jnp.arange(half, dtype=F32) / half)
    ang = positions.astype(F32)[:, None] * inv_freq
    cos, sin = jnp.cos(ang), jnp.sin(ang)
    zeros = jnp.zeros((positions.shape[0], 128 - ROPE), F32)
    return jnp.concatenate([cos, cos, zeros], axis=1), jnp.concatenate([-sin, sin, zeros], axis=1)


def _mod_fn(x, scale, shift):
    return (_modulate(x, scale, shift),)


def _ffn_down_loss(hid, wo4, x, gate_w, target, name):
    S = x.shape[0]
    tb = _pick(S, (512, 256, 128))
    n = S // tb

    def body(hid_ref, wo_ref, x_ref, g_ref, t_ref, dx_ref, df_ref, dg_ref, l_ref, acc_ref):
        i, k = pl.program_id(0), pl.program_id(1)

        @pl.when(k == 0)
        def _():
            acc_ref[...] = jnp.zeros_like(acc_ref)

        acc_ref[...] += _dot_raw(hid_ref[...], wo_ref[...], "nn")

        @pl.when((k == 0) & (i == 0))
        def _():
            dg_ref[...] = jnp.zeros_like(dg_ref)
            l_ref[...] = jnp.zeros_like(l_ref)

        @pl.when(k == HID_PIECES - 1)
        def _():
            f = acc_ref[...]
            diff = x_ref[...] + 0.5 * g_ref[...] * f - t_ref[...]
            dx = diff * (1.0 / D_MODEL)
            dx_ref[...] = dx
            df_ref[...] = (0.5 * g_ref[...] * dx).astype(df_ref.dtype)
            dg_ref[...] += jnp.sum(0.5 * f * dx, axis=0, keepdims=True)
            l_ref[...] += jnp.sum(diff * diff, axis=0, keepdims=True)

        @pl.when((k == HID_PIECES - 1) & (i == n - 1))
        def _():
            l_ref[...] = jnp.full(l_ref.shape, (0.5 / D_MODEL) * jnp.sum(l_ref[...]), F32)

    row = pl.BlockSpec((tb, D_MODEL), lambda i, k: (i, 0))
    par = pl.BlockSpec((1, D_MODEL), lambda i, k: (0, 0))
    return pl.pallas_call(
        body, name=name, grid=(n, HID_PIECES),
        in_specs=[pl.BlockSpec((None, tb, FFN_PIECE), lambda i, k: (k, i, 0)),
                  pl.BlockSpec((None, FFN_PIECE, D_MODEL), lambda i, k: (k, 0, 0)), row, par, row],
        out_specs=[row, row, par, par],
        out_shape=[jax.ShapeDtypeStruct((S, D_MODEL), F32), jax.ShapeDtypeStruct((S, D_MODEL), BF16),
                   jax.ShapeDtypeStruct((1, D_MODEL), F32), jax.ShapeDtypeStruct((1, D_MODEL), F32)],
        scratch_shapes=[pltpu.VMEM((tb, D_MODEL), F32)],
        compiler_params=_params(("arbitrary", "arbitrary")),
    )(hid, wo4, x, gate_w, target)


def _ffn_fwd(x, scale, shift, gate_w, w8, wo4, tag, target=None):
    S = x.shape[0]
    tm = _pick(S, (512, 256, 128))
    (h,) = _rowwise(_mod_fn, [(x, tm, D_MODEL, 0)], [scale, shift], [(tm, D_MODEL, BF16)], S // tm, tag + "_mod")
    by_gate, by_up, hid, hid_t = _ffn_up(h, w8, tag + "_up")
    if target is not None:
        dx_out, df, d_gate_w, loss_row = _ffn_down_loss(hid, wo4, x, gate_w, target, tag + "_down")
        return (dx_out, loss_row), (h, by_gate, by_up, hid_t, None, df, d_gate_w)
    tb = _pick(S, MATMUL_ROWS)
    mn = pl.BlockSpec((tb, D_MODEL), lambda i, j, k: (i, j))
    f, x_out = _mmg(hid, wo4, "nn", name=tag + "_down", grid=(S // tb, 1, HID_PIECES),
                    a_spec=pl.BlockSpec((None, tb, FFN_PIECE), lambda i, j, k: (k, i, 0)),
                    b_spec=pl.BlockSpec((None, FFN_PIECE, D_MODEL), lambda i, j, k: (k, 0, j)),
                    out_spec=mn, out_shapes=[jax.ShapeDtypeStruct((S, D_MODEL), F32)] * 2, acc_shape=(tb, D_MODEL),
                    extras=[x, gate_w], extra_specs=[mn, pl.BlockSpec((1, D_MODEL), lambda i, j, k: (0, j))],
                    epi=lambda acc, x_, g_: (acc, x_ + 0.5 * g_ * acc))
    return x_out, (h, by_gate, by_up, hid_t, f, None, None)


def _ffn_bwd(d_out, x, scale, shift, gate_w, w8, wo4, saved, tag, grad_ready, below=None):
    h, gate, up, hid_t, f, df, d_gate_w = saved
    S = x.shape[0]
    tm = _pick(S, (512, 256, 128))
    tk = _pick(S, (512, 256, 128))
    n = S // tm
    if df is None:
        (df,), (d_gate_w,) = _rowwise_bwd(lambda f_, g_: (0.5 * g_ * f_,), [(f, tm, D_MODEL, 0)], [], [gate_w],
                                          [(d_out, tm, D_MODEL, 0)], n, tag + "_dres", row_dtypes=(BF16,))
    tb = _pick(S, MATMUL_ROWS)
    piece = pl.BlockSpec((None, tb, FFN_PIECE), lambda i, j, k: (j, i, 0))
    d_gate, d_up = _mmg(df, wo4, "nt", name=tag + "_ddown", grid=(S // tb, HID_PIECES, 1),
                        a_spec=pl.BlockSpec((tb, D_MODEL), lambda i, j, k: (i, 0)),
                        b_spec=pl.BlockSpec((None, FFN_PIECE, D_MODEL), lambda i, j, k: (j, 0, 0)),
                        out_spec=piece, out_shapes=[jax.ShapeDtypeStruct((HID_PIECES, S, FFN_PIECE), BF16)] * 2,
                        acc_shape=(tb, FFN_PIECE), extras=[gate, up], extra_specs=[piece, piece], epi=_swiglu_bwd)
    tk = _pick(S, MATMUL_ROWS)
    g_wo4 = _mmg(hid_t, df, "nn", name=tag + "_gwo", grid=(HID_PIECES, 1, S // tk),
                 a_spec=pl.BlockSpec((None, FFN_PIECE, tk), lambda i, j, k: (i, 0, k)),
                 b_spec=pl.BlockSpec((tk, D_MODEL), lambda i, j, k: (k, j)),
                 out_spec=pl.BlockSpec((None, FFN_PIECE, D_MODEL), lambda i, j, k: (i, 0, j)),
                 out_shapes=[jax.ShapeDtypeStruct((HID_PIECES, FFN_PIECE, D_MODEL), BF16)], acc_shape=(FFN_PIECE, D_MODEL))
    g_w8 = _ffn_gw8(h, d_gate, d_up, tag + "_gw8", after=grad_ready("wo4", g_wo4))
    res = _ffn_dh(d_gate, d_up, w8, x, d_out, scale, shift, tag + "_dh", after=grad_ready("w8", g_w8), below=below)
    return (res[0], res[1], res[2], d_gate_w) + tuple(res[3:])


def _dproj_dmod(d_proj, wp, x, d_out, scale, shift, name, below=None):
    S, K = d_proj.shape
    tm = _pick(S, (512, 256, 128))
    tk = _pick(K, (1408, 512, 256, 128))
    nk = K // tk
    n_below = 0 if below is None else 2

    def body(dp_ref, w_ref, x_ref, do_ref, sc_ref, sh_ref, *rest):
        below_in = rest[:n_below]
        dx_ref, dsc_ref, dsh_ref = rest[n_below:n_below + 3]
        below_out, acc_ref = rest[n_below + 3:n_below + 3 + n_below], rest[-1]
        i, k = pl.program_id(0), pl.program_id(1)

        @pl.when(k == 0)
        def _():
            acc_ref[...] = jnp.zeros_like(acc_ref)

        acc_ref[...] += _dot_raw(dp_ref[...], w_ref[...], "nt")

        @pl.when((k == 0) & (i == 0))
        def _():
            for r in (dsc_ref, dsh_ref) + tuple(below_out[1:]):
                r[...] = jnp.zeros_like(r)

        @pl.when(k == nk - 1)
        def _():
            _, vjp = jax.vjp(_modulate, x_ref[...], sc_ref[...], sh_ref[...])
            dx, dsc, dsh = vjp(acc_ref[...])
            dx = dx + do_ref[...]
            dx_ref[...] = dx
            dsc_ref[...] += dsc
            dsh_ref[...] += dsh
            if below is not None:
                _gate_below(dx, below_in[0], below_in[1], below[2], below_out[0], below_out[1])

    row = pl.BlockSpec((tm, D_MODEL), lambda i, k: (i, 0))
    par = pl.BlockSpec((1, D_MODEL), lambda i, k: (0, 0))
    row_shape, par_shape = jax.ShapeDtypeStruct((S, D_MODEL), F32), jax.ShapeDtypeStruct((1, D_MODEL), F32)
    return pl.pallas_call(
        body, name=name, grid=(S // tm, nk),
        in_specs=[pl.BlockSpec((tm, tk), lambda i, k: (i, k)), pl.BlockSpec((D_MODEL, tk), lambda i, k: (0, k)),
                  row, row, par, par] + [row, par][:n_below],
        out_specs=[row, par, par] + [row, par][:n_below],
        out_shape=[row_shape, par_shape, par_shape] + [jax.ShapeDtypeStruct((S, D_MODEL), BF16), par_shape][:n_below],
        scratch_shapes=[pltpu.VMEM((tm, D_MODEL), F32)],
        compiler_params=_params(("arbitrary", "arbitrary")),
    )(d_proj, wp, x, d_out, scale, shift, *(below[:2] if below is not None else ()))


def _mixer_fwd(x1, scale, shift, gate_w, cos_p, sin_p, W):
    S = x1.shape[0]
    tm = _pick(S, (512, 256, 128))
    tv = _pick(S, (256, 128))
    ta = _pick(S, (512, 256, 128))
    nc = S // CHUNK
    (h2,) = _rowwise(_mod_fn, [(x1, tm, D_MODEL, 0)], [scale, shift], [(tm, D_MODEL, BF16)], S // tm, "mix_mod")
    proj = _mm(h2, W["wp"], "nn", name="mix_proj")
    qkvc = _conv_fwd(proj, W["conv_w"], tv, "gdn_conv")
    kab = (proj, tv, 128, 21)
    q_a, k_a, v_a, gb = _rowwise(_gdn_pre_fn, [(qkvc, tv, 1536, 0), kab], [W["alog_p"], W["dt_p"]],
                                 [(tv, 512, F32)] * 3 + [(tv, 128, F32)], S // tv, "gdn_pre")
    ti = _pick(S, INTRA_ROWS)
    intra = _rowwise(_gdn_intra_fn, [(q_a, ti, 512, 0), (k_a, ti, 512, 0), (v_a, ti, 512, 0), (gb, ti, 128, 0)],
                     [], [(ti, 512, F32)] * 4 + [(ti, CHUNK, F32)] * 4 + [(ti // 8, 512, F32)] + [(ti, CHUNK, F32)] * 4,
                     S // ti, "gdn_intra")
    u, wk, qd, kd, qks, gl, invs = intra[0], intra[1], intra[2], intra[3], tuple(intra[4:8]), intra[8], tuple(intra[9:])
    o_a, s_prev = _gdn_scan_fwd(u, wk, qd, kd, qks, gl, "gdn_scan")
    mla_params = [W["qnw"], W["kvnw"], W["wuq"], W["wukv"], W["qn_w"], W["qr_w"], W["kn_w"], W["kr_w"]]
    def mla_pre_with_vt(*a):
        q_, k_, v_ = _mla_pre_fn(*a)
        return q_, k_, v_, jnp.transpose(v_)

    q_b, k_b, v_b, vt_b = _rowwise(mla_pre_with_vt,
                                   [(proj, tv, 256, 8), (proj, tv, 384, 6), kab, (cos_p, tv, 128, 0), (sin_p, tv, 128, 0)],
                                   mla_params, [(tv, 1024, BF16), (tv, 1024, BF16), (tv, 512, BF16), (512, tv, BF16, "across")],
                                   S // tv, "mla_pre")
    o_b, lse = _attn_fwd(q_b, k_b, vt_b, ta, "mla_attn")
    (mixed,) = _rowwise(_mix_post_fn, [(o_a, tv, 512, 0), (proj, tv, 512, 3), (o_b, tv, 512, 0)], [W["gnw"], W["onw"]],
                        [(tv, D_MODEL, BF16)], S // tv, "mix_post")
    y, x2 = _mm(mixed, W["wout"], "nn", name="mix_out", out_dtypes=(F32, F32), extras=[x1], extra_params=[gate_w],
                epi=lambda acc, x_, g_: (acc, x_ + g_ * acc))
    saved = (h2, proj, qkvc, q_a, k_a, v_a, gb, u, wk, qd, kd, qks, gl, invs, s_prev, o_a, q_b, k_b, v_b, o_b, lse, mixed, y)
    return x2, saved


def _mixer_bwd(d_out, dy, x1, scale, shift, cos_p, sin_p, W, saved, below):
    (h2, proj, qkvc, q_a, k_a, v_a, gb, u, wk, qd, kd, qks, gl, invs, s_prev, o_a, q_b, k_b, v_b, o_b, lse, mixed, y) = saved
    S = x1.shape[0]
    tm = _pick(S, (512, 256, 128))
    tv = _pick(S, (256, 128))
    ta = _pick(S, (512, 256, 128))
    nc = S // CHUNK
    G = {}
    d_mixed = _mm(dy, W["wout"], "nt", name="mix_dout")
    G["wout"] = _mm(mixed, dy, "tn", name="mix_gwout")
    (do_a, dz, do_b), (G["gnw"], G["onw"]) = _rowwise_bwd(
        _mix_post_fn, [(o_a, tv, 512, 0), (proj, tv, 512, 3), (o_b, tv, 512, 0)], [], [W["gnw"], W["onw"]],
        [(d_mixed, tv, D_MODEL, 0)], S // tv, "mix_dpost")
    stats = _attn_stats(o_b, lse, do_b, ta, "mla_stats")
    dq_b, dk_b, dv_b = _attn_bwd(q_b, k_b, v_b, do_b, stats, ta, "mla_dattn")
    kab = (proj, tv, 128, 21)
    mla_params = [W["qnw"], W["kvnw"], W["wuq"], W["wukv"], W["qn_w"], W["qr_w"], W["kn_w"], W["kr_w"]]
    (d_ckv, d_cq, d_kab), mla_grads = _rowwise_bwd(
        _mla_pre_fn, [(proj, tv, 256, 8), (proj, tv, 384, 6), kab], [(cos_p, tv, 128, 0), (sin_p, tv, 128, 0)], mla_params,
        [(dq_b, tv, 1024, 0), (dk_b, tv, 1024, 0), (dv_b, tv, 512, 0)], S // tv, "mla_dpre")
    for key, g in zip(("qnw", "kvnw", "wuq", "wukv", "qn_w", "qr_w", "kn_w", "kr_w"), mla_grads):
        G[key] = g
    scan_grads = _gdn_scan_bwd(u, wk, qd, kd, qks, gl, s_prev, do_a, "gdn_dscan")
    ti = _pick(S, INTRA_ROWS)
    intra_douts = [(scan_grads[i], ti, 512, 0) for i in range(4)] + [(scan_grads[4 + i], ti, CHUNK, 0) for i in range(4)]
    intra_douts.append((scan_grads[8], ti // 8, 512, 0))
    (dq_a, dk_a, dv_a, d_gb), _ = _rowwise_bwd(
        _gdn_intra_fn, [(q_a, ti, 512, 0), (k_a, ti, 512, 0), (v_a, ti, 512, 0), (gb, ti, 128, 0)],
        [(x_, ti, CHUNK, 0) for x_ in invs], [], intra_douts, S // ti, "gdn_dintra")
    (d_qkvc, d_kab), (G["alog_p"], G["dt_p"]) = _rowwise_bwd(
        _gdn_pre_fn, [(qkvc, tv, 1536, 0), kab], [], [W["alog_p"], W["dt_p"]],
        [(dq_a, tv, 512, 0), (dk_a, tv, 512, 0), (dv_a, tv, 512, 0), (d_gb, tv, 128, 0)], S // tv, "gdn_dpre",
        adds=[(1, d_kab)])
    d_qkv, g_conv = _conv_bwd(proj, d_qkvc, W["conv_w"], tv, "gdn_dconv")
    G["conv_w"] = g_conv[:4]
    d_proj = jnp.concatenate([d_qkv, dz, d_ckv, d_cq, d_kab], axis=1).astype(BF16)
    G["wp"] = _mm(h2, d_proj, "tn", name="mix_gwp")
    dx1, G["s2"], G["sh2"], d_below, dg_below = _dproj_dmod(d_proj, W["wp"], x1, d_out, scale, shift, "mix_dproj", below=below)
    return dx1, d_below, dg_below, G


def _local_step(x, target, mod, cos_p, sin_p, W1, mixer_weights, ffn2_weights, ffn_grad_ready, mixer_grads_ready):
    sh1, s1, g1, sh2, s2, g2, sh3, s3, g3 = [mod[:, D_MODEL * i:D_MODEL * (i + 1)] for i in range(N_MOD)]
    x1, saved1 = _ffn_fwd(x, s1, sh1, g1, W1["f1_w8"], W1["f1_wo4"], "ffn1")
    W = mixer_weights(x1)
    x2, saved2 = _mixer_fwd(x1, s2, sh2, g2, cos_p, sin_p, W)
    W.update(ffn2_weights(x2))
    (dx3, loss_row), saved3 = _ffn_fwd(x2, s3, sh3, g3, W["f2_w8"], W["f2_wo4"], "ffn2", target=target)
    dx2, d_s3, d_sh3, d_g3, dy, d_g2 = _ffn_bwd(dx3, x2, s3, sh3, g3, W["f2_w8"], W["f2_wo4"], saved3, "ffn2",
                                                ffn_grad_ready("f2"), below=(saved2[-1], g2, 1.0))
    dx1, df1, d_g1, G = _mixer_bwd(dx2, dy, x1, s2, sh2, cos_p, sin_p, W, saved2, below=(saved1[4], g1, 0.5))
    d_sh2, d_s2 = G.pop("sh2"), G.pop("s2")
    saved1 = saved1[:5] + (df1, d_g1 + mixer_grads_ready(G))
    dx, d_s1, d_sh1, d_g1 = _ffn_bwd(dx1, x, s1, sh1, g1, W1["f1_w8"], W1["f1_wo4"], saved1, "ffn1", ffn_grad_ready("f1"))
    d_mod = jnp.concatenate([d_sh1, d_s1, d_g1, d_sh2, d_s2, d_g2, d_sh3, d_s3, d_g3], axis=1)
    return loss_row, dx, d_mod


WEIGHT_NAMES = ("w_ada", "b_ada", "ffn1_w_in", "ffn1_w_out", "w_in", "gdn_conv_w", "gdn_a_log", "gdn_dt_bias", "gdn_norm_w",
                "mla_q_norm_w", "mla_w_uq", "mla_kv_norm_w", "mla_w_ukv", "qkn_q_nope", "qkn_q_rope", "qkn_k_nope",
                "qkn_k_rope", "mla_out_norm_w", "w_out", "ffn2_w_in", "ffn2_w_out")
FFN_SHARDED = ("ffn1_w_in", "ffn1_w_out", "ffn2_w_in", "ffn2_w_out")
SHEETED = (("w_in", "col"), ("gdn_conv_w", "col"), ("mla_w_uq", "col"), ("mla_w_ukv", "col"), ("w_out", "row"))
MOD_ROWS = N_MOD * D_MODEL // 128
SMALL = {"gdn_a_log": (MOD_ROWS, 1, 64, 4), "gdn_dt_bias": (MOD_ROWS + 1, 1, 64, 4), "gdn_norm_w": (MOD_ROWS + 2, 1, 0, 128),
         "mla_q_norm_w": (MOD_ROWS + 3, 3, 0, 384), "mla_kv_norm_w": (MOD_ROWS + 6, 2, 0, 256),
         "qkn_q_nope": (MOD_ROWS + 8, 1, 0, 128), "qkn_q_rope": (MOD_ROWS + 9, 1, 0, 64), "qkn_k_nope": (MOD_ROWS + 10, 1, 0, 128),
         "qkn_k_rope": (MOD_ROWS + 11, 1, 0, 64), "mla_out_norm_w": (MOD_ROWS + 12, 1, 0, 128)}
LOSS_ROW = MOD_ROWS + 13
CONV_ROW, CONV_ROWS = 88, 4 * 1536 // 128
SHEET_ROWS = CONV_ROW + CONV_ROWS


def _to_sheet(flat, dtype, sublanes):
    n = flat.shape[-1]
    unit = sublanes * 128
    pad = (-n) % unit
    flat = jnp.pad(flat.astype(dtype), [(0, 0)] * (flat.ndim - 1) + [(0, pad)])
    return flat.reshape(flat.shape[:-1] + ((n + pad) // 128, 128))


def _small_sheet(b_like, small):
    sheet = jnp.zeros((SHEET_ROWS, 128), F32).at[:MOD_ROWS].set(b_like.reshape(MOD_ROWS, 128))
    for name, (row, rows, lane, n) in SMALL.items():
        v = small[name].reshape(1, n)
        if rows == 1:
            sheet = sheet.at[row, lane:lane + n].set(v[0])
        else:
            sheet = sheet.at[row:row + rows].set(v.reshape(rows, 128))
    return sheet


def _from_small_sheet(sheet):
    out = {"b_ada": sheet[:MOD_ROWS].reshape(1, N_MOD * D_MODEL)}
    for name, (row, rows, lane, n) in SMALL.items():
        out[name] = sheet[row, lane:lane + n].reshape(1, n) if rows == 1 else sheet[row:row + rows].reshape(1, n)
    return out


def kernel(x, c, positions, w_ada, b_ada, ffn1_w_in, ffn1_w_out, w_in, gdn_conv_w, gdn_a_log, gdn_dt_bias, gdn_norm_w, mla_q_norm_w, mla_w_uq, mla_kv_norm_w, mla_w_ukv, qkn_q_nope, qkn_q_rope, qkn_k_nope, qkn_k_rope, mla_out_norm_w, w_out, ffn2_w_in, ffn2_w_out, loss_target, m_w_ada, m_b_ada, m_ffn1_w_in, m_ffn1_w_out, m_w_in, m_gdn_conv_w, m_gdn_a_log, m_gdn_dt_bias, m_gdn_norm_w, m_mla_q_norm_w, m_mla_w_uq, m_mla_kv_norm_w, m_mla_w_ukv, m_qkn_q_nope, m_qkn_q_rope, m_qkn_k_nope, m_qkn_k_rope, m_mla_out_norm_w, m_w_out, m_ffn2_w_in, m_ffn2_w_out, v_w_ada, v_b_ada, v_ffn1_w_in, v_ffn1_w_out, v_w_in, v_gdn_conv_w, v_gdn_a_log, v_gdn_dt_bias, v_gdn_norm_w, v_mla_q_norm_w, v_mla_w_uq, v_mla_kv_norm_w, v_mla_w_ukv, v_qkn_q_nope, v_qkn_q_rope, v_qkn_k_nope, v_qkn_k_rope, v_mla_out_norm_w, v_w_out, v_ffn2_w_in, v_ffn2_w_out):
    args = locals()
    w = {n: args[n] for n in WEIGHT_NAMES}
    m = {n: args["m_" + n] for n in WEIGHT_NAMES}
    v = {n: args["v_" + n] for n in WEIGHT_NAMES}
    me = 4 * lax.axis_index("x") + 2 * lax.axis_index("y") + lax.axis_index("c")
    cols = N_MOD * D_MODEL // N_DEV
    shard = {n: w[n][0] for n in FFN_SHARDED + tuple(s[0] for s in SHEETED)}

    sc = c * _sigmoid(c)
    first = _to_sheet(jnp.concatenate([sc.reshape(-1), shard["gdn_conv_w"].reshape(-1)]), F32, 8)
    (first_all,) = _all_gather([first], "gather_c")
    sc_all = first_all[:, :D_MODEL // 128].reshape(N_DEV, D_MODEL)
    n_taps = shard["gdn_conv_w"].size
    conv_all = first_all.reshape(N_DEV, -1)[:, D_MODEL:D_MODEL + n_taps].reshape(N_DEV, 4, -1)
    b_mine = lax.dynamic_slice(b_ada, (0, me * cols), (1, cols))
    mod_cols = _mm(sc_all, w_ada[0], "nn", name="ada_mod", extra_params=[b_mine], epi=lambda acc, b_: (acc + b_,))
    (mod_all,) = _all_to_all([_to_sheet(mod_cols, F32, 8)], "scatter_mod")
    mod = mod_all.reshape(N_DEV, -1)[:, :cols].reshape(1, N_MOD * D_MODEL)

    f1_shards, mod = lax.optimization_barrier(([shard["ffn1_w_in"].astype(BF16), shard["ffn1_w_out"].astype(BF16)], mod))
    f1_w8, f1_out = _all_gather(f1_shards, "gather_w1")
    travel = [s for s in SHEETED if s[0] != "gdn_conv_w"]
    tied = lax.optimization_barrier(([shard[n].astype(BF16) for n, _ in travel], f1_w8))
    f1_w8 = tied[1]
    mixer_w = _exchange_start(tied[0], False, "gather_wm_start")
    ffn2_w = _exchange_start([shard["ffn2_w_in"].astype(BF16) + mixer_w[4][0:1, 0:1].astype(BF16),
                              shard["ffn2_w_out"].astype(BF16)], False, "gather_w2_start")
    mod = mod + ffn2_w[4][0:1, 0:1]
    W1 = dict(f1_w8=f1_w8, f1_wo4=f1_out.reshape(HID_PIECES, FFN_PIECE, D_MODEL))

    def mixer_weights(after):
        got = _exchange_wait(mixer_w, False, after, "gather_wm_wait")
        P = {n: jnp.concatenate(list(g), axis=1) if kind == "col" else g.reshape(-1, g.shape[-1])
             for (n, kind), g in zip(travel, got)}
        P["gdn_conv_w"] = jnp.concatenate(list(conv_all), axis=1)
        for n in SMALL:
            P[n] = w[n]
        return _pack_weights(P)

    def ffn2_weights(after):
        f2_w8, f2_out = _exchange_wait(ffn2_w, False, after, "gather_w2_wait")
        return dict(f2_w8=f2_w8, f2_wo4=f2_out.reshape(HID_PIECES, FFN_PIECE, D_MODEL))

    pending, small_grads = {}, {}

    def ffn_grad_ready(tag):
        def ready(which, g):
            pieces = g if which == "w8" else g.reshape((N_DEV,) + shard["ffn1_w_out"].shape)
            pending[tag + which] = _exchange_start([pieces], True, "scatter_%s_%s_start" % (tag, which))
            return pending[tag + which][4]
        return ready

    def mixer_grads_ready(G):
        g_full = _unpack_grads(G)
        small_grads.update({n: g_full[n] for n in SMALL})
        small_grads["gdn_conv_w"] = g_full["gdn_conv_w"]
        pieces = []
        for n, kind in travel:
            r, cc = shard[n].shape
            g = g_full[n].astype(BF16)
            pieces.append(jnp.stack([g[:, cc * p:cc * (p + 1)] for p in range(N_DEV)]) if kind == "col"
                          else g.reshape(N_DEV, r, cc))
        pending["mixer"] = _exchange_start(pieces, True, "scatter_mx_start")
        return pending["mixer"][4][0:1, 0:1]

    cos_p, sin_p = _rope_tables(positions[0])
    loss_row, dx, d_mod = _local_step(x[0], loss_target[0], mod, cos_p, sin_p, W1, mixer_weights, ffn2_weights,
                                      ffn_grad_ready, mixer_grads_ready)

    sheet = _small_sheet(d_mod, small_grads).at[LOSS_ROW].set(loss_row[0, :128])
    sheet = sheet.at[CONV_ROW:CONV_ROW + CONV_ROWS].set(small_grads["gdn_conv_w"].reshape(CONV_ROWS, 128))
    (sheets,) = _all_gather([sheet], "gather_small")
    summed = _sum_devices(sheets, "sum_small")
    d_mod_all = sheets[:, :MOD_ROWS].reshape(N_DEV, N_MOD * D_MODEL)
    d_mod_mine = lax.dynamic_slice(d_mod_all, (0, me * cols), (N_DEV, cols))
    grads = _from_small_sheet(summed)
    grads["w_ada"] = _mm(sc_all, d_mod_mine, "tn", name="ada_gw", hi=True)
    conv_taps = shard["gdn_conv_w"].shape[1]
    grads["gdn_conv_w"] = lax.dynamic_slice(summed[CONV_ROW:CONV_ROW + CONV_ROWS].reshape(4, -1), (0, me * conv_taps),
                                            (4, conv_taps))
    loss = summed[LOSS_ROW, 0]

    delta, new_m, new_v = {}, {}, {}
    arrived = {}
    for n, key in zip(FFN_SHARDED, ("f1w8", "f1wo4", "f2w8", "f2wo4")):
        (arrived[n],) = _exchange_wait(pending[key], True, summed, "scatter_%s_wait" % key)
    arrived.update(zip([n for n, _ in travel], _exchange_wait(pending["mixer"], True, summed, "scatter_mx_wait")))
    for n, parts in arrived.items():
        grads[n], delta[n], new_m[n], new_v[n] = _sum_adamw(parts, w[n][0], m[n][0], v[n][0], "adamw_" + n)
    for n in ("w_ada", "gdn_conv_w"):
        delta[n], new_m[n], new_v[n] = _adamw(w[n][0], grads[n], m[n][0], v[n][0], "adamw_" + n)
    small_in = [_small_sheet(t["b_ada"], t) for t in (w, grads, m, v)]
    for res, out in zip(_adamw(*small_in, "adamw_small"), (delta, new_m, new_v)):
        out.update(_from_small_sheet(res))

    def shaped(d):
        return [d[n].reshape(w[n].shape) for n in WEIGHT_NAMES]

    return (loss, dx[None], *shaped(grads), *shaped(delta), *shaped(new_m), *shaped(new_v))
```

```python
import functools

import jax
import jax.numpy as jnp
import numpy as np
from jax import lax
from jax.experimental import pallas as pl
from jax.experimental.pallas import tpu as pltpu

F32 = jnp.float32
BF16 = jnp.bfloat16

D_MODEL = 1024
D_FF = 2816
N_MOD = 9
HEADS = 4
HEAD_DIM = 128
CHUNK = 64
EPS = 1e-6
ROPE = 64
Q_LORA = 384
KV_LORA = 256
N_IN = 2760
N_IN_PACKED = 2816
ROPE_BASE = 10000.0
LOG2_E = 1.4426950408889634
N_DEV = 8

ADAM_LR = 0.001
ADAM_B1 = 0.9
ADAM_B2 = 0.999
ADAM_EPS = 1e-08
ADAM_WD = 0.01
ADAM_STEP = 10

VMEM_LIMIT_BYTES = 56 * 1024 * 1024
MATMUL_ROWS = (1024, 512, 256, 128)
MESH = pl.DeviceIdType.MESH


def _params(sem=None):
    return pltpu.CompilerParams(dimension_semantics=sem, vmem_limit_bytes=VMEM_LIMIT_BYTES)


def _pick(dim, prefs):
    for p in prefs:
        if dim % p == 0:
            return p
    return dim


_DIMS = {"nn": (((1,), (0,)), ((), ())), "nt": (((1,), (1,)), ((), ())), "tn": (((0,), (0,)), ((), ()))}


def _dot_raw(a, b, mode):
    return lax.dot_general(a.astype(BF16), b.astype(BF16), _DIMS[mode], preferred_element_type=F32)


def _dot_hi(a, b, mode="nn"):
    return lax.dot_general(a, b, _DIMS[mode], precision=lax.Precision.HIGHEST, preferred_element_type=F32)


@functools.partial(jax.custom_vjp, nondiff_argnums=(2,))
def _bdot(a, b, mode):
    return _dot_raw(a, b, mode)


def _bdot_fwd(a, b, mode):
    return _dot_raw(a, b, mode), (a, b)


def _bdot_bwd(mode, res, g):
    a, b = res
    if mode == "nn":
        return _dot_raw(g, b, "nt"), _dot_raw(a, g, "tn")
    if mode == "nt":
        return _dot_raw(g, b, "nn"), _dot_raw(g, a, "tn")
    return _dot_raw(b, g, "nt"), _dot_raw(a, g, "nn")


_bdot.defvjp(_bdot_fwd, _bdot_bwd)


def _mm(a, b, mode, *, name, out_dtypes=(F32,), epi=None, extras=(), extra_params=(), hi=False,
        tm=None, tn=None, tk=None):
    if mode == "nn":
        (M, K), (_, N) = a.shape, b.shape
    elif mode == "nt":
        (M, K), (N, _) = a.shape, b.shape
    else:
        (K, M), (_, N) = a.shape, b.shape
    tm = tm or _pick(M, (512, 1408, 256, 128) if mode == "tn" else MATMUL_ROWS + (384, 352))
    tn = tn or _pick(N, (1024, 1408, 768, 512, 384, 256, 128))
    tk = tk or _pick(K, (1024, 1408, 512, 384, 256, 128))
    a_spec = {"nn": pl.BlockSpec((tm, tk), lambda i, j, k: (i, k)), "nt": pl.BlockSpec((tm, tk), lambda i, j, k: (i, k)),
              "tn": pl.BlockSpec((tk, tm), lambda i, j, k: (k, i))}[mode]
    b_spec = {"nn": pl.BlockSpec((tk, tn), lambda i, j, k: (k, j)), "nt": pl.BlockSpec((tn, tk), lambda i, j, k: (j, k)),
              "tn": pl.BlockSpec((tk, tn), lambda i, j, k: (k, j))}[mode]
    mn_spec = pl.BlockSpec((tm, tn), lambda i, j, k: (i, j))
    return _mmg(a, b, mode, name=name, grid=(M // tm, N // tn, K // tk), a_spec=a_spec, b_spec=b_spec, out_spec=mn_spec,
                out_shapes=[jax.ShapeDtypeStruct((M, N), dt) for dt in out_dtypes], acc_shape=(tm, tn), epi=epi,
                extras=list(extras) + list(extra_params),
                extra_specs=[mn_spec] * len(extras) + [pl.BlockSpec((1, tn), lambda i, j, k: (0, j))] * len(extra_params),
                hi=hi)


def _mmg(a, b, mode, *, name, grid, a_spec, b_spec, out_spec, out_shapes, acc_shape, epi=None, extras=(),
         extra_specs=(), hi=False):
    nk = grid[2]
    n_e, n_o = len(extras), len(out_shapes)

    def body(*refs):
        a_ref, b_ref = refs[:2]
        e_refs = refs[2:2 + n_e]
        o_refs = refs[2 + n_e:2 + n_e + n_o]
        acc_ref = refs[-1]
        k = pl.program_id(2)

        @pl.when(k == 0)
        def _():
            acc_ref[...] = jnp.zeros_like(acc_ref)

        if hi:
            acc_ref[...] += _dot_hi(a_ref[...].astype(F32), b_ref[...].astype(F32), mode)
        else:
            acc_ref[...] += _dot_raw(a_ref[...], b_ref[...], mode)

        @pl.when(k == nk - 1)
        def _():
            acc = acc_ref[...]
            outs = (acc,) if epi is None else epi(acc, *[e[...].astype(F32) for e in e_refs])
            for o_ref, o in zip(o_refs, outs):
                o_ref[...] = o.astype(o_ref.dtype)

    outs = pl.pallas_call(
        body, name=name, grid=grid,
        in_specs=[a_spec, b_spec] + list(extra_specs),
        out_specs=[out_spec] * n_o,
        out_shape=list(out_shapes),
        scratch_shapes=[pltpu.VMEM(acc_shape, F32)],
        compiler_params=_params(("parallel", "parallel", "arbitrary")),
    )(a, b, *extras)
    return outs if n_o > 1 else outs[0]


def _row_spec(th, cw, ci):
    return pl.BlockSpec((th, cw), lambda i: (i, ci))


def _full_spec(shape):
    return pl.BlockSpec(shape, lambda i: (0,) * len(shape))


def _rowwise(fn, rows, params, outs, n_steps, name):
    n_r, n_p, n_o = len(rows), len(params), len(outs)

    def body(*refs):
        vals = [r[...].astype(F32) for r in refs[:n_r + n_p]]
        res = fn(*vals)
        for o_ref, o in zip(refs[n_r + n_p:], res):
            o_ref[...] = o.astype(o_ref.dtype)

    across = [len(o) == 4 for o in outs]
    res = pl.pallas_call(
        body, name=name, grid=(n_steps,),
        in_specs=[_row_spec(th, cw, ci) for (_, th, cw, ci) in rows] + [_full_spec(p.shape) for p in params],
        out_specs=[pl.BlockSpec((o[0], o[1]), lambda i: (0, i)) if ac else _row_spec(o[0], o[1], 0)
                   for o, ac in zip(outs, across)],
        out_shape=[jax.ShapeDtypeStruct((o[0], n_steps * o[1]) if ac else (n_steps * o[0], o[1]), o[2])
                   for o, ac in zip(outs, across)],
        compiler_params=_params(("parallel",)),
    )(*[r[0] for r in rows], *params)
    return res


def _rowwise_bwd(fn, rows, aux, params, douts, n_steps, name, row_dtypes=None, adds=()):
    n_r, n_a, n_p, n_d, n_add = len(rows), len(aux), len(params), len(douts), len(adds)
    row_dtypes = row_dtypes or (F32,) * n_r

    def body(*refs):
        it = iter(refs)
        r_vals = [next(it)[...].astype(F32) for _ in range(n_r)]
        a_vals = [next(it)[...].astype(F32) for _ in range(n_a)]
        p_vals = [next(it)[...].astype(F32) for _ in range(n_p)]
        d_vals = [next(it)[...].astype(F32) for _ in range(n_d)]
        add_vals = [next(it)[...].astype(F32) for _ in range(n_add)]
        dr_refs = [next(it) for _ in range(n_r)]
        dp_refs = [next(it) for _ in range(n_p)]

        def f(*rp):
            return tuple(fn(*rp[:n_r], *a_vals, *rp[n_r:]))

        _, vjp = jax.vjp(f, *r_vals, *p_vals)
        grads = list(vjp(tuple(d_vals)))
        for (ri, _), av in zip(adds, add_vals):
            grads[ri] = grads[ri] + av
        for dr_ref, g in zip(dr_refs, grads[:n_r]):
            dr_ref[...] = g.astype(dr_ref.dtype)

        @pl.when(pl.program_id(0) == 0)
        def _():
            for dp_ref in dp_refs:
                dp_ref[...] = jnp.zeros_like(dp_ref)

        for dp_ref, g in zip(dp_refs, grads[n_r:]):
            dp_ref[...] += g

    all_rows = list(rows) + list(aux) + list(douts) + [(arr,) + tuple(rows[ri][1:3]) + (0,) for ri, arr in adds]
    in_specs = ([_row_spec(th, cw, ci) for (_, th, cw, ci) in list(rows) + list(aux)]
                + [_full_spec(p.shape) for p in params]
                + [_row_spec(th, cw, ci) for (_, th, cw, ci) in all_rows[n_r + n_a:]])
    res = pl.pallas_call(
        body, name=name, grid=(n_steps,),
        in_specs=in_specs,
        out_specs=[_row_spec(th, cw, 0) for (_, th, cw, _) in rows] + [_full_spec(p.shape) for p in params],
        out_shape=[jax.ShapeDtypeStruct((n_steps * th, cw), dt) for (_, th, cw, _), dt in zip(rows, row_dtypes)]
        + [jax.ShapeDtypeStruct(p.shape, F32) for p in params],
        compiler_params=_params(("arbitrary",)),
    )(*[r[0] for r in list(rows) + list(aux)], *params, *[r[0] for r in all_rows[n_r + n_a:]])
    return res[:n_r], res[n_r:]


def _sigmoid(x):
    return lax.logistic(x)


def _silu(x):
    return x * _sigmoid(x)


def _rms(x, w=None, n=None):
    n = n or x.shape[-1]
    y = x * lax.rsqrt(jnp.sum(x * x, axis=-1, keepdims=True) * (1.0 / n) + EPS)
    return y if w is None else y * w


def _modulate(x, scale, shift):
    return _rms(x) * (1.0 + scale) + shift


def _softplus(x):
    return jnp.maximum(x, 0.0) + jnp.log1p(jnp.exp(-jnp.abs(x)))


@jax.custom_vjp
def _rot_half64(x):
    lane = lax.broadcasted_iota(jnp.int32, x.shape, 1)
    up = pltpu.roll(x, 96, 1)
    down = pltpu.roll(x, 32, 1)
    return jnp.where(lane < 32, up, jnp.where(lane < 64, down, 0.0))


_rot_half64.defvjp(lambda x: (_rot_half64(x), None), lambda _, g: (_rot_half64(g),))


def _rope128(x, cos_p, sin_p):
    return x * cos_p + _rot_half64(x) * sin_p


def _gdn_pre_fn(qkvc, kab, alog_p, dt_p):
    a = _silu(qkvc)
    qs, ks = [], []
    for h in range(HEADS):
        qh = a[:, HEAD_DIM * h:HEAD_DIM * (h + 1)]
        kh = a[:, 512 + HEAD_DIM * h:512 + HEAD_DIM * (h + 1)]
        qs.append(qh * lax.rsqrt(jnp.sum(qh * qh, axis=-1, keepdims=True) + EPS) * (HEAD_DIM ** -0.5))
        ks.append(kh * lax.rsqrt(jnp.sum(kh * kh, axis=-1, keepdims=True) + EPS))
    lane = lax.broadcasted_iota(jnp.int32, kab.shape, 1)
    g_full = -jnp.exp(alog_p) * _softplus(kab + dt_p)
    b_full = _sigmoid(kab)
    gb = jnp.where((lane >= 64) & (lane < 68), g_full, jnp.where((lane >= 68) & (lane < 72), b_full, 0.0))
    return jnp.concatenate(qs, axis=1), jnp.concatenate(ks, axis=1), a[:, 1024:1536], gb


INTRA_ROWS = (256, 128, 64)

_BNN = (((2,), (1,)), ((0,), (0,)))
_BNT = (((2,), (2,)), ((0,), (0,)))


def _split_bf16(a):
    hi = a.astype(BF16)
    return hi, (a - hi.astype(F32)).astype(BF16)


def _dot3_raw(a, b, dims):
    a_hi, a_lo = _split_bf16(a)
    b_hi, b_lo = _split_bf16(b)
    dot = lambda x_, y_: lax.dot_general(x_, y_, dims, preferred_element_type=F32)
    return dot(a_hi, b_hi) + (dot(a_hi, b_lo) + dot(a_lo, b_hi))


@functools.partial(jax.custom_vjp, nondiff_argnums=(2, 3))
def _dot3(a, b, nt, exact_bwd=True):
    return _dot3_raw(a, b, _BNT if nt else _BNN)


def _dot3_fwd(a, b, nt, exact_bwd):
    return _dot3_raw(a, b, _BNT if nt else _BNN), (a, b)


def _dot3_bwd(nt, exact_bwd, res, g):
    a, b = res
    if exact_bwd:
        dot = _dot3_raw
    else:
        dot = lambda x_, y_, d_: lax.dot_general(x_.astype(BF16), y_.astype(BF16), d_, preferred_element_type=F32)
    if nt:
        return dot(g, b, _BNN), dot(jnp.swapaxes(g, 1, 2), a, _BNN)
    return dot(g, b, _BNT), dot(jnp.swapaxes(a, 1, 2), g, _BNN)


_dot3.defvjp(_dot3_fwd, _dot3_bwd)


@functools.partial(jax.custom_vjp, nondiff_argnums=(2,))
def _bdot_b(a, b, nt):
    return lax.dot_general(a.astype(BF16), b.astype(BF16), _BNT if nt else _BNN, preferred_element_type=F32)


def _bdot_b_fwd(a, b, nt):
    return _bdot_b(a, b, nt), (a, b)


def _bdot_b_bwd(nt, res, g):
    a, b = res
    dot = lambda x_, y_, d_: lax.dot_general(x_.astype(BF16), y_.astype(BF16), d_, preferred_element_type=F32)
    if nt:
        return dot(g, b, _BNN), dot(jnp.swapaxes(g, 1, 2), a, _BNN)
    return dot(g, b, _BNT), dot(jnp.swapaxes(a, 1, 2), g, _BNN)


_bdot_b.defvjp(_bdot_b_fwd, _bdot_b_bwd)


@jax.custom_vjp
def _inverse_given(a_mat, inv):
    return inv


def _inverse_given_bwd(inv, g):
    inv_t = jnp.swapaxes(inv, 1, 2)
    return -_dot3_raw(_dot3_raw(inv_t, g, _BNN), inv_t, _BNN), jnp.zeros_like(inv)


_inverse_given.defvjp(lambda a_mat, inv: (inv, inv), _inverse_given_bwd)


def _intra_batched(q, k, v, g_col, b_col, inv_known=None):
    c = CHUNK
    nb = q.shape[0]
    row = lax.broadcasted_iota(jnp.int32, (1, c, c), 1)
    col = lax.broadcasted_iota(jnp.int32, (1, c, c), 2)
    incl, strict, eye = row >= col, row > col, row == col
    tri = jnp.broadcast_to(jnp.where(incl, 1.0, 0.0).astype(F32), (nb, c, c))
    ident = jnp.where(eye, 1.0, 0.0).astype(F32)
    g_wide = _dot3(tri, jnp.broadcast_to(g_col, (nb, c, HEAD_DIM)), False)
    g_i = g_wide[:, :, :c]
    g_j = jnp.sum(jnp.where(eye, g_i, 0.0), axis=1, keepdims=True)
    decay = jnp.where(incl, jnp.exp(jnp.where(incl, g_i - g_j, 0.0)), 0.0)
    kk = _bdot_b(k, k, True)
    a_mat = jnp.where(strict, b_col * kk * decay, 0.0)
    if inv_known is None:
        x_pow = -a_mat
        inv = ident + x_pow
        for _ in range(5):
            x_pow = _dot3(x_pow, x_pow, False, False)
            inv = inv + _dot3(inv, x_pow, False, False)
    else:
        inv = _inverse_given(a_mat, inv_known)
    e_wide = jnp.exp(g_wide)
    u = _dot3(inv, v * b_col, False)
    wk = _dot3(inv, k * b_col * e_wide, False)
    qk = _bdot_b(q, k, True) * decay
    last = lax.broadcasted_iota(jnp.int32, (1, c, HEAD_DIM), 1) == c - 1
    g_last = jnp.sum(jnp.where(last, g_wide, 0.0), axis=1, keepdims=True)
    qd = q * e_wide
    kd = k * jnp.exp(g_last - g_wide)
    gl = jnp.broadcast_to(jnp.exp(g_last), (nb, 8, HEAD_DIM))
    return u, wk, qd, kd, qk, gl, inv


def _gdn_intra_fn(q, k, v, gb, *inv_known):
    t = q.shape[0]
    nch = t // CHUNK
    lane = lax.broadcasted_iota(jnp.int32, gb.shape, 1)

    def heads_first(x_):
        return jnp.concatenate([x_[:, HEAD_DIM * h:HEAD_DIM * (h + 1)].reshape(nch, CHUNK, HEAD_DIM) for h in range(HEADS)],
                               axis=0)

    def column(first_lane):
        return jnp.concatenate([jnp.sum(jnp.where(lane == first_lane + h, gb, 0.0), axis=1, keepdims=True)
                                .reshape(nch, CHUNK, 1) for h in range(HEADS)], axis=0)

    known = jnp.concatenate([x_.reshape(nch, CHUNK, CHUNK) for x_ in inv_known], axis=0) if inv_known else None
    u, wk, qd, kd, qk, gl, inv = _intra_batched(heads_first(q), heads_first(k), heads_first(v), column(64), column(68), known)

    def rows_first(x_):
        r, w_ = x_.shape[1], x_.shape[2]
        return jnp.concatenate([x_[nch * h:nch * (h + 1)].reshape(nch * r, w_) for h in range(HEADS)], axis=1)

    per_head = lambda x_: [x_[nch * h:nch * (h + 1)].reshape(t, CHUNK) for h in range(HEADS)]
    outs = (rows_first(u), rows_first(wk), rows_first(qd), rows_first(kd), *per_head(qk), rows_first(gl))
    return outs if inv_known else outs + tuple(per_head(inv))


def _scan_step(s0, u, wk, qd, kd, qk, gl):
    v_new = u - _bdot_b(wk, s0, False)
    o = _bdot_b(qd, s0, False) + _bdot_b(qk, v_new, False)
    s1 = s0 * gl[:, 0:1, :] + _bdot_b(jnp.swapaxes(kd, 1, 2), v_new, False)
    return o, s1


def _mix_post_fn(o_a, z, o_b, gnw, onw):
    parts = [_rms(o_a[:, HEAD_DIM * h:HEAD_DIM * (h + 1)], gnw) * _silu(z[:, HEAD_DIM * h:HEAD_DIM * (h + 1)])
             for h in range(HEADS)]
    parts += [_rms(o_b[:, HEAD_DIM * h:HEAD_DIM * (h + 1)], onw) for h in range(HEADS)]
    return (jnp.concatenate(parts, axis=1),)


def _mla_pre_fn(ckv, cq, kab, cos_p, sin_p, qnw, kvnw, wuq, wukv, qn_w, qr_w, kn_w, kr_w):
    scale = (HEAD_DIM + ROPE) ** -0.5 * LOG2_E
    qf = _bdot(_rms(cq, qnw), wuq, "nn")
    kvf = _bdot(_rms(ckv, kvnw), wukv, "nn")
    lane = lax.broadcasted_iota(jnp.int32, kab.shape, 1)
    kr = _rope128(_rms(jnp.where(lane < ROPE, kab, 0.0), kr_w, n=ROPE), cos_p, sin_p)
    qs, ks = [], []
    for h in range(HEADS):
        qn = _rms(qf[:, 256 * h:256 * h + 128], qn_w) * scale
        qr = _rope128(_rms(qf[:, 256 * h + 128:256 * h + 256], qr_w, n=ROPE), cos_p, sin_p) * scale
        qs += [qn, qr]
        ks += [_rms(kvf[:, 128 * h:128 * (h + 1)], kn_w), kr]
    return jnp.concatenate(qs, axis=1), jnp.concatenate(ks, axis=1), kvf[:, 512:]


def _conv_fwd(proj, conv_w, tm, name):
    S = proj.shape[0]
    C = 1536
    nb = tm // 8

    def body(x_ref, prev_ref, w_ref, o_ref, ext_ref):
        i = pl.program_id(0)
        ext_ref[0:8, :] = jnp.where(i > 0, prev_ref[...], 0.0)
        ext_ref[8:, :] = x_ref[...]
        acc = jnp.zeros((tm, C), F32)
        for k in range(4):
            acc = acc + w_ref[k:k + 1, :] * ext_ref[pl.ds(5 + k, tm), :]
        o_ref[...] = acc

    return pl.pallas_call(
        body, name=name, grid=(S // tm,),
        in_specs=[pl.BlockSpec((tm, C), lambda i: (i, 0)),
                  pl.BlockSpec((8, C), lambda i: (jnp.maximum(i * nb - 1, 0), 0)),
                  pl.BlockSpec((4, C), lambda i: (0, 0))],
        out_specs=pl.BlockSpec((tm, C), lambda i: (i, 0)),
        out_shape=jax.ShapeDtypeStruct((S, C), F32),
        scratch_shapes=[pltpu.VMEM((tm + 8, C), F32)],
        compiler_params=_params(("arbitrary",)),
    )(proj, proj, conv_w)


def _conv_bwd(proj, dout, conv_w, tm, name):
    S = proj.shape[0]
    C = 1536
    nb = tm // 8
    n_steps = S // tm

    def body(x_ref, prev_ref, d_ref, next_ref, w_ref, dx_ref, dw_ref, xext_ref, dext_ref):
        i = pl.program_id(0)
        xext_ref[0:8, :] = jnp.where(i > 0, prev_ref[...], 0.0)
        xext_ref[8:, :] = x_ref[...]
        dext_ref[0:tm, :] = d_ref[...]
        dext_ref[tm:, :] = jnp.where(i < n_steps - 1, next_ref[...], 0.0)
        d = d_ref[...]
        acc = jnp.zeros((tm, C), F32)
        dws = []
        for k in range(4):
            acc = acc + w_ref[k:k + 1, :] * dext_ref[pl.ds(3 - k, tm), :]
            dws.append(jnp.sum(d * xext_ref[pl.ds(5 + k, tm), :], axis=0, keepdims=True))
        dx_ref[...] = acc

        @pl.when(i == 0)
        def _():
            dw_ref[...] = jnp.zeros_like(dw_ref)

        dw_ref[...] += jnp.concatenate(dws + [jnp.zeros((4, C), F32)], axis=0)

    return pl.pallas_call(
        body, name=name, grid=(n_steps,),
        in_specs=[pl.BlockSpec((tm, C), lambda i: (i, 0)),
                  pl.BlockSpec((8, C), lambda i: (jnp.maximum(i * nb - 1, 0), 0)),
                  pl.BlockSpec((tm, C), lambda i: (i, 0)),
                  pl.BlockSpec((8, C), lambda i: (jnp.minimum((i + 1) * nb, S // 8 - 1), 0)),
                  pl.BlockSpec((4, C), lambda i: (0, 0))],
        out_specs=[pl.BlockSpec((tm, C), lambda i: (i, 0)), pl.BlockSpec((8, C), lambda i: (0, 0))],
        out_shape=[jax.ShapeDtypeStruct((S, C), F32), jax.ShapeDtypeStruct((8, C), F32)],
        scratch_shapes=[pltpu.VMEM((tm + 8, C), F32), pltpu.VMEM((tm + 8, C), F32)],
        compiler_params=_params(("arbitrary",)),
    )(proj, proj, dout, dout, conv_w)


SCAN_CHUNKS = (4, 2, 1)


def _gdn_scan_fwd(u, wk, qd, kd, qks, gl, name):
    S = u.shape[0]
    nc = S // CHUNK
    cs = _pick(nc, SCAN_CHUNKS)
    W = HEADS * HEAD_DIM

    def body(u_ref, wk_ref, qd_ref, kd_ref, qk0, qk1, qk2, qk3, gl_ref, o_ref, sp_ref, s_ref):
        @pl.when(pl.program_id(0) == 0)
        def _():
            s_ref[...] = jnp.zeros_like(s_ref)

        state = s_ref[...]
        for c in range(cs):
            rows, gl_rows = slice(CHUNK * c, CHUNK * (c + 1)), slice(8 * c, 8 * (c + 1))
            sp_ref[c] = state
            o, state = _scan_step(state, _heads(u_ref, HEAD_DIM, rows), _heads(wk_ref, HEAD_DIM, rows),
                                  _heads(qd_ref, HEAD_DIM, rows), _heads(kd_ref, HEAD_DIM, rows),
                                  jnp.stack([r[rows, :] for r in (qk0, qk1, qk2, qk3)]), _heads(gl_ref, HEAD_DIM, gl_rows))
            for h in range(HEADS):
                o_ref[rows, HEAD_DIM * h:HEAD_DIM * (h + 1)] = o[h]
        s_ref[...] = state

    row = pl.BlockSpec((cs * CHUNK, W), lambda n: (n, 0))
    qk_spec = pl.BlockSpec((cs * CHUNK, CHUNK), lambda n: (n, 0))
    return pl.pallas_call(
        body, name=name, grid=(nc // cs,),
        in_specs=[row, row, row, row, qk_spec, qk_spec, qk_spec, qk_spec, pl.BlockSpec((cs * 8, W), lambda n: (n, 0))],
        out_specs=[row, pl.BlockSpec((cs, HEADS, HEAD_DIM, HEAD_DIM), lambda n: (n, 0, 0, 0))],
        out_shape=[jax.ShapeDtypeStruct((S, W), F32), jax.ShapeDtypeStruct((nc, HEADS, HEAD_DIM, HEAD_DIM), F32)],
        scratch_shapes=[pltpu.VMEM((HEADS, HEAD_DIM, HEAD_DIM), F32)],
        compiler_params=_params(("arbitrary",)),
    )(u, wk, qd, kd, *qks, gl)


def _gdn_scan_bwd(u, wk, qd, kd, qks, gl, s_prev, d_o, name):
    S = u.shape[0]
    nc = S // CHUNK
    cs = _pick(nc, SCAN_CHUNKS)
    nb = nc // cs
    W = HEADS * HEAD_DIM

    def body(u_ref, wk_ref, qd_ref, kd_ref, qk0, qk1, qk2, qk3, gl_ref, sp_ref, do_ref,
             du_ref, dwk_ref, dqd_ref, dkd_ref, dqk0, dqk1, dqk2, dqk3, dgl_ref, ds_ref):
        @pl.when(pl.program_id(0) == 0)
        def _():
            ds_ref[...] = jnp.zeros_like(ds_ref)

        d_state = ds_ref[...]
        for c in reversed(range(cs)):
            rows, gl_rows = slice(CHUNK * c, CHUNK * (c + 1)), slice(8 * c, 8 * (c + 1))
            _, vjp = jax.vjp(_scan_step, sp_ref[c], _heads(u_ref, HEAD_DIM, rows), _heads(wk_ref, HEAD_DIM, rows),
                             _heads(qd_ref, HEAD_DIM, rows), _heads(kd_ref, HEAD_DIM, rows),
                             jnp.stack([r[rows, :] for r in (qk0, qk1, qk2, qk3)]), _heads(gl_ref, HEAD_DIM, gl_rows))
            d_state, du, dwk, dqd, dkd, dqk, dgl = vjp((_heads(do_ref, HEAD_DIM, rows), d_state))
            for h, dqk_ref in enumerate((dqk0, dqk1, dqk2, dqk3)):
                sl = slice(HEAD_DIM * h, HEAD_DIM * (h + 1))
                du_ref[rows, sl] = du[h]
                dwk_ref[rows, sl] = dwk[h]
                dqd_ref[rows, sl] = dqd[h]
                dkd_ref[rows, sl] = dkd[h]
                dqk_ref[rows, :] = dqk[h]
                dgl_ref[gl_rows, sl] = dgl[h]
        ds_ref[...] = d_state

    rev = lambda n: (nb - 1 - n, 0)
    row = pl.BlockSpec((cs * CHUNK, W), rev)
    qk_spec = pl.BlockSpec((cs * CHUNK, CHUNK), rev)
    gl_spec = pl.BlockSpec((cs * 8, W), rev)
    qk_shape = jax.ShapeDtypeStruct((S, CHUNK), F32)
    row_shape = jax.ShapeDtypeStruct((S, W), F32)
    return pl.pallas_call(
        body, name=name, grid=(nb,),
        in_specs=[row, row, row, row, qk_spec, qk_spec, qk_spec, qk_spec, gl_spec,
                  pl.BlockSpec((cs, HEADS, HEAD_DIM, HEAD_DIM), lambda n: (nb - 1 - n, 0, 0, 0)), row],
        out_specs=[row, row, row, row, qk_spec, qk_spec, qk_spec, qk_spec, gl_spec],
        out_shape=[row_shape] * 4 + [qk_shape] * 4 + [jax.ShapeDtypeStruct((nc * 8, W), F32)],
        scratch_shapes=[pltpu.VMEM((HEADS, HEAD_DIM, HEAD_DIM), F32)],
        compiler_params=_params(("arbitrary",)),
    )(u, wk, qd, kd, *qks, gl, s_prev, d_o)


NEG = -1e30


def _chunk_mask(i, j, t, transposed=False):
    q_axis, k_axis = (1, 0) if transposed else (0, 1)
    r = (i * t + lax.broadcasted_iota(jnp.int32, (t, t), q_axis)) // CHUNK
    c = (j * t + lax.broadcasted_iota(jnp.int32, (t, t), k_axis)) // CHUNK
    return c <= r


def _tile_pairs(n, by_key):
    pairs = [(i, j) for j in range(n) for i in range(j, n)] if by_key else [(i, j) for i in range(n) for j in range(i + 1)]
    return jnp.asarray(np.array([p[0] for p in pairs], np.int32)), jnp.asarray(np.array([p[1] for p in pairs], np.int32))


def _heads(ref, width, rows=slice(None)):
    return jnp.stack([ref[rows, width * h:width * (h + 1)] for h in range(HEADS)])


def _bmm(a, b, dims):
    return lax.dot_general(a.astype(BF16), b.astype(BF16), dims, preferred_element_type=F32)


def _attn_fwd(q, k, v_t, t, name):
    S = q.shape[0]
    n = S // t
    qi, kj = _tile_pairs(n, by_key=False)

    def body(qi_ref, kj_ref, q_ref, k_ref, vt_ref, o_ref, lse_ref, m_ref, l_ref, acc_ref):
        i, j = qi_ref[pl.program_id(0)], kj_ref[pl.program_id(0)]

        @pl.when(j == 0)
        def _():
            m_ref[...] = jnp.full_like(m_ref, NEG)
            l_ref[...] = jnp.zeros_like(l_ref)
            acc_ref[...] = jnp.zeros_like(acc_ref)

        def update(masked):
            s_t = _bmm(_heads(k_ref, 256), _heads(q_ref, 256), _BNT)
            if masked:
                s_t = jnp.where(_chunk_mask(i, j, t, transposed=True)[None], s_t, NEG)
            m_old = m_ref[...]
            m_new = jnp.maximum(m_old, jnp.max(s_t, axis=1, keepdims=True))
            p_t = jnp.exp2(s_t - m_new)
            alpha = jnp.exp2(m_old - m_new)
            l_ref[...] = alpha * l_ref[...] + jnp.sum(p_t, axis=1, keepdims=True)
            v_heads = jnp.stack([vt_ref[HEAD_DIM * h:HEAD_DIM * (h + 1), :] for h in range(HEADS)])
            acc_ref[...] = alpha * acc_ref[...] + _bmm(v_heads, p_t, _BNN)
            m_ref[...] = m_new

        @pl.when(j < i)
        def _():
            update(False)

        @pl.when(j == i)
        def _():
            update(True)
            for h in range(HEADS):
                sl = slice(HEAD_DIM * h, HEAD_DIM * (h + 1))
                o_ref[:, sl] = jnp.transpose(acc_ref[h] / l_ref[h])
                lse_ref[:, sl] = jnp.transpose(jnp.broadcast_to(m_ref[h] + jnp.log(l_ref[h]) * LOG2_E, (HEAD_DIM, t)))

    row = lambda p, qi_, kj_: (qi_[p], 0)
    return pl.pallas_call(
        body, name=name,
        grid_spec=pltpu.PrefetchScalarGridSpec(
            num_scalar_prefetch=2, grid=(qi.shape[0],),
            in_specs=[pl.BlockSpec((t, HEADS * 256), row), pl.BlockSpec((t, HEADS * 256), lambda p, qi_, kj_: (kj_[p], 0)),
                      pl.BlockSpec((HEADS * HEAD_DIM, t), lambda p, qi_, kj_: (0, kj_[p]))],
            out_specs=[pl.BlockSpec((t, HEADS * HEAD_DIM), row)] * 2,
            scratch_shapes=[pltpu.VMEM((HEADS, 1, t), F32), pltpu.VMEM((HEADS, 1, t), F32),
                            pltpu.VMEM((HEADS, HEAD_DIM, t), F32)]),
        out_shape=[jax.ShapeDtypeStruct((S, HEADS * HEAD_DIM), F32)] * 2,
        compiler_params=_params(("arbitrary",)),
    )(qi, kj, q, k, v_t)


def _attn_stats(o, lse, d_o, t, name):
    S = o.shape[0]

    def body(o_ref, lse_ref, do_ref, st_ref):
        lane = lax.broadcasted_iota(jnp.int32, (t, HEAD_DIM), 1)
        stats = jnp.zeros((t, HEAD_DIM), F32)
        for h in range(HEADS):
            sl = slice(HEAD_DIM * h, HEAD_DIM * (h + 1))
            delta = jnp.sum(do_ref[:, sl] * o_ref[:, sl], axis=1, keepdims=True)
            stats = stats + jnp.where(lane == h, lse_ref[:, sl], 0.0) + jnp.where(lane == HEADS + h, delta, 0.0)
        st_ref[...] = jnp.transpose(stats)[0:8, :]

    row = pl.BlockSpec((t, HEADS * HEAD_DIM), lambda i: (i, 0))
    return pl.pallas_call(
        body, name=name, grid=(S // t,),
        in_specs=[row, row, row], out_specs=pl.BlockSpec((8, t), lambda i: (0, i)),
        out_shape=jax.ShapeDtypeStruct((8, S), F32),
        compiler_params=_params(("parallel",)),
    )(o, lse, d_o)


BWD_GROUP = 2


def _attn_bwd(q, k, v, d_o, stats, t, name):
    S = q.shape[0]
    n = S // t
    groups = HEADS // BWD_GROUP
    gq, gv = BWD_GROUP * 256, BWD_GROUP * HEAD_DIM
    qi, kj = _tile_pairs(n, by_key=True)
    n_pairs = qi.shape[0]
    st = stats.reshape(2, groups, BWD_GROUP, S).transpose(1, 0, 2, 3).reshape(groups, 2 * BWD_GROUP, S)
    st = jnp.pad(st, ((0, 0), (0, 8 - 2 * BWD_GROUP), (0, 0)))

    def heads(ref, width, rows=slice(None)):
        return jnp.stack([ref[rows, width * h:width * (h + 1)] for h in range(BWD_GROUP)])

    def body(qi_ref, kj_ref, q_ref, k_ref, v_ref, do_ref, st_ref, dq_hbm, dk_ref, dv_ref, dq_acc, sem):
        g, p = pl.program_id(0), pl.program_id(1)
        i, j = qi_ref[p], kj_ref[p]

        @pl.when(i == j)
        def _():
            dk_ref[...] = jnp.zeros_like(dk_ref)
            dv_ref[...] = jnp.zeros_like(dv_ref)

        def update(masked):
            qh, kh = heads(q_ref, 256), heads(k_ref, 256)
            d_out = heads(do_ref, HEAD_DIM)
            stv = st_ref[...]
            lse_row = jnp.stack([stv[h:h + 1, :] for h in range(BWD_GROUP)])
            delta_row = jnp.stack([stv[BWD_GROUP + h:BWD_GROUP + h + 1, :] for h in range(BWD_GROUP)])
            s_t = _bmm(kh, qh, _BNT)
            p_t = jnp.exp2(s_t - lse_row)
            if masked:
                p_t = jnp.where(_chunk_mask(i, j, t, transposed=True)[None], p_t, 0.0)
            dv = _bmm(p_t, d_out, _BNN)
            dp_t = _bmm(heads(v_ref, HEAD_DIM), d_out, _BNT)
            ds_t = p_t * (dp_t - delta_row)
            dk = _bmm(ds_t, qh, _BNN)
            dq = _bmm(jnp.swapaxes(ds_t, 1, 2), kh, _BNN)
            rows = pl.ds(pl.multiple_of(i * t, t), t)
            for h in range(BWD_GROUP):
                dk_ref[:, 256 * h:256 * (h + 1)] += dk[h]
                dv_ref[:, HEAD_DIM * h:HEAD_DIM * (h + 1)] += dv[h]

            @pl.when(j == 0)
            def _():
                for h in range(BWD_GROUP):
                    dq_acc[rows, 256 * h:256 * (h + 1)] = dq[h]

            @pl.when(j > 0)
            def _():
                for h in range(BWD_GROUP):
                    dq_acc[rows, 256 * h:256 * (h + 1)] += dq[h]

        @pl.when(i == j)
        def _():
            update(True)

        @pl.when(i > j)
        def _():
            update(False)

        @pl.when(i == n - 1)
        def _():
            dk_ref[...] *= 1.0 / LOG2_E

        @pl.when(p == n_pairs - 1)
        def _():
            dq_acc[...] *= 1.0 / LOG2_E
            for gg in range(groups):
                @pl.when(g == gg)
                def _():
                    cp = pltpu.make_async_copy(dq_acc, dq_hbm.at[:, gq * gg:gq * (gg + 1)], sem)
                    cp.start()
                    cp.wait()

    q_blk = lambda g, p, qi_, kj_: (qi_[p], g)
    k_blk = lambda g, p, qi_, kj_: (kj_[p], g)
    return pl.pallas_call(
        body, name=name,
        grid_spec=pltpu.PrefetchScalarGridSpec(
            num_scalar_prefetch=2, grid=(groups, n_pairs),
            in_specs=[pl.BlockSpec((t, gq), q_blk), pl.BlockSpec((t, gq), k_blk), pl.BlockSpec((t, gv), k_blk),
                      pl.BlockSpec((t, gv), q_blk), pl.BlockSpec((None, 8, t), lambda g, p, qi_, kj_: (g, 0, qi_[p]))],
            out_specs=[pl.BlockSpec(memory_space=pl.ANY), pl.BlockSpec((t, gq), k_blk), pl.BlockSpec((t, gv), k_blk)],
            scratch_shapes=[pltpu.VMEM((S, gq), F32), pltpu.SemaphoreType.DMA]),
        out_shape=[jax.ShapeDtypeStruct((S, HEADS * 256), F32), jax.ShapeDtypeStruct((S, HEADS * 256), F32),
                   jax.ShapeDtypeStruct((S, HEADS * HEAD_DIM), F32)],
        compiler_params=_params(("arbitrary", "arbitrary")),
    )(qi, kj, q, k, v, d_o, st)


FFN_PIECE = 2 * D_FF // N_DEV
HID_PIECES = D_FF // FFN_PIECE


def _ffn_up(h, w8, name):
    S = h.shape[0]
    tm = _pick(S, MATMUL_ROWS)

    def body(h_ref, wg_ref, wu_ref, g_ref, u_ref, hid_ref, hid_t_ref):
        gate = _dot_raw(h_ref[...], wg_ref[...], "nn")
        up = _dot_raw(h_ref[...], wu_ref[...], "nn")
        sg = _sigmoid(gate)
        act = gate * sg
        hid = act * up
        g_ref[...] = (up * (sg * (1.0 + gate * (1.0 - sg)))).astype(BF16)
        u_ref[...] = act.astype(BF16)
        hid_ref[...] = hid.astype(BF16)
        hid_t_ref[...] = jnp.transpose(hid).astype(BF16)

    o_spec = pl.BlockSpec((None, tm, FFN_PIECE), lambda i, j: (j, i, 0))
    return pl.pallas_call(
        body, name=name, grid=(S // tm, HID_PIECES),
        in_specs=[pl.BlockSpec((tm, D_MODEL), lambda i, j: (i, 0)),
                  pl.BlockSpec((None, D_MODEL, FFN_PIECE), lambda i, j: (j, 0, 0)),
                  pl.BlockSpec((None, D_MODEL, FFN_PIECE), lambda i, j: (j + HID_PIECES, 0, 0))],
        out_specs=[o_spec] * 3 + [pl.BlockSpec((None, FFN_PIECE, tm), lambda i, j: (j, 0, i))],
        out_shape=[jax.ShapeDtypeStruct((HID_PIECES, S, FFN_PIECE), BF16)] * 3
        + [jax.ShapeDtypeStruct((HID_PIECES, FFN_PIECE, S), BF16)],
        compiler_params=_params(("parallel", "parallel")),
    )(h, w8, w8)


def _after_specs(after):
    return [] if after is None else [pl.BlockSpec(memory_space=pl.ANY)]


def _after_args(after):
    return [] if after is None else [after]


def _ffn_gw8(h, d_gate, d_up, name, after=None):
    S = h.shape[0]
    tm = 512
    tk = _pick(S, (512, 256, 128))
    nk = S // tk

    def body(h_ref, dg_ref, du_ref, *rest):
        o_ref, acc_ref = rest[-2:]
        k = pl.program_id(1)

        @pl.when(k == 0)
        def _():
            acc_ref[...] = jnp.zeros_like(acc_ref)

        h_t = jnp.transpose(h_ref[...])
        for p in range(HID_PIECES):
            acc_ref[p] += _dot_raw(h_t, dg_ref[p], "nn")
            acc_ref[HID_PIECES + p] += _dot_raw(h_t, du_ref[p], "nn")

        @pl.when(k == nk - 1)
        def _():
            o_ref[...] = acc_ref[...].astype(o_ref.dtype)

    d_spec = pl.BlockSpec((HID_PIECES, tk, FFN_PIECE), lambda i, k: (0, k, 0))
    return pl.pallas_call(
        body, name=name, grid=(D_MODEL // tm, nk),
        in_specs=[pl.BlockSpec((tk, tm), lambda i, k: (k, i)), d_spec, d_spec] + _after_specs(after),
        out_specs=pl.BlockSpec((2 * HID_PIECES, tm, FFN_PIECE), lambda i, k: (0, i, 0)),
        out_shape=jax.ShapeDtypeStruct((2 * HID_PIECES, D_MODEL, FFN_PIECE), BF16),
        scratch_shapes=[pltpu.VMEM((2 * HID_PIECES, tm, FFN_PIECE), F32)],
        compiler_params=_params(("parallel", "arbitrary")),
    )(h, d_gate, d_up, *_after_args(after))


EPILOGUE_ROWS = 256


def _dmod_epilogue(acc_ref, x_ref, do_ref, sc_ref, sh_ref, dx_ref, dsc_ref, dsh_ref, below, below_in, below_out):
    rows_total = acc_ref.shape[0]
    step = min(EPILOGUE_ROWS, rows_total)
    dsc, dsh, dg = 0.0, 0.0, 0.0
    for r in range(rows_total // step):
        rows = slice(step * r, step * (r + 1))
        _, vjp = jax.vjp(_modulate, x_ref[rows, :], sc_ref[...], sh_ref[...])
        dx, dsc_r, dsh_r = vjp(acc_ref[rows, :])
        dx = dx + do_ref[rows, :]
        dx_ref[rows, :] = dx
        dsc, dsh = dsc + dsc_r, dsh + dsh_r
        if below is not None:
            coef = below[2]
            below_out[0][rows, :] = (coef * below_in[1][...] * dx).astype(below_out[0].dtype)
            dg = dg + jnp.sum(coef * below_in[0][rows, :] * dx, axis=0, keepdims=True)
    dsc_ref[...] += dsc
    dsh_ref[...] += dsh
    if below is not None:
        below_out[1][...] += dg


def _ffn_dh(d_gate, d_up, w8, x, d_out, scale, shift, name, after=None, below=None):
    S = d_gate.shape[1]
    tm = _pick(S, MATMUL_ROWS if below is None else MATMUL_ROWS[1:])
    n_below = 0 if below is None else 2

    def body(dg_ref, du_ref, wg_ref, wu_ref, x_ref, do_ref, sc_ref, sh_ref, *rest):
        below_in = rest[:n_below]
        outs = rest[len(rest) - 4 - n_below:]
        dx_ref, dsc_ref, dsh_ref = outs[:3]
        below_out, acc_ref = outs[3:3 + n_below], outs[-1]
        i, k = pl.program_id(0), pl.program_id(1)

        @pl.when(k == 0)
        def _():
            acc_ref[...] = jnp.zeros_like(acc_ref)

        acc_ref[...] += _dot_raw(dg_ref[...], wg_ref[...], "nt") + _dot_raw(du_ref[...], wu_ref[...], "nt")

        @pl.when((k == 0) & (i == 0))
        def _():
            for r in (dsc_ref, dsh_ref) + tuple(below_out[1:]):
                r[...] = jnp.zeros_like(r)

        @pl.when(k == HID_PIECES - 1)
        def _():
            _dmod_epilogue(acc_ref, x_ref, do_ref, sc_ref, sh_ref, dx_ref, dsc_ref, dsh_ref, below, below_in, below_out)

    d_spec = pl.BlockSpec((None, tm, FFN_PIECE), lambda i, k: (k, i, 0))
    row = pl.BlockSpec((tm, D_MODEL), lambda i, k: (i, 0))
    par = pl.BlockSpec((1, D_MODEL), lambda i, k: (0, 0))
    row_shape, par_shape = jax.ShapeDtypeStruct((S, D_MODEL), F32), jax.ShapeDtypeStruct((1, D_MODEL), F32)
    return pl.pallas_call(
        body, name=name, grid=(S // tm, HID_PIECES),
        in_specs=[d_spec, d_spec,
                  pl.BlockSpec((None, D_MODEL, FFN_PIECE), lambda i, k: (k, 0, 0)),
                  pl.BlockSpec((None, D_MODEL, FFN_PIECE), lambda i, k: (k + HID_PIECES, 0, 0)),
                  row, row, par, par] + [row, par][:n_below] + _after_specs(after),
        out_specs=[row, par, par] + [row, par][:n_below],
        out_shape=[row_shape, par_shape, par_shape] + [jax.ShapeDtypeStruct((S, D_MODEL), BF16), par_shape][:n_below],
        scratch_shapes=[pltpu.VMEM((tm, D_MODEL), F32)],
        compiler_params=_params(("arbitrary", "arbitrary")),
    )(d_gate, d_up, w8, w8, x, d_out, scale, shift, *(below[:2] if below is not None else ()), *_after_args(after))


def _swiglu_bwd(d_hid, hid_by_gate, hid_by_up):
    return d_hid * hid_by_gate, d_hid * hid_by_up


def _adamw_math(w_, g_, m_, v_):
    m_ = ADAM_B1 * m_ + (1.0 - ADAM_B1) * g_
    v_ = ADAM_B2 * v_ + (1.0 - ADAM_B2) * (g_ * g_)
    m_hat = m_ / (1.0 - ADAM_B1 ** ADAM_STEP)
    v_hat = v_ / (1.0 - ADAM_B2 ** ADAM_STEP)
    return -ADAM_LR * (m_hat / (jnp.sqrt(v_hat) + ADAM_EPS) + ADAM_WD * w_), m_, v_


def _adamw(w, g, m, v, name):
    R, C = w.shape
    tr = _pick(R, (256, 176, 128, 64, 32, 16, 8))

    def body(w_ref, g_ref, m_ref, v_ref, d_ref, nm_ref, nv_ref):
        d_ref[...], nm_ref[...], nv_ref[...] = _adamw_math(w_ref[...], g_ref[...], m_ref[...], v_ref[...])

    spec = pl.BlockSpec((tr, C), lambda i: (i, 0))
    return pl.pallas_call(
        body, name=name, grid=(R // tr,),
        in_specs=[spec] * 4, out_specs=[spec] * 3,
        out_shape=[jax.ShapeDtypeStruct((R, C), F32)] * 3,
        compiler_params=_params(("parallel",)),
    )(w, g, m, v)


def _sum_adamw(parts, w, m, v, name):
    R, C = w.shape
    tr = _pick(R, (256, 176, 128, 64, 32, 16, 8))

    def body(p_ref, w_ref, m_ref, v_ref, g_ref, d_ref, nm_ref, nv_ref):
        g_ = p_ref[0].astype(F32)
        for d in range(1, N_DEV):
            g_ = g_ + p_ref[d].astype(F32)
        g_ref[...] = g_
        d_ref[...], nm_ref[...], nv_ref[...] = _adamw_math(w_ref[...], g_, m_ref[...], v_ref[...])

    spec = pl.BlockSpec((tr, C), lambda i: (i, 0))
    return pl.pallas_call(
        body, name=name, grid=(R // tr,),
        in_specs=[pl.BlockSpec((N_DEV, tr, C), lambda i: (0, i, 0)), spec, spec, spec], out_specs=[spec] * 4,
        out_shape=[jax.ShapeDtypeStruct((R, C), F32)] * 4,
        compiler_params=_params(("parallel",)),
    )(parts, w, m, v)


def _sum_devices(parts, name):
    _, R, C = parts.shape
    tr = _pick(R, (512, 256, 176, 128, 64, 32, 16, 8))

    def body(p_ref, o_ref):
        acc = p_ref[0].astype(F32)
        for d in range(1, N_DEV):
            acc = acc + p_ref[d].astype(F32)
        o_ref[...] = acc

    return pl.pallas_call(
        body, name=name, grid=(R // tr,),
        in_specs=[pl.BlockSpec((N_DEV, tr, C), lambda i: (0, i, 0))],
        out_specs=pl.BlockSpec((tr, C), lambda i: (i, 0)),
        out_shape=jax.ShapeDtypeStruct((R, C), F32),
        compiler_params=_params(("parallel",)),
    )(parts)


def _my_place():
    return lax.axis_index("x"), lax.axis_index("y"), lax.axis_index("c")


def _all_gather(blocks, name):
    n = len(blocks)

    def body(*refs):
        x_refs, out_refs = refs[:n], refs[n:2 * n]
        send_sems, recv_sems, local_sems = refs[2 * n:]
        x, y, c = _my_place()
        me, sibling = (x, y, c), (x, y, 1 - c)
        chips = [(1 - x, y), (x, 1 - y), (1 - x, 1 - y)]

        def copy(a, k, blk, to, own=False):
            slot = out_refs[a].at[4 * blk[0] + 2 * blk[1] + blk[2]]
            return pltpu.make_async_remote_copy(
                src_ref=x_refs[a] if own else slot, dst_ref=slot,
                send_sem=send_sems.at[7 * a + k], recv_sem=recv_sems.at[7 * a + k], device_id=to, device_id_type=MESH)

        mine = [pltpu.make_async_copy(x_refs[a], out_refs[a].at[4 * x + 2 * y + c], local_sems.at[a]) for a in range(n)]
        for cp in mine:
            cp.start()
        first = []
        for j, chip in enumerate(chips):
            first += [copy(a, 1 + j, me, (*chip, c), own=True) for a in range(n)]
        first += [copy(a, 0, me, sibling, own=True) for a in range(n)]
        for cp in first:
            cp.start()
        passed = []
        for j, chip in enumerate(chips):
            for a in range(n):
                copy(a, 1 + j, (*chip, c), me).wait_recv()
                passed.append(copy(a, 4 + j, (*chip, c), sibling))
                passed[-1].start()
        for a in range(n):
            copy(a, 0, sibling, me).wait_recv()
        for j, chip in enumerate(chips):
            for a in range(n):
                copy(a, 4 + j, (*chip, 1 - c), me).wait_recv()
        for cp in first + passed:
            cp.wait_send()
        for cp in mine:
            cp.wait()

    return pl.pallas_call(
        body, name=name,
        out_shape=[jax.ShapeDtypeStruct((N_DEV,) + b.shape, b.dtype) for b in blocks],
        in_specs=[pl.BlockSpec(memory_space=pl.ANY)] * n,
        out_specs=[pl.BlockSpec(memory_space=pl.ANY)] * n,
        scratch_shapes=[pltpu.SemaphoreType.DMA((7 * n,)), pltpu.SemaphoreType.DMA((7 * n,)), pltpu.SemaphoreType.DMA((n,))],
    )(*blocks)


def _all_to_all(pieces, name):
    n = len(pieces)

    def body(*refs):
        x_refs, out_refs = refs[:n], refs[n:2 * n]
        send_sems, recv_sems, local_sems = refs[2 * n:]
        x, y, c = _my_place()
        me = 4 * x + 2 * y + c
        mine = [pltpu.make_async_copy(x_refs[a].at[me], out_refs[a].at[me], local_sems.at[a]) for a in range(n)]
        for cp in mine:
            cp.start()
        copies = []
        for k in (2, 4, 6, 3, 5, 7, 1):
            px = 1 - x if k & 4 else x
            py = 1 - y if k & 2 else y
            pc = 1 - c if k & 1 else c
            peer = 4 * px + 2 * py + pc
            for a in range(n):
                copies.append(pltpu.make_async_remote_copy(
                    src_ref=x_refs[a].at[peer], dst_ref=out_refs[a].at[me],
                    send_sem=send_sems.at[7 * a + k - 1], recv_sem=recv_sems.at[7 * a + k - 1],
                    device_id=(px, py, pc), device_id_type=MESH))
        for cp in copies:
            cp.start()
        for cp in copies:
            cp.wait_recv()
        for cp in copies:
            cp.wait_send()
        for cp in mine:
            cp.wait()

    return pl.pallas_call(
        body, name=name,
        out_shape=[jax.ShapeDtypeStruct(p.shape, p.dtype) for p in pieces],
        in_specs=[pl.BlockSpec(memory_space=pl.ANY)] * n,
        out_specs=[pl.BlockSpec(memory_space=pl.ANY)] * n,
        scratch_shapes=[pltpu.SemaphoreType.DMA((7 * n,)), pltpu.SemaphoreType.DMA((7 * n,)), pltpu.SemaphoreType.DMA((n,))],
    )(*pieces)


def _peers():
    x, y, c = _my_place()
    out = []
    for k in (2, 4, 6, 3, 5, 7, 1):
        px = 1 - x if k & 4 else x
        py = 1 - y if k & 2 else y
        pc = 1 - c if k & 1 else c
        out.append((k, (px, py, pc), 4 * px + 2 * py + pc))
    return out


def _exchange_copies(x_refs, land_refs, send_sems, recv_sems, scatter):
    x, y, c = _my_place()
    me = 4 * x + 2 * y + c
    starts, arrivals = [], []
    for k, place, peer in _peers():
        for a, (x_ref, land_ref) in enumerate(zip(x_refs, land_refs)):
            sems = dict(send_sem=send_sems.at[7 * a + k - 1], recv_sem=recv_sems.at[7 * a + k - 1],
                        device_id=place, device_id_type=MESH)
            src = x_ref.at[peer] if scatter else x_ref
            starts.append(pltpu.make_async_remote_copy(src_ref=src, dst_ref=land_ref.at[me], **sems))
            arrivals.append(pltpu.make_async_remote_copy(src_ref=src, dst_ref=land_ref.at[peer], **sems))
    return starts, arrivals


def _exchange_start(arrays, scatter, name):
    n = len(arrays)
    hbm = pl.BlockSpec(memory_space=pltpu.HBM)
    sem = pl.BlockSpec(memory_space=pltpu.SEMAPHORE)
    lands = [lax.empty(a.shape if scatter else (N_DEV,) + a.shape, a.dtype) for a in arrays]

    def body(*refs):
        x_refs, land_refs = refs[:n], refs[n:2 * n]
        send_sems, recv_sems = refs[2 * n], refs[2 * n + 1]
        token = refs[-1]
        starts, _ = _exchange_copies(x_refs, land_refs, send_sems, recv_sems, scatter)
        for cp in starts:
            cp.start()
        token[...] = jnp.zeros_like(token)

    res = pl.pallas_call(
        body, name=name,
        out_shape=(pltpu.SemaphoreType.DMA((7 * n,)), pltpu.SemaphoreType.DMA((7 * n,)),
                   *[pltpu.HBM(a.shape, a.dtype) for a in arrays], *[pltpu.HBM(l.shape, l.dtype) for l in lands],
                   jax.ShapeDtypeStruct((8, 128), F32)),
        in_specs=[hbm] * (2 * n),
        out_specs=(sem, sem, *[hbm] * (2 * n), pl.BlockSpec(memory_space=pltpu.VMEM)),
        input_output_aliases={i: 2 + i for i in range(2 * n)},
        compiler_params=pltpu.CompilerParams(has_side_effects=pltpu.SideEffectType.DATAFLOW_SIDE_EFFECTING),
    )(*[pltpu.with_memory_space_constraint(a, pltpu.HBM) for a in arrays],
      *[pltpu.with_memory_space_constraint(l, pltpu.HBM) for l in lands])
    return res[0], res[1], list(res[2:2 + n]), list(res[2 + n:2 + 2 * n]), res[-1]


def _exchange_wait(handles, scatter, after, name):
    send_sems, recv_sems, arrays, lands, _ = handles
    n = len(arrays)
    hbm = pl.BlockSpec(memory_space=pltpu.HBM)
    sem = pl.BlockSpec(memory_space=pltpu.SEMAPHORE)

    def body(*refs):
        x_refs, land_refs = refs[:n], refs[n:2 * n]
        send_s, recv_s = refs[2 * n], refs[2 * n + 1]
        starts, arrivals = _exchange_copies(x_refs, land_refs, send_s, recv_s, scatter)
        for cp in arrivals:
            cp.wait_recv()
        for cp in starts:
            cp.wait_send()

    res = pl.pallas_call(
        body, name=name,
        out_shape=(*[pltpu.HBM(a.shape, a.dtype) for a in arrays], *[pltpu.HBM(l.shape, l.dtype) for l in lands]),
        in_specs=[hbm] * (2 * n) + [sem, sem, pl.BlockSpec(memory_space=pl.ANY)],
        out_specs=tuple([hbm] * (2 * n)),
        input_output_aliases={i: i for i in range(2 * n)},
        compiler_params=pltpu.CompilerParams(has_side_effects=pltpu.SideEffectType.DATAFLOW_SIDE_EFFECTING),
    )(*arrays, *lands, send_sems, recv_sems, after)
    me = 4 * lax.axis_index("x") + 2 * lax.axis_index("y") + lax.axis_index("c")
    out = []
    for src, got in zip(res[:n], res[n:]):
        zeros = (0,) * (got.ndim - 1)
        own = lax.dynamic_slice(src, (me,) + zeros, (1,) + src.shape[1:]) if scatter else src[None]
        out.append(lax.dynamic_update_slice(got, own, (me,) + zeros))
    return out


def _pad_lanes(v, at=0, width=128):
    return jnp.pad(v, ((0, 0), (at, width - at - v.shape[1])))


def _pack_weights(P):
    W = {}
    w = P["w_in"]
    W["wp"] = jnp.concatenate([w[:, :2048], w[:, 2440:2696], w[:, 2056:2440], w[:, 2696:2760], w[:, 2048:2056],
                               jnp.zeros((D_MODEL, N_IN_PACKED - N_IN), w.dtype)], axis=1).astype(BF16)
    W["conv_w"] = P["gdn_conv_w"].astype(F32)
    W["alog_p"] = _pad_lanes(P["gdn_a_log"], 64)
    W["dt_p"] = _pad_lanes(P["gdn_dt_bias"], 64)
    W["gnw"] = P["gdn_norm_w"]
    W["qnw"] = P["mla_q_norm_w"]
    W["kvnw"] = P["mla_kv_norm_w"]
    uq = P["mla_w_uq"].reshape(Q_LORA, HEADS, HEAD_DIM + ROPE)
    W["wuq"] = jnp.pad(uq, ((0, 0), (0, 0), (0, 256 - HEAD_DIM - ROPE))).reshape(Q_LORA, HEADS * 256).astype(BF16)
    ukv = P["mla_w_ukv"].reshape(KV_LORA, HEADS, 2, HEAD_DIM)
    W["wukv"] = ukv.transpose(0, 2, 1, 3).reshape(KV_LORA, 2 * HEADS * HEAD_DIM).astype(BF16)
    W["qn_w"] = P["qkn_q_nope"]
    W["qr_w"] = _pad_lanes(P["qkn_q_rope"])
    W["kn_w"] = P["qkn_k_nope"]
    W["kr_w"] = _pad_lanes(P["qkn_k_rope"])
    W["onw"] = P["mla_out_norm_w"]
    W["wout"] = P["w_out"].astype(BF16)
    return W


def _unpack_grads(G):
    g = G["wp"]
    uq = G["wuq"].reshape(Q_LORA, HEADS, 256)[:, :, :HEAD_DIM + ROPE].reshape(Q_LORA, HEADS * (HEAD_DIM + ROPE))
    ukv = G["wukv"].reshape(KV_LORA, 2, HEADS, HEAD_DIM).transpose(0, 2, 1, 3).reshape(KV_LORA, 2 * HEADS * HEAD_DIM)
    return {
        "w_in": jnp.concatenate([g[:, :2048], g[:, 2752:2760], g[:, 2304:2688], g[:, 2048:2304], g[:, 2688:2752]], axis=1),
        "gdn_conv_w": G["conv_w"], "gdn_a_log": G["alog_p"][:, 64:68], "gdn_dt_bias": G["dt_p"][:, 64:68],
        "gdn_norm_w": G["gnw"], "mla_q_norm_w": G["qnw"], "mla_w_uq": uq, "mla_kv_norm_w": G["kvnw"], "mla_w_ukv": ukv,
        "qkn_q_nope": G["qn_w"], "qkn_q_rope": G["qr_w"][:, :ROPE], "qkn_k_nope": G["kn_w"], "qkn_k_rope": G["kr_w"][:, :ROPE],
        "mla_out_norm_w": G["onw"], "w_out": G["wout"],
    }


def _rope_tables(positions):
    half = ROPE // 2
    inv_freq = ROPE_BASE ** (-jnp.arange(half, dtype=F32) / half)
    ang = positions.astype(F32)[:, None] * inv_freq
    cos, sin = jnp.cos(ang), jnp.sin(ang)
    zeros = jnp.zeros((positions.shape[0], 128 - ROPE), F32)
    return jnp.concatenate([cos, cos, zeros], axis=1), jnp.concatenate([-sin, sin, zeros], axis=1)


def _mod_fn(x, scale, shift):
    return (_modulate(x, scale, shift),)


def _ffn_down_loss(hid, wo4, x, gate_w, target, name):
    S = x.shape[0]
    tb = _pick(S, MATMUL_ROWS)
    n = S // tb

    def body(hid_ref, wo_ref, x_ref, g_ref, t_ref, dx_ref, df_ref, dg_ref, l_ref, acc_ref):
        i, k = pl.program_id(0), pl.program_id(1)

        @pl.when(k == 0)
        def _():
            acc_ref[...] = jnp.zeros_like(acc_ref)

        acc_ref[...] += _dot_raw(hid_ref[...], wo_ref[...], "nn")

        @pl.when((k == 0) & (i == 0))
        def _():
            dg_ref[...] = jnp.zeros_like(dg_ref)
            l_ref[...] = jnp.zeros_like(l_ref)

        @pl.when(k == HID_PIECES - 1)
        def _():
            step = min(EPILOGUE_ROWS, tb)
            for r in range(tb // step):
                rows = slice(step * r, step * (r + 1))
                f = acc_ref[rows, :]
                diff = x_ref[rows, :] + 0.5 * g_ref[...] * f - t_ref[rows, :]
                dx = diff * (1.0 / D_MODEL)
                dx_ref[rows, :] = dx
                df_ref[rows, :] = (0.5 * g_ref[...] * dx).astype(df_ref.dtype)
                dg_ref[...] += jnp.sum(0.5 * f * dx, axis=0, keepdims=True)
                l_ref[...] += jnp.sum(diff * diff, axis=0, keepdims=True)

        @pl.when((k == HID_PIECES - 1) & (i == n - 1))
        def _():
            l_ref[...] = jnp.full(l_ref.shape, (0.5 / D_MODEL) * jnp.sum(l_ref[...]), F32)

    row = pl.BlockSpec((tb, D_MODEL), lambda i, k: (i, 0))
    par = pl.BlockSpec((1, D_MODEL), lambda i, k: (0, 0))
    return pl.pallas_call(
        body, name=name, grid=(n, HID_PIECES),
        in_specs=[pl.BlockSpec((None, tb, FFN_PIECE), lambda i, k: (k, i, 0)),
                  pl.BlockSpec((None, FFN_PIECE, D_MODEL), lambda i, k: (k, 0, 0)), row, par, row],
        out_specs=[row, row, par, par],
        out_shape=[jax.ShapeDtypeStruct((S, D_MODEL), F32), jax.ShapeDtypeStruct((S, D_MODEL), BF16),
                   jax.ShapeDtypeStruct((1, D_MODEL), F32), jax.ShapeDtypeStruct((1, D_MODEL), F32)],
        scratch_shapes=[pltpu.VMEM((tb, D_MODEL), F32)],
        compiler_params=_params(("arbitrary", "arbitrary")),
    )(hid, wo4, x, gate_w, target)


def _ffn_fwd(x, scale, shift, gate_w, w8, wo4, tag, target=None):
    S = x.shape[0]
    tm = _pick(S, (512, 256, 128))
    (h,) = _rowwise(_mod_fn, [(x, tm, D_MODEL, 0)], [scale, shift], [(tm, D_MODEL, BF16)], S // tm, tag + "_mod")
    by_gate, by_up, hid, hid_t = _ffn_up(h, w8, tag + "_up")
    if target is not None:
        dx_out, df, d_gate_w, loss_row = _ffn_down_loss(hid, wo4, x, gate_w, target, tag + "_down")
        return (dx_out, loss_row), (h, by_gate, by_up, hid_t, None, df, d_gate_w)
    tb = _pick(S, MATMUL_ROWS)
    mn = pl.BlockSpec((tb, D_MODEL), lambda i, j, k: (i, j))
    f, x_out = _mmg(hid, wo4, "nn", name=tag + "_down", grid=(S // tb, 1, HID_PIECES),
                    a_spec=pl.BlockSpec((None, tb, FFN_PIECE), lambda i, j, k: (k, i, 0)),
                    b_spec=pl.BlockSpec((None, FFN_PIECE, D_MODEL), lambda i, j, k: (k, 0, j)),
                    out_spec=mn, out_shapes=[jax.ShapeDtypeStruct((S, D_MODEL), F32)] * 2, acc_shape=(tb, D_MODEL),
                    extras=[x, gate_w], extra_specs=[mn, pl.BlockSpec((1, D_MODEL), lambda i, j, k: (0, j))],
                    epi=lambda acc, x_, g_: (acc, x_ + 0.5 * g_ * acc))
    return x_out, (h, by_gate, by_up, hid_t, f, None, None)


def _ffn_bwd(d_out, x, scale, shift, gate_w, w8, wo4, saved, tag, grad_ready, below=None):
    h, gate, up, hid_t, f, df, d_gate_w = saved
    S = x.shape[0]
    tm = _pick(S, (512, 256, 128))
    tk = _pick(S, (512, 256, 128))
    n = S // tm
    if df is None:
        (df,), (d_gate_w,) = _rowwise_bwd(lambda f_, g_: (0.5 * g_ * f_,), [(f, tm, D_MODEL, 0)], [], [gate_w],
                                          [(d_out, tm, D_MODEL, 0)], n, tag + "_dres", row_dtypes=(BF16,))
    tb = _pick(S, MATMUL_ROWS)
    piece = pl.BlockSpec((None, tb, FFN_PIECE), lambda i, j, k: (j, i, 0))
    d_gate, d_up = _mmg(df, wo4, "nt", name=tag + "_ddown", grid=(S // tb, HID_PIECES, 1),
                        a_spec=pl.BlockSpec((tb, D_MODEL), lambda i, j, k: (i, 0)),
                        b_spec=pl.BlockSpec((None, FFN_PIECE, D_MODEL), lambda i, j, k: (j, 0, 0)),
                        out_spec=piece, out_shapes=[jax.ShapeDtypeStruct((HID_PIECES, S, FFN_PIECE), BF16)] * 2,
                        acc_shape=(tb, FFN_PIECE), extras=[gate, up], extra_specs=[piece, piece], epi=_swiglu_bwd)
    tk = _pick(S, MATMUL_ROWS)
    g_wo4 = _mmg(hid_t, df, "nn", name=tag + "_gwo", grid=(HID_PIECES, 1, S // tk),
                 a_spec=pl.BlockSpec((None, FFN_PIECE, tk), lambda i, j, k: (i, 0, k)),
                 b_spec=pl.BlockSpec((tk, D_MODEL), lambda i, j, k: (k, j)),
                 out_spec=pl.BlockSpec((None, FFN_PIECE, D_MODEL), lambda i, j, k: (i, 0, j)),
                 out_shapes=[jax.ShapeDtypeStruct((HID_PIECES, FFN_PIECE, D_MODEL), BF16)], acc_shape=(FFN_PIECE, D_MODEL))
    g_w8 = _ffn_gw8(h, d_gate, d_up, tag + "_gw8", after=grad_ready("wo4", g_wo4))
    res = _ffn_dh(d_gate, d_up, w8, x, d_out, scale, shift, tag + "_dh", after=grad_ready("w8", g_w8), below=below)
    return (res[0], res[1], res[2], d_gate_w) + tuple(res[3:])


def _dproj_dmod(d_proj, wp, x, d_out, scale, shift, name, below=None):
    S, K = d_proj.shape
    tm = _pick(S, MATMUL_ROWS if below is None else MATMUL_ROWS[1:])
    tk = _pick(K, (1408, 512, 256, 128))
    nk = K // tk
    n_below = 0 if below is None else 2

    def body(dp_ref, w_ref, x_ref, do_ref, sc_ref, sh_ref, *rest):
        below_in = rest[:n_below]
        dx_ref, dsc_ref, dsh_ref = rest[n_below:n_below + 3]
        below_out, acc_ref = rest[n_below + 3:n_below + 3 + n_below], rest[-1]
        i, k = pl.program_id(0), pl.program_id(1)

        @pl.when(k == 0)
        def _():
            acc_ref[...] = jnp.zeros_like(acc_ref)

        acc_ref[...] += _dot_raw(dp_ref[...], w_ref[...], "nt")

        @pl.when((k == 0) & (i == 0))
        def _():
            for r in (dsc_ref, dsh_ref) + tuple(below_out[1:]):
                r[...] = jnp.zeros_like(r)

        @pl.when(k == nk - 1)
        def _():
            _dmod_epilogue(acc_ref, x_ref, do_ref, sc_ref, sh_ref, dx_ref, dsc_ref, dsh_ref, below, below_in, below_out)

    row = pl.BlockSpec((tm, D_MODEL), lambda i, k: (i, 0))
    par = pl.BlockSpec((1, D_MODEL), lambda i, k: (0, 0))
    row_shape, par_shape = jax.ShapeDtypeStruct((S, D_MODEL), F32), jax.ShapeDtypeStruct((1, D_MODEL), F32)
    return pl.pallas_call(
        body, name=name, grid=(S // tm, nk),
        in_specs=[pl.BlockSpec((tm, tk), lambda i, k: (i, k)), pl.BlockSpec((D_MODEL, tk), lambda i, k: (0, k)),
                  row, row, par, par] + [row, par][:n_below],
        out_specs=[row, par, par] + [row, par][:n_below],
        out_shape=[row_shape, par_shape, par_shape] + [jax.ShapeDtypeStruct((S, D_MODEL), BF16), par_shape][:n_below],
        scratch_shapes=[pltpu.VMEM((tm, D_MODEL), F32)],
        compiler_params=_params(("arbitrary", "arbitrary")),
    )(d_proj, wp, x, d_out, scale, shift, *(below[:2] if below is not None else ()))


def _mixer_fwd(x1, scale, shift, gate_w, cos_p, sin_p, W):
    S = x1.shape[0]
    tm = _pick(S, (512, 256, 128))
    tv = _pick(S, (256, 128))
    ta = _pick(S, (512, 256, 128))
    nc = S // CHUNK
    (h2,) = _rowwise(_mod_fn, [(x1, tm, D_MODEL, 0)], [scale, shift], [(tm, D_MODEL, BF16)], S // tm, "mix_mod")
    proj = _mm(h2, W["wp"], "nn", name="mix_proj")
    qkvc = _conv_fwd(proj, W["conv_w"], tv, "gdn_conv")
    kab = (proj, tv, 128, 21)
    q_a, k_a, v_a, gb = _rowwise(_gdn_pre_fn, [(qkvc, tv, 1536, 0), kab], [W["alog_p"], W["dt_p"]],
                                 [(tv, 512, F32)] * 3 + [(tv, 128, F32)], S // tv, "gdn_pre")
    ti = _pick(S, INTRA_ROWS)
    intra = _rowwise(_gdn_intra_fn, [(q_a, ti, 512, 0), (k_a, ti, 512, 0), (v_a, ti, 512, 0), (gb, ti, 128, 0)],
                     [], [(ti, 512, F32)] * 4 + [(ti, CHUNK, F32)] * 4 + [(ti // 8, 512, F32)] + [(ti, CHUNK, F32)] * 4,
                     S // ti, "gdn_intra")
    u, wk, qd, kd, qks, gl, invs = intra[0], intra[1], intra[2], intra[3], tuple(intra[4:8]), intra[8], tuple(intra[9:])
    o_a, s_prev = _gdn_scan_fwd(u, wk, qd, kd, qks, gl, "gdn_scan")
    mla_params = [W["qnw"], W["kvnw"], W["wuq"], W["wukv"], W["qn_w"], W["qr_w"], W["kn_w"], W["kr_w"]]
    def mla_pre_with_vt(*a):
        q_, k_, v_ = _mla_pre_fn(*a)
        return q_, k_, v_, jnp.transpose(v_)

    q_b, k_b, v_b, vt_b = _rowwise(mla_pre_with_vt,
                                   [(proj, tv, 256, 8), (proj, tv, 384, 6), kab, (cos_p, tv, 128, 0), (sin_p, tv, 128, 0)],
                                   mla_params, [(tv, 1024, BF16), (tv, 1024, BF16), (tv, 512, BF16), (512, tv, BF16, "across")],
                                   S // tv, "mla_pre")
    o_b, lse = _attn_fwd(q_b, k_b, vt_b, ta, "mla_attn")
    (mixed,) = _rowwise(_mix_post_fn, [(o_a, tv, 512, 0), (proj, tv, 512, 3), (o_b, tv, 512, 0)], [W["gnw"], W["onw"]],
                        [(tv, D_MODEL, BF16)], S // tv, "mix_post")
    y, x2 = _mm(mixed, W["wout"], "nn", name="mix_out", out_dtypes=(F32, F32), extras=[x1], extra_params=[gate_w],
                epi=lambda acc, x_, g_: (acc, x_ + g_ * acc))
    saved = (h2, proj, qkvc, q_a, k_a, v_a, gb, u, wk, qd, kd, qks, gl, invs, s_prev, o_a, q_b, k_b, v_b, o_b, lse, mixed, y)
    return x2, saved


def _mixer_bwd(d_out, dy, x1, scale, shift, cos_p, sin_p, W, saved, below):
    (h2, proj, qkvc, q_a, k_a, v_a, gb, u, wk, qd, kd, qks, gl, invs, s_prev, o_a, q_b, k_b, v_b, o_b, lse, mixed, y) = saved
    S = x1.shape[0]
    tm = _pick(S, (512, 256, 128))
    tv = _pick(S, (256, 128))
    ta = _pick(S, (512, 256, 128))
    nc = S // CHUNK
    G = {}
    d_mixed = _mm(dy, W["wout"], "nt", name="mix_dout")
    G["wout"] = _mm(mixed, dy, "tn", name="mix_gwout")
    (do_a, dz, do_b), (G["gnw"], G["onw"]) = _rowwise_bwd(
        _mix_post_fn, [(o_a, tv, 512, 0), (proj, tv, 512, 3), (o_b, tv, 512, 0)], [], [W["gnw"], W["onw"]],
        [(d_mixed, tv, D_MODEL, 0)], S // tv, "mix_dpost")
    stats = _attn_stats(o_b, lse, do_b, ta, "mla_stats")
    dq_b, dk_b, dv_b = _attn_bwd(q_b, k_b, v_b, do_b, stats, ta, "mla_dattn")
    kab = (proj, tv, 128, 21)
    mla_params = [W["qnw"], W["kvnw"], W["wuq"], W["wukv"], W["qn_w"], W["qr_w"], W["kn_w"], W["kr_w"]]
    (d_ckv, d_cq, d_kab), mla_grads = _rowwise_bwd(
        _mla_pre_fn, [(proj, tv, 256, 8), (proj, tv, 384, 6), kab], [(cos_p, tv, 128, 0), (sin_p, tv, 128, 0)], mla_params,
        [(dq_b, tv, 1024, 0), (dk_b, tv, 1024, 0), (dv_b, tv, 512, 0)], S // tv, "mla_dpre")
    for key, g in zip(("qnw", "kvnw", "wuq", "wukv", "qn_w", "qr_w", "kn_w", "kr_w"), mla_grads):
        G[key] = g
    scan_grads = _gdn_scan_bwd(u, wk, qd, kd, qks, gl, s_prev, do_a, "gdn_dscan")
    ti = _pick(S, INTRA_ROWS)
    intra_douts = [(scan_grads[i], ti, 512, 0) for i in range(4)] + [(scan_grads[4 + i], ti, CHUNK, 0) for i in range(4)]
    intra_douts.append((scan_grads[8], ti // 8, 512, 0))
    (dq_a, dk_a, dv_a, d_gb), _ = _rowwise_bwd(
        _gdn_intra_fn, [(q_a, ti, 512, 0), (k_a, ti, 512, 0), (v_a, ti, 512, 0), (gb, ti, 128, 0)],
        [(x_, ti, CHUNK, 0) for x_ in invs], [], intra_douts, S // ti, "gdn_dintra")
    (d_qkvc, d_kab), (G["alog_p"], G["dt_p"]) = _rowwise_bwd(
        _gdn_pre_fn, [(qkvc, tv, 1536, 0), kab], [], [W["alog_p"], W["dt_p"]],
        [(dq_a, tv, 512, 0), (dk_a, tv, 512, 0), (dv_a, tv, 512, 0), (d_gb, tv, 128, 0)], S // tv, "gdn_dpre",
        adds=[(1, d_kab)])
    d_qkv, g_conv = _conv_bwd(proj, d_qkvc, W["conv_w"], tv, "gdn_dconv")
    G["conv_w"] = g_conv[:4]
    d_proj = jnp.concatenate([d_qkv, dz, d_ckv, d_cq, d_kab], axis=1).astype(BF16)
    G["wp"] = _mm(h2, d_proj, "tn", name="mix_gwp")
    dx1, G["s2"], G["sh2"], d_below, dg_below = _dproj_dmod(d_proj, W["wp"], x1, d_out, scale, shift, "mix_dproj", below=below)
    return dx1, d_below, dg_below, G


def _local_step(x, target, mod, cos_p, sin_p, W1, mixer_weights, ffn2_weights, ffn_grad_ready, mixer_grads_ready):
    sh1, s1, g1, sh2, s2, g2, sh3, s3, g3 = [mod[:, D_MODEL * i:D_MODEL * (i + 1)] for i in range(N_MOD)]
    x1, saved1 = _ffn_fwd(x, s1, sh1, g1, W1["f1_w8"], W1["f1_wo4"], "ffn1")
    W = mixer_weights(x1)
    x2, saved2 = _mixer_fwd(x1, s2, sh2, g2, cos_p, sin_p, W)
    W.update(ffn2_weights(x2))
    (dx3, loss_row), saved3 = _ffn_fwd(x2, s3, sh3, g3, W["f2_w8"], W["f2_wo4"], "ffn2", target=target)
    dx2, d_s3, d_sh3, d_g3, dy, d_g2 = _ffn_bwd(dx3, x2, s3, sh3, g3, W["f2_w8"], W["f2_wo4"], saved3, "ffn2",
                                                ffn_grad_ready("f2"), below=(saved2[-1], g2, 1.0))
    dx1, df1, d_g1, G = _mixer_bwd(dx2, dy, x1, s2, sh2, cos_p, sin_p, W, saved2, below=(saved1[4], g1, 0.5))
    d_sh2, d_s2 = G.pop("sh2"), G.pop("s2")
    saved1 = saved1[:5] + (df1, d_g1 + mixer_grads_ready(G))
    dx, d_s1, d_sh1, d_g1 = _ffn_bwd(dx1, x, s1, sh1, g1, W1["f1_w8"], W1["f1_wo4"], saved1, "ffn1", ffn_grad_ready("f1"))
    d_mod = jnp.concatenate([d_sh1, d_s1, d_g1, d_sh2, d_s2, d_g2, d_sh3, d_s3, d_g3], axis=1)
    return loss_row, dx, d_mod


WEIGHT_NAMES = ("w_ada", "b_ada", "ffn1_w_in", "ffn1_w_out", "w_in", "gdn_conv_w", "gdn_a_log", "gdn_dt_bias", "gdn_norm_w",
                "mla_q_norm_w", "mla_w_uq", "mla_kv_norm_w", "mla_w_ukv", "qkn_q_nope", "qkn_q_rope", "qkn_k_nope",
                "qkn_k_rope", "mla_out_norm_w", "w_out", "ffn2_w_in", "ffn2_w_out")
FFN_SHARDED = ("ffn1_w_in", "ffn1_w_out", "ffn2_w_in", "ffn2_w_out")
SHEETED = (("w_in", "col"), ("gdn_conv_w", "col"), ("mla_w_uq", "col"), ("mla_w_ukv", "col"), ("w_out", "row"))
MOD_ROWS = N_MOD * D_MODEL // 128
SMALL = {"gdn_a_log": (MOD_ROWS, 1, 64, 4), "gdn_dt_bias": (MOD_ROWS + 1, 1, 64, 4), "gdn_norm_w": (MOD_ROWS + 2, 1, 0, 128),
         "mla_q_norm_w": (MOD_ROWS + 3, 3, 0, 384), "mla_kv_norm_w": (MOD_ROWS + 6, 2, 0, 256),
         "qkn_q_nope": (MOD_ROWS + 8, 1, 0, 128), "qkn_q_rope": (MOD_ROWS + 9, 1, 0, 64), "qkn_k_nope": (MOD_ROWS + 10, 1, 0, 128),
         "qkn_k_rope": (MOD_ROWS + 11, 1, 0, 64), "mla_out_norm_w": (MOD_ROWS + 12, 1, 0, 128)}
LOSS_ROW = MOD_ROWS + 13
CONV_ROW, CONV_ROWS = 88, 4 * 1536 // 128
SHEET_ROWS = CONV_ROW + CONV_ROWS


def _to_sheet(flat, dtype, sublanes):
    n = flat.shape[-1]
    unit = sublanes * 128
    pad = (-n) % unit
    flat = jnp.pad(flat.astype(dtype), [(0, 0)] * (flat.ndim - 1) + [(0, pad)])
    return flat.reshape(flat.shape[:-1] + ((n + pad) // 128, 128))


def _small_sheet(b_like, small):
    sheet = jnp.zeros((SHEET_ROWS, 128), F32).at[:MOD_ROWS].set(b_like.reshape(MOD_ROWS, 128))
    for name, (row, rows, lane, n) in SMALL.items():
        v = small[name].reshape(1, n)
        if rows == 1:
            sheet = sheet.at[row, lane:lane + n].set(v[0])
        else:
            sheet = sheet.at[row:row + rows].set(v.reshape(rows, 128))
    return sheet


def _from_small_sheet(sheet):
    out = {"b_ada": sheet[:MOD_ROWS].reshape(1, N_MOD * D_MODEL)}
    for name, (row, rows, lane, n) in SMALL.items():
        out[name] = sheet[row, lane:lane + n].reshape(1, n) if rows == 1 else sheet[row:row + rows].reshape(1, n)
    return out


def kernel(x, c, positions, w_ada, b_ada, ffn1_w_in, ffn1_w_out, w_in, gdn_conv_w, gdn_a_log, gdn_dt_bias, gdn_norm_w, mla_q_norm_w, mla_w_uq, mla_kv_norm_w, mla_w_ukv, qkn_q_nope, qkn_q_rope, qkn_k_nope, qkn_k_rope, mla_out_norm_w, w_out, ffn2_w_in, ffn2_w_out, loss_target, m_w_ada, m_b_ada, m_ffn1_w_in, m_ffn1_w_out, m_w_in, m_gdn_conv_w, m_gdn_a_log, m_gdn_dt_bias, m_gdn_norm_w, m_mla_q_norm_w, m_mla_w_uq, m_mla_kv_norm_w, m_mla_w_ukv, m_qkn_q_nope, m_qkn_q_rope, m_qkn_k_nope, m_qkn_k_rope, m_mla_out_norm_w, m_w_out, m_ffn2_w_in, m_ffn2_w_out, v_w_ada, v_b_ada, v_ffn1_w_in, v_ffn1_w_out, v_w_in, v_gdn_conv_w, v_gdn_a_log, v_gdn_dt_bias, v_gdn_norm_w, v_mla_q_norm_w, v_mla_w_uq, v_mla_kv_norm_w, v_mla_w_ukv, v_qkn_q_nope, v_qkn_q_rope, v_qkn_k_nope, v_qkn_k_rope, v_mla_out_norm_w, v_w_out, v_ffn2_w_in, v_ffn2_w_out):
    args = locals()
    w = {n: args[n] for n in WEIGHT_NAMES}
    m = {n: args["m_" + n] for n in WEIGHT_NAMES}
    v = {n: args["v_" + n] for n in WEIGHT_NAMES}
    me = 4 * lax.axis_index("x") + 2 * lax.axis_index("y") + lax.axis_index("c")
    cols = N_MOD * D_MODEL // N_DEV
    shard = {n: w[n][0] for n in FFN_SHARDED + tuple(s[0] for s in SHEETED)}

    sc = c * _sigmoid(c)
    first = _to_sheet(jnp.concatenate([sc.reshape(-1), shard["gdn_conv_w"].reshape(-1)]), F32, 8)
    (first_all,) = _all_gather([first], "gather_c")
    sc_all = first_all[:, :D_MODEL // 128].reshape(N_DEV, D_MODEL)
    n_taps = shard["gdn_conv_w"].size
    conv_all = first_all.reshape(N_DEV, -1)[:, D_MODEL:D_MODEL + n_taps].reshape(N_DEV, 4, -1)
    b_mine = lax.dynamic_slice(b_ada, (0, me * cols), (1, cols))
    mod_cols = _mm(sc_all, w_ada[0], "nn", name="ada_mod", extra_params=[b_mine], epi=lambda acc, b_: (acc + b_,))
    (mod_all,) = _all_to_all([_to_sheet(mod_cols, F32, 8)], "scatter_mod")
    mod = mod_all.reshape(N_DEV, -1)[:, :cols].reshape(1, N_MOD * D_MODEL)

    f1_shards, mod = lax.optimization_barrier(([shard["ffn1_w_in"].astype(BF16), shard["ffn1_w_out"].astype(BF16)], mod))
    f1_w8, f1_out = _all_gather(f1_shards, "gather_w1")
    travel = [s for s in SHEETED if s[0] != "gdn_conv_w"]
    tied = lax.optimization_barrier(([shard[n].astype(BF16) for n, _ in travel], f1_w8))
    f1_w8 = tied[1]
    mixer_w = _exchange_start(tied[0], False, "gather_wm_start")
    ffn2_w = _exchange_start([shard["ffn2_w_in"].astype(BF16) + mixer_w[4][0:1, 0:1].astype(BF16),
                              shard["ffn2_w_out"].astype(BF16)], False, "gather_w2_start")
    mod = mod + ffn2_w[4][0:1, 0:1]
    W1 = dict(f1_w8=f1_w8, f1_wo4=f1_out.reshape(HID_PIECES, FFN_PIECE, D_MODEL))

    def mixer_weights(after):
        got = _exchange_wait(mixer_w, False, after, "gather_wm_wait")
        P = {n: jnp.concatenate(list(g), axis=1) if kind == "col" else g.reshape(-1, g.shape[-1])
             for (n, kind), g in zip(travel, got)}
        P["gdn_conv_w"] = jnp.concatenate(list(conv_all), axis=1)
        for n in SMALL:
            P[n] = w[n]
        return _pack_weights(P)

    def ffn2_weights(after):
        f2_w8, f2_out = _exchange_wait(ffn2_w, False, after, "gather_w2_wait")
        return dict(f2_w8=f2_w8, f2_wo4=f2_out.reshape(HID_PIECES, FFN_PIECE, D_MODEL))

    pending, small_grads = {}, {}

    def ffn_grad_ready(tag):
        def ready(which, g):
            pieces = g if which == "w8" else g.reshape((N_DEV,) + shard["ffn1_w_out"].shape)
            pending[tag + which] = _exchange_start([pieces], True, "scatter_%s_%s_start" % (tag, which))
            return pending[tag + which][4]
        return ready

    def mixer_grads_ready(G):
        g_full = _unpack_grads(G)
        small_grads.update({n: g_full[n] for n in SMALL})
        small_grads["gdn_conv_w"] = g_full["gdn_conv_w"]
        pieces = []
        for n, kind in travel:
            r, cc = shard[n].shape
            g = g_full[n].astype(BF16)
            pieces.append(jnp.stack([g[:, cc * p:cc * (p + 1)] for p in range(N_DEV)]) if kind == "col"
                          else g.reshape(N_DEV, r, cc))
        pending["mixer"] = _exchange_start(pieces, True, "scatter_mx_start")
        return pending["mixer"][4][0:1, 0:1]

    cos_p, sin_p = _rope_tables(positions[0])
    loss_row, dx, d_mod = _local_step(x[0], loss_target[0], mod, cos_p, sin_p, W1, mixer_weights, ffn2_weights,
                                      ffn_grad_ready, mixer_grads_ready)

    sheet = _small_sheet(d_mod, small_grads).at[LOSS_ROW].set(loss_row[0, :128])
    sheet = sheet.at[CONV_ROW:CONV_ROW + CONV_ROWS].set(small_grads["gdn_conv_w"].reshape(CONV_ROWS, 128))
    (sheets,) = _all_gather([sheet], "gather_small")
    summed = _sum_devices(sheets, "sum_small")
    d_mod_all = sheets[:, :MOD_ROWS].reshape(N_DEV, N_MOD * D_MODEL)
    d_mod_mine = lax.dynamic_slice(d_mod_all, (0, me * cols), (N_DEV, cols))
    grads = _from_small_sheet(summed)
    grads["w_ada"] = _mm(sc_all, d_mod_mine, "tn", name="ada_gw", hi=True)
    conv_taps = shard["gdn_conv_w"].shape[1]
    grads["gdn_conv_w"] = lax.dynamic_slice(summed[CONV_ROW:CONV_ROW + CONV_ROWS].reshape(4, -1), (0, me * conv_taps),
                                            (4, conv_taps))
    loss = summed[LOSS_ROW, 0]

    delta, new_m, new_v = {}, {}, {}
    arrived = {}
    for n, key in zip(FFN_SHARDED, ("f1w8", "f1wo4", "f2w8", "f2wo4")):
        (arrived[n],) = _exchange_wait(pending[key], True, summed, "scatter_%s_wait" % key)
    arrived.update(zip([n for n, _ in travel], _exchange_wait(pending["mixer"], True, summed, "scatter_mx_wait")))
    for n, parts in arrived.items():
        grads[n], delta[n], new_m[n], new_v[n] = _sum_adamw(parts, w[n][0], m[n][0], v[n][0], "adamw_" + n)
    for n in ("w_ada", "gdn_conv_w"):
        delta[n], new_m[n], new_v[n] = _adamw(w[n][0], grads[n], m[n][0], v[n][0], "adamw_" + n)
    small_in = [_small_sheet(t["b_ada"], t) for t in (w, grads, m, v)]
    for res, out in zip(_adamw(*small_in, "adamw_small"), (delta, new_m, new_v)):
        out.update(_from_small_sheet(res))

    def shaped(d):
        return [d[n].reshape(w[n].shape) for n in WEIGHT_NAMES]

    return (loss, dx[None], *shaped(grads), *shaped(delta), *shaped(new_m), *shaped(new_v))
```

```python
import functools

import jax
import jax.numpy as jnp
import numpy as np
from jax import lax
from jax.experimental import pallas as pl
from jax.experimental.pallas import tpu as pltpu

F32 = jnp.float32
BF16 = jnp.bfloat16

D_MODEL = 1024
D_FF = 2816
N_MOD = 9
HEADS = 4
HEAD_DIM = 128
CHUNK = 64
EPS = 1e-6
ROPE = 64
Q_LORA = 384
KV_LORA = 256
N_IN = 2760
N_IN_PACKED = 2816
ROPE_BASE = 10000.0
LOG2_E = 1.4426950408889634
N_DEV = 8

ADAM_LR = 0.001
ADAM_B1 = 0.9
ADAM_B2 = 0.999
ADAM_EPS = 1e-08
ADAM_WD = 0.01
ADAM_STEP = 10

VMEM_LIMIT_BYTES = 56 * 1024 * 1024
MATMUL_ROWS = (1024, 512, 256, 128)
MESH = pl.DeviceIdType.MESH


def _params(sem=None):
    return pltpu.CompilerParams(dimension_semantics=sem, vmem_limit_bytes=VMEM_LIMIT_BYTES)


def _pick(dim, prefs):
    for p in prefs:
        if dim % p == 0:
            return p
    return dim


_DIMS = {"nn": (((1,), (0,)), ((), ())), "nt": (((1,), (1,)), ((), ())), "tn": (((0,), (0,)), ((), ()))}


def _dot_raw(a, b, mode):
    return lax.dot_general(a.astype(BF16), b.astype(BF16), _DIMS[mode], preferred_element_type=F32)


def _dot_hi(a, b, mode="nn"):
    return lax.dot_general(a, b, _DIMS[mode], precision=lax.Precision.HIGHEST, preferred_element_type=F32)


@functools.partial(jax.custom_vjp, nondiff_argnums=(2,))
def _bdot(a, b, mode):
    return _dot_raw(a, b, mode)


def _bdot_fwd(a, b, mode):
    return _dot_raw(a, b, mode), (a, b)


def _bdot_bwd(mode, res, g):
    a, b = res
    if mode == "nn":
        return _dot_raw(g, b, "nt"), _dot_raw(a, g, "tn")
    if mode == "nt":
        return _dot_raw(g, b, "nn"), _dot_raw(g, a, "tn")
    return _dot_raw(b, g, "nt"), _dot_raw(a, g, "nn")


_bdot.defvjp(_bdot_fwd, _bdot_bwd)


def _mm(a, b, mode, *, name, out_dtypes=(F32,), epi=None, extras=(), extra_params=(), hi=False,
        tm=None, tn=None, tk=None):
    if mode == "nn":
        (M, K), (_, N) = a.shape, b.shape
    elif mode == "nt":
        (M, K), (N, _) = a.shape, b.shape
    else:
        (K, M), (_, N) = a.shape, b.shape
    tm = tm or _pick(M, (512, 1408, 256, 128) if mode == "tn" else MATMUL_ROWS + (384, 352))
    tn = tn or _pick(N, (1024, 1408, 768, 512, 384, 256, 128))
    tk = tk or _pick(K, (1024, 1408, 512, 384, 256, 128))
    a_spec = {"nn": pl.BlockSpec((tm, tk), lambda i, j, k: (i, k)), "nt": pl.BlockSpec((tm, tk), lambda i, j, k: (i, k)),
              "tn": pl.BlockSpec((tk, tm), lambda i, j, k: (k, i))}[mode]
    b_spec = {"nn": pl.BlockSpec((tk, tn), lambda i, j, k: (k, j)), "nt": pl.BlockSpec((tn, tk), lambda i, j, k: (j, k)),
              "tn": pl.BlockSpec((tk, tn), lambda i, j, k: (k, j))}[mode]
    mn_spec = pl.BlockSpec((tm, tn), lambda i, j, k: (i, j))
    return _mmg(a, b, mode, name=name, grid=(M // tm, N // tn, K // tk), a_spec=a_spec, b_spec=b_spec, out_spec=mn_spec,
                out_shapes=[jax.ShapeDtypeStruct((M, N), dt) for dt in out_dtypes], acc_shape=(tm, tn), epi=epi,
                extras=list(extras) + list(extra_params),
                extra_specs=[mn_spec] * len(extras) + [pl.BlockSpec((1, tn), lambda i, j, k: (0, j))] * len(extra_params),
                hi=hi)


def _mmg(a, b, mode, *, name, grid, a_spec, b_spec, out_spec, out_shapes, acc_shape, epi=None, extras=(),
         extra_specs=(), hi=False):
    nk = grid[2]
    n_e, n_o = len(extras), len(out_shapes)

    def body(*refs):
        a_ref, b_ref = refs[:2]
        e_refs = refs[2:2 + n_e]
        o_refs = refs[2 + n_e:2 + n_e + n_o]
        acc_ref = refs[-1]
        k = pl.program_id(2)

        @pl.when(k == 0)
        def _():
            acc_ref[...] = jnp.zeros_like(acc_ref)

        if hi:
            acc_ref[...] += _dot_hi(a_ref[...].astype(F32), b_ref[...].astype(F32), mode)
        else:
            acc_ref[...] += _dot_raw(a_ref[...], b_ref[...], mode)

        @pl.when(k == nk - 1)
        def _():
            acc = acc_ref[...]
            outs = (acc,) if epi is None else epi(acc, *[e[...].astype(F32) for e in e_refs])
            for o_ref, o in zip(o_refs, outs):
                o_ref[...] = o.astype(o_ref.dtype)

    outs = pl.pallas_call(
        body, name=name, grid=grid,
        in_specs=[a_spec, b_spec] + list(extra_specs),
        out_specs=[out_spec] * n_o,
        out_shape=list(out_shapes),
        scratch_shapes=[pltpu.VMEM(acc_shape, F32)],
        compiler_params=_params(("parallel", "parallel", "arbitrary")),
    )(a, b, *extras)
    return outs if n_o > 1 else outs[0]


def _row_spec(th, cw, ci):
    return pl.BlockSpec((th, cw), lambda i: (i, ci))


def _full_spec(shape):
    return pl.BlockSpec(shape, lambda i: (0,) * len(shape))


def _rowwise(fn, rows, params, outs, n_steps, name):
    n_r, n_p, n_o = len(rows), len(params), len(outs)

    def body(*refs):
        vals = [r[...].astype(F32) for r in refs[:n_r + n_p]]
        res = fn(*vals)
        for o_ref, o in zip(refs[n_r + n_p:], res):
            o_ref[...] = o.astype(o_ref.dtype)

    across = [len(o) == 4 for o in outs]
    res = pl.pallas_call(
        body, name=name, grid=(n_steps,),
        in_specs=[_row_spec(th, cw, ci) for (_, th, cw, ci) in rows] + [_full_spec(p.shape) for p in params],
        out_specs=[pl.BlockSpec((o[0], o[1]), lambda i: (0, i)) if ac else _row_spec(o[0], o[1], 0)
                   for o, ac in zip(outs, across)],
        out_shape=[jax.ShapeDtypeStruct((o[0], n_steps * o[1]) if ac else (n_steps * o[0], o[1]), o[2])
                   for o, ac in zip(outs, across)],
        compiler_params=_params(("parallel",)),
    )(*[r[0] for r in rows], *params)
    return res


def _rowwise_bwd(fn, rows, aux, params, douts, n_steps, name, row_dtypes=None, adds=()):
    n_r, n_a, n_p, n_d, n_add = len(rows), len(aux), len(params), len(douts), len(adds)
    row_dtypes = row_dtypes or (F32,) * n_r

    def body(*refs):
        it = iter(refs)
        r_vals = [next(it)[...].astype(F32) for _ in range(n_r)]
        a_vals = [next(it)[...].astype(F32) for _ in range(n_a)]
        p_vals = [next(it)[...].astype(F32) for _ in range(n_p)]
        d_vals = [next(it)[...].astype(F32) for _ in range(n_d)]
        add_vals = [next(it)[...].astype(F32) for _ in range(n_add)]
        dr_refs = [next(it) for _ in range(n_r)]
        dp_refs = [next(it) for _ in range(n_p)]

        def f(*rp):
            return tuple(fn(*rp[:n_r], *a_vals, *rp[n_r:]))

        _, vjp = jax.vjp(f, *r_vals, *p_vals)
        grads = list(vjp(tuple(d_vals)))
        for (ri, _), av in zip(adds, add_vals):
            grads[ri] = grads[ri] + av
        for dr_ref, g in zip(dr_refs, grads[:n_r]):
            dr_ref[...] = g.astype(dr_ref.dtype)

        @pl.when(pl.program_id(0) == 0)
        def _():
            for dp_ref in dp_refs:
                dp_ref[...] = jnp.zeros_like(dp_ref)

        for dp_ref, g in zip(dp_refs, grads[n_r:]):
            dp_ref[...] += g

    all_rows = list(rows) + list(aux) + list(douts) + [(arr,) + tuple(rows[ri][1:3]) + (0,) for ri, arr in adds]
    in_specs = ([_row_spec(th, cw, ci) for (_, th, cw, ci) in list(rows) + list(aux)]
                + [_full_spec(p.shape) for p in params]
                + [_row_spec(th, cw, ci) for (_, th, cw, ci) in all_rows[n_r + n_a:]])
    res = pl.pallas_call(
        body, name=name, grid=(n_steps,),
        in_specs=in_specs,
        out_specs=[_row_spec(th, cw, 0) for (_, th, cw, _) in rows] + [_full_spec(p.shape) for p in params],
        out_shape=[jax.ShapeDtypeStruct((n_steps * th, cw), dt) for (_, th, cw, _), dt in zip(rows, row_dtypes)]
        + [jax.ShapeDtypeStruct(p.shape, F32) for p in params],
        compiler_params=_params(("arbitrary",)),
    )(*[r[0] for r in list(rows) + list(aux)], *params, *[r[0] for r in all_rows[n_r + n_a:]])
    return res[:n_r], res[n_r:]


def _sigmoid(x):
    return lax.logistic(x)


def _silu(x):
    return x * _sigmoid(x)


def _rms(x, w=None, n=None):
    n = n or x.shape[-1]
    y = x * lax.rsqrt(jnp.sum(x * x, axis=-1, keepdims=True) * (1.0 / n) + EPS)
    return y if w is None else y * w


def _modulate(x, scale, shift):
    return _rms(x) * (1.0 + scale) + shift


def _softplus(x):
    return jnp.maximum(x, 0.0) + jnp.log1p(jnp.exp(-jnp.abs(x)))


@jax.custom_vjp
def _rot_half64(x):
    lane = lax.broadcasted_iota(jnp.int32, x.shape, 1)
    up = pltpu.roll(x, 96, 1)
    down = pltpu.roll(x, 32, 1)
    return jnp.where(lane < 32, up, jnp.where(lane < 64, down, 0.0))


_rot_half64.defvjp(lambda x: (_rot_half64(x), None), lambda _, g: (_rot_half64(g),))


def _rope128(x, cos_p, sin_p):
    return x * cos_p + _rot_half64(x) * sin_p


def _gdn_pre_fn(qkvc, kab, alog_p, dt_p):
    a = _silu(qkvc)
    qs, ks = [], []
    for h in range(HEADS):
        qh = a[:, HEAD_DIM * h:HEAD_DIM * (h + 1)]
        kh = a[:, 512 + HEAD_DIM * h:512 + HEAD_DIM * (h + 1)]
        qs.append(qh * lax.rsqrt(jnp.sum(qh * qh, axis=-1, keepdims=True) + EPS) * (HEAD_DIM ** -0.5))
        ks.append(kh * lax.rsqrt(jnp.sum(kh * kh, axis=-1, keepdims=True) + EPS))
    lane = lax.broadcasted_iota(jnp.int32, kab.shape, 1)
    g_full = -jnp.exp(alog_p) * _softplus(kab + dt_p)
    b_full = _sigmoid(kab)
    gb = jnp.where((lane >= 64) & (lane < 68), g_full, jnp.where((lane >= 68) & (lane < 72), b_full, 0.0))
    return jnp.concatenate(qs, axis=1), jnp.concatenate(ks, axis=1), a[:, 1024:1536], gb


INTRA_ROWS = (256, 128, 64)

_BNN = (((2,), (1,)), ((0,), (0,)))
_BNT = (((2,), (2,)), ((0,), (0,)))


def _split_bf16(a):
    hi = a.astype(BF16)
    return hi, (a - hi.astype(F32)).astype(BF16)


def _dot3_raw(a, b, dims):
    a_hi, a_lo = _split_bf16(a)
    b_hi, b_lo = _split_bf16(b)
    dot = lambda x_, y_: lax.dot_general(x_, y_, dims, preferred_element_type=F32)
    return dot(a_hi, b_hi) + (dot(a_hi, b_lo) + dot(a_lo, b_hi))


@functools.partial(jax.custom_vjp, nondiff_argnums=(2, 3))
def _dot3(a, b, nt, exact_bwd=True):
    return _dot3_raw(a, b, _BNT if nt else _BNN)


def _dot3_fwd(a, b, nt, exact_bwd):
    return _dot3_raw(a, b, _BNT if nt else _BNN), (a, b)


def _dot3_bwd(nt, exact_bwd, res, g):
    a, b = res
    if exact_bwd:
        dot = _dot3_raw
    else:
        dot = lambda x_, y_, d_: lax.dot_general(x_.astype(BF16), y_.astype(BF16), d_, preferred_element_type=F32)
    if nt:
        return dot(g, b, _BNN), dot(jnp.swapaxes(g, 1, 2), a, _BNN)
    return dot(g, b, _BNT), dot(jnp.swapaxes(a, 1, 2), g, _BNN)


_dot3.defvjp(_dot3_fwd, _dot3_bwd)


@functools.partial(jax.custom_vjp, nondiff_argnums=(2,))
def _bdot_b(a, b, nt):
    return lax.dot_general(a.astype(BF16), b.astype(BF16), _BNT if nt else _BNN, preferred_element_type=F32)


def _bdot_b_fwd(a, b, nt):
    return _bdot_b(a, b, nt), (a, b)


def _bdot_b_bwd(nt, res, g):
    a, b = res
    dot = lambda x_, y_, d_: lax.dot_general(x_.astype(BF16), y_.astype(BF16), d_, preferred_element_type=F32)
    if nt:
        return dot(g, b, _BNN), dot(jnp.swapaxes(g, 1, 2), a, _BNN)
    return dot(g, b, _BNT), dot(jnp.swapaxes(a, 1, 2), g, _BNN)


_bdot_b.defvjp(_bdot_b_fwd, _bdot_b_bwd)


@jax.custom_vjp
def _inverse_given(a_mat, inv):
    return inv


def _inverse_given_bwd(inv, g):
    inv_t = jnp.swapaxes(inv, 1, 2)
    return -_dot3_raw(_dot3_raw(inv_t, g, _BNN), inv_t, _BNN), jnp.zeros_like(inv)


_inverse_given.defvjp(lambda a_mat, inv: (inv, inv), _inverse_given_bwd)


def _intra_batched(q, k, v, g_col, b_col, inv_known=None):
    c = CHUNK
    nb = q.shape[0]
    row = lax.broadcasted_iota(jnp.int32, (1, c, c), 1)
    col = lax.broadcasted_iota(jnp.int32, (1, c, c), 2)
    incl, strict, eye = row >= col, row > col, row == col
    tri = jnp.broadcast_to(jnp.where(incl, 1.0, 0.0).astype(F32), (nb, c, c))
    ident = jnp.where(eye, 1.0, 0.0).astype(F32)
    g_wide = _dot3(tri, jnp.broadcast_to(g_col, (nb, c, HEAD_DIM)), False)
    g_i = g_wide[:, :, :c]
    g_j = jnp.sum(jnp.where(eye, g_i, 0.0), axis=1, keepdims=True)
    decay = jnp.where(incl, jnp.exp(jnp.where(incl, g_i - g_j, 0.0)), 0.0)
    kk = _bdot_b(k, k, True)
    a_mat = jnp.where(strict, b_col * kk * decay, 0.0)
    if inv_known is None:
        x_pow = -a_mat
        inv = ident + x_pow
        for _ in range(5):
            x_pow = _dot3(x_pow, x_pow, False, False)
            inv = inv + _dot3(inv, x_pow, False, False)
    else:
        inv = _inverse_given(a_mat, inv_known)
    e_wide = jnp.exp(g_wide)
    u = _dot3(inv, v * b_col, False)
    wk = _dot3(inv, k * b_col * e_wide, False)
    qk = _bdot_b(q, k, True) * decay
    last = lax.broadcasted_iota(jnp.int32, (1, c, HEAD_DIM), 1) == c - 1
    g_last = jnp.sum(jnp.where(last, g_wide, 0.0), axis=1, keepdims=True)
    qd = q * e_wide
    kd = k * jnp.exp(g_last - g_wide)
    gl = jnp.broadcast_to(jnp.exp(g_last), (nb, 8, HEAD_DIM))
    return u, wk, qd, kd, qk, gl, inv


def _gdn_intra_fn(q, k, v, gb, *inv_known):
    t = q.shape[0]
    nch = t // CHUNK
    lane = lax.broadcasted_iota(jnp.int32, gb.shape, 1)

    def heads_first(x_):
        return jnp.concatenate([x_[:, HEAD_DIM * h:HEAD_DIM * (h + 1)].reshape(nch, CHUNK, HEAD_DIM) for h in range(HEADS)],
                               axis=0)

    def column(first_lane):
        return jnp.concatenate([jnp.sum(jnp.where(lane == first_lane + h, gb, 0.0), axis=1, keepdims=True)
                                .reshape(nch, CHUNK, 1) for h in range(HEADS)], axis=0)

    known = jnp.concatenate([x_.reshape(nch, CHUNK, CHUNK) for x_ in inv_known], axis=0) if inv_known else None
    u, wk, qd, kd, qk, gl, inv = _intra_batched(heads_first(q), heads_first(k), heads_first(v), column(64), column(68), known)

    def rows_first(x_):
        r, w_ = x_.shape[1], x_.shape[2]
        return jnp.concatenate([x_[nch * h:nch * (h + 1)].reshape(nch * r, w_) for h in range(HEADS)], axis=1)

    per_head = lambda x_: [x_[nch * h:nch * (h + 1)].reshape(t, CHUNK) for h in range(HEADS)]
    outs = (rows_first(u), rows_first(wk), rows_first(qd), rows_first(kd), *per_head(qk), rows_first(gl))
    return outs if inv_known else outs + tuple(per_head(inv))


def _scan_step(s0, u, wk, qd, kd, qk, gl):
    v_new = u - _bdot_b(wk, s0, False)
    o = _bdot_b(qd, s0, False) + _bdot_b(qk, v_new, False)
    s1 = s0 * gl[:, 0:1, :] + _bdot_b(jnp.swapaxes(kd, 1, 2), v_new, False)
    return o, s1


def _mix_post_fn(o_a, z, o_b, gnw, onw):
    parts = [_rms(o_a[:, HEAD_DIM * h:HEAD_DIM * (h + 1)], gnw) * _silu(z[:, HEAD_DIM * h:HEAD_DIM * (h + 1)])
             for h in range(HEADS)]
    parts += [_rms(o_b[:, HEAD_DIM * h:HEAD_DIM * (h + 1)], onw) for h in range(HEADS)]
    return (jnp.concatenate(parts, axis=1),)


def _mla_pre_fn(ckv, cq, kab, cos_p, sin_p, qnw, kvnw, wuq, wukv, qn_w, qr_w, kn_w, kr_w):
    scale = (HEAD_DIM + ROPE) ** -0.5 * LOG2_E
    qf = _bdot(_rms(cq, qnw), wuq, "nn")
    kvf = _bdot(_rms(ckv, kvnw), wukv, "nn")
    lane = lax.broadcasted_iota(jnp.int32, kab.shape, 1)
    kr = _rope128(_rms(jnp.where(lane < ROPE, kab, 0.0), kr_w, n=ROPE), cos_p, sin_p)
    qs, ks = [], []
    for h in range(HEADS):
        qn = _rms(qf[:, 256 * h:256 * h + 128], qn_w) * scale
        qr = _rope128(_rms(qf[:, 256 * h + 128:256 * h + 256], qr_w, n=ROPE), cos_p, sin_p) * scale
        qs += [qn, qr]
        ks += [_rms(kvf[:, 128 * h:128 * (h + 1)], kn_w), kr]
    return jnp.concatenate(qs, axis=1), jnp.concatenate(ks, axis=1), kvf[:, 512:]


def _conv_fwd(proj, conv_w, tm, name):
    S = proj.shape[0]
    C = 1536
    nb = tm // 8

    def body(x_ref, prev_ref, w_ref, o_ref, ext_ref):
        i = pl.program_id(0)
        ext_ref[0:8, :] = jnp.where(i > 0, prev_ref[...], 0.0)
        ext_ref[8:, :] = x_ref[...]
        acc = jnp.zeros((tm, C), F32)
        for k in range(4):
            acc = acc + w_ref[k:k + 1, :] * ext_ref[pl.ds(5 + k, tm), :]
        o_ref[...] = acc

    return pl.pallas_call(
        body, name=name, grid=(S // tm,),
        in_specs=[pl.BlockSpec((tm, C), lambda i: (i, 0)),
                  pl.BlockSpec((8, C), lambda i: (jnp.maximum(i * nb - 1, 0), 0)),
                  pl.BlockSpec((4, C), lambda i: (0, 0))],
        out_specs=pl.BlockSpec((tm, C), lambda i: (i, 0)),
        out_shape=jax.ShapeDtypeStruct((S, C), F32),
        scratch_shapes=[pltpu.VMEM((tm + 8, C), F32)],
        compiler_params=_params(("arbitrary",)),
    )(proj, proj, conv_w)


def _conv_bwd(proj, dout, conv_w, tm, name):
    S = proj.shape[0]
    C = 1536
    nb = tm // 8
    n_steps = S // tm

    def body(x_ref, prev_ref, d_ref, next_ref, w_ref, dx_ref, dw_ref, xext_ref, dext_ref):
        i = pl.program_id(0)
        xext_ref[0:8, :] = jnp.where(i > 0, prev_ref[...], 0.0)
        xext_ref[8:, :] = x_ref[...]
        dext_ref[0:tm, :] = d_ref[...]
        dext_ref[tm:, :] = jnp.where(i < n_steps - 1, next_ref[...], 0.0)
        d = d_ref[...]
        acc = jnp.zeros((tm, C), F32)
        dws = []
        for k in range(4):
            acc = acc + w_ref[k:k + 1, :] * dext_ref[pl.ds(3 - k, tm), :]
            dws.append(jnp.sum(d * xext_ref[pl.ds(5 + k, tm), :], axis=0, keepdims=True))
        dx_ref[...] = acc

        @pl.when(i == 0)
        def _():
            dw_ref[...] = jnp.zeros_like(dw_ref)

        dw_ref[...] += jnp.concatenate(dws + [jnp.zeros((4, C), F32)], axis=0)

    return pl.pallas_call(
        body, name=name, grid=(n_steps,),
        in_specs=[pl.BlockSpec((tm, C), lambda i: (i, 0)),
                  pl.BlockSpec((8, C), lambda i: (jnp.maximum(i * nb - 1, 0), 0)),
                  pl.BlockSpec((tm, C), lambda i: (i, 0)),
                  pl.BlockSpec((8, C), lambda i: (jnp.minimum((i + 1) * nb, S // 8 - 1), 0)),
                  pl.BlockSpec((4, C), lambda i: (0, 0))],
        out_specs=[pl.BlockSpec((tm, C), lambda i: (i, 0)), pl.BlockSpec((8, C), lambda i: (0, 0))],
        out_shape=[jax.ShapeDtypeStruct((S, C), F32), jax.ShapeDtypeStruct((8, C), F32)],
        scratch_shapes=[pltpu.VMEM((tm + 8, C), F32), pltpu.VMEM((tm + 8, C), F32)],
        compiler_params=_params(("arbitrary",)),
    )(proj, proj, dout, dout, conv_w)


SCAN_CHUNKS = (4, 2, 1)


def _gdn_scan_fwd(u, wk, qd, kd, qks, gl, name):
    S = u.shape[0]
    nc = S // CHUNK
    cs = _pick(nc, SCAN_CHUNKS)
    W = HEADS * HEAD_DIM

    def body(u_ref, wk_ref, qd_ref, kd_ref, qk0, qk1, qk2, qk3, gl_ref, o_ref, sp_ref, s_ref):
        @pl.when(pl.program_id(0) == 0)
        def _():
            s_ref[...] = jnp.zeros_like(s_ref)

        state = s_ref[...]
        for c in range(cs):
            rows, gl_rows = slice(CHUNK * c, CHUNK * (c + 1)), slice(8 * c, 8 * (c + 1))
            sp_ref[c] = state
            o, state = _scan_step(state, _heads(u_ref, HEAD_DIM, rows), _heads(wk_ref, HEAD_DIM, rows),
                                  _heads(qd_ref, HEAD_DIM, rows), _heads(kd_ref, HEAD_DIM, rows),
                                  jnp.stack([r[rows, :] for r in (qk0, qk1, qk2, qk3)]), _heads(gl_ref, HEAD_DIM, gl_rows))
            for h in range(HEADS):
                o_ref[rows, HEAD_DIM * h:HEAD_DIM * (h + 1)] = o[h]
        s_ref[...] = state

    row = pl.BlockSpec((cs * CHUNK, W), lambda n: (n, 0))
    qk_spec = pl.BlockSpec((cs * CHUNK, CHUNK), lambda n: (n, 0))
    return pl.pallas_call(
        body, name=name, grid=(nc // cs,),
        in_specs=[row, row, row, row, qk_spec, qk_spec, qk_spec, qk_spec, pl.BlockSpec((cs * 8, W), lambda n: (n, 0))],
        out_specs=[row, pl.BlockSpec((cs, HEADS, HEAD_DIM, HEAD_DIM), lambda n: (n, 0, 0, 0))],
        out_shape=[jax.ShapeDtypeStruct((S, W), F32), jax.ShapeDtypeStruct((nc, HEADS, HEAD_DIM, HEAD_DIM), F32)],
        scratch_shapes=[pltpu.VMEM((HEADS, HEAD_DIM, HEAD_DIM), F32)],
        compiler_params=_params(("arbitrary",)),
    )(u, wk, qd, kd, *qks, gl)


def _gdn_scan_bwd(u, wk, qd, kd, qks, gl, s_prev, d_o, name):
    S = u.shape[0]
    nc = S // CHUNK
    cs = _pick(nc, SCAN_CHUNKS)
    nb = nc // cs
    W = HEADS * HEAD_DIM

    def body(u_ref, wk_ref, qd_ref, kd_ref, qk0, qk1, qk2, qk3, gl_ref, sp_ref, do_ref,
             du_ref, dwk_ref, dqd_ref, dkd_ref, dqk0, dqk1, dqk2, dqk3, dgl_ref, ds_ref):
        @pl.when(pl.program_id(0) == 0)
        def _():
            ds_ref[...] = jnp.zeros_like(ds_ref)

        d_state = ds_ref[...]
        for c in reversed(range(cs)):
            rows, gl_rows = slice(CHUNK * c, CHUNK * (c + 1)), slice(8 * c, 8 * (c + 1))
            _, vjp = jax.vjp(_scan_step, sp_ref[c], _heads(u_ref, HEAD_DIM, rows), _heads(wk_ref, HEAD_DIM, rows),
                             _heads(qd_ref, HEAD_DIM, rows), _heads(kd_ref, HEAD_DIM, rows),
                             jnp.stack([r[rows, :] for r in (qk0, qk1, qk2, qk3)]), _heads(gl_ref, HEAD_DIM, gl_rows))
            d_state, du, dwk, dqd, dkd, dqk, dgl = vjp((_heads(do_ref, HEAD_DIM, rows), d_state))
            for h, dqk_ref in enumerate((dqk0, dqk1, dqk2, dqk3)):
                sl = slice(HEAD_DIM * h, HEAD_DIM * (h + 1))
                du_ref[rows, sl] = du[h]
                dwk_ref[rows, sl] = dwk[h]
                dqd_ref[rows, sl] = dqd[h]
                dkd_ref[rows, sl] = dkd[h]
                dqk_ref[rows, :] = dqk[h]
                dgl_ref[gl_rows, sl] = dgl[h]
        ds_ref[...] = d_state

    rev = lambda n: (nb - 1 - n, 0)
    row = pl.BlockSpec((cs * CHUNK, W), rev)
    qk_spec = pl.BlockSpec((cs * CHUNK, CHUNK), rev)
    gl_spec = pl.BlockSpec((cs * 8, W), rev)
    qk_shape = jax.ShapeDtypeStruct((S, CHUNK), F32)
    row_shape = jax.ShapeDtypeStruct((S, W), F32)
    return pl.pallas_call(
        body, name=name, grid=(nb,),
        in_specs=[row, row, row, row, qk_spec, qk_spec, qk_spec, qk_spec, gl_spec,
                  pl.BlockSpec((cs, HEADS, HEAD_DIM, HEAD_DIM), lambda n: (nb - 1 - n, 0, 0, 0)), row],
        out_specs=[row, row, row, row, qk_spec, qk_spec, qk_spec, qk_spec, gl_spec],
        out_shape=[row_shape] * 4 + [qk_shape] * 4 + [jax.ShapeDtypeStruct((nc * 8, W), F32)],
        scratch_shapes=[pltpu.VMEM((HEADS, HEAD_DIM, HEAD_DIM), F32)],
        compiler_params=_params(("arbitrary",)),
    )(u, wk, qd, kd, *qks, gl, s_prev, d_o)


NEG = -1e30


def _chunk_mask(i, j, t, transposed=False):
    q_axis, k_axis = (1, 0) if transposed else (0, 1)
    r = (i * t + lax.broadcasted_iota(jnp.int32, (t, t), q_axis)) // CHUNK
    c = (j * t + lax.broadcasted_iota(jnp.int32, (t, t), k_axis)) // CHUNK
    return c <= r


def _tile_pairs(n, by_key):
    pairs = [(i, j) for j in range(n) for i in range(j, n)] if by_key else [(i, j) for i in range(n) for j in range(i + 1)]
    return jnp.asarray(np.array([p[0] for p in pairs], np.int32)), jnp.asarray(np.array([p[1] for p in pairs], np.int32))


def _heads(ref, width, rows=slice(None)):
    return jnp.stack([ref[rows, width * h:width * (h + 1)] for h in range(HEADS)])


def _bmm(a, b, dims):
    return lax.dot_general(a.astype(BF16), b.astype(BF16), dims, preferred_element_type=F32)


def _attn_fwd(q, k, v_t, t, name):
    S = q.shape[0]
    n = S // t
    qi, kj = _tile_pairs(n, by_key=False)

    def body(qi_ref, kj_ref, q_ref, k_ref, vt_ref, o_ref, lse_ref, m_ref, l_ref, acc_ref):
        i, j = qi_ref[pl.program_id(0)], kj_ref[pl.program_id(0)]

        @pl.when(j == 0)
        def _():
            m_ref[...] = jnp.full_like(m_ref, NEG)
            l_ref[...] = jnp.zeros_like(l_ref)
            acc_ref[...] = jnp.zeros_like(acc_ref)

        def update(masked):
            s_t = _bmm(_heads(k_ref, 256), _heads(q_ref, 256), _BNT)
            if masked:
                s_t = jnp.where(_chunk_mask(i, j, t, transposed=True)[None], s_t, NEG)
            m_old = m_ref[...]
            m_new = jnp.maximum(m_old, jnp.max(s_t, axis=1, keepdims=True))
            p_t = jnp.exp2(s_t - m_new)
            alpha = jnp.exp2(m_old - m_new)
            l_ref[...] = alpha * l_ref[...] + jnp.sum(p_t, axis=1, keepdims=True)
            v_heads = jnp.stack([vt_ref[HEAD_DIM * h:HEAD_DIM * (h + 1), :] for h in range(HEADS)])
            acc_ref[...] = alpha * acc_ref[...] + _bmm(v_heads, p_t, _BNN)
            m_ref[...] = m_new

        @pl.when(j < i)
        def _():
            update(False)

        @pl.when(j == i)
        def _():
            update(True)
            for h in range(HEADS):
                sl = slice(HEAD_DIM * h, HEAD_DIM * (h + 1))
                o_ref[:, sl] = jnp.transpose(acc_ref[h] / l_ref[h])
                lse_ref[:, sl] = jnp.transpose(jnp.broadcast_to(m_ref[h] + jnp.log(l_ref[h]) * LOG2_E, (HEAD_DIM, t)))

    row = lambda p, qi_, kj_: (qi_[p], 0)
    return pl.pallas_call(
        body, name=name,
        grid_spec=pltpu.PrefetchScalarGridSpec(
            num_scalar_prefetch=2, grid=(qi.shape[0],),
            in_specs=[pl.BlockSpec((t, HEADS * 256), row), pl.BlockSpec((t, HEADS * 256), lambda p, qi_, kj_: (kj_[p], 0)),
                      pl.BlockSpec((HEADS * HEAD_DIM, t), lambda p, qi_, kj_: (0, kj_[p]))],
            out_specs=[pl.BlockSpec((t, HEADS * HEAD_DIM), row)] * 2,
            scratch_shapes=[pltpu.VMEM((HEADS, 1, t), F32), pltpu.VMEM((HEADS, 1, t), F32),
                            pltpu.VMEM((HEADS, HEAD_DIM, t), F32)]),
        out_shape=[jax.ShapeDtypeStruct((S, HEADS * HEAD_DIM), F32)] * 2,
        compiler_params=_params(("arbitrary",)),
    )(qi, kj, q, k, v_t)


def _attn_stats(o, lse, d_o, t, name):
    S = o.shape[0]

    def body(o_ref, lse_ref, do_ref, st_ref):
        lane = lax.broadcasted_iota(jnp.int32, (t, HEAD_DIM), 1)
        stats = jnp.zeros((t, HEAD_DIM), F32)
        for h in range(HEADS):
            sl = slice(HEAD_DIM * h, HEAD_DIM * (h + 1))
            delta = jnp.sum(do_ref[:, sl] * o_ref[:, sl], axis=1, keepdims=True)
            stats = stats + jnp.where(lane == h, lse_ref[:, sl], 0.0) + jnp.where(lane == HEADS + h, delta, 0.0)
        st_ref[...] = jnp.transpose(stats)[0:8, :]

    row = pl.BlockSpec((t, HEADS * HEAD_DIM), lambda i: (i, 0))
    return pl.pallas_call(
        body, name=name, grid=(S // t,),
        in_specs=[row, row, row], out_specs=pl.BlockSpec((8, t), lambda i: (0, i)),
        out_shape=jax.ShapeDtypeStruct((8, S), F32),
        compiler_params=_params(("parallel",)),
    )(o, lse, d_o)


BWD_GROUP = 2


def _attn_bwd(q, k, v, d_o, stats, t, name):
    S = q.shape[0]
    n = S // t
    groups = HEADS // BWD_GROUP
    gq, gv = BWD_GROUP * 256, BWD_GROUP * HEAD_DIM
    qi, kj = _tile_pairs(n, by_key=True)
    n_pairs = qi.shape[0]
    st = stats.reshape(2, groups, BWD_GROUP, S).transpose(1, 0, 2, 3).reshape(groups, 2 * BWD_GROUP, S)
    st = jnp.pad(st, ((0, 0), (0, 8 - 2 * BWD_GROUP), (0, 0)))

    def heads(ref, width, rows=slice(None)):
        return jnp.stack([ref[rows, width * h:width * (h + 1)] for h in range(BWD_GROUP)])

    def body(qi_ref, kj_ref, q_ref, k_ref, v_ref, do_ref, st_ref, dq_hbm, dk_ref, dv_ref, dq_acc, sem):
        g, p = pl.program_id(0), pl.program_id(1)
        i, j = qi_ref[p], kj_ref[p]

        @pl.when(i == j)
        def _():
            dk_ref[...] = jnp.zeros_like(dk_ref)
            dv_ref[...] = jnp.zeros_like(dv_ref)

        def update(masked):
            qh, kh = heads(q_ref, 256), heads(k_ref, 256)
            d_out = heads(do_ref, HEAD_DIM)
            stv = st_ref[...]
            lse_row = jnp.stack([stv[h:h + 1, :] for h in range(BWD_GROUP)])
            delta_row = jnp.stack([stv[BWD_GROUP + h:BWD_GROUP + h + 1, :] for h in range(BWD_GROUP)])
            s_t = _bmm(kh, qh, _BNT)
            p_t = jnp.exp2(s_t - lse_row)
            if masked:
                p_t = jnp.where(_chunk_mask(i, j, t, transposed=True)[None], p_t, 0.0)
            dv = _bmm(p_t, d_out, _BNN)
            dp_t = _bmm(heads(v_ref, HEAD_DIM), d_out, _BNT)
            ds_t = p_t * (dp_t - delta_row)
            dk = _bmm(ds_t, qh, _BNN)
            dq = _bmm(jnp.swapaxes(ds_t, 1, 2), kh, _BNN)
            rows = pl.ds(pl.multiple_of(i * t, t), t)
            for h in range(BWD_GROUP):
                dk_ref[:, 256 * h:256 * (h + 1)] += dk[h]
                dv_ref[:, HEAD_DIM * h:HEAD_DIM * (h + 1)] += dv[h]

            @pl.when(j == 0)
            def _():
                for h in range(BWD_GROUP):
                    dq_acc[rows, 256 * h:256 * (h + 1)] = dq[h]

            @pl.when(j > 0)
            def _():
                for h in range(BWD_GROUP):
                    dq_acc[rows, 256 * h:256 * (h + 1)] += dq[h]

        @pl.when(i == j)
        def _():
            update(True)

        @pl.when(i > j)
        def _():
            update(False)

        @pl.when(i == n - 1)
        def _():
            dk_ref[...] *= 1.0 / LOG2_E

        @pl.when(p == n_pairs - 1)
        def _():
            dq_acc[...] *= 1.0 / LOG2_E
            for gg in range(groups):
                @pl.when(g == gg)
                def _():
                    cp = pltpu.make_async_copy(dq_acc, dq_hbm.at[:, gq * gg:gq * (gg + 1)], sem)
                    cp.start()
                    cp.wait()

    q_blk = lambda g, p, qi_, kj_: (qi_[p], g)
    k_blk = lambda g, p, qi_, kj_: (kj_[p], g)
    return pl.pallas_call(
        body, name=name,
        grid_spec=pltpu.PrefetchScalarGridSpec(
            num_scalar_prefetch=2, grid=(groups, n_pairs),
            in_specs=[pl.BlockSpec((t, gq), q_blk), pl.BlockSpec((t, gq), k_blk), pl.BlockSpec((t, gv), k_blk),
                      pl.BlockSpec((t, gv), q_blk), pl.BlockSpec((None, 8, t), lambda g, p, qi_, kj_: (g, 0, qi_[p]))],
            out_specs=[pl.BlockSpec(memory_space=pl.ANY), pl.BlockSpec((t, gq), k_blk), pl.BlockSpec((t, gv), k_blk)],
            scratch_shapes=[pltpu.VMEM((S, gq), F32), pltpu.SemaphoreType.DMA]),
        out_shape=[jax.ShapeDtypeStruct((S, HEADS * 256), F32), jax.ShapeDtypeStruct((S, HEADS * 256), F32),
                   jax.ShapeDtypeStruct((S, HEADS * HEAD_DIM), F32)],
        compiler_params=_params(("arbitrary", "arbitrary")),
    )(qi, kj, q, k, v, d_o, st)


FFN_PIECE = 2 * D_FF // N_DEV
HID_PIECES = D_FF // FFN_PIECE


def _ffn_up(h, w8, name):
    S = h.shape[0]
    tm = _pick(S, MATMUL_ROWS)

    def body(h_ref, wg_ref, wu_ref, g_ref, u_ref, hid_ref, hid_t_ref):
        gate = _dot_raw(h_ref[...], wg_ref[...], "nn")
        up = _dot_raw(h_ref[...], wu_ref[...], "nn")
        sg = _sigmoid(gate)
        act = gate * sg
        hid = act * up
        g_ref[...] = (up * (sg * (1.0 + gate * (1.0 - sg)))).astype(BF16)
        u_ref[...] = act.astype(BF16)
        hid_ref[...] = hid.astype(BF16)
        hid_t_ref[...] = jnp.transpose(hid).astype(BF16)

    o_spec = pl.BlockSpec((None, tm, FFN_PIECE), lambda i, j: (j, i, 0))
    return pl.pallas_call(
        body, name=name, grid=(S // tm, HID_PIECES),
        in_specs=[pl.BlockSpec((tm, D_MODEL), lambda i, j: (i, 0)),
                  pl.BlockSpec((None, D_MODEL, FFN_PIECE), lambda i, j: (j, 0, 0)),
                  pl.BlockSpec((None, D_MODEL, FFN_PIECE), lambda i, j: (j + HID_PIECES, 0, 0))],
        out_specs=[o_spec] * 3 + [pl.BlockSpec((None, FFN_PIECE, tm), lambda i, j: (j, 0, i))],
        out_shape=[jax.ShapeDtypeStruct((HID_PIECES, S, FFN_PIECE), BF16)] * 3
        + [jax.ShapeDtypeStruct((HID_PIECES, FFN_PIECE, S), BF16)],
        compiler_params=_params(("parallel", "parallel")),
    )(h, w8, w8)


def _after_specs(after):
    return [] if after is None else [pl.BlockSpec(memory_space=pl.ANY)]


def _after_args(after):
    return [] if after is None else [after]


def _ffn_gw8(h, d_gate, d_up, name, after=None):
    S = h.shape[0]
    tm = 512
    tk = _pick(S, (512, 256, 128))
    nk = S // tk

    def body(h_ref, dg_ref, du_ref, *rest):
        o_ref, acc_ref = rest[-2:]
        k = pl.program_id(1)

        @pl.when(k == 0)
        def _():
            acc_ref[...] = jnp.zeros_like(acc_ref)

        h_t = jnp.transpose(h_ref[...])
        for p in range(HID_PIECES):
            acc_ref[p] += _dot_raw(h_t, dg_ref[p], "nn")
            acc_ref[HID_PIECES + p] += _dot_raw(h_t, du_ref[p], "nn")

        @pl.when(k == nk - 1)
        def _():
            o_ref[...] = acc_ref[...].astype(o_ref.dtype)

    d_spec = pl.BlockSpec((HID_PIECES, tk, FFN_PIECE), lambda i, k: (0, k, 0))
    return pl.pallas_call(
        body, name=name, grid=(D_MODEL // tm, nk),
        in_specs=[pl.BlockSpec((tk, tm), lambda i, k: (k, i)), d_spec, d_spec] + _after_specs(after),
        out_specs=pl.BlockSpec((2 * HID_PIECES, tm, FFN_PIECE), lambda i, k: (0, i, 0)),
        out_shape=jax.ShapeDtypeStruct((2 * HID_PIECES, D_MODEL, FFN_PIECE), BF16),
        scratch_shapes=[pltpu.VMEM((2 * HID_PIECES, tm, FFN_PIECE), F32)],
        compiler_params=_params(("parallel", "arbitrary")),
    )(h, d_gate, d_up, *_after_args(after))


EPILOGUE_ROWS = 256


def _dmod_epilogue(acc_ref, x_ref, do_ref, sc_ref, sh_ref, dx_ref, dsc_ref, dsh_ref, below, below_in, below_out):
    rows_total = acc_ref.shape[0]
    step = min(EPILOGUE_ROWS, rows_total)
    dsc, dsh, dg = 0.0, 0.0, 0.0
    for r in range(rows_total // step):
        rows = slice(step * r, step * (r + 1))
        _, vjp = jax.vjp(_modulate, x_ref[rows, :], sc_ref[...], sh_ref[...])
        dx, dsc_r, dsh_r = vjp(acc_ref[rows, :])
        dx = dx + do_ref[rows, :]
        dx_ref[rows, :] = dx
        dsc, dsh = dsc + dsc_r, dsh + dsh_r
        if below is not None:
            coef = below[2]
            below_out[0][rows, :] = (coef * below_in[1][...] * dx).astype(below_out[0].dtype)
            dg = dg + jnp.sum(coef * below_in[0][rows, :] * dx, axis=0, keepdims=True)
    dsc_ref[...] += dsc
    dsh_ref[...] += dsh
    if below is not None:
        below_out[1][...] += dg


def _ffn_dh(d_gate, d_up, w8, x, d_out, scale, shift, name, after=None, below=None):
    S = d_gate.shape[1]
    tm = _pick(S, MATMUL_ROWS if below is None else MATMUL_ROWS[1:])
    n_below = 0 if below is None else 2

    def body(dg_ref, du_ref, wg_ref, wu_ref, x_ref, do_ref, sc_ref, sh_ref, *rest):
        below_in = rest[:n_below]
        outs = rest[len(rest) - 4 - n_below:]
        dx_ref, dsc_ref, dsh_ref = outs[:3]
        below_out, acc_ref = outs[3:3 + n_below], outs[-1]
        i, k = pl.program_id(0), pl.program_id(1)

        @pl.when(k == 0)
        def _():
            acc_ref[...] = jnp.zeros_like(acc_ref)

        acc_ref[...] += _dot_raw(dg_ref[...], wg_ref[...], "nt") + _dot_raw(du_ref[...], wu_ref[...], "nt")

        @pl.when((k == 0) & (i == 0))
        def _():
            for r in (dsc_ref, dsh_ref) + tuple(below_out[1:]):
                r[...] = jnp.zeros_like(r)

        @pl.when(k == HID_PIECES - 1)
        def _():
            _dmod_epilogue(acc_ref, x_ref, do_ref, sc_ref, sh_ref, dx_ref, dsc_ref, dsh_ref, below, below_in, below_out)

    d_spec = pl.BlockSpec((None, tm, FFN_PIECE), lambda i, k: (k, i, 0))
    row = pl.BlockSpec((tm, D_MODEL), lambda i, k: (i, 0))
    par = pl.BlockSpec((1, D_MODEL), lambda i, k: (0, 0))
    row_shape, par_shape = jax.ShapeDtypeStruct((S, D_MODEL), F32), jax.ShapeDtypeStruct((1, D_MODEL), F32)
    return pl.pallas_call(
        body, name=name, grid=(S // tm, HID_PIECES),
        in_specs=[d_spec, d_spec,
                  pl.BlockSpec((None, D_MODEL, FFN_PIECE), lambda i, k: (k, 0, 0)),
                  pl.BlockSpec((None, D_MODEL, FFN_PIECE), lambda i, k: (k + HID_PIECES, 0, 0)),
                  row, row, par, par] + [row, par][:n_below] + _after_specs(after),
        out_specs=[row, par, par] + [row, par][:n_below],
        out_shape=[row_shape, par_shape, par_shape] + [jax.ShapeDtypeStruct((S, D_MODEL), BF16), par_shape][:n_below],
        scratch_shapes=[pltpu.VMEM((tm, D_MODEL), F32)],
        compiler_params=_params(("arbitrary", "arbitrary")),
    )(d_gate, d_up, w8, w8, x, d_out, scale, shift, *(below[:2] if below is not None else ()), *_after_args(after))


def _swiglu_bwd(d_hid, hid_by_gate, hid_by_up):
    return d_hid * hid_by_gate, d_hid * hid_by_up


def _adamw_math(w_, g_, m_, v_):
    m_ = ADAM_B1 * m_ + (1.0 - ADAM_B1) * g_
    v_ = ADAM_B2 * v_ + (1.0 - ADAM_B2) * (g_ * g_)
    m_hat = m_ / (1.0 - ADAM_B1 ** ADAM_STEP)
    v_hat = v_ / (1.0 - ADAM_B2 ** ADAM_STEP)
    return -ADAM_LR * (m_hat / (jnp.sqrt(v_hat) + ADAM_EPS) + ADAM_WD * w_), m_, v_


def _adamw(w, g, m, v, name):
    R, C = w.shape
    tr = _pick(R, (256, 176, 128, 64, 32, 16, 8))

    def body(w_ref, g_ref, m_ref, v_ref, d_ref, nm_ref, nv_ref):
        d_ref[...], nm_ref[...], nv_ref[...] = _adamw_math(w_ref[...], g_ref[...], m_ref[...], v_ref[...])

    spec = pl.BlockSpec((tr, C), lambda i: (i, 0))
    return pl.pallas_call(
        body, name=name, grid=(R // tr,),
        in_specs=[spec] * 4, out_specs=[spec] * 3,
        out_shape=[jax.ShapeDtypeStruct((R, C), F32)] * 3,
        compiler_params=_params(("parallel",)),
    )(w, g, m, v)


def _sum_adamw(parts, w, m, v, name, transposed=False):
    _, R, C = parts.shape
    tr = _pick(R, (256, 176, 128, 64, 32, 16, 8))

    def body(p_ref, w_ref, m_ref, v_ref, g_ref, d_ref, nm_ref, nv_ref):
        g_ = p_ref[0].astype(F32)
        for d in range(1, N_DEV):
            g_ = g_ + p_ref[d].astype(F32)
        if transposed:
            g_ = jnp.transpose(g_)
        g_ref[...] = g_
        d_ref[...], nm_ref[...], nv_ref[...] = _adamw_math(w_ref[...], g_, m_ref[...], v_ref[...])

    spec = pl.BlockSpec((C, tr), lambda i: (0, i)) if transposed else pl.BlockSpec((tr, C), lambda i: (i, 0))
    return pl.pallas_call(
        body, name=name, grid=(R // tr,),
        in_specs=[pl.BlockSpec((N_DEV, tr, C), lambda i: (0, i, 0)), spec, spec, spec], out_specs=[spec] * 4,
        out_shape=[jax.ShapeDtypeStruct(w.shape, F32)] * 4,
        compiler_params=_params(("parallel",)),
    )(parts, w, m, v)


def _sum_devices(parts, name):
    _, R, C = parts.shape
    tr = _pick(R, (512, 256, 176, 128, 64, 32, 16, 8))

    def body(p_ref, o_ref):
        acc = p_ref[0].astype(F32)
        for d in range(1, N_DEV):
            acc = acc + p_ref[d].astype(F32)
        o_ref[...] = acc

    return pl.pallas_call(
        body, name=name, grid=(R // tr,),
        in_specs=[pl.BlockSpec((N_DEV, tr, C), lambda i: (0, i, 0))],
        out_specs=pl.BlockSpec((tr, C), lambda i: (i, 0)),
        out_shape=jax.ShapeDtypeStruct((R, C), F32),
        compiler_params=_params(("parallel",)),
    )(parts)


def _my_place():
    return lax.axis_index("x"), lax.axis_index("y"), lax.axis_index("c")


def _all_gather(blocks, name):
    n = len(blocks)

    def body(*refs):
        x_refs, out_refs = refs[:n], refs[n:2 * n]
        send_sems, recv_sems, local_sems = refs[2 * n:]
        x, y, c = _my_place()
        me, sibling = (x, y, c), (x, y, 1 - c)
        chips = [(1 - x, y), (x, 1 - y), (1 - x, 1 - y)]

        def copy(a, k, blk, to, own=False):
            slot = out_refs[a].at[4 * blk[0] + 2 * blk[1] + blk[2]]
            return pltpu.make_async_remote_copy(
                src_ref=x_refs[a] if own else slot, dst_ref=slot,
                send_sem=send_sems.at[7 * a + k], recv_sem=recv_sems.at[7 * a + k], device_id=to, device_id_type=MESH)

        mine = [pltpu.make_async_copy(x_refs[a], out_refs[a].at[4 * x + 2 * y + c], local_sems.at[a]) for a in range(n)]
        for cp in mine:
            cp.start()
        first = []
        for j, chip in enumerate(chips):
            first += [copy(a, 1 + j, me, (*chip, c), own=True) for a in range(n)]
        first += [copy(a, 0, me, sibling, own=True) for a in range(n)]
        for cp in first:
            cp.start()
        passed = []
        for j, chip in enumerate(chips):
            for a in range(n):
                copy(a, 1 + j, (*chip, c), me).wait_recv()
                passed.append(copy(a, 4 + j, (*chip, c), sibling))
                passed[-1].start()
        for a in range(n):
            copy(a, 0, sibling, me).wait_recv()
        for j, chip in enumerate(chips):
            for a in range(n):
                copy(a, 4 + j, (*chip, 1 - c), me).wait_recv()
        for cp in first + passed:
            cp.wait_send()
        for cp in mine:
            cp.wait()

    return pl.pallas_call(
        body, name=name,
        out_shape=[jax.ShapeDtypeStruct((N_DEV,) + b.shape, b.dtype) for b in blocks],
        in_specs=[pl.BlockSpec(memory_space=pl.ANY)] * n,
        out_specs=[pl.BlockSpec(memory_space=pl.ANY)] * n,
        scratch_shapes=[pltpu.SemaphoreType.DMA((7 * n,)), pltpu.SemaphoreType.DMA((7 * n,)), pltpu.SemaphoreType.DMA((n,))],
    )(*blocks)


def _all_to_all(pieces, name):
    n = len(pieces)

    def body(*refs):
        x_refs, out_refs = refs[:n], refs[n:2 * n]
        send_sems, recv_sems, local_sems = refs[2 * n:]
        x, y, c = _my_place()
        me = 4 * x + 2 * y + c
        mine = [pltpu.make_async_copy(x_refs[a].at[me], out_refs[a].at[me], local_sems.at[a]) for a in range(n)]
        for cp in mine:
            cp.start()
        copies = []
        for k in (2, 4, 6, 3, 5, 7, 1):
            px = 1 - x if k & 4 else x
            py = 1 - y if k & 2 else y
            pc = 1 - c if k & 1 else c
            peer = 4 * px + 2 * py + pc
            for a in range(n):
                copies.append(pltpu.make_async_remote_copy(
                    src_ref=x_refs[a].at[peer], dst_ref=out_refs[a].at[me],
                    send_sem=send_sems.at[7 * a + k - 1], recv_sem=recv_sems.at[7 * a + k - 1],
                    device_id=(px, py, pc), device_id_type=MESH))
        for cp in copies:
            cp.start()
        for cp in copies:
            cp.wait_recv()
        for cp in copies:
            cp.wait_send()
        for cp in mine:
            cp.wait()

    return pl.pallas_call(
        body, name=name,
        out_shape=[jax.ShapeDtypeStruct(p.shape, p.dtype) for p in pieces],
        in_specs=[pl.BlockSpec(memory_space=pl.ANY)] * n,
        out_specs=[pl.BlockSpec(memory_space=pl.ANY)] * n,
        scratch_shapes=[pltpu.SemaphoreType.DMA((7 * n,)), pltpu.SemaphoreType.DMA((7 * n,)), pltpu.SemaphoreType.DMA((n,))],
    )(*pieces)


def _peers():
    x, y, c = _my_place()
    out = []
    for k in (2, 4, 6, 3, 5, 7, 1):
        px = 1 - x if k & 4 else x
        py = 1 - y if k & 2 else y
        pc = 1 - c if k & 1 else c
        out.append((k, (px, py, pc), 4 * px + 2 * py + pc))
    return out


def _exchange_copies(x_refs, land_refs, send_sems, recv_sems, scatter):
    x, y, c = _my_place()
    me = 4 * x + 2 * y + c
    starts, arrivals = [], []
    for k, place, peer in _peers():
        for a, (x_ref, land_ref) in enumerate(zip(x_refs, land_refs)):
            sems = dict(send_sem=send_sems.at[7 * a + k - 1], recv_sem=recv_sems.at[7 * a + k - 1],
                        device_id=place, device_id_type=MESH)
            src = x_ref.at[peer] if scatter else x_ref
            starts.append(pltpu.make_async_remote_copy(src_ref=src, dst_ref=land_ref.at[me], **sems))
            arrivals.append(pltpu.make_async_remote_copy(src_ref=src, dst_ref=land_ref.at[peer], **sems))
    return starts, arrivals


def _exchange_start(arrays, scatter, name):
    n = len(arrays)
    hbm = pl.BlockSpec(memory_space=pltpu.HBM)
    sem = pl.BlockSpec(memory_space=pltpu.SEMAPHORE)
    lands = [lax.empty(a.shape if scatter else (N_DEV,) + a.shape, a.dtype) for a in arrays]

    def body(*refs):
        x_refs, land_refs = refs[:n], refs[n:2 * n]
        send_sems, recv_sems = refs[2 * n], refs[2 * n + 1]
        token = refs[-1]
        starts, _ = _exchange_copies(x_refs, land_refs, send_sems, recv_sems, scatter)
        for cp in starts:
            cp.start()
        token[...] = jnp.zeros_like(token)

    res = pl.pallas_call(
        body, name=name,
        out_shape=(pltpu.SemaphoreType.DMA((7 * n,)), pltpu.SemaphoreType.DMA((7 * n,)),
                   *[pltpu.HBM(a.shape, a.dtype) for a in arrays], *[pltpu.HBM(l.shape, l.dtype) for l in lands],
                   jax.ShapeDtypeStruct((8, 128), F32)),
        in_specs=[hbm] * (2 * n),
        out_specs=(sem, sem, *[hbm] * (2 * n), pl.BlockSpec(memory_space=pltpu.VMEM)),
        input_output_aliases={i: 2 + i for i in range(2 * n)},
        compiler_params=pltpu.CompilerParams(has_side_effects=pltpu.SideEffectType.DATAFLOW_SIDE_EFFECTING),
    )(*[pltpu.with_memory_space_constraint(a, pltpu.HBM) for a in arrays],
      *[pltpu.with_memory_space_constraint(l, pltpu.HBM) for l in lands])
    return res[0], res[1], list(res[2:2 + n]), list(res[2 + n:2 + 2 * n]), res[-1]


def _exchange_wait(handles, scatter, after, name):
    send_sems, recv_sems, arrays, lands, _ = handles
    n = len(arrays)
    hbm = pl.BlockSpec(memory_space=pltpu.HBM)
    sem = pl.BlockSpec(memory_space=pltpu.SEMAPHORE)

    def body(*refs):
        x_refs, land_refs = refs[:n], refs[n:2 * n]
        send_s, recv_s = refs[2 * n], refs[2 * n + 1]
        starts, arrivals = _exchange_copies(x_refs, land_refs, send_s, recv_s, scatter)
        for cp in arrivals:
            cp.wait_recv()
        for cp in starts:
            cp.wait_send()

    res = pl.pallas_call(
        body, name=name,
        out_shape=(*[pltpu.HBM(a.shape, a.dtype) for a in arrays], *[pltpu.HBM(l.shape, l.dtype) for l in lands]),
        in_specs=[hbm] * (2 * n) + [sem, sem, pl.BlockSpec(memory_space=pl.ANY)],
        out_specs=tuple([hbm] * (2 * n)),
        input_output_aliases={i: i for i in range(2 * n)},
        compiler_params=pltpu.CompilerParams(has_side_effects=pltpu.SideEffectType.DATAFLOW_SIDE_EFFECTING),
    )(*arrays, *lands, send_sems, recv_sems, after)
    me = 4 * lax.axis_index("x") + 2 * lax.axis_index("y") + lax.axis_index("c")
    out = []
    for src, got in zip(res[:n], res[n:]):
        zeros = (0,) * (got.ndim - 1)
        own = lax.dynamic_slice(src, (me,) + zeros, (1,) + src.shape[1:]) if scatter else src[None]
        out.append(lax.dynamic_update_slice(got, own, (me,) + zeros))
    return out


def _pad_lanes(v, at=0, width=128):
    return jnp.pad(v, ((0, 0), (at, width - at - v.shape[1])))


def _pack_weights(P):
    W = {}
    w = P["w_in"]
    W["wp"] = jnp.concatenate([w[:, :2048], w[:, 2440:2696], w[:, 2056:2440], w[:, 2696:2760], w[:, 2048:2056],
                               jnp.zeros((D_MODEL, N_IN_PACKED - N_IN), w.dtype)], axis=1).astype(BF16)
    W["conv_w"] = P["gdn_conv_w"].astype(F32)
    W["alog_p"] = _pad_lanes(P["gdn_a_log"], 64)
    W["dt_p"] = _pad_lanes(P["gdn_dt_bias"], 64)
    W["gnw"] = P["gdn_norm_w"]
    W["qnw"] = P["mla_q_norm_w"]
    W["kvnw"] = P["mla_kv_norm_w"]
    uq = P["mla_w_uq"].reshape(Q_LORA, HEADS, HEAD_DIM + ROPE)
    W["wuq"] = jnp.pad(uq, ((0, 0), (0, 0), (0, 256 - HEAD_DIM - ROPE))).reshape(Q_LORA, HEADS * 256).astype(BF16)
    ukv = P["mla_w_ukv"].reshape(KV_LORA, HEADS, 2, HEAD_DIM)
    W["wukv"] = ukv.transpose(0, 2, 1, 3).reshape(KV_LORA, 2 * HEADS * HEAD_DIM).astype(BF16)
    W["qn_w"] = P["qkn_q_nope"]
    W["qr_w"] = _pad_lanes(P["qkn_q_rope"])
    W["kn_w"] = P["qkn_k_nope"]
    W["kr_w"] = _pad_lanes(P["qkn_k_rope"])
    W["onw"] = P["mla_out_norm_w"]
    W["wout"] = P["w_out"].astype(BF16)
    return W


def _unpack_grads(G):
    g = G["wp"]
    uq = G["wuq"].reshape(Q_LORA, HEADS, 256)[:, :, :HEAD_DIM + ROPE].reshape(Q_LORA, HEADS * (HEAD_DIM + ROPE))
    ukv = G["wukv"].reshape(KV_LORA, 2, HEADS, HEAD_DIM).transpose(0, 2, 1, 3).reshape(KV_LORA, 2 * HEADS * HEAD_DIM)
    return {
        "w_in": jnp.concatenate([g[:, :2048], g[:, 2752:2760], g[:, 2304:2688], g[:, 2048:2304], g[:, 2688:2752]], axis=1),
        "gdn_conv_w": G["conv_w"], "gdn_a_log": G["alog_p"][:, 64:68], "gdn_dt_bias": G["dt_p"][:, 64:68],
        "gdn_norm_w": G["gnw"], "mla_q_norm_w": G["qnw"], "mla_w_uq": uq, "mla_kv_norm_w": G["kvnw"], "mla_w_ukv": ukv,
        "qkn_q_nope": G["qn_w"], "qkn_q_rope": G["qr_w"][:, :ROPE], "qkn_k_nope": G["kn_w"], "qkn_k_rope": G["kr_w"][:, :ROPE],
        "mla_out_norm_w": G["onw"], "w_out": G["wout"],
    }


def _rope_tables(positions):
    half = ROPE // 2
    inv_freq = ROPE_BASE ** (-jnp.arange(half, dtype=F32) / half)
    ang = positions.astype(F32)[:, None] * inv_freq
    cos, sin = jnp.cos(ang), jnp.sin(ang)
    zeros = jnp.zeros((positions.shape[0], 128 - ROPE), F32)
    return jnp.concatenate([cos, cos, zeros], axis=1), jnp.concatenate([-sin, sin, zeros], axis=1)


def _mod_fn(x, scale, shift):
    return (_modulate(x, scale, shift),)


def _ffn_down_loss(hid, wo4, x, gate_w, target, name):
    S = x.shape[0]
    tb = _pick(S, MATMUL_ROWS)
    n = S // tb

    def body(hid_ref, wo_ref, x_ref, g_ref, t_ref, dx_ref, df_ref, dg_ref, l_ref, acc_ref):
        i, k = pl.program_id(0), pl.program_id(1)

        @pl.when(k == 0)
        def _():
            acc_ref[...] = jnp.zeros_like(acc_ref)

        acc_ref[...] += _dot_raw(hid_ref[...], wo_ref[...], "nn")

        @pl.when((k == 0) & (i == 0))
        def _():
            dg_ref[...] = jnp.zeros_like(dg_ref)
            l_ref[...] = jnp.zeros_like(l_ref)

        @pl.when(k == HID_PIECES - 1)
        def _():
            step = min(EPILOGUE_ROWS, tb)
            for r in range(tb // step):
                rows = slice(step * r, step * (r + 1))
                f = acc_ref[rows, :]
                diff = x_ref[rows, :] + 0.5 * g_ref[...] * f - t_ref[rows, :]
                dx = diff * (1.0 / D_MODEL)
                dx_ref[rows, :] = dx
                df_ref[rows, :] = (0.5 * g_ref[...] * dx).astype(df_ref.dtype)
                dg_ref[...] += jnp.sum(0.5 * f * dx, axis=0, keepdims=True)
                l_ref[...] += jnp.sum(diff * diff, axis=0, keepdims=True)

        @pl.when((k == HID_PIECES - 1) & (i == n - 1))
        def _():
            l_ref[...] = jnp.full(l_ref.shape, (0.5 / D_MODEL) * jnp.sum(l_ref[...]), F32)

    row = pl.BlockSpec((tb, D_MODEL), lambda i, k: (i, 0))
    par = pl.BlockSpec((1, D_MODEL), lambda i, k: (0, 0))
    return pl.pallas_call(
        body, name=name, grid=(n, HID_PIECES),
        in_specs=[pl.BlockSpec((None, tb, FFN_PIECE), lambda i, k: (k, i, 0)),
                  pl.BlockSpec((None, FFN_PIECE, D_MODEL), lambda i, k: (k, 0, 0)), row, par, row],
        out_specs=[row, row, par, par],
        out_shape=[jax.ShapeDtypeStruct((S, D_MODEL), F32), jax.ShapeDtypeStruct((S, D_MODEL), BF16),
                   jax.ShapeDtypeStruct((1, D_MODEL), F32), jax.ShapeDtypeStruct((1, D_MODEL), F32)],
        scratch_shapes=[pltpu.VMEM((tb, D_MODEL), F32)],
        compiler_params=_params(("arbitrary", "arbitrary")),
    )(hid, wo4, x, gate_w, target)


def _ffn_fwd(x, scale, shift, gate_w, w8, wo4, tag, target=None):
    S = x.shape[0]
    tm = _pick(S, (512, 256, 128))
    (h,) = _rowwise(_mod_fn, [(x, tm, D_MODEL, 0)], [scale, shift], [(tm, D_MODEL, BF16)], S // tm, tag + "_mod")
    by_gate, by_up, hid, hid_t = _ffn_up(h, w8, tag + "_up")
    if target is not None:
        dx_out, df, d_gate_w, loss_row = _ffn_down_loss(hid, wo4, x, gate_w, target, tag + "_down")
        return (dx_out, loss_row), (h, by_gate, by_up, hid_t, None, df, d_gate_w)
    tb = _pick(S, MATMUL_ROWS)
    mn = pl.BlockSpec((tb, D_MODEL), lambda i, j, k: (i, j))
    f, x_out = _mmg(hid, wo4, "nn", name=tag + "_down", grid=(S // tb, 1, HID_PIECES),
                    a_spec=pl.BlockSpec((None, tb, FFN_PIECE), lambda i, j, k: (k, i, 0)),
                    b_spec=pl.BlockSpec((None, FFN_PIECE, D_MODEL), lambda i, j, k: (k, 0, j)),
                    out_spec=mn, out_shapes=[jax.ShapeDtypeStruct((S, D_MODEL), F32)] * 2, acc_shape=(tb, D_MODEL),
                    extras=[x, gate_w], extra_specs=[mn, pl.BlockSpec((1, D_MODEL), lambda i, j, k: (0, j))],
                    epi=lambda acc, x_, g_: (acc, x_ + 0.5 * g_ * acc))
    return x_out, (h, by_gate, by_up, hid_t, f, None, None)


def _ffn_bwd(d_out, x, scale, shift, gate_w, w8, wo4, saved, tag, grad_ready, below=None):
    h, gate, up, hid_t, f, df, d_gate_w = saved
    S = x.shape[0]
    tm = _pick(S, (512, 256, 128))
    tk = _pick(S, (512, 256, 128))
    n = S // tm
    if df is None:
        (df,), (d_gate_w,) = _rowwise_bwd(lambda f_, g_: (0.5 * g_ * f_,), [(f, tm, D_MODEL, 0)], [], [gate_w],
                                          [(d_out, tm, D_MODEL, 0)], n, tag + "_dres", row_dtypes=(BF16,))
    tb = _pick(S, MATMUL_ROWS)
    piece = pl.BlockSpec((None, tb, FFN_PIECE), lambda i, j, k: (j, i, 0))
    d_gate, d_up = _mmg(df, wo4, "nt", name=tag + "_ddown", grid=(S // tb, HID_PIECES, 1),
                        a_spec=pl.BlockSpec((tb, D_MODEL), lambda i, j, k: (i, 0)),
                        b_spec=pl.BlockSpec((None, FFN_PIECE, D_MODEL), lambda i, j, k: (j, 0, 0)),
                        out_spec=piece, out_shapes=[jax.ShapeDtypeStruct((HID_PIECES, S, FFN_PIECE), BF16)] * 2,
                        acc_shape=(tb, FFN_PIECE), extras=[gate, up], extra_specs=[piece, piece], epi=_swiglu_bwd)
    tk = _pick(S, MATMUL_ROWS)
    g_wo4 = _mmg(hid_t, df, "nn", name=tag + "_gwo", grid=(HID_PIECES, 1, S // tk),
                 a_spec=pl.BlockSpec((None, FFN_PIECE, tk), lambda i, j, k: (i, 0, k)),
                 b_spec=pl.BlockSpec((tk, D_MODEL), lambda i, j, k: (k, j)),
                 out_spec=pl.BlockSpec((None, FFN_PIECE, D_MODEL), lambda i, j, k: (i, 0, j)),
                 out_shapes=[jax.ShapeDtypeStruct((HID_PIECES, FFN_PIECE, D_MODEL), BF16)], acc_shape=(FFN_PIECE, D_MODEL))
    g_w8 = _ffn_gw8(h, d_gate, d_up, tag + "_gw8", after=grad_ready("wo4", g_wo4))
    res = _ffn_dh(d_gate, d_up, w8, x, d_out, scale, shift, tag + "_dh", after=grad_ready("w8", g_w8), below=below)
    return (res[0], res[1], res[2], d_gate_w) + tuple(res[3:])


def _dproj_dmod(d_proj, wp, x, d_out, scale, shift, name, below=None):
    S, K = d_proj.shape
    tm = _pick(S, MATMUL_ROWS if below is None else MATMUL_ROWS[1:])
    tk = _pick(K, (1408, 512, 256, 128))
    nk = K // tk
    n_below = 0 if below is None else 2

    def body(dp_ref, w_ref, x_ref, do_ref, sc_ref, sh_ref, *rest):
        below_in = rest[:n_below]
        dx_ref, dsc_ref, dsh_ref = rest[n_below:n_below + 3]
        below_out, acc_ref = rest[n_below + 3:n_below + 3 + n_below], rest[-1]
        i, k = pl.program_id(0), pl.program_id(1)

        @pl.when(k == 0)
        def _():
            acc_ref[...] = jnp.zeros_like(acc_ref)

        acc_ref[...] += _dot_raw(dp_ref[...], w_ref[...], "nt")

        @pl.when((k == 0) & (i == 0))
        def _():
            for r in (dsc_ref, dsh_ref) + tuple(below_out[1:]):
                r[...] = jnp.zeros_like(r)

        @pl.when(k == nk - 1)
        def _():
            _dmod_epilogue(acc_ref, x_ref, do_ref, sc_ref, sh_ref, dx_ref, dsc_ref, dsh_ref, below, below_in, below_out)

    row = pl.BlockSpec((tm, D_MODEL), lambda i, k: (i, 0))
    par = pl.BlockSpec((1, D_MODEL), lambda i, k: (0, 0))
    row_shape, par_shape = jax.ShapeDtypeStruct((S, D_MODEL), F32), jax.ShapeDtypeStruct((1, D_MODEL), F32)
    return pl.pallas_call(
        body, name=name, grid=(S // tm, nk),
        in_specs=[pl.BlockSpec((tm, tk), lambda i, k: (i, k)), pl.BlockSpec((D_MODEL, tk), lambda i, k: (0, k)),
                  row, row, par, par] + [row, par][:n_below],
        out_specs=[row, par, par] + [row, par][:n_below],
        out_shape=[row_shape, par_shape, par_shape] + [jax.ShapeDtypeStruct((S, D_MODEL), BF16), par_shape][:n_below],
        scratch_shapes=[pltpu.VMEM((tm, D_MODEL), F32)],
        compiler_params=_params(("arbitrary", "arbitrary")),
    )(d_proj, wp, x, d_out, scale, shift, *(below[:2] if below is not None else ()))


def _mixer_fwd(x1, scale, shift, gate_w, cos_p, sin_p, W):
    S = x1.shape[0]
    tm = _pick(S, (512, 256, 128))
    tv = _pick(S, (256, 128))
    ta = _pick(S, (512, 256, 128))
    nc = S // CHUNK
    (h2,) = _rowwise(_mod_fn, [(x1, tm, D_MODEL, 0)], [scale, shift], [(tm, D_MODEL, BF16)], S // tm, "mix_mod")
    proj = _mm(h2, W["wp"], "nn", name="mix_proj")
    qkvc = _conv_fwd(proj, W["conv_w"], tv, "gdn_conv")
    kab = (proj, tv, 128, 21)
    q_a, k_a, v_a, gb = _rowwise(_gdn_pre_fn, [(qkvc, tv, 1536, 0), kab], [W["alog_p"], W["dt_p"]],
                                 [(tv, 512, F32)] * 3 + [(tv, 128, F32)], S // tv, "gdn_pre")
    ti = _pick(S, INTRA_ROWS)
    intra = _rowwise(_gdn_intra_fn, [(q_a, ti, 512, 0), (k_a, ti, 512, 0), (v_a, ti, 512, 0), (gb, ti, 128, 0)],
                     [], [(ti, 512, F32)] * 4 + [(ti, CHUNK, F32)] * 4 + [(ti // 8, 512, F32)] + [(ti, CHUNK, F32)] * 4,
                     S // ti, "gdn_intra")
    u, wk, qd, kd, qks, gl, invs = intra[0], intra[1], intra[2], intra[3], tuple(intra[4:8]), intra[8], tuple(intra[9:])
    o_a, s_prev = _gdn_scan_fwd(u, wk, qd, kd, qks, gl, "gdn_scan")
    mla_params = [W["qnw"], W["kvnw"], W["wuq"], W["wukv"], W["qn_w"], W["qr_w"], W["kn_w"], W["kr_w"]]
    def mla_pre_with_vt(*a):
        q_, k_, v_ = _mla_pre_fn(*a)
        return q_, k_, v_, jnp.transpose(v_)

    q_b, k_b, v_b, vt_b = _rowwise(mla_pre_with_vt,
                                   [(proj, tv, 256, 8), (proj, tv, 384, 6), kab, (cos_p, tv, 128, 0), (sin_p, tv, 128, 0)],
                                   mla_params, [(tv, 1024, BF16), (tv, 1024, BF16), (tv, 512, BF16), (512, tv, BF16, "across")],
                                   S // tv, "mla_pre")
    o_b, lse = _attn_fwd(q_b, k_b, vt_b, ta, "mla_attn")
    (mixed,) = _rowwise(_mix_post_fn, [(o_a, tv, 512, 0), (proj, tv, 512, 3), (o_b, tv, 512, 0)], [W["gnw"], W["onw"]],
                        [(tv, D_MODEL, BF16)], S // tv, "mix_post")
    y, x2 = _mm(mixed, W["wout"], "nn", name="mix_out", out_dtypes=(F32, F32), extras=[x1], extra_params=[gate_w],
                epi=lambda acc, x_, g_: (acc, x_ + g_ * acc))
    saved = (h2, proj, qkvc, q_a, k_a, v_a, gb, u, wk, qd, kd, qks, gl, invs, s_prev, o_a, q_b, k_b, v_b, o_b, lse, mixed, y)
    return x2, saved


def _mixer_bwd(d_out, dy, x1, scale, shift, cos_p, sin_p, W, saved, below):
    (h2, proj, qkvc, q_a, k_a, v_a, gb, u, wk, qd, kd, qks, gl, invs, s_prev, o_a, q_b, k_b, v_b, o_b, lse, mixed, y) = saved
    S = x1.shape[0]
    tm = _pick(S, (512, 256, 128))
    tv = _pick(S, (256, 128))
    ta = _pick(S, (512, 256, 128))
    nc = S // CHUNK
    G = {}
    d_mixed = _mm(dy, W["wout"], "nt", name="mix_dout")
    G["wout"] = _mm(mixed, dy, "tn", name="mix_gwout")
    (do_a, dz, do_b), (G["gnw"], G["onw"]) = _rowwise_bwd(
        _mix_post_fn, [(o_a, tv, 512, 0), (proj, tv, 512, 3), (o_b, tv, 512, 0)], [], [W["gnw"], W["onw"]],
        [(d_mixed, tv, D_MODEL, 0)], S // tv, "mix_dpost")
    stats = _attn_stats(o_b, lse, do_b, ta, "mla_stats")
    dq_b, dk_b, dv_b = _attn_bwd(q_b, k_b, v_b, do_b, stats, ta, "mla_dattn")
    kab = (proj, tv, 128, 21)
    mla_params = [W["qnw"], W["kvnw"], W["wuq"], W["wukv"], W["qn_w"], W["qr_w"], W["kn_w"], W["kr_w"]]
    (d_ckv, d_cq, d_kab), mla_grads = _rowwise_bwd(
        _mla_pre_fn, [(proj, tv, 256, 8), (proj, tv, 384, 6), kab], [(cos_p, tv, 128, 0), (sin_p, tv, 128, 0)], mla_params,
        [(dq_b, tv, 1024, 0), (dk_b, tv, 1024, 0), (dv_b, tv, 512, 0)], S // tv, "mla_dpre")
    for key, g in zip(("qnw", "kvnw", "wuq", "wukv", "qn_w", "qr_w", "kn_w", "kr_w"), mla_grads):
        G[key] = g
    scan_grads = _gdn_scan_bwd(u, wk, qd, kd, qks, gl, s_prev, do_a, "gdn_dscan")
    ti = _pick(S, INTRA_ROWS)
    intra_douts = [(scan_grads[i], ti, 512, 0) for i in range(4)] + [(scan_grads[4 + i], ti, CHUNK, 0) for i in range(4)]
    intra_douts.append((scan_grads[8], ti // 8, 512, 0))
    (dq_a, dk_a, dv_a, d_gb), _ = _rowwise_bwd(
        _gdn_intra_fn, [(q_a, ti, 512, 0), (k_a, ti, 512, 0), (v_a, ti, 512, 0), (gb, ti, 128, 0)],
        [(x_, ti, CHUNK, 0) for x_ in invs], [], intra_douts, S // ti, "gdn_dintra")
    (d_qkvc, d_kab), (G["alog_p"], G["dt_p"]) = _rowwise_bwd(
        _gdn_pre_fn, [(qkvc, tv, 1536, 0), kab], [], [W["alog_p"], W["dt_p"]],
        [(dq_a, tv, 512, 0), (dk_a, tv, 512, 0), (dv_a, tv, 512, 0), (d_gb, tv, 128, 0)], S // tv, "gdn_dpre",
        adds=[(1, d_kab)])
    d_qkv, g_conv = _conv_bwd(proj, d_qkvc, W["conv_w"], tv, "gdn_dconv")
    G["conv_w"] = g_conv[:4]
    d_proj = jnp.concatenate([d_qkv, dz, d_ckv, d_cq, d_kab], axis=1).astype(BF16)
    G["wp"] = _mm(h2, d_proj, "tn", name="mix_gwp")
    dx1, G["s2"], G["sh2"], d_below, dg_below = _dproj_dmod(d_proj, W["wp"], x1, d_out, scale, shift, "mix_dproj", below=below)
    return dx1, d_below, dg_below, G


def _local_step(x, target, mod, cos_p, sin_p, W1, mixer_weights, ffn2_weights, ffn_grad_ready, mixer_grads_ready):
    sh1, s1, g1, sh2, s2, g2, sh3, s3, g3 = [mod[:, D_MODEL * i:D_MODEL * (i + 1)] for i in range(N_MOD)]
    x1, saved1 = _ffn_fwd(x, s1, sh1, g1, W1["f1_w8"], W1["f1_wo4"], "ffn1")
    W = mixer_weights(x1)
    x2, saved2 = _mixer_fwd(x1, s2, sh2, g2, cos_p, sin_p, W)
    W.update(ffn2_weights(x2))
    (dx3, loss_row), saved3 = _ffn_fwd(x2, s3, sh3, g3, W["f2_w8"], W["f2_wo4"], "ffn2", target=target)
    dx2, d_s3, d_sh3, d_g3, dy, d_g2 = _ffn_bwd(dx3, x2, s3, sh3, g3, W["f2_w8"], W["f2_wo4"], saved3, "ffn2",
                                                ffn_grad_ready("f2"), below=(saved2[-1], g2, 1.0))
    dx1, df1, d_g1, G = _mixer_bwd(dx2, dy, x1, s2, sh2, cos_p, sin_p, W, saved2, below=(saved1[4], g1, 0.5))
    d_sh2, d_s2 = G.pop("sh2"), G.pop("s2")
    saved1 = saved1[:5] + (df1, d_g1 + mixer_grads_ready(G))
    dx, d_s1, d_sh1, d_g1 = _ffn_bwd(dx1, x, s1, sh1, g1, W1["f1_w8"], W1["f1_wo4"], saved1, "ffn1", ffn_grad_ready("f1"))
    d_mod = jnp.concatenate([d_sh1, d_s1, d_g1, d_sh2, d_s2, d_g2, d_sh3, d_s3, d_g3], axis=1)
    return loss_row, dx, d_mod


WEIGHT_NAMES = ("w_ada", "b_ada", "ffn1_w_in", "ffn1_w_out", "w_in", "gdn_conv_w", "gdn_a_log", "gdn_dt_bias", "gdn_norm_w",
                "mla_q_norm_w", "mla_w_uq", "mla_kv_norm_w", "mla_w_ukv", "qkn_q_nope", "qkn_q_rope", "qkn_k_nope",
                "qkn_k_rope", "mla_out_norm_w", "w_out", "ffn2_w_in", "ffn2_w_out")
FFN_SHARDED = ("ffn1_w_in", "ffn1_w_out", "ffn2_w_in", "ffn2_w_out")
TRANSPOSED_ENTRY = ("ffn1_w_in", "ffn2_w_in", "w_in", "mla_w_uq")
SHEETED = (("w_in", "col"), ("gdn_conv_w", "col"), ("mla_w_uq", "col"), ("mla_w_ukv", "col"), ("w_out", "row"))
MOD_ROWS = N_MOD * D_MODEL // 128
SMALL = {"gdn_a_log": (MOD_ROWS, 1, 64, 4), "gdn_dt_bias": (MOD_ROWS + 1, 1, 64, 4), "gdn_norm_w": (MOD_ROWS + 2, 1, 0, 128),
         "mla_q_norm_w": (MOD_ROWS + 3, 3, 0, 384), "mla_kv_norm_w": (MOD_ROWS + 6, 2, 0, 256),
         "qkn_q_nope": (MOD_ROWS + 8, 1, 0, 128), "qkn_q_rope": (MOD_ROWS + 9, 1, 0, 64), "qkn_k_nope": (MOD_ROWS + 10, 1, 0, 128),
         "qkn_k_rope": (MOD_ROWS + 11, 1, 0, 64), "mla_out_norm_w": (MOD_ROWS + 12, 1, 0, 128)}
LOSS_ROW = MOD_ROWS + 13
CONV_ROW, CONV_ROWS = 88, 4 * 1536 // 128
SHEET_ROWS = CONV_ROW + CONV_ROWS


def _to_sheet(flat, dtype, sublanes):
    n = flat.shape[-1]
    unit = sublanes * 128
    pad = (-n) % unit
    flat = jnp.pad(flat.astype(dtype), [(0, 0)] * (flat.ndim - 1) + [(0, pad)])
    return flat.reshape(flat.shape[:-1] + ((n + pad) // 128, 128))


def _small_sheet(b_like, small):
    sheet = jnp.zeros((SHEET_ROWS, 128), F32).at[:MOD_ROWS].set(b_like.reshape(MOD_ROWS, 128))
    for name, (row, rows, lane, n) in SMALL.items():
        v = small[name].reshape(1, n)
        if rows == 1:
            sheet = sheet.at[row, lane:lane + n].set(v[0])
        else:
            sheet = sheet.at[row:row + rows].set(v.reshape(rows, 128))
    return sheet


def _from_small_sheet(sheet):
    out = {"b_ada": sheet[:MOD_ROWS].reshape(1, N_MOD * D_MODEL)}
    for name, (row, rows, lane, n) in SMALL.items():
        out[name] = sheet[row, lane:lane + n].reshape(1, n) if rows == 1 else sheet[row:row + rows].reshape(1, n)
    return out


def kernel(x, c, positions, w_ada, b_ada, ffn1_w_in, ffn1_w_out, w_in, gdn_conv_w, gdn_a_log, gdn_dt_bias, gdn_norm_w, mla_q_norm_w, mla_w_uq, mla_kv_norm_w, mla_w_ukv, qkn_q_nope, qkn_q_rope, qkn_k_nope, qkn_k_rope, mla_out_norm_w, w_out, ffn2_w_in, ffn2_w_out, loss_target, m_w_ada, m_b_ada, m_ffn1_w_in, m_ffn1_w_out, m_w_in, m_gdn_conv_w, m_gdn_a_log, m_gdn_dt_bias, m_gdn_norm_w, m_mla_q_norm_w, m_mla_w_uq, m_mla_kv_norm_w, m_mla_w_ukv, m_qkn_q_nope, m_qkn_q_rope, m_qkn_k_nope, m_qkn_k_rope, m_mla_out_norm_w, m_w_out, m_ffn2_w_in, m_ffn2_w_out, v_w_ada, v_b_ada, v_ffn1_w_in, v_ffn1_w_out, v_w_in, v_gdn_conv_w, v_gdn_a_log, v_gdn_dt_bias, v_gdn_norm_w, v_mla_q_norm_w, v_mla_w_uq, v_mla_kv_norm_w, v_mla_w_ukv, v_qkn_q_nope, v_qkn_q_rope, v_qkn_k_nope, v_qkn_k_rope, v_mla_out_norm_w, v_w_out, v_ffn2_w_in, v_ffn2_w_out):
    args = locals()
    w = {n: args[n] for n in WEIGHT_NAMES}
    m = {n: args["m_" + n] for n in WEIGHT_NAMES}
    v = {n: args["v_" + n] for n in WEIGHT_NAMES}
    me = 4 * lax.axis_index("x") + 2 * lax.axis_index("y") + lax.axis_index("c")
    cols = N_MOD * D_MODEL // N_DEV
    shard = {n: w[n][0] for n in FFN_SHARDED + tuple(s[0] for s in SHEETED)}

    sc = c * _sigmoid(c)
    first = _to_sheet(jnp.concatenate([sc.reshape(-1), shard["gdn_conv_w"].reshape(-1)]), F32, 8)
    (first_all,) = _all_gather([first], "gather_c")
    sc_all = first_all[:, :D_MODEL // 128].reshape(N_DEV, D_MODEL)
    n_taps = shard["gdn_conv_w"].size
    conv_all = first_all.reshape(N_DEV, -1)[:, D_MODEL:D_MODEL + n_taps].reshape(N_DEV, 4, -1)
    b_mine = lax.dynamic_slice(b_ada, (0, me * cols), (1, cols))
    mod_cols = _mm(sc_all, w_ada[0], "nn", name="ada_mod", extra_params=[b_mine], epi=lambda acc, b_: (acc + b_,))
    (mod_all,) = _all_to_all([_to_sheet(mod_cols, F32, 8)], "scatter_mod")
    mod = mod_all.reshape(N_DEV, -1)[:, :cols].reshape(1, N_MOD * D_MODEL)

    f1_shards, mod = lax.optimization_barrier(([shard["ffn1_w_in"].astype(BF16), shard["ffn1_w_out"].astype(BF16)], mod))
    f1_w8, f1_out = _all_gather(f1_shards, "gather_w1")
    travel = [s for s in SHEETED if s[0] != "gdn_conv_w"]
    tied = lax.optimization_barrier(([shard[n].astype(BF16) for n, _ in travel], f1_w8))
    f1_w8 = tied[1]
    mixer_w = _exchange_start(tied[0], False, "gather_wm_start")
    ffn2_w = _exchange_start([shard["ffn2_w_in"].astype(BF16) + mixer_w[4][0:1, 0:1].astype(BF16),
                              shard["ffn2_w_out"].astype(BF16)], False, "gather_w2_start")
    mod = mod + ffn2_w[4][0:1, 0:1]
    W1 = dict(f1_w8=f1_w8, f1_wo4=f1_out.reshape(HID_PIECES, FFN_PIECE, D_MODEL))

    def mixer_weights(after):
        got = _exchange_wait(mixer_w, False, after, "gather_wm_wait")
        P = {n: jnp.concatenate(list(g), axis=1) if kind == "col" else g.reshape(-1, g.shape[-1])
             for (n, kind), g in zip(travel, got)}
        P["gdn_conv_w"] = jnp.concatenate(list(conv_all), axis=1)
        for n in SMALL:
            P[n] = w[n]
        return _pack_weights(P)

    def ffn2_weights(after):
        f2_w8, f2_out = _exchange_wait(ffn2_w, False, after, "gather_w2_wait")
        return dict(f2_w8=f2_w8, f2_wo4=f2_out.reshape(HID_PIECES, FFN_PIECE, D_MODEL))

    pending, small_grads = {}, {}

    def ffn_grad_ready(tag):
        def ready(which, g):
            pieces = g if which == "w8" else g.reshape((N_DEV,) + shard["ffn1_w_out"].shape)
            pending[tag + which] = _exchange_start([pieces], True, "scatter_%s_%s_start" % (tag, which))
            return pending[tag + which][4]
        return ready

    def mixer_grads_ready(G):
        g_full = _unpack_grads(G)
        small_grads.update({n: g_full[n] for n in SMALL})
        small_grads["gdn_conv_w"] = g_full["gdn_conv_w"]
        pieces = []
        for n, kind in travel:
            r, cc = shard[n].shape
            g = g_full[n].astype(BF16)
            pieces.append(jnp.stack([g[:, cc * p:cc * (p + 1)] for p in range(N_DEV)]) if kind == "col"
                          else g.reshape(N_DEV, r, cc))
        pending["mixer"] = _exchange_start(pieces, True, "scatter_mx_start")
        return pending["mixer"][4][0:1, 0:1]

    cos_p, sin_p = _rope_tables(positions[0])
    loss_row, dx, d_mod = _local_step(x[0], loss_target[0], mod, cos_p, sin_p, W1, mixer_weights, ffn2_weights,
                                      ffn_grad_ready, mixer_grads_ready)

    sheet = _small_sheet(d_mod, small_grads).at[LOSS_ROW].set(loss_row[0, :128])
    sheet = sheet.at[CONV_ROW:CONV_ROW + CONV_ROWS].set(small_grads["gdn_conv_w"].reshape(CONV_ROWS, 128))
    (sheets,) = _all_gather([sheet], "gather_small")
    summed = _sum_devices(sheets, "sum_small")
    d_mod_all = sheets[:, :MOD_ROWS].reshape(N_DEV, N_MOD * D_MODEL)
    d_mod_mine = lax.dynamic_slice(d_mod_all, (0, me * cols), (N_DEV, cols))
    grads = _from_small_sheet(summed)
    grads["w_ada"] = _mm(sc_all, d_mod_mine, "tn", name="ada_gw", hi=True)
    conv_taps = shard["gdn_conv_w"].shape[1]
    grads["gdn_conv_w"] = lax.dynamic_slice(summed[CONV_ROW:CONV_ROW + CONV_ROWS].reshape(4, -1), (0, me * conv_taps),
                                            (4, conv_taps))
    loss = summed[LOSS_ROW, 0]

    delta, new_m, new_v = {}, {}, {}
    arrived = {}
    for n, key in zip(FFN_SHARDED, ("f1w8", "f1wo4", "f2w8", "f2wo4")):
        (arrived[n],) = _exchange_wait(pending[key], True, summed, "scatter_%s_wait" % key)
    arrived.update(zip([n for n, _ in travel], _exchange_wait(pending["mixer"], True, summed, "scatter_mx_wait")))
    for n, parts in arrived.items():
        if n in TRANSPOSED_ENTRY:
            res = _sum_adamw(parts, w[n][0].T, m[n][0].T, v[n][0].T, "adamw_" + n, transposed=True)
            grads[n], delta[n], new_m[n], new_v[n] = [r.T for r in res]
        else:
            grads[n], delta[n], new_m[n], new_v[n] = _sum_adamw(parts, w[n][0], m[n][0], v[n][0], "adamw_" + n)
    for n in ("w_ada", "gdn_conv_w"):
        delta[n], new_m[n], new_v[n] = _adamw(w[n][0], grads[n], m[n][0], v[n][0], "adamw_" + n)
    small_in = [_small_sheet(t["b_ada"], t) for t in (w, grads, m, v)]
    for res, out in zip(_adamw(*small_in, "adamw_small"), (delta, new_m, new_v)):
        out.update(_from_small_sheet(res))

    def shaped(d):
        return [d[n].reshape(w[n].shape) for n in WEIGHT_NAMES]

    return (loss, dx[None], *shaped(grads), *shaped(delta), *shaped(new_m), *shaped(new_v))
```

```python
import functools

import jax
import jax.numpy as jnp
import numpy as np
from jax import lax
from jax.experimental import pallas as pl
from jax.experimental.pallas import tpu as pltpu

F32 = jnp.float32
BF16 = jnp.bfloat16

D_MODEL = 1024
D_FF = 2816
N_MOD = 9
HEADS = 4
HEAD_DIM = 128
CHUNK = 64
EPS = 1e-6
ROPE = 64
Q_LORA = 384
KV_LORA = 256
N_IN = 2760
N_IN_PACKED = 2816
ROPE_BASE = 10000.0
LOG2_E = 1.4426950408889634
N_DEV = 8

ADAM_LR = 0.001
ADAM_B1 = 0.9
ADAM_B2 = 0.999
ADAM_EPS = 1e-08
ADAM_WD = 0.01
ADAM_STEP = 10

VMEM_LIMIT_BYTES = 56 * 1024 * 1024
MATMUL_ROWS = (1024, 512, 256, 128)
MESH = pl.DeviceIdType.MESH


def _params(sem=None):
    return pltpu.CompilerParams(dimension_semantics=sem, vmem_limit_bytes=VMEM_LIMIT_BYTES)


def _pick(dim, prefs):
    for p in prefs:
        if dim % p == 0:
            return p
    return dim


_DIMS = {"nn": (((1,), (0,)), ((), ())), "nt": (((1,), (1,)), ((), ())), "tn": (((0,), (0,)), ((), ()))}


def _dot_raw(a, b, mode):
    return lax.dot_general(a.astype(BF16), b.astype(BF16), _DIMS[mode], preferred_element_type=F32)


def _dot_hi(a, b, mode="nn"):
    return lax.dot_general(a, b, _DIMS[mode], precision=lax.Precision.HIGHEST, preferred_element_type=F32)


@functools.partial(jax.custom_vjp, nondiff_argnums=(2,))
def _bdot(a, b, mode):
    return _dot_raw(a, b, mode)


def _bdot_fwd(a, b, mode):
    return _dot_raw(a, b, mode), (a, b)


def _bdot_bwd(mode, res, g):
    a, b = res
    if mode == "nn":
        return _dot_raw(g, b, "nt"), _dot_raw(a, g, "tn")
    if mode == "nt":
        return _dot_raw(g, b, "nn"), _dot_raw(g, a, "tn")
    return _dot_raw(b, g, "nt"), _dot_raw(a, g, "nn")


_bdot.defvjp(_bdot_fwd, _bdot_bwd)


def _mm(a, b, mode, *, name, out_dtypes=(F32,), epi=None, extras=(), extra_params=(), hi=False,
        tm=None, tn=None, tk=None):
    if mode == "nn":
        (M, K), (_, N) = a.shape, b.shape
    elif mode == "nt":
        (M, K), (N, _) = a.shape, b.shape
    else:
        (K, M), (_, N) = a.shape, b.shape
    tm = tm or _pick(M, (512, 1408, 256, 128) if mode == "tn" else MATMUL_ROWS + (384, 352))
    tn = tn or _pick(N, (1024, 1408, 768, 512, 384, 256, 128))
    tk = tk or _pick(K, (1024, 1408, 512, 384, 256, 128))
    a_spec = {"nn": pl.BlockSpec((tm, tk), lambda i, j, k: (i, k)), "nt": pl.BlockSpec((tm, tk), lambda i, j, k: (i, k)),
              "tn": pl.BlockSpec((tk, tm), lambda i, j, k: (k, i))}[mode]
    b_spec = {"nn": pl.BlockSpec((tk, tn), lambda i, j, k: (k, j)), "nt": pl.BlockSpec((tn, tk), lambda i, j, k: (j, k)),
              "tn": pl.BlockSpec((tk, tn), lambda i, j, k: (k, j))}[mode]
    mn_spec = pl.BlockSpec((tm, tn), lambda i, j, k: (i, j))
    return _mmg(a, b, mode, name=name, grid=(M // tm, N // tn, K // tk), a_spec=a_spec, b_spec=b_spec, out_spec=mn_spec,
                out_shapes=[jax.ShapeDtypeStruct((M, N), dt) for dt in out_dtypes], acc_shape=(tm, tn), epi=epi,
                extras=list(extras) + list(extra_params),
                extra_specs=[mn_spec] * len(extras) + [pl.BlockSpec((1, tn), lambda i, j, k: (0, j))] * len(extra_params),
                hi=hi)


def _mmg(a, b, mode, *, name, grid, a_spec, b_spec, out_spec, out_shapes, acc_shape, epi=None, extras=(),
         extra_specs=(), hi=False):
    nk = grid[2]
    n_e, n_o = len(extras), len(out_shapes)

    def body(*refs):
        a_ref, b_ref = refs[:2]
        e_refs = refs[2:2 + n_e]
        o_refs = refs[2 + n_e:2 + n_e + n_o]
        acc_ref = refs[-1]
        k = pl.program_id(2)

        @pl.when(k == 0)
        def _():
            acc_ref[...] = jnp.zeros_like(acc_ref)

        if hi:
            acc_ref[...] += _dot_hi(a_ref[...].astype(F32), b_ref[...].astype(F32), mode)
        else:
            acc_ref[...] += _dot_raw(a_ref[...], b_ref[...], mode)

        @pl.when(k == nk - 1)
        def _():
            acc = acc_ref[...]
            outs = (acc,) if epi is None else epi(acc, *[e[...].astype(F32) for e in e_refs])
            for o_ref, o in zip(o_refs, outs):
                o_ref[...] = o.astype(o_ref.dtype)

    outs = pl.pallas_call(
        body, name=name, grid=grid,
        in_specs=[a_spec, b_spec] + list(extra_specs),
        out_specs=[out_spec] * n_o,
        out_shape=list(out_shapes),
        scratch_shapes=[pltpu.VMEM(acc_shape, F32)],
        compiler_params=_params(("parallel", "parallel", "arbitrary")),
    )(a, b, *extras)
    return outs if n_o > 1 else outs[0]


def _row_spec(th, cw, ci):
    return pl.BlockSpec((th, cw), lambda i: (i, ci))


def _full_spec(shape):
    return pl.BlockSpec(shape, lambda i: (0,) * len(shape))


def _rowwise(fn, rows, params, outs, n_steps, name):
    n_r, n_p, n_o = len(rows), len(params), len(outs)

    def body(*refs):
        vals = [r[...].astype(F32) for r in refs[:n_r + n_p]]
        res = fn(*vals)
        for o_ref, o in zip(refs[n_r + n_p:], res):
            o_ref[...] = o.astype(o_ref.dtype)

    across = [len(o) == 4 for o in outs]
    res = pl.pallas_call(
        body, name=name, grid=(n_steps,),
        in_specs=[_row_spec(th, cw, ci) for (_, th, cw, ci) in rows] + [_full_spec(p.shape) for p in params],
        out_specs=[pl.BlockSpec((o[0], o[1]), lambda i: (0, i)) if ac else _row_spec(o[0], o[1], 0)
                   for o, ac in zip(outs, across)],
        out_shape=[jax.ShapeDtypeStruct((o[0], n_steps * o[1]) if ac else (n_steps * o[0], o[1]), o[2])
                   for o, ac in zip(outs, across)],
        compiler_params=_params(("parallel",)),
    )(*[r[0] for r in rows], *params)
    return res


def _rowwise_bwd(fn, rows, aux, params, douts, n_steps, name, row_dtypes=None, adds=()):
    n_r, n_a, n_p, n_d, n_add = len(rows), len(aux), len(params), len(douts), len(adds)
    row_dtypes = row_dtypes or (F32,) * n_r

    def body(*refs):
        it = iter(refs)
        r_vals = [next(it)[...].astype(F32) for _ in range(n_r)]
        a_vals = [next(it)[...].astype(F32) for _ in range(n_a)]
        p_vals = [next(it)[...].astype(F32) for _ in range(n_p)]
        d_vals = [next(it)[...].astype(F32) for _ in range(n_d)]
        add_vals = [next(it)[...].astype(F32) for _ in range(n_add)]
        dr_refs = [next(it) for _ in range(n_r)]
        dp_refs = [next(it) for _ in range(n_p)]

        def f(*rp):
            return tuple(fn(*rp[:n_r], *a_vals, *rp[n_r:]))

        _, vjp = jax.vjp(f, *r_vals, *p_vals)
        grads = list(vjp(tuple(d_vals)))
        for (ri, _), av in zip(adds, add_vals):
            grads[ri] = grads[ri] + av
        for dr_ref, g in zip(dr_refs, grads[:n_r]):
            dr_ref[...] = g.astype(dr_ref.dtype)

        @pl.when(pl.program_id(0) == 0)
        def _():
            for dp_ref in dp_refs:
                dp_ref[...] = jnp.zeros_like(dp_ref)

        for dp_ref, g in zip(dp_refs, grads[n_r:]):
            dp_ref[...] += g

    all_rows = list(rows) + list(aux) + list(douts) + [(arr,) + tuple(rows[ri][1:3]) + (0,) for ri, arr in adds]
    in_specs = ([_row_spec(th, cw, ci) for (_, th, cw, ci) in list(rows) + list(aux)]
                + [_full_spec(p.shape) for p in params]
                + [_row_spec(th, cw, ci) for (_, th, cw, ci) in all_rows[n_r + n_a:]])
    res = pl.pallas_call(
        body, name=name, grid=(n_steps,),
        in_specs=in_specs,
        out_specs=[_row_spec(th, cw, 0) for (_, th, cw, _) in rows] + [_full_spec(p.shape) for p in params],
        out_shape=[jax.ShapeDtypeStruct((n_steps * th, cw), dt) for (_, th, cw, _), dt in zip(rows, row_dtypes)]
        + [jax.ShapeDtypeStruct(p.shape, F32) for p in params],
        compiler_params=_params(("arbitrary",)),
    )(*[r[0] for r in list(rows) + list(aux)], *params, *[r[0] for r in all_rows[n_r + n_a:]])
    return res[:n_r], res[n_r:]


def _sigmoid(x):
    return lax.logistic(x)


def _silu(x):
    return x * _sigmoid(x)


def _rms(x, w=None, n=None):
    n = n or x.shape[-1]
    y = x * lax.rsqrt(jnp.sum(x * x, axis=-1, keepdims=True) * (1.0 / n) + EPS)
    return y if w is None else y * w


def _modulate(x, scale, shift):
    return _rms(x) * (1.0 + scale) + shift


def _softplus(x):
    return jnp.maximum(x, 0.0) + jnp.log1p(jnp.exp(-jnp.abs(x)))


@jax.custom_vjp
def _rot_half64(x):
    lane = lax.broadcasted_iota(jnp.int32, x.shape, 1)
    up = pltpu.roll(x, 96, 1)
    down = pltpu.roll(x, 32, 1)
    return jnp.where(lane < 32, up, jnp.where(lane < 64, down, 0.0))


_rot_half64.defvjp(lambda x: (_rot_half64(x), None), lambda _, g: (_rot_half64(g),))


def _rope128(x, cos_p, sin_p):
    return x * cos_p + _rot_half64(x) * sin_p


def _gdn_pre_fn(qkvc, kab, alog_p, dt_p):
    a = _silu(qkvc)
    qs, ks = [], []
    for h in range(HEADS):
        qh = a[:, HEAD_DIM * h:HEAD_DIM * (h + 1)]
        kh = a[:, 512 + HEAD_DIM * h:512 + HEAD_DIM * (h + 1)]
        qs.append(qh * lax.rsqrt(jnp.sum(qh * qh, axis=-1, keepdims=True) + EPS) * (HEAD_DIM ** -0.5))
        ks.append(kh * lax.rsqrt(jnp.sum(kh * kh, axis=-1, keepdims=True) + EPS))
    lane = lax.broadcasted_iota(jnp.int32, kab.shape, 1)
    g_full = -jnp.exp(alog_p) * _softplus(kab + dt_p)
    b_full = _sigmoid(kab)
    gb = jnp.where((lane >= 64) & (lane < 68), g_full, jnp.where((lane >= 68) & (lane < 72), b_full, 0.0))
    return jnp.concatenate(qs, axis=1), jnp.concatenate(ks, axis=1), a[:, 1024:1536], gb


INTRA_ROWS = (256, 128, 64)

_BNN = (((2,), (1,)), ((0,), (0,)))
_BNT = (((2,), (2,)), ((0,), (0,)))


def _split_bf16(a):
    hi = a.astype(BF16)
    return hi, (a - hi.astype(F32)).astype(BF16)


def _dot3_raw(a, b, dims):
    a_hi, a_lo = _split_bf16(a)
    b_hi, b_lo = _split_bf16(b)
    dot = lambda x_, y_: lax.dot_general(x_, y_, dims, preferred_element_type=F32)
    return dot(a_hi, b_hi) + (dot(a_hi, b_lo) + dot(a_lo, b_hi))


@functools.partial(jax.custom_vjp, nondiff_argnums=(2, 3))
def _dot3(a, b, nt, exact_bwd=True):
    return _dot3_raw(a, b, _BNT if nt else _BNN)


def _dot3_fwd(a, b, nt, exact_bwd):
    return _dot3_raw(a, b, _BNT if nt else _BNN), (a, b)


def _dot3_bwd(nt, exact_bwd, res, g):
    a, b = res
    if exact_bwd:
        dot = _dot3_raw
    else:
        dot = lambda x_, y_, d_: lax.dot_general(x_.astype(BF16), y_.astype(BF16), d_, preferred_element_type=F32)
    if nt:
        return dot(g, b, _BNN), dot(jnp.swapaxes(g, 1, 2), a, _BNN)
    return dot(g, b, _BNT), dot(jnp.swapaxes(a, 1, 2), g, _BNN)


_dot3.defvjp(_dot3_fwd, _dot3_bwd)


@functools.partial(jax.custom_vjp, nondiff_argnums=(2,))
def _bdot_b(a, b, nt):
    return lax.dot_general(a.astype(BF16), b.astype(BF16), _BNT if nt else _BNN, preferred_element_type=F32)


def _bdot_b_fwd(a, b, nt):
    return _bdot_b(a, b, nt), (a, b)


def _bdot_b_bwd(nt, res, g):
    a, b = res
    dot = lambda x_, y_, d_: lax.dot_general(x_.astype(BF16), y_.astype(BF16), d_, preferred_element_type=F32)
    if nt:
        return dot(g, b, _BNN), dot(jnp.swapaxes(g, 1, 2), a, _BNN)
    return dot(g, b, _BNT), dot(jnp.swapaxes(a, 1, 2), g, _BNN)


_bdot_b.defvjp(_bdot_b_fwd, _bdot_b_bwd)


@jax.custom_vjp
def _inverse_given(a_mat, inv):
    return inv


def _inverse_given_bwd(inv, g):
    inv_t = jnp.swapaxes(inv, 1, 2)
    return -_dot3_raw(_dot3_raw(inv_t, g, _BNN), inv_t, _BNN), jnp.zeros_like(inv)


_inverse_given.defvjp(lambda a_mat, inv: (inv, inv), _inverse_given_bwd)


def _intra_batched(q, k, v, g_col, b_col, inv_known=None):
    c = CHUNK
    nb = q.shape[0]
    row = lax.broadcasted_iota(jnp.int32, (1, c, c), 1)
    col = lax.broadcasted_iota(jnp.int32, (1, c, c), 2)
    incl, strict, eye = row >= col, row > col, row == col
    tri = jnp.broadcast_to(jnp.where(incl, 1.0, 0.0).astype(F32), (nb, c, c))
    ident = jnp.where(eye, 1.0, 0.0).astype(F32)
    g_wide = _dot3(tri, jnp.broadcast_to(g_col, (nb, c, HEAD_DIM)), False)
    g_i = g_wide[:, :, :c]
    g_j = jnp.sum(jnp.where(eye, g_i, 0.0), axis=1, keepdims=True)
    decay = jnp.where(incl, jnp.exp(jnp.where(incl, g_i - g_j, 0.0)), 0.0)
    kk = _bdot_b(k, k, True)
    a_mat = jnp.where(strict, b_col * kk * decay, 0.0)
    if inv_known is None:
        x_pow = -a_mat
        inv = ident + x_pow
        for _ in range(5):
            x_pow = _dot3(x_pow, x_pow, False, False)
            inv = inv + _dot3(inv, x_pow, False, False)
    else:
        inv = _inverse_given(a_mat, inv_known)
    e_wide = jnp.exp(g_wide)
    u = _dot3(inv, v * b_col, False)
    wk = _dot3(inv, k * b_col * e_wide, False)
    qk = _bdot_b(q, k, True) * decay
    last = lax.broadcasted_iota(jnp.int32, (1, c, HEAD_DIM), 1) == c - 1
    g_last = jnp.sum(jnp.where(last, g_wide, 0.0), axis=1, keepdims=True)
    qd = q * e_wide
    kd = k * jnp.exp(g_last - g_wide)
    gl = jnp.broadcast_to(jnp.exp(g_last), (nb, 8, HEAD_DIM))
    return u, wk, qd, kd, qk, gl, inv


def _gdn_intra_fn(q, k, v, gb, *inv_known):
    t = q.shape[0]
    nch = t // CHUNK
    lane = lax.broadcasted_iota(jnp.int32, gb.shape, 1)

    def heads_first(x_):
        return jnp.concatenate([x_[:, HEAD_DIM * h:HEAD_DIM * (h + 1)].reshape(nch, CHUNK, HEAD_DIM) for h in range(HEADS)],
                               axis=0)

    def column(first_lane):
        return jnp.concatenate([jnp.sum(jnp.where(lane == first_lane + h, gb, 0.0), axis=1, keepdims=True)
                                .reshape(nch, CHUNK, 1) for h in range(HEADS)], axis=0)

    known = jnp.concatenate([x_.reshape(nch, CHUNK, CHUNK) for x_ in inv_known], axis=0) if inv_known else None
    u, wk, qd, kd, qk, gl, inv = _intra_batched(heads_first(q), heads_first(k), heads_first(v), column(64), column(68), known)

    def rows_first(x_):
        r, w_ = x_.shape[1], x_.shape[2]
        return jnp.concatenate([x_[nch * h:nch * (h + 1)].reshape(nch * r, w_) for h in range(HEADS)], axis=1)

    per_head = lambda x_: [x_[nch * h:nch * (h + 1)].reshape(t, CHUNK) for h in range(HEADS)]
    outs = (rows_first(u), rows_first(wk), rows_first(qd), rows_first(kd), *per_head(qk), rows_first(gl))
    return outs if inv_known else outs + tuple(per_head(inv))


def _scan_step(s0, u, wk, qd, kd, qk, gl):
    v_new = u - _bdot_b(wk, s0, False)
    o = _bdot_b(qd, s0, False) + _bdot_b(qk, v_new, False)
    s1 = s0 * gl[:, 0:1, :] + _bdot_b(jnp.swapaxes(kd, 1, 2), v_new, False)
    return o, s1


def _mix_post_fn(o_a, z, o_b, gnw, onw):
    parts = [_rms(o_a[:, HEAD_DIM * h:HEAD_DIM * (h + 1)], gnw) * _silu(z[:, HEAD_DIM * h:HEAD_DIM * (h + 1)])
             for h in range(HEADS)]
    parts += [_rms(o_b[:, HEAD_DIM * h:HEAD_DIM * (h + 1)], onw) for h in range(HEADS)]
    return (jnp.concatenate(parts, axis=1),)


def _mla_pre_fn(ckv, cq, kab, cos_p, sin_p, qnw, kvnw, wuq, wukv, qn_w, qr_w, kn_w, kr_w):
    scale = (HEAD_DIM + ROPE) ** -0.5 * LOG2_E
    qf = _bdot(_rms(cq, qnw), wuq, "nn")
    kvf = _bdot(_rms(ckv, kvnw), wukv, "nn")
    lane = lax.broadcasted_iota(jnp.int32, kab.shape, 1)
    kr = _rope128(_rms(jnp.where(lane < ROPE, kab, 0.0), kr_w, n=ROPE), cos_p, sin_p)
    qs, ks = [], []
    for h in range(HEADS):
        qn = _rms(qf[:, 256 * h:256 * h + 128], qn_w) * scale
        qr = _rope128(_rms(qf[:, 256 * h + 128:256 * h + 256], qr_w, n=ROPE), cos_p, sin_p) * scale
        qs += [qn, qr]
        ks += [_rms(kvf[:, 128 * h:128 * (h + 1)], kn_w), kr]
    return jnp.concatenate(qs, axis=1), jnp.concatenate(ks, axis=1), kvf[:, 512:]


def _conv_fwd(proj, conv_w, tm, name):
    S = proj.shape[0]
    C = 1536
    nb = tm // 8

    def body(x_ref, prev_ref, w_ref, o_ref, ext_ref):
        i = pl.program_id(0)
        ext_ref[0:8, :] = jnp.where(i > 0, prev_ref[...], 0.0)
        ext_ref[8:, :] = x_ref[...]
        acc = jnp.zeros((tm, C), F32)
        for k in range(4):
            acc = acc + w_ref[k:k + 1, :] * ext_ref[pl.ds(5 + k, tm), :]
        o_ref[...] = acc

    return pl.pallas_call(
        body, name=name, grid=(S // tm,),
        in_specs=[pl.BlockSpec((tm, C), lambda i: (i, 0)),
                  pl.BlockSpec((8, C), lambda i: (jnp.maximum(i * nb - 1, 0), 0)),
                  pl.BlockSpec((4, C), lambda i: (0, 0))],
        out_specs=pl.BlockSpec((tm, C), lambda i: (i, 0)),
        out_shape=jax.ShapeDtypeStruct((S, C), F32),
        scratch_shapes=[pltpu.VMEM((tm + 8, C), F32)],
        compiler_params=_params(("arbitrary",)),
    )(proj, proj, conv_w)


def _conv_bwd(proj, dout, conv_w, tm, name):
    S = proj.shape[0]
    C = 1536
    nb = tm // 8
    n_steps = S // tm

    def body(x_ref, prev_ref, d_ref, next_ref, w_ref, dx_ref, dw_ref, xext_ref, dext_ref):
        i = pl.program_id(0)
        xext_ref[0:8, :] = jnp.where(i > 0, prev_ref[...], 0.0)
        xext_ref[8:, :] = x_ref[...]
        dext_ref[0:tm, :] = d_ref[...]
        dext_ref[tm:, :] = jnp.where(i < n_steps - 1, next_ref[...], 0.0)
        d = d_ref[...]
        acc = jnp.zeros((tm, C), F32)
        dws = []
        for k in range(4):
            acc = acc + w_ref[k:k + 1, :] * dext_ref[pl.ds(3 - k, tm), :]
            dws.append(jnp.sum(d * xext_ref[pl.ds(5 + k, tm), :], axis=0, keepdims=True))
        dx_ref[...] = acc.astype(dx_ref.dtype)

        @pl.when(i == 0)
        def _():
            dw_ref[...] = jnp.zeros_like(dw_ref)

        dw_ref[...] += jnp.concatenate(dws + [jnp.zeros((4, C), F32)], axis=0)

    return pl.pallas_call(
        body, name=name, grid=(n_steps,),
        in_specs=[pl.BlockSpec((tm, C), lambda i: (i, 0)),
                  pl.BlockSpec((8, C), lambda i: (jnp.maximum(i * nb - 1, 0), 0)),
                  pl.BlockSpec((tm, C), lambda i: (i, 0)),
                  pl.BlockSpec((8, C), lambda i: (jnp.minimum((i + 1) * nb, S // 8 - 1), 0)),
                  pl.BlockSpec((4, C), lambda i: (0, 0))],
        out_specs=[pl.BlockSpec((tm, C), lambda i: (i, 0)), pl.BlockSpec((8, C), lambda i: (0, 0))],
        out_shape=[jax.ShapeDtypeStruct((S, C), BF16), jax.ShapeDtypeStruct((8, C), F32)],
        scratch_shapes=[pltpu.VMEM((tm + 8, C), F32), pltpu.VMEM((tm + 8, C), F32)],
        compiler_params=_params(("arbitrary",)),
    )(proj, proj, dout, dout, conv_w)


SCAN_CHUNKS = (4, 2, 1)


def _gdn_scan_fwd(u, wk, qd, kd, qks, gl, name):
    S = u.shape[0]
    nc = S // CHUNK
    cs = _pick(nc, SCAN_CHUNKS)
    W = HEADS * HEAD_DIM

    def body(u_ref, wk_ref, qd_ref, kd_ref, qk0, qk1, qk2, qk3, gl_ref, o_ref, sp_ref, s_ref):
        @pl.when(pl.program_id(0) == 0)
        def _():
            s_ref[...] = jnp.zeros_like(s_ref)

        state = s_ref[...]
        for c in range(cs):
            rows, gl_rows = slice(CHUNK * c, CHUNK * (c + 1)), slice(8 * c, 8 * (c + 1))
            sp_ref[c] = state
            o, state = _scan_step(state, _heads(u_ref, HEAD_DIM, rows), _heads(wk_ref, HEAD_DIM, rows),
                                  _heads(qd_ref, HEAD_DIM, rows), _heads(kd_ref, HEAD_DIM, rows),
                                  jnp.stack([r[rows, :] for r in (qk0, qk1, qk2, qk3)]), _heads(gl_ref, HEAD_DIM, gl_rows))
            for h in range(HEADS):
                o_ref[rows, HEAD_DIM * h:HEAD_DIM * (h + 1)] = o[h]
        s_ref[...] = state

    row = pl.BlockSpec((cs * CHUNK, W), lambda n: (n, 0))
    qk_spec = pl.BlockSpec((cs * CHUNK, CHUNK), lambda n: (n, 0))
    return pl.pallas_call(
        body, name=name, grid=(nc // cs,),
        in_specs=[row, row, row, row, qk_spec, qk_spec, qk_spec, qk_spec, pl.BlockSpec((cs * 8, W), lambda n: (n, 0))],
        out_specs=[row, pl.BlockSpec((cs, HEADS, HEAD_DIM, HEAD_DIM), lambda n: (n, 0, 0, 0))],
        out_shape=[jax.ShapeDtypeStruct((S, W), F32), jax.ShapeDtypeStruct((nc, HEADS, HEAD_DIM, HEAD_DIM), F32)],
        scratch_shapes=[pltpu.VMEM((HEADS, HEAD_DIM, HEAD_DIM), F32)],
        compiler_params=_params(("arbitrary",)),
    )(u, wk, qd, kd, *qks, gl)


def _gdn_scan_bwd(u, wk, qd, kd, qks, gl, s_prev, d_o, name):
    S = u.shape[0]
    nc = S // CHUNK
    cs = _pick(nc, SCAN_CHUNKS)
    nb = nc // cs
    W = HEADS * HEAD_DIM

    def body(u_ref, wk_ref, qd_ref, kd_ref, qk0, qk1, qk2, qk3, gl_ref, sp_ref, do_ref,
             du_ref, dwk_ref, dqd_ref, dkd_ref, dqk0, dqk1, dqk2, dqk3, dgl_ref, ds_ref):
        @pl.when(pl.program_id(0) == 0)
        def _():
            ds_ref[...] = jnp.zeros_like(ds_ref)

        d_state = ds_ref[...]
        for c in reversed(range(cs)):
            rows, gl_rows = slice(CHUNK * c, CHUNK * (c + 1)), slice(8 * c, 8 * (c + 1))
            _, vjp = jax.vjp(_scan_step, sp_ref[c], _heads(u_ref, HEAD_DIM, rows), _heads(wk_ref, HEAD_DIM, rows),
                             _heads(qd_ref, HEAD_DIM, rows), _heads(kd_ref, HEAD_DIM, rows),
                             jnp.stack([r[rows, :] for r in (qk0, qk1, qk2, qk3)]), _heads(gl_ref, HEAD_DIM, gl_rows))
            d_state, du, dwk, dqd, dkd, dqk, dgl = vjp((_heads(do_ref, HEAD_DIM, rows), d_state))
            for h, dqk_ref in enumerate((dqk0, dqk1, dqk2, dqk3)):
                sl = slice(HEAD_DIM * h, HEAD_DIM * (h + 1))
                du_ref[rows, sl] = du[h]
                dwk_ref[rows, sl] = dwk[h]
                dqd_ref[rows, sl] = dqd[h]
                dkd_ref[rows, sl] = dkd[h]
                dqk_ref[rows, :] = dqk[h]
                dgl_ref[gl_rows, sl] = dgl[h]
        ds_ref[...] = d_state

    rev = lambda n: (nb - 1 - n, 0)
    row = pl.BlockSpec((cs * CHUNK, W), rev)
    qk_spec = pl.BlockSpec((cs * CHUNK, CHUNK), rev)
    gl_spec = pl.BlockSpec((cs * 8, W), rev)
    qk_shape = jax.ShapeDtypeStruct((S, CHUNK), F32)
    row_shape = jax.ShapeDtypeStruct((S, W), F32)
    return pl.pallas_call(
        body, name=name, grid=(nb,),
        in_specs=[row, row, row, row, qk_spec, qk_spec, qk_spec, qk_spec, gl_spec,
                  pl.BlockSpec((cs, HEADS, HEAD_DIM, HEAD_DIM), lambda n: (nb - 1 - n, 0, 0, 0)), row],
        out_specs=[row, row, row, row, qk_spec, qk_spec, qk_spec, qk_spec, gl_spec],
        out_shape=[row_shape] * 4 + [qk_shape] * 4 + [jax.ShapeDtypeStruct((nc * 8, W), F32)],
        scratch_shapes=[pltpu.VMEM((HEADS, HEAD_DIM, HEAD_DIM), F32)],
        compiler_params=_params(("arbitrary",)),
    )(u, wk, qd, kd, *qks, gl, s_prev, d_o)


NEG = -1e30


def _chunk_mask(i, j, t, transposed=False):
    q_axis, k_axis = (1, 0) if transposed else (0, 1)
    r = (i * t + lax.broadcasted_iota(jnp.int32, (t, t), q_axis)) // CHUNK
    c = (j * t + lax.broadcasted_iota(jnp.int32, (t, t), k_axis)) // CHUNK
    return c <= r


def _tile_pairs(n, by_key):
    pairs = [(i, j) for j in range(n) for i in range(j, n)] if by_key else [(i, j) for i in range(n) for j in range(i + 1)]
    return jnp.asarray(np.array([p[0] for p in pairs], np.int32)), jnp.asarray(np.array([p[1] for p in pairs], np.int32))


def _heads(ref, width, rows=slice(None)):
    return jnp.stack([ref[rows, width * h:width * (h + 1)] for h in range(HEADS)])


def _bmm(a, b, dims):
    return lax.dot_general(a.astype(BF16), b.astype(BF16), dims, preferred_element_type=F32)


def _attn_fwd(q, k, v_t, t, name):
    S = q.shape[0]
    n = S // t
    qi, kj = _tile_pairs(n, by_key=False)

    def body(qi_ref, kj_ref, q_ref, k_ref, vt_ref, o_ref, lse_ref, m_ref, l_ref, acc_ref):
        i, j = qi_ref[pl.program_id(0)], kj_ref[pl.program_id(0)]

        @pl.when(j == 0)
        def _():
            m_ref[...] = jnp.full_like(m_ref, NEG)
            l_ref[...] = jnp.zeros_like(l_ref)
            acc_ref[...] = jnp.zeros_like(acc_ref)

        def update(masked):
            s_t = _bmm(_heads(k_ref, 256), _heads(q_ref, 256), _BNT)
            if masked:
                s_t = jnp.where(_chunk_mask(i, j, t, transposed=True)[None], s_t, NEG)
            m_old = m_ref[...]
            m_new = jnp.maximum(m_old, jnp.max(s_t, axis=1, keepdims=True))
            p_t = jnp.exp2(s_t - m_new)
            alpha = jnp.exp2(m_old - m_new)
            l_ref[...] = alpha * l_ref[...] + jnp.sum(p_t, axis=1, keepdims=True)
            v_heads = jnp.stack([vt_ref[HEAD_DIM * h:HEAD_DIM * (h + 1), :] for h in range(HEADS)])
            acc_ref[...] = alpha * acc_ref[...] + _bmm(v_heads, p_t, _BNN)
            m_ref[...] = m_new

        @pl.when(j < i)
        def _():
            update(False)

        @pl.when(j == i)
        def _():
            update(True)
            for h in range(HEADS):
                sl = slice(HEAD_DIM * h, HEAD_DIM * (h + 1))
                o_ref[:, sl] = jnp.transpose(acc_ref[h] / l_ref[h])
                lse_ref[:, sl] = jnp.transpose(jnp.broadcast_to(m_ref[h] + jnp.log(l_ref[h]) * LOG2_E, (HEAD_DIM, t)))

    row = lambda p, qi_, kj_: (qi_[p], 0)
    return pl.pallas_call(
        body, name=name,
        grid_spec=pltpu.PrefetchScalarGridSpec(
            num_scalar_prefetch=2, grid=(qi.shape[0],),
            in_specs=[pl.BlockSpec((t, HEADS * 256), row), pl.BlockSpec((t, HEADS * 256), lambda p, qi_, kj_: (kj_[p], 0)),
                      pl.BlockSpec((HEADS * HEAD_DIM, t), lambda p, qi_, kj_: (0, kj_[p]))],
            out_specs=[pl.BlockSpec((t, HEADS * HEAD_DIM), row)] * 2,
            scratch_shapes=[pltpu.VMEM((HEADS, 1, t), F32), pltpu.VMEM((HEADS, 1, t), F32),
                            pltpu.VMEM((HEADS, HEAD_DIM, t), F32)]),
        out_shape=[jax.ShapeDtypeStruct((S, HEADS * HEAD_DIM), F32)] * 2,
        compiler_params=_params(("arbitrary",)),
    )(qi, kj, q, k, v_t)


def _attn_stats(o, lse, d_o, t, name):
    S = o.shape[0]

    def body(o_ref, lse_ref, do_ref, st_ref):
        lane = lax.broadcasted_iota(jnp.int32, (t, HEAD_DIM), 1)
        stats = jnp.zeros((t, HEAD_DIM), F32)
        for h in range(HEADS):
            sl = slice(HEAD_DIM * h, HEAD_DIM * (h + 1))
            delta = jnp.sum(do_ref[:, sl] * o_ref[:, sl], axis=1, keepdims=True)
            stats = stats + jnp.where(lane == h, lse_ref[:, sl], 0.0) + jnp.where(lane == HEADS + h, delta, 0.0)
        st_ref[...] = jnp.transpose(stats)[0:8, :]

    row = pl.BlockSpec((t, HEADS * HEAD_DIM), lambda i: (i, 0))
    return pl.pallas_call(
        body, name=name, grid=(S // t,),
        in_specs=[row, row, row], out_specs=pl.BlockSpec((8, t), lambda i: (0, i)),
        out_shape=jax.ShapeDtypeStruct((8, S), F32),
        compiler_params=_params(("parallel",)),
    )(o, lse, d_o)


BWD_GROUP = 2


def _attn_bwd(q, k, v, d_o, stats, t, name):
    S = q.shape[0]
    n = S // t
    groups = HEADS // BWD_GROUP
    gq, gv = BWD_GROUP * 256, BWD_GROUP * HEAD_DIM
    qi, kj = _tile_pairs(n, by_key=True)
    n_pairs = qi.shape[0]
    st = stats.reshape(2, groups, BWD_GROUP, S).transpose(1, 0, 2, 3).reshape(groups, 2 * BWD_GROUP, S)
    st = jnp.pad(st, ((0, 0), (0, 8 - 2 * BWD_GROUP), (0, 0)))

    def heads(ref, width, rows=slice(None)):
        return jnp.stack([ref[rows, width * h:width * (h + 1)] for h in range(BWD_GROUP)])

    def body(qi_ref, kj_ref, q_ref, k_ref, v_ref, do_ref, st_ref, dq_hbm, dk_ref, dv_ref, dq_acc, sem):
        g, p = pl.program_id(0), pl.program_id(1)
        i, j = qi_ref[p], kj_ref[p]

        @pl.when(i == j)
        def _():
            dk_ref[...] = jnp.zeros_like(dk_ref)
            dv_ref[...] = jnp.zeros_like(dv_ref)

        def update(masked):
            qh, kh = heads(q_ref, 256), heads(k_ref, 256)
            d_out = heads(do_ref, HEAD_DIM)
            stv = st_ref[...]
            lse_row = jnp.stack([stv[h:h + 1, :] for h in range(BWD_GROUP)])
            delta_row = jnp.stack([stv[BWD_GROUP + h:BWD_GROUP + h + 1, :] for h in range(BWD_GROUP)])
            s_t = _bmm(kh, qh, _BNT)
            p_t = jnp.exp2(s_t - lse_row)
            if masked:
                p_t = jnp.where(_chunk_mask(i, j, t, transposed=True)[None], p_t, 0.0)
            dv = _bmm(p_t, d_out, _BNN)
            dp_t = _bmm(heads(v_ref, HEAD_DIM), d_out, _BNT)
            ds_t = p_t * (dp_t - delta_row)
            dk = _bmm(ds_t, qh, _BNN)
            dq = _bmm(jnp.swapaxes(ds_t, 1, 2), kh, _BNN)
            rows = pl.ds(pl.multiple_of(i * t, t), t)
            for h in range(BWD_GROUP):
                dk_ref[:, 256 * h:256 * (h + 1)] += dk[h]
                dv_ref[:, HEAD_DIM * h:HEAD_DIM * (h + 1)] += dv[h]

            @pl.when(j == 0)
            def _():
                for h in range(BWD_GROUP):
                    dq_acc[rows, 256 * h:256 * (h + 1)] = dq[h]

            @pl.when(j > 0)
            def _():
                for h in range(BWD_GROUP):
                    dq_acc[rows, 256 * h:256 * (h + 1)] += dq[h]

        @pl.when(i == j)
        def _():
            update(True)

        @pl.when(i > j)
        def _():
            update(False)

        @pl.when(i == n - 1)
        def _():
            dk_ref[...] *= 1.0 / LOG2_E

        @pl.when(p == n_pairs - 1)
        def _():
            dq_acc[...] *= 1.0 / LOG2_E
            for gg in range(groups):
                @pl.when(g == gg)
                def _():
                    cp = pltpu.make_async_copy(dq_acc, dq_hbm.at[:, gq * gg:gq * (gg + 1)], sem)
                    cp.start()
                    cp.wait()

    q_blk = lambda g, p, qi_, kj_: (qi_[p], g)
    k_blk = lambda g, p, qi_, kj_: (kj_[p], g)
    return pl.pallas_call(
        body, name=name,
        grid_spec=pltpu.PrefetchScalarGridSpec(
            num_scalar_prefetch=2, grid=(groups, n_pairs),
            in_specs=[pl.BlockSpec((t, gq), q_blk), pl.BlockSpec((t, gq), k_blk), pl.BlockSpec((t, gv), k_blk),
                      pl.BlockSpec((t, gv), q_blk), pl.BlockSpec((None, 8, t), lambda g, p, qi_, kj_: (g, 0, qi_[p]))],
            out_specs=[pl.BlockSpec(memory_space=pl.ANY), pl.BlockSpec((t, gq), k_blk), pl.BlockSpec((t, gv), k_blk)],
            scratch_shapes=[pltpu.VMEM((S, gq), F32), pltpu.SemaphoreType.DMA]),
        out_shape=[jax.ShapeDtypeStruct((S, HEADS * 256), F32), jax.ShapeDtypeStruct((S, HEADS * 256), F32),
                   jax.ShapeDtypeStruct((S, HEADS * HEAD_DIM), F32)],
        compiler_params=_params(("arbitrary", "arbitrary")),
    )(qi, kj, q, k, v, d_o, st)


FFN_PIECE = 2 * D_FF // N_DEV
HID_PIECES = D_FF // FFN_PIECE


def _ffn_up(h, w8, name):
    S = h.shape[0]
    tm = _pick(S, MATMUL_ROWS)

    def body(h_ref, wg_ref, wu_ref, g_ref, u_ref, hid_ref, hid_t_ref):
        gate = _dot_raw(h_ref[...], wg_ref[...], "nn")
        up = _dot_raw(h_ref[...], wu_ref[...], "nn")
        sg = _sigmoid(gate)
        act = gate * sg
        hid = act * up
        g_ref[...] = (up * (sg * (1.0 + gate * (1.0 - sg)))).astype(BF16)
        u_ref[...] = act.astype(BF16)
        hid_ref[...] = hid.astype(BF16)
        hid_t_ref[...] = jnp.transpose(hid).astype(BF16)

    o_spec = pl.BlockSpec((None, tm, FFN_PIECE), lambda i, j: (j, i, 0))
    return pl.pallas_call(
        body, name=name, grid=(S // tm, HID_PIECES),
        in_specs=[pl.BlockSpec((tm, D_MODEL), lambda i, j: (i, 0)),
                  pl.BlockSpec((None, D_MODEL, FFN_PIECE), lambda i, j: (j, 0, 0)),
                  pl.BlockSpec((None, D_MODEL, FFN_PIECE), lambda i, j: (j + HID_PIECES, 0, 0))],
        out_specs=[o_spec] * 3 + [pl.BlockSpec((None, FFN_PIECE, tm), lambda i, j: (j, 0, i))],
        out_shape=[jax.ShapeDtypeStruct((HID_PIECES, S, FFN_PIECE), BF16)] * 3
        + [jax.ShapeDtypeStruct((HID_PIECES, FFN_PIECE, S), BF16)],
        compiler_params=_params(("parallel", "parallel")),
    )(h, w8, w8)


def _after_specs(after):
    return [] if after is None else [pl.BlockSpec(memory_space=pl.ANY)]


def _after_args(after):
    return [] if after is None else [after]


def _ffn_gw8(h, d_gate, d_up, name, after=None):
    S = h.shape[0]
    tm = 512
    tk = _pick(S, (512, 256, 128))
    nk = S // tk

    def body(h_ref, dg_ref, du_ref, *rest):
        o_ref, acc_ref = rest[-2:]
        k = pl.program_id(1)

        @pl.when(k == 0)
        def _():
            acc_ref[...] = jnp.zeros_like(acc_ref)

        h_t = jnp.transpose(h_ref[...])
        for p in range(HID_PIECES):
            acc_ref[p] += _dot_raw(h_t, dg_ref[p], "nn")
            acc_ref[HID_PIECES + p] += _dot_raw(h_t, du_ref[p], "nn")

        @pl.when(k == nk - 1)
        def _():
            o_ref[...] = acc_ref[...].astype(o_ref.dtype)

    d_spec = pl.BlockSpec((HID_PIECES, tk, FFN_PIECE), lambda i, k: (0, k, 0))
    return pl.pallas_call(
        body, name=name, grid=(D_MODEL // tm, nk),
        in_specs=[pl.BlockSpec((tk, tm), lambda i, k: (k, i)), d_spec, d_spec] + _after_specs(after),
        out_specs=pl.BlockSpec((2 * HID_PIECES, tm, FFN_PIECE), lambda i, k: (0, i, 0)),
        out_shape=jax.ShapeDtypeStruct((2 * HID_PIECES, D_MODEL, FFN_PIECE), BF16),
        scratch_shapes=[pltpu.VMEM((2 * HID_PIECES, tm, FFN_PIECE), F32)],
        compiler_params=_params(("parallel", "arbitrary")),
    )(h, d_gate, d_up, *_after_args(after))


EPILOGUE_ROWS = 256


def _dmod_epilogue(acc_ref, x_ref, do_ref, sc_ref, sh_ref, dx_ref, dsc_ref, dsh_ref, below, below_in, below_out):
    rows_total = acc_ref.shape[0]
    step = min(EPILOGUE_ROWS, rows_total)
    dsc, dsh, dg = 0.0, 0.0, 0.0
    for r in range(rows_total // step):
        rows = slice(step * r, step * (r + 1))
        _, vjp = jax.vjp(_modulate, x_ref[rows, :], sc_ref[...], sh_ref[...])
        dx, dsc_r, dsh_r = vjp(acc_ref[rows, :])
        dx = dx + do_ref[rows, :]
        dx_ref[rows, :] = dx
        dsc, dsh = dsc + dsc_r, dsh + dsh_r
        if below is not None:
            coef = below[2]
            below_out[0][rows, :] = (coef * below_in[1][...] * dx).astype(below_out[0].dtype)
            dg = dg + jnp.sum(coef * below_in[0][rows, :] * dx, axis=0, keepdims=True)
    dsc_ref[...] += dsc
    dsh_ref[...] += dsh
    if below is not None:
        below_out[1][...] += dg


def _ffn_dh(d_gate, d_up, w8, x, d_out, scale, shift, name, after=None, below=None):
    S = d_gate.shape[1]
    tm = _pick(S, MATMUL_ROWS if below is None else MATMUL_ROWS[1:])
    n_below = 0 if below is None else 2

    def body(dg_ref, du_ref, wg_ref, wu_ref, x_ref, do_ref, sc_ref, sh_ref, *rest):
        below_in = rest[:n_below]
        outs = rest[len(rest) - 4 - n_below:]
        dx_ref, dsc_ref, dsh_ref = outs[:3]
        below_out, acc_ref = outs[3:3 + n_below], outs[-1]
        i, k = pl.program_id(0), pl.program_id(1)

        @pl.when(k == 0)
        def _():
            acc_ref[...] = jnp.zeros_like(acc_ref)

        acc_ref[...] += _dot_raw(dg_ref[...], wg_ref[...], "nt") + _dot_raw(du_ref[...], wu_ref[...], "nt")

        @pl.when((k == 0) & (i == 0))
        def _():
            for r in (dsc_ref, dsh_ref) + tuple(below_out[1:]):
                r[...] = jnp.zeros_like(r)

        @pl.when(k == HID_PIECES - 1)
        def _():
            _dmod_epilogue(acc_ref, x_ref, do_ref, sc_ref, sh_ref, dx_ref, dsc_ref, dsh_ref, below, below_in, below_out)

    d_spec = pl.BlockSpec((None, tm, FFN_PIECE), lambda i, k: (k, i, 0))
    row = pl.BlockSpec((tm, D_MODEL), lambda i, k: (i, 0))
    par = pl.BlockSpec((1, D_MODEL), lambda i, k: (0, 0))
    row_shape, par_shape = jax.ShapeDtypeStruct((S, D_MODEL), F32), jax.ShapeDtypeStruct((1, D_MODEL), F32)
    return pl.pallas_call(
        body, name=name, grid=(S // tm, HID_PIECES),
        in_specs=[d_spec, d_spec,
                  pl.BlockSpec((None, D_MODEL, FFN_PIECE), lambda i, k: (k, 0, 0)),
                  pl.BlockSpec((None, D_MODEL, FFN_PIECE), lambda i, k: (k + HID_PIECES, 0, 0)),
                  row, row, par, par] + [row, par][:n_below] + _after_specs(after),
        out_specs=[row, par, par] + [row, par][:n_below],
        out_shape=[row_shape, par_shape, par_shape] + [jax.ShapeDtypeStruct((S, D_MODEL), BF16), par_shape][:n_below],
        scratch_shapes=[pltpu.VMEM((tm, D_MODEL), F32)],
        compiler_params=_params(("arbitrary", "arbitrary")),
    )(d_gate, d_up, w8, w8, x, d_out, scale, shift, *(below[:2] if below is not None else ()), *_after_args(after))


def _swiglu_bwd(d_hid, hid_by_gate, hid_by_up):
    return d_hid * hid_by_gate, d_hid * hid_by_up


def _adamw_math(w_, g_, m_, v_):
    m_ = ADAM_B1 * m_ + (1.0 - ADAM_B1) * g_
    v_ = ADAM_B2 * v_ + (1.0 - ADAM_B2) * (g_ * g_)
    m_hat = m_ / (1.0 - ADAM_B1 ** ADAM_STEP)
    v_hat = v_ / (1.0 - ADAM_B2 ** ADAM_STEP)
    return -ADAM_LR * (m_hat / (jnp.sqrt(v_hat) + ADAM_EPS) + ADAM_WD * w_), m_, v_


def _adamw(w, g, m, v, name):
    R, C = w.shape
    tr = _pick(R, (256, 176, 128, 64, 32, 16, 8))

    def body(w_ref, g_ref, m_ref, v_ref, d_ref, nm_ref, nv_ref):
        d_ref[...], nm_ref[...], nv_ref[...] = _adamw_math(w_ref[...], g_ref[...], m_ref[...], v_ref[...])

    spec = pl.BlockSpec((tr, C), lambda i: (i, 0))
    return pl.pallas_call(
        body, name=name, grid=(R // tr,),
        in_specs=[spec] * 4, out_specs=[spec] * 3,
        out_shape=[jax.ShapeDtypeStruct((R, C), F32)] * 3,
        compiler_params=_params(("parallel",)),
    )(w, g, m, v)


def _sum_adamw(parts, w, m, v, name, transposed=False):
    _, R, C = parts.shape
    tr = _pick(R, (256, 176, 128, 64, 32, 16, 8))

    def body(p_ref, w_ref, m_ref, v_ref, g_ref, d_ref, nm_ref, nv_ref):
        g_ = p_ref[0].astype(F32)
        for d in range(1, N_DEV):
            g_ = g_ + p_ref[d].astype(F32)
        if transposed:
            g_ = jnp.transpose(g_)
        g_ref[...] = g_
        d_ref[...], nm_ref[...], nv_ref[...] = _adamw_math(w_ref[...], g_, m_ref[...], v_ref[...])

    spec = pl.BlockSpec((C, tr), lambda i: (0, i)) if transposed else pl.BlockSpec((tr, C), lambda i: (i, 0))
    return pl.pallas_call(
        body, name=name, grid=(R // tr,),
        in_specs=[pl.BlockSpec((N_DEV, tr, C), lambda i: (0, i, 0)), spec, spec, spec], out_specs=[spec] * 4,
        out_shape=[jax.ShapeDtypeStruct(w.shape, F32)] * 4,
        compiler_params=_params(("parallel",)),
    )(parts, w, m, v)


def _sum_devices(parts, name):
    _, R, C = parts.shape
    tr = _pick(R, (512, 256, 176, 128, 64, 32, 16, 8))

    def body(p_ref, o_ref):
        acc = p_ref[0].astype(F32)
        for d in range(1, N_DEV):
            acc = acc + p_ref[d].astype(F32)
        o_ref[...] = acc

    return pl.pallas_call(
        body, name=name, grid=(R // tr,),
        in_specs=[pl.BlockSpec((N_DEV, tr, C), lambda i: (0, i, 0))],
        out_specs=pl.BlockSpec((tr, C), lambda i: (i, 0)),
        out_shape=jax.ShapeDtypeStruct((R, C), F32),
        compiler_params=_params(("parallel",)),
    )(parts)


def _my_place():
    return lax.axis_index("x"), lax.axis_index("y"), lax.axis_index("c")


def _all_gather(blocks, name):
    n = len(blocks)

    def body(*refs):
        x_refs, out_refs = refs[:n], refs[n:2 * n]
        send_sems, recv_sems, local_sems = refs[2 * n:]
        x, y, c = _my_place()
        me, sibling = (x, y, c), (x, y, 1 - c)
        chips = [(1 - x, y), (x, 1 - y), (1 - x, 1 - y)]

        def copy(a, k, blk, to, own=False):
            slot = out_refs[a].at[4 * blk[0] + 2 * blk[1] + blk[2]]
            return pltpu.make_async_remote_copy(
                src_ref=x_refs[a] if own else slot, dst_ref=slot,
                send_sem=send_sems.at[7 * a + k], recv_sem=recv_sems.at[7 * a + k], device_id=to, device_id_type=MESH)

        mine = [pltpu.make_async_copy(x_refs[a], out_refs[a].at[4 * x + 2 * y + c], local_sems.at[a]) for a in range(n)]
        for cp in mine:
            cp.start()
        first = []
        for j, chip in enumerate(chips):
            first += [copy(a, 1 + j, me, (*chip, c), own=True) for a in range(n)]
        first += [copy(a, 0, me, sibling, own=True) for a in range(n)]
        for cp in first:
            cp.start()
        passed = []
        for j, chip in enumerate(chips):
            for a in range(n):
                copy(a, 1 + j, (*chip, c), me).wait_recv()
                passed.append(copy(a, 4 + j, (*chip, c), sibling))
                passed[-1].start()
        for a in range(n):
            copy(a, 0, sibling, me).wait_recv()
        for j, chip in enumerate(chips):
            for a in range(n):
                copy(a, 4 + j, (*chip, 1 - c), me).wait_recv()
        for cp in first + passed:
            cp.wait_send()
        for cp in mine:
            cp.wait()

    return pl.pallas_call(
        body, name=name,
        out_shape=[jax.ShapeDtypeStruct((N_DEV,) + b.shape, b.dtype) for b in blocks],
        in_specs=[pl.BlockSpec(memory_space=pl.ANY)] * n,
        out_specs=[pl.BlockSpec(memory_space=pl.ANY)] * n,
        scratch_shapes=[pltpu.SemaphoreType.DMA((7 * n,)), pltpu.SemaphoreType.DMA((7 * n,)), pltpu.SemaphoreType.DMA((n,))],
    )(*blocks)


def _all_to_all(pieces, name):
    n = len(pieces)

    def body(*refs):
        x_refs, out_refs = refs[:n], refs[n:2 * n]
        send_sems, recv_sems, local_sems = refs[2 * n:]
        x, y, c = _my_place()
        me = 4 * x + 2 * y + c
        mine = [pltpu.make_async_copy(x_refs[a].at[me], out_refs[a].at[me], local_sems.at[a]) for a in range(n)]
        for cp in mine:
            cp.start()
        copies = []
        for k in (2, 4, 6, 3, 5, 7, 1):
            px = 1 - x if k & 4 else x
            py = 1 - y if k & 2 else y
            pc = 1 - c if k & 1 else c
            peer = 4 * px + 2 * py + pc
            for a in range(n):
                copies.append(pltpu.make_async_remote_copy(
                    src_ref=x_refs[a].at[peer], dst_ref=out_refs[a].at[me],
                    send_sem=send_sems.at[7 * a + k - 1], recv_sem=recv_sems.at[7 * a + k - 1],
                    device_id=(px, py, pc), device_id_type=MESH))
        for cp in copies:
            cp.start()
        for cp in copies:
            cp.wait_recv()
        for cp in copies:
            cp.wait_send()
        for cp in mine:
            cp.wait()

    return pl.pallas_call(
        body, name=name,
        out_shape=[jax.ShapeDtypeStruct(p.shape, p.dtype) for p in pieces],
        in_specs=[pl.BlockSpec(memory_space=pl.ANY)] * n,
        out_specs=[pl.BlockSpec(memory_space=pl.ANY)] * n,
        scratch_shapes=[pltpu.SemaphoreType.DMA((7 * n,)), pltpu.SemaphoreType.DMA((7 * n,)), pltpu.SemaphoreType.DMA((n,))],
    )(*pieces)


def _peers():
    x, y, c = _my_place()
    out = []
    for k in (2, 4, 6, 3, 5, 7, 1):
        px = 1 - x if k & 4 else x
        py = 1 - y if k & 2 else y
        pc = 1 - c if k & 1 else c
        out.append((k, (px, py, pc), 4 * px + 2 * py + pc))
    return out


def _exchange_copies(x_refs, land_refs, send_sems, recv_sems, scatter):
    x, y, c = _my_place()
    me = 4 * x + 2 * y + c
    starts, arrivals = [], []
    for k, place, peer in _peers():
        for a, (x_ref, land_ref) in enumerate(zip(x_refs, land_refs)):
            sems = dict(send_sem=send_sems.at[7 * a + k - 1], recv_sem=recv_sems.at[7 * a + k - 1],
                        device_id=place, device_id_type=MESH)
            src = x_ref.at[peer] if scatter else x_ref
            starts.append(pltpu.make_async_remote_copy(src_ref=src, dst_ref=land_ref.at[me], **sems))
            arrivals.append(pltpu.make_async_remote_copy(src_ref=src, dst_ref=land_ref.at[peer], **sems))
    return starts, arrivals


def _exchange_start(arrays, scatter, name):
    n = len(arrays)
    hbm = pl.BlockSpec(memory_space=pltpu.HBM)
    sem = pl.BlockSpec(memory_space=pltpu.SEMAPHORE)
    lands = [lax.empty(a.shape if scatter else (N_DEV,) + a.shape, a.dtype) for a in arrays]

    def body(*refs):
        x_refs, land_refs = refs[:n], refs[n:2 * n]
        send_sems, recv_sems = refs[2 * n], refs[2 * n + 1]
        token = refs[-1]
        starts, _ = _exchange_copies(x_refs, land_refs, send_sems, recv_sems, scatter)
        for cp in starts:
            cp.start()
        token[...] = jnp.zeros_like(token)

    res = pl.pallas_call(
        body, name=name,
        out_shape=(pltpu.SemaphoreType.DMA((7 * n,)), pltpu.SemaphoreType.DMA((7 * n,)),
                   *[pltpu.HBM(a.shape, a.dtype) for a in arrays], *[pltpu.HBM(l.shape, l.dtype) for l in lands],
                   jax.ShapeDtypeStruct((8, 128), F32)),
        in_specs=[hbm] * (2 * n),
        out_specs=(sem, sem, *[hbm] * (2 * n), pl.BlockSpec(memory_space=pltpu.VMEM)),
        input_output_aliases={i: 2 + i for i in range(2 * n)},
        compiler_params=pltpu.CompilerParams(has_side_effects=pltpu.SideEffectType.DATAFLOW_SIDE_EFFECTING),
    )(*[pltpu.with_memory_space_constraint(a, pltpu.HBM) for a in arrays],
      *[pltpu.with_memory_space_constraint(l, pltpu.HBM) for l in lands])
    return res[0], res[1], list(res[2:2 + n]), list(res[2 + n:2 + 2 * n]), res[-1]


def _exchange_wait(handles, scatter, after, name):
    send_sems, recv_sems, arrays, lands, _ = handles
    n = len(arrays)
    hbm = pl.BlockSpec(memory_space=pltpu.HBM)
    sem = pl.BlockSpec(memory_space=pltpu.SEMAPHORE)

    def body(*refs):
        x_refs, land_refs = refs[:n], refs[n:2 * n]
        send_s, recv_s = refs[2 * n], refs[2 * n + 1]
        starts, arrivals = _exchange_copies(x_refs, land_refs, send_s, recv_s, scatter)
        for cp in arrivals:
            cp.wait_recv()
        for cp in starts:
            cp.wait_send()

    res = pl.pallas_call(
        body, name=name,
        out_shape=(*[pltpu.HBM(a.shape, a.dtype) for a in arrays], *[pltpu.HBM(l.shape, l.dtype) for l in lands]),
        in_specs=[hbm] * (2 * n) + [sem, sem, pl.BlockSpec(memory_space=pl.ANY)],
        out_specs=tuple([hbm] * (2 * n)),
        input_output_aliases={i: i for i in range(2 * n)},
        compiler_params=pltpu.CompilerParams(has_side_effects=pltpu.SideEffectType.DATAFLOW_SIDE_EFFECTING),
    )(*arrays, *lands, send_sems, recv_sems, after)
    me = 4 * lax.axis_index("x") + 2 * lax.axis_index("y") + lax.axis_index("c")
    out = []
    for src, got in zip(res[:n], res[n:]):
        zeros = (0,) * (got.ndim - 1)
        own = lax.dynamic_slice(src, (me,) + zeros, (1,) + src.shape[1:]) if scatter else src[None]
        out.append(lax.dynamic_update_slice(got, own, (me,) + zeros))
    return out


def _pad_lanes(v, at=0, width=128):
    return jnp.pad(v, ((0, 0), (at, width - at - v.shape[1])))


def _pack_weights(P):
    W = {}
    w = P["w_in"]
    W["wp"] = jnp.concatenate([w[:, :2048], w[:, 2440:2696], w[:, 2056:2440], w[:, 2696:2760], w[:, 2048:2056],
                               jnp.zeros((D_MODEL, N_IN_PACKED - N_IN), w.dtype)], axis=1).astype(BF16)
    W["conv_w"] = P["gdn_conv_w"].astype(F32)
    W["alog_p"] = _pad_lanes(P["gdn_a_log"], 64)
    W["dt_p"] = _pad_lanes(P["gdn_dt_bias"], 64)
    W["gnw"] = P["gdn_norm_w"]
    W["qnw"] = P["mla_q_norm_w"]
    W["kvnw"] = P["mla_kv_norm_w"]
    uq = P["mla_w_uq"].reshape(Q_LORA, HEADS, HEAD_DIM + ROPE)
    W["wuq"] = jnp.pad(uq, ((0, 0), (0, 0), (0, 256 - HEAD_DIM - ROPE))).reshape(Q_LORA, HEADS * 256).astype(BF16)
    ukv = P["mla_w_ukv"].reshape(KV_LORA, HEADS, 2, HEAD_DIM)
    W["wukv"] = ukv.transpose(0, 2, 1, 3).reshape(KV_LORA, 2 * HEADS * HEAD_DIM).astype(BF16)
    W["qn_w"] = P["qkn_q_nope"]
    W["qr_w"] = _pad_lanes(P["qkn_q_rope"])
    W["kn_w"] = P["qkn_k_nope"]
    W["kr_w"] = _pad_lanes(P["qkn_k_rope"])
    W["onw"] = P["mla_out_norm_w"]
    W["wout"] = P["w_out"].astype(BF16)
    return W


def _unpack_grads(G):
    g = G["wp"]
    uq = G["wuq"].reshape(Q_LORA, HEADS, 256)[:, :, :HEAD_DIM + ROPE].reshape(Q_LORA, HEADS * (HEAD_DIM + ROPE))
    ukv = G["wukv"].reshape(KV_LORA, 2, HEADS, HEAD_DIM).transpose(0, 2, 1, 3).reshape(KV_LORA, 2 * HEADS * HEAD_DIM)
    return {
        "w_in": jnp.concatenate([g[:, :2048], g[:, 2752:2760], g[:, 2304:2688], g[:, 2048:2304], g[:, 2688:2752]], axis=1),
        "gdn_conv_w": G["conv_w"], "gdn_a_log": G["alog_p"][:, 64:68], "gdn_dt_bias": G["dt_p"][:, 64:68],
        "gdn_norm_w": G["gnw"], "mla_q_norm_w": G["qnw"], "mla_w_uq": uq, "mla_kv_norm_w": G["kvnw"], "mla_w_ukv": ukv,
        "qkn_q_nope": G["qn_w"], "qkn_q_rope": G["qr_w"][:, :ROPE], "qkn_k_nope": G["kn_w"], "qkn_k_rope": G["kr_w"][:, :ROPE],
        "mla_out_norm_w": G["onw"], "w_out": G["wout"],
    }


def _rope_tables(positions):
    half = ROPE // 2
    inv_freq = ROPE_BASE ** (-jnp.arange(half, dtype=F32) / half)
    ang = positions.astype(F32)[:, None] * inv_freq
    cos, sin = jnp.cos(ang), jnp.sin(ang)
    zeros = jnp.zeros((positions.shape[0], 128 - ROPE), F32)
    return jnp.concatenate([cos, cos, zeros], axis=1), jnp.concatenate([-sin, sin, zeros], axis=1)


def _mod_fn(x, scale, shift):
    return (_modulate(x, scale, shift),)


def _ffn_down_loss(hid, wo4, x, gate_w, target, name):
    S = x.shape[0]
    tb = _pick(S, MATMUL_ROWS)
    n = S // tb

    def body(hid_ref, wo_ref, x_ref, g_ref, t_ref, dx_ref, df_ref, dg_ref, l_ref, acc_ref):
        i, k = pl.program_id(0), pl.program_id(1)

        @pl.when(k == 0)
        def _():
            acc_ref[...] = jnp.zeros_like(acc_ref)

        acc_ref[...] += _dot_raw(hid_ref[...], wo_ref[...], "nn")

        @pl.when((k == 0) & (i == 0))
        def _():
            dg_ref[...] = jnp.zeros_like(dg_ref)
            l_ref[...] = jnp.zeros_like(l_ref)

        @pl.when(k == HID_PIECES - 1)
        def _():
            step = min(EPILOGUE_ROWS, tb)
            for r in range(tb // step):
                rows = slice(step * r, step * (r + 1))
                f = acc_ref[rows, :]
                diff = x_ref[rows, :] + 0.5 * g_ref[...] * f - t_ref[rows, :]
                dx = diff * (1.0 / D_MODEL)
                dx_ref[rows, :] = dx
                df_ref[rows, :] = (0.5 * g_ref[...] * dx).astype(df_ref.dtype)
                dg_ref[...] += jnp.sum(0.5 * f * dx, axis=0, keepdims=True)
                l_ref[...] += jnp.sum(diff * diff, axis=0, keepdims=True)

        @pl.when((k == HID_PIECES - 1) & (i == n - 1))
        def _():
            l_ref[...] = jnp.full(l_ref.shape, (0.5 / D_MODEL) * jnp.sum(l_ref[...]), F32)

    row = pl.BlockSpec((tb, D_MODEL), lambda i, k: (i, 0))
    par = pl.BlockSpec((1, D_MODEL), lambda i, k: (0, 0))
    return pl.pallas_call(
        body, name=name, grid=(n, HID_PIECES),
        in_specs=[pl.BlockSpec((None, tb, FFN_PIECE), lambda i, k: (k, i, 0)),
                  pl.BlockSpec((None, FFN_PIECE, D_MODEL), lambda i, k: (k, 0, 0)), row, par, row],
        out_specs=[row, row, par, par],
        out_shape=[jax.ShapeDtypeStruct((S, D_MODEL), F32), jax.ShapeDtypeStruct((S, D_MODEL), BF16),
                   jax.ShapeDtypeStruct((1, D_MODEL), F32), jax.ShapeDtypeStruct((1, D_MODEL), F32)],
        scratch_shapes=[pltpu.VMEM((tb, D_MODEL), F32)],
        compiler_params=_params(("arbitrary", "arbitrary")),
    )(hid, wo4, x, gate_w, target)


def _ffn_fwd(x, scale, shift, gate_w, w8, wo4, tag, target=None):
    S = x.shape[0]
    tm = _pick(S, (512, 256, 128))
    (h,) = _rowwise(_mod_fn, [(x, tm, D_MODEL, 0)], [scale, shift], [(tm, D_MODEL, BF16)], S // tm, tag + "_mod")
    by_gate, by_up, hid, hid_t = _ffn_up(h, w8, tag + "_up")
    if target is not None:
        dx_out, df, d_gate_w, loss_row = _ffn_down_loss(hid, wo4, x, gate_w, target, tag + "_down")
        return (dx_out, loss_row), (h, by_gate, by_up, hid_t, None, df, d_gate_w)
    tb = _pick(S, MATMUL_ROWS)
    mn = pl.BlockSpec((tb, D_MODEL), lambda i, j, k: (i, j))
    f, x_out = _mmg(hid, wo4, "nn", name=tag + "_down", grid=(S // tb, 1, HID_PIECES),
                    a_spec=pl.BlockSpec((None, tb, FFN_PIECE), lambda i, j, k: (k, i, 0)),
                    b_spec=pl.BlockSpec((None, FFN_PIECE, D_MODEL), lambda i, j, k: (k, 0, j)),
                    out_spec=mn, out_shapes=[jax.ShapeDtypeStruct((S, D_MODEL), F32)] * 2, acc_shape=(tb, D_MODEL),
                    extras=[x, gate_w], extra_specs=[mn, pl.BlockSpec((1, D_MODEL), lambda i, j, k: (0, j))],
                    epi=lambda acc, x_, g_: (acc, x_ + 0.5 * g_ * acc))
    return x_out, (h, by_gate, by_up, hid_t, f, None, None)


def _ffn_bwd(d_out, x, scale, shift, gate_w, w8, wo4, saved, tag, grad_ready, below=None):
    h, gate, up, hid_t, f, df, d_gate_w = saved
    S = x.shape[0]
    tm = _pick(S, (512, 256, 128))
    tk = _pick(S, (512, 256, 128))
    n = S // tm
    if df is None:
        (df,), (d_gate_w,) = _rowwise_bwd(lambda f_, g_: (0.5 * g_ * f_,), [(f, tm, D_MODEL, 0)], [], [gate_w],
                                          [(d_out, tm, D_MODEL, 0)], n, tag + "_dres", row_dtypes=(BF16,))
    tb = _pick(S, MATMUL_ROWS)
    piece = pl.BlockSpec((None, tb, FFN_PIECE), lambda i, j, k: (j, i, 0))
    d_gate, d_up = _mmg(df, wo4, "nt", name=tag + "_ddown", grid=(S // tb, HID_PIECES, 1),
                        a_spec=pl.BlockSpec((tb, D_MODEL), lambda i, j, k: (i, 0)),
                        b_spec=pl.BlockSpec((None, FFN_PIECE, D_MODEL), lambda i, j, k: (j, 0, 0)),
                        out_spec=piece, out_shapes=[jax.ShapeDtypeStruct((HID_PIECES, S, FFN_PIECE), BF16)] * 2,
                        acc_shape=(tb, FFN_PIECE), extras=[gate, up], extra_specs=[piece, piece], epi=_swiglu_bwd)
    tk = _pick(S, MATMUL_ROWS)
    g_wo4 = _mmg(hid_t, df, "nn", name=tag + "_gwo", grid=(HID_PIECES, 1, S // tk),
                 a_spec=pl.BlockSpec((None, FFN_PIECE, tk), lambda i, j, k: (i, 0, k)),
                 b_spec=pl.BlockSpec((tk, D_MODEL), lambda i, j, k: (k, j)),
                 out_spec=pl.BlockSpec((None, FFN_PIECE, D_MODEL), lambda i, j, k: (i, 0, j)),
                 out_shapes=[jax.ShapeDtypeStruct((HID_PIECES, FFN_PIECE, D_MODEL), BF16)], acc_shape=(FFN_PIECE, D_MODEL))
    g_w8 = _ffn_gw8(h, d_gate, d_up, tag + "_gw8", after=grad_ready("wo4", g_wo4))
    res = _ffn_dh(d_gate, d_up, w8, x, d_out, scale, shift, tag + "_dh", after=grad_ready("w8", g_w8), below=below)
    return (res[0], res[1], res[2], d_gate_w) + tuple(res[3:])


def _dproj_dmod(d_proj, wp, x, d_out, scale, shift, name, below=None):
    S, K = d_proj.shape
    tm = _pick(S, MATMUL_ROWS if below is None else MATMUL_ROWS[1:])
    tk = _pick(K, (1408, 512, 256, 128))
    nk = K // tk
    n_below = 0 if below is None else 2

    def body(dp_ref, w_ref, x_ref, do_ref, sc_ref, sh_ref, *rest):
        below_in = rest[:n_below]
        dx_ref, dsc_ref, dsh_ref = rest[n_below:n_below + 3]
        below_out, acc_ref = rest[n_below + 3:n_below + 3 + n_below], rest[-1]
        i, k = pl.program_id(0), pl.program_id(1)

        @pl.when(k == 0)
        def _():
            acc_ref[...] = jnp.zeros_like(acc_ref)

        acc_ref[...] += _dot_raw(dp_ref[...], w_ref[...], "nt")

        @pl.when((k == 0) & (i == 0))
        def _():
            for r in (dsc_ref, dsh_ref) + tuple(below_out[1:]):
                r[...] = jnp.zeros_like(r)

        @pl.when(k == nk - 1)
        def _():
            _dmod_epilogue(acc_ref, x_ref, do_ref, sc_ref, sh_ref, dx_ref, dsc_ref, dsh_ref, below, below_in, below_out)

    row = pl.BlockSpec((tm, D_MODEL), lambda i, k: (i, 0))
    par = pl.BlockSpec((1, D_MODEL), lambda i, k: (0, 0))
    row_shape, par_shape = jax.ShapeDtypeStruct((S, D_MODEL), F32), jax.ShapeDtypeStruct((1, D_MODEL), F32)
    return pl.pallas_call(
        body, name=name, grid=(S // tm, nk),
        in_specs=[pl.BlockSpec((tm, tk), lambda i, k: (i, k)), pl.BlockSpec((D_MODEL, tk), lambda i, k: (0, k)),
                  row, row, par, par] + [row, par][:n_below],
        out_specs=[row, par, par] + [row, par][:n_below],
        out_shape=[row_shape, par_shape, par_shape] + [jax.ShapeDtypeStruct((S, D_MODEL), BF16), par_shape][:n_below],
        scratch_shapes=[pltpu.VMEM((tm, D_MODEL), F32)],
        compiler_params=_params(("arbitrary", "arbitrary")),
    )(d_proj, wp, x, d_out, scale, shift, *(below[:2] if below is not None else ()))


def _mixer_fwd(x1, scale, shift, gate_w, cos_p, sin_p, W):
    S = x1.shape[0]
    tm = _pick(S, (512, 256, 128))
    tv = _pick(S, (256, 128))
    ta = _pick(S, (512, 256, 128))
    nc = S // CHUNK
    (h2,) = _rowwise(_mod_fn, [(x1, tm, D_MODEL, 0)], [scale, shift], [(tm, D_MODEL, BF16)], S // tm, "mix_mod")
    proj = _mm(h2, W["wp"], "nn", name="mix_proj")
    qkvc = _conv_fwd(proj, W["conv_w"], tv, "gdn_conv")
    kab = (proj, tv, 128, 21)
    q_a, k_a, v_a, gb = _rowwise(_gdn_pre_fn, [(qkvc, tv, 1536, 0), kab], [W["alog_p"], W["dt_p"]],
                                 [(tv, 512, F32)] * 3 + [(tv, 128, F32)], S // tv, "gdn_pre")
    ti = _pick(S, INTRA_ROWS)
    intra = _rowwise(_gdn_intra_fn, [(q_a, ti, 512, 0), (k_a, ti, 512, 0), (v_a, ti, 512, 0), (gb, ti, 128, 0)],
                     [], [(ti, 512, F32)] * 4 + [(ti, CHUNK, F32)] * 4 + [(ti // 8, 512, F32)] + [(ti, CHUNK, F32)] * 4,
                     S // ti, "gdn_intra")
    u, wk, qd, kd, qks, gl, invs = intra[0], intra[1], intra[2], intra[3], tuple(intra[4:8]), intra[8], tuple(intra[9:])
    o_a, s_prev = _gdn_scan_fwd(u, wk, qd, kd, qks, gl, "gdn_scan")
    mla_params = [W["qnw"], W["kvnw"], W["wuq"], W["wukv"], W["qn_w"], W["qr_w"], W["kn_w"], W["kr_w"]]
    def mla_pre_with_vt(*a):
        q_, k_, v_ = _mla_pre_fn(*a)
        return q_, k_, v_, jnp.transpose(v_)

    q_b, k_b, v_b, vt_b = _rowwise(mla_pre_with_vt,
                                   [(proj, tv, 256, 8), (proj, tv, 384, 6), kab, (cos_p, tv, 128, 0), (sin_p, tv, 128, 0)],
                                   mla_params, [(tv, 1024, BF16), (tv, 1024, BF16), (tv, 512, BF16), (512, tv, BF16, "across")],
                                   S // tv, "mla_pre")
    o_b, lse = _attn_fwd(q_b, k_b, vt_b, ta, "mla_attn")
    (mixed,) = _rowwise(_mix_post_fn, [(o_a, tv, 512, 0), (proj, tv, 512, 3), (o_b, tv, 512, 0)], [W["gnw"], W["onw"]],
                        [(tv, D_MODEL, BF16)], S // tv, "mix_post")
    y, x2 = _mm(mixed, W["wout"], "nn", name="mix_out", out_dtypes=(F32, F32), extras=[x1], extra_params=[gate_w],
                epi=lambda acc, x_, g_: (acc, x_ + g_ * acc))
    saved = (h2, proj, qkvc, q_a, k_a, v_a, gb, u, wk, qd, kd, qks, gl, invs, s_prev, o_a, q_b, k_b, v_b, o_b, lse, mixed, y)
    return x2, saved


def _mixer_bwd(d_out, dy, x1, scale, shift, cos_p, sin_p, W, saved, below):
    (h2, proj, qkvc, q_a, k_a, v_a, gb, u, wk, qd, kd, qks, gl, invs, s_prev, o_a, q_b, k_b, v_b, o_b, lse, mixed, y) = saved
    S = x1.shape[0]
    tm = _pick(S, (512, 256, 128))
    tv = _pick(S, (256, 128))
    ta = _pick(S, (512, 256, 128))
    nc = S // CHUNK
    G = {}
    d_mixed = _mm(dy, W["wout"], "nt", name="mix_dout")
    G["wout"] = _mm(mixed, dy, "tn", name="mix_gwout")
    (do_a, dz, do_b), (G["gnw"], G["onw"]) = _rowwise_bwd(
        _mix_post_fn, [(o_a, tv, 512, 0), (proj, tv, 512, 3), (o_b, tv, 512, 0)], [], [W["gnw"], W["onw"]],
        [(d_mixed, tv, D_MODEL, 0)], S // tv, "mix_dpost", row_dtypes=(F32, BF16, F32))
    stats = _attn_stats(o_b, lse, do_b, ta, "mla_stats")
    dq_b, dk_b, dv_b = _attn_bwd(q_b, k_b, v_b, do_b, stats, ta, "mla_dattn")
    kab = (proj, tv, 128, 21)
    mla_params = [W["qnw"], W["kvnw"], W["wuq"], W["wukv"], W["qn_w"], W["qr_w"], W["kn_w"], W["kr_w"]]
    (d_ckv, d_cq, d_kab), mla_grads = _rowwise_bwd(
        _mla_pre_fn, [(proj, tv, 256, 8), (proj, tv, 384, 6), kab], [(cos_p, tv, 128, 0), (sin_p, tv, 128, 0)], mla_params,
        [(dq_b, tv, 1024, 0), (dk_b, tv, 1024, 0), (dv_b, tv, 512, 0)], S // tv, "mla_dpre", row_dtypes=(BF16, BF16, F32))
    for key, g in zip(("qnw", "kvnw", "wuq", "wukv", "qn_w", "qr_w", "kn_w", "kr_w"), mla_grads):
        G[key] = g
    scan_grads = _gdn_scan_bwd(u, wk, qd, kd, qks, gl, s_prev, do_a, "gdn_dscan")
    ti = _pick(S, INTRA_ROWS)
    intra_douts = [(scan_grads[i], ti, 512, 0) for i in range(4)] + [(scan_grads[4 + i], ti, CHUNK, 0) for i in range(4)]
    intra_douts.append((scan_grads[8], ti // 8, 512, 0))
    (dq_a, dk_a, dv_a, d_gb), _ = _rowwise_bwd(
        _gdn_intra_fn, [(q_a, ti, 512, 0), (k_a, ti, 512, 0), (v_a, ti, 512, 0), (gb, ti, 128, 0)],
        [(x_, ti, CHUNK, 0) for x_ in invs], [], intra_douts, S // ti, "gdn_dintra")
    (d_qkvc, d_kab), (G["alog_p"], G["dt_p"]) = _rowwise_bwd(
        _gdn_pre_fn, [(qkvc, tv, 1536, 0), kab], [], [W["alog_p"], W["dt_p"]],
        [(dq_a, tv, 512, 0), (dk_a, tv, 512, 0), (dv_a, tv, 512, 0), (d_gb, tv, 128, 0)], S // tv, "gdn_dpre",
        adds=[(1, d_kab)], row_dtypes=(F32, BF16))
    d_qkv, g_conv = _conv_bwd(proj, d_qkvc, W["conv_w"], tv, "gdn_dconv")
    G["conv_w"] = g_conv[:4]
    d_proj = jnp.concatenate([d_qkv, dz, d_ckv, d_cq, d_kab], axis=1)
    G["wp"] = _mm(h2, d_proj, "tn", name="mix_gwp")
    dx1, G["s2"], G["sh2"], d_below, dg_below = _dproj_dmod(d_proj, W["wp"], x1, d_out, scale, shift, "mix_dproj", below=below)
    return dx1, d_below, dg_below, G


def _local_step(x, target, mod, cos_p, sin_p, W1, mixer_weights, ffn2_weights, ffn_grad_ready, mixer_grads_ready):
    sh1, s1, g1, sh2, s2, g2, sh3, s3, g3 = [mod[:, D_MODEL * i:D_MODEL * (i + 1)] for i in range(N_MOD)]
    x1, saved1 = _ffn_fwd(x, s1, sh1, g1, W1["f1_w8"], W1["f1_wo4"], "ffn1")
    W = mixer_weights(x1)
    x2, saved2 = _mixer_fwd(x1, s2, sh2, g2, cos_p, sin_p, W)
    W.update(ffn2_weights(x2))
    (dx3, loss_row), saved3 = _ffn_fwd(x2, s3, sh3, g3, W["f2_w8"], W["f2_wo4"], "ffn2", target=target)
    dx2, d_s3, d_sh3, d_g3, dy, d_g2 = _ffn_bwd(dx3, x2, s3, sh3, g3, W["f2_w8"], W["f2_wo4"], saved3, "ffn2",
                                                ffn_grad_ready("f2"), below=(saved2[-1], g2, 1.0))
    dx1, df1, d_g1, G = _mixer_bwd(dx2, dy, x1, s2, sh2, cos_p, sin_p, W, saved2, below=(saved1[4], g1, 0.5))
    d_sh2, d_s2 = G.pop("sh2"), G.pop("s2")
    saved1 = saved1[:5] + (df1, d_g1 + mixer_grads_ready(G))
    dx, d_s1, d_sh1, d_g1 = _ffn_bwd(dx1, x, s1, sh1, g1, W1["f1_w8"], W1["f1_wo4"], saved1, "ffn1", ffn_grad_ready("f1"))
    d_mod = jnp.concatenate([d_sh1, d_s1, d_g1, d_sh2, d_s2, d_g2, d_sh3, d_s3, d_g3], axis=1)
    return loss_row, dx, d_mod


WEIGHT_NAMES = ("w_ada", "b_ada", "ffn1_w_in", "ffn1_w_out", "w_in", "gdn_conv_w", "gdn_a_log", "gdn_dt_bias", "gdn_norm_w",
                "mla_q_norm_w", "mla_w_uq", "mla_kv_norm_w", "mla_w_ukv", "qkn_q_nope", "qkn_q_rope", "qkn_k_nope",
                "qkn_k_rope", "mla_out_norm_w", "w_out", "ffn2_w_in", "ffn2_w_out")
FFN_SHARDED = ("ffn1_w_in", "ffn1_w_out", "ffn2_w_in", "ffn2_w_out")
TRANSPOSED_ENTRY = ("ffn1_w_in", "ffn2_w_in", "w_in", "mla_w_uq")
SHEETED = (("w_in", "col"), ("gdn_conv_w", "col"), ("mla_w_uq", "col"), ("mla_w_ukv", "col"), ("w_out", "row"))
MOD_ROWS = N_MOD * D_MODEL // 128
SMALL = {"gdn_a_log": (MOD_ROWS, 1, 64, 4), "gdn_dt_bias": (MOD_ROWS + 1, 1, 64, 4), "gdn_norm_w": (MOD_ROWS + 2, 1, 0, 128),
         "mla_q_norm_w": (MOD_ROWS + 3, 3, 0, 384), "mla_kv_norm_w": (MOD_ROWS + 6, 2, 0, 256),
         "qkn_q_nope": (MOD_ROWS + 8, 1, 0, 128), "qkn_q_rope": (MOD_ROWS + 9, 1, 0, 64), "qkn_k_nope": (MOD_ROWS + 10, 1, 0, 128),
         "qkn_k_rope": (MOD_ROWS + 11, 1, 0, 64), "mla_out_norm_w": (MOD_ROWS + 12, 1, 0, 128)}
LOSS_ROW = MOD_ROWS + 13
CONV_ROW, CONV_ROWS = 88, 4 * 1536 // 128
SHEET_ROWS = CONV_ROW + CONV_ROWS


def _to_sheet(flat, dtype, sublanes):
    n = flat.shape[-1]
    unit = sublanes * 128
    pad = (-n) % unit
    flat = jnp.pad(flat.astype(dtype), [(0, 0)] * (flat.ndim - 1) + [(0, pad)])
    return flat.reshape(flat.shape[:-1] + ((n + pad) // 128, 128))


def _small_sheet(b_like, small):
    sheet = jnp.zeros((SHEET_ROWS, 128), F32).at[:MOD_ROWS].set(b_like.reshape(MOD_ROWS, 128))
    for name, (row, rows, lane, n) in SMALL.items():
        v = small[name].reshape(1, n)
        if rows == 1:
            sheet = sheet.at[row, lane:lane + n].set(v[0])
        else:
            sheet = sheet.at[row:row + rows].set(v.reshape(rows, 128))
    return sheet


def _from_small_sheet(sheet):
    out = {"b_ada": sheet[:MOD_ROWS].reshape(1, N_MOD * D_MODEL)}
    for name, (row, rows, lane, n) in SMALL.items():
        out[name] = sheet[row, lane:lane + n].reshape(1, n) if rows == 1 else sheet[row:row + rows].reshape(1, n)
    return out


def kernel(x, c, positions, w_ada, b_ada, ffn1_w_in, ffn1_w_out, w_in, gdn_conv_w, gdn_a_log, gdn_dt_bias, gdn_norm_w, mla_q_norm_w, mla_w_uq, mla_kv_norm_w, mla_w_ukv, qkn_q_nope, qkn_q_rope, qkn_k_nope, qkn_k_rope, mla_out_norm_w, w_out, ffn2_w_in, ffn2_w_out, loss_target, m_w_ada, m_b_ada, m_ffn1_w_in, m_ffn1_w_out, m_w_in, m_gdn_conv_w, m_gdn_a_log, m_gdn_dt_bias, m_gdn_norm_w, m_mla_q_norm_w, m_mla_w_uq, m_mla_kv_norm_w, m_mla_w_ukv, m_qkn_q_nope, m_qkn_q_rope, m_qkn_k_nope, m_qkn_k_rope, m_mla_out_norm_w, m_w_out, m_ffn2_w_in, m_ffn2_w_out, v_w_ada, v_b_ada, v_ffn1_w_in, v_ffn1_w_out, v_w_in, v_gdn_conv_w, v_gdn_a_log, v_gdn_dt_bias, v_gdn_norm_w, v_mla_q_norm_w, v_mla_w_uq, v_mla_kv_norm_w, v_mla_w_ukv, v_qkn_q_nope, v_qkn_q_rope, v_qkn_k_nope, v_qkn_k_rope, v_mla_out_norm_w, v_w_out, v_ffn2_w_in, v_ffn2_w_out):
    args = locals()
    w = {n: args[n] for n in WEIGHT_NAMES}
    m = {n: args["m_" + n] for n in WEIGHT_NAMES}
    v = {n: args["v_" + n] for n in WEIGHT_NAMES}
    me = 4 * lax.axis_index("x") + 2 * lax.axis_index("y") + lax.axis_index("c")
    cols = N_MOD * D_MODEL // N_DEV
    shard = {n: w[n][0] for n in FFN_SHARDED + tuple(s[0] for s in SHEETED)}

    sc = c * _sigmoid(c)
    first = _to_sheet(jnp.concatenate([sc.reshape(-1), shard["gdn_conv_w"].reshape(-1)]), F32, 8)
    (first_all,) = _all_gather([first], "gather_c")
    sc_all = first_all[:, :D_MODEL // 128].reshape(N_DEV, D_MODEL)
    n_taps = shard["gdn_conv_w"].size
    conv_all = first_all.reshape(N_DEV, -1)[:, D_MODEL:D_MODEL + n_taps].reshape(N_DEV, 4, -1)
    b_mine = lax.dynamic_slice(b_ada, (0, me * cols), (1, cols))
    mod_cols = _mm(sc_all, w_ada[0], "nn", name="ada_mod", extra_params=[b_mine], epi=lambda acc, b_: (acc + b_,))
    (mod_all,) = _all_to_all([_to_sheet(mod_cols, F32, 8)], "scatter_mod")
    mod = mod_all.reshape(N_DEV, -1)[:, :cols].reshape(1, N_MOD * D_MODEL)

    f1_shards, mod = lax.optimization_barrier(([shard["ffn1_w_in"].astype(BF16), shard["ffn1_w_out"].astype(BF16)], mod))
    f1_w8, f1_out = _all_gather(f1_shards, "gather_w1")
    travel = [s for s in SHEETED if s[0] != "gdn_conv_w"]
    tied = lax.optimization_barrier(([shard[n].astype(BF16) for n, _ in travel], f1_w8))
    f1_w8 = tied[1]
    mixer_w = _exchange_start(tied[0], False, "gather_wm_start")
    ffn2_w = _exchange_start([shard["ffn2_w_in"].astype(BF16) + mixer_w[4][0:1, 0:1].astype(BF16),
                              shard["ffn2_w_out"].astype(BF16)], False, "gather_w2_start")
    mod = mod + ffn2_w[4][0:1, 0:1]
    W1 = dict(f1_w8=f1_w8, f1_wo4=f1_out.reshape(HID_PIECES, FFN_PIECE, D_MODEL))

    def mixer_weights(after):
        got = _exchange_wait(mixer_w, False, after, "gather_wm_wait")
        P = {n: jnp.concatenate(list(g), axis=1) if kind == "col" else g.reshape(-1, g.shape[-1])
             for (n, kind), g in zip(travel, got)}
        P["gdn_conv_w"] = jnp.concatenate(list(conv_all), axis=1)
        for n in SMALL:
            P[n] = w[n]
        return _pack_weights(P)

    def ffn2_weights(after):
        f2_w8, f2_out = _exchange_wait(ffn2_w, False, after, "gather_w2_wait")
        return dict(f2_w8=f2_w8, f2_wo4=f2_out.reshape(HID_PIECES, FFN_PIECE, D_MODEL))

    pending, small_grads = {}, {}

    def ffn_grad_ready(tag):
        def ready(which, g):
            pieces = g if which == "w8" else g.reshape((N_DEV,) + shard["ffn1_w_out"].shape)
            pending[tag + which] = _exchange_start([pieces], True, "scatter_%s_%s_start" % (tag, which))
            return pending[tag + which][4]
        return ready

    def mixer_grads_ready(G):
        g_full = _unpack_grads(G)
        small_grads.update({n: g_full[n] for n in SMALL})
        small_grads["gdn_conv_w"] = g_full["gdn_conv_w"]
        pieces = []
        for n, kind in travel:
            r, cc = shard[n].shape
            g = g_full[n].astype(BF16)
            pieces.append(jnp.stack([g[:, cc * p:cc * (p + 1)] for p in range(N_DEV)]) if kind == "col"
                          else g.reshape(N_DEV, r, cc))
        pending["mixer"] = _exchange_start(pieces, True, "scatter_mx_start")
        return pending["mixer"][4][0:1, 0:1]

    cos_p, sin_p = _rope_tables(positions[0])
    loss_row, dx, d_mod = _local_step(x[0], loss_target[0], mod, cos_p, sin_p, W1, mixer_weights, ffn2_weights,
                                      ffn_grad_ready, mixer_grads_ready)

    sheet = _small_sheet(d_mod, small_grads).at[LOSS_ROW].set(loss_row[0, :128])
    sheet = sheet.at[CONV_ROW:CONV_ROW + CONV_ROWS].set(small_grads["gdn_conv_w"].reshape(CONV_ROWS, 128))
    (sheets,) = _all_gather([sheet], "gather_small")
    summed = _sum_devices(sheets, "sum_small")
    d_mod_all = sheets[:, :MOD_ROWS].reshape(N_DEV, N_MOD * D_MODEL)
    d_mod_mine = lax.dynamic_slice(d_mod_all, (0, me * cols), (N_DEV, cols))
    grads = _from_small_sheet(summed)
    grads["w_ada"] = _mm(sc_all, d_mod_mine, "tn", name="ada_gw", hi=True)
    conv_taps = shard["gdn_conv_w"].shape[1]
    grads["gdn_conv_w"] = lax.dynamic_slice(summed[CONV_ROW:CONV_ROW + CONV_ROWS].reshape(4, -1), (0, me * conv_taps),
                                            (4, conv_taps))
    loss = summed[LOSS_ROW, 0]

    delta, new_m, new_v = {}, {}, {}
    arrived = {}
    for n, key in zip(FFN_SHARDED, ("f1w8", "f1wo4", "f2w8", "f2wo4")):
        (arrived[n],) = _exchange_wait(pending[key], True, summed, "scatter_%s_wait" % key)
    arrived.update(zip([n for n, _ in travel], _exchange_wait(pending["mixer"], True, summed, "scatter_mx_wait")))
    for n, parts in arrived.items():
        if n in TRANSPOSED_ENTRY:
            res = _sum_adamw(parts, w[n][0].T, m[n][0].T, v[n][0].T, "adamw_" + n, transposed=True)
            grads[n], delta[n], new_m[n], new_v[n] = [r.T for r in res]
        else:
            grads[n], delta[n], new_m[n], new_v[n] = _sum_adamw(parts, w[n][0], m[n][0], v[n][0], "adamw_" + n)
    for n in ("w_ada", "gdn_conv_w"):
        delta[n], new_m[n], new_v[n] = _adamw(w[n][0], grads[n], m[n][0], v[n][0], "adamw_" + n)
    small_in = [_small_sheet(t["b_ada"], t) for t in (w, grads, m, v)]
    for res, out in zip(_adamw(*small_in, "adamw_small"), (delta, new_m, new_v)):
        out.update(_from_small_sheet(res))

    def shaped(d):
        return [d[n].reshape(w[n].shape) for n in WEIGHT_NAMES]

    return (loss, dx[None], *shaped(grads), *shaped(delta), *shaped(new_m), *shaped(new_v))
```

```python
import functools

import jax
import jax.numpy as jnp
import numpy as np
from jax import lax
from jax.experimental import pallas as pl
from jax.experimental.pallas import tpu as pltpu

F32 = jnp.float32
BF16 = jnp.bfloat16

D_MODEL = 1024
D_FF = 2816
N_MOD = 9
HEADS = 4
HEAD_DIM = 128
CHUNK = 64
EPS = 1e-6
ROPE = 64
Q_LORA = 384
KV_LORA = 256
N_IN = 2760
N_IN_PACKED = 2816
ROPE_BASE = 10000.0
LOG2_E = 1.4426950408889634
N_DEV = 8

ADAM_LR = 0.001
ADAM_B1 = 0.9
ADAM_B2 = 0.999
ADAM_EPS = 1e-08
ADAM_WD = 0.01
ADAM_STEP = 10

VMEM_LIMIT_BYTES = 56 * 1024 * 1024
MATMUL_ROWS = (1024, 512, 256, 128)
MESH = pl.DeviceIdType.MESH


def _params(sem=None):
    return pltpu.CompilerParams(dimension_semantics=sem, vmem_limit_bytes=VMEM_LIMIT_BYTES)


def _pick(dim, prefs):
    for p in prefs:
        if dim % p == 0:
            return p
    return dim


_DIMS = {"nn": (((1,), (0,)), ((), ())), "nt": (((1,), (1,)), ((), ())), "tn": (((0,), (0,)), ((), ()))}


def _dot_raw(a, b, mode):
    return lax.dot_general(a.astype(BF16), b.astype(BF16), _DIMS[mode], preferred_element_type=F32)


def _dot_hi(a, b, mode="nn"):
    return lax.dot_general(a, b, _DIMS[mode], precision=lax.Precision.HIGHEST, preferred_element_type=F32)


@functools.partial(jax.custom_vjp, nondiff_argnums=(2,))
def _bdot(a, b, mode):
    return _dot_raw(a, b, mode)


def _bdot_fwd(a, b, mode):
    return _dot_raw(a, b, mode), (a, b)


def _bdot_bwd(mode, res, g):
    a, b = res
    if mode == "nn":
        return _dot_raw(g, b, "nt"), _dot_raw(a, g, "tn")
    if mode == "nt":
        return _dot_raw(g, b, "nn"), _dot_raw(g, a, "tn")
    return _dot_raw(b, g, "nt"), _dot_raw(a, g, "nn")


_bdot.defvjp(_bdot_fwd, _bdot_bwd)


def _mm(a, b, mode, *, name, out_dtypes=(F32,), epi=None, extras=(), extra_params=(), hi=False,
        tm=None, tn=None, tk=None):
    if mode == "nn":
        (M, K), (_, N) = a.shape, b.shape
    elif mode == "nt":
        (M, K), (N, _) = a.shape, b.shape
    else:
        (K, M), (_, N) = a.shape, b.shape
    tm = tm or _pick(M, (512, 1408, 256, 128) if mode == "tn" else MATMUL_ROWS + (384, 352))
    tn = tn or _pick(N, (1024, 1408, 768, 512, 384, 256, 128))
    tk = tk or _pick(K, (1024, 1408, 512, 384, 256, 128))
    a_spec = {"nn": pl.BlockSpec((tm, tk), lambda i, j, k: (i, k)), "nt": pl.BlockSpec((tm, tk), lambda i, j, k: (i, k)),
              "tn": pl.BlockSpec((tk, tm), lambda i, j, k: (k, i))}[mode]
    b_spec = {"nn": pl.BlockSpec((tk, tn), lambda i, j, k: (k, j)), "nt": pl.BlockSpec((tn, tk), lambda i, j, k: (j, k)),
              "tn": pl.BlockSpec((tk, tn), lambda i, j, k: (k, j))}[mode]
    mn_spec = pl.BlockSpec((tm, tn), lambda i, j, k: (i, j))
    return _mmg(a, b, mode, name=name, grid=(M // tm, N // tn, K // tk), a_spec=a_spec, b_spec=b_spec, out_spec=mn_spec,
                out_shapes=[jax.ShapeDtypeStruct((M, N), dt) for dt in out_dtypes], acc_shape=(tm, tn), epi=epi,
                extras=list(extras) + list(extra_params),
                extra_specs=[mn_spec] * len(extras) + [pl.BlockSpec((1, tn), lambda i, j, k: (0, j))] * len(extra_params),
                hi=hi)


def _mmg(a, b, mode, *, name, grid, a_spec, b_spec, out_spec, out_shapes, acc_shape, epi=None, extras=(),
         extra_specs=(), hi=False):
    nk = grid[2]
    n_e, n_o = len(extras), len(out_shapes)

    def body(*refs):
        a_ref, b_ref = refs[:2]
        e_refs = refs[2:2 + n_e]
        o_refs = refs[2 + n_e:2 + n_e + n_o]
        acc_ref = refs[-1]
        k = pl.program_id(2)

        @pl.when(k == 0)
        def _():
            acc_ref[...] = jnp.zeros_like(acc_ref)

        if hi:
            acc_ref[...] += _dot_hi(a_ref[...].astype(F32), b_ref[...].astype(F32), mode)
        else:
            acc_ref[...] += _dot_raw(a_ref[...], b_ref[...], mode)

        @pl.when(k == nk - 1)
        def _():
            acc = acc_ref[...]
            outs = (acc,) if epi is None else epi(acc, *[e[...].astype(F32) for e in e_refs])
            for o_ref, o in zip(o_refs, outs):
                o_ref[...] = o.astype(o_ref.dtype)

    outs = pl.pallas_call(
        body, name=name, grid=grid,
        in_specs=[a_spec, b_spec] + list(extra_specs),
        out_specs=[out_spec] * n_o,
        out_shape=list(out_shapes),
        scratch_shapes=[pltpu.VMEM(acc_shape, F32)],
        compiler_params=_params(("parallel", "parallel", "arbitrary")),
    )(a, b, *extras)
    return outs if n_o > 1 else outs[0]


def _row_spec(th, cw, ci):
    return pl.BlockSpec((th, cw), lambda i: (i, ci))


def _full_spec(shape):
    return pl.BlockSpec(shape, lambda i: (0,) * len(shape))


def _rowwise(fn, rows, params, outs, n_steps, name):
    n_r, n_p, n_o = len(rows), len(params), len(outs)

    def body(*refs):
        vals = [r[...].astype(F32) for r in refs[:n_r + n_p]]
        res = fn(*vals)
        for o_ref, o in zip(refs[n_r + n_p:], res):
            o_ref[...] = o.astype(o_ref.dtype)

    across = [len(o) == 4 for o in outs]
    res = pl.pallas_call(
        body, name=name, grid=(n_steps,),
        in_specs=[_row_spec(th, cw, ci) for (_, th, cw, ci) in rows] + [_full_spec(p.shape) for p in params],
        out_specs=[pl.BlockSpec((o[0], o[1]), lambda i: (0, i)) if ac else _row_spec(o[0], o[1], 0)
                   for o, ac in zip(outs, across)],
        out_shape=[jax.ShapeDtypeStruct((o[0], n_steps * o[1]) if ac else (n_steps * o[0], o[1]), o[2])
                   for o, ac in zip(outs, across)],
        compiler_params=_params(("parallel",)),
    )(*[r[0] for r in rows], *params)
    return res


def _rowwise_bwd(fn, rows, aux, params, douts, n_steps, name, row_dtypes=None, adds=()):
    n_r, n_a, n_p, n_d, n_add = len(rows), len(aux), len(params), len(douts), len(adds)
    row_dtypes = row_dtypes or (F32,) * n_r

    def body(*refs):
        it = iter(refs)
        r_vals = [next(it)[...].astype(F32) for _ in range(n_r)]
        a_vals = [next(it)[...].astype(F32) for _ in range(n_a)]
        p_vals = [next(it)[...].astype(F32) for _ in range(n_p)]
        d_vals = [next(it)[...].astype(F32) for _ in range(n_d)]
        add_vals = [next(it)[...].astype(F32) for _ in range(n_add)]
        dr_refs = [next(it) for _ in range(n_r)]
        dp_refs = [next(it) for _ in range(n_p)]

        def f(*rp):
            return tuple(fn(*rp[:n_r], *a_vals, *rp[n_r:]))

        _, vjp = jax.vjp(f, *r_vals, *p_vals)
        grads = list(vjp(tuple(d_vals)))
        for (ri, _), av in zip(adds, add_vals):
            grads[ri] = grads[ri] + av
        for dr_ref, g in zip(dr_refs, grads[:n_r]):
            dr_ref[...] = g.astype(dr_ref.dtype)

        @pl.when(pl.program_id(0) == 0)
        def _():
            for dp_ref in dp_refs:
                dp_ref[...] = jnp.zeros_like(dp_ref)

        for dp_ref, g in zip(dp_refs, grads[n_r:]):
            dp_ref[...] += g

    all_rows = list(rows) + list(aux) + list(douts) + [(arr,) + tuple(rows[ri][1:3]) + (0,) for ri, arr in adds]
    in_specs = ([_row_spec(th, cw, ci) for (_, th, cw, ci) in list(rows) + list(aux)]
                + [_full_spec(p.shape) for p in params]
                + [_row_spec(th, cw, ci) for (_, th, cw, ci) in all_rows[n_r + n_a:]])
    res = pl.pallas_call(
        body, name=name, grid=(n_steps,),
        in_specs=in_specs,
        out_specs=[_row_spec(th, cw, 0) for (_, th, cw, _) in rows] + [_full_spec(p.shape) for p in params],
        out_shape=[jax.ShapeDtypeStruct((n_steps * th, cw), dt) for (_, th, cw, _), dt in zip(rows, row_dtypes)]
        + [jax.ShapeDtypeStruct(p.shape, F32) for p in params],
        compiler_params=_params(("arbitrary",)),
    )(*[r[0] for r in list(rows) + list(aux)], *params, *[r[0] for r in all_rows[n_r + n_a:]])
    return res[:n_r], res[n_r:]


def _sigmoid(x):
    return lax.logistic(x)


def _silu(x):
    return x * _sigmoid(x)


def _rms(x, w=None, n=None):
    n = n or x.shape[-1]
    y = x * lax.rsqrt(jnp.sum(x * x, axis=-1, keepdims=True) * (1.0 / n) + EPS)
    return y if w is None else y * w


def _modulate(x, scale, shift):
    return _rms(x) * (1.0 + scale) + shift


def _softplus(x):
    return jnp.maximum(x, 0.0) + jnp.log1p(jnp.exp(-jnp.abs(x)))


@jax.custom_vjp
def _rot_half64(x):
    lane = lax.broadcasted_iota(jnp.int32, x.shape, 1)
    up = pltpu.roll(x, 96, 1)
    down = pltpu.roll(x, 32, 1)
    return jnp.where(lane < 32, up, jnp.where(lane < 64, down, 0.0))


_rot_half64.defvjp(lambda x: (_rot_half64(x), None), lambda _, g: (_rot_half64(g),))


def _rope128(x, cos_p, sin_p):
    return x * cos_p + _rot_half64(x) * sin_p


def _gdn_pre_fn(qkvc, kab, alog_p, dt_p):
    a = _silu(qkvc)
    qs, ks = [], []
    for h in range(HEADS):
        qh = a[:, HEAD_DIM * h:HEAD_DIM * (h + 1)]
        kh = a[:, 512 + HEAD_DIM * h:512 + HEAD_DIM * (h + 1)]
        qs.append(qh * lax.rsqrt(jnp.sum(qh * qh, axis=-1, keepdims=True) + EPS) * (HEAD_DIM ** -0.5))
        ks.append(kh * lax.rsqrt(jnp.sum(kh * kh, axis=-1, keepdims=True) + EPS))
    lane = lax.broadcasted_iota(jnp.int32, kab.shape, 1)
    g_full = -jnp.exp(alog_p) * _softplus(kab + dt_p)
    b_full = _sigmoid(kab)
    gb = jnp.where((lane >= 64) & (lane < 68), g_full, jnp.where((lane >= 68) & (lane < 72), b_full, 0.0))
    return jnp.concatenate(qs, axis=1), jnp.concatenate(ks, axis=1), a[:, 1024:1536], gb


INTRA_ROWS = (256, 128, 64)

_BNN = (((2,), (1,)), ((0,), (0,)))
_BNT = (((2,), (2,)), ((0,), (0,)))


def _split_bf16(a):
    hi = a.astype(BF16)
    return hi, (a - hi.astype(F32)).astype(BF16)


def _dot3_raw(a, b, dims):
    a_hi, a_lo = _split_bf16(a)
    b_hi, b_lo = _split_bf16(b)
    dot = lambda x_, y_: lax.dot_general(x_, y_, dims, preferred_element_type=F32)
    return dot(a_hi, b_hi) + (dot(a_hi, b_lo) + dot(a_lo, b_hi))


@functools.partial(jax.custom_vjp, nondiff_argnums=(2, 3))
def _dot3(a, b, nt, exact_bwd=True):
    return _dot3_raw(a, b, _BNT if nt else _BNN)


def _dot3_fwd(a, b, nt, exact_bwd):
    return _dot3_raw(a, b, _BNT if nt else _BNN), (a, b)


def _dot3_bwd(nt, exact_bwd, res, g):
    a, b = res
    if exact_bwd:
        dot = _dot3_raw
    else:
        dot = lambda x_, y_, d_: lax.dot_general(x_.astype(BF16), y_.astype(BF16), d_, preferred_element_type=F32)
    if nt:
        return dot(g, b, _BNN), dot(jnp.swapaxes(g, 1, 2), a, _BNN)
    return dot(g, b, _BNT), dot(jnp.swapaxes(a, 1, 2), g, _BNN)


_dot3.defvjp(_dot3_fwd, _dot3_bwd)


@functools.partial(jax.custom_vjp, nondiff_argnums=(2,))
def _bdot_b(a, b, nt):
    return lax.dot_general(a.astype(BF16), b.astype(BF16), _BNT if nt else _BNN, preferred_element_type=F32)


def _bdot_b_fwd(a, b, nt):
    return _bdot_b(a, b, nt), (a, b)


def _bdot_b_bwd(nt, res, g):
    a, b = res
    dot = lambda x_, y_, d_: lax.dot_general(x_.astype(BF16), y_.astype(BF16), d_, preferred_element_type=F32)
    if nt:
        return dot(g, b, _BNN), dot(jnp.swapaxes(g, 1, 2), a, _BNN)
    return dot(g, b, _BNT), dot(jnp.swapaxes(a, 1, 2), g, _BNN)


_bdot_b.defvjp(_bdot_b_fwd, _bdot_b_bwd)


@jax.custom_vjp
def _inverse_given(a_mat, inv):
    return inv


def _inverse_given_bwd(inv, g):
    inv_t = jnp.swapaxes(inv, 1, 2)
    return -_dot3_raw(_dot3_raw(inv_t, g, _BNN), inv_t, _BNN), jnp.zeros_like(inv)


_inverse_given.defvjp(lambda a_mat, inv: (inv, inv), _inverse_given_bwd)


def _intra_batched(q, k, v, g_col, b_col, inv_known=None):
    c = CHUNK
    nb = q.shape[0]
    row = lax.broadcasted_iota(jnp.int32, (1, c, c), 1)
    col = lax.broadcasted_iota(jnp.int32, (1, c, c), 2)
    incl, strict, eye = row >= col, row > col, row == col
    tri = jnp.broadcast_to(jnp.where(incl, 1.0, 0.0).astype(F32), (nb, c, c))
    ident = jnp.where(eye, 1.0, 0.0).astype(F32)
    g_wide = _dot3(tri, jnp.broadcast_to(g_col, (nb, c, HEAD_DIM)), False)
    g_i = g_wide[:, :, :c]
    g_j = jnp.sum(jnp.where(eye, g_i, 0.0), axis=1, keepdims=True)
    decay = jnp.where(incl, jnp.exp(jnp.where(incl, g_i - g_j, 0.0)), 0.0)
    kk = _bdot_b(k, k, True)
    a_mat = jnp.where(strict, b_col * kk * decay, 0.0)
    if inv_known is None:
        x_pow = -a_mat
        inv = ident + x_pow
        for _ in range(5):
            x_pow = _dot3(x_pow, x_pow, False, False)
            inv = inv + _dot3(inv, x_pow, False, False)
    else:
        inv = _inverse_given(a_mat, inv_known)
    e_wide = jnp.exp(g_wide)
    u = _dot3(inv, v * b_col, False)
    wk = _dot3(inv, k * b_col * e_wide, False)
    qk = _bdot_b(q, k, True) * decay
    last = lax.broadcasted_iota(jnp.int32, (1, c, HEAD_DIM), 1) == c - 1
    g_last = jnp.sum(jnp.where(last, g_wide, 0.0), axis=1, keepdims=True)
    qd = q * e_wide
    kd = k * jnp.exp(g_last - g_wide)
    gl = jnp.broadcast_to(jnp.exp(g_last), (nb, 8, HEAD_DIM))
    return u, wk, qd, kd, qk, gl, inv


def _gdn_intra_fn(q, k, v, gb, *inv_known):
    t = q.shape[0]
    nch = t // CHUNK
    lane = lax.broadcasted_iota(jnp.int32, gb.shape, 1)

    def heads_first(x_):
        return jnp.concatenate([x_[:, HEAD_DIM * h:HEAD_DIM * (h + 1)].reshape(nch, CHUNK, HEAD_DIM) for h in range(HEADS)],
                               axis=0)

    def column(first_lane):
        return jnp.concatenate([jnp.sum(jnp.where(lane == first_lane + h, gb, 0.0), axis=1, keepdims=True)
                                .reshape(nch, CHUNK, 1) for h in range(HEADS)], axis=0)

    known = jnp.concatenate([x_.reshape(nch, CHUNK, CHUNK) for x_ in inv_known], axis=0) if inv_known else None
    u, wk, qd, kd, qk, gl, inv = _intra_batched(heads_first(q), heads_first(k), heads_first(v), column(64), column(68), known)

    def rows_first(x_):
        r, w_ = x_.shape[1], x_.shape[2]
        return jnp.concatenate([x_[nch * h:nch * (h + 1)].reshape(nch * r, w_) for h in range(HEADS)], axis=1)

    per_head = lambda x_: [x_[nch * h:nch * (h + 1)].reshape(t, CHUNK) for h in range(HEADS)]
    outs = (rows_first(u), rows_first(wk), rows_first(qd), rows_first(kd), *per_head(qk), rows_first(gl))
    return outs if inv_known else outs + tuple(per_head(inv))


def _scan_step(s0, u, wk, qd, kd, qk, gl):
    v_new = u - _bdot_b(wk, s0, False)
    o = _bdot_b(qd, s0, False) + _bdot_b(qk, v_new, False)
    s1 = s0 * gl[:, 0:1, :] + _bdot_b(jnp.swapaxes(kd, 1, 2), v_new, False)
    return o, s1


def _mix_post_fn(o_a, z, o_b, gnw, onw):
    parts = [_rms(o_a[:, HEAD_DIM * h:HEAD_DIM * (h + 1)], gnw) * _silu(z[:, HEAD_DIM * h:HEAD_DIM * (h + 1)])
             for h in range(HEADS)]
    parts += [_rms(o_b[:, HEAD_DIM * h:HEAD_DIM * (h + 1)], onw) for h in range(HEADS)]
    return (jnp.concatenate(parts, axis=1),)


def _mla_pre_fn(ckv, cq, kab, cos_p, sin_p, qnw, kvnw, wuq, wukv, qn_w, qr_w, kn_w, kr_w):
    scale = (HEAD_DIM + ROPE) ** -0.5 * LOG2_E
    qf = _bdot(_rms(cq, qnw), wuq, "nn")
    kvf = _bdot(_rms(ckv, kvnw), wukv, "nn")
    lane = lax.broadcasted_iota(jnp.int32, kab.shape, 1)
    kr = _rope128(_rms(jnp.where(lane < ROPE, kab, 0.0), kr_w, n=ROPE), cos_p, sin_p)
    qs, ks = [], []
    for h in range(HEADS):
        qn = _rms(qf[:, 256 * h:256 * h + 128], qn_w) * scale
        qr = _rope128(_rms(qf[:, 256 * h + 128:256 * h + 256], qr_w, n=ROPE), cos_p, sin_p) * scale
        qs += [qn, qr]
        ks += [_rms(kvf[:, 128 * h:128 * (h + 1)], kn_w), kr]
    return jnp.concatenate(qs, axis=1), jnp.concatenate(ks, axis=1), kvf[:, 512:]


def _conv_fwd(proj, conv_w, tm, name):
    S = proj.shape[0]
    C = 1536
    nb = tm // 8

    def body(x_ref, prev_ref, w_ref, o_ref, ext_ref):
        i = pl.program_id(0)
        ext_ref[0:8, :] = jnp.where(i > 0, prev_ref[...], 0.0)
        ext_ref[8:, :] = x_ref[...]
        acc = jnp.zeros((tm, C), F32)
        for k in range(4):
            acc = acc + w_ref[k:k + 1, :] * ext_ref[pl.ds(5 + k, tm), :]
        o_ref[...] = acc

    return pl.pallas_call(
        body, name=name, grid=(S // tm,),
        in_specs=[pl.BlockSpec((tm, C), lambda i: (i, 0)),
                  pl.BlockSpec((8, C), lambda i: (jnp.maximum(i * nb - 1, 0), 0)),
                  pl.BlockSpec((4, C), lambda i: (0, 0))],
        out_specs=pl.BlockSpec((tm, C), lambda i: (i, 0)),
        out_shape=jax.ShapeDtypeStruct((S, C), F32),
        scratch_shapes=[pltpu.VMEM((tm + 8, C), F32)],
        compiler_params=_params(("arbitrary",)),
    )(proj, proj, conv_w)


def _conv_bwd(proj, dout, conv_w, tm, name):
    S = proj.shape[0]
    C = 1536
    nb = tm // 8
    n_steps = S // tm

    def body(x_ref, prev_ref, d_ref, next_ref, w_ref, dx_ref, dw_ref, xext_ref, dext_ref):
        i = pl.program_id(0)
        xext_ref[0:8, :] = jnp.where(i > 0, prev_ref[...], 0.0)
        xext_ref[8:, :] = x_ref[...]
        dext_ref[0:tm, :] = d_ref[...]
        dext_ref[tm:, :] = jnp.where(i < n_steps - 1, next_ref[...], 0.0)
        d = d_ref[...]
        acc = jnp.zeros((tm, C), F32)
        dws = []
        for k in range(4):
            acc = acc + w_ref[k:k + 1, :] * dext_ref[pl.ds(3 - k, tm), :]
            dws.append(jnp.sum(d * xext_ref[pl.ds(5 + k, tm), :], axis=0, keepdims=True))
        dx_ref[...] = acc.astype(dx_ref.dtype)

        @pl.when(i == 0)
        def _():
            dw_ref[...] = jnp.zeros_like(dw_ref)

        dw_ref[...] += jnp.concatenate(dws + [jnp.zeros((4, C), F32)], axis=0)

    return pl.pallas_call(
        body, name=name, grid=(n_steps,),
        in_specs=[pl.BlockSpec((tm, C), lambda i: (i, 0)),
                  pl.BlockSpec((8, C), lambda i: (jnp.maximum(i * nb - 1, 0), 0)),
                  pl.BlockSpec((tm, C), lambda i: (i, 0)),
                  pl.BlockSpec((8, C), lambda i: (jnp.minimum((i + 1) * nb, S // 8 - 1), 0)),
                  pl.BlockSpec((4, C), lambda i: (0, 0))],
        out_specs=[pl.BlockSpec((tm, C), lambda i: (i, 0)), pl.BlockSpec((8, C), lambda i: (0, 0))],
        out_shape=[jax.ShapeDtypeStruct((S, C), BF16), jax.ShapeDtypeStruct((8, C), F32)],
        scratch_shapes=[pltpu.VMEM((tm + 8, C), F32), pltpu.VMEM((tm + 8, C), F32)],
        compiler_params=_params(("arbitrary",)),
    )(proj, proj, dout, dout, conv_w)


SCAN_CHUNKS = (4, 2, 1)


def _gdn_scan_fwd(u, wk, qd, kd, qks, gl, name):
    S = u.shape[0]
    nc = S // CHUNK
    cs = _pick(nc, SCAN_CHUNKS)
    W = HEADS * HEAD_DIM

    def body(u_ref, wk_ref, qd_ref, kd_ref, qk0, qk1, qk2, qk3, gl_ref, o_ref, sp_ref, s_ref):
        @pl.when(pl.program_id(0) == 0)
        def _():
            s_ref[...] = jnp.zeros_like(s_ref)

        state = s_ref[...]
        for c in range(cs):
            rows, gl_rows = slice(CHUNK * c, CHUNK * (c + 1)), slice(8 * c, 8 * (c + 1))
            sp_ref[c] = state
            o, state = _scan_step(state, _heads(u_ref, HEAD_DIM, rows), _heads(wk_ref, HEAD_DIM, rows),
                                  _heads(qd_ref, HEAD_DIM, rows), _heads(kd_ref, HEAD_DIM, rows),
                                  jnp.stack([r[rows, :] for r in (qk0, qk1, qk2, qk3)]), _heads(gl_ref, HEAD_DIM, gl_rows))
            for h in range(HEADS):
                o_ref[rows, HEAD_DIM * h:HEAD_DIM * (h + 1)] = o[h]
        s_ref[...] = state

    row = pl.BlockSpec((cs * CHUNK, W), lambda n: (n, 0))
    qk_spec = pl.BlockSpec((cs * CHUNK, CHUNK), lambda n: (n, 0))
    return pl.pallas_call(
        body, name=name, grid=(nc // cs,),
        in_specs=[row, row, row, row, qk_spec, qk_spec, qk_spec, qk_spec, pl.BlockSpec((cs * 8, W), lambda n: (n, 0))],
        out_specs=[row, pl.BlockSpec((cs, HEADS, HEAD_DIM, HEAD_DIM), lambda n: (n, 0, 0, 0))],
        out_shape=[jax.ShapeDtypeStruct((S, W), F32), jax.ShapeDtypeStruct((nc, HEADS, HEAD_DIM, HEAD_DIM), F32)],
        scratch_shapes=[pltpu.VMEM((HEADS, HEAD_DIM, HEAD_DIM), F32)],
        compiler_params=_params(("arbitrary",)),
    )(u, wk, qd, kd, *qks, gl)


def _gdn_scan_bwd(u, wk, qd, kd, qks, gl, s_prev, d_o, name):
    S = u.shape[0]
    nc = S // CHUNK
    cs = _pick(nc, SCAN_CHUNKS)
    nb = nc // cs
    W = HEADS * HEAD_DIM

    def body(u_ref, wk_ref, qd_ref, kd_ref, qk0, qk1, qk2, qk3, gl_ref, sp_ref, do_ref,
             du_ref, dwk_ref, dqd_ref, dkd_ref, dqk0, dqk1, dqk2, dqk3, dgl_ref, ds_ref):
        @pl.when(pl.program_id(0) == 0)
        def _():
            ds_ref[...] = jnp.zeros_like(ds_ref)

        d_state = ds_ref[...]
        for c in reversed(range(cs)):
            rows, gl_rows = slice(CHUNK * c, CHUNK * (c + 1)), slice(8 * c, 8 * (c + 1))
            _, vjp = jax.vjp(_scan_step, sp_ref[c], _heads(u_ref, HEAD_DIM, rows), _heads(wk_ref, HEAD_DIM, rows),
                             _heads(qd_ref, HEAD_DIM, rows), _heads(kd_ref, HEAD_DIM, rows),
                             jnp.stack([r[rows, :] for r in (qk0, qk1, qk2, qk3)]), _heads(gl_ref, HEAD_DIM, gl_rows))
            d_state, du, dwk, dqd, dkd, dqk, dgl = vjp((_heads(do_ref, HEAD_DIM, rows), d_state))
            for h, dqk_ref in enumerate((dqk0, dqk1, dqk2, dqk3)):
                sl = slice(HEAD_DIM * h, HEAD_DIM * (h + 1))
                du_ref[rows, sl] = du[h]
                dwk_ref[rows, sl] = dwk[h]
                dqd_ref[rows, sl] = dqd[h]
                dkd_ref[rows, sl] = dkd[h]
                dqk_ref[rows, :] = dqk[h]
                dgl_ref[gl_rows, sl] = dgl[h]
        ds_ref[...] = d_state

    rev = lambda n: (nb - 1 - n, 0)
    row = pl.BlockSpec((cs * CHUNK, W), rev)
    qk_spec = pl.BlockSpec((cs * CHUNK, CHUNK), rev)
    gl_spec = pl.BlockSpec((cs * 8, W), rev)
    qk_shape = jax.ShapeDtypeStruct((S, CHUNK), F32)
    row_shape = jax.ShapeDtypeStruct((S, W), F32)
    return pl.pallas_call(
        body, name=name, grid=(nb,),
        in_specs=[row, row, row, row, qk_spec, qk_spec, qk_spec, qk_spec, gl_spec,
                  pl.BlockSpec((cs, HEADS, HEAD_DIM, HEAD_DIM), lambda n: (nb - 1 - n, 0, 0, 0)), row],
        out_specs=[row, row, row, row, qk_spec, qk_spec, qk_spec, qk_spec, gl_spec],
        out_shape=[row_shape] * 4 + [qk_shape] * 4 + [jax.ShapeDtypeStruct((nc * 8, W), F32)],
        scratch_shapes=[pltpu.VMEM((HEADS, HEAD_DIM, HEAD_DIM), F32)],
        compiler_params=_params(("arbitrary",)),
    )(u, wk, qd, kd, *qks, gl, s_prev, d_o)


NEG = -1e30


def _chunk_mask(i, j, t, transposed=False):
    q_axis, k_axis = (1, 0) if transposed else (0, 1)
    r = (i * t + lax.broadcasted_iota(jnp.int32, (t, t), q_axis)) // CHUNK
    c = (j * t + lax.broadcasted_iota(jnp.int32, (t, t), k_axis)) // CHUNK
    return c <= r


def _tile_pairs(n, by_key):
    pairs = [(i, j) for j in range(n) for i in range(j, n)] if by_key else [(i, j) for i in range(n) for j in range(i + 1)]
    return jnp.asarray(np.array([p[0] for p in pairs], np.int32)), jnp.asarray(np.array([p[1] for p in pairs], np.int32))


def _heads(ref, width, rows=slice(None)):
    return jnp.stack([ref[rows, width * h:width * (h + 1)] for h in range(HEADS)])


def _bmm(a, b, dims):
    return lax.dot_general(a.astype(BF16), b.astype(BF16), dims, preferred_element_type=F32)


def _attn_fwd(q, k, v_t, t, name):
    S = q.shape[0]
    n = S // t
    qi, kj = _tile_pairs(n, by_key=False)

    def body(qi_ref, kj_ref, q_ref, k_ref, vt_ref, o_ref, lse_ref, m_ref, l_ref, acc_ref):
        i, j = qi_ref[pl.program_id(0)], kj_ref[pl.program_id(0)]

        @pl.when(j == 0)
        def _():
            m_ref[...] = jnp.full_like(m_ref, NEG)
            l_ref[...] = jnp.zeros_like(l_ref)
            acc_ref[...] = jnp.zeros_like(acc_ref)

        def update(masked):
            s_t = _bmm(_heads(k_ref, 256), _heads(q_ref, 256), _BNT)
            if masked:
                s_t = jnp.where(_chunk_mask(i, j, t, transposed=True)[None], s_t, NEG)
            m_old = m_ref[...]
            m_new = jnp.maximum(m_old, jnp.max(s_t, axis=1, keepdims=True))
            p_t = jnp.exp2(s_t - m_new)
            alpha = jnp.exp2(m_old - m_new)
            l_ref[...] = alpha * l_ref[...] + jnp.sum(p_t, axis=1, keepdims=True)
            v_heads = jnp.stack([vt_ref[HEAD_DIM * h:HEAD_DIM * (h + 1), :] for h in range(HEADS)])
            acc_ref[...] = alpha * acc_ref[...] + _bmm(v_heads, p_t, _BNN)
            m_ref[...] = m_new

        @pl.when(j < i)
        def _():
            update(False)

        @pl.when(j == i)
        def _():
            update(True)
            for h in range(HEADS):
                sl = slice(HEAD_DIM * h, HEAD_DIM * (h + 1))
                o_ref[:, sl] = jnp.transpose(acc_ref[h] / l_ref[h])
                lse_ref[:, sl] = jnp.transpose(jnp.broadcast_to(m_ref[h] + jnp.log(l_ref[h]) * LOG2_E, (HEAD_DIM, t)))

    row = lambda p, qi_, kj_: (qi_[p], 0)
    return pl.pallas_call(
        body, name=name,
        grid_spec=pltpu.PrefetchScalarGridSpec(
            num_scalar_prefetch=2, grid=(qi.shape[0],),
            in_specs=[pl.BlockSpec((t, HEADS * 256), row), pl.BlockSpec((t, HEADS * 256), lambda p, qi_, kj_: (kj_[p], 0)),
                      pl.BlockSpec((HEADS * HEAD_DIM, t), lambda p, qi_, kj_: (0, kj_[p]))],
            out_specs=[pl.BlockSpec((t, HEADS * HEAD_DIM), row)] * 2,
            scratch_shapes=[pltpu.VMEM((HEADS, 1, t), F32), pltpu.VMEM((HEADS, 1, t), F32),
                            pltpu.VMEM((HEADS, HEAD_DIM, t), F32)]),
        out_shape=[jax.ShapeDtypeStruct((S, HEADS * HEAD_DIM), F32)] * 2,
        compiler_params=_params(("arbitrary",)),
    )(qi, kj, q, k, v_t)


def _attn_stats(o, lse, d_o, t, name):
    S = o.shape[0]

    def body(o_ref, lse_ref, do_ref, st_ref):
        lane = lax.broadcasted_iota(jnp.int32, (t, HEAD_DIM), 1)
        stats = jnp.zeros((t, HEAD_DIM), F32)
        for h in range(HEADS):
            sl = slice(HEAD_DIM * h, HEAD_DIM * (h + 1))
            delta = jnp.sum(do_ref[:, sl] * o_ref[:, sl], axis=1, keepdims=True)
            stats = stats + jnp.where(lane == h, lse_ref[:, sl], 0.0) + jnp.where(lane == HEADS + h, delta, 0.0)
        st_ref[...] = jnp.transpose(stats)[0:8, :]

    row = pl.BlockSpec((t, HEADS * HEAD_DIM), lambda i: (i, 0))
    return pl.pallas_call(
        body, name=name, grid=(S // t,),
        in_specs=[row, row, row], out_specs=pl.BlockSpec((8, t), lambda i: (0, i)),
        out_shape=jax.ShapeDtypeStruct((8, S), F32),
        compiler_params=_params(("parallel",)),
    )(o, lse, d_o)


BWD_GROUP = 2


def _attn_bwd(q, k, v, d_o, stats, t, name):
    S = q.shape[0]
    n = S // t
    groups = HEADS // BWD_GROUP
    gq, gv = BWD_GROUP * 256, BWD_GROUP * HEAD_DIM
    qi, kj = _tile_pairs(n, by_key=True)
    n_pairs = qi.shape[0]
    st = stats.reshape(2, groups, BWD_GROUP, S).transpose(1, 0, 2, 3).reshape(groups, 2 * BWD_GROUP, S)
    st = jnp.pad(st, ((0, 0), (0, 8 - 2 * BWD_GROUP), (0, 0)))

    def heads(ref, width, rows=slice(None)):
        return jnp.stack([ref[rows, width * h:width * (h + 1)] for h in range(BWD_GROUP)])

    def body(qi_ref, kj_ref, q_ref, k_ref, v_ref, do_ref, st_ref, dq_hbm, dk_ref, dv_ref, dq_acc, sem):
        g, p = pl.program_id(0), pl.program_id(1)
        i, j = qi_ref[p], kj_ref[p]

        @pl.when(i == j)
        def _():
            dk_ref[...] = jnp.zeros_like(dk_ref)
            dv_ref[...] = jnp.zeros_like(dv_ref)

        def update(masked):
            qh, kh = heads(q_ref, 256), heads(k_ref, 256)
            d_out = heads(do_ref, HEAD_DIM)
            stv = st_ref[...]
            lse_row = jnp.stack([stv[h:h + 1, :] for h in range(BWD_GROUP)])
            delta_row = jnp.stack([stv[BWD_GROUP + h:BWD_GROUP + h + 1, :] for h in range(BWD_GROUP)])
            s_t = _bmm(kh, qh, _BNT)
            p_t = jnp.exp2(s_t - lse_row)
            if masked:
                p_t = jnp.where(_chunk_mask(i, j, t, transposed=True)[None], p_t, 0.0)
            dv = _bmm(p_t, d_out, _BNN)
            dp_t = _bmm(heads(v_ref, HEAD_DIM), d_out, _BNT)
            ds_t = p_t * (dp_t - delta_row)
            dk = _bmm(ds_t, qh, _BNN)
            dq = _bmm(jnp.swapaxes(ds_t, 1, 2), kh, _BNN)
            rows = pl.ds(pl.multiple_of(i * t, t), t)
            for h in range(BWD_GROUP):
                dk_ref[:, 256 * h:256 * (h + 1)] += dk[h]
                dv_ref[:, HEAD_DIM * h:HEAD_DIM * (h + 1)] += dv[h]

            @pl.when(j == 0)
            def _():
                for h in range(BWD_GROUP):
                    dq_acc[rows, 256 * h:256 * (h + 1)] = dq[h]

            @pl.when(j > 0)
            def _():
                for h in range(BWD_GROUP):
                    dq_acc[rows, 256 * h:256 * (h + 1)] += dq[h]

        @pl.when(i == j)
        def _():
            update(True)

        @pl.when(i > j)
        def _():
            update(False)

        @pl.when(i == n - 1)
        def _():
            dk_ref[...] *= 1.0 / LOG2_E

        @pl.when(p == n_pairs - 1)
        def _():
            dq_acc[...] *= 1.0 / LOG2_E
            for gg in range(groups):
                @pl.when(g == gg)
                def _():
                    cp = pltpu.make_async_copy(dq_acc, dq_hbm.at[:, gq * gg:gq * (gg + 1)], sem)
                    cp.start()
                    cp.wait()

    q_blk = lambda g, p, qi_, kj_: (qi_[p], g)
    k_blk = lambda g, p, qi_, kj_: (kj_[p], g)
    return pl.pallas_call(
        body, name=name,
        grid_spec=pltpu.PrefetchScalarGridSpec(
            num_scalar_prefetch=2, grid=(groups, n_pairs),
            in_specs=[pl.BlockSpec((t, gq), q_blk), pl.BlockSpec((t, gq), k_blk), pl.BlockSpec((t, gv), k_blk),
                      pl.BlockSpec((t, gv), q_blk), pl.BlockSpec((None, 8, t), lambda g, p, qi_, kj_: (g, 0, qi_[p]))],
            out_specs=[pl.BlockSpec(memory_space=pl.ANY), pl.BlockSpec((t, gq), k_blk), pl.BlockSpec((t, gv), k_blk)],
            scratch_shapes=[pltpu.VMEM((S, gq), F32), pltpu.SemaphoreType.DMA]),
        out_shape=[jax.ShapeDtypeStruct((S, HEADS * 256), F32), jax.ShapeDtypeStruct((S, HEADS * 256), F32),
                   jax.ShapeDtypeStruct((S, HEADS * HEAD_DIM), F32)],
        compiler_params=_params(("arbitrary", "arbitrary")),
    )(qi, kj, q, k, v, d_o, st)


FFN_PIECE = 2 * D_FF // N_DEV
HID_PIECES = D_FF // FFN_PIECE


def _after_specs(after):
    return [] if after is None else [pl.BlockSpec(memory_space=pl.ANY)]


def _after_args(after):
    return [] if after is None else [after]


def _ffn_gw8(h, d_gate, d_up, name, after=None):
    S = h.shape[0]
    tm = 512
    tk = _pick(S, (512, 256, 128))
    nk = S // tk

    def body(h_ref, dg_ref, du_ref, *rest):
        o_ref, acc_ref = rest[-2:]
        k = pl.program_id(1)

        @pl.when(k == 0)
        def _():
            acc_ref[...] = jnp.zeros_like(acc_ref)

        h_t = jnp.transpose(h_ref[...])
        for p in range(HID_PIECES):
            acc_ref[p] += _dot_raw(h_t, dg_ref[p], "nn")
            acc_ref[HID_PIECES + p] += _dot_raw(h_t, du_ref[p], "nn")

        @pl.when(k == nk - 1)
        def _():
            o_ref[...] = acc_ref[...].astype(o_ref.dtype)

    d_spec = pl.BlockSpec((HID_PIECES, tk, FFN_PIECE), lambda i, k: (0, k, 0))
    return pl.pallas_call(
        body, name=name, grid=(D_MODEL // tm, nk),
        in_specs=[pl.BlockSpec((tk, tm), lambda i, k: (k, i)), d_spec, d_spec] + _after_specs(after),
        out_specs=pl.BlockSpec((2 * HID_PIECES, tm, FFN_PIECE), lambda i, k: (0, i, 0)),
        out_shape=jax.ShapeDtypeStruct((2 * HID_PIECES, D_MODEL, FFN_PIECE), BF16),
        scratch_shapes=[pltpu.VMEM((2 * HID_PIECES, tm, FFN_PIECE), F32)],
        compiler_params=_params(("parallel", "arbitrary")),
    )(h, d_gate, d_up, *_after_args(after))


EPILOGUE_ROWS = 256


def _dmod_epilogue(acc_ref, x_ref, do_ref, sc_ref, sh_ref, dx_ref, dsc_ref, dsh_ref, below, below_in, below_out):
    rows_total = acc_ref.shape[0]
    step = min(EPILOGUE_ROWS, rows_total)
    dsc, dsh, dg = 0.0, 0.0, 0.0
    for r in range(rows_total // step):
        rows = slice(step * r, step * (r + 1))
        _, vjp = jax.vjp(_modulate, x_ref[rows, :], sc_ref[...], sh_ref[...])
        dx, dsc_r, dsh_r = vjp(acc_ref[rows, :])
        dx = dx + do_ref[rows, :]
        dx_ref[rows, :] = dx
        dsc, dsh = dsc + dsc_r, dsh + dsh_r
        if below is not None:
            coef = below[2]
            below_out[0][rows, :] = (coef * below_in[1][...] * dx).astype(below_out[0].dtype)
            dg = dg + jnp.sum(coef * below_in[0][rows, :] * dx, axis=0, keepdims=True)
    dsc_ref[...] += dsc
    dsh_ref[...] += dsh
    if below is not None:
        below_out[1][...] += dg


def _ffn_dh(d_gate, d_up, w8, x, d_out, scale, shift, name, after=None, below=None):
    S = d_gate.shape[1]
    tm = _pick(S, MATMUL_ROWS if below is None else MATMUL_ROWS[1:])
    n_below = 0 if below is None else 2

    def body(dg_ref, du_ref, wg_ref, wu_ref, x_ref, do_ref, sc_ref, sh_ref, *rest):
        below_in = rest[:n_below]
        outs = rest[len(rest) - 4 - n_below:]
        dx_ref, dsc_ref, dsh_ref = outs[:3]
        below_out, acc_ref = outs[3:3 + n_below], outs[-1]
        i, k = pl.program_id(0), pl.program_id(1)

        @pl.when(k == 0)
        def _():
            acc_ref[...] = jnp.zeros_like(acc_ref)

        acc_ref[...] += _dot_raw(dg_ref[...], wg_ref[...], "nt") + _dot_raw(du_ref[...], wu_ref[...], "nt")

        @pl.when((k == 0) & (i == 0))
        def _():
            for r in (dsc_ref, dsh_ref) + tuple(below_out[1:]):
                r[...] = jnp.zeros_like(r)

        @pl.when(k == HID_PIECES - 1)
        def _():
            _dmod_epilogue(acc_ref, x_ref, do_ref, sc_ref, sh_ref, dx_ref, dsc_ref, dsh_ref, below, below_in, below_out)

    d_spec = pl.BlockSpec((None, tm, FFN_PIECE), lambda i, k: (k, i, 0))
    row = pl.BlockSpec((tm, D_MODEL), lambda i, k: (i, 0))
    par = pl.BlockSpec((1, D_MODEL), lambda i, k: (0, 0))
    row_shape, par_shape = jax.ShapeDtypeStruct((S, D_MODEL), F32), jax.ShapeDtypeStruct((1, D_MODEL), F32)
    return pl.pallas_call(
        body, name=name, grid=(S // tm, HID_PIECES),
        in_specs=[d_spec, d_spec,
                  pl.BlockSpec((None, D_MODEL, FFN_PIECE), lambda i, k: (k, 0, 0)),
                  pl.BlockSpec((None, D_MODEL, FFN_PIECE), lambda i, k: (k + HID_PIECES, 0, 0)),
                  row, row, par, par] + [row, par][:n_below] + _after_specs(after),
        out_specs=[row, par, par] + [row, par][:n_below],
        out_shape=[row_shape, par_shape, par_shape] + [jax.ShapeDtypeStruct((S, D_MODEL), BF16), par_shape][:n_below],
        scratch_shapes=[pltpu.VMEM((tm, D_MODEL), F32)],
        compiler_params=_params(("arbitrary", "arbitrary")),
    )(d_gate, d_up, w8, w8, x, d_out, scale, shift, *(below[:2] if below is not None else ()), *_after_args(after))


def _swiglu_bwd(d_hid, hid_by_gate, hid_by_up):
    return d_hid * hid_by_gate, d_hid * hid_by_up


def _adamw_math(w_, g_, m_, v_):
    m_ = ADAM_B1 * m_ + (1.0 - ADAM_B1) * g_
    v_ = ADAM_B2 * v_ + (1.0 - ADAM_B2) * (g_ * g_)
    m_hat = m_ / (1.0 - ADAM_B1 ** ADAM_STEP)
    v_hat = v_ / (1.0 - ADAM_B2 ** ADAM_STEP)
    return -ADAM_LR * (m_hat / (jnp.sqrt(v_hat) + ADAM_EPS) + ADAM_WD * w_), m_, v_


def _adamw(w, g, m, v, name):
    R, C = w.shape
    tr = _pick(R, (256, 176, 128, 64, 32, 16, 8))

    def body(w_ref, g_ref, m_ref, v_ref, d_ref, nm_ref, nv_ref):
        d_ref[...], nm_ref[...], nv_ref[...] = _adamw_math(w_ref[...], g_ref[...], m_ref[...], v_ref[...])

    spec = pl.BlockSpec((tr, C), lambda i: (i, 0))
    return pl.pallas_call(
        body, name=name, grid=(R // tr,),
        in_specs=[spec] * 4, out_specs=[spec] * 3,
        out_shape=[jax.ShapeDtypeStruct((R, C), F32)] * 3,
        compiler_params=_params(("parallel",)),
    )(w, g, m, v)


def _sum_adamw(parts, w, m, v, name, transposed=False):
    _, R, C = parts.shape
    tr = _pick(R, (256, 176, 128, 64, 32, 16, 8))

    def body(p_ref, w_ref, m_ref, v_ref, g_ref, d_ref, nm_ref, nv_ref):
        g_ = p_ref[0].astype(F32)
        for d in range(1, N_DEV):
            g_ = g_ + p_ref[d].astype(F32)
        if transposed:
            g_ = jnp.transpose(g_)
        g_ref[...] = g_
        d_ref[...], nm_ref[...], nv_ref[...] = _adamw_math(w_ref[...], g_, m_ref[...], v_ref[...])

    spec = pl.BlockSpec((C, tr), lambda i: (0, i)) if transposed else pl.BlockSpec((tr, C), lambda i: (i, 0))
    return pl.pallas_call(
        body, name=name, grid=(R // tr,),
        in_specs=[pl.BlockSpec((N_DEV, tr, C), lambda i: (0, i, 0)), spec, spec, spec], out_specs=[spec] * 4,
        out_shape=[jax.ShapeDtypeStruct(w.shape, F32)] * 4,
        compiler_params=_params(("parallel",)),
    )(parts, w, m, v)


def _sum_devices(parts, name):
    _, R, C = parts.shape
    tr = _pick(R, (512, 256, 176, 128, 64, 32, 16, 8))

    def body(p_ref, o_ref):
        acc = p_ref[0].astype(F32)
        for d in range(1, N_DEV):
            acc = acc + p_ref[d].astype(F32)
        o_ref[...] = acc

    return pl.pallas_call(
        body, name=name, grid=(R // tr,),
        in_specs=[pl.BlockSpec((N_DEV, tr, C), lambda i: (0, i, 0))],
        out_specs=pl.BlockSpec((tr, C), lambda i: (i, 0)),
        out_shape=jax.ShapeDtypeStruct((R, C), F32),
        compiler_params=_params(("parallel",)),
    )(parts)


def _my_place():
    return lax.axis_index("x"), lax.axis_index("y"), lax.axis_index("c")


def _all_gather(blocks, name):
    n = len(blocks)

    def body(*refs):
        x_refs, out_refs = refs[:n], refs[n:2 * n]
        send_sems, recv_sems, local_sems = refs[2 * n:]
        x, y, c = _my_place()
        me, sibling = (x, y, c), (x, y, 1 - c)
        chips = [(1 - x, y), (x, 1 - y), (1 - x, 1 - y)]

        def copy(a, k, blk, to, own=False):
            slot = out_refs[a].at[4 * blk[0] + 2 * blk[1] + blk[2]]
            return pltpu.make_async_remote_copy(
                src_ref=x_refs[a] if own else slot, dst_ref=slot,
                send_sem=send_sems.at[7 * a + k], recv_sem=recv_sems.at[7 * a + k], device_id=to, device_id_type=MESH)

        mine = [pltpu.make_async_copy(x_refs[a], out_refs[a].at[4 * x + 2 * y + c], local_sems.at[a]) for a in range(n)]
        for cp in mine:
            cp.start()
        first = []
        for j, chip in enumerate(chips):
            first += [copy(a, 1 + j, me, (*chip, c), own=True) for a in range(n)]
        first += [copy(a, 0, me, sibling, own=True) for a in range(n)]
        for cp in first:
            cp.start()
        passed = []
        for j, chip in enumerate(chips):
            for a in range(n):
                copy(a, 1 + j, (*chip, c), me).wait_recv()
                passed.append(copy(a, 4 + j, (*chip, c), sibling))
                passed[-1].start()
        for a in range(n):
            copy(a, 0, sibling, me).wait_recv()
        for j, chip in enumerate(chips):
            for a in range(n):
                copy(a, 4 + j, (*chip, 1 - c), me).wait_recv()
        for cp in first + passed:
            cp.wait_send()
        for cp in mine:
            cp.wait()

    return pl.pallas_call(
        body, name=name,
        out_shape=[jax.ShapeDtypeStruct((N_DEV,) + b.shape, b.dtype) for b in blocks],
        in_specs=[pl.BlockSpec(memory_space=pl.ANY)] * n,
        out_specs=[pl.BlockSpec(memory_space=pl.ANY)] * n,
        scratch_shapes=[pltpu.SemaphoreType.DMA((7 * n,)), pltpu.SemaphoreType.DMA((7 * n,)), pltpu.SemaphoreType.DMA((n,))],
    )(*blocks)


def _all_to_all(pieces, name):
    n = len(pieces)

    def body(*refs):
        x_refs, out_refs = refs[:n], refs[n:2 * n]
        send_sems, recv_sems, local_sems = refs[2 * n:]
        x, y, c = _my_place()
        me = 4 * x + 2 * y + c
        mine = [pltpu.make_async_copy(x_refs[a].at[me], out_refs[a].at[me], local_sems.at[a]) for a in range(n)]
        for cp in mine:
            cp.start()
        copies = []
        for k in (2, 4, 6, 3, 5, 7, 1):
            px = 1 - x if k & 4 else x
            py = 1 - y if k & 2 else y
            pc = 1 - c if k & 1 else c
            peer = 4 * px + 2 * py + pc
            for a in range(n):
                copies.append(pltpu.make_async_remote_copy(
                    src_ref=x_refs[a].at[peer], dst_ref=out_refs[a].at[me],
                    send_sem=send_sems.at[7 * a + k - 1], recv_sem=recv_sems.at[7 * a + k - 1],
                    device_id=(px, py, pc), device_id_type=MESH))
        for cp in copies:
            cp.start()
        for cp in copies:
            cp.wait_recv()
        for cp in copies:
            cp.wait_send()
        for cp in mine:
            cp.wait()

    return pl.pallas_call(
        body, name=name,
        out_shape=[jax.ShapeDtypeStruct(p.shape, p.dtype) for p in pieces],
        in_specs=[pl.BlockSpec(memory_space=pl.ANY)] * n,
        out_specs=[pl.BlockSpec(memory_space=pl.ANY)] * n,
        scratch_shapes=[pltpu.SemaphoreType.DMA((7 * n,)), pltpu.SemaphoreType.DMA((7 * n,)), pltpu.SemaphoreType.DMA((n,))],
    )(*pieces)


def _peers():
    x, y, c = _my_place()
    out = []
    for k in (2, 4, 6, 3, 5, 7, 1):
        px = 1 - x if k & 4 else x
        py = 1 - y if k & 2 else y
        pc = 1 - c if k & 1 else c
        out.append((k, (px, py, pc), 4 * px + 2 * py + pc))
    return out


def _exchange_copies(x_refs, land_refs, send_sems, recv_sems, scatter):
    x, y, c = _my_place()
    me = 4 * x + 2 * y + c
    starts, arrivals = [], []
    for k, place, peer in _peers():
        for a, (x_ref, land_ref) in enumerate(zip(x_refs, land_refs)):
            sems = dict(send_sem=send_sems.at[7 * a + k - 1], recv_sem=recv_sems.at[7 * a + k - 1],
                        device_id=place, device_id_type=MESH)
            src = x_ref.at[peer] if scatter else x_ref
            starts.append(pltpu.make_async_remote_copy(src_ref=src, dst_ref=land_ref.at[me], **sems))
            arrivals.append(pltpu.make_async_remote_copy(src_ref=src, dst_ref=land_ref.at[peer], **sems))
    return starts, arrivals


def _exchange_start(arrays, scatter, name):
    n = len(arrays)
    hbm = pl.BlockSpec(memory_space=pltpu.HBM)
    sem = pl.BlockSpec(memory_space=pltpu.SEMAPHORE)
    lands = [lax.empty(a.shape if scatter else (N_DEV,) + a.shape, a.dtype) for a in arrays]

    def body(*refs):
        x_refs, land_refs = refs[:n], refs[n:2 * n]
        send_sems, recv_sems = refs[2 * n], refs[2 * n + 1]
        token = refs[-1]
        starts, _ = _exchange_copies(x_refs, land_refs, send_sems, recv_sems, scatter)
        for cp in starts:
            cp.start()
        token[...] = jnp.zeros_like(token)

    res = pl.pallas_call(
        body, name=name,
        out_shape=(pltpu.SemaphoreType.DMA((7 * n,)), pltpu.SemaphoreType.DMA((7 * n,)),
                   *[pltpu.HBM(a.shape, a.dtype) for a in arrays], *[pltpu.HBM(l.shape, l.dtype) for l in lands],
                   jax.ShapeDtypeStruct((8, 128), F32)),
        in_specs=[hbm] * (2 * n),
        out_specs=(sem, sem, *[hbm] * (2 * n), pl.BlockSpec(memory_space=pltpu.VMEM)),
        input_output_aliases={i: 2 + i for i in range(2 * n)},
        compiler_params=pltpu.CompilerParams(has_side_effects=pltpu.SideEffectType.DATAFLOW_SIDE_EFFECTING),
    )(*[pltpu.with_memory_space_constraint(a, pltpu.HBM) for a in arrays],
      *[pltpu.with_memory_space_constraint(l, pltpu.HBM) for l in lands])
    return res[0], res[1], list(res[2:2 + n]), list(res[2 + n:2 + 2 * n]), res[-1]


def _exchange_wait(handles, scatter, after, name):
    send_sems, recv_sems, arrays, lands, _ = handles
    n = len(arrays)
    hbm = pl.BlockSpec(memory_space=pltpu.HBM)
    sem = pl.BlockSpec(memory_space=pltpu.SEMAPHORE)

    def body(*refs):
        x_refs, land_refs = refs[:n], refs[n:2 * n]
        send_s, recv_s = refs[2 * n], refs[2 * n + 1]
        starts, arrivals = _exchange_copies(x_refs, land_refs, send_s, recv_s, scatter)
        for cp in arrivals:
            cp.wait_recv()
        for cp in starts:
            cp.wait_send()

    res = pl.pallas_call(
        body, name=name,
        out_shape=(*[pltpu.HBM(a.shape, a.dtype) for a in arrays], *[pltpu.HBM(l.shape, l.dtype) for l in lands]),
        in_specs=[hbm] * (2 * n) + [sem, sem, pl.BlockSpec(memory_space=pl.ANY)],
        out_specs=tuple([hbm] * (2 * n)),
        input_output_aliases={i: i for i in range(2 * n)},
        compiler_params=pltpu.CompilerParams(has_side_effects=pltpu.SideEffectType.DATAFLOW_SIDE_EFFECTING),
    )(*arrays, *lands, send_sems, recv_sems, after)
    me = 4 * lax.axis_index("x") + 2 * lax.axis_index("y") + lax.axis_index("c")
    out = []
    for src, got in zip(res[:n], res[n:]):
        zeros = (0,) * (got.ndim - 1)
        own = lax.dynamic_slice(src, (me,) + zeros, (1,) + src.shape[1:]) if scatter else src[None]
        out.append(lax.dynamic_update_slice(got, own, (me,) + zeros))
    return out


def _pad_lanes(v, at=0, width=128):
    return jnp.pad(v, ((0, 0), (at, width - at - v.shape[1])))


def _pack_weights(P):
    W = {}
    w = P["w_in"]
    W["wp"] = jnp.concatenate([w[:, :2048], w[:, 2440:2696], w[:, 2056:2440], w[:, 2696:2760], w[:, 2048:2056],
                               jnp.zeros((D_MODEL, N_IN_PACKED - N_IN), w.dtype)], axis=1).astype(BF16)
    W["conv_w"] = P["gdn_conv_w"].astype(F32)
    W["alog_p"] = _pad_lanes(P["gdn_a_log"], 64)
    W["dt_p"] = _pad_lanes(P["gdn_dt_bias"], 64)
    W["gnw"] = P["gdn_norm_w"]
    W["qnw"] = P["mla_q_norm_w"]
    W["kvnw"] = P["mla_kv_norm_w"]
    uq = P["mla_w_uq"].reshape(Q_LORA, HEADS, HEAD_DIM + ROPE)
    W["wuq"] = jnp.pad(uq, ((0, 0), (0, 0), (0, 256 - HEAD_DIM - ROPE))).reshape(Q_LORA, HEADS * 256).astype(BF16)
    ukv = P["mla_w_ukv"].reshape(KV_LORA, HEADS, 2, HEAD_DIM)
    W["wukv"] = ukv.transpose(0, 2, 1, 3).reshape(KV_LORA, 2 * HEADS * HEAD_DIM).astype(BF16)
    W["qn_w"] = P["qkn_q_nope"]
    W["qr_w"] = _pad_lanes(P["qkn_q_rope"])
    W["kn_w"] = P["qkn_k_nope"]
    W["kr_w"] = _pad_lanes(P["qkn_k_rope"])
    W["onw"] = P["mla_out_norm_w"]
    W["wout"] = P["w_out"].astype(BF16)
    return W


def _unpack_grads(G):
    g = G["wp"]
    uq = G["wuq"].reshape(Q_LORA, HEADS, 256)[:, :, :HEAD_DIM + ROPE].reshape(Q_LORA, HEADS * (HEAD_DIM + ROPE))
    ukv = G["wukv"].reshape(KV_LORA, 2, HEADS, HEAD_DIM).transpose(0, 2, 1, 3).reshape(KV_LORA, 2 * HEADS * HEAD_DIM)
    return {
        "w_in": jnp.concatenate([g[:, :2048], g[:, 2752:2760], g[:, 2304:2688], g[:, 2048:2304], g[:, 2688:2752]], axis=1),
        "gdn_conv_w": G["conv_w"], "gdn_a_log": G["alog_p"][:, 64:68], "gdn_dt_bias": G["dt_p"][:, 64:68],
        "gdn_norm_w": G["gnw"], "mla_q_norm_w": G["qnw"], "mla_w_uq": uq, "mla_kv_norm_w": G["kvnw"], "mla_w_ukv": ukv,
        "qkn_q_nope": G["qn_w"], "qkn_q_rope": G["qr_w"][:, :ROPE], "qkn_k_nope": G["kn_w"], "qkn_k_rope": G["kr_w"][:, :ROPE],
        "mla_out_norm_w": G["onw"], "w_out": G["wout"],
    }


def _rope_tables(positions):
    half = ROPE // 2
    inv_freq = ROPE_BASE ** (-jnp.arange(half, dtype=F32) / half)
    ang = positions.astype(F32)[:, None] * inv_freq
    cos, sin = jnp.cos(ang), jnp.sin(ang)
    zeros = jnp.zeros((positions.shape[0], 128 - ROPE), F32)
    return jnp.concatenate([cos, cos, zeros], axis=1), jnp.concatenate([-sin, sin, zeros], axis=1)


def _mod_fn(x, scale, shift):
    return (_modulate(x, scale, shift),)


def _ffn_forward(x, scale, shift, gate_w, w8, wo4, name, target=None):
    S = x.shape[0]
    tm = _pick(S, (512, 256, 128))
    n = S // tm
    with_loss = target is not None

    def body(x_ref, sc_ref, sh_ref, g_ref, wg_ref, wu_ref, wo_ref, *rest):
        t_ref = rest[0] if with_loss else None
        h_ref, bg_ref, bu_ref, ht_ref = rest[with_loss:with_loss + 4]
        tail = rest[with_loss + 4:]
        h_scr, acc_ref = tail[-2:]
        i, p = pl.program_id(0), pl.program_id(1)

        @pl.when(p == 0)
        def _():
            h_new = _modulate(x_ref[...], sc_ref[...], sh_ref[...]).astype(BF16)
            h_scr[...] = h_new
            h_ref[...] = h_new
            acc_ref[...] = jnp.zeros_like(acc_ref)

        h = h_scr[...]
        gate = _dot_raw(h, wg_ref[...], "nn")
        up = _dot_raw(h, wu_ref[...], "nn")
        sg = _sigmoid(gate)
        act = gate * sg
        hid = act * up
        bg_ref[...] = (up * (sg * (1.0 + gate * (1.0 - sg)))).astype(BF16)
        bu_ref[...] = act.astype(BF16)
        ht_ref[...] = jnp.transpose(hid).astype(BF16)
        acc_ref[...] += _dot_raw(hid, wo_ref[...], "nn")

        if with_loss:
            dx_ref, df_ref, dg_ref, l_ref = tail[:4]

            @pl.when((p == 0) & (i == 0))
            def _():
                dg_ref[...] = jnp.zeros_like(dg_ref)
                l_ref[...] = jnp.zeros_like(l_ref)

            @pl.when(p == HID_PIECES - 1)
            def _():
                step = min(EPILOGUE_ROWS, tm)
                for r in range(tm // step):
                    rows = slice(step * r, step * (r + 1))
                    f = acc_ref[rows, :]
                    diff = x_ref[rows, :] + 0.5 * g_ref[...] * f - t_ref[rows, :]
                    dx = diff * (1.0 / D_MODEL)
                    dx_ref[rows, :] = dx
                    df_ref[rows, :] = (0.5 * g_ref[...] * dx).astype(df_ref.dtype)
                    dg_ref[...] += jnp.sum(0.5 * f * dx, axis=0, keepdims=True)
                    l_ref[...] += jnp.sum(diff * diff, axis=0, keepdims=True)

            @pl.when((p == HID_PIECES - 1) & (i == n - 1))
            def _():
                l_ref[...] = jnp.full(l_ref.shape, (0.5 / D_MODEL) * jnp.sum(l_ref[...]), F32)
        else:
            f_ref, xo_ref = tail[:2]

            @pl.when(p == HID_PIECES - 1)
            def _():
                f = acc_ref[...]
                f_ref[...] = f
                xo_ref[...] = x_ref[...] + 0.5 * g_ref[...] * f

    row = pl.BlockSpec((tm, D_MODEL), lambda i, p: (i, 0))
    par = pl.BlockSpec((1, D_MODEL), lambda i, p: (0, 0))
    piece = pl.BlockSpec((None, tm, FFN_PIECE), lambda i, p: (p, i, 0))
    row_f32, row_bf16 = jax.ShapeDtypeStruct((S, D_MODEL), F32), jax.ShapeDtypeStruct((S, D_MODEL), BF16)
    par_f32 = jax.ShapeDtypeStruct((1, D_MODEL), F32)
    piece_shape = jax.ShapeDtypeStruct((HID_PIECES, S, FFN_PIECE), BF16)
    return pl.pallas_call(
        body, name=name, grid=(n, HID_PIECES),
        in_specs=[row, par, par, par,
                  pl.BlockSpec((None, D_MODEL, FFN_PIECE), lambda i, p: (p, 0, 0)),
                  pl.BlockSpec((None, D_MODEL, FFN_PIECE), lambda i, p: (p + HID_PIECES, 0, 0)),
                  pl.BlockSpec((None, FFN_PIECE, D_MODEL), lambda i, p: (p, 0, 0))] + [row] * with_loss,
        out_specs=[row, piece, piece, pl.BlockSpec((None, FFN_PIECE, tm), lambda i, p: (p, 0, i))]
        + ([row, row, par, par] if with_loss else [row, row]),
        out_shape=[row_bf16, piece_shape, piece_shape, jax.ShapeDtypeStruct((HID_PIECES, FFN_PIECE, S), BF16)]
        + ([row_f32, row_bf16, par_f32, par_f32] if with_loss else [row_f32, row_f32]),
        scratch_shapes=[pltpu.VMEM((tm, D_MODEL), BF16), pltpu.VMEM((tm, D_MODEL), F32)],
        compiler_params=_params(("arbitrary", "arbitrary")),
    )(x, scale, shift, gate_w, w8, w8, wo4, *([target] if with_loss else []))


def _ffn_fwd(x, scale, shift, gate_w, w8, wo4, tag, target=None):
    res = _ffn_forward(x, scale, shift, gate_w, w8, wo4, tag + "_fwd", target)
    h, by_gate, by_up, hid_t = res[:4]
    if target is not None:
        dx_out, df, d_gate_w, loss_row = res[4:]
        return (dx_out, loss_row), (h, by_gate, by_up, hid_t, None, df, d_gate_w)
    f, x_out = res[4:]
    return x_out, (h, by_gate, by_up, hid_t, f, None, None)


def _ffn_bwd(d_out, x, scale, shift, gate_w, w8, wo4, saved, tag, grad_ready, below=None):
    h, gate, up, hid_t, f, df, d_gate_w = saved
    S = x.shape[0]
    tm = _pick(S, (512, 256, 128))
    tk = _pick(S, (512, 256, 128))
    n = S // tm
    if df is None:
        (df,), (d_gate_w,) = _rowwise_bwd(lambda f_, g_: (0.5 * g_ * f_,), [(f, tm, D_MODEL, 0)], [], [gate_w],
                                          [(d_out, tm, D_MODEL, 0)], n, tag + "_dres", row_dtypes=(BF16,))
    tb = _pick(S, MATMUL_ROWS)
    piece = pl.BlockSpec((None, tb, FFN_PIECE), lambda i, j, k: (j, i, 0))
    d_gate, d_up = _mmg(df, wo4, "nt", name=tag + "_ddown", grid=(S // tb, HID_PIECES, 1),
                        a_spec=pl.BlockSpec((tb, D_MODEL), lambda i, j, k: (i, 0)),
                        b_spec=pl.BlockSpec((None, FFN_PIECE, D_MODEL), lambda i, j, k: (j, 0, 0)),
                        out_spec=piece, out_shapes=[jax.ShapeDtypeStruct((HID_PIECES, S, FFN_PIECE), BF16)] * 2,
                        acc_shape=(tb, FFN_PIECE), extras=[gate, up], extra_specs=[piece, piece], epi=_swiglu_bwd)
    tk = _pick(S, MATMUL_ROWS)
    g_wo4 = _mmg(hid_t, df, "nn", name=tag + "_gwo", grid=(HID_PIECES, 1, S // tk),
                 a_spec=pl.BlockSpec((None, FFN_PIECE, tk), lambda i, j, k: (i, 0, k)),
                 b_spec=pl.BlockSpec((tk, D_MODEL), lambda i, j, k: (k, j)),
                 out_spec=pl.BlockSpec((None, FFN_PIECE, D_MODEL), lambda i, j, k: (i, 0, j)),
                 out_shapes=[jax.ShapeDtypeStruct((HID_PIECES, FFN_PIECE, D_MODEL), BF16)], acc_shape=(FFN_PIECE, D_MODEL))
    g_w8 = _ffn_gw8(h, d_gate, d_up, tag + "_gw8", after=grad_ready("wo4", g_wo4))
    res = _ffn_dh(d_gate, d_up, w8, x, d_out, scale, shift, tag + "_dh", after=grad_ready("w8", g_w8), below=below)
    return (res[0], res[1], res[2], d_gate_w) + tuple(res[3:])


def _dproj_dmod(d_proj, wp, x, d_out, scale, shift, name, below=None):
    S, K = d_proj.shape
    tm = _pick(S, MATMUL_ROWS if below is None else MATMUL_ROWS[1:])
    tk = _pick(K, (1408, 512, 256, 128))
    nk = K // tk
    n_below = 0 if below is None else 2

    def body(dp_ref, w_ref, x_ref, do_ref, sc_ref, sh_ref, *rest):
        below_in = rest[:n_below]
        dx_ref, dsc_ref, dsh_ref = rest[n_below:n_below + 3]
        below_out, acc_ref = rest[n_below + 3:n_below + 3 + n_below], rest[-1]
        i, k = pl.program_id(0), pl.program_id(1)

        @pl.when(k == 0)
        def _():
            acc_ref[...] = jnp.zeros_like(acc_ref)

        acc_ref[...] += _dot_raw(dp_ref[...], w_ref[...], "nt")

        @pl.when((k == 0) & (i == 0))
        def _():
            for r in (dsc_ref, dsh_ref) + tuple(below_out[1:]):
                r[...] = jnp.zeros_like(r)

        @pl.when(k == nk - 1)
        def _():
            _dmod_epilogue(acc_ref, x_ref, do_ref, sc_ref, sh_ref, dx_ref, dsc_ref, dsh_ref, below, below_in, below_out)

    row = pl.BlockSpec((tm, D_MODEL), lambda i, k: (i, 0))
    par = pl.BlockSpec((1, D_MODEL), lambda i, k: (0, 0))
    row_shape, par_shape = jax.ShapeDtypeStruct((S, D_MODEL), F32), jax.ShapeDtypeStruct((1, D_MODEL), F32)
    return pl.pallas_call(
        body, name=name, grid=(S // tm, nk),
        in_specs=[pl.BlockSpec((tm, tk), lambda i, k: (i, k)), pl.BlockSpec((D_MODEL, tk), lambda i, k: (0, k)),
                  row, row, par, par] + [row, par][:n_below],
        out_specs=[row, par, par] + [row, par][:n_below],
        out_shape=[row_shape, par_shape, par_shape] + [jax.ShapeDtypeStruct((S, D_MODEL), BF16), par_shape][:n_below],
        scratch_shapes=[pltpu.VMEM((tm, D_MODEL), F32)],
        compiler_params=_params(("arbitrary", "arbitrary")),
    )(d_proj, wp, x, d_out, scale, shift, *(below[:2] if below is not None else ()))


def _mixer_fwd(x1, scale, shift, gate_w, cos_p, sin_p, W):
    S = x1.shape[0]
    tm = _pick(S, (512, 256, 128))
    tv = _pick(S, (256, 128))
    ta = _pick(S, (512, 256, 128))
    nc = S // CHUNK
    (h2,) = _rowwise(_mod_fn, [(x1, tm, D_MODEL, 0)], [scale, shift], [(tm, D_MODEL, BF16)], S // tm, "mix_mod")
    proj = _mm(h2, W["wp"], "nn", name="mix_proj")
    qkvc = _conv_fwd(proj, W["conv_w"], tv, "gdn_conv")
    kab = (proj, tv, 128, 21)
    q_a, k_a, v_a, gb = _rowwise(_gdn_pre_fn, [(qkvc, tv, 1536, 0), kab], [W["alog_p"], W["dt_p"]],
                                 [(tv, 512, F32)] * 3 + [(tv, 128, F32)], S // tv, "gdn_pre")
    ti = _pick(S, INTRA_ROWS)
    intra = _rowwise(_gdn_intra_fn, [(q_a, ti, 512, 0), (k_a, ti, 512, 0), (v_a, ti, 512, 0), (gb, ti, 128, 0)],
                     [], [(ti, 512, F32)] * 4 + [(ti, CHUNK, F32)] * 4 + [(ti // 8, 512, F32)] + [(ti, CHUNK, F32)] * 4,
                     S // ti, "gdn_intra")
    u, wk, qd, kd, qks, gl, invs = intra[0], intra[1], intra[2], intra[3], tuple(intra[4:8]), intra[8], tuple(intra[9:])
    o_a, s_prev = _gdn_scan_fwd(u, wk, qd, kd, qks, gl, "gdn_scan")
    mla_params = [W["qnw"], W["kvnw"], W["wuq"], W["wukv"], W["qn_w"], W["qr_w"], W["kn_w"], W["kr_w"]]
    def mla_pre_with_vt(*a):
        q_, k_, v_ = _mla_pre_fn(*a)
        return q_, k_, v_, jnp.transpose(v_)

    q_b, k_b, v_b, vt_b = _rowwise(mla_pre_with_vt,
                                   [(proj, tv, 256, 8), (proj, tv, 384, 6), kab, (cos_p, tv, 128, 0), (sin_p, tv, 128, 0)],
                                   mla_params, [(tv, 1024, BF16), (tv, 1024, BF16), (tv, 512, BF16), (512, tv, BF16, "across")],
                                   S // tv, "mla_pre")
    o_b, lse = _attn_fwd(q_b, k_b, vt_b, ta, "mla_attn")
    (mixed,) = _rowwise(_mix_post_fn, [(o_a, tv, 512, 0), (proj, tv, 512, 3), (o_b, tv, 512, 0)], [W["gnw"], W["onw"]],
                        [(tv, D_MODEL, BF16)], S // tv, "mix_post")
    y, x2 = _mm(mixed, W["wout"], "nn", name="mix_out", out_dtypes=(F32, F32), extras=[x1], extra_params=[gate_w],
                epi=lambda acc, x_, g_: (acc, x_ + g_ * acc))
    saved = (h2, proj, qkvc, q_a, k_a, v_a, gb, u, wk, qd, kd, qks, gl, invs, s_prev, o_a, q_b, k_b, v_b, o_b, lse, mixed, y)
    return x2, saved


def _mixer_bwd(d_out, dy, x1, scale, shift, cos_p, sin_p, W, saved, below):
    (h2, proj, qkvc, q_a, k_a, v_a, gb, u, wk, qd, kd, qks, gl, invs, s_prev, o_a, q_b, k_b, v_b, o_b, lse, mixed, y) = saved
    S = x1.shape[0]
    tm = _pick(S, (512, 256, 128))
    tv = _pick(S, (256, 128))
    ta = _pick(S, (512, 256, 128))
    nc = S // CHUNK
    G = {}
    d_mixed = _mm(dy, W["wout"], "nt", name="mix_dout")
    G["wout"] = _mm(mixed, dy, "tn", name="mix_gwout")
    (do_a, dz, do_b), (G["gnw"], G["onw"]) = _rowwise_bwd(
        _mix_post_fn, [(o_a, tv, 512, 0), (proj, tv, 512, 3), (o_b, tv, 512, 0)], [], [W["gnw"], W["onw"]],
        [(d_mixed, tv, D_MODEL, 0)], S // tv, "mix_dpost", row_dtypes=(F32, BF16, F32))
    stats = _attn_stats(o_b, lse, do_b, ta, "mla_stats")
    dq_b, dk_b, dv_b = _attn_bwd(q_b, k_b, v_b, do_b, stats, ta, "mla_dattn")
    kab = (proj, tv, 128, 21)
    mla_params = [W["qnw"], W["kvnw"], W["wuq"], W["wukv"], W["qn_w"], W["qr_w"], W["kn_w"], W["kr_w"]]
    (d_ckv, d_cq, d_kab), mla_grads = _rowwise_bwd(
        _mla_pre_fn, [(proj, tv, 256, 8), (proj, tv, 384, 6), kab], [(cos_p, tv, 128, 0), (sin_p, tv, 128, 0)], mla_params,
        [(dq_b, tv, 1024, 0), (dk_b, tv, 1024, 0), (dv_b, tv, 512, 0)], S // tv, "mla_dpre", row_dtypes=(BF16, BF16, F32))
    for key, g in zip(("qnw", "kvnw", "wuq", "wukv", "qn_w", "qr_w", "kn_w", "kr_w"), mla_grads):
        G[key] = g
    scan_grads = _gdn_scan_bwd(u, wk, qd, kd, qks, gl, s_prev, do_a, "gdn_dscan")
    ti = _pick(S, INTRA_ROWS)
    intra_douts = [(scan_grads[i], ti, 512, 0) for i in range(4)] + [(scan_grads[4 + i], ti, CHUNK, 0) for i in range(4)]
    intra_douts.append((scan_grads[8], ti // 8, 512, 0))
    (dq_a, dk_a, dv_a, d_gb), _ = _rowwise_bwd(
        _gdn_intra_fn, [(q_a, ti, 512, 0), (k_a, ti, 512, 0), (v_a, ti, 512, 0), (gb, ti, 128, 0)],
        [(x_, ti, CHUNK, 0) for x_ in invs], [], intra_douts, S // ti, "gdn_dintra")
    (d_qkvc, d_kab), (G["alog_p"], G["dt_p"]) = _rowwise_bwd(
        _gdn_pre_fn, [(qkvc, tv, 1536, 0), kab], [], [W["alog_p"], W["dt_p"]],
        [(dq_a, tv, 512, 0), (dk_a, tv, 512, 0), (dv_a, tv, 512, 0), (d_gb, tv, 128, 0)], S // tv, "gdn_dpre",
        adds=[(1, d_kab)], row_dtypes=(F32, BF16))
    d_qkv, g_conv = _conv_bwd(proj, d_qkvc, W["conv_w"], tv, "gdn_dconv")
    G["conv_w"] = g_conv[:4]
    d_proj = jnp.concatenate([d_qkv, dz, d_ckv, d_cq, d_kab], axis=1)
    G["wp"] = _mm(h2, d_proj, "tn", name="mix_gwp")
    dx1, G["s2"], G["sh2"], d_below, dg_below = _dproj_dmod(d_proj, W["wp"], x1, d_out, scale, shift, "mix_dproj", below=below)
    return dx1, d_below, dg_below, G


def _local_step(x, target, mod, cos_p, sin_p, W1, mixer_weights, ffn2_weights, ffn_grad_ready, mixer_grads_ready):
    sh1, s1, g1, sh2, s2, g2, sh3, s3, g3 = [mod[:, D_MODEL * i:D_MODEL * (i + 1)] for i in range(N_MOD)]
    x1, saved1 = _ffn_fwd(x, s1, sh1, g1, W1["f1_w8"], W1["f1_wo4"], "ffn1")
    W = mixer_weights(x1)
    x2, saved2 = _mixer_fwd(x1, s2, sh2, g2, cos_p, sin_p, W)
    W.update(ffn2_weights(x2))
    (dx3, loss_row), saved3 = _ffn_fwd(x2, s3, sh3, g3, W["f2_w8"], W["f2_wo4"], "ffn2", target=target)
    dx2, d_s3, d_sh3, d_g3, dy, d_g2 = _ffn_bwd(dx3, x2, s3, sh3, g3, W["f2_w8"], W["f2_wo4"], saved3, "ffn2",
                                                ffn_grad_ready("f2"), below=(saved2[-1], g2, 1.0))
    dx1, df1, d_g1, G = _mixer_bwd(dx2, dy, x1, s2, sh2, cos_p, sin_p, W, saved2, below=(saved1[4], g1, 0.5))
    d_sh2, d_s2 = G.pop("sh2"), G.pop("s2")
    saved1 = saved1[:5] + (df1, d_g1 + mixer_grads_ready(G))
    dx, d_s1, d_sh1, d_g1 = _ffn_bwd(dx1, x, s1, sh1, g1, W1["f1_w8"], W1["f1_wo4"], saved1, "ffn1", ffn_grad_ready("f1"))
    d_mod = jnp.concatenate([d_sh1, d_s1, d_g1, d_sh2, d_s2, d_g2, d_sh3, d_s3, d_g3], axis=1)
    return loss_row, dx, d_mod


WEIGHT_NAMES = ("w_ada", "b_ada", "ffn1_w_in", "ffn1_w_out", "w_in", "gdn_conv_w", "gdn_a_log", "gdn_dt_bias", "gdn_norm_w",
                "mla_q_norm_w", "mla_w_uq", "mla_kv_norm_w", "mla_w_ukv", "qkn_q_nope", "qkn_q_rope", "qkn_k_nope",
                "qkn_k_rope", "mla_out_norm_w", "w_out", "ffn2_w_in", "ffn2_w_out")
FFN_SHARDED = ("ffn1_w_in", "ffn1_w_out", "ffn2_w_in", "ffn2_w_out")
TRANSPOSED_ENTRY = ("ffn1_w_in", "ffn2_w_in", "w_in", "mla_w_uq")
SHEETED = (("w_in", "col"), ("gdn_conv_w", "col"), ("mla_w_uq", "col"), ("mla_w_ukv", "col"), ("w_out", "row"))
MOD_ROWS = N_MOD * D_MODEL // 128
SMALL = {"gdn_a_log": (MOD_ROWS, 1, 64, 4), "gdn_dt_bias": (MOD_ROWS + 1, 1, 64, 4), "gdn_norm_w": (MOD_ROWS + 2, 1, 0, 128),
         "mla_q_norm_w": (MOD_ROWS + 3, 3, 0, 384), "mla_kv_norm_w": (MOD_ROWS + 6, 2, 0, 256),
         "qkn_q_nope": (MOD_ROWS + 8, 1, 0, 128), "qkn_q_rope": (MOD_ROWS + 9, 1, 0, 64), "qkn_k_nope": (MOD_ROWS + 10, 1, 0, 128),
         "qkn_k_rope": (MOD_ROWS + 11, 1, 0, 64), "mla_out_norm_w": (MOD_ROWS + 12, 1, 0, 128)}
LOSS_ROW = MOD_ROWS + 13
CONV_ROW, CONV_ROWS = 88, 4 * 1536 // 128
SHEET_ROWS = CONV_ROW + CONV_ROWS


def _to_sheet(flat, dtype, sublanes):
    n = flat.shape[-1]
    unit = sublanes * 128
    pad = (-n) % unit
    flat = jnp.pad(flat.astype(dtype), [(0, 0)] * (flat.ndim - 1) + [(0, pad)])
    return flat.reshape(flat.shape[:-1] + ((n + pad) // 128, 128))


def _small_sheet(b_like, small):
    sheet = jnp.zeros((SHEET_ROWS, 128), F32).at[:MOD_ROWS].set(b_like.reshape(MOD_ROWS, 128))
    for name, (row, rows, lane, n) in SMALL.items():
        v = small[name].reshape(1, n)
        if rows == 1:
            sheet = sheet.at[row, lane:lane + n].set(v[0])
        else:
            sheet = sheet.at[row:row + rows].set(v.reshape(rows, 128))
    return sheet


def _from_small_sheet(sheet):
    out = {"b_ada": sheet[:MOD_ROWS].reshape(1, N_MOD * D_MODEL)}
    for name, (row, rows, lane, n) in SMALL.items():
        out[name] = sheet[row, lane:lane + n].reshape(1, n) if rows == 1 else sheet[row:row + rows].reshape(1, n)
    return out


def kernel(x, c, positions, w_ada, b_ada, ffn1_w_in, ffn1_w_out, w_in, gdn_conv_w, gdn_a_log, gdn_dt_bias, gdn_norm_w, mla_q_norm_w, mla_w_uq, mla_kv_norm_w, mla_w_ukv, qkn_q_nope, qkn_q_rope, qkn_k_nope, qkn_k_rope, mla_out_norm_w, w_out, ffn2_w_in, ffn2_w_out, loss_target, m_w_ada, m_b_ada, m_ffn1_w_in, m_ffn1_w_out, m_w_in, m_gdn_conv_w, m_gdn_a_log, m_gdn_dt_bias, m_gdn_norm_w, m_mla_q_norm_w, m_mla_w_uq, m_mla_kv_norm_w, m_mla_w_ukv, m_qkn_q_nope, m_qkn_q_rope, m_qkn_k_nope, m_qkn_k_rope, m_mla_out_norm_w, m_w_out, m_ffn2_w_in, m_ffn2_w_out, v_w_ada, v_b_ada, v_ffn1_w_in, v_ffn1_w_out, v_w_in, v_gdn_conv_w, v_gdn_a_log, v_gdn_dt_bias, v_gdn_norm_w, v_mla_q_norm_w, v_mla_w_uq, v_mla_kv_norm_w, v_mla_w_ukv, v_qkn_q_nope, v_qkn_q_rope, v_qkn_k_nope, v_qkn_k_rope, v_mla_out_norm_w, v_w_out, v_ffn2_w_in, v_ffn2_w_out):
    args = locals()
    w = {n: args[n] for n in WEIGHT_NAMES}
    m = {n: args["m_" + n] for n in WEIGHT_NAMES}
    v = {n: args["v_" + n] for n in WEIGHT_NAMES}
    me = 4 * lax.axis_index("x") + 2 * lax.axis_index("y") + lax.axis_index("c")
    cols = N_MOD * D_MODEL // N_DEV
    shard = {n: w[n][0] for n in FFN_SHARDED + tuple(s[0] for s in SHEETED)}

    sc = c * _sigmoid(c)
    first = _to_sheet(jnp.concatenate([sc.reshape(-1), shard["gdn_conv_w"].reshape(-1)]), F32, 8)
    (first_all,) = _all_gather([first], "gather_c")
    sc_all = first_all[:, :D_MODEL // 128].reshape(N_DEV, D_MODEL)
    n_taps = shard["gdn_conv_w"].size
    conv_all = first_all.reshape(N_DEV, -1)[:, D_MODEL:D_MODEL + n_taps].reshape(N_DEV, 4, -1)
    b_mine = lax.dynamic_slice(b_ada, (0, me * cols), (1, cols))
    mod_cols = _mm(sc_all, w_ada[0], "nn", name="ada_mod", extra_params=[b_mine], epi=lambda acc, b_: (acc + b_,))
    (mod_all,) = _all_to_all([_to_sheet(mod_cols, F32, 8)], "scatter_mod")
    mod = mod_all.reshape(N_DEV, -1)[:, :cols].reshape(1, N_MOD * D_MODEL)

    f1_shards, mod = lax.optimization_barrier(([shard["ffn1_w_in"].astype(BF16), shard["ffn1_w_out"].astype(BF16)], mod))
    f1_w8, f1_out = _all_gather(f1_shards, "gather_w1")
    travel = [s for s in SHEETED if s[0] != "gdn_conv_w"]
    tied = lax.optimization_barrier(([shard[n].astype(BF16) for n, _ in travel], f1_w8))
    f1_w8 = tied[1]
    mixer_w = _exchange_start(tied[0], False, "gather_wm_start")
    ffn2_w = _exchange_start([shard["ffn2_w_in"].astype(BF16) + mixer_w[4][0:1, 0:1].astype(BF16),
                              shard["ffn2_w_out"].astype(BF16)], False, "gather_w2_start")
    mod = mod + ffn2_w[4][0:1, 0:1]
    W1 = dict(f1_w8=f1_w8, f1_wo4=f1_out.reshape(HID_PIECES, FFN_PIECE, D_MODEL))

    def mixer_weights(after):
        got = _exchange_wait(mixer_w, False, after, "gather_wm_wait")
        P = {n: jnp.concatenate(list(g), axis=1) if kind == "col" else g.reshape(-1, g.shape[-1])
             for (n, kind), g in zip(travel, got)}
        P["gdn_conv_w"] = jnp.concatenate(list(conv_all), axis=1)
        for n in SMALL:
            P[n] = w[n]
        return _pack_weights(P)

    def ffn2_weights(after):
        f2_w8, f2_out = _exchange_wait(ffn2_w, False, after, "gather_w2_wait")
        return dict(f2_w8=f2_w8, f2_wo4=f2_out.reshape(HID_PIECES, FFN_PIECE, D_MODEL))

    pending, small_grads = {}, {}

    def ffn_grad_ready(tag):
        def ready(which, g):
            pieces = g if which == "w8" else g.reshape((N_DEV,) + shard["ffn1_w_out"].shape)
            pending[tag + which] = _exchange_start([pieces], True, "scatter_%s_%s_start" % (tag, which))
            return pending[tag + which][4]
        return ready

    def mixer_grads_ready(G):
        g_full = _unpack_grads(G)
        small_grads.update({n: g_full[n] for n in SMALL})
        small_grads["gdn_conv_w"] = g_full["gdn_conv_w"]
        pieces = []
        for n, kind in travel:
            r, cc = shard[n].shape
            g = g_full[n].astype(BF16)
            pieces.append(jnp.stack([g[:, cc * p:cc * (p + 1)] for p in range(N_DEV)]) if kind == "col"
                          else g.reshape(N_DEV, r, cc))
        pending["mixer"] = _exchange_start(pieces, True, "scatter_mx_start")
        return pending["mixer"][4][0:1, 0:1]

    cos_p, sin_p = _rope_tables(positions[0])
    loss_row, dx, d_mod = _local_step(x[0], loss_target[0], mod, cos_p, sin_p, W1, mixer_weights, ffn2_weights,
                                      ffn_grad_ready, mixer_grads_ready)

    sheet = _small_sheet(d_mod, small_grads).at[LOSS_ROW].set(loss_row[0, :128])
    sheet = sheet.at[CONV_ROW:CONV_ROW + CONV_ROWS].set(small_grads["gdn_conv_w"].reshape(CONV_ROWS, 128))
    (sheets,) = _all_gather([sheet], "gather_small")
    summed = _sum_devices(sheets, "sum_small")
    d_mod_all = sheets[:, :MOD_ROWS].reshape(N_DEV, N_MOD * D_MODEL)
    d_mod_mine = lax.dynamic_slice(d_mod_all, (0, me * cols), (N_DEV, cols))
    grads = _from_small_sheet(summed)
    grads["w_ada"] = _mm(sc_all, d_mod_mine, "tn", name="ada_gw", hi=True)
    conv_taps = shard["gdn_conv_w"].shape[1]
    grads["gdn_conv_w"] = lax.dynamic_slice(summed[CONV_ROW:CONV_ROW + CONV_ROWS].reshape(4, -1), (0, me * conv_taps),
                                            (4, conv_taps))
    loss = summed[LOSS_ROW, 0]

    delta, new_m, new_v = {}, {}, {}
    arrived = {}
    for n, key in zip(FFN_SHARDED, ("f1w8", "f1wo4", "f2w8", "f2wo4")):
        (arrived[n],) = _exchange_wait(pending[key], True, summed, "scatter_%s_wait" % key)
    arrived.update(zip([n for n, _ in travel], _exchange_wait(pending["mixer"], True, summed, "scatter_mx_wait")))
    for n, parts in arrived.items():
        if n in TRANSPOSED_ENTRY:
            res = _sum_adamw(parts, w[n][0].T, m[n][0].T, v[n][0].T, "adamw_" + n, transposed=True)
            grads[n], delta[n], new_m[n], new_v[n] = [r.T for r in res]
        else:
            grads[n], delta[n], new_m[n], new_v[n] = _sum_adamw(parts, w[n][0], m[n][0], v[n][0], "adamw_" + n)
    for n in ("w_ada", "gdn_conv_w"):
        delta[n], new_m[n], new_v[n] = _adamw(w[n][0], grads[n], m[n][0], v[n][0], "adamw_" + n)
    small_in = [_small_sheet(t["b_ada"], t) for t in (w, grads, m, v)]
    for res, out in zip(_adamw(*small_in, "adamw_small"), (delta, new_m, new_v)):
        out.update(_from_small_sheet(res))

    def shaped(d):
        return [d[n].reshape(w[n].shape) for n in WEIGHT_NAMES]

    return (loss, dx[None], *shaped(grads), *shaped(delta), *shaped(new_m), *shaped(new_v))
```

```python
import functools

import jax
import jax.numpy as jnp
import numpy as np
from jax import lax
from jax.experimental import pallas as pl
from jax.experimental.pallas import tpu as pltpu

F32 = jnp.float32
BF16 = jnp.bfloat16

D_MODEL = 1024
D_FF = 2816
N_MOD = 9
HEADS = 4
HEAD_DIM = 128
CHUNK = 64
EPS = 1e-6
ROPE = 64
Q_LORA = 384
KV_LORA = 256
N_IN = 2760
N_IN_PACKED = 2816
ROPE_BASE = 10000.0
LOG2_E = 1.4426950408889634
N_DEV = 8

ADAM_LR = 0.001
ADAM_B1 = 0.9
ADAM_B2 = 0.999
ADAM_EPS = 1e-08
ADAM_WD = 0.01
ADAM_STEP = 10

VMEM_LIMIT_BYTES = 56 * 1024 * 1024
MATMUL_ROWS = (1024, 512, 256, 128)
MESH = pl.DeviceIdType.MESH


def _params(sem=None):
    return pltpu.CompilerParams(dimension_semantics=sem, vmem_limit_bytes=VMEM_LIMIT_BYTES)


def _pick(dim, prefs):
    for p in prefs:
        if dim % p == 0:
            return p
    return dim


_DIMS = {"nn": (((1,), (0,)), ((), ())), "nt": (((1,), (1,)), ((), ())), "tn": (((0,), (0,)), ((), ()))}


def _dot_raw(a, b, mode):
    return lax.dot_general(a.astype(BF16), b.astype(BF16), _DIMS[mode], preferred_element_type=F32)


def _dot_hi(a, b, mode="nn"):
    return lax.dot_general(a, b, _DIMS[mode], precision=lax.Precision.HIGHEST, preferred_element_type=F32)


@functools.partial(jax.custom_vjp, nondiff_argnums=(2,))
def _bdot(a, b, mode):
    return _dot_raw(a, b, mode)


def _bdot_fwd(a, b, mode):
    return _dot_raw(a, b, mode), (a, b)


def _bdot_bwd(mode, res, g):
    a, b = res
    if mode == "nn":
        return _dot_raw(g, b, "nt"), _dot_raw(a, g, "tn")
    if mode == "nt":
        return _dot_raw(g, b, "nn"), _dot_raw(g, a, "tn")
    return _dot_raw(b, g, "nt"), _dot_raw(a, g, "nn")


_bdot.defvjp(_bdot_fwd, _bdot_bwd)


def _mm(a, b, mode, *, name, out_dtypes=(F32,), epi=None, extras=(), extra_params=(), hi=False,
        tm=None, tn=None, tk=None):
    if mode == "nn":
        (M, K), (_, N) = a.shape, b.shape
    elif mode == "nt":
        (M, K), (N, _) = a.shape, b.shape
    else:
        (K, M), (_, N) = a.shape, b.shape
    tm = tm or _pick(M, (512, 1408, 256, 128) if mode == "tn" else MATMUL_ROWS + (384, 352))
    tn = tn or _pick(N, (1024, 1408, 768, 512, 384, 256, 128))
    tk = tk or _pick(K, (1024, 1408, 512, 384, 256, 128))
    a_spec = {"nn": pl.BlockSpec((tm, tk), lambda i, j, k: (i, k)), "nt": pl.BlockSpec((tm, tk), lambda i, j, k: (i, k)),
              "tn": pl.BlockSpec((tk, tm), lambda i, j, k: (k, i))}[mode]
    b_spec = {"nn": pl.BlockSpec((tk, tn), lambda i, j, k: (k, j)), "nt": pl.BlockSpec((tn, tk), lambda i, j, k: (j, k)),
              "tn": pl.BlockSpec((tk, tn), lambda i, j, k: (k, j))}[mode]
    mn_spec = pl.BlockSpec((tm, tn), lambda i, j, k: (i, j))
    return _mmg(a, b, mode, name=name, grid=(M // tm, N // tn, K // tk), a_spec=a_spec, b_spec=b_spec, out_spec=mn_spec,
                out_shapes=[jax.ShapeDtypeStruct((M, N), dt) for dt in out_dtypes], acc_shape=(tm, tn), epi=epi,
                extras=list(extras) + list(extra_params),
                extra_specs=[mn_spec] * len(extras) + [pl.BlockSpec((1, tn), lambda i, j, k: (0, j))] * len(extra_params),
                hi=hi)


def _mmg(a, b, mode, *, name, grid, a_spec, b_spec, out_spec, out_shapes, acc_shape, epi=None, extras=(),
         extra_specs=(), hi=False):
    nk = grid[2]
    n_e, n_o = len(extras), len(out_shapes)

    def body(*refs):
        a_ref, b_ref = refs[:2]
        e_refs = refs[2:2 + n_e]
        o_refs = refs[2 + n_e:2 + n_e + n_o]
        acc_ref = refs[-1]
        k = pl.program_id(2)

        @pl.when(k == 0)
        def _():
            acc_ref[...] = jnp.zeros_like(acc_ref)

        if hi:
            acc_ref[...] += _dot_hi(a_ref[...].astype(F32), b_ref[...].astype(F32), mode)
        else:
            acc_ref[...] += _dot_raw(a_ref[...], b_ref[...], mode)

        @pl.when(k == nk - 1)
        def _():
            acc = acc_ref[...]
            outs = (acc,) if epi is None else epi(acc, *[e[...].astype(F32) for e in e_refs])
            for o_ref, o in zip(o_refs, outs):
                o_ref[...] = o.astype(o_ref.dtype)

    outs = pl.pallas_call(
        body, name=name, grid=grid,
        in_specs=[a_spec, b_spec] + list(extra_specs),
        out_specs=[out_spec] * n_o,
        out_shape=list(out_shapes),
        scratch_shapes=[pltpu.VMEM(acc_shape, F32)],
        compiler_params=_params(("parallel", "parallel", "arbitrary")),
    )(a, b, *extras)
    return outs if n_o > 1 else outs[0]


def _row_spec(th, cw, ci):
    return pl.BlockSpec((th, cw), lambda i: (i, ci))


def _full_spec(shape):
    return pl.BlockSpec(shape, lambda i: (0,) * len(shape))


def _rowwise(fn, rows, params, outs, n_steps, name):
    n_r, n_p, n_o = len(rows), len(params), len(outs)

    def body(*refs):
        vals = [r[...].astype(F32) for r in refs[:n_r + n_p]]
        res = fn(*vals)
        for o_ref, o in zip(refs[n_r + n_p:], res):
            o_ref[...] = o.astype(o_ref.dtype)

    across = [len(o) == 4 for o in outs]
    res = pl.pallas_call(
        body, name=name, grid=(n_steps,),
        in_specs=[_row_spec(th, cw, ci) for (_, th, cw, ci) in rows] + [_full_spec(p.shape) for p in params],
        out_specs=[pl.BlockSpec((o[0], o[1]), lambda i: (0, i)) if ac else _row_spec(o[0], o[1], 0)
                   for o, ac in zip(outs, across)],
        out_shape=[jax.ShapeDtypeStruct((o[0], n_steps * o[1]) if ac else (n_steps * o[0], o[1]), o[2])
                   for o, ac in zip(outs, across)],
        compiler_params=_params(("parallel",)),
    )(*[r[0] for r in rows], *params)
    return res


def _rowwise_bwd(fn, rows, aux, params, douts, n_steps, name, row_dtypes=None, adds=()):
    n_r, n_a, n_p, n_d, n_add = len(rows), len(aux), len(params), len(douts), len(adds)
    row_dtypes = row_dtypes or (F32,) * n_r

    def body(*refs):
        it = iter(refs)
        r_vals = [next(it)[...].astype(F32) for _ in range(n_r)]
        a_vals = [next(it)[...].astype(F32) for _ in range(n_a)]
        p_vals = [next(it)[...].astype(F32) for _ in range(n_p)]
        d_vals = [next(it)[...].astype(F32) for _ in range(n_d)]
        add_vals = [next(it)[...].astype(F32) for _ in range(n_add)]
        dr_refs = [next(it) for _ in range(n_r)]
        dp_refs = [next(it) for _ in range(n_p)]

        def f(*rp):
            return tuple(fn(*rp[:n_r], *a_vals, *rp[n_r:]))

        _, vjp = jax.vjp(f, *r_vals, *p_vals)
        grads = list(vjp(tuple(d_vals)))
        for (ri, _), av in zip(adds, add_vals):
            grads[ri] = grads[ri] + av
        for dr_ref, g in zip(dr_refs, grads[:n_r]):
            dr_ref[...] = g.astype(dr_ref.dtype)

        @pl.when(pl.program_id(0) == 0)
        def _():
            for dp_ref in dp_refs:
                dp_ref[...] = jnp.zeros_like(dp_ref)

        for dp_ref, g in zip(dp_refs, grads[n_r:]):
            dp_ref[...] += g

    all_rows = list(rows) + list(aux) + list(douts) + [(arr,) + tuple(rows[ri][1:3]) + (0,) for ri, arr in adds]
    in_specs = ([_row_spec(th, cw, ci) for (_, th, cw, ci) in list(rows) + list(aux)]
                + [_full_spec(p.shape) for p in params]
                + [_row_spec(th, cw, ci) for (_, th, cw, ci) in all_rows[n_r + n_a:]])
    res = pl.pallas_call(
        body, name=name, grid=(n_steps,),
        in_specs=in_specs,
        out_specs=[_row_spec(th, cw, 0) for (_, th, cw, _) in rows] + [_full_spec(p.shape) for p in params],
        out_shape=[jax.ShapeDtypeStruct((n_steps * th, cw), dt) for (_, th, cw, _), dt in zip(rows, row_dtypes)]
        + [jax.ShapeDtypeStruct(p.shape, F32) for p in params],
        compiler_params=_params(("arbitrary",)),
    )(*[r[0] for r in list(rows) + list(aux)], *params, *[r[0] for r in all_rows[n_r + n_a:]])
    return res[:n_r], res[n_r:]


def _sigmoid(x):
    return lax.logistic(x)


def _silu(x):
    return x * _sigmoid(x)


def _rms(x, w=None, n=None):
    n = n or x.shape[-1]
    y = x * lax.rsqrt(jnp.sum(x * x, axis=-1, keepdims=True) * (1.0 / n) + EPS)
    return y if w is None else y * w


def _modulate(x, scale, shift):
    return _rms(x) * (1.0 + scale) + shift


def _softplus(x):
    return jnp.maximum(x, 0.0) + jnp.log1p(jnp.exp(-jnp.abs(x)))


@jax.custom_vjp
def _rot_half64(x):
    lane = lax.broadcasted_iota(jnp.int32, x.shape, 1)
    up = pltpu.roll(x, 96, 1)
    down = pltpu.roll(x, 32, 1)
    return jnp.where(lane < 32, up, jnp.where(lane < 64, down, 0.0))


_rot_half64.defvjp(lambda x: (_rot_half64(x), None), lambda _, g: (_rot_half64(g),))


def _rope128(x, cos_p, sin_p):
    return x * cos_p + _rot_half64(x) * sin_p


def _gdn_pre_fn(qkvc, kab, alog_p, dt_p):
    a = _silu(qkvc)
    qs, ks = [], []
    for h in range(HEADS):
        qh = a[:, HEAD_DIM * h:HEAD_DIM * (h + 1)]
        kh = a[:, 512 + HEAD_DIM * h:512 + HEAD_DIM * (h + 1)]
        qs.append(qh * lax.rsqrt(jnp.sum(qh * qh, axis=-1, keepdims=True) + EPS) * (HEAD_DIM ** -0.5))
        ks.append(kh * lax.rsqrt(jnp.sum(kh * kh, axis=-1, keepdims=True) + EPS))
    lane = lax.broadcasted_iota(jnp.int32, kab.shape, 1)
    g_full = -jnp.exp(alog_p) * _softplus(kab + dt_p)
    b_full = _sigmoid(kab)
    gb = jnp.where((lane >= 64) & (lane < 68), g_full, jnp.where((lane >= 68) & (lane < 72), b_full, 0.0))
    return jnp.concatenate(qs, axis=1), jnp.concatenate(ks, axis=1), a[:, 1024:1536], gb


INTRA_ROWS = (256, 128, 64)

_BNN = (((2,), (1,)), ((0,), (0,)))
_BNT = (((2,), (2,)), ((0,), (0,)))


def _split_bf16(a):
    hi = a.astype(BF16)
    return hi, (a - hi.astype(F32)).astype(BF16)


def _dot3_raw(a, b, dims):
    a_hi, a_lo = _split_bf16(a)
    b_hi, b_lo = _split_bf16(b)
    dot = lambda x_, y_: lax.dot_general(x_, y_, dims, preferred_element_type=F32)
    return dot(a_hi, b_hi) + (dot(a_hi, b_lo) + dot(a_lo, b_hi))


@functools.partial(jax.custom_vjp, nondiff_argnums=(2, 3))
def _dot3(a, b, nt, exact_bwd=True):
    return _dot3_raw(a, b, _BNT if nt else _BNN)


def _dot3_fwd(a, b, nt, exact_bwd):
    return _dot3_raw(a, b, _BNT if nt else _BNN), (a, b)


def _dot3_bwd(nt, exact_bwd, res, g):
    a, b = res
    if exact_bwd:
        dot = _dot3_raw
    else:
        dot = lambda x_, y_, d_: lax.dot_general(x_.astype(BF16), y_.astype(BF16), d_, preferred_element_type=F32)
    if nt:
        return dot(g, b, _BNN), dot(jnp.swapaxes(g, 1, 2), a, _BNN)
    return dot(g, b, _BNT), dot(jnp.swapaxes(a, 1, 2), g, _BNN)


_dot3.defvjp(_dot3_fwd, _dot3_bwd)


@functools.partial(jax.custom_vjp, nondiff_argnums=(2,))
def _bdot_b(a, b, nt):
    return lax.dot_general(a.astype(BF16), b.astype(BF16), _BNT if nt else _BNN, preferred_element_type=F32)


def _bdot_b_fwd(a, b, nt):
    return _bdot_b(a, b, nt), (a, b)


def _bdot_b_bwd(nt, res, g):
    a, b = res
    dot = lambda x_, y_, d_: lax.dot_general(x_.astype(BF16), y_.astype(BF16), d_, preferred_element_type=F32)
    if nt:
        return dot(g, b, _BNN), dot(jnp.swapaxes(g, 1, 2), a, _BNN)
    return dot(g, b, _BNT), dot(jnp.swapaxes(a, 1, 2), g, _BNN)


_bdot_b.defvjp(_bdot_b_fwd, _bdot_b_bwd)


@jax.custom_vjp
def _inverse_given(a_mat, inv):
    return inv


def _inverse_given_bwd(inv, g):
    inv_t = jnp.swapaxes(inv, 1, 2)
    return -_dot3_raw(_dot3_raw(inv_t, g, _BNN), inv_t, _BNN), jnp.zeros_like(inv)


_inverse_given.defvjp(lambda a_mat, inv: (inv, inv), _inverse_given_bwd)


def _intra_batched(q, k, v, g_col, b_col, inv_known=None):
    c = CHUNK
    nb = q.shape[0]
    row = lax.broadcasted_iota(jnp.int32, (1, c, c), 1)
    col = lax.broadcasted_iota(jnp.int32, (1, c, c), 2)
    incl, strict, eye = row >= col, row > col, row == col
    tri = jnp.broadcast_to(jnp.where(incl, 1.0, 0.0).astype(F32), (nb, c, c))
    ident = jnp.where(eye, 1.0, 0.0).astype(F32)
    g_wide = _dot3(tri, jnp.broadcast_to(g_col, (nb, c, HEAD_DIM)), False)
    g_i = g_wide[:, :, :c]
    g_j = jnp.sum(jnp.where(eye, g_i, 0.0), axis=1, keepdims=True)
    decay = jnp.where(incl, jnp.exp(jnp.where(incl, g_i - g_j, 0.0)), 0.0)
    kk = _bdot_b(k, k, True)
    a_mat = jnp.where(strict, b_col * kk * decay, 0.0)
    if inv_known is None:
        x_pow = -a_mat
        inv = ident + x_pow
        for _ in range(5):
            x_pow = _dot3(x_pow, x_pow, False, False)
            inv = inv + _dot3(inv, x_pow, False, False)
    else:
        inv = _inverse_given(a_mat, inv_known)
    e_wide = jnp.exp(g_wide)
    u = _dot3(inv, v * b_col, False)
    wk = _dot3(inv, k * b_col * e_wide, False)
    qk = _bdot_b(q, k, True) * decay
    last = lax.broadcasted_iota(jnp.int32, (1, c, HEAD_DIM), 1) == c - 1
    g_last = jnp.sum(jnp.where(last, g_wide, 0.0), axis=1, keepdims=True)
    qd = q * e_wide
    kd = k * jnp.exp(g_last - g_wide)
    gl = jnp.broadcast_to(jnp.exp(g_last), (nb, 8, HEAD_DIM))
    return u, wk, qd, kd, qk, gl, inv


def _gdn_intra_fn(q, k, v, gb, *inv_known):
    t = q.shape[0]
    nch = t // CHUNK
    lane = lax.broadcasted_iota(jnp.int32, gb.shape, 1)

    def heads_first(x_):
        return jnp.concatenate([x_[:, HEAD_DIM * h:HEAD_DIM * (h + 1)].reshape(nch, CHUNK, HEAD_DIM) for h in range(HEADS)],
                               axis=0)

    def column(first_lane):
        return jnp.concatenate([jnp.sum(jnp.where(lane == first_lane + h, gb, 0.0), axis=1, keepdims=True)
                                .reshape(nch, CHUNK, 1) for h in range(HEADS)], axis=0)

    known = jnp.concatenate([x_.reshape(nch, CHUNK, CHUNK) for x_ in inv_known], axis=0) if inv_known else None
    u, wk, qd, kd, qk, gl, inv = _intra_batched(heads_first(q), heads_first(k), heads_first(v), column(64), column(68), known)

    def rows_first(x_):
        r, w_ = x_.shape[1], x_.shape[2]
        return jnp.concatenate([x_[nch * h:nch * (h + 1)].reshape(nch * r, w_) for h in range(HEADS)], axis=1)

    per_head = lambda x_: [x_[nch * h:nch * (h + 1)].reshape(t, CHUNK) for h in range(HEADS)]
    outs = (rows_first(u), rows_first(wk), rows_first(qd), rows_first(kd), *per_head(qk), rows_first(gl))
    return outs if inv_known else outs + tuple(per_head(inv))


def _scan_step(s0, u, wk, qd, kd, qk, gl):
    v_new = u - _bdot_b(wk, s0, False)
    o = _bdot_b(qd, s0, False) + _bdot_b(qk, v_new, False)
    s1 = s0 * gl[:, 0:1, :] + _bdot_b(jnp.swapaxes(kd, 1, 2), v_new, False)
    return o, s1


def _mix_post_fn(o_a, z, o_b, gnw, onw):
    parts = [_rms(o_a[:, HEAD_DIM * h:HEAD_DIM * (h + 1)], gnw) * _silu(z[:, HEAD_DIM * h:HEAD_DIM * (h + 1)])
             for h in range(HEADS)]
    parts += [_rms(o_b[:, HEAD_DIM * h:HEAD_DIM * (h + 1)], onw) for h in range(HEADS)]
    return (jnp.concatenate(parts, axis=1),)


def _mla_pre_fn(ckv, cq, kab, cos_p, sin_p, qnw, kvnw, wuq, wukv, qn_w, qr_w, kn_w, kr_w):
    scale = (HEAD_DIM + ROPE) ** -0.5 * LOG2_E
    qf = _bdot(_rms(cq, qnw), wuq, "nn")
    kvf = _bdot(_rms(ckv, kvnw), wukv, "nn")
    lane = lax.broadcasted_iota(jnp.int32, kab.shape, 1)
    kr = _rope128(_rms(jnp.where(lane < ROPE, kab, 0.0), kr_w, n=ROPE), cos_p, sin_p)
    qs, ks = [], []
    for h in range(HEADS):
        qn = _rms(qf[:, 256 * h:256 * h + 128], qn_w) * scale
        qr = _rope128(_rms(qf[:, 256 * h + 128:256 * h + 256], qr_w, n=ROPE), cos_p, sin_p) * scale
        qs += [qn, qr]
        ks += [_rms(kvf[:, 128 * h:128 * (h + 1)], kn_w), kr]
    return jnp.concatenate(qs, axis=1), jnp.concatenate(ks, axis=1), kvf[:, 512:]


def _conv_fwd(proj, conv_w, tm, name):
    S = proj.shape[0]
    C = 1536
    nb = tm // 8

    def body(x_ref, prev_ref, w_ref, o_ref, ext_ref):
        i = pl.program_id(0)
        ext_ref[0:8, :] = jnp.where(i > 0, prev_ref[...], 0.0)
        ext_ref[8:, :] = x_ref[...]
        acc = jnp.zeros((tm, C), F32)
        for k in range(4):
            acc = acc + w_ref[k:k + 1, :] * ext_ref[pl.ds(5 + k, tm), :]
        o_ref[...] = acc

    return pl.pallas_call(
        body, name=name, grid=(S // tm,),
        in_specs=[pl.BlockSpec((tm, C), lambda i: (i, 0)),
                  pl.BlockSpec((8, C), lambda i: (jnp.maximum(i * nb - 1, 0), 0)),
                  pl.BlockSpec((4, C), lambda i: (0, 0))],
        out_specs=pl.BlockSpec((tm, C), lambda i: (i, 0)),
        out_shape=jax.ShapeDtypeStruct((S, C), F32),
        scratch_shapes=[pltpu.VMEM((tm + 8, C), F32)],
        compiler_params=_params(("arbitrary",)),
    )(proj, proj, conv_w)


def _conv_bwd(proj, dout, conv_w, tm, name):
    S = proj.shape[0]
    C = 1536
    nb = tm // 8
    n_steps = S // tm

    def body(x_ref, prev_ref, d_ref, next_ref, w_ref, dx_ref, dw_ref, xext_ref, dext_ref):
        i = pl.program_id(0)
        xext_ref[0:8, :] = jnp.where(i > 0, prev_ref[...], 0.0)
        xext_ref[8:, :] = x_ref[...]
        dext_ref[0:tm, :] = d_ref[...]
        dext_ref[tm:, :] = jnp.where(i < n_steps - 1, next_ref[...], 0.0)
        d = d_ref[...]
        acc = jnp.zeros((tm, C), F32)
        dws = []
        for k in range(4):
            acc = acc + w_ref[k:k + 1, :] * dext_ref[pl.ds(3 - k, tm), :]
            dws.append(jnp.sum(d * xext_ref[pl.ds(5 + k, tm), :], axis=0, keepdims=True))
        dx_ref[...] = acc.astype(dx_ref.dtype)

        @pl.when(i == 0)
        def _():
            dw_ref[...] = jnp.zeros_like(dw_ref)

        dw_ref[...] += jnp.concatenate(dws + [jnp.zeros((4, C), F32)], axis=0)

    return pl.pallas_call(
        body, name=name, grid=(n_steps,),
        in_specs=[pl.BlockSpec((tm, C), lambda i: (i, 0)),
                  pl.BlockSpec((8, C), lambda i: (jnp.maximum(i * nb - 1, 0), 0)),
                  pl.BlockSpec((tm, C), lambda i: (i, 0)),
                  pl.BlockSpec((8, C), lambda i: (jnp.minimum((i + 1) * nb, S // 8 - 1), 0)),
                  pl.BlockSpec((4, C), lambda i: (0, 0))],
        out_specs=[pl.BlockSpec((tm, C), lambda i: (i, 0)), pl.BlockSpec((8, C), lambda i: (0, 0))],
        out_shape=[jax.ShapeDtypeStruct((S, C), BF16), jax.ShapeDtypeStruct((8, C), F32)],
        scratch_shapes=[pltpu.VMEM((tm + 8, C), F32), pltpu.VMEM((tm + 8, C), F32)],
        compiler_params=_params(("arbitrary",)),
    )(proj, proj, dout, dout, conv_w)


SCAN_CHUNKS = (4, 2, 1)


def _gdn_scan_fwd(u, wk, qd, kd, qks, gl, name):
    S = u.shape[0]
    nc = S // CHUNK
    cs = _pick(nc, SCAN_CHUNKS)
    W = HEADS * HEAD_DIM

    def body(u_ref, wk_ref, qd_ref, kd_ref, qk0, qk1, qk2, qk3, gl_ref, o_ref, sp_ref, s_ref):
        @pl.when(pl.program_id(0) == 0)
        def _():
            s_ref[...] = jnp.zeros_like(s_ref)

        state = s_ref[...]
        for c in range(cs):
            rows, gl_rows = slice(CHUNK * c, CHUNK * (c + 1)), slice(8 * c, 8 * (c + 1))
            sp_ref[c] = state
            o, state = _scan_step(state, _heads(u_ref, HEAD_DIM, rows), _heads(wk_ref, HEAD_DIM, rows),
                                  _heads(qd_ref, HEAD_DIM, rows), _heads(kd_ref, HEAD_DIM, rows),
                                  jnp.stack([r[rows, :] for r in (qk0, qk1, qk2, qk3)]), _heads(gl_ref, HEAD_DIM, gl_rows))
            for h in range(HEADS):
                o_ref[rows, HEAD_DIM * h:HEAD_DIM * (h + 1)] = o[h]
        s_ref[...] = state

    row = pl.BlockSpec((cs * CHUNK, W), lambda n: (n, 0))
    qk_spec = pl.BlockSpec((cs * CHUNK, CHUNK), lambda n: (n, 0))
    return pl.pallas_call(
        body, name=name, grid=(nc // cs,),
        in_specs=[row, row, row, row, qk_spec, qk_spec, qk_spec, qk_spec, pl.BlockSpec((cs * 8, W), lambda n: (n, 0))],
        out_specs=[row, pl.BlockSpec((cs, HEADS, HEAD_DIM, HEAD_DIM), lambda n: (n, 0, 0, 0))],
        out_shape=[jax.ShapeDtypeStruct((S, W), F32), jax.ShapeDtypeStruct((nc, HEADS, HEAD_DIM, HEAD_DIM), F32)],
        scratch_shapes=[pltpu.VMEM((HEADS, HEAD_DIM, HEAD_DIM), F32)],
        compiler_params=_params(("arbitrary",)),
    )(u, wk, qd, kd, *qks, gl)


def _gdn_scan_bwd(u, wk, qd, kd, qks, gl, s_prev, d_o, name):
    S = u.shape[0]
    nc = S // CHUNK
    cs = _pick(nc, SCAN_CHUNKS)
    nb = nc // cs
    W = HEADS * HEAD_DIM

    def body(u_ref, wk_ref, qd_ref, kd_ref, qk0, qk1, qk2, qk3, gl_ref, sp_ref, do_ref,
             du_ref, dwk_ref, dqd_ref, dkd_ref, dqk0, dqk1, dqk2, dqk3, dgl_ref, ds_ref):
        @pl.when(pl.program_id(0) == 0)
        def _():
            ds_ref[...] = jnp.zeros_like(ds_ref)

        d_state = ds_ref[...]
        for c in reversed(range(cs)):
            rows, gl_rows = slice(CHUNK * c, CHUNK * (c + 1)), slice(8 * c, 8 * (c + 1))
            _, vjp = jax.vjp(_scan_step, sp_ref[c], _heads(u_ref, HEAD_DIM, rows), _heads(wk_ref, HEAD_DIM, rows),
                             _heads(qd_ref, HEAD_DIM, rows), _heads(kd_ref, HEAD_DIM, rows),
                             jnp.stack([r[rows, :] for r in (qk0, qk1, qk2, qk3)]), _heads(gl_ref, HEAD_DIM, gl_rows))
            d_state, du, dwk, dqd, dkd, dqk, dgl = vjp((_heads(do_ref, HEAD_DIM, rows), d_state))
            for h, dqk_ref in enumerate((dqk0, dqk1, dqk2, dqk3)):
                sl = slice(HEAD_DIM * h, HEAD_DIM * (h + 1))
                du_ref[rows, sl] = du[h]
                dwk_ref[rows, sl] = dwk[h]
                dqd_ref[rows, sl] = dqd[h]
                dkd_ref[rows, sl] = dkd[h]
                dqk_ref[rows, :] = dqk[h]
                dgl_ref[gl_rows, sl] = dgl[h]
        ds_ref[...] = d_state

    rev = lambda n: (nb - 1 - n, 0)
    row = pl.BlockSpec((cs * CHUNK, W), rev)
    qk_spec = pl.BlockSpec((cs * CHUNK, CHUNK), rev)
    gl_spec = pl.BlockSpec((cs * 8, W), rev)
    qk_shape = jax.ShapeDtypeStruct((S, CHUNK), F32)
    row_shape = jax.ShapeDtypeStruct((S, W), F32)
    return pl.pallas_call(
        body, name=name, grid=(nb,),
        in_specs=[row, row, row, row, qk_spec, qk_spec, qk_spec, qk_spec, gl_spec,
                  pl.BlockSpec((cs, HEADS, HEAD_DIM, HEAD_DIM), lambda n: (nb - 1 - n, 0, 0, 0)), row],
        out_specs=[row, row, row, row, qk_spec, qk_spec, qk_spec, qk_spec, gl_spec],
        out_shape=[row_shape] * 4 + [qk_shape] * 4 + [jax.ShapeDtypeStruct((nc * 8, W), F32)],
        scratch_shapes=[pltpu.VMEM((HEADS, HEAD_DIM, HEAD_DIM), F32)],
        compiler_params=_params(("arbitrary",)),
    )(u, wk, qd, kd, *qks, gl, s_prev, d_o)


NEG = -1e30


def _chunk_mask(i, j, t, transposed=False):
    q_axis, k_axis = (1, 0) if transposed else (0, 1)
    r = (i * t + lax.broadcasted_iota(jnp.int32, (t, t), q_axis)) // CHUNK
    c = (j * t + lax.broadcasted_iota(jnp.int32, (t, t), k_axis)) // CHUNK
    return c <= r


def _tile_pairs(n, by_key):
    pairs = [(i, j) for j in range(n) for i in range(j, n)] if by_key else [(i, j) for i in range(n) for j in range(i + 1)]
    return jnp.asarray(np.array([p[0] for p in pairs], np.int32)), jnp.asarray(np.array([p[1] for p in pairs], np.int32))


def _heads(ref, width, rows=slice(None)):
    return jnp.stack([ref[rows, width * h:width * (h + 1)] for h in range(HEADS)])


def _bmm(a, b, dims):
    return lax.dot_general(a.astype(BF16), b.astype(BF16), dims, preferred_element_type=F32)


def _attn_fwd(q, k, v_t, t, name):
    S = q.shape[0]
    n = S // t
    qi, kj = _tile_pairs(n, by_key=False)

    def body(qi_ref, kj_ref, q_ref, k_ref, vt_ref, o_ref, lse_ref, m_ref, l_ref, acc_ref):
        i, j = qi_ref[pl.program_id(0)], kj_ref[pl.program_id(0)]

        @pl.when(j == 0)
        def _():
            m_ref[...] = jnp.full_like(m_ref, NEG)
            l_ref[...] = jnp.zeros_like(l_ref)
            acc_ref[...] = jnp.zeros_like(acc_ref)

        def update(masked):
            s_t = _bmm(_heads(k_ref, 256), _heads(q_ref, 256), _BNT)
            if masked:
                s_t = jnp.where(_chunk_mask(i, j, t, transposed=True)[None], s_t, NEG)
            m_old = m_ref[...]
            m_new = jnp.maximum(m_old, jnp.max(s_t, axis=1, keepdims=True))
            p_t = jnp.exp2(s_t - m_new)
            alpha = jnp.exp2(m_old - m_new)
            l_ref[...] = alpha * l_ref[...] + jnp.sum(p_t, axis=1, keepdims=True)
            v_heads = jnp.stack([vt_ref[HEAD_DIM * h:HEAD_DIM * (h + 1), :] for h in range(HEADS)])
            acc_ref[...] = alpha * acc_ref[...] + _bmm(v_heads, p_t, _BNN)
            m_ref[...] = m_new

        @pl.when(j < i)
        def _():
            update(False)

        @pl.when(j == i)
        def _():
            update(True)
            for h in range(HEADS):
                sl = slice(HEAD_DIM * h, HEAD_DIM * (h + 1))
                o_ref[:, sl] = jnp.transpose(acc_ref[h] / l_ref[h])
                lse_ref[:, sl] = jnp.transpose(jnp.broadcast_to(m_ref[h] + jnp.log(l_ref[h]) * LOG2_E, (HEAD_DIM, t)))

    row = lambda p, qi_, kj_: (qi_[p], 0)
    return pl.pallas_call(
        body, name=name,
        grid_spec=pltpu.PrefetchScalarGridSpec(
            num_scalar_prefetch=2, grid=(qi.shape[0],),
            in_specs=[pl.BlockSpec((t, HEADS * 256), row), pl.BlockSpec((t, HEADS * 256), lambda p, qi_, kj_: (kj_[p], 0)),
                      pl.BlockSpec((HEADS * HEAD_DIM, t), lambda p, qi_, kj_: (0, kj_[p]))],
            out_specs=[pl.BlockSpec((t, HEADS * HEAD_DIM), row)] * 2,
            scratch_shapes=[pltpu.VMEM((HEADS, 1, t), F32), pltpu.VMEM((HEADS, 1, t), F32),
                            pltpu.VMEM((HEADS, HEAD_DIM, t), F32)]),
        out_shape=[jax.ShapeDtypeStruct((S, HEADS * HEAD_DIM), F32)] * 2,
        compiler_params=_params(("arbitrary",)),
    )(qi, kj, q, k, v_t)


def _attn_stats(o, lse, d_o, t, name):
    S = o.shape[0]

    def body(o_ref, lse_ref, do_ref, st_ref):
        lane = lax.broadcasted_iota(jnp.int32, (t, HEAD_DIM), 1)
        stats = jnp.zeros((t, HEAD_DIM), F32)
        for h in range(HEADS):
            sl = slice(HEAD_DIM * h, HEAD_DIM * (h + 1))
            delta = jnp.sum(do_ref[:, sl] * o_ref[:, sl], axis=1, keepdims=True)
            stats = stats + jnp.where(lane == h, lse_ref[:, sl], 0.0) + jnp.where(lane == HEADS + h, delta, 0.0)
        st_ref[...] = jnp.transpose(stats)[0:8, :]

    row = pl.BlockSpec((t, HEADS * HEAD_DIM), lambda i: (i, 0))
    return pl.pallas_call(
        body, name=name, grid=(S // t,),
        in_specs=[row, row, row], out_specs=pl.BlockSpec((8, t), lambda i: (0, i)),
        out_shape=jax.ShapeDtypeStruct((8, S), F32),
        compiler_params=_params(("parallel",)),
    )(o, lse, d_o)


BWD_GROUP = 2


def _attn_bwd(q, k, v, d_o, stats, t, name):
    S = q.shape[0]
    n = S // t
    groups = HEADS // BWD_GROUP
    gq, gv = BWD_GROUP * 256, BWD_GROUP * HEAD_DIM
    qi, kj = _tile_pairs(n, by_key=True)
    n_pairs = qi.shape[0]
    st = stats.reshape(2, groups, BWD_GROUP, S).transpose(1, 0, 2, 3).reshape(groups, 2 * BWD_GROUP, S)
    st = jnp.pad(st, ((0, 0), (0, 8 - 2 * BWD_GROUP), (0, 0)))

    def heads(ref, width, rows=slice(None)):
        return jnp.stack([ref[rows, width * h:width * (h + 1)] for h in range(BWD_GROUP)])

    def body(qi_ref, kj_ref, q_ref, k_ref, v_ref, do_ref, st_ref, dq_hbm, dk_ref, dv_ref, dq_acc, sem):
        g, p = pl.program_id(0), pl.program_id(1)
        i, j = qi_ref[p], kj_ref[p]

        @pl.when(i == j)
        def _():
            dk_ref[...] = jnp.zeros_like(dk_ref)
            dv_ref[...] = jnp.zeros_like(dv_ref)

        def update(masked):
            qh, kh = heads(q_ref, 256), heads(k_ref, 256)
            d_out = heads(do_ref, HEAD_DIM)
            stv = st_ref[...]
            lse_row = jnp.stack([stv[h:h + 1, :] for h in range(BWD_GROUP)])
            delta_row = jnp.stack([stv[BWD_GROUP + h:BWD_GROUP + h + 1, :] for h in range(BWD_GROUP)])
            s_t = _bmm(kh, qh, _BNT)
            p_t = jnp.exp2(s_t - lse_row)
            if masked:
                p_t = jnp.where(_chunk_mask(i, j, t, transposed=True)[None], p_t, 0.0)
            dv = _bmm(p_t, d_out, _BNN)
            dp_t = _bmm(heads(v_ref, HEAD_DIM), d_out, _BNT)
            ds_t = p_t * (dp_t - delta_row)
            dk = _bmm(ds_t, qh, _BNN)
            dq = _bmm(jnp.swapaxes(ds_t, 1, 2), kh, _BNN)
            rows = pl.ds(pl.multiple_of(i * t, t), t)
            for h in range(BWD_GROUP):
                dk_ref[:, 256 * h:256 * (h + 1)] += dk[h]
                dv_ref[:, HEAD_DIM * h:HEAD_DIM * (h + 1)] += dv[h]

            @pl.when(j == 0)
            def _():
                for h in range(BWD_GROUP):
                    dq_acc[rows, 256 * h:256 * (h + 1)] = dq[h]

            @pl.when(j > 0)
            def _():
                for h in range(BWD_GROUP):
                    dq_acc[rows, 256 * h:256 * (h + 1)] += dq[h]

        @pl.when(i == j)
        def _():
            update(True)

        @pl.when(i > j)
        def _():
            update(False)

        @pl.when(i == n - 1)
        def _():
            dk_ref[...] *= 1.0 / LOG2_E

        @pl.when(p == n_pairs - 1)
        def _():
            dq_acc[...] *= 1.0 / LOG2_E
            for gg in range(groups):
                @pl.when(g == gg)
                def _():
                    cp = pltpu.make_async_copy(dq_acc, dq_hbm.at[:, gq * gg:gq * (gg + 1)], sem)
                    cp.start()
                    cp.wait()

    q_blk = lambda g, p, qi_, kj_: (qi_[p], g)
    k_blk = lambda g, p, qi_, kj_: (kj_[p], g)
    return pl.pallas_call(
        body, name=name,
        grid_spec=pltpu.PrefetchScalarGridSpec(
            num_scalar_prefetch=2, grid=(groups, n_pairs),
            in_specs=[pl.BlockSpec((t, gq), q_blk), pl.BlockSpec((t, gq), k_blk), pl.BlockSpec((t, gv), k_blk),
                      pl.BlockSpec((t, gv), q_blk), pl.BlockSpec((None, 8, t), lambda g, p, qi_, kj_: (g, 0, qi_[p]))],
            out_specs=[pl.BlockSpec(memory_space=pl.ANY), pl.BlockSpec((t, gq), k_blk), pl.BlockSpec((t, gv), k_blk)],
            scratch_shapes=[pltpu.VMEM((S, gq), F32), pltpu.SemaphoreType.DMA]),
        out_shape=[jax.ShapeDtypeStruct((S, HEADS * 256), F32), jax.ShapeDtypeStruct((S, HEADS * 256), F32),
                   jax.ShapeDtypeStruct((S, HEADS * HEAD_DIM), F32)],
        compiler_params=_params(("arbitrary", "arbitrary")),
    )(qi, kj, q, k, v, d_o, st)


FFN_PIECE = 2 * D_FF // N_DEV
HID_PIECES = D_FF // FFN_PIECE


def _after_specs(after):
    return [] if after is None else [pl.BlockSpec(memory_space=pl.ANY)]


def _after_args(after):
    return [] if after is None else [after]


def _ffn_gw8(h, d_gate, d_up, name, after=None):
    S = h.shape[0]
    tm = 512
    tk = _pick(S, (512, 256, 128))
    nk = S // tk

    def body(h_ref, dg_ref, du_ref, *rest):
        o_ref, acc_ref = rest[-2:]
        k = pl.program_id(1)

        @pl.when(k == 0)
        def _():
            acc_ref[...] = jnp.zeros_like(acc_ref)

        h_t = jnp.transpose(h_ref[...])
        for p in range(HID_PIECES):
            acc_ref[p] += _dot_raw(h_t, dg_ref[p], "nn")
            acc_ref[HID_PIECES + p] += _dot_raw(h_t, du_ref[p], "nn")

        @pl.when(k == nk - 1)
        def _():
            o_ref[...] = acc_ref[...].astype(o_ref.dtype)

    d_spec = pl.BlockSpec((HID_PIECES, tk, FFN_PIECE), lambda i, k: (0, k, 0))
    return pl.pallas_call(
        body, name=name, grid=(D_MODEL // tm, nk),
        in_specs=[pl.BlockSpec((tk, tm), lambda i, k: (k, i)), d_spec, d_spec] + _after_specs(after),
        out_specs=pl.BlockSpec((2 * HID_PIECES, tm, FFN_PIECE), lambda i, k: (0, i, 0)),
        out_shape=jax.ShapeDtypeStruct((2 * HID_PIECES, D_MODEL, FFN_PIECE), BF16),
        scratch_shapes=[pltpu.VMEM((2 * HID_PIECES, tm, FFN_PIECE), F32)],
        compiler_params=_params(("parallel", "arbitrary")),
    )(h, d_gate, d_up, *_after_args(after))


EPILOGUE_ROWS = 256


def _dmod_epilogue(acc_ref, x_ref, do_ref, sc_ref, sh_ref, dx_ref, dsc_ref, dsh_ref, below, below_in, below_out):
    rows_total = acc_ref.shape[0]
    step = min(EPILOGUE_ROWS, rows_total)
    dsc, dsh, dg = 0.0, 0.0, 0.0
    for r in range(rows_total // step):
        rows = slice(step * r, step * (r + 1))
        _, vjp = jax.vjp(_modulate, x_ref[rows, :], sc_ref[...], sh_ref[...])
        dx, dsc_r, dsh_r = vjp(acc_ref[rows, :])
        dx = dx + do_ref[rows, :]
        dx_ref[rows, :] = dx
        dsc, dsh = dsc + dsc_r, dsh + dsh_r
        if below is not None:
            coef = below[2]
            below_out[0][rows, :] = (coef * below_in[1][...] * dx).astype(below_out[0].dtype)
            dg = dg + jnp.sum(coef * below_in[0][rows, :] * dx, axis=0, keepdims=True)
    dsc_ref[...] += dsc
    dsh_ref[...] += dsh
    if below is not None:
        below_out[1][...] += dg


def _ffn_dh(d_gate, d_up, w8, x, d_out, scale, shift, name, after=None, below=None):
    S = d_gate.shape[1]
    tm = _pick(S, MATMUL_ROWS)
    n_below = 0 if below is None else 2

    def body(dg_ref, du_ref, wg_ref, wu_ref, x_ref, do_ref, sc_ref, sh_ref, *rest):
        below_in = rest[:n_below]
        outs = rest[len(rest) - 4 - n_below:]
        dx_ref, dsc_ref, dsh_ref = outs[:3]
        below_out, acc_ref = outs[3:3 + n_below], outs[-1]
        i, k = pl.program_id(0), pl.program_id(1)

        @pl.when(k == 0)
        def _():
            acc_ref[...] = jnp.zeros_like(acc_ref)

        acc_ref[...] += _dot_raw(dg_ref[...], wg_ref[...], "nt") + _dot_raw(du_ref[...], wu_ref[...], "nt")

        @pl.when((k == 0) & (i == 0))
        def _():
            for r in (dsc_ref, dsh_ref) + tuple(below_out[1:]):
                r[...] = jnp.zeros_like(r)

        @pl.when(k == HID_PIECES - 1)
        def _():
            _dmod_epilogue(acc_ref, x_ref, do_ref, sc_ref, sh_ref, dx_ref, dsc_ref, dsh_ref, below, below_in, below_out)

    d_spec = pl.BlockSpec((None, tm, FFN_PIECE), lambda i, k: (k, i, 0))
    row = pl.BlockSpec((tm, D_MODEL), lambda i, k: (i, 0))
    par = pl.BlockSpec((1, D_MODEL), lambda i, k: (0, 0))
    row_shape, par_shape = jax.ShapeDtypeStruct((S, D_MODEL), F32), jax.ShapeDtypeStruct((1, D_MODEL), F32)
    return pl.pallas_call(
        body, name=name, grid=(S // tm, HID_PIECES),
        in_specs=[d_spec, d_spec,
                  pl.BlockSpec((None, D_MODEL, FFN_PIECE), lambda i, k: (k, 0, 0)),
                  pl.BlockSpec((None, D_MODEL, FFN_PIECE), lambda i, k: (k + HID_PIECES, 0, 0)),
                  row, row, par, par] + [row, par][:n_below] + _after_specs(after),
        out_specs=[row, par, par] + [row, par][:n_below],
        out_shape=[row_shape, par_shape, par_shape] + [jax.ShapeDtypeStruct((S, D_MODEL), BF16), par_shape][:n_below],
        scratch_shapes=[pltpu.VMEM((tm, D_MODEL), F32)],
        compiler_params=_params(("arbitrary", "arbitrary")),
    )(d_gate, d_up, w8, w8, x, d_out, scale, shift, *(below[:2] if below is not None else ()), *_after_args(after))


def _swiglu_bwd(d_hid, hid_by_gate, hid_by_up):
    return d_hid * hid_by_gate, d_hid * hid_by_up


def _adamw_math(w_, g_, m_, v_):
    m_ = ADAM_B1 * m_ + (1.0 - ADAM_B1) * g_
    v_ = ADAM_B2 * v_ + (1.0 - ADAM_B2) * (g_ * g_)
    m_hat = m_ / (1.0 - ADAM_B1 ** ADAM_STEP)
    v_hat = v_ / (1.0 - ADAM_B2 ** ADAM_STEP)
    return -ADAM_LR * (m_hat / (jnp.sqrt(v_hat) + ADAM_EPS) + ADAM_WD * w_), m_, v_


def _adamw(w, g, m, v, name):
    R, C = w.shape
    tr = _pick(R, (256, 176, 128, 64, 32, 16, 8))

    def body(w_ref, g_ref, m_ref, v_ref, d_ref, nm_ref, nv_ref):
        d_ref[...], nm_ref[...], nv_ref[...] = _adamw_math(w_ref[...], g_ref[...], m_ref[...], v_ref[...])

    spec = pl.BlockSpec((tr, C), lambda i: (i, 0))
    return pl.pallas_call(
        body, name=name, grid=(R // tr,),
        in_specs=[spec] * 4, out_specs=[spec] * 3,
        out_shape=[jax.ShapeDtypeStruct((R, C), F32)] * 3,
        compiler_params=_params(("parallel",)),
    )(w, g, m, v)


def _sum_adamw(parts, w, m, v, name, transposed=False):
    _, R, C = parts.shape
    tr = _pick(R, (256, 176, 128, 64, 32, 16, 8))

    def body(p_ref, w_ref, m_ref, v_ref, g_ref, d_ref, nm_ref, nv_ref):
        g_ = p_ref[0].astype(F32)
        for d in range(1, N_DEV):
            g_ = g_ + p_ref[d].astype(F32)
        if transposed:
            g_ = jnp.transpose(g_)
        g_ref[...] = g_
        d_ref[...], nm_ref[...], nv_ref[...] = _adamw_math(w_ref[...], g_, m_ref[...], v_ref[...])

    spec = pl.BlockSpec((C, tr), lambda i: (0, i)) if transposed else pl.BlockSpec((tr, C), lambda i: (i, 0))
    return pl.pallas_call(
        body, name=name, grid=(R // tr,),
        in_specs=[pl.BlockSpec((N_DEV, tr, C), lambda i: (0, i, 0)), spec, spec, spec], out_specs=[spec] * 4,
        out_shape=[jax.ShapeDtypeStruct(w.shape, F32)] * 4,
        compiler_params=_params(("parallel",)),
    )(parts, w, m, v)


def _sum_devices(parts, name):
    _, R, C = parts.shape
    tr = _pick(R, (512, 256, 176, 128, 64, 32, 16, 8))

    def body(p_ref, o_ref):
        acc = p_ref[0].astype(F32)
        for d in range(1, N_DEV):
            acc = acc + p_ref[d].astype(F32)
        o_ref[...] = acc

    return pl.pallas_call(
        body, name=name, grid=(R // tr,),
        in_specs=[pl.BlockSpec((N_DEV, tr, C), lambda i: (0, i, 0))],
        out_specs=pl.BlockSpec((tr, C), lambda i: (i, 0)),
        out_shape=jax.ShapeDtypeStruct((R, C), F32),
        compiler_params=_params(("parallel",)),
    )(parts)


def _my_place():
    return lax.axis_index("x"), lax.axis_index("y"), lax.axis_index("c")


def _all_gather(blocks, name):
    n = len(blocks)

    def body(*refs):
        x_refs, out_refs = refs[:n], refs[n:2 * n]
        send_sems, recv_sems, local_sems = refs[2 * n:]
        x, y, c = _my_place()
        me, sibling = (x, y, c), (x, y, 1 - c)
        chips = [(1 - x, y), (x, 1 - y), (1 - x, 1 - y)]

        def copy(a, k, blk, to, own=False):
            slot = out_refs[a].at[4 * blk[0] + 2 * blk[1] + blk[2]]
            return pltpu.make_async_remote_copy(
                src_ref=x_refs[a] if own else slot, dst_ref=slot,
                send_sem=send_sems.at[7 * a + k], recv_sem=recv_sems.at[7 * a + k], device_id=to, device_id_type=MESH)

        mine = [pltpu.make_async_copy(x_refs[a], out_refs[a].at[4 * x + 2 * y + c], local_sems.at[a]) for a in range(n)]
        for cp in mine:
            cp.start()
        first = []
        for j, chip in enumerate(chips):
            first += [copy(a, 1 + j, me, (*chip, c), own=True) for a in range(n)]
        first += [copy(a, 0, me, sibling, own=True) for a in range(n)]
        for cp in first:
            cp.start()
        passed = []
        for j, chip in enumerate(chips):
            for a in range(n):
                copy(a, 1 + j, (*chip, c), me).wait_recv()
                passed.append(copy(a, 4 + j, (*chip, c), sibling))
                passed[-1].start()
        for a in range(n):
            copy(a, 0, sibling, me).wait_recv()
        for j, chip in enumerate(chips):
            for a in range(n):
                copy(a, 4 + j, (*chip, 1 - c), me).wait_recv()
        for cp in first + passed:
            cp.wait_send()
        for cp in mine:
            cp.wait()

    return pl.pallas_call(
        body, name=name,
        out_shape=[jax.ShapeDtypeStruct((N_DEV,) + b.shape, b.dtype) for b in blocks],
        in_specs=[pl.BlockSpec(memory_space=pl.ANY)] * n,
        out_specs=[pl.BlockSpec(memory_space=pl.ANY)] * n,
        scratch_shapes=[pltpu.SemaphoreType.DMA((7 * n,)), pltpu.SemaphoreType.DMA((7 * n,)), pltpu.SemaphoreType.DMA((n,))],
    )(*blocks)


def _all_to_all(pieces, name):
    n = len(pieces)

    def body(*refs):
        x_refs, out_refs = refs[:n], refs[n:2 * n]
        send_sems, recv_sems, local_sems = refs[2 * n:]
        x, y, c = _my_place()
        me = 4 * x + 2 * y + c
        mine = [pltpu.make_async_copy(x_refs[a].at[me], out_refs[a].at[me], local_sems.at[a]) for a in range(n)]
        for cp in mine:
            cp.start()
        copies = []
        for k in (2, 4, 6, 3, 5, 7, 1):
            px = 1 - x if k & 4 else x
            py = 1 - y if k & 2 else y
            pc = 1 - c if k & 1 else c
            peer = 4 * px + 2 * py + pc
            for a in range(n):
                copies.append(pltpu.make_async_remote_copy(
                    src_ref=x_refs[a].at[peer], dst_ref=out_refs[a].at[me],
                    send_sem=send_sems.at[7 * a + k - 1], recv_sem=recv_sems.at[7 * a + k - 1],
                    device_id=(px, py, pc), device_id_type=MESH))
        for cp in copies:
            cp.start()
        for cp in copies:
            cp.wait_recv()
        for cp in copies:
            cp.wait_send()
        for cp in mine:
            cp.wait()

    return pl.pallas_call(
        body, name=name,
        out_shape=[jax.ShapeDtypeStruct(p.shape, p.dtype) for p in pieces],
        in_specs=[pl.BlockSpec(memory_space=pl.ANY)] * n,
        out_specs=[pl.BlockSpec(memory_space=pl.ANY)] * n,
        scratch_shapes=[pltpu.SemaphoreType.DMA((7 * n,)), pltpu.SemaphoreType.DMA((7 * n,)), pltpu.SemaphoreType.DMA((n,))],
    )(*pieces)


def _peers():
    x, y, c = _my_place()
    out = []
    for k in (2, 4, 6, 3, 5, 7, 1):
        px = 1 - x if k & 4 else x
        py = 1 - y if k & 2 else y
        pc = 1 - c if k & 1 else c
        out.append((k, (px, py, pc), 4 * px + 2 * py + pc))
    return out


def _exchange_copies(x_refs, land_refs, send_sems, recv_sems, scatter):
    x, y, c = _my_place()
    me = 4 * x + 2 * y + c
    starts, arrivals = [], []
    for k, place, peer in _peers():
        for a, (x_ref, land_ref) in enumerate(zip(x_refs, land_refs)):
            sems = dict(send_sem=send_sems.at[7 * a + k - 1], recv_sem=recv_sems.at[7 * a + k - 1],
                        device_id=place, device_id_type=MESH)
            src = x_ref.at[peer] if scatter else x_ref
            starts.append(pltpu.make_async_remote_copy(src_ref=src, dst_ref=land_ref.at[me], **sems))
            arrivals.append(pltpu.make_async_remote_copy(src_ref=src, dst_ref=land_ref.at[peer], **sems))
    return starts, arrivals


def _exchange_start(arrays, scatter, name):
    n = len(arrays)
    hbm = pl.BlockSpec(memory_space=pltpu.HBM)
    sem = pl.BlockSpec(memory_space=pltpu.SEMAPHORE)
    lands = [lax.empty(a.shape if scatter else (N_DEV,) + a.shape, a.dtype) for a in arrays]

    def body(*refs):
        x_refs, land_refs = refs[:n], refs[n:2 * n]
        send_sems, recv_sems = refs[2 * n], refs[2 * n + 1]
        token = refs[-1]
        starts, _ = _exchange_copies(x_refs, land_refs, send_sems, recv_sems, scatter)
        for cp in starts:
            cp.start()
        token[...] = jnp.zeros_like(token)

    res = pl.pallas_call(
        body, name=name,
        out_shape=(pltpu.SemaphoreType.DMA((7 * n,)), pltpu.SemaphoreType.DMA((7 * n,)),
                   *[pltpu.HBM(a.shape, a.dtype) for a in arrays], *[pltpu.HBM(l.shape, l.dtype) for l in lands],
                   jax.ShapeDtypeStruct((8, 128), F32)),
        in_specs=[hbm] * (2 * n),
        out_specs=(sem, sem, *[hbm] * (2 * n), pl.BlockSpec(memory_space=pltpu.VMEM)),
        input_output_aliases={i: 2 + i for i in range(2 * n)},
        compiler_params=pltpu.CompilerParams(has_side_effects=pltpu.SideEffectType.DATAFLOW_SIDE_EFFECTING),
    )(*[pltpu.with_memory_space_constraint(a, pltpu.HBM) for a in arrays],
      *[pltpu.with_memory_space_constraint(l, pltpu.HBM) for l in lands])
    return res[0], res[1], list(res[2:2 + n]), list(res[2 + n:2 + 2 * n]), res[-1]


def _exchange_wait(handles, scatter, after, name):
    send_sems, recv_sems, arrays, lands, _ = handles
    n = len(arrays)
    hbm = pl.BlockSpec(memory_space=pltpu.HBM)
    sem = pl.BlockSpec(memory_space=pltpu.SEMAPHORE)

    def body(*refs):
        x_refs, land_refs = refs[:n], refs[n:2 * n]
        send_s, recv_s = refs[2 * n], refs[2 * n + 1]
        starts, arrivals = _exchange_copies(x_refs, land_refs, send_s, recv_s, scatter)
        for cp in arrivals:
            cp.wait_recv()
        for cp in starts:
            cp.wait_send()

    res = pl.pallas_call(
        body, name=name,
        out_shape=(*[pltpu.HBM(a.shape, a.dtype) for a in arrays], *[pltpu.HBM(l.shape, l.dtype) for l in lands]),
        in_specs=[hbm] * (2 * n) + [sem, sem, pl.BlockSpec(memory_space=pl.ANY)],
        out_specs=tuple([hbm] * (2 * n)),
        input_output_aliases={i: i for i in range(2 * n)},
        compiler_params=pltpu.CompilerParams(has_side_effects=pltpu.SideEffectType.DATAFLOW_SIDE_EFFECTING),
    )(*arrays, *lands, send_sems, recv_sems, after)
    me = 4 * lax.axis_index("x") + 2 * lax.axis_index("y") + lax.axis_index("c")
    out = []
    for src, got in zip(res[:n], res[n:]):
        zeros = (0,) * (got.ndim - 1)
        own = lax.dynamic_slice(src, (me,) + zeros, (1,) + src.shape[1:]) if scatter else src[None]
        out.append(lax.dynamic_update_slice(got, own, (me,) + zeros))
    return out


def _pad_lanes(v, at=0, width=128):
    return jnp.pad(v, ((0, 0), (at, width - at - v.shape[1])))


def _pack_weights(P):
    W = {}
    w = P["w_in"]
    W["wp"] = jnp.concatenate([w[:, :2048], w[:, 2440:2696], w[:, 2056:2440], w[:, 2696:2760], w[:, 2048:2056],
                               jnp.zeros((D_MODEL, N_IN_PACKED - N_IN), w.dtype)], axis=1).astype(BF16)
    W["conv_w"] = P["gdn_conv_w"].astype(F32)
    W["alog_p"] = _pad_lanes(P["gdn_a_log"], 64)
    W["dt_p"] = _pad_lanes(P["gdn_dt_bias"], 64)
    W["gnw"] = P["gdn_norm_w"]
    W["qnw"] = P["mla_q_norm_w"]
    W["kvnw"] = P["mla_kv_norm_w"]
    uq = P["mla_w_uq"].reshape(Q_LORA, HEADS, HEAD_DIM + ROPE)
    W["wuq"] = jnp.pad(uq, ((0, 0), (0, 0), (0, 256 - HEAD_DIM - ROPE))).reshape(Q_LORA, HEADS * 256).astype(BF16)
    ukv = P["mla_w_ukv"].reshape(KV_LORA, HEADS, 2, HEAD_DIM)
    W["wukv"] = ukv.transpose(0, 2, 1, 3).reshape(KV_LORA, 2 * HEADS * HEAD_DIM).astype(BF16)
    W["qn_w"] = P["qkn_q_nope"]
    W["qr_w"] = _pad_lanes(P["qkn_q_rope"])
    W["kn_w"] = P["qkn_k_nope"]
    W["kr_w"] = _pad_lanes(P["qkn_k_rope"])
    W["onw"] = P["mla_out_norm_w"]
    W["wout"] = P["w_out"].astype(BF16)
    return W


def _unpack_grads(G):
    g = G["wp"]
    uq = G["wuq"].reshape(Q_LORA, HEADS, 256)[:, :, :HEAD_DIM + ROPE].reshape(Q_LORA, HEADS * (HEAD_DIM + ROPE))
    ukv = G["wukv"].reshape(KV_LORA, 2, HEADS, HEAD_DIM).transpose(0, 2, 1, 3).reshape(KV_LORA, 2 * HEADS * HEAD_DIM)
    return {
        "w_in": jnp.concatenate([g[:, :2048], g[:, 2752:2760], g[:, 2304:2688], g[:, 2048:2304], g[:, 2688:2752]], axis=1),
        "gdn_conv_w": G["conv_w"], "gdn_a_log": G["alog_p"][:, 64:68], "gdn_dt_bias": G["dt_p"][:, 64:68],
        "gdn_norm_w": G["gnw"], "mla_q_norm_w": G["qnw"], "mla_w_uq": uq, "mla_kv_norm_w": G["kvnw"], "mla_w_ukv": ukv,
        "qkn_q_nope": G["qn_w"], "qkn_q_rope": G["qr_w"][:, :ROPE], "qkn_k_nope": G["kn_w"], "qkn_k_rope": G["kr_w"][:, :ROPE],
        "mla_out_norm_w": G["onw"], "w_out": G["wout"],
    }


def _rope_tables(positions):
    half = ROPE // 2
    inv_freq = ROPE_BASE ** (-jnp.arange(half, dtype=F32) / half)
    ang = positions.astype(F32)[:, None] * inv_freq
    cos, sin = jnp.cos(ang), jnp.sin(ang)
    zeros = jnp.zeros((positions.shape[0], 128 - ROPE), F32)
    return jnp.concatenate([cos, cos, zeros], axis=1), jnp.concatenate([-sin, sin, zeros], axis=1)


def _mod_fn(x, scale, shift):
    return (_modulate(x, scale, shift),)


def _ffn_forward(x, scale, shift, gate_w, w8, wo4, name, target=None):
    S = x.shape[0]
    tm = _pick(S, (512, 256, 128))
    n = S // tm
    with_loss = target is not None

    def body(x_ref, sc_ref, sh_ref, g_ref, wg_ref, wu_ref, wo_ref, *rest):
        t_ref = rest[0] if with_loss else None
        h_ref, bg_ref, bu_ref, ht_ref = rest[with_loss:with_loss + 4]
        tail = rest[with_loss + 4:]
        h_scr, acc_ref = tail[-2:]
        i, p = pl.program_id(0), pl.program_id(1)

        @pl.when(p == 0)
        def _():
            h_new = _modulate(x_ref[...], sc_ref[...], sh_ref[...]).astype(BF16)
            h_scr[...] = h_new
            h_ref[...] = h_new
            acc_ref[...] = jnp.zeros_like(acc_ref)

        h = h_scr[...]
        gate = _dot_raw(h, wg_ref[...], "nn")
        up = _dot_raw(h, wu_ref[...], "nn")
        sg = _sigmoid(gate)
        act = gate * sg
        hid = act * up
        bg_ref[...] = (up * (sg * (1.0 + gate * (1.0 - sg)))).astype(BF16)
        bu_ref[...] = act.astype(BF16)
        ht_ref[...] = jnp.transpose(hid).astype(BF16)
        acc_ref[...] += _dot_raw(hid, wo_ref[...], "nn")

        if with_loss:
            dx_ref, df_ref, dg_ref, l_ref = tail[:4]

            @pl.when((p == 0) & (i == 0))
            def _():
                dg_ref[...] = jnp.zeros_like(dg_ref)
                l_ref[...] = jnp.zeros_like(l_ref)

            @pl.when(p == HID_PIECES - 1)
            def _():
                step = min(EPILOGUE_ROWS, tm)
                for r in range(tm // step):
                    rows = slice(step * r, step * (r + 1))
                    f = acc_ref[rows, :]
                    diff = x_ref[rows, :] + 0.5 * g_ref[...] * f - t_ref[rows, :]
                    dx = diff * (1.0 / D_MODEL)
                    dx_ref[rows, :] = dx
                    df_ref[rows, :] = (0.5 * g_ref[...] * dx).astype(df_ref.dtype)
                    dg_ref[...] += jnp.sum(0.5 * f * dx, axis=0, keepdims=True)
                    l_ref[...] += jnp.sum(diff * diff, axis=0, keepdims=True)

            @pl.when((p == HID_PIECES - 1) & (i == n - 1))
            def _():
                l_ref[...] = jnp.full(l_ref.shape, (0.5 / D_MODEL) * jnp.sum(l_ref[...]), F32)
        else:
            f_ref, xo_ref = tail[:2]

            @pl.when(p == HID_PIECES - 1)
            def _():
                f = acc_ref[...]
                f_ref[...] = f.astype(f_ref.dtype)
                xo_ref[...] = x_ref[...] + 0.5 * g_ref[...] * f

    row = pl.BlockSpec((tm, D_MODEL), lambda i, p: (i, 0))
    par = pl.BlockSpec((1, D_MODEL), lambda i, p: (0, 0))
    piece = pl.BlockSpec((None, tm, FFN_PIECE), lambda i, p: (p, i, 0))
    row_f32, row_bf16 = jax.ShapeDtypeStruct((S, D_MODEL), F32), jax.ShapeDtypeStruct((S, D_MODEL), BF16)
    par_f32 = jax.ShapeDtypeStruct((1, D_MODEL), F32)
    piece_shape = jax.ShapeDtypeStruct((HID_PIECES, S, FFN_PIECE), BF16)
    return pl.pallas_call(
        body, name=name, grid=(n, HID_PIECES),
        in_specs=[row, par, par, par,
                  pl.BlockSpec((None, D_MODEL, FFN_PIECE), lambda i, p: (p, 0, 0)),
                  pl.BlockSpec((None, D_MODEL, FFN_PIECE), lambda i, p: (p + HID_PIECES, 0, 0)),
                  pl.BlockSpec((None, FFN_PIECE, D_MODEL), lambda i, p: (p, 0, 0))] + [row] * with_loss,
        out_specs=[row, piece, piece, pl.BlockSpec((None, FFN_PIECE, tm), lambda i, p: (p, 0, i))]
        + ([row, row, par, par] if with_loss else [row, row]),
        out_shape=[row_bf16, piece_shape, piece_shape, jax.ShapeDtypeStruct((HID_PIECES, FFN_PIECE, S), BF16)]
        + ([row_f32, row_bf16, par_f32, par_f32] if with_loss else [row_bf16, row_f32]),
        scratch_shapes=[pltpu.VMEM((tm, D_MODEL), BF16), pltpu.VMEM((tm, D_MODEL), F32)],
        compiler_params=_params(("arbitrary", "arbitrary")),
    )(x, scale, shift, gate_w, w8, w8, wo4, *([target] if with_loss else []))


def _ffn_fwd(x, scale, shift, gate_w, w8, wo4, tag, target=None):
    res = _ffn_forward(x, scale, shift, gate_w, w8, wo4, tag + "_fwd", target)
    h, by_gate, by_up, hid_t = res[:4]
    if target is not None:
        dx_out, df, d_gate_w, loss_row = res[4:]
        return (dx_out, loss_row), (h, by_gate, by_up, hid_t, None, df, d_gate_w)
    f, x_out = res[4:]
    return x_out, (h, by_gate, by_up, hid_t, f, None, None)


def _ffn_bwd(d_out, x, scale, shift, gate_w, w8, wo4, saved, tag, grad_ready, below=None):
    h, gate, up, hid_t, f, df, d_gate_w = saved
    S = x.shape[0]
    tm = _pick(S, (512, 256, 128))
    tk = _pick(S, (512, 256, 128))
    n = S // tm
    if df is None:
        (df,), (d_gate_w,) = _rowwise_bwd(lambda f_, g_: (0.5 * g_ * f_,), [(f, tm, D_MODEL, 0)], [], [gate_w],
                                          [(d_out, tm, D_MODEL, 0)], n, tag + "_dres", row_dtypes=(BF16,))
    tb = _pick(S, MATMUL_ROWS)
    piece = pl.BlockSpec((None, tb, FFN_PIECE), lambda i, j, k: (j, i, 0))
    d_gate, d_up = _mmg(df, wo4, "nt", name=tag + "_ddown", grid=(S // tb, HID_PIECES, 1),
                        a_spec=pl.BlockSpec((tb, D_MODEL), lambda i, j, k: (i, 0)),
                        b_spec=pl.BlockSpec((None, FFN_PIECE, D_MODEL), lambda i, j, k: (j, 0, 0)),
                        out_spec=piece, out_shapes=[jax.ShapeDtypeStruct((HID_PIECES, S, FFN_PIECE), BF16)] * 2,
                        acc_shape=(tb, FFN_PIECE), extras=[gate, up], extra_specs=[piece, piece], epi=_swiglu_bwd)
    tk = _pick(S, MATMUL_ROWS)
    g_wo4 = _mmg(hid_t, df, "nn", name=tag + "_gwo", grid=(HID_PIECES, 1, S // tk),
                 a_spec=pl.BlockSpec((None, FFN_PIECE, tk), lambda i, j, k: (i, 0, k)),
                 b_spec=pl.BlockSpec((tk, D_MODEL), lambda i, j, k: (k, j)),
                 out_spec=pl.BlockSpec((None, FFN_PIECE, D_MODEL), lambda i, j, k: (i, 0, j)),
                 out_shapes=[jax.ShapeDtypeStruct((HID_PIECES, FFN_PIECE, D_MODEL), BF16)], acc_shape=(FFN_PIECE, D_MODEL))
    g_w8 = _ffn_gw8(h, d_gate, d_up, tag + "_gw8", after=grad_ready("wo4", g_wo4))
    res = _ffn_dh(d_gate, d_up, w8, x, d_out, scale, shift, tag + "_dh", after=grad_ready("w8", g_w8), below=below)
    return (res[0], res[1], res[2], d_gate_w) + tuple(res[3:])


def _dproj_dmod(d_proj, wp, x, d_out, scale, shift, name, below=None):
    S, K = d_proj.shape
    tm = _pick(S, MATMUL_ROWS)
    tk = _pick(K, (1408, 512, 256, 128))
    nk = K // tk
    n_below = 0 if below is None else 2

    def body(dp_ref, w_ref, x_ref, do_ref, sc_ref, sh_ref, *rest):
        below_in = rest[:n_below]
        dx_ref, dsc_ref, dsh_ref = rest[n_below:n_below + 3]
        below_out, acc_ref = rest[n_below + 3:n_below + 3 + n_below], rest[-1]
        i, k = pl.program_id(0), pl.program_id(1)

        @pl.when(k == 0)
        def _():
            acc_ref[...] = jnp.zeros_like(acc_ref)

        acc_ref[...] += _dot_raw(dp_ref[...], w_ref[...], "nt")

        @pl.when((k == 0) & (i == 0))
        def _():
            for r in (dsc_ref, dsh_ref) + tuple(below_out[1:]):
                r[...] = jnp.zeros_like(r)

        @pl.when(k == nk - 1)
        def _():
            _dmod_epilogue(acc_ref, x_ref, do_ref, sc_ref, sh_ref, dx_ref, dsc_ref, dsh_ref, below, below_in, below_out)

    row = pl.BlockSpec((tm, D_MODEL), lambda i, k: (i, 0))
    par = pl.BlockSpec((1, D_MODEL), lambda i, k: (0, 0))
    row_shape, par_shape = jax.ShapeDtypeStruct((S, D_MODEL), F32), jax.ShapeDtypeStruct((1, D_MODEL), F32)
    return pl.pallas_call(
        body, name=name, grid=(S // tm, nk),
        in_specs=[pl.BlockSpec((tm, tk), lambda i, k: (i, k)), pl.BlockSpec((D_MODEL, tk), lambda i, k: (0, k)),
                  row, row, par, par] + [row, par][:n_below],
        out_specs=[row, par, par] + [row, par][:n_below],
        out_shape=[row_shape, par_shape, par_shape] + [jax.ShapeDtypeStruct((S, D_MODEL), BF16), par_shape][:n_below],
        scratch_shapes=[pltpu.VMEM((tm, D_MODEL), F32)],
        compiler_params=_params(("arbitrary", "arbitrary")),
    )(d_proj, wp, x, d_out, scale, shift, *(below[:2] if below is not None else ()))


def _mixer_fwd(x1, scale, shift, gate_w, cos_p, sin_p, W):
    S = x1.shape[0]
    tm = _pick(S, (512, 256, 128))
    tv = _pick(S, (256, 128))
    ta = _pick(S, (512, 256, 128))
    nc = S // CHUNK
    (h2,) = _rowwise(_mod_fn, [(x1, tm, D_MODEL, 0)], [scale, shift], [(tm, D_MODEL, BF16)], S // tm, "mix_mod")
    proj = _mm(h2, W["wp"], "nn", name="mix_proj")
    qkvc = _conv_fwd(proj, W["conv_w"], tv, "gdn_conv")
    kab = (proj, tv, 128, 21)
    q_a, k_a, v_a, gb = _rowwise(_gdn_pre_fn, [(qkvc, tv, 1536, 0), kab], [W["alog_p"], W["dt_p"]],
                                 [(tv, 512, F32)] * 3 + [(tv, 128, F32)], S // tv, "gdn_pre")
    ti = _pick(S, INTRA_ROWS)
    intra = _rowwise(_gdn_intra_fn, [(q_a, ti, 512, 0), (k_a, ti, 512, 0), (v_a, ti, 512, 0), (gb, ti, 128, 0)],
                     [], [(ti, 512, F32)] * 4 + [(ti, CHUNK, F32)] * 4 + [(ti // 8, 512, F32)] + [(ti, CHUNK, F32)] * 4,
                     S // ti, "gdn_intra")
    u, wk, qd, kd, qks, gl, invs = intra[0], intra[1], intra[2], intra[3], tuple(intra[4:8]), intra[8], tuple(intra[9:])
    o_a, s_prev = _gdn_scan_fwd(u, wk, qd, kd, qks, gl, "gdn_scan")
    mla_params = [W["qnw"], W["kvnw"], W["wuq"], W["wukv"], W["qn_w"], W["qr_w"], W["kn_w"], W["kr_w"]]
    def mla_pre_with_vt(*a):
        q_, k_, v_ = _mla_pre_fn(*a)
        return q_, k_, v_, jnp.transpose(v_)

    q_b, k_b, v_b, vt_b = _rowwise(mla_pre_with_vt,
                                   [(proj, tv, 256, 8), (proj, tv, 384, 6), kab, (cos_p, tv, 128, 0), (sin_p, tv, 128, 0)],
                                   mla_params, [(tv, 1024, BF16), (tv, 1024, BF16), (tv, 512, BF16), (512, tv, BF16, "across")],
                                   S // tv, "mla_pre")
    o_b, lse = _attn_fwd(q_b, k_b, vt_b, ta, "mla_attn")
    (mixed,) = _rowwise(_mix_post_fn, [(o_a, tv, 512, 0), (proj, tv, 512, 3), (o_b, tv, 512, 0)], [W["gnw"], W["onw"]],
                        [(tv, D_MODEL, BF16)], S // tv, "mix_post")
    y, x2 = _mm(mixed, W["wout"], "nn", name="mix_out", out_dtypes=(BF16, F32), extras=[x1], extra_params=[gate_w],
                epi=lambda acc, x_, g_: (acc, x_ + g_ * acc))
    saved = (h2, proj, qkvc, q_a, k_a, v_a, gb, u, wk, qd, kd, qks, gl, invs, s_prev, o_a, q_b, k_b, v_b, o_b, lse, mixed, y)
    return x2, saved


def _mixer_bwd(d_out, dy, x1, scale, shift, cos_p, sin_p, W, saved, below):
    (h2, proj, qkvc, q_a, k_a, v_a, gb, u, wk, qd, kd, qks, gl, invs, s_prev, o_a, q_b, k_b, v_b, o_b, lse, mixed, y) = saved
    S = x1.shape[0]
    tm = _pick(S, (512, 256, 128))
    tv = _pick(S, (256, 128))
    ta = _pick(S, (512, 256, 128))
    nc = S // CHUNK
    G = {}
    d_mixed = _mm(dy, W["wout"], "nt", name="mix_dout")
    G["wout"] = _mm(mixed, dy, "tn", name="mix_gwout")
    (do_a, dz, do_b), (G["gnw"], G["onw"]) = _rowwise_bwd(
        _mix_post_fn, [(o_a, tv, 512, 0), (proj, tv, 512, 3), (o_b, tv, 512, 0)], [], [W["gnw"], W["onw"]],
        [(d_mixed, tv, D_MODEL, 0)], S // tv, "mix_dpost", row_dtypes=(F32, BF16, F32))
    stats = _attn_stats(o_b, lse, do_b, ta, "mla_stats")
    dq_b, dk_b, dv_b = _attn_bwd(q_b, k_b, v_b, do_b, stats, ta, "mla_dattn")
    kab = (proj, tv, 128, 21)
    mla_params = [W["qnw"], W["kvnw"], W["wuq"], W["wukv"], W["qn_w"], W["qr_w"], W["kn_w"], W["kr_w"]]
    (d_ckv, d_cq, d_kab), mla_grads = _rowwise_bwd(
        _mla_pre_fn, [(proj, tv, 256, 8), (proj, tv, 384, 6), kab], [(cos_p, tv, 128, 0), (sin_p, tv, 128, 0)], mla_params,
        [(dq_b, tv, 1024, 0), (dk_b, tv, 1024, 0), (dv_b, tv, 512, 0)], S // tv, "mla_dpre", row_dtypes=(BF16, BF16, F32))
    for key, g in zip(("qnw", "kvnw", "wuq", "wukv", "qn_w", "qr_w", "kn_w", "kr_w"), mla_grads):
        G[key] = g
    scan_grads = _gdn_scan_bwd(u, wk, qd, kd, qks, gl, s_prev, do_a, "gdn_dscan")
    ti = _pick(S, INTRA_ROWS)
    intra_douts = [(scan_grads[i], ti, 512, 0) for i in range(4)] + [(scan_grads[4 + i], ti, CHUNK, 0) for i in range(4)]
    intra_douts.append((scan_grads[8], ti // 8, 512, 0))
    (dq_a, dk_a, dv_a, d_gb), _ = _rowwise_bwd(
        _gdn_intra_fn, [(q_a, ti, 512, 0), (k_a, ti, 512, 0), (v_a, ti, 512, 0), (gb, ti, 128, 0)],
        [(x_, ti, CHUNK, 0) for x_ in invs], [], intra_douts, S // ti, "gdn_dintra")
    (d_qkvc, d_kab), (G["alog_p"], G["dt_p"]) = _rowwise_bwd(
        _gdn_pre_fn, [(qkvc, tv, 1536, 0), kab], [], [W["alog_p"], W["dt_p"]],
        [(dq_a, tv, 512, 0), (dk_a, tv, 512, 0), (dv_a, tv, 512, 0), (d_gb, tv, 128, 0)], S // tv, "gdn_dpre",
        adds=[(1, d_kab)], row_dtypes=(F32, BF16))
    d_qkv, g_conv = _conv_bwd(proj, d_qkvc, W["conv_w"], tv, "gdn_dconv")
    G["conv_w"] = g_conv[:4]
    d_proj = jnp.concatenate([d_qkv, dz, d_ckv, d_cq, d_kab], axis=1)
    G["wp"] = _mm(h2, d_proj, "tn", name="mix_gwp")
    dx1, G["s2"], G["sh2"], d_below, dg_below = _dproj_dmod(d_proj, W["wp"], x1, d_out, scale, shift, "mix_dproj", below=below)
    return dx1, d_below, dg_below, G


def _local_step(x, target, mod, cos_p, sin_p, W1, mixer_weights, ffn2_weights, ffn_grad_ready, mixer_grads_ready):
    sh1, s1, g1, sh2, s2, g2, sh3, s3, g3 = [mod[:, D_MODEL * i:D_MODEL * (i + 1)] for i in range(N_MOD)]
    x1, saved1 = _ffn_fwd(x, s1, sh1, g1, W1["f1_w8"], W1["f1_wo4"], "ffn1")
    W = mixer_weights(x1)
    x2, saved2 = _mixer_fwd(x1, s2, sh2, g2, cos_p, sin_p, W)
    W.update(ffn2_weights(x2))
    (dx3, loss_row), saved3 = _ffn_fwd(x2, s3, sh3, g3, W["f2_w8"], W["f2_wo4"], "ffn2", target=target)
    dx2, d_s3, d_sh3, d_g3, dy, d_g2 = _ffn_bwd(dx3, x2, s3, sh3, g3, W["f2_w8"], W["f2_wo4"], saved3, "ffn2",
                                                ffn_grad_ready("f2"), below=(saved2[-1], g2, 1.0))
    dx1, df1, d_g1, G = _mixer_bwd(dx2, dy, x1, s2, sh2, cos_p, sin_p, W, saved2, below=(saved1[4], g1, 0.5))
    d_sh2, d_s2 = G.pop("sh2"), G.pop("s2")
    saved1 = saved1[:5] + (df1, d_g1 + mixer_grads_ready(G))
    dx, d_s1, d_sh1, d_g1 = _ffn_bwd(dx1, x, s1, sh1, g1, W1["f1_w8"], W1["f1_wo4"], saved1, "ffn1", ffn_grad_ready("f1"))
    d_mod = jnp.concatenate([d_sh1, d_s1, d_g1, d_sh2, d_s2, d_g2, d_sh3, d_s3, d_g3], axis=1)
    return loss_row, dx, d_mod


WEIGHT_NAMES = ("w_ada", "b_ada", "ffn1_w_in", "ffn1_w_out", "w_in", "gdn_conv_w", "gdn_a_log", "gdn_dt_bias", "gdn_norm_w",
                "mla_q_norm_w", "mla_w_uq", "mla_kv_norm_w", "mla_w_ukv", "qkn_q_nope", "qkn_q_rope", "qkn_k_nope",
                "qkn_k_rope", "mla_out_norm_w", "w_out", "ffn2_w_in", "ffn2_w_out")
FFN_SHARDED = ("ffn1_w_in", "ffn1_w_out", "ffn2_w_in", "ffn2_w_out")
TRANSPOSED_ENTRY = ("ffn1_w_in", "ffn2_w_in", "w_in", "mla_w_uq")
SHEETED = (("w_in", "col"), ("gdn_conv_w", "col"), ("mla_w_uq", "col"), ("mla_w_ukv", "col"), ("w_out", "row"))
MOD_ROWS = N_MOD * D_MODEL // 128
SMALL = {"gdn_a_log": (MOD_ROWS, 1, 64, 4), "gdn_dt_bias": (MOD_ROWS + 1, 1, 64, 4), "gdn_norm_w": (MOD_ROWS + 2, 1, 0, 128),
         "mla_q_norm_w": (MOD_ROWS + 3, 3, 0, 384), "mla_kv_norm_w": (MOD_ROWS + 6, 2, 0, 256),
         "qkn_q_nope": (MOD_ROWS + 8, 1, 0, 128), "qkn_q_rope": (MOD_ROWS + 9, 1, 0, 64), "qkn_k_nope": (MOD_ROWS + 10, 1, 0, 128),
         "qkn_k_rope": (MOD_ROWS + 11, 1, 0, 64), "mla_out_norm_w": (MOD_ROWS + 12, 1, 0, 128)}
LOSS_ROW = MOD_ROWS + 13
CONV_ROW, CONV_ROWS = 88, 4 * 1536 // 128
SHEET_ROWS = CONV_ROW + CONV_ROWS


def _to_sheet(flat, dtype, sublanes):
    n = flat.shape[-1]
    unit = sublanes * 128
    pad = (-n) % unit
    flat = jnp.pad(flat.astype(dtype), [(0, 0)] * (flat.ndim - 1) + [(0, pad)])
    return flat.reshape(flat.shape[:-1] + ((n + pad) // 128, 128))


def _small_sheet(b_like, small):
    sheet = jnp.zeros((SHEET_ROWS, 128), F32).at[:MOD_ROWS].set(b_like.reshape(MOD_ROWS, 128))
    for name, (row, rows, lane, n) in SMALL.items():
        v = small[name].reshape(1, n)
        if rows == 1:
            sheet = sheet.at[row, lane:lane + n].set(v[0])
        else:
            sheet = sheet.at[row:row + rows].set(v.reshape(rows, 128))
    return sheet


def _from_small_sheet(sheet):
    out = {"b_ada": sheet[:MOD_ROWS].reshape(1, N_MOD * D_MODEL)}
    for name, (row, rows, lane, n) in SMALL.items():
        out[name] = sheet[row, lane:lane + n].reshape(1, n) if rows == 1 else sheet[row:row + rows].reshape(1, n)
    return out


def kernel(x, c, positions, w_ada, b_ada, ffn1_w_in, ffn1_w_out, w_in, gdn_conv_w, gdn_a_log, gdn_dt_bias, gdn_norm_w, mla_q_norm_w, mla_w_uq, mla_kv_norm_w, mla_w_ukv, qkn_q_nope, qkn_q_rope, qkn_k_nope, qkn_k_rope, mla_out_norm_w, w_out, ffn2_w_in, ffn2_w_out, loss_target, m_w_ada, m_b_ada, m_ffn1_w_in, m_ffn1_w_out, m_w_in, m_gdn_conv_w, m_gdn_a_log, m_gdn_dt_bias, m_gdn_norm_w, m_mla_q_norm_w, m_mla_w_uq, m_mla_kv_norm_w, m_mla_w_ukv, m_qkn_q_nope, m_qkn_q_rope, m_qkn_k_nope, m_qkn_k_rope, m_mla_out_norm_w, m_w_out, m_ffn2_w_in, m_ffn2_w_out, v_w_ada, v_b_ada, v_ffn1_w_in, v_ffn1_w_out, v_w_in, v_gdn_conv_w, v_gdn_a_log, v_gdn_dt_bias, v_gdn_norm_w, v_mla_q_norm_w, v_mla_w_uq, v_mla_kv_norm_w, v_mla_w_ukv, v_qkn_q_nope, v_qkn_q_rope, v_qkn_k_nope, v_qkn_k_rope, v_mla_out_norm_w, v_w_out, v_ffn2_w_in, v_ffn2_w_out):
    args = locals()
    w = {n: args[n] for n in WEIGHT_NAMES}
    m = {n: args["m_" + n] for n in WEIGHT_NAMES}
    v = {n: args["v_" + n] for n in WEIGHT_NAMES}
    me = 4 * lax.axis_index("x") + 2 * lax.axis_index("y") + lax.axis_index("c")
    cols = N_MOD * D_MODEL // N_DEV
    shard = {n: w[n][0] for n in FFN_SHARDED + tuple(s[0] for s in SHEETED)}

    sc = c * _sigmoid(c)
    first = _to_sheet(jnp.concatenate([sc.reshape(-1), shard["gdn_conv_w"].reshape(-1)]), F32, 8)
    (first_all,) = _all_gather([first], "gather_c")
    sc_all = first_all[:, :D_MODEL // 128].reshape(N_DEV, D_MODEL)
    n_taps = shard["gdn_conv_w"].size
    conv_all = first_all.reshape(N_DEV, -1)[:, D_MODEL:D_MODEL + n_taps].reshape(N_DEV, 4, -1)
    b_mine = lax.dynamic_slice(b_ada, (0, me * cols), (1, cols))
    mod_cols = _mm(sc_all, w_ada[0], "nn", name="ada_mod", extra_params=[b_mine], epi=lambda acc, b_: (acc + b_,))
    (mod_all,) = _all_to_all([_to_sheet(mod_cols, F32, 8)], "scatter_mod")
    mod = mod_all.reshape(N_DEV, -1)[:, :cols].reshape(1, N_MOD * D_MODEL)

    f1_shards, mod = lax.optimization_barrier(([shard["ffn1_w_in"].astype(BF16), shard["ffn1_w_out"].astype(BF16)], mod))
    f1_w8, f1_out = _all_gather(f1_shards, "gather_w1")
    travel = [s for s in SHEETED if s[0] != "gdn_conv_w"]
    tied = lax.optimization_barrier(([shard[n].astype(BF16) for n, _ in travel], f1_w8))
    f1_w8 = tied[1]
    mixer_w = _exchange_start(tied[0], False, "gather_wm_start")
    ffn2_w = _exchange_start([shard["ffn2_w_in"].astype(BF16) + mixer_w[4][0:1, 0:1].astype(BF16),
                              shard["ffn2_w_out"].astype(BF16)], False, "gather_w2_start")
    mod = mod + ffn2_w[4][0:1, 0:1]
    W1 = dict(f1_w8=f1_w8, f1_wo4=f1_out.reshape(HID_PIECES, FFN_PIECE, D_MODEL))

    def mixer_weights(after):
        got = _exchange_wait(mixer_w, False, after, "gather_wm_wait")
        P = {n: jnp.concatenate(list(g), axis=1) if kind == "col" else g.reshape(-1, g.shape[-1])
             for (n, kind), g in zip(travel, got)}
        P["gdn_conv_w"] = jnp.concatenate(list(conv_all), axis=1)
        for n in SMALL:
            P[n] = w[n]
        return _pack_weights(P)

    def ffn2_weights(after):
        f2_w8, f2_out = _exchange_wait(ffn2_w, False, after, "gather_w2_wait")
        return dict(f2_w8=f2_w8, f2_wo4=f2_out.reshape(HID_PIECES, FFN_PIECE, D_MODEL))

    pending, small_grads = {}, {}

    def ffn_grad_ready(tag):
        def ready(which, g):
            pieces = g if which == "w8" else g.reshape((N_DEV,) + shard["ffn1_w_out"].shape)
            pending[tag + which] = _exchange_start([pieces], True, "scatter_%s_%s_start" % (tag, which))
            return pending[tag + which][4]
        return ready

    def mixer_grads_ready(G):
        g_full = _unpack_grads(G)
        small_grads.update({n: g_full[n] for n in SMALL})
        small_grads["gdn_conv_w"] = g_full["gdn_conv_w"]
        pieces = []
        for n, kind in travel:
            r, cc = shard[n].shape
            g = g_full[n].astype(BF16)
            pieces.append(jnp.stack([g[:, cc * p:cc * (p + 1)] for p in range(N_DEV)]) if kind == "col"
                          else g.reshape(N_DEV, r, cc))
        pending["mixer"] = _exchange_start(pieces, True, "scatter_mx_start")
        return pending["mixer"][4][0:1, 0:1]

    cos_p, sin_p = _rope_tables(positions[0])
    loss_row, dx, d_mod = _local_step(x[0], loss_target[0], mod, cos_p, sin_p, W1, mixer_weights, ffn2_weights,
                                      ffn_grad_ready, mixer_grads_ready)

    sheet = _small_sheet(d_mod, small_grads).at[LOSS_ROW].set(loss_row[0, :128])
    sheet = sheet.at[CONV_ROW:CONV_ROW + CONV_ROWS].set(small_grads["gdn_conv_w"].reshape(CONV_ROWS, 128))
    (sheets,) = _all_gather([sheet], "gather_small")
    summed = _sum_devices(sheets, "sum_small")
    d_mod_all = sheets[:, :MOD_ROWS].reshape(N_DEV, N_MOD * D_MODEL)
    d_mod_mine = lax.dynamic_slice(d_mod_all, (0, me * cols), (N_DEV, cols))
    grads = _from_small_sheet(summed)
    grads["w_ada"] = _mm(sc_all, d_mod_mine, "tn", name="ada_gw", hi=True)
    conv_taps = shard["gdn_conv_w"].shape[1]
    grads["gdn_conv_w"] = lax.dynamic_slice(summed[CONV_ROW:CONV_ROW + CONV_ROWS].reshape(4, -1), (0, me * conv_taps),
                                            (4, conv_taps))
    loss = summed[LOSS_ROW, 0]

    delta, new_m, new_v = {}, {}, {}
    arrived = {}
    for n, key in zip(FFN_SHARDED, ("f1w8", "f1wo4", "f2w8", "f2wo4")):
        (arrived[n],) = _exchange_wait(pending[key], True, summed, "scatter_%s_wait" % key)
    arrived.update(zip([n for n, _ in travel], _exchange_wait(pending["mixer"], True, summed, "scatter_mx_wait")))
    for n, parts in arrived.items():
        if n in TRANSPOSED_ENTRY:
            res = _sum_adamw(parts, w[n][0].T, m[n][0].T, v[n][0].T, "adamw_" + n, transposed=True)
            grads[n], delta[n], new_m[n], new_v[n] = [r.T for r in res]
        else:
            grads[n], delta[n], new_m[n], new_v[n] = _sum_adamw(parts, w[n][0], m[n][0], v[n][0], "adamw_" + n)
    for n in ("w_ada", "gdn_conv_w"):
        delta[n], new_m[n], new_v[n] = _adamw(w[n][0], grads[n], m[n][0], v[n][0], "adamw_" + n)
    small_in = [_small_sheet(t["b_ada"], t) for t in (w, grads, m, v)]
    for res, out in zip(_adamw(*small_in, "adamw_small"), (delta, new_m, new_v)):
        out.update(_from_small_sheet(res))

    def shaped(d):
        return [d[n].reshape(w[n].shape) for n in WEIGHT_NAMES]

    return (loss, dx[None], *shaped(grads), *shaped(delta), *shaped(new_m), *shaped(new_v))
```

```python
import functools

import jax
import jax.numpy as jnp
import numpy as np
from jax import lax
from jax.experimental import pallas as pl
from jax.experimental.pallas import tpu as pltpu

F32 = jnp.float32
BF16 = jnp.bfloat16

D_MODEL = 1024
D_FF = 2816
N_MOD = 9
HEADS = 4
HEAD_DIM = 128
CHUNK = 64
EPS = 1e-6
ROPE = 64
Q_LORA = 384
KV_LORA = 256
N_IN = 2760
N_IN_PACKED = 2816
ROPE_BASE = 10000.0
LOG2_E = 1.4426950408889634
N_DEV = 8

ADAM_LR = 0.001
ADAM_B1 = 0.9
ADAM_B2 = 0.999
ADAM_EPS = 1e-08
ADAM_WD = 0.01
ADAM_STEP = 10

VMEM_LIMIT_BYTES = 56 * 1024 * 1024
MATMUL_ROWS = (1024, 512, 256, 128)
MESH = pl.DeviceIdType.MESH


def _params(sem=None):
    return pltpu.CompilerParams(dimension_semantics=sem, vmem_limit_bytes=VMEM_LIMIT_BYTES)


def _pick(dim, prefs):
    for p in prefs:
        if dim % p == 0:
            return p
    return dim


_DIMS = {"nn": (((1,), (0,)), ((), ())), "nt": (((1,), (1,)), ((), ())), "tn": (((0,), (0,)), ((), ()))}


def _dot_raw(a, b, mode):
    return lax.dot_general(a.astype(BF16), b.astype(BF16), _DIMS[mode], preferred_element_type=F32)


def _dot_hi(a, b, mode="nn"):
    return lax.dot_general(a, b, _DIMS[mode], precision=lax.Precision.HIGHEST, preferred_element_type=F32)


@functools.partial(jax.custom_vjp, nondiff_argnums=(2,))
def _bdot(a, b, mode):
    return _dot_raw(a, b, mode)


def _bdot_fwd(a, b, mode):
    return _dot_raw(a, b, mode), (a, b)


def _bdot_bwd(mode, res, g):
    a, b = res
    if mode == "nn":
        return _dot_raw(g, b, "nt"), _dot_raw(a, g, "tn")
    if mode == "nt":
        return _dot_raw(g, b, "nn"), _dot_raw(g, a, "tn")
    return _dot_raw(b, g, "nt"), _dot_raw(a, g, "nn")


_bdot.defvjp(_bdot_fwd, _bdot_bwd)


def _mm(a, b, mode, *, name, out_dtypes=(F32,), epi=None, extras=(), extra_params=(), hi=False,
        tm=None, tn=None, tk=None):
    if mode == "nn":
        (M, K), (_, N) = a.shape, b.shape
    elif mode == "nt":
        (M, K), (N, _) = a.shape, b.shape
    else:
        (K, M), (_, N) = a.shape, b.shape
    tm = tm or _pick(M, (512, 1408, 256, 128) if mode == "tn" else MATMUL_ROWS + (384, 352))
    tn = tn or _pick(N, (1024, 1408, 768, 512, 384, 256, 128))
    tk = tk or _pick(K, (1024, 1408, 512, 384, 256, 128))
    a_spec = {"nn": pl.BlockSpec((tm, tk), lambda i, j, k: (i, k)), "nt": pl.BlockSpec((tm, tk), lambda i, j, k: (i, k)),
              "tn": pl.BlockSpec((tk, tm), lambda i, j, k: (k, i))}[mode]
    b_spec = {"nn": pl.BlockSpec((tk, tn), lambda i, j, k: (k, j)), "nt": pl.BlockSpec((tn, tk), lambda i, j, k: (j, k)),
              "tn": pl.BlockSpec((tk, tn), lambda i, j, k: (k, j))}[mode]
    mn_spec = pl.BlockSpec((tm, tn), lambda i, j, k: (i, j))
    return _mmg(a, b, mode, name=name, grid=(M // tm, N // tn, K // tk), a_spec=a_spec, b_spec=b_spec, out_spec=mn_spec,
                out_shapes=[jax.ShapeDtypeStruct((M, N), dt) for dt in out_dtypes], acc_shape=(tm, tn), epi=epi,
                extras=list(extras) + list(extra_params),
                extra_specs=[mn_spec] * len(extras) + [pl.BlockSpec((1, tn), lambda i, j, k: (0, j))] * len(extra_params),
                hi=hi)


def _mmg(a, b, mode, *, name, grid, a_spec, b_spec, out_spec, out_shapes, acc_shape, epi=None, extras=(),
         extra_specs=(), hi=False):
    nk = grid[2]
    n_e, n_o = len(extras), len(out_shapes)

    def body(*refs):
        a_ref, b_ref = refs[:2]
        e_refs = refs[2:2 + n_e]
        o_refs = refs[2 + n_e:2 + n_e + n_o]
        acc_ref = refs[-1]
        k = pl.program_id(2)

        @pl.when(k == 0)
        def _():
            acc_ref[...] = jnp.zeros_like(acc_ref)

        if hi:
            acc_ref[...] += _dot_hi(a_ref[...].astype(F32), b_ref[...].astype(F32), mode)
        else:
            acc_ref[...] += _dot_raw(a_ref[...], b_ref[...], mode)

        @pl.when(k == nk - 1)
        def _():
            acc = acc_ref[...]
            outs = (acc,) if epi is None else epi(acc, *[e[...].astype(F32) for e in e_refs])
            for o_ref, o in zip(o_refs, outs):
                o_ref[...] = o.astype(o_ref.dtype)

    outs = pl.pallas_call(
        body, name=name, grid=grid,
        in_specs=[a_spec, b_spec] + list(extra_specs),
        out_specs=[out_spec] * n_o,
        out_shape=list(out_shapes),
        scratch_shapes=[pltpu.VMEM(acc_shape, F32)],
        compiler_params=_params(("parallel", "parallel", "arbitrary")),
    )(a, b, *extras)
    return outs if n_o > 1 else outs[0]


def _row_spec(th, cw, ci):
    return pl.BlockSpec((th, cw), lambda i: (i, ci))


def _full_spec(shape):
    return pl.BlockSpec(shape, lambda i: (0,) * len(shape))


def _rowwise(fn, rows, params, outs, n_steps, name):
    n_r, n_p, n_o = len(rows), len(params), len(outs)

    def body(*refs):
        vals = [r[...].astype(F32) for r in refs[:n_r + n_p]]
        res = fn(*vals)
        for o_ref, o in zip(refs[n_r + n_p:], res):
            o_ref[...] = o.astype(o_ref.dtype)

    across = [len(o) == 4 for o in outs]
    res = pl.pallas_call(
        body, name=name, grid=(n_steps,),
        in_specs=[_row_spec(th, cw, ci) for (_, th, cw, ci) in rows] + [_full_spec(p.shape) for p in params],
        out_specs=[pl.BlockSpec((o[0], o[1]), lambda i: (0, i)) if ac else _row_spec(o[0], o[1], 0)
                   for o, ac in zip(outs, across)],
        out_shape=[jax.ShapeDtypeStruct((o[0], n_steps * o[1]) if ac else (n_steps * o[0], o[1]), o[2])
                   for o, ac in zip(outs, across)],
        compiler_params=_params(("parallel",)),
    )(*[r[0] for r in rows], *params)
    return res


def _rowwise_bwd(fn, rows, aux, params, douts, n_steps, name, row_dtypes=None, adds=()):
    n_r, n_a, n_p, n_d, n_add = len(rows), len(aux), len(params), len(douts), len(adds)
    row_dtypes = row_dtypes or (F32,) * n_r

    def body(*refs):
        it = iter(refs)
        r_vals = [next(it)[...].astype(F32) for _ in range(n_r)]
        a_vals = [next(it)[...].astype(F32) for _ in range(n_a)]
        p_vals = [next(it)[...].astype(F32) for _ in range(n_p)]
        d_vals = [next(it)[...].astype(F32) for _ in range(n_d)]
        add_vals = [next(it)[...].astype(F32) for _ in range(n_add)]
        dr_refs = [next(it) for _ in range(n_r)]
        dp_refs = [next(it) for _ in range(n_p)]

        def f(*rp):
            return tuple(fn(*rp[:n_r], *a_vals, *rp[n_r:]))

        _, vjp = jax.vjp(f, *r_vals, *p_vals)
        grads = list(vjp(tuple(d_vals)))
        for (ri, _), av in zip(adds, add_vals):
            grads[ri] = grads[ri] + av
        for dr_ref, g in zip(dr_refs, grads[:n_r]):
            dr_ref[...] = g.astype(dr_ref.dtype)

        @pl.when(pl.program_id(0) == 0)
        def _():
            for dp_ref in dp_refs:
                dp_ref[...] = jnp.zeros_like(dp_ref)

        for dp_ref, g in zip(dp_refs, grads[n_r:]):
            dp_ref[...] += g

    all_rows = list(rows) + list(aux) + list(douts) + [(arr,) + tuple(rows[ri][1:3]) + (0,) for ri, arr in adds]
    in_specs = ([_row_spec(th, cw, ci) for (_, th, cw, ci) in list(rows) + list(aux)]
                + [_full_spec(p.shape) for p in params]
                + [_row_spec(th, cw, ci) for (_, th, cw, ci) in all_rows[n_r + n_a:]])
    res = pl.pallas_call(
        body, name=name, grid=(n_steps,),
        in_specs=in_specs,
        out_specs=[_row_spec(th, cw, 0) for (_, th, cw, _) in rows] + [_full_spec(p.shape) for p in params],
        out_shape=[jax.ShapeDtypeStruct((n_steps * th, cw), dt) for (_, th, cw, _), dt in zip(rows, row_dtypes)]
        + [jax.ShapeDtypeStruct(p.shape, F32) for p in params],
        compiler_params=_params(("arbitrary",)),
    )(*[r[0] for r in list(rows) + list(aux)], *params, *[r[0] for r in all_rows[n_r + n_a:]])
    return res[:n_r], res[n_r:]


def _sigmoid(x):
    return lax.logistic(x)


def _silu(x):
    return x * _sigmoid(x)


def _rms(x, w=None, n=None):
    n = n or x.shape[-1]
    y = x * lax.rsqrt(jnp.sum(x * x, axis=-1, keepdims=True) * (1.0 / n) + EPS)
    return y if w is None else y * w


def _modulate(x, scale, shift):
    return _rms(x) * (1.0 + scale) + shift


def _softplus(x):
    return jnp.maximum(x, 0.0) + jnp.log1p(jnp.exp(-jnp.abs(x)))


@jax.custom_vjp
def _rot_half64(x):
    lane = lax.broadcasted_iota(jnp.int32, x.shape, 1)
    up = pltpu.roll(x, 96, 1)
    down = pltpu.roll(x, 32, 1)
    return jnp.where(lane < 32, up, jnp.where(lane < 64, down, 0.0))


_rot_half64.defvjp(lambda x: (_rot_half64(x), None), lambda _, g: (_rot_half64(g),))


def _rope128(x, cos_p, sin_p):
    return x * cos_p + _rot_half64(x) * sin_p


def _gdn_pre_fn(qkvc, kab, alog_p, dt_p):
    a = _silu(qkvc)
    qs, ks = [], []
    for h in range(HEADS):
        qh = a[:, HEAD_DIM * h:HEAD_DIM * (h + 1)]
        kh = a[:, 512 + HEAD_DIM * h:512 + HEAD_DIM * (h + 1)]
        qs.append(qh * lax.rsqrt(jnp.sum(qh * qh, axis=-1, keepdims=True) + EPS) * (HEAD_DIM ** -0.5))
        ks.append(kh * lax.rsqrt(jnp.sum(kh * kh, axis=-1, keepdims=True) + EPS))
    lane = lax.broadcasted_iota(jnp.int32, kab.shape, 1)
    g_full = -jnp.exp(alog_p) * _softplus(kab + dt_p)
    b_full = _sigmoid(kab)
    gb = jnp.where((lane >= 64) & (lane < 68), g_full, jnp.where((lane >= 68) & (lane < 72), b_full, 0.0))
    return jnp.concatenate(qs, axis=1), jnp.concatenate(ks, axis=1), a[:, 1024:1536], gb


INTRA_ROWS = (512, 256, 128, 64)
TOKEN_ROWS = (512, 256, 128)

_BNN = (((2,), (1,)), ((0,), (0,)))
_BNT = (((2,), (2,)), ((0,), (0,)))


def _split_bf16(a):
    hi = a.astype(BF16)
    return hi, (a - hi.astype(F32)).astype(BF16)


def _dot3_raw(a, b, dims):
    a_hi, a_lo = _split_bf16(a)
    b_hi, b_lo = _split_bf16(b)
    dot = lambda x_, y_: lax.dot_general(x_, y_, dims, preferred_element_type=F32)
    return dot(a_hi, b_hi) + (dot(a_hi, b_lo) + dot(a_lo, b_hi))


@functools.partial(jax.custom_vjp, nondiff_argnums=(2, 3))
def _dot3(a, b, nt, exact_bwd=True):
    return _dot3_raw(a, b, _BNT if nt else _BNN)


def _dot3_fwd(a, b, nt, exact_bwd):
    return _dot3_raw(a, b, _BNT if nt else _BNN), (a, b)


def _dot3_bwd(nt, exact_bwd, res, g):
    a, b = res
    if exact_bwd:
        dot = _dot3_raw
    else:
        dot = lambda x_, y_, d_: lax.dot_general(x_.astype(BF16), y_.astype(BF16), d_, preferred_element_type=F32)
    if nt:
        return dot(g, b, _BNN), dot(jnp.swapaxes(g, 1, 2), a, _BNN)
    return dot(g, b, _BNT), dot(jnp.swapaxes(a, 1, 2), g, _BNN)


_dot3.defvjp(_dot3_fwd, _dot3_bwd)


@functools.partial(jax.custom_vjp, nondiff_argnums=(2,))
def _bdot_b(a, b, nt):
    return lax.dot_general(a.astype(BF16), b.astype(BF16), _BNT if nt else _BNN, preferred_element_type=F32)


def _bdot_b_fwd(a, b, nt):
    return _bdot_b(a, b, nt), (a, b)


def _bdot_b_bwd(nt, res, g):
    a, b = res
    dot = lambda x_, y_, d_: lax.dot_general(x_.astype(BF16), y_.astype(BF16), d_, preferred_element_type=F32)
    if nt:
        return dot(g, b, _BNN), dot(jnp.swapaxes(g, 1, 2), a, _BNN)
    return dot(g, b, _BNT), dot(jnp.swapaxes(a, 1, 2), g, _BNN)


_bdot_b.defvjp(_bdot_b_fwd, _bdot_b_bwd)


@jax.custom_vjp
def _inverse_given(a_mat, inv):
    return inv


def _inverse_given_bwd(inv, g):
    inv_t = jnp.swapaxes(inv, 1, 2)
    return -_dot3_raw(_dot3_raw(inv_t, g, _BNN), inv_t, _BNN), jnp.zeros_like(inv)


_inverse_given.defvjp(lambda a_mat, inv: (inv, inv), _inverse_given_bwd)


def _intra_batched(q, k, v, g_col, b_col, inv_known=None):
    c = CHUNK
    nb = q.shape[0]
    row = lax.broadcasted_iota(jnp.int32, (1, c, c), 1)
    col = lax.broadcasted_iota(jnp.int32, (1, c, c), 2)
    incl, strict, eye = row >= col, row > col, row == col
    tri = jnp.broadcast_to(jnp.where(incl, 1.0, 0.0).astype(F32), (nb, c, c))
    ident = jnp.where(eye, 1.0, 0.0).astype(F32)
    g_wide = _dot3(tri, jnp.broadcast_to(g_col, (nb, c, HEAD_DIM)), False)
    g_i = g_wide[:, :, :c]
    g_j = jnp.sum(jnp.where(eye, g_i, 0.0), axis=1, keepdims=True)
    decay = jnp.where(incl, jnp.exp(jnp.where(incl, g_i - g_j, 0.0)), 0.0)
    kk = _bdot_b(k, k, True)
    a_mat = jnp.where(strict, b_col * kk * decay, 0.0)
    if inv_known is None:
        x_pow = -a_mat
        inv = ident + x_pow
        for _ in range(5):
            x_pow = _dot3(x_pow, x_pow, False, False)
            inv = inv + _dot3(inv, x_pow, False, False)
    else:
        inv = _inverse_given(a_mat, inv_known)
    e_wide = jnp.exp(g_wide)
    u = _dot3(inv, v * b_col, False)
    wk = _dot3(inv, k * b_col * e_wide, False)
    qk = _bdot_b(q, k, True) * decay
    last = lax.broadcasted_iota(jnp.int32, (1, c, HEAD_DIM), 1) == c - 1
    g_last = jnp.sum(jnp.where(last, g_wide, 0.0), axis=1, keepdims=True)
    qd = q * e_wide
    kd = k * jnp.exp(g_last - g_wide)
    gl = jnp.broadcast_to(jnp.exp(g_last), (nb, 8, HEAD_DIM))
    return u, wk, qd, kd, qk, gl, inv


def _gdn_intra_fn(q, k, v, gb, *inv_known):
    t = q.shape[0]
    nch = t // CHUNK
    lane = lax.broadcasted_iota(jnp.int32, gb.shape, 1)

    def heads_first(x_):
        return jnp.concatenate([x_[:, HEAD_DIM * h:HEAD_DIM * (h + 1)].reshape(nch, CHUNK, HEAD_DIM) for h in range(HEADS)],
                               axis=0)

    def column(first_lane):
        return jnp.concatenate([jnp.sum(jnp.where(lane == first_lane + h, gb, 0.0), axis=1, keepdims=True)
                                .reshape(nch, CHUNK, 1) for h in range(HEADS)], axis=0)

    known = jnp.concatenate([x_.reshape(nch, CHUNK, CHUNK) for x_ in inv_known], axis=0) if inv_known else None
    u, wk, qd, kd, qk, gl, inv = _intra_batched(heads_first(q), heads_first(k), heads_first(v), column(64), column(68), known)

    def rows_first(x_):
        r, w_ = x_.shape[1], x_.shape[2]
        return jnp.concatenate([x_[nch * h:nch * (h + 1)].reshape(nch * r, w_) for h in range(HEADS)], axis=1)

    per_head = lambda x_: [x_[nch * h:nch * (h + 1)].reshape(t, CHUNK) for h in range(HEADS)]
    outs = (rows_first(u), rows_first(wk), rows_first(qd), rows_first(kd), *per_head(qk), rows_first(gl))
    return outs if inv_known else outs + tuple(per_head(inv))


def _scan_step(s0, u, wk, qd, kd, qk, gl):
    v_new = u - _bdot_b(wk, s0, False)
    o = _bdot_b(qd, s0, False) + _bdot_b(qk, v_new, False)
    s1 = s0 * gl[:, 0:1, :] + _bdot_b(jnp.swapaxes(kd, 1, 2), v_new, False)
    return o, s1


def _mix_post_fn(o_a, z, o_b, gnw, onw):
    parts = [_rms(o_a[:, HEAD_DIM * h:HEAD_DIM * (h + 1)], gnw) * _silu(z[:, HEAD_DIM * h:HEAD_DIM * (h + 1)])
             for h in range(HEADS)]
    parts += [_rms(o_b[:, HEAD_DIM * h:HEAD_DIM * (h + 1)], onw) for h in range(HEADS)]
    return (jnp.concatenate(parts, axis=1),)


def _mla_pre_fn(ckv, cq, kab, cos_p, sin_p, qnw, kvnw, wuq, wukv, qn_w, qr_w, kn_w, kr_w):
    scale = (HEAD_DIM + ROPE) ** -0.5 * LOG2_E
    qf = _bdot(_rms(cq, qnw), wuq, "nn")
    kvf = _bdot(_rms(ckv, kvnw), wukv, "nn")
    lane = lax.broadcasted_iota(jnp.int32, kab.shape, 1)
    kr = _rope128(_rms(jnp.where(lane < ROPE, kab, 0.0), kr_w, n=ROPE), cos_p, sin_p)
    qs, ks = [], []
    for h in range(HEADS):
        qn = _rms(qf[:, 256 * h:256 * h + 128], qn_w) * scale
        qr = _rope128(_rms(qf[:, 256 * h + 128:256 * h + 256], qr_w, n=ROPE), cos_p, sin_p) * scale
        qs += [qn, qr]
        ks += [_rms(kvf[:, 128 * h:128 * (h + 1)], kn_w), kr]
    return jnp.concatenate(qs, axis=1), jnp.concatenate(ks, axis=1), kvf[:, 512:]


def _conv_fwd(proj, conv_w, tm, name):
    S = proj.shape[0]
    C = 1536
    nb = tm // 8

    def body(x_ref, prev_ref, w_ref, o_ref, ext_ref):
        i = pl.program_id(0)
        ext_ref[0:8, :] = jnp.where(i > 0, prev_ref[...], 0.0)
        ext_ref[8:, :] = x_ref[...]
        acc = jnp.zeros((tm, C), F32)
        for k in range(4):
            acc = acc + w_ref[k:k + 1, :] * ext_ref[pl.ds(5 + k, tm), :]
        o_ref[...] = acc

    return pl.pallas_call(
        body, name=name, grid=(S // tm,),
        in_specs=[pl.BlockSpec((tm, C), lambda i: (i, 0)),
                  pl.BlockSpec((8, C), lambda i: (jnp.maximum(i * nb - 1, 0), 0)),
                  pl.BlockSpec((4, C), lambda i: (0, 0))],
        out_specs=pl.BlockSpec((tm, C), lambda i: (i, 0)),
        out_shape=jax.ShapeDtypeStruct((S, C), F32),
        scratch_shapes=[pltpu.VMEM((tm + 8, C), F32)],
        compiler_params=_params(("arbitrary",)),
    )(proj, proj, conv_w)


def _conv_bwd(proj, dout, conv_w, tm, name):
    S = proj.shape[0]
    C = 1536
    nb = tm // 8
    n_steps = S // tm

    def body(x_ref, prev_ref, d_ref, next_ref, w_ref, dx_ref, dw_ref, xext_ref, dext_ref):
        i = pl.program_id(0)
        xext_ref[0:8, :] = jnp.where(i > 0, prev_ref[...], 0.0)
        xext_ref[8:, :] = x_ref[...]
        dext_ref[0:tm, :] = d_ref[...]
        dext_ref[tm:, :] = jnp.where(i < n_steps - 1, next_ref[...], 0.0)
        d = d_ref[...]
        acc = jnp.zeros((tm, C), F32)
        dws = []
        for k in range(4):
            acc = acc + w_ref[k:k + 1, :] * dext_ref[pl.ds(3 - k, tm), :]
            dws.append(jnp.sum(d * xext_ref[pl.ds(5 + k, tm), :], axis=0, keepdims=True))
        dx_ref[...] = acc.astype(dx_ref.dtype)

        @pl.when(i == 0)
        def _():
            dw_ref[...] = jnp.zeros_like(dw_ref)

        dw_ref[...] += jnp.concatenate(dws + [jnp.zeros((4, C), F32)], axis=0)

    return pl.pallas_call(
        body, name=name, grid=(n_steps,),
        in_specs=[pl.BlockSpec((tm, C), lambda i: (i, 0)),
                  pl.BlockSpec((8, C), lambda i: (jnp.maximum(i * nb - 1, 0), 0)),
                  pl.BlockSpec((tm, C), lambda i: (i, 0)),
                  pl.BlockSpec((8, C), lambda i: (jnp.minimum((i + 1) * nb, S // 8 - 1), 0)),
                  pl.BlockSpec((4, C), lambda i: (0, 0))],
        out_specs=[pl.BlockSpec((tm, C), lambda i: (i, 0)), pl.BlockSpec((8, C), lambda i: (0, 0))],
        out_shape=[jax.ShapeDtypeStruct((S, C), BF16), jax.ShapeDtypeStruct((8, C), F32)],
        scratch_shapes=[pltpu.VMEM((tm + 8, C), F32), pltpu.VMEM((tm + 8, C), F32)],
        compiler_params=_params(("arbitrary",)),
    )(proj, proj, dout, dout, conv_w)


SCAN_CHUNKS = (8, 4, 2, 1)


def _gdn_scan_fwd(u, wk, qd, kd, qks, gl, name):
    S = u.shape[0]
    nc = S // CHUNK
    cs = _pick(nc, SCAN_CHUNKS)
    W = HEADS * HEAD_DIM

    def body(u_ref, wk_ref, qd_ref, kd_ref, qk0, qk1, qk2, qk3, gl_ref, o_ref, sp_ref, s_ref):
        @pl.when(pl.program_id(0) == 0)
        def _():
            s_ref[...] = jnp.zeros_like(s_ref)

        state = s_ref[...]
        for c in range(cs):
            rows, gl_rows = slice(CHUNK * c, CHUNK * (c + 1)), slice(8 * c, 8 * (c + 1))
            sp_ref[c] = state
            o, state = _scan_step(state, _heads(u_ref, HEAD_DIM, rows), _heads(wk_ref, HEAD_DIM, rows),
                                  _heads(qd_ref, HEAD_DIM, rows), _heads(kd_ref, HEAD_DIM, rows),
                                  jnp.stack([r[rows, :] for r in (qk0, qk1, qk2, qk3)]), _heads(gl_ref, HEAD_DIM, gl_rows))
            for h in range(HEADS):
                o_ref[rows, HEAD_DIM * h:HEAD_DIM * (h + 1)] = o[h]
        s_ref[...] = state

    row = pl.BlockSpec((cs * CHUNK, W), lambda n: (n, 0))
    qk_spec = pl.BlockSpec((cs * CHUNK, CHUNK), lambda n: (n, 0))
    return pl.pallas_call(
        body, name=name, grid=(nc // cs,),
        in_specs=[row, row, row, row, qk_spec, qk_spec, qk_spec, qk_spec, pl.BlockSpec((cs * 8, W), lambda n: (n, 0))],
        out_specs=[row, pl.BlockSpec((cs, HEADS, HEAD_DIM, HEAD_DIM), lambda n: (n, 0, 0, 0))],
        out_shape=[jax.ShapeDtypeStruct((S, W), F32), jax.ShapeDtypeStruct((nc, HEADS, HEAD_DIM, HEAD_DIM), F32)],
        scratch_shapes=[pltpu.VMEM((HEADS, HEAD_DIM, HEAD_DIM), F32)],
        compiler_params=_params(("arbitrary",)),
    )(u, wk, qd, kd, *qks, gl)


def _gdn_scan_bwd(u, wk, qd, kd, qks, gl, s_prev, d_o, name):
    S = u.shape[0]
    nc = S // CHUNK
    cs = _pick(nc, SCAN_CHUNKS)
    nb = nc // cs
    W = HEADS * HEAD_DIM

    def body(u_ref, wk_ref, qd_ref, kd_ref, qk0, qk1, qk2, qk3, gl_ref, sp_ref, do_ref,
             du_ref, dwk_ref, dqd_ref, dkd_ref, dqk0, dqk1, dqk2, dqk3, dgl_ref, ds_ref):
        @pl.when(pl.program_id(0) == 0)
        def _():
            ds_ref[...] = jnp.zeros_like(ds_ref)

        d_state = ds_ref[...]
        for c in reversed(range(cs)):
            rows, gl_rows = slice(CHUNK * c, CHUNK * (c + 1)), slice(8 * c, 8 * (c + 1))
            _, vjp = jax.vjp(_scan_step, sp_ref[c], _heads(u_ref, HEAD_DIM, rows), _heads(wk_ref, HEAD_DIM, rows),
                             _heads(qd_ref, HEAD_DIM, rows), _heads(kd_ref, HEAD_DIM, rows),
                             jnp.stack([r[rows, :] for r in (qk0, qk1, qk2, qk3)]), _heads(gl_ref, HEAD_DIM, gl_rows))
            d_state, du, dwk, dqd, dkd, dqk, dgl = vjp((_heads(do_ref, HEAD_DIM, rows), d_state))
            for h, dqk_ref in enumerate((dqk0, dqk1, dqk2, dqk3)):
                sl = slice(HEAD_DIM * h, HEAD_DIM * (h + 1))
                du_ref[rows, sl] = du[h]
                dwk_ref[rows, sl] = dwk[h]
                dqd_ref[rows, sl] = dqd[h]
                dkd_ref[rows, sl] = dkd[h]
                dqk_ref[rows, :] = dqk[h]
                dgl_ref[gl_rows, sl] = dgl[h]
        ds_ref[...] = d_state

    rev = lambda n: (nb - 1 - n, 0)
    row = pl.BlockSpec((cs * CHUNK, W), rev)
    qk_spec = pl.BlockSpec((cs * CHUNK, CHUNK), rev)
    gl_spec = pl.BlockSpec((cs * 8, W), rev)
    qk_shape = jax.ShapeDtypeStruct((S, CHUNK), F32)
    row_shape = jax.ShapeDtypeStruct((S, W), F32)
    return pl.pallas_call(
        body, name=name, grid=(nb,),
        in_specs=[row, row, row, row, qk_spec, qk_spec, qk_spec, qk_spec, gl_spec,
                  pl.BlockSpec((cs, HEADS, HEAD_DIM, HEAD_DIM), lambda n: (nb - 1 - n, 0, 0, 0)), row],
        out_specs=[row, row, row, row, qk_spec, qk_spec, qk_spec, qk_spec, gl_spec],
        out_shape=[row_shape] * 4 + [qk_shape] * 4 + [jax.ShapeDtypeStruct((nc * 8, W), F32)],
        scratch_shapes=[pltpu.VMEM((HEADS, HEAD_DIM, HEAD_DIM), F32)],
        compiler_params=_params(("arbitrary",)),
    )(u, wk, qd, kd, *qks, gl, s_prev, d_o)


NEG = -1e30


def _chunk_mask(i, j, t, transposed=False):
    q_axis, k_axis = (1, 0) if transposed else (0, 1)
    r = (i * t + lax.broadcasted_iota(jnp.int32, (t, t), q_axis)) // CHUNK
    c = (j * t + lax.broadcasted_iota(jnp.int32, (t, t), k_axis)) // CHUNK
    return c <= r


def _tile_pairs(n, by_key):
    pairs = [(i, j) for j in range(n) for i in range(j, n)] if by_key else [(i, j) for i in range(n) for j in range(i + 1)]
    return jnp.asarray(np.array([p[0] for p in pairs], np.int32)), jnp.asarray(np.array([p[1] for p in pairs], np.int32))


def _heads(ref, width, rows=slice(None)):
    return jnp.stack([ref[rows, width * h:width * (h + 1)] for h in range(HEADS)])


def _bmm(a, b, dims):
    return lax.dot_general(a.astype(BF16), b.astype(BF16), dims, preferred_element_type=F32)


def _attn_fwd(q, k, v_t, t, name):
    S = q.shape[0]
    n = S // t
    qi, kj = _tile_pairs(n, by_key=False)

    def body(qi_ref, kj_ref, q_ref, k_ref, vt_ref, o_ref, lse_ref, m_ref, l_ref, acc_ref):
        i, j = qi_ref[pl.program_id(0)], kj_ref[pl.program_id(0)]

        @pl.when(j == 0)
        def _():
            m_ref[...] = jnp.full_like(m_ref, NEG)
            l_ref[...] = jnp.zeros_like(l_ref)
            acc_ref[...] = jnp.zeros_like(acc_ref)

        def update(masked):
            s_t = _bmm(_heads(k_ref, 256), _heads(q_ref, 256), _BNT)
            if masked:
                s_t = jnp.where(_chunk_mask(i, j, t, transposed=True)[None], s_t, NEG)
            m_old = m_ref[...]
            m_new = jnp.maximum(m_old, jnp.max(s_t, axis=1, keepdims=True))
            p_t = jnp.exp2(s_t - m_new)
            alpha = jnp.exp2(m_old - m_new)
            l_ref[...] = alpha * l_ref[...] + jnp.sum(p_t, axis=1, keepdims=True)
            v_heads = jnp.stack([vt_ref[HEAD_DIM * h:HEAD_DIM * (h + 1), :] for h in range(HEADS)])
            acc_ref[...] = alpha * acc_ref[...] + _bmm(v_heads, p_t, _BNN)
            m_ref[...] = m_new

        @pl.when(j < i)
        def _():
            update(False)

        @pl.when(j == i)
        def _():
            update(True)
            for h in range(HEADS):
                sl = slice(HEAD_DIM * h, HEAD_DIM * (h + 1))
                o_ref[:, sl] = jnp.transpose(acc_ref[h] / l_ref[h])
                lse_ref[:, sl] = jnp.transpose(jnp.broadcast_to(m_ref[h] + jnp.log(l_ref[h]) * LOG2_E, (HEAD_DIM, t)))

    row = lambda p, qi_, kj_: (qi_[p], 0)
    return pl.pallas_call(
        body, name=name,
        grid_spec=pltpu.PrefetchScalarGridSpec(
            num_scalar_prefetch=2, grid=(qi.shape[0],),
            in_specs=[pl.BlockSpec((t, HEADS * 256), row), pl.BlockSpec((t, HEADS * 256), lambda p, qi_, kj_: (kj_[p], 0)),
                      pl.BlockSpec((HEADS * HEAD_DIM, t), lambda p, qi_, kj_: (0, kj_[p]))],
            out_specs=[pl.BlockSpec((t, HEADS * HEAD_DIM), row)] * 2,
            scratch_shapes=[pltpu.VMEM((HEADS, 1, t), F32), pltpu.VMEM((HEADS, 1, t), F32),
                            pltpu.VMEM((HEADS, HEAD_DIM, t), F32)]),
        out_shape=[jax.ShapeDtypeStruct((S, HEADS * HEAD_DIM), F32)] * 2,
        compiler_params=_params(("arbitrary",)),
    )(qi, kj, q, k, v_t)


def _attn_stats(o, lse, d_o, t, name):
    S = o.shape[0]

    def body(o_ref, lse_ref, do_ref, st_ref):
        lane = lax.broadcasted_iota(jnp.int32, (t, HEAD_DIM), 1)
        stats = jnp.zeros((t, HEAD_DIM), F32)
        for h in range(HEADS):
            sl = slice(HEAD_DIM * h, HEAD_DIM * (h + 1))
            delta = jnp.sum(do_ref[:, sl] * o_ref[:, sl], axis=1, keepdims=True)
            stats = stats + jnp.where(lane == h, lse_ref[:, sl], 0.0) + jnp.where(lane == HEADS + h, delta, 0.0)
        st_ref[...] = jnp.transpose(stats)[0:8, :]

    row = pl.BlockSpec((t, HEADS * HEAD_DIM), lambda i: (i, 0))
    return pl.pallas_call(
        body, name=name, grid=(S // t,),
        in_specs=[row, row, row], out_specs=pl.BlockSpec((8, t), lambda i: (0, i)),
        out_shape=jax.ShapeDtypeStruct((8, S), F32),
        compiler_params=_params(("parallel",)),
    )(o, lse, d_o)


BWD_GROUP = 2


def _attn_bwd(q, k, v, d_o, stats, t, name):
    S = q.shape[0]
    n = S // t
    groups = HEADS // BWD_GROUP
    gq, gv = BWD_GROUP * 256, BWD_GROUP * HEAD_DIM
    qi, kj = _tile_pairs(n, by_key=True)
    n_pairs = qi.shape[0]
    st = stats.reshape(2, groups, BWD_GROUP, S).transpose(1, 0, 2, 3).reshape(groups, 2 * BWD_GROUP, S)
    st = jnp.pad(st, ((0, 0), (0, 8 - 2 * BWD_GROUP), (0, 0)))

    def heads(ref, width, rows=slice(None)):
        return jnp.stack([ref[rows, width * h:width * (h + 1)] for h in range(BWD_GROUP)])

    def body(qi_ref, kj_ref, q_ref, k_ref, v_ref, do_ref, st_ref, dq_hbm, dk_ref, dv_ref, dq_acc, sem):
        g, p = pl.program_id(0), pl.program_id(1)
        i, j = qi_ref[p], kj_ref[p]

        @pl.when(i == j)
        def _():
            dk_ref[...] = jnp.zeros_like(dk_ref)
            dv_ref[...] = jnp.zeros_like(dv_ref)

        def update(masked):
            qh, kh = heads(q_ref, 256), heads(k_ref, 256)
            d_out = heads(do_ref, HEAD_DIM)
            stv = st_ref[...]
            lse_row = jnp.stack([stv[h:h + 1, :] for h in range(BWD_GROUP)])
            delta_row = jnp.stack([stv[BWD_GROUP + h:BWD_GROUP + h + 1, :] for h in range(BWD_GROUP)])
            s_t = _bmm(kh, qh, _BNT)
            p_t = jnp.exp2(s_t - lse_row)
            if masked:
                p_t = jnp.where(_chunk_mask(i, j, t, transposed=True)[None], p_t, 0.0)
            dv = _bmm(p_t, d_out, _BNN)
            dp_t = _bmm(heads(v_ref, HEAD_DIM), d_out, _BNT)
            ds_t = p_t * (dp_t - delta_row)
            dk = _bmm(ds_t, qh, _BNN)
            dq = _bmm(jnp.swapaxes(ds_t, 1, 2), kh, _BNN)
            rows = pl.ds(pl.multiple_of(i * t, t), t)
            for h in range(BWD_GROUP):
                dk_ref[:, 256 * h:256 * (h + 1)] += dk[h]
                dv_ref[:, HEAD_DIM * h:HEAD_DIM * (h + 1)] += dv[h]

            @pl.when(j == 0)
            def _():
                for h in range(BWD_GROUP):
                    dq_acc[rows, 256 * h:256 * (h + 1)] = dq[h]

            @pl.when(j > 0)
            def _():
                for h in range(BWD_GROUP):
                    dq_acc[rows, 256 * h:256 * (h + 1)] += dq[h]

        @pl.when(i == j)
        def _():
            update(True)

        @pl.when(i > j)
        def _():
            update(False)

        @pl.when(i == n - 1)
        def _():
            dk_ref[...] *= 1.0 / LOG2_E

        @pl.when(p == n_pairs - 1)
        def _():
            dq_acc[...] *= 1.0 / LOG2_E
            for gg in range(groups):
                @pl.when(g == gg)
                def _():
                    cp = pltpu.make_async_copy(dq_acc, dq_hbm.at[:, gq * gg:gq * (gg + 1)], sem)
                    cp.start()
                    cp.wait()

    q_blk = lambda g, p, qi_, kj_: (qi_[p], g)
    k_blk = lambda g, p, qi_, kj_: (kj_[p], g)
    return pl.pallas_call(
        body, name=name,
        grid_spec=pltpu.PrefetchScalarGridSpec(
            num_scalar_prefetch=2, grid=(groups, n_pairs),
            in_specs=[pl.BlockSpec((t, gq), q_blk), pl.BlockSpec((t, gq), k_blk), pl.BlockSpec((t, gv), k_blk),
                      pl.BlockSpec((t, gv), q_blk), pl.BlockSpec((None, 8, t), lambda g, p, qi_, kj_: (g, 0, qi_[p]))],
            out_specs=[pl.BlockSpec(memory_space=pl.ANY), pl.BlockSpec((t, gq), k_blk), pl.BlockSpec((t, gv), k_blk)],
            scratch_shapes=[pltpu.VMEM((S, gq), F32), pltpu.SemaphoreType.DMA]),
        out_shape=[jax.ShapeDtypeStruct((S, HEADS * 256), F32), jax.ShapeDtypeStruct((S, HEADS * 256), F32),
                   jax.ShapeDtypeStruct((S, HEADS * HEAD_DIM), F32)],
        compiler_params=_params(("arbitrary", "arbitrary")),
    )(qi, kj, q, k, v, d_o, st)


FFN_PIECE = 2 * D_FF // N_DEV
HID_PIECES = D_FF // FFN_PIECE


def _after_specs(after):
    return [] if after is None else [pl.BlockSpec(memory_space=pl.ANY)]


def _after_args(after):
    return [] if after is None else [after]


def _ffn_gw8(h, d_gate, d_up, name, after=None):
    S = h.shape[0]
    tm = 512
    tk = _pick(S, (512, 256, 128))
    nk = S // tk

    def body(h_ref, dg_ref, du_ref, *rest):
        o_ref, acc_ref = rest[-2:]
        k = pl.program_id(1)

        @pl.when(k == 0)
        def _():
            acc_ref[...] = jnp.zeros_like(acc_ref)

        h_t = jnp.transpose(h_ref[...])
        for p in range(HID_PIECES):
            acc_ref[p] += _dot_raw(h_t, dg_ref[p], "nn")
            acc_ref[HID_PIECES + p] += _dot_raw(h_t, du_ref[p], "nn")

        @pl.when(k == nk - 1)
        def _():
            o_ref[...] = acc_ref[...].astype(o_ref.dtype)

    d_spec = pl.BlockSpec((HID_PIECES, tk, FFN_PIECE), lambda i, k: (0, k, 0))
    return pl.pallas_call(
        body, name=name, grid=(D_MODEL // tm, nk),
        in_specs=[pl.BlockSpec((tk, tm), lambda i, k: (k, i)), d_spec, d_spec] + _after_specs(after),
        out_specs=pl.BlockSpec((2 * HID_PIECES, tm, FFN_PIECE), lambda i, k: (0, i, 0)),
        out_shape=jax.ShapeDtypeStruct((2 * HID_PIECES, D_MODEL, FFN_PIECE), BF16),
        scratch_shapes=[pltpu.VMEM((2 * HID_PIECES, tm, FFN_PIECE), F32)],
        compiler_params=_params(("parallel", "arbitrary")),
    )(h, d_gate, d_up, *_after_args(after))


EPILOGUE_ROWS = 256


def _dmod_epilogue(acc_ref, x_ref, do_ref, sc_ref, sh_ref, dx_ref, dsc_ref, dsh_ref, below, below_in, below_out):
    rows_total = acc_ref.shape[0]
    step = min(EPILOGUE_ROWS, rows_total)
    dsc, dsh, dg = 0.0, 0.0, 0.0
    for r in range(rows_total // step):
        rows = slice(step * r, step * (r + 1))
        _, vjp = jax.vjp(_modulate, x_ref[rows, :], sc_ref[...], sh_ref[...])
        dx, dsc_r, dsh_r = vjp(acc_ref[rows, :])
        dx = dx + do_ref[rows, :]
        dx_ref[rows, :] = dx
        dsc, dsh = dsc + dsc_r, dsh + dsh_r
        if below is not None:
            coef = below[2]
            below_out[0][rows, :] = (coef * below_in[1][...] * dx).astype(below_out[0].dtype)
            dg = dg + jnp.sum(coef * below_in[0][rows, :] * dx, axis=0, keepdims=True)
    dsc_ref[...] += dsc
    dsh_ref[...] += dsh
    if below is not None:
        below_out[1][...] += dg


def _ffn_dh(d_gate, d_up, w8, x, d_out, scale, shift, name, after=None, below=None):
    S = d_gate.shape[1]
    tm = _pick(S, MATMUL_ROWS)
    n_below = 0 if below is None else 2

    def body(dg_ref, du_ref, wg_ref, wu_ref, x_ref, do_ref, sc_ref, sh_ref, *rest):
        below_in = rest[:n_below]
        outs = rest[len(rest) - 4 - n_below:]
        dx_ref, dsc_ref, dsh_ref = outs[:3]
        below_out, acc_ref = outs[3:3 + n_below], outs[-1]
        i, k = pl.program_id(0), pl.program_id(1)

        @pl.when(k == 0)
        def _():
            acc_ref[...] = jnp.zeros_like(acc_ref)

        acc_ref[...] += _dot_raw(dg_ref[...], wg_ref[...], "nt") + _dot_raw(du_ref[...], wu_ref[...], "nt")

        @pl.when((k == 0) & (i == 0))
        def _():
            for r in (dsc_ref, dsh_ref) + tuple(below_out[1:]):
                r[...] = jnp.zeros_like(r)

        @pl.when(k == HID_PIECES - 1)
        def _():
            _dmod_epilogue(acc_ref, x_ref, do_ref, sc_ref, sh_ref, dx_ref, dsc_ref, dsh_ref, below, below_in, below_out)

    d_spec = pl.BlockSpec((None, tm, FFN_PIECE), lambda i, k: (k, i, 0))
    row = pl.BlockSpec((tm, D_MODEL), lambda i, k: (i, 0))
    par = pl.BlockSpec((1, D_MODEL), lambda i, k: (0, 0))
    row_shape, par_shape = jax.ShapeDtypeStruct((S, D_MODEL), F32), jax.ShapeDtypeStruct((1, D_MODEL), F32)
    return pl.pallas_call(
        body, name=name, grid=(S // tm, HID_PIECES),
        in_specs=[d_spec, d_spec,
                  pl.BlockSpec((None, D_MODEL, FFN_PIECE), lambda i, k: (k, 0, 0)),
                  pl.BlockSpec((None, D_MODEL, FFN_PIECE), lambda i, k: (k + HID_PIECES, 0, 0)),
                  row, row, par, par] + [row, par][:n_below] + _after_specs(after),
        out_specs=[row, par, par] + [row, par][:n_below],
        out_shape=[row_shape, par_shape, par_shape] + [jax.ShapeDtypeStruct((S, D_MODEL), BF16), par_shape][:n_below],
        scratch_shapes=[pltpu.VMEM((tm, D_MODEL), F32)],
        compiler_params=_params(("arbitrary", "arbitrary")),
    )(d_gate, d_up, w8, w8, x, d_out, scale, shift, *(below[:2] if below is not None else ()), *_after_args(after))


def _swiglu_bwd(d_hid, hid_by_gate, hid_by_up):
    return d_hid * hid_by_gate, d_hid * hid_by_up


def _adamw_math(w_, g_, m_, v_):
    m_ = ADAM_B1 * m_ + (1.0 - ADAM_B1) * g_
    v_ = ADAM_B2 * v_ + (1.0 - ADAM_B2) * (g_ * g_)
    m_hat = m_ / (1.0 - ADAM_B1 ** ADAM_STEP)
    v_hat = v_ / (1.0 - ADAM_B2 ** ADAM_STEP)
    return -ADAM_LR * (m_hat / (jnp.sqrt(v_hat) + ADAM_EPS) + ADAM_WD * w_), m_, v_


def _adamw(w, g, m, v, name):
    R, C = w.shape
    tr = _pick(R, (256, 176, 128, 64, 32, 16, 8))

    def body(w_ref, g_ref, m_ref, v_ref, d_ref, nm_ref, nv_ref):
        d_ref[...], nm_ref[...], nv_ref[...] = _adamw_math(w_ref[...], g_ref[...], m_ref[...], v_ref[...])

    spec = pl.BlockSpec((tr, C), lambda i: (i, 0))
    return pl.pallas_call(
        body, name=name, grid=(R // tr,),
        in_specs=[spec] * 4, out_specs=[spec] * 3,
        out_shape=[jax.ShapeDtypeStruct((R, C), F32)] * 3,
        compiler_params=_params(("parallel",)),
    )(w, g, m, v)


def _sum_adamw(parts, w, m, v, name, transposed=False):
    _, R, C = parts.shape
    tr = _pick(R, (256, 176, 128, 64, 32, 16, 8))

    def body(p_ref, w_ref, m_ref, v_ref, g_ref, d_ref, nm_ref, nv_ref):
        g_ = p_ref[0].astype(F32)
        for d in range(1, N_DEV):
            g_ = g_ + p_ref[d].astype(F32)
        if transposed:
            g_ = jnp.transpose(g_)
        g_ref[...] = g_
        d_ref[...], nm_ref[...], nv_ref[...] = _adamw_math(w_ref[...], g_, m_ref[...], v_ref[...])

    spec = pl.BlockSpec((C, tr), lambda i: (0, i)) if transposed else pl.BlockSpec((tr, C), lambda i: (i, 0))
    return pl.pallas_call(
        body, name=name, grid=(R // tr,),
        in_specs=[pl.BlockSpec((N_DEV, tr, C), lambda i: (0, i, 0)), spec, spec, spec], out_specs=[spec] * 4,
        out_shape=[jax.ShapeDtypeStruct(w.shape, F32)] * 4,
        compiler_params=_params(("parallel",)),
    )(parts, w, m, v)


def _sum_devices(parts, name):
    _, R, C = parts.shape
    tr = _pick(R, (512, 256, 176, 128, 64, 32, 16, 8))

    def body(p_ref, o_ref):
        acc = p_ref[0].astype(F32)
        for d in range(1, N_DEV):
            acc = acc + p_ref[d].astype(F32)
        o_ref[...] = acc

    return pl.pallas_call(
        body, name=name, grid=(R // tr,),
        in_specs=[pl.BlockSpec((N_DEV, tr, C), lambda i: (0, i, 0))],
        out_specs=pl.BlockSpec((tr, C), lambda i: (i, 0)),
        out_shape=jax.ShapeDtypeStruct((R, C), F32),
        compiler_params=_params(("parallel",)),
    )(parts)


def _my_place():
    return lax.axis_index("x"), lax.axis_index("y"), lax.axis_index("c")


def _all_gather(blocks, name):
    n = len(blocks)

    def body(*refs):
        x_refs, out_refs = refs[:n], refs[n:2 * n]
        send_sems, recv_sems, local_sems = refs[2 * n:]
        x, y, c = _my_place()
        me, sibling = (x, y, c), (x, y, 1 - c)
        chips = [(1 - x, y), (x, 1 - y), (1 - x, 1 - y)]

        def copy(a, k, blk, to, own=False):
            slot = out_refs[a].at[4 * blk[0] + 2 * blk[1] + blk[2]]
            return pltpu.make_async_remote_copy(
                src_ref=x_refs[a] if own else slot, dst_ref=slot,
                send_sem=send_sems.at[7 * a + k], recv_sem=recv_sems.at[7 * a + k], device_id=to, device_id_type=MESH)

        mine = [pltpu.make_async_copy(x_refs[a], out_refs[a].at[4 * x + 2 * y + c], local_sems.at[a]) for a in range(n)]
        for cp in mine:
            cp.start()
        first = []
        for j, chip in enumerate(chips):
            first += [copy(a, 1 + j, me, (*chip, c), own=True) for a in range(n)]
        first += [copy(a, 0, me, sibling, own=True) for a in range(n)]
        for cp in first:
            cp.start()
        passed = []
        for j, chip in enumerate(chips):
            for a in range(n):
                copy(a, 1 + j, (*chip, c), me).wait_recv()
                passed.append(copy(a, 4 + j, (*chip, c), sibling))
                passed[-1].start()
        for a in range(n):
            copy(a, 0, sibling, me).wait_recv()
        for j, chip in enumerate(chips):
            for a in range(n):
                copy(a, 4 + j, (*chip, 1 - c), me).wait_recv()
        for cp in first + passed:
            cp.wait_send()
        for cp in mine:
            cp.wait()

    return pl.pallas_call(
        body, name=name,
        out_shape=[jax.ShapeDtypeStruct((N_DEV,) + b.shape, b.dtype) for b in blocks],
        in_specs=[pl.BlockSpec(memory_space=pl.ANY)] * n,
        out_specs=[pl.BlockSpec(memory_space=pl.ANY)] * n,
        scratch_shapes=[pltpu.SemaphoreType.DMA((7 * n,)), pltpu.SemaphoreType.DMA((7 * n,)), pltpu.SemaphoreType.DMA((n,))],
    )(*blocks)


def _all_to_all(pieces, name):
    n = len(pieces)

    def body(*refs):
        x_refs, out_refs = refs[:n], refs[n:2 * n]
        send_sems, recv_sems, local_sems = refs[2 * n:]
        x, y, c = _my_place()
        me = 4 * x + 2 * y + c
        mine = [pltpu.make_async_copy(x_refs[a].at[me], out_refs[a].at[me], local_sems.at[a]) for a in range(n)]
        for cp in mine:
            cp.start()
        copies = []
        for k in (2, 4, 6, 3, 5, 7, 1):
            px = 1 - x if k & 4 else x
            py = 1 - y if k & 2 else y
            pc = 1 - c if k & 1 else c
            peer = 4 * px + 2 * py + pc
            for a in range(n):
                copies.append(pltpu.make_async_remote_copy(
                    src_ref=x_refs[a].at[peer], dst_ref=out_refs[a].at[me],
                    send_sem=send_sems.at[7 * a + k - 1], recv_sem=recv_sems.at[7 * a + k - 1],
                    device_id=(px, py, pc), device_id_type=MESH))
        for cp in copies:
            cp.start()
        for cp in copies:
            cp.wait_recv()
        for cp in copies:
            cp.wait_send()
        for cp in mine:
            cp.wait()

    return pl.pallas_call(
        body, name=name,
        out_shape=[jax.ShapeDtypeStruct(p.shape, p.dtype) for p in pieces],
        in_specs=[pl.BlockSpec(memory_space=pl.ANY)] * n,
        out_specs=[pl.BlockSpec(memory_space=pl.ANY)] * n,
        scratch_shapes=[pltpu.SemaphoreType.DMA((7 * n,)), pltpu.SemaphoreType.DMA((7 * n,)), pltpu.SemaphoreType.DMA((n,))],
    )(*pieces)


def _peers():
    x, y, c = _my_place()
    out = []
    for k in (2, 4, 6, 3, 5, 7, 1):
        px = 1 - x if k & 4 else x
        py = 1 - y if k & 2 else y
        pc = 1 - c if k & 1 else c
        out.append((k, (px, py, pc), 4 * px + 2 * py + pc))
    return out


def _exchange_copies(x_refs, land_refs, send_sems, recv_sems, scatter):
    x, y, c = _my_place()
    me = 4 * x + 2 * y + c
    starts, arrivals = [], []
    for k, place, peer in _peers():
        for a, (x_ref, land_ref) in enumerate(zip(x_refs, land_refs)):
            sems = dict(send_sem=send_sems.at[7 * a + k - 1], recv_sem=recv_sems.at[7 * a + k - 1],
                        device_id=place, device_id_type=MESH)
            src = x_ref.at[peer] if scatter else x_ref
            starts.append(pltpu.make_async_remote_copy(src_ref=src, dst_ref=land_ref.at[me], **sems))
            arrivals.append(pltpu.make_async_remote_copy(src_ref=src, dst_ref=land_ref.at[peer], **sems))
    return starts, arrivals


def _exchange_start(arrays, scatter, name):
    n = len(arrays)
    hbm = pl.BlockSpec(memory_space=pltpu.HBM)
    sem = pl.BlockSpec(memory_space=pltpu.SEMAPHORE)
    lands = [lax.empty(a.shape if scatter else (N_DEV,) + a.shape, a.dtype) for a in arrays]

    def body(*refs):
        x_refs, land_refs = refs[:n], refs[n:2 * n]
        send_sems, recv_sems = refs[2 * n], refs[2 * n + 1]
        token = refs[-1]
        starts, _ = _exchange_copies(x_refs, land_refs, send_sems, recv_sems, scatter)
        for cp in starts:
            cp.start()
        token[...] = jnp.zeros_like(token)

    res = pl.pallas_call(
        body, name=name,
        out_shape=(pltpu.SemaphoreType.DMA((7 * n,)), pltpu.SemaphoreType.DMA((7 * n,)),
                   *[pltpu.HBM(a.shape, a.dtype) for a in arrays], *[pltpu.HBM(l.shape, l.dtype) for l in lands],
                   jax.ShapeDtypeStruct((8, 128), F32)),
        in_specs=[hbm] * (2 * n),
        out_specs=(sem, sem, *[hbm] * (2 * n), pl.BlockSpec(memory_space=pltpu.VMEM)),
        input_output_aliases={i: 2 + i for i in range(2 * n)},
        compiler_params=pltpu.CompilerParams(has_side_effects=pltpu.SideEffectType.DATAFLOW_SIDE_EFFECTING),
    )(*[pltpu.with_memory_space_constraint(a, pltpu.HBM) for a in arrays],
      *[pltpu.with_memory_space_constraint(l, pltpu.HBM) for l in lands])
    return res[0], res[1], list(res[2:2 + n]), list(res[2 + n:2 + 2 * n]), res[-1]


def _exchange_wait(handles, scatter, after, name):
    send_sems, recv_sems, arrays, lands, _ = handles
    n = len(arrays)
    hbm = pl.BlockSpec(memory_space=pltpu.HBM)
    sem = pl.BlockSpec(memory_space=pltpu.SEMAPHORE)

    def body(*refs):
        x_refs, land_refs = refs[:n], refs[n:2 * n]
        send_s, recv_s = refs[2 * n], refs[2 * n + 1]
        starts, arrivals = _exchange_copies(x_refs, land_refs, send_s, recv_s, scatter)
        for cp in arrivals:
            cp.wait_recv()
        for cp in starts:
            cp.wait_send()

    res = pl.pallas_call(
        body, name=name,
        out_shape=(*[pltpu.HBM(a.shape, a.dtype) for a in arrays], *[pltpu.HBM(l.shape, l.dtype) for l in lands]),
        in_specs=[hbm] * (2 * n) + [sem, sem, pl.BlockSpec(memory_space=pl.ANY)],
        out_specs=tuple([hbm] * (2 * n)),
        input_output_aliases={i: i for i in range(2 * n)},
        compiler_params=pltpu.CompilerParams(has_side_effects=pltpu.SideEffectType.DATAFLOW_SIDE_EFFECTING),
    )(*arrays, *lands, send_sems, recv_sems, after)
    me = 4 * lax.axis_index("x") + 2 * lax.axis_index("y") + lax.axis_index("c")
    out = []
    for src, got in zip(res[:n], res[n:]):
        zeros = (0,) * (got.ndim - 1)
        own = lax.dynamic_slice(src, (me,) + zeros, (1,) + src.shape[1:]) if scatter else src[None]
        out.append(lax.dynamic_update_slice(got, own, (me,) + zeros))
    return out


def _pad_lanes(v, at=0, width=128):
    return jnp.pad(v, ((0, 0), (at, width - at - v.shape[1])))


def _pack_weights(P):
    W = {}
    w = P["w_in"]
    W["wp"] = jnp.concatenate([w[:, :2048], w[:, 2440:2696], w[:, 2056:2440], w[:, 2696:2760], w[:, 2048:2056],
                               jnp.zeros((D_MODEL, N_IN_PACKED - N_IN), w.dtype)], axis=1).astype(BF16)
    W["conv_w"] = P["gdn_conv_w"].astype(F32)
    W["alog_p"] = _pad_lanes(P["gdn_a_log"], 64)
    W["dt_p"] = _pad_lanes(P["gdn_dt_bias"], 64)
    W["gnw"] = P["gdn_norm_w"]
    W["qnw"] = P["mla_q_norm_w"]
    W["kvnw"] = P["mla_kv_norm_w"]
    uq = P["mla_w_uq"].reshape(Q_LORA, HEADS, HEAD_DIM + ROPE)
    W["wuq"] = jnp.pad(uq, ((0, 0), (0, 0), (0, 256 - HEAD_DIM - ROPE))).reshape(Q_LORA, HEADS * 256).astype(BF16)
    ukv = P["mla_w_ukv"].reshape(KV_LORA, HEADS, 2, HEAD_DIM)
    W["wukv"] = ukv.transpose(0, 2, 1, 3).reshape(KV_LORA, 2 * HEADS * HEAD_DIM).astype(BF16)
    W["qn_w"] = P["qkn_q_nope"]
    W["qr_w"] = _pad_lanes(P["qkn_q_rope"])
    W["kn_w"] = P["qkn_k_nope"]
    W["kr_w"] = _pad_lanes(P["qkn_k_rope"])
    W["onw"] = P["mla_out_norm_w"]
    W["wout"] = P["w_out"].astype(BF16)
    return W


def _unpack_grads(G):
    g = G["wp"]
    uq = G["wuq"].reshape(Q_LORA, HEADS, 256)[:, :, :HEAD_DIM + ROPE].reshape(Q_LORA, HEADS * (HEAD_DIM + ROPE))
    ukv = G["wukv"].reshape(KV_LORA, 2, HEADS, HEAD_DIM).transpose(0, 2, 1, 3).reshape(KV_LORA, 2 * HEADS * HEAD_DIM)
    return {
        "w_in": jnp.concatenate([g[:, :2048], g[:, 2752:2760], g[:, 2304:2688], g[:, 2048:2304], g[:, 2688:2752]], axis=1),
        "gdn_conv_w": G["conv_w"], "gdn_a_log": G["alog_p"][:, 64:68], "gdn_dt_bias": G["dt_p"][:, 64:68],
        "gdn_norm_w": G["gnw"], "mla_q_norm_w": G["qnw"], "mla_w_uq": uq, "mla_kv_norm_w": G["kvnw"], "mla_w_ukv": ukv,
        "qkn_q_nope": G["qn_w"], "qkn_q_rope": G["qr_w"][:, :ROPE], "qkn_k_nope": G["kn_w"], "qkn_k_rope": G["kr_w"][:, :ROPE],
        "mla_out_norm_w": G["onw"], "w_out": G["wout"],
    }


def _rope_tables(positions):
    half = ROPE // 2
    inv_freq = ROPE_BASE ** (-jnp.arange(half, dtype=F32) / half)
    ang = positions.astype(F32)[:, None] * inv_freq
    cos, sin = jnp.cos(ang), jnp.sin(ang)
    zeros = jnp.zeros((positions.shape[0], 128 - ROPE), F32)
    return jnp.concatenate([cos, cos, zeros], axis=1), jnp.concatenate([-sin, sin, zeros], axis=1)


def _mod_fn(x, scale, shift):
    return (_modulate(x, scale, shift),)


def _ffn_forward(x, scale, shift, gate_w, w8, wo4, name, target=None):
    S = x.shape[0]
    tm = _pick(S, (512, 256, 128))
    n = S // tm
    with_loss = target is not None

    def body(x_ref, sc_ref, sh_ref, g_ref, wg_ref, wu_ref, wo_ref, *rest):
        t_ref = rest[0] if with_loss else None
        h_ref, bg_ref, bu_ref, ht_ref = rest[with_loss:with_loss + 4]
        tail = rest[with_loss + 4:]
        h_scr, acc_ref = tail[-2:]
        i, p = pl.program_id(0), pl.program_id(1)

        @pl.when(p == 0)
        def _():
            h_new = _modulate(x_ref[...], sc_ref[...], sh_ref[...]).astype(BF16)
            h_scr[...] = h_new
            h_ref[...] = h_new
            acc_ref[...] = jnp.zeros_like(acc_ref)

        h = h_scr[...]
        gate = _dot_raw(h, wg_ref[...], "nn")
        up = _dot_raw(h, wu_ref[...], "nn")
        sg = _sigmoid(gate)
        act = gate * sg
        hid = act * up
        bg_ref[...] = (up * (sg * (1.0 + gate * (1.0 - sg)))).astype(BF16)
        bu_ref[...] = act.astype(BF16)
        ht_ref[...] = jnp.transpose(hid).astype(BF16)
        acc_ref[...] += _dot_raw(hid, wo_ref[...], "nn")

        if with_loss:
            dx_ref, df_ref, dg_ref, l_ref = tail[:4]

            @pl.when((p == 0) & (i == 0))
            def _():
                dg_ref[...] = jnp.zeros_like(dg_ref)
                l_ref[...] = jnp.zeros_like(l_ref)

            @pl.when(p == HID_PIECES - 1)
            def _():
                step = min(EPILOGUE_ROWS, tm)
                for r in range(tm // step):
                    rows = slice(step * r, step * (r + 1))
                    f = acc_ref[rows, :]
                    diff = x_ref[rows, :] + 0.5 * g_ref[...] * f - t_ref[rows, :]
                    dx = diff * (1.0 / D_MODEL)
                    dx_ref[rows, :] = dx
                    df_ref[rows, :] = (0.5 * g_ref[...] * dx).astype(df_ref.dtype)
                    dg_ref[...] += jnp.sum(0.5 * f * dx, axis=0, keepdims=True)
                    l_ref[...] += jnp.sum(diff * diff, axis=0, keepdims=True)

            @pl.when((p == HID_PIECES - 1) & (i == n - 1))
            def _():
                l_ref[...] = jnp.full(l_ref.shape, (0.5 / D_MODEL) * jnp.sum(l_ref[...]), F32)
        else:
            f_ref, xo_ref = tail[:2]

            @pl.when(p == HID_PIECES - 1)
            def _():
                f = acc_ref[...]
                f_ref[...] = f.astype(f_ref.dtype)
                xo_ref[...] = x_ref[...] + 0.5 * g_ref[...] * f

    row = pl.BlockSpec((tm, D_MODEL), lambda i, p: (i, 0))
    par = pl.BlockSpec((1, D_MODEL), lambda i, p: (0, 0))
    piece = pl.BlockSpec((None, tm, FFN_PIECE), lambda i, p: (p, i, 0))
    row_f32, row_bf16 = jax.ShapeDtypeStruct((S, D_MODEL), F32), jax.ShapeDtypeStruct((S, D_MODEL), BF16)
    par_f32 = jax.ShapeDtypeStruct((1, D_MODEL), F32)
    piece_shape = jax.ShapeDtypeStruct((HID_PIECES, S, FFN_PIECE), BF16)
    return pl.pallas_call(
        body, name=name, grid=(n, HID_PIECES),
        in_specs=[row, par, par, par,
                  pl.BlockSpec((None, D_MODEL, FFN_PIECE), lambda i, p: (p, 0, 0)),
                  pl.BlockSpec((None, D_MODEL, FFN_PIECE), lambda i, p: (p + HID_PIECES, 0, 0)),
                  pl.BlockSpec((None, FFN_PIECE, D_MODEL), lambda i, p: (p, 0, 0))] + [row] * with_loss,
        out_specs=[row, piece, piece, pl.BlockSpec((None, FFN_PIECE, tm), lambda i, p: (p, 0, i))]
        + ([row, row, par, par] if with_loss else [row, row]),
        out_shape=[row_bf16, piece_shape, piece_shape, jax.ShapeDtypeStruct((HID_PIECES, FFN_PIECE, S), BF16)]
        + ([row_f32, row_bf16, par_f32, par_f32] if with_loss else [row_bf16, row_f32]),
        scratch_shapes=[pltpu.VMEM((tm, D_MODEL), BF16), pltpu.VMEM((tm, D_MODEL), F32)],
        compiler_params=_params(("arbitrary", "arbitrary")),
    )(x, scale, shift, gate_w, w8, w8, wo4, *([target] if with_loss else []))


def _ffn_fwd(x, scale, shift, gate_w, w8, wo4, tag, target=None):
    res = _ffn_forward(x, scale, shift, gate_w, w8, wo4, tag + "_fwd", target)
    h, by_gate, by_up, hid_t = res[:4]
    if target is not None:
        dx_out, df, d_gate_w, loss_row = res[4:]
        return (dx_out, loss_row), (h, by_gate, by_up, hid_t, None, df, d_gate_w)
    f, x_out = res[4:]
    return x_out, (h, by_gate, by_up, hid_t, f, None, None)


def _ffn_bwd(d_out, x, scale, shift, gate_w, w8, wo4, saved, tag, grad_ready, below=None):
    h, gate, up, hid_t, f, df, d_gate_w = saved
    S = x.shape[0]
    tm = _pick(S, (512, 256, 128))
    tk = _pick(S, (512, 256, 128))
    n = S // tm
    if df is None:
        (df,), (d_gate_w,) = _rowwise_bwd(lambda f_, g_: (0.5 * g_ * f_,), [(f, tm, D_MODEL, 0)], [], [gate_w],
                                          [(d_out, tm, D_MODEL, 0)], n, tag + "_dres", row_dtypes=(BF16,))
    tb = _pick(S, MATMUL_ROWS)
    piece = pl.BlockSpec((None, tb, FFN_PIECE), lambda i, j, k: (j, i, 0))
    d_gate, d_up = _mmg(df, wo4, "nt", name=tag + "_ddown", grid=(S // tb, HID_PIECES, 1),
                        a_spec=pl.BlockSpec((tb, D_MODEL), lambda i, j, k: (i, 0)),
                        b_spec=pl.BlockSpec((None, FFN_PIECE, D_MODEL), lambda i, j, k: (j, 0, 0)),
                        out_spec=piece, out_shapes=[jax.ShapeDtypeStruct((HID_PIECES, S, FFN_PIECE), BF16)] * 2,
                        acc_shape=(tb, FFN_PIECE), extras=[gate, up], extra_specs=[piece, piece], epi=_swiglu_bwd)
    tk = _pick(S, MATMUL_ROWS)
    g_wo4 = _mmg(hid_t, df, "nn", name=tag + "_gwo", grid=(HID_PIECES, 1, S // tk),
                 a_spec=pl.BlockSpec((None, FFN_PIECE, tk), lambda i, j, k: (i, 0, k)),
                 b_spec=pl.BlockSpec((tk, D_MODEL), lambda i, j, k: (k, j)),
                 out_spec=pl.BlockSpec((None, FFN_PIECE, D_MODEL), lambda i, j, k: (i, 0, j)),
                 out_shapes=[jax.ShapeDtypeStruct((HID_PIECES, FFN_PIECE, D_MODEL), BF16)], acc_shape=(FFN_PIECE, D_MODEL))
    g_w8 = _ffn_gw8(h, d_gate, d_up, tag + "_gw8", after=grad_ready("wo4", g_wo4))
    res = _ffn_dh(d_gate, d_up, w8, x, d_out, scale, shift, tag + "_dh", after=grad_ready("w8", g_w8), below=below)
    return (res[0], res[1], res[2], d_gate_w) + tuple(res[3:])


def _dproj_dmod(d_proj, wp, x, d_out, scale, shift, name, below=None):
    S, K = d_proj.shape
    tm = _pick(S, MATMUL_ROWS)
    tk = _pick(K, (1408, 512, 256, 128))
    nk = K // tk
    n_below = 0 if below is None else 2

    def body(dp_ref, w_ref, x_ref, do_ref, sc_ref, sh_ref, *rest):
        below_in = rest[:n_below]
        dx_ref, dsc_ref, dsh_ref = rest[n_below:n_below + 3]
        below_out, acc_ref = rest[n_below + 3:n_below + 3 + n_below], rest[-1]
        i, k = pl.program_id(0), pl.program_id(1)

        @pl.when(k == 0)
        def _():
            acc_ref[...] = jnp.zeros_like(acc_ref)

        acc_ref[...] += _dot_raw(dp_ref[...], w_ref[...], "nt")

        @pl.when((k == 0) & (i == 0))
        def _():
            for r in (dsc_ref, dsh_ref) + tuple(below_out[1:]):
                r[...] = jnp.zeros_like(r)

        @pl.when(k == nk - 1)
        def _():
            _dmod_epilogue(acc_ref, x_ref, do_ref, sc_ref, sh_ref, dx_ref, dsc_ref, dsh_ref, below, below_in, below_out)

    row = pl.BlockSpec((tm, D_MODEL), lambda i, k: (i, 0))
    par = pl.BlockSpec((1, D_MODEL), lambda i, k: (0, 0))
    row_shape, par_shape = jax.ShapeDtypeStruct((S, D_MODEL), F32), jax.ShapeDtypeStruct((1, D_MODEL), F32)
    return pl.pallas_call(
        body, name=name, grid=(S // tm, nk),
        in_specs=[pl.BlockSpec((tm, tk), lambda i, k: (i, k)), pl.BlockSpec((D_MODEL, tk), lambda i, k: (0, k)),
                  row, row, par, par] + [row, par][:n_below],
        out_specs=[row, par, par] + [row, par][:n_below],
        out_shape=[row_shape, par_shape, par_shape] + [jax.ShapeDtypeStruct((S, D_MODEL), BF16), par_shape][:n_below],
        scratch_shapes=[pltpu.VMEM((tm, D_MODEL), F32)],
        compiler_params=_params(("arbitrary", "arbitrary")),
    )(d_proj, wp, x, d_out, scale, shift, *(below[:2] if below is not None else ()))


def _mixer_fwd(x1, scale, shift, gate_w, cos_p, sin_p, W):
    S = x1.shape[0]
    tm = _pick(S, (512, 256, 128))
    tv = _pick(S, TOKEN_ROWS)
    ta = _pick(S, (512, 256, 128))
    nc = S // CHUNK
    (h2,) = _rowwise(_mod_fn, [(x1, tm, D_MODEL, 0)], [scale, shift], [(tm, D_MODEL, BF16)], S // tm, "mix_mod")
    proj = _mm(h2, W["wp"], "nn", name="mix_proj")
    qkvc = _conv_fwd(proj, W["conv_w"], tv, "gdn_conv")
    kab = (proj, tv, 128, 21)
    q_a, k_a, v_a, gb = _rowwise(_gdn_pre_fn, [(qkvc, tv, 1536, 0), kab], [W["alog_p"], W["dt_p"]],
                                 [(tv, 512, F32)] * 3 + [(tv, 128, F32)], S // tv, "gdn_pre")
    ti = _pick(S, INTRA_ROWS)
    intra = _rowwise(_gdn_intra_fn, [(q_a, ti, 512, 0), (k_a, ti, 512, 0), (v_a, ti, 512, 0), (gb, ti, 128, 0)],
                     [], [(ti, 512, F32)] * 4 + [(ti, CHUNK, F32)] * 4 + [(ti // 8, 512, F32)] + [(ti, CHUNK, F32)] * 4,
                     S // ti, "gdn_intra")
    u, wk, qd, kd, qks, gl, invs = intra[0], intra[1], intra[2], intra[3], tuple(intra[4:8]), intra[8], tuple(intra[9:])
    o_a, s_prev = _gdn_scan_fwd(u, wk, qd, kd, qks, gl, "gdn_scan")
    mla_params = [W["qnw"], W["kvnw"], W["wuq"], W["wukv"], W["qn_w"], W["qr_w"], W["kn_w"], W["kr_w"]]
    def mla_pre_with_vt(*a):
        q_, k_, v_ = _mla_pre_fn(*a)
        return q_, k_, v_, jnp.transpose(v_)

    q_b, k_b, v_b, vt_b = _rowwise(mla_pre_with_vt,
                                   [(proj, tv, 256, 8), (proj, tv, 384, 6), kab, (cos_p, tv, 128, 0), (sin_p, tv, 128, 0)],
                                   mla_params, [(tv, 1024, BF16), (tv, 1024, BF16), (tv, 512, BF16), (512, tv, BF16, "across")],
                                   S // tv, "mla_pre")
    o_b, lse = _attn_fwd(q_b, k_b, vt_b, ta, "mla_attn")
    (mixed,) = _rowwise(_mix_post_fn, [(o_a, tv, 512, 0), (proj, tv, 512, 3), (o_b, tv, 512, 0)], [W["gnw"], W["onw"]],
                        [(tv, D_MODEL, BF16)], S // tv, "mix_post")
    y, x2 = _mm(mixed, W["wout"], "nn", name="mix_out", out_dtypes=(BF16, F32), extras=[x1], extra_params=[gate_w],
                epi=lambda acc, x_, g_: (acc, x_ + g_ * acc))
    saved = (h2, proj, qkvc, q_a, k_a, v_a, gb, u, wk, qd, kd, qks, gl, invs, s_prev, o_a, q_b, k_b, v_b, o_b, lse, mixed, y)
    return x2, saved


def _mixer_bwd(d_out, dy, x1, scale, shift, cos_p, sin_p, W, saved, below):
    (h2, proj, qkvc, q_a, k_a, v_a, gb, u, wk, qd, kd, qks, gl, invs, s_prev, o_a, q_b, k_b, v_b, o_b, lse, mixed, y) = saved
    S = x1.shape[0]
    tm = _pick(S, (512, 256, 128))
    tv = _pick(S, TOKEN_ROWS)
    ta = _pick(S, (512, 256, 128))
    nc = S // CHUNK
    G = {}
    d_mixed = _mm(dy, W["wout"], "nt", name="mix_dout")
    G["wout"] = _mm(mixed, dy, "tn", name="mix_gwout")
    (do_a, dz, do_b), (G["gnw"], G["onw"]) = _rowwise_bwd(
        _mix_post_fn, [(o_a, tv, 512, 0), (proj, tv, 512, 3), (o_b, tv, 512, 0)], [], [W["gnw"], W["onw"]],
        [(d_mixed, tv, D_MODEL, 0)], S // tv, "mix_dpost", row_dtypes=(F32, BF16, F32))
    stats = _attn_stats(o_b, lse, do_b, ta, "mla_stats")
    dq_b, dk_b, dv_b = _attn_bwd(q_b, k_b, v_b, do_b, stats, ta, "mla_dattn")
    kab = (proj, tv, 128, 21)
    mla_params = [W["qnw"], W["kvnw"], W["wuq"], W["wukv"], W["qn_w"], W["qr_w"], W["kn_w"], W["kr_w"]]
    (d_ckv, d_cq, d_kab), mla_grads = _rowwise_bwd(
        _mla_pre_fn, [(proj, tv, 256, 8), (proj, tv, 384, 6), kab], [(cos_p, tv, 128, 0), (sin_p, tv, 128, 0)], mla_params,
        [(dq_b, tv, 1024, 0), (dk_b, tv, 1024, 0), (dv_b, tv, 512, 0)], S // tv, "mla_dpre", row_dtypes=(BF16, BF16, F32))
    for key, g in zip(("qnw", "kvnw", "wuq", "wukv", "qn_w", "qr_w", "kn_w", "kr_w"), mla_grads):
        G[key] = g
    scan_grads = _gdn_scan_bwd(u, wk, qd, kd, qks, gl, s_prev, do_a, "gdn_dscan")
    ti = _pick(S, INTRA_ROWS)
    intra_douts = [(scan_grads[i], ti, 512, 0) for i in range(4)] + [(scan_grads[4 + i], ti, CHUNK, 0) for i in range(4)]
    intra_douts.append((scan_grads[8], ti // 8, 512, 0))
    (dq_a, dk_a, dv_a, d_gb), _ = _rowwise_bwd(
        _gdn_intra_fn, [(q_a, ti, 512, 0), (k_a, ti, 512, 0), (v_a, ti, 512, 0), (gb, ti, 128, 0)],
        [(x_, ti, CHUNK, 0) for x_ in invs], [], intra_douts, S // ti, "gdn_dintra")
    (d_qkvc, d_kab), (G["alog_p"], G["dt_p"]) = _rowwise_bwd(
        _gdn_pre_fn, [(qkvc, tv, 1536, 0), kab], [], [W["alog_p"], W["dt_p"]],
        [(dq_a, tv, 512, 0), (dk_a, tv, 512, 0), (dv_a, tv, 512, 0), (d_gb, tv, 128, 0)], S // tv, "gdn_dpre",
        adds=[(1, d_kab)], row_dtypes=(F32, BF16))
    d_qkv, g_conv = _conv_bwd(proj, d_qkvc, W["conv_w"], tv, "gdn_dconv")
    G["conv_w"] = g_conv[:4]
    d_proj = jnp.concatenate([d_qkv, dz, d_ckv, d_cq, d_kab], axis=1)
    G["wp"] = _mm(h2, d_proj, "tn", name="mix_gwp")
    dx1, G["s2"], G["sh2"], d_below, dg_below = _dproj_dmod(d_proj, W["wp"], x1, d_out, scale, shift, "mix_dproj", below=below)
    return dx1, d_below, dg_below, G


def _local_step(x, target, mod, cos_p, sin_p, W1, mixer_weights, ffn2_weights, ffn_grad_ready, mixer_grads_ready):
    sh1, s1, g1, sh2, s2, g2, sh3, s3, g3 = [mod[:, D_MODEL * i:D_MODEL * (i + 1)] for i in range(N_MOD)]
    x1, saved1 = _ffn_fwd(x, s1, sh1, g1, W1["f1_w8"], W1["f1_wo4"], "ffn1")
    W = mixer_weights(x1)
    x2, saved2 = _mixer_fwd(x1, s2, sh2, g2, cos_p, sin_p, W)
    W.update(ffn2_weights(x2))
    (dx3, loss_row), saved3 = _ffn_fwd(x2, s3, sh3, g3, W["f2_w8"], W["f2_wo4"], "ffn2", target=target)
    dx2, d_s3, d_sh3, d_g3, dy, d_g2 = _ffn_bwd(dx3, x2, s3, sh3, g3, W["f2_w8"], W["f2_wo4"], saved3, "ffn2",
                                                ffn_grad_ready("f2"), below=(saved2[-1], g2, 1.0))
    dx1, df1, d_g1, G = _mixer_bwd(dx2, dy, x1, s2, sh2, cos_p, sin_p, W, saved2, below=(saved1[4], g1, 0.5))
    d_sh2, d_s2 = G.pop("sh2"), G.pop("s2")
    saved1 = saved1[:5] + (df1, d_g1 + mixer_grads_ready(G))
    dx, d_s1, d_sh1, d_g1 = _ffn_bwd(dx1, x, s1, sh1, g1, W1["f1_w8"], W1["f1_wo4"], saved1, "ffn1", ffn_grad_ready("f1"))
    d_mod = jnp.concatenate([d_sh1, d_s1, d_g1, d_sh2, d_s2, d_g2, d_sh3, d_s3, d_g3], axis=1)
    return loss_row, dx, d_mod


WEIGHT_NAMES = ("w_ada", "b_ada", "ffn1_w_in", "ffn1_w_out", "w_in", "gdn_conv_w", "gdn_a_log", "gdn_dt_bias", "gdn_norm_w",
                "mla_q_norm_w", "mla_w_uq", "mla_kv_norm_w", "mla_w_ukv", "qkn_q_nope", "qkn_q_rope", "qkn_k_nope",
                "qkn_k_rope", "mla_out_norm_w", "w_out", "ffn2_w_in", "ffn2_w_out")
FFN_SHARDED = ("ffn1_w_in", "ffn1_w_out", "ffn2_w_in", "ffn2_w_out")
TRANSPOSED_ENTRY = ("ffn1_w_in", "ffn2_w_in", "w_in", "mla_w_uq")
SHEETED = (("w_in", "col"), ("gdn_conv_w", "col"), ("mla_w_uq", "col"), ("mla_w_ukv", "col"), ("w_out", "row"))
MOD_ROWS = N_MOD * D_MODEL // 128
SMALL = {"gdn_a_log": (MOD_ROWS, 1, 64, 4), "gdn_dt_bias": (MOD_ROWS + 1, 1, 64, 4), "gdn_norm_w": (MOD_ROWS + 2, 1, 0, 128),
         "mla_q_norm_w": (MOD_ROWS + 3, 3, 0, 384), "mla_kv_norm_w": (MOD_ROWS + 6, 2, 0, 256),
         "qkn_q_nope": (MOD_ROWS + 8, 1, 0, 128), "qkn_q_rope": (MOD_ROWS + 9, 1, 0, 64), "qkn_k_nope": (MOD_ROWS + 10, 1, 0, 128),
         "qkn_k_rope": (MOD_ROWS + 11, 1, 0, 64), "mla_out_norm_w": (MOD_ROWS + 12, 1, 0, 128)}
LOSS_ROW = MOD_ROWS + 13
CONV_ROW, CONV_ROWS = 88, 4 * 1536 // 128
SHEET_ROWS = CONV_ROW + CONV_ROWS


def _to_sheet(flat, dtype, sublanes):
    n = flat.shape[-1]
    unit = sublanes * 128
    pad = (-n) % unit
    flat = jnp.pad(flat.astype(dtype), [(0, 0)] * (flat.ndim - 1) + [(0, pad)])
    return flat.reshape(flat.shape[:-1] + ((n + pad) // 128, 128))


def _small_sheet(b_like, small):
    sheet = jnp.zeros((SHEET_ROWS, 128), F32).at[:MOD_ROWS].set(b_like.reshape(MOD_ROWS, 128))
    for name, (row, rows, lane, n) in SMALL.items():
        v = small[name].reshape(1, n)
        if rows == 1:
            sheet = sheet.at[row, lane:lane + n].set(v[0])
        else:
            sheet = sheet.at[row:row + rows].set(v.reshape(rows, 128))
    return sheet


def _from_small_sheet(sheet):
    out = {"b_ada": sheet[:MOD_ROWS].reshape(1, N_MOD * D_MODEL)}
    for name, (row, rows, lane, n) in SMALL.items():
        out[name] = sheet[row, lane:lane + n].reshape(1, n) if rows == 1 else sheet[row:row + rows].reshape(1, n)
    return out


def kernel(x, c, positions, w_ada, b_ada, ffn1_w_in, ffn1_w_out, w_in, gdn_conv_w, gdn_a_log, gdn_dt_bias, gdn_norm_w, mla_q_norm_w, mla_w_uq, mla_kv_norm_w, mla_w_ukv, qkn_q_nope, qkn_q_rope, qkn_k_nope, qkn_k_rope, mla_out_norm_w, w_out, ffn2_w_in, ffn2_w_out, loss_target, m_w_ada, m_b_ada, m_ffn1_w_in, m_ffn1_w_out, m_w_in, m_gdn_conv_w, m_gdn_a_log, m_gdn_dt_bias, m_gdn_norm_w, m_mla_q_norm_w, m_mla_w_uq, m_mla_kv_norm_w, m_mla_w_ukv, m_qkn_q_nope, m_qkn_q_rope, m_qkn_k_nope, m_qkn_k_rope, m_mla_out_norm_w, m_w_out, m_ffn2_w_in, m_ffn2_w_out, v_w_ada, v_b_ada, v_ffn1_w_in, v_ffn1_w_out, v_w_in, v_gdn_conv_w, v_gdn_a_log, v_gdn_dt_bias, v_gdn_norm_w, v_mla_q_norm_w, v_mla_w_uq, v_mla_kv_norm_w, v_mla_w_ukv, v_qkn_q_nope, v_qkn_q_rope, v_qkn_k_nope, v_qkn_k_rope, v_mla_out_norm_w, v_w_out, v_ffn2_w_in, v_ffn2_w_out):
    args = locals()
    w = {n: args[n] for n in WEIGHT_NAMES}
    m = {n: args["m_" + n] for n in WEIGHT_NAMES}
    v = {n: args["v_" + n] for n in WEIGHT_NAMES}
    me = 4 * lax.axis_index("x") + 2 * lax.axis_index("y") + lax.axis_index("c")
    cols = N_MOD * D_MODEL // N_DEV
    shard = {n: w[n][0] for n in FFN_SHARDED + tuple(s[0] for s in SHEETED)}

    sc = c * _sigmoid(c)
    first = _to_sheet(jnp.concatenate([sc.reshape(-1), shard["gdn_conv_w"].reshape(-1)]), F32, 8)
    (first_all,) = _all_gather([first], "gather_c")
    sc_all = first_all[:, :D_MODEL // 128].reshape(N_DEV, D_MODEL)
    n_taps = shard["gdn_conv_w"].size
    conv_all = first_all.reshape(N_DEV, -1)[:, D_MODEL:D_MODEL + n_taps].reshape(N_DEV, 4, -1)
    b_mine = lax.dynamic_slice(b_ada, (0, me * cols), (1, cols))
    mod_cols = _mm(sc_all, w_ada[0], "nn", name="ada_mod", extra_params=[b_mine], epi=lambda acc, b_: (acc + b_,))
    (mod_all,) = _all_to_all([_to_sheet(mod_cols, F32, 8)], "scatter_mod")
    mod = mod_all.reshape(N_DEV, -1)[:, :cols].reshape(1, N_MOD * D_MODEL)

    f1_shards, mod = lax.optimization_barrier(([shard["ffn1_w_in"].astype(BF16), shard["ffn1_w_out"].astype(BF16)], mod))
    f1_w8, f1_out = _all_gather(f1_shards, "gather_w1")
    travel = [s for s in SHEETED if s[0] != "gdn_conv_w"]
    tied = lax.optimization_barrier(([shard[n].astype(BF16) for n, _ in travel], f1_w8))
    f1_w8 = tied[1]
    mixer_w = _exchange_start(tied[0], False, "gather_wm_start")
    ffn2_w = _exchange_start([shard["ffn2_w_in"].astype(BF16) + mixer_w[4][0:1, 0:1].astype(BF16),
                              shard["ffn2_w_out"].astype(BF16)], False, "gather_w2_start")
    mod = mod + ffn2_w[4][0:1, 0:1]
    W1 = dict(f1_w8=f1_w8, f1_wo4=f1_out.reshape(HID_PIECES, FFN_PIECE, D_MODEL))

    def mixer_weights(after):
        got = _exchange_wait(mixer_w, False, after, "gather_wm_wait")
        P = {n: jnp.concatenate(list(g), axis=1) if kind == "col" else g.reshape(-1, g.shape[-1])
             for (n, kind), g in zip(travel, got)}
        P["gdn_conv_w"] = jnp.concatenate(list(conv_all), axis=1)
        for n in SMALL:
            P[n] = w[n]
        return _pack_weights(P)

    def ffn2_weights(after):
        f2_w8, f2_out = _exchange_wait(ffn2_w, False, after, "gather_w2_wait")
        return dict(f2_w8=f2_w8, f2_wo4=f2_out.reshape(HID_PIECES, FFN_PIECE, D_MODEL))

    pending, small_grads = {}, {}

    def ffn_grad_ready(tag):
        def ready(which, g):
            pieces = g if which == "w8" else g.reshape((N_DEV,) + shard["ffn1_w_out"].shape)
            pending[tag + which] = _exchange_start([pieces], True, "scatter_%s_%s_start" % (tag, which))
            return pending[tag + which][4]
        return ready

    def mixer_grads_ready(G):
        g_full = _unpack_grads(G)
        small_grads.update({n: g_full[n] for n in SMALL})
        small_grads["gdn_conv_w"] = g_full["gdn_conv_w"]
        pieces = []
        for n, kind in travel:
            r, cc = shard[n].shape
            g = g_full[n].astype(BF16)
            pieces.append(jnp.stack([g[:, cc * p:cc * (p + 1)] for p in range(N_DEV)]) if kind == "col"
                          else g.reshape(N_DEV, r, cc))
        pending["mixer"] = _exchange_start(pieces, True, "scatter_mx_start")
        return pending["mixer"][4][0:1, 0:1]

    cos_p, sin_p = _rope_tables(positions[0])
    loss_row, dx, d_mod = _local_step(x[0], loss_target[0], mod, cos_p, sin_p, W1, mixer_weights, ffn2_weights,
                                      ffn_grad_ready, mixer_grads_ready)

    sheet = _small_sheet(d_mod, small_grads).at[LOSS_ROW].set(loss_row[0, :128])
    sheet = sheet.at[CONV_ROW:CONV_ROW + CONV_ROWS].set(small_grads["gdn_conv_w"].reshape(CONV_ROWS, 128))
    (sheets,) = _all_gather([sheet], "gather_small")
    summed = _sum_devices(sheets, "sum_small")
    d_mod_all = sheets[:, :MOD_ROWS].reshape(N_DEV, N_MOD * D_MODEL)
    d_mod_mine = lax.dynamic_slice(d_mod_all, (0, me * cols), (N_DEV, cols))
    grads = _from_small_sheet(summed)
    grads["w_ada"] = _mm(sc_all, d_mod_mine, "tn", name="ada_gw", hi=True)
    conv_taps = shard["gdn_conv_w"].shape[1]
    grads["gdn_conv_w"] = lax.dynamic_slice(summed[CONV_ROW:CONV_ROW + CONV_ROWS].reshape(4, -1), (0, me * conv_taps),
                                            (4, conv_taps))
    loss = summed[LOSS_ROW, 0]

    delta, new_m, new_v = {}, {}, {}
    arrived = {}
    for n, key in zip(FFN_SHARDED, ("f1w8", "f1wo4", "f2w8", "f2wo4")):
        (arrived[n],) = _exchange_wait(pending[key], True, summed, "scatter_%s_wait" % key)
    arrived.update(zip([n for n, _ in travel], _exchange_wait(pending["mixer"], True, summed, "scatter_mx_wait")))
    for n, parts in arrived.items():
        if n in TRANSPOSED_ENTRY:
            res = _sum_adamw(parts, w[n][0].T, m[n][0].T, v[n][0].T, "adamw_" + n, transposed=True)
            grads[n], delta[n], new_m[n], new_v[n] = [r.T for r in res]
        else:
            grads[n], delta[n], new_m[n], new_v[n] = _sum_adamw(parts, w[n][0], m[n][0], v[n][0], "adamw_" + n)
    for n in ("w_ada", "gdn_conv_w"):
        delta[n], new_m[n], new_v[n] = _adamw(w[n][0], grads[n], m[n][0], v[n][0], "adamw_" + n)
    small_in = [_small_sheet(t["b_ada"], t) for t in (w, grads, m, v)]
    for res, out in zip(_adamw(*small_in, "adamw_small"), (delta, new_m, new_v)):
        out.update(_from_small_sheet(res))

    def shaped(d):
        return [d[n].reshape(w[n].shape) for n in WEIGHT_NAMES]

    return (loss, dx[None], *shaped(grads), *shaped(delta), *shaped(new_m), *shaped(new_v))
```

```python
import functools

import jax
import jax.numpy as jnp
import numpy as np
from jax import lax
from jax.experimental import pallas as pl
from jax.experimental.pallas import tpu as pltpu

F32 = jnp.float32
BF16 = jnp.bfloat16

D_MODEL = 1024
D_FF = 2816
N_MOD = 9
HEADS = 4
HEAD_DIM = 128
CHUNK = 64
EPS = 1e-6
ROPE = 64
Q_LORA = 384
KV_LORA = 256
N_IN = 2760
N_IN_PACKED = 2816
ROPE_BASE = 10000.0
LOG2_E = 1.4426950408889634
N_DEV = 8

ADAM_LR = 0.001
ADAM_B1 = 0.9
ADAM_B2 = 0.999
ADAM_EPS = 1e-08
ADAM_WD = 0.01
ADAM_STEP = 10

VMEM_LIMIT_BYTES = 56 * 1024 * 1024
MATMUL_ROWS = (1024, 512, 256, 128)
MESH = pl.DeviceIdType.MESH


def _params(sem=None):
    return pltpu.CompilerParams(dimension_semantics=sem, vmem_limit_bytes=VMEM_LIMIT_BYTES)


def _pick(dim, prefs):
    for p in prefs:
        if dim % p == 0:
            return p
    return dim


_DIMS = {"nn": (((1,), (0,)), ((), ())), "nt": (((1,), (1,)), ((), ())), "tn": (((0,), (0,)), ((), ()))}


def _dot_raw(a, b, mode):
    return lax.dot_general(a.astype(BF16), b.astype(BF16), _DIMS[mode], preferred_element_type=F32)


def _dot_hi(a, b, mode="nn"):
    return lax.dot_general(a, b, _DIMS[mode], precision=lax.Precision.HIGHEST, preferred_element_type=F32)


@functools.partial(jax.custom_vjp, nondiff_argnums=(2,))
def _bdot(a, b, mode):
    return _dot_raw(a, b, mode)


def _bdot_fwd(a, b, mode):
    return _dot_raw(a, b, mode), (a, b)


def _bdot_bwd(mode, res, g):
    a, b = res
    if mode == "nn":
        return _dot_raw(g, b, "nt"), _dot_raw(a, g, "tn")
    if mode == "nt":
        return _dot_raw(g, b, "nn"), _dot_raw(g, a, "tn")
    return _dot_raw(b, g, "nt"), _dot_raw(a, g, "nn")


_bdot.defvjp(_bdot_fwd, _bdot_bwd)


def _mm(a, b, mode, *, name, out_dtypes=(F32,), epi=None, extras=(), extra_params=(), hi=False,
        tm=None, tn=None, tk=None):
    if mode == "nn":
        (M, K), (_, N) = a.shape, b.shape
    elif mode == "nt":
        (M, K), (N, _) = a.shape, b.shape
    else:
        (K, M), (_, N) = a.shape, b.shape
    tm = tm or _pick(M, (512, 1408, 256, 128) if mode == "tn" else MATMUL_ROWS + (384, 352))
    tn = tn or _pick(N, (1024, 1408, 768, 512, 384, 256, 128))
    tk = tk or _pick(K, (1024, 1408, 512, 384, 256, 128))
    a_spec = {"nn": pl.BlockSpec((tm, tk), lambda i, j, k: (i, k)), "nt": pl.BlockSpec((tm, tk), lambda i, j, k: (i, k)),
              "tn": pl.BlockSpec((tk, tm), lambda i, j, k: (k, i))}[mode]
    b_spec = {"nn": pl.BlockSpec((tk, tn), lambda i, j, k: (k, j)), "nt": pl.BlockSpec((tn, tk), lambda i, j, k: (j, k)),
              "tn": pl.BlockSpec((tk, tn), lambda i, j, k: (k, j))}[mode]
    mn_spec = pl.BlockSpec((tm, tn), lambda i, j, k: (i, j))
    return _mmg(a, b, mode, name=name, grid=(M // tm, N // tn, K // tk), a_spec=a_spec, b_spec=b_spec, out_spec=mn_spec,
                out_shapes=[jax.ShapeDtypeStruct((M, N), dt) for dt in out_dtypes], acc_shape=(tm, tn), epi=epi,
                extras=list(extras) + list(extra_params),
                extra_specs=[mn_spec] * len(extras) + [pl.BlockSpec((1, tn), lambda i, j, k: (0, j))] * len(extra_params),
                hi=hi)


def _mmg(a, b, mode, *, name, grid, a_spec, b_spec, out_spec, out_shapes, acc_shape, epi=None, extras=(),
         extra_specs=(), hi=False):
    nk = grid[2]
    n_e, n_o = len(extras), len(out_shapes)

    def body(*refs):
        a_ref, b_ref = refs[:2]
        e_refs = refs[2:2 + n_e]
        o_refs = refs[2 + n_e:2 + n_e + n_o]
        acc_ref = refs[-1]
        k = pl.program_id(2)

        @pl.when(k == 0)
        def _():
            acc_ref[...] = jnp.zeros_like(acc_ref)

        if hi:
            acc_ref[...] += _dot_hi(a_ref[...].astype(F32), b_ref[...].astype(F32), mode)
        else:
            acc_ref[...] += _dot_raw(a_ref[...], b_ref[...], mode)

        @pl.when(k == nk - 1)
        def _():
            acc = acc_ref[...]
            outs = (acc,) if epi is None else epi(acc, *[e[...].astype(F32) for e in e_refs])
            for o_ref, o in zip(o_refs, outs):
                o_ref[...] = o.astype(o_ref.dtype)

    outs = pl.pallas_call(
        body, name=name, grid=grid,
        in_specs=[a_spec, b_spec] + list(extra_specs),
        out_specs=[out_spec] * n_o,
        out_shape=list(out_shapes),
        scratch_shapes=[pltpu.VMEM(acc_shape, F32)],
        compiler_params=_params(("parallel", "parallel", "arbitrary")),
    )(a, b, *extras)
    return outs if n_o > 1 else outs[0]


def _row_spec(th, cw, ci):
    return pl.BlockSpec((th, cw), lambda i: (i, ci))


def _full_spec(shape):
    return pl.BlockSpec(shape, lambda i: (0,) * len(shape))


def _rowwise(fn, rows, params, outs, n_steps, name):
    n_r, n_p, n_o = len(rows), len(params), len(outs)

    def body(*refs):
        vals = [r[...].astype(F32) for r in refs[:n_r + n_p]]
        res = fn(*vals)
        for o_ref, o in zip(refs[n_r + n_p:], res):
            o_ref[...] = o.astype(o_ref.dtype)

    across = [len(o) == 4 for o in outs]
    res = pl.pallas_call(
        body, name=name, grid=(n_steps,),
        in_specs=[_row_spec(th, cw, ci) for (_, th, cw, ci) in rows] + [_full_spec(p.shape) for p in params],
        out_specs=[pl.BlockSpec((o[0], o[1]), lambda i: (0, i)) if ac else _row_spec(o[0], o[1], 0)
                   for o, ac in zip(outs, across)],
        out_shape=[jax.ShapeDtypeStruct((o[0], n_steps * o[1]) if ac else (n_steps * o[0], o[1]), o[2])
                   for o, ac in zip(outs, across)],
        compiler_params=_params(("parallel",)),
    )(*[r[0] for r in rows], *params)
    return res


def _rowwise_bwd(fn, rows, aux, params, douts, n_steps, name, row_dtypes=None, adds=()):
    n_r, n_a, n_p, n_d, n_add = len(rows), len(aux), len(params), len(douts), len(adds)
    row_dtypes = row_dtypes or (F32,) * n_r

    def body(*refs):
        it = iter(refs)
        r_vals = [next(it)[...].astype(F32) for _ in range(n_r)]
        a_vals = [next(it)[...].astype(F32) for _ in range(n_a)]
        p_vals = [next(it)[...].astype(F32) for _ in range(n_p)]
        d_vals = [next(it)[...].astype(F32) for _ in range(n_d)]
        add_vals = [next(it)[...].astype(F32) for _ in range(n_add)]
        dr_refs = [next(it) for _ in range(n_r)]
        dp_refs = [next(it) for _ in range(n_p)]

        def f(*rp):
            return tuple(fn(*rp[:n_r], *a_vals, *rp[n_r:]))

        _, vjp = jax.vjp(f, *r_vals, *p_vals)
        grads = list(vjp(tuple(d_vals)))
        for (ri, _), av in zip(adds, add_vals):
            grads[ri] = grads[ri] + av
        for dr_ref, g in zip(dr_refs, grads[:n_r]):
            dr_ref[...] = g.astype(dr_ref.dtype)

        @pl.when(pl.program_id(0) == 0)
        def _():
            for dp_ref in dp_refs:
                dp_ref[...] = jnp.zeros_like(dp_ref)

        for dp_ref, g in zip(dp_refs, grads[n_r:]):
            dp_ref[...] += g

    all_rows = list(rows) + list(aux) + list(douts) + [(arr,) + tuple(rows[ri][1:3]) + (0,) for ri, arr in adds]
    in_specs = ([_row_spec(th, cw, ci) for (_, th, cw, ci) in list(rows) + list(aux)]
                + [_full_spec(p.shape) for p in params]
                + [_row_spec(th, cw, ci) for (_, th, cw, ci) in all_rows[n_r + n_a:]])
    res = pl.pallas_call(
        body, name=name, grid=(n_steps,),
        in_specs=in_specs,
        out_specs=[_row_spec(th, cw, 0) for (_, th, cw, _) in rows] + [_full_spec(p.shape) for p in params],
        out_shape=[jax.ShapeDtypeStruct((n_steps * th, cw), dt) for (_, th, cw, _), dt in zip(rows, row_dtypes)]
        + [jax.ShapeDtypeStruct(p.shape, F32) for p in params],
        compiler_params=_params(("arbitrary",)),
    )(*[r[0] for r in list(rows) + list(aux)], *params, *[r[0] for r in all_rows[n_r + n_a:]])
    return res[:n_r], res[n_r:]


def _sigmoid(x):
    return lax.logistic(x)


def _silu(x):
    return x * _sigmoid(x)


def _rms(x, w=None, n=None):
    n = n or x.shape[-1]
    y = x * lax.rsqrt(jnp.sum(x * x, axis=-1, keepdims=True) * (1.0 / n) + EPS)
    return y if w is None else y * w


def _modulate(x, scale, shift):
    return _rms(x) * (1.0 + scale) + shift


def _softplus(x):
    return jnp.maximum(x, 0.0) + jnp.log1p(jnp.exp(-jnp.abs(x)))


@jax.custom_vjp
def _rot_half64(x):
    lane = lax.broadcasted_iota(jnp.int32, x.shape, 1)
    up = pltpu.roll(x, 96, 1)
    down = pltpu.roll(x, 32, 1)
    return jnp.where(lane < 32, up, jnp.where(lane < 64, down, 0.0))


_rot_half64.defvjp(lambda x: (_rot_half64(x), None), lambda _, g: (_rot_half64(g),))


def _rope128(x, cos_p, sin_p):
    return x * cos_p + _rot_half64(x) * sin_p


def _gdn_pre_fn(qkvc, kab, alog_p, dt_p):
    a = _silu(qkvc)
    qs, ks = [], []
    for h in range(HEADS):
        qh = a[:, HEAD_DIM * h:HEAD_DIM * (h + 1)]
        kh = a[:, 512 + HEAD_DIM * h:512 + HEAD_DIM * (h + 1)]
        qs.append(qh * lax.rsqrt(jnp.sum(qh * qh, axis=-1, keepdims=True) + EPS) * (HEAD_DIM ** -0.5))
        ks.append(kh * lax.rsqrt(jnp.sum(kh * kh, axis=-1, keepdims=True) + EPS))
    lane = lax.broadcasted_iota(jnp.int32, kab.shape, 1)
    g_full = -jnp.exp(alog_p) * _softplus(kab + dt_p)
    b_full = _sigmoid(kab)
    gb = jnp.where((lane >= 64) & (lane < 68), g_full, jnp.where((lane >= 68) & (lane < 72), b_full, 0.0))
    return jnp.concatenate(qs, axis=1), jnp.concatenate(ks, axis=1), a[:, 1024:1536], gb


INTRA_ROWS = (512, 256, 128, 64)
TOKEN_ROWS = (512, 256, 128)

_BNN = (((2,), (1,)), ((0,), (0,)))
_BNT = (((2,), (2,)), ((0,), (0,)))
_BTN = (((1,), (1,)), ((0,), (0,)))


def _split_bf16(a):
    hi = a.astype(BF16)
    return hi, (a - hi.astype(F32)).astype(BF16)


def _dot3_raw(a, b, dims):
    a_hi, a_lo = _split_bf16(a)
    b_hi, b_lo = _split_bf16(b)
    dot = lambda x_, y_: lax.dot_general(x_, y_, dims, preferred_element_type=F32)
    return dot(a_hi, b_hi) + (dot(a_hi, b_lo) + dot(a_lo, b_hi))


@functools.partial(jax.custom_vjp, nondiff_argnums=(2, 3))
def _dot3(a, b, nt, exact_bwd=True):
    return _dot3_raw(a, b, _BNT if nt else _BNN)


def _dot3_fwd(a, b, nt, exact_bwd):
    return _dot3_raw(a, b, _BNT if nt else _BNN), (a, b)


def _dot3_bwd(nt, exact_bwd, res, g):
    a, b = res
    if exact_bwd:
        dot = _dot3_raw
    else:
        dot = lambda x_, y_, d_: lax.dot_general(x_.astype(BF16), y_.astype(BF16), d_, preferred_element_type=F32)
    if nt:
        return dot(g, b, _BNN), dot(jnp.swapaxes(g, 1, 2), a, _BNN)
    return dot(g, b, _BNT), dot(jnp.swapaxes(a, 1, 2), g, _BNN)


_dot3.defvjp(_dot3_fwd, _dot3_bwd)


@functools.partial(jax.custom_vjp, nondiff_argnums=(2,))
def _bdot_b(a, b, nt):
    return lax.dot_general(a.astype(BF16), b.astype(BF16), _BNT if nt else _BNN, preferred_element_type=F32)


def _bdot_b_fwd(a, b, nt):
    return _bdot_b(a, b, nt), (a, b)


def _bdot_b_bwd(nt, res, g):
    a, b = res
    dot = lambda x_, y_, d_: lax.dot_general(x_.astype(BF16), y_.astype(BF16), d_, preferred_element_type=F32)
    if nt:
        return dot(g, b, _BNN), dot(jnp.swapaxes(g, 1, 2), a, _BNN)
    return dot(g, b, _BNT), dot(jnp.swapaxes(a, 1, 2), g, _BNN)


_bdot_b.defvjp(_bdot_b_fwd, _bdot_b_bwd)


@jax.custom_vjp
def _inverse_given(a_mat, inv):
    return inv


def _inverse_given_bwd(inv, g):
    inv_t = jnp.swapaxes(inv, 1, 2)
    return -_dot3_raw(_dot3_raw(inv_t, g, _BNN), inv_t, _BNN), jnp.zeros_like(inv)


_inverse_given.defvjp(lambda a_mat, inv: (inv, inv), _inverse_given_bwd)


def _intra_batched(q, k, v, g_col, b_col, inv_known=None):
    c = CHUNK
    nb = q.shape[0]
    row = lax.broadcasted_iota(jnp.int32, (1, c, c), 1)
    col = lax.broadcasted_iota(jnp.int32, (1, c, c), 2)
    incl, strict, eye = row >= col, row > col, row == col
    tri = jnp.broadcast_to(jnp.where(incl, 1.0, 0.0).astype(F32), (nb, c, c))
    ident = jnp.where(eye, 1.0, 0.0).astype(F32)
    g_wide = _dot3(tri, jnp.broadcast_to(g_col, (nb, c, HEAD_DIM)), False)
    g_i = g_wide[:, :, :c]
    g_j = jnp.sum(jnp.where(eye, g_i, 0.0), axis=1, keepdims=True)
    decay = jnp.where(incl, jnp.exp(jnp.where(incl, g_i - g_j, 0.0)), 0.0)
    kk = _bdot_b(k, k, True)
    a_mat = jnp.where(strict, b_col * kk * decay, 0.0)
    if inv_known is None:
        x_pow = -a_mat
        inv = ident + x_pow
        for _ in range(5):
            x_pow = _dot3(x_pow, x_pow, False, False)
            inv = inv + _dot3(inv, x_pow, False, False)
    else:
        inv = _inverse_given(a_mat, inv_known)
    e_wide = jnp.exp(g_wide)
    u = _dot3(inv, v * b_col, False)
    wk = _dot3(inv, k * b_col * e_wide, False)
    qk = _bdot_b(q, k, True) * decay
    last = lax.broadcasted_iota(jnp.int32, (1, c, HEAD_DIM), 1) == c - 1
    g_last = jnp.sum(jnp.where(last, g_wide, 0.0), axis=1, keepdims=True)
    qd = q * e_wide
    kd = k * jnp.exp(g_last - g_wide)
    gl = jnp.broadcast_to(jnp.exp(g_last), (nb, 8, HEAD_DIM))
    return u, wk, qd, kd, qk, gl, inv


def _gdn_intra_fn(q, k, v, gb, *inv_known):
    t = q.shape[0]
    nch = t // CHUNK
    lane = lax.broadcasted_iota(jnp.int32, gb.shape, 1)

    def heads_first(x_):
        return jnp.concatenate([x_[:, HEAD_DIM * h:HEAD_DIM * (h + 1)].reshape(nch, CHUNK, HEAD_DIM) for h in range(HEADS)],
                               axis=0)

    def column(first_lane):
        return jnp.concatenate([jnp.sum(jnp.where(lane == first_lane + h, gb, 0.0), axis=1, keepdims=True)
                                .reshape(nch, CHUNK, 1) for h in range(HEADS)], axis=0)

    known = jnp.concatenate([x_.reshape(nch, CHUNK, CHUNK) for x_ in inv_known], axis=0) if inv_known else None
    u, wk, qd, kd, qk, gl, inv = _intra_batched(heads_first(q), heads_first(k), heads_first(v), column(64), column(68), known)

    def rows_first(x_):
        r, w_ = x_.shape[1], x_.shape[2]
        return jnp.concatenate([x_[nch * h:nch * (h + 1)].reshape(nch * r, w_) for h in range(HEADS)], axis=1)

    per_head = lambda x_: [x_[nch * h:nch * (h + 1)].reshape(t, CHUNK) for h in range(HEADS)]
    outs = (rows_first(u), rows_first(wk), rows_first(qd), rows_first(kd), *per_head(qk), rows_first(gl))
    return outs if inv_known else outs + tuple(per_head(inv))


def _scan_step(s0, u, wk, qd, kd, qk, gl):
    v_new = u - _bdot_b(wk, s0, False)
    o = _bdot_b(qd, s0, False) + _bdot_b(qk, v_new, False)
    s1 = s0 * gl[:, 0:1, :] + _bdot_b(jnp.swapaxes(kd, 1, 2), v_new, False)
    return o, s1


def _mix_post_fn(o_a, z, o_b, gnw, onw):
    parts = [_rms(o_a[:, HEAD_DIM * h:HEAD_DIM * (h + 1)], gnw) * _silu(z[:, HEAD_DIM * h:HEAD_DIM * (h + 1)])
             for h in range(HEADS)]
    parts += [_rms(o_b[:, HEAD_DIM * h:HEAD_DIM * (h + 1)], onw) for h in range(HEADS)]
    return (jnp.concatenate(parts, axis=1),)


def _mla_pre_fn(ckv, cq, kab, cos_p, sin_p, qnw, kvnw, wuq, wukv, qn_w, qr_w, kn_w, kr_w):
    scale = (HEAD_DIM + ROPE) ** -0.5 * LOG2_E
    qf = _bdot(_rms(cq, qnw), wuq, "nn")
    kvf = _bdot(_rms(ckv, kvnw), wukv, "nn")
    lane = lax.broadcasted_iota(jnp.int32, kab.shape, 1)
    kr = _rope128(_rms(jnp.where(lane < ROPE, kab, 0.0), kr_w, n=ROPE), cos_p, sin_p)
    qs, ks = [], []
    for h in range(HEADS):
        qn = _rms(qf[:, 256 * h:256 * h + 128], qn_w) * scale
        qr = _rope128(_rms(qf[:, 256 * h + 128:256 * h + 256], qr_w, n=ROPE), cos_p, sin_p) * scale
        qs += [qn, qr]
        ks += [_rms(kvf[:, 128 * h:128 * (h + 1)], kn_w), kr]
    return jnp.concatenate(qs, axis=1), jnp.concatenate(ks, axis=1), kvf[:, 512:]


def _conv_fwd(proj, conv_w, tm, name):
    S = proj.shape[0]
    C = 1536
    nb = tm // 8

    def body(x_ref, prev_ref, w_ref, o_ref, ext_ref):
        i = pl.program_id(0)
        ext_ref[0:8, :] = jnp.where(i > 0, prev_ref[...], 0.0)
        ext_ref[8:, :] = x_ref[...]
        acc = jnp.zeros((tm, C), F32)
        for k in range(4):
            acc = acc + w_ref[k:k + 1, :] * ext_ref[pl.ds(5 + k, tm), :]
        o_ref[...] = acc

    return pl.pallas_call(
        body, name=name, grid=(S // tm,),
        in_specs=[pl.BlockSpec((tm, C), lambda i: (i, 0)),
                  pl.BlockSpec((8, C), lambda i: (jnp.maximum(i * nb - 1, 0), 0)),
                  pl.BlockSpec((4, C), lambda i: (0, 0))],
        out_specs=pl.BlockSpec((tm, C), lambda i: (i, 0)),
        out_shape=jax.ShapeDtypeStruct((S, C), F32),
        scratch_shapes=[pltpu.VMEM((tm + 8, C), F32)],
        compiler_params=_params(("arbitrary",)),
    )(proj, proj, conv_w)


def _conv_bwd(proj, dout, conv_w, tm, name):
    S = proj.shape[0]
    C = 1536
    nb = tm // 8
    n_steps = S // tm

    def body(x_ref, prev_ref, d_ref, next_ref, w_ref, dx_ref, dw_ref, xext_ref, dext_ref):
        i = pl.program_id(0)
        xext_ref[0:8, :] = jnp.where(i > 0, prev_ref[...], 0.0)
        xext_ref[8:, :] = x_ref[...]
        dext_ref[0:tm, :] = d_ref[...]
        dext_ref[tm:, :] = jnp.where(i < n_steps - 1, next_ref[...], 0.0)
        d = d_ref[...]
        acc = jnp.zeros((tm, C), F32)
        dws = []
        for k in range(4):
            acc = acc + w_ref[k:k + 1, :] * dext_ref[pl.ds(3 - k, tm), :]
            dws.append(jnp.sum(d * xext_ref[pl.ds(5 + k, tm), :], axis=0, keepdims=True))
        dx_ref[...] = acc.astype(dx_ref.dtype)

        @pl.when(i == 0)
        def _():
            dw_ref[...] = jnp.zeros_like(dw_ref)

        dw_ref[...] += jnp.concatenate(dws + [jnp.zeros((4, C), F32)], axis=0)

    return pl.pallas_call(
        body, name=name, grid=(n_steps,),
        in_specs=[pl.BlockSpec((tm, C), lambda i: (i, 0)),
                  pl.BlockSpec((8, C), lambda i: (jnp.maximum(i * nb - 1, 0), 0)),
                  pl.BlockSpec((tm, C), lambda i: (i, 0)),
                  pl.BlockSpec((8, C), lambda i: (jnp.minimum((i + 1) * nb, S // 8 - 1), 0)),
                  pl.BlockSpec((4, C), lambda i: (0, 0))],
        out_specs=[pl.BlockSpec((tm, C), lambda i: (i, 0)), pl.BlockSpec((8, C), lambda i: (0, 0))],
        out_shape=[jax.ShapeDtypeStruct((S, C), BF16), jax.ShapeDtypeStruct((8, C), F32)],
        scratch_shapes=[pltpu.VMEM((tm + 8, C), F32), pltpu.VMEM((tm + 8, C), F32)],
        compiler_params=_params(("arbitrary",)),
    )(proj, proj, dout, dout, conv_w)


SCAN_CHUNKS = (8, 4, 2, 1)


def _gdn_scan_fwd(u, wk, qd, kd, qks, gl, name):
    S = u.shape[0]
    nc = S // CHUNK
    cs = _pick(nc, SCAN_CHUNKS)
    W = HEADS * HEAD_DIM

    def body(u_ref, wk_ref, qd_ref, kd_ref, qk0, qk1, qk2, qk3, gl_ref, o_ref, sp_ref, s_ref):
        @pl.when(pl.program_id(0) == 0)
        def _():
            s_ref[...] = jnp.zeros_like(s_ref)

        state = s_ref[...]
        for c in range(cs):
            rows, gl_rows = slice(CHUNK * c, CHUNK * (c + 1)), slice(8 * c, 8 * (c + 1))
            sp_ref[c] = state
            o, state = _scan_step(state, _heads(u_ref, HEAD_DIM, rows), _heads(wk_ref, HEAD_DIM, rows),
                                  _heads(qd_ref, HEAD_DIM, rows), _heads(kd_ref, HEAD_DIM, rows),
                                  jnp.stack([r[rows, :] for r in (qk0, qk1, qk2, qk3)]), _heads(gl_ref, HEAD_DIM, gl_rows))
            for h in range(HEADS):
                o_ref[rows, HEAD_DIM * h:HEAD_DIM * (h + 1)] = o[h]
        s_ref[...] = state

    row = pl.BlockSpec((cs * CHUNK, W), lambda n: (n, 0))
    qk_spec = pl.BlockSpec((cs * CHUNK, CHUNK), lambda n: (n, 0))
    return pl.pallas_call(
        body, name=name, grid=(nc // cs,),
        in_specs=[row, row, row, row, qk_spec, qk_spec, qk_spec, qk_spec, pl.BlockSpec((cs * 8, W), lambda n: (n, 0))],
        out_specs=[row, pl.BlockSpec((cs, HEADS, HEAD_DIM, HEAD_DIM), lambda n: (n, 0, 0, 0))],
        out_shape=[jax.ShapeDtypeStruct((S, W), F32), jax.ShapeDtypeStruct((nc, HEADS, HEAD_DIM, HEAD_DIM), F32)],
        scratch_shapes=[pltpu.VMEM((HEADS, HEAD_DIM, HEAD_DIM), F32)],
        compiler_params=_params(("arbitrary",)),
    )(u, wk, qd, kd, *qks, gl)


def _gdn_scan_bwd(u, wk, qd, kd, qks, gl, s_prev, d_o, name):
    S = u.shape[0]
    nc = S // CHUNK
    cs = _pick(nc, SCAN_CHUNKS)
    nb = nc // cs
    W = HEADS * HEAD_DIM

    def body(u_ref, wk_ref, qd_ref, kd_ref, qk0, qk1, qk2, qk3, gl_ref, sp_ref, do_ref,
             du_ref, dwk_ref, dqd_ref, dkd_ref, dqk0, dqk1, dqk2, dqk3, dgl_ref, ds_ref):
        @pl.when(pl.program_id(0) == 0)
        def _():
            ds_ref[...] = jnp.zeros_like(ds_ref)

        d_state = ds_ref[...]
        for c in reversed(range(cs)):
            rows, gl_rows = slice(CHUNK * c, CHUNK * (c + 1)), slice(8 * c, 8 * (c + 1))
            _, vjp = jax.vjp(_scan_step, sp_ref[c], _heads(u_ref, HEAD_DIM, rows), _heads(wk_ref, HEAD_DIM, rows),
                             _heads(qd_ref, HEAD_DIM, rows), _heads(kd_ref, HEAD_DIM, rows),
                             jnp.stack([r[rows, :] for r in (qk0, qk1, qk2, qk3)]), _heads(gl_ref, HEAD_DIM, gl_rows))
            d_state, du, dwk, dqd, dkd, dqk, dgl = vjp((_heads(do_ref, HEAD_DIM, rows), d_state))
            for h, dqk_ref in enumerate((dqk0, dqk1, dqk2, dqk3)):
                sl = slice(HEAD_DIM * h, HEAD_DIM * (h + 1))
                du_ref[rows, sl] = du[h]
                dwk_ref[rows, sl] = dwk[h]
                dqd_ref[rows, sl] = dqd[h]
                dkd_ref[rows, sl] = dkd[h]
                dqk_ref[rows, :] = dqk[h]
                dgl_ref[gl_rows, sl] = dgl[h]
        ds_ref[...] = d_state

    rev = lambda n: (nb - 1 - n, 0)
    row = pl.BlockSpec((cs * CHUNK, W), rev)
    qk_spec = pl.BlockSpec((cs * CHUNK, CHUNK), rev)
    gl_spec = pl.BlockSpec((cs * 8, W), rev)
    qk_shape = jax.ShapeDtypeStruct((S, CHUNK), F32)
    row_shape = jax.ShapeDtypeStruct((S, W), F32)
    return pl.pallas_call(
        body, name=name, grid=(nb,),
        in_specs=[row, row, row, row, qk_spec, qk_spec, qk_spec, qk_spec, gl_spec,
                  pl.BlockSpec((cs, HEADS, HEAD_DIM, HEAD_DIM), lambda n: (nb - 1 - n, 0, 0, 0)), row],
        out_specs=[row, row, row, row, qk_spec, qk_spec, qk_spec, qk_spec, gl_spec],
        out_shape=[row_shape] * 4 + [qk_shape] * 4 + [jax.ShapeDtypeStruct((nc * 8, W), F32)],
        scratch_shapes=[pltpu.VMEM((HEADS, HEAD_DIM, HEAD_DIM), F32)],
        compiler_params=_params(("arbitrary",)),
    )(u, wk, qd, kd, *qks, gl, s_prev, d_o)


NEG = -1e30


def _chunk_mask(i, j, t, transposed=False):
    q_axis, k_axis = (1, 0) if transposed else (0, 1)
    r = (i * t + lax.broadcasted_iota(jnp.int32, (t, t), q_axis)) // CHUNK
    c = (j * t + lax.broadcasted_iota(jnp.int32, (t, t), k_axis)) // CHUNK
    return c <= r


def _tile_pairs(n, by_key):
    pairs = [(i, j) for j in range(n) for i in range(j, n)] if by_key else [(i, j) for i in range(n) for j in range(i + 1)]
    return jnp.asarray(np.array([p[0] for p in pairs], np.int32)), jnp.asarray(np.array([p[1] for p in pairs], np.int32))


def _heads(ref, width, rows=slice(None)):
    return jnp.stack([ref[rows, width * h:width * (h + 1)] for h in range(HEADS)])


def _bmm(a, b, dims):
    return lax.dot_general(a.astype(BF16), b.astype(BF16), dims, preferred_element_type=F32)


def _attn_fwd(q, k, v_t, t, name):
    S = q.shape[0]
    n = S // t
    qi, kj = _tile_pairs(n, by_key=False)

    def body(qi_ref, kj_ref, q_ref, k_ref, vt_ref, o_ref, lse_ref, m_ref, l_ref, acc_ref):
        i, j = qi_ref[pl.program_id(0)], kj_ref[pl.program_id(0)]

        @pl.when(j == 0)
        def _():
            m_ref[...] = jnp.full_like(m_ref, NEG)
            l_ref[...] = jnp.zeros_like(l_ref)
            acc_ref[...] = jnp.zeros_like(acc_ref)

        def update(masked):
            s_t = _bmm(_heads(k_ref, 256), _heads(q_ref, 256), _BNT)
            if masked:
                s_t = jnp.where(_chunk_mask(i, j, t, transposed=True)[None], s_t, NEG)
            m_old = m_ref[...]
            m_new = jnp.maximum(m_old, jnp.max(s_t, axis=1, keepdims=True))
            p_t = jnp.exp2(s_t - m_new)
            alpha = jnp.exp2(m_old - m_new)
            l_ref[...] = alpha * l_ref[...] + jnp.sum(p_t, axis=1, keepdims=True)
            v_heads = jnp.stack([vt_ref[HEAD_DIM * h:HEAD_DIM * (h + 1), :] for h in range(HEADS)])
            acc_ref[...] = alpha * acc_ref[...] + _bmm(v_heads, p_t, _BNN)
            m_ref[...] = m_new

        @pl.when(j < i)
        def _():
            update(False)

        @pl.when(j == i)
        def _():
            update(True)
            for h in range(HEADS):
                sl = slice(HEAD_DIM * h, HEAD_DIM * (h + 1))
                o_ref[:, sl] = jnp.transpose(acc_ref[h] / l_ref[h])
                lse_ref[:, sl] = jnp.transpose(jnp.broadcast_to(m_ref[h] + jnp.log(l_ref[h]) * LOG2_E, (HEAD_DIM, t)))

    row = lambda p, qi_, kj_: (qi_[p], 0)
    return pl.pallas_call(
        body, name=name,
        grid_spec=pltpu.PrefetchScalarGridSpec(
            num_scalar_prefetch=2, grid=(qi.shape[0],),
            in_specs=[pl.BlockSpec((t, HEADS * 256), row), pl.BlockSpec((t, HEADS * 256), lambda p, qi_, kj_: (kj_[p], 0)),
                      pl.BlockSpec((HEADS * HEAD_DIM, t), lambda p, qi_, kj_: (0, kj_[p]))],
            out_specs=[pl.BlockSpec((t, HEADS * HEAD_DIM), row)] * 2,
            scratch_shapes=[pltpu.VMEM((HEADS, 1, t), F32), pltpu.VMEM((HEADS, 1, t), F32),
                            pltpu.VMEM((HEADS, HEAD_DIM, t), F32)]),
        out_shape=[jax.ShapeDtypeStruct((S, HEADS * HEAD_DIM), F32)] * 2,
        compiler_params=_params(("arbitrary",)),
    )(qi, kj, q, k, v_t)


def _attn_stats(o, lse, d_o, t, name):
    S = o.shape[0]

    def body(o_ref, lse_ref, do_ref, st_ref):
        lane = lax.broadcasted_iota(jnp.int32, (t, HEAD_DIM), 1)
        stats = jnp.zeros((t, HEAD_DIM), F32)
        for h in range(HEADS):
            sl = slice(HEAD_DIM * h, HEAD_DIM * (h + 1))
            delta = jnp.sum(do_ref[:, sl] * o_ref[:, sl], axis=1, keepdims=True)
            stats = stats + jnp.where(lane == h, lse_ref[:, sl], 0.0) + jnp.where(lane == HEADS + h, delta, 0.0)
        st_ref[...] = jnp.transpose(stats)[0:8, :]

    row = pl.BlockSpec((t, HEADS * HEAD_DIM), lambda i: (i, 0))
    return pl.pallas_call(
        body, name=name, grid=(S // t,),
        in_specs=[row, row, row], out_specs=pl.BlockSpec((8, t), lambda i: (0, i)),
        out_shape=jax.ShapeDtypeStruct((8, S), F32),
        compiler_params=_params(("parallel",)),
    )(o, lse, d_o)


BWD_GROUP = 2


def _attn_bwd(q, k, v, d_o, stats, t, name):
    S = q.shape[0]
    n = S // t
    groups = HEADS // BWD_GROUP
    gq, gv = BWD_GROUP * 256, BWD_GROUP * HEAD_DIM
    qi, kj = _tile_pairs(n, by_key=True)
    n_pairs = qi.shape[0]
    st = stats.reshape(2, groups, BWD_GROUP, S).transpose(1, 0, 2, 3).reshape(groups, 2 * BWD_GROUP, S)
    st = jnp.pad(st, ((0, 0), (0, 8 - 2 * BWD_GROUP), (0, 0)))

    def heads(ref, width, rows=slice(None)):
        return jnp.stack([ref[rows, width * h:width * (h + 1)] for h in range(BWD_GROUP)])

    def body(qi_ref, kj_ref, q_ref, k_ref, v_ref, do_ref, st_ref, dq_hbm, dk_ref, dv_ref, dq_acc, sem):
        g, p = pl.program_id(0), pl.program_id(1)
        i, j = qi_ref[p], kj_ref[p]

        @pl.when(i == j)
        def _():
            dk_ref[...] = jnp.zeros_like(dk_ref)
            dv_ref[...] = jnp.zeros_like(dv_ref)

        def update(masked):
            qh, kh = heads(q_ref, 256), heads(k_ref, 256)
            d_out = heads(do_ref, HEAD_DIM)
            stv = st_ref[...]
            lse_row = jnp.stack([stv[h:h + 1, :] for h in range(BWD_GROUP)])
            delta_row = jnp.stack([stv[BWD_GROUP + h:BWD_GROUP + h + 1, :] for h in range(BWD_GROUP)])
            s_t = _bmm(kh, qh, _BNT)
            p_t = jnp.exp2(s_t - lse_row)
            if masked:
                p_t = jnp.where(_chunk_mask(i, j, t, transposed=True)[None], p_t, 0.0)
            dv = _bmm(p_t, d_out, _BNN)
            dp_t = _bmm(heads(v_ref, HEAD_DIM), d_out, _BNT)
            ds_t = p_t * (dp_t - delta_row)
            dk = _bmm(ds_t, qh, _BNN)
            dq = _bmm(ds_t, kh, _BTN)
            rows = pl.ds(pl.multiple_of(i * t, t), t)
            for h in range(BWD_GROUP):
                dk_ref[:, 256 * h:256 * (h + 1)] += dk[h]
                dv_ref[:, HEAD_DIM * h:HEAD_DIM * (h + 1)] += dv[h]

            @pl.when(j == 0)
            def _():
                for h in range(BWD_GROUP):
                    dq_acc[rows, 256 * h:256 * (h + 1)] = dq[h]

            @pl.when(j > 0)
            def _():
                for h in range(BWD_GROUP):
                    dq_acc[rows, 256 * h:256 * (h + 1)] += dq[h]

        @pl.when(i == j)
        def _():
            update(True)

        @pl.when(i > j)
        def _():
            update(False)

        @pl.when(i == n - 1)
        def _():
            dk_ref[...] *= 1.0 / LOG2_E

        @pl.when(p == n_pairs - 1)
        def _():
            dq_acc[...] *= 1.0 / LOG2_E
            for gg in range(groups):
                @pl.when(g == gg)
                def _():
                    cp = pltpu.make_async_copy(dq_acc, dq_hbm.at[:, gq * gg:gq * (gg + 1)], sem)
                    cp.start()
                    cp.wait()

    q_blk = lambda g, p, qi_, kj_: (qi_[p], g)
    k_blk = lambda g, p, qi_, kj_: (kj_[p], g)
    return pl.pallas_call(
        body, name=name,
        grid_spec=pltpu.PrefetchScalarGridSpec(
            num_scalar_prefetch=2, grid=(groups, n_pairs),
            in_specs=[pl.BlockSpec((t, gq), q_blk), pl.BlockSpec((t, gq), k_blk), pl.BlockSpec((t, gv), k_blk),
                      pl.BlockSpec((t, gv), q_blk), pl.BlockSpec((None, 8, t), lambda g, p, qi_, kj_: (g, 0, qi_[p]))],
            out_specs=[pl.BlockSpec(memory_space=pl.ANY), pl.BlockSpec((t, gq), k_blk), pl.BlockSpec((t, gv), k_blk)],
            scratch_shapes=[pltpu.VMEM((S, gq), F32), pltpu.SemaphoreType.DMA]),
        out_shape=[jax.ShapeDtypeStruct((S, HEADS * 256), F32), jax.ShapeDtypeStruct((S, HEADS * 256), F32),
                   jax.ShapeDtypeStruct((S, HEADS * HEAD_DIM), F32)],
        compiler_params=_params(("arbitrary", "arbitrary")),
    )(qi, kj, q, k, v, d_o, st)


FFN_PIECE = 2 * D_FF // N_DEV
HID_PIECES = D_FF // FFN_PIECE


def _after_specs(after):
    return [] if after is None else [pl.BlockSpec(memory_space=pl.ANY)]


def _after_args(after):
    return [] if after is None else [after]


def _ffn_gw8(h, d_gate, d_up, name, after=None):
    S = h.shape[0]
    tm = 512
    tk = _pick(S, MATMUL_ROWS)
    nk = S // tk

    def body(h_ref, dg_ref, du_ref, *rest):
        o_ref, acc_ref = rest[-2:]
        k = pl.program_id(1)

        @pl.when(k == 0)
        def _():
            acc_ref[...] = jnp.zeros_like(acc_ref)

        h_t = jnp.transpose(h_ref[...])
        for p in range(HID_PIECES):
            acc_ref[p] += _dot_raw(h_t, dg_ref[p], "nn")
            acc_ref[HID_PIECES + p] += _dot_raw(h_t, du_ref[p], "nn")

        @pl.when(k == nk - 1)
        def _():
            o_ref[...] = acc_ref[...].astype(o_ref.dtype)

    d_spec = pl.BlockSpec((HID_PIECES, tk, FFN_PIECE), lambda i, k: (0, k, 0))
    return pl.pallas_call(
        body, name=name, grid=(D_MODEL // tm, nk),
        in_specs=[pl.BlockSpec((tk, tm), lambda i, k: (k, i)), d_spec, d_spec] + _after_specs(after),
        out_specs=pl.BlockSpec((2 * HID_PIECES, tm, FFN_PIECE), lambda i, k: (0, i, 0)),
        out_shape=jax.ShapeDtypeStruct((2 * HID_PIECES, D_MODEL, FFN_PIECE), BF16),
        scratch_shapes=[pltpu.VMEM((2 * HID_PIECES, tm, FFN_PIECE), F32)],
        compiler_params=_params(("parallel", "arbitrary")),
    )(h, d_gate, d_up, *_after_args(after))


EPILOGUE_ROWS = 256


def _dmod_epilogue(acc_ref, x_ref, do_ref, sc_ref, sh_ref, dx_ref, dsc_ref, dsh_ref, below, below_in, below_out):
    rows_total = acc_ref.shape[0]
    step = min(EPILOGUE_ROWS, rows_total)
    dsc, dsh, dg = 0.0, 0.0, 0.0
    for r in range(rows_total // step):
        rows = slice(step * r, step * (r + 1))
        _, vjp = jax.vjp(_modulate, x_ref[rows, :], sc_ref[...], sh_ref[...])
        dx, dsc_r, dsh_r = vjp(acc_ref[rows, :])
        dx = dx + do_ref[rows, :]
        dx_ref[rows, :] = dx
        dsc, dsh = dsc + dsc_r, dsh + dsh_r
        if below is not None:
            coef = below[2]
            below_out[0][rows, :] = (coef * below_in[1][...] * dx).astype(below_out[0].dtype)
            dg = dg + jnp.sum(coef * below_in[0][rows, :] * dx, axis=0, keepdims=True)
    dsc_ref[...] += dsc
    dsh_ref[...] += dsh
    if below is not None:
        below_out[1][...] += dg


def _ffn_dh(d_gate, d_up, w8, x, d_out, scale, shift, name, after=None, below=None):
    S = d_gate.shape[1]
    tm = _pick(S, MATMUL_ROWS)
    n_below = 0 if below is None else 2

    def body(dg_ref, du_ref, wg_ref, wu_ref, x_ref, do_ref, sc_ref, sh_ref, *rest):
        below_in = rest[:n_below]
        outs = rest[len(rest) - 4 - n_below:]
        dx_ref, dsc_ref, dsh_ref = outs[:3]
        below_out, acc_ref = outs[3:3 + n_below], outs[-1]
        i, k = pl.program_id(0), pl.program_id(1)

        @pl.when(k == 0)
        def _():
            acc_ref[...] = jnp.zeros_like(acc_ref)

        acc_ref[...] += _dot_raw(dg_ref[...], wg_ref[...], "nt") + _dot_raw(du_ref[...], wu_ref[...], "nt")

        @pl.when((k == 0) & (i == 0))
        def _():
            for r in (dsc_ref, dsh_ref) + tuple(below_out[1:]):
                r[...] = jnp.zeros_like(r)

        @pl.when(k == HID_PIECES - 1)
        def _():
            _dmod_epilogue(acc_ref, x_ref, do_ref, sc_ref, sh_ref, dx_ref, dsc_ref, dsh_ref, below, below_in, below_out)

    d_spec = pl.BlockSpec((None, tm, FFN_PIECE), lambda i, k: (k, i, 0))
    row = pl.BlockSpec((tm, D_MODEL), lambda i, k: (i, 0))
    par = pl.BlockSpec((1, D_MODEL), lambda i, k: (0, 0))
    row_shape, par_shape = jax.ShapeDtypeStruct((S, D_MODEL), F32), jax.ShapeDtypeStruct((1, D_MODEL), F32)
    return pl.pallas_call(
        body, name=name, grid=(S // tm, HID_PIECES),
        in_specs=[d_spec, d_spec,
                  pl.BlockSpec((None, D_MODEL, FFN_PIECE), lambda i, k: (k, 0, 0)),
                  pl.BlockSpec((None, D_MODEL, FFN_PIECE), lambda i, k: (k + HID_PIECES, 0, 0)),
                  row, row, par, par] + [row, par][:n_below] + _after_specs(after),
        out_specs=[row, par, par] + [row, par][:n_below],
        out_shape=[row_shape, par_shape, par_shape] + [jax.ShapeDtypeStruct((S, D_MODEL), BF16), par_shape][:n_below],
        scratch_shapes=[pltpu.VMEM((tm, D_MODEL), F32)],
        compiler_params=_params(("arbitrary", "arbitrary")),
    )(d_gate, d_up, w8, w8, x, d_out, scale, shift, *(below[:2] if below is not None else ()), *_after_args(after))


def _swiglu_bwd(d_hid, hid_by_gate, hid_by_up):
    return d_hid * hid_by_gate, d_hid * hid_by_up


def _adamw_math(w_, g_, m_, v_):
    m_ = ADAM_B1 * m_ + (1.0 - ADAM_B1) * g_
    v_ = ADAM_B2 * v_ + (1.0 - ADAM_B2) * (g_ * g_)
    m_hat = m_ / (1.0 - ADAM_B1 ** ADAM_STEP)
    v_hat = v_ / (1.0 - ADAM_B2 ** ADAM_STEP)
    return -ADAM_LR * (m_hat / (jnp.sqrt(v_hat) + ADAM_EPS) + ADAM_WD * w_), m_, v_


def _adamw(w, g, m, v, name):
    R, C = w.shape
    tr = _pick(R, (256, 176, 128, 64, 32, 16, 8))

    def body(w_ref, g_ref, m_ref, v_ref, d_ref, nm_ref, nv_ref):
        d_ref[...], nm_ref[...], nv_ref[...] = _adamw_math(w_ref[...], g_ref[...], m_ref[...], v_ref[...])

    spec = pl.BlockSpec((tr, C), lambda i: (i, 0))
    return pl.pallas_call(
        body, name=name, grid=(R // tr,),
        in_specs=[spec] * 4, out_specs=[spec] * 3,
        out_shape=[jax.ShapeDtypeStruct((R, C), F32)] * 3,
        compiler_params=_params(("parallel",)),
    )(w, g, m, v)


def _sum_adamw(parts, w, m, v, name, transposed=False):
    _, R, C = parts.shape
    tr = _pick(R, (256, 176, 128, 64, 32, 16, 8))

    def body(p_ref, w_ref, m_ref, v_ref, g_ref, d_ref, nm_ref, nv_ref):
        g_ = p_ref[0].astype(F32)
        for d in range(1, N_DEV):
            g_ = g_ + p_ref[d].astype(F32)
        if transposed:
            g_ = jnp.transpose(g_)
        g_ref[...] = g_
        d_ref[...], nm_ref[...], nv_ref[...] = _adamw_math(w_ref[...], g_, m_ref[...], v_ref[...])

    spec = pl.BlockSpec((C, tr), lambda i: (0, i)) if transposed else pl.BlockSpec((tr, C), lambda i: (i, 0))
    return pl.pallas_call(
        body, name=name, grid=(R // tr,),
        in_specs=[pl.BlockSpec((N_DEV, tr, C), lambda i: (0, i, 0)), spec, spec, spec], out_specs=[spec] * 4,
        out_shape=[jax.ShapeDtypeStruct(w.shape, F32)] * 4,
        compiler_params=_params(("parallel",)),
    )(parts, w, m, v)


def _sum_devices(parts, name):
    _, R, C = parts.shape
    tr = _pick(R, (512, 256, 176, 128, 64, 32, 16, 8))

    def body(p_ref, o_ref):
        acc = p_ref[0].astype(F32)
        for d in range(1, N_DEV):
            acc = acc + p_ref[d].astype(F32)
        o_ref[...] = acc

    return pl.pallas_call(
        body, name=name, grid=(R // tr,),
        in_specs=[pl.BlockSpec((N_DEV, tr, C), lambda i: (0, i, 0))],
        out_specs=pl.BlockSpec((tr, C), lambda i: (i, 0)),
        out_shape=jax.ShapeDtypeStruct((R, C), F32),
        compiler_params=_params(("parallel",)),
    )(parts)


def _my_place():
    return lax.axis_index("x"), lax.axis_index("y"), lax.axis_index("c")


def _all_gather(blocks, name):
    n = len(blocks)

    def body(*refs):
        x_refs, out_refs = refs[:n], refs[n:2 * n]
        send_sems, recv_sems, local_sems = refs[2 * n:]
        x, y, c = _my_place()
        me, sibling = (x, y, c), (x, y, 1 - c)
        chips = [(1 - x, y), (x, 1 - y), (1 - x, 1 - y)]

        def copy(a, k, blk, to, own=False):
            slot = out_refs[a].at[4 * blk[0] + 2 * blk[1] + blk[2]]
            return pltpu.make_async_remote_copy(
                src_ref=x_refs[a] if own else slot, dst_ref=slot,
                send_sem=send_sems.at[7 * a + k], recv_sem=recv_sems.at[7 * a + k], device_id=to, device_id_type=MESH)

        mine = [pltpu.make_async_copy(x_refs[a], out_refs[a].at[4 * x + 2 * y + c], local_sems.at[a]) for a in range(n)]
        for cp in mine:
            cp.start()
        first = []
        for j, chip in enumerate(chips):
            first += [copy(a, 1 + j, me, (*chip, c), own=True) for a in range(n)]
        first += [copy(a, 0, me, sibling, own=True) for a in range(n)]
        for cp in first:
            cp.start()
        passed = []
        for j, chip in enumerate(chips):
            for a in range(n):
                copy(a, 1 + j, (*chip, c), me).wait_recv()
                passed.append(copy(a, 4 + j, (*chip, c), sibling))
                passed[-1].start()
        for a in range(n):
            copy(a, 0, sibling, me).wait_recv()
        for j, chip in enumerate(chips):
            for a in range(n):
                copy(a, 4 + j, (*chip, 1 - c), me).wait_recv()
        for cp in first + passed:
            cp.wait_send()
        for cp in mine:
            cp.wait()

    return pl.pallas_call(
        body, name=name,
        out_shape=[jax.ShapeDtypeStruct((N_DEV,) + b.shape, b.dtype) for b in blocks],
        in_specs=[pl.BlockSpec(memory_space=pl.ANY)] * n,
        out_specs=[pl.BlockSpec(memory_space=pl.ANY)] * n,
        scratch_shapes=[pltpu.SemaphoreType.DMA((7 * n,)), pltpu.SemaphoreType.DMA((7 * n,)), pltpu.SemaphoreType.DMA((n,))],
    )(*blocks)


def _all_to_all(pieces, name):
    n = len(pieces)

    def body(*refs):
        x_refs, out_refs = refs[:n], refs[n:2 * n]
        send_sems, recv_sems, local_sems = refs[2 * n:]
        x, y, c = _my_place()
        me = 4 * x + 2 * y + c
        mine = [pltpu.make_async_copy(x_refs[a].at[me], out_refs[a].at[me], local_sems.at[a]) for a in range(n)]
        for cp in mine:
            cp.start()
        copies = []
        for k in (2, 4, 6, 3, 5, 7, 1):
            px = 1 - x if k & 4 else x
            py = 1 - y if k & 2 else y
            pc = 1 - c if k & 1 else c
            peer = 4 * px + 2 * py + pc
            for a in range(n):
                copies.append(pltpu.make_async_remote_copy(
                    src_ref=x_refs[a].at[peer], dst_ref=out_refs[a].at[me],
                    send_sem=send_sems.at[7 * a + k - 1], recv_sem=recv_sems.at[7 * a + k - 1],
                    device_id=(px, py, pc), device_id_type=MESH))
        for cp in copies:
            cp.start()
        for cp in copies:
            cp.wait_recv()
        for cp in copies:
            cp.wait_send()
        for cp in mine:
            cp.wait()

    return pl.pallas_call(
        body, name=name,
        out_shape=[jax.ShapeDtypeStruct(p.shape, p.dtype) for p in pieces],
        in_specs=[pl.BlockSpec(memory_space=pl.ANY)] * n,
        out_specs=[pl.BlockSpec(memory_space=pl.ANY)] * n,
        scratch_shapes=[pltpu.SemaphoreType.DMA((7 * n,)), pltpu.SemaphoreType.DMA((7 * n,)), pltpu.SemaphoreType.DMA((n,))],
    )(*pieces)


def _peers():
    x, y, c = _my_place()
    out = []
    for k in (2, 4, 6, 3, 5, 7, 1):
        px = 1 - x if k & 4 else x
        py = 1 - y if k & 2 else y
        pc = 1 - c if k & 1 else c
        out.append((k, (px, py, pc), 4 * px + 2 * py + pc))
    return out


def _exchange_copies(x_refs, land_refs, send_sems, recv_sems, scatter):
    x, y, c = _my_place()
    me = 4 * x + 2 * y + c
    starts, arrivals = [], []
    for k, place, peer in _peers():
        for a, (x_ref, land_ref) in enumerate(zip(x_refs, land_refs)):
            sems = dict(send_sem=send_sems.at[7 * a + k - 1], recv_sem=recv_sems.at[7 * a + k - 1],
                        device_id=place, device_id_type=MESH)
            src = x_ref.at[peer] if scatter else x_ref
            starts.append(pltpu.make_async_remote_copy(src_ref=src, dst_ref=land_ref.at[me], **sems))
            arrivals.append(pltpu.make_async_remote_copy(src_ref=src, dst_ref=land_ref.at[peer], **sems))
    return starts, arrivals


def _exchange_start(arrays, scatter, name):
    n = len(arrays)
    hbm = pl.BlockSpec(memory_space=pltpu.HBM)
    sem = pl.BlockSpec(memory_space=pltpu.SEMAPHORE)
    lands = [lax.empty(a.shape if scatter else (N_DEV,) + a.shape, a.dtype) for a in arrays]

    def body(*refs):
        x_refs, land_refs = refs[:n], refs[n:2 * n]
        send_sems, recv_sems = refs[2 * n], refs[2 * n + 1]
        token = refs[-1]
        starts, _ = _exchange_copies(x_refs, land_refs, send_sems, recv_sems, scatter)
        for cp in starts:
            cp.start()
        token[...] = jnp.zeros_like(token)

    res = pl.pallas_call(
        body, name=name,
        out_shape=(pltpu.SemaphoreType.DMA((7 * n,)), pltpu.SemaphoreType.DMA((7 * n,)),
                   *[pltpu.HBM(a.shape, a.dtype) for a in arrays], *[pltpu.HBM(l.shape, l.dtype) for l in lands],
                   jax.ShapeDtypeStruct((8, 128), F32)),
        in_specs=[hbm] * (2 * n),
        out_specs=(sem, sem, *[hbm] * (2 * n), pl.BlockSpec(memory_space=pltpu.VMEM)),
        input_output_aliases={i: 2 + i for i in range(2 * n)},
        compiler_params=pltpu.CompilerParams(has_side_effects=pltpu.SideEffectType.DATAFLOW_SIDE_EFFECTING),
    )(*[pltpu.with_memory_space_constraint(a, pltpu.HBM) for a in arrays],
      *[pltpu.with_memory_space_constraint(l, pltpu.HBM) for l in lands])
    return res[0], res[1], list(res[2:2 + n]), list(res[2 + n:2 + 2 * n]), res[-1]


def _exchange_wait(handles, scatter, after, name):
    send_sems, recv_sems, arrays, lands, _ = handles
    n = len(arrays)
    hbm = pl.BlockSpec(memory_space=pltpu.HBM)
    sem = pl.BlockSpec(memory_space=pltpu.SEMAPHORE)

    def body(*refs):
        x_refs, land_refs = refs[:n], refs[n:2 * n]
        send_s, recv_s = refs[2 * n], refs[2 * n + 1]
        starts, arrivals = _exchange_copies(x_refs, land_refs, send_s, recv_s, scatter)
        for cp in arrivals:
            cp.wait_recv()
        for cp in starts:
            cp.wait_send()

    res = pl.pallas_call(
        body, name=name,
        out_shape=(*[pltpu.HBM(a.shape, a.dtype) for a in arrays], *[pltpu.HBM(l.shape, l.dtype) for l in lands]),
        in_specs=[hbm] * (2 * n) + [sem, sem, pl.BlockSpec(memory_space=pl.ANY)],
        out_specs=tuple([hbm] * (2 * n)),
        input_output_aliases={i: i for i in range(2 * n)},
        compiler_params=pltpu.CompilerParams(has_side_effects=pltpu.SideEffectType.DATAFLOW_SIDE_EFFECTING),
    )(*arrays, *lands, send_sems, recv_sems, after)
    me = 4 * lax.axis_index("x") + 2 * lax.axis_index("y") + lax.axis_index("c")
    out = []
    for src, got in zip(res[:n], res[n:]):
        zeros = (0,) * (got.ndim - 1)
        own = lax.dynamic_slice(src, (me,) + zeros, (1,) + src.shape[1:]) if scatter else src[None]
        out.append(lax.dynamic_update_slice(got, own, (me,) + zeros))
    return out


def _pad_lanes(v, at=0, width=128):
    return jnp.pad(v, ((0, 0), (at, width - at - v.shape[1])))


def _pack_weights(P):
    W = {}
    w = P["w_in"]
    W["wp"] = jnp.concatenate([w[:, :2048], w[:, 2440:2696], w[:, 2056:2440], w[:, 2696:2760], w[:, 2048:2056],
                               jnp.zeros((D_MODEL, N_IN_PACKED - N_IN), w.dtype)], axis=1).astype(BF16)
    W["conv_w"] = P["gdn_conv_w"].astype(F32)
    W["alog_p"] = _pad_lanes(P["gdn_a_log"], 64)
    W["dt_p"] = _pad_lanes(P["gdn_dt_bias"], 64)
    W["gnw"] = P["gdn_norm_w"]
    W["qnw"] = P["mla_q_norm_w"]
    W["kvnw"] = P["mla_kv_norm_w"]
    uq = P["mla_w_uq"].reshape(Q_LORA, HEADS, HEAD_DIM + ROPE)
    W["wuq"] = jnp.pad(uq, ((0, 0), (0, 0), (0, 256 - HEAD_DIM - ROPE))).reshape(Q_LORA, HEADS * 256).astype(BF16)
    ukv = P["mla_w_ukv"].reshape(KV_LORA, HEADS, 2, HEAD_DIM)
    W["wukv"] = ukv.transpose(0, 2, 1, 3).reshape(KV_LORA, 2 * HEADS * HEAD_DIM).astype(BF16)
    W["qn_w"] = P["qkn_q_nope"]
    W["qr_w"] = _pad_lanes(P["qkn_q_rope"])
    W["kn_w"] = P["qkn_k_nope"]
    W["kr_w"] = _pad_lanes(P["qkn_k_rope"])
    W["onw"] = P["mla_out_norm_w"]
    W["wout"] = P["w_out"].astype(BF16)
    return W


def _unpack_grads(G):
    g = G["wp"]
    uq = G["wuq"].reshape(Q_LORA, HEADS, 256)[:, :, :HEAD_DIM + ROPE].reshape(Q_LORA, HEADS * (HEAD_DIM + ROPE))
    ukv = G["wukv"].reshape(KV_LORA, 2, HEADS, HEAD_DIM).transpose(0, 2, 1, 3).reshape(KV_LORA, 2 * HEADS * HEAD_DIM)
    return {
        "w_in": jnp.concatenate([g[:, :2048], g[:, 2752:2760], g[:, 2304:2688], g[:, 2048:2304], g[:, 2688:2752]], axis=1),
        "gdn_conv_w": G["conv_w"], "gdn_a_log": G["alog_p"][:, 64:68], "gdn_dt_bias": G["dt_p"][:, 64:68],
        "gdn_norm_w": G["gnw"], "mla_q_norm_w": G["qnw"], "mla_w_uq": uq, "mla_kv_norm_w": G["kvnw"], "mla_w_ukv": ukv,
        "qkn_q_nope": G["qn_w"], "qkn_q_rope": G["qr_w"][:, :ROPE], "qkn_k_nope": G["kn_w"], "qkn_k_rope": G["kr_w"][:, :ROPE],
        "mla_out_norm_w": G["onw"], "w_out": G["wout"],
    }


def _rope_tables(positions):
    half = ROPE // 2
    inv_freq = ROPE_BASE ** (-jnp.arange(half, dtype=F32) / half)
    ang = positions.astype(F32)[:, None] * inv_freq
    cos, sin = jnp.cos(ang), jnp.sin(ang)
    zeros = jnp.zeros((positions.shape[0], 128 - ROPE), F32)
    return jnp.concatenate([cos, cos, zeros], axis=1), jnp.concatenate([-sin, sin, zeros], axis=1)


def _ffn_forward(x, scale, shift, gate_w, w8, wo4, name, target=None):
    S = x.shape[0]
    tm = _pick(S, (512, 256, 128))
    n = S // tm
    with_loss = target is not None

    def body(x_ref, sc_ref, sh_ref, g_ref, wg_ref, wu_ref, wo_ref, *rest):
        t_ref = rest[0] if with_loss else None
        h_ref, bg_ref, bu_ref, ht_ref = rest[with_loss:with_loss + 4]
        tail = rest[with_loss + 4:]
        h_scr, acc_ref = tail[-2:]
        i, p = pl.program_id(0), pl.program_id(1)

        @pl.when(p == 0)
        def _():
            h_new = _modulate(x_ref[...], sc_ref[...], sh_ref[...]).astype(BF16)
            h_scr[...] = h_new
            h_ref[...] = h_new
            acc_ref[...] = jnp.zeros_like(acc_ref)

        h = h_scr[...]
        gate = _dot_raw(h, wg_ref[...], "nn")
        up = _dot_raw(h, wu_ref[...], "nn")
        sg = _sigmoid(gate)
        act = gate * sg
        hid = act * up
        bg_ref[...] = (up * (sg * (1.0 + gate * (1.0 - sg)))).astype(BF16)
        bu_ref[...] = act.astype(BF16)
        ht_ref[...] = jnp.transpose(hid).astype(BF16)
        acc_ref[...] += _dot_raw(hid, wo_ref[...], "nn")

        if with_loss:
            dx_ref, df_ref, dg_ref, l_ref = tail[:4]

            @pl.when((p == 0) & (i == 0))
            def _():
                dg_ref[...] = jnp.zeros_like(dg_ref)
                l_ref[...] = jnp.zeros_like(l_ref)

            @pl.when(p == HID_PIECES - 1)
            def _():
                step = min(EPILOGUE_ROWS, tm)
                for r in range(tm // step):
                    rows = slice(step * r, step * (r + 1))
                    f = acc_ref[rows, :]
                    diff = x_ref[rows, :] + 0.5 * g_ref[...] * f - t_ref[rows, :]
                    dx = diff * (1.0 / D_MODEL)
                    dx_ref[rows, :] = dx
                    df_ref[rows, :] = (0.5 * g_ref[...] * dx).astype(df_ref.dtype)
                    dg_ref[...] += jnp.sum(0.5 * f * dx, axis=0, keepdims=True)
                    l_ref[...] += jnp.sum(diff * diff, axis=0, keepdims=True)

            @pl.when((p == HID_PIECES - 1) & (i == n - 1))
            def _():
                l_ref[...] = jnp.full(l_ref.shape, (0.5 / D_MODEL) * jnp.sum(l_ref[...]), F32)
        else:
            f_ref, xo_ref = tail[:2]

            @pl.when(p == HID_PIECES - 1)
            def _():
                f = acc_ref[...]
                f_ref[...] = f.astype(f_ref.dtype)
                xo_ref[...] = x_ref[...] + 0.5 * g_ref[...] * f

    row = pl.BlockSpec((tm, D_MODEL), lambda i, p: (i, 0))
    par = pl.BlockSpec((1, D_MODEL), lambda i, p: (0, 0))
    piece = pl.BlockSpec((None, tm, FFN_PIECE), lambda i, p: (p, i, 0))
    row_f32, row_bf16 = jax.ShapeDtypeStruct((S, D_MODEL), F32), jax.ShapeDtypeStruct((S, D_MODEL), BF16)
    par_f32 = jax.ShapeDtypeStruct((1, D_MODEL), F32)
    piece_shape = jax.ShapeDtypeStruct((HID_PIECES, S, FFN_PIECE), BF16)
    return pl.pallas_call(
        body, name=name, grid=(n, HID_PIECES),
        in_specs=[row, par, par, par,
                  pl.BlockSpec((None, D_MODEL, FFN_PIECE), lambda i, p: (p, 0, 0)),
                  pl.BlockSpec((None, D_MODEL, FFN_PIECE), lambda i, p: (p + HID_PIECES, 0, 0)),
                  pl.BlockSpec((None, FFN_PIECE, D_MODEL), lambda i, p: (p, 0, 0))] + [row] * with_loss,
        out_specs=[row, piece, piece, pl.BlockSpec((None, FFN_PIECE, tm), lambda i, p: (p, 0, i))]
        + ([row, row, par, par] if with_loss else [row, row]),
        out_shape=[row_bf16, piece_shape, piece_shape, jax.ShapeDtypeStruct((HID_PIECES, FFN_PIECE, S), BF16)]
        + ([row_f32, row_bf16, par_f32, par_f32] if with_loss else [row_bf16, row_f32]),
        scratch_shapes=[pltpu.VMEM((tm, D_MODEL), BF16), pltpu.VMEM((tm, D_MODEL), F32)],
        compiler_params=_params(("arbitrary", "arbitrary")),
    )(x, scale, shift, gate_w, w8, w8, wo4, *([target] if with_loss else []))


def _ffn_fwd(x, scale, shift, gate_w, w8, wo4, tag, target=None):
    res = _ffn_forward(x, scale, shift, gate_w, w8, wo4, tag + "_fwd", target)
    h, by_gate, by_up, hid_t = res[:4]
    if target is not None:
        dx_out, df, d_gate_w, loss_row = res[4:]
        return (dx_out, loss_row), (h, by_gate, by_up, hid_t, None, df, d_gate_w)
    f, x_out = res[4:]
    return x_out, (h, by_gate, by_up, hid_t, f, None, None)


def _ffn_bwd(d_out, x, scale, shift, gate_w, w8, wo4, saved, tag, grad_ready, below=None):
    h, gate, up, hid_t, f, df, d_gate_w = saved
    S = x.shape[0]
    tm = _pick(S, (512, 256, 128))
    tk = _pick(S, (512, 256, 128))
    n = S // tm
    if df is None:
        (df,), (d_gate_w,) = _rowwise_bwd(lambda f_, g_: (0.5 * g_ * f_,), [(f, tm, D_MODEL, 0)], [], [gate_w],
                                          [(d_out, tm, D_MODEL, 0)], n, tag + "_dres", row_dtypes=(BF16,))
    tb = _pick(S, MATMUL_ROWS)
    piece = pl.BlockSpec((None, tb, FFN_PIECE), lambda i, j, k: (j, i, 0))
    d_gate, d_up = _mmg(df, wo4, "nt", name=tag + "_ddown", grid=(S // tb, HID_PIECES, 1),
                        a_spec=pl.BlockSpec((tb, D_MODEL), lambda i, j, k: (i, 0)),
                        b_spec=pl.BlockSpec((None, FFN_PIECE, D_MODEL), lambda i, j, k: (j, 0, 0)),
                        out_spec=piece, out_shapes=[jax.ShapeDtypeStruct((HID_PIECES, S, FFN_PIECE), BF16)] * 2,
                        acc_shape=(tb, FFN_PIECE), extras=[gate, up], extra_specs=[piece, piece], epi=_swiglu_bwd)
    tk = _pick(S, MATMUL_ROWS)
    g_wo4 = _mmg(hid_t, df, "nn", name=tag + "_gwo", grid=(HID_PIECES, 1, S // tk),
                 a_spec=pl.BlockSpec((None, FFN_PIECE, tk), lambda i, j, k: (i, 0, k)),
                 b_spec=pl.BlockSpec((tk, D_MODEL), lambda i, j, k: (k, j)),
                 out_spec=pl.BlockSpec((None, FFN_PIECE, D_MODEL), lambda i, j, k: (i, 0, j)),
                 out_shapes=[jax.ShapeDtypeStruct((HID_PIECES, FFN_PIECE, D_MODEL), BF16)], acc_shape=(FFN_PIECE, D_MODEL))
    g_w8 = _ffn_gw8(h, d_gate, d_up, tag + "_gw8", after=grad_ready("wo4", g_wo4))
    res = _ffn_dh(d_gate, d_up, w8, x, d_out, scale, shift, tag + "_dh", after=grad_ready("w8", g_w8), below=below)
    return (res[0], res[1], res[2], d_gate_w) + tuple(res[3:])


def _dproj_dmod(d_proj, wp, x, d_out, scale, shift, name, below=None):
    S, K = d_proj.shape
    tm = _pick(S, MATMUL_ROWS)
    tk = _pick(K, (1408, 512, 256, 128))
    nk = K // tk
    n_below = 0 if below is None else 2

    def body(dp_ref, w_ref, x_ref, do_ref, sc_ref, sh_ref, *rest):
        below_in = rest[:n_below]
        dx_ref, dsc_ref, dsh_ref = rest[n_below:n_below + 3]
        below_out, acc_ref = rest[n_below + 3:n_below + 3 + n_below], rest[-1]
        i, k = pl.program_id(0), pl.program_id(1)

        @pl.when(k == 0)
        def _():
            acc_ref[...] = jnp.zeros_like(acc_ref)

        acc_ref[...] += _dot_raw(dp_ref[...], w_ref[...], "nt")

        @pl.when((k == 0) & (i == 0))
        def _():
            for r in (dsc_ref, dsh_ref) + tuple(below_out[1:]):
                r[...] = jnp.zeros_like(r)

        @pl.when(k == nk - 1)
        def _():
            _dmod_epilogue(acc_ref, x_ref, do_ref, sc_ref, sh_ref, dx_ref, dsc_ref, dsh_ref, below, below_in, below_out)

    row = pl.BlockSpec((tm, D_MODEL), lambda i, k: (i, 0))
    par = pl.BlockSpec((1, D_MODEL), lambda i, k: (0, 0))
    row_shape, par_shape = jax.ShapeDtypeStruct((S, D_MODEL), F32), jax.ShapeDtypeStruct((1, D_MODEL), F32)
    return pl.pallas_call(
        body, name=name, grid=(S // tm, nk),
        in_specs=[pl.BlockSpec((tm, tk), lambda i, k: (i, k)), pl.BlockSpec((D_MODEL, tk), lambda i, k: (0, k)),
                  row, row, par, par] + [row, par][:n_below],
        out_specs=[row, par, par] + [row, par][:n_below],
        out_shape=[row_shape, par_shape, par_shape] + [jax.ShapeDtypeStruct((S, D_MODEL), BF16), par_shape][:n_below],
        scratch_shapes=[pltpu.VMEM((tm, D_MODEL), F32)],
        compiler_params=_params(("arbitrary", "arbitrary")),
    )(d_proj, wp, x, d_out, scale, shift, *(below[:2] if below is not None else ()))


def _mod_proj(x, scale, shift, wp, name):
    S = x.shape[0]
    N = wp.shape[1]
    tm = _pick(S, MATMUL_ROWS)
    tn = _pick(N, (1408, 1024, 512, 256, 128))

    def body(x_ref, sc_ref, sh_ref, w_ref, h_ref, o_ref, h_scr):
        @pl.when(pl.program_id(1) == 0)
        def _():
            h_new = _modulate(x_ref[...], sc_ref[...], sh_ref[...]).astype(BF16)
            h_scr[...] = h_new
            h_ref[...] = h_new

        o_ref[...] = _dot_raw(h_scr[...], w_ref[...], "nn")

    row = pl.BlockSpec((tm, D_MODEL), lambda i, j: (i, 0))
    par = pl.BlockSpec((1, D_MODEL), lambda i, j: (0, 0))
    return pl.pallas_call(
        body, name=name, grid=(S // tm, N // tn),
        in_specs=[row, par, par, pl.BlockSpec((D_MODEL, tn), lambda i, j: (0, j))],
        out_specs=[row, pl.BlockSpec((tm, tn), lambda i, j: (i, j))],
        out_shape=[jax.ShapeDtypeStruct((S, D_MODEL), BF16), jax.ShapeDtypeStruct((S, N), F32)],
        scratch_shapes=[pltpu.VMEM((tm, D_MODEL), BF16)],
        compiler_params=_params(("parallel", "arbitrary")),
    )(x, scale, shift, wp)


def _mixer_fwd(x1, scale, shift, gate_w, cos_p, sin_p, W):
    S = x1.shape[0]
    tm = _pick(S, (512, 256, 128))
    tv = _pick(S, TOKEN_ROWS)
    ta = _pick(S, (512, 256, 128))
    nc = S // CHUNK
    h2, proj = _mod_proj(x1, scale, shift, W["wp"], "mix_proj")
    qkvc = _conv_fwd(proj, W["conv_w"], tv, "gdn_conv")
    kab = (proj, tv, 128, 21)
    q_a, k_a, v_a, gb = _rowwise(_gdn_pre_fn, [(qkvc, tv, 1536, 0), kab], [W["alog_p"], W["dt_p"]],
                                 [(tv, 512, F32)] * 3 + [(tv, 128, F32)], S // tv, "gdn_pre")
    ti = _pick(S, INTRA_ROWS)
    intra = _rowwise(_gdn_intra_fn, [(q_a, ti, 512, 0), (k_a, ti, 512, 0), (v_a, ti, 512, 0), (gb, ti, 128, 0)],
                     [], [(ti, 512, F32)] * 4 + [(ti, CHUNK, F32)] * 4 + [(ti // 8, 512, F32)] + [(ti, CHUNK, F32)] * 4,
                     S // ti, "gdn_intra")
    u, wk, qd, kd, qks, gl, invs = intra[0], intra[1], intra[2], intra[3], tuple(intra[4:8]), intra[8], tuple(intra[9:])
    o_a, s_prev = _gdn_scan_fwd(u, wk, qd, kd, qks, gl, "gdn_scan")
    mla_params = [W["qnw"], W["kvnw"], W["wuq"], W["wukv"], W["qn_w"], W["qr_w"], W["kn_w"], W["kr_w"]]
    def mla_pre_with_vt(*a):
        q_, k_, v_ = _mla_pre_fn(*a)
        return q_, k_, v_, jnp.transpose(v_)

    q_b, k_b, v_b, vt_b = _rowwise(mla_pre_with_vt,
                                   [(proj, tv, 256, 8), (proj, tv, 384, 6), kab, (cos_p, tv, 128, 0), (sin_p, tv, 128, 0)],
                                   mla_params, [(tv, 1024, BF16), (tv, 1024, BF16), (tv, 512, BF16), (512, tv, BF16, "across")],
                                   S // tv, "mla_pre")
    o_b, lse = _attn_fwd(q_b, k_b, vt_b, ta, "mla_attn")
    (mixed,) = _rowwise(_mix_post_fn, [(o_a, tv, 512, 0), (proj, tv, 512, 3), (o_b, tv, 512, 0)], [W["gnw"], W["onw"]],
                        [(tv, D_MODEL, BF16)], S // tv, "mix_post")
    y, x2 = _mm(mixed, W["wout"], "nn", name="mix_out", out_dtypes=(BF16, F32), extras=[x1], extra_params=[gate_w],
                epi=lambda acc, x_, g_: (acc, x_ + g_ * acc))
    saved = (h2, proj, qkvc, q_a, k_a, v_a, gb, u, wk, qd, kd, qks, gl, invs, s_prev, o_a, q_b, k_b, v_b, o_b, lse, mixed, y)
    return x2, saved


def _mixer_bwd(d_out, dy, x1, scale, shift, cos_p, sin_p, W, saved, below):
    (h2, proj, qkvc, q_a, k_a, v_a, gb, u, wk, qd, kd, qks, gl, invs, s_prev, o_a, q_b, k_b, v_b, o_b, lse, mixed, y) = saved
    S = x1.shape[0]
    tm = _pick(S, (512, 256, 128))
    tv = _pick(S, TOKEN_ROWS)
    ta = _pick(S, (512, 256, 128))
    nc = S // CHUNK
    G = {}
    d_mixed = _mm(dy, W["wout"], "nt", name="mix_dout")
    G["wout"] = _mm(mixed, dy, "tn", name="mix_gwout")
    (do_a, dz, do_b), (G["gnw"], G["onw"]) = _rowwise_bwd(
        _mix_post_fn, [(o_a, tv, 512, 0), (proj, tv, 512, 3), (o_b, tv, 512, 0)], [], [W["gnw"], W["onw"]],
        [(d_mixed, tv, D_MODEL, 0)], S // tv, "mix_dpost", row_dtypes=(F32, BF16, F32))
    stats = _attn_stats(o_b, lse, do_b, ta, "mla_stats")
    dq_b, dk_b, dv_b = _attn_bwd(q_b, k_b, v_b, do_b, stats, ta, "mla_dattn")
    kab = (proj, tv, 128, 21)
    mla_params = [W["qnw"], W["kvnw"], W["wuq"], W["wukv"], W["qn_w"], W["qr_w"], W["kn_w"], W["kr_w"]]
    (d_ckv, d_cq, d_kab), mla_grads = _rowwise_bwd(
        _mla_pre_fn, [(proj, tv, 256, 8), (proj, tv, 384, 6), kab], [(cos_p, tv, 128, 0), (sin_p, tv, 128, 0)], mla_params,
        [(dq_b, tv, 1024, 0), (dk_b, tv, 1024, 0), (dv_b, tv, 512, 0)], S // tv, "mla_dpre", row_dtypes=(BF16, BF16, F32))
    for key, g in zip(("qnw", "kvnw", "wuq", "wukv", "qn_w", "qr_w", "kn_w", "kr_w"), mla_grads):
        G[key] = g
    scan_grads = _gdn_scan_bwd(u, wk, qd, kd, qks, gl, s_prev, do_a, "gdn_dscan")
    ti = _pick(S, INTRA_ROWS)
    intra_douts = [(scan_grads[i], ti, 512, 0) for i in range(4)] + [(scan_grads[4 + i], ti, CHUNK, 0) for i in range(4)]
    intra_douts.append((scan_grads[8], ti // 8, 512, 0))
    (dq_a, dk_a, dv_a, d_gb), _ = _rowwise_bwd(
        _gdn_intra_fn, [(q_a, ti, 512, 0), (k_a, ti, 512, 0), (v_a, ti, 512, 0), (gb, ti, 128, 0)],
        [(x_, ti, CHUNK, 0) for x_ in invs], [], intra_douts, S // ti, "gdn_dintra")
    (d_qkvc, d_kab), (G["alog_p"], G["dt_p"]) = _rowwise_bwd(
        _gdn_pre_fn, [(qkvc, tv, 1536, 0), kab], [], [W["alog_p"], W["dt_p"]],
        [(dq_a, tv, 512, 0), (dk_a, tv, 512, 0), (dv_a, tv, 512, 0), (d_gb, tv, 128, 0)], S // tv, "gdn_dpre",
        adds=[(1, d_kab)], row_dtypes=(F32, BF16))
    d_qkv, g_conv = _conv_bwd(proj, d_qkvc, W["conv_w"], tv, "gdn_dconv")
    G["conv_w"] = g_conv[:4]
    d_proj = jnp.concatenate([d_qkv, dz, d_ckv, d_cq, d_kab], axis=1)
    G["wp"] = _mm(h2, d_proj, "tn", name="mix_gwp")
    dx1, G["s2"], G["sh2"], d_below, dg_below = _dproj_dmod(d_proj, W["wp"], x1, d_out, scale, shift, "mix_dproj", below=below)
    return dx1, d_below, dg_below, G


def _local_step(x, target, mod, cos_p, sin_p, W1, mixer_weights, ffn2_weights, ffn_grad_ready, mixer_grads_ready):
    sh1, s1, g1, sh2, s2, g2, sh3, s3, g3 = [mod[:, D_MODEL * i:D_MODEL * (i + 1)] for i in range(N_MOD)]
    x1, saved1 = _ffn_fwd(x, s1, sh1, g1, W1["f1_w8"], W1["f1_wo4"], "ffn1")
    W = mixer_weights(x1)
    x2, saved2 = _mixer_fwd(x1, s2, sh2, g2, cos_p, sin_p, W)
    W.update(ffn2_weights(x2))
    (dx3, loss_row), saved3 = _ffn_fwd(x2, s3, sh3, g3, W["f2_w8"], W["f2_wo4"], "ffn2", target=target)
    dx2, d_s3, d_sh3, d_g3, dy, d_g2 = _ffn_bwd(dx3, x2, s3, sh3, g3, W["f2_w8"], W["f2_wo4"], saved3, "ffn2",
                                                ffn_grad_ready("f2"), below=(saved2[-1], g2, 1.0))
    dx1, df1, d_g1, G = _mixer_bwd(dx2, dy, x1, s2, sh2, cos_p, sin_p, W, saved2, below=(saved1[4], g1, 0.5))
    d_sh2, d_s2 = G.pop("sh2"), G.pop("s2")
    saved1 = saved1[:5] + (df1, d_g1 + mixer_grads_ready(G))
    dx, d_s1, d_sh1, d_g1 = _ffn_bwd(dx1, x, s1, sh1, g1, W1["f1_w8"], W1["f1_wo4"], saved1, "ffn1", ffn_grad_ready("f1"))
    d_mod = jnp.concatenate([d_sh1, d_s1, d_g1, d_sh2, d_s2, d_g2, d_sh3, d_s3, d_g3], axis=1)
    return loss_row, dx, d_mod


WEIGHT_NAMES = ("w_ada", "b_ada", "ffn1_w_in", "ffn1_w_out", "w_in", "gdn_conv_w", "gdn_a_log", "gdn_dt_bias", "gdn_norm_w",
                "mla_q_norm_w", "mla_w_uq", "mla_kv_norm_w", "mla_w_ukv", "qkn_q_nope", "qkn_q_rope", "qkn_k_nope",
                "qkn_k_rope", "mla_out_norm_w", "w_out", "ffn2_w_in", "ffn2_w_out")
FFN_SHARDED = ("ffn1_w_in", "ffn1_w_out", "ffn2_w_in", "ffn2_w_out")
TRANSPOSED_ENTRY = ("ffn1_w_in", "ffn2_w_in", "w_in", "mla_w_uq")
SHEETED = (("w_in", "col"), ("gdn_conv_w", "col"), ("mla_w_uq", "col"), ("mla_w_ukv", "col"), ("w_out", "row"))
MOD_ROWS = N_MOD * D_MODEL // 128
SMALL = {"gdn_a_log": (MOD_ROWS, 1, 64, 4), "gdn_dt_bias": (MOD_ROWS + 1, 1, 64, 4), "gdn_norm_w": (MOD_ROWS + 2, 1, 0, 128),
         "mla_q_norm_w": (MOD_ROWS + 3, 3, 0, 384), "mla_kv_norm_w": (MOD_ROWS + 6, 2, 0, 256),
         "qkn_q_nope": (MOD_ROWS + 8, 1, 0, 128), "qkn_q_rope": (MOD_ROWS + 9, 1, 0, 64), "qkn_k_nope": (MOD_ROWS + 10, 1, 0, 128),
         "qkn_k_rope": (MOD_ROWS + 11, 1, 0, 64), "mla_out_norm_w": (MOD_ROWS + 12, 1, 0, 128)}
LOSS_ROW = MOD_ROWS + 13
CONV_ROW, CONV_ROWS = 88, 4 * 1536 // 128
SHEET_ROWS = CONV_ROW + CONV_ROWS


def _to_sheet(flat, dtype, sublanes):
    n = flat.shape[-1]
    unit = sublanes * 128
    pad = (-n) % unit
    flat = jnp.pad(flat.astype(dtype), [(0, 0)] * (flat.ndim - 1) + [(0, pad)])
    return flat.reshape(flat.shape[:-1] + ((n + pad) // 128, 128))


def _small_sheet(b_like, small):
    sheet = jnp.zeros((SHEET_ROWS, 128), F32).at[:MOD_ROWS].set(b_like.reshape(MOD_ROWS, 128))
    for name, (row, rows, lane, n) in SMALL.items():
        v = small[name].reshape(1, n)
        if rows == 1:
            sheet = sheet.at[row, lane:lane + n].set(v[0])
        else:
            sheet = sheet.at[row:row + rows].set(v.reshape(rows, 128))
    return sheet


def _from_small_sheet(sheet):
    out = {"b_ada": sheet[:MOD_ROWS].reshape(1, N_MOD * D_MODEL)}
    for name, (row, rows, lane, n) in SMALL.items():
        out[name] = sheet[row, lane:lane + n].reshape(1, n) if rows == 1 else sheet[row:row + rows].reshape(1, n)
    return out


def kernel(x, c, positions, w_ada, b_ada, ffn1_w_in, ffn1_w_out, w_in, gdn_conv_w, gdn_a_log, gdn_dt_bias, gdn_norm_w, mla_q_norm_w, mla_w_uq, mla_kv_norm_w, mla_w_ukv, qkn_q_nope, qkn_q_rope, qkn_k_nope, qkn_k_rope, mla_out_norm_w, w_out, ffn2_w_in, ffn2_w_out, loss_target, m_w_ada, m_b_ada, m_ffn1_w_in, m_ffn1_w_out, m_w_in, m_gdn_conv_w, m_gdn_a_log, m_gdn_dt_bias, m_gdn_norm_w, m_mla_q_norm_w, m_mla_w_uq, m_mla_kv_norm_w, m_mla_w_ukv, m_qkn_q_nope, m_qkn_q_rope, m_qkn_k_nope, m_qkn_k_rope, m_mla_out_norm_w, m_w_out, m_ffn2_w_in, m_ffn2_w_out, v_w_ada, v_b_ada, v_ffn1_w_in, v_ffn1_w_out, v_w_in, v_gdn_conv_w, v_gdn_a_log, v_gdn_dt_bias, v_gdn_norm_w, v_mla_q_norm_w, v_mla_w_uq, v_mla_kv_norm_w, v_mla_w_ukv, v_qkn_q_nope, v_qkn_q_rope, v_qkn_k_nope, v_qkn_k_rope, v_mla_out_norm_w, v_w_out, v_ffn2_w_in, v_ffn2_w_out):
    args = locals()
    w = {n: args[n] for n in WEIGHT_NAMES}
    m = {n: args["m_" + n] for n in WEIGHT_NAMES}
    v = {n: args["v_" + n] for n in WEIGHT_NAMES}
    me = 4 * lax.axis_index("x") + 2 * lax.axis_index("y") + lax.axis_index("c")
    cols = N_MOD * D_MODEL // N_DEV
    shard = {n: w[n][0] for n in FFN_SHARDED + tuple(s[0] for s in SHEETED)}

    sc = c * _sigmoid(c)
    first = _to_sheet(jnp.concatenate([sc.reshape(-1), shard["gdn_conv_w"].reshape(-1)]), F32, 8)
    (first_all,) = _all_gather([first], "gather_c")
    sc_all = first_all[:, :D_MODEL // 128].reshape(N_DEV, D_MODEL)
    n_taps = shard["gdn_conv_w"].size
    conv_all = first_all.reshape(N_DEV, -1)[:, D_MODEL:D_MODEL + n_taps].reshape(N_DEV, 4, -1)
    b_mine = lax.dynamic_slice(b_ada, (0, me * cols), (1, cols))
    mod_cols = _mm(sc_all, w_ada[0], "nn", name="ada_mod", extra_params=[b_mine], epi=lambda acc, b_: (acc + b_,))
    (mod_all,) = _all_to_all([_to_sheet(mod_cols, F32, 8)], "scatter_mod")
    mod = mod_all.reshape(N_DEV, -1)[:, :cols].reshape(1, N_MOD * D_MODEL)

    f1_shards, mod = lax.optimization_barrier(([shard["ffn1_w_in"].astype(BF16), shard["ffn1_w_out"].astype(BF16)], mod))
    f1_w8, f1_out = _all_gather(f1_shards, "gather_w1")
    travel = [s for s in SHEETED if s[0] != "gdn_conv_w"]
    tied = lax.optimization_barrier(([shard[n].astype(BF16) for n, _ in travel], f1_w8))
    f1_w8 = tied[1]
    mixer_w = _exchange_start(tied[0], False, "gather_wm_start")
    ffn2_w = _exchange_start([shard["ffn2_w_in"].astype(BF16) + mixer_w[4][0:1, 0:1].astype(BF16),
                              shard["ffn2_w_out"].astype(BF16)], False, "gather_w2_start")
    mod = mod + ffn2_w[4][0:1, 0:1]
    W1 = dict(f1_w8=f1_w8, f1_wo4=f1_out.reshape(HID_PIECES, FFN_PIECE, D_MODEL))

    def mixer_weights(after):
        got = _exchange_wait(mixer_w, False, after, "gather_wm_wait")
        P = {n: jnp.concatenate(list(g), axis=1) if kind == "col" else g.reshape(-1, g.shape[-1])
             for (n, kind), g in zip(travel, got)}
        P["gdn_conv_w"] = jnp.concatenate(list(conv_all), axis=1)
        for n in SMALL:
            P[n] = w[n]
        return _pack_weights(P)

    def ffn2_weights(after):
        f2_w8, f2_out = _exchange_wait(ffn2_w, False, after, "gather_w2_wait")
        return dict(f2_w8=f2_w8, f2_wo4=f2_out.reshape(HID_PIECES, FFN_PIECE, D_MODEL))

    pending, small_grads = {}, {}

    def ffn_grad_ready(tag):
        def ready(which, g):
            pieces = g if which == "w8" else g.reshape((N_DEV,) + shard["ffn1_w_out"].shape)
            pending[tag + which] = _exchange_start([pieces], True, "scatter_%s_%s_start" % (tag, which))
            return pending[tag + which][4]
        return ready

    def mixer_grads_ready(G):
        g_full = _unpack_grads(G)
        small_grads.update({n: g_full[n] for n in SMALL})
        small_grads["gdn_conv_w"] = g_full["gdn_conv_w"]
        pieces = []
        for n, kind in travel:
            r, cc = shard[n].shape
            g = g_full[n].astype(BF16)
            pieces.append(jnp.stack([g[:, cc * p:cc * (p + 1)] for p in range(N_DEV)]) if kind == "col"
                          else g.reshape(N_DEV, r, cc))
        pending["mixer"] = _exchange_start(pieces, True, "scatter_mx_start")
        return pending["mixer"][4][0:1, 0:1]

    cos_p, sin_p = _rope_tables(positions[0])
    loss_row, dx, d_mod = _local_step(x[0], loss_target[0], mod, cos_p, sin_p, W1, mixer_weights, ffn2_weights,
                                      ffn_grad_ready, mixer_grads_ready)

    sheet = _small_sheet(d_mod, small_grads).at[LOSS_ROW].set(loss_row[0, :128])
    sheet = sheet.at[CONV_ROW:CONV_ROW + CONV_ROWS].set(small_grads["gdn_conv_w"].reshape(CONV_ROWS, 128))
    (sheets,) = _all_gather([sheet], "gather_small")
    summed = _sum_devices(sheets, "sum_small")
    d_mod_all = sheets[:, :MOD_ROWS].reshape(N_DEV, N_MOD * D_MODEL)
    d_mod_mine = lax.dynamic_slice(d_mod_all, (0, me * cols), (N_DEV, cols))
    grads = _from_small_sheet(summed)
    grads["w_ada"] = _mm(sc_all, d_mod_mine, "tn", name="ada_gw", hi=True)
    conv_taps = shard["gdn_conv_w"].shape[1]
    grads["gdn_conv_w"] = lax.dynamic_slice(summed[CONV_ROW:CONV_ROW + CONV_ROWS].reshape(4, -1), (0, me * conv_taps),
                                            (4, conv_taps))
    loss = summed[LOSS_ROW, 0]

    delta, new_m, new_v = {}, {}, {}
    arrived = {}
    for n, key in zip(FFN_SHARDED, ("f1w8", "f1wo4", "f2w8", "f2wo4")):
        (arrived[n],) = _exchange_wait(pending[key], True, summed, "scatter_%s_wait" % key)
    arrived.update(zip([n for n, _ in travel], _exchange_wait(pending["mixer"], True, summed, "scatter_mx_wait")))
    for n, parts in arrived.items():
        if n in TRANSPOSED_ENTRY:
            res = _sum_adamw(parts, w[n][0].T, m[n][0].T, v[n][0].T, "adamw_" + n, transposed=True)
            grads[n], delta[n], new_m[n], new_v[n] = [r.T for r in res]
        else:
            grads[n], delta[n], new_m[n], new_v[n] = _sum_adamw(parts, w[n][0], m[n][0], v[n][0], "adamw_" + n)
    for n in ("w_ada", "gdn_conv_w"):
        delta[n], new_m[n], new_v[n] = _adamw(w[n][0], grads[n], m[n][0], v[n][0], "adamw_" + n)
    small_in = [_small_sheet(t["b_ada"], t) for t in (w, grads, m, v)]
    for res, out in zip(_adamw(*small_in, "adamw_small"), (delta, new_m, new_v)):
        out.update(_from_small_sheet(res))

    def shaped(d):
        return [d[n].reshape(w[n].shape) for n in WEIGHT_NAMES]

    return (loss, dx[None], *shaped(grads), *shaped(delta), *shaped(new_m), *shaped(new_v))
```

```python
import functools

import jax
import jax.numpy as jnp
import numpy as np
from jax import lax
from jax.experimental import pallas as pl
from jax.experimental.pallas import tpu as pltpu

F32 = jnp.float32
BF16 = jnp.bfloat16

D_MODEL = 1024
D_FF = 2816
N_MOD = 9
HEADS = 4
HEAD_DIM = 128
CHUNK = 64
EPS = 1e-6
ROPE = 64
Q_LORA = 384
KV_LORA = 256
N_IN = 2760
N_IN_PACKED = 2816
ROPE_BASE = 10000.0
LOG2_E = 1.4426950408889634
N_DEV = 8

ADAM_LR = 0.001
ADAM_B1 = 0.9
ADAM_B2 = 0.999
ADAM_EPS = 1e-08
ADAM_WD = 0.01
ADAM_STEP = 10

VMEM_LIMIT_BYTES = 56 * 1024 * 1024
MATMUL_ROWS = (1024, 512, 256, 128)
MESH = pl.DeviceIdType.MESH


def _params(sem=None):
    return pltpu.CompilerParams(dimension_semantics=sem, vmem_limit_bytes=VMEM_LIMIT_BYTES)


def _pick(dim, prefs):
    for p in prefs:
        if dim % p == 0:
            return p
    return dim


_DIMS = {"nn": (((1,), (0,)), ((), ())), "nt": (((1,), (1,)), ((), ())), "tn": (((0,), (0,)), ((), ()))}


def _dot_raw(a, b, mode):
    return lax.dot_general(a.astype(BF16), b.astype(BF16), _DIMS[mode], preferred_element_type=F32)


def _dot_hi(a, b, mode="nn"):
    return lax.dot_general(a, b, _DIMS[mode], precision=lax.Precision.HIGHEST, preferred_element_type=F32)


@functools.partial(jax.custom_vjp, nondiff_argnums=(2,))
def _bdot(a, b, mode):
    return _dot_raw(a, b, mode)


def _bdot_fwd(a, b, mode):
    return _dot_raw(a, b, mode), (a, b)


def _bdot_bwd(mode, res, g):
    a, b = res
    if mode == "nn":
        return _dot_raw(g, b, "nt"), _dot_raw(a, g, "tn")
    if mode == "nt":
        return _dot_raw(g, b, "nn"), _dot_raw(g, a, "tn")
    return _dot_raw(b, g, "nt"), _dot_raw(a, g, "nn")


_bdot.defvjp(_bdot_fwd, _bdot_bwd)


def _mm(a, b, mode, *, name, out_dtypes=(F32,), epi=None, extras=(), extra_params=(), hi=False,
        tm=None, tn=None, tk=None):
    if mode == "nn":
        (M, K), (_, N) = a.shape, b.shape
    elif mode == "nt":
        (M, K), (N, _) = a.shape, b.shape
    else:
        (K, M), (_, N) = a.shape, b.shape
    tm = tm or _pick(M, (512, 1408, 256, 128) if mode == "tn" else MATMUL_ROWS + (384, 352))
    tn = tn or _pick(N, (1024, 1408, 768, 512, 384, 256, 128))
    tk = tk or _pick(K, (1024, 1408, 512, 384, 256, 128))
    a_spec = {"nn": pl.BlockSpec((tm, tk), lambda i, j, k: (i, k)), "nt": pl.BlockSpec((tm, tk), lambda i, j, k: (i, k)),
              "tn": pl.BlockSpec((tk, tm), lambda i, j, k: (k, i))}[mode]
    b_spec = {"nn": pl.BlockSpec((tk, tn), lambda i, j, k: (k, j)), "nt": pl.BlockSpec((tn, tk), lambda i, j, k: (j, k)),
              "tn": pl.BlockSpec((tk, tn), lambda i, j, k: (k, j))}[mode]
    mn_spec = pl.BlockSpec((tm, tn), lambda i, j, k: (i, j))
    return _mmg(a, b, mode, name=name, grid=(M // tm, N // tn, K // tk), a_spec=a_spec, b_spec=b_spec, out_spec=mn_spec,
                out_shapes=[jax.ShapeDtypeStruct((M, N), dt) for dt in out_dtypes], acc_shape=(tm, tn), epi=epi,
                extras=list(extras) + list(extra_params),
                extra_specs=[mn_spec] * len(extras) + [pl.BlockSpec((1, tn), lambda i, j, k: (0, j))] * len(extra_params),
                hi=hi)


def _mmg(a, b, mode, *, name, grid, a_spec, b_spec, out_spec, out_shapes, acc_shape, epi=None, extras=(),
         extra_specs=(), hi=False):
    nk = grid[2]
    n_e, n_o = len(extras), len(out_shapes)

    def body(*refs):
        a_ref, b_ref = refs[:2]
        e_refs = refs[2:2 + n_e]
        o_refs = refs[2 + n_e:2 + n_e + n_o]
        acc_ref = refs[-1]
        k = pl.program_id(2)

        @pl.when(k == 0)
        def _():
            acc_ref[...] = jnp.zeros_like(acc_ref)

        if hi:
            acc_ref[...] += _dot_hi(a_ref[...].astype(F32), b_ref[...].astype(F32), mode)
        else:
            acc_ref[...] += _dot_raw(a_ref[...], b_ref[...], mode)

        @pl.when(k == nk - 1)
        def _():
            acc = acc_ref[...]
            outs = (acc,) if epi is None else epi(acc, *[e[...].astype(F32) for e in e_refs])
            for o_ref, o in zip(o_refs, outs):
                o_ref[...] = o.astype(o_ref.dtype)

    outs = pl.pallas_call(
        body, name=name, grid=grid,
        in_specs=[a_spec, b_spec] + list(extra_specs),
        out_specs=[out_spec] * n_o,
        out_shape=list(out_shapes),
        scratch_shapes=[pltpu.VMEM(acc_shape, F32)],
        compiler_params=_params(("parallel", "parallel", "arbitrary")),
    )(a, b, *extras)
    return outs if n_o > 1 else outs[0]


def _row_spec(th, cw, ci):
    return pl.BlockSpec((th, cw), lambda i: (i, ci))


def _full_spec(shape):
    return pl.BlockSpec(shape, lambda i: (0,) * len(shape))


def _rowwise(fn, rows, params, outs, n_steps, name):
    n_r, n_p, n_o = len(rows), len(params), len(outs)

    def body(*refs):
        vals = [r[...].astype(F32) for r in refs[:n_r + n_p]]
        res = fn(*vals)
        for o_ref, o in zip(refs[n_r + n_p:], res):
            o_ref[...] = o.astype(o_ref.dtype)

    across = [len(o) == 4 for o in outs]
    res = pl.pallas_call(
        body, name=name, grid=(n_steps,),
        in_specs=[_row_spec(th, cw, ci) for (_, th, cw, ci) in rows] + [_full_spec(p.shape) for p in params],
        out_specs=[pl.BlockSpec((o[0], o[1]), lambda i: (0, i)) if ac else _row_spec(o[0], o[1], 0)
                   for o, ac in zip(outs, across)],
        out_shape=[jax.ShapeDtypeStruct((o[0], n_steps * o[1]) if ac else (n_steps * o[0], o[1]), o[2])
                   for o, ac in zip(outs, across)],
        compiler_params=_params(("parallel",)),
    )(*[r[0] for r in rows], *params)
    return res


def _rowwise_bwd(fn, rows, aux, params, douts, n_steps, name, row_dtypes=None, adds=()):
    n_r, n_a, n_p, n_d, n_add = len(rows), len(aux), len(params), len(douts), len(adds)
    row_dtypes = row_dtypes or (F32,) * n_r

    def body(*refs):
        it = iter(refs)
        r_vals = [next(it)[...].astype(F32) for _ in range(n_r)]
        a_vals = [next(it)[...].astype(F32) for _ in range(n_a)]
        p_vals = [next(it)[...].astype(F32) for _ in range(n_p)]
        d_vals = [next(it)[...].astype(F32) for _ in range(n_d)]
        add_vals = [next(it)[...].astype(F32) for _ in range(n_add)]
        dr_refs = [next(it) for _ in range(n_r)]
        dp_refs = [next(it) for _ in range(n_p)]

        def f(*rp):
            return tuple(fn(*rp[:n_r], *a_vals, *rp[n_r:]))

        _, vjp = jax.vjp(f, *r_vals, *p_vals)
        grads = list(vjp(tuple(d_vals)))
        for (ri, _), av in zip(adds, add_vals):
            grads[ri] = grads[ri] + av
        for dr_ref, g in zip(dr_refs, grads[:n_r]):
            dr_ref[...] = g.astype(dr_ref.dtype)

        @pl.when(pl.program_id(0) == 0)
        def _():
            for dp_ref in dp_refs:
                dp_ref[...] = jnp.zeros_like(dp_ref)

        for dp_ref, g in zip(dp_refs, grads[n_r:]):
            dp_ref[...] += g

    all_rows = list(rows) + list(aux) + list(douts) + [(arr,) + tuple(rows[ri][1:3]) + (0,) for ri, arr in adds]
    in_specs = ([_row_spec(th, cw, ci) for (_, th, cw, ci) in list(rows) + list(aux)]
                + [_full_spec(p.shape) for p in params]
                + [_row_spec(th, cw, ci) for (_, th, cw, ci) in all_rows[n_r + n_a:]])
    res = pl.pallas_call(
        body, name=name, grid=(n_steps,),
        in_specs=in_specs,
        out_specs=[_row_spec(th, cw, 0) for (_, th, cw, _) in rows] + [_full_spec(p.shape) for p in params],
        out_shape=[jax.ShapeDtypeStruct((n_steps * th, cw), dt) for (_, th, cw, _), dt in zip(rows, row_dtypes)]
        + [jax.ShapeDtypeStruct(p.shape, F32) for p in params],
        compiler_params=_params(("arbitrary",)),
    )(*[r[0] for r in list(rows) + list(aux)], *params, *[r[0] for r in all_rows[n_r + n_a:]])
    return res[:n_r], res[n_r:]


def _sigmoid(x):
    return lax.logistic(x)


def _silu(x):
    return x * _sigmoid(x)


def _rms(x, w=None, n=None):
    n = n or x.shape[-1]
    y = x * lax.rsqrt(jnp.sum(x * x, axis=-1, keepdims=True) * (1.0 / n) + EPS)
    return y if w is None else y * w


def _modulate(x, scale, shift):
    return _rms(x) * (1.0 + scale) + shift


def _softplus(x):
    return jnp.maximum(x, 0.0) + jnp.log1p(jnp.exp(-jnp.abs(x)))


@jax.custom_vjp
def _rot_half64(x):
    lane = lax.broadcasted_iota(jnp.int32, x.shape, 1)
    up = pltpu.roll(x, 96, 1)
    down = pltpu.roll(x, 32, 1)
    return jnp.where(lane < 32, up, jnp.where(lane < 64, down, 0.0))


_rot_half64.defvjp(lambda x: (_rot_half64(x), None), lambda _, g: (_rot_half64(g),))


def _rope128(x, cos_p, sin_p):
    return x * cos_p + _rot_half64(x) * sin_p


def _gdn_pre_fn(qkvc, kab, alog_p, dt_p):
    a = _silu(qkvc)
    qs, ks = [], []
    for h in range(HEADS):
        qh = a[:, HEAD_DIM * h:HEAD_DIM * (h + 1)]
        kh = a[:, 512 + HEAD_DIM * h:512 + HEAD_DIM * (h + 1)]
        qs.append(qh * lax.rsqrt(jnp.sum(qh * qh, axis=-1, keepdims=True) + EPS) * (HEAD_DIM ** -0.5))
        ks.append(kh * lax.rsqrt(jnp.sum(kh * kh, axis=-1, keepdims=True) + EPS))
    lane = lax.broadcasted_iota(jnp.int32, kab.shape, 1)
    g_full = -jnp.exp(alog_p) * _softplus(kab + dt_p)
    b_full = _sigmoid(kab)
    gb = jnp.where((lane >= 64) & (lane < 68), g_full, jnp.where((lane >= 68) & (lane < 72), b_full, 0.0))
    return jnp.concatenate(qs, axis=1), jnp.concatenate(ks, axis=1), a[:, 1024:1536], gb


INTRA_ROWS = (512, 256, 128, 64)
TOKEN_ROWS = (512, 256, 128)

_BNN = (((2,), (1,)), ((0,), (0,)))
_BNT = (((2,), (2,)), ((0,), (0,)))
_BTN = (((1,), (1,)), ((0,), (0,)))


def _split_bf16(a):
    hi = a.astype(BF16)
    return hi, (a - hi.astype(F32)).astype(BF16)


def _dot3_raw(a, b, dims):
    a_hi, a_lo = _split_bf16(a)
    b_hi, b_lo = _split_bf16(b)
    dot = lambda x_, y_: lax.dot_general(x_, y_, dims, preferred_element_type=F32)
    return dot(a_hi, b_hi) + (dot(a_hi, b_lo) + dot(a_lo, b_hi))


@functools.partial(jax.custom_vjp, nondiff_argnums=(2, 3))
def _dot3(a, b, nt, exact_bwd=True):
    return _dot3_raw(a, b, _BNT if nt else _BNN)


def _dot3_fwd(a, b, nt, exact_bwd):
    return _dot3_raw(a, b, _BNT if nt else _BNN), (a, b)


def _dot3_bwd(nt, exact_bwd, res, g):
    a, b = res
    if exact_bwd:
        dot = _dot3_raw
    else:
        dot = lambda x_, y_, d_: lax.dot_general(x_.astype(BF16), y_.astype(BF16), d_, preferred_element_type=F32)
    if nt:
        return dot(g, b, _BNN), dot(jnp.swapaxes(g, 1, 2), a, _BNN)
    return dot(g, b, _BNT), dot(jnp.swapaxes(a, 1, 2), g, _BNN)


_dot3.defvjp(_dot3_fwd, _dot3_bwd)


@functools.partial(jax.custom_vjp, nondiff_argnums=(2,))
def _bdot_b(a, b, nt):
    return lax.dot_general(a.astype(BF16), b.astype(BF16), _BNT if nt else _BNN, preferred_element_type=F32)


def _bdot_b_fwd(a, b, nt):
    return _bdot_b(a, b, nt), (a, b)


def _bdot_b_bwd(nt, res, g):
    a, b = res
    dot = lambda x_, y_, d_: lax.dot_general(x_.astype(BF16), y_.astype(BF16), d_, preferred_element_type=F32)
    if nt:
        return dot(g, b, _BNN), dot(jnp.swapaxes(g, 1, 2), a, _BNN)
    return dot(g, b, _BNT), dot(jnp.swapaxes(a, 1, 2), g, _BNN)


_bdot_b.defvjp(_bdot_b_fwd, _bdot_b_bwd)


@jax.custom_vjp
def _inverse_given(a_mat, inv):
    return inv


def _inverse_given_bwd(inv, g):
    inv_t = jnp.swapaxes(inv, 1, 2)
    return -_dot3_raw(_dot3_raw(inv_t, g, _BNN), inv_t, _BNN), jnp.zeros_like(inv)


_inverse_given.defvjp(lambda a_mat, inv: (inv, inv), _inverse_given_bwd)


def _intra_batched(q, k, v, g_col, b_col, inv_known=None):
    c = CHUNK
    nb = q.shape[0]
    row = lax.broadcasted_iota(jnp.int32, (1, c, c), 1)
    col = lax.broadcasted_iota(jnp.int32, (1, c, c), 2)
    incl, strict, eye = row >= col, row > col, row == col
    tri = jnp.broadcast_to(jnp.where(incl, 1.0, 0.0).astype(F32), (nb, c, c))
    ident = jnp.where(eye, 1.0, 0.0).astype(F32)
    g_wide = _dot3(tri, jnp.broadcast_to(g_col, (nb, c, HEAD_DIM)), False)
    g_i = g_wide[:, :, :c]
    g_j = jnp.sum(jnp.where(eye, g_i, 0.0), axis=1, keepdims=True)
    decay = jnp.where(incl, jnp.exp(jnp.where(incl, g_i - g_j, 0.0)), 0.0)
    kk = _bdot_b(k, k, True)
    a_mat = jnp.where(strict, b_col * kk * decay, 0.0)
    if inv_known is None:
        x_pow = -a_mat
        inv = ident + x_pow
        for _ in range(5):
            x_pow = _dot3(x_pow, x_pow, False, False)
            inv = inv + _dot3(inv, x_pow, False, False)
    else:
        inv = _inverse_given(a_mat, inv_known)
    e_wide = jnp.exp(g_wide)
    u = _dot3(inv, v * b_col, False)
    wk = _dot3(inv, k * b_col * e_wide, False)
    qk = _bdot_b(q, k, True) * decay
    last = lax.broadcasted_iota(jnp.int32, (1, c, HEAD_DIM), 1) == c - 1
    g_last = jnp.sum(jnp.where(last, g_wide, 0.0), axis=1, keepdims=True)
    qd = q * e_wide
    kd = k * jnp.exp(g_last - g_wide)
    gl = jnp.broadcast_to(jnp.exp(g_last), (nb, 8, HEAD_DIM))
    return u, wk, qd, kd, qk, gl, inv


def _gdn_intra_fn(q, k, v, gb, *inv_known):
    t = q.shape[0]
    nch = t // CHUNK
    lane = lax.broadcasted_iota(jnp.int32, gb.shape, 1)

    def heads_first(x_):
        return jnp.concatenate([x_[:, HEAD_DIM * h:HEAD_DIM * (h + 1)].reshape(nch, CHUNK, HEAD_DIM) for h in range(HEADS)],
                               axis=0)

    def column(first_lane):
        return jnp.concatenate([jnp.sum(jnp.where(lane == first_lane + h, gb, 0.0), axis=1, keepdims=True)
                                .reshape(nch, CHUNK, 1) for h in range(HEADS)], axis=0)

    known = jnp.concatenate([x_.reshape(nch, CHUNK, CHUNK) for x_ in inv_known], axis=0) if inv_known else None
    u, wk, qd, kd, qk, gl, inv = _intra_batched(heads_first(q), heads_first(k), heads_first(v), column(64), column(68), known)

    def rows_first(x_):
        r, w_ = x_.shape[1], x_.shape[2]
        return jnp.concatenate([x_[nch * h:nch * (h + 1)].reshape(nch * r, w_) for h in range(HEADS)], axis=1)

    per_head = lambda x_: [x_[nch * h:nch * (h + 1)].reshape(t, CHUNK) for h in range(HEADS)]
    outs = (rows_first(u), rows_first(wk), rows_first(qd), rows_first(kd), *per_head(qk), rows_first(gl))
    return outs if inv_known else outs + tuple(per_head(inv))


def _scan_step(s0, u, wk, qd, kd, qk, gl):
    v_new = u - _bdot_b(wk, s0, False)
    o = _bdot_b(qd, s0, False) + _bdot_b(qk, v_new, False)
    s1 = s0 * gl[:, 0:1, :] + _bdot_b(jnp.swapaxes(kd, 1, 2), v_new, False)
    return o, s1


def _mix_post_fn(o_a, z, o_b, gnw, onw):
    parts = [_rms(o_a[:, HEAD_DIM * h:HEAD_DIM * (h + 1)], gnw) * _silu(z[:, HEAD_DIM * h:HEAD_DIM * (h + 1)])
             for h in range(HEADS)]
    parts += [_rms(o_b[:, HEAD_DIM * h:HEAD_DIM * (h + 1)], onw) for h in range(HEADS)]
    return (jnp.concatenate(parts, axis=1),)


def _mla_pre_fn(ckv, cq, kab, cos_p, sin_p, qnw, kvnw, wuq, wukv, qn_w, qr_w, kn_w, kr_w):
    scale = (HEAD_DIM + ROPE) ** -0.5 * LOG2_E
    qf = _bdot(_rms(cq, qnw), wuq, "nn")
    kvf = _bdot(_rms(ckv, kvnw), wukv, "nn")
    lane = lax.broadcasted_iota(jnp.int32, kab.shape, 1)
    kr = _rope128(_rms(jnp.where(lane < ROPE, kab, 0.0), kr_w, n=ROPE), cos_p, sin_p)
    qs, ks = [], []
    for h in range(HEADS):
        qn = _rms(qf[:, 256 * h:256 * h + 128], qn_w) * scale
        qr = _rope128(_rms(qf[:, 256 * h + 128:256 * h + 256], qr_w, n=ROPE), cos_p, sin_p) * scale
        qs += [qn, qr]
        ks += [_rms(kvf[:, 128 * h:128 * (h + 1)], kn_w), kr]
    return jnp.concatenate(qs, axis=1), jnp.concatenate(ks, axis=1), kvf[:, 512:]


def _conv_fwd(proj, conv_w, tm, name):
    S = proj.shape[0]
    C = 1536
    nb = tm // 8

    def body(x_ref, prev_ref, w_ref, o_ref, ext_ref):
        i = pl.program_id(0)
        ext_ref[0:8, :] = jnp.where(i > 0, prev_ref[...], 0.0)
        ext_ref[8:, :] = x_ref[...]
        acc = jnp.zeros((tm, C), F32)
        for k in range(4):
            acc = acc + w_ref[k:k + 1, :] * ext_ref[pl.ds(5 + k, tm), :]
        o_ref[...] = acc

    return pl.pallas_call(
        body, name=name, grid=(S // tm,),
        in_specs=[pl.BlockSpec((tm, C), lambda i: (i, 0)),
                  pl.BlockSpec((8, C), lambda i: (jnp.maximum(i * nb - 1, 0), 0)),
                  pl.BlockSpec((4, C), lambda i: (0, 0))],
        out_specs=pl.BlockSpec((tm, C), lambda i: (i, 0)),
        out_shape=jax.ShapeDtypeStruct((S, C), F32),
        scratch_shapes=[pltpu.VMEM((tm + 8, C), F32)],
        compiler_params=_params(("arbitrary",)),
    )(proj, proj, conv_w)


def _conv_bwd(proj, dout, conv_w, tm, name):
    S = proj.shape[0]
    C = 1536
    nb = tm // 8
    n_steps = S // tm

    def body(x_ref, prev_ref, d_ref, next_ref, w_ref, dx_ref, dw_ref, xext_ref, dext_ref):
        i = pl.program_id(0)
        xext_ref[0:8, :] = jnp.where(i > 0, prev_ref[...], 0.0)
        xext_ref[8:, :] = x_ref[...]
        dext_ref[0:tm, :] = d_ref[...]
        dext_ref[tm:, :] = jnp.where(i < n_steps - 1, next_ref[...], 0.0)
        d = d_ref[...]
        acc = jnp.zeros((tm, C), F32)
        dws = []
        for k in range(4):
            acc = acc + w_ref[k:k + 1, :] * dext_ref[pl.ds(3 - k, tm), :]
            dws.append(jnp.sum(d * xext_ref[pl.ds(5 + k, tm), :], axis=0, keepdims=True))
        dx_ref[...] = acc.astype(dx_ref.dtype)

        @pl.when(i == 0)
        def _():
            dw_ref[...] = jnp.zeros_like(dw_ref)

        dw_ref[...] += jnp.concatenate(dws + [jnp.zeros((4, C), F32)], axis=0)

    return pl.pallas_call(
        body, name=name, grid=(n_steps,),
        in_specs=[pl.BlockSpec((tm, C), lambda i: (i, 0)),
                  pl.BlockSpec((8, C), lambda i: (jnp.maximum(i * nb - 1, 0), 0)),
                  pl.BlockSpec((tm, C), lambda i: (i, 0)),
                  pl.BlockSpec((8, C), lambda i: (jnp.minimum((i + 1) * nb, S // 8 - 1), 0)),
                  pl.BlockSpec((4, C), lambda i: (0, 0))],
        out_specs=[pl.BlockSpec((tm, C), lambda i: (i, 0)), pl.BlockSpec((8, C), lambda i: (0, 0))],
        out_shape=[jax.ShapeDtypeStruct((S, C), BF16), jax.ShapeDtypeStruct((8, C), F32)],
        scratch_shapes=[pltpu.VMEM((tm + 8, C), F32), pltpu.VMEM((tm + 8, C), F32)],
        compiler_params=_params(("arbitrary",)),
    )(proj, proj, dout, dout, conv_w)


SCAN_CHUNKS = (8, 4, 2, 1)


def _gdn_scan_fwd(u, wk, qd, kd, qks, gl, name):
    S = u.shape[0]
    nc = S // CHUNK
    cs = _pick(nc, SCAN_CHUNKS)
    W = HEADS * HEAD_DIM

    def body(u_ref, wk_ref, qd_ref, kd_ref, qk0, qk1, qk2, qk3, gl_ref, o_ref, sp_ref, s_ref):
        @pl.when(pl.program_id(0) == 0)
        def _():
            s_ref[...] = jnp.zeros_like(s_ref)

        state = s_ref[...]
        for c in range(cs):
            rows, gl_rows = slice(CHUNK * c, CHUNK * (c + 1)), slice(8 * c, 8 * (c + 1))
            sp_ref[c] = state
            o, state = _scan_step(state, _heads(u_ref, HEAD_DIM, rows), _heads(wk_ref, HEAD_DIM, rows),
                                  _heads(qd_ref, HEAD_DIM, rows), _heads(kd_ref, HEAD_DIM, rows),
                                  jnp.stack([r[rows, :] for r in (qk0, qk1, qk2, qk3)]), _heads(gl_ref, HEAD_DIM, gl_rows))
            for h in range(HEADS):
                o_ref[rows, HEAD_DIM * h:HEAD_DIM * (h + 1)] = o[h]
        s_ref[...] = state

    row = pl.BlockSpec((cs * CHUNK, W), lambda n: (n, 0))
    qk_spec = pl.BlockSpec((cs * CHUNK, CHUNK), lambda n: (n, 0))
    return pl.pallas_call(
        body, name=name, grid=(nc // cs,),
        in_specs=[row, row, row, row, qk_spec, qk_spec, qk_spec, qk_spec, pl.BlockSpec((cs * 8, W), lambda n: (n, 0))],
        out_specs=[row, pl.BlockSpec((cs, HEADS, HEAD_DIM, HEAD_DIM), lambda n: (n, 0, 0, 0))],
        out_shape=[jax.ShapeDtypeStruct((S, W), F32), jax.ShapeDtypeStruct((nc, HEADS, HEAD_DIM, HEAD_DIM), F32)],
        scratch_shapes=[pltpu.VMEM((HEADS, HEAD_DIM, HEAD_DIM), F32)],
        compiler_params=_params(("arbitrary",)),
    )(u, wk, qd, kd, *qks, gl)


def _gdn_scan_bwd(u, wk, qd, kd, qks, gl, s_prev, d_o, name):
    S = u.shape[0]
    nc = S // CHUNK
    cs = _pick(nc, SCAN_CHUNKS)
    nb = nc // cs
    W = HEADS * HEAD_DIM

    def body(u_ref, wk_ref, qd_ref, kd_ref, qk0, qk1, qk2, qk3, gl_ref, sp_ref, do_ref,
             du_ref, dwk_ref, dqd_ref, dkd_ref, dqk0, dqk1, dqk2, dqk3, dgl_ref, ds_ref):
        @pl.when(pl.program_id(0) == 0)
        def _():
            ds_ref[...] = jnp.zeros_like(ds_ref)

        d_state = ds_ref[...]
        for c in reversed(range(cs)):
            rows, gl_rows = slice(CHUNK * c, CHUNK * (c + 1)), slice(8 * c, 8 * (c + 1))
            _, vjp = jax.vjp(_scan_step, sp_ref[c], _heads(u_ref, HEAD_DIM, rows), _heads(wk_ref, HEAD_DIM, rows),
                             _heads(qd_ref, HEAD_DIM, rows), _heads(kd_ref, HEAD_DIM, rows),
                             jnp.stack([r[rows, :] for r in (qk0, qk1, qk2, qk3)]), _heads(gl_ref, HEAD_DIM, gl_rows))
            d_state, du, dwk, dqd, dkd, dqk, dgl = vjp((_heads(do_ref, HEAD_DIM, rows), d_state))
            for h, dqk_ref in enumerate((dqk0, dqk1, dqk2, dqk3)):
                sl = slice(HEAD_DIM * h, HEAD_DIM * (h + 1))
                du_ref[rows, sl] = du[h]
                dwk_ref[rows, sl] = dwk[h]
                dqd_ref[rows, sl] = dqd[h]
                dkd_ref[rows, sl] = dkd[h]
                dqk_ref[rows, :] = dqk[h]
                dgl_ref[gl_rows, sl] = dgl[h]
        ds_ref[...] = d_state

    rev = lambda n: (nb - 1 - n, 0)
    row = pl.BlockSpec((cs * CHUNK, W), rev)
    qk_spec = pl.BlockSpec((cs * CHUNK, CHUNK), rev)
    gl_spec = pl.BlockSpec((cs * 8, W), rev)
    qk_shape = jax.ShapeDtypeStruct((S, CHUNK), F32)
    row_shape = jax.ShapeDtypeStruct((S, W), F32)
    return pl.pallas_call(
        body, name=name, grid=(nb,),
        in_specs=[row, row, row, row, qk_spec, qk_spec, qk_spec, qk_spec, gl_spec,
                  pl.BlockSpec((cs, HEADS, HEAD_DIM, HEAD_DIM), lambda n: (nb - 1 - n, 0, 0, 0)), row],
        out_specs=[row, row, row, row, qk_spec, qk_spec, qk_spec, qk_spec, gl_spec],
        out_shape=[row_shape] * 4 + [qk_shape] * 4 + [jax.ShapeDtypeStruct((nc * 8, W), F32)],
        scratch_shapes=[pltpu.VMEM((HEADS, HEAD_DIM, HEAD_DIM), F32)],
        compiler_params=_params(("arbitrary",)),
    )(u, wk, qd, kd, *qks, gl, s_prev, d_o)


NEG = -1e30


def _chunk_mask(i, j, t, transposed=False):
    q_axis, k_axis = (1, 0) if transposed else (0, 1)
    r = (i * t + lax.broadcasted_iota(jnp.int32, (t, t), q_axis)) // CHUNK
    c = (j * t + lax.broadcasted_iota(jnp.int32, (t, t), k_axis)) // CHUNK
    return c <= r


def _tile_pairs(n, by_key):
    pairs = [(i, j) for j in range(n) for i in range(j, n)] if by_key else [(i, j) for i in range(n) for j in range(i + 1)]
    return jnp.asarray(np.array([p[0] for p in pairs], np.int32)), jnp.asarray(np.array([p[1] for p in pairs], np.int32))


def _heads(ref, width, rows=slice(None)):
    return jnp.stack([ref[rows, width * h:width * (h + 1)] for h in range(HEADS)])


def _bmm(a, b, dims):
    return lax.dot_general(a.astype(BF16), b.astype(BF16), dims, preferred_element_type=F32)


def _attn_fwd(q, k, v_t, t, name):
    S = q.shape[0]
    n = S // t
    qi, kj = _tile_pairs(n, by_key=False)

    def body(qi_ref, kj_ref, q_ref, k_ref, vt_ref, o_ref, lse_ref, m_ref, l_ref, acc_ref):
        i, j = qi_ref[pl.program_id(0)], kj_ref[pl.program_id(0)]

        @pl.when(j == 0)
        def _():
            m_ref[...] = jnp.full_like(m_ref, NEG)
            l_ref[...] = jnp.zeros_like(l_ref)
            acc_ref[...] = jnp.zeros_like(acc_ref)

        def update(masked):
            s_t = _bmm(_heads(k_ref, 256), _heads(q_ref, 256), _BNT)
            if masked:
                s_t = jnp.where(_chunk_mask(i, j, t, transposed=True)[None], s_t, NEG)
            m_old = m_ref[...]
            m_new = jnp.maximum(m_old, jnp.max(s_t, axis=1, keepdims=True))
            p_t = jnp.exp2(s_t - m_new)
            alpha = jnp.exp2(m_old - m_new)
            l_ref[...] = alpha * l_ref[...] + jnp.sum(p_t, axis=1, keepdims=True)
            v_heads = jnp.stack([vt_ref[HEAD_DIM * h:HEAD_DIM * (h + 1), :] for h in range(HEADS)])
            acc_ref[...] = alpha * acc_ref[...] + _bmm(v_heads, p_t, _BNN)
            m_ref[...] = m_new

        @pl.when(j < i)
        def _():
            update(False)

        @pl.when(j == i)
        def _():
            update(True)
            for h in range(HEADS):
                sl = slice(HEAD_DIM * h, HEAD_DIM * (h + 1))
                o_ref[:, sl] = jnp.transpose(acc_ref[h] / l_ref[h])
                lse_ref[h:h + 1, :] = m_ref[h] + jnp.log(l_ref[h]) * LOG2_E
            lse_ref[HEADS:, :] = jnp.zeros((8 - HEADS, t), F32)

    row = lambda p, qi_, kj_: (qi_[p], 0)
    return pl.pallas_call(
        body, name=name,
        grid_spec=pltpu.PrefetchScalarGridSpec(
            num_scalar_prefetch=2, grid=(qi.shape[0],),
            in_specs=[pl.BlockSpec((t, HEADS * 256), row), pl.BlockSpec((t, HEADS * 256), lambda p, qi_, kj_: (kj_[p], 0)),
                      pl.BlockSpec((HEADS * HEAD_DIM, t), lambda p, qi_, kj_: (0, kj_[p]))],
            out_specs=[pl.BlockSpec((t, HEADS * HEAD_DIM), row), pl.BlockSpec((8, t), lambda p, qi_, kj_: (0, qi_[p]))],
            scratch_shapes=[pltpu.VMEM((HEADS, 1, t), F32), pltpu.VMEM((HEADS, 1, t), F32),
                            pltpu.VMEM((HEADS, HEAD_DIM, t), F32)]),
        out_shape=[jax.ShapeDtypeStruct((S, HEADS * HEAD_DIM), F32), jax.ShapeDtypeStruct((8, S), F32)],
        compiler_params=_params(("arbitrary",)),
    )(qi, kj, q, k, v_t)


def _attn_stats(o, lse, d_o, t, name):
    S = o.shape[0]

    def body(o_ref, lse_ref, do_ref, st_ref):
        lane = lax.broadcasted_iota(jnp.int32, (t, HEAD_DIM), 1)
        stats = jnp.zeros((t, HEAD_DIM), F32)
        for h in range(HEADS):
            sl = slice(HEAD_DIM * h, HEAD_DIM * (h + 1))
            delta = jnp.sum(do_ref[:, sl] * o_ref[:, sl], axis=1, keepdims=True)
            stats = stats + jnp.where(lane == HEADS + h, delta, 0.0)
        st_ref[...] = lse_ref[...] + jnp.transpose(stats)[0:8, :]

    row = pl.BlockSpec((t, HEADS * HEAD_DIM), lambda i: (i, 0))
    col = pl.BlockSpec((8, t), lambda i: (0, i))
    return pl.pallas_call(
        body, name=name, grid=(S // t,),
        in_specs=[row, col, row], out_specs=col,
        out_shape=jax.ShapeDtypeStruct((8, S), F32),
        compiler_params=_params(("parallel",)),
    )(o, lse, d_o)


BWD_GROUP = 2


def _attn_bwd(q, k, v, d_o, stats, t, name):
    S = q.shape[0]
    n = S // t
    groups = HEADS // BWD_GROUP
    gq, gv = BWD_GROUP * 256, BWD_GROUP * HEAD_DIM
    qi, kj = _tile_pairs(n, by_key=True)
    n_pairs = qi.shape[0]
    st = stats.reshape(2, groups, BWD_GROUP, S).transpose(1, 0, 2, 3).reshape(groups, 2 * BWD_GROUP, S)
    st = jnp.pad(st, ((0, 0), (0, 8 - 2 * BWD_GROUP), (0, 0)))

    def heads(ref, width, rows=slice(None)):
        return jnp.stack([ref[rows, width * h:width * (h + 1)] for h in range(BWD_GROUP)])

    def body(qi_ref, kj_ref, q_ref, k_ref, v_ref, do_ref, st_ref, dq_hbm, dk_ref, dv_ref, dq_acc, sem):
        g, p = pl.program_id(0), pl.program_id(1)
        i, j = qi_ref[p], kj_ref[p]

        @pl.when(i == j)
        def _():
            dk_ref[...] = jnp.zeros_like(dk_ref)
            dv_ref[...] = jnp.zeros_like(dv_ref)

        def update(masked):
            qh, kh = heads(q_ref, 256), heads(k_ref, 256)
            d_out = heads(do_ref, HEAD_DIM)
            stv = st_ref[...]
            lse_row = jnp.stack([stv[h:h + 1, :] for h in range(BWD_GROUP)])
            delta_row = jnp.stack([stv[BWD_GROUP + h:BWD_GROUP + h + 1, :] for h in range(BWD_GROUP)])
            s_t = _bmm(kh, qh, _BNT)
            p_t = jnp.exp2(s_t - lse_row)
            if masked:
                p_t = jnp.where(_chunk_mask(i, j, t, transposed=True)[None], p_t, 0.0)
            dv = _bmm(p_t, d_out, _BNN)
            dp_t = _bmm(heads(v_ref, HEAD_DIM), d_out, _BNT)
            ds_t = p_t * (dp_t - delta_row)
            dk = _bmm(ds_t, qh, _BNN)
            dq = _bmm(ds_t, kh, _BTN)
            rows = pl.ds(pl.multiple_of(i * t, t), t)
            for h in range(BWD_GROUP):
                dk_ref[:, 256 * h:256 * (h + 1)] += dk[h]
                dv_ref[:, HEAD_DIM * h:HEAD_DIM * (h + 1)] += dv[h]

            @pl.when(j == 0)
            def _():
                for h in range(BWD_GROUP):
                    dq_acc[rows, 256 * h:256 * (h + 1)] = dq[h]

            @pl.when(j > 0)
            def _():
                for h in range(BWD_GROUP):
                    dq_acc[rows, 256 * h:256 * (h + 1)] += dq[h]

        @pl.when(i == j)
        def _():
            update(True)

        @pl.when(i > j)
        def _():
            update(False)

        @pl.when(i == n - 1)
        def _():
            dk_ref[...] *= 1.0 / LOG2_E

        @pl.when(p == n_pairs - 1)
        def _():
            dq_acc[...] *= 1.0 / LOG2_E
            for gg in range(groups):
                @pl.when(g == gg)
                def _():
                    cp = pltpu.make_async_copy(dq_acc, dq_hbm.at[:, gq * gg:gq * (gg + 1)], sem)
                    cp.start()
                    cp.wait()

    q_blk = lambda g, p, qi_, kj_: (qi_[p], g)
    k_blk = lambda g, p, qi_, kj_: (kj_[p], g)
    return pl.pallas_call(
        body, name=name,
        grid_spec=pltpu.PrefetchScalarGridSpec(
            num_scalar_prefetch=2, grid=(groups, n_pairs),
            in_specs=[pl.BlockSpec((t, gq), q_blk), pl.BlockSpec((t, gq), k_blk), pl.BlockSpec((t, gv), k_blk),
                      pl.BlockSpec((t, gv), q_blk), pl.BlockSpec((None, 8, t), lambda g, p, qi_, kj_: (g, 0, qi_[p]))],
            out_specs=[pl.BlockSpec(memory_space=pl.ANY), pl.BlockSpec((t, gq), k_blk), pl.BlockSpec((t, gv), k_blk)],
            scratch_shapes=[pltpu.VMEM((S, gq), F32), pltpu.SemaphoreType.DMA]),
        out_shape=[jax.ShapeDtypeStruct((S, HEADS * 256), F32), jax.ShapeDtypeStruct((S, HEADS * 256), F32),
                   jax.ShapeDtypeStruct((S, HEADS * HEAD_DIM), F32)],
        compiler_params=_params(("arbitrary", "arbitrary")),
    )(qi, kj, q, k, v, d_o, st)


FFN_PIECE = 2 * D_FF // N_DEV
HID_PIECES = D_FF // FFN_PIECE


def _after_specs(after):
    return [] if after is None else [pl.BlockSpec(memory_space=pl.ANY)]


def _after_args(after):
    return [] if after is None else [after]


def _ffn_gw8(h, d_gate, d_up, name, after=None):
    S = h.shape[0]
    tm = 512
    tk = _pick(S, MATMUL_ROWS)
    nk = S // tk

    def body(h_ref, dg_ref, du_ref, *rest):
        o_ref, acc_ref = rest[-2:]
        k = pl.program_id(1)

        @pl.when(k == 0)
        def _():
            acc_ref[...] = jnp.zeros_like(acc_ref)

        h_t = jnp.transpose(h_ref[...])
        for p in range(HID_PIECES):
            acc_ref[p] += _dot_raw(h_t, dg_ref[p], "nn")
            acc_ref[HID_PIECES + p] += _dot_raw(h_t, du_ref[p], "nn")

        @pl.when(k == nk - 1)
        def _():
            o_ref[...] = acc_ref[...].astype(o_ref.dtype)

    d_spec = pl.BlockSpec((HID_PIECES, tk, FFN_PIECE), lambda i, k: (0, k, 0))
    return pl.pallas_call(
        body, name=name, grid=(D_MODEL // tm, nk),
        in_specs=[pl.BlockSpec((tk, tm), lambda i, k: (k, i)), d_spec, d_spec] + _after_specs(after),
        out_specs=pl.BlockSpec((2 * HID_PIECES, tm, FFN_PIECE), lambda i, k: (0, i, 0)),
        out_shape=jax.ShapeDtypeStruct((2 * HID_PIECES, D_MODEL, FFN_PIECE), BF16),
        scratch_shapes=[pltpu.VMEM((2 * HID_PIECES, tm, FFN_PIECE), F32)],
        compiler_params=_params(("parallel", "arbitrary")),
    )(h, d_gate, d_up, *_after_args(after))


EPILOGUE_ROWS = 256


def _dmod_epilogue(acc_ref, x_ref, do_ref, sc_ref, sh_ref, dx_ref, dsc_ref, dsh_ref, below, below_in, below_out):
    rows_total = acc_ref.shape[0]
    step = min(EPILOGUE_ROWS, rows_total)
    dsc, dsh, dg = 0.0, 0.0, 0.0
    for r in range(rows_total // step):
        rows = slice(step * r, step * (r + 1))
        _, vjp = jax.vjp(_modulate, x_ref[rows, :], sc_ref[...], sh_ref[...])
        dx, dsc_r, dsh_r = vjp(acc_ref[rows, :])
        dx = dx + do_ref[rows, :]
        dx_ref[rows, :] = dx
        dsc, dsh = dsc + dsc_r, dsh + dsh_r
        if below is not None:
            coef = below[2]
            below_out[0][rows, :] = (coef * below_in[1][...] * dx).astype(below_out[0].dtype)
            dg = dg + jnp.sum(coef * below_in[0][rows, :] * dx, axis=0, keepdims=True)
    dsc_ref[...] += dsc
    dsh_ref[...] += dsh
    if below is not None:
        below_out[1][...] += dg


def _ffn_dh(d_gate, d_up, w8, x, d_out, scale, shift, name, after=None, below=None):
    S = d_gate.shape[1]
    tm = _pick(S, MATMUL_ROWS)
    n_below = 0 if below is None else 2

    def body(dg_ref, du_ref, wg_ref, wu_ref, x_ref, do_ref, sc_ref, sh_ref, *rest):
        below_in = rest[:n_below]
        outs = rest[len(rest) - 4 - n_below:]
        dx_ref, dsc_ref, dsh_ref = outs[:3]
        below_out, acc_ref = outs[3:3 + n_below], outs[-1]
        i, k = pl.program_id(0), pl.program_id(1)

        @pl.when(k == 0)
        def _():
            acc_ref[...] = jnp.zeros_like(acc_ref)

        acc_ref[...] += _dot_raw(dg_ref[...], wg_ref[...], "nt") + _dot_raw(du_ref[...], wu_ref[...], "nt")

        @pl.when((k == 0) & (i == 0))
        def _():
            for r in (dsc_ref, dsh_ref) + tuple(below_out[1:]):
                r[...] = jnp.zeros_like(r)

        @pl.when(k == HID_PIECES - 1)
        def _():
            _dmod_epilogue(acc_ref, x_ref, do_ref, sc_ref, sh_ref, dx_ref, dsc_ref, dsh_ref, below, below_in, below_out)

    d_spec = pl.BlockSpec((None, tm, FFN_PIECE), lambda i, k: (k, i, 0))
    row = pl.BlockSpec((tm, D_MODEL), lambda i, k: (i, 0))
    par = pl.BlockSpec((1, D_MODEL), lambda i, k: (0, 0))
    row_shape, par_shape = jax.ShapeDtypeStruct((S, D_MODEL), F32), jax.ShapeDtypeStruct((1, D_MODEL), F32)
    return pl.pallas_call(
        body, name=name, grid=(S // tm, HID_PIECES),
        in_specs=[d_spec, d_spec,
                  pl.BlockSpec((None, D_MODEL, FFN_PIECE), lambda i, k: (k, 0, 0)),
                  pl.BlockSpec((None, D_MODEL, FFN_PIECE), lambda i, k: (k + HID_PIECES, 0, 0)),
                  row, row, par, par] + [row, par][:n_below] + _after_specs(after),
        out_specs=[row, par, par] + [row, par][:n_below],
        out_shape=[row_shape, par_shape, par_shape] + [jax.ShapeDtypeStruct((S, D_MODEL), BF16), par_shape][:n_below],
        scratch_shapes=[pltpu.VMEM((tm, D_MODEL), F32)],
        compiler_params=_params(("arbitrary", "arbitrary")),
    )(d_gate, d_up, w8, w8, x, d_out, scale, shift, *(below[:2] if below is not None else ()), *_after_args(after))


def _swiglu_bwd(d_hid, hid_by_gate, hid_by_up):
    return d_hid * hid_by_gate, d_hid * hid_by_up


def _adamw_math(w_, g_, m_, v_):
    m_ = ADAM_B1 * m_ + (1.0 - ADAM_B1) * g_
    v_ = ADAM_B2 * v_ + (1.0 - ADAM_B2) * (g_ * g_)
    m_hat = m_ / (1.0 - ADAM_B1 ** ADAM_STEP)
    v_hat = v_ / (1.0 - ADAM_B2 ** ADAM_STEP)
    return -ADAM_LR * (m_hat / (jnp.sqrt(v_hat) + ADAM_EPS) + ADAM_WD * w_), m_, v_


def _adamw(w, g, m, v, name):
    R, C = w.shape
    tr = _pick(R, (256, 176, 128, 64, 32, 16, 8))

    def body(w_ref, g_ref, m_ref, v_ref, d_ref, nm_ref, nv_ref):
        d_ref[...], nm_ref[...], nv_ref[...] = _adamw_math(w_ref[...], g_ref[...], m_ref[...], v_ref[...])

    spec = pl.BlockSpec((tr, C), lambda i: (i, 0))
    return pl.pallas_call(
        body, name=name, grid=(R // tr,),
        in_specs=[spec] * 4, out_specs=[spec] * 3,
        out_shape=[jax.ShapeDtypeStruct((R, C), F32)] * 3,
        compiler_params=_params(("parallel",)),
    )(w, g, m, v)


def _sum_adamw(parts, w, m, v, name, transposed=False):
    _, R, C = parts.shape
    tr = _pick(R, (256, 176, 128, 64, 32, 16, 8))

    def body(p_ref, w_ref, m_ref, v_ref, g_ref, d_ref, nm_ref, nv_ref):
        g_ = p_ref[0].astype(F32)
        for d in range(1, N_DEV):
            g_ = g_ + p_ref[d].astype(F32)
        if transposed:
            g_ = jnp.transpose(g_)
        g_ref[...] = g_
        d_ref[...], nm_ref[...], nv_ref[...] = _adamw_math(w_ref[...], g_, m_ref[...], v_ref[...])

    spec = pl.BlockSpec((C, tr), lambda i: (0, i)) if transposed else pl.BlockSpec((tr, C), lambda i: (i, 0))
    return pl.pallas_call(
        body, name=name, grid=(R // tr,),
        in_specs=[pl.BlockSpec((N_DEV, tr, C), lambda i: (0, i, 0)), spec, spec, spec], out_specs=[spec] * 4,
        out_shape=[jax.ShapeDtypeStruct(w.shape, F32)] * 4,
        compiler_params=_params(("parallel",)),
    )(parts, w, m, v)


def _sum_devices(parts, name):
    _, R, C = parts.shape
    tr = _pick(R, (512, 256, 176, 128, 64, 32, 16, 8))

    def body(p_ref, o_ref):
        acc = p_ref[0].astype(F32)
        for d in range(1, N_DEV):
            acc = acc + p_ref[d].astype(F32)
        o_ref[...] = acc

    return pl.pallas_call(
        body, name=name, grid=(R // tr,),
        in_specs=[pl.BlockSpec((N_DEV, tr, C), lambda i: (0, i, 0))],
        out_specs=pl.BlockSpec((tr, C), lambda i: (i, 0)),
        out_shape=jax.ShapeDtypeStruct((R, C), F32),
        compiler_params=_params(("parallel",)),
    )(parts)


def _my_place():
    return lax.axis_index("x"), lax.axis_index("y"), lax.axis_index("c")


def _all_gather(blocks, name):
    n = len(blocks)

    def body(*refs):
        x_refs, out_refs = refs[:n], refs[n:2 * n]
        send_sems, recv_sems, local_sems = refs[2 * n:]
        x, y, c = _my_place()
        me, sibling = (x, y, c), (x, y, 1 - c)
        chips = [(1 - x, y), (x, 1 - y), (1 - x, 1 - y)]

        def copy(a, k, blk, to, own=False):
            slot = out_refs[a].at[4 * blk[0] + 2 * blk[1] + blk[2]]
            return pltpu.make_async_remote_copy(
                src_ref=x_refs[a] if own else slot, dst_ref=slot,
                send_sem=send_sems.at[7 * a + k], recv_sem=recv_sems.at[7 * a + k], device_id=to, device_id_type=MESH)

        mine = [pltpu.make_async_copy(x_refs[a], out_refs[a].at[4 * x + 2 * y + c], local_sems.at[a]) for a in range(n)]
        for cp in mine:
            cp.start()
        first = []
        for j, chip in enumerate(chips):
            first += [copy(a, 1 + j, me, (*chip, c), own=True) for a in range(n)]
        first += [copy(a, 0, me, sibling, own=True) for a in range(n)]
        for cp in first:
            cp.start()
        passed = []
        for j, chip in enumerate(chips):
            for a in range(n):
                copy(a, 1 + j, (*chip, c), me).wait_recv()
                passed.append(copy(a, 4 + j, (*chip, c), sibling))
                passed[-1].start()
        for a in range(n):
            copy(a, 0, sibling, me).wait_recv()
        for j, chip in enumerate(chips):
            for a in range(n):
                copy(a, 4 + j, (*chip, 1 - c), me).wait_recv()
        for cp in first + passed:
            cp.wait_send()
        for cp in mine:
            cp.wait()

    return pl.pallas_call(
        body, name=name,
        out_shape=[jax.ShapeDtypeStruct((N_DEV,) + b.shape, b.dtype) for b in blocks],
        in_specs=[pl.BlockSpec(memory_space=pl.ANY)] * n,
        out_specs=[pl.BlockSpec(memory_space=pl.ANY)] * n,
        scratch_shapes=[pltpu.SemaphoreType.DMA((7 * n,)), pltpu.SemaphoreType.DMA((7 * n,)), pltpu.SemaphoreType.DMA((n,))],
    )(*blocks)


def _all_to_all(pieces, name):
    n = len(pieces)

    def body(*refs):
        x_refs, out_refs = refs[:n], refs[n:2 * n]
        send_sems, recv_sems, local_sems = refs[2 * n:]
        x, y, c = _my_place()
        me = 4 * x + 2 * y + c
        mine = [pltpu.make_async_copy(x_refs[a].at[me], out_refs[a].at[me], local_sems.at[a]) for a in range(n)]
        for cp in mine:
            cp.start()
        copies = []
        for k in (2, 4, 6, 3, 5, 7, 1):
            px = 1 - x if k & 4 else x
            py = 1 - y if k & 2 else y
            pc = 1 - c if k & 1 else c
            peer = 4 * px + 2 * py + pc
            for a in range(n):
                copies.append(pltpu.make_async_remote_copy(
                    src_ref=x_refs[a].at[peer], dst_ref=out_refs[a].at[me],
                    send_sem=send_sems.at[7 * a + k - 1], recv_sem=recv_sems.at[7 * a + k - 1],
                    device_id=(px, py, pc), device_id_type=MESH))
        for cp in copies:
            cp.start()
        for cp in copies:
            cp.wait_recv()
        for cp in copies:
            cp.wait_send()
        for cp in mine:
            cp.wait()

    return pl.pallas_call(
        body, name=name,
        out_shape=[jax.ShapeDtypeStruct(p.shape, p.dtype) for p in pieces],
        in_specs=[pl.BlockSpec(memory_space=pl.ANY)] * n,
        out_specs=[pl.BlockSpec(memory_space=pl.ANY)] * n,
        scratch_shapes=[pltpu.SemaphoreType.DMA((7 * n,)), pltpu.SemaphoreType.DMA((7 * n,)), pltpu.SemaphoreType.DMA((n,))],
    )(*pieces)


def _peers():
    x, y, c = _my_place()
    out = []
    for k in (2, 4, 6, 3, 5, 7, 1):
        px = 1 - x if k & 4 else x
        py = 1 - y if k & 2 else y
        pc = 1 - c if k & 1 else c
        out.append((k, (px, py, pc), 4 * px + 2 * py + pc))
    return out


def _exchange_copies(x_refs, land_refs, send_sems, recv_sems, scatter):
    x, y, c = _my_place()
    me = 4 * x + 2 * y + c
    starts, arrivals = [], []
    for k, place, peer in _peers():
        for a, (x_ref, land_ref) in enumerate(zip(x_refs, land_refs)):
            sems = dict(send_sem=send_sems.at[7 * a + k - 1], recv_sem=recv_sems.at[7 * a + k - 1],
                        device_id=place, device_id_type=MESH)
            src = x_ref.at[peer] if scatter else x_ref
            starts.append(pltpu.make_async_remote_copy(src_ref=src, dst_ref=land_ref.at[me], **sems))
            arrivals.append(pltpu.make_async_remote_copy(src_ref=src, dst_ref=land_ref.at[peer], **sems))
    return starts, arrivals


def _exchange_start(arrays, scatter, name):
    n = len(arrays)
    hbm = pl.BlockSpec(memory_space=pltpu.HBM)
    sem = pl.BlockSpec(memory_space=pltpu.SEMAPHORE)
    lands = [lax.empty(a.shape if scatter else (N_DEV,) + a.shape, a.dtype) for a in arrays]

    def body(*refs):
        x_refs, land_refs = refs[:n], refs[n:2 * n]
        send_sems, recv_sems = refs[2 * n], refs[2 * n + 1]
        token = refs[-1]
        starts, _ = _exchange_copies(x_refs, land_refs, send_sems, recv_sems, scatter)
        for cp in starts:
            cp.start()
        token[...] = jnp.zeros_like(token)

    res = pl.pallas_call(
        body, name=name,
        out_shape=(pltpu.SemaphoreType.DMA((7 * n,)), pltpu.SemaphoreType.DMA((7 * n,)),
                   *[pltpu.HBM(a.shape, a.dtype) for a in arrays], *[pltpu.HBM(l.shape, l.dtype) for l in lands],
                   jax.ShapeDtypeStruct((8, 128), F32)),
        in_specs=[hbm] * (2 * n),
        out_specs=(sem, sem, *[hbm] * (2 * n), pl.BlockSpec(memory_space=pltpu.VMEM)),
        input_output_aliases={i: 2 + i for i in range(2 * n)},
        compiler_params=pltpu.CompilerParams(has_side_effects=pltpu.SideEffectType.DATAFLOW_SIDE_EFFECTING),
    )(*[pltpu.with_memory_space_constraint(a, pltpu.HBM) for a in arrays],
      *[pltpu.with_memory_space_constraint(l, pltpu.HBM) for l in lands])
    return res[0], res[1], list(res[2:2 + n]), list(res[2 + n:2 + 2 * n]), res[-1]


def _exchange_wait(handles, scatter, after, name):
    send_sems, recv_sems, arrays, lands, _ = handles
    n = len(arrays)
    hbm = pl.BlockSpec(memory_space=pltpu.HBM)
    sem = pl.BlockSpec(memory_space=pltpu.SEMAPHORE)

    def body(*refs):
        x_refs, land_refs = refs[:n], refs[n:2 * n]
        send_s, recv_s = refs[2 * n], refs[2 * n + 1]
        starts, arrivals = _exchange_copies(x_refs, land_refs, send_s, recv_s, scatter)
        for cp in arrivals:
            cp.wait_recv()
        for cp in starts:
            cp.wait_send()

    res = pl.pallas_call(
        body, name=name,
        out_shape=(*[pltpu.HBM(a.shape, a.dtype) for a in arrays], *[pltpu.HBM(l.shape, l.dtype) for l in lands]),
        in_specs=[hbm] * (2 * n) + [sem, sem, pl.BlockSpec(memory_space=pl.ANY)],
        out_specs=tuple([hbm] * (2 * n)),
        input_output_aliases={i: i for i in range(2 * n)},
        compiler_params=pltpu.CompilerParams(has_side_effects=pltpu.SideEffectType.DATAFLOW_SIDE_EFFECTING),
    )(*arrays, *lands, send_sems, recv_sems, after)
    me = 4 * lax.axis_index("x") + 2 * lax.axis_index("y") + lax.axis_index("c")
    out = []
    for src, got in zip(res[:n], res[n:]):
        zeros = (0,) * (got.ndim - 1)
        own = lax.dynamic_slice(src, (me,) + zeros, (1,) + src.shape[1:]) if scatter else src[None]
        out.append(lax.dynamic_update_slice(got, own, (me,) + zeros))
    return out


def _pad_lanes(v, at=0, width=128):
    return jnp.pad(v, ((0, 0), (at, width - at - v.shape[1])))


def _pack_weights(P):
    W = {}
    w = P["w_in"]
    W["wp"] = jnp.concatenate([w[:, :2048], w[:, 2440:2696], w[:, 2056:2440], w[:, 2696:2760], w[:, 2048:2056],
                               jnp.zeros((D_MODEL, N_IN_PACKED - N_IN), w.dtype)], axis=1).astype(BF16)
    W["conv_w"] = P["gdn_conv_w"].astype(F32)
    W["alog_p"] = _pad_lanes(P["gdn_a_log"], 64)
    W["dt_p"] = _pad_lanes(P["gdn_dt_bias"], 64)
    W["gnw"] = P["gdn_norm_w"]
    W["qnw"] = P["mla_q_norm_w"]
    W["kvnw"] = P["mla_kv_norm_w"]
    uq = P["mla_w_uq"].reshape(Q_LORA, HEADS, HEAD_DIM + ROPE)
    W["wuq"] = jnp.pad(uq, ((0, 0), (0, 0), (0, 256 - HEAD_DIM - ROPE))).reshape(Q_LORA, HEADS * 256).astype(BF16)
    ukv = P["mla_w_ukv"].reshape(KV_LORA, HEADS, 2, HEAD_DIM)
    W["wukv"] = ukv.transpose(0, 2, 1, 3).reshape(KV_LORA, 2 * HEADS * HEAD_DIM).astype(BF16)
    W["qn_w"] = P["qkn_q_nope"]
    W["qr_w"] = _pad_lanes(P["qkn_q_rope"])
    W["kn_w"] = P["qkn_k_nope"]
    W["kr_w"] = _pad_lanes(P["qkn_k_rope"])
    W["onw"] = P["mla_out_norm_w"]
    W["wout"] = P["w_out"].astype(BF16)
    return W


def _unpack_grads(G):
    g_qkv, g_z, g_ckv, g_cq, g_kab = G["wp"]
    uq = G["wuq"].reshape(Q_LORA, HEADS, 256)[:, :, :HEAD_DIM + ROPE].reshape(Q_LORA, HEADS * (HEAD_DIM + ROPE))
    ukv = G["wukv"].reshape(KV_LORA, 2, HEADS, HEAD_DIM).transpose(0, 2, 1, 3).reshape(KV_LORA, 2 * HEADS * HEAD_DIM)
    return {
        "w_in": jnp.concatenate([g_qkv, g_z, g_kab[:, ROPE:ROPE + 8], g_cq, g_ckv, g_kab[:, :ROPE]], axis=1),
        "gdn_conv_w": G["conv_w"], "gdn_a_log": G["alog_p"][:, 64:68], "gdn_dt_bias": G["dt_p"][:, 64:68],
        "gdn_norm_w": G["gnw"], "mla_q_norm_w": G["qnw"], "mla_w_uq": uq, "mla_kv_norm_w": G["kvnw"], "mla_w_ukv": ukv,
        "qkn_q_nope": G["qn_w"], "qkn_q_rope": G["qr_w"][:, :ROPE], "qkn_k_nope": G["kn_w"], "qkn_k_rope": G["kr_w"][:, :ROPE],
        "mla_out_norm_w": G["onw"], "w_out": G["wout"],
    }


def _rope_tables(positions):
    half = ROPE // 2
    inv_freq = ROPE_BASE ** (-jnp.arange(half, dtype=F32) / half)
    ang = positions.astype(F32)[:, None] * inv_freq
    cos, sin = jnp.cos(ang), jnp.sin(ang)
    zeros = jnp.zeros((positions.shape[0], 128 - ROPE), F32)
    return jnp.concatenate([cos, cos, zeros], axis=1), jnp.concatenate([-sin, sin, zeros], axis=1)


def _ffn_forward(x, scale, shift, gate_w, w8, wo4, name, target=None):
    S = x.shape[0]
    tm = _pick(S, (512, 256, 128))
    n = S // tm
    with_loss = target is not None

    def body(x_ref, sc_ref, sh_ref, g_ref, wg_ref, wu_ref, wo_ref, *rest):
        t_ref = rest[0] if with_loss else None
        h_ref, bg_ref, bu_ref, ht_ref = rest[with_loss:with_loss + 4]
        tail = rest[with_loss + 4:]
        h_scr, acc_ref = tail[-2:]
        i, p = pl.program_id(0), pl.program_id(1)

        @pl.when(p == 0)
        def _():
            h_new = _modulate(x_ref[...], sc_ref[...], sh_ref[...]).astype(BF16)
            h_scr[...] = h_new
            h_ref[...] = h_new
            acc_ref[...] = jnp.zeros_like(acc_ref)

        h = h_scr[...]
        gate = _dot_raw(h, wg_ref[...], "nn")
        up = _dot_raw(h, wu_ref[...], "nn")
        sg = _sigmoid(gate)
        act = gate * sg
        hid = act * up
        bg_ref[...] = (up * (sg * (1.0 + gate * (1.0 - sg)))).astype(BF16)
        bu_ref[...] = act.astype(BF16)
        ht_ref[...] = jnp.transpose(hid).astype(BF16)
        acc_ref[...] += _dot_raw(hid, wo_ref[...], "nn")

        if with_loss:
            dx_ref, df_ref, dg_ref, l_ref = tail[:4]

            @pl.when((p == 0) & (i == 0))
            def _():
                dg_ref[...] = jnp.zeros_like(dg_ref)
                l_ref[...] = jnp.zeros_like(l_ref)

            @pl.when(p == HID_PIECES - 1)
            def _():
                step = min(EPILOGUE_ROWS, tm)
                for r in range(tm // step):
                    rows = slice(step * r, step * (r + 1))
                    f = acc_ref[rows, :]
                    diff = x_ref[rows, :] + 0.5 * g_ref[...] * f - t_ref[rows, :]
                    dx = diff * (1.0 / D_MODEL)
                    dx_ref[rows, :] = dx
                    df_ref[rows, :] = (0.5 * g_ref[...] * dx).astype(df_ref.dtype)
                    dg_ref[...] += jnp.sum(0.5 * f * dx, axis=0, keepdims=True)
                    l_ref[...] += jnp.sum(diff * diff, axis=0, keepdims=True)

            @pl.when((p == HID_PIECES - 1) & (i == n - 1))
            def _():
                l_ref[...] = jnp.full(l_ref.shape, (0.5 / D_MODEL) * jnp.sum(l_ref[...]), F32)
        else:
            f_ref, xo_ref = tail[:2]

            @pl.when(p == HID_PIECES - 1)
            def _():
                f = acc_ref[...]
                f_ref[...] = f.astype(f_ref.dtype)
                xo_ref[...] = x_ref[...] + 0.5 * g_ref[...] * f

    row = pl.BlockSpec((tm, D_MODEL), lambda i, p: (i, 0))
    par = pl.BlockSpec((1, D_MODEL), lambda i, p: (0, 0))
    piece = pl.BlockSpec((None, tm, FFN_PIECE), lambda i, p: (p, i, 0))
    row_f32, row_bf16 = jax.ShapeDtypeStruct((S, D_MODEL), F32), jax.ShapeDtypeStruct((S, D_MODEL), BF16)
    par_f32 = jax.ShapeDtypeStruct((1, D_MODEL), F32)
    piece_shape = jax.ShapeDtypeStruct((HID_PIECES, S, FFN_PIECE), BF16)
    return pl.pallas_call(
        body, name=name, grid=(n, HID_PIECES),
        in_specs=[row, par, par, par,
                  pl.BlockSpec((None, D_MODEL, FFN_PIECE), lambda i, p: (p, 0, 0)),
                  pl.BlockSpec((None, D_MODEL, FFN_PIECE), lambda i, p: (p + HID_PIECES, 0, 0)),
                  pl.BlockSpec((None, FFN_PIECE, D_MODEL), lambda i, p: (p, 0, 0))] + [row] * with_loss,
        out_specs=[row, piece, piece, pl.BlockSpec((None, FFN_PIECE, tm), lambda i, p: (p, 0, i))]
        + ([row, row, par, par] if with_loss else [row, row]),
        out_shape=[row_bf16, piece_shape, piece_shape, jax.ShapeDtypeStruct((HID_PIECES, FFN_PIECE, S), BF16)]
        + ([row_f32, row_bf16, par_f32, par_f32] if with_loss else [row_bf16, row_f32]),
        scratch_shapes=[pltpu.VMEM((tm, D_MODEL), BF16), pltpu.VMEM((tm, D_MODEL), F32)],
        compiler_params=_params(("arbitrary", "arbitrary")),
    )(x, scale, shift, gate_w, w8, w8, wo4, *([target] if with_loss else []))


def _ffn_fwd(x, scale, shift, gate_w, w8, wo4, tag, target=None):
    res = _ffn_forward(x, scale, shift, gate_w, w8, wo4, tag + "_fwd", target)
    h, by_gate, by_up, hid_t = res[:4]
    if target is not None:
        dx_out, df, d_gate_w, loss_row = res[4:]
        return (dx_out, loss_row), (h, by_gate, by_up, hid_t, None, df, d_gate_w)
    f, x_out = res[4:]
    return x_out, (h, by_gate, by_up, hid_t, f, None, None)


def _ffn_bwd(d_out, x, scale, shift, gate_w, w8, wo4, saved, tag, grad_ready, below=None):
    h, gate, up, hid_t, f, df, d_gate_w = saved
    S = x.shape[0]
    tm = _pick(S, (512, 256, 128))
    tk = _pick(S, (512, 256, 128))
    n = S // tm
    if df is None:
        (df,), (d_gate_w,) = _rowwise_bwd(lambda f_, g_: (0.5 * g_ * f_,), [(f, tm, D_MODEL, 0)], [], [gate_w],
                                          [(d_out, tm, D_MODEL, 0)], n, tag + "_dres", row_dtypes=(BF16,))
    tb = _pick(S, MATMUL_ROWS)
    piece = pl.BlockSpec((None, tb, FFN_PIECE), lambda i, j, k: (j, i, 0))
    d_gate, d_up = _mmg(df, wo4, "nt", name=tag + "_ddown", grid=(S // tb, HID_PIECES, 1),
                        a_spec=pl.BlockSpec((tb, D_MODEL), lambda i, j, k: (i, 0)),
                        b_spec=pl.BlockSpec((None, FFN_PIECE, D_MODEL), lambda i, j, k: (j, 0, 0)),
                        out_spec=piece, out_shapes=[jax.ShapeDtypeStruct((HID_PIECES, S, FFN_PIECE), BF16)] * 2,
                        acc_shape=(tb, FFN_PIECE), extras=[gate, up], extra_specs=[piece, piece], epi=_swiglu_bwd)
    tk = _pick(S, MATMUL_ROWS)
    g_wo4 = _mmg(hid_t, df, "nn", name=tag + "_gwo", grid=(HID_PIECES, 1, S // tk),
                 a_spec=pl.BlockSpec((None, FFN_PIECE, tk), lambda i, j, k: (i, 0, k)),
                 b_spec=pl.BlockSpec((tk, D_MODEL), lambda i, j, k: (k, j)),
                 out_spec=pl.BlockSpec((None, FFN_PIECE, D_MODEL), lambda i, j, k: (i, 0, j)),
                 out_shapes=[jax.ShapeDtypeStruct((HID_PIECES, FFN_PIECE, D_MODEL), BF16)], acc_shape=(FFN_PIECE, D_MODEL))
    g_w8 = _ffn_gw8(h, d_gate, d_up, tag + "_gw8", after=grad_ready("wo4", g_wo4))
    res = _ffn_dh(d_gate, d_up, w8, x, d_out, scale, shift, tag + "_dh", after=grad_ready("w8", g_w8), below=below)
    return (res[0], res[1], res[2], d_gate_w) + tuple(res[3:])


def _dproj_dmod(d_pieces, wp, x, d_out, scale, shift, name, below=None):
    S = x.shape[0]
    tm = _pick(S, TOKEN_ROWS)
    widths = [p.shape[1] for p in d_pieces]
    starts = [sum(widths[:n]) for n in range(len(widths))]
    n_p = len(d_pieces)
    n_below = 0 if below is None else 2

    def body(*refs):
        dp_refs, (w_ref, x_ref, do_ref, sc_ref, sh_ref), rest = refs[:n_p], refs[n_p:n_p + 5], refs[n_p + 5:]
        below_in = rest[:n_below]
        dx_ref, dsc_ref, dsh_ref = rest[n_below:n_below + 3]
        below_out, acc_ref = rest[n_below + 3:n_below + 3 + n_below], rest[-1]

        @pl.when(pl.program_id(0) == 0)
        def _():
            for r in (dsc_ref, dsh_ref) + tuple(below_out[1:]):
                r[...] = jnp.zeros_like(r)

        acc = None
        for dp_ref, at, width in zip(dp_refs, starts, widths):
            part = _dot_raw(dp_ref[...], w_ref[:, at:at + width], "nt")
            acc = part if acc is None else acc + part
        acc_ref[...] = acc
        _dmod_epilogue(acc_ref, x_ref, do_ref, sc_ref, sh_ref, dx_ref, dsc_ref, dsh_ref, below, below_in, below_out)

    row = pl.BlockSpec((tm, D_MODEL), lambda i: (i, 0))
    par = pl.BlockSpec((1, D_MODEL), lambda i: (0, 0))
    row_shape, par_shape = jax.ShapeDtypeStruct((S, D_MODEL), F32), jax.ShapeDtypeStruct((1, D_MODEL), F32)
    return pl.pallas_call(
        body, name=name, grid=(S // tm,),
        in_specs=[pl.BlockSpec((tm, width), lambda i: (i, 0)) for width in widths]
        + [pl.BlockSpec(wp.shape, lambda i: (0, 0)), row, row, par, par] + [row, par][:n_below],
        out_specs=[row, par, par] + [row, par][:n_below],
        out_shape=[row_shape, par_shape, par_shape] + [jax.ShapeDtypeStruct((S, D_MODEL), BF16), par_shape][:n_below],
        scratch_shapes=[pltpu.VMEM((tm, D_MODEL), F32)],
        compiler_params=_params(("arbitrary",)),
    )(*d_pieces, wp, x, d_out, scale, shift, *(below[:2] if below is not None else ()))


def _gw_pieces(h, d_pieces, name):
    S = h.shape[0]
    tm = 512
    tk = _pick(S, MATMUL_ROWS)
    n_p = len(d_pieces)
    widths = [p.shape[1] for p in d_pieces]

    def body(h_ref, *rest):
        d_refs, o_refs = rest[:n_p], rest[n_p:]
        k = pl.program_id(1)
        h_t = jnp.transpose(h_ref[...])
        for d_ref, o_ref in zip(d_refs, o_refs):
            part = _dot_raw(h_t, d_ref[...], "nn")

            @pl.when(k == 0)
            def _():
                o_ref[...] = part

            @pl.when(k > 0)
            def _():
                o_ref[...] += part

    return pl.pallas_call(
        body, name=name, grid=(D_MODEL // tm, S // tk),
        in_specs=[pl.BlockSpec((tk, tm), lambda i, k: (k, i))] + [pl.BlockSpec((tk, width), lambda i, k: (k, 0)) for width in widths],
        out_specs=[pl.BlockSpec((tm, width), lambda i, k: (i, 0)) for width in widths],
        out_shape=[jax.ShapeDtypeStruct((D_MODEL, width), F32) for width in widths],
        compiler_params=_params(("parallel", "arbitrary")),
    )(h, *d_pieces)


def _mod_proj(x, scale, shift, wp, name):
    S = x.shape[0]
    N = wp.shape[1]
    tm = _pick(S, MATMUL_ROWS)
    tn = _pick(N, (1408, 1024, 512, 256, 128))

    def body(x_ref, sc_ref, sh_ref, w_ref, h_ref, o_ref, h_scr):
        @pl.when(pl.program_id(1) == 0)
        def _():
            h_new = _modulate(x_ref[...], sc_ref[...], sh_ref[...]).astype(BF16)
            h_scr[...] = h_new
            h_ref[...] = h_new

        o_ref[...] = _dot_raw(h_scr[...], w_ref[...], "nn")

    row = pl.BlockSpec((tm, D_MODEL), lambda i, j: (i, 0))
    par = pl.BlockSpec((1, D_MODEL), lambda i, j: (0, 0))
    return pl.pallas_call(
        body, name=name, grid=(S // tm, N // tn),
        in_specs=[row, par, par, pl.BlockSpec((D_MODEL, tn), lambda i, j: (0, j))],
        out_specs=[row, pl.BlockSpec((tm, tn), lambda i, j: (i, j))],
        out_shape=[jax.ShapeDtypeStruct((S, D_MODEL), BF16), jax.ShapeDtypeStruct((S, N), F32)],
        scratch_shapes=[pltpu.VMEM((tm, D_MODEL), BF16)],
        compiler_params=_params(("parallel", "arbitrary")),
    )(x, scale, shift, wp)


def _mixer_fwd(x1, scale, shift, gate_w, cos_p, sin_p, W):
    S = x1.shape[0]
    tm = _pick(S, (512, 256, 128))
    tv = _pick(S, TOKEN_ROWS)
    ta = _pick(S, (512, 256, 128))
    nc = S // CHUNK
    h2, proj = _mod_proj(x1, scale, shift, W["wp"], "mix_proj")
    qkvc = _conv_fwd(proj, W["conv_w"], tv, "gdn_conv")
    kab = (proj, tv, 128, 21)
    q_a, k_a, v_a, gb = _rowwise(_gdn_pre_fn, [(qkvc, tv, 1536, 0), kab], [W["alog_p"], W["dt_p"]],
                                 [(tv, 512, F32)] * 3 + [(tv, 128, F32)], S // tv, "gdn_pre")
    ti = _pick(S, INTRA_ROWS)
    intra = _rowwise(_gdn_intra_fn, [(q_a, ti, 512, 0), (k_a, ti, 512, 0), (v_a, ti, 512, 0), (gb, ti, 128, 0)],
                     [], [(ti, 512, F32)] * 4 + [(ti, CHUNK, F32)] * 4 + [(ti // 8, 512, F32)] + [(ti, CHUNK, F32)] * 4,
                     S // ti, "gdn_intra")
    u, wk, qd, kd, qks, gl, invs = intra[0], intra[1], intra[2], intra[3], tuple(intra[4:8]), intra[8], tuple(intra[9:])
    o_a, s_prev = _gdn_scan_fwd(u, wk, qd, kd, qks, gl, "gdn_scan")
    mla_params = [W["qnw"], W["kvnw"], W["wuq"], W["wukv"], W["qn_w"], W["qr_w"], W["kn_w"], W["kr_w"]]
    def mla_pre_with_vt(*a):
        q_, k_, v_ = _mla_pre_fn(*a)
        return q_, k_, v_, jnp.transpose(v_)

    q_b, k_b, v_b, vt_b = _rowwise(mla_pre_with_vt,
                                   [(proj, tv, 256, 8), (proj, tv, 384, 6), kab, (cos_p, tv, 128, 0), (sin_p, tv, 128, 0)],
                                   mla_params, [(tv, 1024, BF16), (tv, 1024, BF16), (tv, 512, BF16), (512, tv, BF16, "across")],
                                   S // tv, "mla_pre")
    o_b, lse = _attn_fwd(q_b, k_b, vt_b, ta, "mla_attn")
    (mixed,) = _rowwise(_mix_post_fn, [(o_a, tv, 512, 0), (proj, tv, 512, 3), (o_b, tv, 512, 0)], [W["gnw"], W["onw"]],
                        [(tv, D_MODEL, BF16)], S // tv, "mix_post")
    y, x2 = _mm(mixed, W["wout"], "nn", name="mix_out", out_dtypes=(BF16, F32), extras=[x1], extra_params=[gate_w],
                epi=lambda acc, x_, g_: (acc, x_ + g_ * acc))
    saved = (h2, proj, qkvc, q_a, k_a, v_a, gb, u, wk, qd, kd, qks, gl, invs, s_prev, o_a, q_b, k_b, v_b, o_b, lse, mixed, y)
    return x2, saved


def _mixer_bwd(d_out, dy, x1, scale, shift, cos_p, sin_p, W, saved, below):
    (h2, proj, qkvc, q_a, k_a, v_a, gb, u, wk, qd, kd, qks, gl, invs, s_prev, o_a, q_b, k_b, v_b, o_b, lse, mixed, y) = saved
    S = x1.shape[0]
    tm = _pick(S, (512, 256, 128))
    tv = _pick(S, TOKEN_ROWS)
    ta = _pick(S, (512, 256, 128))
    nc = S // CHUNK
    G = {}
    d_mixed = _mm(dy, W["wout"], "nt", name="mix_dout")
    G["wout"] = _mm(mixed, dy, "tn", name="mix_gwout")
    (do_a, dz, do_b), (G["gnw"], G["onw"]) = _rowwise_bwd(
        _mix_post_fn, [(o_a, tv, 512, 0), (proj, tv, 512, 3), (o_b, tv, 512, 0)], [], [W["gnw"], W["onw"]],
        [(d_mixed, tv, D_MODEL, 0)], S // tv, "mix_dpost", row_dtypes=(F32, BF16, F32))
    stats = _attn_stats(o_b, lse, do_b, ta, "mla_stats")
    dq_b, dk_b, dv_b = _attn_bwd(q_b, k_b, v_b, do_b, stats, ta, "mla_dattn")
    kab = (proj, tv, 128, 21)
    mla_params = [W["qnw"], W["kvnw"], W["wuq"], W["wukv"], W["qn_w"], W["qr_w"], W["kn_w"], W["kr_w"]]
    (d_ckv, d_cq, d_kab), mla_grads = _rowwise_bwd(
        _mla_pre_fn, [(proj, tv, 256, 8), (proj, tv, 384, 6), kab], [(cos_p, tv, 128, 0), (sin_p, tv, 128, 0)], mla_params,
        [(dq_b, tv, 1024, 0), (dk_b, tv, 1024, 0), (dv_b, tv, 512, 0)], S // tv, "mla_dpre", row_dtypes=(BF16, BF16, F32))
    for key, g in zip(("qnw", "kvnw", "wuq", "wukv", "qn_w", "qr_w", "kn_w", "kr_w"), mla_grads):
        G[key] = g
    scan_grads = _gdn_scan_bwd(u, wk, qd, kd, qks, gl, s_prev, do_a, "gdn_dscan")
    ti = _pick(S, INTRA_ROWS)
    intra_douts = [(scan_grads[i], ti, 512, 0) for i in range(4)] + [(scan_grads[4 + i], ti, CHUNK, 0) for i in range(4)]
    intra_douts.append((scan_grads[8], ti // 8, 512, 0))
    (dq_a, dk_a, dv_a, d_gb), _ = _rowwise_bwd(
        _gdn_intra_fn, [(q_a, ti, 512, 0), (k_a, ti, 512, 0), (v_a, ti, 512, 0), (gb, ti, 128, 0)],
        [(x_, ti, CHUNK, 0) for x_ in invs], [], intra_douts, S // ti, "gdn_dintra")
    (d_qkvc, d_kab), (G["alog_p"], G["dt_p"]) = _rowwise_bwd(
        _gdn_pre_fn, [(qkvc, tv, 1536, 0), kab], [], [W["alog_p"], W["dt_p"]],
        [(dq_a, tv, 512, 0), (dk_a, tv, 512, 0), (dv_a, tv, 512, 0), (d_gb, tv, 128, 0)], S // tv, "gdn_dpre",
        adds=[(1, d_kab)], row_dtypes=(F32, BF16))
    d_qkv, g_conv = _conv_bwd(proj, d_qkvc, W["conv_w"], tv, "gdn_dconv")
    G["conv_w"] = g_conv[:4]
    d_proj = [d_qkv, dz, d_ckv, d_cq, d_kab]
    G["wp"] = _gw_pieces(h2, d_proj, "mix_gwp")
    dx1, G["s2"], G["sh2"], d_below, dg_below = _dproj_dmod(d_proj, W["wp"], x1, d_out, scale, shift, "mix_dproj", below=below)
    return dx1, d_below, dg_below, G


def _local_step(x, target, mod, cos_p, sin_p, W1, mixer_weights, ffn2_weights, ffn_grad_ready, mixer_grads_ready):
    sh1, s1, g1, sh2, s2, g2, sh3, s3, g3 = [mod[:, D_MODEL * i:D_MODEL * (i + 1)] for i in range(N_MOD)]
    x1, saved1 = _ffn_fwd(x, s1, sh1, g1, W1["f1_w8"], W1["f1_wo4"], "ffn1")
    W = mixer_weights(x1)
    x2, saved2 = _mixer_fwd(x1, s2, sh2, g2, cos_p, sin_p, W)
    W.update(ffn2_weights(x2))
    (dx3, loss_row), saved3 = _ffn_fwd(x2, s3, sh3, g3, W["f2_w8"], W["f2_wo4"], "ffn2", target=target)
    dx2, d_s3, d_sh3, d_g3, dy, d_g2 = _ffn_bwd(dx3, x2, s3, sh3, g3, W["f2_w8"], W["f2_wo4"], saved3, "ffn2",
                                                ffn_grad_ready("f2"), below=(saved2[-1], g2, 1.0))
    dx1, df1, d_g1, G = _mixer_bwd(dx2, dy, x1, s2, sh2, cos_p, sin_p, W, saved2, below=(saved1[4], g1, 0.5))
    d_sh2, d_s2 = G.pop("sh2"), G.pop("s2")
    saved1 = saved1[:5] + (df1, d_g1 + mixer_grads_ready(G))
    dx, d_s1, d_sh1, d_g1 = _ffn_bwd(dx1, x, s1, sh1, g1, W1["f1_w8"], W1["f1_wo4"], saved1, "ffn1", ffn_grad_ready("f1"))
    d_mod = jnp.concatenate([d_sh1, d_s1, d_g1, d_sh2, d_s2, d_g2, d_sh3, d_s3, d_g3], axis=1)
    return loss_row, dx, d_mod


WEIGHT_NAMES = ("w_ada", "b_ada", "ffn1_w_in", "ffn1_w_out", "w_in", "gdn_conv_w", "gdn_a_log", "gdn_dt_bias", "gdn_norm_w",
                "mla_q_norm_w", "mla_w_uq", "mla_kv_norm_w", "mla_w_ukv", "qkn_q_nope", "qkn_q_rope", "qkn_k_nope",
                "qkn_k_rope", "mla_out_norm_w", "w_out", "ffn2_w_in", "ffn2_w_out")
FFN_SHARDED = ("ffn1_w_in", "ffn1_w_out", "ffn2_w_in", "ffn2_w_out")
TRANSPOSED_ENTRY = ("ffn1_w_in", "ffn2_w_in", "w_in", "mla_w_uq")
SHEETED = (("w_in", "col"), ("gdn_conv_w", "col"), ("mla_w_uq", "col"), ("mla_w_ukv", "col"), ("w_out", "row"))
MOD_ROWS = N_MOD * D_MODEL // 128
SMALL = {"gdn_a_log": (MOD_ROWS, 1, 64, 4), "gdn_dt_bias": (MOD_ROWS + 1, 1, 64, 4), "gdn_norm_w": (MOD_ROWS + 2, 1, 0, 128),
         "mla_q_norm_w": (MOD_ROWS + 3, 3, 0, 384), "mla_kv_norm_w": (MOD_ROWS + 6, 2, 0, 256),
         "qkn_q_nope": (MOD_ROWS + 8, 1, 0, 128), "qkn_q_rope": (MOD_ROWS + 9, 1, 0, 64), "qkn_k_nope": (MOD_ROWS + 10, 1, 0, 128),
         "qkn_k_rope": (MOD_ROWS + 11, 1, 0, 64), "mla_out_norm_w": (MOD_ROWS + 12, 1, 0, 128)}
LOSS_ROW = MOD_ROWS + 13
CONV_ROW, CONV_ROWS = 88, 4 * 1536 // 128
SHEET_ROWS = CONV_ROW + CONV_ROWS


def _to_sheet(flat, dtype, sublanes):
    n = flat.shape[-1]
    unit = sublanes * 128
    pad = (-n) % unit
    flat = jnp.pad(flat.astype(dtype), [(0, 0)] * (flat.ndim - 1) + [(0, pad)])
    return flat.reshape(flat.shape[:-1] + ((n + pad) // 128, 128))


def _small_sheet(b_like, small):
    sheet = jnp.zeros((SHEET_ROWS, 128), F32).at[:MOD_ROWS].set(b_like.reshape(MOD_ROWS, 128))
    for name, (row, rows, lane, n) in SMALL.items():
        v = small[name].reshape(1, n)
        if rows == 1:
            sheet = sheet.at[row, lane:lane + n].set(v[0])
        else:
            sheet = sheet.at[row:row + rows].set(v.reshape(rows, 128))
    return sheet


def _from_small_sheet(sheet):
    out = {"b_ada": sheet[:MOD_ROWS].reshape(1, N_MOD * D_MODEL)}
    for name, (row, rows, lane, n) in SMALL.items():
        out[name] = sheet[row, lane:lane + n].reshape(1, n) if rows == 1 else sheet[row:row + rows].reshape(1, n)
    return out


def kernel(x, c, positions, w_ada, b_ada, ffn1_w_in, ffn1_w_out, w_in, gdn_conv_w, gdn_a_log, gdn_dt_bias, gdn_norm_w, mla_q_norm_w, mla_w_uq, mla_kv_norm_w, mla_w_ukv, qkn_q_nope, qkn_q_rope, qkn_k_nope, qkn_k_rope, mla_out_norm_w, w_out, ffn2_w_in, ffn2_w_out, loss_target, m_w_ada, m_b_ada, m_ffn1_w_in, m_ffn1_w_out, m_w_in, m_gdn_conv_w, m_gdn_a_log, m_gdn_dt_bias, m_gdn_norm_w, m_mla_q_norm_w, m_mla_w_uq, m_mla_kv_norm_w, m_mla_w_ukv, m_qkn_q_nope, m_qkn_q_rope, m_qkn_k_nope, m_qkn_k_rope, m_mla_out_norm_w, m_w_out, m_ffn2_w_in, m_ffn2_w_out, v_w_ada, v_b_ada, v_ffn1_w_in, v_ffn1_w_out, v_w_in, v_gdn_conv_w, v_gdn_a_log, v_gdn_dt_bias, v_gdn_norm_w, v_mla_q_norm_w, v_mla_w_uq, v_mla_kv_norm_w, v_mla_w_ukv, v_qkn_q_nope, v_qkn_q_rope, v_qkn_k_nope, v_qkn_k_rope, v_mla_out_norm_w, v_w_out, v_ffn2_w_in, v_ffn2_w_out):
    args = locals()
    w = {n: args[n] for n in WEIGHT_NAMES}
    m = {n: args["m_" + n] for n in WEIGHT_NAMES}
    v = {n: args["v_" + n] for n in WEIGHT_NAMES}
    me = 4 * lax.axis_index("x") + 2 * lax.axis_index("y") + lax.axis_index("c")
    cols = N_MOD * D_MODEL // N_DEV
    shard = {n: w[n][0] for n in FFN_SHARDED + tuple(s[0] for s in SHEETED)}

    sc = c * _sigmoid(c)
    first = _to_sheet(jnp.concatenate([sc.reshape(-1), shard["gdn_conv_w"].reshape(-1)]), F32, 8)
    (first_all,) = _all_gather([first], "gather_c")
    sc_all = first_all[:, :D_MODEL // 128].reshape(N_DEV, D_MODEL)
    n_taps = shard["gdn_conv_w"].size
    conv_all = first_all.reshape(N_DEV, -1)[:, D_MODEL:D_MODEL + n_taps].reshape(N_DEV, 4, -1)
    b_mine = lax.dynamic_slice(b_ada, (0, me * cols), (1, cols))
    mod_cols = _mm(sc_all, w_ada[0], "nn", name="ada_mod", extra_params=[b_mine], epi=lambda acc, b_: (acc + b_,))
    (mod_all,) = _all_to_all([_to_sheet(mod_cols, F32, 8)], "scatter_mod")
    mod = mod_all.reshape(N_DEV, -1)[:, :cols].reshape(1, N_MOD * D_MODEL)

    f1_shards, mod = lax.optimization_barrier(([shard["ffn1_w_in"].astype(BF16), shard["ffn1_w_out"].astype(BF16)], mod))
    f1_w8, f1_out = _all_gather(f1_shards, "gather_w1")
    travel = [s for s in SHEETED if s[0] != "gdn_conv_w"]
    tied = lax.optimization_barrier(([shard[n].astype(BF16) for n, _ in travel], f1_w8))
    f1_w8 = tied[1]
    mixer_w = _exchange_start(tied[0], False, "gather_wm_start")
    ffn2_w = _exchange_start([shard["ffn2_w_in"].astype(BF16) + mixer_w[4][0:1, 0:1].astype(BF16),
                              shard["ffn2_w_out"].astype(BF16)], False, "gather_w2_start")
    mod = mod + ffn2_w[4][0:1, 0:1]
    W1 = dict(f1_w8=f1_w8, f1_wo4=f1_out.reshape(HID_PIECES, FFN_PIECE, D_MODEL))

    def mixer_weights(after):
        got = _exchange_wait(mixer_w, False, after, "gather_wm_wait")
        P = {n: jnp.concatenate(list(g), axis=1) if kind == "col" else g.reshape(-1, g.shape[-1])
             for (n, kind), g in zip(travel, got)}
        P["gdn_conv_w"] = jnp.concatenate(list(conv_all), axis=1)
        for n in SMALL:
            P[n] = w[n]
        return _pack_weights(P)

    def ffn2_weights(after):
        f2_w8, f2_out = _exchange_wait(ffn2_w, False, after, "gather_w2_wait")
        return dict(f2_w8=f2_w8, f2_wo4=f2_out.reshape(HID_PIECES, FFN_PIECE, D_MODEL))

    pending, small_grads = {}, {}

    def ffn_grad_ready(tag):
        def ready(which, g):
            pieces = g if which == "w8" else g.reshape((N_DEV,) + shard["ffn1_w_out"].shape)
            pending[tag + which] = _exchange_start([pieces], True, "scatter_%s_%s_start" % (tag, which))
            return pending[tag + which][4]
        return ready

    def mixer_grads_ready(G):
        g_full = _unpack_grads(G)
        small_grads.update({n: g_full[n] for n in SMALL})
        small_grads["gdn_conv_w"] = g_full["gdn_conv_w"]
        pieces = []
        for n, kind in travel:
            r, cc = shard[n].shape
            g = g_full[n].astype(BF16)
            pieces.append(jnp.stack([g[:, cc * p:cc * (p + 1)] for p in range(N_DEV)]) if kind == "col"
                          else g.reshape(N_DEV, r, cc))
        pending["mixer"] = _exchange_start(pieces, True, "scatter_mx_start")
        return pending["mixer"][4][0:1, 0:1]

    cos_p, sin_p = _rope_tables(positions[0])
    loss_row, dx, d_mod = _local_step(x[0], loss_target[0], mod, cos_p, sin_p, W1, mixer_weights, ffn2_weights,
                                      ffn_grad_ready, mixer_grads_ready)

    sheet = _small_sheet(d_mod, small_grads).at[LOSS_ROW].set(loss_row[0, :128])
    sheet = sheet.at[CONV_ROW:CONV_ROW + CONV_ROWS].set(small_grads["gdn_conv_w"].reshape(CONV_ROWS, 128))
    (sheets,) = _all_gather([sheet], "gather_small")
    summed = _sum_devices(sheets, "sum_small")
    d_mod_all = sheets[:, :MOD_ROWS].reshape(N_DEV, N_MOD * D_MODEL)
    d_mod_mine = lax.dynamic_slice(d_mod_all, (0, me * cols), (N_DEV, cols))
    grads = _from_small_sheet(summed)
    grads["w_ada"] = _mm(sc_all, d_mod_mine, "tn", name="ada_gw", hi=True)
    conv_taps = shard["gdn_conv_w"].shape[1]
    grads["gdn_conv_w"] = lax.dynamic_slice(summed[CONV_ROW:CONV_ROW + CONV_ROWS].reshape(4, -1), (0, me * conv_taps),
                                            (4, conv_taps))
    loss = summed[LOSS_ROW, 0]

    delta, new_m, new_v = {}, {}, {}
    arrived = {}
    for n, key in zip(FFN_SHARDED, ("f1w8", "f1wo4", "f2w8", "f2wo4")):
        (arrived[n],) = _exchange_wait(pending[key], True, summed, "scatter_%s_wait" % key)
    arrived.update(zip([n for n, _ in travel], _exchange_wait(pending["mixer"], True, summed, "scatter_mx_wait")))
    for n, parts in arrived.items():
        if n in TRANSPOSED_ENTRY:
            res = _sum_adamw(parts, w[n][0].T, m[n][0].T, v[n][0].T, "adamw_" + n, transposed=True)
            grads[n], delta[n], new_m[n], new_v[n] = [r.T for r in res]
        else:
            grads[n], delta[n], new_m[n], new_v[n] = _sum_adamw(parts, w[n][0], m[n][0], v[n][0], "adamw_" + n)
    for n in ("w_ada", "gdn_conv_w"):
        delta[n], new_m[n], new_v[n] = _adamw(w[n][0], grads[n], m[n][0], v[n][0], "adamw_" + n)
    small_in = [_small_sheet(t["b_ada"], t) for t in (w, grads, m, v)]
    for res, out in zip(_adamw(*small_in, "adamw_small"), (delta, new_m, new_v)):
        out.update(_from_small_sheet(res))

    def shaped(d):
        return [d[n].reshape(w[n].shape) for n in WEIGHT_NAMES]

    return (loss, dx[None], *shaped(grads), *shaped(delta), *shaped(new_m), *shaped(new_v))
```

```python
import functools

import jax
import jax.numpy as jnp
import numpy as np
from jax import lax
from jax.experimental import pallas as pl
from jax.experimental.pallas import tpu as pltpu

F32 = jnp.float32
BF16 = jnp.bfloat16

D_MODEL = 1024
D_FF = 2816
N_MOD = 9
HEADS = 4
HEAD_DIM = 128
CHUNK = 64
EPS = 1e-6
ROPE = 64
Q_LORA = 384
KV_LORA = 256
N_IN = 2760
N_IN_PACKED = 2816
ROPE_BASE = 10000.0
LOG2_E = 1.4426950408889634
N_DEV = 8

ADAM_LR = 0.001
ADAM_B1 = 0.9
ADAM_B2 = 0.999
ADAM_EPS = 1e-08
ADAM_WD = 0.01
ADAM_STEP = 10

VMEM_LIMIT_BYTES = 56 * 1024 * 1024
MATMUL_ROWS = (1024, 512, 256, 128)
MESH = pl.DeviceIdType.MESH


def _params(sem=None):
    return pltpu.CompilerParams(dimension_semantics=sem, vmem_limit_bytes=VMEM_LIMIT_BYTES)


def _pick(dim, prefs):
    for p in prefs:
        if dim % p == 0:
            return p
    return dim


_DIMS = {"nn": (((1,), (0,)), ((), ())), "nt": (((1,), (1,)), ((), ())), "tn": (((0,), (0,)), ((), ()))}


def _dot_raw(a, b, mode):
    return lax.dot_general(a.astype(BF16), b.astype(BF16), _DIMS[mode], preferred_element_type=F32)


def _dot_hi(a, b, mode="nn"):
    return lax.dot_general(a, b, _DIMS[mode], precision=lax.Precision.HIGHEST, preferred_element_type=F32)


@functools.partial(jax.custom_vjp, nondiff_argnums=(2,))
def _bdot(a, b, mode):
    return _dot_raw(a, b, mode)


def _bdot_fwd(a, b, mode):
    return _dot_raw(a, b, mode), (a, b)


def _bdot_bwd(mode, res, g):
    a, b = res
    if mode == "nn":
        return _dot_raw(g, b, "nt"), _dot_raw(a, g, "tn")
    if mode == "nt":
        return _dot_raw(g, b, "nn"), _dot_raw(g, a, "tn")
    return _dot_raw(b, g, "nt"), _dot_raw(a, g, "nn")


_bdot.defvjp(_bdot_fwd, _bdot_bwd)


def _mm(a, b, mode, *, name, out_dtypes=(F32,), epi=None, extras=(), extra_params=(), hi=False,
        tm=None, tn=None, tk=None):
    if mode == "nn":
        (M, K), (_, N) = a.shape, b.shape
    elif mode == "nt":
        (M, K), (N, _) = a.shape, b.shape
    else:
        (K, M), (_, N) = a.shape, b.shape
    tm = tm or _pick(M, (512, 1408, 256, 128) if mode == "tn" else MATMUL_ROWS + (384, 352))
    tn = tn or _pick(N, (1024, 1408, 768, 512, 384, 256, 128))
    tk = tk or _pick(K, (1024, 1408, 512, 384, 256, 128))
    a_spec = {"nn": pl.BlockSpec((tm, tk), lambda i, j, k: (i, k)), "nt": pl.BlockSpec((tm, tk), lambda i, j, k: (i, k)),
              "tn": pl.BlockSpec((tk, tm), lambda i, j, k: (k, i))}[mode]
    b_spec = {"nn": pl.BlockSpec((tk, tn), lambda i, j, k: (k, j)), "nt": pl.BlockSpec((tn, tk), lambda i, j, k: (j, k)),
              "tn": pl.BlockSpec((tk, tn), lambda i, j, k: (k, j))}[mode]
    mn_spec = pl.BlockSpec((tm, tn), lambda i, j, k: (i, j))
    return _mmg(a, b, mode, name=name, grid=(M // tm, N // tn, K // tk), a_spec=a_spec, b_spec=b_spec, out_spec=mn_spec,
                out_shapes=[jax.ShapeDtypeStruct((M, N), dt) for dt in out_dtypes], acc_shape=(tm, tn), epi=epi,
                extras=list(extras) + list(extra_params),
                extra_specs=[mn_spec] * len(extras) + [pl.BlockSpec((1, tn), lambda i, j, k: (0, j))] * len(extra_params),
                hi=hi)


def _mmg(a, b, mode, *, name, grid, a_spec, b_spec, out_spec, out_shapes, acc_shape, epi=None, extras=(),
         extra_specs=(), hi=False):
    nk = grid[2]
    n_e, n_o = len(extras), len(out_shapes)

    def body(*refs):
        a_ref, b_ref = refs[:2]
        e_refs = refs[2:2 + n_e]
        o_refs = refs[2 + n_e:2 + n_e + n_o]
        acc_ref = refs[-1]
        k = pl.program_id(2)

        @pl.when(k == 0)
        def _():
            acc_ref[...] = jnp.zeros_like(acc_ref)

        if hi:
            acc_ref[...] += _dot_hi(a_ref[...].astype(F32), b_ref[...].astype(F32), mode)
        else:
            acc_ref[...] += _dot_raw(a_ref[...], b_ref[...], mode)

        @pl.when(k == nk - 1)
        def _():
            acc = acc_ref[...]
            outs = (acc,) if epi is None else epi(acc, *[e[...].astype(F32) for e in e_refs])
            for o_ref, o in zip(o_refs, outs):
                o_ref[...] = o.astype(o_ref.dtype)

    outs = pl.pallas_call(
        body, name=name, grid=grid,
        in_specs=[a_spec, b_spec] + list(extra_specs),
        out_specs=[out_spec] * n_o,
        out_shape=list(out_shapes),
        scratch_shapes=[pltpu.VMEM(acc_shape, F32)],
        compiler_params=_params(("parallel", "parallel", "arbitrary")),
    )(a, b, *extras)
    return outs if n_o > 1 else outs[0]


def _row_spec(th, cw, ci):
    return pl.BlockSpec((th, cw), lambda i: (i, ci))


def _full_spec(shape):
    return pl.BlockSpec(shape, lambda i: (0,) * len(shape))


def _rowwise(fn, rows, params, outs, n_steps, name):
    n_r, n_p, n_o = len(rows), len(params), len(outs)

    def body(*refs):
        vals = [r[...].astype(F32) for r in refs[:n_r + n_p]]
        res = fn(*vals)
        for o_ref, o in zip(refs[n_r + n_p:], res):
            o_ref[...] = o.astype(o_ref.dtype)

    across = [len(o) == 4 for o in outs]
    res = pl.pallas_call(
        body, name=name, grid=(n_steps,),
        in_specs=[_row_spec(th, cw, ci) for (_, th, cw, ci) in rows] + [_full_spec(p.shape) for p in params],
        out_specs=[pl.BlockSpec((o[0], o[1]), lambda i: (0, i)) if ac else _row_spec(o[0], o[1], 0)
                   for o, ac in zip(outs, across)],
        out_shape=[jax.ShapeDtypeStruct((o[0], n_steps * o[1]) if ac else (n_steps * o[0], o[1]), o[2])
                   for o, ac in zip(outs, across)],
        compiler_params=_params(("parallel",)),
    )(*[r[0] for r in rows], *params)
    return res


def _rowwise_bwd(fn, rows, aux, params, douts, n_steps, name, row_dtypes=None, adds=()):
    n_r, n_a, n_p, n_d, n_add = len(rows), len(aux), len(params), len(douts), len(adds)
    row_dtypes = row_dtypes or (F32,) * n_r

    def body(*refs):
        it = iter(refs)
        r_vals = [next(it)[...].astype(F32) for _ in range(n_r)]
        a_vals = [next(it)[...].astype(F32) for _ in range(n_a)]
        p_vals = [next(it)[...].astype(F32) for _ in range(n_p)]
        d_vals = [next(it)[...].astype(F32) for _ in range(n_d)]
        add_vals = [next(it)[...].astype(F32) for _ in range(n_add)]
        dr_refs = [next(it) for _ in range(n_r)]
        dp_refs = [next(it) for _ in range(n_p)]

        def f(*rp):
            return tuple(fn(*rp[:n_r], *a_vals, *rp[n_r:]))

        _, vjp = jax.vjp(f, *r_vals, *p_vals)
        grads = list(vjp(tuple(d_vals)))
        for (ri, _), av in zip(adds, add_vals):
            grads[ri] = grads[ri] + av
        for dr_ref, g in zip(dr_refs, grads[:n_r]):
            dr_ref[...] = g.astype(dr_ref.dtype)

        @pl.when(pl.program_id(0) == 0)
        def _():
            for dp_ref in dp_refs:
                dp_ref[...] = jnp.zeros_like(dp_ref)

        for dp_ref, g in zip(dp_refs, grads[n_r:]):
            dp_ref[...] += g

    all_rows = list(rows) + list(aux) + list(douts) + [(arr,) + tuple(rows[ri][1:3]) + (0,) for ri, arr in adds]
    in_specs = ([_row_spec(th, cw, ci) for (_, th, cw, ci) in list(rows) + list(aux)]
                + [_full_spec(p.shape) for p in params]
                + [_row_spec(th, cw, ci) for (_, th, cw, ci) in all_rows[n_r + n_a:]])
    res = pl.pallas_call(
        body, name=name, grid=(n_steps,),
        in_specs=in_specs,
        out_specs=[_row_spec(th, cw, 0) for (_, th, cw, _) in rows] + [_full_spec(p.shape) for p in params],
        out_shape=[jax.ShapeDtypeStruct((n_steps * th, cw), dt) for (_, th, cw, _), dt in zip(rows, row_dtypes)]
        + [jax.ShapeDtypeStruct(p.shape, F32) for p in params],
        compiler_params=_params(("arbitrary",)),
    )(*[r[0] for r in list(rows) + list(aux)], *params, *[r[0] for r in all_rows[n_r + n_a:]])
    return res[:n_r], res[n_r:]


def _sigmoid(x):
    return lax.logistic(x)


def _silu(x):
    return x * _sigmoid(x)


def _rms(x, w=None, n=None):
    n = n or x.shape[-1]
    y = x * lax.rsqrt(jnp.sum(x * x, axis=-1, keepdims=True) * (1.0 / n) + EPS)
    return y if w is None else y * w


def _modulate(x, scale, shift):
    return _rms(x) * (1.0 + scale) + shift


def _softplus(x):
    return jnp.maximum(x, 0.0) + jnp.log1p(jnp.exp(-jnp.abs(x)))


@jax.custom_vjp
def _rot_half64(x):
    lane = lax.broadcasted_iota(jnp.int32, x.shape, 1)
    up = pltpu.roll(x, 96, 1)
    down = pltpu.roll(x, 32, 1)
    return jnp.where(lane < 32, up, jnp.where(lane < 64, down, 0.0))


_rot_half64.defvjp(lambda x: (_rot_half64(x), None), lambda _, g: (_rot_half64(g),))


def _rope128(x, cos_p, sin_p):
    return x * cos_p + _rot_half64(x) * sin_p


def _gdn_pre_fn(qkvc, kab, alog_p, dt_p):
    a = _silu(qkvc)
    qs, ks = [], []
    for h in range(HEADS):
        qh = a[:, HEAD_DIM * h:HEAD_DIM * (h + 1)]
        kh = a[:, 512 + HEAD_DIM * h:512 + HEAD_DIM * (h + 1)]
        qs.append(qh * lax.rsqrt(jnp.sum(qh * qh, axis=-1, keepdims=True) + EPS) * (HEAD_DIM ** -0.5))
        ks.append(kh * lax.rsqrt(jnp.sum(kh * kh, axis=-1, keepdims=True) + EPS))
    lane = lax.broadcasted_iota(jnp.int32, kab.shape, 1)
    g_full = -jnp.exp(alog_p) * _softplus(kab + dt_p)
    b_full = _sigmoid(kab)
    gb = jnp.where((lane >= 64) & (lane < 68), g_full, jnp.where((lane >= 68) & (lane < 72), b_full, 0.0))
    return jnp.concatenate(qs, axis=1), jnp.concatenate(ks, axis=1), a[:, 1024:1536], gb


INTRA_ROWS = (512, 256, 128, 64)
TOKEN_ROWS = (512, 256, 128)

_BNN = (((2,), (1,)), ((0,), (0,)))
_BNT = (((2,), (2,)), ((0,), (0,)))
_BTN = (((1,), (1,)), ((0,), (0,)))


def _split_bf16(a):
    hi = a.astype(BF16)
    return hi, (a - hi.astype(F32)).astype(BF16)


def _dot3_raw(a, b, dims):
    a_hi, a_lo = _split_bf16(a)
    b_hi, b_lo = _split_bf16(b)
    dot = lambda x_, y_: lax.dot_general(x_, y_, dims, preferred_element_type=F32)
    return dot(a_hi, b_hi) + (dot(a_hi, b_lo) + dot(a_lo, b_hi))


@functools.partial(jax.custom_vjp, nondiff_argnums=(2, 3))
def _dot3(a, b, nt, exact_bwd=True):
    return _dot3_raw(a, b, _BNT if nt else _BNN)


def _dot3_fwd(a, b, nt, exact_bwd):
    return _dot3_raw(a, b, _BNT if nt else _BNN), (a, b)


def _dot3_bwd(nt, exact_bwd, res, g):
    a, b = res
    if exact_bwd:
        dot = _dot3_raw
    else:
        dot = lambda x_, y_, d_: lax.dot_general(x_.astype(BF16), y_.astype(BF16), d_, preferred_element_type=F32)
    if nt:
        return dot(g, b, _BNN), dot(jnp.swapaxes(g, 1, 2), a, _BNN)
    return dot(g, b, _BNT), dot(jnp.swapaxes(a, 1, 2), g, _BNN)


_dot3.defvjp(_dot3_fwd, _dot3_bwd)


@functools.partial(jax.custom_vjp, nondiff_argnums=(2,))
def _bdot_b(a, b, nt):
    return lax.dot_general(a.astype(BF16), b.astype(BF16), _BNT if nt else _BNN, preferred_element_type=F32)


def _bdot_b_fwd(a, b, nt):
    return _bdot_b(a, b, nt), (a, b)


def _bdot_b_bwd(nt, res, g):
    a, b = res
    dot = lambda x_, y_, d_: lax.dot_general(x_.astype(BF16), y_.astype(BF16), d_, preferred_element_type=F32)
    if nt:
        return dot(g, b, _BNN), dot(jnp.swapaxes(g, 1, 2), a, _BNN)
    return dot(g, b, _BNT), dot(jnp.swapaxes(a, 1, 2), g, _BNN)


_bdot_b.defvjp(_bdot_b_fwd, _bdot_b_bwd)


@jax.custom_vjp
def _inverse_given(a_mat, inv):
    return inv


def _inverse_given_bwd(inv, g):
    inv_t = jnp.swapaxes(inv, 1, 2)
    return -_dot3_raw(_dot3_raw(inv_t, g, _BNN), inv_t, _BNN), jnp.zeros_like(inv)


_inverse_given.defvjp(lambda a_mat, inv: (inv, inv), _inverse_given_bwd)


def _intra_batched(q, k, v, g_col, b_col, inv_known=None):
    c = CHUNK
    nb = q.shape[0]
    row = lax.broadcasted_iota(jnp.int32, (1, c, c), 1)
    col = lax.broadcasted_iota(jnp.int32, (1, c, c), 2)
    incl, strict, eye = row >= col, row > col, row == col
    tri = jnp.broadcast_to(jnp.where(incl, 1.0, 0.0).astype(F32), (nb, c, c))
    ident = jnp.where(eye, 1.0, 0.0).astype(F32)
    g_wide = _dot3(tri, jnp.broadcast_to(g_col, (nb, c, HEAD_DIM)), False)
    g_i = g_wide[:, :, :c]
    g_j = jnp.sum(jnp.where(eye, g_i, 0.0), axis=1, keepdims=True)
    decay = jnp.where(incl, jnp.exp(jnp.where(incl, g_i - g_j, 0.0)), 0.0)
    kk = _bdot_b(k, k, True)
    a_mat = jnp.where(strict, b_col * kk * decay, 0.0)
    if inv_known is None:
        x_pow = -a_mat
        inv = ident + x_pow
        for _ in range(5):
            x_pow = _dot3(x_pow, x_pow, False, False)
            inv = inv + _dot3(inv, x_pow, False, False)
    else:
        inv = _inverse_given(a_mat, inv_known)
    e_wide = jnp.exp(g_wide)
    u = _dot3(inv, v * b_col, False)
    wk = _dot3(inv, k * b_col * e_wide, False)
    qk = _bdot_b(q, k, True) * decay
    last = lax.broadcasted_iota(jnp.int32, (1, c, HEAD_DIM), 1) == c - 1
    g_last = jnp.sum(jnp.where(last, g_wide, 0.0), axis=1, keepdims=True)
    qd = q * e_wide
    kd = k * jnp.exp(g_last - g_wide)
    gl = jnp.broadcast_to(jnp.exp(g_last), (nb, 8, HEAD_DIM))
    return u, wk, qd, kd, qk, gl, inv


def _gdn_intra_fn(q, k, v, gb, *inv_known):
    t = q.shape[0]
    nch = t // CHUNK
    lane = lax.broadcasted_iota(jnp.int32, gb.shape, 1)

    def heads_first(x_):
        return jnp.concatenate([x_[:, HEAD_DIM * h:HEAD_DIM * (h + 1)].reshape(nch, CHUNK, HEAD_DIM) for h in range(HEADS)],
                               axis=0)

    def column(first_lane):
        return jnp.concatenate([jnp.sum(jnp.where(lane == first_lane + h, gb, 0.0), axis=1, keepdims=True)
                                .reshape(nch, CHUNK, 1) for h in range(HEADS)], axis=0)

    known = jnp.concatenate([x_.reshape(nch, CHUNK, CHUNK) for x_ in inv_known], axis=0) if inv_known else None
    u, wk, qd, kd, qk, gl, inv = _intra_batched(heads_first(q), heads_first(k), heads_first(v), column(64), column(68), known)

    def rows_first(x_):
        r, w_ = x_.shape[1], x_.shape[2]
        return jnp.concatenate([x_[nch * h:nch * (h + 1)].reshape(nch * r, w_) for h in range(HEADS)], axis=1)

    per_head = lambda x_: [x_[nch * h:nch * (h + 1)].reshape(t, CHUNK) for h in range(HEADS)]
    outs = (rows_first(u), rows_first(wk), rows_first(qd), rows_first(kd), *per_head(qk), rows_first(gl))
    return outs if inv_known else outs + tuple(per_head(inv))


def _scan_step(s0, u, wk, qd, kd, qk, gl):
    v_new = u - _bdot_b(wk, s0, False)
    o = _bdot_b(qd, s0, False) + _bdot_b(qk, v_new, False)
    s1 = s0 * gl[:, 0:1, :] + _bdot_b(jnp.swapaxes(kd, 1, 2), v_new, False)
    return o, s1


def _mix_post_fn(o_a, z, o_b, gnw, onw):
    parts = [_rms(o_a[:, HEAD_DIM * h:HEAD_DIM * (h + 1)], gnw) * _silu(z[:, HEAD_DIM * h:HEAD_DIM * (h + 1)])
             for h in range(HEADS)]
    parts += [_rms(o_b[:, HEAD_DIM * h:HEAD_DIM * (h + 1)], onw) for h in range(HEADS)]
    return (jnp.concatenate(parts, axis=1),)


def _mla_pre_fn(ckv, cq, kab, cos_p, sin_p, qnw, kvnw, wuq, wukv, qn_w, qr_w, kn_w, kr_w):
    scale = (HEAD_DIM + ROPE) ** -0.5 * LOG2_E
    qf = _bdot(_rms(cq, qnw), wuq, "nn")
    kvf = _bdot(_rms(ckv, kvnw), wukv, "nn")
    lane = lax.broadcasted_iota(jnp.int32, kab.shape, 1)
    kr = _rope128(_rms(jnp.where(lane < ROPE, kab, 0.0), kr_w, n=ROPE), cos_p, sin_p)
    qs, ks = [], []
    for h in range(HEADS):
        qn = _rms(qf[:, 256 * h:256 * h + 128], qn_w) * scale
        qr = _rope128(_rms(qf[:, 256 * h + 128:256 * h + 256], qr_w, n=ROPE), cos_p, sin_p) * scale
        qs += [qn, qr]
        ks += [_rms(kvf[:, 128 * h:128 * (h + 1)], kn_w), kr]
    return jnp.concatenate(qs, axis=1), jnp.concatenate(ks, axis=1), kvf[:, 512:]


def _conv_fwd(proj, conv_w, tm, name):
    S = proj.shape[0]
    C = 1536
    nb = tm // 8

    def body(x_ref, prev_ref, w_ref, o_ref, ext_ref):
        i = pl.program_id(0)
        ext_ref[0:8, :] = jnp.where(i > 0, prev_ref[...], 0.0)
        ext_ref[8:, :] = x_ref[...]
        acc = jnp.zeros((tm, C), F32)
        for k in range(4):
            acc = acc + w_ref[k:k + 1, :] * ext_ref[pl.ds(5 + k, tm), :]
        o_ref[...] = acc

    return pl.pallas_call(
        body, name=name, grid=(S // tm,),
        in_specs=[pl.BlockSpec((tm, C), lambda i: (i, 0)),
                  pl.BlockSpec((8, C), lambda i: (jnp.maximum(i * nb - 1, 0), 0)),
                  pl.BlockSpec((4, C), lambda i: (0, 0))],
        out_specs=pl.BlockSpec((tm, C), lambda i: (i, 0)),
        out_shape=jax.ShapeDtypeStruct((S, C), F32),
        scratch_shapes=[pltpu.VMEM((tm + 8, C), F32)],
        compiler_params=_params(("arbitrary",)),
    )(proj, proj, conv_w)


def _conv_bwd(proj, dout, conv_w, tm, name):
    S = proj.shape[0]
    C = 1536
    nb = tm // 8
    n_steps = S // tm

    def body(x_ref, prev_ref, d_ref, next_ref, w_ref, dx_ref, dw_ref, xext_ref, dext_ref):
        i = pl.program_id(0)
        xext_ref[0:8, :] = jnp.where(i > 0, prev_ref[...], 0.0)
        xext_ref[8:, :] = x_ref[...]
        dext_ref[0:tm, :] = d_ref[...]
        dext_ref[tm:, :] = jnp.where(i < n_steps - 1, next_ref[...], 0.0)
        d = d_ref[...]
        acc = jnp.zeros((tm, C), F32)
        dws = []
        for k in range(4):
            acc = acc + w_ref[k:k + 1, :] * dext_ref[pl.ds(3 - k, tm), :]
            dws.append(jnp.sum(d * xext_ref[pl.ds(5 + k, tm), :], axis=0, keepdims=True))
        dx_ref[...] = acc.astype(dx_ref.dtype)

        @pl.when(i == 0)
        def _():
            dw_ref[...] = jnp.zeros_like(dw_ref)

        dw_ref[...] += jnp.concatenate(dws + [jnp.zeros((4, C), F32)], axis=0)

    return pl.pallas_call(
        body, name=name, grid=(n_steps,),
        in_specs=[pl.BlockSpec((tm, C), lambda i: (i, 0)),
                  pl.BlockSpec((8, C), lambda i: (jnp.maximum(i * nb - 1, 0), 0)),
                  pl.BlockSpec((tm, C), lambda i: (i, 0)),
                  pl.BlockSpec((8, C), lambda i: (jnp.minimum((i + 1) * nb, S // 8 - 1), 0)),
                  pl.BlockSpec((4, C), lambda i: (0, 0))],
        out_specs=[pl.BlockSpec((tm, C), lambda i: (i, 0)), pl.BlockSpec((8, C), lambda i: (0, 0))],
        out_shape=[jax.ShapeDtypeStruct((S, C), BF16), jax.ShapeDtypeStruct((8, C), F32)],
        scratch_shapes=[pltpu.VMEM((tm + 8, C), F32), pltpu.VMEM((tm + 8, C), F32)],
        compiler_params=_params(("arbitrary",)),
    )(proj, proj, dout, dout, conv_w)


SCAN_CHUNKS = (8, 4, 2, 1)


def _gdn_scan_fwd(u, wk, qd, kd, qks, gl, name):
    S = u.shape[0]
    nc = S // CHUNK
    cs = _pick(nc, SCAN_CHUNKS)
    W = HEADS * HEAD_DIM

    def body(u_ref, wk_ref, qd_ref, kd_ref, qk0, qk1, qk2, qk3, gl_ref, o_ref, sp_ref, s_ref):
        @pl.when(pl.program_id(0) == 0)
        def _():
            s_ref[...] = jnp.zeros_like(s_ref)

        state = s_ref[...]
        for c in range(cs):
            rows, gl_rows = slice(CHUNK * c, CHUNK * (c + 1)), slice(8 * c, 8 * (c + 1))
            sp_ref[c] = state
            o, state = _scan_step(state, _heads(u_ref, HEAD_DIM, rows), _heads(wk_ref, HEAD_DIM, rows),
                                  _heads(qd_ref, HEAD_DIM, rows), _heads(kd_ref, HEAD_DIM, rows),
                                  jnp.stack([r[rows, :] for r in (qk0, qk1, qk2, qk3)]), _heads(gl_ref, HEAD_DIM, gl_rows))
            for h in range(HEADS):
                o_ref[rows, HEAD_DIM * h:HEAD_DIM * (h + 1)] = o[h]
        s_ref[...] = state

    row = pl.BlockSpec((cs * CHUNK, W), lambda n: (n, 0))
    qk_spec = pl.BlockSpec((cs * CHUNK, CHUNK), lambda n: (n, 0))
    return pl.pallas_call(
        body, name=name, grid=(nc // cs,),
        in_specs=[row, row, row, row, qk_spec, qk_spec, qk_spec, qk_spec, pl.BlockSpec((cs * 8, W), lambda n: (n, 0))],
        out_specs=[row, pl.BlockSpec((cs, HEADS, HEAD_DIM, HEAD_DIM), lambda n: (n, 0, 0, 0))],
        out_shape=[jax.ShapeDtypeStruct((S, W), F32), jax.ShapeDtypeStruct((nc, HEADS, HEAD_DIM, HEAD_DIM), F32)],
        scratch_shapes=[pltpu.VMEM((HEADS, HEAD_DIM, HEAD_DIM), F32)],
        compiler_params=_params(("arbitrary",)),
    )(u, wk, qd, kd, *qks, gl)


def _gdn_scan_bwd(u, wk, qd, kd, qks, gl, s_prev, d_o, name):
    S = u.shape[0]
    nc = S // CHUNK
    cs = _pick(nc, SCAN_CHUNKS)
    nb = nc // cs
    W = HEADS * HEAD_DIM

    def body(u_ref, wk_ref, qd_ref, kd_ref, qk0, qk1, qk2, qk3, gl_ref, sp_ref, do_ref,
             du_ref, dwk_ref, dqd_ref, dkd_ref, dqk0, dqk1, dqk2, dqk3, dgl_ref, ds_ref):
        @pl.when(pl.program_id(0) == 0)
        def _():
            ds_ref[...] = jnp.zeros_like(ds_ref)

        d_state = ds_ref[...]
        for c in reversed(range(cs)):
            rows, gl_rows = slice(CHUNK * c, CHUNK * (c + 1)), slice(8 * c, 8 * (c + 1))
            _, vjp = jax.vjp(_scan_step, sp_ref[c], _heads(u_ref, HEAD_DIM, rows), _heads(wk_ref, HEAD_DIM, rows),
                             _heads(qd_ref, HEAD_DIM, rows), _heads(kd_ref, HEAD_DIM, rows),
                             jnp.stack([r[rows, :] for r in (qk0, qk1, qk2, qk3)]), _heads(gl_ref, HEAD_DIM, gl_rows))
            d_state, du, dwk, dqd, dkd, dqk, dgl = vjp((_heads(do_ref, HEAD_DIM, rows), d_state))
            for h, dqk_ref in enumerate((dqk0, dqk1, dqk2, dqk3)):
                sl = slice(HEAD_DIM * h, HEAD_DIM * (h + 1))
                du_ref[rows, sl] = du[h]
                dwk_ref[rows, sl] = dwk[h]
                dqd_ref[rows, sl] = dqd[h]
                dkd_ref[rows, sl] = dkd[h]
                dqk_ref[rows, :] = dqk[h]
                dgl_ref[gl_rows, sl] = dgl[h]
        ds_ref[...] = d_state

    rev = lambda n: (nb - 1 - n, 0)
    row = pl.BlockSpec((cs * CHUNK, W), rev)
    qk_spec = pl.BlockSpec((cs * CHUNK, CHUNK), rev)
    gl_spec = pl.BlockSpec((cs * 8, W), rev)
    qk_shape = jax.ShapeDtypeStruct((S, CHUNK), F32)
    row_shape = jax.ShapeDtypeStruct((S, W), F32)
    return pl.pallas_call(
        body, name=name, grid=(nb,),
        in_specs=[row, row, row, row, qk_spec, qk_spec, qk_spec, qk_spec, gl_spec,
                  pl.BlockSpec((cs, HEADS, HEAD_DIM, HEAD_DIM), lambda n: (nb - 1 - n, 0, 0, 0)), row],
        out_specs=[row, row, row, row, qk_spec, qk_spec, qk_spec, qk_spec, gl_spec],
        out_shape=[row_shape] * 4 + [qk_shape] * 4 + [jax.ShapeDtypeStruct((nc * 8, W), F32)],
        scratch_shapes=[pltpu.VMEM((HEADS, HEAD_DIM, HEAD_DIM), F32)],
        compiler_params=_params(("arbitrary",)),
    )(u, wk, qd, kd, *qks, gl, s_prev, d_o)


NEG = -1e30


def _chunk_mask(i, j, t, transposed=False):
    q_axis, k_axis = (1, 0) if transposed else (0, 1)
    r = (i * t + lax.broadcasted_iota(jnp.int32, (t, t), q_axis)) // CHUNK
    c = (j * t + lax.broadcasted_iota(jnp.int32, (t, t), k_axis)) // CHUNK
    return c <= r


def _tile_pairs(n, by_key):
    pairs = [(i, j) for j in range(n) for i in range(j, n)] if by_key else [(i, j) for i in range(n) for j in range(i + 1)]
    return jnp.asarray(np.array([p[0] for p in pairs], np.int32)), jnp.asarray(np.array([p[1] for p in pairs], np.int32))


def _heads(ref, width, rows=slice(None)):
    return jnp.stack([ref[rows, width * h:width * (h + 1)] for h in range(HEADS)])


def _bmm(a, b, dims):
    return lax.dot_general(a.astype(BF16), b.astype(BF16), dims, preferred_element_type=F32)


def _attn_fwd(q, k, v_t, t, name):
    S = q.shape[0]
    n = S // t
    qi, kj = _tile_pairs(n, by_key=False)

    def body(qi_ref, kj_ref, q_ref, k_ref, vt_ref, o_ref, lse_ref, m_ref, l_ref, acc_ref):
        i, j = qi_ref[pl.program_id(0)], kj_ref[pl.program_id(0)]

        @pl.when(j == 0)
        def _():
            m_ref[...] = jnp.full_like(m_ref, NEG)
            l_ref[...] = jnp.zeros_like(l_ref)
            acc_ref[...] = jnp.zeros_like(acc_ref)

        def update(masked):
            s_t = _bmm(_heads(k_ref, 256), _heads(q_ref, 256), _BNT)
            if masked:
                s_t = jnp.where(_chunk_mask(i, j, t, transposed=True)[None], s_t, NEG)
            m_old = m_ref[...]
            m_new = jnp.maximum(m_old, jnp.max(s_t, axis=1, keepdims=True))
            p_t = jnp.exp2(s_t - m_new)
            alpha = jnp.exp2(m_old - m_new)
            l_ref[...] = alpha * l_ref[...] + jnp.sum(p_t, axis=1, keepdims=True)
            v_heads = jnp.stack([vt_ref[HEAD_DIM * h:HEAD_DIM * (h + 1), :] for h in range(HEADS)])
            acc_ref[...] = alpha * acc_ref[...] + _bmm(v_heads, p_t, _BNN)
            m_ref[...] = m_new

        @pl.when(j < i)
        def _():
            update(False)

        @pl.when(j == i)
        def _():
            update(True)
            for h in range(HEADS):
                sl = slice(HEAD_DIM * h, HEAD_DIM * (h + 1))
                o_ref[:, sl] = jnp.transpose(acc_ref[h] / l_ref[h])
                lse_ref[h:h + 1, :] = m_ref[h] + jnp.log(l_ref[h]) * LOG2_E
            lse_ref[HEADS:, :] = jnp.zeros((8 - HEADS, t), F32)

    row = lambda p, qi_, kj_: (qi_[p], 0)
    return pl.pallas_call(
        body, name=name,
        grid_spec=pltpu.PrefetchScalarGridSpec(
            num_scalar_prefetch=2, grid=(qi.shape[0],),
            in_specs=[pl.BlockSpec((t, HEADS * 256), row), pl.BlockSpec((t, HEADS * 256), lambda p, qi_, kj_: (kj_[p], 0)),
                      pl.BlockSpec((HEADS * HEAD_DIM, t), lambda p, qi_, kj_: (0, kj_[p]))],
            out_specs=[pl.BlockSpec((t, HEADS * HEAD_DIM), row), pl.BlockSpec((8, t), lambda p, qi_, kj_: (0, qi_[p]))],
            scratch_shapes=[pltpu.VMEM((HEADS, 1, t), F32), pltpu.VMEM((HEADS, 1, t), F32),
                            pltpu.VMEM((HEADS, HEAD_DIM, t), F32)]),
        out_shape=[jax.ShapeDtypeStruct((S, HEADS * HEAD_DIM), F32), jax.ShapeDtypeStruct((8, S), F32)],
        compiler_params=_params(("arbitrary",)),
    )(qi, kj, q, k, v_t)


def _attn_stats(o, lse, d_o, t, name):
    S = o.shape[0]

    def body(o_ref, lse_ref, do_ref, st_ref):
        lane = lax.broadcasted_iota(jnp.int32, (t, HEAD_DIM), 1)
        stats = jnp.zeros((t, HEAD_DIM), F32)
        for h in range(HEADS):
            sl = slice(HEAD_DIM * h, HEAD_DIM * (h + 1))
            delta = jnp.sum(do_ref[:, sl] * o_ref[:, sl], axis=1, keepdims=True)
            stats = stats + jnp.where(lane == HEADS + h, delta, 0.0)
        st_ref[...] = lse_ref[...] + jnp.transpose(stats)[0:8, :]

    row = pl.BlockSpec((t, HEADS * HEAD_DIM), lambda i: (i, 0))
    col = pl.BlockSpec((8, t), lambda i: (0, i))
    return pl.pallas_call(
        body, name=name, grid=(S // t,),
        in_specs=[row, col, row], out_specs=col,
        out_shape=jax.ShapeDtypeStruct((8, S), F32),
        compiler_params=_params(("parallel",)),
    )(o, lse, d_o)


BWD_GROUP = 2


def _attn_bwd(q, k, v, d_o, stats, t, name):
    S = q.shape[0]
    n = S // t
    groups = HEADS // BWD_GROUP
    gq, gv = BWD_GROUP * 256, BWD_GROUP * HEAD_DIM
    qi, kj = _tile_pairs(n, by_key=True)
    n_pairs = qi.shape[0]
    st = stats.reshape(2, groups, BWD_GROUP, S).transpose(1, 0, 2, 3).reshape(groups, 2 * BWD_GROUP, S)
    st = jnp.pad(st, ((0, 0), (0, 8 - 2 * BWD_GROUP), (0, 0)))

    def heads(ref, width, rows=slice(None)):
        return jnp.stack([ref[rows, width * h:width * (h + 1)] for h in range(BWD_GROUP)])

    def body(qi_ref, kj_ref, q_ref, k_ref, v_ref, do_ref, st_ref, dq_hbm, dk_ref, dv_ref, dq_acc, sem):
        g, p = pl.program_id(0), pl.program_id(1)
        i, j = qi_ref[p], kj_ref[p]

        @pl.when(i == j)
        def _():
            dk_ref[...] = jnp.zeros_like(dk_ref)
            dv_ref[...] = jnp.zeros_like(dv_ref)

        def update(masked):
            qh, kh = heads(q_ref, 256), heads(k_ref, 256)
            d_out = heads(do_ref, HEAD_DIM)
            stv = st_ref[...]
            lse_row = jnp.stack([stv[h:h + 1, :] for h in range(BWD_GROUP)])
            delta_row = jnp.stack([stv[BWD_GROUP + h:BWD_GROUP + h + 1, :] for h in range(BWD_GROUP)])
            s_t = _bmm(kh, qh, _BNT)
            p_t = jnp.exp2(s_t - lse_row)
            if masked:
                p_t = jnp.where(_chunk_mask(i, j, t, transposed=True)[None], p_t, 0.0)
            dv = _bmm(p_t, d_out, _BNN)
            dp_t = _bmm(heads(v_ref, HEAD_DIM), d_out, _BNT)
            ds_t = p_t * (dp_t - delta_row)
            dk = _bmm(ds_t, qh, _BNN)
            dq = _bmm(ds_t, kh, _BTN)
            rows = pl.ds(pl.multiple_of(i * t, t), t)
            for h in range(BWD_GROUP):
                dk_ref[:, 256 * h:256 * (h + 1)] += dk[h]
                dv_ref[:, HEAD_DIM * h:HEAD_DIM * (h + 1)] += dv[h]

            @pl.when(j == 0)
            def _():
                for h in range(BWD_GROUP):
                    dq_acc[rows, 256 * h:256 * (h + 1)] = dq[h]

            @pl.when(j > 0)
            def _():
                for h in range(BWD_GROUP):
                    dq_acc[rows, 256 * h:256 * (h + 1)] += dq[h]

        @pl.when(i == j)
        def _():
            update(True)

        @pl.when(i > j)
        def _():
            update(False)

        @pl.when(i == n - 1)
        def _():
            dk_ref[...] *= 1.0 / LOG2_E

        @pl.when(p == n_pairs - 1)
        def _():
            dq_acc[...] *= 1.0 / LOG2_E
            for gg in range(groups):
                @pl.when(g == gg)
                def _():
                    cp = pltpu.make_async_copy(dq_acc, dq_hbm.at[:, gq * gg:gq * (gg + 1)], sem)
                    cp.start()
                    cp.wait()

    q_blk = lambda g, p, qi_, kj_: (qi_[p], g)
    k_blk = lambda g, p, qi_, kj_: (kj_[p], g)
    return pl.pallas_call(
        body, name=name,
        grid_spec=pltpu.PrefetchScalarGridSpec(
            num_scalar_prefetch=2, grid=(groups, n_pairs),
            in_specs=[pl.BlockSpec((t, gq), q_blk), pl.BlockSpec((t, gq), k_blk), pl.BlockSpec((t, gv), k_blk),
                      pl.BlockSpec((t, gv), q_blk), pl.BlockSpec((None, 8, t), lambda g, p, qi_, kj_: (g, 0, qi_[p]))],
            out_specs=[pl.BlockSpec(memory_space=pl.ANY), pl.BlockSpec((t, gq), k_blk), pl.BlockSpec((t, gv), k_blk)],
            scratch_shapes=[pltpu.VMEM((S, gq), F32), pltpu.SemaphoreType.DMA]),
        out_shape=[jax.ShapeDtypeStruct((S, HEADS * 256), F32), jax.ShapeDtypeStruct((S, HEADS * 256), F32),
                   jax.ShapeDtypeStruct((S, HEADS * HEAD_DIM), F32)],
        compiler_params=_params(("arbitrary", "arbitrary")),
    )(qi, kj, q, k, v, d_o, st)


FFN_PIECE = 2 * D_FF // N_DEV
HID_PIECES = D_FF // FFN_PIECE


def _after_specs(after):
    return [] if after is None else [pl.BlockSpec(memory_space=pl.ANY)]


def _after_args(after):
    return [] if after is None else [after]


def _ffn_gw8(h, d_gate, d_up, name, after=None):
    S = h.shape[0]
    tm = 512
    tk = _pick(S, MATMUL_ROWS)
    nk = S // tk

    def body(h_ref, dg_ref, du_ref, *rest):
        o_ref, acc_ref = rest[-2:]
        k = pl.program_id(1)

        @pl.when(k == 0)
        def _():
            acc_ref[...] = jnp.zeros_like(acc_ref)

        h_t = jnp.transpose(h_ref[...])
        for p in range(HID_PIECES):
            acc_ref[p] += _dot_raw(h_t, dg_ref[p], "nn")
            acc_ref[HID_PIECES + p] += _dot_raw(h_t, du_ref[p], "nn")

        @pl.when(k == nk - 1)
        def _():
            o_ref[...] = acc_ref[...].astype(o_ref.dtype)

    d_spec = pl.BlockSpec((HID_PIECES, tk, FFN_PIECE), lambda i, k: (0, k, 0))
    return pl.pallas_call(
        body, name=name, grid=(D_MODEL // tm, nk),
        in_specs=[pl.BlockSpec((tk, tm), lambda i, k: (k, i)), d_spec, d_spec] + _after_specs(after),
        out_specs=pl.BlockSpec((2 * HID_PIECES, tm, FFN_PIECE), lambda i, k: (0, i, 0)),
        out_shape=jax.ShapeDtypeStruct((2 * HID_PIECES, D_MODEL, FFN_PIECE), BF16),
        scratch_shapes=[pltpu.VMEM((2 * HID_PIECES, tm, FFN_PIECE), F32)],
        compiler_params=_params(("parallel", "arbitrary")),
    )(h, d_gate, d_up, *_after_args(after))


EPILOGUE_ROWS = 256


def _dmod_epilogue(acc_ref, x_ref, do_ref, sc_ref, sh_ref, dx_ref, dsc_ref, dsh_ref, below, below_in, below_out):
    rows_total = acc_ref.shape[0]
    step = min(EPILOGUE_ROWS, rows_total)
    dsc, dsh, dg = 0.0, 0.0, 0.0
    for r in range(rows_total // step):
        rows = slice(step * r, step * (r + 1))
        _, vjp = jax.vjp(_modulate, x_ref[rows, :], sc_ref[...], sh_ref[...])
        dx, dsc_r, dsh_r = vjp(acc_ref[rows, :])
        dx = dx + do_ref[rows, :]
        dx_ref[rows, :] = dx
        dsc, dsh = dsc + dsc_r, dsh + dsh_r
        if below is not None:
            coef = below[2]
            below_out[0][rows, :] = (coef * below_in[1][...] * dx).astype(below_out[0].dtype)
            dg = dg + jnp.sum(coef * below_in[0][rows, :] * dx, axis=0, keepdims=True)
    dsc_ref[...] += dsc
    dsh_ref[...] += dsh
    if below is not None:
        below_out[1][...] += dg


def _ffn_dh(d_gate, d_up, w8, x, d_out, scale, shift, name, after=None, below=None):
    S = d_gate.shape[1]
    tm = _pick(S, MATMUL_ROWS)
    n_below = 0 if below is None else 2

    def body(dg_ref, du_ref, wg_ref, wu_ref, x_ref, do_ref, sc_ref, sh_ref, *rest):
        below_in = rest[:n_below]
        outs = rest[len(rest) - 4 - n_below:]
        dx_ref, dsc_ref, dsh_ref = outs[:3]
        below_out, acc_ref = outs[3:3 + n_below], outs[-1]
        i, k = pl.program_id(0), pl.program_id(1)

        @pl.when(k == 0)
        def _():
            acc_ref[...] = jnp.zeros_like(acc_ref)

        acc_ref[...] += _dot_raw(dg_ref[...], wg_ref[...], "nt") + _dot_raw(du_ref[...], wu_ref[...], "nt")

        @pl.when((k == 0) & (i == 0))
        def _():
            for r in (dsc_ref, dsh_ref) + tuple(below_out[1:]):
                r[...] = jnp.zeros_like(r)

        @pl.when(k == HID_PIECES - 1)
        def _():
            _dmod_epilogue(acc_ref, x_ref, do_ref, sc_ref, sh_ref, dx_ref, dsc_ref, dsh_ref, below, below_in, below_out)

    d_spec = pl.BlockSpec((None, tm, FFN_PIECE), lambda i, k: (k, i, 0))
    row = pl.BlockSpec((tm, D_MODEL), lambda i, k: (i, 0))
    par = pl.BlockSpec((1, D_MODEL), lambda i, k: (0, 0))
    row_shape, par_shape = jax.ShapeDtypeStruct((S, D_MODEL), F32), jax.ShapeDtypeStruct((1, D_MODEL), F32)
    return pl.pallas_call(
        body, name=name, grid=(S // tm, HID_PIECES),
        in_specs=[d_spec, d_spec,
                  pl.BlockSpec((None, D_MODEL, FFN_PIECE), lambda i, k: (k, 0, 0)),
                  pl.BlockSpec((None, D_MODEL, FFN_PIECE), lambda i, k: (k + HID_PIECES, 0, 0)),
                  row, row, par, par] + [row, par][:n_below] + _after_specs(after),
        out_specs=[row, par, par] + [row, par][:n_below],
        out_shape=[row_shape, par_shape, par_shape] + [jax.ShapeDtypeStruct((S, D_MODEL), BF16), par_shape][:n_below],
        scratch_shapes=[pltpu.VMEM((tm, D_MODEL), F32)],
        compiler_params=_params(("arbitrary", "arbitrary")),
    )(d_gate, d_up, w8, w8, x, d_out, scale, shift, *(below[:2] if below is not None else ()), *_after_args(after))


def _swiglu_bwd(d_hid, hid_by_gate, hid_by_up):
    return d_hid * hid_by_gate, d_hid * hid_by_up


def _adamw_math(w_, g_, m_, v_):
    m_ = ADAM_B1 * m_ + (1.0 - ADAM_B1) * g_
    v_ = ADAM_B2 * v_ + (1.0 - ADAM_B2) * (g_ * g_)
    m_hat = m_ / (1.0 - ADAM_B1 ** ADAM_STEP)
    v_hat = v_ / (1.0 - ADAM_B2 ** ADAM_STEP)
    return -ADAM_LR * (m_hat / (jnp.sqrt(v_hat) + ADAM_EPS) + ADAM_WD * w_), m_, v_


def _adamw(w, g, m, v, name):
    R, C = w.shape
    tr = _pick(R, (256, 176, 128, 64, 32, 16, 8))

    def body(w_ref, g_ref, m_ref, v_ref, d_ref, nm_ref, nv_ref):
        d_ref[...], nm_ref[...], nv_ref[...] = _adamw_math(w_ref[...], g_ref[...], m_ref[...], v_ref[...])

    spec = pl.BlockSpec((tr, C), lambda i: (i, 0))
    return pl.pallas_call(
        body, name=name, grid=(R // tr,),
        in_specs=[spec] * 4, out_specs=[spec] * 3,
        out_shape=[jax.ShapeDtypeStruct((R, C), F32)] * 3,
        compiler_params=_params(("parallel",)),
    )(w, g, m, v)


def _sum_adamw(parts, w, m, v, name, transposed=False):
    _, R, C = parts.shape
    tr = _pick(R, (256, 176, 128, 64, 32, 16, 8))

    def body(p_ref, w_ref, m_ref, v_ref, g_ref, d_ref, nm_ref, nv_ref):
        g_ = p_ref[0].astype(F32)
        for d in range(1, N_DEV):
            g_ = g_ + p_ref[d].astype(F32)
        if transposed:
            g_ = jnp.transpose(g_)
        g_ref[...] = g_
        d_ref[...], nm_ref[...], nv_ref[...] = _adamw_math(w_ref[...], g_, m_ref[...], v_ref[...])

    spec = pl.BlockSpec((C, tr), lambda i: (0, i)) if transposed else pl.BlockSpec((tr, C), lambda i: (i, 0))
    return pl.pallas_call(
        body, name=name, grid=(R // tr,),
        in_specs=[pl.BlockSpec((N_DEV, tr, C), lambda i: (0, i, 0)), spec, spec, spec], out_specs=[spec] * 4,
        out_shape=[jax.ShapeDtypeStruct(w.shape, F32)] * 4,
        compiler_params=_params(("parallel",)),
    )(parts, w, m, v)


def _sum_devices(parts, name):
    _, R, C = parts.shape
    tr = _pick(R, (512, 256, 176, 128, 64, 32, 16, 8))

    def body(p_ref, o_ref):
        acc = p_ref[0].astype(F32)
        for d in range(1, N_DEV):
            acc = acc + p_ref[d].astype(F32)
        o_ref[...] = acc

    return pl.pallas_call(
        body, name=name, grid=(R // tr,),
        in_specs=[pl.BlockSpec((N_DEV, tr, C), lambda i: (0, i, 0))],
        out_specs=pl.BlockSpec((tr, C), lambda i: (i, 0)),
        out_shape=jax.ShapeDtypeStruct((R, C), F32),
        compiler_params=_params(("parallel",)),
    )(parts)


def _my_place():
    return lax.axis_index("x"), lax.axis_index("y"), lax.axis_index("c")


def _all_gather(blocks, name):
    n = len(blocks)

    def body(*refs):
        x_refs, out_refs = refs[:n], refs[n:2 * n]
        send_sems, recv_sems, local_sems = refs[2 * n:]
        x, y, c = _my_place()
        me, sibling = (x, y, c), (x, y, 1 - c)
        chips = [(1 - x, y), (x, 1 - y), (1 - x, 1 - y)]

        def copy(a, k, blk, to, own=False):
            slot = out_refs[a].at[4 * blk[0] + 2 * blk[1] + blk[2]]
            return pltpu.make_async_remote_copy(
                src_ref=x_refs[a] if own else slot, dst_ref=slot,
                send_sem=send_sems.at[7 * a + k], recv_sem=recv_sems.at[7 * a + k], device_id=to, device_id_type=MESH)

        mine = [pltpu.make_async_copy(x_refs[a], out_refs[a].at[4 * x + 2 * y + c], local_sems.at[a]) for a in range(n)]
        for cp in mine:
            cp.start()
        first = []
        for j, chip in enumerate(chips):
            first += [copy(a, 1 + j, me, (*chip, c), own=True) for a in range(n)]
        first += [copy(a, 0, me, sibling, own=True) for a in range(n)]
        for cp in first:
            cp.start()
        passed = []
        for j, chip in enumerate(chips):
            for a in range(n):
                copy(a, 1 + j, (*chip, c), me).wait_recv()
                passed.append(copy(a, 4 + j, (*chip, c), sibling))
                passed[-1].start()
        for a in range(n):
            copy(a, 0, sibling, me).wait_recv()
        for j, chip in enumerate(chips):
            for a in range(n):
                copy(a, 4 + j, (*chip, 1 - c), me).wait_recv()
        for cp in first + passed:
            cp.wait_send()
        for cp in mine:
            cp.wait()

    return pl.pallas_call(
        body, name=name,
        out_shape=[jax.ShapeDtypeStruct((N_DEV,) + b.shape, b.dtype) for b in blocks],
        in_specs=[pl.BlockSpec(memory_space=pl.ANY)] * n,
        out_specs=[pl.BlockSpec(memory_space=pl.ANY)] * n,
        scratch_shapes=[pltpu.SemaphoreType.DMA((7 * n,)), pltpu.SemaphoreType.DMA((7 * n,)), pltpu.SemaphoreType.DMA((n,))],
    )(*blocks)


def _all_to_all(pieces, name):
    n = len(pieces)

    def body(*refs):
        x_refs, out_refs = refs[:n], refs[n:2 * n]
        send_sems, recv_sems, local_sems = refs[2 * n:]
        x, y, c = _my_place()
        me = 4 * x + 2 * y + c
        mine = [pltpu.make_async_copy(x_refs[a].at[me], out_refs[a].at[me], local_sems.at[a]) for a in range(n)]
        for cp in mine:
            cp.start()
        copies = []
        for k in (2, 4, 6, 3, 5, 7, 1):
            px = 1 - x if k & 4 else x
            py = 1 - y if k & 2 else y
            pc = 1 - c if k & 1 else c
            peer = 4 * px + 2 * py + pc
            for a in range(n):
                copies.append(pltpu.make_async_remote_copy(
                    src_ref=x_refs[a].at[peer], dst_ref=out_refs[a].at[me],
                    send_sem=send_sems.at[7 * a + k - 1], recv_sem=recv_sems.at[7 * a + k - 1],
                    device_id=(px, py, pc), device_id_type=MESH))
        for cp in copies:
            cp.start()
        for cp in copies:
            cp.wait_recv()
        for cp in copies:
            cp.wait_send()
        for cp in mine:
            cp.wait()

    return pl.pallas_call(
        body, name=name,
        out_shape=[jax.ShapeDtypeStruct(p.shape, p.dtype) for p in pieces],
        in_specs=[pl.BlockSpec(memory_space=pl.ANY)] * n,
        out_specs=[pl.BlockSpec(memory_space=pl.ANY)] * n,
        scratch_shapes=[pltpu.SemaphoreType.DMA((7 * n,)), pltpu.SemaphoreType.DMA((7 * n,)), pltpu.SemaphoreType.DMA((n,))],
    )(*pieces)


def _peers():
    x, y, c = _my_place()
    out = []
    for k in (2, 4, 6, 3, 5, 7, 1):
        px = 1 - x if k & 4 else x
        py = 1 - y if k & 2 else y
        pc = 1 - c if k & 1 else c
        out.append((k, (px, py, pc), 4 * px + 2 * py + pc))
    return out


def _exchange_copies(x_refs, land_refs, send_sems, recv_sems, scatter):
    x, y, c = _my_place()
    me = 4 * x + 2 * y + c
    starts, arrivals = [], []
    for k, place, peer in _peers():
        for a, (x_ref, land_ref) in enumerate(zip(x_refs, land_refs)):
            sems = dict(send_sem=send_sems.at[7 * a + k - 1], recv_sem=recv_sems.at[7 * a + k - 1],
                        device_id=place, device_id_type=MESH)
            src = x_ref.at[peer] if scatter else x_ref
            starts.append(pltpu.make_async_remote_copy(src_ref=src, dst_ref=land_ref.at[me], **sems))
            arrivals.append(pltpu.make_async_remote_copy(src_ref=src, dst_ref=land_ref.at[peer], **sems))
    return starts, arrivals


def _exchange_start(arrays, scatter, name):
    n = len(arrays)
    hbm = pl.BlockSpec(memory_space=pltpu.HBM)
    sem = pl.BlockSpec(memory_space=pltpu.SEMAPHORE)
    lands = [lax.empty(a.shape if scatter else (N_DEV,) + a.shape, a.dtype) for a in arrays]

    def body(*refs):
        x_refs, land_refs = refs[:n], refs[n:2 * n]
        send_sems, recv_sems = refs[2 * n], refs[2 * n + 1]
        token = refs[-1]
        starts, _ = _exchange_copies(x_refs, land_refs, send_sems, recv_sems, scatter)
        for cp in starts:
            cp.start()
        token[...] = jnp.zeros_like(token)

    res = pl.pallas_call(
        body, name=name,
        out_shape=(pltpu.SemaphoreType.DMA((7 * n,)), pltpu.SemaphoreType.DMA((7 * n,)),
                   *[pltpu.HBM(a.shape, a.dtype) for a in arrays], *[pltpu.HBM(l.shape, l.dtype) for l in lands],
                   jax.ShapeDtypeStruct((8, 128), F32)),
        in_specs=[hbm] * (2 * n),
        out_specs=(sem, sem, *[hbm] * (2 * n), pl.BlockSpec(memory_space=pltpu.VMEM)),
        input_output_aliases={i: 2 + i for i in range(2 * n)},
        compiler_params=pltpu.CompilerParams(has_side_effects=pltpu.SideEffectType.DATAFLOW_SIDE_EFFECTING),
    )(*[pltpu.with_memory_space_constraint(a, pltpu.HBM) for a in arrays],
      *[pltpu.with_memory_space_constraint(l, pltpu.HBM) for l in lands])
    return res[0], res[1], list(res[2:2 + n]), list(res[2 + n:2 + 2 * n]), res[-1]


def _exchange_wait(handles, scatter, after, name):
    send_sems, recv_sems, arrays, lands, _ = handles
    n = len(arrays)
    hbm = pl.BlockSpec(memory_space=pltpu.HBM)
    sem = pl.BlockSpec(memory_space=pltpu.SEMAPHORE)

    def body(*refs):
        x_refs, land_refs = refs[:n], refs[n:2 * n]
        send_s, recv_s = refs[2 * n], refs[2 * n + 1]
        starts, arrivals = _exchange_copies(x_refs, land_refs, send_s, recv_s, scatter)
        for cp in arrivals:
            cp.wait_recv()
        for cp in starts:
            cp.wait_send()

    res = pl.pallas_call(
        body, name=name,
        out_shape=(*[pltpu.HBM(a.shape, a.dtype) for a in arrays], *[pltpu.HBM(l.shape, l.dtype) for l in lands]),
        in_specs=[hbm] * (2 * n) + [sem, sem, pl.BlockSpec(memory_space=pl.ANY)],
        out_specs=tuple([hbm] * (2 * n)),
        input_output_aliases={i: i for i in range(2 * n)},
        compiler_params=pltpu.CompilerParams(has_side_effects=pltpu.SideEffectType.DATAFLOW_SIDE_EFFECTING),
    )(*arrays, *lands, send_sems, recv_sems, after)
    me = 4 * lax.axis_index("x") + 2 * lax.axis_index("y") + lax.axis_index("c")
    out = []
    for src, got in zip(res[:n], res[n:]):
        zeros = (0,) * (got.ndim - 1)
        own = lax.dynamic_slice(src, (me,) + zeros, (1,) + src.shape[1:]) if scatter else src[None]
        out.append(lax.dynamic_update_slice(got, own, (me,) + zeros))
    return out


def _pad_lanes(v, at=0, width=128):
    return jnp.pad(v, ((0, 0), (at, width - at - v.shape[1])))


def _pack_weights(P):
    W = {}
    w = P["w_in"]
    W["wp"] = jnp.concatenate([w[:, :2048], w[:, 2440:2696], w[:, 2056:2440], w[:, 2696:2760], w[:, 2048:2056],
                               jnp.zeros((D_MODEL, N_IN_PACKED - N_IN), w.dtype)], axis=1).astype(BF16)
    W["conv_w"] = P["gdn_conv_w"].astype(F32)
    W["alog_p"] = _pad_lanes(P["gdn_a_log"], 64)
    W["dt_p"] = _pad_lanes(P["gdn_dt_bias"], 64)
    W["gnw"] = P["gdn_norm_w"]
    W["qnw"] = P["mla_q_norm_w"]
    W["kvnw"] = P["mla_kv_norm_w"]
    uq = P["mla_w_uq"].reshape(Q_LORA, HEADS, HEAD_DIM + ROPE)
    W["wuq"] = jnp.pad(uq, ((0, 0), (0, 0), (0, 256 - HEAD_DIM - ROPE))).reshape(Q_LORA, HEADS * 256).astype(BF16)
    ukv = P["mla_w_ukv"].reshape(KV_LORA, HEADS, 2, HEAD_DIM)
    W["wukv"] = ukv.transpose(0, 2, 1, 3).reshape(KV_LORA, 2 * HEADS * HEAD_DIM).astype(BF16)
    W["qn_w"] = P["qkn_q_nope"]
    W["qr_w"] = _pad_lanes(P["qkn_q_rope"])
    W["kn_w"] = P["qkn_k_nope"]
    W["kr_w"] = _pad_lanes(P["qkn_k_rope"])
    W["onw"] = P["mla_out_norm_w"]
    W["wout"] = P["w_out"].astype(BF16)
    return W


def _unpack_grads(G):
    g_qkv, g_z, g_ckv, g_cq, g_kab = G["wp"]
    uq = G["wuq"].reshape(Q_LORA, HEADS, 256)[:, :, :HEAD_DIM + ROPE].reshape(Q_LORA, HEADS * (HEAD_DIM + ROPE))
    ukv = G["wukv"].reshape(KV_LORA, 2, HEADS, HEAD_DIM).transpose(0, 2, 1, 3).reshape(KV_LORA, 2 * HEADS * HEAD_DIM)
    return {
        "w_in": jnp.concatenate([g_qkv, g_z, g_kab[:, ROPE:ROPE + 8], g_cq, g_ckv, g_kab[:, :ROPE]], axis=1),
        "gdn_conv_w": G["conv_w"], "gdn_a_log": G["alog_p"][:, 64:68], "gdn_dt_bias": G["dt_p"][:, 64:68],
        "gdn_norm_w": G["gnw"], "mla_q_norm_w": G["qnw"], "mla_w_uq": uq, "mla_kv_norm_w": G["kvnw"], "mla_w_ukv": ukv,
        "qkn_q_nope": G["qn_w"], "qkn_q_rope": G["qr_w"][:, :ROPE], "qkn_k_nope": G["kn_w"], "qkn_k_rope": G["kr_w"][:, :ROPE],
        "mla_out_norm_w": G["onw"], "w_out": G["wout"],
    }


def _rope_tables(positions):
    half = ROPE // 2
    inv_freq = ROPE_BASE ** (-jnp.arange(half, dtype=F32) / half)
    ang = positions.astype(F32)[:, None] * inv_freq
    cos, sin = jnp.cos(ang), jnp.sin(ang)
    zeros = jnp.zeros((positions.shape[0], 128 - ROPE), F32)
    return jnp.concatenate([cos, cos, zeros], axis=1), jnp.concatenate([-sin, sin, zeros], axis=1)


def _ffn_forward(x, scale, shift, gate_w, w8, wo4, name, target=None):
    S = x.shape[0]
    tm = _pick(S, (512, 256, 128))
    n = S // tm
    with_loss = target is not None

    def body(x_ref, sc_ref, sh_ref, g_ref, wg_ref, wu_ref, wo_ref, *rest):
        t_ref = rest[0] if with_loss else None
        h_ref, bg_ref, bu_ref, ht_ref = rest[with_loss:with_loss + 4]
        tail = rest[with_loss + 4:]
        h_scr, acc_ref = tail[-2:]
        i, p = pl.program_id(0), pl.program_id(1)

        @pl.when(p == 0)
        def _():
            h_new = _modulate(x_ref[...], sc_ref[...], sh_ref[...]).astype(BF16)
            h_scr[...] = h_new
            h_ref[...] = h_new
            acc_ref[...] = jnp.zeros_like(acc_ref)

        h = h_scr[...]
        gate = _dot_raw(h, wg_ref[...], "nn")
        up = _dot_raw(h, wu_ref[...], "nn")
        sg = _sigmoid(gate)
        act = gate * sg
        hid = act * up
        bg_ref[...] = (up * (sg * (1.0 + gate * (1.0 - sg)))).astype(BF16)
        bu_ref[...] = act.astype(BF16)
        ht_ref[...] = jnp.transpose(hid).astype(BF16)
        acc_ref[...] += _dot_raw(hid, wo_ref[...], "nn")

        if with_loss:
            dx_ref, df_ref, dg_ref, l_ref = tail[:4]

            @pl.when((p == 0) & (i == 0))
            def _():
                dg_ref[...] = jnp.zeros_like(dg_ref)
                l_ref[...] = jnp.zeros_like(l_ref)

            @pl.when(p == HID_PIECES - 1)
            def _():
                step = min(EPILOGUE_ROWS, tm)
                for r in range(tm // step):
                    rows = slice(step * r, step * (r + 1))
                    f = acc_ref[rows, :]
                    diff = x_ref[rows, :] + 0.5 * g_ref[...] * f - t_ref[rows, :]
                    dx = diff * (1.0 / D_MODEL)
                    dx_ref[rows, :] = dx
                    df_ref[rows, :] = (0.5 * g_ref[...] * dx).astype(df_ref.dtype)
                    dg_ref[...] += jnp.sum(0.5 * f * dx, axis=0, keepdims=True)
                    l_ref[...] += jnp.sum(diff * diff, axis=0, keepdims=True)

            @pl.when((p == HID_PIECES - 1) & (i == n - 1))
            def _():
                l_ref[...] = jnp.full(l_ref.shape, (0.5 / D_MODEL) * jnp.sum(l_ref[...]), F32)
        else:
            f_ref, xo_ref = tail[:2]

            @pl.when(p == HID_PIECES - 1)
            def _():
                f = acc_ref[...]
                f_ref[...] = f.astype(f_ref.dtype)
                xo_ref[...] = x_ref[...] + 0.5 * g_ref[...] * f

    row = pl.BlockSpec((tm, D_MODEL), lambda i, p: (i, 0))
    par = pl.BlockSpec((1, D_MODEL), lambda i, p: (0, 0))
    piece = pl.BlockSpec((None, tm, FFN_PIECE), lambda i, p: (p, i, 0))
    row_f32, row_bf16 = jax.ShapeDtypeStruct((S, D_MODEL), F32), jax.ShapeDtypeStruct((S, D_MODEL), BF16)
    par_f32 = jax.ShapeDtypeStruct((1, D_MODEL), F32)
    piece_shape = jax.ShapeDtypeStruct((HID_PIECES, S, FFN_PIECE), BF16)
    return pl.pallas_call(
        body, name=name, grid=(n, HID_PIECES),
        in_specs=[row, par, par, par,
                  pl.BlockSpec((None, D_MODEL, FFN_PIECE), lambda i, p: (p, 0, 0)),
                  pl.BlockSpec((None, D_MODEL, FFN_PIECE), lambda i, p: (p + HID_PIECES, 0, 0)),
                  pl.BlockSpec((None, FFN_PIECE, D_MODEL), lambda i, p: (p, 0, 0))] + [row] * with_loss,
        out_specs=[row, piece, piece, pl.BlockSpec((None, FFN_PIECE, tm), lambda i, p: (p, 0, i))]
        + ([row, row, par, par] if with_loss else [row, row]),
        out_shape=[row_bf16, piece_shape, piece_shape, jax.ShapeDtypeStruct((HID_PIECES, FFN_PIECE, S), BF16)]
        + ([row_f32, row_bf16, par_f32, par_f32] if with_loss else [row_bf16, row_f32]),
        scratch_shapes=[pltpu.VMEM((tm, D_MODEL), BF16), pltpu.VMEM((tm, D_MODEL), F32)],
        compiler_params=_params(("arbitrary", "arbitrary")),
    )(x, scale, shift, gate_w, w8, w8, wo4, *([target] if with_loss else []))


def _ffn_fwd(x, scale, shift, gate_w, w8, wo4, tag, target=None):
    res = _ffn_forward(x, scale, shift, gate_w, w8, wo4, tag + "_fwd", target)
    h, by_gate, by_up, hid_t = res[:4]
    if target is not None:
        dx_out, df, d_gate_w, loss_row = res[4:]
        return (dx_out, loss_row), (h, by_gate, by_up, hid_t, None, df, d_gate_w)
    f, x_out = res[4:]
    return x_out, (h, by_gate, by_up, hid_t, f, None, None)


def _ffn_bwd(d_out, x, scale, shift, gate_w, w8, wo4, saved, tag, grad_ready, below=None):
    h, gate, up, hid_t, f, df, d_gate_w = saved
    S = x.shape[0]
    tm = _pick(S, (512, 256, 128))
    tk = _pick(S, (512, 256, 128))
    n = S // tm
    if df is None:
        (df,), (d_gate_w,) = _rowwise_bwd(lambda f_, g_: (0.5 * g_ * f_,), [(f, tm, D_MODEL, 0)], [], [gate_w],
                                          [(d_out, tm, D_MODEL, 0)], n, tag + "_dres", row_dtypes=(BF16,))
    tb = _pick(S, MATMUL_ROWS)
    piece = pl.BlockSpec((None, tb, FFN_PIECE), lambda i, j, k: (j, i, 0))
    d_gate, d_up = _mmg(df, wo4, "nt", name=tag + "_ddown", grid=(S // tb, HID_PIECES, 1),
                        a_spec=pl.BlockSpec((tb, D_MODEL), lambda i, j, k: (i, 0)),
                        b_spec=pl.BlockSpec((None, FFN_PIECE, D_MODEL), lambda i, j, k: (j, 0, 0)),
                        out_spec=piece, out_shapes=[jax.ShapeDtypeStruct((HID_PIECES, S, FFN_PIECE), BF16)] * 2,
                        acc_shape=(tb, FFN_PIECE), extras=[gate, up], extra_specs=[piece, piece], epi=_swiglu_bwd)
    tk = _pick(S, MATMUL_ROWS)
    g_wo4 = _mmg(hid_t, df, "nn", name=tag + "_gwo", grid=(HID_PIECES, 1, S // tk),
                 a_spec=pl.BlockSpec((None, FFN_PIECE, tk), lambda i, j, k: (i, 0, k)),
                 b_spec=pl.BlockSpec((tk, D_MODEL), lambda i, j, k: (k, j)),
                 out_spec=pl.BlockSpec((None, FFN_PIECE, D_MODEL), lambda i, j, k: (i, 0, j)),
                 out_shapes=[jax.ShapeDtypeStruct((HID_PIECES, FFN_PIECE, D_MODEL), BF16)], acc_shape=(FFN_PIECE, D_MODEL))
    g_w8 = _ffn_gw8(h, d_gate, d_up, tag + "_gw8", after=grad_ready("wo4", g_wo4))
    res = _ffn_dh(d_gate, d_up, w8, x, d_out, scale, shift, tag + "_dh", after=grad_ready("w8", g_w8), below=below)
    return (res[0], res[1], res[2], d_gate_w) + tuple(res[3:])


def _dproj_dmod(d_pieces, wp, x, d_out, scale, shift, name, below=None):
    S = x.shape[0]
    tm = _pick(S, TOKEN_ROWS)
    widths = [p.shape[1] for p in d_pieces]
    starts = [sum(widths[:n]) for n in range(len(widths))]
    n_p = len(d_pieces)
    n_below = 0 if below is None else 2

    def body(*refs):
        dp_refs, (w_ref, x_ref, do_ref, sc_ref, sh_ref), rest = refs[:n_p], refs[n_p:n_p + 5], refs[n_p + 5:]
        below_in = rest[:n_below]
        dx_ref, dsc_ref, dsh_ref = rest[n_below:n_below + 3]
        below_out, acc_ref = rest[n_below + 3:n_below + 3 + n_below], rest[-1]

        @pl.when(pl.program_id(0) == 0)
        def _():
            for r in (dsc_ref, dsh_ref) + tuple(below_out[1:]):
                r[...] = jnp.zeros_like(r)

        acc = None
        for dp_ref, at, width in zip(dp_refs, starts, widths):
            part = _dot_raw(dp_ref[...], w_ref[:, at:at + width], "nt")
            acc = part if acc is None else acc + part
        acc_ref[...] = acc
        _dmod_epilogue(acc_ref, x_ref, do_ref, sc_ref, sh_ref, dx_ref, dsc_ref, dsh_ref, below, below_in, below_out)

    row = pl.BlockSpec((tm, D_MODEL), lambda i: (i, 0))
    par = pl.BlockSpec((1, D_MODEL), lambda i: (0, 0))
    row_shape, par_shape = jax.ShapeDtypeStruct((S, D_MODEL), F32), jax.ShapeDtypeStruct((1, D_MODEL), F32)
    return pl.pallas_call(
        body, name=name, grid=(S // tm,),
        in_specs=[pl.BlockSpec((tm, width), lambda i: (i, 0)) for width in widths]
        + [pl.BlockSpec(wp.shape, lambda i: (0, 0)), row, row, par, par] + [row, par][:n_below],
        out_specs=[row, par, par] + [row, par][:n_below],
        out_shape=[row_shape, par_shape, par_shape] + [jax.ShapeDtypeStruct((S, D_MODEL), BF16), par_shape][:n_below],
        scratch_shapes=[pltpu.VMEM((tm, D_MODEL), F32)],
        compiler_params=_params(("arbitrary",)),
    )(*d_pieces, wp, x, d_out, scale, shift, *(below[:2] if below is not None else ()))


def _gw_pieces(h, d_pieces, name):
    S = h.shape[0]
    tm = 512
    tk = _pick(S, MATMUL_ROWS)
    n_p = len(d_pieces)
    widths = [p.shape[1] for p in d_pieces]

    def body(h_ref, *rest):
        d_refs, o_refs = rest[:n_p], rest[n_p:]

        @pl.when(pl.program_id(1) == 0)
        def _():
            for o_ref in o_refs:
                o_ref[...] = jnp.zeros_like(o_ref)

        h_t = jnp.transpose(h_ref[...])
        for d_ref, o_ref in zip(d_refs, o_refs):
            o_ref[...] += _dot_raw(h_t, d_ref[...], "nn")

    return pl.pallas_call(
        body, name=name, grid=(D_MODEL // tm, S // tk),
        in_specs=[pl.BlockSpec((tk, tm), lambda i, k: (k, i))] + [pl.BlockSpec((tk, width), lambda i, k: (k, 0)) for width in widths],
        out_specs=[pl.BlockSpec((tm, width), lambda i, k: (i, 0)) for width in widths],
        out_shape=[jax.ShapeDtypeStruct((D_MODEL, width), F32) for width in widths],
        compiler_params=_params(("parallel", "arbitrary")),
    )(h, *d_pieces)


def _mod_proj(x, scale, shift, wp, name):
    S = x.shape[0]
    N = wp.shape[1]
    tm = _pick(S, MATMUL_ROWS)
    tn = _pick(N, (1408, 1024, 512, 256, 128))

    def body(x_ref, sc_ref, sh_ref, w_ref, h_ref, o_ref, h_scr):
        @pl.when(pl.program_id(1) == 0)
        def _():
            h_new = _modulate(x_ref[...], sc_ref[...], sh_ref[...]).astype(BF16)
            h_scr[...] = h_new
            h_ref[...] = h_new

        o_ref[...] = _dot_raw(h_scr[...], w_ref[...], "nn")

    row = pl.BlockSpec((tm, D_MODEL), lambda i, j: (i, 0))
    par = pl.BlockSpec((1, D_MODEL), lambda i, j: (0, 0))
    return pl.pallas_call(
        body, name=name, grid=(S // tm, N // tn),
        in_specs=[row, par, par, pl.BlockSpec((D_MODEL, tn), lambda i, j: (0, j))],
        out_specs=[row, pl.BlockSpec((tm, tn), lambda i, j: (i, j))],
        out_shape=[jax.ShapeDtypeStruct((S, D_MODEL), BF16), jax.ShapeDtypeStruct((S, N), F32)],
        scratch_shapes=[pltpu.VMEM((tm, D_MODEL), BF16)],
        compiler_params=_params(("parallel", "arbitrary")),
    )(x, scale, shift, wp)


def _mixer_fwd(x1, scale, shift, gate_w, cos_p, sin_p, W):
    S = x1.shape[0]
    tm = _pick(S, (512, 256, 128))
    tv = _pick(S, TOKEN_ROWS)
    ta = _pick(S, (512, 256, 128))
    nc = S // CHUNK
    h2, proj = _mod_proj(x1, scale, shift, W["wp"], "mix_proj")
    qkvc = _conv_fwd(proj, W["conv_w"], tv, "gdn_conv")
    kab = (proj, tv, 128, 21)
    q_a, k_a, v_a, gb = _rowwise(_gdn_pre_fn, [(qkvc, tv, 1536, 0), kab], [W["alog_p"], W["dt_p"]],
                                 [(tv, 512, F32)] * 3 + [(tv, 128, F32)], S // tv, "gdn_pre")
    ti = _pick(S, INTRA_ROWS)
    intra = _rowwise(_gdn_intra_fn, [(q_a, ti, 512, 0), (k_a, ti, 512, 0), (v_a, ti, 512, 0), (gb, ti, 128, 0)],
                     [], [(ti, 512, F32)] * 4 + [(ti, CHUNK, F32)] * 4 + [(ti // 8, 512, F32)] + [(ti, CHUNK, F32)] * 4,
                     S // ti, "gdn_intra")
    u, wk, qd, kd, qks, gl, invs = intra[0], intra[1], intra[2], intra[3], tuple(intra[4:8]), intra[8], tuple(intra[9:])
    o_a, s_prev = _gdn_scan_fwd(u, wk, qd, kd, qks, gl, "gdn_scan")
    mla_params = [W["qnw"], W["kvnw"], W["wuq"], W["wukv"], W["qn_w"], W["qr_w"], W["kn_w"], W["kr_w"]]
    def mla_pre_with_vt(*a):
        q_, k_, v_ = _mla_pre_fn(*a)
        return q_, k_, v_, jnp.transpose(v_)

    q_b, k_b, v_b, vt_b = _rowwise(mla_pre_with_vt,
                                   [(proj, tv, 256, 8), (proj, tv, 384, 6), kab, (cos_p, tv, 128, 0), (sin_p, tv, 128, 0)],
                                   mla_params, [(tv, 1024, BF16), (tv, 1024, BF16), (tv, 512, BF16), (512, tv, BF16, "across")],
                                   S // tv, "mla_pre")
    o_b, lse = _attn_fwd(q_b, k_b, vt_b, ta, "mla_attn")
    (mixed,) = _rowwise(_mix_post_fn, [(o_a, tv, 512, 0), (proj, tv, 512, 3), (o_b, tv, 512, 0)], [W["gnw"], W["onw"]],
                        [(tv, D_MODEL, BF16)], S // tv, "mix_post")
    y, x2 = _mm(mixed, W["wout"], "nn", name="mix_out", out_dtypes=(BF16, F32), extras=[x1], extra_params=[gate_w],
                epi=lambda acc, x_, g_: (acc, x_ + g_ * acc))
    saved = (h2, proj, qkvc, q_a, k_a, v_a, gb, u, wk, qd, kd, qks, gl, invs, s_prev, o_a, q_b, k_b, v_b, o_b, lse, mixed, y)
    return x2, saved


def _mixer_bwd(d_out, dy, x1, scale, shift, cos_p, sin_p, W, saved, below):
    (h2, proj, qkvc, q_a, k_a, v_a, gb, u, wk, qd, kd, qks, gl, invs, s_prev, o_a, q_b, k_b, v_b, o_b, lse, mixed, y) = saved
    S = x1.shape[0]
    tm = _pick(S, (512, 256, 128))
    tv = _pick(S, TOKEN_ROWS)
    ta = _pick(S, (512, 256, 128))
    nc = S // CHUNK
    G = {}
    d_mixed = _mm(dy, W["wout"], "nt", name="mix_dout")
    G["wout"] = _mm(mixed, dy, "tn", name="mix_gwout")
    (do_a, dz, do_b), (G["gnw"], G["onw"]) = _rowwise_bwd(
        _mix_post_fn, [(o_a, tv, 512, 0), (proj, tv, 512, 3), (o_b, tv, 512, 0)], [], [W["gnw"], W["onw"]],
        [(d_mixed, tv, D_MODEL, 0)], S // tv, "mix_dpost", row_dtypes=(F32, BF16, F32))
    stats = _attn_stats(o_b, lse, do_b, ta, "mla_stats")
    dq_b, dk_b, dv_b = _attn_bwd(q_b, k_b, v_b, do_b, stats, ta, "mla_dattn")
    kab = (proj, tv, 128, 21)
    mla_params = [W["qnw"], W["kvnw"], W["wuq"], W["wukv"], W["qn_w"], W["qr_w"], W["kn_w"], W["kr_w"]]
    (d_ckv, d_cq, d_kab), mla_grads = _rowwise_bwd(
        _mla_pre_fn, [(proj, tv, 256, 8), (proj, tv, 384, 6), kab], [(cos_p, tv, 128, 0), (sin_p, tv, 128, 0)], mla_params,
        [(dq_b, tv, 1024, 0), (dk_b, tv, 1024, 0), (dv_b, tv, 512, 0)], S // tv, "mla_dpre", row_dtypes=(BF16, BF16, F32))
    for key, g in zip(("qnw", "kvnw", "wuq", "wukv", "qn_w", "qr_w", "kn_w", "kr_w"), mla_grads):
        G[key] = g
    scan_grads = _gdn_scan_bwd(u, wk, qd, kd, qks, gl, s_prev, do_a, "gdn_dscan")
    ti = _pick(S, INTRA_ROWS)
    intra_douts = [(scan_grads[i], ti, 512, 0) for i in range(4)] + [(scan_grads[4 + i], ti, CHUNK, 0) for i in range(4)]
    intra_douts.append((scan_grads[8], ti // 8, 512, 0))
    (dq_a, dk_a, dv_a, d_gb), _ = _rowwise_bwd(
        _gdn_intra_fn, [(q_a, ti, 512, 0), (k_a, ti, 512, 0), (v_a, ti, 512, 0), (gb, ti, 128, 0)],
        [(x_, ti, CHUNK, 0) for x_ in invs], [], intra_douts, S // ti, "gdn_dintra")
    (d_qkvc, d_kab), (G["alog_p"], G["dt_p"]) = _rowwise_bwd(
        _gdn_pre_fn, [(qkvc, tv, 1536, 0), kab], [], [W["alog_p"], W["dt_p"]],
        [(dq_a, tv, 512, 0), (dk_a, tv, 512, 0), (dv_a, tv, 512, 0), (d_gb, tv, 128, 0)], S // tv, "gdn_dpre",
        adds=[(1, d_kab)], row_dtypes=(F32, BF16))
    d_qkv, g_conv = _conv_bwd(proj, d_qkvc, W["conv_w"], tv, "gdn_dconv")
    G["conv_w"] = g_conv[:4]
    d_proj = [d_qkv, dz, d_ckv, d_cq, d_kab]
    G["wp"] = _gw_pieces(h2, d_proj, "mix_gwp")
    dx1, G["s2"], G["sh2"], d_below, dg_below = _dproj_dmod(d_proj, W["wp"], x1, d_out, scale, shift, "mix_dproj", below=below)
    return dx1, d_below, dg_below, G


def _local_step(x, target, mod, cos_p, sin_p, W1, mixer_weights, ffn2_weights, ffn_grad_ready, mixer_grads_ready):
    sh1, s1, g1, sh2, s2, g2, sh3, s3, g3 = [mod[:, D_MODEL * i:D_MODEL * (i + 1)] for i in range(N_MOD)]
    x1, saved1 = _ffn_fwd(x, s1, sh1, g1, W1["f1_w8"], W1["f1_wo4"], "ffn1")
    W = mixer_weights(x1)
    x2, saved2 = _mixer_fwd(x1, s2, sh2, g2, cos_p, sin_p, W)
    W.update(ffn2_weights(x2))
    (dx3, loss_row), saved3 = _ffn_fwd(x2, s3, sh3, g3, W["f2_w8"], W["f2_wo4"], "ffn2", target=target)
    dx2, d_s3, d_sh3, d_g3, dy, d_g2 = _ffn_bwd(dx3, x2, s3, sh3, g3, W["f2_w8"], W["f2_wo4"], saved3, "ffn2",
                                                ffn_grad_ready("f2"), below=(saved2[-1], g2, 1.0))
    dx1, df1, d_g1, G = _mixer_bwd(dx2, dy, x1, s2, sh2, cos_p, sin_p, W, saved2, below=(saved1[4], g1, 0.5))
    d_sh2, d_s2 = G.pop("sh2"), G.pop("s2")
    saved1 = saved1[:5] + (df1, d_g1 + mixer_grads_ready(G))
    dx, d_s1, d_sh1, d_g1 = _ffn_bwd(dx1, x, s1, sh1, g1, W1["f1_w8"], W1["f1_wo4"], saved1, "ffn1", ffn_grad_ready("f1"))
    d_mod = jnp.concatenate([d_sh1, d_s1, d_g1, d_sh2, d_s2, d_g2, d_sh3, d_s3, d_g3], axis=1)
    return loss_row, dx, d_mod


WEIGHT_NAMES = ("w_ada", "b_ada", "ffn1_w_in", "ffn1_w_out", "w_in", "gdn_conv_w", "gdn_a_log", "gdn_dt_bias", "gdn_norm_w",
                "mla_q_norm_w", "mla_w_uq", "mla_kv_norm_w", "mla_w_ukv", "qkn_q_nope", "qkn_q_rope", "qkn_k_nope",
                "qkn_k_rope", "mla_out_norm_w", "w_out", "ffn2_w_in", "ffn2_w_out")
FFN_SHARDED = ("ffn1_w_in", "ffn1_w_out", "ffn2_w_in", "ffn2_w_out")
TRANSPOSED_ENTRY = ("ffn1_w_in", "ffn2_w_in", "w_in", "mla_w_uq")
SHEETED = (("w_in", "col"), ("gdn_conv_w", "col"), ("mla_w_uq", "col"), ("mla_w_ukv", "col"), ("w_out", "row"))
MOD_ROWS = N_MOD * D_MODEL // 128
SMALL = {"gdn_a_log": (MOD_ROWS, 1, 64, 4), "gdn_dt_bias": (MOD_ROWS + 1, 1, 64, 4), "gdn_norm_w": (MOD_ROWS + 2, 1, 0, 128),
         "mla_q_norm_w": (MOD_ROWS + 3, 3, 0, 384), "mla_kv_norm_w": (MOD_ROWS + 6, 2, 0, 256),
         "qkn_q_nope": (MOD_ROWS + 8, 1, 0, 128), "qkn_q_rope": (MOD_ROWS + 9, 1, 0, 64), "qkn_k_nope": (MOD_ROWS + 10, 1, 0, 128),
         "qkn_k_rope": (MOD_ROWS + 11, 1, 0, 64), "mla_out_norm_w": (MOD_ROWS + 12, 1, 0, 128)}
LOSS_ROW = MOD_ROWS + 13
CONV_ROW, CONV_ROWS = 88, 4 * 1536 // 128
SHEET_ROWS = CONV_ROW + CONV_ROWS


def _to_sheet(flat, dtype, sublanes):
    n = flat.shape[-1]
    unit = sublanes * 128
    pad = (-n) % unit
    flat = jnp.pad(flat.astype(dtype), [(0, 0)] * (flat.ndim - 1) + [(0, pad)])
    return flat.reshape(flat.shape[:-1] + ((n + pad) // 128, 128))


def _small_sheet(b_like, small):
    sheet = jnp.zeros((SHEET_ROWS, 128), F32).at[:MOD_ROWS].set(b_like.reshape(MOD_ROWS, 128))
    for name, (row, rows, lane, n) in SMALL.items():
        v = small[name].reshape(1, n)
        if rows == 1:
            sheet = sheet.at[row, lane:lane + n].set(v[0])
        else:
            sheet = sheet.at[row:row + rows].set(v.reshape(rows, 128))
    return sheet


def _from_small_sheet(sheet):
    out = {"b_ada": sheet[:MOD_ROWS].reshape(1, N_MOD * D_MODEL)}
    for name, (row, rows, lane, n) in SMALL.items():
        out[name] = sheet[row, lane:lane + n].reshape(1, n) if rows == 1 else sheet[row:row + rows].reshape(1, n)
    return out


def kernel(x, c, positions, w_ada, b_ada, ffn1_w_in, ffn1_w_out, w_in, gdn_conv_w, gdn_a_log, gdn_dt_bias, gdn_norm_w, mla_q_norm_w, mla_w_uq, mla_kv_norm_w, mla_w_ukv, qkn_q_nope, qkn_q_rope, qkn_k_nope, qkn_k_rope, mla_out_norm_w, w_out, ffn2_w_in, ffn2_w_out, loss_target, m_w_ada, m_b_ada, m_ffn1_w_in, m_ffn1_w_out, m_w_in, m_gdn_conv_w, m_gdn_a_log, m_gdn_dt_bias, m_gdn_norm_w, m_mla_q_norm_w, m_mla_w_uq, m_mla_kv_norm_w, m_mla_w_ukv, m_qkn_q_nope, m_qkn_q_rope, m_qkn_k_nope, m_qkn_k_rope, m_mla_out_norm_w, m_w_out, m_ffn2_w_in, m_ffn2_w_out, v_w_ada, v_b_ada, v_ffn1_w_in, v_ffn1_w_out, v_w_in, v_gdn_conv_w, v_gdn_a_log, v_gdn_dt_bias, v_gdn_norm_w, v_mla_q_norm_w, v_mla_w_uq, v_mla_kv_norm_w, v_mla_w_ukv, v_qkn_q_nope, v_qkn_q_rope, v_qkn_k_nope, v_qkn_k_rope, v_mla_out_norm_w, v_w_out, v_ffn2_w_in, v_ffn2_w_out):
    args = locals()
    w = {n: args[n] for n in WEIGHT_NAMES}
    m = {n: args["m_" + n] for n in WEIGHT_NAMES}
    v = {n: args["v_" + n] for n in WEIGHT_NAMES}
    me = 4 * lax.axis_index("x") + 2 * lax.axis_index("y") + lax.axis_index("c")
    cols = N_MOD * D_MODEL // N_DEV
    shard = {n: w[n][0] for n in FFN_SHARDED + tuple(s[0] for s in SHEETED)}

    sc = c * _sigmoid(c)
    first = _to_sheet(jnp.concatenate([sc.reshape(-1), shard["gdn_conv_w"].reshape(-1)]), F32, 8)
    (first_all,) = _all_gather([first], "gather_c")
    sc_all = first_all[:, :D_MODEL // 128].reshape(N_DEV, D_MODEL)
    n_taps = shard["gdn_conv_w"].size
    conv_all = first_all.reshape(N_DEV, -1)[:, D_MODEL:D_MODEL + n_taps].reshape(N_DEV, 4, -1)
    b_mine = lax.dynamic_slice(b_ada, (0, me * cols), (1, cols))
    mod_cols = _mm(sc_all, w_ada[0], "nn", name="ada_mod", extra_params=[b_mine], epi=lambda acc, b_: (acc + b_,))
    (mod_all,) = _all_to_all([_to_sheet(mod_cols, F32, 8)], "scatter_mod")
    mod = mod_all.reshape(N_DEV, -1)[:, :cols].reshape(1, N_MOD * D_MODEL)

    f1_shards, mod = lax.optimization_barrier(([shard["ffn1_w_in"].astype(BF16), shard["ffn1_w_out"].astype(BF16)], mod))
    f1_w8, f1_out = _all_gather(f1_shards, "gather_w1")
    travel = [s for s in SHEETED if s[0] != "gdn_conv_w"]
    tied = lax.optimization_barrier(([shard[n].astype(BF16) for n, _ in travel], f1_w8))
    f1_w8 = tied[1]
    mixer_w = _exchange_start(tied[0], False, "gather_wm_start")
    ffn2_w = _exchange_start([shard["ffn2_w_in"].astype(BF16) + mixer_w[4][0:1, 0:1].astype(BF16),
                              shard["ffn2_w_out"].astype(BF16)], False, "gather_w2_start")
    mod = mod + ffn2_w[4][0:1, 0:1]
    W1 = dict(f1_w8=f1_w8, f1_wo4=f1_out.reshape(HID_PIECES, FFN_PIECE, D_MODEL))

    def mixer_weights(after):
        got = _exchange_wait(mixer_w, False, after, "gather_wm_wait")
        P = {n: jnp.concatenate(list(g), axis=1) if kind == "col" else g.reshape(-1, g.shape[-1])
             for (n, kind), g in zip(travel, got)}
        P["gdn_conv_w"] = jnp.concatenate(list(conv_all), axis=1)
        for n in SMALL:
            P[n] = w[n]
        return _pack_weights(P)

    def ffn2_weights(after):
        f2_w8, f2_out = _exchange_wait(ffn2_w, False, after, "gather_w2_wait")
        return dict(f2_w8=f2_w8, f2_wo4=f2_out.reshape(HID_PIECES, FFN_PIECE, D_MODEL))

    pending, small_grads = {}, {}

    def ffn_grad_ready(tag):
        def ready(which, g):
            pieces = g if which == "w8" else g.reshape((N_DEV,) + shard["ffn1_w_out"].shape)
            pending[tag + which] = _exchange_start([pieces], True, "scatter_%s_%s_start" % (tag, which))
            return pending[tag + which][4]
        return ready

    def mixer_grads_ready(G):
        g_full = _unpack_grads(G)
        small_grads.update({n: g_full[n] for n in SMALL})
        small_grads["gdn_conv_w"] = g_full["gdn_conv_w"]
        pieces = []
        for n, kind in travel:
            r, cc = shard[n].shape
            g = g_full[n].astype(BF16)
            pieces.append(jnp.stack([g[:, cc * p:cc * (p + 1)] for p in range(N_DEV)]) if kind == "col"
                          else g.reshape(N_DEV, r, cc))
        pending["mixer"] = _exchange_start(pieces, True, "scatter_mx_start")
        return pending["mixer"][4][0:1, 0:1]

    cos_p, sin_p = _rope_tables(positions[0])
    loss_row, dx, d_mod = _local_step(x[0], loss_target[0], mod, cos_p, sin_p, W1, mixer_weights, ffn2_weights,
                                      ffn_grad_ready, mixer_grads_ready)

    sheet = _small_sheet(d_mod, small_grads).at[LOSS_ROW].set(loss_row[0, :128])
    sheet = sheet.at[CONV_ROW:CONV_ROW + CONV_ROWS].set(small_grads["gdn_conv_w"].reshape(CONV_ROWS, 128))
    (sheets,) = _all_gather([sheet], "gather_small")
    summed = _sum_devices(sheets, "sum_small")
    d_mod_all = sheets[:, :MOD_ROWS].reshape(N_DEV, N_MOD * D_MODEL)
    d_mod_mine = lax.dynamic_slice(d_mod_all, (0, me * cols), (N_DEV, cols))
    grads = _from_small_sheet(summed)
    grads["w_ada"] = _mm(sc_all, d_mod_mine, "tn", name="ada_gw", hi=True)
    conv_taps = shard["gdn_conv_w"].shape[1]
    grads["gdn_conv_w"] = lax.dynamic_slice(summed[CONV_ROW:CONV_ROW + CONV_ROWS].reshape(4, -1), (0, me * conv_taps),
                                            (4, conv_taps))
    loss = summed[LOSS_ROW, 0]

    delta, new_m, new_v = {}, {}, {}
    arrived = {}
    for n, key in zip(FFN_SHARDED, ("f1w8", "f1wo4", "f2w8", "f2wo4")):
        (arrived[n],) = _exchange_wait(pending[key], True, summed, "scatter_%s_wait" % key)
    arrived.update(zip([n for n, _ in travel], _exchange_wait(pending["mixer"], True, summed, "scatter_mx_wait")))
    for n, parts in arrived.items():
        if n in TRANSPOSED_ENTRY:
            res = _sum_adamw(parts, w[n][0].T, m[n][0].T, v[n][0].T, "adamw_" + n, transposed=True)
            grads[n], delta[n], new_m[n], new_v[n] = [r.T for r in res]
        else:
            grads[n], delta[n], new_m[n], new_v[n] = _sum_adamw(parts, w[n][0], m[n][0], v[n][0], "adamw_" + n)
    for n in ("w_ada", "gdn_conv_w"):
        delta[n], new_m[n], new_v[n] = _adamw(w[n][0], grads[n], m[n][0], v[n][0], "adamw_" + n)
    small_in = [_small_sheet(t["b_ada"], t) for t in (w, grads, m, v)]
    for res, out in zip(_adamw(*small_in, "adamw_small"), (delta, new_m, new_v)):
        out.update(_from_small_sheet(res))

    def shaped(d):
        return [d[n].reshape(w[n].shape) for n in WEIGHT_NAMES]

    return (loss, dx[None], *shaped(grads), *shaped(delta), *shaped(new_m), *shaped(new_v))
```

```python
import functools

import jax
import jax.numpy as jnp
import numpy as np
from jax import lax
from jax.experimental import pallas as pl
from jax.experimental.pallas import tpu as pltpu

F32 = jnp.float32
BF16 = jnp.bfloat16

D_MODEL = 1024
D_FF = 2816
N_MOD = 9
HEADS = 4
HEAD_DIM = 128
CHUNK = 64
EPS = 1e-6
ROPE = 64
Q_LORA = 384
KV_LORA = 256
N_IN = 2760
N_IN_PACKED = 2816
ROPE_BASE = 10000.0
LOG2_E = 1.4426950408889634
N_DEV = 8

ADAM_LR = 0.001
ADAM_B1 = 0.9
ADAM_B2 = 0.999
ADAM_EPS = 1e-08
ADAM_WD = 0.01
ADAM_STEP = 10

VMEM_LIMIT_BYTES = 56 * 1024 * 1024
MATMUL_ROWS = (1024, 512, 256, 128)
MESH = pl.DeviceIdType.MESH


def _params(sem=None):
    return pltpu.CompilerParams(dimension_semantics=sem, vmem_limit_bytes=VMEM_LIMIT_BYTES)


def _pick(dim, prefs):
    for p in prefs:
        if dim % p == 0:
            return p
    return dim


_DIMS = {"nn": (((1,), (0,)), ((), ())), "nt": (((1,), (1,)), ((), ())), "tn": (((0,), (0,)), ((), ()))}


def _dot_raw(a, b, mode):
    return lax.dot_general(a.astype(BF16), b.astype(BF16), _DIMS[mode], preferred_element_type=F32)


def _dot_hi(a, b, mode="nn"):
    return lax.dot_general(a, b, _DIMS[mode], precision=lax.Precision.HIGHEST, preferred_element_type=F32)


@functools.partial(jax.custom_vjp, nondiff_argnums=(2,))
def _bdot(a, b, mode):
    return _dot_raw(a, b, mode)


def _bdot_fwd(a, b, mode):
    return _dot_raw(a, b, mode), (a, b)


def _bdot_bwd(mode, res, g):
    a, b = res
    if mode == "nn":
        return _dot_raw(g, b, "nt"), _dot_raw(a, g, "tn")
    if mode == "nt":
        return _dot_raw(g, b, "nn"), _dot_raw(g, a, "tn")
    return _dot_raw(b, g, "nt"), _dot_raw(a, g, "nn")


_bdot.defvjp(_bdot_fwd, _bdot_bwd)


def _mm(a, b, mode, *, name, out_dtypes=(F32,), epi=None, extras=(), extra_params=(), hi=False,
        tm=None, tn=None, tk=None):
    if mode == "nn":
        (M, K), (_, N) = a.shape, b.shape
    elif mode == "nt":
        (M, K), (N, _) = a.shape, b.shape
    else:
        (K, M), (_, N) = a.shape, b.shape
    tm = tm or _pick(M, (512, 1408, 256, 128) if mode == "tn" else MATMUL_ROWS + (384, 352))
    tn = tn or _pick(N, (1024, 1408, 768, 512, 384, 256, 128))
    tk = tk or _pick(K, (1024, 1408, 512, 384, 256, 128))
    a_spec = {"nn": pl.BlockSpec((tm, tk), lambda i, j, k: (i, k)), "nt": pl.BlockSpec((tm, tk), lambda i, j, k: (i, k)),
              "tn": pl.BlockSpec((tk, tm), lambda i, j, k: (k, i))}[mode]
    b_spec = {"nn": pl.BlockSpec((tk, tn), lambda i, j, k: (k, j)), "nt": pl.BlockSpec((tn, tk), lambda i, j, k: (j, k)),
              "tn": pl.BlockSpec((tk, tn), lambda i, j, k: (k, j))}[mode]
    mn_spec = pl.BlockSpec((tm, tn), lambda i, j, k: (i, j))
    return _mmg(a, b, mode, name=name, grid=(M // tm, N // tn, K // tk), a_spec=a_spec, b_spec=b_spec, out_spec=mn_spec,
                out_shapes=[jax.ShapeDtypeStruct((M, N), dt) for dt in out_dtypes], acc_shape=(tm, tn), epi=epi,
                extras=list(extras) + list(extra_params),
                extra_specs=[mn_spec] * len(extras) + [pl.BlockSpec((1, tn), lambda i, j, k: (0, j))] * len(extra_params),
                hi=hi)


def _mmg(a, b, mode, *, name, grid, a_spec, b_spec, out_spec, out_shapes, acc_shape, epi=None, extras=(),
         extra_specs=(), hi=False):
    nk = grid[2]
    n_e, n_o = len(extras), len(out_shapes)

    def body(*refs):
        a_ref, b_ref = refs[:2]
        e_refs = refs[2:2 + n_e]
        o_refs = refs[2 + n_e:2 + n_e + n_o]
        acc_ref = refs[-1]
        k = pl.program_id(2)

        @pl.when(k == 0)
        def _():
            acc_ref[...] = jnp.zeros_like(acc_ref)

        if hi:
            acc_ref[...] += _dot_hi(a_ref[...].astype(F32), b_ref[...].astype(F32), mode)
        else:
            acc_ref[...] += _dot_raw(a_ref[...], b_ref[...], mode)

        @pl.when(k == nk - 1)
        def _():
            acc = acc_ref[...]
            outs = (acc,) if epi is None else epi(acc, *[e[...].astype(F32) for e in e_refs])
            for o_ref, o in zip(o_refs, outs):
                o_ref[...] = o.astype(o_ref.dtype)

    outs = pl.pallas_call(
        body, name=name, grid=grid,
        in_specs=[a_spec, b_spec] + list(extra_specs),
        out_specs=[out_spec] * n_o,
        out_shape=list(out_shapes),
        scratch_shapes=[pltpu.VMEM(acc_shape, F32)],
        compiler_params=_params(("parallel", "parallel", "arbitrary")),
    )(a, b, *extras)
    return outs if n_o > 1 else outs[0]


def _row_spec(th, cw, ci):
    return pl.BlockSpec((th, cw), lambda i: (i, ci))


def _full_spec(shape):
    return pl.BlockSpec(shape, lambda i: (0,) * len(shape))


def _rowwise(fn, rows, params, outs, n_steps, name):
    n_r, n_p, n_o = len(rows), len(params), len(outs)

    def body(*refs):
        vals = [r[...].astype(F32) for r in refs[:n_r + n_p]]
        res = fn(*vals)
        for o_ref, o in zip(refs[n_r + n_p:], res):
            o_ref[...] = o.astype(o_ref.dtype)

    across = [len(o) == 4 for o in outs]
    res = pl.pallas_call(
        body, name=name, grid=(n_steps,),
        in_specs=[_row_spec(th, cw, ci) for (_, th, cw, ci) in rows] + [_full_spec(p.shape) for p in params],
        out_specs=[pl.BlockSpec((o[0], o[1]), lambda i: (0, i)) if ac else _row_spec(o[0], o[1], 0)
                   for o, ac in zip(outs, across)],
        out_shape=[jax.ShapeDtypeStruct((o[0], n_steps * o[1]) if ac else (n_steps * o[0], o[1]), o[2])
                   for o, ac in zip(outs, across)],
        compiler_params=_params(("parallel",)),
    )(*[r[0] for r in rows], *params)
    return res


def _rowwise_bwd(fn, rows, aux, params, douts, n_steps, name, row_dtypes=None, adds=(), across=None):
    n_r, n_a, n_p, n_d, n_add = len(rows), len(aux), len(params), len(douts), len(adds)
    row_dtypes = row_dtypes or (F32,) * n_r

    def body(*refs):
        it = iter(refs)
        r_vals = [next(it)[...].astype(F32) for _ in range(n_r)]
        a_vals = [next(it)[...].astype(F32) for _ in range(n_a)]
        p_vals = [next(it)[...].astype(F32) for _ in range(n_p)]
        d_vals = [next(it)[...].astype(F32) for _ in range(n_d)]
        add_vals = [next(it)[...].astype(F32) for _ in range(n_add)]
        across_in = next(it) if across is not None else None
        dr_refs = [next(it) for _ in range(n_r)]
        dp_refs = [next(it) for _ in range(n_p)]

        def f(*rp):
            return tuple(fn(*rp[:n_r], *a_vals, *rp[n_r:]))

        _, vjp = jax.vjp(f, *r_vals, *p_vals)
        grads = list(vjp(tuple(d_vals)))
        for (ri, _), av in zip(adds, add_vals):
            grads[ri] = grads[ri] + av
        for dr_ref, g in zip(dr_refs, grads[:n_r]):
            dr_ref[...] = g.astype(dr_ref.dtype)
        if across is not None:
            next(it)[...] = across[1](r_vals, grads[:n_r], across_in[...])

        @pl.when(pl.program_id(0) == 0)
        def _():
            for dp_ref in dp_refs:
                dp_ref[...] = jnp.zeros_like(dp_ref)

        for dp_ref, g in zip(dp_refs, grads[n_r:]):
            dp_ref[...] += g

    all_rows = list(rows) + list(aux) + list(douts) + [(arr,) + tuple(rows[ri][1:3]) + (0,) for ri, arr in adds]
    in_specs = ([_row_spec(th, cw, ci) for (_, th, cw, ci) in list(rows) + list(aux)]
                + [_full_spec(p.shape) for p in params]
                + [_row_spec(th, cw, ci) for (_, th, cw, ci) in all_rows[n_r + n_a:]])
    across_specs = [] if across is None else [pl.BlockSpec((8, rows[0][1]), lambda i: (0, i))]
    res = pl.pallas_call(
        body, name=name, grid=(n_steps,),
        in_specs=in_specs + across_specs,
        out_specs=[_row_spec(th, cw, 0) for (_, th, cw, _) in rows] + [_full_spec(p.shape) for p in params] + across_specs,
        out_shape=[jax.ShapeDtypeStruct((n_steps * th, cw), dt) for (_, th, cw, _), dt in zip(rows, row_dtypes)]
        + [jax.ShapeDtypeStruct(p.shape, F32) for p in params]
        + [jax.ShapeDtypeStruct(across[0].shape, F32) for _ in across_specs],
        compiler_params=_params(("arbitrary",)),
    )(*[r[0] for r in list(rows) + list(aux)], *params, *[r[0] for r in all_rows[n_r + n_a:]],
      *[across[0] for _ in across_specs])
    return res[:n_r], res[n_r:]


def _sigmoid(x):
    return lax.logistic(x)


def _silu(x):
    return x * _sigmoid(x)


def _rms(x, w=None, n=None):
    n = n or x.shape[-1]
    y = x * lax.rsqrt(jnp.sum(x * x, axis=-1, keepdims=True) * (1.0 / n) + EPS)
    return y if w is None else y * w


def _modulate(x, scale, shift):
    return _rms(x) * (1.0 + scale) + shift


def _softplus(x):
    return jnp.maximum(x, 0.0) + jnp.log1p(jnp.exp(-jnp.abs(x)))


@jax.custom_vjp
def _rot_half64(x):
    lane = lax.broadcasted_iota(jnp.int32, x.shape, 1)
    up = pltpu.roll(x, 96, 1)
    down = pltpu.roll(x, 32, 1)
    return jnp.where(lane < 32, up, jnp.where(lane < 64, down, 0.0))


_rot_half64.defvjp(lambda x: (_rot_half64(x), None), lambda _, g: (_rot_half64(g),))


def _rope128(x, cos_p, sin_p):
    return x * cos_p + _rot_half64(x) * sin_p


def _gdn_pre_fn(qkvc, kab, alog_p, dt_p):
    a = _silu(qkvc)
    qs, ks = [], []
    for h in range(HEADS):
        qh = a[:, HEAD_DIM * h:HEAD_DIM * (h + 1)]
        kh = a[:, 512 + HEAD_DIM * h:512 + HEAD_DIM * (h + 1)]
        qs.append(qh * lax.rsqrt(jnp.sum(qh * qh, axis=-1, keepdims=True) + EPS) * (HEAD_DIM ** -0.5))
        ks.append(kh * lax.rsqrt(jnp.sum(kh * kh, axis=-1, keepdims=True) + EPS))
    lane = lax.broadcasted_iota(jnp.int32, kab.shape, 1)
    g_full = -jnp.exp(alog_p) * _softplus(kab + dt_p)
    b_full = _sigmoid(kab)
    gb = jnp.where((lane >= 64) & (lane < 68), g_full, jnp.where((lane >= 68) & (lane < 72), b_full, 0.0))
    return jnp.concatenate(qs, axis=1), jnp.concatenate(ks, axis=1), a[:, 1024:1536], gb


INTRA_ROWS = (512, 256, 128, 64)
TOKEN_ROWS = (512, 256, 128)

_BNN = (((2,), (1,)), ((0,), (0,)))
_BNT = (((2,), (2,)), ((0,), (0,)))
_BTN = (((1,), (1,)), ((0,), (0,)))


def _split_bf16(a):
    hi = a.astype(BF16)
    return hi, (a - hi.astype(F32)).astype(BF16)


def _dot3_raw(a, b, dims):
    a_hi, a_lo = _split_bf16(a)
    b_hi, b_lo = _split_bf16(b)
    dot = lambda x_, y_: lax.dot_general(x_, y_, dims, preferred_element_type=F32)
    return dot(a_hi, b_hi) + (dot(a_hi, b_lo) + dot(a_lo, b_hi))


@functools.partial(jax.custom_vjp, nondiff_argnums=(2, 3))
def _dot3(a, b, nt, exact_bwd=True):
    return _dot3_raw(a, b, _BNT if nt else _BNN)


def _dot3_fwd(a, b, nt, exact_bwd):
    return _dot3_raw(a, b, _BNT if nt else _BNN), (a, b)


def _dot3_bwd(nt, exact_bwd, res, g):
    a, b = res
    if exact_bwd:
        dot = _dot3_raw
    else:
        dot = lambda x_, y_, d_: lax.dot_general(x_.astype(BF16), y_.astype(BF16), d_, preferred_element_type=F32)
    if nt:
        return dot(g, b, _BNN), dot(jnp.swapaxes(g, 1, 2), a, _BNN)
    return dot(g, b, _BNT), dot(jnp.swapaxes(a, 1, 2), g, _BNN)


_dot3.defvjp(_dot3_fwd, _dot3_bwd)


@functools.partial(jax.custom_vjp, nondiff_argnums=(2,))
def _bdot_b(a, b, nt):
    return lax.dot_general(a.astype(BF16), b.astype(BF16), _BNT if nt else _BNN, preferred_element_type=F32)


def _bdot_b_fwd(a, b, nt):
    return _bdot_b(a, b, nt), (a, b)


def _bdot_b_bwd(nt, res, g):
    a, b = res
    dot = lambda x_, y_, d_: lax.dot_general(x_.astype(BF16), y_.astype(BF16), d_, preferred_element_type=F32)
    if nt:
        return dot(g, b, _BNN), dot(jnp.swapaxes(g, 1, 2), a, _BNN)
    return dot(g, b, _BNT), dot(jnp.swapaxes(a, 1, 2), g, _BNN)


_bdot_b.defvjp(_bdot_b_fwd, _bdot_b_bwd)


@jax.custom_vjp
def _inverse_given(a_mat, inv):
    return inv


def _inverse_given_bwd(inv, g):
    inv_t = jnp.swapaxes(inv, 1, 2)
    return -_dot3_raw(_dot3_raw(inv_t, g, _BNN), inv_t, _BNN), jnp.zeros_like(inv)


_inverse_given.defvjp(lambda a_mat, inv: (inv, inv), _inverse_given_bwd)


def _intra_batched(q, k, v, g_col, b_col, inv_known=None):
    c = CHUNK
    nb = q.shape[0]
    row = lax.broadcasted_iota(jnp.int32, (1, c, c), 1)
    col = lax.broadcasted_iota(jnp.int32, (1, c, c), 2)
    incl, strict, eye = row >= col, row > col, row == col
    tri = jnp.broadcast_to(jnp.where(incl, 1.0, 0.0).astype(F32), (nb, c, c))
    ident = jnp.where(eye, 1.0, 0.0).astype(F32)
    g_wide = _dot3(tri, jnp.broadcast_to(g_col, (nb, c, HEAD_DIM)), False)
    g_i = g_wide[:, :, :c]
    g_j = jnp.sum(jnp.where(eye, g_i, 0.0), axis=1, keepdims=True)
    decay = jnp.where(incl, jnp.exp(jnp.where(incl, g_i - g_j, 0.0)), 0.0)
    kk = _bdot_b(k, k, True)
    a_mat = jnp.where(strict, b_col * kk * decay, 0.0)
    if inv_known is None:
        x_pow = -a_mat
        inv = ident + x_pow
        for _ in range(5):
            x_pow = _dot3(x_pow, x_pow, False, False)
            inv = inv + _dot3(inv, x_pow, False, False)
    else:
        inv = _inverse_given(a_mat, inv_known)
    e_wide = jnp.exp(g_wide)
    u = _dot3(inv, v * b_col, False)
    wk = _dot3(inv, k * b_col * e_wide, False)
    qk = _bdot_b(q, k, True) * decay
    last = lax.broadcasted_iota(jnp.int32, (1, c, HEAD_DIM), 1) == c - 1
    g_last = jnp.sum(jnp.where(last, g_wide, 0.0), axis=1, keepdims=True)
    qd = q * e_wide
    kd = k * jnp.exp(g_last - g_wide)
    gl = jnp.broadcast_to(jnp.exp(g_last), (nb, 8, HEAD_DIM))
    return u, wk, qd, kd, qk, gl, inv


def _gdn_intra_fn(q, k, v, gb, *inv_known):
    t = q.shape[0]
    nch = t // CHUNK
    lane = lax.broadcasted_iota(jnp.int32, gb.shape, 1)

    def heads_first(x_):
        return jnp.concatenate([x_[:, HEAD_DIM * h:HEAD_DIM * (h + 1)].reshape(nch, CHUNK, HEAD_DIM) for h in range(HEADS)],
                               axis=0)

    def column(first_lane):
        return jnp.concatenate([jnp.sum(jnp.where(lane == first_lane + h, gb, 0.0), axis=1, keepdims=True)
                                .reshape(nch, CHUNK, 1) for h in range(HEADS)], axis=0)

    known = jnp.concatenate([x_.reshape(nch, CHUNK, CHUNK) for x_ in inv_known], axis=0) if inv_known else None
    u, wk, qd, kd, qk, gl, inv = _intra_batched(heads_first(q), heads_first(k), heads_first(v), column(64), column(68), known)

    def rows_first(x_):
        r, w_ = x_.shape[1], x_.shape[2]
        return jnp.concatenate([x_[nch * h:nch * (h + 1)].reshape(nch * r, w_) for h in range(HEADS)], axis=1)

    per_head = lambda x_: [x_[nch * h:nch * (h + 1)].reshape(t, CHUNK) for h in range(HEADS)]
    outs = (rows_first(u), rows_first(wk), rows_first(qd), rows_first(kd), *per_head(qk), rows_first(gl))
    return outs if inv_known else outs + tuple(per_head(inv))


def _scan_step(s0, u, wk, qd, kd, qk, gl):
    v_new = u - _bdot_b(wk, s0, False)
    o = _bdot_b(qd, s0, False) + _bdot_b(qk, v_new, False)
    s1 = s0 * gl[:, 0:1, :] + _bdot_b(jnp.swapaxes(kd, 1, 2), v_new, False)
    return o, s1


def _mix_post_fn(o_a, z, o_b, gnw, onw):
    parts = [_rms(o_a[:, HEAD_DIM * h:HEAD_DIM * (h + 1)], gnw) * _silu(z[:, HEAD_DIM * h:HEAD_DIM * (h + 1)])
             for h in range(HEADS)]
    parts += [_rms(o_b[:, HEAD_DIM * h:HEAD_DIM * (h + 1)], onw) for h in range(HEADS)]
    return (jnp.concatenate(parts, axis=1),)


def _mla_pre_fn(ckv, cq, kab, cos_p, sin_p, qnw, kvnw, wuq, wukv, qn_w, qr_w, kn_w, kr_w):
    scale = (HEAD_DIM + ROPE) ** -0.5 * LOG2_E
    qf = _bdot(_rms(cq, qnw), wuq, "nn")
    kvf = _bdot(_rms(ckv, kvnw), wukv, "nn")
    lane = lax.broadcasted_iota(jnp.int32, kab.shape, 1)
    kr = _rope128(_rms(jnp.where(lane < ROPE, kab, 0.0), kr_w, n=ROPE), cos_p, sin_p)
    qs, ks = [], []
    for h in range(HEADS):
        qn = _rms(qf[:, 256 * h:256 * h + 128], qn_w) * scale
        qr = _rope128(_rms(qf[:, 256 * h + 128:256 * h + 256], qr_w, n=ROPE), cos_p, sin_p) * scale
        qs += [qn, qr]
        ks += [_rms(kvf[:, 128 * h:128 * (h + 1)], kn_w), kr]
    return jnp.concatenate(qs, axis=1), jnp.concatenate(ks, axis=1), kvf[:, 512:]


def _conv_fwd(proj, conv_w, tm, name):
    S = proj.shape[0]
    C = 1536
    nb = tm // 8

    def body(x_ref, prev_ref, w_ref, o_ref, ext_ref):
        i = pl.program_id(0)
        ext_ref[0:8, :] = jnp.where(i > 0, prev_ref[...], 0.0)
        ext_ref[8:, :] = x_ref[...]
        acc = jnp.zeros((tm, C), F32)
        for k in range(4):
            acc = acc + w_ref[k:k + 1, :] * ext_ref[pl.ds(5 + k, tm), :]
        o_ref[...] = acc

    return pl.pallas_call(
        body, name=name, grid=(S // tm,),
        in_specs=[pl.BlockSpec((tm, C), lambda i: (i, 0)),
                  pl.BlockSpec((8, C), lambda i: (jnp.maximum(i * nb - 1, 0), 0)),
                  pl.BlockSpec((4, C), lambda i: (0, 0))],
        out_specs=pl.BlockSpec((tm, C), lambda i: (i, 0)),
        out_shape=jax.ShapeDtypeStruct((S, C), F32),
        scratch_shapes=[pltpu.VMEM((tm + 8, C), F32)],
        compiler_params=_params(("arbitrary",)),
    )(proj, proj, conv_w)


def _conv_bwd(proj, dout, conv_w, tm, name):
    S = proj.shape[0]
    C = 1536
    nb = tm // 8
    n_steps = S // tm

    def body(x_ref, prev_ref, d_ref, next_ref, w_ref, dx_ref, dw_ref, xext_ref, dext_ref):
        i = pl.program_id(0)
        xext_ref[0:8, :] = jnp.where(i > 0, prev_ref[...], 0.0)
        xext_ref[8:, :] = x_ref[...]
        dext_ref[0:tm, :] = d_ref[...]
        dext_ref[tm:, :] = jnp.where(i < n_steps - 1, next_ref[...], 0.0)
        d = d_ref[...]
        acc = jnp.zeros((tm, C), F32)
        dws = []
        for k in range(4):
            acc = acc + w_ref[k:k + 1, :] * dext_ref[pl.ds(3 - k, tm), :]
            dws.append(jnp.sum(d * xext_ref[pl.ds(5 + k, tm), :], axis=0, keepdims=True))
        dx_ref[...] = acc.astype(dx_ref.dtype)

        @pl.when(i == 0)
        def _():
            dw_ref[...] = jnp.zeros_like(dw_ref)

        dw_ref[...] += jnp.concatenate(dws + [jnp.zeros((4, C), F32)], axis=0)

    return pl.pallas_call(
        body, name=name, grid=(n_steps,),
        in_specs=[pl.BlockSpec((tm, C), lambda i: (i, 0)),
                  pl.BlockSpec((8, C), lambda i: (jnp.maximum(i * nb - 1, 0), 0)),
                  pl.BlockSpec((tm, C), lambda i: (i, 0)),
                  pl.BlockSpec((8, C), lambda i: (jnp.minimum((i + 1) * nb, S // 8 - 1), 0)),
                  pl.BlockSpec((4, C), lambda i: (0, 0))],
        out_specs=[pl.BlockSpec((tm, C), lambda i: (i, 0)), pl.BlockSpec((8, C), lambda i: (0, 0))],
        out_shape=[jax.ShapeDtypeStruct((S, C), BF16), jax.ShapeDtypeStruct((8, C), F32)],
        scratch_shapes=[pltpu.VMEM((tm + 8, C), F32), pltpu.VMEM((tm + 8, C), F32)],
        compiler_params=_params(("arbitrary",)),
    )(proj, proj, dout, dout, conv_w)


SCAN_CHUNKS = (8, 4, 2, 1)


def _gdn_scan_fwd(u, wk, qd, kd, qks, gl, name):
    S = u.shape[0]
    nc = S // CHUNK
    cs = _pick(nc, SCAN_CHUNKS)
    W = HEADS * HEAD_DIM

    def body(u_ref, wk_ref, qd_ref, kd_ref, qk0, qk1, qk2, qk3, gl_ref, o_ref, sp_ref, s_ref):
        @pl.when(pl.program_id(0) == 0)
        def _():
            s_ref[...] = jnp.zeros_like(s_ref)

        state = s_ref[...]
        for c in range(cs):
            rows, gl_rows = slice(CHUNK * c, CHUNK * (c + 1)), slice(8 * c, 8 * (c + 1))
            sp_ref[c] = state
            o, state = _scan_step(state, _heads(u_ref, HEAD_DIM, rows), _heads(wk_ref, HEAD_DIM, rows),
                                  _heads(qd_ref, HEAD_DIM, rows), _heads(kd_ref, HEAD_DIM, rows),
                                  jnp.stack([r[rows, :] for r in (qk0, qk1, qk2, qk3)]), _heads(gl_ref, HEAD_DIM, gl_rows))
            for h in range(HEADS):
                o_ref[rows, HEAD_DIM * h:HEAD_DIM * (h + 1)] = o[h]
        s_ref[...] = state

    row = pl.BlockSpec((cs * CHUNK, W), lambda n: (n, 0))
    qk_spec = pl.BlockSpec((cs * CHUNK, CHUNK), lambda n: (n, 0))
    return pl.pallas_call(
        body, name=name, grid=(nc // cs,),
        in_specs=[row, row, row, row, qk_spec, qk_spec, qk_spec, qk_spec, pl.BlockSpec((cs * 8, W), lambda n: (n, 0))],
        out_specs=[row, pl.BlockSpec((cs, HEADS, HEAD_DIM, HEAD_DIM), lambda n: (n, 0, 0, 0))],
        out_shape=[jax.ShapeDtypeStruct((S, W), F32), jax.ShapeDtypeStruct((nc, HEADS, HEAD_DIM, HEAD_DIM), F32)],
        scratch_shapes=[pltpu.VMEM((HEADS, HEAD_DIM, HEAD_DIM), F32)],
        compiler_params=_params(("arbitrary",)),
    )(u, wk, qd, kd, *qks, gl)


def _gdn_scan_bwd(u, wk, qd, kd, qks, gl, s_prev, d_o, name):
    S = u.shape[0]
    nc = S // CHUNK
    cs = _pick(nc, SCAN_CHUNKS)
    nb = nc // cs
    W = HEADS * HEAD_DIM

    def body(u_ref, wk_ref, qd_ref, kd_ref, qk0, qk1, qk2, qk3, gl_ref, sp_ref, do_ref,
             du_ref, dwk_ref, dqd_ref, dkd_ref, dqk0, dqk1, dqk2, dqk3, dgl_ref, ds_ref):
        @pl.when(pl.program_id(0) == 0)
        def _():
            ds_ref[...] = jnp.zeros_like(ds_ref)

        d_state = ds_ref[...]
        for c in reversed(range(cs)):
            rows, gl_rows = slice(CHUNK * c, CHUNK * (c + 1)), slice(8 * c, 8 * (c + 1))
            _, vjp = jax.vjp(_scan_step, sp_ref[c], _heads(u_ref, HEAD_DIM, rows), _heads(wk_ref, HEAD_DIM, rows),
                             _heads(qd_ref, HEAD_DIM, rows), _heads(kd_ref, HEAD_DIM, rows),
                             jnp.stack([r[rows, :] for r in (qk0, qk1, qk2, qk3)]), _heads(gl_ref, HEAD_DIM, gl_rows))
            d_state, du, dwk, dqd, dkd, dqk, dgl = vjp((_heads(do_ref, HEAD_DIM, rows), d_state))
            for h, dqk_ref in enumerate((dqk0, dqk1, dqk2, dqk3)):
                sl = slice(HEAD_DIM * h, HEAD_DIM * (h + 1))
                du_ref[rows, sl] = du[h]
                dwk_ref[rows, sl] = dwk[h]
                dqd_ref[rows, sl] = dqd[h]
                dkd_ref[rows, sl] = dkd[h]
                dqk_ref[rows, :] = dqk[h]
                dgl_ref[gl_rows, sl] = dgl[h]
        ds_ref[...] = d_state

    rev = lambda n: (nb - 1 - n, 0)
    row = pl.BlockSpec((cs * CHUNK, W), rev)
    qk_spec = pl.BlockSpec((cs * CHUNK, CHUNK), rev)
    gl_spec = pl.BlockSpec((cs * 8, W), rev)
    qk_shape = jax.ShapeDtypeStruct((S, CHUNK), F32)
    row_shape = jax.ShapeDtypeStruct((S, W), F32)
    return pl.pallas_call(
        body, name=name, grid=(nb,),
        in_specs=[row, row, row, row, qk_spec, qk_spec, qk_spec, qk_spec, gl_spec,
                  pl.BlockSpec((cs, HEADS, HEAD_DIM, HEAD_DIM), lambda n: (nb - 1 - n, 0, 0, 0)), row],
        out_specs=[row, row, row, row, qk_spec, qk_spec, qk_spec, qk_spec, gl_spec],
        out_shape=[row_shape] * 4 + [qk_shape] * 4 + [jax.ShapeDtypeStruct((nc * 8, W), F32)],
        scratch_shapes=[pltpu.VMEM((HEADS, HEAD_DIM, HEAD_DIM), F32)],
        compiler_params=_params(("arbitrary",)),
    )(u, wk, qd, kd, *qks, gl, s_prev, d_o)


NEG = -1e30


def _chunk_mask(i, j, t, transposed=False):
    q_axis, k_axis = (1, 0) if transposed else (0, 1)
    r = (i * t + lax.broadcasted_iota(jnp.int32, (t, t), q_axis)) // CHUNK
    c = (j * t + lax.broadcasted_iota(jnp.int32, (t, t), k_axis)) // CHUNK
    return c <= r


def _tile_pairs(n, by_key):
    pairs = [(i, j) for j in range(n) for i in range(j, n)] if by_key else [(i, j) for i in range(n) for j in range(i + 1)]
    return jnp.asarray(np.array([p[0] for p in pairs], np.int32)), jnp.asarray(np.array([p[1] for p in pairs], np.int32))


def _heads(ref, width, rows=slice(None)):
    return jnp.stack([ref[rows, width * h:width * (h + 1)] for h in range(HEADS)])


def _bmm(a, b, dims):
    return lax.dot_general(a.astype(BF16), b.astype(BF16), dims, preferred_element_type=F32)


def _attn_fwd(q, k, v_t, t, name):
    S = q.shape[0]
    n = S // t
    qi, kj = _tile_pairs(n, by_key=False)

    def body(qi_ref, kj_ref, q_ref, k_ref, vt_ref, o_ref, lse_ref, m_ref, l_ref, acc_ref):
        i, j = qi_ref[pl.program_id(0)], kj_ref[pl.program_id(0)]

        @pl.when(j == 0)
        def _():
            m_ref[...] = jnp.full_like(m_ref, NEG)
            l_ref[...] = jnp.zeros_like(l_ref)
            acc_ref[...] = jnp.zeros_like(acc_ref)

        def update(masked):
            s_t = _bmm(_heads(k_ref, 256), _heads(q_ref, 256), _BNT)
            if masked:
                s_t = jnp.where(_chunk_mask(i, j, t, transposed=True)[None], s_t, NEG)
            m_old = m_ref[...]
            m_new = jnp.maximum(m_old, jnp.max(s_t, axis=1, keepdims=True))
            p_t = jnp.exp2(s_t - m_new)
            alpha = jnp.exp2(m_old - m_new)
            l_ref[...] = alpha * l_ref[...] + jnp.sum(p_t, axis=1, keepdims=True)
            v_heads = jnp.stack([vt_ref[HEAD_DIM * h:HEAD_DIM * (h + 1), :] for h in range(HEADS)])
            acc_ref[...] = alpha * acc_ref[...] + _bmm(v_heads, p_t, _BNN)
            m_ref[...] = m_new

        @pl.when(j < i)
        def _():
            update(False)

        @pl.when(j == i)
        def _():
            update(True)
            for h in range(HEADS):
                sl = slice(HEAD_DIM * h, HEAD_DIM * (h + 1))
                o_ref[:, sl] = jnp.transpose(acc_ref[h] / l_ref[h])
                lse_ref[h:h + 1, :] = m_ref[h] + jnp.log(l_ref[h]) * LOG2_E
            lse_ref[HEADS:, :] = jnp.zeros((8 - HEADS, t), F32)

    row = lambda p, qi_, kj_: (qi_[p], 0)
    return pl.pallas_call(
        body, name=name,
        grid_spec=pltpu.PrefetchScalarGridSpec(
            num_scalar_prefetch=2, grid=(qi.shape[0],),
            in_specs=[pl.BlockSpec((t, HEADS * 256), row), pl.BlockSpec((t, HEADS * 256), lambda p, qi_, kj_: (kj_[p], 0)),
                      pl.BlockSpec((HEADS * HEAD_DIM, t), lambda p, qi_, kj_: (0, kj_[p]))],
            out_specs=[pl.BlockSpec((t, HEADS * HEAD_DIM), row), pl.BlockSpec((8, t), lambda p, qi_, kj_: (0, qi_[p]))],
            scratch_shapes=[pltpu.VMEM((HEADS, 1, t), F32), pltpu.VMEM((HEADS, 1, t), F32),
                            pltpu.VMEM((HEADS, HEAD_DIM, t), F32)]),
        out_shape=[jax.ShapeDtypeStruct((S, HEADS * HEAD_DIM), F32), jax.ShapeDtypeStruct((8, S), F32)],
        compiler_params=_params(("arbitrary",)),
    )(qi, kj, q, k, v_t)


def _attn_stats(o, d_o, lse):
    lane = lax.broadcasted_iota(jnp.int32, (o.shape[0], HEAD_DIM), 1)
    stats = jnp.zeros((o.shape[0], HEAD_DIM), F32)
    for h in range(HEADS):
        sl = slice(HEAD_DIM * h, HEAD_DIM * (h + 1))
        delta = jnp.sum(d_o[:, sl] * o[:, sl], axis=1, keepdims=True)
        stats = stats + jnp.where(lane == HEADS + h, delta, 0.0)
    return lse + jnp.transpose(stats)[0:8, :]


BWD_GROUP = 2


def _attn_bwd(q, k, v, d_o, stats, t, name):
    S = q.shape[0]
    n = S // t
    groups = HEADS // BWD_GROUP
    gq, gv = BWD_GROUP * 256, BWD_GROUP * HEAD_DIM
    qi, kj = _tile_pairs(n, by_key=True)
    n_pairs = qi.shape[0]
    st = stats.reshape(2, groups, BWD_GROUP, S).transpose(1, 0, 2, 3).reshape(groups, 2 * BWD_GROUP, S)
    st = jnp.pad(st, ((0, 0), (0, 8 - 2 * BWD_GROUP), (0, 0)))

    def heads(ref, width, rows=slice(None)):
        return jnp.stack([ref[rows, width * h:width * (h + 1)] for h in range(BWD_GROUP)])

    def body(qi_ref, kj_ref, q_ref, k_ref, v_ref, do_ref, st_ref, dq_hbm, dk_ref, dv_ref, dq_acc, sem):
        g, p = pl.program_id(0), pl.program_id(1)
        i, j = qi_ref[p], kj_ref[p]

        @pl.when(i == j)
        def _():
            dk_ref[...] = jnp.zeros_like(dk_ref)
            dv_ref[...] = jnp.zeros_like(dv_ref)

        def update(masked):
            qh, kh = heads(q_ref, 256), heads(k_ref, 256)
            d_out = heads(do_ref, HEAD_DIM)
            stv = st_ref[...]
            lse_row = jnp.stack([stv[h:h + 1, :] for h in range(BWD_GROUP)])
            delta_row = jnp.stack([stv[BWD_GROUP + h:BWD_GROUP + h + 1, :] for h in range(BWD_GROUP)])
            s_t = _bmm(kh, qh, _BNT)
            p_t = jnp.exp2(s_t - lse_row)
            if masked:
                p_t = jnp.where(_chunk_mask(i, j, t, transposed=True)[None], p_t, 0.0)
            dv = _bmm(p_t, d_out, _BNN)
            dp_t = _bmm(heads(v_ref, HEAD_DIM), d_out, _BNT)
            ds_t = p_t * (dp_t - delta_row)
            dk = _bmm(ds_t, qh, _BNN)
            dq = _bmm(ds_t, kh, _BTN)
            rows = pl.ds(pl.multiple_of(i * t, t), t)
            for h in range(BWD_GROUP):
                dk_ref[:, 256 * h:256 * (h + 1)] += dk[h]
                dv_ref[:, HEAD_DIM * h:HEAD_DIM * (h + 1)] += dv[h]

            @pl.when(j == 0)
            def _():
                for h in range(BWD_GROUP):
                    dq_acc[rows, 256 * h:256 * (h + 1)] = dq[h]

            @pl.when(j > 0)
            def _():
                for h in range(BWD_GROUP):
                    dq_acc[rows, 256 * h:256 * (h + 1)] += dq[h]

        @pl.when(i == j)
        def _():
            update(True)

        @pl.when(i > j)
        def _():
            update(False)

        @pl.when(i == n - 1)
        def _():
            dk_ref[...] *= 1.0 / LOG2_E

        @pl.when(p == n_pairs - 1)
        def _():
            dq_acc[...] *= 1.0 / LOG2_E
            for gg in range(groups):
                @pl.when(g == gg)
                def _():
                    cp = pltpu.make_async_copy(dq_acc, dq_hbm.at[:, gq * gg:gq * (gg + 1)], sem)
                    cp.start()
                    cp.wait()

    q_blk = lambda g, p, qi_, kj_: (qi_[p], g)
    k_blk = lambda g, p, qi_, kj_: (kj_[p], g)
    return pl.pallas_call(
        body, name=name,
        grid_spec=pltpu.PrefetchScalarGridSpec(
            num_scalar_prefetch=2, grid=(groups, n_pairs),
            in_specs=[pl.BlockSpec((t, gq), q_blk), pl.BlockSpec((t, gq), k_blk), pl.BlockSpec((t, gv), k_blk),
                      pl.BlockSpec((t, gv), q_blk), pl.BlockSpec((None, 8, t), lambda g, p, qi_, kj_: (g, 0, qi_[p]))],
            out_specs=[pl.BlockSpec(memory_space=pl.ANY), pl.BlockSpec((t, gq), k_blk), pl.BlockSpec((t, gv), k_blk)],
            scratch_shapes=[pltpu.VMEM((S, gq), F32), pltpu.SemaphoreType.DMA]),
        out_shape=[jax.ShapeDtypeStruct((S, HEADS * 256), F32), jax.ShapeDtypeStruct((S, HEADS * 256), F32),
                   jax.ShapeDtypeStruct((S, HEADS * HEAD_DIM), F32)],
        compiler_params=_params(("arbitrary", "arbitrary")),
    )(qi, kj, q, k, v, d_o, st)


FFN_PIECE = 2 * D_FF // N_DEV
HID_PIECES = D_FF // FFN_PIECE


def _after_specs(after):
    return [] if after is None else [pl.BlockSpec(memory_space=pl.ANY)]


def _after_args(after):
    return [] if after is None else [after]


def _ffn_gw8(h, d_gate, d_up, name, after=None):
    S = h.shape[0]
    tm = 512
    tk = _pick(S, MATMUL_ROWS)
    nk = S // tk

    def body(h_ref, dg_ref, du_ref, *rest):
        o_ref, acc_ref = rest[-2:]
        k = pl.program_id(1)

        @pl.when(k == 0)
        def _():
            acc_ref[...] = jnp.zeros_like(acc_ref)

        h_t = jnp.transpose(h_ref[...])
        for p in range(HID_PIECES):
            acc_ref[p] += _dot_raw(h_t, dg_ref[p], "nn")
            acc_ref[HID_PIECES + p] += _dot_raw(h_t, du_ref[p], "nn")

        @pl.when(k == nk - 1)
        def _():
            o_ref[...] = acc_ref[...].astype(o_ref.dtype)

    d_spec = pl.BlockSpec((HID_PIECES, tk, FFN_PIECE), lambda i, k: (0, k, 0))
    return pl.pallas_call(
        body, name=name, grid=(D_MODEL // tm, nk),
        in_specs=[pl.BlockSpec((tk, tm), lambda i, k: (k, i)), d_spec, d_spec] + _after_specs(after),
        out_specs=pl.BlockSpec((2 * HID_PIECES, tm, FFN_PIECE), lambda i, k: (0, i, 0)),
        out_shape=jax.ShapeDtypeStruct((2 * HID_PIECES, D_MODEL, FFN_PIECE), BF16),
        scratch_shapes=[pltpu.VMEM((2 * HID_PIECES, tm, FFN_PIECE), F32)],
        compiler_params=_params(("parallel", "arbitrary")),
    )(h, d_gate, d_up, *_after_args(after))


EPILOGUE_ROWS = 256


def _dmod_epilogue(acc_ref, x_ref, do_ref, sc_ref, sh_ref, dx_ref, dsc_ref, dsh_ref, below, below_in, below_out):
    rows_total = acc_ref.shape[0]
    step = min(EPILOGUE_ROWS, rows_total)
    dsc, dsh, dg = 0.0, 0.0, 0.0
    for r in range(rows_total // step):
        rows = slice(step * r, step * (r + 1))
        _, vjp = jax.vjp(_modulate, x_ref[rows, :], sc_ref[...], sh_ref[...])
        dx, dsc_r, dsh_r = vjp(acc_ref[rows, :])
        dx = dx + do_ref[rows, :]
        dx_ref[rows, :] = dx
        dsc, dsh = dsc + dsc_r, dsh + dsh_r
        if below is not None:
            coef = below[2]
            below_out[0][rows, :] = (coef * below_in[1][...] * dx).astype(below_out[0].dtype)
            dg = dg + jnp.sum(coef * below_in[0][rows, :] * dx, axis=0, keepdims=True)
    dsc_ref[...] += dsc
    dsh_ref[...] += dsh
    if below is not None:
        below_out[1][...] += dg


def _ffn_dh(d_gate, d_up, w8, x, d_out, scale, shift, name, after=None, below=None):
    S = d_gate.shape[1]
    tm = _pick(S, MATMUL_ROWS)
    n_below = 0 if below is None else 2

    def body(dg_ref, du_ref, wg_ref, wu_ref, x_ref, do_ref, sc_ref, sh_ref, *rest):
        below_in = rest[:n_below]
        outs = rest[len(rest) - 4 - n_below:]
        dx_ref, dsc_ref, dsh_ref = outs[:3]
        below_out, acc_ref = outs[3:3 + n_below], outs[-1]
        i, k = pl.program_id(0), pl.program_id(1)

        @pl.when(k == 0)
        def _():
            acc_ref[...] = jnp.zeros_like(acc_ref)

        acc_ref[...] += _dot_raw(dg_ref[...], wg_ref[...], "nt") + _dot_raw(du_ref[...], wu_ref[...], "nt")

        @pl.when((k == 0) & (i == 0))
        def _():
            for r in (dsc_ref, dsh_ref) + tuple(below_out[1:]):
                r[...] = jnp.zeros_like(r)

        @pl.when(k == HID_PIECES - 1)
        def _():
            _dmod_epilogue(acc_ref, x_ref, do_ref, sc_ref, sh_ref, dx_ref, dsc_ref, dsh_ref, below, below_in, below_out)

    d_spec = pl.BlockSpec((None, tm, FFN_PIECE), lambda i, k: (k, i, 0))
    row = pl.BlockSpec((tm, D_MODEL), lambda i, k: (i, 0))
    par = pl.BlockSpec((1, D_MODEL), lambda i, k: (0, 0))
    row_shape, par_shape = jax.ShapeDtypeStruct((S, D_MODEL), F32), jax.ShapeDtypeStruct((1, D_MODEL), F32)
    return pl.pallas_call(
        body, name=name, grid=(S // tm, HID_PIECES),
        in_specs=[d_spec, d_spec,
                  pl.BlockSpec((None, D_MODEL, FFN_PIECE), lambda i, k: (k, 0, 0)),
                  pl.BlockSpec((None, D_MODEL, FFN_PIECE), lambda i, k: (k + HID_PIECES, 0, 0)),
                  row, row, par, par] + [row, par][:n_below] + _after_specs(after),
        out_specs=[row, par, par] + [row, par][:n_below],
        out_shape=[row_shape, par_shape, par_shape] + [jax.ShapeDtypeStruct((S, D_MODEL), BF16), par_shape][:n_below],
        scratch_shapes=[pltpu.VMEM((tm, D_MODEL), F32)],
        compiler_params=_params(("arbitrary", "arbitrary")),
    )(d_gate, d_up, w8, w8, x, d_out, scale, shift, *(below[:2] if below is not None else ()), *_after_args(after))


def _swiglu_bwd(d_hid, hid_by_gate, hid_by_up):
    return d_hid * hid_by_gate, d_hid * hid_by_up


def _adamw_math(w_, g_, m_, v_):
    m_ = ADAM_B1 * m_ + (1.0 - ADAM_B1) * g_
    v_ = ADAM_B2 * v_ + (1.0 - ADAM_B2) * (g_ * g_)
    m_hat = m_ / (1.0 - ADAM_B1 ** ADAM_STEP)
    v_hat = v_ / (1.0 - ADAM_B2 ** ADAM_STEP)
    return -ADAM_LR * (m_hat / (jnp.sqrt(v_hat) + ADAM_EPS) + ADAM_WD * w_), m_, v_


def _adamw(w, g, m, v, name):
    R, C = w.shape
    tr = _pick(R, (256, 176, 128, 64, 32, 16, 8))

    def body(w_ref, g_ref, m_ref, v_ref, d_ref, nm_ref, nv_ref):
        d_ref[...], nm_ref[...], nv_ref[...] = _adamw_math(w_ref[...], g_ref[...], m_ref[...], v_ref[...])

    spec = pl.BlockSpec((tr, C), lambda i: (i, 0))
    return pl.pallas_call(
        body, name=name, grid=(R // tr,),
        in_specs=[spec] * 4, out_specs=[spec] * 3,
        out_shape=[jax.ShapeDtypeStruct((R, C), F32)] * 3,
        compiler_params=_params(("parallel",)),
    )(w, g, m, v)


def _sum_adamw(parts, w, m, v, name, transposed=False):
    _, R, C = parts.shape
    tr = _pick(R, (256, 176, 128, 64, 32, 16, 8))

    def body(p_ref, w_ref, m_ref, v_ref, g_ref, d_ref, nm_ref, nv_ref):
        g_ = p_ref[0].astype(F32)
        for d in range(1, N_DEV):
            g_ = g_ + p_ref[d].astype(F32)
        if transposed:
            g_ = jnp.transpose(g_)
        g_ref[...] = g_
        d_ref[...], nm_ref[...], nv_ref[...] = _adamw_math(w_ref[...], g_, m_ref[...], v_ref[...])

    spec = pl.BlockSpec((C, tr), lambda i: (0, i)) if transposed else pl.BlockSpec((tr, C), lambda i: (i, 0))
    return pl.pallas_call(
        body, name=name, grid=(R // tr,),
        in_specs=[pl.BlockSpec((N_DEV, tr, C), lambda i: (0, i, 0)), spec, spec, spec], out_specs=[spec] * 4,
        out_shape=[jax.ShapeDtypeStruct(w.shape, F32)] * 4,
        compiler_params=_params(("parallel",)),
    )(parts, w, m, v)


def _sum_devices(parts, name):
    _, R, C = parts.shape
    tr = _pick(R, (512, 256, 176, 128, 64, 32, 16, 8))

    def body(p_ref, o_ref):
        acc = p_ref[0].astype(F32)
        for d in range(1, N_DEV):
            acc = acc + p_ref[d].astype(F32)
        o_ref[...] = acc

    return pl.pallas_call(
        body, name=name, grid=(R // tr,),
        in_specs=[pl.BlockSpec((N_DEV, tr, C), lambda i: (0, i, 0))],
        out_specs=pl.BlockSpec((tr, C), lambda i: (i, 0)),
        out_shape=jax.ShapeDtypeStruct((R, C), F32),
        compiler_params=_params(("parallel",)),
    )(parts)


def _my_place():
    return lax.axis_index("x"), lax.axis_index("y"), lax.axis_index("c")


def _all_gather(blocks, name):
    n = len(blocks)

    def body(*refs):
        x_refs, out_refs = refs[:n], refs[n:2 * n]
        send_sems, recv_sems, local_sems = refs[2 * n:]
        x, y, c = _my_place()
        me, sibling = (x, y, c), (x, y, 1 - c)
        chips = [(1 - x, y), (x, 1 - y), (1 - x, 1 - y)]

        def copy(a, k, blk, to, own=False):
            slot = out_refs[a].at[4 * blk[0] + 2 * blk[1] + blk[2]]
            return pltpu.make_async_remote_copy(
                src_ref=x_refs[a] if own else slot, dst_ref=slot,
                send_sem=send_sems.at[7 * a + k], recv_sem=recv_sems.at[7 * a + k], device_id=to, device_id_type=MESH)

        mine = [pltpu.make_async_copy(x_refs[a], out_refs[a].at[4 * x + 2 * y + c], local_sems.at[a]) for a in range(n)]
        for cp in mine:
            cp.start()
        first = []
        for j, chip in enumerate(chips):
            first += [copy(a, 1 + j, me, (*chip, c), own=True) for a in range(n)]
        first += [copy(a, 0, me, sibling, own=True) for a in range(n)]
        for cp in first:
            cp.start()
        passed = []
        for j, chip in enumerate(chips):
            for a in range(n):
                copy(a, 1 + j, (*chip, c), me).wait_recv()
                passed.append(copy(a, 4 + j, (*chip, c), sibling))
                passed[-1].start()
        for a in range(n):
            copy(a, 0, sibling, me).wait_recv()
        for j, chip in enumerate(chips):
            for a in range(n):
                copy(a, 4 + j, (*chip, 1 - c), me).wait_recv()
        for cp in first + passed:
            cp.wait_send()
        for cp in mine:
            cp.wait()

    return pl.pallas_call(
        body, name=name,
        out_shape=[jax.ShapeDtypeStruct((N_DEV,) + b.shape, b.dtype) for b in blocks],
        in_specs=[pl.BlockSpec(memory_space=pl.ANY)] * n,
        out_specs=[pl.BlockSpec(memory_space=pl.ANY)] * n,
        scratch_shapes=[pltpu.SemaphoreType.DMA((7 * n,)), pltpu.SemaphoreType.DMA((7 * n,)), pltpu.SemaphoreType.DMA((n,))],
    )(*blocks)


def _all_to_all(pieces, name):
    n = len(pieces)

    def body(*refs):
        x_refs, out_refs = refs[:n], refs[n:2 * n]
        send_sems, recv_sems, local_sems = refs[2 * n:]
        x, y, c = _my_place()
        me = 4 * x + 2 * y + c
        mine = [pltpu.make_async_copy(x_refs[a].at[me], out_refs[a].at[me], local_sems.at[a]) for a in range(n)]
        for cp in mine:
            cp.start()
        copies = []
        for k in (2, 4, 6, 3, 5, 7, 1):
            px = 1 - x if k & 4 else x
            py = 1 - y if k & 2 else y
            pc = 1 - c if k & 1 else c
            peer = 4 * px + 2 * py + pc
            for a in range(n):
                copies.append(pltpu.make_async_remote_copy(
                    src_ref=x_refs[a].at[peer], dst_ref=out_refs[a].at[me],
                    send_sem=send_sems.at[7 * a + k - 1], recv_sem=recv_sems.at[7 * a + k - 1],
                    device_id=(px, py, pc), device_id_type=MESH))
        for cp in copies:
            cp.start()
        for cp in copies:
            cp.wait_recv()
        for cp in copies:
            cp.wait_send()
        for cp in mine:
            cp.wait()

    return pl.pallas_call(
        body, name=name,
        out_shape=[jax.ShapeDtypeStruct(p.shape, p.dtype) for p in pieces],
        in_specs=[pl.BlockSpec(memory_space=pl.ANY)] * n,
        out_specs=[pl.BlockSpec(memory_space=pl.ANY)] * n,
        scratch_shapes=[pltpu.SemaphoreType.DMA((7 * n,)), pltpu.SemaphoreType.DMA((7 * n,)), pltpu.SemaphoreType.DMA((n,))],
    )(*pieces)


def _peers():
    x, y, c = _my_place()
    out = []
    for k in (2, 4, 6, 3, 5, 7, 1):
        px = 1 - x if k & 4 else x
        py = 1 - y if k & 2 else y
        pc = 1 - c if k & 1 else c
        out.append((k, (px, py, pc), 4 * px + 2 * py + pc))
    return out


def _exchange_copies(x_refs, land_refs, send_sems, recv_sems, scatter):
    x, y, c = _my_place()
    me = 4 * x + 2 * y + c
    starts, arrivals = [], []
    for k, place, peer in _peers():
        for a, (x_ref, land_ref) in enumerate(zip(x_refs, land_refs)):
            sems = dict(send_sem=send_sems.at[7 * a + k - 1], recv_sem=recv_sems.at[7 * a + k - 1],
                        device_id=place, device_id_type=MESH)
            src = x_ref.at[peer] if scatter else x_ref
            starts.append(pltpu.make_async_remote_copy(src_ref=src, dst_ref=land_ref.at[me], **sems))
            arrivals.append(pltpu.make_async_remote_copy(src_ref=src, dst_ref=land_ref.at[peer], **sems))
    return starts, arrivals


def _exchange_start(arrays, scatter, name):
    n = len(arrays)
    hbm = pl.BlockSpec(memory_space=pltpu.HBM)
    sem = pl.BlockSpec(memory_space=pltpu.SEMAPHORE)
    lands = [lax.empty(a.shape if scatter else (N_DEV,) + a.shape, a.dtype) for a in arrays]

    def body(*refs):
        x_refs, land_refs = refs[:n], refs[n:2 * n]
        send_sems, recv_sems = refs[2 * n], refs[2 * n + 1]
        token = refs[-1]
        starts, _ = _exchange_copies(x_refs, land_refs, send_sems, recv_sems, scatter)
        for cp in starts:
            cp.start()
        token[...] = jnp.zeros_like(token)

    res = pl.pallas_call(
        body, name=name,
        out_shape=(pltpu.SemaphoreType.DMA((7 * n,)), pltpu.SemaphoreType.DMA((7 * n,)),
                   *[pltpu.HBM(a.shape, a.dtype) for a in arrays], *[pltpu.HBM(l.shape, l.dtype) for l in lands],
                   jax.ShapeDtypeStruct((8, 128), F32)),
        in_specs=[hbm] * (2 * n),
        out_specs=(sem, sem, *[hbm] * (2 * n), pl.BlockSpec(memory_space=pltpu.VMEM)),
        input_output_aliases={i: 2 + i for i in range(2 * n)},
        compiler_params=pltpu.CompilerParams(has_side_effects=pltpu.SideEffectType.DATAFLOW_SIDE_EFFECTING),
    )(*[pltpu.with_memory_space_constraint(a, pltpu.HBM) for a in arrays],
      *[pltpu.with_memory_space_constraint(l, pltpu.HBM) for l in lands])
    return res[0], res[1], list(res[2:2 + n]), list(res[2 + n:2 + 2 * n]), res[-1]


def _exchange_wait(handles, scatter, after, name):
    send_sems, recv_sems, arrays, lands, _ = handles
    n = len(arrays)
    hbm = pl.BlockSpec(memory_space=pltpu.HBM)
    sem = pl.BlockSpec(memory_space=pltpu.SEMAPHORE)

    def body(*refs):
        x_refs, land_refs = refs[:n], refs[n:2 * n]
        send_s, recv_s = refs[2 * n], refs[2 * n + 1]
        starts, arrivals = _exchange_copies(x_refs, land_refs, send_s, recv_s, scatter)
        for cp in arrivals:
            cp.wait_recv()
        for cp in starts:
            cp.wait_send()

    res = pl.pallas_call(
        body, name=name,
        out_shape=(*[pltpu.HBM(a.shape, a.dtype) for a in arrays], *[pltpu.HBM(l.shape, l.dtype) for l in lands]),
        in_specs=[hbm] * (2 * n) + [sem, sem, pl.BlockSpec(memory_space=pl.ANY)],
        out_specs=tuple([hbm] * (2 * n)),
        input_output_aliases={i: i for i in range(2 * n)},
        compiler_params=pltpu.CompilerParams(has_side_effects=pltpu.SideEffectType.DATAFLOW_SIDE_EFFECTING),
    )(*arrays, *lands, send_sems, recv_sems, after)
    me = 4 * lax.axis_index("x") + 2 * lax.axis_index("y") + lax.axis_index("c")
    out = []
    for src, got in zip(res[:n], res[n:]):
        zeros = (0,) * (got.ndim - 1)
        own = lax.dynamic_slice(src, (me,) + zeros, (1,) + src.shape[1:]) if scatter else src[None]
        out.append(lax.dynamic_update_slice(got, own, (me,) + zeros))
    return out


def _pad_lanes(v, at=0, width=128):
    return jnp.pad(v, ((0, 0), (at, width - at - v.shape[1])))


def _pack_weights(P):
    W = {}
    w = P["w_in"]
    W["wp"] = jnp.concatenate([w[:, :2048], w[:, 2440:2696], w[:, 2056:2440], w[:, 2696:2760], w[:, 2048:2056],
                               jnp.zeros((D_MODEL, N_IN_PACKED - N_IN), w.dtype)], axis=1).astype(BF16)
    W["conv_w"] = P["gdn_conv_w"].astype(F32)
    W["alog_p"] = _pad_lanes(P["gdn_a_log"], 64)
    W["dt_p"] = _pad_lanes(P["gdn_dt_bias"], 64)
    W["gnw"] = P["gdn_norm_w"]
    W["qnw"] = P["mla_q_norm_w"]
    W["kvnw"] = P["mla_kv_norm_w"]
    uq = P["mla_w_uq"].reshape(Q_LORA, HEADS, HEAD_DIM + ROPE)
    W["wuq"] = jnp.pad(uq, ((0, 0), (0, 0), (0, 256 - HEAD_DIM - ROPE))).reshape(Q_LORA, HEADS * 256).astype(BF16)
    ukv = P["mla_w_ukv"].reshape(KV_LORA, HEADS, 2, HEAD_DIM)
    W["wukv"] = ukv.transpose(0, 2, 1, 3).reshape(KV_LORA, 2 * HEADS * HEAD_DIM).astype(BF16)
    W["qn_w"] = P["qkn_q_nope"]
    W["qr_w"] = _pad_lanes(P["qkn_q_rope"])
    W["kn_w"] = P["qkn_k_nope"]
    W["kr_w"] = _pad_lanes(P["qkn_k_rope"])
    W["onw"] = P["mla_out_norm_w"]
    W["wout"] = P["w_out"].astype(BF16)
    return W


def _unpack_grads(G):
    g_qkv, g_z, g_ckv, g_cq, g_kab = G["wp"]
    uq = G["wuq"].reshape(Q_LORA, HEADS, 256)[:, :, :HEAD_DIM + ROPE].reshape(Q_LORA, HEADS * (HEAD_DIM + ROPE))
    ukv = G["wukv"].reshape(KV_LORA, 2, HEADS, HEAD_DIM).transpose(0, 2, 1, 3).reshape(KV_LORA, 2 * HEADS * HEAD_DIM)
    return {
        "w_in": jnp.concatenate([g_qkv, g_z, g_kab[:, ROPE:ROPE + 8], g_cq, g_ckv, g_kab[:, :ROPE]], axis=1),
        "gdn_conv_w": G["conv_w"], "gdn_a_log": G["alog_p"][:, 64:68], "gdn_dt_bias": G["dt_p"][:, 64:68],
        "gdn_norm_w": G["gnw"], "mla_q_norm_w": G["qnw"], "mla_w_uq": uq, "mla_kv_norm_w": G["kvnw"], "mla_w_ukv": ukv,
        "qkn_q_nope": G["qn_w"], "qkn_q_rope": G["qr_w"][:, :ROPE], "qkn_k_nope": G["kn_w"], "qkn_k_rope": G["kr_w"][:, :ROPE],
        "mla_out_norm_w": G["onw"], "w_out": G["wout"],
    }


def _rope_tables(positions):
    half = ROPE // 2
    inv_freq = ROPE_BASE ** (-jnp.arange(half, dtype=F32) / half)
    ang = positions.astype(F32)[:, None] * inv_freq
    cos, sin = jnp.cos(ang), jnp.sin(ang)
    zeros = jnp.zeros((positions.shape[0], 128 - ROPE), F32)
    return jnp.concatenate([cos, cos, zeros], axis=1), jnp.concatenate([-sin, sin, zeros], axis=1)


def _ffn_forward(x, scale, shift, gate_w, w8, wo4, name, target=None):
    S = x.shape[0]
    tm = _pick(S, (512, 256, 128))
    n = S // tm
    with_loss = target is not None

    def body(x_ref, sc_ref, sh_ref, g_ref, wg_ref, wu_ref, wo_ref, *rest):
        t_ref = rest[0] if with_loss else None
        h_ref, bg_ref, bu_ref, ht_ref = rest[with_loss:with_loss + 4]
        tail = rest[with_loss + 4:]
        h_scr, acc_ref = tail[-2:]
        i, p = pl.program_id(0), pl.program_id(1)

        @pl.when(p == 0)
        def _():
            h_new = _modulate(x_ref[...], sc_ref[...], sh_ref[...]).astype(BF16)
            h_scr[...] = h_new
            h_ref[...] = h_new
            acc_ref[...] = jnp.zeros_like(acc_ref)

        h = h_scr[...]
        gate = _dot_raw(h, wg_ref[...], "nn")
        up = _dot_raw(h, wu_ref[...], "nn")
        sg = _sigmoid(gate)
        act = gate * sg
        hid = act * up
        bg_ref[...] = (up * (sg * (1.0 + gate * (1.0 - sg)))).astype(BF16)
        bu_ref[...] = act.astype(BF16)
        ht_ref[...] = jnp.transpose(hid).astype(BF16)
        acc_ref[...] += _dot_raw(hid, wo_ref[...], "nn")

        if with_loss:
            dx_ref, df_ref, dg_ref, l_ref = tail[:4]

            @pl.when((p == 0) & (i == 0))
            def _():
                dg_ref[...] = jnp.zeros_like(dg_ref)
                l_ref[...] = jnp.zeros_like(l_ref)

            @pl.when(p == HID_PIECES - 1)
            def _():
                step = min(EPILOGUE_ROWS, tm)
                for r in range(tm // step):
                    rows = slice(step * r, step * (r + 1))
                    f = acc_ref[rows, :]
                    diff = x_ref[rows, :] + 0.5 * g_ref[...] * f - t_ref[rows, :]
                    dx = diff * (1.0 / D_MODEL)
                    dx_ref[rows, :] = dx
                    df_ref[rows, :] = (0.5 * g_ref[...] * dx).astype(df_ref.dtype)
                    dg_ref[...] += jnp.sum(0.5 * f * dx, axis=0, keepdims=True)
                    l_ref[...] += jnp.sum(diff * diff, axis=0, keepdims=True)

            @pl.when((p == HID_PIECES - 1) & (i == n - 1))
            def _():
                l_ref[...] = jnp.full(l_ref.shape, (0.5 / D_MODEL) * jnp.sum(l_ref[...]), F32)
        else:
            f_ref, xo_ref = tail[:2]

            @pl.when(p == HID_PIECES - 1)
            def _():
                f = acc_ref[...]
                f_ref[...] = f.astype(f_ref.dtype)
                xo_ref[...] = x_ref[...] + 0.5 * g_ref[...] * f

    row = pl.BlockSpec((tm, D_MODEL), lambda i, p: (i, 0))
    par = pl.BlockSpec((1, D_MODEL), lambda i, p: (0, 0))
    piece = pl.BlockSpec((None, tm, FFN_PIECE), lambda i, p: (p, i, 0))
    row_f32, row_bf16 = jax.ShapeDtypeStruct((S, D_MODEL), F32), jax.ShapeDtypeStruct((S, D_MODEL), BF16)
    par_f32 = jax.ShapeDtypeStruct((1, D_MODEL), F32)
    piece_shape = jax.ShapeDtypeStruct((HID_PIECES, S, FFN_PIECE), BF16)
    return pl.pallas_call(
        body, name=name, grid=(n, HID_PIECES),
        in_specs=[row, par, par, par,
                  pl.BlockSpec((None, D_MODEL, FFN_PIECE), lambda i, p: (p, 0, 0)),
                  pl.BlockSpec((None, D_MODEL, FFN_PIECE), lambda i, p: (p + HID_PIECES, 0, 0)),
                  pl.BlockSpec((None, FFN_PIECE, D_MODEL), lambda i, p: (p, 0, 0))] + [row] * with_loss,
        out_specs=[row, piece, piece, pl.BlockSpec((None, FFN_PIECE, tm), lambda i, p: (p, 0, i))]
        + ([row, row, par, par] if with_loss else [row, row]),
        out_shape=[row_bf16, piece_shape, piece_shape, jax.ShapeDtypeStruct((HID_PIECES, FFN_PIECE, S), BF16)]
        + ([row_f32, row_bf16, par_f32, par_f32] if with_loss else [row_bf16, row_f32]),
        scratch_shapes=[pltpu.VMEM((tm, D_MODEL), BF16), pltpu.VMEM((tm, D_MODEL), F32)],
        compiler_params=_params(("arbitrary", "arbitrary")),
    )(x, scale, shift, gate_w, w8, w8, wo4, *([target] if with_loss else []))


def _ffn_fwd(x, scale, shift, gate_w, w8, wo4, tag, target=None):
    res = _ffn_forward(x, scale, shift, gate_w, w8, wo4, tag + "_fwd", target)
    h, by_gate, by_up, hid_t = res[:4]
    if target is not None:
        dx_out, df, d_gate_w, loss_row = res[4:]
        return (dx_out, loss_row), (h, by_gate, by_up, hid_t, None, df, d_gate_w)
    f, x_out = res[4:]
    return x_out, (h, by_gate, by_up, hid_t, f, None, None)


def _ffn_bwd(d_out, x, scale, shift, gate_w, w8, wo4, saved, tag, grad_ready, below=None):
    h, gate, up, hid_t, f, df, d_gate_w = saved
    S = x.shape[0]
    tm = _pick(S, (512, 256, 128))
    tk = _pick(S, (512, 256, 128))
    n = S // tm
    if df is None:
        (df,), (d_gate_w,) = _rowwise_bwd(lambda f_, g_: (0.5 * g_ * f_,), [(f, tm, D_MODEL, 0)], [], [gate_w],
                                          [(d_out, tm, D_MODEL, 0)], n, tag + "_dres", row_dtypes=(BF16,))
    tb = _pick(S, MATMUL_ROWS)
    piece = pl.BlockSpec((None, tb, FFN_PIECE), lambda i, j, k: (j, i, 0))
    d_gate, d_up = _mmg(df, wo4, "nt", name=tag + "_ddown", grid=(S // tb, HID_PIECES, 1),
                        a_spec=pl.BlockSpec((tb, D_MODEL), lambda i, j, k: (i, 0)),
                        b_spec=pl.BlockSpec((None, FFN_PIECE, D_MODEL), lambda i, j, k: (j, 0, 0)),
                        out_spec=piece, out_shapes=[jax.ShapeDtypeStruct((HID_PIECES, S, FFN_PIECE), BF16)] * 2,
                        acc_shape=(tb, FFN_PIECE), extras=[gate, up], extra_specs=[piece, piece], epi=_swiglu_bwd)
    tk = _pick(S, MATMUL_ROWS)
    g_wo4 = _mmg(hid_t, df, "nn", name=tag + "_gwo", grid=(HID_PIECES, 1, S // tk),
                 a_spec=pl.BlockSpec((None, FFN_PIECE, tk), lambda i, j, k: (i, 0, k)),
                 b_spec=pl.BlockSpec((tk, D_MODEL), lambda i, j, k: (k, j)),
                 out_spec=pl.BlockSpec((None, FFN_PIECE, D_MODEL), lambda i, j, k: (i, 0, j)),
                 out_shapes=[jax.ShapeDtypeStruct((HID_PIECES, FFN_PIECE, D_MODEL), BF16)], acc_shape=(FFN_PIECE, D_MODEL))
    g_w8 = _ffn_gw8(h, d_gate, d_up, tag + "_gw8", after=grad_ready("wo4", g_wo4))
    res = _ffn_dh(d_gate, d_up, w8, x, d_out, scale, shift, tag + "_dh", after=grad_ready("w8", g_w8), below=below)
    return (res[0], res[1], res[2], d_gate_w) + tuple(res[3:])


def _dproj_dmod(d_pieces, wp, x, d_out, scale, shift, name, below=None):
    S = x.shape[0]
    tm = _pick(S, TOKEN_ROWS)
    widths = [p.shape[1] for p in d_pieces]
    starts = [sum(widths[:n]) for n in range(len(widths))]
    n_p = len(d_pieces)
    n_below = 0 if below is None else 2

    def body(*refs):
        dp_refs, (w_ref, x_ref, do_ref, sc_ref, sh_ref), rest = refs[:n_p], refs[n_p:n_p + 5], refs[n_p + 5:]
        below_in = rest[:n_below]
        dx_ref, dsc_ref, dsh_ref = rest[n_below:n_below + 3]
        below_out, acc_ref = rest[n_below + 3:n_below + 3 + n_below], rest[-1]

        @pl.when(pl.program_id(0) == 0)
        def _():
            for r in (dsc_ref, dsh_ref) + tuple(below_out[1:]):
                r[...] = jnp.zeros_like(r)

        acc = None
        for dp_ref, at, width in zip(dp_refs, starts, widths):
            part = _dot_raw(dp_ref[...], w_ref[:, at:at + width], "nt")
            acc = part if acc is None else acc + part
        acc_ref[...] = acc
        _dmod_epilogue(acc_ref, x_ref, do_ref, sc_ref, sh_ref, dx_ref, dsc_ref, dsh_ref, below, below_in, below_out)

    row = pl.BlockSpec((tm, D_MODEL), lambda i: (i, 0))
    par = pl.BlockSpec((1, D_MODEL), lambda i: (0, 0))
    row_shape, par_shape = jax.ShapeDtypeStruct((S, D_MODEL), F32), jax.ShapeDtypeStruct((1, D_MODEL), F32)
    return pl.pallas_call(
        body, name=name, grid=(S // tm,),
        in_specs=[pl.BlockSpec((tm, width), lambda i: (i, 0)) for width in widths]
        + [pl.BlockSpec(wp.shape, lambda i: (0, 0)), row, row, par, par] + [row, par][:n_below],
        out_specs=[row, par, par] + [row, par][:n_below],
        out_shape=[row_shape, par_shape, par_shape] + [jax.ShapeDtypeStruct((S, D_MODEL), BF16), par_shape][:n_below],
        scratch_shapes=[pltpu.VMEM((tm, D_MODEL), F32)],
        compiler_params=_params(("arbitrary",)),
    )(*d_pieces, wp, x, d_out, scale, shift, *(below[:2] if below is not None else ()))


def _gw_pieces(h, d_pieces, name):
    S = h.shape[0]
    tm = 512
    tk = _pick(S, MATMUL_ROWS)
    n_p = len(d_pieces)
    widths = [p.shape[1] for p in d_pieces]

    def body(h_ref, *rest):
        d_refs, o_refs = rest[:n_p], rest[n_p:]

        @pl.when(pl.program_id(1) == 0)
        def _():
            for o_ref in o_refs:
                o_ref[...] = jnp.zeros_like(o_ref)

        h_t = jnp.transpose(h_ref[...])
        for d_ref, o_ref in zip(d_refs, o_refs):
            o_ref[...] += _dot_raw(h_t, d_ref[...], "nn")

    return pl.pallas_call(
        body, name=name, grid=(D_MODEL // tm, S // tk),
        in_specs=[pl.BlockSpec((tk, tm), lambda i, k: (k, i))] + [pl.BlockSpec((tk, width), lambda i, k: (k, 0)) for width in widths],
        out_specs=[pl.BlockSpec((tm, width), lambda i, k: (i, 0)) for width in widths],
        out_shape=[jax.ShapeDtypeStruct((D_MODEL, width), F32) for width in widths],
        compiler_params=_params(("parallel", "arbitrary")),
    )(h, *d_pieces)


def _mod_proj(x, scale, shift, wp, name):
    S = x.shape[0]
    N = wp.shape[1]
    tm = _pick(S, MATMUL_ROWS)
    tn = _pick(N, (1408, 1024, 512, 256, 128))

    def body(x_ref, sc_ref, sh_ref, w_ref, h_ref, o_ref, h_scr):
        @pl.when(pl.program_id(1) == 0)
        def _():
            h_new = _modulate(x_ref[...], sc_ref[...], sh_ref[...]).astype(BF16)
            h_scr[...] = h_new
            h_ref[...] = h_new

        o_ref[...] = _dot_raw(h_scr[...], w_ref[...], "nn")

    row = pl.BlockSpec((tm, D_MODEL), lambda i, j: (i, 0))
    par = pl.BlockSpec((1, D_MODEL), lambda i, j: (0, 0))
    return pl.pallas_call(
        body, name=name, grid=(S // tm, N // tn),
        in_specs=[row, par, par, pl.BlockSpec((D_MODEL, tn), lambda i, j: (0, j))],
        out_specs=[row, pl.BlockSpec((tm, tn), lambda i, j: (i, j))],
        out_shape=[jax.ShapeDtypeStruct((S, D_MODEL), BF16), jax.ShapeDtypeStruct((S, N), F32)],
        scratch_shapes=[pltpu.VMEM((tm, D_MODEL), BF16)],
        compiler_params=_params(("parallel", "arbitrary")),
    )(x, scale, shift, wp)


def _mixer_fwd(x1, scale, shift, gate_w, cos_p, sin_p, W):
    S = x1.shape[0]
    tm = _pick(S, (512, 256, 128))
    tv = _pick(S, TOKEN_ROWS)
    ta = _pick(S, (512, 256, 128))
    nc = S // CHUNK
    h2, proj = _mod_proj(x1, scale, shift, W["wp"], "mix_proj")
    qkvc = _conv_fwd(proj, W["conv_w"], tv, "gdn_conv")
    kab = (proj, tv, 128, 21)
    q_a, k_a, v_a, gb = _rowwise(_gdn_pre_fn, [(qkvc, tv, 1536, 0), kab], [W["alog_p"], W["dt_p"]],
                                 [(tv, 512, F32)] * 3 + [(tv, 128, F32)], S // tv, "gdn_pre")
    ti = _pick(S, INTRA_ROWS)
    intra = _rowwise(_gdn_intra_fn, [(q_a, ti, 512, 0), (k_a, ti, 512, 0), (v_a, ti, 512, 0), (gb, ti, 128, 0)],
                     [], [(ti, 512, F32)] * 4 + [(ti, CHUNK, F32)] * 4 + [(ti // 8, 512, F32)] + [(ti, CHUNK, F32)] * 4,
                     S // ti, "gdn_intra")
    u, wk, qd, kd, qks, gl, invs = intra[0], intra[1], intra[2], intra[3], tuple(intra[4:8]), intra[8], tuple(intra[9:])
    o_a, s_prev = _gdn_scan_fwd(u, wk, qd, kd, qks, gl, "gdn_scan")
    mla_params = [W["qnw"], W["kvnw"], W["wuq"], W["wukv"], W["qn_w"], W["qr_w"], W["kn_w"], W["kr_w"]]
    def mla_pre_with_vt(*a):
        q_, k_, v_ = _mla_pre_fn(*a)
        return q_, k_, v_, jnp.transpose(v_)

    q_b, k_b, v_b, vt_b = _rowwise(mla_pre_with_vt,
                                   [(proj, tv, 256, 8), (proj, tv, 384, 6), kab, (cos_p, tv, 128, 0), (sin_p, tv, 128, 0)],
                                   mla_params, [(tv, 1024, BF16), (tv, 1024, BF16), (tv, 512, BF16), (512, tv, BF16, "across")],
                                   S // tv, "mla_pre")
    o_b, lse = _attn_fwd(q_b, k_b, vt_b, ta, "mla_attn")
    (mixed,) = _rowwise(_mix_post_fn, [(o_a, tv, 512, 0), (proj, tv, 512, 3), (o_b, tv, 512, 0)], [W["gnw"], W["onw"]],
                        [(tv, D_MODEL, BF16)], S // tv, "mix_post")
    y, x2 = _mm(mixed, W["wout"], "nn", name="mix_out", out_dtypes=(BF16, F32), extras=[x1], extra_params=[gate_w],
                epi=lambda acc, x_, g_: (acc, x_ + g_ * acc))
    saved = (h2, proj, qkvc, q_a, k_a, v_a, gb, u, wk, qd, kd, qks, gl, invs, s_prev, o_a, q_b, k_b, v_b, o_b, lse, mixed, y)
    return x2, saved


def _mixer_bwd(d_out, dy, x1, scale, shift, cos_p, sin_p, W, saved, below):
    (h2, proj, qkvc, q_a, k_a, v_a, gb, u, wk, qd, kd, qks, gl, invs, s_prev, o_a, q_b, k_b, v_b, o_b, lse, mixed, y) = saved
    S = x1.shape[0]
    tm = _pick(S, (512, 256, 128))
    tv = _pick(S, TOKEN_ROWS)
    ta = _pick(S, (512, 256, 128))
    nc = S // CHUNK
    G = {}
    d_mixed = _mm(dy, W["wout"], "nt", name="mix_dout")
    G["wout"] = _mm(mixed, dy, "tn", name="mix_gwout")
    (do_a, dz, do_b), (G["gnw"], G["onw"], stats) = _rowwise_bwd(
        _mix_post_fn, [(o_a, tv, 512, 0), (proj, tv, 512, 3), (o_b, tv, 512, 0)], [], [W["gnw"], W["onw"]],
        [(d_mixed, tv, D_MODEL, 0)], S // tv, "mix_dpost", row_dtypes=(F32, BF16, F32),
        across=(lse, lambda r_vals, d_rows, lse_tile: _attn_stats(r_vals[2], d_rows[2], lse_tile)))
    dq_b, dk_b, dv_b = _attn_bwd(q_b, k_b, v_b, do_b, stats, ta, "mla_dattn")
    kab = (proj, tv, 128, 21)
    mla_params = [W["qnw"], W["kvnw"], W["wuq"], W["wukv"], W["qn_w"], W["qr_w"], W["kn_w"], W["kr_w"]]
    (d_ckv, d_cq, d_kab), mla_grads = _rowwise_bwd(
        _mla_pre_fn, [(proj, tv, 256, 8), (proj, tv, 384, 6), kab], [(cos_p, tv, 128, 0), (sin_p, tv, 128, 0)], mla_params,
        [(dq_b, tv, 1024, 0), (dk_b, tv, 1024, 0), (dv_b, tv, 512, 0)], S // tv, "mla_dpre", row_dtypes=(BF16, BF16, F32))
    for key, g in zip(("qnw", "kvnw", "wuq", "wukv", "qn_w", "qr_w", "kn_w", "kr_w"), mla_grads):
        G[key] = g
    scan_grads = _gdn_scan_bwd(u, wk, qd, kd, qks, gl, s_prev, do_a, "gdn_dscan")
    ti = _pick(S, INTRA_ROWS)
    intra_douts = [(scan_grads[i], ti, 512, 0) for i in range(4)] + [(scan_grads[4 + i], ti, CHUNK, 0) for i in range(4)]
    intra_douts.append((scan_grads[8], ti // 8, 512, 0))
    (dq_a, dk_a, dv_a, d_gb), _ = _rowwise_bwd(
        _gdn_intra_fn, [(q_a, ti, 512, 0), (k_a, ti, 512, 0), (v_a, ti, 512, 0), (gb, ti, 128, 0)],
        [(x_, ti, CHUNK, 0) for x_ in invs], [], intra_douts, S // ti, "gdn_dintra")
    (d_qkvc, d_kab), (G["alog_p"], G["dt_p"]) = _rowwise_bwd(
        _gdn_pre_fn, [(qkvc, tv, 1536, 0), kab], [], [W["alog_p"], W["dt_p"]],
        [(dq_a, tv, 512, 0), (dk_a, tv, 512, 0), (dv_a, tv, 512, 0), (d_gb, tv, 128, 0)], S // tv, "gdn_dpre",
        adds=[(1, d_kab)], row_dtypes=(F32, BF16))
    d_qkv, g_conv = _conv_bwd(proj, d_qkvc, W["conv_w"], tv, "gdn_dconv")
    G["conv_w"] = g_conv[:4]
    d_proj = [d_qkv, dz, d_ckv, d_cq, d_kab]
    G["wp"] = _gw_pieces(h2, d_proj, "mix_gwp")
    dx1, G["s2"], G["sh2"], d_below, dg_below = _dproj_dmod(d_proj, W["wp"], x1, d_out, scale, shift, "mix_dproj", below=below)
    return dx1, d_below, dg_below, G


def _local_step(x, target, mod, cos_p, sin_p, W1, mixer_weights, ffn2_weights, ffn_grad_ready, mixer_grads_ready):
    sh1, s1, g1, sh2, s2, g2, sh3, s3, g3 = [mod[:, D_MODEL * i:D_MODEL * (i + 1)] for i in range(N_MOD)]
    x1, saved1 = _ffn_fwd(x, s1, sh1, g1, W1["f1_w8"], W1["f1_wo4"], "ffn1")
    W = mixer_weights(x1)
    x2, saved2 = _mixer_fwd(x1, s2, sh2, g2, cos_p, sin_p, W)
    W.update(ffn2_weights(x2))
    (dx3, loss_row), saved3 = _ffn_fwd(x2, s3, sh3, g3, W["f2_w8"], W["f2_wo4"], "ffn2", target=target)
    dx2, d_s3, d_sh3, d_g3, dy, d_g2 = _ffn_bwd(dx3, x2, s3, sh3, g3, W["f2_w8"], W["f2_wo4"], saved3, "ffn2",
                                                ffn_grad_ready("f2"), below=(saved2[-1], g2, 1.0))
    dx1, df1, d_g1, G = _mixer_bwd(dx2, dy, x1, s2, sh2, cos_p, sin_p, W, saved2, below=(saved1[4], g1, 0.5))
    d_sh2, d_s2 = G.pop("sh2"), G.pop("s2")
    saved1 = saved1[:5] + (df1, d_g1 + mixer_grads_ready(G))
    dx, d_s1, d_sh1, d_g1 = _ffn_bwd(dx1, x, s1, sh1, g1, W1["f1_w8"], W1["f1_wo4"], saved1, "ffn1", ffn_grad_ready("f1"))
    d_mod = jnp.concatenate([d_sh1, d_s1, d_g1, d_sh2, d_s2, d_g2, d_sh3, d_s3, d_g3], axis=1)
    return loss_row, dx, d_mod


WEIGHT_NAMES = ("w_ada", "b_ada", "ffn1_w_in", "ffn1_w_out", "w_in", "gdn_conv_w", "gdn_a_log", "gdn_dt_bias", "gdn_norm_w",
                "mla_q_norm_w", "mla_w_uq", "mla_kv_norm_w", "mla_w_ukv", "qkn_q_nope", "qkn_q_rope", "qkn_k_nope",
                "qkn_k_rope", "mla_out_norm_w", "w_out", "ffn2_w_in", "ffn2_w_out")
FFN_SHARDED = ("ffn1_w_in", "ffn1_w_out", "ffn2_w_in", "ffn2_w_out")
TRANSPOSED_ENTRY = ("ffn1_w_in", "ffn2_w_in", "w_in", "mla_w_uq")
SHEETED = (("w_in", "col"), ("gdn_conv_w", "col"), ("mla_w_uq", "col"), ("mla_w_ukv", "col"), ("w_out", "row"))
MOD_ROWS = N_MOD * D_MODEL // 128
SMALL = {"gdn_a_log": (MOD_ROWS, 1, 64, 4), "gdn_dt_bias": (MOD_ROWS + 1, 1, 64, 4), "gdn_norm_w": (MOD_ROWS + 2, 1, 0, 128),
         "mla_q_norm_w": (MOD_ROWS + 3, 3, 0, 384), "mla_kv_norm_w": (MOD_ROWS + 6, 2, 0, 256),
         "qkn_q_nope": (MOD_ROWS + 8, 1, 0, 128), "qkn_q_rope": (MOD_ROWS + 9, 1, 0, 64), "qkn_k_nope": (MOD_ROWS + 10, 1, 0, 128),
         "qkn_k_rope": (MOD_ROWS + 11, 1, 0, 64), "mla_out_norm_w": (MOD_ROWS + 12, 1, 0, 128)}
LOSS_ROW = MOD_ROWS + 13
CONV_ROW, CONV_ROWS = 88, 4 * 1536 // 128
SHEET_ROWS = CONV_ROW + CONV_ROWS


def _to_sheet(flat, dtype, sublanes):
    n = flat.shape[-1]
    unit = sublanes * 128
    pad = (-n) % unit
    flat = jnp.pad(flat.astype(dtype), [(0, 0)] * (flat.ndim - 1) + [(0, pad)])
    return flat.reshape(flat.shape[:-1] + ((n + pad) // 128, 128))


def _small_sheet(b_like, small):
    sheet = jnp.zeros((SHEET_ROWS, 128), F32).at[:MOD_ROWS].set(b_like.reshape(MOD_ROWS, 128))
    for name, (row, rows, lane, n) in SMALL.items():
        v = small[name].reshape(1, n)
        if rows == 1:
            sheet = sheet.at[row, lane:lane + n].set(v[0])
        else:
            sheet = sheet.at[row:row + rows].set(v.reshape(rows, 128))
    return sheet


def _from_small_sheet(sheet):
    out = {"b_ada": sheet[:MOD_ROWS].reshape(1, N_MOD * D_MODEL)}
    for name, (row, rows, lane, n) in SMALL.items():
        out[name] = sheet[row, lane:lane + n].reshape(1, n) if rows == 1 else sheet[row:row + rows].reshape(1, n)
    return out


def kernel(x, c, positions, w_ada, b_ada, ffn1_w_in, ffn1_w_out, w_in, gdn_conv_w, gdn_a_log, gdn_dt_bias, gdn_norm_w, mla_q_norm_w, mla_w_uq, mla_kv_norm_w, mla_w_ukv, qkn_q_nope, qkn_q_rope, qkn_k_nope, qkn_k_rope, mla_out_norm_w, w_out, ffn2_w_in, ffn2_w_out, loss_target, m_w_ada, m_b_ada, m_ffn1_w_in, m_ffn1_w_out, m_w_in, m_gdn_conv_w, m_gdn_a_log, m_gdn_dt_bias, m_gdn_norm_w, m_mla_q_norm_w, m_mla_w_uq, m_mla_kv_norm_w, m_mla_w_ukv, m_qkn_q_nope, m_qkn_q_rope, m_qkn_k_nope, m_qkn_k_rope, m_mla_out_norm_w, m_w_out, m_ffn2_w_in, m_ffn2_w_out, v_w_ada, v_b_ada, v_ffn1_w_in, v_ffn1_w_out, v_w_in, v_gdn_conv_w, v_gdn_a_log, v_gdn_dt_bias, v_gdn_norm_w, v_mla_q_norm_w, v_mla_w_uq, v_mla_kv_norm_w, v_mla_w_ukv, v_qkn_q_nope, v_qkn_q_rope, v_qkn_k_nope, v_qkn_k_rope, v_mla_out_norm_w, v_w_out, v_ffn2_w_in, v_ffn2_w_out):
    args = locals()
    w = {n: args[n] for n in WEIGHT_NAMES}
    m = {n: args["m_" + n] for n in WEIGHT_NAMES}
    v = {n: args["v_" + n] for n in WEIGHT_NAMES}
    me = 4 * lax.axis_index("x") + 2 * lax.axis_index("y") + lax.axis_index("c")
    cols = N_MOD * D_MODEL // N_DEV
    shard = {n: w[n][0] for n in FFN_SHARDED + tuple(s[0] for s in SHEETED)}

    sc = c * _sigmoid(c)
    first = _to_sheet(jnp.concatenate([sc.reshape(-1), shard["gdn_conv_w"].reshape(-1)]), F32, 8)
    (first_all,) = _all_gather([first], "gather_c")
    sc_all = first_all[:, :D_MODEL // 128].reshape(N_DEV, D_MODEL)
    n_taps = shard["gdn_conv_w"].size
    conv_all = first_all.reshape(N_DEV, -1)[:, D_MODEL:D_MODEL + n_taps].reshape(N_DEV, 4, -1)
    b_mine = lax.dynamic_slice(b_ada, (0, me * cols), (1, cols))
    mod_cols = _mm(sc_all, w_ada[0], "nn", name="ada_mod", extra_params=[b_mine], epi=lambda acc, b_: (acc + b_,))
    (mod_all,) = _all_to_all([_to_sheet(mod_cols, F32, 8)], "scatter_mod")
    mod = mod_all.reshape(N_DEV, -1)[:, :cols].reshape(1, N_MOD * D_MODEL)

    f1_shards, mod = lax.optimization_barrier(([shard["ffn1_w_in"].astype(BF16), shard["ffn1_w_out"].astype(BF16)], mod))
    f1_w8, f1_out = _all_gather(f1_shards, "gather_w1")
    travel = [s for s in SHEETED if s[0] != "gdn_conv_w"]
    tied = lax.optimization_barrier(([shard[n].astype(BF16) for n, _ in travel], f1_w8))
    f1_w8 = tied[1]
    mixer_w = _exchange_start(tied[0], False, "gather_wm_start")
    ffn2_w = _exchange_start([shard["ffn2_w_in"].astype(BF16) + mixer_w[4][0:1, 0:1].astype(BF16),
                              shard["ffn2_w_out"].astype(BF16)], False, "gather_w2_start")
    mod = mod + ffn2_w[4][0:1, 0:1]
    W1 = dict(f1_w8=f1_w8, f1_wo4=f1_out.reshape(HID_PIECES, FFN_PIECE, D_MODEL))

    def mixer_weights(after):
        got = _exchange_wait(mixer_w, False, after, "gather_wm_wait")
        P = {n: jnp.concatenate(list(g), axis=1) if kind == "col" else g.reshape(-1, g.shape[-1])
             for (n, kind), g in zip(travel, got)}
        P["gdn_conv_w"] = jnp.concatenate(list(conv_all), axis=1)
        for n in SMALL:
            P[n] = w[n]
        return _pack_weights(P)

    def ffn2_weights(after):
        f2_w8, f2_out = _exchange_wait(ffn2_w, False, after, "gather_w2_wait")
        return dict(f2_w8=f2_w8, f2_wo4=f2_out.reshape(HID_PIECES, FFN_PIECE, D_MODEL))

    pending, small_grads = {}, {}

    def ffn_grad_ready(tag):
        def ready(which, g):
            pieces = g if which == "w8" else g.reshape((N_DEV,) + shard["ffn1_w_out"].shape)
            pending[tag + which] = _exchange_start([pieces], True, "scatter_%s_%s_start" % (tag, which))
            return pending[tag + which][4]
        return ready

    def mixer_grads_ready(G):
        g_full = _unpack_grads(G)
        small_grads.update({n: g_full[n] for n in SMALL})
        small_grads["gdn_conv_w"] = g_full["gdn_conv_w"]
        pieces = []
        for n, kind in travel:
            r, cc = shard[n].shape
            g = g_full[n].astype(BF16)
            pieces.append(jnp.stack([g[:, cc * p:cc * (p + 1)] for p in range(N_DEV)]) if kind == "col"
                          else g.reshape(N_DEV, r, cc))
        pending["mixer"] = _exchange_start(pieces, True, "scatter_mx_start")
        return pending["mixer"][4][0:1, 0:1]

    cos_p, sin_p = _rope_tables(positions[0])
    loss_row, dx, d_mod = _local_step(x[0], loss_target[0], mod, cos_p, sin_p, W1, mixer_weights, ffn2_weights,
                                      ffn_grad_ready, mixer_grads_ready)

    sheet = _small_sheet(d_mod, small_grads).at[LOSS_ROW].set(loss_row[0, :128])
    sheet = sheet.at[CONV_ROW:CONV_ROW + CONV_ROWS].set(small_grads["gdn_conv_w"].reshape(CONV_ROWS, 128))
    (sheets,) = _all_gather([sheet], "gather_small")
    summed = _sum_devices(sheets, "sum_small")
    d_mod_all = sheets[:, :MOD_ROWS].reshape(N_DEV, N_MOD * D_MODEL)
    d_mod_mine = lax.dynamic_slice(d_mod_all, (0, me * cols), (N_DEV, cols))
    grads = _from_small_sheet(summed)
    grads["w_ada"] = _mm(sc_all, d_mod_mine, "tn", name="ada_gw", hi=True)
    conv_taps = shard["gdn_conv_w"].shape[1]
    grads["gdn_conv_w"] = lax.dynamic_slice(summed[CONV_ROW:CONV_ROW + CONV_ROWS].reshape(4, -1), (0, me * conv_taps),
                                            (4, conv_taps))
    loss = summed[LOSS_ROW, 0]

    delta, new_m, new_v = {}, {}, {}
    arrived = {}
    for n, key in zip(FFN_SHARDED, ("f1w8", "f1wo4", "f2w8", "f2wo4")):
        (arrived[n],) = _exchange_wait(pending[key], True, summed, "scatter_%s_wait" % key)
    arrived.update(zip([n for n, _ in travel], _exchange_wait(pending["mixer"], True, summed, "scatter_mx_wait")))
    for n, parts in arrived.items():
        if n in TRANSPOSED_ENTRY:
            res = _sum_adamw(parts, w[n][0].T, m[n][0].T, v[n][0].T, "adamw_" + n, transposed=True)
            grads[n], delta[n], new_m[n], new_v[n] = [r.T for r in res]
        else:
            grads[n], delta[n], new_m[n], new_v[n] = _sum_adamw(parts, w[n][0], m[n][0], v[n][0], "adamw_" + n)
    for n in ("w_ada", "gdn_conv_w"):
        delta[n], new_m[n], new_v[n] = _adamw(w[n][0], grads[n], m[n][0], v[n][0], "adamw_" + n)
    small_in = [_small_sheet(t["b_ada"], t) for t in (w, grads, m, v)]
    for res, out in zip(_adamw(*small_in, "adamw_small"), (delta, new_m, new_v)):
        out.update(_from_small_sheet(res))

    def shaped(d):
        return [d[n].reshape(w[n].shape) for n in WEIGHT_NAMES]

    return (loss, dx[None], *shaped(grads), *shaped(delta), *shaped(new_m), *shaped(new_v))
```

```python
import functools

import jax
import jax.numpy as jnp
import numpy as np
from jax import lax
from jax.experimental import pallas as pl
from jax.experimental.pallas import tpu as pltpu

F32 = jnp.float32
BF16 = jnp.bfloat16

D_MODEL = 1024
D_FF = 2816
N_MOD = 9
HEADS = 4
HEAD_DIM = 128
CHUNK = 64
EPS = 1e-6
ROPE = 64
Q_LORA = 384
KV_LORA = 256
N_IN = 2760
N_IN_PACKED = 2816
ROPE_BASE = 10000.0
LOG2_E = 1.4426950408889634
N_DEV = 8

ADAM_LR = 0.001
ADAM_B1 = 0.9
ADAM_B2 = 0.999
ADAM_EPS = 1e-08
ADAM_WD = 0.01
ADAM_STEP = 10

VMEM_LIMIT_BYTES = 56 * 1024 * 1024
MATMUL_ROWS = (1024, 512, 256, 128)
MESH = pl.DeviceIdType.MESH


def _params(sem=None):
    return pltpu.CompilerParams(dimension_semantics=sem, vmem_limit_bytes=VMEM_LIMIT_BYTES)


def _pick(dim, prefs):
    for p in prefs:
        if dim % p == 0:
            return p
    return dim


_DIMS = {"nn": (((1,), (0,)), ((), ())), "nt": (((1,), (1,)), ((), ())), "tn": (((0,), (0,)), ((), ()))}


def _dot_raw(a, b, mode):
    return lax.dot_general(a.astype(BF16), b.astype(BF16), _DIMS[mode], preferred_element_type=F32)


def _dot_hi(a, b, mode="nn"):
    return lax.dot_general(a, b, _DIMS[mode], precision=lax.Precision.HIGHEST, preferred_element_type=F32)


@functools.partial(jax.custom_vjp, nondiff_argnums=(2,))
def _bdot(a, b, mode):
    return _dot_raw(a, b, mode)


def _bdot_fwd(a, b, mode):
    return _dot_raw(a, b, mode), (a, b)


def _bdot_bwd(mode, res, g):
    a, b = res
    if mode == "nn":
        return _dot_raw(g, b, "nt"), _dot_raw(a, g, "tn")
    if mode == "nt":
        return _dot_raw(g, b, "nn"), _dot_raw(g, a, "tn")
    return _dot_raw(b, g, "nt"), _dot_raw(a, g, "nn")


_bdot.defvjp(_bdot_fwd, _bdot_bwd)


def _mm(a, b, mode, *, name, out_dtypes=(F32,), epi=None, extras=(), extra_params=(), hi=False,
        tm=None, tn=None, tk=None):
    if mode == "nn":
        (M, K), (_, N) = a.shape, b.shape
    elif mode == "nt":
        (M, K), (N, _) = a.shape, b.shape
    else:
        (K, M), (_, N) = a.shape, b.shape
    tm = tm or _pick(M, (512, 1408, 256, 128) if mode == "tn" else MATMUL_ROWS + (384, 352))
    tn = tn or _pick(N, (1024, 1408, 768, 512, 384, 256, 128))
    tk = tk or _pick(K, (1024, 1408, 512, 384, 256, 128))
    a_spec = {"nn": pl.BlockSpec((tm, tk), lambda i, j, k: (i, k)), "nt": pl.BlockSpec((tm, tk), lambda i, j, k: (i, k)),
              "tn": pl.BlockSpec((tk, tm), lambda i, j, k: (k, i))}[mode]
    b_spec = {"nn": pl.BlockSpec((tk, tn), lambda i, j, k: (k, j)), "nt": pl.BlockSpec((tn, tk), lambda i, j, k: (j, k)),
              "tn": pl.BlockSpec((tk, tn), lambda i, j, k: (k, j))}[mode]
    mn_spec = pl.BlockSpec((tm, tn), lambda i, j, k: (i, j))
    return _mmg(a, b, mode, name=name, grid=(M // tm, N // tn, K // tk), a_spec=a_spec, b_spec=b_spec, out_spec=mn_spec,
                out_shapes=[jax.ShapeDtypeStruct((M, N), dt) for dt in out_dtypes], acc_shape=(tm, tn), epi=epi,
                extras=list(extras) + list(extra_params),
                extra_specs=[mn_spec] * len(extras) + [pl.BlockSpec((1, tn), lambda i, j, k: (0, j))] * len(extra_params),
                hi=hi)


def _mmg(a, b, mode, *, name, grid, a_spec, b_spec, out_spec, out_shapes, acc_shape, epi=None, extras=(),
         extra_specs=(), hi=False):
    nk = grid[2]
    n_e, n_o = len(extras), len(out_shapes)

    def body(*refs):
        a_ref, b_ref = refs[:2]
        e_refs = refs[2:2 + n_e]
        o_refs = refs[2 + n_e:2 + n_e + n_o]
        acc_ref = refs[-1]
        k = pl.program_id(2)

        @pl.when(k == 0)
        def _():
            acc_ref[...] = jnp.zeros_like(acc_ref)

        if hi:
            acc_ref[...] += _dot_hi(a_ref[...].astype(F32), b_ref[...].astype(F32), mode)
        else:
            acc_ref[...] += _dot_raw(a_ref[...], b_ref[...], mode)

        @pl.when(k == nk - 1)
        def _():
            acc = acc_ref[...]
            outs = (acc,) if epi is None else epi(acc, *[e[...].astype(F32) for e in e_refs])
            for o_ref, o in zip(o_refs, outs):
                o_ref[...] = o.astype(o_ref.dtype)

    outs = pl.pallas_call(
        body, name=name, grid=grid,
        in_specs=[a_spec, b_spec] + list(extra_specs),
        out_specs=[out_spec] * n_o,
        out_shape=list(out_shapes),
        scratch_shapes=[pltpu.VMEM(acc_shape, F32)],
        compiler_params=_params(("parallel", "parallel", "arbitrary")),
    )(a, b, *extras)
    return outs if n_o > 1 else outs[0]


def _row_spec(th, cw, ci):
    return pl.BlockSpec((th, cw), lambda i: (i, ci))


def _full_spec(shape):
    return pl.BlockSpec(shape, lambda i: (0,) * len(shape))


def _rowwise(fn, rows, params, outs, n_steps, name):
    n_r, n_p, n_o = len(rows), len(params), len(outs)

    def body(*refs):
        vals = [r[...].astype(F32) for r in refs[:n_r + n_p]]
        res = fn(*vals)
        for o_ref, o in zip(refs[n_r + n_p:], res):
            o_ref[...] = o.astype(o_ref.dtype)

    across = [len(o) == 4 for o in outs]
    res = pl.pallas_call(
        body, name=name, grid=(n_steps,),
        in_specs=[_row_spec(th, cw, ci) for (_, th, cw, ci) in rows] + [_full_spec(p.shape) for p in params],
        out_specs=[pl.BlockSpec((o[0], o[1]), lambda i: (0, i)) if ac else _row_spec(o[0], o[1], 0)
                   for o, ac in zip(outs, across)],
        out_shape=[jax.ShapeDtypeStruct((o[0], n_steps * o[1]) if ac else (n_steps * o[0], o[1]), o[2])
                   for o, ac in zip(outs, across)],
        compiler_params=_params(("parallel",)),
    )(*[r[0] for r in rows], *params)
    return res


def _rowwise_bwd(fn, rows, aux, params, douts, n_steps, name, row_dtypes=None, adds=(), across=None, dout_from=None):
    n_r, n_a, n_p, n_d, n_add = len(rows), len(aux), len(params), len(douts), len(adds)
    row_dtypes = row_dtypes or (F32,) * n_r

    def body(*refs):
        it = iter(refs)
        r_vals = [next(it)[...].astype(F32) for _ in range(n_r)]
        a_vals = [next(it)[...].astype(F32) for _ in range(n_a)]
        p_vals = [next(it)[...].astype(F32) for _ in range(n_p)]
        d_vals = [next(it)[...].astype(F32) for _ in range(n_d)]
        add_vals = [next(it)[...].astype(F32) for _ in range(n_add)]
        across_in = next(it) if across is not None else None
        if dout_from is not None:
            d_vals = [_dot_raw(next(it)[...], next(it)[...], "nt")]
        dr_refs = [next(it) for _ in range(n_r)]
        dp_refs = [next(it) for _ in range(n_p)]

        def f(*rp):
            return tuple(fn(*rp[:n_r], *a_vals, *rp[n_r:]))

        _, vjp = jax.vjp(f, *r_vals, *p_vals)
        grads = list(vjp(tuple(d_vals)))
        for (ri, _), av in zip(adds, add_vals):
            grads[ri] = grads[ri] + av
        for dr_ref, g in zip(dr_refs, grads[:n_r]):
            dr_ref[...] = g.astype(dr_ref.dtype)
        if across is not None:
            next(it)[...] = across[1](r_vals, grads[:n_r], across_in[...])

        @pl.when(pl.program_id(0) == 0)
        def _():
            for dp_ref in dp_refs:
                dp_ref[...] = jnp.zeros_like(dp_ref)

        for dp_ref, g in zip(dp_refs, grads[n_r:]):
            dp_ref[...] += g

    all_rows = list(rows) + list(aux) + list(douts) + [(arr,) + tuple(rows[ri][1:3]) + (0,) for ri, arr in adds]
    in_specs = ([_row_spec(th, cw, ci) for (_, th, cw, ci) in list(rows) + list(aux)]
                + [_full_spec(p.shape) for p in params]
                + [_row_spec(th, cw, ci) for (_, th, cw, ci) in all_rows[n_r + n_a:]])
    across_specs = [] if across is None else [pl.BlockSpec((8, rows[0][1]), lambda i: (0, i))]
    from_specs = [] if dout_from is None else [_row_spec(rows[0][1], dout_from[0].shape[1], 0), _full_spec(dout_from[1].shape)]
    res = pl.pallas_call(
        body, name=name, grid=(n_steps,),
        in_specs=in_specs + across_specs + from_specs,
        out_specs=[_row_spec(th, cw, 0) for (_, th, cw, _) in rows] + [_full_spec(p.shape) for p in params] + across_specs,
        out_shape=[jax.ShapeDtypeStruct((n_steps * th, cw), dt) for (_, th, cw, _), dt in zip(rows, row_dtypes)]
        + [jax.ShapeDtypeStruct(p.shape, F32) for p in params]
        + [jax.ShapeDtypeStruct(across[0].shape, F32) for _ in across_specs],
        compiler_params=_params(("arbitrary",)),
    )(*[r[0] for r in list(rows) + list(aux)], *params, *[r[0] for r in all_rows[n_r + n_a:]],
      *[across[0] for _ in across_specs], *(dout_from or ()))
    return res[:n_r], res[n_r:]


def _sigmoid(x):
    return lax.logistic(x)


def _silu(x):
    return x * _sigmoid(x)


def _rms(x, w=None, n=None):
    n = n or x.shape[-1]
    y = x * lax.rsqrt(jnp.sum(x * x, axis=-1, keepdims=True) * (1.0 / n) + EPS)
    return y if w is None else y * w


def _modulate(x, scale, shift):
    return _rms(x) * (1.0 + scale) + shift


def _softplus(x):
    return jnp.maximum(x, 0.0) + jnp.log1p(jnp.exp(-jnp.abs(x)))


@jax.custom_vjp
def _rot_half64(x):
    lane = lax.broadcasted_iota(jnp.int32, x.shape, 1)
    up = pltpu.roll(x, 96, 1)
    down = pltpu.roll(x, 32, 1)
    return jnp.where(lane < 32, up, jnp.where(lane < 64, down, 0.0))


_rot_half64.defvjp(lambda x: (_rot_half64(x), None), lambda _, g: (_rot_half64(g),))


def _rope128(x, cos_p, sin_p):
    return x * cos_p + _rot_half64(x) * sin_p


def _gdn_pre_fn(qkvc, kab, alog_p, dt_p):
    a = _silu(qkvc)
    qs, ks = [], []
    for h in range(HEADS):
        qh = a[:, HEAD_DIM * h:HEAD_DIM * (h + 1)]
        kh = a[:, 512 + HEAD_DIM * h:512 + HEAD_DIM * (h + 1)]
        qs.append(qh * lax.rsqrt(jnp.sum(qh * qh, axis=-1, keepdims=True) + EPS) * (HEAD_DIM ** -0.5))
        ks.append(kh * lax.rsqrt(jnp.sum(kh * kh, axis=-1, keepdims=True) + EPS))
    lane = lax.broadcasted_iota(jnp.int32, kab.shape, 1)
    g_full = -jnp.exp(alog_p) * _softplus(kab + dt_p)
    b_full = _sigmoid(kab)
    gb = jnp.where((lane >= 64) & (lane < 68), g_full, jnp.where((lane >= 68) & (lane < 72), b_full, 0.0))
    return jnp.concatenate(qs, axis=1), jnp.concatenate(ks, axis=1), a[:, 1024:1536], gb


INTRA_ROWS = (512, 256, 128, 64)
TOKEN_ROWS = (512, 256, 128)

_BNN = (((2,), (1,)), ((0,), (0,)))
_BNT = (((2,), (2,)), ((0,), (0,)))
_BTN = (((1,), (1,)), ((0,), (0,)))


def _split_bf16(a):
    hi = a.astype(BF16)
    return hi, (a - hi.astype(F32)).astype(BF16)


def _dot3_raw(a, b, dims):
    a_hi, a_lo = _split_bf16(a)
    b_hi, b_lo = _split_bf16(b)
    dot = lambda x_, y_: lax.dot_general(x_, y_, dims, preferred_element_type=F32)
    return dot(a_hi, b_hi) + (dot(a_hi, b_lo) + dot(a_lo, b_hi))


@functools.partial(jax.custom_vjp, nondiff_argnums=(2, 3))
def _dot3(a, b, nt, exact_bwd=True):
    return _dot3_raw(a, b, _BNT if nt else _BNN)


def _dot3_fwd(a, b, nt, exact_bwd):
    return _dot3_raw(a, b, _BNT if nt else _BNN), (a, b)


def _dot3_bwd(nt, exact_bwd, res, g):
    a, b = res
    if exact_bwd:
        dot = _dot3_raw
    else:
        dot = lambda x_, y_, d_: lax.dot_general(x_.astype(BF16), y_.astype(BF16), d_, preferred_element_type=F32)
    if nt:
        return dot(g, b, _BNN), dot(jnp.swapaxes(g, 1, 2), a, _BNN)
    return dot(g, b, _BNT), dot(jnp.swapaxes(a, 1, 2), g, _BNN)


_dot3.defvjp(_dot3_fwd, _dot3_bwd)


@functools.partial(jax.custom_vjp, nondiff_argnums=(2,))
def _bdot_b(a, b, nt):
    return lax.dot_general(a.astype(BF16), b.astype(BF16), _BNT if nt else _BNN, preferred_element_type=F32)


def _bdot_b_fwd(a, b, nt):
    return _bdot_b(a, b, nt), (a, b)


def _bdot_b_bwd(nt, res, g):
    a, b = res
    dot = lambda x_, y_, d_: lax.dot_general(x_.astype(BF16), y_.astype(BF16), d_, preferred_element_type=F32)
    if nt:
        return dot(g, b, _BNN), dot(jnp.swapaxes(g, 1, 2), a, _BNN)
    return dot(g, b, _BNT), dot(jnp.swapaxes(a, 1, 2), g, _BNN)


_bdot_b.defvjp(_bdot_b_fwd, _bdot_b_bwd)


@jax.custom_vjp
def _inverse_given(a_mat, inv):
    return inv


def _inverse_given_bwd(inv, g):
    inv_t = jnp.swapaxes(inv, 1, 2)
    return -_dot3_raw(_dot3_raw(inv_t, g, _BNN), inv_t, _BNN), jnp.zeros_like(inv)


_inverse_given.defvjp(lambda a_mat, inv: (inv, inv), _inverse_given_bwd)


def _intra_batched(q, k, v, g_col, b_col, inv_known=None):
    c = CHUNK
    nb = q.shape[0]
    row = lax.broadcasted_iota(jnp.int32, (1, c, c), 1)
    col = lax.broadcasted_iota(jnp.int32, (1, c, c), 2)
    incl, strict, eye = row >= col, row > col, row == col
    tri = jnp.broadcast_to(jnp.where(incl, 1.0, 0.0).astype(F32), (nb, c, c))
    ident = jnp.where(eye, 1.0, 0.0).astype(F32)
    g_wide = _dot3(tri, jnp.broadcast_to(g_col, (nb, c, HEAD_DIM)), False)
    g_i = g_wide[:, :, :c]
    g_j = jnp.sum(jnp.where(eye, g_i, 0.0), axis=1, keepdims=True)
    decay = jnp.where(incl, jnp.exp(jnp.where(incl, g_i - g_j, 0.0)), 0.0)
    kk = _bdot_b(k, k, True)
    a_mat = jnp.where(strict, b_col * kk * decay, 0.0)
    if inv_known is None:
        x_pow = -a_mat
        inv = ident + x_pow
        for _ in range(5):
            x_pow = _dot3(x_pow, x_pow, False, False)
            inv = inv + _dot3(inv, x_pow, False, False)
    else:
        inv = _inverse_given(a_mat, inv_known)
    e_wide = jnp.exp(g_wide)
    u = _dot3(inv, v * b_col, False)
    wk = _dot3(inv, k * b_col * e_wide, False)
    qk = _bdot_b(q, k, True) * decay
    last = lax.broadcasted_iota(jnp.int32, (1, c, HEAD_DIM), 1) == c - 1
    g_last = jnp.sum(jnp.where(last, g_wide, 0.0), axis=1, keepdims=True)
    qd = q * e_wide
    kd = k * jnp.exp(g_last - g_wide)
    gl = jnp.broadcast_to(jnp.exp(g_last), (nb, 8, HEAD_DIM))
    return u, wk, qd, kd, qk, gl, inv


def _gdn_intra_fn(q, k, v, gb, *inv_known):
    t = q.shape[0]
    nch = t // CHUNK
    lane = lax.broadcasted_iota(jnp.int32, gb.shape, 1)

    def heads_first(x_):
        return jnp.concatenate([x_[:, HEAD_DIM * h:HEAD_DIM * (h + 1)].reshape(nch, CHUNK, HEAD_DIM) for h in range(HEADS)],
                               axis=0)

    def column(first_lane):
        return jnp.concatenate([jnp.sum(jnp.where(lane == first_lane + h, gb, 0.0), axis=1, keepdims=True)
                                .reshape(nch, CHUNK, 1) for h in range(HEADS)], axis=0)

    known = jnp.concatenate([x_.reshape(nch, CHUNK, CHUNK) for x_ in inv_known], axis=0) if inv_known else None
    u, wk, qd, kd, qk, gl, inv = _intra_batched(heads_first(q), heads_first(k), heads_first(v), column(64), column(68), known)

    def rows_first(x_):
        r, w_ = x_.shape[1], x_.shape[2]
        return jnp.concatenate([x_[nch * h:nch * (h + 1)].reshape(nch * r, w_) for h in range(HEADS)], axis=1)

    per_head = lambda x_: [x_[nch * h:nch * (h + 1)].reshape(t, CHUNK) for h in range(HEADS)]
    outs = (rows_first(u), rows_first(wk), rows_first(qd), rows_first(kd), *per_head(qk), rows_first(gl))
    return outs if inv_known else outs + tuple(per_head(inv))


def _scan_step(s0, u, wk, qd, kd, qk, gl):
    v_new = u - _bdot_b(wk, s0, False)
    o = _bdot_b(qd, s0, False) + _bdot_b(qk, v_new, False)
    s1 = s0 * gl[:, 0:1, :] + _bdot_b(jnp.swapaxes(kd, 1, 2), v_new, False)
    return o, s1


def _mix_post_fn(o_a, z, o_b, gnw, onw):
    parts = [_rms(o_a[:, HEAD_DIM * h:HEAD_DIM * (h + 1)], gnw) * _silu(z[:, HEAD_DIM * h:HEAD_DIM * (h + 1)])
             for h in range(HEADS)]
    parts += [_rms(o_b[:, HEAD_DIM * h:HEAD_DIM * (h + 1)], onw) for h in range(HEADS)]
    return (jnp.concatenate(parts, axis=1),)


def _mla_pre_fn(ckv, cq, kab, cos_p, sin_p, qnw, kvnw, wuq, wukv, qn_w, qr_w, kn_w, kr_w):
    scale = (HEAD_DIM + ROPE) ** -0.5 * LOG2_E
    qf = _bdot(_rms(cq, qnw), wuq, "nn")
    kvf = _bdot(_rms(ckv, kvnw), wukv, "nn")
    lane = lax.broadcasted_iota(jnp.int32, kab.shape, 1)
    kr = _rope128(_rms(jnp.where(lane < ROPE, kab, 0.0), kr_w, n=ROPE), cos_p, sin_p)
    qs, ks = [], []
    for h in range(HEADS):
        qn = _rms(qf[:, 256 * h:256 * h + 128], qn_w) * scale
        qr = _rope128(_rms(qf[:, 256 * h + 128:256 * h + 256], qr_w, n=ROPE), cos_p, sin_p) * scale
        qs += [qn, qr]
        ks += [_rms(kvf[:, 128 * h:128 * (h + 1)], kn_w), kr]
    return jnp.concatenate(qs, axis=1), jnp.concatenate(ks, axis=1), kvf[:, 512:]


def _conv_fwd(proj, conv_w, tm, name):
    S = proj.shape[0]
    C = 1536
    nb = tm // 8

    def body(x_ref, prev_ref, w_ref, o_ref, ext_ref):
        i = pl.program_id(0)
        ext_ref[0:8, :] = jnp.where(i > 0, prev_ref[...], 0.0)
        ext_ref[8:, :] = x_ref[...]
        acc = jnp.zeros((tm, C), F32)
        for k in range(4):
            acc = acc + w_ref[k:k + 1, :] * ext_ref[pl.ds(5 + k, tm), :]
        o_ref[...] = acc

    return pl.pallas_call(
        body, name=name, grid=(S // tm,),
        in_specs=[pl.BlockSpec((tm, C), lambda i: (i, 0)),
                  pl.BlockSpec((8, C), lambda i: (jnp.maximum(i * nb - 1, 0), 0)),
                  pl.BlockSpec((4, C), lambda i: (0, 0))],
        out_specs=pl.BlockSpec((tm, C), lambda i: (i, 0)),
        out_shape=jax.ShapeDtypeStruct((S, C), F32),
        scratch_shapes=[pltpu.VMEM((tm + 8, C), F32)],
        compiler_params=_params(("arbitrary",)),
    )(proj, proj, conv_w)


def _conv_bwd(proj, dout, conv_w, tm, name):
    S = proj.shape[0]
    C = 1536
    nb = tm // 8
    n_steps = S // tm

    def body(x_ref, prev_ref, d_ref, next_ref, w_ref, dx_ref, dw_ref, xext_ref, dext_ref):
        i = pl.program_id(0)
        xext_ref[0:8, :] = jnp.where(i > 0, prev_ref[...], 0.0)
        xext_ref[8:, :] = x_ref[...]
        dext_ref[0:tm, :] = d_ref[...]
        dext_ref[tm:, :] = jnp.where(i < n_steps - 1, next_ref[...], 0.0)
        d = d_ref[...]
        acc = jnp.zeros((tm, C), F32)
        dws = []
        for k in range(4):
            acc = acc + w_ref[k:k + 1, :] * dext_ref[pl.ds(3 - k, tm), :]
            dws.append(jnp.sum(d * xext_ref[pl.ds(5 + k, tm), :], axis=0, keepdims=True))
        dx_ref[...] = acc.astype(dx_ref.dtype)

        @pl.when(i == 0)
        def _():
            dw_ref[...] = jnp.zeros_like(dw_ref)

        dw_ref[...] += jnp.concatenate(dws + [jnp.zeros((4, C), F32)], axis=0)

    return pl.pallas_call(
        body, name=name, grid=(n_steps,),
        in_specs=[pl.BlockSpec((tm, C), lambda i: (i, 0)),
                  pl.BlockSpec((8, C), lambda i: (jnp.maximum(i * nb - 1, 0), 0)),
                  pl.BlockSpec((tm, C), lambda i: (i, 0)),
                  pl.BlockSpec((8, C), lambda i: (jnp.minimum((i + 1) * nb, S // 8 - 1), 0)),
                  pl.BlockSpec((4, C), lambda i: (0, 0))],
        out_specs=[pl.BlockSpec((tm, C), lambda i: (i, 0)), pl.BlockSpec((8, C), lambda i: (0, 0))],
        out_shape=[jax.ShapeDtypeStruct((S, C), BF16), jax.ShapeDtypeStruct((8, C), F32)],
        scratch_shapes=[pltpu.VMEM((tm + 8, C), F32), pltpu.VMEM((tm + 8, C), F32)],
        compiler_params=_params(("arbitrary",)),
    )(proj, proj, dout, dout, conv_w)


SCAN_CHUNKS = (8, 4, 2, 1)


def _gdn_scan_fwd(u, wk, qd, kd, qks, gl, name):
    S = u.shape[0]
    nc = S // CHUNK
    cs = _pick(nc, SCAN_CHUNKS)
    W = HEADS * HEAD_DIM

    def body(u_ref, wk_ref, qd_ref, kd_ref, qk0, qk1, qk2, qk3, gl_ref, o_ref, sp_ref, s_ref):
        @pl.when(pl.program_id(0) == 0)
        def _():
            s_ref[...] = jnp.zeros_like(s_ref)

        state = s_ref[...]
        for c in range(cs):
            rows, gl_rows = slice(CHUNK * c, CHUNK * (c + 1)), slice(8 * c, 8 * (c + 1))
            sp_ref[c] = state
            o, state = _scan_step(state, _heads(u_ref, HEAD_DIM, rows), _heads(wk_ref, HEAD_DIM, rows),
                                  _heads(qd_ref, HEAD_DIM, rows), _heads(kd_ref, HEAD_DIM, rows),
                                  jnp.stack([r[rows, :] for r in (qk0, qk1, qk2, qk3)]), _heads(gl_ref, HEAD_DIM, gl_rows))
            for h in range(HEADS):
                o_ref[rows, HEAD_DIM * h:HEAD_DIM * (h + 1)] = o[h]
        s_ref[...] = state

    row = pl.BlockSpec((cs * CHUNK, W), lambda n: (n, 0))
    qk_spec = pl.BlockSpec((cs * CHUNK, CHUNK), lambda n: (n, 0))
    return pl.pallas_call(
        body, name=name, grid=(nc // cs,),
        in_specs=[row, row, row, row, qk_spec, qk_spec, qk_spec, qk_spec, pl.BlockSpec((cs * 8, W), lambda n: (n, 0))],
        out_specs=[row, pl.BlockSpec((cs, HEADS, HEAD_DIM, HEAD_DIM), lambda n: (n, 0, 0, 0))],
        out_shape=[jax.ShapeDtypeStruct((S, W), F32), jax.ShapeDtypeStruct((nc, HEADS, HEAD_DIM, HEAD_DIM), F32)],
        scratch_shapes=[pltpu.VMEM((HEADS, HEAD_DIM, HEAD_DIM), F32)],
        compiler_params=_params(("arbitrary",)),
    )(u, wk, qd, kd, *qks, gl)


def _gdn_scan_bwd(u, wk, qd, kd, qks, gl, s_prev, d_o, name):
    S = u.shape[0]
    nc = S // CHUNK
    cs = _pick(nc, SCAN_CHUNKS)
    nb = nc // cs
    W = HEADS * HEAD_DIM

    def body(u_ref, wk_ref, qd_ref, kd_ref, qk0, qk1, qk2, qk3, gl_ref, sp_ref, do_ref,
             du_ref, dwk_ref, dqd_ref, dkd_ref, dqk0, dqk1, dqk2, dqk3, dgl_ref, ds_ref):
        @pl.when(pl.program_id(0) == 0)
        def _():
            ds_ref[...] = jnp.zeros_like(ds_ref)

        d_state = ds_ref[...]
        for c in reversed(range(cs)):
            rows, gl_rows = slice(CHUNK * c, CHUNK * (c + 1)), slice(8 * c, 8 * (c + 1))
            _, vjp = jax.vjp(_scan_step, sp_ref[c], _heads(u_ref, HEAD_DIM, rows), _heads(wk_ref, HEAD_DIM, rows),
                             _heads(qd_ref, HEAD_DIM, rows), _heads(kd_ref, HEAD_DIM, rows),
                             jnp.stack([r[rows, :] for r in (qk0, qk1, qk2, qk3)]), _heads(gl_ref, HEAD_DIM, gl_rows))
            d_state, du, dwk, dqd, dkd, dqk, dgl = vjp((_heads(do_ref, HEAD_DIM, rows), d_state))
            for h, dqk_ref in enumerate((dqk0, dqk1, dqk2, dqk3)):
                sl = slice(HEAD_DIM * h, HEAD_DIM * (h + 1))
                du_ref[rows, sl] = du[h]
                dwk_ref[rows, sl] = dwk[h]
                dqd_ref[rows, sl] = dqd[h]
                dkd_ref[rows, sl] = dkd[h]
                dqk_ref[rows, :] = dqk[h]
                dgl_ref[gl_rows, sl] = dgl[h]
        ds_ref[...] = d_state

    rev = lambda n: (nb - 1 - n, 0)
    row = pl.BlockSpec((cs * CHUNK, W), rev)
    qk_spec = pl.BlockSpec((cs * CHUNK, CHUNK), rev)
    gl_spec = pl.BlockSpec((cs * 8, W), rev)
    qk_shape = jax.ShapeDtypeStruct((S, CHUNK), F32)
    row_shape = jax.ShapeDtypeStruct((S, W), F32)
    return pl.pallas_call(
        body, name=name, grid=(nb,),
        in_specs=[row, row, row, row, qk_spec, qk_spec, qk_spec, qk_spec, gl_spec,
                  pl.BlockSpec((cs, HEADS, HEAD_DIM, HEAD_DIM), lambda n: (nb - 1 - n, 0, 0, 0)), row],
        out_specs=[row, row, row, row, qk_spec, qk_spec, qk_spec, qk_spec, gl_spec],
        out_shape=[row_shape] * 4 + [qk_shape] * 4 + [jax.ShapeDtypeStruct((nc * 8, W), F32)],
        scratch_shapes=[pltpu.VMEM((HEADS, HEAD_DIM, HEAD_DIM), F32)],
        compiler_params=_params(("arbitrary",)),
    )(u, wk, qd, kd, *qks, gl, s_prev, d_o)


NEG = -1e30


def _chunk_mask(i, j, t, transposed=False):
    q_axis, k_axis = (1, 0) if transposed else (0, 1)
    r = (i * t + lax.broadcasted_iota(jnp.int32, (t, t), q_axis)) // CHUNK
    c = (j * t + lax.broadcasted_iota(jnp.int32, (t, t), k_axis)) // CHUNK
    return c <= r


def _tile_pairs(n, by_key):
    pairs = [(i, j) for j in range(n) for i in range(j, n)] if by_key else [(i, j) for i in range(n) for j in range(i + 1)]
    return jnp.asarray(np.array([p[0] for p in pairs], np.int32)), jnp.asarray(np.array([p[1] for p in pairs], np.int32))


def _heads(ref, width, rows=slice(None)):
    return jnp.stack([ref[rows, width * h:width * (h + 1)] for h in range(HEADS)])


def _bmm(a, b, dims):
    return lax.dot_general(a.astype(BF16), b.astype(BF16), dims, preferred_element_type=F32)


def _attn_fwd(q, k, v_t, t, name):
    S = q.shape[0]
    n = S // t
    qi, kj = _tile_pairs(n, by_key=False)

    def body(qi_ref, kj_ref, q_ref, k_ref, vt_ref, o_ref, lse_ref, m_ref, l_ref, acc_ref):
        i, j = qi_ref[pl.program_id(0)], kj_ref[pl.program_id(0)]

        @pl.when(j == 0)
        def _():
            m_ref[...] = jnp.full_like(m_ref, NEG)
            l_ref[...] = jnp.zeros_like(l_ref)
            acc_ref[...] = jnp.zeros_like(acc_ref)

        def update(masked):
            s_t = _bmm(_heads(k_ref, 256), _heads(q_ref, 256), _BNT)
            if masked:
                s_t = jnp.where(_chunk_mask(i, j, t, transposed=True)[None], s_t, NEG)
            m_old = m_ref[...]
            m_new = jnp.maximum(m_old, jnp.max(s_t, axis=1, keepdims=True))
            p_t = jnp.exp2(s_t - m_new)
            alpha = jnp.exp2(m_old - m_new)
            l_ref[...] = alpha * l_ref[...] + jnp.sum(p_t, axis=1, keepdims=True)
            v_heads = jnp.stack([vt_ref[HEAD_DIM * h:HEAD_DIM * (h + 1), :] for h in range(HEADS)])
            acc_ref[...] = alpha * acc_ref[...] + _bmm(v_heads, p_t, _BNN)
            m_ref[...] = m_new

        @pl.when(j < i)
        def _():
            update(False)

        @pl.when(j == i)
        def _():
            update(True)
            for h in range(HEADS):
                sl = slice(HEAD_DIM * h, HEAD_DIM * (h + 1))
                o_ref[:, sl] = jnp.transpose(acc_ref[h] / l_ref[h])
                lse_ref[h:h + 1, :] = m_ref[h] + jnp.log(l_ref[h]) * LOG2_E
            lse_ref[HEADS:, :] = jnp.zeros((8 - HEADS, t), F32)

    row = lambda p, qi_, kj_: (qi_[p], 0)
    return pl.pallas_call(
        body, name=name,
        grid_spec=pltpu.PrefetchScalarGridSpec(
            num_scalar_prefetch=2, grid=(qi.shape[0],),
            in_specs=[pl.BlockSpec((t, HEADS * 256), row), pl.BlockSpec((t, HEADS * 256), lambda p, qi_, kj_: (kj_[p], 0)),
                      pl.BlockSpec((HEADS * HEAD_DIM, t), lambda p, qi_, kj_: (0, kj_[p]))],
            out_specs=[pl.BlockSpec((t, HEADS * HEAD_DIM), row), pl.BlockSpec((8, t), lambda p, qi_, kj_: (0, qi_[p]))],
            scratch_shapes=[pltpu.VMEM((HEADS, 1, t), F32), pltpu.VMEM((HEADS, 1, t), F32),
                            pltpu.VMEM((HEADS, HEAD_DIM, t), F32)]),
        out_shape=[jax.ShapeDtypeStruct((S, HEADS * HEAD_DIM), F32), jax.ShapeDtypeStruct((8, S), F32)],
        compiler_params=_params(("arbitrary",)),
    )(qi, kj, q, k, v_t)


def _attn_stats(o, d_o, lse):
    lane = lax.broadcasted_iota(jnp.int32, (o.shape[0], HEAD_DIM), 1)
    stats = jnp.zeros((o.shape[0], HEAD_DIM), F32)
    for h in range(HEADS):
        sl = slice(HEAD_DIM * h, HEAD_DIM * (h + 1))
        delta = jnp.sum(d_o[:, sl] * o[:, sl], axis=1, keepdims=True)
        stats = stats + jnp.where(lane == HEADS + h, delta, 0.0)
    return lse + jnp.transpose(stats)[0:8, :]


BWD_GROUP = 2


def _attn_bwd(q, k, v, d_o, stats, t, name):
    S = q.shape[0]
    n = S // t
    groups = HEADS // BWD_GROUP
    gq, gv = BWD_GROUP * 256, BWD_GROUP * HEAD_DIM
    qi, kj = _tile_pairs(n, by_key=True)
    n_pairs = qi.shape[0]
    st = stats.reshape(2, groups, BWD_GROUP, S).transpose(1, 0, 2, 3).reshape(groups, 2 * BWD_GROUP, S)
    st = jnp.pad(st, ((0, 0), (0, 8 - 2 * BWD_GROUP), (0, 0)))

    def heads(ref, width, rows=slice(None)):
        return jnp.stack([ref[rows, width * h:width * (h + 1)] for h in range(BWD_GROUP)])

    def body(qi_ref, kj_ref, q_ref, k_ref, v_ref, do_ref, st_ref, dq_hbm, dk_ref, dv_ref, dq_acc, sem):
        g, p = pl.program_id(0), pl.program_id(1)
        i, j = qi_ref[p], kj_ref[p]

        @pl.when(i == j)
        def _():
            dk_ref[...] = jnp.zeros_like(dk_ref)
            dv_ref[...] = jnp.zeros_like(dv_ref)

        def update(masked):
            qh, kh = heads(q_ref, 256), heads(k_ref, 256)
            d_out = heads(do_ref, HEAD_DIM)
            stv = st_ref[...]
            lse_row = jnp.stack([stv[h:h + 1, :] for h in range(BWD_GROUP)])
            delta_row = jnp.stack([stv[BWD_GROUP + h:BWD_GROUP + h + 1, :] for h in range(BWD_GROUP)])
            s_t = _bmm(kh, qh, _BNT)
            p_t = jnp.exp2(s_t - lse_row)
            if masked:
                p_t = jnp.where(_chunk_mask(i, j, t, transposed=True)[None], p_t, 0.0)
            dv = _bmm(p_t, d_out, _BNN)
            dp_t = _bmm(heads(v_ref, HEAD_DIM), d_out, _BNT)
            ds_t = p_t * (dp_t - delta_row)
            dk = _bmm(ds_t, qh, _BNN)
            dq = _bmm(ds_t, kh, _BTN)
            rows = pl.ds(pl.multiple_of(i * t, t), t)
            for h in range(BWD_GROUP):
                dk_ref[:, 256 * h:256 * (h + 1)] += dk[h]
                dv_ref[:, HEAD_DIM * h:HEAD_DIM * (h + 1)] += dv[h]

            @pl.when(j == 0)
            def _():
                for h in range(BWD_GROUP):
                    dq_acc[rows, 256 * h:256 * (h + 1)] = dq[h]

            @pl.when(j > 0)
            def _():
                for h in range(BWD_GROUP):
                    dq_acc[rows, 256 * h:256 * (h + 1)] += dq[h]

        @pl.when(i == j)
        def _():
            update(True)

        @pl.when(i > j)
        def _():
            update(False)

        @pl.when(i == n - 1)
        def _():
            dk_ref[...] *= 1.0 / LOG2_E

        @pl.when(p == n_pairs - 1)
        def _():
            dq_acc[...] *= 1.0 / LOG2_E
            for gg in range(groups):
                @pl.when(g == gg)
                def _():
                    cp = pltpu.make_async_copy(dq_acc, dq_hbm.at[:, gq * gg:gq * (gg + 1)], sem)
                    cp.start()
                    cp.wait()

    q_blk = lambda g, p, qi_, kj_: (qi_[p], g)
    k_blk = lambda g, p, qi_, kj_: (kj_[p], g)
    return pl.pallas_call(
        body, name=name,
        grid_spec=pltpu.PrefetchScalarGridSpec(
            num_scalar_prefetch=2, grid=(groups, n_pairs),
            in_specs=[pl.BlockSpec((t, gq), q_blk), pl.BlockSpec((t, gq), k_blk), pl.BlockSpec((t, gv), k_blk),
                      pl.BlockSpec((t, gv), q_blk), pl.BlockSpec((None, 8, t), lambda g, p, qi_, kj_: (g, 0, qi_[p]))],
            out_specs=[pl.BlockSpec(memory_space=pl.ANY), pl.BlockSpec((t, gq), k_blk), pl.BlockSpec((t, gv), k_blk)],
            scratch_shapes=[pltpu.VMEM((S, gq), F32), pltpu.SemaphoreType.DMA]),
        out_shape=[jax.ShapeDtypeStruct((S, HEADS * 256), F32), jax.ShapeDtypeStruct((S, HEADS * 256), F32),
                   jax.ShapeDtypeStruct((S, HEADS * HEAD_DIM), F32)],
        compiler_params=_params(("arbitrary", "arbitrary")),
    )(qi, kj, q, k, v, d_o, st)


FFN_PIECE = 2 * D_FF // N_DEV
HID_PIECES = D_FF // FFN_PIECE


def _after_specs(after):
    return [] if after is None else [pl.BlockSpec(memory_space=pl.ANY)]


def _after_args(after):
    return [] if after is None else [after]


def _ffn_gw8(h, d_gate, d_up, name, after=None):
    S = h.shape[0]
    tm = 512
    tk = _pick(S, MATMUL_ROWS)
    nk = S // tk

    def body(h_ref, dg_ref, du_ref, *rest):
        o_ref, acc_ref = rest[-2:]
        k = pl.program_id(1)

        @pl.when(k == 0)
        def _():
            acc_ref[...] = jnp.zeros_like(acc_ref)

        h_t = jnp.transpose(h_ref[...])
        for p in range(HID_PIECES):
            acc_ref[p] += _dot_raw(h_t, dg_ref[p], "nn")
            acc_ref[HID_PIECES + p] += _dot_raw(h_t, du_ref[p], "nn")

        @pl.when(k == nk - 1)
        def _():
            o_ref[...] = acc_ref[...].astype(o_ref.dtype)

    d_spec = pl.BlockSpec((HID_PIECES, tk, FFN_PIECE), lambda i, k: (0, k, 0))
    return pl.pallas_call(
        body, name=name, grid=(D_MODEL // tm, nk),
        in_specs=[pl.BlockSpec((tk, tm), lambda i, k: (k, i)), d_spec, d_spec] + _after_specs(after),
        out_specs=pl.BlockSpec((2 * HID_PIECES, tm, FFN_PIECE), lambda i, k: (0, i, 0)),
        out_shape=jax.ShapeDtypeStruct((2 * HID_PIECES, D_MODEL, FFN_PIECE), BF16),
        scratch_shapes=[pltpu.VMEM((2 * HID_PIECES, tm, FFN_PIECE), F32)],
        compiler_params=_params(("parallel", "arbitrary")),
    )(h, d_gate, d_up, *_after_args(after))


EPILOGUE_ROWS = 256


def _dmod_epilogue(acc_ref, x_ref, do_ref, sc_ref, sh_ref, dx_ref, dsc_ref, dsh_ref, below, below_in, below_out):
    rows_total = acc_ref.shape[0]
    step = min(EPILOGUE_ROWS, rows_total)
    dsc, dsh, dg = 0.0, 0.0, 0.0
    for r in range(rows_total // step):
        rows = slice(step * r, step * (r + 1))
        _, vjp = jax.vjp(_modulate, x_ref[rows, :], sc_ref[...], sh_ref[...])
        dx, dsc_r, dsh_r = vjp(acc_ref[rows, :])
        dx = dx + do_ref[rows, :]
        dx_ref[rows, :] = dx
        dsc, dsh = dsc + dsc_r, dsh + dsh_r
        if below is not None:
            coef = below[2]
            below_out[0][rows, :] = (coef * below_in[1][...] * dx).astype(below_out[0].dtype)
            dg = dg + jnp.sum(coef * below_in[0][rows, :] * dx, axis=0, keepdims=True)
    dsc_ref[...] += dsc
    dsh_ref[...] += dsh
    if below is not None:
        below_out[1][...] += dg


def _ffn_dh(d_gate, d_up, w8, x, d_out, scale, shift, name, after=None, below=None):
    S = d_gate.shape[1]
    tm = _pick(S, MATMUL_ROWS)
    n_below = 0 if below is None else 2

    def body(dg_ref, du_ref, wg_ref, wu_ref, x_ref, do_ref, sc_ref, sh_ref, *rest):
        below_in = rest[:n_below]
        outs = rest[len(rest) - 4 - n_below:]
        dx_ref, dsc_ref, dsh_ref = outs[:3]
        below_out, acc_ref = outs[3:3 + n_below], outs[-1]
        i, k = pl.program_id(0), pl.program_id(1)

        @pl.when(k == 0)
        def _():
            acc_ref[...] = jnp.zeros_like(acc_ref)

        acc_ref[...] += _dot_raw(dg_ref[...], wg_ref[...], "nt") + _dot_raw(du_ref[...], wu_ref[...], "nt")

        @pl.when((k == 0) & (i == 0))
        def _():
            for r in (dsc_ref, dsh_ref) + tuple(below_out[1:]):
                r[...] = jnp.zeros_like(r)

        @pl.when(k == HID_PIECES - 1)
        def _():
            _dmod_epilogue(acc_ref, x_ref, do_ref, sc_ref, sh_ref, dx_ref, dsc_ref, dsh_ref, below, below_in, below_out)

    d_spec = pl.BlockSpec((None, tm, FFN_PIECE), lambda i, k: (k, i, 0))
    row = pl.BlockSpec((tm, D_MODEL), lambda i, k: (i, 0))
    par = pl.BlockSpec((1, D_MODEL), lambda i, k: (0, 0))
    row_shape, par_shape = jax.ShapeDtypeStruct((S, D_MODEL), F32), jax.ShapeDtypeStruct((1, D_MODEL), F32)
    return pl.pallas_call(
        body, name=name, grid=(S // tm, HID_PIECES),
        in_specs=[d_spec, d_spec,
                  pl.BlockSpec((None, D_MODEL, FFN_PIECE), lambda i, k: (k, 0, 0)),
                  pl.BlockSpec((None, D_MODEL, FFN_PIECE), lambda i, k: (k + HID_PIECES, 0, 0)),
                  row, row, par, par] + [row, par][:n_below] + _after_specs(after),
        out_specs=[row, par, par] + [row, par][:n_below],
        out_shape=[row_shape, par_shape, par_shape] + [jax.ShapeDtypeStruct((S, D_MODEL), BF16), par_shape][:n_below],
        scratch_shapes=[pltpu.VMEM((tm, D_MODEL), F32)],
        compiler_params=_params(("arbitrary", "arbitrary")),
    )(d_gate, d_up, w8, w8, x, d_out, scale, shift, *(below[:2] if below is not None else ()), *_after_args(after))


def _swiglu_bwd(d_hid, hid_by_gate, hid_by_up):
    return d_hid * hid_by_gate, d_hid * hid_by_up


def _adamw_math(w_, g_, m_, v_):
    m_ = ADAM_B1 * m_ + (1.0 - ADAM_B1) * g_
    v_ = ADAM_B2 * v_ + (1.0 - ADAM_B2) * (g_ * g_)
    m_hat = m_ / (1.0 - ADAM_B1 ** ADAM_STEP)
    v_hat = v_ / (1.0 - ADAM_B2 ** ADAM_STEP)
    return -ADAM_LR * (m_hat / (jnp.sqrt(v_hat) + ADAM_EPS) + ADAM_WD * w_), m_, v_


def _adamw(w, g, m, v, name):
    R, C = w.shape
    tr = _pick(R, (256, 176, 128, 64, 32, 16, 8))

    def body(w_ref, g_ref, m_ref, v_ref, d_ref, nm_ref, nv_ref):
        d_ref[...], nm_ref[...], nv_ref[...] = _adamw_math(w_ref[...], g_ref[...], m_ref[...], v_ref[...])

    spec = pl.BlockSpec((tr, C), lambda i: (i, 0))
    return pl.pallas_call(
        body, name=name, grid=(R // tr,),
        in_specs=[spec] * 4, out_specs=[spec] * 3,
        out_shape=[jax.ShapeDtypeStruct((R, C), F32)] * 3,
        compiler_params=_params(("parallel",)),
    )(w, g, m, v)


def _sum_adamw(parts, w, m, v, name, transposed=False):
    _, R, C = parts.shape
    tr = _pick(R, (256, 176, 128, 64, 32, 16, 8))

    def body(p_ref, w_ref, m_ref, v_ref, g_ref, d_ref, nm_ref, nv_ref):
        g_ = p_ref[0].astype(F32)
        for d in range(1, N_DEV):
            g_ = g_ + p_ref[d].astype(F32)
        if transposed:
            g_ = jnp.transpose(g_)
        g_ref[...] = g_
        d_ref[...], nm_ref[...], nv_ref[...] = _adamw_math(w_ref[...], g_, m_ref[...], v_ref[...])

    spec = pl.BlockSpec((C, tr), lambda i: (0, i)) if transposed else pl.BlockSpec((tr, C), lambda i: (i, 0))
    return pl.pallas_call(
        body, name=name, grid=(R // tr,),
        in_specs=[pl.BlockSpec((N_DEV, tr, C), lambda i: (0, i, 0)), spec, spec, spec], out_specs=[spec] * 4,
        out_shape=[jax.ShapeDtypeStruct(w.shape, F32)] * 4,
        compiler_params=_params(("parallel",)),
    )(parts, w, m, v)


def _sum_devices(parts, name):
    _, R, C = parts.shape
    tr = _pick(R, (512, 256, 176, 128, 64, 32, 16, 8))

    def body(p_ref, o_ref):
        acc = p_ref[0].astype(F32)
        for d in range(1, N_DEV):
            acc = acc + p_ref[d].astype(F32)
        o_ref[...] = acc

    return pl.pallas_call(
        body, name=name, grid=(R // tr,),
        in_specs=[pl.BlockSpec((N_DEV, tr, C), lambda i: (0, i, 0))],
        out_specs=pl.BlockSpec((tr, C), lambda i: (i, 0)),
        out_shape=jax.ShapeDtypeStruct((R, C), F32),
        compiler_params=_params(("parallel",)),
    )(parts)


def _my_place():
    return lax.axis_index("x"), lax.axis_index("y"), lax.axis_index("c")


def _all_gather(blocks, name):
    n = len(blocks)

    def body(*refs):
        x_refs, out_refs = refs[:n], refs[n:2 * n]
        send_sems, recv_sems, local_sems = refs[2 * n:]
        x, y, c = _my_place()
        me, sibling = (x, y, c), (x, y, 1 - c)
        chips = [(1 - x, y), (x, 1 - y), (1 - x, 1 - y)]

        def copy(a, k, blk, to, own=False):
            slot = out_refs[a].at[4 * blk[0] + 2 * blk[1] + blk[2]]
            return pltpu.make_async_remote_copy(
                src_ref=x_refs[a] if own else slot, dst_ref=slot,
                send_sem=send_sems.at[7 * a + k], recv_sem=recv_sems.at[7 * a + k], device_id=to, device_id_type=MESH)

        mine = [pltpu.make_async_copy(x_refs[a], out_refs[a].at[4 * x + 2 * y + c], local_sems.at[a]) for a in range(n)]
        for cp in mine:
            cp.start()
        first = []
        for j, chip in enumerate(chips):
            first += [copy(a, 1 + j, me, (*chip, c), own=True) for a in range(n)]
        first += [copy(a, 0, me, sibling, own=True) for a in range(n)]
        for cp in first:
            cp.start()
        passed = []
        for j, chip in enumerate(chips):
            for a in range(n):
                copy(a, 1 + j, (*chip, c), me).wait_recv()
                passed.append(copy(a, 4 + j, (*chip, c), sibling))
                passed[-1].start()
        for a in range(n):
            copy(a, 0, sibling, me).wait_recv()
        for j, chip in enumerate(chips):
            for a in range(n):
                copy(a, 4 + j, (*chip, 1 - c), me).wait_recv()
        for cp in first + passed:
            cp.wait_send()
        for cp in mine:
            cp.wait()

    return pl.pallas_call(
        body, name=name,
        out_shape=[jax.ShapeDtypeStruct((N_DEV,) + b.shape, b.dtype) for b in blocks],
        in_specs=[pl.BlockSpec(memory_space=pl.ANY)] * n,
        out_specs=[pl.BlockSpec(memory_space=pl.ANY)] * n,
        scratch_shapes=[pltpu.SemaphoreType.DMA((7 * n,)), pltpu.SemaphoreType.DMA((7 * n,)), pltpu.SemaphoreType.DMA((n,))],
    )(*blocks)


def _all_to_all(pieces, name):
    n = len(pieces)

    def body(*refs):
        x_refs, out_refs = refs[:n], refs[n:2 * n]
        send_sems, recv_sems, local_sems = refs[2 * n:]
        x, y, c = _my_place()
        me = 4 * x + 2 * y + c
        mine = [pltpu.make_async_copy(x_refs[a].at[me], out_refs[a].at[me], local_sems.at[a]) for a in range(n)]
        for cp in mine:
            cp.start()
        copies = []
        for k in (2, 4, 6, 3, 5, 7, 1):
            px = 1 - x if k & 4 else x
            py = 1 - y if k & 2 else y
            pc = 1 - c if k & 1 else c
            peer = 4 * px + 2 * py + pc
            for a in range(n):
                copies.append(pltpu.make_async_remote_copy(
                    src_ref=x_refs[a].at[peer], dst_ref=out_refs[a].at[me],
                    send_sem=send_sems.at[7 * a + k - 1], recv_sem=recv_sems.at[7 * a + k - 1],
                    device_id=(px, py, pc), device_id_type=MESH))
        for cp in copies:
            cp.start()
        for cp in copies:
            cp.wait_recv()
        for cp in copies:
            cp.wait_send()
        for cp in mine:
            cp.wait()

    return pl.pallas_call(
        body, name=name,
        out_shape=[jax.ShapeDtypeStruct(p.shape, p.dtype) for p in pieces],
        in_specs=[pl.BlockSpec(memory_space=pl.ANY)] * n,
        out_specs=[pl.BlockSpec(memory_space=pl.ANY)] * n,
        scratch_shapes=[pltpu.SemaphoreType.DMA((7 * n,)), pltpu.SemaphoreType.DMA((7 * n,)), pltpu.SemaphoreType.DMA((n,))],
    )(*pieces)


def _peers():
    x, y, c = _my_place()
    out = []
    for k in (2, 4, 6, 3, 5, 7, 1):
        px = 1 - x if k & 4 else x
        py = 1 - y if k & 2 else y
        pc = 1 - c if k & 1 else c
        out.append((k, (px, py, pc), 4 * px + 2 * py + pc))
    return out


def _exchange_copies(x_refs, land_refs, send_sems, recv_sems, scatter):
    x, y, c = _my_place()
    me = 4 * x + 2 * y + c
    starts, arrivals = [], []
    for k, place, peer in _peers():
        for a, (x_ref, land_ref) in enumerate(zip(x_refs, land_refs)):
            sems = dict(send_sem=send_sems.at[7 * a + k - 1], recv_sem=recv_sems.at[7 * a + k - 1],
                        device_id=place, device_id_type=MESH)
            src = x_ref.at[peer] if scatter else x_ref
            starts.append(pltpu.make_async_remote_copy(src_ref=src, dst_ref=land_ref.at[me], **sems))
            arrivals.append(pltpu.make_async_remote_copy(src_ref=src, dst_ref=land_ref.at[peer], **sems))
    return starts, arrivals


def _exchange_start(arrays, scatter, name):
    n = len(arrays)
    hbm = pl.BlockSpec(memory_space=pltpu.HBM)
    sem = pl.BlockSpec(memory_space=pltpu.SEMAPHORE)
    lands = [lax.empty(a.shape if scatter else (N_DEV,) + a.shape, a.dtype) for a in arrays]

    def body(*refs):
        x_refs, land_refs = refs[:n], refs[n:2 * n]
        send_sems, recv_sems = refs[2 * n], refs[2 * n + 1]
        token = refs[-1]
        starts, _ = _exchange_copies(x_refs, land_refs, send_sems, recv_sems, scatter)
        for cp in starts:
            cp.start()
        token[...] = jnp.zeros_like(token)

    res = pl.pallas_call(
        body, name=name,
        out_shape=(pltpu.SemaphoreType.DMA((7 * n,)), pltpu.SemaphoreType.DMA((7 * n,)),
                   *[pltpu.HBM(a.shape, a.dtype) for a in arrays], *[pltpu.HBM(l.shape, l.dtype) for l in lands],
                   jax.ShapeDtypeStruct((8, 128), F32)),
        in_specs=[hbm] * (2 * n),
        out_specs=(sem, sem, *[hbm] * (2 * n), pl.BlockSpec(memory_space=pltpu.VMEM)),
        input_output_aliases={i: 2 + i for i in range(2 * n)},
        compiler_params=pltpu.CompilerParams(has_side_effects=pltpu.SideEffectType.DATAFLOW_SIDE_EFFECTING),
    )(*[pltpu.with_memory_space_constraint(a, pltpu.HBM) for a in arrays],
      *[pltpu.with_memory_space_constraint(l, pltpu.HBM) for l in lands])
    return res[0], res[1], list(res[2:2 + n]), list(res[2 + n:2 + 2 * n]), res[-1]


def _exchange_wait(handles, scatter, after, name):
    send_sems, recv_sems, arrays, lands, _ = handles
    n = len(arrays)
    hbm = pl.BlockSpec(memory_space=pltpu.HBM)
    sem = pl.BlockSpec(memory_space=pltpu.SEMAPHORE)

    def body(*refs):
        x_refs, land_refs = refs[:n], refs[n:2 * n]
        send_s, recv_s = refs[2 * n], refs[2 * n + 1]
        starts, arrivals = _exchange_copies(x_refs, land_refs, send_s, recv_s, scatter)
        for cp in arrivals:
            cp.wait_recv()
        for cp in starts:
            cp.wait_send()

    res = pl.pallas_call(
        body, name=name,
        out_shape=(*[pltpu.HBM(a.shape, a.dtype) for a in arrays], *[pltpu.HBM(l.shape, l.dtype) for l in lands]),
        in_specs=[hbm] * (2 * n) + [sem, sem, pl.BlockSpec(memory_space=pl.ANY)],
        out_specs=tuple([hbm] * (2 * n)),
        input_output_aliases={i: i for i in range(2 * n)},
        compiler_params=pltpu.CompilerParams(has_side_effects=pltpu.SideEffectType.DATAFLOW_SIDE_EFFECTING),
    )(*arrays, *lands, send_sems, recv_sems, after)
    me = 4 * lax.axis_index("x") + 2 * lax.axis_index("y") + lax.axis_index("c")
    out = []
    for src, got in zip(res[:n], res[n:]):
        zeros = (0,) * (got.ndim - 1)
        own = lax.dynamic_slice(src, (me,) + zeros, (1,) + src.shape[1:]) if scatter else src[None]
        out.append(lax.dynamic_update_slice(got, own, (me,) + zeros))
    return out


def _pad_lanes(v, at=0, width=128):
    return jnp.pad(v, ((0, 0), (at, width - at - v.shape[1])))


def _pack_weights(P):
    W = {}
    w = P["w_in"]
    W["wp"] = jnp.concatenate([w[:, :2048], w[:, 2440:2696], w[:, 2056:2440], w[:, 2696:2760], w[:, 2048:2056],
                               jnp.zeros((D_MODEL, N_IN_PACKED - N_IN), w.dtype)], axis=1).astype(BF16)
    W["conv_w"] = P["gdn_conv_w"].astype(F32)
    W["alog_p"] = _pad_lanes(P["gdn_a_log"], 64)
    W["dt_p"] = _pad_lanes(P["gdn_dt_bias"], 64)
    W["gnw"] = P["gdn_norm_w"]
    W["qnw"] = P["mla_q_norm_w"]
    W["kvnw"] = P["mla_kv_norm_w"]
    uq = P["mla_w_uq"].reshape(Q_LORA, HEADS, HEAD_DIM + ROPE)
    W["wuq"] = jnp.pad(uq, ((0, 0), (0, 0), (0, 256 - HEAD_DIM - ROPE))).reshape(Q_LORA, HEADS * 256).astype(BF16)
    ukv = P["mla_w_ukv"].reshape(KV_LORA, HEADS, 2, HEAD_DIM)
    W["wukv"] = ukv.transpose(0, 2, 1, 3).reshape(KV_LORA, 2 * HEADS * HEAD_DIM).astype(BF16)
    W["qn_w"] = P["qkn_q_nope"]
    W["qr_w"] = _pad_lanes(P["qkn_q_rope"])
    W["kn_w"] = P["qkn_k_nope"]
    W["kr_w"] = _pad_lanes(P["qkn_k_rope"])
    W["onw"] = P["mla_out_norm_w"]
    W["wout"] = P["w_out"].astype(BF16)
    return W


def _unpack_grads(G):
    g_qkv, g_z, g_ckv, g_cq, g_kab = G["wp"]
    uq = G["wuq"].reshape(Q_LORA, HEADS, 256)[:, :, :HEAD_DIM + ROPE].reshape(Q_LORA, HEADS * (HEAD_DIM + ROPE))
    ukv = G["wukv"].reshape(KV_LORA, 2, HEADS, HEAD_DIM).transpose(0, 2, 1, 3).reshape(KV_LORA, 2 * HEADS * HEAD_DIM)
    return {
        "w_in": jnp.concatenate([g_qkv, g_z, g_kab[:, ROPE:ROPE + 8], g_cq, g_ckv, g_kab[:, :ROPE]], axis=1),
        "gdn_conv_w": G["conv_w"], "gdn_a_log": G["alog_p"][:, 64:68], "gdn_dt_bias": G["dt_p"][:, 64:68],
        "gdn_norm_w": G["gnw"], "mla_q_norm_w": G["qnw"], "mla_w_uq": uq, "mla_kv_norm_w": G["kvnw"], "mla_w_ukv": ukv,
        "qkn_q_nope": G["qn_w"], "qkn_q_rope": G["qr_w"][:, :ROPE], "qkn_k_nope": G["kn_w"], "qkn_k_rope": G["kr_w"][:, :ROPE],
        "mla_out_norm_w": G["onw"], "w_out": G["wout"],
    }


def _rope_tables(positions):
    half = ROPE // 2
    inv_freq = ROPE_BASE ** (-jnp.arange(half, dtype=F32) / half)
    ang = positions.astype(F32)[:, None] * inv_freq
    cos, sin = jnp.cos(ang), jnp.sin(ang)
    zeros = jnp.zeros((positions.shape[0], 128 - ROPE), F32)
    return jnp.concatenate([cos, cos, zeros], axis=1), jnp.concatenate([-sin, sin, zeros], axis=1)


def _ffn_forward(x, scale, shift, gate_w, w8, wo4, name, target=None):
    S = x.shape[0]
    tm = _pick(S, (512, 256, 128))
    n = S // tm
    with_loss = target is not None

    def body(x_ref, sc_ref, sh_ref, g_ref, wg_ref, wu_ref, wo_ref, *rest):
        t_ref = rest[0] if with_loss else None
        h_ref, bg_ref, bu_ref, ht_ref = rest[with_loss:with_loss + 4]
        tail = rest[with_loss + 4:]
        h_scr, acc_ref = tail[-2:]
        i, p = pl.program_id(0), pl.program_id(1)

        @pl.when(p == 0)
        def _():
            h_new = _modulate(x_ref[...], sc_ref[...], sh_ref[...]).astype(BF16)
            h_scr[...] = h_new
            h_ref[...] = h_new
            acc_ref[...] = jnp.zeros_like(acc_ref)

        h = h_scr[...]
        gate = _dot_raw(h, wg_ref[...], "nn")
        up = _dot_raw(h, wu_ref[...], "nn")
        sg = _sigmoid(gate)
        act = gate * sg
        hid = act * up
        bg_ref[...] = (up * (sg * (1.0 + gate * (1.0 - sg)))).astype(BF16)
        bu_ref[...] = act.astype(BF16)
        ht_ref[...] = jnp.transpose(hid).astype(BF16)
        acc_ref[...] += _dot_raw(hid, wo_ref[...], "nn")

        if with_loss:
            dx_ref, df_ref, dg_ref, l_ref = tail[:4]

            @pl.when((p == 0) & (i == 0))
            def _():
                dg_ref[...] = jnp.zeros_like(dg_ref)
                l_ref[...] = jnp.zeros_like(l_ref)

            @pl.when(p == HID_PIECES - 1)
            def _():
                step = min(EPILOGUE_ROWS, tm)
                for r in range(tm // step):
                    rows = slice(step * r, step * (r + 1))
                    f = acc_ref[rows, :]
                    diff = x_ref[rows, :] + 0.5 * g_ref[...] * f - t_ref[rows, :]
                    dx = diff * (1.0 / D_MODEL)
                    dx_ref[rows, :] = dx
                    df_ref[rows, :] = (0.5 * g_ref[...] * dx).astype(df_ref.dtype)
                    dg_ref[...] += jnp.sum(0.5 * f * dx, axis=0, keepdims=True)
                    l_ref[...] += jnp.sum(diff * diff, axis=0, keepdims=True)

            @pl.when((p == HID_PIECES - 1) & (i == n - 1))
            def _():
                l_ref[...] = jnp.full(l_ref.shape, (0.5 / D_MODEL) * jnp.sum(l_ref[...]), F32)
        else:
            f_ref, xo_ref = tail[:2]

            @pl.when(p == HID_PIECES - 1)
            def _():
                f = acc_ref[...]
                f_ref[...] = f.astype(f_ref.dtype)
                xo_ref[...] = x_ref[...] + 0.5 * g_ref[...] * f

    row = pl.BlockSpec((tm, D_MODEL), lambda i, p: (i, 0))
    par = pl.BlockSpec((1, D_MODEL), lambda i, p: (0, 0))
    piece = pl.BlockSpec((None, tm, FFN_PIECE), lambda i, p: (p, i, 0))
    row_f32, row_bf16 = jax.ShapeDtypeStruct((S, D_MODEL), F32), jax.ShapeDtypeStruct((S, D_MODEL), BF16)
    par_f32 = jax.ShapeDtypeStruct((1, D_MODEL), F32)
    piece_shape = jax.ShapeDtypeStruct((HID_PIECES, S, FFN_PIECE), BF16)
    return pl.pallas_call(
        body, name=name, grid=(n, HID_PIECES),
        in_specs=[row, par, par, par,
                  pl.BlockSpec((None, D_MODEL, FFN_PIECE), lambda i, p: (p, 0, 0)),
                  pl.BlockSpec((None, D_MODEL, FFN_PIECE), lambda i, p: (p + HID_PIECES, 0, 0)),
                  pl.BlockSpec((None, FFN_PIECE, D_MODEL), lambda i, p: (p, 0, 0))] + [row] * with_loss,
        out_specs=[row, piece, piece, pl.BlockSpec((None, FFN_PIECE, tm), lambda i, p: (p, 0, i))]
        + ([row, row, par, par] if with_loss else [row, row]),
        out_shape=[row_bf16, piece_shape, piece_shape, jax.ShapeDtypeStruct((HID_PIECES, FFN_PIECE, S), BF16)]
        + ([row_f32, row_bf16, par_f32, par_f32] if with_loss else [row_bf16, row_f32]),
        scratch_shapes=[pltpu.VMEM((tm, D_MODEL), BF16), pltpu.VMEM((tm, D_MODEL), F32)],
        compiler_params=_params(("arbitrary", "arbitrary")),
    )(x, scale, shift, gate_w, w8, w8, wo4, *([target] if with_loss else []))


def _ffn_fwd(x, scale, shift, gate_w, w8, wo4, tag, target=None):
    res = _ffn_forward(x, scale, shift, gate_w, w8, wo4, tag + "_fwd", target)
    h, by_gate, by_up, hid_t = res[:4]
    if target is not None:
        dx_out, df, d_gate_w, loss_row = res[4:]
        return (dx_out, loss_row), (h, by_gate, by_up, hid_t, None, df, d_gate_w)
    f, x_out = res[4:]
    return x_out, (h, by_gate, by_up, hid_t, f, None, None)


def _ffn_bwd(d_out, x, scale, shift, gate_w, w8, wo4, saved, tag, grad_ready, below=None):
    h, gate, up, hid_t, f, df, d_gate_w = saved
    S = x.shape[0]
    tm = _pick(S, (512, 256, 128))
    tk = _pick(S, (512, 256, 128))
    n = S // tm
    if df is None:
        (df,), (d_gate_w,) = _rowwise_bwd(lambda f_, g_: (0.5 * g_ * f_,), [(f, tm, D_MODEL, 0)], [], [gate_w],
                                          [(d_out, tm, D_MODEL, 0)], n, tag + "_dres", row_dtypes=(BF16,))
    tb = _pick(S, MATMUL_ROWS)
    piece = pl.BlockSpec((None, tb, FFN_PIECE), lambda i, j, k: (j, i, 0))
    d_gate, d_up = _mmg(df, wo4, "nt", name=tag + "_ddown", grid=(S // tb, HID_PIECES, 1),
                        a_spec=pl.BlockSpec((tb, D_MODEL), lambda i, j, k: (i, 0)),
                        b_spec=pl.BlockSpec((None, FFN_PIECE, D_MODEL), lambda i, j, k: (j, 0, 0)),
                        out_spec=piece, out_shapes=[jax.ShapeDtypeStruct((HID_PIECES, S, FFN_PIECE), BF16)] * 2,
                        acc_shape=(tb, FFN_PIECE), extras=[gate, up], extra_specs=[piece, piece], epi=_swiglu_bwd)
    tk = _pick(S, MATMUL_ROWS)
    g_wo4 = _mmg(hid_t, df, "nn", name=tag + "_gwo", grid=(HID_PIECES, 1, S // tk),
                 a_spec=pl.BlockSpec((None, FFN_PIECE, tk), lambda i, j, k: (i, 0, k)),
                 b_spec=pl.BlockSpec((tk, D_MODEL), lambda i, j, k: (k, j)),
                 out_spec=pl.BlockSpec((None, FFN_PIECE, D_MODEL), lambda i, j, k: (i, 0, j)),
                 out_shapes=[jax.ShapeDtypeStruct((HID_PIECES, FFN_PIECE, D_MODEL), BF16)], acc_shape=(FFN_PIECE, D_MODEL))
    g_w8 = _ffn_gw8(h, d_gate, d_up, tag + "_gw8", after=grad_ready("wo4", g_wo4))
    res = _ffn_dh(d_gate, d_up, w8, x, d_out, scale, shift, tag + "_dh", after=grad_ready("w8", g_w8), below=below)
    return (res[0], res[1], res[2], d_gate_w) + tuple(res[3:])


def _dproj_dmod(d_pieces, wp, x, d_out, scale, shift, name, below=None):
    S = x.shape[0]
    tm = _pick(S, TOKEN_ROWS)
    widths = [p.shape[1] for p in d_pieces]
    starts = [sum(widths[:n]) for n in range(len(widths))]
    n_p = len(d_pieces)
    n_below = 0 if below is None else 2

    def body(*refs):
        dp_refs, (w_ref, x_ref, do_ref, sc_ref, sh_ref), rest = refs[:n_p], refs[n_p:n_p + 5], refs[n_p + 5:]
        below_in = rest[:n_below]
        dx_ref, dsc_ref, dsh_ref = rest[n_below:n_below + 3]
        below_out, acc_ref = rest[n_below + 3:n_below + 3 + n_below], rest[-1]

        @pl.when(pl.program_id(0) == 0)
        def _():
            for r in (dsc_ref, dsh_ref) + tuple(below_out[1:]):
                r[...] = jnp.zeros_like(r)

        acc = None
        for dp_ref, at, width in zip(dp_refs, starts, widths):
            part = _dot_raw(dp_ref[...], w_ref[:, at:at + width], "nt")
            acc = part if acc is None else acc + part
        acc_ref[...] = acc
        _dmod_epilogue(acc_ref, x_ref, do_ref, sc_ref, sh_ref, dx_ref, dsc_ref, dsh_ref, below, below_in, below_out)

    row = pl.BlockSpec((tm, D_MODEL), lambda i: (i, 0))
    par = pl.BlockSpec((1, D_MODEL), lambda i: (0, 0))
    row_shape, par_shape = jax.ShapeDtypeStruct((S, D_MODEL), F32), jax.ShapeDtypeStruct((1, D_MODEL), F32)
    return pl.pallas_call(
        body, name=name, grid=(S // tm,),
        in_specs=[pl.BlockSpec((tm, width), lambda i: (i, 0)) for width in widths]
        + [pl.BlockSpec(wp.shape, lambda i: (0, 0)), row, row, par, par] + [row, par][:n_below],
        out_specs=[row, par, par] + [row, par][:n_below],
        out_shape=[row_shape, par_shape, par_shape] + [jax.ShapeDtypeStruct((S, D_MODEL), BF16), par_shape][:n_below],
        scratch_shapes=[pltpu.VMEM((tm, D_MODEL), F32)],
        compiler_params=_params(("arbitrary",)),
    )(*d_pieces, wp, x, d_out, scale, shift, *(below[:2] if below is not None else ()))


def _gw_pieces(h, d_pieces, name):
    S = h.shape[0]
    tm = 512
    tk = _pick(S, MATMUL_ROWS)
    n_p = len(d_pieces)
    widths = [p.shape[1] for p in d_pieces]

    def body(h_ref, *rest):
        d_refs, o_refs = rest[:n_p], rest[n_p:]

        @pl.when(pl.program_id(1) == 0)
        def _():
            for o_ref in o_refs:
                o_ref[...] = jnp.zeros_like(o_ref)

        h_t = jnp.transpose(h_ref[...])
        for d_ref, o_ref in zip(d_refs, o_refs):
            o_ref[...] += _dot_raw(h_t, d_ref[...], "nn")

    return pl.pallas_call(
        body, name=name, grid=(D_MODEL // tm, S // tk),
        in_specs=[pl.BlockSpec((tk, tm), lambda i, k: (k, i))] + [pl.BlockSpec((tk, width), lambda i, k: (k, 0)) for width in widths],
        out_specs=[pl.BlockSpec((tm, width), lambda i, k: (i, 0)) for width in widths],
        out_shape=[jax.ShapeDtypeStruct((D_MODEL, width), F32) for width in widths],
        compiler_params=_params(("parallel", "arbitrary")),
    )(h, *d_pieces)


def _mod_proj(x, scale, shift, wp, name):
    S = x.shape[0]
    N = wp.shape[1]
    tm = _pick(S, MATMUL_ROWS)
    tn = _pick(N, (1408, 1024, 512, 256, 128))

    def body(x_ref, sc_ref, sh_ref, w_ref, h_ref, o_ref, h_scr):
        @pl.when(pl.program_id(1) == 0)
        def _():
            h_new = _modulate(x_ref[...], sc_ref[...], sh_ref[...]).astype(BF16)
            h_scr[...] = h_new
            h_ref[...] = h_new

        o_ref[...] = _dot_raw(h_scr[...], w_ref[...], "nn")

    row = pl.BlockSpec((tm, D_MODEL), lambda i, j: (i, 0))
    par = pl.BlockSpec((1, D_MODEL), lambda i, j: (0, 0))
    return pl.pallas_call(
        body, name=name, grid=(S // tm, N // tn),
        in_specs=[row, par, par, pl.BlockSpec((D_MODEL, tn), lambda i, j: (0, j))],
        out_specs=[row, pl.BlockSpec((tm, tn), lambda i, j: (i, j))],
        out_shape=[jax.ShapeDtypeStruct((S, D_MODEL), BF16), jax.ShapeDtypeStruct((S, N), F32)],
        scratch_shapes=[pltpu.VMEM((tm, D_MODEL), BF16)],
        compiler_params=_params(("parallel", "arbitrary")),
    )(x, scale, shift, wp)


def _mix_out(o_a, proj, o_b, gnw, onw, wout, x, gate_w, t, name):
    S = x.shape[0]

    def body(oa_ref, z_ref, ob_ref, gn_ref, on_ref, w_ref, x_ref, g_ref, mixed_ref, y_ref, xo_ref):
        (mixed,) = _mix_post_fn(oa_ref[...], z_ref[...], ob_ref[...], gn_ref[...], on_ref[...])
        mixed = mixed.astype(BF16)
        mixed_ref[...] = mixed
        y = _dot_raw(mixed, w_ref[...], "nn")
        y_ref[...] = y.astype(BF16)
        xo_ref[...] = x_ref[...] + g_ref[...] * y

    half, row = _row_spec(t, 512, 0), _row_spec(t, D_MODEL, 0)
    return pl.pallas_call(
        body, name=name, grid=(S // t,),
        in_specs=[half, _row_spec(t, 512, 3), half, _full_spec(gnw.shape), _full_spec(onw.shape), _full_spec(wout.shape),
                  row, _full_spec(gate_w.shape)],
        out_specs=[row, row, row],
        out_shape=[jax.ShapeDtypeStruct((S, D_MODEL), BF16)] * 2 + [jax.ShapeDtypeStruct((S, D_MODEL), F32)],
        compiler_params=_params(("parallel",)),
    )(o_a, proj, o_b, gnw, onw, wout, x, gate_w)


def _mixer_fwd(x1, scale, shift, gate_w, cos_p, sin_p, W):
    S = x1.shape[0]
    tm = _pick(S, (512, 256, 128))
    tv = _pick(S, TOKEN_ROWS)
    ta = _pick(S, (512, 256, 128))
    nc = S // CHUNK
    h2, proj = _mod_proj(x1, scale, shift, W["wp"], "mix_proj")
    qkvc = _conv_fwd(proj, W["conv_w"], tv, "gdn_conv")
    kab = (proj, tv, 128, 21)
    q_a, k_a, v_a, gb = _rowwise(_gdn_pre_fn, [(qkvc, tv, 1536, 0), kab], [W["alog_p"], W["dt_p"]],
                                 [(tv, 512, F32)] * 3 + [(tv, 128, F32)], S // tv, "gdn_pre")
    ti = _pick(S, INTRA_ROWS)
    intra = _rowwise(_gdn_intra_fn, [(q_a, ti, 512, 0), (k_a, ti, 512, 0), (v_a, ti, 512, 0), (gb, ti, 128, 0)],
                     [], [(ti, 512, F32)] * 4 + [(ti, CHUNK, F32)] * 4 + [(ti // 8, 512, F32)] + [(ti, CHUNK, F32)] * 4,
                     S // ti, "gdn_intra")
    u, wk, qd, kd, qks, gl, invs = intra[0], intra[1], intra[2], intra[3], tuple(intra[4:8]), intra[8], tuple(intra[9:])
    o_a, s_prev = _gdn_scan_fwd(u, wk, qd, kd, qks, gl, "gdn_scan")
    mla_params = [W["qnw"], W["kvnw"], W["wuq"], W["wukv"], W["qn_w"], W["qr_w"], W["kn_w"], W["kr_w"]]
    def mla_pre_with_vt(*a):
        q_, k_, v_ = _mla_pre_fn(*a)
        return q_, k_, v_, jnp.transpose(v_)

    q_b, k_b, v_b, vt_b = _rowwise(mla_pre_with_vt,
                                   [(proj, tv, 256, 8), (proj, tv, 384, 6), kab, (cos_p, tv, 128, 0), (sin_p, tv, 128, 0)],
                                   mla_params, [(tv, 1024, BF16), (tv, 1024, BF16), (tv, 512, BF16), (512, tv, BF16, "across")],
                                   S // tv, "mla_pre")
    o_b, lse = _attn_fwd(q_b, k_b, vt_b, ta, "mla_attn")
    mixed, y, x2 = _mix_out(o_a, proj, o_b, W["gnw"], W["onw"], W["wout"], x1, gate_w, tv, "mix_out")
    saved = (h2, proj, qkvc, q_a, k_a, v_a, gb, u, wk, qd, kd, qks, gl, invs, s_prev, o_a, q_b, k_b, v_b, o_b, lse, mixed, y)
    return x2, saved


def _mixer_bwd(d_out, dy, x1, scale, shift, cos_p, sin_p, W, saved, below):
    (h2, proj, qkvc, q_a, k_a, v_a, gb, u, wk, qd, kd, qks, gl, invs, s_prev, o_a, q_b, k_b, v_b, o_b, lse, mixed, y) = saved
    S = x1.shape[0]
    tm = _pick(S, (512, 256, 128))
    tv = _pick(S, TOKEN_ROWS)
    ta = _pick(S, (512, 256, 128))
    nc = S // CHUNK
    G = {}
    G["wout"] = _mm(mixed, dy, "tn", name="mix_gwout")
    (do_a, dz, do_b), (G["gnw"], G["onw"], stats) = _rowwise_bwd(
        _mix_post_fn, [(o_a, tv, 512, 0), (proj, tv, 512, 3), (o_b, tv, 512, 0)], [], [W["gnw"], W["onw"]],
        [], S // tv, "mix_dpost", row_dtypes=(F32, BF16, F32), dout_from=(dy, W["wout"]),
        across=(lse, lambda r_vals, d_rows, lse_tile: _attn_stats(r_vals[2], d_rows[2], lse_tile)))
    dq_b, dk_b, dv_b = _attn_bwd(q_b, k_b, v_b, do_b, stats, ta, "mla_dattn")
    kab = (proj, tv, 128, 21)
    mla_params = [W["qnw"], W["kvnw"], W["wuq"], W["wukv"], W["qn_w"], W["qr_w"], W["kn_w"], W["kr_w"]]
    (d_ckv, d_cq, d_kab), mla_grads = _rowwise_bwd(
        _mla_pre_fn, [(proj, tv, 256, 8), (proj, tv, 384, 6), kab], [(cos_p, tv, 128, 0), (sin_p, tv, 128, 0)], mla_params,
        [(dq_b, tv, 1024, 0), (dk_b, tv, 1024, 0), (dv_b, tv, 512, 0)], S // tv, "mla_dpre", row_dtypes=(BF16, BF16, F32))
    for key, g in zip(("qnw", "kvnw", "wuq", "wukv", "qn_w", "qr_w", "kn_w", "kr_w"), mla_grads):
        G[key] = g
    scan_grads = _gdn_scan_bwd(u, wk, qd, kd, qks, gl, s_prev, do_a, "gdn_dscan")
    ti = _pick(S, INTRA_ROWS)
    intra_douts = [(scan_grads[i], ti, 512, 0) for i in range(4)] + [(scan_grads[4 + i], ti, CHUNK, 0) for i in range(4)]
    intra_douts.append((scan_grads[8], ti // 8, 512, 0))
    (dq_a, dk_a, dv_a, d_gb), _ = _rowwise_bwd(
        _gdn_intra_fn, [(q_a, ti, 512, 0), (k_a, ti, 512, 0), (v_a, ti, 512, 0), (gb, ti, 128, 0)],
        [(x_, ti, CHUNK, 0) for x_ in invs], [], intra_douts, S // ti, "gdn_dintra")
    (d_qkvc, d_kab), (G["alog_p"], G["dt_p"]) = _rowwise_bwd(
        _gdn_pre_fn, [(qkvc, tv, 1536, 0), kab], [], [W["alog_p"], W["dt_p"]],
        [(dq_a, tv, 512, 0), (dk_a, tv, 512, 0), (dv_a, tv, 512, 0), (d_gb, tv, 128, 0)], S // tv, "gdn_dpre",
        adds=[(1, d_kab)], row_dtypes=(F32, BF16))
    d_qkv, g_conv = _conv_bwd(proj, d_qkvc, W["conv_w"], tv, "gdn_dconv")
    G["conv_w"] = g_conv[:4]
    d_proj = [d_qkv, dz, d_ckv, d_cq, d_kab]
    G["wp"] = _gw_pieces(h2, d_proj, "mix_gwp")
    dx1, G["s2"], G["sh2"], d_below, dg_below = _dproj_dmod(d_proj, W["wp"], x1, d_out, scale, shift, "mix_dproj", below=below)
    return dx1, d_below, dg_below, G


def _local_step(x, target, mod, cos_p, sin_p, W1, mixer_weights, ffn2_weights, ffn_grad_ready, mixer_grads_ready):
    sh1, s1, g1, sh2, s2, g2, sh3, s3, g3 = [mod[:, D_MODEL * i:D_MODEL * (i + 1)] for i in range(N_MOD)]
    x1, saved1 = _ffn_fwd(x, s1, sh1, g1, W1["f1_w8"], W1["f1_wo4"], "ffn1")
    W = mixer_weights(x1)
    x2, saved2 = _mixer_fwd(x1, s2, sh2, g2, cos_p, sin_p, W)
    W.update(ffn2_weights(x2))
    (dx3, loss_row), saved3 = _ffn_fwd(x2, s3, sh3, g3, W["f2_w8"], W["f2_wo4"], "ffn2", target=target)
    dx2, d_s3, d_sh3, d_g3, dy, d_g2 = _ffn_bwd(dx3, x2, s3, sh3, g3, W["f2_w8"], W["f2_wo4"], saved3, "ffn2",
                                                ffn_grad_ready("f2"), below=(saved2[-1], g2, 1.0))
    dx1, df1, d_g1, G = _mixer_bwd(dx2, dy, x1, s2, sh2, cos_p, sin_p, W, saved2, below=(saved1[4], g1, 0.5))
    d_sh2, d_s2 = G.pop("sh2"), G.pop("s2")
    saved1 = saved1[:5] + (df1, d_g1 + mixer_grads_ready(G))
    dx, d_s1, d_sh1, d_g1 = _ffn_bwd(dx1, x, s1, sh1, g1, W1["f1_w8"], W1["f1_wo4"], saved1, "ffn1", ffn_grad_ready("f1"))
    d_mod = jnp.concatenate([d_sh1, d_s1, d_g1, d_sh2, d_s2, d_g2, d_sh3, d_s3, d_g3], axis=1)
    return loss_row, dx, d_mod


WEIGHT_NAMES = ("w_ada", "b_ada", "ffn1_w_in", "ffn1_w_out", "w_in", "gdn_conv_w", "gdn_a_log", "gdn_dt_bias", "gdn_norm_w",
                "mla_q_norm_w", "mla_w_uq", "mla_kv_norm_w", "mla_w_ukv", "qkn_q_nope", "qkn_q_rope", "qkn_k_nope",
                "qkn_k_rope", "mla_out_norm_w", "w_out", "ffn2_w_in", "ffn2_w_out")
FFN_SHARDED = ("ffn1_w_in", "ffn1_w_out", "ffn2_w_in", "ffn2_w_out")
TRANSPOSED_ENTRY = ("ffn1_w_in", "ffn2_w_in", "w_in", "mla_w_uq")
SHEETED = (("w_in", "col"), ("gdn_conv_w", "col"), ("mla_w_uq", "col"), ("mla_w_ukv", "col"), ("w_out", "row"))
MOD_ROWS = N_MOD * D_MODEL // 128
SMALL = {"gdn_a_log": (MOD_ROWS, 1, 64, 4), "gdn_dt_bias": (MOD_ROWS + 1, 1, 64, 4), "gdn_norm_w": (MOD_ROWS + 2, 1, 0, 128),
         "mla_q_norm_w": (MOD_ROWS + 3, 3, 0, 384), "mla_kv_norm_w": (MOD_ROWS + 6, 2, 0, 256),
         "qkn_q_nope": (MOD_ROWS + 8, 1, 0, 128), "qkn_q_rope": (MOD_ROWS + 9, 1, 0, 64), "qkn_k_nope": (MOD_ROWS + 10, 1, 0, 128),
         "qkn_k_rope": (MOD_ROWS + 11, 1, 0, 64), "mla_out_norm_w": (MOD_ROWS + 12, 1, 0, 128)}
LOSS_ROW = MOD_ROWS + 13
CONV_ROW, CONV_ROWS = 88, 4 * 1536 // 128
SHEET_ROWS = CONV_ROW + CONV_ROWS


def _to_sheet(flat, dtype, sublanes):
    n = flat.shape[-1]
    unit = sublanes * 128
    pad = (-n) % unit
    flat = jnp.pad(flat.astype(dtype), [(0, 0)] * (flat.ndim - 1) + [(0, pad)])
    return flat.reshape(flat.shape[:-1] + ((n + pad) // 128, 128))


def _small_sheet(b_like, small):
    sheet = jnp.zeros((SHEET_ROWS, 128), F32).at[:MOD_ROWS].set(b_like.reshape(MOD_ROWS, 128))
    for name, (row, rows, lane, n) in SMALL.items():
        v = small[name].reshape(1, n)
        if rows == 1:
            sheet = sheet.at[row, lane:lane + n].set(v[0])
        else:
            sheet = sheet.at[row:row + rows].set(v.reshape(rows, 128))
    return sheet


def _from_small_sheet(sheet):
    out = {"b_ada": sheet[:MOD_ROWS].reshape(1, N_MOD * D_MODEL)}
    for name, (row, rows, lane, n) in SMALL.items():
        out[name] = sheet[row, lane:lane + n].reshape(1, n) if rows == 1 else sheet[row:row + rows].reshape(1, n)
    return out


def kernel(x, c, positions, w_ada, b_ada, ffn1_w_in, ffn1_w_out, w_in, gdn_conv_w, gdn_a_log, gdn_dt_bias, gdn_norm_w, mla_q_norm_w, mla_w_uq, mla_kv_norm_w, mla_w_ukv, qkn_q_nope, qkn_q_rope, qkn_k_nope, qkn_k_rope, mla_out_norm_w, w_out, ffn2_w_in, ffn2_w_out, loss_target, m_w_ada, m_b_ada, m_ffn1_w_in, m_ffn1_w_out, m_w_in, m_gdn_conv_w, m_gdn_a_log, m_gdn_dt_bias, m_gdn_norm_w, m_mla_q_norm_w, m_mla_w_uq, m_mla_kv_norm_w, m_mla_w_ukv, m_qkn_q_nope, m_qkn_q_rope, m_qkn_k_nope, m_qkn_k_rope, m_mla_out_norm_w, m_w_out, m_ffn2_w_in, m_ffn2_w_out, v_w_ada, v_b_ada, v_ffn1_w_in, v_ffn1_w_out, v_w_in, v_gdn_conv_w, v_gdn_a_log, v_gdn_dt_bias, v_gdn_norm_w, v_mla_q_norm_w, v_mla_w_uq, v_mla_kv_norm_w, v_mla_w_ukv, v_qkn_q_nope, v_qkn_q_rope, v_qkn_k_nope, v_qkn_k_rope, v_mla_out_norm_w, v_w_out, v_ffn2_w_in, v_ffn2_w_out):
    args = locals()
    w = {n: args[n] for n in WEIGHT_NAMES}
    m = {n: args["m_" + n] for n in WEIGHT_NAMES}
    v = {n: args["v_" + n] for n in WEIGHT_NAMES}
    me = 4 * lax.axis_index("x") + 2 * lax.axis_index("y") + lax.axis_index("c")
    cols = N_MOD * D_MODEL // N_DEV
    shard = {n: w[n][0] for n in FFN_SHARDED + tuple(s[0] for s in SHEETED)}

    sc = c * _sigmoid(c)
    first = _to_sheet(jnp.concatenate([sc.reshape(-1), shard["gdn_conv_w"].reshape(-1)]), F32, 8)
    (first_all,) = _all_gather([first], "gather_c")
    sc_all = first_all[:, :D_MODEL // 128].reshape(N_DEV, D_MODEL)
    n_taps = shard["gdn_conv_w"].size
    conv_all = first_all.reshape(N_DEV, -1)[:, D_MODEL:D_MODEL + n_taps].reshape(N_DEV, 4, -1)
    b_mine = lax.dynamic_slice(b_ada, (0, me * cols), (1, cols))
    mod_cols = _mm(sc_all, w_ada[0], "nn", name="ada_mod", extra_params=[b_mine], epi=lambda acc, b_: (acc + b_,))
    (mod_all,) = _all_to_all([_to_sheet(mod_cols, F32, 8)], "scatter_mod")
    mod = mod_all.reshape(N_DEV, -1)[:, :cols].reshape(1, N_MOD * D_MODEL)

    f1_shards, mod = lax.optimization_barrier(([shard["ffn1_w_in"].astype(BF16), shard["ffn1_w_out"].astype(BF16)], mod))
    f1_w8, f1_out = _all_gather(f1_shards, "gather_w1")
    travel = [s for s in SHEETED if s[0] != "gdn_conv_w"]
    tied = lax.optimization_barrier(([shard[n].astype(BF16) for n, _ in travel], f1_w8))
    f1_w8 = tied[1]
    mixer_w = _exchange_start(tied[0], False, "gather_wm_start")
    ffn2_w = _exchange_start([shard["ffn2_w_in"].astype(BF16) + mixer_w[4][0:1, 0:1].astype(BF16),
                              shard["ffn2_w_out"].astype(BF16)], False, "gather_w2_start")
    mod = mod + ffn2_w[4][0:1, 0:1]
    W1 = dict(f1_w8=f1_w8, f1_wo4=f1_out.reshape(HID_PIECES, FFN_PIECE, D_MODEL))

    def mixer_weights(after):
        got = _exchange_wait(mixer_w, False, after, "gather_wm_wait")
        P = {n: jnp.concatenate(list(g), axis=1) if kind == "col" else g.reshape(-1, g.shape[-1])
             for (n, kind), g in zip(travel, got)}
        P["gdn_conv_w"] = jnp.concatenate(list(conv_all), axis=1)
        for n in SMALL:
            P[n] = w[n]
        return _pack_weights(P)

    def ffn2_weights(after):
        f2_w8, f2_out = _exchange_wait(ffn2_w, False, after, "gather_w2_wait")
        return dict(f2_w8=f2_w8, f2_wo4=f2_out.reshape(HID_PIECES, FFN_PIECE, D_MODEL))

    pending, small_grads = {}, {}

    def ffn_grad_ready(tag):
        def ready(which, g):
            pieces = g if which == "w8" else g.reshape((N_DEV,) + shard["ffn1_w_out"].shape)
            pending[tag + which] = _exchange_start([pieces], True, "scatter_%s_%s_start" % (tag, which))
            return pending[tag + which][4]
        return ready

    def mixer_grads_ready(G):
        g_full = _unpack_grads(G)
        small_grads.update({n: g_full[n] for n in SMALL})
        small_grads["gdn_conv_w"] = g_full["gdn_conv_w"]
        pieces = []
        for n, kind in travel:
            r, cc = shard[n].shape
            g = g_full[n].astype(BF16)
            pieces.append(jnp.stack([g[:, cc * p:cc * (p + 1)] for p in range(N_DEV)]) if kind == "col"
                          else g.reshape(N_DEV, r, cc))
        pending["mixer"] = _exchange_start(pieces, True, "scatter_mx_start")
        return pending["mixer"][4][0:1, 0:1]

    cos_p, sin_p = _rope_tables(positions[0])
    loss_row, dx, d_mod = _local_step(x[0], loss_target[0], mod, cos_p, sin_p, W1, mixer_weights, ffn2_weights,
                                      ffn_grad_ready, mixer_grads_ready)

    sheet = _small_sheet(d_mod, small_grads).at[LOSS_ROW].set(loss_row[0, :128])
    sheet = sheet.at[CONV_ROW:CONV_ROW + CONV_ROWS].set(small_grads["gdn_conv_w"].reshape(CONV_ROWS, 128))
    (sheets,) = _all_gather([sheet], "gather_small")
    summed = _sum_devices(sheets, "sum_small")
    d_mod_all = sheets[:, :MOD_ROWS].reshape(N_DEV, N_MOD * D_MODEL)
    d_mod_mine = lax.dynamic_slice(d_mod_all, (0, me * cols), (N_DEV, cols))
    grads = _from_small_sheet(summed)
    grads["w_ada"] = _mm(sc_all, d_mod_mine, "tn", name="ada_gw", hi=True)
    conv_taps = shard["gdn_conv_w"].shape[1]
    grads["gdn_conv_w"] = lax.dynamic_slice(summed[CONV_ROW:CONV_ROW + CONV_ROWS].reshape(4, -1), (0, me * conv_taps),
                                            (4, conv_taps))
    loss = summed[LOSS_ROW, 0]

    delta, new_m, new_v = {}, {}, {}
    arrived = {}
    for n, key in zip(FFN_SHARDED, ("f1w8", "f1wo4", "f2w8", "f2wo4")):
        (arrived[n],) = _exchange_wait(pending[key], True, summed, "scatter_%s_wait" % key)
    arrived.update(zip([n for n, _ in travel], _exchange_wait(pending["mixer"], True, summed, "scatter_mx_wait")))
    for n, parts in arrived.items():
        if n in TRANSPOSED_ENTRY:
            res = _sum_adamw(parts, w[n][0].T, m[n][0].T, v[n][0].T, "adamw_" + n, transposed=True)
            grads[n], delta[n], new_m[n], new_v[n] = [r.T for r in res]
        else:
            grads[n], delta[n], new_m[n], new_v[n] = _sum_adamw(parts, w[n][0], m[n][0], v[n][0], "adamw_" + n)
    for n in ("w_ada", "gdn_conv_w"):
        delta[n], new_m[n], new_v[n] = _adamw(w[n][0], grads[n], m[n][0], v[n][0], "adamw_" + n)
    small_in = [_small_sheet(t["b_ada"], t) for t in (w, grads, m, v)]
    for res, out in zip(_adamw(*small_in, "adamw_small"), (delta, new_m, new_v)):
        out.update(_from_small_sheet(res))

    def shaped(d):
        return [d[n].reshape(w[n].shape) for n in WEIGHT_NAMES]

    return (loss, dx[None], *shaped(grads), *shaped(delta), *shaped(new_m), *shaped(new_v))
```

```python
import functools

import jax
import jax.numpy as jnp
import numpy as np
from jax import lax
from jax.experimental import pallas as pl
from jax.experimental.pallas import tpu as pltpu

F32 = jnp.float32
BF16 = jnp.bfloat16

D_MODEL = 1024
D_FF = 2816
N_MOD = 9
HEADS = 4
HEAD_DIM = 128
CHUNK = 64
EPS = 1e-6
ROPE = 64
Q_LORA = 384
KV_LORA = 256
N_IN = 2760
N_IN_PACKED = 2816
ROPE_BASE = 10000.0
LOG2_E = 1.4426950408889634
N_DEV = 8

ADAM_LR = 0.001
ADAM_B1 = 0.9
ADAM_B2 = 0.999
ADAM_EPS = 1e-08
ADAM_WD = 0.01
ADAM_STEP = 10

VMEM_LIMIT_BYTES = 56 * 1024 * 1024
MATMUL_ROWS = (1024, 512, 256, 128)
MESH = pl.DeviceIdType.MESH


def _params(sem=None):
    return pltpu.CompilerParams(dimension_semantics=sem, vmem_limit_bytes=VMEM_LIMIT_BYTES)


def _pick(dim, prefs):
    for p in prefs:
        if dim % p == 0:
            return p
    return dim


_DIMS = {"nn": (((1,), (0,)), ((), ())), "nt": (((1,), (1,)), ((), ())), "tn": (((0,), (0,)), ((), ()))}


def _dot_raw(a, b, mode):
    return lax.dot_general(a.astype(BF16), b.astype(BF16), _DIMS[mode], preferred_element_type=F32)


def _dot_hi(a, b, mode="nn"):
    return lax.dot_general(a, b, _DIMS[mode], precision=lax.Precision.HIGHEST, preferred_element_type=F32)


@functools.partial(jax.custom_vjp, nondiff_argnums=(2,))
def _bdot(a, b, mode):
    return _dot_raw(a, b, mode)


def _bdot_fwd(a, b, mode):
    return _dot_raw(a, b, mode), (a, b)


def _bdot_bwd(mode, res, g):
    a, b = res
    if mode == "nn":
        return _dot_raw(g, b, "nt"), _dot_raw(a, g, "tn")
    if mode == "nt":
        return _dot_raw(g, b, "nn"), _dot_raw(g, a, "tn")
    return _dot_raw(b, g, "nt"), _dot_raw(a, g, "nn")


_bdot.defvjp(_bdot_fwd, _bdot_bwd)


def _mm(a, b, mode, *, name, out_dtypes=(F32,), epi=None, extras=(), extra_params=(), hi=False,
        tm=None, tn=None, tk=None):
    if mode == "nn":
        (M, K), (_, N) = a.shape, b.shape
    elif mode == "nt":
        (M, K), (N, _) = a.shape, b.shape
    else:
        (K, M), (_, N) = a.shape, b.shape
    tm = tm or _pick(M, (512, 1408, 256, 128) if mode == "tn" else MATMUL_ROWS + (384, 352))
    tn = tn or _pick(N, (1024, 1408, 768, 512, 384, 256, 128))
    tk = tk or _pick(K, (1024, 1408, 512, 384, 256, 128))
    a_spec = {"nn": pl.BlockSpec((tm, tk), lambda i, j, k: (i, k)), "nt": pl.BlockSpec((tm, tk), lambda i, j, k: (i, k)),
              "tn": pl.BlockSpec((tk, tm), lambda i, j, k: (k, i))}[mode]
    b_spec = {"nn": pl.BlockSpec((tk, tn), lambda i, j, k: (k, j)), "nt": pl.BlockSpec((tn, tk), lambda i, j, k: (j, k)),
              "tn": pl.BlockSpec((tk, tn), lambda i, j, k: (k, j))}[mode]
    mn_spec = pl.BlockSpec((tm, tn), lambda i, j, k: (i, j))
    return _mmg(a, b, mode, name=name, grid=(M // tm, N // tn, K // tk), a_spec=a_spec, b_spec=b_spec, out_spec=mn_spec,
                out_shapes=[jax.ShapeDtypeStruct((M, N), dt) for dt in out_dtypes], acc_shape=(tm, tn), epi=epi,
                extras=list(extras) + list(extra_params),
                extra_specs=[mn_spec] * len(extras) + [pl.BlockSpec((1, tn), lambda i, j, k: (0, j))] * len(extra_params),
                hi=hi)


def _mmg(a, b, mode, *, name, grid, a_spec, b_spec, out_spec, out_shapes, acc_shape, epi=None, extras=(),
         extra_specs=(), hi=False):
    nk = grid[2]
    n_e, n_o = len(extras), len(out_shapes)

    def body(*refs):
        a_ref, b_ref = refs[:2]
        e_refs = refs[2:2 + n_e]
        o_refs = refs[2 + n_e:2 + n_e + n_o]
        acc_ref = refs[-1]
        k = pl.program_id(2)

        @pl.when(k == 0)
        def _():
            acc_ref[...] = jnp.zeros_like(acc_ref)

        if hi:
            acc_ref[...] += _dot_hi(a_ref[...].astype(F32), b_ref[...].astype(F32), mode)
        else:
            acc_ref[...] += _dot_raw(a_ref[...], b_ref[...], mode)

        @pl.when(k == nk - 1)
        def _():
            acc = acc_ref[...]
            outs = (acc,) if epi is None else epi(acc, *[e[...].astype(F32) for e in e_refs])
            for o_ref, o in zip(o_refs, outs):
                o_ref[...] = o.astype(o_ref.dtype)

    outs = pl.pallas_call(
        body, name=name, grid=grid,
        in_specs=[a_spec, b_spec] + list(extra_specs),
        out_specs=[out_spec] * n_o,
        out_shape=list(out_shapes),
        scratch_shapes=[pltpu.VMEM(acc_shape, F32)],
        compiler_params=_params(("parallel", "parallel", "arbitrary")),
    )(a, b, *extras)
    return outs if n_o > 1 else outs[0]


def _row_spec(th, cw, ci):
    return pl.BlockSpec((th, cw), lambda i: (i, ci))


def _full_spec(shape):
    return pl.BlockSpec(shape, lambda i: (0,) * len(shape))


def _rowwise(fn, rows, params, outs, n_steps, name):
    n_r, n_p, n_o = len(rows), len(params), len(outs)

    def body(*refs):
        vals = [r[...].astype(F32) for r in refs[:n_r + n_p]]
        res = fn(*vals)
        for o_ref, o in zip(refs[n_r + n_p:], res):
            o_ref[...] = o.astype(o_ref.dtype)

    across = [len(o) == 4 for o in outs]
    res = pl.pallas_call(
        body, name=name, grid=(n_steps,),
        in_specs=[_row_spec(th, cw, ci) for (_, th, cw, ci) in rows] + [_full_spec(p.shape) for p in params],
        out_specs=[pl.BlockSpec((o[0], o[1]), lambda i: (0, i)) if ac else _row_spec(o[0], o[1], 0)
                   for o, ac in zip(outs, across)],
        out_shape=[jax.ShapeDtypeStruct((o[0], n_steps * o[1]) if ac else (n_steps * o[0], o[1]), o[2])
                   for o, ac in zip(outs, across)],
        compiler_params=_params(("parallel",)),
    )(*[r[0] for r in rows], *params)
    return res


def _rowwise_bwd(fn, rows, aux, params, douts, n_steps, name, row_dtypes=None, adds=(), across=None, dout_from=None):
    n_r, n_a, n_p, n_d, n_add = len(rows), len(aux), len(params), len(douts), len(adds)
    row_dtypes = row_dtypes or (F32,) * n_r

    def body(*refs):
        it = iter(refs)
        r_vals = [next(it)[...].astype(F32) for _ in range(n_r)]
        a_vals = [next(it)[...].astype(F32) for _ in range(n_a)]
        p_vals = [next(it)[...].astype(F32) for _ in range(n_p)]
        d_vals = [next(it)[...].astype(F32) for _ in range(n_d)]
        add_vals = [next(it)[...].astype(F32) for _ in range(n_add)]
        across_in = next(it) if across is not None else None
        if dout_from is not None:
            d_vals = [_dot_raw(next(it)[...], next(it)[...], "nt")]
        dr_refs = [next(it) for _ in range(n_r)]
        dp_refs = [next(it) for _ in range(n_p)]

        def f(*rp):
            return tuple(fn(*rp[:n_r], *a_vals, *rp[n_r:]))

        _, vjp = jax.vjp(f, *r_vals, *p_vals)
        grads = list(vjp(tuple(d_vals)))
        for (ri, _), av in zip(adds, add_vals):
            grads[ri] = grads[ri] + av
        for dr_ref, g in zip(dr_refs, grads[:n_r]):
            dr_ref[...] = g.astype(dr_ref.dtype)
        if across is not None:
            next(it)[...] = across[1](r_vals, grads[:n_r], across_in[...])

        @pl.when(pl.program_id(0) == 0)
        def _():
            for dp_ref in dp_refs:
                dp_ref[...] = jnp.zeros_like(dp_ref)

        for dp_ref, g in zip(dp_refs, grads[n_r:]):
            dp_ref[...] += g

    all_rows = list(rows) + list(aux) + list(douts) + [(arr,) + tuple(rows[ri][1:3]) + (0,) for ri, arr in adds]
    in_specs = ([_row_spec(th, cw, ci) for (_, th, cw, ci) in list(rows) + list(aux)]
                + [_full_spec(p.shape) for p in params]
                + [_row_spec(th, cw, ci) for (_, th, cw, ci) in all_rows[n_r + n_a:]])
    across_specs = [] if across is None else [pl.BlockSpec((8, rows[0][1]), lambda i: (0, i))]
    from_specs = [] if dout_from is None else [_row_spec(rows[0][1], dout_from[0].shape[1], 0), _full_spec(dout_from[1].shape)]
    res = pl.pallas_call(
        body, name=name, grid=(n_steps,),
        in_specs=in_specs + across_specs + from_specs,
        out_specs=[_row_spec(th, cw, 0) for (_, th, cw, _) in rows] + [_full_spec(p.shape) for p in params] + across_specs,
        out_shape=[jax.ShapeDtypeStruct((n_steps * th, cw), dt) for (_, th, cw, _), dt in zip(rows, row_dtypes)]
        + [jax.ShapeDtypeStruct(p.shape, F32) for p in params]
        + [jax.ShapeDtypeStruct(across[0].shape, F32) for _ in across_specs],
        compiler_params=_params(("arbitrary",)),
    )(*[r[0] for r in list(rows) + list(aux)], *params, *[r[0] for r in all_rows[n_r + n_a:]],
      *[across[0] for _ in across_specs], *(dout_from or ()))
    return res[:n_r], res[n_r:]


def _sigmoid(x):
    return lax.logistic(x)


def _silu(x):
    return x * _sigmoid(x)


def _rms(x, w=None, n=None):
    n = n or x.shape[-1]
    y = x * lax.rsqrt(jnp.sum(x * x, axis=-1, keepdims=True) * (1.0 / n) + EPS)
    return y if w is None else y * w


def _modulate(x, scale, shift):
    return _rms(x) * (1.0 + scale) + shift


def _softplus(x):
    return jnp.maximum(x, 0.0) + jnp.log1p(jnp.exp(-jnp.abs(x)))


@jax.custom_vjp
def _rot_half64(x):
    lane = lax.broadcasted_iota(jnp.int32, x.shape, 1)
    up = pltpu.roll(x, 96, 1)
    down = pltpu.roll(x, 32, 1)
    return jnp.where(lane < 32, up, jnp.where(lane < 64, down, 0.0))


_rot_half64.defvjp(lambda x: (_rot_half64(x), None), lambda _, g: (_rot_half64(g),))


def _rope128(x, cos_p, sin_p):
    return x * cos_p + _rot_half64(x) * sin_p


def _gdn_pre_fn(qkvc, kab, alog_p, dt_p):
    a = _silu(qkvc)
    qs, ks = [], []
    for h in range(HEADS):
        qh = a[:, HEAD_DIM * h:HEAD_DIM * (h + 1)]
        kh = a[:, 512 + HEAD_DIM * h:512 + HEAD_DIM * (h + 1)]
        qs.append(qh * lax.rsqrt(jnp.sum(qh * qh, axis=-1, keepdims=True) + EPS) * (HEAD_DIM ** -0.5))
        ks.append(kh * lax.rsqrt(jnp.sum(kh * kh, axis=-1, keepdims=True) + EPS))
    lane = lax.broadcasted_iota(jnp.int32, kab.shape, 1)
    g_full = -jnp.exp(alog_p) * _softplus(kab + dt_p)
    b_full = _sigmoid(kab)
    gb = jnp.where((lane >= 64) & (lane < 68), g_full, jnp.where((lane >= 68) & (lane < 72), b_full, 0.0))
    return jnp.concatenate(qs, axis=1), jnp.concatenate(ks, axis=1), a[:, 1024:1536], gb


INTRA_ROWS = (512, 256, 128, 64)
TOKEN_ROWS = (512, 256, 128)

_BNN = (((2,), (1,)), ((0,), (0,)))
_BNT = (((2,), (2,)), ((0,), (0,)))
_BTN = (((1,), (1,)), ((0,), (0,)))


def _split_bf16(a):
    hi = a.astype(BF16)
    return hi, (a - hi.astype(F32)).astype(BF16)


def _dot3_raw(a, b, dims):
    a_hi, a_lo = _split_bf16(a)
    b_hi, b_lo = _split_bf16(b)
    dot = lambda x_, y_: lax.dot_general(x_, y_, dims, preferred_element_type=F32)
    return dot(a_hi, b_hi) + (dot(a_hi, b_lo) + dot(a_lo, b_hi))


@functools.partial(jax.custom_vjp, nondiff_argnums=(2, 3))
def _dot3(a, b, nt, exact_bwd=True):
    return _dot3_raw(a, b, _BNT if nt else _BNN)


def _dot3_fwd(a, b, nt, exact_bwd):
    return _dot3_raw(a, b, _BNT if nt else _BNN), (a, b)


def _dot3_bwd(nt, exact_bwd, res, g):
    a, b = res
    if exact_bwd:
        dot = _dot3_raw
    else:
        dot = lambda x_, y_, d_: lax.dot_general(x_.astype(BF16), y_.astype(BF16), d_, preferred_element_type=F32)
    if nt:
        return dot(g, b, _BNN), dot(jnp.swapaxes(g, 1, 2), a, _BNN)
    return dot(g, b, _BNT), dot(jnp.swapaxes(a, 1, 2), g, _BNN)


_dot3.defvjp(_dot3_fwd, _dot3_bwd)


@functools.partial(jax.custom_vjp, nondiff_argnums=(2,))
def _bdot_b(a, b, nt):
    return lax.dot_general(a.astype(BF16), b.astype(BF16), _BNT if nt else _BNN, preferred_element_type=F32)


def _bdot_b_fwd(a, b, nt):
    return _bdot_b(a, b, nt), (a, b)


def _bdot_b_bwd(nt, res, g):
    a, b = res
    dot = lambda x_, y_, d_: lax.dot_general(x_.astype(BF16), y_.astype(BF16), d_, preferred_element_type=F32)
    if nt:
        return dot(g, b, _BNN), dot(jnp.swapaxes(g, 1, 2), a, _BNN)
    return dot(g, b, _BNT), dot(jnp.swapaxes(a, 1, 2), g, _BNN)


_bdot_b.defvjp(_bdot_b_fwd, _bdot_b_bwd)


@jax.custom_vjp
def _inverse_given(a_mat, inv):
    return inv


def _inverse_given_bwd(inv, g):
    inv_t = jnp.swapaxes(inv, 1, 2)
    return -_dot3_raw(_dot3_raw(inv_t, g, _BNN), inv_t, _BNN), jnp.zeros_like(inv)


_inverse_given.defvjp(lambda a_mat, inv: (inv, inv), _inverse_given_bwd)


def _intra_batched(q, k, v, g_col, b_col, inv_known=None):
    c = CHUNK
    nb = q.shape[0]
    row = lax.broadcasted_iota(jnp.int32, (1, c, c), 1)
    col = lax.broadcasted_iota(jnp.int32, (1, c, c), 2)
    incl, strict, eye = row >= col, row > col, row == col
    tri = jnp.broadcast_to(jnp.where(incl, 1.0, 0.0).astype(F32), (nb, c, c))
    ident = jnp.where(eye, 1.0, 0.0).astype(F32)
    g_wide = _dot3(tri, jnp.broadcast_to(g_col, (nb, c, HEAD_DIM)), False)
    g_i = g_wide[:, :, :c]
    g_j = jnp.sum(jnp.where(eye, g_i, 0.0), axis=1, keepdims=True)
    decay = jnp.where(incl, jnp.exp(jnp.where(incl, g_i - g_j, 0.0)), 0.0)
    kk = _bdot_b(k, k, True)
    a_mat = jnp.where(strict, b_col * kk * decay, 0.0)
    if inv_known is None:
        x_pow = -a_mat
        inv = ident + x_pow
        for _ in range(5):
            x_pow = _dot3(x_pow, x_pow, False, False)
            inv = inv + _dot3(inv, x_pow, False, False)
    else:
        inv = _inverse_given(a_mat, inv_known)
    e_wide = jnp.exp(g_wide)
    u = _dot3(inv, v * b_col, False)
    wk = _dot3(inv, k * b_col * e_wide, False)
    qk = _bdot_b(q, k, True) * decay
    last = lax.broadcasted_iota(jnp.int32, (1, c, HEAD_DIM), 1) == c - 1
    g_last = jnp.sum(jnp.where(last, g_wide, 0.0), axis=1, keepdims=True)
    qd = q * e_wide
    kd = k * jnp.exp(g_last - g_wide)
    gl = jnp.broadcast_to(jnp.exp(g_last), (nb, 8, HEAD_DIM))
    return u, wk, qd, kd, qk, gl, inv


def _gdn_intra_fn(q, k, v, gb, *inv_known):
    t = q.shape[0]
    nch = t // CHUNK
    lane = lax.broadcasted_iota(jnp.int32, gb.shape, 1)

    def heads_first(x_):
        return jnp.concatenate([x_[:, HEAD_DIM * h:HEAD_DIM * (h + 1)].reshape(nch, CHUNK, HEAD_DIM) for h in range(HEADS)],
                               axis=0)

    def column(first_lane):
        return jnp.concatenate([jnp.sum(jnp.where(lane == first_lane + h, gb, 0.0), axis=1, keepdims=True)
                                .reshape(nch, CHUNK, 1) for h in range(HEADS)], axis=0)

    known = jnp.concatenate([x_.reshape(nch, CHUNK, CHUNK) for x_ in inv_known], axis=0) if inv_known else None
    u, wk, qd, kd, qk, gl, inv = _intra_batched(heads_first(q), heads_first(k), heads_first(v), column(64), column(68), known)

    def rows_first(x_):
        r, w_ = x_.shape[1], x_.shape[2]
        return jnp.concatenate([x_[nch * h:nch * (h + 1)].reshape(nch * r, w_) for h in range(HEADS)], axis=1)

    per_head = lambda x_: [x_[nch * h:nch * (h + 1)].reshape(t, CHUNK) for h in range(HEADS)]
    outs = (rows_first(u), rows_first(wk), rows_first(qd), rows_first(kd), *per_head(qk), rows_first(gl))
    return outs if inv_known else outs + tuple(per_head(inv))


def _scan_step(s0, u, wk, qd, kd, qk, gl):
    v_new = u - _bdot_b(wk, s0, False)
    o = _bdot_b(qd, s0, False) + _bdot_b(qk, v_new, False)
    s1 = s0 * gl[:, 0:1, :] + _bdot_b(jnp.swapaxes(kd, 1, 2), v_new, False)
    return o, s1


def _mix_post_fn(o_a, z, o_b, gnw, onw):
    parts = [_rms(o_a[:, HEAD_DIM * h:HEAD_DIM * (h + 1)], gnw) * _silu(z[:, HEAD_DIM * h:HEAD_DIM * (h + 1)])
             for h in range(HEADS)]
    parts += [_rms(o_b[:, HEAD_DIM * h:HEAD_DIM * (h + 1)], onw) for h in range(HEADS)]
    return (jnp.concatenate(parts, axis=1),)


def _mla_pre_fn(ckv, cq, kab, cos_p, sin_p, qnw, kvnw, wuq, wukv, qn_w, qr_w, kn_w, kr_w):
    scale = (HEAD_DIM + ROPE) ** -0.5 * LOG2_E
    qf = _bdot(_rms(cq, qnw), wuq, "nn")
    kvf = _bdot(_rms(ckv, kvnw), wukv, "nn")
    lane = lax.broadcasted_iota(jnp.int32, kab.shape, 1)
    kr = _rope128(_rms(jnp.where(lane < ROPE, kab, 0.0), kr_w, n=ROPE), cos_p, sin_p)
    qs, ks = [], []
    for h in range(HEADS):
        qn = _rms(qf[:, 256 * h:256 * h + 128], qn_w) * scale
        qr = _rope128(_rms(qf[:, 256 * h + 128:256 * h + 256], qr_w, n=ROPE), cos_p, sin_p) * scale
        qs += [qn, qr]
        ks += [_rms(kvf[:, 128 * h:128 * (h + 1)], kn_w), kr]
    return jnp.concatenate(qs, axis=1), jnp.concatenate(ks, axis=1), kvf[:, 512:]


def _conv_fwd(proj, conv_w, alog_p, dt_p, tm, name):
    S = proj.shape[0]
    C = 1536
    nb = tm // 8

    def body(x_ref, prev_ref, kab_ref, w_ref, al_ref, dt_ref, o_ref, q_ref, k_ref, v_ref, gb_ref, ext_ref):
        i = pl.program_id(0)
        ext_ref[0:8, :] = jnp.where(i > 0, prev_ref[...], 0.0)
        ext_ref[8:, :] = x_ref[...]
        acc = jnp.zeros((tm, C), F32)
        for k in range(4):
            acc = acc + w_ref[k:k + 1, :] * ext_ref[pl.ds(5 + k, tm), :]
        o_ref[...] = acc
        for ref, val in zip((q_ref, k_ref, v_ref, gb_ref), _gdn_pre_fn(acc, kab_ref[...], al_ref[...], dt_ref[...])):
            ref[...] = val

    third = pl.BlockSpec((tm, 512), lambda i: (i, 0))
    return pl.pallas_call(
        body, name=name, grid=(S // tm,),
        in_specs=[pl.BlockSpec((tm, C), lambda i: (i, 0)),
                  pl.BlockSpec((8, C), lambda i: (jnp.maximum(i * nb - 1, 0), 0)),
                  _row_spec(tm, 128, 21), pl.BlockSpec((4, C), lambda i: (0, 0)), _full_spec(alog_p.shape), _full_spec(dt_p.shape)],
        out_specs=[pl.BlockSpec((tm, C), lambda i: (i, 0)), third, third, third, _row_spec(tm, 128, 0)],
        out_shape=[jax.ShapeDtypeStruct((S, C), F32)] + [jax.ShapeDtypeStruct((S, 512), F32)] * 3
        + [jax.ShapeDtypeStruct((S, 128), F32)],
        scratch_shapes=[pltpu.VMEM((tm + 8, C), F32)],
        compiler_params=_params(("arbitrary",)),
    )(proj, proj, proj, conv_w, alog_p, dt_p)


def _conv_bwd(proj, dout, conv_w, tm, name):
    S = proj.shape[0]
    C = 1536
    nb = tm // 8
    n_steps = S // tm

    def body(x_ref, prev_ref, d_ref, next_ref, w_ref, dx_ref, dw_ref, xext_ref, dext_ref):
        i = pl.program_id(0)
        xext_ref[0:8, :] = jnp.where(i > 0, prev_ref[...], 0.0)
        xext_ref[8:, :] = x_ref[...]
        dext_ref[0:tm, :] = d_ref[...]
        dext_ref[tm:, :] = jnp.where(i < n_steps - 1, next_ref[...], 0.0)
        d = d_ref[...]
        acc = jnp.zeros((tm, C), F32)
        dws = []
        for k in range(4):
            acc = acc + w_ref[k:k + 1, :] * dext_ref[pl.ds(3 - k, tm), :]
            dws.append(jnp.sum(d * xext_ref[pl.ds(5 + k, tm), :], axis=0, keepdims=True))
        dx_ref[...] = acc.astype(dx_ref.dtype)

        @pl.when(i == 0)
        def _():
            dw_ref[...] = jnp.zeros_like(dw_ref)

        dw_ref[...] += jnp.concatenate(dws + [jnp.zeros((4, C), F32)], axis=0)

    return pl.pallas_call(
        body, name=name, grid=(n_steps,),
        in_specs=[pl.BlockSpec((tm, C), lambda i: (i, 0)),
                  pl.BlockSpec((8, C), lambda i: (jnp.maximum(i * nb - 1, 0), 0)),
                  pl.BlockSpec((tm, C), lambda i: (i, 0)),
                  pl.BlockSpec((8, C), lambda i: (jnp.minimum((i + 1) * nb, S // 8 - 1), 0)),
                  pl.BlockSpec((4, C), lambda i: (0, 0))],
        out_specs=[pl.BlockSpec((tm, C), lambda i: (i, 0)), pl.BlockSpec((8, C), lambda i: (0, 0))],
        out_shape=[jax.ShapeDtypeStruct((S, C), BF16), jax.ShapeDtypeStruct((8, C), F32)],
        scratch_shapes=[pltpu.VMEM((tm + 8, C), F32), pltpu.VMEM((tm + 8, C), F32)],
        compiler_params=_params(("arbitrary",)),
    )(proj, proj, dout, dout, conv_w)


SCAN_CHUNKS = (8, 4, 2, 1)


def _gdn_scan_fwd(u, wk, qd, kd, qks, gl, name):
    S = u.shape[0]
    nc = S // CHUNK
    cs = _pick(nc, SCAN_CHUNKS)
    W = HEADS * HEAD_DIM

    def body(u_ref, wk_ref, qd_ref, kd_ref, qk0, qk1, qk2, qk3, gl_ref, o_ref, sp_ref, s_ref):
        @pl.when(pl.program_id(0) == 0)
        def _():
            s_ref[...] = jnp.zeros_like(s_ref)

        state = s_ref[...]
        for c in range(cs):
            rows, gl_rows = slice(CHUNK * c, CHUNK * (c + 1)), slice(8 * c, 8 * (c + 1))
            sp_ref[c] = state
            o, state = _scan_step(state, _heads(u_ref, HEAD_DIM, rows), _heads(wk_ref, HEAD_DIM, rows),
                                  _heads(qd_ref, HEAD_DIM, rows), _heads(kd_ref, HEAD_DIM, rows),
                                  jnp.stack([r[rows, :] for r in (qk0, qk1, qk2, qk3)]), _heads(gl_ref, HEAD_DIM, gl_rows))
            for h in range(HEADS):
                o_ref[rows, HEAD_DIM * h:HEAD_DIM * (h + 1)] = o[h]
        s_ref[...] = state

    row = pl.BlockSpec((cs * CHUNK, W), lambda n: (n, 0))
    qk_spec = pl.BlockSpec((cs * CHUNK, CHUNK), lambda n: (n, 0))
    return pl.pallas_call(
        body, name=name, grid=(nc // cs,),
        in_specs=[row, row, row, row, qk_spec, qk_spec, qk_spec, qk_spec, pl.BlockSpec((cs * 8, W), lambda n: (n, 0))],
        out_specs=[row, pl.BlockSpec((cs, HEADS, HEAD_DIM, HEAD_DIM), lambda n: (n, 0, 0, 0))],
        out_shape=[jax.ShapeDtypeStruct((S, W), F32), jax.ShapeDtypeStruct((nc, HEADS, HEAD_DIM, HEAD_DIM), F32)],
        scratch_shapes=[pltpu.VMEM((HEADS, HEAD_DIM, HEAD_DIM), F32)],
        compiler_params=_params(("arbitrary",)),
    )(u, wk, qd, kd, *qks, gl)


def _gdn_scan_bwd(u, wk, qd, kd, qks, gl, s_prev, d_o, name):
    S = u.shape[0]
    nc = S // CHUNK
    cs = _pick(nc, SCAN_CHUNKS)
    nb = nc // cs
    W = HEADS * HEAD_DIM

    def body(u_ref, wk_ref, qd_ref, kd_ref, qk0, qk1, qk2, qk3, gl_ref, sp_ref, do_ref,
             du_ref, dwk_ref, dqd_ref, dkd_ref, dqk0, dqk1, dqk2, dqk3, dgl_ref, ds_ref):
        @pl.when(pl.program_id(0) == 0)
        def _():
            ds_ref[...] = jnp.zeros_like(ds_ref)

        d_state = ds_ref[...]
        for c in reversed(range(cs)):
            rows, gl_rows = slice(CHUNK * c, CHUNK * (c + 1)), slice(8 * c, 8 * (c + 1))
            _, vjp = jax.vjp(_scan_step, sp_ref[c], _heads(u_ref, HEAD_DIM, rows), _heads(wk_ref, HEAD_DIM, rows),
                             _heads(qd_ref, HEAD_DIM, rows), _heads(kd_ref, HEAD_DIM, rows),
                             jnp.stack([r[rows, :] for r in (qk0, qk1, qk2, qk3)]), _heads(gl_ref, HEAD_DIM, gl_rows))
            d_state, du, dwk, dqd, dkd, dqk, dgl = vjp((_heads(do_ref, HEAD_DIM, rows), d_state))
            for h, dqk_ref in enumerate((dqk0, dqk1, dqk2, dqk3)):
                sl = slice(HEAD_DIM * h, HEAD_DIM * (h + 1))
                du_ref[rows, sl] = du[h]
                dwk_ref[rows, sl] = dwk[h]
                dqd_ref[rows, sl] = dqd[h]
                dkd_ref[rows, sl] = dkd[h]
                dqk_ref[rows, :] = dqk[h]
                dgl_ref[gl_rows, sl] = dgl[h]
        ds_ref[...] = d_state

    rev = lambda n: (nb - 1 - n, 0)
    row = pl.BlockSpec((cs * CHUNK, W), rev)
    qk_spec = pl.BlockSpec((cs * CHUNK, CHUNK), rev)
    gl_spec = pl.BlockSpec((cs * 8, W), rev)
    qk_shape = jax.ShapeDtypeStruct((S, CHUNK), F32)
    row_shape = jax.ShapeDtypeStruct((S, W), F32)
    return pl.pallas_call(
        body, name=name, grid=(nb,),
        in_specs=[row, row, row, row, qk_spec, qk_spec, qk_spec, qk_spec, gl_spec,
                  pl.BlockSpec((cs, HEADS, HEAD_DIM, HEAD_DIM), lambda n: (nb - 1 - n, 0, 0, 0)), row],
        out_specs=[row, row, row, row, qk_spec, qk_spec, qk_spec, qk_spec, gl_spec],
        out_shape=[row_shape] * 4 + [qk_shape] * 4 + [jax.ShapeDtypeStruct((nc * 8, W), F32)],
        scratch_shapes=[pltpu.VMEM((HEADS, HEAD_DIM, HEAD_DIM), F32)],
        compiler_params=_params(("arbitrary",)),
    )(u, wk, qd, kd, *qks, gl, s_prev, d_o)


NEG = -1e30


def _chunk_mask(i, j, t, transposed=False):
    q_axis, k_axis = (1, 0) if transposed else (0, 1)
    r = (i * t + lax.broadcasted_iota(jnp.int32, (t, t), q_axis)) // CHUNK
    c = (j * t + lax.broadcasted_iota(jnp.int32, (t, t), k_axis)) // CHUNK
    return c <= r


def _tile_pairs(n, by_key):
    pairs = [(i, j) for j in range(n) for i in range(j, n)] if by_key else [(i, j) for i in range(n) for j in range(i + 1)]
    return jnp.asarray(np.array([p[0] for p in pairs], np.int32)), jnp.asarray(np.array([p[1] for p in pairs], np.int32))


def _heads(ref, width, rows=slice(None)):
    return jnp.stack([ref[rows, width * h:width * (h + 1)] for h in range(HEADS)])


def _bmm(a, b, dims):
    return lax.dot_general(a.astype(BF16), b.astype(BF16), dims, preferred_element_type=F32)


def _attn_fwd(q, k, v_t, t, name):
    S = q.shape[0]
    n = S // t
    qi, kj = _tile_pairs(n, by_key=False)

    def body(qi_ref, kj_ref, q_ref, k_ref, vt_ref, o_ref, lse_ref, m_ref, l_ref, acc_ref):
        i, j = qi_ref[pl.program_id(0)], kj_ref[pl.program_id(0)]

        @pl.when(j == 0)
        def _():
            m_ref[...] = jnp.full_like(m_ref, NEG)
            l_ref[...] = jnp.zeros_like(l_ref)
            acc_ref[...] = jnp.zeros_like(acc_ref)

        def update(masked):
            s_t = _bmm(_heads(k_ref, 256), _heads(q_ref, 256), _BNT)
            if masked:
                s_t = jnp.where(_chunk_mask(i, j, t, transposed=True)[None], s_t, NEG)
            m_old = m_ref[...]
            m_new = jnp.maximum(m_old, jnp.max(s_t, axis=1, keepdims=True))
            p_t = jnp.exp2(s_t - m_new)
            alpha = jnp.exp2(m_old - m_new)
            l_ref[...] = alpha * l_ref[...] + jnp.sum(p_t, axis=1, keepdims=True)
            v_heads = jnp.stack([vt_ref[HEAD_DIM * h:HEAD_DIM * (h + 1), :] for h in range(HEADS)])
            acc_ref[...] = alpha * acc_ref[...] + _bmm(v_heads, p_t, _BNN)
            m_ref[...] = m_new

        @pl.when(j < i)
        def _():
            update(False)

        @pl.when(j == i)
        def _():
            update(True)
            for h in range(HEADS):
                sl = slice(HEAD_DIM * h, HEAD_DIM * (h + 1))
                o_ref[:, sl] = jnp.transpose(acc_ref[h] / l_ref[h])
                lse_ref[h:h + 1, :] = m_ref[h] + jnp.log(l_ref[h]) * LOG2_E
            lse_ref[HEADS:, :] = jnp.zeros((8 - HEADS, t), F32)

    row = lambda p, qi_, kj_: (qi_[p], 0)
    return pl.pallas_call(
        body, name=name,
        grid_spec=pltpu.PrefetchScalarGridSpec(
            num_scalar_prefetch=2, grid=(qi.shape[0],),
            in_specs=[pl.BlockSpec((t, HEADS * 256), row), pl.BlockSpec((t, HEADS * 256), lambda p, qi_, kj_: (kj_[p], 0)),
                      pl.BlockSpec((HEADS * HEAD_DIM, t), lambda p, qi_, kj_: (0, kj_[p]))],
            out_specs=[pl.BlockSpec((t, HEADS * HEAD_DIM), row), pl.BlockSpec((8, t), lambda p, qi_, kj_: (0, qi_[p]))],
            scratch_shapes=[pltpu.VMEM((HEADS, 1, t), F32), pltpu.VMEM((HEADS, 1, t), F32),
                            pltpu.VMEM((HEADS, HEAD_DIM, t), F32)]),
        out_shape=[jax.ShapeDtypeStruct((S, HEADS * HEAD_DIM), F32), jax.ShapeDtypeStruct((8, S), F32)],
        compiler_params=_params(("arbitrary",)),
    )(qi, kj, q, k, v_t)


def _attn_stats(o, d_o, lse):
    lane = lax.broadcasted_iota(jnp.int32, (o.shape[0], HEAD_DIM), 1)
    stats = jnp.zeros((o.shape[0], HEAD_DIM), F32)
    for h in range(HEADS):
        sl = slice(HEAD_DIM * h, HEAD_DIM * (h + 1))
        delta = jnp.sum(d_o[:, sl] * o[:, sl], axis=1, keepdims=True)
        stats = stats + jnp.where(lane == HEADS + h, delta, 0.0)
    return lse + jnp.transpose(stats)[0:8, :]


BWD_GROUP = 2


def _attn_bwd(q, k, v, d_o, stats, t, name):
    S = q.shape[0]
    n = S // t
    groups = HEADS // BWD_GROUP
    gq, gv = BWD_GROUP * 256, BWD_GROUP * HEAD_DIM
    qi, kj = _tile_pairs(n, by_key=True)
    n_pairs = qi.shape[0]
    st = stats.reshape(2, groups, BWD_GROUP, S).transpose(1, 0, 2, 3).reshape(groups, 2 * BWD_GROUP, S)
    st = jnp.pad(st, ((0, 0), (0, 8 - 2 * BWD_GROUP), (0, 0)))

    def heads(ref, width, rows=slice(None)):
        return jnp.stack([ref[rows, width * h:width * (h + 1)] for h in range(BWD_GROUP)])

    def body(qi_ref, kj_ref, q_ref, k_ref, v_ref, do_ref, st_ref, dq_hbm, dk_ref, dv_ref, dq_acc, sem):
        g, p = pl.program_id(0), pl.program_id(1)
        i, j = qi_ref[p], kj_ref[p]

        @pl.when(i == j)
        def _():
            dk_ref[...] = jnp.zeros_like(dk_ref)
            dv_ref[...] = jnp.zeros_like(dv_ref)

        def update(masked):
            qh, kh = heads(q_ref, 256), heads(k_ref, 256)
            d_out = heads(do_ref, HEAD_DIM)
            stv = st_ref[...]
            lse_row = jnp.stack([stv[h:h + 1, :] for h in range(BWD_GROUP)])
            delta_row = jnp.stack([stv[BWD_GROUP + h:BWD_GROUP + h + 1, :] for h in range(BWD_GROUP)])
            s_t = _bmm(kh, qh, _BNT)
            p_t = jnp.exp2(s_t - lse_row)
            if masked:
                p_t = jnp.where(_chunk_mask(i, j, t, transposed=True)[None], p_t, 0.0)
            dv = _bmm(p_t, d_out, _BNN)
            dp_t = _bmm(heads(v_ref, HEAD_DIM), d_out, _BNT)
            ds_t = p_t * (dp_t - delta_row)
            dk = _bmm(ds_t, qh, _BNN)
            dq = _bmm(ds_t, kh, _BTN)
            rows = pl.ds(pl.multiple_of(i * t, t), t)
            for h in range(BWD_GROUP):
                dk_ref[:, 256 * h:256 * (h + 1)] += dk[h]
                dv_ref[:, HEAD_DIM * h:HEAD_DIM * (h + 1)] += dv[h]

            @pl.when(j == 0)
            def _():
                for h in range(BWD_GROUP):
                    dq_acc[rows, 256 * h:256 * (h + 1)] = dq[h]

            @pl.when(j > 0)
            def _():
                for h in range(BWD_GROUP):
                    dq_acc[rows, 256 * h:256 * (h + 1)] += dq[h]

        @pl.when(i == j)
        def _():
            update(True)

        @pl.when(i > j)
        def _():
            update(False)

        @pl.when(i == n - 1)
        def _():
            dk_ref[...] *= 1.0 / LOG2_E

        @pl.when(p == n_pairs - 1)
        def _():
            dq_acc[...] *= 1.0 / LOG2_E
            for gg in range(groups):
                @pl.when(g == gg)
                def _():
                    cp = pltpu.make_async_copy(dq_acc, dq_hbm.at[:, gq * gg:gq * (gg + 1)], sem)
                    cp.start()
                    cp.wait()

    q_blk = lambda g, p, qi_, kj_: (qi_[p], g)
    k_blk = lambda g, p, qi_, kj_: (kj_[p], g)
    return pl.pallas_call(
        body, name=name,
        grid_spec=pltpu.PrefetchScalarGridSpec(
            num_scalar_prefetch=2, grid=(groups, n_pairs),
            in_specs=[pl.BlockSpec((t, gq), q_blk), pl.BlockSpec((t, gq), k_blk), pl.BlockSpec((t, gv), k_blk),
                      pl.BlockSpec((t, gv), q_blk), pl.BlockSpec((None, 8, t), lambda g, p, qi_, kj_: (g, 0, qi_[p]))],
            out_specs=[pl.BlockSpec(memory_space=pl.ANY), pl.BlockSpec((t, gq), k_blk), pl.BlockSpec((t, gv), k_blk)],
            scratch_shapes=[pltpu.VMEM((S, gq), F32), pltpu.SemaphoreType.DMA]),
        out_shape=[jax.ShapeDtypeStruct((S, HEADS * 256), F32), jax.ShapeDtypeStruct((S, HEADS * 256), F32),
                   jax.ShapeDtypeStruct((S, HEADS * HEAD_DIM), F32)],
        compiler_params=_params(("arbitrary", "arbitrary")),
    )(qi, kj, q, k, v, d_o, st)


FFN_PIECE = 2 * D_FF // N_DEV
HID_PIECES = D_FF // FFN_PIECE


def _after_specs(after):
    return [] if after is None else [pl.BlockSpec(memory_space=pl.ANY)]


def _after_args(after):
    return [] if after is None else [after]


def _ffn_gw8(h, d_gate, d_up, name, after=None):
    S = h.shape[0]
    tm = 512
    tk = _pick(S, MATMUL_ROWS)
    nk = S // tk

    def body(h_ref, dg_ref, du_ref, *rest):
        o_ref, acc_ref = rest[-2:]
        k = pl.program_id(1)

        @pl.when(k == 0)
        def _():
            acc_ref[...] = jnp.zeros_like(acc_ref)

        h_t = jnp.transpose(h_ref[...])
        for p in range(HID_PIECES):
            acc_ref[p] += _dot_raw(h_t, dg_ref[p], "nn")
            acc_ref[HID_PIECES + p] += _dot_raw(h_t, du_ref[p], "nn")

        @pl.when(k == nk - 1)
        def _():
            o_ref[...] = acc_ref[...].astype(o_ref.dtype)

    d_spec = pl.BlockSpec((HID_PIECES, tk, FFN_PIECE), lambda i, k: (0, k, 0))
    return pl.pallas_call(
        body, name=name, grid=(D_MODEL // tm, nk),
        in_specs=[pl.BlockSpec((tk, tm), lambda i, k: (k, i)), d_spec, d_spec] + _after_specs(after),
        out_specs=pl.BlockSpec((2 * HID_PIECES, tm, FFN_PIECE), lambda i, k: (0, i, 0)),
        out_shape=jax.ShapeDtypeStruct((2 * HID_PIECES, D_MODEL, FFN_PIECE), BF16),
        scratch_shapes=[pltpu.VMEM((2 * HID_PIECES, tm, FFN_PIECE), F32)],
        compiler_params=_params(("parallel", "arbitrary")),
    )(h, d_gate, d_up, *_after_args(after))


EPILOGUE_ROWS = 256


def _dmod_epilogue(acc_ref, x_ref, do_ref, sc_ref, sh_ref, dx_ref, dsc_ref, dsh_ref, below, below_in, below_out):
    rows_total = acc_ref.shape[0]
    step = min(EPILOGUE_ROWS, rows_total)
    dsc, dsh, dg = 0.0, 0.0, 0.0
    for r in range(rows_total // step):
        rows = slice(step * r, step * (r + 1))
        _, vjp = jax.vjp(_modulate, x_ref[rows, :], sc_ref[...], sh_ref[...])
        dx, dsc_r, dsh_r = vjp(acc_ref[rows, :])
        dx = dx + do_ref[rows, :]
        dx_ref[rows, :] = dx
        dsc, dsh = dsc + dsc_r, dsh + dsh_r
        if below is not None:
            coef = below[2]
            below_out[0][rows, :] = (coef * below_in[1][...] * dx).astype(below_out[0].dtype)
            dg = dg + jnp.sum(coef * below_in[0][rows, :] * dx, axis=0, keepdims=True)
    dsc_ref[...] += dsc
    dsh_ref[...] += dsh
    if below is not None:
        below_out[1][...] += dg


def _ffn_dh(d_gate, d_up, w8, x, d_out, scale, shift, name, after=None, below=None):
    S = d_gate.shape[1]
    tm = _pick(S, MATMUL_ROWS)
    n_below = 0 if below is None else 2

    def body(dg_ref, du_ref, wg_ref, wu_ref, x_ref, do_ref, sc_ref, sh_ref, *rest):
        below_in = rest[:n_below]
        outs = rest[len(rest) - 4 - n_below:]
        dx_ref, dsc_ref, dsh_ref = outs[:3]
        below_out, acc_ref = outs[3:3 + n_below], outs[-1]
        i, k = pl.program_id(0), pl.program_id(1)

        @pl.when(k == 0)
        def _():
            acc_ref[...] = jnp.zeros_like(acc_ref)

        acc_ref[...] += _dot_raw(dg_ref[...], wg_ref[...], "nt") + _dot_raw(du_ref[...], wu_ref[...], "nt")

        @pl.when((k == 0) & (i == 0))
        def _():
            for r in (dsc_ref, dsh_ref) + tuple(below_out[1:]):
                r[...] = jnp.zeros_like(r)

        @pl.when(k == HID_PIECES - 1)
        def _():
            _dmod_epilogue(acc_ref, x_ref, do_ref, sc_ref, sh_ref, dx_ref, dsc_ref, dsh_ref, below, below_in, below_out)

    d_spec = pl.BlockSpec((None, tm, FFN_PIECE), lambda i, k: (k, i, 0))
    row = pl.BlockSpec((tm, D_MODEL), lambda i, k: (i, 0))
    par = pl.BlockSpec((1, D_MODEL), lambda i, k: (0, 0))
    row_shape, par_shape = jax.ShapeDtypeStruct((S, D_MODEL), F32), jax.ShapeDtypeStruct((1, D_MODEL), F32)
    return pl.pallas_call(
        body, name=name, grid=(S // tm, HID_PIECES),
        in_specs=[d_spec, d_spec,
                  pl.BlockSpec((None, D_MODEL, FFN_PIECE), lambda i, k: (k, 0, 0)),
                  pl.BlockSpec((None, D_MODEL, FFN_PIECE), lambda i, k: (k + HID_PIECES, 0, 0)),
                  row, row, par, par] + [row, par][:n_below] + _after_specs(after),
        out_specs=[row, par, par] + [row, par][:n_below],
        out_shape=[row_shape, par_shape, par_shape] + [jax.ShapeDtypeStruct((S, D_MODEL), BF16), par_shape][:n_below],
        scratch_shapes=[pltpu.VMEM((tm, D_MODEL), F32)],
        compiler_params=_params(("arbitrary", "arbitrary")),
    )(d_gate, d_up, w8, w8, x, d_out, scale, shift, *(below[:2] if below is not None else ()), *_after_args(after))


def _swiglu_bwd(d_hid, hid_by_gate, hid_by_up):
    return d_hid * hid_by_gate, d_hid * hid_by_up


def _adamw_math(w_, g_, m_, v_):
    m_ = ADAM_B1 * m_ + (1.0 - ADAM_B1) * g_
    v_ = ADAM_B2 * v_ + (1.0 - ADAM_B2) * (g_ * g_)
    m_hat = m_ / (1.0 - ADAM_B1 ** ADAM_STEP)
    v_hat = v_ / (1.0 - ADAM_B2 ** ADAM_STEP)
    return -ADAM_LR * (m_hat / (jnp.sqrt(v_hat) + ADAM_EPS) + ADAM_WD * w_), m_, v_


def _adamw(w, g, m, v, name):
    R, C = w.shape
    tr = _pick(R, (256, 176, 128, 64, 32, 16, 8))

    def body(w_ref, g_ref, m_ref, v_ref, d_ref, nm_ref, nv_ref):
        d_ref[...], nm_ref[...], nv_ref[...] = _adamw_math(w_ref[...], g_ref[...], m_ref[...], v_ref[...])

    spec = pl.BlockSpec((tr, C), lambda i: (i, 0))
    return pl.pallas_call(
        body, name=name, grid=(R // tr,),
        in_specs=[spec] * 4, out_specs=[spec] * 3,
        out_shape=[jax.ShapeDtypeStruct((R, C), F32)] * 3,
        compiler_params=_params(("parallel",)),
    )(w, g, m, v)


def _sum_adamw(parts, w, m, v, name, transposed=False):
    _, R, C = parts.shape
    tr = _pick(R, (256, 176, 128, 64, 32, 16, 8))

    def body(p_ref, w_ref, m_ref, v_ref, g_ref, d_ref, nm_ref, nv_ref):
        g_ = p_ref[0].astype(F32)
        for d in range(1, N_DEV):
            g_ = g_ + p_ref[d].astype(F32)
        if transposed:
            g_ = jnp.transpose(g_)
        g_ref[...] = g_
        d_ref[...], nm_ref[...], nv_ref[...] = _adamw_math(w_ref[...], g_, m_ref[...], v_ref[...])

    spec = pl.BlockSpec((C, tr), lambda i: (0, i)) if transposed else pl.BlockSpec((tr, C), lambda i: (i, 0))
    return pl.pallas_call(
        body, name=name, grid=(R // tr,),
        in_specs=[pl.BlockSpec((N_DEV, tr, C), lambda i: (0, i, 0)), spec, spec, spec], out_specs=[spec] * 4,
        out_shape=[jax.ShapeDtypeStruct(w.shape, F32)] * 4,
        compiler_params=_params(("parallel",)),
    )(parts, w, m, v)


def _sum_devices(parts, name):
    _, R, C = parts.shape
    tr = _pick(R, (512, 256, 176, 128, 64, 32, 16, 8))

    def body(p_ref, o_ref):
        acc = p_ref[0].astype(F32)
        for d in range(1, N_DEV):
            acc = acc + p_ref[d].astype(F32)
        o_ref[...] = acc

    return pl.pallas_call(
        body, name=name, grid=(R // tr,),
        in_specs=[pl.BlockSpec((N_DEV, tr, C), lambda i: (0, i, 0))],
        out_specs=pl.BlockSpec((tr, C), lambda i: (i, 0)),
        out_shape=jax.ShapeDtypeStruct((R, C), F32),
        compiler_params=_params(("parallel",)),
    )(parts)


def _my_place():
    return lax.axis_index("x"), lax.axis_index("y"), lax.axis_index("c")


def _all_gather(blocks, name):
    n = len(blocks)

    def body(*refs):
        x_refs, out_refs = refs[:n], refs[n:2 * n]
        send_sems, recv_sems, local_sems = refs[2 * n:]
        x, y, c = _my_place()
        me, sibling = (x, y, c), (x, y, 1 - c)
        chips = [(1 - x, y), (x, 1 - y), (1 - x, 1 - y)]

        def copy(a, k, blk, to, own=False):
            slot = out_refs[a].at[4 * blk[0] + 2 * blk[1] + blk[2]]
            return pltpu.make_async_remote_copy(
                src_ref=x_refs[a] if own else slot, dst_ref=slot,
                send_sem=send_sems.at[7 * a + k], recv_sem=recv_sems.at[7 * a + k], device_id=to, device_id_type=MESH)

        mine = [pltpu.make_async_copy(x_refs[a], out_refs[a].at[4 * x + 2 * y + c], local_sems.at[a]) for a in range(n)]
        for cp in mine:
            cp.start()
        first = []
        for j, chip in enumerate(chips):
            first += [copy(a, 1 + j, me, (*chip, c), own=True) for a in range(n)]
        first += [copy(a, 0, me, sibling, own=True) for a in range(n)]
        for cp in first:
            cp.start()
        passed = []
        for j, chip in enumerate(chips):
            for a in range(n):
                copy(a, 1 + j, (*chip, c), me).wait_recv()
                passed.append(copy(a, 4 + j, (*chip, c), sibling))
                passed[-1].start()
        for a in range(n):
            copy(a, 0, sibling, me).wait_recv()
        for j, chip in enumerate(chips):
            for a in range(n):
                copy(a, 4 + j, (*chip, 1 - c), me).wait_recv()
        for cp in first + passed:
            cp.wait_send()
        for cp in mine:
            cp.wait()

    return pl.pallas_call(
        body, name=name,
        out_shape=[jax.ShapeDtypeStruct((N_DEV,) + b.shape, b.dtype) for b in blocks],
        in_specs=[pl.BlockSpec(memory_space=pl.ANY)] * n,
        out_specs=[pl.BlockSpec(memory_space=pl.ANY)] * n,
        scratch_shapes=[pltpu.SemaphoreType.DMA((7 * n,)), pltpu.SemaphoreType.DMA((7 * n,)), pltpu.SemaphoreType.DMA((n,))],
    )(*blocks)


def _all_to_all(pieces, name):
    n = len(pieces)

    def body(*refs):
        x_refs, out_refs = refs[:n], refs[n:2 * n]
        send_sems, recv_sems, local_sems = refs[2 * n:]
        x, y, c = _my_place()
        me = 4 * x + 2 * y + c
        mine = [pltpu.make_async_copy(x_refs[a].at[me], out_refs[a].at[me], local_sems.at[a]) for a in range(n)]
        for cp in mine:
            cp.start()
        copies = []
        for k in (2, 4, 6, 3, 5, 7, 1):
            px = 1 - x if k & 4 else x
            py = 1 - y if k & 2 else y
            pc = 1 - c if k & 1 else c
            peer = 4 * px + 2 * py + pc
            for a in range(n):
                copies.append(pltpu.make_async_remote_copy(
                    src_ref=x_refs[a].at[peer], dst_ref=out_refs[a].at[me],
                    send_sem=send_sems.at[7 * a + k - 1], recv_sem=recv_sems.at[7 * a + k - 1],
                    device_id=(px, py, pc), device_id_type=MESH))
        for cp in copies:
            cp.start()
        for cp in copies:
            cp.wait_recv()
        for cp in copies:
            cp.wait_send()
        for cp in mine:
            cp.wait()

    return pl.pallas_call(
        body, name=name,
        out_shape=[jax.ShapeDtypeStruct(p.shape, p.dtype) for p in pieces],
        in_specs=[pl.BlockSpec(memory_space=pl.ANY)] * n,
        out_specs=[pl.BlockSpec(memory_space=pl.ANY)] * n,
        scratch_shapes=[pltpu.SemaphoreType.DMA((7 * n,)), pltpu.SemaphoreType.DMA((7 * n,)), pltpu.SemaphoreType.DMA((n,))],
    )(*pieces)


def _peers():
    x, y, c = _my_place()
    out = []
    for k in (2, 4, 6, 3, 5, 7, 1):
        px = 1 - x if k & 4 else x
        py = 1 - y if k & 2 else y
        pc = 1 - c if k & 1 else c
        out.append((k, (px, py, pc), 4 * px + 2 * py + pc))
    return out


def _exchange_copies(x_refs, land_refs, send_sems, recv_sems, scatter):
    x, y, c = _my_place()
    me = 4 * x + 2 * y + c
    starts, arrivals = [], []
    for k, place, peer in _peers():
        for a, (x_ref, land_ref) in enumerate(zip(x_refs, land_refs)):
            sems = dict(send_sem=send_sems.at[7 * a + k - 1], recv_sem=recv_sems.at[7 * a + k - 1],
                        device_id=place, device_id_type=MESH)
            src = x_ref.at[peer] if scatter else x_ref
            starts.append(pltpu.make_async_remote_copy(src_ref=src, dst_ref=land_ref.at[me], **sems))
            arrivals.append(pltpu.make_async_remote_copy(src_ref=src, dst_ref=land_ref.at[peer], **sems))
    return starts, arrivals


def _exchange_start(arrays, scatter, name):
    n = len(arrays)
    hbm = pl.BlockSpec(memory_space=pltpu.HBM)
    sem = pl.BlockSpec(memory_space=pltpu.SEMAPHORE)
    lands = [lax.empty(a.shape if scatter else (N_DEV,) + a.shape, a.dtype) for a in arrays]

    def body(*refs):
        x_refs, land_refs = refs[:n], refs[n:2 * n]
        send_sems, recv_sems = refs[2 * n], refs[2 * n + 1]
        token = refs[-1]
        starts, _ = _exchange_copies(x_refs, land_refs, send_sems, recv_sems, scatter)
        for cp in starts:
            cp.start()
        token[...] = jnp.zeros_like(token)

    res = pl.pallas_call(
        body, name=name,
        out_shape=(pltpu.SemaphoreType.DMA((7 * n,)), pltpu.SemaphoreType.DMA((7 * n,)),
                   *[pltpu.HBM(a.shape, a.dtype) for a in arrays], *[pltpu.HBM(l.shape, l.dtype) for l in lands],
                   jax.ShapeDtypeStruct((8, 128), F32)),
        in_specs=[hbm] * (2 * n),
        out_specs=(sem, sem, *[hbm] * (2 * n), pl.BlockSpec(memory_space=pltpu.VMEM)),
        input_output_aliases={i: 2 + i for i in range(2 * n)},
        compiler_params=pltpu.CompilerParams(has_side_effects=pltpu.SideEffectType.DATAFLOW_SIDE_EFFECTING),
    )(*[pltpu.with_memory_space_constraint(a, pltpu.HBM) for a in arrays],
      *[pltpu.with_memory_space_constraint(l, pltpu.HBM) for l in lands])
    return res[0], res[1], list(res[2:2 + n]), list(res[2 + n:2 + 2 * n]), res[-1]


def _exchange_wait(handles, scatter, after, name):
    send_sems, recv_sems, arrays, lands, _ = handles
    n = len(arrays)
    hbm = pl.BlockSpec(memory_space=pltpu.HBM)
    sem = pl.BlockSpec(memory_space=pltpu.SEMAPHORE)

    def body(*refs):
        x_refs, land_refs = refs[:n], refs[n:2 * n]
        send_s, recv_s = refs[2 * n], refs[2 * n + 1]
        starts, arrivals = _exchange_copies(x_refs, land_refs, send_s, recv_s, scatter)
        for cp in arrivals:
            cp.wait_recv()
        for cp in starts:
            cp.wait_send()

    res = pl.pallas_call(
        body, name=name,
        out_shape=(*[pltpu.HBM(a.shape, a.dtype) for a in arrays], *[pltpu.HBM(l.shape, l.dtype) for l in lands]),
        in_specs=[hbm] * (2 * n) + [sem, sem, pl.BlockSpec(memory_space=pl.ANY)],
        out_specs=tuple([hbm] * (2 * n)),
        input_output_aliases={i: i for i in range(2 * n)},
        compiler_params=pltpu.CompilerParams(has_side_effects=pltpu.SideEffectType.DATAFLOW_SIDE_EFFECTING),
    )(*arrays, *lands, send_sems, recv_sems, after)
    me = 4 * lax.axis_index("x") + 2 * lax.axis_index("y") + lax.axis_index("c")
    out = []
    for src, got in zip(res[:n], res[n:]):
        zeros = (0,) * (got.ndim - 1)
        own = lax.dynamic_slice(src, (me,) + zeros, (1,) + src.shape[1:]) if scatter else src[None]
        out.append(lax.dynamic_update_slice(got, own, (me,) + zeros))
    return out


def _pad_lanes(v, at=0, width=128):
    return jnp.pad(v, ((0, 0), (at, width - at - v.shape[1])))


def _pack_weights(P):
    W = {}
    w = P["w_in"]
    W["wp"] = jnp.concatenate([w[:, :2048], w[:, 2440:2696], w[:, 2056:2440], w[:, 2696:2760], w[:, 2048:2056],
                               jnp.zeros((D_MODEL, N_IN_PACKED - N_IN), w.dtype)], axis=1).astype(BF16)
    W["conv_w"] = P["gdn_conv_w"].astype(F32)
    W["alog_p"] = _pad_lanes(P["gdn_a_log"], 64)
    W["dt_p"] = _pad_lanes(P["gdn_dt_bias"], 64)
    W["gnw"] = P["gdn_norm_w"]
    W["qnw"] = P["mla_q_norm_w"]
    W["kvnw"] = P["mla_kv_norm_w"]
    uq = P["mla_w_uq"].reshape(Q_LORA, HEADS, HEAD_DIM + ROPE)
    W["wuq"] = jnp.pad(uq, ((0, 0), (0, 0), (0, 256 - HEAD_DIM - ROPE))).reshape(Q_LORA, HEADS * 256).astype(BF16)
    ukv = P["mla_w_ukv"].reshape(KV_LORA, HEADS, 2, HEAD_DIM)
    W["wukv"] = ukv.transpose(0, 2, 1, 3).reshape(KV_LORA, 2 * HEADS * HEAD_DIM).astype(BF16)
    W["qn_w"] = P["qkn_q_nope"]
    W["qr_w"] = _pad_lanes(P["qkn_q_rope"])
    W["kn_w"] = P["qkn_k_nope"]
    W["kr_w"] = _pad_lanes(P["qkn_k_rope"])
    W["onw"] = P["mla_out_norm_w"]
    W["wout"] = P["w_out"].astype(BF16)
    return W


def _unpack_grads(G):
    g_qkv, g_z, g_ckv, g_cq, g_kab = G["wp"]
    uq = G["wuq"].reshape(Q_LORA, HEADS, 256)[:, :, :HEAD_DIM + ROPE].reshape(Q_LORA, HEADS * (HEAD_DIM + ROPE))
    ukv = G["wukv"].reshape(KV_LORA, 2, HEADS, HEAD_DIM).transpose(0, 2, 1, 3).reshape(KV_LORA, 2 * HEADS * HEAD_DIM)
    return {
        "w_in": jnp.concatenate([g_qkv, g_z, g_kab[:, ROPE:ROPE + 8], g_cq, g_ckv, g_kab[:, :ROPE]], axis=1),
        "gdn_conv_w": G["conv_w"], "gdn_a_log": G["alog_p"][:, 64:68], "gdn_dt_bias": G["dt_p"][:, 64:68],
        "gdn_norm_w": G["gnw"], "mla_q_norm_w": G["qnw"], "mla_w_uq": uq, "mla_kv_norm_w": G["kvnw"], "mla_w_ukv": ukv,
        "qkn_q_nope": G["qn_w"], "qkn_q_rope": G["qr_w"][:, :ROPE], "qkn_k_nope": G["kn_w"], "qkn_k_rope": G["kr_w"][:, :ROPE],
        "mla_out_norm_w": G["onw"], "w_out": G["wout"],
    }


def _rope_tables(positions):
    half = ROPE // 2
    inv_freq = ROPE_BASE ** (-jnp.arange(half, dtype=F32) / half)
    ang = positions.astype(F32)[:, None] * inv_freq
    cos, sin = jnp.cos(ang), jnp.sin(ang)
    zeros = jnp.zeros((positions.shape[0], 128 - ROPE), F32)
    return jnp.concatenate([cos, cos, zeros], axis=1), jnp.concatenate([-sin, sin, zeros], axis=1)


def _ffn_forward(x, scale, shift, gate_w, w8, wo4, name, target=None):
    S = x.shape[0]
    tm = _pick(S, (512, 256, 128))
    n = S // tm
    with_loss = target is not None

    def body(x_ref, sc_ref, sh_ref, g_ref, wg_ref, wu_ref, wo_ref, *rest):
        t_ref = rest[0] if with_loss else None
        h_ref, bg_ref, bu_ref, ht_ref = rest[with_loss:with_loss + 4]
        tail = rest[with_loss + 4:]
        h_scr, acc_ref = tail[-2:]
        i, p = pl.program_id(0), pl.program_id(1)

        @pl.when(p == 0)
        def _():
            h_new = _modulate(x_ref[...], sc_ref[...], sh_ref[...]).astype(BF16)
            h_scr[...] = h_new
            h_ref[...] = h_new
            acc_ref[...] = jnp.zeros_like(acc_ref)

        h = h_scr[...]
        gate = _dot_raw(h, wg_ref[...], "nn")
        up = _dot_raw(h, wu_ref[...], "nn")
        sg = _sigmoid(gate)
        act = gate * sg
        hid = act * up
        bg_ref[...] = (up * (sg * (1.0 + gate * (1.0 - sg)))).astype(BF16)
        bu_ref[...] = act.astype(BF16)
        ht_ref[...] = jnp.transpose(hid).astype(BF16)
        acc_ref[...] += _dot_raw(hid, wo_ref[...], "nn")

        if with_loss:
            dx_ref, df_ref, dg_ref, l_ref = tail[:4]

            @pl.when((p == 0) & (i == 0))
            def _():
                dg_ref[...] = jnp.zeros_like(dg_ref)
                l_ref[...] = jnp.zeros_like(l_ref)

            @pl.when(p == HID_PIECES - 1)
            def _():
                step = min(EPILOGUE_ROWS, tm)
                for r in range(tm // step):
                    rows = slice(step * r, step * (r + 1))
                    f = acc_ref[rows, :]
                    diff = x_ref[rows, :] + 0.5 * g_ref[...] * f - t_ref[rows, :]
                    dx = diff * (1.0 / D_MODEL)
                    dx_ref[rows, :] = dx
                    df_ref[rows, :] = (0.5 * g_ref[...] * dx).astype(df_ref.dtype)
                    dg_ref[...] += jnp.sum(0.5 * f * dx, axis=0, keepdims=True)
                    l_ref[...] += jnp.sum(diff * diff, axis=0, keepdims=True)

            @pl.when((p == HID_PIECES - 1) & (i == n - 1))
            def _():
                l_ref[...] = jnp.full(l_ref.shape, (0.5 / D_MODEL) * jnp.sum(l_ref[...]), F32)
        else:
            f_ref, xo_ref = tail[:2]

            @pl.when(p == HID_PIECES - 1)
            def _():
                f = acc_ref[...]
                f_ref[...] = f.astype(f_ref.dtype)
                xo_ref[...] = x_ref[...] + 0.5 * g_ref[...] * f

    row = pl.BlockSpec((tm, D_MODEL), lambda i, p: (i, 0))
    par = pl.BlockSpec((1, D_MODEL), lambda i, p: (0, 0))
    piece = pl.BlockSpec((None, tm, FFN_PIECE), lambda i, p: (p, i, 0))
    row_f32, row_bf16 = jax.ShapeDtypeStruct((S, D_MODEL), F32), jax.ShapeDtypeStruct((S, D_MODEL), BF16)
    par_f32 = jax.ShapeDtypeStruct((1, D_MODEL), F32)
    piece_shape = jax.ShapeDtypeStruct((HID_PIECES, S, FFN_PIECE), BF16)
    return pl.pallas_call(
        body, name=name, grid=(n, HID_PIECES),
        in_specs=[row, par, par, par,
                  pl.BlockSpec((None, D_MODEL, FFN_PIECE), lambda i, p: (p, 0, 0)),
                  pl.BlockSpec((None, D_MODEL, FFN_PIECE), lambda i, p: (p + HID_PIECES, 0, 0)),
                  pl.BlockSpec((None, FFN_PIECE, D_MODEL), lambda i, p: (p, 0, 0))] + [row] * with_loss,
        out_specs=[row, piece, piece, pl.BlockSpec((None, FFN_PIECE, tm), lambda i, p: (p, 0, i))]
        + ([row, row, par, par] if with_loss else [row, row]),
        out_shape=[row_bf16, piece_shape, piece_shape, jax.ShapeDtypeStruct((HID_PIECES, FFN_PIECE, S), BF16)]
        + ([row_f32, row_bf16, par_f32, par_f32] if with_loss else [row_bf16, row_f32]),
        scratch_shapes=[pltpu.VMEM((tm, D_MODEL), BF16), pltpu.VMEM((tm, D_MODEL), F32)],
        compiler_params=_params(("arbitrary", "arbitrary")),
    )(x, scale, shift, gate_w, w8, w8, wo4, *([target] if with_loss else []))


def _ffn_fwd(x, scale, shift, gate_w, w8, wo4, tag, target=None):
    res = _ffn_forward(x, scale, shift, gate_w, w8, wo4, tag + "_fwd", target)
    h, by_gate, by_up, hid_t = res[:4]
    if target is not None:
        dx_out, df, d_gate_w, loss_row = res[4:]
        return (dx_out, loss_row), (h, by_gate, by_up, hid_t, None, df, d_gate_w)
    f, x_out = res[4:]
    return x_out, (h, by_gate, by_up, hid_t, f, None, None)


def _ffn_bwd(d_out, x, scale, shift, gate_w, w8, wo4, saved, tag, grad_ready, below=None):
    h, gate, up, hid_t, f, df, d_gate_w = saved
    S = x.shape[0]
    tm = _pick(S, (512, 256, 128))
    tk = _pick(S, (512, 256, 128))
    n = S // tm
    if df is None:
        (df,), (d_gate_w,) = _rowwise_bwd(lambda f_, g_: (0.5 * g_ * f_,), [(f, tm, D_MODEL, 0)], [], [gate_w],
                                          [(d_out, tm, D_MODEL, 0)], n, tag + "_dres", row_dtypes=(BF16,))
    tb = _pick(S, MATMUL_ROWS)
    piece = pl.BlockSpec((None, tb, FFN_PIECE), lambda i, j, k: (j, i, 0))
    d_gate, d_up = _mmg(df, wo4, "nt", name=tag + "_ddown", grid=(S // tb, HID_PIECES, 1),
                        a_spec=pl.BlockSpec((tb, D_MODEL), lambda i, j, k: (i, 0)),
                        b_spec=pl.BlockSpec((None, FFN_PIECE, D_MODEL), lambda i, j, k: (j, 0, 0)),
                        out_spec=piece, out_shapes=[jax.ShapeDtypeStruct((HID_PIECES, S, FFN_PIECE), BF16)] * 2,
                        acc_shape=(tb, FFN_PIECE), extras=[gate, up], extra_specs=[piece, piece], epi=_swiglu_bwd)
    tk = _pick(S, MATMUL_ROWS)
    g_wo4 = _mmg(hid_t, df, "nn", name=tag + "_gwo", grid=(HID_PIECES, 1, S // tk),
                 a_spec=pl.BlockSpec((None, FFN_PIECE, tk), lambda i, j, k: (i, 0, k)),
                 b_spec=pl.BlockSpec((tk, D_MODEL), lambda i, j, k: (k, j)),
                 out_spec=pl.BlockSpec((None, FFN_PIECE, D_MODEL), lambda i, j, k: (i, 0, j)),
                 out_shapes=[jax.ShapeDtypeStruct((HID_PIECES, FFN_PIECE, D_MODEL), BF16)], acc_shape=(FFN_PIECE, D_MODEL))
    g_w8 = _ffn_gw8(h, d_gate, d_up, tag + "_gw8", after=grad_ready("wo4", g_wo4))
    res = _ffn_dh(d_gate, d_up, w8, x, d_out, scale, shift, tag + "_dh", after=grad_ready("w8", g_w8), below=below)
    return (res[0], res[1], res[2], d_gate_w) + tuple(res[3:])


def _dproj_dmod(d_pieces, wp, x, d_out, scale, shift, name, below=None):
    S = x.shape[0]
    tm = _pick(S, TOKEN_ROWS)
    widths = [p.shape[1] for p in d_pieces]
    starts = [sum(widths[:n]) for n in range(len(widths))]
    n_p = len(d_pieces)
    n_below = 0 if below is None else 2

    def body(*refs):
        dp_refs, (w_ref, x_ref, do_ref, sc_ref, sh_ref), rest = refs[:n_p], refs[n_p:n_p + 5], refs[n_p + 5:]
        below_in = rest[:n_below]
        dx_ref, dsc_ref, dsh_ref = rest[n_below:n_below + 3]
        below_out, acc_ref = rest[n_below + 3:n_below + 3 + n_below], rest[-1]

        @pl.when(pl.program_id(0) == 0)
        def _():
            for r in (dsc_ref, dsh_ref) + tuple(below_out[1:]):
                r[...] = jnp.zeros_like(r)

        acc = None
        for dp_ref, at, width in zip(dp_refs, starts, widths):
            part = _dot_raw(dp_ref[...], w_ref[:, at:at + width], "nt")
            acc = part if acc is None else acc + part
        acc_ref[...] = acc
        _dmod_epilogue(acc_ref, x_ref, do_ref, sc_ref, sh_ref, dx_ref, dsc_ref, dsh_ref, below, below_in, below_out)

    row = pl.BlockSpec((tm, D_MODEL), lambda i: (i, 0))
    par = pl.BlockSpec((1, D_MODEL), lambda i: (0, 0))
    row_shape, par_shape = jax.ShapeDtypeStruct((S, D_MODEL), F32), jax.ShapeDtypeStruct((1, D_MODEL), F32)
    return pl.pallas_call(
        body, name=name, grid=(S // tm,),
        in_specs=[pl.BlockSpec((tm, width), lambda i: (i, 0)) for width in widths]
        + [pl.BlockSpec(wp.shape, lambda i: (0, 0)), row, row, par, par] + [row, par][:n_below],
        out_specs=[row, par, par] + [row, par][:n_below],
        out_shape=[row_shape, par_shape, par_shape] + [jax.ShapeDtypeStruct((S, D_MODEL), BF16), par_shape][:n_below],
        scratch_shapes=[pltpu.VMEM((tm, D_MODEL), F32)],
        compiler_params=_params(("arbitrary",)),
    )(*d_pieces, wp, x, d_out, scale, shift, *(below[:2] if below is not None else ()))


def _gw_pieces(h, d_pieces, name):
    S = h.shape[0]
    tm = 512
    tk = _pick(S, MATMUL_ROWS)
    n_p = len(d_pieces)
    widths = [p.shape[1] for p in d_pieces]

    def body(h_ref, *rest):
        d_refs, o_refs = rest[:n_p], rest[n_p:]

        @pl.when(pl.program_id(1) == 0)
        def _():
            for o_ref in o_refs:
                o_ref[...] = jnp.zeros_like(o_ref)

        h_t = jnp.transpose(h_ref[...])
        for d_ref, o_ref in zip(d_refs, o_refs):
            o_ref[...] += _dot_raw(h_t, d_ref[...], "nn")

    return pl.pallas_call(
        body, name=name, grid=(D_MODEL // tm, S // tk),
        in_specs=[pl.BlockSpec((tk, tm), lambda i, k: (k, i))] + [pl.BlockSpec((tk, width), lambda i, k: (k, 0)) for width in widths],
        out_specs=[pl.BlockSpec((tm, width), lambda i, k: (i, 0)) for width in widths],
        out_shape=[jax.ShapeDtypeStruct((D_MODEL, width), F32) for width in widths],
        compiler_params=_params(("parallel", "arbitrary")),
    )(h, *d_pieces)


def _mod_proj(x, scale, shift, wp, name):
    S = x.shape[0]
    N = wp.shape[1]
    tm = _pick(S, MATMUL_ROWS)
    tn = _pick(N, (1408, 1024, 512, 256, 128))

    def body(x_ref, sc_ref, sh_ref, w_ref, h_ref, o_ref, h_scr):
        @pl.when(pl.program_id(1) == 0)
        def _():
            h_new = _modulate(x_ref[...], sc_ref[...], sh_ref[...]).astype(BF16)
            h_scr[...] = h_new
            h_ref[...] = h_new

        o_ref[...] = _dot_raw(h_scr[...], w_ref[...], "nn")

    row = pl.BlockSpec((tm, D_MODEL), lambda i, j: (i, 0))
    par = pl.BlockSpec((1, D_MODEL), lambda i, j: (0, 0))
    return pl.pallas_call(
        body, name=name, grid=(S // tm, N // tn),
        in_specs=[row, par, par, pl.BlockSpec((D_MODEL, tn), lambda i, j: (0, j))],
        out_specs=[row, pl.BlockSpec((tm, tn), lambda i, j: (i, j))],
        out_shape=[jax.ShapeDtypeStruct((S, D_MODEL), BF16), jax.ShapeDtypeStruct((S, N), F32)],
        scratch_shapes=[pltpu.VMEM((tm, D_MODEL), BF16)],
        compiler_params=_params(("parallel", "arbitrary")),
    )(x, scale, shift, wp)


def _mix_out(o_a, proj, o_b, gnw, onw, wout, x, gate_w, t, name):
    S = x.shape[0]

    def body(oa_ref, z_ref, ob_ref, gn_ref, on_ref, w_ref, x_ref, g_ref, mixed_ref, y_ref, xo_ref):
        (mixed,) = _mix_post_fn(oa_ref[...], z_ref[...], ob_ref[...], gn_ref[...], on_ref[...])
        mixed = mixed.astype(BF16)
        mixed_ref[...] = mixed
        y = _dot_raw(mixed, w_ref[...], "nn")
        y_ref[...] = y.astype(BF16)
        xo_ref[...] = x_ref[...] + g_ref[...] * y

    half, row = _row_spec(t, 512, 0), _row_spec(t, D_MODEL, 0)
    return pl.pallas_call(
        body, name=name, grid=(S // t,),
        in_specs=[half, _row_spec(t, 512, 3), half, _full_spec(gnw.shape), _full_spec(onw.shape), _full_spec(wout.shape),
                  row, _full_spec(gate_w.shape)],
        out_specs=[row, row, row],
        out_shape=[jax.ShapeDtypeStruct((S, D_MODEL), BF16)] * 2 + [jax.ShapeDtypeStruct((S, D_MODEL), F32)],
        compiler_params=_params(("parallel",)),
    )(o_a, proj, o_b, gnw, onw, wout, x, gate_w)


def _mixer_fwd(x1, scale, shift, gate_w, cos_p, sin_p, W):
    S = x1.shape[0]
    tm = _pick(S, (512, 256, 128))
    tv = _pick(S, TOKEN_ROWS)
    ta = _pick(S, (512, 256, 128))
    nc = S // CHUNK
    h2, proj = _mod_proj(x1, scale, shift, W["wp"], "mix_proj")
    qkvc, q_a, k_a, v_a, gb = _conv_fwd(proj, W["conv_w"], W["alog_p"], W["dt_p"], tv, "gdn_conv")
    kab = (proj, tv, 128, 21)
    ti = _pick(S, INTRA_ROWS)
    intra = _rowwise(_gdn_intra_fn, [(q_a, ti, 512, 0), (k_a, ti, 512, 0), (v_a, ti, 512, 0), (gb, ti, 128, 0)],
                     [], [(ti, 512, F32)] * 4 + [(ti, CHUNK, F32)] * 4 + [(ti // 8, 512, F32)] + [(ti, CHUNK, F32)] * 4,
                     S // ti, "gdn_intra")
    u, wk, qd, kd, qks, gl, invs = intra[0], intra[1], intra[2], intra[3], tuple(intra[4:8]), intra[8], tuple(intra[9:])
    o_a, s_prev = _gdn_scan_fwd(u, wk, qd, kd, qks, gl, "gdn_scan")
    mla_params = [W["qnw"], W["kvnw"], W["wuq"], W["wukv"], W["qn_w"], W["qr_w"], W["kn_w"], W["kr_w"]]
    def mla_pre_with_vt(*a):
        q_, k_, v_ = _mla_pre_fn(*a)
        return q_, k_, v_, jnp.transpose(v_)

    q_b, k_b, v_b, vt_b = _rowwise(mla_pre_with_vt,
                                   [(proj, tv, 256, 8), (proj, tv, 384, 6), kab, (cos_p, tv, 128, 0), (sin_p, tv, 128, 0)],
                                   mla_params, [(tv, 1024, BF16), (tv, 1024, BF16), (tv, 512, BF16), (512, tv, BF16, "across")],
                                   S // tv, "mla_pre")
    o_b, lse = _attn_fwd(q_b, k_b, vt_b, ta, "mla_attn")
    mixed, y, x2 = _mix_out(o_a, proj, o_b, W["gnw"], W["onw"], W["wout"], x1, gate_w, tv, "mix_out")
    saved = (h2, proj, qkvc, q_a, k_a, v_a, gb, u, wk, qd, kd, qks, gl, invs, s_prev, o_a, q_b, k_b, v_b, o_b, lse, mixed, y)
    return x2, saved


def _mixer_bwd(d_out, dy, x1, scale, shift, cos_p, sin_p, W, saved, below):
    (h2, proj, qkvc, q_a, k_a, v_a, gb, u, wk, qd, kd, qks, gl, invs, s_prev, o_a, q_b, k_b, v_b, o_b, lse, mixed, y) = saved
    S = x1.shape[0]
    tm = _pick(S, (512, 256, 128))
    tv = _pick(S, TOKEN_ROWS)
    ta = _pick(S, (512, 256, 128))
    nc = S // CHUNK
    G = {}
    G["wout"] = _mm(mixed, dy, "tn", name="mix_gwout")
    (do_a, dz, do_b), (G["gnw"], G["onw"], stats) = _rowwise_bwd(
        _mix_post_fn, [(o_a, tv, 512, 0), (proj, tv, 512, 3), (o_b, tv, 512, 0)], [], [W["gnw"], W["onw"]],
        [], S // tv, "mix_dpost", row_dtypes=(F32, BF16, F32), dout_from=(dy, W["wout"]),
        across=(lse, lambda r_vals, d_rows, lse_tile: _attn_stats(r_vals[2], d_rows[2], lse_tile)))
    dq_b, dk_b, dv_b = _attn_bwd(q_b, k_b, v_b, do_b, stats, ta, "mla_dattn")
    kab = (proj, tv, 128, 21)
    mla_params = [W["qnw"], W["kvnw"], W["wuq"], W["wukv"], W["qn_w"], W["qr_w"], W["kn_w"], W["kr_w"]]
    (d_ckv, d_cq, d_kab), mla_grads = _rowwise_bwd(
        _mla_pre_fn, [(proj, tv, 256, 8), (proj, tv, 384, 6), kab], [(cos_p, tv, 128, 0), (sin_p, tv, 128, 0)], mla_params,
        [(dq_b, tv, 1024, 0), (dk_b, tv, 1024, 0), (dv_b, tv, 512, 0)], S // tv, "mla_dpre", row_dtypes=(BF16, BF16, F32))
    for key, g in zip(("qnw", "kvnw", "wuq", "wukv", "qn_w", "qr_w", "kn_w", "kr_w"), mla_grads):
        G[key] = g
    scan_grads = _gdn_scan_bwd(u, wk, qd, kd, qks, gl, s_prev, do_a, "gdn_dscan")
    ti = _pick(S, INTRA_ROWS)
    intra_douts = [(scan_grads[i], ti, 512, 0) for i in range(4)] + [(scan_grads[4 + i], ti, CHUNK, 0) for i in range(4)]
    intra_douts.append((scan_grads[8], ti // 8, 512, 0))
    (dq_a, dk_a, dv_a, d_gb), _ = _rowwise_bwd(
        _gdn_intra_fn, [(q_a, ti, 512, 0), (k_a, ti, 512, 0), (v_a, ti, 512, 0), (gb, ti, 128, 0)],
        [(x_, ti, CHUNK, 0) for x_ in invs], [], intra_douts, S // ti, "gdn_dintra")
    (d_qkvc, d_kab), (G["alog_p"], G["dt_p"]) = _rowwise_bwd(
        _gdn_pre_fn, [(qkvc, tv, 1536, 0), kab], [], [W["alog_p"], W["dt_p"]],
        [(dq_a, tv, 512, 0), (dk_a, tv, 512, 0), (dv_a, tv, 512, 0), (d_gb, tv, 128, 0)], S // tv, "gdn_dpre",
        adds=[(1, d_kab)], row_dtypes=(F32, BF16))
    d_qkv, g_conv = _conv_bwd(proj, d_qkvc, W["conv_w"], tv, "gdn_dconv")
    G["conv_w"] = g_conv[:4]
    d_proj = [d_qkv, dz, d_ckv, d_cq, d_kab]
    G["wp"] = _gw_pieces(h2, d_proj, "mix_gwp")
    dx1, G["s2"], G["sh2"], d_below, dg_below = _dproj_dmod(d_proj, W["wp"], x1, d_out, scale, shift, "mix_dproj", below=below)
    return dx1, d_below, dg_below, G


def _local_step(x, target, mod, cos_p, sin_p, W1, mixer_weights, ffn2_weights, ffn_grad_ready, mixer_grads_ready):
    sh1, s1, g1, sh2, s2, g2, sh3, s3, g3 = [mod[:, D_MODEL * i:D_MODEL * (i + 1)] for i in range(N_MOD)]
    x1, saved1 = _ffn_fwd(x, s1, sh1, g1, W1["f1_w8"], W1["f1_wo4"], "ffn1")
    W = mixer_weights(x1)
    x2, saved2 = _mixer_fwd(x1, s2, sh2, g2, cos_p, sin_p, W)
    W.update(ffn2_weights(x2))
    (dx3, loss_row), saved3 = _ffn_fwd(x2, s3, sh3, g3, W["f2_w8"], W["f2_wo4"], "ffn2", target=target)
    dx2, d_s3, d_sh3, d_g3, dy, d_g2 = _ffn_bwd(dx3, x2, s3, sh3, g3, W["f2_w8"], W["f2_wo4"], saved3, "ffn2",
                                                ffn_grad_ready("f2"), below=(saved2[-1], g2, 1.0))
    dx1, df1, d_g1, G = _mixer_bwd(dx2, dy, x1, s2, sh2, cos_p, sin_p, W, saved2, below=(saved1[4], g1, 0.5))
    d_sh2, d_s2 = G.pop("sh2"), G.pop("s2")
    saved1 = saved1[:5] + (df1, d_g1 + mixer_grads_ready(G))
    dx, d_s1, d_sh1, d_g1 = _ffn_bwd(dx1, x, s1, sh1, g1, W1["f1_w8"], W1["f1_wo4"], saved1, "ffn1", ffn_grad_ready("f1"))
    d_mod = jnp.concatenate([d_sh1, d_s1, d_g1, d_sh2, d_s2, d_g2, d_sh3, d_s3, d_g3], axis=1)
    return loss_row, dx, d_mod


WEIGHT_NAMES = ("w_ada", "b_ada", "ffn1_w_in", "ffn1_w_out", "w_in", "gdn_conv_w", "gdn_a_log", "gdn_dt_bias", "gdn_norm_w",
                "mla_q_norm_w", "mla_w_uq", "mla_kv_norm_w", "mla_w_ukv", "qkn_q_nope", "qkn_q_rope", "qkn_k_nope",
                "qkn_k_rope", "mla_out_norm_w", "w_out", "ffn2_w_in", "ffn2_w_out")
FFN_SHARDED = ("ffn1_w_in", "ffn1_w_out", "ffn2_w_in", "ffn2_w_out")
TRANSPOSED_ENTRY = ("ffn1_w_in", "ffn2_w_in", "w_in", "mla_w_uq")
SHEETED = (("w_in", "col"), ("gdn_conv_w", "col"), ("mla_w_uq", "col"), ("mla_w_ukv", "col"), ("w_out", "row"))
MOD_ROWS = N_MOD * D_MODEL // 128
SMALL = {"gdn_a_log": (MOD_ROWS, 1, 64, 4), "gdn_dt_bias": (MOD_ROWS + 1, 1, 64, 4), "gdn_norm_w": (MOD_ROWS + 2, 1, 0, 128),
         "mla_q_norm_w": (MOD_ROWS + 3, 3, 0, 384), "mla_kv_norm_w": (MOD_ROWS + 6, 2, 0, 256),
         "qkn_q_nope": (MOD_ROWS + 8, 1, 0, 128), "qkn_q_rope": (MOD_ROWS + 9, 1, 0, 64), "qkn_k_nope": (MOD_ROWS + 10, 1, 0, 128),
         "qkn_k_rope": (MOD_ROWS + 11, 1, 0, 64), "mla_out_norm_w": (MOD_ROWS + 12, 1, 0, 128)}
LOSS_ROW = MOD_ROWS + 13
CONV_ROW, CONV_ROWS = 88, 4 * 1536 // 128
SHEET_ROWS = CONV_ROW + CONV_ROWS


def _to_sheet(flat, dtype, sublanes):
    n = flat.shape[-1]
    unit = sublanes * 128
    pad = (-n) % unit
    flat = jnp.pad(flat.astype(dtype), [(0, 0)] * (flat.ndim - 1) + [(0, pad)])
    return flat.reshape(flat.shape[:-1] + ((n + pad) // 128, 128))


def _small_sheet(b_like, small):
    sheet = jnp.zeros((SHEET_ROWS, 128), F32).at[:MOD_ROWS].set(b_like.reshape(MOD_ROWS, 128))
    for name, (row, rows, lane, n) in SMALL.items():
        v = small[name].reshape(1, n)
        if rows == 1:
            sheet = sheet.at[row, lane:lane + n].set(v[0])
        else:
            sheet = sheet.at[row:row + rows].set(v.reshape(rows, 128))
    return sheet


def _from_small_sheet(sheet):
    out = {"b_ada": sheet[:MOD_ROWS].reshape(1, N_MOD * D_MODEL)}
    for name, (row, rows, lane, n) in SMALL.items():
        out[name] = sheet[row, lane:lane + n].reshape(1, n) if rows == 1 else sheet[row:row + rows].reshape(1, n)
    return out


def kernel(x, c, positions, w_ada, b_ada, ffn1_w_in, ffn1_w_out, w_in, gdn_conv_w, gdn_a_log, gdn_dt_bias, gdn_norm_w, mla_q_norm_w, mla_w_uq, mla_kv_norm_w, mla_w_ukv, qkn_q_nope, qkn_q_rope, qkn_k_nope, qkn_k_rope, mla_out_norm_w, w_out, ffn2_w_in, ffn2_w_out, loss_target, m_w_ada, m_b_ada, m_ffn1_w_in, m_ffn1_w_out, m_w_in, m_gdn_conv_w, m_gdn_a_log, m_gdn_dt_bias, m_gdn_norm_w, m_mla_q_norm_w, m_mla_w_uq, m_mla_kv_norm_w, m_mla_w_ukv, m_qkn_q_nope, m_qkn_q_rope, m_qkn_k_nope, m_qkn_k_rope, m_mla_out_norm_w, m_w_out, m_ffn2_w_in, m_ffn2_w_out, v_w_ada, v_b_ada, v_ffn1_w_in, v_ffn1_w_out, v_w_in, v_gdn_conv_w, v_gdn_a_log, v_gdn_dt_bias, v_gdn_norm_w, v_mla_q_norm_w, v_mla_w_uq, v_mla_kv_norm_w, v_mla_w_ukv, v_qkn_q_nope, v_qkn_q_rope, v_qkn_k_nope, v_qkn_k_rope, v_mla_out_norm_w, v_w_out, v_ffn2_w_in, v_ffn2_w_out):
    args = locals()
    w = {n: args[n] for n in WEIGHT_NAMES}
    m = {n: args["m_" + n] for n in WEIGHT_NAMES}
    v = {n: args["v_" + n] for n in WEIGHT_NAMES}
    me = 4 * lax.axis_index("x") + 2 * lax.axis_index("y") + lax.axis_index("c")
    cols = N_MOD * D_MODEL // N_DEV
    shard = {n: w[n][0] for n in FFN_SHARDED + tuple(s[0] for s in SHEETED)}

    sc = c * _sigmoid(c)
    first = _to_sheet(jnp.concatenate([sc.reshape(-1), shard["gdn_conv_w"].reshape(-1)]), F32, 8)
    (first_all,) = _all_gather([first], "gather_c")
    sc_all = first_all[:, :D_MODEL // 128].reshape(N_DEV, D_MODEL)
    n_taps = shard["gdn_conv_w"].size
    conv_all = first_all.reshape(N_DEV, -1)[:, D_MODEL:D_MODEL + n_taps].reshape(N_DEV, 4, -1)
    b_mine = lax.dynamic_slice(b_ada, (0, me * cols), (1, cols))
    mod_cols = _mm(sc_all, w_ada[0], "nn", name="ada_mod", extra_params=[b_mine], epi=lambda acc, b_: (acc + b_,))
    (mod_all,) = _all_to_all([_to_sheet(mod_cols, F32, 8)], "scatter_mod")
    mod = mod_all.reshape(N_DEV, -1)[:, :cols].reshape(1, N_MOD * D_MODEL)

    f1_shards, mod = lax.optimization_barrier(([shard["ffn1_w_in"].astype(BF16), shard["ffn1_w_out"].astype(BF16)], mod))
    f1_w8, f1_out = _all_gather(f1_shards, "gather_w1")
    travel = [s for s in SHEETED if s[0] != "gdn_conv_w"]
    tied = lax.optimization_barrier(([shard[n].astype(BF16) for n, _ in travel], f1_w8))
    f1_w8 = tied[1]
    mixer_w = _exchange_start(tied[0], False, "gather_wm_start")
    ffn2_w = _exchange_start([shard["ffn2_w_in"].astype(BF16) + mixer_w[4][0:1, 0:1].astype(BF16),
                              shard["ffn2_w_out"].astype(BF16)], False, "gather_w2_start")
    mod = mod + ffn2_w[4][0:1, 0:1]
    W1 = dict(f1_w8=f1_w8, f1_wo4=f1_out.reshape(HID_PIECES, FFN_PIECE, D_MODEL))

    def mixer_weights(after):
        got = _exchange_wait(mixer_w, False, after, "gather_wm_wait")
        P = {n: jnp.concatenate(list(g), axis=1) if kind == "col" else g.reshape(-1, g.shape[-1])
             for (n, kind), g in zip(travel, got)}
        P["gdn_conv_w"] = jnp.concatenate(list(conv_all), axis=1)
        for n in SMALL:
            P[n] = w[n]
        return _pack_weights(P)

    def ffn2_weights(after):
        f2_w8, f2_out = _exchange_wait(ffn2_w, False, after, "gather_w2_wait")
        return dict(f2_w8=f2_w8, f2_wo4=f2_out.reshape(HID_PIECES, FFN_PIECE, D_MODEL))

    pending, small_grads = {}, {}

    def ffn_grad_ready(tag):
        def ready(which, g):
            pieces = g if which == "w8" else g.reshape((N_DEV,) + shard["ffn1_w_out"].shape)
            pending[tag + which] = _exchange_start([pieces], True, "scatter_%s_%s_start" % (tag, which))
            return pending[tag + which][4]
        return ready

    def mixer_grads_ready(G):
        g_full = _unpack_grads(G)
        small_grads.update({n: g_full[n] for n in SMALL})
        small_grads["gdn_conv_w"] = g_full["gdn_conv_w"]
        pieces = []
        for n, kind in travel:
            r, cc = shard[n].shape
            g = g_full[n].astype(BF16)
            pieces.append(jnp.stack([g[:, cc * p:cc * (p + 1)] for p in range(N_DEV)]) if kind == "col"
                          else g.reshape(N_DEV, r, cc))
        pending["mixer"] = _exchange_start(pieces, True, "scatter_mx_start")
        return pending["mixer"][4][0:1, 0:1]

    cos_p, sin_p = _rope_tables(positions[0])
    loss_row, dx, d_mod = _local_step(x[0], loss_target[0], mod, cos_p, sin_p, W1, mixer_weights, ffn2_weights,
                                      ffn_grad_ready, mixer_grads_ready)

    sheet = _small_sheet(d_mod, small_grads).at[LOSS_ROW].set(loss_row[0, :128])
    sheet = sheet.at[CONV_ROW:CONV_ROW + CONV_ROWS].set(small_grads["gdn_conv_w"].reshape(CONV_ROWS, 128))
    (sheets,) = _all_gather([sheet], "gather_small")
    summed = _sum_devices(sheets, "sum_small")
    d_mod_all = sheets[:, :MOD_ROWS].reshape(N_DEV, N_MOD * D_MODEL)
    d_mod_mine = lax.dynamic_slice(d_mod_all, (0, me * cols), (N_DEV, cols))
    grads = _from_small_sheet(summed)
    grads["w_ada"] = _mm(sc_all, d_mod_mine, "tn", name="ada_gw", hi=True)
    conv_taps = shard["gdn_conv_w"].shape[1]
    grads["gdn_conv_w"] = lax.dynamic_slice(summed[CONV_ROW:CONV_ROW + CONV_ROWS].reshape(4, -1), (0, me * conv_taps),
                                            (4, conv_taps))
    loss = summed[LOSS_ROW, 0]

    delta, new_m, new_v = {}, {}, {}
    arrived = {}
    for n, key in zip(FFN_SHARDED, ("f1w8", "f1wo4", "f2w8", "f2wo4")):
        (arrived[n],) = _exchange_wait(pending[key], True, summed, "scatter_%s_wait" % key)
    arrived.update(zip([n for n, _ in travel], _exchange_wait(pending["mixer"], True, summed, "scatter_mx_wait")))
    for n, parts in arrived.items():
        if n in TRANSPOSED_ENTRY:
            res = _sum_adamw(parts, w[n][0].T, m[n][0].T, v[n][0].T, "adamw_" + n, transposed=True)
            grads[n], delta[n], new_m[n], new_v[n] = [r.T for r in res]
        else:
            grads[n], delta[n], new_m[n], new_v[n] = _sum_adamw(parts, w[n][0], m[n][0], v[n][0], "adamw_" + n)
    for n in ("w_ada", "gdn_conv_w"):
        delta[n], new_m[n], new_v[n] = _adamw(w[n][0], grads[n], m[n][0], v[n][0], "adamw_" + n)
    small_in = [_small_sheet(t["b_ada"], t) for t in (w, grads, m, v)]
    for res, out in zip(_adamw(*small_in, "adamw_small"), (delta, new_m, new_v)):
        out.update(_from_small_sheet(res))

    def shaped(d):
        return [d[n].reshape(w[n].shape) for n in WEIGHT_NAMES]

    return (loss, dx[None], *shaped(grads), *shaped(delta), *shaped(new_m), *shaped(new_v))
```

```python
import functools

import jax
import jax.numpy as jnp
import numpy as np
from jax import lax
from jax.experimental import pallas as pl
from jax.experimental.pallas import tpu as pltpu

F32 = jnp.float32
BF16 = jnp.bfloat16

D_MODEL = 1024
D_FF = 2816
N_MOD = 9
HEADS = 4
HEAD_DIM = 128
CHUNK = 64
EPS = 1e-6
ROPE = 64
Q_LORA = 384
KV_LORA = 256
N_IN = 2760
N_IN_PACKED = 2816
ROPE_BASE = 10000.0
LOG2_E = 1.4426950408889634
N_DEV = 8

ADAM_LR = 0.001
ADAM_B1 = 0.9
ADAM_B2 = 0.999
ADAM_EPS = 1e-08
ADAM_WD = 0.01
ADAM_STEP = 10

VMEM_LIMIT_BYTES = 56 * 1024 * 1024
MATMUL_ROWS = (1024, 512, 256, 128)
MESH = pl.DeviceIdType.MESH


def _params(sem=None):
    return pltpu.CompilerParams(dimension_semantics=sem, vmem_limit_bytes=VMEM_LIMIT_BYTES)


def _pick(dim, prefs):
    for p in prefs:
        if dim % p == 0:
            return p
    return dim


_DIMS = {"nn": (((1,), (0,)), ((), ())), "nt": (((1,), (1,)), ((), ())), "tn": (((0,), (0,)), ((), ()))}


def _dot_raw(a, b, mode):
    return lax.dot_general(a.astype(BF16), b.astype(BF16), _DIMS[mode], preferred_element_type=F32)


def _dot_hi(a, b, mode="nn"):
    return lax.dot_general(a, b, _DIMS[mode], precision=lax.Precision.HIGHEST, preferred_element_type=F32)


@functools.partial(jax.custom_vjp, nondiff_argnums=(2,))
def _bdot(a, b, mode):
    return _dot_raw(a, b, mode)


def _bdot_fwd(a, b, mode):
    return _dot_raw(a, b, mode), (a, b)


def _bdot_bwd(mode, res, g):
    a, b = res
    if mode == "nn":
        return _dot_raw(g, b, "nt"), _dot_raw(a, g, "tn")
    if mode == "nt":
        return _dot_raw(g, b, "nn"), _dot_raw(g, a, "tn")
    return _dot_raw(b, g, "nt"), _dot_raw(a, g, "nn")


_bdot.defvjp(_bdot_fwd, _bdot_bwd)


def _mm(a, b, mode, *, name, out_dtypes=(F32,), epi=None, extras=(), extra_params=(), hi=False,
        tm=None, tn=None, tk=None):
    if mode == "nn":
        (M, K), (_, N) = a.shape, b.shape
    elif mode == "nt":
        (M, K), (N, _) = a.shape, b.shape
    else:
        (K, M), (_, N) = a.shape, b.shape
    tm = tm or _pick(M, (512, 1408, 256, 128) if mode == "tn" else MATMUL_ROWS + (384, 352))
    tn = tn or _pick(N, (1024, 1408, 768, 512, 384, 256, 128))
    tk = tk or _pick(K, (1024, 1408, 512, 384, 256, 128))
    a_spec = {"nn": pl.BlockSpec((tm, tk), lambda i, j, k: (i, k)), "nt": pl.BlockSpec((tm, tk), lambda i, j, k: (i, k)),
              "tn": pl.BlockSpec((tk, tm), lambda i, j, k: (k, i))}[mode]
    b_spec = {"nn": pl.BlockSpec((tk, tn), lambda i, j, k: (k, j)), "nt": pl.BlockSpec((tn, tk), lambda i, j, k: (j, k)),
              "tn": pl.BlockSpec((tk, tn), lambda i, j, k: (k, j))}[mode]
    mn_spec = pl.BlockSpec((tm, tn), lambda i, j, k: (i, j))
    return _mmg(a, b, mode, name=name, grid=(M // tm, N // tn, K // tk), a_spec=a_spec, b_spec=b_spec, out_spec=mn_spec,
                out_shapes=[jax.ShapeDtypeStruct((M, N), dt) for dt in out_dtypes], acc_shape=(tm, tn), epi=epi,
                extras=list(extras) + list(extra_params),
                extra_specs=[mn_spec] * len(extras) + [pl.BlockSpec((1, tn), lambda i, j, k: (0, j))] * len(extra_params),
                hi=hi)


def _mmg(a, b, mode, *, name, grid, a_spec, b_spec, out_spec, out_shapes, acc_shape, epi=None, extras=(),
         extra_specs=(), hi=False):
    nk = grid[2]
    n_e, n_o = len(extras), len(out_shapes)

    def body(*refs):
        a_ref, b_ref = refs[:2]
        e_refs = refs[2:2 + n_e]
        o_refs = refs[2 + n_e:2 + n_e + n_o]
        acc_ref = refs[-1]
        k = pl.program_id(2)

        @pl.when(k == 0)
        def _():
            acc_ref[...] = jnp.zeros_like(acc_ref)

        if hi:
            acc_ref[...] += _dot_hi(a_ref[...].astype(F32), b_ref[...].astype(F32), mode)
        else:
            acc_ref[...] += _dot_raw(a_ref[...], b_ref[...], mode)

        @pl.when(k == nk - 1)
        def _():
            acc = acc_ref[...]
            outs = (acc,) if epi is None else epi(acc, *[e[...].astype(F32) for e in e_refs])
            for o_ref, o in zip(o_refs, outs):
                o_ref[...] = o.astype(o_ref.dtype)

    outs = pl.pallas_call(
        body, name=name, grid=grid,
        in_specs=[a_spec, b_spec] + list(extra_specs),
        out_specs=[out_spec] * n_o,
        out_shape=list(out_shapes),
        scratch_shapes=[pltpu.VMEM(acc_shape, F32)],
        compiler_params=_params(("parallel", "parallel", "arbitrary")),
    )(a, b, *extras)
    return outs if n_o > 1 else outs[0]


def _row_spec(th, cw, ci):
    return pl.BlockSpec((th, cw), lambda i: (i, ci))


def _full_spec(shape):
    return pl.BlockSpec(shape, lambda i: (0,) * len(shape))


def _rowwise(fn, rows, params, outs, n_steps, name):
    n_r, n_p, n_o = len(rows), len(params), len(outs)

    def body(*refs):
        vals = [r[...].astype(F32) for r in refs[:n_r + n_p]]
        res = fn(*vals)
        for o_ref, o in zip(refs[n_r + n_p:], res):
            o_ref[...] = o.astype(o_ref.dtype)

    across = [len(o) == 4 for o in outs]
    res = pl.pallas_call(
        body, name=name, grid=(n_steps,),
        in_specs=[_row_spec(th, cw, ci) for (_, th, cw, ci) in rows] + [_full_spec(p.shape) for p in params],
        out_specs=[pl.BlockSpec((o[0], o[1]), lambda i: (0, i)) if ac else _row_spec(o[0], o[1], 0)
                   for o, ac in zip(outs, across)],
        out_shape=[jax.ShapeDtypeStruct((o[0], n_steps * o[1]) if ac else (n_steps * o[0], o[1]), o[2])
                   for o, ac in zip(outs, across)],
        compiler_params=_params(("parallel",)),
    )(*[r[0] for r in rows], *params)
    return res


def _rowwise_bwd(fn, rows, aux, params, douts, n_steps, name, row_dtypes=None, adds=(), across=None, dout_from=None):
    n_r, n_a, n_p, n_d, n_add = len(rows), len(aux), len(params), len(douts), len(adds)
    row_dtypes = row_dtypes or (F32,) * n_r

    def body(*refs):
        it = iter(refs)
        r_vals = [next(it)[...].astype(F32) for _ in range(n_r)]
        a_vals = [next(it)[...].astype(F32) for _ in range(n_a)]
        p_vals = [next(it)[...].astype(F32) for _ in range(n_p)]
        d_vals = [next(it)[...].astype(F32) for _ in range(n_d)]
        add_vals = [next(it)[...].astype(F32) for _ in range(n_add)]
        across_in = next(it) if across is not None else None
        if dout_from is not None:
            d_vals = [_dot_raw(next(it)[...], next(it)[...], "nt")]
        dr_refs = [next(it) for _ in range(n_r)]
        dp_refs = [next(it) for _ in range(n_p)]

        def f(*rp):
            return tuple(fn(*rp[:n_r], *a_vals, *rp[n_r:]))

        _, vjp = jax.vjp(f, *r_vals, *p_vals)
        grads = list(vjp(tuple(d_vals)))
        for (ri, _), av in zip(adds, add_vals):
            grads[ri] = grads[ri] + av
        for dr_ref, g in zip(dr_refs, grads[:n_r]):
            dr_ref[...] = g.astype(dr_ref.dtype)
        if across is not None:
            next(it)[...] = across[1](r_vals, grads[:n_r], across_in[...])

        @pl.when(pl.program_id(0) == 0)
        def _():
            for dp_ref in dp_refs:
                dp_ref[...] = jnp.zeros_like(dp_ref)

        for dp_ref, g in zip(dp_refs, grads[n_r:]):
            dp_ref[...] += g

    all_rows = list(rows) + list(aux) + list(douts) + [(arr,) + tuple(rows[ri][1:3]) + (0,) for ri, arr in adds]
    in_specs = ([_row_spec(th, cw, ci) for (_, th, cw, ci) in list(rows) + list(aux)]
                + [_full_spec(p.shape) for p in params]
                + [_row_spec(th, cw, ci) for (_, th, cw, ci) in all_rows[n_r + n_a:]])
    across_specs = [] if across is None else [pl.BlockSpec((8, rows[0][1]), lambda i: (0, i))]
    from_specs = [] if dout_from is None else [_row_spec(rows[0][1], dout_from[0].shape[1], 0), _full_spec(dout_from[1].shape)]
    res = pl.pallas_call(
        body, name=name, grid=(n_steps,),
        in_specs=in_specs + across_specs + from_specs,
        out_specs=[_row_spec(th, cw, 0) for (_, th, cw, _) in rows] + [_full_spec(p.shape) for p in params] + across_specs,
        out_shape=[jax.ShapeDtypeStruct((n_steps * th, cw), dt) for (_, th, cw, _), dt in zip(rows, row_dtypes)]
        + [jax.ShapeDtypeStruct(p.shape, F32) for p in params]
        + [jax.ShapeDtypeStruct(across[0].shape, F32) for _ in across_specs],
        compiler_params=_params(("arbitrary",)),
    )(*[r[0] for r in list(rows) + list(aux)], *params, *[r[0] for r in all_rows[n_r + n_a:]],
      *[across[0] for _ in across_specs], *(dout_from or ()))
    return res[:n_r], res[n_r:]


def _sigmoid(x):
    return lax.logistic(x)


def _silu(x):
    return x * _sigmoid(x)


def _rms(x, w=None, n=None):
    n = n or x.shape[-1]
    y = x * lax.rsqrt(jnp.sum(x * x, axis=-1, keepdims=True) * (1.0 / n) + EPS)
    return y if w is None else y * w


def _modulate(x, scale, shift):
    return _rms(x) * (1.0 + scale) + shift


def _softplus(x):
    return jnp.maximum(x, 0.0) + jnp.log1p(jnp.exp(-jnp.abs(x)))


@jax.custom_vjp
def _rot_half64(x):
    lane = lax.broadcasted_iota(jnp.int32, x.shape, 1)
    up = pltpu.roll(x, 96, 1)
    down = pltpu.roll(x, 32, 1)
    return jnp.where(lane < 32, up, jnp.where(lane < 64, down, 0.0))


_rot_half64.defvjp(lambda x: (_rot_half64(x), None), lambda _, g: (_rot_half64(g),))


def _rope128(x, cos_p, sin_p):
    return x * cos_p + _rot_half64(x) * sin_p


def _gdn_pre_fn(qkvc, kab, alog_p, dt_p):
    a = _silu(qkvc)
    qs, ks = [], []
    for h in range(HEADS):
        qh = a[:, HEAD_DIM * h:HEAD_DIM * (h + 1)]
        kh = a[:, 512 + HEAD_DIM * h:512 + HEAD_DIM * (h + 1)]
        qs.append(qh * lax.rsqrt(jnp.sum(qh * qh, axis=-1, keepdims=True) + EPS) * (HEAD_DIM ** -0.5))
        ks.append(kh * lax.rsqrt(jnp.sum(kh * kh, axis=-1, keepdims=True) + EPS))
    lane = lax.broadcasted_iota(jnp.int32, kab.shape, 1)
    g_full = -jnp.exp(alog_p) * _softplus(kab + dt_p)
    b_full = _sigmoid(kab)
    gb = jnp.where((lane >= 64) & (lane < 68), g_full, jnp.where((lane >= 68) & (lane < 72), b_full, 0.0))
    return jnp.concatenate(qs, axis=1), jnp.concatenate(ks, axis=1), a[:, 1024:1536], gb


INTRA_ROWS = (512, 256, 128, 64)
TOKEN_ROWS = (512, 256, 128)

_BNN = (((2,), (1,)), ((0,), (0,)))
_BNT = (((2,), (2,)), ((0,), (0,)))
_BTN = (((1,), (1,)), ((0,), (0,)))


def _split_bf16(a):
    hi = a.astype(BF16)
    return hi, (a - hi.astype(F32)).astype(BF16)


def _dot3_raw(a, b, dims):
    a_hi, a_lo = _split_bf16(a)
    b_hi, b_lo = _split_bf16(b)
    dot = lambda x_, y_: lax.dot_general(x_, y_, dims, preferred_element_type=F32)
    return dot(a_hi, b_hi) + (dot(a_hi, b_lo) + dot(a_lo, b_hi))


@functools.partial(jax.custom_vjp, nondiff_argnums=(2, 3))
def _dot3(a, b, nt, exact_bwd=True):
    return _dot3_raw(a, b, _BNT if nt else _BNN)


def _dot3_fwd(a, b, nt, exact_bwd):
    return _dot3_raw(a, b, _BNT if nt else _BNN), (a, b)


def _dot3_bwd(nt, exact_bwd, res, g):
    a, b = res
    if exact_bwd:
        dot = _dot3_raw
    else:
        dot = lambda x_, y_, d_: lax.dot_general(x_.astype(BF16), y_.astype(BF16), d_, preferred_element_type=F32)
    if nt:
        return dot(g, b, _BNN), dot(jnp.swapaxes(g, 1, 2), a, _BNN)
    return dot(g, b, _BNT), dot(jnp.swapaxes(a, 1, 2), g, _BNN)


_dot3.defvjp(_dot3_fwd, _dot3_bwd)


@functools.partial(jax.custom_vjp, nondiff_argnums=(2,))
def _bdot_b(a, b, nt):
    return lax.dot_general(a.astype(BF16), b.astype(BF16), _BNT if nt else _BNN, preferred_element_type=F32)


def _bdot_b_fwd(a, b, nt):
    return _bdot_b(a, b, nt), (a, b)


def _bdot_b_bwd(nt, res, g):
    a, b = res
    dot = lambda x_, y_, d_: lax.dot_general(x_.astype(BF16), y_.astype(BF16), d_, preferred_element_type=F32)
    if nt:
        return dot(g, b, _BNN), dot(jnp.swapaxes(g, 1, 2), a, _BNN)
    return dot(g, b, _BNT), dot(jnp.swapaxes(a, 1, 2), g, _BNN)


_bdot_b.defvjp(_bdot_b_fwd, _bdot_b_bwd)


@jax.custom_vjp
def _inverse_given(a_mat, inv):
    return inv


def _inverse_given_bwd(inv, g):
    inv_t = jnp.swapaxes(inv, 1, 2)
    return -_dot3_raw(_dot3_raw(inv_t, g, _BNN), inv_t, _BNN), jnp.zeros_like(inv)


_inverse_given.defvjp(lambda a_mat, inv: (inv, inv), _inverse_given_bwd)


def _intra_batched(q, k, v, g_col, b_col, inv_known=None):
    c = CHUNK
    nb = q.shape[0]
    row = lax.broadcasted_iota(jnp.int32, (1, c, c), 1)
    col = lax.broadcasted_iota(jnp.int32, (1, c, c), 2)
    incl, strict, eye = row >= col, row > col, row == col
    tri = jnp.broadcast_to(jnp.where(incl, 1.0, 0.0).astype(F32), (nb, c, c))
    ident = jnp.where(eye, 1.0, 0.0).astype(F32)
    g_wide = _dot3(tri, jnp.broadcast_to(g_col, (nb, c, HEAD_DIM)), False)
    g_i = g_wide[:, :, :c]
    g_j = jnp.sum(jnp.where(eye, g_i, 0.0), axis=1, keepdims=True)
    decay = jnp.where(incl, jnp.exp(jnp.where(incl, g_i - g_j, 0.0)), 0.0)
    kk = _bdot_b(k, k, True)
    a_mat = jnp.where(strict, b_col * kk * decay, 0.0)
    if inv_known is None:
        x_pow = -a_mat
        inv = ident + x_pow
        for _ in range(5):
            x_pow = _dot3(x_pow, x_pow, False, False)
            inv = inv + _dot3(inv, x_pow, False, False)
    else:
        inv = _inverse_given(a_mat, inv_known)
    e_wide = jnp.exp(g_wide)
    u = _dot3(inv, v * b_col, False)
    wk = _dot3(inv, k * b_col * e_wide, False)
    qk = _bdot_b(q, k, True) * decay
    last = lax.broadcasted_iota(jnp.int32, (1, c, HEAD_DIM), 1) == c - 1
    g_last = jnp.sum(jnp.where(last, g_wide, 0.0), axis=1, keepdims=True)
    qd = q * e_wide
    kd = k * jnp.exp(g_last - g_wide)
    gl = jnp.broadcast_to(jnp.exp(g_last), (nb, 8, HEAD_DIM))
    return u, wk, qd, kd, qk, gl, inv


def _gdn_intra_fn(q, k, v, gb, *inv_known):
    t = q.shape[0]
    nch = t // CHUNK
    lane = lax.broadcasted_iota(jnp.int32, gb.shape, 1)

    def heads_first(x_):
        return jnp.concatenate([x_[:, HEAD_DIM * h:HEAD_DIM * (h + 1)].reshape(nch, CHUNK, HEAD_DIM) for h in range(HEADS)],
                               axis=0)

    def column(first_lane):
        return jnp.concatenate([jnp.sum(jnp.where(lane == first_lane + h, gb, 0.0), axis=1, keepdims=True)
                                .reshape(nch, CHUNK, 1) for h in range(HEADS)], axis=0)

    known = jnp.concatenate([x_.reshape(nch, CHUNK, CHUNK) for x_ in inv_known], axis=0) if inv_known else None
    u, wk, qd, kd, qk, gl, inv = _intra_batched(heads_first(q), heads_first(k), heads_first(v), column(64), column(68), known)

    def rows_first(x_):
        r, w_ = x_.shape[1], x_.shape[2]
        return jnp.concatenate([x_[nch * h:nch * (h + 1)].reshape(nch * r, w_) for h in range(HEADS)], axis=1)

    per_head = lambda x_: [x_[nch * h:nch * (h + 1)].reshape(t, CHUNK) for h in range(HEADS)]
    outs = (rows_first(u), rows_first(wk), rows_first(qd), rows_first(kd), *per_head(qk), rows_first(gl))
    return outs if inv_known else outs + tuple(per_head(inv))


def _scan_step(s0, u, wk, qd, kd, qk, gl):
    v_new = u - _bdot_b(wk, s0, False)
    o = _bdot_b(qd, s0, False) + _bdot_b(qk, v_new, False)
    s1 = s0 * gl[:, 0:1, :] + _bdot_b(jnp.swapaxes(kd, 1, 2), v_new, False)
    return o, s1


def _mix_post_fn(o_a, z, o_b, gnw, onw):
    parts = [_rms(o_a[:, HEAD_DIM * h:HEAD_DIM * (h + 1)], gnw) * _silu(z[:, HEAD_DIM * h:HEAD_DIM * (h + 1)])
             for h in range(HEADS)]
    parts += [_rms(o_b[:, HEAD_DIM * h:HEAD_DIM * (h + 1)], onw) for h in range(HEADS)]
    return (jnp.concatenate(parts, axis=1),)


def _mla_pre_fn(ckv, cq, kab, cos_p, sin_p, qnw, kvnw, wuq, wukv, qn_w, qr_w, kn_w, kr_w):
    scale = (HEAD_DIM + ROPE) ** -0.5 * LOG2_E
    qf = _bdot(_rms(cq, qnw), wuq, "nn")
    kvf = _bdot(_rms(ckv, kvnw), wukv, "nn")
    lane = lax.broadcasted_iota(jnp.int32, kab.shape, 1)
    kr = _rope128(_rms(jnp.where(lane < ROPE, kab, 0.0), kr_w, n=ROPE), cos_p, sin_p)
    qs, ks = [], []
    for h in range(HEADS):
        qn = _rms(qf[:, 256 * h:256 * h + 128], qn_w) * scale
        qr = _rope128(_rms(qf[:, 256 * h + 128:256 * h + 256], qr_w, n=ROPE), cos_p, sin_p) * scale
        qs += [qn, qr]
        ks += [_rms(kvf[:, 128 * h:128 * (h + 1)], kn_w), kr]
    return jnp.concatenate(qs, axis=1), jnp.concatenate(ks, axis=1), kvf[:, 512:]


def _conv_fwd(proj, conv_w, alog_p, dt_p, tm, name):
    S = proj.shape[0]
    C = 1536
    nb = tm // 8

    def body(x_ref, prev_ref, kab_ref, w_ref, al_ref, dt_ref, o_ref, q_ref, k_ref, v_ref, gb_ref, ext_ref):
        i = pl.program_id(0)
        ext_ref[0:8, :] = jnp.where(i > 0, prev_ref[...], 0.0)
        ext_ref[8:, :] = x_ref[...]
        acc = jnp.zeros((tm, C), F32)
        for k in range(4):
            acc = acc + w_ref[k:k + 1, :] * ext_ref[pl.ds(5 + k, tm), :]
        o_ref[...] = acc
        for ref, val in zip((q_ref, k_ref, v_ref, gb_ref), _gdn_pre_fn(acc, kab_ref[...], al_ref[...], dt_ref[...])):
            ref[...] = val

    third = pl.BlockSpec((tm, 512), lambda i: (i, 0))
    return pl.pallas_call(
        body, name=name, grid=(S // tm,),
        in_specs=[pl.BlockSpec((tm, C), lambda i: (i, 0)),
                  pl.BlockSpec((8, C), lambda i: (jnp.maximum(i * nb - 1, 0), 0)),
                  _row_spec(tm, 128, 21), pl.BlockSpec((4, C), lambda i: (0, 0)), _full_spec(alog_p.shape), _full_spec(dt_p.shape)],
        out_specs=[pl.BlockSpec((tm, C), lambda i: (i, 0)), third, third, third, _row_spec(tm, 128, 0)],
        out_shape=[jax.ShapeDtypeStruct((S, C), F32)] + [jax.ShapeDtypeStruct((S, 512), F32)] * 3
        + [jax.ShapeDtypeStruct((S, 128), F32)],
        scratch_shapes=[pltpu.VMEM((tm + 8, C), F32)],
        compiler_params=_params(("arbitrary",)),
    )(proj, proj, proj, conv_w, alog_p, dt_p)


def _conv_bwd(proj, qkvc, conv_w, alog_p, dt_p, douts, d_kab_add, tm, name):
    S = proj.shape[0]
    C = 1536
    nb = tm // 8
    n_steps = S // tm

    def body(x_ref, prev_ref, c_ref, kab_ref, w_ref, al_ref, dt_ref, dq_ref, dk_ref, dv_ref, dgb_ref, add_ref,
             dx_ref, dkab_ref, dw_ref, dal_ref, ddt_ref, xext_ref, dext_ref, halo_ref):
        step = pl.program_id(0)

        @pl.when(step == 0)
        def _():
            halo_ref[...] = jnp.zeros_like(halo_ref)
            for r in (dw_ref, dal_ref, ddt_ref):
                r[...] = jnp.zeros_like(r)

        _, vjp = jax.vjp(_gdn_pre_fn, c_ref[...], kab_ref[...], al_ref[...], dt_ref[...])
        d, d_kab, d_al, d_dt = vjp((dq_ref[...], dk_ref[...], dv_ref[...], dgb_ref[...]))
        dkab_ref[...] = (d_kab + add_ref[...]).astype(dkab_ref.dtype)
        xext_ref[0:8, :] = jnp.where(step < n_steps - 1, prev_ref[...], 0.0)
        xext_ref[8:, :] = x_ref[...]
        dext_ref[0:tm, :] = d
        dext_ref[tm:, :] = halo_ref[...]
        halo_ref[...] = d[0:8, :]
        acc = jnp.zeros((tm, C), F32)
        dws = []
        for k in range(4):
            acc = acc + w_ref[k:k + 1, :] * dext_ref[pl.ds(3 - k, tm), :]
            dws.append(jnp.sum(d * xext_ref[pl.ds(5 + k, tm), :], axis=0, keepdims=True))
        dx_ref[...] = acc.astype(dx_ref.dtype)
        dw_ref[...] += jnp.concatenate(dws + [jnp.zeros((4, C), F32)], axis=0)
        dal_ref[...] += d_al
        ddt_ref[...] += d_dt

    tile = lambda cols, block=0: pl.BlockSpec((tm, cols), lambda s_: (n_steps - 1 - s_, block))
    return pl.pallas_call(
        body, name=name, grid=(n_steps,),
        in_specs=[tile(C), pl.BlockSpec((8, C), lambda s_: (jnp.maximum((n_steps - 1 - s_) * nb - 1, 0), 0)),
                  tile(C), tile(128, 21), pl.BlockSpec((4, C), lambda s_: (0, 0)), _full_spec(alog_p.shape), _full_spec(dt_p.shape),
                  tile(512), tile(512), tile(512), tile(128), tile(128)],
        out_specs=[tile(C), tile(128), pl.BlockSpec((8, C), lambda s_: (0, 0)), _full_spec(alog_p.shape), _full_spec(dt_p.shape)],
        out_shape=[jax.ShapeDtypeStruct((S, C), BF16), jax.ShapeDtypeStruct((S, 128), BF16), jax.ShapeDtypeStruct((8, C), F32),
                   jax.ShapeDtypeStruct(alog_p.shape, F32), jax.ShapeDtypeStruct(dt_p.shape, F32)],
        scratch_shapes=[pltpu.VMEM((tm + 8, C), F32), pltpu.VMEM((tm + 8, C), F32), pltpu.VMEM((8, C), F32)],
        compiler_params=_params(("arbitrary",)),
    )(proj, proj, qkvc, proj, conv_w, alog_p, dt_p, *douts, d_kab_add)


SCAN_CHUNKS = (8, 4, 2, 1)


def _gdn_scan_fwd(u, wk, qd, kd, qks, gl, name):
    S = u.shape[0]
    nc = S // CHUNK
    cs = _pick(nc, SCAN_CHUNKS)
    W = HEADS * HEAD_DIM

    def body(u_ref, wk_ref, qd_ref, kd_ref, qk0, qk1, qk2, qk3, gl_ref, o_ref, sp_ref, s_ref):
        @pl.when(pl.program_id(0) == 0)
        def _():
            s_ref[...] = jnp.zeros_like(s_ref)

        state = s_ref[...]
        for c in range(cs):
            rows, gl_rows = slice(CHUNK * c, CHUNK * (c + 1)), slice(8 * c, 8 * (c + 1))
            sp_ref[c] = state
            o, state = _scan_step(state, _heads(u_ref, HEAD_DIM, rows), _heads(wk_ref, HEAD_DIM, rows),
                                  _heads(qd_ref, HEAD_DIM, rows), _heads(kd_ref, HEAD_DIM, rows),
                                  jnp.stack([r[rows, :] for r in (qk0, qk1, qk2, qk3)]), _heads(gl_ref, HEAD_DIM, gl_rows))
            for h in range(HEADS):
                o_ref[rows, HEAD_DIM * h:HEAD_DIM * (h + 1)] = o[h]
        s_ref[...] = state

    row = pl.BlockSpec((cs * CHUNK, W), lambda n: (n, 0))
    qk_spec = pl.BlockSpec((cs * CHUNK, CHUNK), lambda n: (n, 0))
    return pl.pallas_call(
        body, name=name, grid=(nc // cs,),
        in_specs=[row, row, row, row, qk_spec, qk_spec, qk_spec, qk_spec, pl.BlockSpec((cs * 8, W), lambda n: (n, 0))],
        out_specs=[row, pl.BlockSpec((cs, HEADS, HEAD_DIM, HEAD_DIM), lambda n: (n, 0, 0, 0))],
        out_shape=[jax.ShapeDtypeStruct((S, W), F32), jax.ShapeDtypeStruct((nc, HEADS, HEAD_DIM, HEAD_DIM), F32)],
        scratch_shapes=[pltpu.VMEM((HEADS, HEAD_DIM, HEAD_DIM), F32)],
        compiler_params=_params(("arbitrary",)),
    )(u, wk, qd, kd, *qks, gl)


def _gdn_scan_bwd(u, wk, qd, kd, qks, gl, s_prev, d_o, name):
    S = u.shape[0]
    nc = S // CHUNK
    cs = _pick(nc, SCAN_CHUNKS)
    nb = nc // cs
    W = HEADS * HEAD_DIM

    def body(u_ref, wk_ref, qd_ref, kd_ref, qk0, qk1, qk2, qk3, gl_ref, sp_ref, do_ref,
             du_ref, dwk_ref, dqd_ref, dkd_ref, dqk0, dqk1, dqk2, dqk3, dgl_ref, ds_ref):
        @pl.when(pl.program_id(0) == 0)
        def _():
            ds_ref[...] = jnp.zeros_like(ds_ref)

        d_state = ds_ref[...]
        for c in reversed(range(cs)):
            rows, gl_rows = slice(CHUNK * c, CHUNK * (c + 1)), slice(8 * c, 8 * (c + 1))
            _, vjp = jax.vjp(_scan_step, sp_ref[c], _heads(u_ref, HEAD_DIM, rows), _heads(wk_ref, HEAD_DIM, rows),
                             _heads(qd_ref, HEAD_DIM, rows), _heads(kd_ref, HEAD_DIM, rows),
                             jnp.stack([r[rows, :] for r in (qk0, qk1, qk2, qk3)]), _heads(gl_ref, HEAD_DIM, gl_rows))
            d_state, du, dwk, dqd, dkd, dqk, dgl = vjp((_heads(do_ref, HEAD_DIM, rows), d_state))
            for h, dqk_ref in enumerate((dqk0, dqk1, dqk2, dqk3)):
                sl = slice(HEAD_DIM * h, HEAD_DIM * (h + 1))
                du_ref[rows, sl] = du[h]
                dwk_ref[rows, sl] = dwk[h]
                dqd_ref[rows, sl] = dqd[h]
                dkd_ref[rows, sl] = dkd[h]
                dqk_ref[rows, :] = dqk[h]
                dgl_ref[gl_rows, sl] = dgl[h]
        ds_ref[...] = d_state

    rev = lambda n: (nb - 1 - n, 0)
    row = pl.BlockSpec((cs * CHUNK, W), rev)
    qk_spec = pl.BlockSpec((cs * CHUNK, CHUNK), rev)
    gl_spec = pl.BlockSpec((cs * 8, W), rev)
    qk_shape = jax.ShapeDtypeStruct((S, CHUNK), F32)
    row_shape = jax.ShapeDtypeStruct((S, W), F32)
    return pl.pallas_call(
        body, name=name, grid=(nb,),
        in_specs=[row, row, row, row, qk_spec, qk_spec, qk_spec, qk_spec, gl_spec,
                  pl.BlockSpec((cs, HEADS, HEAD_DIM, HEAD_DIM), lambda n: (nb - 1 - n, 0, 0, 0)), row],
        out_specs=[row, row, row, row, qk_spec, qk_spec, qk_spec, qk_spec, gl_spec],
        out_shape=[row_shape] * 4 + [qk_shape] * 4 + [jax.ShapeDtypeStruct((nc * 8, W), F32)],
        scratch_shapes=[pltpu.VMEM((HEADS, HEAD_DIM, HEAD_DIM), F32)],
        compiler_params=_params(("arbitrary",)),
    )(u, wk, qd, kd, *qks, gl, s_prev, d_o)


NEG = -1e30


def _chunk_mask(i, j, t, transposed=False):
    q_axis, k_axis = (1, 0) if transposed else (0, 1)
    r = (i * t + lax.broadcasted_iota(jnp.int32, (t, t), q_axis)) // CHUNK
    c = (j * t + lax.broadcasted_iota(jnp.int32, (t, t), k_axis)) // CHUNK
    return c <= r


def _tile_pairs(n, by_key):
    pairs = [(i, j) for j in range(n) for i in range(j, n)] if by_key else [(i, j) for i in range(n) for j in range(i + 1)]
    return jnp.asarray(np.array([p[0] for p in pairs], np.int32)), jnp.asarray(np.array([p[1] for p in pairs], np.int32))


def _heads(ref, width, rows=slice(None)):
    return jnp.stack([ref[rows, width * h:width * (h + 1)] for h in range(HEADS)])


def _bmm(a, b, dims):
    return lax.dot_general(a.astype(BF16), b.astype(BF16), dims, preferred_element_type=F32)


def _attn_fwd(q, k, v_t, t, name):
    S = q.shape[0]
    n = S // t
    qi, kj = _tile_pairs(n, by_key=False)

    def body(qi_ref, kj_ref, q_ref, k_ref, vt_ref, o_ref, lse_ref, m_ref, l_ref, acc_ref):
        i, j = qi_ref[pl.program_id(0)], kj_ref[pl.program_id(0)]

        @pl.when(j == 0)
        def _():
            m_ref[...] = jnp.full_like(m_ref, NEG)
            l_ref[...] = jnp.zeros_like(l_ref)
            acc_ref[...] = jnp.zeros_like(acc_ref)

        def update(masked):
            s_t = _bmm(_heads(k_ref, 256), _heads(q_ref, 256), _BNT)
            if masked:
                s_t = jnp.where(_chunk_mask(i, j, t, transposed=True)[None], s_t, NEG)
            m_old = m_ref[...]
            m_new = jnp.maximum(m_old, jnp.max(s_t, axis=1, keepdims=True))
            p_t = jnp.exp2(s_t - m_new)
            alpha = jnp.exp2(m_old - m_new)
            l_ref[...] = alpha * l_ref[...] + jnp.sum(p_t, axis=1, keepdims=True)
            v_heads = jnp.stack([vt_ref[HEAD_DIM * h:HEAD_DIM * (h + 1), :] for h in range(HEADS)])
            acc_ref[...] = alpha * acc_ref[...] + _bmm(v_heads, p_t, _BNN)
            m_ref[...] = m_new

        @pl.when(j < i)
        def _():
            update(False)

        @pl.when(j == i)
        def _():
            update(True)
            for h in range(HEADS):
                sl = slice(HEAD_DIM * h, HEAD_DIM * (h + 1))
                o_ref[:, sl] = jnp.transpose(acc_ref[h] / l_ref[h])
                lse_ref[h:h + 1, :] = m_ref[h] + jnp.log(l_ref[h]) * LOG2_E
            lse_ref[HEADS:, :] = jnp.zeros((8 - HEADS, t), F32)

    row = lambda p, qi_, kj_: (qi_[p], 0)
    return pl.pallas_call(
        body, name=name,
        grid_spec=pltpu.PrefetchScalarGridSpec(
            num_scalar_prefetch=2, grid=(qi.shape[0],),
            in_specs=[pl.BlockSpec((t, HEADS * 256), row), pl.BlockSpec((t, HEADS * 256), lambda p, qi_, kj_: (kj_[p], 0)),
                      pl.BlockSpec((HEADS * HEAD_DIM, t), lambda p, qi_, kj_: (0, kj_[p]))],
            out_specs=[pl.BlockSpec((t, HEADS * HEAD_DIM), row), pl.BlockSpec((8, t), lambda p, qi_, kj_: (0, qi_[p]))],
            scratch_shapes=[pltpu.VMEM((HEADS, 1, t), F32), pltpu.VMEM((HEADS, 1, t), F32),
                            pltpu.VMEM((HEADS, HEAD_DIM, t), F32)]),
        out_shape=[jax.ShapeDtypeStruct((S, HEADS * HEAD_DIM), F32), jax.ShapeDtypeStruct((8, S), F32)],
        compiler_params=_params(("arbitrary",)),
    )(qi, kj, q, k, v_t)


def _attn_stats(o, d_o, lse):
    lane = lax.broadcasted_iota(jnp.int32, (o.shape[0], HEAD_DIM), 1)
    stats = jnp.zeros((o.shape[0], HEAD_DIM), F32)
    for h in range(HEADS):
        sl = slice(HEAD_DIM * h, HEAD_DIM * (h + 1))
        delta = jnp.sum(d_o[:, sl] * o[:, sl], axis=1, keepdims=True)
        stats = stats + jnp.where(lane == HEADS + h, delta, 0.0)
    return lse + jnp.transpose(stats)[0:8, :]


BWD_GROUP = 2


def _attn_bwd(q, k, v, d_o, stats, t, name):
    S = q.shape[0]
    n = S // t
    groups = HEADS // BWD_GROUP
    gq, gv = BWD_GROUP * 256, BWD_GROUP * HEAD_DIM
    qi, kj = _tile_pairs(n, by_key=True)
    n_pairs = qi.shape[0]
    st = stats.reshape(2, groups, BWD_GROUP, S).transpose(1, 0, 2, 3).reshape(groups, 2 * BWD_GROUP, S)
    st = jnp.pad(st, ((0, 0), (0, 8 - 2 * BWD_GROUP), (0, 0)))

    def heads(ref, width, rows=slice(None)):
        return jnp.stack([ref[rows, width * h:width * (h + 1)] for h in range(BWD_GROUP)])

    def body(qi_ref, kj_ref, q_ref, k_ref, v_ref, do_ref, st_ref, dq_hbm, dk_ref, dv_ref, dq_acc, sem):
        g, p = pl.program_id(0), pl.program_id(1)
        i, j = qi_ref[p], kj_ref[p]

        @pl.when(i == j)
        def _():
            dk_ref[...] = jnp.zeros_like(dk_ref)
            dv_ref[...] = jnp.zeros_like(dv_ref)

        def update(masked):
            qh, kh = heads(q_ref, 256), heads(k_ref, 256)
            d_out = heads(do_ref, HEAD_DIM)
            stv = st_ref[...]
            lse_row = jnp.stack([stv[h:h + 1, :] for h in range(BWD_GROUP)])
            delta_row = jnp.stack([stv[BWD_GROUP + h:BWD_GROUP + h + 1, :] for h in range(BWD_GROUP)])
            s_t = _bmm(kh, qh, _BNT)
            p_t = jnp.exp2(s_t - lse_row)
            if masked:
                p_t = jnp.where(_chunk_mask(i, j, t, transposed=True)[None], p_t, 0.0)
            dv = _bmm(p_t, d_out, _BNN)
            dp_t = _bmm(heads(v_ref, HEAD_DIM), d_out, _BNT)
            ds_t = p_t * (dp_t - delta_row)
            dk = _bmm(ds_t, qh, _BNN)
            dq = _bmm(ds_t, kh, _BTN)
            rows = pl.ds(pl.multiple_of(i * t, t), t)
            for h in range(BWD_GROUP):
                dk_ref[:, 256 * h:256 * (h + 1)] += dk[h]
                dv_ref[:, HEAD_DIM * h:HEAD_DIM * (h + 1)] += dv[h]

            @pl.when(j == 0)
            def _():
                for h in range(BWD_GROUP):
                    dq_acc[rows, 256 * h:256 * (h + 1)] = dq[h]

            @pl.when(j > 0)
            def _():
                for h in range(BWD_GROUP):
                    dq_acc[rows, 256 * h:256 * (h + 1)] += dq[h]

        @pl.when(i == j)
        def _():
            update(True)

        @pl.when(i > j)
        def _():
            update(False)

        @pl.when(i == n - 1)
        def _():
            dk_ref[...] *= 1.0 / LOG2_E

        @pl.when(p == n_pairs - 1)
        def _():
            dq_acc[...] *= 1.0 / LOG2_E
            for gg in range(groups):
                @pl.when(g == gg)
                def _():
                    cp = pltpu.make_async_copy(dq_acc, dq_hbm.at[:, gq * gg:gq * (gg + 1)], sem)
                    cp.start()
                    cp.wait()

    q_blk = lambda g, p, qi_, kj_: (qi_[p], g)
    k_blk = lambda g, p, qi_, kj_: (kj_[p], g)
    return pl.pallas_call(
        body, name=name,
        grid_spec=pltpu.PrefetchScalarGridSpec(
            num_scalar_prefetch=2, grid=(groups, n_pairs),
            in_specs=[pl.BlockSpec((t, gq), q_blk), pl.BlockSpec((t, gq), k_blk), pl.BlockSpec((t, gv), k_blk),
                      pl.BlockSpec((t, gv), q_blk), pl.BlockSpec((None, 8, t), lambda g, p, qi_, kj_: (g, 0, qi_[p]))],
            out_specs=[pl.BlockSpec(memory_space=pl.ANY), pl.BlockSpec((t, gq), k_blk), pl.BlockSpec((t, gv), k_blk)],
            scratch_shapes=[pltpu.VMEM((S, gq), F32), pltpu.SemaphoreType.DMA]),
        out_shape=[jax.ShapeDtypeStruct((S, HEADS * 256), F32), jax.ShapeDtypeStruct((S, HEADS * 256), F32),
                   jax.ShapeDtypeStruct((S, HEADS * HEAD_DIM), F32)],
        compiler_params=_params(("arbitrary", "arbitrary")),
    )(qi, kj, q, k, v, d_o, st)


FFN_PIECE = 2 * D_FF // N_DEV
HID_PIECES = D_FF // FFN_PIECE


def _after_specs(after):
    return [] if after is None else [pl.BlockSpec(memory_space=pl.ANY)]


def _after_args(after):
    return [] if after is None else [after]


def _ffn_gw8(h, d_gate, d_up, name, after=None):
    S = h.shape[0]
    tm = 512
    tk = _pick(S, MATMUL_ROWS)
    nk = S // tk

    def body(h_ref, dg_ref, du_ref, *rest):
        o_ref, acc_ref = rest[-2:]
        k = pl.program_id(1)

        @pl.when(k == 0)
        def _():
            acc_ref[...] = jnp.zeros_like(acc_ref)

        h_t = jnp.transpose(h_ref[...])
        for p in range(HID_PIECES):
            acc_ref[p] += _dot_raw(h_t, dg_ref[p], "nn")
            acc_ref[HID_PIECES + p] += _dot_raw(h_t, du_ref[p], "nn")

        @pl.when(k == nk - 1)
        def _():
            o_ref[...] = acc_ref[...].astype(o_ref.dtype)

    d_spec = pl.BlockSpec((HID_PIECES, tk, FFN_PIECE), lambda i, k: (0, k, 0))
    return pl.pallas_call(
        body, name=name, grid=(D_MODEL // tm, nk),
        in_specs=[pl.BlockSpec((tk, tm), lambda i, k: (k, i)), d_spec, d_spec] + _after_specs(after),
        out_specs=pl.BlockSpec((2 * HID_PIECES, tm, FFN_PIECE), lambda i, k: (0, i, 0)),
        out_shape=jax.ShapeDtypeStruct((2 * HID_PIECES, D_MODEL, FFN_PIECE), BF16),
        scratch_shapes=[pltpu.VMEM((2 * HID_PIECES, tm, FFN_PIECE), F32)],
        compiler_params=_params(("parallel", "arbitrary")),
    )(h, d_gate, d_up, *_after_args(after))


EPILOGUE_ROWS = 256


def _dmod_epilogue(acc_ref, x_ref, do_ref, sc_ref, sh_ref, dx_ref, dsc_ref, dsh_ref, below, below_in, below_out):
    rows_total = acc_ref.shape[0]
    step = min(EPILOGUE_ROWS, rows_total)
    dsc, dsh, dg = 0.0, 0.0, 0.0
    for r in range(rows_total // step):
        rows = slice(step * r, step * (r + 1))
        _, vjp = jax.vjp(_modulate, x_ref[rows, :], sc_ref[...], sh_ref[...])
        dx, dsc_r, dsh_r = vjp(acc_ref[rows, :])
        dx = dx + do_ref[rows, :]
        dx_ref[rows, :] = dx
        dsc, dsh = dsc + dsc_r, dsh + dsh_r
        if below is not None:
            coef = below[2]
            below_out[0][rows, :] = (coef * below_in[1][...] * dx).astype(below_out[0].dtype)
            dg = dg + jnp.sum(coef * below_in[0][rows, :] * dx, axis=0, keepdims=True)
    dsc_ref[...] += dsc
    dsh_ref[...] += dsh
    if below is not None:
        below_out[1][...] += dg


def _ffn_dh(d_gate, d_up, w8, x, d_out, scale, shift, name, after=None, below=None):
    S = d_gate.shape[1]
    tm = _pick(S, MATMUL_ROWS)
    n_below = 0 if below is None else 2

    def body(dg_ref, du_ref, wg_ref, wu_ref, x_ref, do_ref, sc_ref, sh_ref, *rest):
        below_in = rest[:n_below]
        outs = rest[len(rest) - 4 - n_below:]
        dx_ref, dsc_ref, dsh_ref = outs[:3]
        below_out, acc_ref = outs[3:3 + n_below], outs[-1]
        i, k = pl.program_id(0), pl.program_id(1)

        @pl.when(k == 0)
        def _():
            acc_ref[...] = jnp.zeros_like(acc_ref)

        acc_ref[...] += _dot_raw(dg_ref[...], wg_ref[...], "nt") + _dot_raw(du_ref[...], wu_ref[...], "nt")

        @pl.when((k == 0) & (i == 0))
        def _():
            for r in (dsc_ref, dsh_ref) + tuple(below_out[1:]):
                r[...] = jnp.zeros_like(r)

        @pl.when(k == HID_PIECES - 1)
        def _():
            _dmod_epilogue(acc_ref, x_ref, do_ref, sc_ref, sh_ref, dx_ref, dsc_ref, dsh_ref, below, below_in, below_out)

    d_spec = pl.BlockSpec((None, tm, FFN_PIECE), lambda i, k: (k, i, 0))
    row = pl.BlockSpec((tm, D_MODEL), lambda i, k: (i, 0))
    par = pl.BlockSpec((1, D_MODEL), lambda i, k: (0, 0))
    row_shape, par_shape = jax.ShapeDtypeStruct((S, D_MODEL), F32), jax.ShapeDtypeStruct((1, D_MODEL), F32)
    return pl.pallas_call(
        body, name=name, grid=(S // tm, HID_PIECES),
        in_specs=[d_spec, d_spec,
                  pl.BlockSpec((None, D_MODEL, FFN_PIECE), lambda i, k: (k, 0, 0)),
                  pl.BlockSpec((None, D_MODEL, FFN_PIECE), lambda i, k: (k + HID_PIECES, 0, 0)),
                  row, row, par, par] + [row, par][:n_below] + _after_specs(after),
        out_specs=[row, par, par] + [row, par][:n_below],
        out_shape=[row_shape, par_shape, par_shape] + [jax.ShapeDtypeStruct((S, D_MODEL), BF16), par_shape][:n_below],
        scratch_shapes=[pltpu.VMEM((tm, D_MODEL), F32)],
        compiler_params=_params(("arbitrary", "arbitrary")),
    )(d_gate, d_up, w8, w8, x, d_out, scale, shift, *(below[:2] if below is not None else ()), *_after_args(after))


def _swiglu_bwd(d_hid, hid_by_gate, hid_by_up):
    return d_hid * hid_by_gate, d_hid * hid_by_up


def _adamw_math(w_, g_, m_, v_):
    m_ = ADAM_B1 * m_ + (1.0 - ADAM_B1) * g_
    v_ = ADAM_B2 * v_ + (1.0 - ADAM_B2) * (g_ * g_)
    m_hat = m_ / (1.0 - ADAM_B1 ** ADAM_STEP)
    v_hat = v_ / (1.0 - ADAM_B2 ** ADAM_STEP)
    return -ADAM_LR * (m_hat / (jnp.sqrt(v_hat) + ADAM_EPS) + ADAM_WD * w_), m_, v_


def _adamw(w, g, m, v, name):
    R, C = w.shape
    tr = _pick(R, (256, 176, 128, 64, 32, 16, 8))

    def body(w_ref, g_ref, m_ref, v_ref, d_ref, nm_ref, nv_ref):
        d_ref[...], nm_ref[...], nv_ref[...] = _adamw_math(w_ref[...], g_ref[...], m_ref[...], v_ref[...])

    spec = pl.BlockSpec((tr, C), lambda i: (i, 0))
    return pl.pallas_call(
        body, name=name, grid=(R // tr,),
        in_specs=[spec] * 4, out_specs=[spec] * 3,
        out_shape=[jax.ShapeDtypeStruct((R, C), F32)] * 3,
        compiler_params=_params(("parallel",)),
    )(w, g, m, v)


def _sum_adamw(parts, w, m, v, name, transposed=False):
    _, R, C = parts.shape
    tr = _pick(R, (256, 176, 128, 64, 32, 16, 8))

    def body(p_ref, w_ref, m_ref, v_ref, g_ref, d_ref, nm_ref, nv_ref):
        g_ = p_ref[0].astype(F32)
        for d in range(1, N_DEV):
            g_ = g_ + p_ref[d].astype(F32)
        if transposed:
            g_ = jnp.transpose(g_)
        g_ref[...] = g_
        d_ref[...], nm_ref[...], nv_ref[...] = _adamw_math(w_ref[...], g_, m_ref[...], v_ref[...])

    spec = pl.BlockSpec((C, tr), lambda i: (0, i)) if transposed else pl.BlockSpec((tr, C), lambda i: (i, 0))
    return pl.pallas_call(
        body, name=name, grid=(R // tr,),
        in_specs=[pl.BlockSpec((N_DEV, tr, C), lambda i: (0, i, 0)), spec, spec, spec], out_specs=[spec] * 4,
        out_shape=[jax.ShapeDtypeStruct(w.shape, F32)] * 4,
        compiler_params=_params(("parallel",)),
    )(parts, w, m, v)


def _sum_devices(parts, name):
    _, R, C = parts.shape
    tr = _pick(R, (512, 256, 176, 128, 64, 32, 16, 8))

    def body(p_ref, o_ref):
        acc = p_ref[0].astype(F32)
        for d in range(1, N_DEV):
            acc = acc + p_ref[d].astype(F32)
        o_ref[...] = acc

    return pl.pallas_call(
        body, name=name, grid=(R // tr,),
        in_specs=[pl.BlockSpec((N_DEV, tr, C), lambda i: (0, i, 0))],
        out_specs=pl.BlockSpec((tr, C), lambda i: (i, 0)),
        out_shape=jax.ShapeDtypeStruct((R, C), F32),
        compiler_params=_params(("parallel",)),
    )(parts)


def _my_place():
    return lax.axis_index("x"), lax.axis_index("y"), lax.axis_index("c")


def _all_gather(blocks, name):
    n = len(blocks)

    def body(*refs):
        x_refs, out_refs = refs[:n], refs[n:2 * n]
        send_sems, recv_sems, local_sems = refs[2 * n:]
        x, y, c = _my_place()
        me, sibling = (x, y, c), (x, y, 1 - c)
        chips = [(1 - x, y), (x, 1 - y), (1 - x, 1 - y)]

        def copy(a, k, blk, to, own=False):
            slot = out_refs[a].at[4 * blk[0] + 2 * blk[1] + blk[2]]
            return pltpu.make_async_remote_copy(
                src_ref=x_refs[a] if own else slot, dst_ref=slot,
                send_sem=send_sems.at[7 * a + k], recv_sem=recv_sems.at[7 * a + k], device_id=to, device_id_type=MESH)

        mine = [pltpu.make_async_copy(x_refs[a], out_refs[a].at[4 * x + 2 * y + c], local_sems.at[a]) for a in range(n)]
        for cp in mine:
            cp.start()
        first = []
        for j, chip in enumerate(chips):
            first += [copy(a, 1 + j, me, (*chip, c), own=True) for a in range(n)]
        first += [copy(a, 0, me, sibling, own=True) for a in range(n)]
        for cp in first:
            cp.start()
        passed = []
        for j, chip in enumerate(chips):
            for a in range(n):
                copy(a, 1 + j, (*chip, c), me).wait_recv()
                passed.append(copy(a, 4 + j, (*chip, c), sibling))
                passed[-1].start()
        for a in range(n):
            copy(a, 0, sibling, me).wait_recv()
        for j, chip in enumerate(chips):
            for a in range(n):
                copy(a, 4 + j, (*chip, 1 - c), me).wait_recv()
        for cp in first + passed:
            cp.wait_send()
        for cp in mine:
            cp.wait()

    return pl.pallas_call(
        body, name=name,
        out_shape=[jax.ShapeDtypeStruct((N_DEV,) + b.shape, b.dtype) for b in blocks],
        in_specs=[pl.BlockSpec(memory_space=pl.ANY)] * n,
        out_specs=[pl.BlockSpec(memory_space=pl.ANY)] * n,
        scratch_shapes=[pltpu.SemaphoreType.DMA((7 * n,)), pltpu.SemaphoreType.DMA((7 * n,)), pltpu.SemaphoreType.DMA((n,))],
    )(*blocks)


def _all_to_all(pieces, name):
    n = len(pieces)

    def body(*refs):
        x_refs, out_refs = refs[:n], refs[n:2 * n]
        send_sems, recv_sems, local_sems = refs[2 * n:]
        x, y, c = _my_place()
        me = 4 * x + 2 * y + c
        mine = [pltpu.make_async_copy(x_refs[a].at[me], out_refs[a].at[me], local_sems.at[a]) for a in range(n)]
        for cp in mine:
            cp.start()
        copies = []
        for k in (2, 4, 6, 3, 5, 7, 1):
            px = 1 - x if k & 4 else x
            py = 1 - y if k & 2 else y
            pc = 1 - c if k & 1 else c
            peer = 4 * px + 2 * py + pc
            for a in range(n):
                copies.append(pltpu.make_async_remote_copy(
                    src_ref=x_refs[a].at[peer], dst_ref=out_refs[a].at[me],
                    send_sem=send_sems.at[7 * a + k - 1], recv_sem=recv_sems.at[7 * a + k - 1],
                    device_id=(px, py, pc), device_id_type=MESH))
        for cp in copies:
            cp.start()
        for cp in copies:
            cp.wait_recv()
        for cp in copies:
            cp.wait_send()
        for cp in mine:
            cp.wait()

    return pl.pallas_call(
        body, name=name,
        out_shape=[jax.ShapeDtypeStruct(p.shape, p.dtype) for p in pieces],
        in_specs=[pl.BlockSpec(memory_space=pl.ANY)] * n,
        out_specs=[pl.BlockSpec(memory_space=pl.ANY)] * n,
        scratch_shapes=[pltpu.SemaphoreType.DMA((7 * n,)), pltpu.SemaphoreType.DMA((7 * n,)), pltpu.SemaphoreType.DMA((n,))],
    )(*pieces)


def _peers():
    x, y, c = _my_place()
    out = []
    for k in (2, 4, 6, 3, 5, 7, 1):
        px = 1 - x if k & 4 else x
        py = 1 - y if k & 2 else y
        pc = 1 - c if k & 1 else c
        out.append((k, (px, py, pc), 4 * px + 2 * py + pc))
    return out


def _exchange_copies(x_refs, land_refs, send_sems, recv_sems, scatter):
    x, y, c = _my_place()
    me = 4 * x + 2 * y + c
    starts, arrivals = [], []
    for k, place, peer in _peers():
        for a, (x_ref, land_ref) in enumerate(zip(x_refs, land_refs)):
            sems = dict(send_sem=send_sems.at[7 * a + k - 1], recv_sem=recv_sems.at[7 * a + k - 1],
                        device_id=place, device_id_type=MESH)
            src = x_ref.at[peer] if scatter else x_ref
            starts.append(pltpu.make_async_remote_copy(src_ref=src, dst_ref=land_ref.at[me], **sems))
            arrivals.append(pltpu.make_async_remote_copy(src_ref=src, dst_ref=land_ref.at[peer], **sems))
    return starts, arrivals


def _exchange_start(arrays, scatter, name):
    n = len(arrays)
    hbm = pl.BlockSpec(memory_space=pltpu.HBM)
    sem = pl.BlockSpec(memory_space=pltpu.SEMAPHORE)
    lands = [lax.empty(a.shape if scatter else (N_DEV,) + a.shape, a.dtype) for a in arrays]

    def body(*refs):
        x_refs, land_refs = refs[:n], refs[n:2 * n]
        send_sems, recv_sems = refs[2 * n], refs[2 * n + 1]
        token = refs[-1]
        starts, _ = _exchange_copies(x_refs, land_refs, send_sems, recv_sems, scatter)
        for cp in starts:
            cp.start()
        token[...] = jnp.zeros_like(token)

    res = pl.pallas_call(
        body, name=name,
        out_shape=(pltpu.SemaphoreType.DMA((7 * n,)), pltpu.SemaphoreType.DMA((7 * n,)),
                   *[pltpu.HBM(a.shape, a.dtype) for a in arrays], *[pltpu.HBM(l.shape, l.dtype) for l in lands],
                   jax.ShapeDtypeStruct((8, 128), F32)),
        in_specs=[hbm] * (2 * n),
        out_specs=(sem, sem, *[hbm] * (2 * n), pl.BlockSpec(memory_space=pltpu.VMEM)),
        input_output_aliases={i: 2 + i for i in range(2 * n)},
        compiler_params=pltpu.CompilerParams(has_side_effects=pltpu.SideEffectType.DATAFLOW_SIDE_EFFECTING),
    )(*[pltpu.with_memory_space_constraint(a, pltpu.HBM) for a in arrays],
      *[pltpu.with_memory_space_constraint(l, pltpu.HBM) for l in lands])
    return res[0], res[1], list(res[2:2 + n]), list(res[2 + n:2 + 2 * n]), res[-1]


def _exchange_wait(handles, scatter, after, name):
    send_sems, recv_sems, arrays, lands, _ = handles
    n = len(arrays)
    hbm = pl.BlockSpec(memory_space=pltpu.HBM)
    sem = pl.BlockSpec(memory_space=pltpu.SEMAPHORE)

    def body(*refs):
        x_refs, land_refs = refs[:n], refs[n:2 * n]
        send_s, recv_s = refs[2 * n], refs[2 * n + 1]
        starts, arrivals = _exchange_copies(x_refs, land_refs, send_s, recv_s, scatter)
        for cp in arrivals:
            cp.wait_recv()
        for cp in starts:
            cp.wait_send()

    res = pl.pallas_call(
        body, name=name,
        out_shape=(*[pltpu.HBM(a.shape, a.dtype) for a in arrays], *[pltpu.HBM(l.shape, l.dtype) for l in lands]),
        in_specs=[hbm] * (2 * n) + [sem, sem, pl.BlockSpec(memory_space=pl.ANY)],
        out_specs=tuple([hbm] * (2 * n)),
        input_output_aliases={i: i for i in range(2 * n)},
        compiler_params=pltpu.CompilerParams(has_side_effects=pltpu.SideEffectType.DATAFLOW_SIDE_EFFECTING),
    )(*arrays, *lands, send_sems, recv_sems, after)
    me = 4 * lax.axis_index("x") + 2 * lax.axis_index("y") + lax.axis_index("c")
    out = []
    for src, got in zip(res[:n], res[n:]):
        zeros = (0,) * (got.ndim - 1)
        own = lax.dynamic_slice(src, (me,) + zeros, (1,) + src.shape[1:]) if scatter else src[None]
        out.append(lax.dynamic_update_slice(got, own, (me,) + zeros))
    return out


def _pad_lanes(v, at=0, width=128):
    return jnp.pad(v, ((0, 0), (at, width - at - v.shape[1])))


def _pack_weights(P):
    W = {}
    w = P["w_in"]
    W["wp"] = jnp.concatenate([w[:, :2048], w[:, 2440:2696], w[:, 2056:2440], w[:, 2696:2760], w[:, 2048:2056],
                               jnp.zeros((D_MODEL, N_IN_PACKED - N_IN), w.dtype)], axis=1).astype(BF16)
    W["conv_w"] = P["gdn_conv_w"].astype(F32)
    W["alog_p"] = _pad_lanes(P["gdn_a_log"], 64)
    W["dt_p"] = _pad_lanes(P["gdn_dt_bias"], 64)
    W["gnw"] = P["gdn_norm_w"]
    W["qnw"] = P["mla_q_norm_w"]
    W["kvnw"] = P["mla_kv_norm_w"]
    uq = P["mla_w_uq"].reshape(Q_LORA, HEADS, HEAD_DIM + ROPE)
    W["wuq"] = jnp.pad(uq, ((0, 0), (0, 0), (0, 256 - HEAD_DIM - ROPE))).reshape(Q_LORA, HEADS * 256).astype(BF16)
    ukv = P["mla_w_ukv"].reshape(KV_LORA, HEADS, 2, HEAD_DIM)
    W["wukv"] = ukv.transpose(0, 2, 1, 3).reshape(KV_LORA, 2 * HEADS * HEAD_DIM).astype(BF16)
    W["qn_w"] = P["qkn_q_nope"]
    W["qr_w"] = _pad_lanes(P["qkn_q_rope"])
    W["kn_w"] = P["qkn_k_nope"]
    W["kr_w"] = _pad_lanes(P["qkn_k_rope"])
    W["onw"] = P["mla_out_norm_w"]
    W["wout"] = P["w_out"].astype(BF16)
    return W


def _unpack_grads(G):
    g_qkv, g_z, g_ckv, g_cq, g_kab = G["wp"]
    uq = G["wuq"].reshape(Q_LORA, HEADS, 256)[:, :, :HEAD_DIM + ROPE].reshape(Q_LORA, HEADS * (HEAD_DIM + ROPE))
    ukv = G["wukv"].reshape(KV_LORA, 2, HEADS, HEAD_DIM).transpose(0, 2, 1, 3).reshape(KV_LORA, 2 * HEADS * HEAD_DIM)
    return {
        "w_in": jnp.concatenate([g_qkv, g_z, g_kab[:, ROPE:ROPE + 8], g_cq, g_ckv, g_kab[:, :ROPE]], axis=1),
        "gdn_conv_w": G["conv_w"], "gdn_a_log": G["alog_p"][:, 64:68], "gdn_dt_bias": G["dt_p"][:, 64:68],
        "gdn_norm_w": G["gnw"], "mla_q_norm_w": G["qnw"], "mla_w_uq": uq, "mla_kv_norm_w": G["kvnw"], "mla_w_ukv": ukv,
        "qkn_q_nope": G["qn_w"], "qkn_q_rope": G["qr_w"][:, :ROPE], "qkn_k_nope": G["kn_w"], "qkn_k_rope": G["kr_w"][:, :ROPE],
        "mla_out_norm_w": G["onw"], "w_out": G["wout"],
    }


def _rope_tables(positions):
    half = ROPE // 2
    inv_freq = ROPE_BASE ** (-jnp.arange(half, dtype=F32) / half)
    ang = positions.astype(F32)[:, None] * inv_freq
    cos, sin = jnp.cos(ang), jnp.sin(ang)
    zeros = jnp.zeros((positions.shape[0], 128 - ROPE), F32)
    return jnp.concatenate([cos, cos, zeros], axis=1), jnp.concatenate([-sin, sin, zeros], axis=1)


def _ffn_forward(x, scale, shift, gate_w, w8, wo4, name, target=None):
    S = x.shape[0]
    tm = _pick(S, (512, 256, 128))
    n = S // tm
    with_loss = target is not None

    def body(x_ref, sc_ref, sh_ref, g_ref, wg_ref, wu_ref, wo_ref, *rest):
        t_ref = rest[0] if with_loss else None
        h_ref, bg_ref, bu_ref, ht_ref = rest[with_loss:with_loss + 4]
        tail = rest[with_loss + 4:]
        h_scr, acc_ref = tail[-2:]
        i, p = pl.program_id(0), pl.program_id(1)

        @pl.when(p == 0)
        def _():
            h_new = _modulate(x_ref[...], sc_ref[...], sh_ref[...]).astype(BF16)
            h_scr[...] = h_new
            h_ref[...] = h_new
            acc_ref[...] = jnp.zeros_like(acc_ref)

        h = h_scr[...]
        gate = _dot_raw(h, wg_ref[...], "nn")
        up = _dot_raw(h, wu_ref[...], "nn")
        sg = _sigmoid(gate)
        act = gate * sg
        hid = act * up
        bg_ref[...] = (up * (sg * (1.0 + gate * (1.0 - sg)))).astype(BF16)
        bu_ref[...] = act.astype(BF16)
        ht_ref[...] = jnp.transpose(hid).astype(BF16)
        acc_ref[...] += _dot_raw(hid, wo_ref[...], "nn")

        if with_loss:
            dx_ref, df_ref, dg_ref, l_ref = tail[:4]

            @pl.when((p == 0) & (i == 0))
            def _():
                dg_ref[...] = jnp.zeros_like(dg_ref)
                l_ref[...] = jnp.zeros_like(l_ref)

            @pl.when(p == HID_PIECES - 1)
            def _():
                step = min(EPILOGUE_ROWS, tm)
                for r in range(tm // step):
                    rows = slice(step * r, step * (r + 1))
                    f = acc_ref[rows, :]
                    diff = x_ref[rows, :] + 0.5 * g_ref[...] * f - t_ref[rows, :]
                    dx = diff * (1.0 / D_MODEL)
                    dx_ref[rows, :] = dx
                    df_ref[rows, :] = (0.5 * g_ref[...] * dx).astype(df_ref.dtype)
                    dg_ref[...] += jnp.sum(0.5 * f * dx, axis=0, keepdims=True)
                    l_ref[...] += jnp.sum(diff * diff, axis=0, keepdims=True)

            @pl.when((p == HID_PIECES - 1) & (i == n - 1))
            def _():
                l_ref[...] = jnp.full(l_ref.shape, (0.5 / D_MODEL) * jnp.sum(l_ref[...]), F32)
        else:
            f_ref, xo_ref = tail[:2]

            @pl.when(p == HID_PIECES - 1)
            def _():
                f = acc_ref[...]
                f_ref[...] = f.astype(f_ref.dtype)
                xo_ref[...] = x_ref[...] + 0.5 * g_ref[...] * f

    row = pl.BlockSpec((tm, D_MODEL), lambda i, p: (i, 0))
    par = pl.BlockSpec((1, D_MODEL), lambda i, p: (0, 0))
    piece = pl.BlockSpec((None, tm, FFN_PIECE), lambda i, p: (p, i, 0))
    row_f32, row_bf16 = jax.ShapeDtypeStruct((S, D_MODEL), F32), jax.ShapeDtypeStruct((S, D_MODEL), BF16)
    par_f32 = jax.ShapeDtypeStruct((1, D_MODEL), F32)
    piece_shape = jax.ShapeDtypeStruct((HID_PIECES, S, FFN_PIECE), BF16)
    return pl.pallas_call(
        body, name=name, grid=(n, HID_PIECES),
        in_specs=[row, par, par, par,
                  pl.BlockSpec((None, D_MODEL, FFN_PIECE), lambda i, p: (p, 0, 0)),
                  pl.BlockSpec((None, D_MODEL, FFN_PIECE), lambda i, p: (p + HID_PIECES, 0, 0)),
                  pl.BlockSpec((None, FFN_PIECE, D_MODEL), lambda i, p: (p, 0, 0))] + [row] * with_loss,
        out_specs=[row, piece, piece, pl.BlockSpec((None, FFN_PIECE, tm), lambda i, p: (p, 0, i))]
        + ([row, row, par, par] if with_loss else [row, row]),
        out_shape=[row_bf16, piece_shape, piece_shape, jax.ShapeDtypeStruct((HID_PIECES, FFN_PIECE, S), BF16)]
        + ([row_f32, row_bf16, par_f32, par_f32] if with_loss else [row_bf16, row_f32]),
        scratch_shapes=[pltpu.VMEM((tm, D_MODEL), BF16), pltpu.VMEM((tm, D_MODEL), F32)],
        compiler_params=_params(("arbitrary", "arbitrary")),
    )(x, scale, shift, gate_w, w8, w8, wo4, *([target] if with_loss else []))


def _ffn_fwd(x, scale, shift, gate_w, w8, wo4, tag, target=None):
    res = _ffn_forward(x, scale, shift, gate_w, w8, wo4, tag + "_fwd", target)
    h, by_gate, by_up, hid_t = res[:4]
    if target is not None:
        dx_out, df, d_gate_w, loss_row = res[4:]
        return (dx_out, loss_row), (h, by_gate, by_up, hid_t, None, df, d_gate_w)
    f, x_out = res[4:]
    return x_out, (h, by_gate, by_up, hid_t, f, None, None)


def _ffn_bwd(d_out, x, scale, shift, gate_w, w8, wo4, saved, tag, grad_ready, below=None):
    h, gate, up, hid_t, f, df, d_gate_w = saved
    S = x.shape[0]
    tm = _pick(S, (512, 256, 128))
    tk = _pick(S, (512, 256, 128))
    n = S // tm
    if df is None:
        (df,), (d_gate_w,) = _rowwise_bwd(lambda f_, g_: (0.5 * g_ * f_,), [(f, tm, D_MODEL, 0)], [], [gate_w],
                                          [(d_out, tm, D_MODEL, 0)], n, tag + "_dres", row_dtypes=(BF16,))
    tb = _pick(S, MATMUL_ROWS)
    piece = pl.BlockSpec((None, tb, FFN_PIECE), lambda i, j, k: (j, i, 0))
    d_gate, d_up = _mmg(df, wo4, "nt", name=tag + "_ddown", grid=(S // tb, HID_PIECES, 1),
                        a_spec=pl.BlockSpec((tb, D_MODEL), lambda i, j, k: (i, 0)),
                        b_spec=pl.BlockSpec((None, FFN_PIECE, D_MODEL), lambda i, j, k: (j, 0, 0)),
                        out_spec=piece, out_shapes=[jax.ShapeDtypeStruct((HID_PIECES, S, FFN_PIECE), BF16)] * 2,
                        acc_shape=(tb, FFN_PIECE), extras=[gate, up], extra_specs=[piece, piece], epi=_swiglu_bwd)
    tk = _pick(S, MATMUL_ROWS)
    g_wo4 = _mmg(hid_t, df, "nn", name=tag + "_gwo", grid=(HID_PIECES, 1, S // tk),
                 a_spec=pl.BlockSpec((None, FFN_PIECE, tk), lambda i, j, k: (i, 0, k)),
                 b_spec=pl.BlockSpec((tk, D_MODEL), lambda i, j, k: (k, j)),
                 out_spec=pl.BlockSpec((None, FFN_PIECE, D_MODEL), lambda i, j, k: (i, 0, j)),
                 out_shapes=[jax.ShapeDtypeStruct((HID_PIECES, FFN_PIECE, D_MODEL), BF16)], acc_shape=(FFN_PIECE, D_MODEL))
    g_w8 = _ffn_gw8(h, d_gate, d_up, tag + "_gw8", after=grad_ready("wo4", g_wo4))
    res = _ffn_dh(d_gate, d_up, w8, x, d_out, scale, shift, tag + "_dh", after=grad_ready("w8", g_w8), below=below)
    return (res[0], res[1], res[2], d_gate_w) + tuple(res[3:])


def _dproj_dmod(d_pieces, wp, x, d_out, scale, shift, name, below=None):
    S = x.shape[0]
    tm = _pick(S, TOKEN_ROWS)
    widths = [p.shape[1] for p in d_pieces]
    starts = [sum(widths[:n]) for n in range(len(widths))]
    n_p = len(d_pieces)
    n_below = 0 if below is None else 2

    def body(*refs):
        dp_refs, (w_ref, x_ref, do_ref, sc_ref, sh_ref), rest = refs[:n_p], refs[n_p:n_p + 5], refs[n_p + 5:]
        below_in = rest[:n_below]
        dx_ref, dsc_ref, dsh_ref = rest[n_below:n_below + 3]
        below_out, acc_ref = rest[n_below + 3:n_below + 3 + n_below], rest[-1]

        @pl.when(pl.program_id(0) == 0)
        def _():
            for r in (dsc_ref, dsh_ref) + tuple(below_out[1:]):
                r[...] = jnp.zeros_like(r)

        acc = None
        for dp_ref, at, width in zip(dp_refs, starts, widths):
            part = _dot_raw(dp_ref[...], w_ref[:, at:at + width], "nt")
            acc = part if acc is None else acc + part
        acc_ref[...] = acc
        _dmod_epilogue(acc_ref, x_ref, do_ref, sc_ref, sh_ref, dx_ref, dsc_ref, dsh_ref, below, below_in, below_out)

    row = pl.BlockSpec((tm, D_MODEL), lambda i: (i, 0))
    par = pl.BlockSpec((1, D_MODEL), lambda i: (0, 0))
    row_shape, par_shape = jax.ShapeDtypeStruct((S, D_MODEL), F32), jax.ShapeDtypeStruct((1, D_MODEL), F32)
    return pl.pallas_call(
        body, name=name, grid=(S // tm,),
        in_specs=[pl.BlockSpec((tm, width), lambda i: (i, 0)) for width in widths]
        + [pl.BlockSpec(wp.shape, lambda i: (0, 0)), row, row, par, par] + [row, par][:n_below],
        out_specs=[row, par, par] + [row, par][:n_below],
        out_shape=[row_shape, par_shape, par_shape] + [jax.ShapeDtypeStruct((S, D_MODEL), BF16), par_shape][:n_below],
        scratch_shapes=[pltpu.VMEM((tm, D_MODEL), F32)],
        compiler_params=_params(("arbitrary",)),
    )(*d_pieces, wp, x, d_out, scale, shift, *(below[:2] if below is not None else ()))


def _gw_pieces(h, d_pieces, name):
    S = h.shape[0]
    tm = 512
    tk = _pick(S, MATMUL_ROWS)
    n_p = len(d_pieces)
    widths = [p.shape[1] for p in d_pieces]

    def body(h_ref, *rest):
        d_refs, o_refs = rest[:n_p], rest[n_p:]

        @pl.when(pl.program_id(1) == 0)
        def _():
            for o_ref in o_refs:
                o_ref[...] = jnp.zeros_like(o_ref)

        h_t = jnp.transpose(h_ref[...])
        for d_ref, o_ref in zip(d_refs, o_refs):
            o_ref[...] += _dot_raw(h_t, d_ref[...], "nn")

    return pl.pallas_call(
        body, name=name, grid=(D_MODEL // tm, S // tk),
        in_specs=[pl.BlockSpec((tk, tm), lambda i, k: (k, i))] + [pl.BlockSpec((tk, width), lambda i, k: (k, 0)) for width in widths],
        out_specs=[pl.BlockSpec((tm, width), lambda i, k: (i, 0)) for width in widths],
        out_shape=[jax.ShapeDtypeStruct((D_MODEL, width), F32) for width in widths],
        compiler_params=_params(("parallel", "arbitrary")),
    )(h, *d_pieces)


def _mod_proj(x, scale, shift, wp, name):
    S = x.shape[0]
    N = wp.shape[1]
    tm = _pick(S, MATMUL_ROWS)
    tn = _pick(N, (1408, 1024, 512, 256, 128))

    def body(x_ref, sc_ref, sh_ref, w_ref, h_ref, o_ref, h_scr):
        @pl.when(pl.program_id(1) == 0)
        def _():
            h_new = _modulate(x_ref[...], sc_ref[...], sh_ref[...]).astype(BF16)
            h_scr[...] = h_new
            h_ref[...] = h_new

        o_ref[...] = _dot_raw(h_scr[...], w_ref[...], "nn")

    row = pl.BlockSpec((tm, D_MODEL), lambda i, j: (i, 0))
    par = pl.BlockSpec((1, D_MODEL), lambda i, j: (0, 0))
    return pl.pallas_call(
        body, name=name, grid=(S // tm, N // tn),
        in_specs=[row, par, par, pl.BlockSpec((D_MODEL, tn), lambda i, j: (0, j))],
        out_specs=[row, pl.BlockSpec((tm, tn), lambda i, j: (i, j))],
        out_shape=[jax.ShapeDtypeStruct((S, D_MODEL), BF16), jax.ShapeDtypeStruct((S, N), F32)],
        scratch_shapes=[pltpu.VMEM((tm, D_MODEL), BF16)],
        compiler_params=_params(("parallel", "arbitrary")),
    )(x, scale, shift, wp)


def _mix_out(o_a, proj, o_b, gnw, onw, wout, x, gate_w, t, name):
    S = x.shape[0]

    def body(oa_ref, z_ref, ob_ref, gn_ref, on_ref, w_ref, x_ref, g_ref, mixed_ref, y_ref, xo_ref):
        (mixed,) = _mix_post_fn(oa_ref[...], z_ref[...], ob_ref[...], gn_ref[...], on_ref[...])
        mixed = mixed.astype(BF16)
        mixed_ref[...] = mixed
        y = _dot_raw(mixed, w_ref[...], "nn")
        y_ref[...] = y.astype(BF16)
        xo_ref[...] = x_ref[...] + g_ref[...] * y

    half, row = _row_spec(t, 512, 0), _row_spec(t, D_MODEL, 0)
    return pl.pallas_call(
        body, name=name, grid=(S // t,),
        in_specs=[half, _row_spec(t, 512, 3), half, _full_spec(gnw.shape), _full_spec(onw.shape), _full_spec(wout.shape),
                  row, _full_spec(gate_w.shape)],
        out_specs=[row, row, row],
        out_shape=[jax.ShapeDtypeStruct((S, D_MODEL), BF16)] * 2 + [jax.ShapeDtypeStruct((S, D_MODEL), F32)],
        compiler_params=_params(("parallel",)),
    )(o_a, proj, o_b, gnw, onw, wout, x, gate_w)


def _mixer_fwd(x1, scale, shift, gate_w, cos_p, sin_p, W):
    S = x1.shape[0]
    tm = _pick(S, (512, 256, 128))
    tv = _pick(S, TOKEN_ROWS)
    ta = _pick(S, (512, 256, 128))
    nc = S // CHUNK
    h2, proj = _mod_proj(x1, scale, shift, W["wp"], "mix_proj")
    qkvc, q_a, k_a, v_a, gb = _conv_fwd(proj, W["conv_w"], W["alog_p"], W["dt_p"], tv, "gdn_conv")
    kab = (proj, tv, 128, 21)
    ti = _pick(S, INTRA_ROWS)
    intra = _rowwise(_gdn_intra_fn, [(q_a, ti, 512, 0), (k_a, ti, 512, 0), (v_a, ti, 512, 0), (gb, ti, 128, 0)],
                     [], [(ti, 512, F32)] * 4 + [(ti, CHUNK, F32)] * 4 + [(ti // 8, 512, F32)] + [(ti, CHUNK, F32)] * 4,
                     S // ti, "gdn_intra")
    u, wk, qd, kd, qks, gl, invs = intra[0], intra[1], intra[2], intra[3], tuple(intra[4:8]), intra[8], tuple(intra[9:])
    o_a, s_prev = _gdn_scan_fwd(u, wk, qd, kd, qks, gl, "gdn_scan")
    mla_params = [W["qnw"], W["kvnw"], W["wuq"], W["wukv"], W["qn_w"], W["qr_w"], W["kn_w"], W["kr_w"]]
    def mla_pre_with_vt(*a):
        q_, k_, v_ = _mla_pre_fn(*a)
        return q_, k_, v_, jnp.transpose(v_)

    q_b, k_b, v_b, vt_b = _rowwise(mla_pre_with_vt,
                                   [(proj, tv, 256, 8), (proj, tv, 384, 6), kab, (cos_p, tv, 128, 0), (sin_p, tv, 128, 0)],
                                   mla_params, [(tv, 1024, BF16), (tv, 1024, BF16), (tv, 512, BF16), (512, tv, BF16, "across")],
                                   S // tv, "mla_pre")
    o_b, lse = _attn_fwd(q_b, k_b, vt_b, ta, "mla_attn")
    mixed, y, x2 = _mix_out(o_a, proj, o_b, W["gnw"], W["onw"], W["wout"], x1, gate_w, tv, "mix_out")
    saved = (h2, proj, qkvc, q_a, k_a, v_a, gb, u, wk, qd, kd, qks, gl, invs, s_prev, o_a, q_b, k_b, v_b, o_b, lse, mixed, y)
    return x2, saved


def _mixer_bwd(d_out, dy, x1, scale, shift, cos_p, sin_p, W, saved, below):
    (h2, proj, qkvc, q_a, k_a, v_a, gb, u, wk, qd, kd, qks, gl, invs, s_prev, o_a, q_b, k_b, v_b, o_b, lse, mixed, y) = saved
    S = x1.shape[0]
    tm = _pick(S, (512, 256, 128))
    tv = _pick(S, TOKEN_ROWS)
    ta = _pick(S, (512, 256, 128))
    nc = S // CHUNK
    G = {}
    G["wout"] = _mm(mixed, dy, "tn", name="mix_gwout")
    (do_a, dz, do_b), (G["gnw"], G["onw"], stats) = _rowwise_bwd(
        _mix_post_fn, [(o_a, tv, 512, 0), (proj, tv, 512, 3), (o_b, tv, 512, 0)], [], [W["gnw"], W["onw"]],
        [], S // tv, "mix_dpost", row_dtypes=(F32, BF16, F32), dout_from=(dy, W["wout"]),
        across=(lse, lambda r_vals, d_rows, lse_tile: _attn_stats(r_vals[2], d_rows[2], lse_tile)))
    dq_b, dk_b, dv_b = _attn_bwd(q_b, k_b, v_b, do_b, stats, ta, "mla_dattn")
    kab = (proj, tv, 128, 21)
    mla_params = [W["qnw"], W["kvnw"], W["wuq"], W["wukv"], W["qn_w"], W["qr_w"], W["kn_w"], W["kr_w"]]
    (d_ckv, d_cq, d_kab), mla_grads = _rowwise_bwd(
        _mla_pre_fn, [(proj, tv, 256, 8), (proj, tv, 384, 6), kab], [(cos_p, tv, 128, 0), (sin_p, tv, 128, 0)], mla_params,
        [(dq_b, tv, 1024, 0), (dk_b, tv, 1024, 0), (dv_b, tv, 512, 0)], S // tv, "mla_dpre", row_dtypes=(BF16, BF16, F32))
    for key, g in zip(("qnw", "kvnw", "wuq", "wukv", "qn_w", "qr_w", "kn_w", "kr_w"), mla_grads):
        G[key] = g
    scan_grads = _gdn_scan_bwd(u, wk, qd, kd, qks, gl, s_prev, do_a, "gdn_dscan")
    ti = _pick(S, INTRA_ROWS)
    intra_douts = [(scan_grads[i], ti, 512, 0) for i in range(4)] + [(scan_grads[4 + i], ti, CHUNK, 0) for i in range(4)]
    intra_douts.append((scan_grads[8], ti // 8, 512, 0))
    (dq_a, dk_a, dv_a, d_gb), _ = _rowwise_bwd(
        _gdn_intra_fn, [(q_a, ti, 512, 0), (k_a, ti, 512, 0), (v_a, ti, 512, 0), (gb, ti, 128, 0)],
        [(x_, ti, CHUNK, 0) for x_ in invs], [], intra_douts, S // ti, "gdn_dintra")
    d_qkv, d_kab, g_conv, G["alog_p"], G["dt_p"] = _conv_bwd(proj, qkvc, W["conv_w"], W["alog_p"], W["dt_p"],
                                                              (dq_a, dk_a, dv_a, d_gb), d_kab, tv, "gdn_dconv")
    G["conv_w"] = g_conv[:4]
    d_proj = [d_qkv, dz, d_ckv, d_cq, d_kab]
    G["wp"] = _gw_pieces(h2, d_proj, "mix_gwp")
    dx1, G["s2"], G["sh2"], d_below, dg_below = _dproj_dmod(d_proj, W["wp"], x1, d_out, scale, shift, "mix_dproj", below=below)
    return dx1, d_below, dg_below, G


def _local_step(x, target, mod, cos_p, sin_p, W1, mixer_weights, ffn2_weights, ffn_grad_ready, mixer_grads_ready):
    sh1, s1, g1, sh2, s2, g2, sh3, s3, g3 = [mod[:, D_MODEL * i:D_MODEL * (i + 1)] for i in range(N_MOD)]
    x1, saved1 = _ffn_fwd(x, s1, sh1, g1, W1["f1_w8"], W1["f1_wo4"], "ffn1")
    W = mixer_weights(x1)
    x2, saved2 = _mixer_fwd(x1, s2, sh2, g2, cos_p, sin_p, W)
    W.update(ffn2_weights(x2))
    (dx3, loss_row), saved3 = _ffn_fwd(x2, s3, sh3, g3, W["f2_w8"], W["f2_wo4"], "ffn2", target=target)
    dx2, d_s3, d_sh3, d_g3, dy, d_g2 = _ffn_bwd(dx3, x2, s3, sh3, g3, W["f2_w8"], W["f2_wo4"], saved3, "ffn2",
                                                ffn_grad_ready("f2"), below=(saved2[-1], g2, 1.0))
    dx1, df1, d_g1, G = _mixer_bwd(dx2, dy, x1, s2, sh2, cos_p, sin_p, W, saved2, below=(saved1[4], g1, 0.5))
    d_sh2, d_s2 = G.pop("sh2"), G.pop("s2")
    saved1 = saved1[:5] + (df1, d_g1 + mixer_grads_ready(G))
    dx, d_s1, d_sh1, d_g1 = _ffn_bwd(dx1, x, s1, sh1, g1, W1["f1_w8"], W1["f1_wo4"], saved1, "ffn1", ffn_grad_ready("f1"))
    d_mod = jnp.concatenate([d_sh1, d_s1, d_g1, d_sh2, d_s2, d_g2, d_sh3, d_s3, d_g3], axis=1)
    return loss_row, dx, d_mod


WEIGHT_NAMES = ("w_ada", "b_ada", "ffn1_w_in", "ffn1_w_out", "w_in", "gdn_conv_w", "gdn_a_log", "gdn_dt_bias", "gdn_norm_w",
                "mla_q_norm_w", "mla_w_uq", "mla_kv_norm_w", "mla_w_ukv", "qkn_q_nope", "qkn_q_rope", "qkn_k_nope",
                "qkn_k_rope", "mla_out_norm_w", "w_out", "ffn2_w_in", "ffn2_w_out")
FFN_SHARDED = ("ffn1_w_in", "ffn1_w_out", "ffn2_w_in", "ffn2_w_out")
TRANSPOSED_ENTRY = ("ffn1_w_in", "ffn2_w_in", "w_in", "mla_w_uq")
SHEETED = (("w_in", "col"), ("gdn_conv_w", "col"), ("mla_w_uq", "col"), ("mla_w_ukv", "col"), ("w_out", "row"))
MOD_ROWS = N_MOD * D_MODEL // 128
SMALL = {"gdn_a_log": (MOD_ROWS, 1, 64, 4), "gdn_dt_bias": (MOD_ROWS + 1, 1, 64, 4), "gdn_norm_w": (MOD_ROWS + 2, 1, 0, 128),
         "mla_q_norm_w": (MOD_ROWS + 3, 3, 0, 384), "mla_kv_norm_w": (MOD_ROWS + 6, 2, 0, 256),
         "qkn_q_nope": (MOD_ROWS + 8, 1, 0, 128), "qkn_q_rope": (MOD_ROWS + 9, 1, 0, 64), "qkn_k_nope": (MOD_ROWS + 10, 1, 0, 128),
         "qkn_k_rope": (MOD_ROWS + 11, 1, 0, 64), "mla_out_norm_w": (MOD_ROWS + 12, 1, 0, 128)}
LOSS_ROW = MOD_ROWS + 13
CONV_ROW, CONV_ROWS = 88, 4 * 1536 // 128
SHEET_ROWS = CONV_ROW + CONV_ROWS


def _to_sheet(flat, dtype, sublanes):
    n = flat.shape[-1]
    unit = sublanes * 128
    pad = (-n) % unit
    flat = jnp.pad(flat.astype(dtype), [(0, 0)] * (flat.ndim - 1) + [(0, pad)])
    return flat.reshape(flat.shape[:-1] + ((n + pad) // 128, 128))


def _small_sheet(b_like, small):
    sheet = jnp.zeros((SHEET_ROWS, 128), F32).at[:MOD_ROWS].set(b_like.reshape(MOD_ROWS, 128))
    for name, (row, rows, lane, n) in SMALL.items():
        v = small[name].reshape(1, n)
        if rows == 1:
            sheet = sheet.at[row, lane:lane + n].set(v[0])
        else:
            sheet = sheet.at[row:row + rows].set(v.reshape(rows, 128))
    return sheet


def _from_small_sheet(sheet):
    out = {"b_ada": sheet[:MOD_ROWS].reshape(1, N_MOD * D_MODEL)}
    for name, (row, rows, lane, n) in SMALL.items():
        out[name] = sheet[row, lane:lane + n].reshape(1, n) if rows == 1 else sheet[row:row + rows].reshape(1, n)
    return out


def kernel(x, c, positions, w_ada, b_ada, ffn1_w_in, ffn1_w_out, w_in, gdn_conv_w, gdn_a_log, gdn_dt_bias, gdn_norm_w, mla_q_norm_w, mla_w_uq, mla_kv_norm_w, mla_w_ukv, qkn_q_nope, qkn_q_rope, qkn_k_nope, qkn_k_rope, mla_out_norm_w, w_out, ffn2_w_in, ffn2_w_out, loss_target, m_w_ada, m_b_ada, m_ffn1_w_in, m_ffn1_w_out, m_w_in, m_gdn_conv_w, m_gdn_a_log, m_gdn_dt_bias, m_gdn_norm_w, m_mla_q_norm_w, m_mla_w_uq, m_mla_kv_norm_w, m_mla_w_ukv, m_qkn_q_nope, m_qkn_q_rope, m_qkn_k_nope, m_qkn_k_rope, m_mla_out_norm_w, m_w_out, m_ffn2_w_in, m_ffn2_w_out, v_w_ada, v_b_ada, v_ffn1_w_in, v_ffn1_w_out, v_w_in, v_gdn_conv_w, v_gdn_a_log, v_gdn_dt_bias, v_gdn_norm_w, v_mla_q_norm_w, v_mla_w_uq, v_mla_kv_norm_w, v_mla_w_ukv, v_qkn_q_nope, v_qkn_q_rope, v_qkn_k_nope, v_qkn_k_rope, v_mla_out_norm_w, v_w_out, v_ffn2_w_in, v_ffn2_w_out):
    args = locals()
    w = {n: args[n] for n in WEIGHT_NAMES}
    m = {n: args["m_" + n] for n in WEIGHT_NAMES}
    v = {n: args["v_" + n] for n in WEIGHT_NAMES}
    me = 4 * lax.axis_index("x") + 2 * lax.axis_index("y") + lax.axis_index("c")
    cols = N_MOD * D_MODEL // N_DEV
    shard = {n: w[n][0] for n in FFN_SHARDED + tuple(s[0] for s in SHEETED)}

    sc = c * _sigmoid(c)
    first = _to_sheet(jnp.concatenate([sc.reshape(-1), shard["gdn_conv_w"].reshape(-1)]), F32, 8)
    (first_all,) = _all_gather([first], "gather_c")
    sc_all = first_all[:, :D_MODEL // 128].reshape(N_DEV, D_MODEL)
    n_taps = shard["gdn_conv_w"].size
    conv_all = first_all.reshape(N_DEV, -1)[:, D_MODEL:D_MODEL + n_taps].reshape(N_DEV, 4, -1)
    b_mine = lax.dynamic_slice(b_ada, (0, me * cols), (1, cols))
    mod_cols = _mm(sc_all, w_ada[0], "nn", name="ada_mod", extra_params=[b_mine], epi=lambda acc, b_: (acc + b_,))
    (mod_all,) = _all_to_all([_to_sheet(mod_cols, F32, 8)], "scatter_mod")
    mod = mod_all.reshape(N_DEV, -1)[:, :cols].reshape(1, N_MOD * D_MODEL)

    f1_shards, mod = lax.optimization_barrier(([shard["ffn1_w_in"].astype(BF16), shard["ffn1_w_out"].astype(BF16)], mod))
    f1_w8, f1_out = _all_gather(f1_shards, "gather_w1")
    travel = [s for s in SHEETED if s[0] != "gdn_conv_w"]
    tied = lax.optimization_barrier(([shard[n].astype(BF16) for n, _ in travel], f1_w8))
    f1_w8 = tied[1]
    mixer_w = _exchange_start(tied[0], False, "gather_wm_start")
    ffn2_w = _exchange_start([shard["ffn2_w_in"].astype(BF16) + mixer_w[4][0:1, 0:1].astype(BF16),
                              shard["ffn2_w_out"].astype(BF16)], False, "gather_w2_start")
    mod = mod + ffn2_w[4][0:1, 0:1]
    W1 = dict(f1_w8=f1_w8, f1_wo4=f1_out.reshape(HID_PIECES, FFN_PIECE, D_MODEL))

    def mixer_weights(after):
        got = _exchange_wait(mixer_w, False, after, "gather_wm_wait")
        P = {n: jnp.concatenate(list(g), axis=1) if kind == "col" else g.reshape(-1, g.shape[-1])
             for (n, kind), g in zip(travel, got)}
        P["gdn_conv_w"] = jnp.concatenate(list(conv_all), axis=1)
        for n in SMALL:
            P[n] = w[n]
        return _pack_weights(P)

    def ffn2_weights(after):
        f2_w8, f2_out = _exchange_wait(ffn2_w, False, after, "gather_w2_wait")
        return dict(f2_w8=f2_w8, f2_wo4=f2_out.reshape(HID_PIECES, FFN_PIECE, D_MODEL))

    pending, small_grads = {}, {}

    def ffn_grad_ready(tag):
        def ready(which, g):
            pieces = g if which == "w8" else g.reshape((N_DEV,) + shard["ffn1_w_out"].shape)
            pending[tag + which] = _exchange_start([pieces], True, "scatter_%s_%s_start" % (tag, which))
            return pending[tag + which][4]
        return ready

    def mixer_grads_ready(G):
        g_full = _unpack_grads(G)
        small_grads.update({n: g_full[n] for n in SMALL})
        small_grads["gdn_conv_w"] = g_full["gdn_conv_w"]
        pieces = []
        for n, kind in travel:
            r, cc = shard[n].shape
            g = g_full[n].astype(BF16)
            pieces.append(jnp.stack([g[:, cc * p:cc * (p + 1)] for p in range(N_DEV)]) if kind == "col"
                          else g.reshape(N_DEV, r, cc))
        pending["mixer"] = _exchange_start(pieces, True, "scatter_mx_start")
        return pending["mixer"][4][0:1, 0:1]

    cos_p, sin_p = _rope_tables(positions[0])
    loss_row, dx, d_mod = _local_step(x[0], loss_target[0], mod, cos_p, sin_p, W1, mixer_weights, ffn2_weights,
                                      ffn_grad_ready, mixer_grads_ready)

    sheet = _small_sheet(d_mod, small_grads).at[LOSS_ROW].set(loss_row[0, :128])
    sheet = sheet.at[CONV_ROW:CONV_ROW + CONV_ROWS].set(small_grads["gdn_conv_w"].reshape(CONV_ROWS, 128))
    (sheets,) = _all_gather([sheet], "gather_small")
    summed = _sum_devices(sheets, "sum_small")
    d_mod_all = sheets[:, :MOD_ROWS].reshape(N_DEV, N_MOD * D_MODEL)
    d_mod_mine = lax.dynamic_slice(d_mod_all, (0, me * cols), (N_DEV, cols))
    grads = _from_small_sheet(summed)
    grads["w_ada"] = _mm(sc_all, d_mod_mine, "tn", name="ada_gw", hi=True)
    conv_taps = shard["gdn_conv_w"].shape[1]
    grads["gdn_conv_w"] = lax.dynamic_slice(summed[CONV_ROW:CONV_ROW + CONV_ROWS].reshape(4, -1), (0, me * conv_taps),
                                            (4, conv_taps))
    loss = summed[LOSS_ROW, 0]

    delta, new_m, new_v = {}, {}, {}
    arrived = {}
    for n, key in zip(FFN_SHARDED, ("f1w8", "f1wo4", "f2w8", "f2wo4")):
        (arrived[n],) = _exchange_wait(pending[key], True, summed, "scatter_%s_wait" % key)
    arrived.update(zip([n for n, _ in travel], _exchange_wait(pending["mixer"], True, summed, "scatter_mx_wait")))
    for n, parts in arrived.items():
        if n in TRANSPOSED_ENTRY:
            res = _sum_adamw(parts, w[n][0].T, m[n][0].T, v[n][0].T, "adamw_" + n, transposed=True)
            grads[n], delta[n], new_m[n], new_v[n] = [r.T for r in res]
        else:
            grads[n], delta[n], new_m[n], new_v[n] = _sum_adamw(parts, w[n][0], m[n][0], v[n][0], "adamw_" + n)
    for n in ("w_ada", "gdn_conv_w"):
        delta[n], new_m[n], new_v[n] = _adamw(w[n][0], grads[n], m[n][0], v[n][0], "adamw_" + n)
    small_in = [_small_sheet(t["b_ada"], t) for t in (w, grads, m, v)]
    for res, out in zip(_adamw(*small_in, "adamw_small"), (delta, new_m, new_v)):
        out.update(_from_small_sheet(res))

    def shaped(d):
        return [d[n].reshape(w[n].shape) for n in WEIGHT_NAMES]

    return (loss, dx[None], *shaped(grads), *shaped(delta), *shaped(new_m), *shaped(new_v))
```

```python
import functools

import jax
import jax.numpy as jnp
import numpy as np
from jax import lax
from jax.experimental import pallas as pl
from jax.experimental.pallas import tpu as pltpu

F32 = jnp.float32
BF16 = jnp.bfloat16

D_MODEL = 1024
D_FF = 2816
N_MOD = 9
HEADS = 4
HEAD_DIM = 128
CHUNK = 64
EPS = 1e-6
ROPE = 64
Q_LORA = 384
KV_LORA = 256
N_IN = 2760
N_IN_PACKED = 2816
ROPE_BASE = 10000.0
LOG2_E = 1.4426950408889634
N_DEV = 8

ADAM_LR = 0.001
ADAM_B1 = 0.9
ADAM_B2 = 0.999
ADAM_EPS = 1e-08
ADAM_WD = 0.01
ADAM_STEP = 10

VMEM_LIMIT_BYTES = 56 * 1024 * 1024
MATMUL_ROWS = (1024, 512, 256, 128)
MESH = pl.DeviceIdType.MESH


def _params(sem=None):
    return pltpu.CompilerParams(dimension_semantics=sem, vmem_limit_bytes=VMEM_LIMIT_BYTES)


def _pick(dim, prefs):
    for p in prefs:
        if dim % p == 0:
            return p
    return dim


_DIMS = {"nn": (((1,), (0,)), ((), ())), "nt": (((1,), (1,)), ((), ())), "tn": (((0,), (0,)), ((), ()))}


def _dot_raw(a, b, mode):
    return lax.dot_general(a.astype(BF16), b.astype(BF16), _DIMS[mode], preferred_element_type=F32)


def _dot_hi(a, b, mode="nn"):
    return lax.dot_general(a, b, _DIMS[mode], precision=lax.Precision.HIGHEST, preferred_element_type=F32)


@functools.partial(jax.custom_vjp, nondiff_argnums=(2,))
def _bdot(a, b, mode):
    return _dot_raw(a, b, mode)


def _bdot_fwd(a, b, mode):
    return _dot_raw(a, b, mode), (a, b)


def _bdot_bwd(mode, res, g):
    a, b = res
    if mode == "nn":
        return _dot_raw(g, b, "nt"), _dot_raw(a, g, "tn")
    if mode == "nt":
        return _dot_raw(g, b, "nn"), _dot_raw(g, a, "tn")
    return _dot_raw(b, g, "nt"), _dot_raw(a, g, "nn")


_bdot.defvjp(_bdot_fwd, _bdot_bwd)


def _mm(a, b, mode, *, name, out_dtypes=(F32,), epi=None, extras=(), extra_params=(), hi=False,
        tm=None, tn=None, tk=None):
    if mode == "nn":
        (M, K), (_, N) = a.shape, b.shape
    elif mode == "nt":
        (M, K), (N, _) = a.shape, b.shape
    else:
        (K, M), (_, N) = a.shape, b.shape
    tm = tm or _pick(M, (512, 1408, 256, 128) if mode == "tn" else MATMUL_ROWS + (384, 352))
    tn = tn or _pick(N, (1024, 1408, 768, 512, 384, 256, 128))
    tk = tk or _pick(K, (1024, 1408, 512, 384, 256, 128))
    a_spec = {"nn": pl.BlockSpec((tm, tk), lambda i, j, k: (i, k)), "nt": pl.BlockSpec((tm, tk), lambda i, j, k: (i, k)),
              "tn": pl.BlockSpec((tk, tm), lambda i, j, k: (k, i))}[mode]
    b_spec = {"nn": pl.BlockSpec((tk, tn), lambda i, j, k: (k, j)), "nt": pl.BlockSpec((tn, tk), lambda i, j, k: (j, k)),
              "tn": pl.BlockSpec((tk, tn), lambda i, j, k: (k, j))}[mode]
    mn_spec = pl.BlockSpec((tm, tn), lambda i, j, k: (i, j))
    return _mmg(a, b, mode, name=name, grid=(M // tm, N // tn, K // tk), a_spec=a_spec, b_spec=b_spec, out_spec=mn_spec,
                out_shapes=[jax.ShapeDtypeStruct((M, N), dt) for dt in out_dtypes], acc_shape=(tm, tn), epi=epi,
                extras=list(extras) + list(extra_params),
                extra_specs=[mn_spec] * len(extras) + [pl.BlockSpec((1, tn), lambda i, j, k: (0, j))] * len(extra_params),
                hi=hi)


def _mmg(a, b, mode, *, name, grid, a_spec, b_spec, out_spec, out_shapes, acc_shape, epi=None, extras=(),
         extra_specs=(), hi=False):
    nk = grid[2]
    n_e, n_o = len(extras), len(out_shapes)

    def body(*refs):
        a_ref, b_ref = refs[:2]
        e_refs = refs[2:2 + n_e]
        o_refs = refs[2 + n_e:2 + n_e + n_o]
        if hi:
            part = _dot_hi(a_ref[...].astype(F32), b_ref[...].astype(F32), mode)
        else:
            part = _dot_raw(a_ref[...], b_ref[...], mode)

        def finish(acc):
            outs = (acc,) if epi is None else epi(acc, *[e[...].astype(F32) for e in e_refs])
            for o_ref, o in zip(o_refs, outs):
                o_ref[...] = o.astype(o_ref.dtype)

        if nk == 1:
            finish(part)
            return
        acc_ref = refs[-1]
        k = pl.program_id(2)

        @pl.when(k == 0)
        def _():
            acc_ref[...] = jnp.zeros_like(acc_ref)

        acc_ref[...] += part

        @pl.when(k == nk - 1)
        def _():
            finish(acc_ref[...])

    outs = pl.pallas_call(
        body, name=name, grid=grid,
        in_specs=[a_spec, b_spec] + list(extra_specs),
        out_specs=[out_spec] * n_o,
        out_shape=list(out_shapes),
        scratch_shapes=[pltpu.VMEM(acc_shape, F32)] if nk > 1 else [],
        compiler_params=_params(("parallel", "parallel", "arbitrary")),
    )(a, b, *extras)
    return outs if n_o > 1 else outs[0]


def _row_spec(th, cw, ci):
    return pl.BlockSpec((th, cw), lambda i: (i, ci))


def _full_spec(shape):
    return pl.BlockSpec(shape, lambda i: (0,) * len(shape))


def _rowwise(fn, rows, params, outs, n_steps, name):
    n_r, n_p, n_o = len(rows), len(params), len(outs)

    def body(*refs):
        vals = [r[...].astype(F32) for r in refs[:n_r + n_p]]
        res = fn(*vals)
        for o_ref, o in zip(refs[n_r + n_p:], res):
            o_ref[...] = o.astype(o_ref.dtype)

    across = [len(o) == 4 for o in outs]
    res = pl.pallas_call(
        body, name=name, grid=(n_steps,),
        in_specs=[_row_spec(th, cw, ci) for (_, th, cw, ci) in rows] + [_full_spec(p.shape) for p in params],
        out_specs=[pl.BlockSpec((o[0], o[1]), lambda i: (0, i)) if ac else _row_spec(o[0], o[1], 0)
                   for o, ac in zip(outs, across)],
        out_shape=[jax.ShapeDtypeStruct((o[0], n_steps * o[1]) if ac else (n_steps * o[0], o[1]), o[2])
                   for o, ac in zip(outs, across)],
        compiler_params=_params(("parallel",)),
    )(*[r[0] for r in rows], *params)
    return res


def _rowwise_bwd(fn, rows, aux, params, douts, n_steps, name, row_dtypes=None, across=None, dout_from=None):
    n_r, n_a, n_p, n_d = len(rows), len(aux), len(params), len(douts)
    row_dtypes = row_dtypes or (F32,) * n_r

    def body(*refs):
        it = iter(refs)
        r_vals = [next(it)[...].astype(F32) for _ in range(n_r)]
        a_vals = [next(it)[...].astype(F32) for _ in range(n_a)]
        p_vals = [next(it)[...].astype(F32) for _ in range(n_p)]
        d_vals = [next(it)[...].astype(F32) for _ in range(n_d)]
        across_in = next(it) if across is not None else None
        if dout_from is not None:
            d_vals = [_dot_raw(next(it)[...], next(it)[...], "nt")]
        dr_refs = [next(it) for _ in range(n_r)]
        dp_refs = [next(it) for _ in range(n_p)]

        def f(*rp):
            return tuple(fn(*rp[:n_r], *a_vals, *rp[n_r:]))

        _, vjp = jax.vjp(f, *r_vals, *p_vals)
        grads = list(vjp(tuple(d_vals)))
        for dr_ref, g in zip(dr_refs, grads[:n_r]):
            dr_ref[...] = g.astype(dr_ref.dtype)
        if across is not None:
            next(it)[...] = across[1](r_vals, grads[:n_r], across_in[...])

        @pl.when(pl.program_id(0) == 0)
        def _():
            for dp_ref in dp_refs:
                dp_ref[...] = jnp.zeros_like(dp_ref)

        for dp_ref, g in zip(dp_refs, grads[n_r:]):
            dp_ref[...] += g

    all_rows = list(rows) + list(aux) + list(douts)
    in_specs = ([_row_spec(th, cw, ci) for (_, th, cw, ci) in list(rows) + list(aux)]
                + [_full_spec(p.shape) for p in params]
                + [_row_spec(th, cw, ci) for (_, th, cw, ci) in all_rows[n_r + n_a:]])
    across_specs = [] if across is None else [pl.BlockSpec((8, rows[0][1]), lambda i: (0, i))]
    from_specs = [] if dout_from is None else [_row_spec(rows[0][1], dout_from[0].shape[1], 0), _full_spec(dout_from[1].shape)]
    res = pl.pallas_call(
        body, name=name, grid=(n_steps,),
        in_specs=in_specs + across_specs + from_specs,
        out_specs=[_row_spec(th, cw, 0) for (_, th, cw, _) in rows] + [_full_spec(p.shape) for p in params] + across_specs,
        out_shape=[jax.ShapeDtypeStruct((n_steps * th, cw), dt) for (_, th, cw, _), dt in zip(rows, row_dtypes)]
        + [jax.ShapeDtypeStruct(p.shape, F32) for p in params]
        + [jax.ShapeDtypeStruct(across[0].shape, F32) for _ in across_specs],
        compiler_params=_params(("arbitrary",)),
    )(*[r[0] for r in list(rows) + list(aux)], *params, *[r[0] for r in all_rows[n_r + n_a:]],
      *[across[0] for _ in across_specs], *(dout_from or ()))
    return res[:n_r], res[n_r:]


def _sigmoid(x):
    return lax.logistic(x)


def _silu(x):
    return x * _sigmoid(x)


def _rms(x, w=None, n=None):
    n = n or x.shape[-1]
    y = x * lax.rsqrt(jnp.sum(x * x, axis=-1, keepdims=True) * (1.0 / n) + EPS)
    return y if w is None else y * w


def _modulate(x, scale, shift):
    return _rms(x) * (1.0 + scale) + shift


def _softplus(x):
    return jnp.maximum(x, 0.0) + jnp.log1p(jnp.exp(-jnp.abs(x)))


@jax.custom_vjp
def _rot_half64(x):
    lane = lax.broadcasted_iota(jnp.int32, x.shape, 1)
    up = pltpu.roll(x, 96, 1)
    down = pltpu.roll(x, 32, 1)
    return jnp.where(lane < 32, up, jnp.where(lane < 64, down, 0.0))


_rot_half64.defvjp(lambda x: (_rot_half64(x), None), lambda _, g: (_rot_half64(g),))


def _rope128(x, cos_p, sin_p):
    return x * cos_p + _rot_half64(x) * sin_p


def _gdn_pre_fn(qkvc, kab, alog_p, dt_p):
    a = _silu(qkvc)
    qs, ks = [], []
    for h in range(HEADS):
        qh = a[:, HEAD_DIM * h:HEAD_DIM * (h + 1)]
        kh = a[:, 512 + HEAD_DIM * h:512 + HEAD_DIM * (h + 1)]
        qs.append(qh * lax.rsqrt(jnp.sum(qh * qh, axis=-1, keepdims=True) + EPS) * (HEAD_DIM ** -0.5))
        ks.append(kh * lax.rsqrt(jnp.sum(kh * kh, axis=-1, keepdims=True) + EPS))
    lane = lax.broadcasted_iota(jnp.int32, kab.shape, 1)
    g_full = -jnp.exp(alog_p) * _softplus(kab + dt_p)
    b_full = _sigmoid(kab)
    gb = jnp.where((lane >= 64) & (lane < 68), g_full, jnp.where((lane >= 68) & (lane < 72), b_full, 0.0))
    return jnp.concatenate(qs, axis=1), jnp.concatenate(ks, axis=1), a[:, 1024:1536], gb


INTRA_ROWS = (512, 256, 128, 64)
TOKEN_ROWS = (512, 256, 128)

_BNN = (((2,), (1,)), ((0,), (0,)))
_BNT = (((2,), (2,)), ((0,), (0,)))
_BTN = (((1,), (1,)), ((0,), (0,)))


def _split_bf16(a):
    hi = a.astype(BF16)
    return hi, (a - hi.astype(F32)).astype(BF16)


def _dot3_raw(a, b, dims):
    a_hi, a_lo = _split_bf16(a)
    b_hi, b_lo = _split_bf16(b)
    dot = lambda x_, y_: lax.dot_general(x_, y_, dims, preferred_element_type=F32)
    return dot(a_hi, b_hi) + (dot(a_hi, b_lo) + dot(a_lo, b_hi))


@functools.partial(jax.custom_vjp, nondiff_argnums=(2, 3))
def _dot3(a, b, nt, exact_bwd=True):
    return _dot3_raw(a, b, _BNT if nt else _BNN)


def _dot3_fwd(a, b, nt, exact_bwd):
    return _dot3_raw(a, b, _BNT if nt else _BNN), (a, b)


def _dot3_bwd(nt, exact_bwd, res, g):
    a, b = res
    if exact_bwd:
        dot = _dot3_raw
    else:
        dot = lambda x_, y_, d_: lax.dot_general(x_.astype(BF16), y_.astype(BF16), d_, preferred_element_type=F32)
    if nt:
        return dot(g, b, _BNN), dot(jnp.swapaxes(g, 1, 2), a, _BNN)
    return dot(g, b, _BNT), dot(jnp.swapaxes(a, 1, 2), g, _BNN)


_dot3.defvjp(_dot3_fwd, _dot3_bwd)


@functools.partial(jax.custom_vjp, nondiff_argnums=(2,))
def _bdot_b(a, b, nt):
    return lax.dot_general(a.astype(BF16), b.astype(BF16), _BNT if nt else _BNN, preferred_element_type=F32)


def _bdot_b_fwd(a, b, nt):
    return _bdot_b(a, b, nt), (a, b)


def _bdot_b_bwd(nt, res, g):
    a, b = res
    dot = lambda x_, y_, d_: lax.dot_general(x_.astype(BF16), y_.astype(BF16), d_, preferred_element_type=F32)
    if nt:
        return dot(g, b, _BNN), dot(jnp.swapaxes(g, 1, 2), a, _BNN)
    return dot(g, b, _BNT), dot(jnp.swapaxes(a, 1, 2), g, _BNN)


_bdot_b.defvjp(_bdot_b_fwd, _bdot_b_bwd)


@jax.custom_vjp
def _inverse_given(a_mat, inv):
    return inv


def _inverse_given_bwd(inv, g):
    inv_t = jnp.swapaxes(inv, 1, 2)
    return -_dot3_raw(_dot3_raw(inv_t, g, _BNN), inv_t, _BNN), jnp.zeros_like(inv)


_inverse_given.defvjp(lambda a_mat, inv: (inv, inv), _inverse_given_bwd)


def _intra_batched(q, k, v, g_col, b_col, inv_known=None):
    c = CHUNK
    nb = q.shape[0]
    row = lax.broadcasted_iota(jnp.int32, (1, c, c), 1)
    col = lax.broadcasted_iota(jnp.int32, (1, c, c), 2)
    incl, strict, eye = row >= col, row > col, row == col
    tri = jnp.broadcast_to(jnp.where(incl, 1.0, 0.0).astype(F32), (nb, c, c))
    ident = jnp.where(eye, 1.0, 0.0).astype(F32)
    g_wide = _dot3(tri, jnp.broadcast_to(g_col, (nb, c, HEAD_DIM)), False)
    g_i = g_wide[:, :, :c]
    g_j = jnp.sum(jnp.where(eye, g_i, 0.0), axis=1, keepdims=True)
    decay = jnp.where(incl, jnp.exp(jnp.where(incl, g_i - g_j, 0.0)), 0.0)
    kk = _bdot_b(k, k, True)
    a_mat = jnp.where(strict, b_col * kk * decay, 0.0)
    if inv_known is None:
        x_pow = -a_mat
        inv = ident + x_pow
        for _ in range(5):
            x_pow = _dot3(x_pow, x_pow, False, False)
            inv = inv + _dot3(inv, x_pow, False, False)
    else:
        inv = _inverse_given(a_mat, inv_known)
    e_wide = jnp.exp(g_wide)
    u = _dot3(inv, v * b_col, False)
    wk = _dot3(inv, k * b_col * e_wide, False)
    qk = _bdot_b(q, k, True) * decay
    last = lax.broadcasted_iota(jnp.int32, (1, c, HEAD_DIM), 1) == c - 1
    g_last = jnp.sum(jnp.where(last, g_wide, 0.0), axis=1, keepdims=True)
    qd = q * e_wide
    kd = k * jnp.exp(g_last - g_wide)
    gl = jnp.broadcast_to(jnp.exp(g_last), (nb, 8, HEAD_DIM))
    return u, wk, qd, kd, qk, gl, inv


def _gdn_intra_fn(q, k, v, gb, *inv_known):
    t = q.shape[0]
    nch = t // CHUNK
    lane = lax.broadcasted_iota(jnp.int32, gb.shape, 1)

    def heads_first(x_):
        return jnp.concatenate([x_[:, HEAD_DIM * h:HEAD_DIM * (h + 1)].reshape(nch, CHUNK, HEAD_DIM) for h in range(HEADS)],
                               axis=0)

    def column(first_lane):
        return jnp.concatenate([jnp.sum(jnp.where(lane == first_lane + h, gb, 0.0), axis=1, keepdims=True)
                                .reshape(nch, CHUNK, 1) for h in range(HEADS)], axis=0)

    known = jnp.concatenate([x_.reshape(nch, CHUNK, CHUNK) for x_ in inv_known], axis=0) if inv_known else None
    u, wk, qd, kd, qk, gl, inv = _intra_batched(heads_first(q), heads_first(k), heads_first(v), column(64), column(68), known)

    def rows_first(x_):
        r, w_ = x_.shape[1], x_.shape[2]
        return jnp.concatenate([x_[nch * h:nch * (h + 1)].reshape(nch * r, w_) for h in range(HEADS)], axis=1)

    per_head = lambda x_: [x_[nch * h:nch * (h + 1)].reshape(t, CHUNK) for h in range(HEADS)]
    outs = (rows_first(u), rows_first(wk), rows_first(qd), rows_first(kd), *per_head(qk), rows_first(gl))
    return outs if inv_known else outs + tuple(per_head(inv))


def _scan_step(s0, u, wk, qd, kd, qk, gl):
    v_new = u - _bdot_b(wk, s0, False)
    o = _bdot_b(qd, s0, False) + _bdot_b(qk, v_new, False)
    s1 = s0 * gl[:, 0:1, :] + _bdot_b(jnp.swapaxes(kd, 1, 2), v_new, False)
    return o, s1


def _mix_post_fn(o_a, z, o_b, gnw, onw):
    parts = [_rms(o_a[:, HEAD_DIM * h:HEAD_DIM * (h + 1)], gnw) * _silu(z[:, HEAD_DIM * h:HEAD_DIM * (h + 1)])
             for h in range(HEADS)]
    parts += [_rms(o_b[:, HEAD_DIM * h:HEAD_DIM * (h + 1)], onw) for h in range(HEADS)]
    return (jnp.concatenate(parts, axis=1),)


def _mla_pre_fn(ckv, cq, kab, cos_p, sin_p, qnw, kvnw, wuq, wukv, qn_w, qr_w, kn_w, kr_w):
    scale = (HEAD_DIM + ROPE) ** -0.5 * LOG2_E
    qf = _bdot(_rms(cq, qnw), wuq, "nn")
    kvf = _bdot(_rms(ckv, kvnw), wukv, "nn")
    lane = lax.broadcasted_iota(jnp.int32, kab.shape, 1)
    kr = _rope128(_rms(jnp.where(lane < ROPE, kab, 0.0), kr_w, n=ROPE), cos_p, sin_p)
    qs, ks = [], []
    for h in range(HEADS):
        qn = _rms(qf[:, 256 * h:256 * h + 128], qn_w) * scale
        qr = _rope128(_rms(qf[:, 256 * h + 128:256 * h + 256], qr_w, n=ROPE), cos_p, sin_p) * scale
        qs += [qn, qr]
        ks += [_rms(kvf[:, 128 * h:128 * (h + 1)], kn_w), kr]
    return jnp.concatenate(qs, axis=1), jnp.concatenate(ks, axis=1), kvf[:, 512:]


def _conv_fwd(proj, conv_w, alog_p, dt_p, tm, name):
    S = proj.shape[0]
    C = 1536
    nb = tm // 8

    def body(x_ref, prev_ref, kab_ref, w_ref, al_ref, dt_ref, o_ref, q_ref, k_ref, v_ref, gb_ref, ext_ref):
        i = pl.program_id(0)
        ext_ref[0:8, :] = jnp.where(i > 0, prev_ref[...], 0.0)
        ext_ref[8:, :] = x_ref[...]
        acc = jnp.zeros((tm, C), F32)
        for k in range(4):
            acc = acc + w_ref[k:k + 1, :] * ext_ref[pl.ds(5 + k, tm), :]
        o_ref[...] = acc
        for ref, val in zip((q_ref, k_ref, v_ref, gb_ref), _gdn_pre_fn(acc, kab_ref[...], al_ref[...], dt_ref[...])):
            ref[...] = val

    third = pl.BlockSpec((tm, 512), lambda i: (i, 0))
    return pl.pallas_call(
        body, name=name, grid=(S // tm,),
        in_specs=[pl.BlockSpec((tm, C), lambda i: (i, 0)),
                  pl.BlockSpec((8, C), lambda i: (jnp.maximum(i * nb - 1, 0), 0)),
                  _row_spec(tm, 128, 21), pl.BlockSpec((4, C), lambda i: (0, 0)), _full_spec(alog_p.shape), _full_spec(dt_p.shape)],
        out_specs=[pl.BlockSpec((tm, C), lambda i: (i, 0)), third, third, third, _row_spec(tm, 128, 0)],
        out_shape=[jax.ShapeDtypeStruct((S, C), F32)] + [jax.ShapeDtypeStruct((S, 512), F32)] * 3
        + [jax.ShapeDtypeStruct((S, 128), F32)],
        scratch_shapes=[pltpu.VMEM((tm + 8, C), F32)],
        compiler_params=_params(("arbitrary",)),
    )(proj, proj, proj, conv_w, alog_p, dt_p)


def _conv_bwd(proj, qkvc, conv_w, alog_p, dt_p, douts, d_kab_add, tm, name):
    S = proj.shape[0]
    C = 1536
    nb = tm // 8
    n_steps = S // tm

    def body(x_ref, prev_ref, c_ref, kab_ref, w_ref, al_ref, dt_ref, dq_ref, dk_ref, dv_ref, dgb_ref, add_ref,
             dx_ref, dkab_ref, dw_ref, dal_ref, ddt_ref, xext_ref, dext_ref, halo_ref):
        step = pl.program_id(0)

        @pl.when(step == 0)
        def _():
            halo_ref[...] = jnp.zeros_like(halo_ref)
            for r in (dw_ref, dal_ref, ddt_ref):
                r[...] = jnp.zeros_like(r)

        _, vjp = jax.vjp(_gdn_pre_fn, c_ref[...], kab_ref[...], al_ref[...], dt_ref[...])
        d, d_kab, d_al, d_dt = vjp((dq_ref[...], dk_ref[...], dv_ref[...], dgb_ref[...]))
        dkab_ref[...] = (d_kab + add_ref[...]).astype(dkab_ref.dtype)
        xext_ref[0:8, :] = jnp.where(step < n_steps - 1, prev_ref[...], 0.0)
        xext_ref[8:, :] = x_ref[...]
        dext_ref[0:tm, :] = d
        dext_ref[tm:, :] = halo_ref[...]
        halo_ref[...] = d[0:8, :]
        acc = jnp.zeros((tm, C), F32)
        dws = []
        for k in range(4):
            acc = acc + w_ref[k:k + 1, :] * dext_ref[pl.ds(3 - k, tm), :]
            dws.append(jnp.sum(d * xext_ref[pl.ds(5 + k, tm), :], axis=0, keepdims=True))
        dx_ref[...] = acc.astype(dx_ref.dtype)
        dw_ref[...] += jnp.concatenate(dws + [jnp.zeros((4, C), F32)], axis=0)
        dal_ref[...] += d_al
        ddt_ref[...] += d_dt

    tile = lambda cols, block=0: pl.BlockSpec((tm, cols), lambda s_: (n_steps - 1 - s_, block))
    return pl.pallas_call(
        body, name=name, grid=(n_steps,),
        in_specs=[tile(C), pl.BlockSpec((8, C), lambda s_: (jnp.maximum((n_steps - 1 - s_) * nb - 1, 0), 0)),
                  tile(C), tile(128, 21), pl.BlockSpec((4, C), lambda s_: (0, 0)), _full_spec(alog_p.shape), _full_spec(dt_p.shape),
                  tile(512), tile(512), tile(512), tile(128), tile(128)],
        out_specs=[tile(C), tile(128), pl.BlockSpec((8, C), lambda s_: (0, 0)), _full_spec(alog_p.shape), _full_spec(dt_p.shape)],
        out_shape=[jax.ShapeDtypeStruct((S, C), BF16), jax.ShapeDtypeStruct((S, 128), BF16), jax.ShapeDtypeStruct((8, C), F32),
                   jax.ShapeDtypeStruct(alog_p.shape, F32), jax.ShapeDtypeStruct(dt_p.shape, F32)],
        scratch_shapes=[pltpu.VMEM((tm + 8, C), F32), pltpu.VMEM((tm + 8, C), F32), pltpu.VMEM((8, C), F32)],
        compiler_params=_params(("arbitrary",)),
    )(proj, proj, qkvc, proj, conv_w, alog_p, dt_p, *douts, d_kab_add)


SCAN_CHUNKS = (8, 4, 2, 1)


def _gdn_scan_fwd(u, wk, qd, kd, qks, gl, name):
    S = u.shape[0]
    nc = S // CHUNK
    cs = _pick(nc, SCAN_CHUNKS)
    W = HEADS * HEAD_DIM

    def body(u_ref, wk_ref, qd_ref, kd_ref, qk0, qk1, qk2, qk3, gl_ref, o_ref, sp_ref, s_ref):
        @pl.when(pl.program_id(0) == 0)
        def _():
            s_ref[...] = jnp.zeros_like(s_ref)

        state = s_ref[...]
        for c in range(cs):
            rows, gl_rows = slice(CHUNK * c, CHUNK * (c + 1)), slice(8 * c, 8 * (c + 1))
            sp_ref[c] = state
            o, state = _scan_step(state, _heads(u_ref, HEAD_DIM, rows), _heads(wk_ref, HEAD_DIM, rows),
                                  _heads(qd_ref, HEAD_DIM, rows), _heads(kd_ref, HEAD_DIM, rows),
                                  jnp.stack([r[rows, :] for r in (qk0, qk1, qk2, qk3)]), _heads(gl_ref, HEAD_DIM, gl_rows))
            for h in range(HEADS):
                o_ref[rows, HEAD_DIM * h:HEAD_DIM * (h + 1)] = o[h]
        s_ref[...] = state

    row = pl.BlockSpec((cs * CHUNK, W), lambda n: (n, 0))
    qk_spec = pl.BlockSpec((cs * CHUNK, CHUNK), lambda n: (n, 0))
    return pl.pallas_call(
        body, name=name, grid=(nc // cs,),
        in_specs=[row, row, row, row, qk_spec, qk_spec, qk_spec, qk_spec, pl.BlockSpec((cs * 8, W), lambda n: (n, 0))],
        out_specs=[row, pl.BlockSpec((cs, HEADS, HEAD_DIM, HEAD_DIM), lambda n: (n, 0, 0, 0))],
        out_shape=[jax.ShapeDtypeStruct((S, W), F32), jax.ShapeDtypeStruct((nc, HEADS, HEAD_DIM, HEAD_DIM), F32)],
        scratch_shapes=[pltpu.VMEM((HEADS, HEAD_DIM, HEAD_DIM), F32)],
        compiler_params=_params(("arbitrary",)),
    )(u, wk, qd, kd, *qks, gl)


def _gdn_scan_bwd(u, wk, qd, kd, qks, gl, s_prev, d_o, name):
    S = u.shape[0]
    nc = S // CHUNK
    cs = _pick(nc, SCAN_CHUNKS)
    nb = nc // cs
    W = HEADS * HEAD_DIM

    def body(u_ref, wk_ref, qd_ref, kd_ref, qk0, qk1, qk2, qk3, gl_ref, sp_ref, do_ref,
             du_ref, dwk_ref, dqd_ref, dkd_ref, dqk0, dqk1, dqk2, dqk3, dgl_ref, ds_ref):
        @pl.when(pl.program_id(0) == 0)
        def _():
            ds_ref[...] = jnp.zeros_like(ds_ref)

        d_state = ds_ref[...]
        for c in reversed(range(cs)):
            rows, gl_rows = slice(CHUNK * c, CHUNK * (c + 1)), slice(8 * c, 8 * (c + 1))
            _, vjp = jax.vjp(_scan_step, sp_ref[c], _heads(u_ref, HEAD_DIM, rows), _heads(wk_ref, HEAD_DIM, rows),
                             _heads(qd_ref, HEAD_DIM, rows), _heads(kd_ref, HEAD_DIM, rows),
                             jnp.stack([r[rows, :] for r in (qk0, qk1, qk2, qk3)]), _heads(gl_ref, HEAD_DIM, gl_rows))
            d_state, du, dwk, dqd, dkd, dqk, dgl = vjp((_heads(do_ref, HEAD_DIM, rows), d_state))
            for h, dqk_ref in enumerate((dqk0, dqk1, dqk2, dqk3)):
                sl = slice(HEAD_DIM * h, HEAD_DIM * (h + 1))
                du_ref[rows, sl] = du[h]
                dwk_ref[rows, sl] = dwk[h]
                dqd_ref[rows, sl] = dqd[h]
                dkd_ref[rows, sl] = dkd[h]
                dqk_ref[rows, :] = dqk[h]
                dgl_ref[gl_rows, sl] = dgl[h]
        ds_ref[...] = d_state

    rev = lambda n: (nb - 1 - n, 0)
    row = pl.BlockSpec((cs * CHUNK, W), rev)
    qk_spec = pl.BlockSpec((cs * CHUNK, CHUNK), rev)
    gl_spec = pl.BlockSpec((cs * 8, W), rev)
    qk_shape = jax.ShapeDtypeStruct((S, CHUNK), F32)
    row_shape = jax.ShapeDtypeStruct((S, W), F32)
    return pl.pallas_call(
        body, name=name, grid=(nb,),
        in_specs=[row, row, row, row, qk_spec, qk_spec, qk_spec, qk_spec, gl_spec,
                  pl.BlockSpec((cs, HEADS, HEAD_DIM, HEAD_DIM), lambda n: (nb - 1 - n, 0, 0, 0)), row],
        out_specs=[row, row, row, row, qk_spec, qk_spec, qk_spec, qk_spec, gl_spec],
        out_shape=[row_shape] * 4 + [qk_shape] * 4 + [jax.ShapeDtypeStruct((nc * 8, W), F32)],
        scratch_shapes=[pltpu.VMEM((HEADS, HEAD_DIM, HEAD_DIM), F32)],
        compiler_params=_params(("arbitrary",)),
    )(u, wk, qd, kd, *qks, gl, s_prev, d_o)


NEG = -1e30


def _chunk_mask(i, j, t, transposed=False):
    q_axis, k_axis = (1, 0) if transposed else (0, 1)
    r = (i * t + lax.broadcasted_iota(jnp.int32, (t, t), q_axis)) // CHUNK
    c = (j * t + lax.broadcasted_iota(jnp.int32, (t, t), k_axis)) // CHUNK
    return c <= r


def _tile_pairs(n, by_key):
    pairs = [(i, j) for j in range(n) for i in range(j, n)] if by_key else [(i, j) for i in range(n) for j in range(i + 1)]
    return jnp.asarray(np.array([p[0] for p in pairs], np.int32)), jnp.asarray(np.array([p[1] for p in pairs], np.int32))


def _heads(ref, width, rows=slice(None)):
    return jnp.stack([ref[rows, width * h:width * (h + 1)] for h in range(HEADS)])


def _bmm(a, b, dims):
    return lax.dot_general(a.astype(BF16), b.astype(BF16), dims, preferred_element_type=F32)


def _attn_fwd(q, k, v_t, t, name):
    S = q.shape[0]
    n = S // t
    qi, kj = _tile_pairs(n, by_key=False)

    def body(qi_ref, kj_ref, q_ref, k_ref, vt_ref, o_ref, lse_ref, m_ref, l_ref, acc_ref):
        i, j = qi_ref[pl.program_id(0)], kj_ref[pl.program_id(0)]

        @pl.when(j == 0)
        def _():
            m_ref[...] = jnp.full_like(m_ref, NEG)
            l_ref[...] = jnp.zeros_like(l_ref)
            acc_ref[...] = jnp.zeros_like(acc_ref)

        def update(masked):
            s_t = _bmm(_heads(k_ref, 256), _heads(q_ref, 256), _BNT)
            if masked:
                s_t = jnp.where(_chunk_mask(i, j, t, transposed=True)[None], s_t, NEG)
            m_old = m_ref[...]
            m_new = jnp.maximum(m_old, jnp.max(s_t, axis=1, keepdims=True))
            p_t = jnp.exp2(s_t - m_new)
            alpha = jnp.exp2(m_old - m_new)
            l_ref[...] = alpha * l_ref[...] + jnp.sum(p_t, axis=1, keepdims=True)
            v_heads = jnp.stack([vt_ref[HEAD_DIM * h:HEAD_DIM * (h + 1), :] for h in range(HEADS)])
            acc_ref[...] = alpha * acc_ref[...] + _bmm(v_heads, p_t, _BNN)
            m_ref[...] = m_new

        @pl.when(j < i)
        def _():
            update(False)

        @pl.when(j == i)
        def _():
            update(True)
            for h in range(HEADS):
                sl = slice(HEAD_DIM * h, HEAD_DIM * (h + 1))
                o_ref[:, sl] = jnp.transpose(acc_ref[h] / l_ref[h])
                lse_ref[h:h + 1, :] = m_ref[h] + jnp.log(l_ref[h]) * LOG2_E
            lse_ref[HEADS:, :] = jnp.zeros((8 - HEADS, t), F32)

    row = lambda p, qi_, kj_: (qi_[p], 0)
    return pl.pallas_call(
        body, name=name,
        grid_spec=pltpu.PrefetchScalarGridSpec(
            num_scalar_prefetch=2, grid=(qi.shape[0],),
            in_specs=[pl.BlockSpec((t, HEADS * 256), row), pl.BlockSpec((t, HEADS * 256), lambda p, qi_, kj_: (kj_[p], 0)),
                      pl.BlockSpec((HEADS * HEAD_DIM, t), lambda p, qi_, kj_: (0, kj_[p]))],
            out_specs=[pl.BlockSpec((t, HEADS * HEAD_DIM), row), pl.BlockSpec((8, t), lambda p, qi_, kj_: (0, qi_[p]))],
            scratch_shapes=[pltpu.VMEM((HEADS, 1, t), F32), pltpu.VMEM((HEADS, 1, t), F32),
                            pltpu.VMEM((HEADS, HEAD_DIM, t), F32)]),
        out_shape=[jax.ShapeDtypeStruct((S, HEADS * HEAD_DIM), F32), jax.ShapeDtypeStruct((8, S), F32)],
        compiler_params=_params(("arbitrary",)),
    )(qi, kj, q, k, v_t)


def _attn_stats(o, d_o, lse):
    lane = lax.broadcasted_iota(jnp.int32, (o.shape[0], HEAD_DIM), 1)
    stats = jnp.zeros((o.shape[0], HEAD_DIM), F32)
    for h in range(HEADS):
        sl = slice(HEAD_DIM * h, HEAD_DIM * (h + 1))
        delta = jnp.sum(d_o[:, sl] * o[:, sl], axis=1, keepdims=True)
        stats = stats + jnp.where(lane == HEADS + h, delta, 0.0)
    return lse + jnp.transpose(stats)[0:8, :]


BWD_GROUP = 2


def _attn_bwd(q, k, v, d_o, stats, t, name):
    S = q.shape[0]
    n = S // t
    groups = HEADS // BWD_GROUP
    gq, gv = BWD_GROUP * 256, BWD_GROUP * HEAD_DIM
    qi, kj = _tile_pairs(n, by_key=True)
    n_pairs = qi.shape[0]
    st = stats.reshape(2, groups, BWD_GROUP, S).transpose(1, 0, 2, 3).reshape(groups, 2 * BWD_GROUP, S)
    st = jnp.pad(st, ((0, 0), (0, 8 - 2 * BWD_GROUP), (0, 0)))

    def heads(ref, width, rows=slice(None)):
        return jnp.stack([ref[rows, width * h:width * (h + 1)] for h in range(BWD_GROUP)])

    def body(qi_ref, kj_ref, q_ref, k_ref, v_ref, do_ref, st_ref, dq_hbm, dk_ref, dv_ref, dq_acc, sem):
        g, p = pl.program_id(0), pl.program_id(1)
        i, j = qi_ref[p], kj_ref[p]

        @pl.when(i == j)
        def _():
            dk_ref[...] = jnp.zeros_like(dk_ref)
            dv_ref[...] = jnp.zeros_like(dv_ref)

        def update(masked):
            qh, kh = heads(q_ref, 256), heads(k_ref, 256)
            d_out = heads(do_ref, HEAD_DIM)
            stv = st_ref[...]
            lse_row = jnp.stack([stv[h:h + 1, :] for h in range(BWD_GROUP)])
            delta_row = jnp.stack([stv[BWD_GROUP + h:BWD_GROUP + h + 1, :] for h in range(BWD_GROUP)])
            s_t = _bmm(kh, qh, _BNT)
            p_t = jnp.exp2(s_t - lse_row)
            if masked:
                p_t = jnp.where(_chunk_mask(i, j, t, transposed=True)[None], p_t, 0.0)
            dv = _bmm(p_t, d_out, _BNN)
            dp_t = _bmm(heads(v_ref, HEAD_DIM), d_out, _BNT)
            ds_t = p_t * (dp_t - delta_row)
            dk = _bmm(ds_t, qh, _BNN)
            dq = _bmm(ds_t, kh, _BTN)
            rows = pl.ds(pl.multiple_of(i * t, t), t)
            for h in range(BWD_GROUP):
                dk_ref[:, 256 * h:256 * (h + 1)] += dk[h]
                dv_ref[:, HEAD_DIM * h:HEAD_DIM * (h + 1)] += dv[h]

            @pl.when(j == 0)
            def _():
                for h in range(BWD_GROUP):
                    dq_acc[rows, 256 * h:256 * (h + 1)] = dq[h]

            @pl.when(j > 0)
            def _():
                for h in range(BWD_GROUP):
                    dq_acc[rows, 256 * h:256 * (h + 1)] += dq[h]

        @pl.when(i == j)
        def _():
            update(True)

        @pl.when(i > j)
        def _():
            update(False)

        @pl.when(i == n - 1)
        def _():
            dk_ref[...] *= 1.0 / LOG2_E

        @pl.when(p == n_pairs - 1)
        def _():
            dq_acc[...] *= 1.0 / LOG2_E
            for gg in range(groups):
                @pl.when(g == gg)
                def _():
                    cp = pltpu.make_async_copy(dq_acc, dq_hbm.at[:, gq * gg:gq * (gg + 1)], sem)
                    cp.start()
                    cp.wait()

    q_blk = lambda g, p, qi_, kj_: (qi_[p], g)
    k_blk = lambda g, p, qi_, kj_: (kj_[p], g)
    return pl.pallas_call(
        body, name=name,
        grid_spec=pltpu.PrefetchScalarGridSpec(
            num_scalar_prefetch=2, grid=(groups, n_pairs),
            in_specs=[pl.BlockSpec((t, gq), q_blk), pl.BlockSpec((t, gq), k_blk), pl.BlockSpec((t, gv), k_blk),
                      pl.BlockSpec((t, gv), q_blk), pl.BlockSpec((None, 8, t), lambda g, p, qi_, kj_: (g, 0, qi_[p]))],
            out_specs=[pl.BlockSpec(memory_space=pl.ANY), pl.BlockSpec((t, gq), k_blk), pl.BlockSpec((t, gv), k_blk)],
            scratch_shapes=[pltpu.VMEM((S, gq), F32), pltpu.SemaphoreType.DMA]),
        out_shape=[jax.ShapeDtypeStruct((S, HEADS * 256), F32), jax.ShapeDtypeStruct((S, HEADS * 256), F32),
                   jax.ShapeDtypeStruct((S, HEADS * HEAD_DIM), F32)],
        compiler_params=_params(("arbitrary", "arbitrary")),
    )(qi, kj, q, k, v, d_o, st)


FFN_PIECE = 2 * D_FF // N_DEV
HID_PIECES = D_FF // FFN_PIECE


def _after_specs(after):
    return [] if after is None else [pl.BlockSpec(memory_space=pl.ANY)]


def _after_args(after):
    return [] if after is None else [after]


def _ffn_gw8(h, d_gate, d_up, name, after=None):
    S = h.shape[0]
    tm = 512
    tk = _pick(S, MATMUL_ROWS)
    nk = S // tk

    def body(h_ref, dg_ref, du_ref, *rest):
        o_ref, acc_ref = rest[-2:]
        k = pl.program_id(1)

        @pl.when(k == 0)
        def _():
            acc_ref[...] = jnp.zeros_like(acc_ref)

        h_t = jnp.transpose(h_ref[...])
        for p in range(HID_PIECES):
            acc_ref[p] += _dot_raw(h_t, dg_ref[p], "nn")
            acc_ref[HID_PIECES + p] += _dot_raw(h_t, du_ref[p], "nn")

        @pl.when(k == nk - 1)
        def _():
            o_ref[...] = acc_ref[...].astype(o_ref.dtype)

    d_spec = pl.BlockSpec((HID_PIECES, tk, FFN_PIECE), lambda i, k: (0, k, 0))
    return pl.pallas_call(
        body, name=name, grid=(D_MODEL // tm, nk),
        in_specs=[pl.BlockSpec((tk, tm), lambda i, k: (k, i)), d_spec, d_spec] + _after_specs(after),
        out_specs=pl.BlockSpec((2 * HID_PIECES, tm, FFN_PIECE), lambda i, k: (0, i, 0)),
        out_shape=jax.ShapeDtypeStruct((2 * HID_PIECES, D_MODEL, FFN_PIECE), BF16),
        scratch_shapes=[pltpu.VMEM((2 * HID_PIECES, tm, FFN_PIECE), F32)],
        compiler_params=_params(("parallel", "arbitrary")),
    )(h, d_gate, d_up, *_after_args(after))


EPILOGUE_ROWS = 256


def _dmod_epilogue(acc_ref, x_ref, do_ref, sc_ref, sh_ref, dx_ref, dsc_ref, dsh_ref, below, below_in, below_out):
    rows_total = acc_ref.shape[0]
    step = min(EPILOGUE_ROWS, rows_total)
    dsc, dsh, dg = 0.0, 0.0, 0.0
    for r in range(rows_total // step):
        rows = slice(step * r, step * (r + 1))
        _, vjp = jax.vjp(_modulate, x_ref[rows, :], sc_ref[...], sh_ref[...])
        dx, dsc_r, dsh_r = vjp(acc_ref[rows, :])
        dx = dx + do_ref[rows, :]
        dx_ref[rows, :] = dx
        dsc, dsh = dsc + dsc_r, dsh + dsh_r
        if below is not None:
            coef = below[2]
            below_out[0][rows, :] = (coef * below_in[1][...] * dx).astype(below_out[0].dtype)
            dg = dg + jnp.sum(coef * below_in[0][rows, :] * dx, axis=0, keepdims=True)
    dsc_ref[...] += dsc
    dsh_ref[...] += dsh
    if below is not None:
        below_out[1][...] += dg


def _ffn_dh(d_gate, d_up, w8, x, d_out, scale, shift, name, after=None, below=None):
    S = d_gate.shape[1]
    tm = _pick(S, MATMUL_ROWS)
    n_below = 0 if below is None else 2

    def body(dg_ref, du_ref, wg_ref, wu_ref, x_ref, do_ref, sc_ref, sh_ref, *rest):
        below_in = rest[:n_below]
        outs = rest[len(rest) - 4 - n_below:]
        dx_ref, dsc_ref, dsh_ref = outs[:3]
        below_out, acc_ref = outs[3:3 + n_below], outs[-1]
        i, k = pl.program_id(0), pl.program_id(1)

        @pl.when(k == 0)
        def _():
            acc_ref[...] = jnp.zeros_like(acc_ref)

        acc_ref[...] += _dot_raw(dg_ref[...], wg_ref[...], "nt") + _dot_raw(du_ref[...], wu_ref[...], "nt")

        @pl.when((k == 0) & (i == 0))
        def _():
            for r in (dsc_ref, dsh_ref) + tuple(below_out[1:]):
                r[...] = jnp.zeros_like(r)

        @pl.when(k == HID_PIECES - 1)
        def _():
            _dmod_epilogue(acc_ref, x_ref, do_ref, sc_ref, sh_ref, dx_ref, dsc_ref, dsh_ref, below, below_in, below_out)

    d_spec = pl.BlockSpec((None, tm, FFN_PIECE), lambda i, k: (k, i, 0))
    row = pl.BlockSpec((tm, D_MODEL), lambda i, k: (i, 0))
    par = pl.BlockSpec((1, D_MODEL), lambda i, k: (0, 0))
    row_shape, par_shape = jax.ShapeDtypeStruct((S, D_MODEL), F32), jax.ShapeDtypeStruct((1, D_MODEL), F32)
    return pl.pallas_call(
        body, name=name, grid=(S // tm, HID_PIECES),
        in_specs=[d_spec, d_spec,
                  pl.BlockSpec((None, D_MODEL, FFN_PIECE), lambda i, k: (k, 0, 0)),
                  pl.BlockSpec((None, D_MODEL, FFN_PIECE), lambda i, k: (k + HID_PIECES, 0, 0)),
                  row, row, par, par] + [row, par][:n_below] + _after_specs(after),
        out_specs=[row, par, par] + [row, par][:n_below],
        out_shape=[row_shape, par_shape, par_shape] + [jax.ShapeDtypeStruct((S, D_MODEL), BF16), par_shape][:n_below],
        scratch_shapes=[pltpu.VMEM((tm, D_MODEL), F32)],
        compiler_params=_params(("arbitrary", "arbitrary")),
    )(d_gate, d_up, w8, w8, x, d_out, scale, shift, *(below[:2] if below is not None else ()), *_after_args(after))


def _swiglu_bwd(d_hid, hid_by_gate, hid_by_up):
    return d_hid * hid_by_gate, d_hid * hid_by_up


def _adamw_math(w_, g_, m_, v_):
    m_ = ADAM_B1 * m_ + (1.0 - ADAM_B1) * g_
    v_ = ADAM_B2 * v_ + (1.0 - ADAM_B2) * (g_ * g_)
    m_hat = m_ / (1.0 - ADAM_B1 ** ADAM_STEP)
    v_hat = v_ / (1.0 - ADAM_B2 ** ADAM_STEP)
    return -ADAM_LR * (m_hat / (jnp.sqrt(v_hat) + ADAM_EPS) + ADAM_WD * w_), m_, v_


def _adamw(w, g, m, v, name):
    R, C = w.shape
    tr = _pick(R, (256, 176, 128, 64, 32, 16, 8))

    def body(w_ref, g_ref, m_ref, v_ref, d_ref, nm_ref, nv_ref):
        d_ref[...], nm_ref[...], nv_ref[...] = _adamw_math(w_ref[...], g_ref[...], m_ref[...], v_ref[...])

    spec = pl.BlockSpec((tr, C), lambda i: (i, 0))
    return pl.pallas_call(
        body, name=name, grid=(R // tr,),
        in_specs=[spec] * 4, out_specs=[spec] * 3,
        out_shape=[jax.ShapeDtypeStruct((R, C), F32)] * 3,
        compiler_params=_params(("parallel",)),
    )(w, g, m, v)


def _sum_adamw(parts, w, m, v, name, transposed=False):
    _, R, C = parts.shape
    tr = _pick(R, (256, 176, 128, 64, 32, 16, 8))

    def body(p_ref, w_ref, m_ref, v_ref, g_ref, d_ref, nm_ref, nv_ref):
        g_ = p_ref[0].astype(F32)
        for d in range(1, N_DEV):
            g_ = g_ + p_ref[d].astype(F32)
        if transposed:
            g_ = jnp.transpose(g_)
        g_ref[...] = g_
        d_ref[...], nm_ref[...], nv_ref[...] = _adamw_math(w_ref[...], g_, m_ref[...], v_ref[...])

    spec = pl.BlockSpec((C, tr), lambda i: (0, i)) if transposed else pl.BlockSpec((tr, C), lambda i: (i, 0))
    return pl.pallas_call(
        body, name=name, grid=(R // tr,),
        in_specs=[pl.BlockSpec((N_DEV, tr, C), lambda i: (0, i, 0)), spec, spec, spec], out_specs=[spec] * 4,
        out_shape=[jax.ShapeDtypeStruct(w.shape, F32)] * 4,
        compiler_params=_params(("parallel",)),
    )(parts, w, m, v)


def _sum_devices(parts, name):
    _, R, C = parts.shape
    tr = _pick(R, (512, 256, 176, 128, 64, 32, 16, 8))

    def body(p_ref, o_ref):
        acc = p_ref[0].astype(F32)
        for d in range(1, N_DEV):
            acc = acc + p_ref[d].astype(F32)
        o_ref[...] = acc

    return pl.pallas_call(
        body, name=name, grid=(R // tr,),
        in_specs=[pl.BlockSpec((N_DEV, tr, C), lambda i: (0, i, 0))],
        out_specs=pl.BlockSpec((tr, C), lambda i: (i, 0)),
        out_shape=jax.ShapeDtypeStruct((R, C), F32),
        compiler_params=_params(("parallel",)),
    )(parts)


def _my_place():
    return lax.axis_index("x"), lax.axis_index("y"), lax.axis_index("c")


def _all_gather(blocks, name):
    n = len(blocks)

    def body(*refs):
        x_refs, out_refs = refs[:n], refs[n:2 * n]
        send_sems, recv_sems, local_sems = refs[2 * n:]
        x, y, c = _my_place()
        me, sibling = (x, y, c), (x, y, 1 - c)
        chips = [(1 - x, y), (x, 1 - y), (1 - x, 1 - y)]

        def copy(a, k, blk, to, own=False):
            slot = out_refs[a].at[4 * blk[0] + 2 * blk[1] + blk[2]]
            return pltpu.make_async_remote_copy(
                src_ref=x_refs[a] if own else slot, dst_ref=slot,
                send_sem=send_sems.at[7 * a + k], recv_sem=recv_sems.at[7 * a + k], device_id=to, device_id_type=MESH)

        mine = [pltpu.make_async_copy(x_refs[a], out_refs[a].at[4 * x + 2 * y + c], local_sems.at[a]) for a in range(n)]
        for cp in mine:
            cp.start()
        first = []
        for j, chip in enumerate(chips):
            first += [copy(a, 1 + j, me, (*chip, c), own=True) for a in range(n)]
        first += [copy(a, 0, me, sibling, own=True) for a in range(n)]
        for cp in first:
            cp.start()
        passed = []
        for j, chip in enumerate(chips):
            for a in range(n):
                copy(a, 1 + j, (*chip, c), me).wait_recv()
                passed.append(copy(a, 4 + j, (*chip, c), sibling))
                passed[-1].start()
        for a in range(n):
            copy(a, 0, sibling, me).wait_recv()
        for j, chip in enumerate(chips):
            for a in range(n):
                copy(a, 4 + j, (*chip, 1 - c), me).wait_recv()
        for cp in first + passed:
            cp.wait_send()
        for cp in mine:
            cp.wait()

    return pl.pallas_call(
        body, name=name,
        out_shape=[jax.ShapeDtypeStruct((N_DEV,) + b.shape, b.dtype) for b in blocks],
        in_specs=[pl.BlockSpec(memory_space=pl.ANY)] * n,
        out_specs=[pl.BlockSpec(memory_space=pl.ANY)] * n,
        scratch_shapes=[pltpu.SemaphoreType.DMA((7 * n,)), pltpu.SemaphoreType.DMA((7 * n,)), pltpu.SemaphoreType.DMA((n,))],
    )(*blocks)


def _all_to_all(pieces, name):
    n = len(pieces)

    def body(*refs):
        x_refs, out_refs = refs[:n], refs[n:2 * n]
        send_sems, recv_sems, local_sems = refs[2 * n:]
        x, y, c = _my_place()
        me = 4 * x + 2 * y + c
        mine = [pltpu.make_async_copy(x_refs[a].at[me], out_refs[a].at[me], local_sems.at[a]) for a in range(n)]
        for cp in mine:
            cp.start()
        copies = []
        for k in (2, 4, 6, 3, 5, 7, 1):
            px = 1 - x if k & 4 else x
            py = 1 - y if k & 2 else y
            pc = 1 - c if k & 1 else c
            peer = 4 * px + 2 * py + pc
            for a in range(n):
                copies.append(pltpu.make_async_remote_copy(
                    src_ref=x_refs[a].at[peer], dst_ref=out_refs[a].at[me],
                    send_sem=send_sems.at[7 * a + k - 1], recv_sem=recv_sems.at[7 * a + k - 1],
                    device_id=(px, py, pc), device_id_type=MESH))
        for cp in copies:
            cp.start()
        for cp in copies:
            cp.wait_recv()
        for cp in copies:
            cp.wait_send()
        for cp in mine:
            cp.wait()

    return pl.pallas_call(
        body, name=name,
        out_shape=[jax.ShapeDtypeStruct(p.shape, p.dtype) for p in pieces],
        in_specs=[pl.BlockSpec(memory_space=pl.ANY)] * n,
        out_specs=[pl.BlockSpec(memory_space=pl.ANY)] * n,
        scratch_shapes=[pltpu.SemaphoreType.DMA((7 * n,)), pltpu.SemaphoreType.DMA((7 * n,)), pltpu.SemaphoreType.DMA((n,))],
    )(*pieces)


def _peers():
    x, y, c = _my_place()
    out = []
    for k in (2, 4, 6, 3, 5, 7, 1):
        px = 1 - x if k & 4 else x
        py = 1 - y if k & 2 else y
        pc = 1 - c if k & 1 else c
        out.append((k, (px, py, pc), 4 * px + 2 * py + pc))
    return out


def _exchange_copies(x_refs, land_refs, send_sems, recv_sems, scatter):
    x, y, c = _my_place()
    me = 4 * x + 2 * y + c
    starts, arrivals = [], []
    for k, place, peer in _peers():
        for a, (x_ref, land_ref) in enumerate(zip(x_refs, land_refs)):
            sems = dict(send_sem=send_sems.at[7 * a + k - 1], recv_sem=recv_sems.at[7 * a + k - 1],
                        device_id=place, device_id_type=MESH)
            src = x_ref.at[peer] if scatter else x_ref
            starts.append(pltpu.make_async_remote_copy(src_ref=src, dst_ref=land_ref.at[me], **sems))
            arrivals.append(pltpu.make_async_remote_copy(src_ref=src, dst_ref=land_ref.at[peer], **sems))
    return starts, arrivals


def _exchange_start(arrays, scatter, name):
    n = len(arrays)
    hbm = pl.BlockSpec(memory_space=pltpu.HBM)
    sem = pl.BlockSpec(memory_space=pltpu.SEMAPHORE)
    lands = [lax.empty(a.shape if scatter else (N_DEV,) + a.shape, a.dtype) for a in arrays]

    def body(*refs):
        x_refs, land_refs = refs[:n], refs[n:2 * n]
        send_sems, recv_sems = refs[2 * n], refs[2 * n + 1]
        token = refs[-1]
        starts, _ = _exchange_copies(x_refs, land_refs, send_sems, recv_sems, scatter)
        for cp in starts:
            cp.start()
        token[...] = jnp.zeros_like(token)

    res = pl.pallas_call(
        body, name=name,
        out_shape=(pltpu.SemaphoreType.DMA((7 * n,)), pltpu.SemaphoreType.DMA((7 * n,)),
                   *[pltpu.HBM(a.shape, a.dtype) for a in arrays], *[pltpu.HBM(l.shape, l.dtype) for l in lands],
                   jax.ShapeDtypeStruct((8, 128), F32)),
        in_specs=[hbm] * (2 * n),
        out_specs=(sem, sem, *[hbm] * (2 * n), pl.BlockSpec(memory_space=pltpu.VMEM)),
        input_output_aliases={i: 2 + i for i in range(2 * n)},
        compiler_params=pltpu.CompilerParams(has_side_effects=pltpu.SideEffectType.DATAFLOW_SIDE_EFFECTING),
    )(*[pltpu.with_memory_space_constraint(a, pltpu.HBM) for a in arrays],
      *[pltpu.with_memory_space_constraint(l, pltpu.HBM) for l in lands])
    return res[0], res[1], list(res[2:2 + n]), list(res[2 + n:2 + 2 * n]), res[-1]


def _exchange_wait(handles, scatter, after, name):
    send_sems, recv_sems, arrays, lands, _ = handles
    n = len(arrays)
    hbm = pl.BlockSpec(memory_space=pltpu.HBM)
    sem = pl.BlockSpec(memory_space=pltpu.SEMAPHORE)

    def body(*refs):
        x_refs, land_refs = refs[:n], refs[n:2 * n]
        send_s, recv_s = refs[2 * n], refs[2 * n + 1]
        starts, arrivals = _exchange_copies(x_refs, land_refs, send_s, recv_s, scatter)
        for cp in arrivals:
            cp.wait_recv()
        for cp in starts:
            cp.wait_send()

    res = pl.pallas_call(
        body, name=name,
        out_shape=(*[pltpu.HBM(a.shape, a.dtype) for a in arrays], *[pltpu.HBM(l.shape, l.dtype) for l in lands]),
        in_specs=[hbm] * (2 * n) + [sem, sem, pl.BlockSpec(memory_space=pl.ANY)],
        out_specs=tuple([hbm] * (2 * n)),
        input_output_aliases={i: i for i in range(2 * n)},
        compiler_params=pltpu.CompilerParams(has_side_effects=pltpu.SideEffectType.DATAFLOW_SIDE_EFFECTING),
    )(*arrays, *lands, send_sems, recv_sems, after)
    me = 4 * lax.axis_index("x") + 2 * lax.axis_index("y") + lax.axis_index("c")
    out = []
    for src, got in zip(res[:n], res[n:]):
        zeros = (0,) * (got.ndim - 1)
        own = lax.dynamic_slice(src, (me,) + zeros, (1,) + src.shape[1:]) if scatter else src[None]
        out.append(lax.dynamic_update_slice(got, own, (me,) + zeros))
    return out


def _pad_lanes(v, at=0, width=128):
    return jnp.pad(v, ((0, 0), (at, width - at - v.shape[1])))


def _pack_weights(P):
    W = {}
    w = P["w_in"]
    W["wp"] = jnp.concatenate([w[:, :2048], w[:, 2440:2696], w[:, 2056:2440], w[:, 2696:2760], w[:, 2048:2056],
                               jnp.zeros((D_MODEL, N_IN_PACKED - N_IN), w.dtype)], axis=1).astype(BF16)
    W["conv_w"] = P["gdn_conv_w"].astype(F32)
    W["alog_p"] = _pad_lanes(P["gdn_a_log"], 64)
    W["dt_p"] = _pad_lanes(P["gdn_dt_bias"], 64)
    W["gnw"] = P["gdn_norm_w"]
    W["qnw"] = P["mla_q_norm_w"]
    W["kvnw"] = P["mla_kv_norm_w"]
    uq = P["mla_w_uq"].reshape(Q_LORA, HEADS, HEAD_DIM + ROPE)
    W["wuq"] = jnp.pad(uq, ((0, 0), (0, 0), (0, 256 - HEAD_DIM - ROPE))).reshape(Q_LORA, HEADS * 256).astype(BF16)
    ukv = P["mla_w_ukv"].reshape(KV_LORA, HEADS, 2, HEAD_DIM)
    W["wukv"] = ukv.transpose(0, 2, 1, 3).reshape(KV_LORA, 2 * HEADS * HEAD_DIM).astype(BF16)
    W["qn_w"] = P["qkn_q_nope"]
    W["qr_w"] = _pad_lanes(P["qkn_q_rope"])
    W["kn_w"] = P["qkn_k_nope"]
    W["kr_w"] = _pad_lanes(P["qkn_k_rope"])
    W["onw"] = P["mla_out_norm_w"]
    W["wout"] = P["w_out"].astype(BF16)
    return W


def _unpack_grads(G):
    g_qkv, g_z, g_ckv, g_cq, g_kab = G["wp"]
    uq = G["wuq"].reshape(Q_LORA, HEADS, 256)[:, :, :HEAD_DIM + ROPE].reshape(Q_LORA, HEADS * (HEAD_DIM + ROPE))
    ukv = G["wukv"].reshape(KV_LORA, 2, HEADS, HEAD_DIM).transpose(0, 2, 1, 3).reshape(KV_LORA, 2 * HEADS * HEAD_DIM)
    return {
        "w_in": jnp.concatenate([g_qkv, g_z, g_kab[:, ROPE:ROPE + 8], g_cq, g_ckv, g_kab[:, :ROPE]], axis=1),
        "gdn_conv_w": G["conv_w"], "gdn_a_log": G["alog_p"][:, 64:68], "gdn_dt_bias": G["dt_p"][:, 64:68],
        "gdn_norm_w": G["gnw"], "mla_q_norm_w": G["qnw"], "mla_w_uq": uq, "mla_kv_norm_w": G["kvnw"], "mla_w_ukv": ukv,
        "qkn_q_nope": G["qn_w"], "qkn_q_rope": G["qr_w"][:, :ROPE], "qkn_k_nope": G["kn_w"], "qkn_k_rope": G["kr_w"][:, :ROPE],
        "mla_out_norm_w": G["onw"], "w_out": G["wout"],
    }


def _rope_tables(positions):
    half = ROPE // 2
    inv_freq = ROPE_BASE ** (-jnp.arange(half, dtype=F32) / half)
    ang = positions.astype(F32)[:, None] * inv_freq
    cos, sin = jnp.cos(ang), jnp.sin(ang)
    zeros = jnp.zeros((positions.shape[0], 128 - ROPE), F32)
    return jnp.concatenate([cos, cos, zeros], axis=1), jnp.concatenate([-sin, sin, zeros], axis=1)


def _ffn_forward(x, scale, shift, gate_w, w8, wo4, name, target=None):
    S = x.shape[0]
    tm = _pick(S, (512, 256, 128))
    n = S // tm
    with_loss = target is not None

    def body(x_ref, sc_ref, sh_ref, g_ref, wg_ref, wu_ref, wo_ref, *rest):
        t_ref = rest[0] if with_loss else None
        h_ref, bg_ref, bu_ref, ht_ref = rest[with_loss:with_loss + 4]
        tail = rest[with_loss + 4:]
        h_scr, acc_ref = tail[-2:]
        i, p = pl.program_id(0), pl.program_id(1)

        @pl.when(p == 0)
        def _():
            h_new = _modulate(x_ref[...], sc_ref[...], sh_ref[...]).astype(BF16)
            h_scr[...] = h_new
            h_ref[...] = h_new
            acc_ref[...] = jnp.zeros_like(acc_ref)

        h = h_scr[...]
        gate = _dot_raw(h, wg_ref[...], "nn")
        up = _dot_raw(h, wu_ref[...], "nn")
        sg = _sigmoid(gate)
        act = gate * sg
        hid = act * up
        bg_ref[...] = (up * (sg * (1.0 + gate * (1.0 - sg)))).astype(BF16)
        bu_ref[...] = act.astype(BF16)
        ht_ref[...] = jnp.transpose(hid).astype(BF16)
        acc_ref[...] += _dot_raw(hid, wo_ref[...], "nn")

        if with_loss:
            dx_ref, df_ref, dg_ref, l_ref = tail[:4]

            @pl.when((p == 0) & (i == 0))
            def _():
                dg_ref[...] = jnp.zeros_like(dg_ref)
                l_ref[...] = jnp.zeros_like(l_ref)

            @pl.when(p == HID_PIECES - 1)
            def _():
                step = min(EPILOGUE_ROWS, tm)
                for r in range(tm // step):
                    rows = slice(step * r, step * (r + 1))
                    f = acc_ref[rows, :]
                    diff = x_ref[rows, :] + 0.5 * g_ref[...] * f - t_ref[rows, :]
                    dx = diff * (1.0 / D_MODEL)
                    dx_ref[rows, :] = dx
                    df_ref[rows, :] = (0.5 * g_ref[...] * dx).astype(df_ref.dtype)
                    dg_ref[...] += jnp.sum(0.5 * f * dx, axis=0, keepdims=True)
                    l_ref[...] += jnp.sum(diff * diff, axis=0, keepdims=True)

            @pl.when((p == HID_PIECES - 1) & (i == n - 1))
            def _():
                l_ref[...] = jnp.full(l_ref.shape, (0.5 / D_MODEL) * jnp.sum(l_ref[...]), F32)
        else:
            f_ref, xo_ref = tail[:2]

            @pl.when(p == HID_PIECES - 1)
            def _():
                f = acc_ref[...]
                f_ref[...] = f.astype(f_ref.dtype)
                xo_ref[...] = x_ref[...] + 0.5 * g_ref[...] * f

    row = pl.BlockSpec((tm, D_MODEL), lambda i, p: (i, 0))
    par = pl.BlockSpec((1, D_MODEL), lambda i, p: (0, 0))
    piece = pl.BlockSpec((None, tm, FFN_PIECE), lambda i, p: (p, i, 0))
    row_f32, row_bf16 = jax.ShapeDtypeStruct((S, D_MODEL), F32), jax.ShapeDtypeStruct((S, D_MODEL), BF16)
    par_f32 = jax.ShapeDtypeStruct((1, D_MODEL), F32)
    piece_shape = jax.ShapeDtypeStruct((HID_PIECES, S, FFN_PIECE), BF16)
    return pl.pallas_call(
        body, name=name, grid=(n, HID_PIECES),
        in_specs=[row, par, par, par,
                  pl.BlockSpec((None, D_MODEL, FFN_PIECE), lambda i, p: (p, 0, 0)),
                  pl.BlockSpec((None, D_MODEL, FFN_PIECE), lambda i, p: (p + HID_PIECES, 0, 0)),
                  pl.BlockSpec((None, FFN_PIECE, D_MODEL), lambda i, p: (p, 0, 0))] + [row] * with_loss,
        out_specs=[row, piece, piece, pl.BlockSpec((None, FFN_PIECE, tm), lambda i, p: (p, 0, i))]
        + ([row, row, par, par] if with_loss else [row, row]),
        out_shape=[row_bf16, piece_shape, piece_shape, jax.ShapeDtypeStruct((HID_PIECES, FFN_PIECE, S), BF16)]
        + ([row_f32, row_bf16, par_f32, par_f32] if with_loss else [row_bf16, row_f32]),
        scratch_shapes=[pltpu.VMEM((tm, D_MODEL), BF16), pltpu.VMEM((tm, D_MODEL), F32)],
        compiler_params=_params(("arbitrary", "arbitrary")),
    )(x, scale, shift, gate_w, w8, w8, wo4, *([target] if with_loss else []))


def _ffn_fwd(x, scale, shift, gate_w, w8, wo4, tag, target=None):
    res = _ffn_forward(x, scale, shift, gate_w, w8, wo4, tag + "_fwd", target)
    h, by_gate, by_up, hid_t = res[:4]
    if target is not None:
        dx_out, df, d_gate_w, loss_row = res[4:]
        return (dx_out, loss_row), (h, by_gate, by_up, hid_t, None, df, d_gate_w)
    f, x_out = res[4:]
    return x_out, (h, by_gate, by_up, hid_t, f, None, None)


def _ffn_bwd(d_out, x, scale, shift, gate_w, w8, wo4, saved, tag, grad_ready, below=None):
    h, gate, up, hid_t, f, df, d_gate_w = saved
    S = x.shape[0]
    tm = _pick(S, (512, 256, 128))
    tk = _pick(S, (512, 256, 128))
    n = S // tm
    if df is None:
        (df,), (d_gate_w,) = _rowwise_bwd(lambda f_, g_: (0.5 * g_ * f_,), [(f, tm, D_MODEL, 0)], [], [gate_w],
                                          [(d_out, tm, D_MODEL, 0)], n, tag + "_dres", row_dtypes=(BF16,))
    tb = _pick(S, MATMUL_ROWS)
    piece = pl.BlockSpec((None, tb, FFN_PIECE), lambda i, j, k: (j, i, 0))
    d_gate, d_up = _mmg(df, wo4, "nt", name=tag + "_ddown", grid=(S // tb, HID_PIECES, 1),
                        a_spec=pl.BlockSpec((tb, D_MODEL), lambda i, j, k: (i, 0)),
                        b_spec=pl.BlockSpec((None, FFN_PIECE, D_MODEL), lambda i, j, k: (j, 0, 0)),
                        out_spec=piece, out_shapes=[jax.ShapeDtypeStruct((HID_PIECES, S, FFN_PIECE), BF16)] * 2,
                        acc_shape=(tb, FFN_PIECE), extras=[gate, up], extra_specs=[piece, piece], epi=_swiglu_bwd)
    tk = _pick(S, MATMUL_ROWS)
    g_wo4 = _mmg(hid_t, df, "nn", name=tag + "_gwo", grid=(HID_PIECES, 1, S // tk),
                 a_spec=pl.BlockSpec((None, FFN_PIECE, tk), lambda i, j, k: (i, 0, k)),
                 b_spec=pl.BlockSpec((tk, D_MODEL), lambda i, j, k: (k, j)),
                 out_spec=pl.BlockSpec((None, FFN_PIECE, D_MODEL), lambda i, j, k: (i, 0, j)),
                 out_shapes=[jax.ShapeDtypeStruct((HID_PIECES, FFN_PIECE, D_MODEL), BF16)], acc_shape=(FFN_PIECE, D_MODEL))
    g_w8 = _ffn_gw8(h, d_gate, d_up, tag + "_gw8", after=grad_ready("wo4", g_wo4))
    res = _ffn_dh(d_gate, d_up, w8, x, d_out, scale, shift, tag + "_dh", after=grad_ready("w8", g_w8), below=below)
    return (res[0], res[1], res[2], d_gate_w) + tuple(res[3:])


def _dproj_dmod(d_pieces, wp, x, d_out, scale, shift, name, below=None):
    S = x.shape[0]
    tm = _pick(S, TOKEN_ROWS)
    widths = [p.shape[1] for p in d_pieces]
    starts = [sum(widths[:n]) for n in range(len(widths))]
    n_p = len(d_pieces)
    n_below = 0 if below is None else 2

    def body(*refs):
        dp_refs, (w_ref, x_ref, do_ref, sc_ref, sh_ref), rest = refs[:n_p], refs[n_p:n_p + 5], refs[n_p + 5:]
        below_in = rest[:n_below]
        dx_ref, dsc_ref, dsh_ref = rest[n_below:n_below + 3]
        below_out, acc_ref = rest[n_below + 3:n_below + 3 + n_below], rest[-1]

        @pl.when(pl.program_id(0) == 0)
        def _():
            for r in (dsc_ref, dsh_ref) + tuple(below_out[1:]):
                r[...] = jnp.zeros_like(r)

        acc = None
        for dp_ref, at, width in zip(dp_refs, starts, widths):
            part = _dot_raw(dp_ref[...], w_ref[:, at:at + width], "nt")
            acc = part if acc is None else acc + part
        acc_ref[...] = acc
        _dmod_epilogue(acc_ref, x_ref, do_ref, sc_ref, sh_ref, dx_ref, dsc_ref, dsh_ref, below, below_in, below_out)

    row = pl.BlockSpec((tm, D_MODEL), lambda i: (i, 0))
    par = pl.BlockSpec((1, D_MODEL), lambda i: (0, 0))
    row_shape, par_shape = jax.ShapeDtypeStruct((S, D_MODEL), F32), jax.ShapeDtypeStruct((1, D_MODEL), F32)
    return pl.pallas_call(
        body, name=name, grid=(S // tm,),
        in_specs=[pl.BlockSpec((tm, width), lambda i: (i, 0)) for width in widths]
        + [pl.BlockSpec(wp.shape, lambda i: (0, 0)), row, row, par, par] + [row, par][:n_below],
        out_specs=[row, par, par] + [row, par][:n_below],
        out_shape=[row_shape, par_shape, par_shape] + [jax.ShapeDtypeStruct((S, D_MODEL), BF16), par_shape][:n_below],
        scratch_shapes=[pltpu.VMEM((tm, D_MODEL), F32)],
        compiler_params=_params(("arbitrary",)),
    )(*d_pieces, wp, x, d_out, scale, shift, *(below[:2] if below is not None else ()))


def _gw_pieces(h, d_pieces, name):
    S = h.shape[0]
    tm = 512
    tk = _pick(S, MATMUL_ROWS)
    n_p = len(d_pieces)
    widths = [p.shape[1] for p in d_pieces]

    def body(h_ref, *rest):
        d_refs, o_refs = rest[:n_p], rest[n_p:]

        @pl.when(pl.program_id(1) == 0)
        def _():
            for o_ref in o_refs:
                o_ref[...] = jnp.zeros_like(o_ref)

        h_t = jnp.transpose(h_ref[...])
        for d_ref, o_ref in zip(d_refs, o_refs):
            o_ref[...] += _dot_raw(h_t, d_ref[...], "nn")

    return pl.pallas_call(
        body, name=name, grid=(D_MODEL // tm, S // tk),
        in_specs=[pl.BlockSpec((tk, tm), lambda i, k: (k, i))] + [pl.BlockSpec((tk, width), lambda i, k: (k, 0)) for width in widths],
        out_specs=[pl.BlockSpec((tm, width), lambda i, k: (i, 0)) for width in widths],
        out_shape=[jax.ShapeDtypeStruct((D_MODEL, width), F32) for width in widths],
        compiler_params=_params(("parallel", "arbitrary")),
    )(h, *d_pieces)


def _mod_proj(x, scale, shift, wp, name):
    S = x.shape[0]
    N = wp.shape[1]
    tm = _pick(S, MATMUL_ROWS)
    tn = _pick(N, (1408, 1024, 512, 256, 128))

    def body(x_ref, sc_ref, sh_ref, w_ref, h_ref, o_ref, h_scr):
        @pl.when(pl.program_id(1) == 0)
        def _():
            h_new = _modulate(x_ref[...], sc_ref[...], sh_ref[...]).astype(BF16)
            h_scr[...] = h_new
            h_ref[...] = h_new

        o_ref[...] = _dot_raw(h_scr[...], w_ref[...], "nn")

    row = pl.BlockSpec((tm, D_MODEL), lambda i, j: (i, 0))
    par = pl.BlockSpec((1, D_MODEL), lambda i, j: (0, 0))
    return pl.pallas_call(
        body, name=name, grid=(S // tm, N // tn),
        in_specs=[row, par, par, pl.BlockSpec((D_MODEL, tn), lambda i, j: (0, j))],
        out_specs=[row, pl.BlockSpec((tm, tn), lambda i, j: (i, j))],
        out_shape=[jax.ShapeDtypeStruct((S, D_MODEL), BF16), jax.ShapeDtypeStruct((S, N), F32)],
        scratch_shapes=[pltpu.VMEM((tm, D_MODEL), BF16)],
        compiler_params=_params(("parallel", "arbitrary")),
    )(x, scale, shift, wp)


def _mix_out(o_a, proj, o_b, gnw, onw, wout, x, gate_w, t, name):
    S = x.shape[0]

    def body(oa_ref, z_ref, ob_ref, gn_ref, on_ref, w_ref, x_ref, g_ref, mixed_ref, y_ref, xo_ref):
        (mixed,) = _mix_post_fn(oa_ref[...], z_ref[...], ob_ref[...], gn_ref[...], on_ref[...])
        mixed = mixed.astype(BF16)
        mixed_ref[...] = mixed
        y = _dot_raw(mixed, w_ref[...], "nn")
        y_ref[...] = y.astype(BF16)
        xo_ref[...] = x_ref[...] + g_ref[...] * y

    half, row = _row_spec(t, 512, 0), _row_spec(t, D_MODEL, 0)
    return pl.pallas_call(
        body, name=name, grid=(S // t,),
        in_specs=[half, _row_spec(t, 512, 3), half, _full_spec(gnw.shape), _full_spec(onw.shape), _full_spec(wout.shape),
                  row, _full_spec(gate_w.shape)],
        out_specs=[row, row, row],
        out_shape=[jax.ShapeDtypeStruct((S, D_MODEL), BF16)] * 2 + [jax.ShapeDtypeStruct((S, D_MODEL), F32)],
        compiler_params=_params(("parallel",)),
    )(o_a, proj, o_b, gnw, onw, wout, x, gate_w)


def _mixer_fwd(x1, scale, shift, gate_w, cos_p, sin_p, W):
    S = x1.shape[0]
    tm = _pick(S, (512, 256, 128))
    tv = _pick(S, TOKEN_ROWS)
    ta = _pick(S, (512, 256, 128))
    nc = S // CHUNK
    h2, proj = _mod_proj(x1, scale, shift, W["wp"], "mix_proj")
    qkvc, q_a, k_a, v_a, gb = _conv_fwd(proj, W["conv_w"], W["alog_p"], W["dt_p"], tv, "gdn_conv")
    kab = (proj, tv, 128, 21)
    ti = _pick(S, INTRA_ROWS)
    intra = _rowwise(_gdn_intra_fn, [(q_a, ti, 512, 0), (k_a, ti, 512, 0), (v_a, ti, 512, 0), (gb, ti, 128, 0)],
                     [], [(ti, 512, F32)] * 4 + [(ti, CHUNK, F32)] * 4 + [(ti // 8, 512, F32)] + [(ti, CHUNK, F32)] * 4,
                     S // ti, "gdn_intra")
    u, wk, qd, kd, qks, gl, invs = intra[0], intra[1], intra[2], intra[3], tuple(intra[4:8]), intra[8], tuple(intra[9:])
    o_a, s_prev = _gdn_scan_fwd(u, wk, qd, kd, qks, gl, "gdn_scan")
    mla_params = [W["qnw"], W["kvnw"], W["wuq"], W["wukv"], W["qn_w"], W["qr_w"], W["kn_w"], W["kr_w"]]
    def mla_pre_with_vt(*a):
        q_, k_, v_ = _mla_pre_fn(*a)
        return q_, k_, v_, jnp.transpose(v_)

    q_b, k_b, v_b, vt_b = _rowwise(mla_pre_with_vt,
                                   [(proj, tv, 256, 8), (proj, tv, 384, 6), kab, (cos_p, tv, 128, 0), (sin_p, tv, 128, 0)],
                                   mla_params, [(tv, 1024, BF16), (tv, 1024, BF16), (tv, 512, BF16), (512, tv, BF16, "across")],
                                   S // tv, "mla_pre")
    o_b, lse = _attn_fwd(q_b, k_b, vt_b, ta, "mla_attn")
    mixed, y, x2 = _mix_out(o_a, proj, o_b, W["gnw"], W["onw"], W["wout"], x1, gate_w, tv, "mix_out")
    saved = (h2, proj, qkvc, q_a, k_a, v_a, gb, u, wk, qd, kd, qks, gl, invs, s_prev, o_a, q_b, k_b, v_b, o_b, lse, mixed, y)
    return x2, saved


def _mixer_bwd(d_out, dy, x1, scale, shift, cos_p, sin_p, W, saved, below):
    (h2, proj, qkvc, q_a, k_a, v_a, gb, u, wk, qd, kd, qks, gl, invs, s_prev, o_a, q_b, k_b, v_b, o_b, lse, mixed, y) = saved
    S = x1.shape[0]
    tm = _pick(S, (512, 256, 128))
    tv = _pick(S, TOKEN_ROWS)
    ta = _pick(S, (512, 256, 128))
    nc = S // CHUNK
    G = {}
    G["wout"] = _mm(mixed, dy, "tn", name="mix_gwout")
    (do_a, dz, do_b), (G["gnw"], G["onw"], stats) = _rowwise_bwd(
        _mix_post_fn, [(o_a, tv, 512, 0), (proj, tv, 512, 3), (o_b, tv, 512, 0)], [], [W["gnw"], W["onw"]],
        [], S // tv, "mix_dpost", row_dtypes=(F32, BF16, F32), dout_from=(dy, W["wout"]),
        across=(lse, lambda r_vals, d_rows, lse_tile: _attn_stats(r_vals[2], d_rows[2], lse_tile)))
    dq_b, dk_b, dv_b = _attn_bwd(q_b, k_b, v_b, do_b, stats, ta, "mla_dattn")
    kab = (proj, tv, 128, 21)
    mla_params = [W["qnw"], W["kvnw"], W["wuq"], W["wukv"], W["qn_w"], W["qr_w"], W["kn_w"], W["kr_w"]]
    (d_ckv, d_cq, d_kab), mla_grads = _rowwise_bwd(
        _mla_pre_fn, [(proj, tv, 256, 8), (proj, tv, 384, 6), kab], [(cos_p, tv, 128, 0), (sin_p, tv, 128, 0)], mla_params,
        [(dq_b, tv, 1024, 0), (dk_b, tv, 1024, 0), (dv_b, tv, 512, 0)], S // tv, "mla_dpre", row_dtypes=(BF16, BF16, F32))
    for key, g in zip(("qnw", "kvnw", "wuq", "wukv", "qn_w", "qr_w", "kn_w", "kr_w"), mla_grads):
        G[key] = g
    scan_grads = _gdn_scan_bwd(u, wk, qd, kd, qks, gl, s_prev, do_a, "gdn_dscan")
    ti = _pick(S, INTRA_ROWS)
    intra_douts = [(scan_grads[i], ti, 512, 0) for i in range(4)] + [(scan_grads[4 + i], ti, CHUNK, 0) for i in range(4)]
    intra_douts.append((scan_grads[8], ti // 8, 512, 0))
    (dq_a, dk_a, dv_a, d_gb), _ = _rowwise_bwd(
        _gdn_intra_fn, [(q_a, ti, 512, 0), (k_a, ti, 512, 0), (v_a, ti, 512, 0), (gb, ti, 128, 0)],
        [(x_, ti, CHUNK, 0) for x_ in invs], [], intra_douts, S // ti, "gdn_dintra")
    d_qkv, d_kab, g_conv, G["alog_p"], G["dt_p"] = _conv_bwd(proj, qkvc, W["conv_w"], W["alog_p"], W["dt_p"],
                                                              (dq_a, dk_a, dv_a, d_gb), d_kab, tv, "gdn_dconv")
    G["conv_w"] = g_conv[:4]
    d_proj = [d_qkv, dz, d_ckv, d_cq, d_kab]
    G["wp"] = _gw_pieces(h2, d_proj, "mix_gwp")
    dx1, G["s2"], G["sh2"], d_below, dg_below = _dproj_dmod(d_proj, W["wp"], x1, d_out, scale, shift, "mix_dproj", below=below)
    return dx1, d_below, dg_below, G


def _local_step(x, target, mod, cos_p, sin_p, W1, mixer_weights, ffn2_weights, ffn_grad_ready, mixer_grads_ready):
    sh1, s1, g1, sh2, s2, g2, sh3, s3, g3 = [mod[:, D_MODEL * i:D_MODEL * (i + 1)] for i in range(N_MOD)]
    x1, saved1 = _ffn_fwd(x, s1, sh1, g1, W1["f1_w8"], W1["f1_wo4"], "ffn1")
    W = mixer_weights(x1)
    x2, saved2 = _mixer_fwd(x1, s2, sh2, g2, cos_p, sin_p, W)
    W.update(ffn2_weights(x2))
    (dx3, loss_row), saved3 = _ffn_fwd(x2, s3, sh3, g3, W["f2_w8"], W["f2_wo4"], "ffn2", target=target)
    dx2, d_s3, d_sh3, d_g3, dy, d_g2 = _ffn_bwd(dx3, x2, s3, sh3, g3, W["f2_w8"], W["f2_wo4"], saved3, "ffn2",
                                                ffn_grad_ready("f2"), below=(saved2[-1], g2, 1.0))
    dx1, df1, d_g1, G = _mixer_bwd(dx2, dy, x1, s2, sh2, cos_p, sin_p, W, saved2, below=(saved1[4], g1, 0.5))
    d_sh2, d_s2 = G.pop("sh2"), G.pop("s2")
    saved1 = saved1[:5] + (df1, d_g1 + mixer_grads_ready(G))
    dx, d_s1, d_sh1, d_g1 = _ffn_bwd(dx1, x, s1, sh1, g1, W1["f1_w8"], W1["f1_wo4"], saved1, "ffn1", ffn_grad_ready("f1"))
    d_mod = jnp.concatenate([d_sh1, d_s1, d_g1, d_sh2, d_s2, d_g2, d_sh3, d_s3, d_g3], axis=1)
    return loss_row, dx, d_mod


WEIGHT_NAMES = ("w_ada", "b_ada", "ffn1_w_in", "ffn1_w_out", "w_in", "gdn_conv_w", "gdn_a_log", "gdn_dt_bias", "gdn_norm_w",
                "mla_q_norm_w", "mla_w_uq", "mla_kv_norm_w", "mla_w_ukv", "qkn_q_nope", "qkn_q_rope", "qkn_k_nope",
                "qkn_k_rope", "mla_out_norm_w", "w_out", "ffn2_w_in", "ffn2_w_out")
FFN_SHARDED = ("ffn1_w_in", "ffn1_w_out", "ffn2_w_in", "ffn2_w_out")
TRANSPOSED_ENTRY = ("ffn1_w_in", "ffn2_w_in", "w_in", "mla_w_uq")
SHEETED = (("w_in", "col"), ("gdn_conv_w", "col"), ("mla_w_uq", "col"), ("mla_w_ukv", "col"), ("w_out", "row"))
MOD_ROWS = N_MOD * D_MODEL // 128
SMALL = {"gdn_a_log": (MOD_ROWS, 1, 64, 4), "gdn_dt_bias": (MOD_ROWS + 1, 1, 64, 4), "gdn_norm_w": (MOD_ROWS + 2, 1, 0, 128),
         "mla_q_norm_w": (MOD_ROWS + 3, 3, 0, 384), "mla_kv_norm_w": (MOD_ROWS + 6, 2, 0, 256),
         "qkn_q_nope": (MOD_ROWS + 8, 1, 0, 128), "qkn_q_rope": (MOD_ROWS + 9, 1, 0, 64), "qkn_k_nope": (MOD_ROWS + 10, 1, 0, 128),
         "qkn_k_rope": (MOD_ROWS + 11, 1, 0, 64), "mla_out_norm_w": (MOD_ROWS + 12, 1, 0, 128)}
LOSS_ROW = MOD_ROWS + 13
CONV_ROW, CONV_ROWS = 88, 4 * 1536 // 128
SHEET_ROWS = CONV_ROW + CONV_ROWS


def _to_sheet(flat, dtype, sublanes):
    n = flat.shape[-1]
    unit = sublanes * 128
    pad = (-n) % unit
    flat = jnp.pad(flat.astype(dtype), [(0, 0)] * (flat.ndim - 1) + [(0, pad)])
    return flat.reshape(flat.shape[:-1] + ((n + pad) // 128, 128))


def _small_sheet(b_like, small):
    sheet = jnp.zeros((SHEET_ROWS, 128), F32).at[:MOD_ROWS].set(b_like.reshape(MOD_ROWS, 128))
    for name, (row, rows, lane, n) in SMALL.items():
        v = small[name].reshape(1, n)
        if rows == 1:
            sheet = sheet.at[row, lane:lane + n].set(v[0])
        else:
            sheet = sheet.at[row:row + rows].set(v.reshape(rows, 128))
    return sheet


def _from_small_sheet(sheet):
    out = {"b_ada": sheet[:MOD_ROWS].reshape(1, N_MOD * D_MODEL)}
    for name, (row, rows, lane, n) in SMALL.items():
        out[name] = sheet[row, lane:lane + n].reshape(1, n) if rows == 1 else sheet[row:row + rows].reshape(1, n)
    return out


def kernel(x, c, positions, w_ada, b_ada, ffn1_w_in, ffn1_w_out, w_in, gdn_conv_w, gdn_a_log, gdn_dt_bias, gdn_norm_w, mla_q_norm_w, mla_w_uq, mla_kv_norm_w, mla_w_ukv, qkn_q_nope, qkn_q_rope, qkn_k_nope, qkn_k_rope, mla_out_norm_w, w_out, ffn2_w_in, ffn2_w_out, loss_target, m_w_ada, m_b_ada, m_ffn1_w_in, m_ffn1_w_out, m_w_in, m_gdn_conv_w, m_gdn_a_log, m_gdn_dt_bias, m_gdn_norm_w, m_mla_q_norm_w, m_mla_w_uq, m_mla_kv_norm_w, m_mla_w_ukv, m_qkn_q_nope, m_qkn_q_rope, m_qkn_k_nope, m_qkn_k_rope, m_mla_out_norm_w, m_w_out, m_ffn2_w_in, m_ffn2_w_out, v_w_ada, v_b_ada, v_ffn1_w_in, v_ffn1_w_out, v_w_in, v_gdn_conv_w, v_gdn_a_log, v_gdn_dt_bias, v_gdn_norm_w, v_mla_q_norm_w, v_mla_w_uq, v_mla_kv_norm_w, v_mla_w_ukv, v_qkn_q_nope, v_qkn_q_rope, v_qkn_k_nope, v_qkn_k_rope, v_mla_out_norm_w, v_w_out, v_ffn2_w_in, v_ffn2_w_out):
    args = locals()
    w = {n: args[n] for n in WEIGHT_NAMES}
    m = {n: args["m_" + n] for n in WEIGHT_NAMES}
    v = {n: args["v_" + n] for n in WEIGHT_NAMES}
    me = 4 * lax.axis_index("x") + 2 * lax.axis_index("y") + lax.axis_index("c")
    cols = N_MOD * D_MODEL // N_DEV
    shard = {n: w[n][0] for n in FFN_SHARDED + tuple(s[0] for s in SHEETED)}

    sc = c * _sigmoid(c)
    first = _to_sheet(jnp.concatenate([sc.reshape(-1), shard["gdn_conv_w"].reshape(-1)]), F32, 8)
    (first_all,) = _all_gather([first], "gather_c")
    sc_all = first_all[:, :D_MODEL // 128].reshape(N_DEV, D_MODEL)
    n_taps = shard["gdn_conv_w"].size
    conv_all = first_all.reshape(N_DEV, -1)[:, D_MODEL:D_MODEL + n_taps].reshape(N_DEV, 4, -1)
    b_mine = lax.dynamic_slice(b_ada, (0, me * cols), (1, cols))
    mod_cols = _mm(sc_all, w_ada[0], "nn", name="ada_mod", extra_params=[b_mine], epi=lambda acc, b_: (acc + b_,))
    (mod_all,) = _all_to_all([_to_sheet(mod_cols, F32, 8)], "scatter_mod")
    mod = mod_all.reshape(N_DEV, -1)[:, :cols].reshape(1, N_MOD * D_MODEL)

    f1_shards, mod = lax.optimization_barrier(([shard["ffn1_w_in"].astype(BF16), shard["ffn1_w_out"].astype(BF16)], mod))
    f1_w8, f1_out = _all_gather(f1_shards, "gather_w1")
    travel = [s for s in SHEETED if s[0] != "gdn_conv_w"]
    tied = lax.optimization_barrier(([shard[n].astype(BF16) for n, _ in travel], f1_w8))
    f1_w8 = tied[1]
    mixer_w = _exchange_start(tied[0], False, "gather_wm_start")
    ffn2_w = _exchange_start([shard["ffn2_w_in"].astype(BF16) + mixer_w[4][0:1, 0:1].astype(BF16),
                              shard["ffn2_w_out"].astype(BF16)], False, "gather_w2_start")
    mod = mod + ffn2_w[4][0:1, 0:1]
    W1 = dict(f1_w8=f1_w8, f1_wo4=f1_out.reshape(HID_PIECES, FFN_PIECE, D_MODEL))

    def mixer_weights(after):
        got = _exchange_wait(mixer_w, False, after, "gather_wm_wait")
        P = {n: jnp.concatenate(list(g), axis=1) if kind == "col" else g.reshape(-1, g.shape[-1])
             for (n, kind), g in zip(travel, got)}
        P["gdn_conv_w"] = jnp.concatenate(list(conv_all), axis=1)
        for n in SMALL:
            P[n] = w[n]
        return _pack_weights(P)

    def ffn2_weights(after):
        f2_w8, f2_out = _exchange_wait(ffn2_w, False, after, "gather_w2_wait")
        return dict(f2_w8=f2_w8, f2_wo4=f2_out.reshape(HID_PIECES, FFN_PIECE, D_MODEL))

    pending, small_grads = {}, {}

    def ffn_grad_ready(tag):
        def ready(which, g):
            pieces = g if which == "w8" else g.reshape((N_DEV,) + shard["ffn1_w_out"].shape)
            pending[tag + which] = _exchange_start([pieces], True, "scatter_%s_%s_start" % (tag, which))
            return pending[tag + which][4]
        return ready

    def mixer_grads_ready(G):
        g_full = _unpack_grads(G)
        small_grads.update({n: g_full[n] for n in SMALL})
        small_grads["gdn_conv_w"] = g_full["gdn_conv_w"]
        pieces = []
        for n, kind in travel:
            r, cc = shard[n].shape
            g = g_full[n].astype(BF16)
            pieces.append(jnp.stack([g[:, cc * p:cc * (p + 1)] for p in range(N_DEV)]) if kind == "col"
                          else g.reshape(N_DEV, r, cc))
        pending["mixer"] = _exchange_start(pieces, True, "scatter_mx_start")
        return pending["mixer"][4][0:1, 0:1]

    cos_p, sin_p = _rope_tables(positions[0])
    loss_row, dx, d_mod = _local_step(x[0], loss_target[0], mod, cos_p, sin_p, W1, mixer_weights, ffn2_weights,
                                      ffn_grad_ready, mixer_grads_ready)

    sheet = _small_sheet(d_mod, small_grads).at[LOSS_ROW].set(loss_row[0, :128])
    sheet = sheet.at[CONV_ROW:CONV_ROW + CONV_ROWS].set(small_grads["gdn_conv_w"].reshape(CONV_ROWS, 128))
    (sheets,) = _all_gather([sheet], "gather_small")
    summed = _sum_devices(sheets, "sum_small")
    d_mod_all = sheets[:, :MOD_ROWS].reshape(N_DEV, N_MOD * D_MODEL)
    d_mod_mine = lax.dynamic_slice(d_mod_all, (0, me * cols), (N_DEV, cols))
    grads = _from_small_sheet(summed)
    grads["w_ada"] = _mm(sc_all, d_mod_mine, "tn", name="ada_gw", hi=True)
    conv_taps = shard["gdn_conv_w"].shape[1]
    grads["gdn_conv_w"] = lax.dynamic_slice(summed[CONV_ROW:CONV_ROW + CONV_ROWS].reshape(4, -1), (0, me * conv_taps),
                                            (4, conv_taps))
    loss = summed[LOSS_ROW, 0]

    delta, new_m, new_v = {}, {}, {}
    arrived = {}
    for n, key in zip(FFN_SHARDED, ("f1w8", "f1wo4", "f2w8", "f2wo4")):
        (arrived[n],) = _exchange_wait(pending[key], True, summed, "scatter_%s_wait" % key)
    arrived.update(zip([n for n, _ in travel], _exchange_wait(pending["mixer"], True, summed, "scatter_mx_wait")))
    for n, parts in arrived.items():
        if n in TRANSPOSED_ENTRY:
            res = _sum_adamw(parts, w[n][0].T, m[n][0].T, v[n][0].T, "adamw_" + n, transposed=True)
            grads[n], delta[n], new_m[n], new_v[n] = [r.T for r in res]
        else:
            grads[n], delta[n], new_m[n], new_v[n] = _sum_adamw(parts, w[n][0], m[n][0], v[n][0], "adamw_" + n)
    for n in ("w_ada", "gdn_conv_w"):
        delta[n], new_m[n], new_v[n] = _adamw(w[n][0], grads[n], m[n][0], v[n][0], "adamw_" + n)
    small_in = [_small_sheet(t["b_ada"], t) for t in (w, grads, m, v)]
    for res, out in zip(_adamw(*small_in, "adamw_small"), (delta, new_m, new_v)):
        out.update(_from_small_sheet(res))

    def shaped(d):
        return [d[n].reshape(w[n].shape) for n in WEIGHT_NAMES]

    return (loss, dx[None], *shaped(grads), *shaped(delta), *shaped(new_m), *shaped(new_v))
```

```python
import functools

import jax
import jax.numpy as jnp
import numpy as np
from jax import lax
from jax.experimental import pallas as pl
from jax.experimental.pallas import tpu as pltpu

F32 = jnp.float32
BF16 = jnp.bfloat16

D_MODEL = 1024
D_FF = 2816
N_MOD = 9
HEADS = 4
HEAD_DIM = 128
CHUNK = 64
EPS = 1e-6
ROPE = 64
Q_LORA = 384
KV_LORA = 256
N_IN = 2760
N_IN_PACKED = 2816
ROPE_BASE = 10000.0
LOG2_E = 1.4426950408889634
N_DEV = 8

ADAM_LR = 0.001
ADAM_B1 = 0.9
ADAM_B2 = 0.999
ADAM_EPS = 1e-08
ADAM_WD = 0.01
ADAM_STEP = 10

VMEM_LIMIT_BYTES = 56 * 1024 * 1024
MATMUL_ROWS = (1024, 512, 256, 128)
MESH = pl.DeviceIdType.MESH


def _params(sem=None):
    return pltpu.CompilerParams(dimension_semantics=sem, vmem_limit_bytes=VMEM_LIMIT_BYTES)


def _pick(dim, prefs):
    for p in prefs:
        if dim % p == 0:
            return p
    return dim


_DIMS = {"nn": (((1,), (0,)), ((), ())), "nt": (((1,), (1,)), ((), ())), "tn": (((0,), (0,)), ((), ()))}


def _dot_raw(a, b, mode):
    return lax.dot_general(a.astype(BF16), b.astype(BF16), _DIMS[mode], preferred_element_type=F32)


def _dot_hi(a, b, mode="nn"):
    return lax.dot_general(a, b, _DIMS[mode], precision=lax.Precision.HIGHEST, preferred_element_type=F32)


@functools.partial(jax.custom_vjp, nondiff_argnums=(2,))
def _bdot(a, b, mode):
    return _dot_raw(a, b, mode)


def _bdot_fwd(a, b, mode):
    return _dot_raw(a, b, mode), (a, b)


def _bdot_bwd(mode, res, g):
    a, b = res
    if mode == "nn":
        return _dot_raw(g, b, "nt"), _dot_raw(a, g, "tn")
    if mode == "nt":
        return _dot_raw(g, b, "nn"), _dot_raw(g, a, "tn")
    return _dot_raw(b, g, "nt"), _dot_raw(a, g, "nn")


_bdot.defvjp(_bdot_fwd, _bdot_bwd)


def _mm(a, b, mode, *, name, out_dtypes=(F32,), epi=None, extras=(), extra_params=(), hi=False,
        tm=None, tn=None, tk=None):
    if mode == "nn":
        (M, K), (_, N) = a.shape, b.shape
    elif mode == "nt":
        (M, K), (N, _) = a.shape, b.shape
    else:
        (K, M), (_, N) = a.shape, b.shape
    tm = tm or _pick(M, (512, 1408, 256, 128) if mode == "tn" else MATMUL_ROWS + (384, 352))
    tn = tn or _pick(N, (1024, 1408, 768, 512, 384, 256, 128))
    tk = tk or _pick(K, (1024, 1408, 512, 384, 256, 128))
    a_spec = {"nn": pl.BlockSpec((tm, tk), lambda i, j, k: (i, k)), "nt": pl.BlockSpec((tm, tk), lambda i, j, k: (i, k)),
              "tn": pl.BlockSpec((tk, tm), lambda i, j, k: (k, i))}[mode]
    b_spec = {"nn": pl.BlockSpec((tk, tn), lambda i, j, k: (k, j)), "nt": pl.BlockSpec((tn, tk), lambda i, j, k: (j, k)),
              "tn": pl.BlockSpec((tk, tn), lambda i, j, k: (k, j))}[mode]
    mn_spec = pl.BlockSpec((tm, tn), lambda i, j, k: (i, j))
    return _mmg(a, b, mode, name=name, grid=(M // tm, N // tn, K // tk), a_spec=a_spec, b_spec=b_spec, out_spec=mn_spec,
                out_shapes=[jax.ShapeDtypeStruct((M, N), dt) for dt in out_dtypes], acc_shape=(tm, tn), epi=epi,
                extras=list(extras) + list(extra_params),
                extra_specs=[mn_spec] * len(extras) + [pl.BlockSpec((1, tn), lambda i, j, k: (0, j))] * len(extra_params),
                hi=hi)


def _mmg(a, b, mode, *, name, grid, a_spec, b_spec, out_spec, out_shapes, acc_shape, epi=None, extras=(),
         extra_specs=(), hi=False):
    nk = grid[2]
    n_e, n_o = len(extras), len(out_shapes)

    def body(*refs):
        a_ref, b_ref = refs[:2]
        e_refs = refs[2:2 + n_e]
        o_refs = refs[2 + n_e:2 + n_e + n_o]
        acc_ref = refs[-1]
        k = pl.program_id(2)

        @pl.when(k == 0)
        def _():
            acc_ref[...] = jnp.zeros_like(acc_ref)

        if hi:
            acc_ref[...] += _dot_hi(a_ref[...].astype(F32), b_ref[...].astype(F32), mode)
        else:
            acc_ref[...] += _dot_raw(a_ref[...], b_ref[...], mode)

        @pl.when(k == nk - 1)
        def _():
            acc = acc_ref[...]
            outs = (acc,) if epi is None else epi(acc, *[e[...].astype(F32) for e in e_refs])
            for o_ref, o in zip(o_refs, outs):
                o_ref[...] = o.astype(o_ref.dtype)

    outs = pl.pallas_call(
        body, name=name, grid=grid,
        in_specs=[a_spec, b_spec] + list(extra_specs),
        out_specs=[out_spec] * n_o,
        out_shape=list(out_shapes),
        scratch_shapes=[pltpu.VMEM(acc_shape, F32)],
        compiler_params=_params(("parallel", "parallel", "arbitrary")),
    )(a, b, *extras)
    return outs if n_o > 1 else outs[0]


def _row_spec(th, cw, ci):
    return pl.BlockSpec((th, cw), lambda i: (i, ci))


def _full_spec(shape):
    return pl.BlockSpec(shape, lambda i: (0,) * len(shape))


def _rowwise(fn, rows, params, outs, n_steps, name):
    n_r, n_p, n_o = len(rows), len(params), len(outs)

    def body(*refs):
        vals = [r[...].astype(F32) for r in refs[:n_r + n_p]]
        res = fn(*vals)
        for o_ref, o in zip(refs[n_r + n_p:], res):
            o_ref[...] = o.astype(o_ref.dtype)

    across = [len(o) == 4 for o in outs]
    res = pl.pallas_call(
        body, name=name, grid=(n_steps,),
        in_specs=[_row_spec(th, cw, ci) for (_, th, cw, ci) in rows] + [_full_spec(p.shape) for p in params],
        out_specs=[pl.BlockSpec((o[0], o[1]), lambda i: (0, i)) if ac else _row_spec(o[0], o[1], 0)
                   for o, ac in zip(outs, across)],
        out_shape=[jax.ShapeDtypeStruct((o[0], n_steps * o[1]) if ac else (n_steps * o[0], o[1]), o[2])
                   for o, ac in zip(outs, across)],
        compiler_params=_params(("parallel",)),
    )(*[r[0] for r in rows], *params)
    return res


def _rowwise_bwd(fn, rows, aux, params, douts, n_steps, name, row_dtypes=None, across=None, dout_from=None):
    n_r, n_a, n_p, n_d = len(rows), len(aux), len(params), len(douts)
    row_dtypes = row_dtypes or (F32,) * n_r

    def body(*refs):
        it = iter(refs)
        r_vals = [next(it)[...].astype(F32) for _ in range(n_r)]
        a_vals = [next(it)[...].astype(F32) for _ in range(n_a)]
        p_vals = [next(it)[...].astype(F32) for _ in range(n_p)]
        d_vals = [next(it)[...].astype(F32) for _ in range(n_d)]
        across_in = next(it) if across is not None else None
        if dout_from is not None:
            d_vals = [_dot_raw(next(it)[...], next(it)[...], "nt")]
        dr_refs = [next(it) for _ in range(n_r)]
        dp_refs = [next(it) for _ in range(n_p)]

        def f(*rp):
            return tuple(fn(*rp[:n_r], *a_vals, *rp[n_r:]))

        _, vjp = jax.vjp(f, *r_vals, *p_vals)
        grads = list(vjp(tuple(d_vals)))
        for dr_ref, g in zip(dr_refs, grads[:n_r]):
            dr_ref[...] = g.astype(dr_ref.dtype)
        if across is not None:
            next(it)[...] = across[1](r_vals, grads[:n_r], across_in[...])

        @pl.when(pl.program_id(0) == 0)
        def _():
            for dp_ref in dp_refs:
                dp_ref[...] = jnp.zeros_like(dp_ref)

        for dp_ref, g in zip(dp_refs, grads[n_r:]):
            dp_ref[...] += g

    all_rows = list(rows) + list(aux) + list(douts)
    in_specs = ([_row_spec(th, cw, ci) for (_, th, cw, ci) in list(rows) + list(aux)]
                + [_full_spec(p.shape) for p in params]
                + [_row_spec(th, cw, ci) for (_, th, cw, ci) in all_rows[n_r + n_a:]])
    across_specs = [] if across is None else [pl.BlockSpec((8, rows[0][1]), lambda i: (0, i))]
    from_specs = [] if dout_from is None else [_row_spec(rows[0][1], dout_from[0].shape[1], 0), _full_spec(dout_from[1].shape)]
    res = pl.pallas_call(
        body, name=name, grid=(n_steps,),
        in_specs=in_specs + across_specs + from_specs,
        out_specs=[_row_spec(th, cw, 0) for (_, th, cw, _) in rows] + [_full_spec(p.shape) for p in params] + across_specs,
        out_shape=[jax.ShapeDtypeStruct((n_steps * th, cw), dt) for (_, th, cw, _), dt in zip(rows, row_dtypes)]
        + [jax.ShapeDtypeStruct(p.shape, F32) for p in params]
        + [jax.ShapeDtypeStruct(across[0].shape, F32) for _ in across_specs],
        compiler_params=_params(("arbitrary",)),
    )(*[r[0] for r in list(rows) + list(aux)], *params, *[r[0] for r in all_rows[n_r + n_a:]],
      *[across[0] for _ in across_specs], *(dout_from or ()))
    return res[:n_r], res[n_r:]


def _sigmoid(x):
    return lax.logistic(x)


def _silu(x):
    return x * _sigmoid(x)


def _rms(x, w=None, n=None):
    n = n or x.shape[-1]
    y = x * lax.rsqrt(jnp.sum(x * x, axis=-1, keepdims=True) * (1.0 / n) + EPS)
    return y if w is None else y * w


def _modulate(x, scale, shift):
    return _rms(x) * (1.0 + scale) + shift


def _softplus(x):
    return jnp.maximum(x, 0.0) + jnp.log1p(jnp.exp(-jnp.abs(x)))


@jax.custom_vjp
def _rot_half64(x):
    lane = lax.broadcasted_iota(jnp.int32, x.shape, 1)
    up = pltpu.roll(x, 96, 1)
    down = pltpu.roll(x, 32, 1)
    return jnp.where(lane < 32, up, jnp.where(lane < 64, down, 0.0))


_rot_half64.defvjp(lambda x: (_rot_half64(x), None), lambda _, g: (_rot_half64(g),))


def _rope128(x, cos_p, sin_p):
    return x * cos_p + _rot_half64(x) * sin_p


def _gdn_pre_fn(qkvc, kab, alog_p, dt_p):
    a = _silu(qkvc)
    qs, ks = [], []
    for h in range(HEADS):
        qh = a[:, HEAD_DIM * h:HEAD_DIM * (h + 1)]
        kh = a[:, 512 + HEAD_DIM * h:512 + HEAD_DIM * (h + 1)]
        qs.append(qh * lax.rsqrt(jnp.sum(qh * qh, axis=-1, keepdims=True) + EPS) * (HEAD_DIM ** -0.5))
        ks.append(kh * lax.rsqrt(jnp.sum(kh * kh, axis=-1, keepdims=True) + EPS))
    lane = lax.broadcasted_iota(jnp.int32, kab.shape, 1)
    g_full = -jnp.exp(alog_p) * _softplus(kab + dt_p)
    b_full = _sigmoid(kab)
    gb = jnp.where((lane >= 64) & (lane < 68), g_full, jnp.where((lane >= 68) & (lane < 72), b_full, 0.0))
    return jnp.concatenate(qs, axis=1), jnp.concatenate(ks, axis=1), a[:, 1024:1536], gb


INTRA_ROWS = (512, 256, 128, 64)
TOKEN_ROWS = (512, 256, 128)

_BNN = (((2,), (1,)), ((0,), (0,)))
_BNT = (((2,), (2,)), ((0,), (0,)))
_BTN = (((1,), (1,)), ((0,), (0,)))


def _split_bf16(a):
    hi = a.astype(BF16)
    return hi, (a - hi.astype(F32)).astype(BF16)


def _dot3_raw(a, b, dims):
    a_hi, a_lo = _split_bf16(a)
    b_hi, b_lo = _split_bf16(b)
    dot = lambda x_, y_: lax.dot_general(x_, y_, dims, preferred_element_type=F32)
    return dot(a_hi, b_hi) + (dot(a_hi, b_lo) + dot(a_lo, b_hi))


@functools.partial(jax.custom_vjp, nondiff_argnums=(2, 3))
def _dot3(a, b, nt, exact_bwd=True):
    return _dot3_raw(a, b, _BNT if nt else _BNN)


def _dot3_fwd(a, b, nt, exact_bwd):
    return _dot3_raw(a, b, _BNT if nt else _BNN), (a, b)


def _dot3_bwd(nt, exact_bwd, res, g):
    a, b = res
    if exact_bwd:
        dot = _dot3_raw
    else:
        dot = lambda x_, y_, d_: lax.dot_general(x_.astype(BF16), y_.astype(BF16), d_, preferred_element_type=F32)
    if nt:
        return dot(g, b, _BNN), dot(jnp.swapaxes(g, 1, 2), a, _BNN)
    return dot(g, b, _BNT), dot(jnp.swapaxes(a, 1, 2), g, _BNN)


_dot3.defvjp(_dot3_fwd, _dot3_bwd)


@functools.partial(jax.custom_vjp, nondiff_argnums=(2,))
def _bdot_b(a, b, nt):
    return lax.dot_general(a.astype(BF16), b.astype(BF16), _BNT if nt else _BNN, preferred_element_type=F32)


def _bdot_b_fwd(a, b, nt):
    return _bdot_b(a, b, nt), (a, b)


def _bdot_b_bwd(nt, res, g):
    a, b = res
    dot = lambda x_, y_, d_: lax.dot_general(x_.astype(BF16), y_.astype(BF16), d_, preferred_element_type=F32)
    if nt:
        return dot(g, b, _BNN), dot(jnp.swapaxes(g, 1, 2), a, _BNN)
    return dot(g, b, _BNT), dot(jnp.swapaxes(a, 1, 2), g, _BNN)


_bdot_b.defvjp(_bdot_b_fwd, _bdot_b_bwd)


@jax.custom_vjp
def _inverse_given(a_mat, inv):
    return inv


def _inverse_given_bwd(inv, g):
    inv_t = jnp.swapaxes(inv, 1, 2)
    return -_dot3_raw(_dot3_raw(inv_t, g, _BNN), inv_t, _BNN), jnp.zeros_like(inv)


_inverse_given.defvjp(lambda a_mat, inv: (inv, inv), _inverse_given_bwd)


def _intra_batched(q, k, v, g_col, b_col, inv_known=None):
    c = CHUNK
    nb = q.shape[0]
    row = lax.broadcasted_iota(jnp.int32, (1, c, c), 1)
    col = lax.broadcasted_iota(jnp.int32, (1, c, c), 2)
    incl, strict, eye = row >= col, row > col, row == col
    tri = jnp.broadcast_to(jnp.where(incl, 1.0, 0.0).astype(F32), (nb, c, c))
    ident = jnp.where(eye, 1.0, 0.0).astype(F32)
    g_wide = _dot3(tri, jnp.broadcast_to(g_col, (nb, c, HEAD_DIM)), False)
    g_i = g_wide[:, :, :c]
    g_j = jnp.sum(jnp.where(eye, g_i, 0.0), axis=1, keepdims=True)
    decay = jnp.where(incl, jnp.exp(jnp.where(incl, g_i - g_j, 0.0)), 0.0)
    kk = _bdot_b(k, k, True)
    a_mat = jnp.where(strict, b_col * kk * decay, 0.0)
    if inv_known is None:
        x_pow = -a_mat
        inv = ident + x_pow
        for _ in range(5):
            x_pow = _dot3(x_pow, x_pow, False, False)
            inv = inv + _dot3(inv, x_pow, False, False)
    else:
        inv = _inverse_given(a_mat, inv_known)
    e_wide = jnp.exp(g_wide)
    u = _dot3(inv, v * b_col, False)
    wk = _dot3(inv, k * b_col * e_wide, False)
    qk = _bdot_b(q, k, True) * decay
    last = lax.broadcasted_iota(jnp.int32, (1, c, HEAD_DIM), 1) == c - 1
    g_last = jnp.sum(jnp.where(last, g_wide, 0.0), axis=1, keepdims=True)
    qd = q * e_wide
    kd = k * jnp.exp(g_last - g_wide)
    gl = jnp.broadcast_to(jnp.exp(g_last), (nb, 8, HEAD_DIM))
    return u, wk, qd, kd, qk, gl, inv


def _gdn_intra_fn(q, k, v, gb, *inv_known):
    t = q.shape[0]
    nch = t // CHUNK
    lane = lax.broadcasted_iota(jnp.int32, gb.shape, 1)

    def heads_first(x_):
        return jnp.concatenate([x_[:, HEAD_DIM * h:HEAD_DIM * (h + 1)].reshape(nch, CHUNK, HEAD_DIM) for h in range(HEADS)],
                               axis=0)

    def column(first_lane):
        return jnp.concatenate([jnp.sum(jnp.where(lane == first_lane + h, gb, 0.0), axis=1, keepdims=True)
                                .reshape(nch, CHUNK, 1) for h in range(HEADS)], axis=0)

    known = jnp.concatenate([x_.reshape(nch, CHUNK, CHUNK) for x_ in inv_known], axis=0) if inv_known else None
    u, wk, qd, kd, qk, gl, inv = _intra_batched(heads_first(q), heads_first(k), heads_first(v), column(64), column(68), known)

    def rows_first(x_):
        r, w_ = x_.shape[1], x_.shape[2]
        return jnp.concatenate([x_[nch * h:nch * (h + 1)].reshape(nch * r, w_) for h in range(HEADS)], axis=1)

    per_head = lambda x_: [x_[nch * h:nch * (h + 1)].reshape(t, CHUNK) for h in range(HEADS)]
    outs = (rows_first(u), rows_first(wk), rows_first(qd), rows_first(kd), *per_head(qk), rows_first(gl))
    return outs if inv_known else outs + tuple(per_head(inv))


def _scan_step(s0, u, wk, qd, kd, qk, gl):
    v_new = u - _bdot_b(wk, s0, False)
    o = _bdot_b(qd, s0, False) + _bdot_b(qk, v_new, False)
    s1 = s0 * gl[:, 0:1, :] + _bdot_b(jnp.swapaxes(kd, 1, 2), v_new, False)
    return o, s1


def _mix_post_fn(o_a, z, o_b, gnw, onw):
    parts = [_rms(o_a[:, HEAD_DIM * h:HEAD_DIM * (h + 1)], gnw) * _silu(z[:, HEAD_DIM * h:HEAD_DIM * (h + 1)])
             for h in range(HEADS)]
    parts += [_rms(o_b[:, HEAD_DIM * h:HEAD_DIM * (h + 1)], onw) for h in range(HEADS)]
    return (jnp.concatenate(parts, axis=1),)


def _mla_pre_fn(ckv, cq, kab, cos_p, sin_p, qnw, kvnw, wuq, wukv, qn_w, qr_w, kn_w, kr_w):
    scale = (HEAD_DIM + ROPE) ** -0.5 * LOG2_E
    qf = _bdot(_rms(cq, qnw), wuq, "nn")
    kvf = _bdot(_rms(ckv, kvnw), wukv, "nn")
    lane = lax.broadcasted_iota(jnp.int32, kab.shape, 1)
    kr = _rope128(_rms(jnp.where(lane < ROPE, kab, 0.0), kr_w, n=ROPE), cos_p, sin_p)
    qs, ks = [], []
    for h in range(HEADS):
        qn = _rms(qf[:, 256 * h:256 * h + 128], qn_w) * scale
        qr = _rope128(_rms(qf[:, 256 * h + 128:256 * h + 256], qr_w, n=ROPE), cos_p, sin_p) * scale
        qs += [qn, qr]
        ks += [_rms(kvf[:, 128 * h:128 * (h + 1)], kn_w), kr]
    return jnp.concatenate(qs, axis=1), jnp.concatenate(ks, axis=1), kvf[:, 512:]


def _conv_fwd(proj, conv_w, alog_p, dt_p, tm, name):
    S = proj.shape[0]
    C = 1536
    nb = tm // 8

    def body(x_ref, prev_ref, kab_ref, w_ref, al_ref, dt_ref, o_ref, q_ref, k_ref, v_ref, gb_ref, ext_ref):
        i = pl.program_id(0)
        ext_ref[0:8, :] = jnp.where(i > 0, prev_ref[...], 0.0)
        ext_ref[8:, :] = x_ref[...]
        acc = jnp.zeros((tm, C), F32)
        for k in range(4):
            acc = acc + w_ref[k:k + 1, :] * ext_ref[pl.ds(5 + k, tm), :]
        o_ref[...] = acc
        for ref, val in zip((q_ref, k_ref, v_ref, gb_ref), _gdn_pre_fn(acc, kab_ref[...], al_ref[...], dt_ref[...])):
            ref[...] = val

    third = pl.BlockSpec((tm, 512), lambda i: (i, 0))
    return pl.pallas_call(
        body, name=name, grid=(S // tm,),
        in_specs=[pl.BlockSpec((tm, C), lambda i: (i, 0)),
                  pl.BlockSpec((8, C), lambda i: (jnp.maximum(i * nb - 1, 0), 0)),
                  _row_spec(tm, 128, 21), pl.BlockSpec((4, C), lambda i: (0, 0)), _full_spec(alog_p.shape), _full_spec(dt_p.shape)],
        out_specs=[pl.BlockSpec((tm, C), lambda i: (i, 0)), third, third, third, _row_spec(tm, 128, 0)],
        out_shape=[jax.ShapeDtypeStruct((S, C), F32)] + [jax.ShapeDtypeStruct((S, 512), F32)] * 3
        + [jax.ShapeDtypeStruct((S, 128), F32)],
        scratch_shapes=[pltpu.VMEM((tm + 8, C), F32)],
        compiler_params=_params(("arbitrary",)),
    )(proj, proj, proj, conv_w, alog_p, dt_p)


def _conv_bwd(proj, qkvc, conv_w, alog_p, dt_p, douts, d_kab_add, tm, name):
    S = proj.shape[0]
    C = 1536
    nb = tm // 8
    n_steps = S // tm

    def body(x_ref, prev_ref, c_ref, kab_ref, w_ref, al_ref, dt_ref, dq_ref, dk_ref, dv_ref, dgb_ref, add_ref,
             dx_ref, dkab_ref, dw_ref, dal_ref, ddt_ref, xext_ref, dext_ref, halo_ref):
        step = pl.program_id(0)

        @pl.when(step == 0)
        def _():
            halo_ref[...] = jnp.zeros_like(halo_ref)
            for r in (dw_ref, dal_ref, ddt_ref):
                r[...] = jnp.zeros_like(r)

        _, vjp = jax.vjp(_gdn_pre_fn, c_ref[...], kab_ref[...], al_ref[...], dt_ref[...])
        d, d_kab, d_al, d_dt = vjp((dq_ref[...], dk_ref[...], dv_ref[...], dgb_ref[...]))
        dkab_ref[...] = (d_kab + add_ref[...]).astype(dkab_ref.dtype)
        xext_ref[0:8, :] = jnp.where(step < n_steps - 1, prev_ref[...], 0.0)
        xext_ref[8:, :] = x_ref[...]
        dext_ref[0:tm, :] = d
        dext_ref[tm:, :] = halo_ref[...]
        halo_ref[...] = d[0:8, :]
        acc = jnp.zeros((tm, C), F32)
        dws = []
        for k in range(4):
            acc = acc + w_ref[k:k + 1, :] * dext_ref[pl.ds(3 - k, tm), :]
            dws.append(jnp.sum(d * xext_ref[pl.ds(5 + k, tm), :], axis=0, keepdims=True))
        dx_ref[...] = acc.astype(dx_ref.dtype)
        dw_ref[...] += jnp.concatenate(dws + [jnp.zeros((4, C), F32)], axis=0)
        dal_ref[...] += d_al
        ddt_ref[...] += d_dt

    tile = lambda cols, block=0: pl.BlockSpec((tm, cols), lambda s_: (n_steps - 1 - s_, block))
    return pl.pallas_call(
        body, name=name, grid=(n_steps,),
        in_specs=[tile(C), pl.BlockSpec((8, C), lambda s_: (jnp.maximum((n_steps - 1 - s_) * nb - 1, 0), 0)),
                  tile(C), tile(128, 21), pl.BlockSpec((4, C), lambda s_: (0, 0)), _full_spec(alog_p.shape), _full_spec(dt_p.shape),
                  tile(512), tile(512), tile(512), tile(128), tile(128)],
        out_specs=[tile(C), tile(128), pl.BlockSpec((8, C), lambda s_: (0, 0)), _full_spec(alog_p.shape), _full_spec(dt_p.shape)],
        out_shape=[jax.ShapeDtypeStruct((S, C), BF16), jax.ShapeDtypeStruct((S, 128), BF16), jax.ShapeDtypeStruct((8, C), F32),
                   jax.ShapeDtypeStruct(alog_p.shape, F32), jax.ShapeDtypeStruct(dt_p.shape, F32)],
        scratch_shapes=[pltpu.VMEM((tm + 8, C), F32), pltpu.VMEM((tm + 8, C), F32), pltpu.VMEM((8, C), F32)],
        compiler_params=_params(("arbitrary",)),
    )(proj, proj, qkvc, proj, conv_w, alog_p, dt_p, *douts, d_kab_add)


SCAN_CHUNKS = (8, 4, 2, 1)


def _gdn_scan_fwd(u, wk, qd, kd, qks, gl, name):
    S = u.shape[0]
    nc = S // CHUNK
    cs = _pick(nc, SCAN_CHUNKS)
    W = HEADS * HEAD_DIM

    def body(u_ref, wk_ref, qd_ref, kd_ref, qk0, qk1, qk2, qk3, gl_ref, o_ref, sp_ref, s_ref):
        @pl.when(pl.program_id(0) == 0)
        def _():
            s_ref[...] = jnp.zeros_like(s_ref)

        state = s_ref[...]
        for c in range(cs):
            rows, gl_rows = slice(CHUNK * c, CHUNK * (c + 1)), slice(8 * c, 8 * (c + 1))
            sp_ref[c] = state
            o, state = _scan_step(state, _heads(u_ref, HEAD_DIM, rows), _heads(wk_ref, HEAD_DIM, rows),
                                  _heads(qd_ref, HEAD_DIM, rows), _heads(kd_ref, HEAD_DIM, rows),
                                  jnp.stack([r[rows, :] for r in (qk0, qk1, qk2, qk3)]), _heads(gl_ref, HEAD_DIM, gl_rows))
            for h in range(HEADS):
                o_ref[rows, HEAD_DIM * h:HEAD_DIM * (h + 1)] = o[h]
        s_ref[...] = state

    row = pl.BlockSpec((cs * CHUNK, W), lambda n: (n, 0))
    qk_spec = pl.BlockSpec((cs * CHUNK, CHUNK), lambda n: (n, 0))
    return pl.pallas_call(
        body, name=name, grid=(nc // cs,),
        in_specs=[row, row, row, row, qk_spec, qk_spec, qk_spec, qk_spec, pl.BlockSpec((cs * 8, W), lambda n: (n, 0))],
        out_specs=[row, pl.BlockSpec((cs, HEADS, HEAD_DIM, HEAD_DIM), lambda n: (n, 0, 0, 0))],
        out_shape=[jax.ShapeDtypeStruct((S, W), F32), jax.ShapeDtypeStruct((nc, HEADS, HEAD_DIM, HEAD_DIM), F32)],
        scratch_shapes=[pltpu.VMEM((HEADS, HEAD_DIM, HEAD_DIM), F32)],
        compiler_params=_params(("arbitrary",)),
    )(u, wk, qd, kd, *qks, gl)


def _gdn_scan_bwd(u, wk, qd, kd, qks, gl, s_prev, d_o, name):
    S = u.shape[0]
    nc = S // CHUNK
    cs = _pick(nc, SCAN_CHUNKS)
    nb = nc // cs
    W = HEADS * HEAD_DIM

    def body(u_ref, wk_ref, qd_ref, kd_ref, qk0, qk1, qk2, qk3, gl_ref, sp_ref, do_ref,
             du_ref, dwk_ref, dqd_ref, dkd_ref, dqk0, dqk1, dqk2, dqk3, dgl_ref, ds_ref):
        @pl.when(pl.program_id(0) == 0)
        def _():
            ds_ref[...] = jnp.zeros_like(ds_ref)

        d_state = ds_ref[...]
        for c in reversed(range(cs)):
            rows, gl_rows = slice(CHUNK * c, CHUNK * (c + 1)), slice(8 * c, 8 * (c + 1))
            _, vjp = jax.vjp(_scan_step, sp_ref[c], _heads(u_ref, HEAD_DIM, rows), _heads(wk_ref, HEAD_DIM, rows),
                             _heads(qd_ref, HEAD_DIM, rows), _heads(kd_ref, HEAD_DIM, rows),
                             jnp.stack([r[rows, :] for r in (qk0, qk1, qk2, qk3)]), _heads(gl_ref, HEAD_DIM, gl_rows))
            d_state, du, dwk, dqd, dkd, dqk, dgl = vjp((_heads(do_ref, HEAD_DIM, rows), d_state))
            for h, dqk_ref in enumerate((dqk0, dqk1, dqk2, dqk3)):
                sl = slice(HEAD_DIM * h, HEAD_DIM * (h + 1))
                du_ref[rows, sl] = du[h]
                dwk_ref[rows, sl] = dwk[h]
                dqd_ref[rows, sl] = dqd[h]
                dkd_ref[rows, sl] = dkd[h]
                dqk_ref[rows, :] = dqk[h]
                dgl_ref[gl_rows, sl] = dgl[h]
        ds_ref[...] = d_state

    rev = lambda n: (nb - 1 - n, 0)
    row = pl.BlockSpec((cs * CHUNK, W), rev)
    qk_spec = pl.BlockSpec((cs * CHUNK, CHUNK), rev)
    gl_spec = pl.BlockSpec((cs * 8, W), rev)
    qk_shape = jax.ShapeDtypeStruct((S, CHUNK), F32)
    row_shape = jax.ShapeDtypeStruct((S, W), F32)
    return pl.pallas_call(
        body, name=name, grid=(nb,),
        in_specs=[row, row, row, row, qk_spec, qk_spec, qk_spec, qk_spec, gl_spec,
                  pl.BlockSpec((cs, HEADS, HEAD_DIM, HEAD_DIM), lambda n: (nb - 1 - n, 0, 0, 0)), row],
        out_specs=[row, row, row, row, qk_spec, qk_spec, qk_spec, qk_spec, gl_spec],
        out_shape=[row_shape] * 4 + [qk_shape] * 4 + [jax.ShapeDtypeStruct((nc * 8, W), F32)],
        scratch_shapes=[pltpu.VMEM((HEADS, HEAD_DIM, HEAD_DIM), F32)],
        compiler_params=_params(("arbitrary",)),
    )(u, wk, qd, kd, *qks, gl, s_prev, d_o)


NEG = -1e30


def _chunk_mask(i, j, t, transposed=False):
    q_axis, k_axis = (1, 0) if transposed else (0, 1)
    r = (i * t + lax.broadcasted_iota(jnp.int32, (t, t), q_axis)) // CHUNK
    c = (j * t + lax.broadcasted_iota(jnp.int32, (t, t), k_axis)) // CHUNK
    return c <= r


def _tile_pairs(n, by_key):
    pairs = [(i, j) for j in range(n) for i in range(j, n)] if by_key else [(i, j) for i in range(n) for j in range(i + 1)]
    return jnp.asarray(np.array([p[0] for p in pairs], np.int32)), jnp.asarray(np.array([p[1] for p in pairs], np.int32))


def _heads(ref, width, rows=slice(None)):
    return jnp.stack([ref[rows, width * h:width * (h + 1)] for h in range(HEADS)])


def _bmm(a, b, dims):
    return lax.dot_general(a.astype(BF16), b.astype(BF16), dims, preferred_element_type=F32)


def _attn_fwd(q, k, v_t, t, name):
    S = q.shape[0]
    n = S // t
    qi, kj = _tile_pairs(n, by_key=False)

    def body(qi_ref, kj_ref, q_ref, k_ref, vt_ref, o_ref, lse_ref, m_ref, l_ref, acc_ref):
        i, j = qi_ref[pl.program_id(0)], kj_ref[pl.program_id(0)]

        @pl.when(j == 0)
        def _():
            m_ref[...] = jnp.full_like(m_ref, NEG)
            l_ref[...] = jnp.zeros_like(l_ref)
            acc_ref[...] = jnp.zeros_like(acc_ref)

        def update(masked):
            s_t = _bmm(_heads(k_ref, 256), _heads(q_ref, 256), _BNT)
            if masked:
                s_t = jnp.where(_chunk_mask(i, j, t, transposed=True)[None], s_t, NEG)
            m_old = m_ref[...]
            m_new = jnp.maximum(m_old, jnp.max(s_t, axis=1, keepdims=True))
            p_t = jnp.exp2(s_t - m_new)
            alpha = jnp.exp2(m_old - m_new)
            l_ref[...] = alpha * l_ref[...] + jnp.sum(p_t, axis=1, keepdims=True)
            v_heads = jnp.stack([vt_ref[HEAD_DIM * h:HEAD_DIM * (h + 1), :] for h in range(HEADS)])
            acc_ref[...] = alpha * acc_ref[...] + _bmm(v_heads, p_t, _BNN)
            m_ref[...] = m_new

        @pl.when(j < i)
        def _():
            update(False)

        @pl.when(j == i)
        def _():
            update(True)
            for h in range(HEADS):
                sl = slice(HEAD_DIM * h, HEAD_DIM * (h + 1))
                o_ref[:, sl] = jnp.transpose(acc_ref[h] / l_ref[h])
                lse_ref[h:h + 1, :] = m_ref[h] + jnp.log(l_ref[h]) * LOG2_E
            lse_ref[HEADS:, :] = jnp.zeros((8 - HEADS, t), F32)

    row = lambda p, qi_, kj_: (qi_[p], 0)
    return pl.pallas_call(
        body, name=name,
        grid_spec=pltpu.PrefetchScalarGridSpec(
            num_scalar_prefetch=2, grid=(qi.shape[0],),
            in_specs=[pl.BlockSpec((t, HEADS * 256), row), pl.BlockSpec((t, HEADS * 256), lambda p, qi_, kj_: (kj_[p], 0)),
                      pl.BlockSpec((HEADS * HEAD_DIM, t), lambda p, qi_, kj_: (0, kj_[p]))],
            out_specs=[pl.BlockSpec((t, HEADS * HEAD_DIM), row), pl.BlockSpec((8, t), lambda p, qi_, kj_: (0, qi_[p]))],
            scratch_shapes=[pltpu.VMEM((HEADS, 1, t), F32), pltpu.VMEM((HEADS, 1, t), F32),
                            pltpu.VMEM((HEADS, HEAD_DIM, t), F32)]),
        out_shape=[jax.ShapeDtypeStruct((S, HEADS * HEAD_DIM), F32), jax.ShapeDtypeStruct((8, S), F32)],
        compiler_params=_params(("arbitrary",)),
    )(qi, kj, q, k, v_t)


def _attn_stats(o, d_o, lse):
    lane = lax.broadcasted_iota(jnp.int32, (o.shape[0], HEAD_DIM), 1)
    stats = jnp.zeros((o.shape[0], HEAD_DIM), F32)
    for h in range(HEADS):
        sl = slice(HEAD_DIM * h, HEAD_DIM * (h + 1))
        delta = jnp.sum(d_o[:, sl] * o[:, sl], axis=1, keepdims=True)
        stats = stats + jnp.where(lane == HEADS + h, delta, 0.0)
    return lse + jnp.transpose(stats)[0:8, :]


BWD_GROUP = 2


def _attn_bwd(q, k, v, d_o, stats, t, name):
    S = q.shape[0]
    n = S // t
    groups = HEADS // BWD_GROUP
    gq, gv = BWD_GROUP * 256, BWD_GROUP * HEAD_DIM
    qi, kj = _tile_pairs(n, by_key=True)
    n_pairs = qi.shape[0]
    st = stats.reshape(2, groups, BWD_GROUP, S).transpose(1, 0, 2, 3).reshape(groups, 2 * BWD_GROUP, S)
    st = jnp.pad(st, ((0, 0), (0, 8 - 2 * BWD_GROUP), (0, 0)))

    def heads(ref, width, rows=slice(None)):
        return jnp.stack([ref[rows, width * h:width * (h + 1)] for h in range(BWD_GROUP)])

    def body(qi_ref, kj_ref, q_ref, k_ref, v_ref, do_ref, st_ref, dq_hbm, dk_ref, dv_ref, dq_acc, sem):
        g, p = pl.program_id(0), pl.program_id(1)
        i, j = qi_ref[p], kj_ref[p]

        @pl.when(i == j)
        def _():
            dk_ref[...] = jnp.zeros_like(dk_ref)
            dv_ref[...] = jnp.zeros_like(dv_ref)

        def update(masked):
            qh, kh = heads(q_ref, 256), heads(k_ref, 256)
            d_out = heads(do_ref, HEAD_DIM)
            stv = st_ref[...]
            lse_row = jnp.stack([stv[h:h + 1, :] for h in range(BWD_GROUP)])
            delta_row = jnp.stack([stv[BWD_GROUP + h:BWD_GROUP + h + 1, :] for h in range(BWD_GROUP)])
            s_t = _bmm(kh, qh, _BNT)
            p_t = jnp.exp2(s_t - lse_row)
            if masked:
                p_t = jnp.where(_chunk_mask(i, j, t, transposed=True)[None], p_t, 0.0)
            dv = _bmm(p_t, d_out, _BNN)
            dp_t = _bmm(heads(v_ref, HEAD_DIM), d_out, _BNT)
            ds_t = p_t * (dp_t - delta_row)
            dk = _bmm(ds_t, qh, _BNN)
            dq = _bmm(ds_t, kh, _BTN)
            rows = pl.ds(pl.multiple_of(i * t, t), t)
            for h in range(BWD_GROUP):
                dk_ref[:, 256 * h:256 * (h + 1)] += dk[h]
                dv_ref[:, HEAD_DIM * h:HEAD_DIM * (h + 1)] += dv[h]

            @pl.when(j == 0)
            def _():
                for h in range(BWD_GROUP):
                    dq_acc[rows, 256 * h:256 * (h + 1)] = dq[h]

            @pl.when(j > 0)
            def _():
                for h in range(BWD_GROUP):
                    dq_acc[rows, 256 * h:256 * (h + 1)] += dq[h]

        def dq_copy(block, gg):
            rows = pl.ds(pl.multiple_of(block * t, t), t)
            return pltpu.make_async_copy(dq_acc.at[rows, :], dq_hbm.at[rows, gq * gg:gq * (gg + 1)], sem.at[block])

        @pl.when(i == j)
        def _():
            update(True)
            rows = pl.ds(pl.multiple_of(i * t, t), t)
            dq_acc[rows, :] *= 1.0 / LOG2_E
            for gg in range(groups):
                @pl.when(g == gg)
                def _():
                    dq_copy(i, gg).start()

        @pl.when(i > j)
        def _():
            update(False)

        @pl.when(i == n - 1)
        def _():
            dk_ref[...] *= 1.0 / LOG2_E

        @pl.when(p == n_pairs - 1)
        def _():
            for gg in range(groups):
                @pl.when(g == gg)
                def _():
                    for block in range(n):
                        dq_copy(block, gg).wait()

    q_blk = lambda g, p, qi_, kj_: (qi_[p], g)
    k_blk = lambda g, p, qi_, kj_: (kj_[p], g)
    return pl.pallas_call(
        body, name=name,
        grid_spec=pltpu.PrefetchScalarGridSpec(
            num_scalar_prefetch=2, grid=(groups, n_pairs),
            in_specs=[pl.BlockSpec((t, gq), q_blk), pl.BlockSpec((t, gq), k_blk), pl.BlockSpec((t, gv), k_blk),
                      pl.BlockSpec((t, gv), q_blk), pl.BlockSpec((None, 8, t), lambda g, p, qi_, kj_: (g, 0, qi_[p]))],
            out_specs=[pl.BlockSpec(memory_space=pl.ANY), pl.BlockSpec((t, gq), k_blk), pl.BlockSpec((t, gv), k_blk)],
            scratch_shapes=[pltpu.VMEM((S, gq), F32), pltpu.SemaphoreType.DMA((n,))]),
        out_shape=[jax.ShapeDtypeStruct((S, HEADS * 256), F32), jax.ShapeDtypeStruct((S, HEADS * 256), F32),
                   jax.ShapeDtypeStruct((S, HEADS * HEAD_DIM), F32)],
        compiler_params=_params(("arbitrary", "arbitrary")),
    )(qi, kj, q, k, v, d_o, st)


FFN_PIECE = 2 * D_FF // N_DEV
HID_PIECES = D_FF // FFN_PIECE


def _after_specs(after):
    return [] if after is None else [pl.BlockSpec(memory_space=pl.ANY)]


def _after_args(after):
    return [] if after is None else [after]


def _ffn_gw8(h, d_gate, d_up, name, after=None):
    S = h.shape[0]
    tm = 512
    tk = _pick(S, MATMUL_ROWS)
    nk = S // tk

    def body(h_ref, dg_ref, du_ref, *rest):
        o_ref, acc_ref = rest[-2:]
        k = pl.program_id(1)

        @pl.when(k == 0)
        def _():
            acc_ref[...] = jnp.zeros_like(acc_ref)

        h_t = jnp.transpose(h_ref[...])
        for p in range(HID_PIECES):
            acc_ref[p] += _dot_raw(h_t, dg_ref[p], "nn")
            acc_ref[HID_PIECES + p] += _dot_raw(h_t, du_ref[p], "nn")

        @pl.when(k == nk - 1)
        def _():
            o_ref[...] = acc_ref[...].astype(o_ref.dtype)

    d_spec = pl.BlockSpec((HID_PIECES, tk, FFN_PIECE), lambda i, k: (0, k, 0))
    return pl.pallas_call(
        body, name=name, grid=(D_MODEL // tm, nk),
        in_specs=[pl.BlockSpec((tk, tm), lambda i, k: (k, i)), d_spec, d_spec] + _after_specs(after),
        out_specs=pl.BlockSpec((2 * HID_PIECES, tm, FFN_PIECE), lambda i, k: (0, i, 0)),
        out_shape=jax.ShapeDtypeStruct((2 * HID_PIECES, D_MODEL, FFN_PIECE), BF16),
        scratch_shapes=[pltpu.VMEM((2 * HID_PIECES, tm, FFN_PIECE), F32)],
        compiler_params=_params(("parallel", "arbitrary")),
    )(h, d_gate, d_up, *_after_args(after))


EPILOGUE_ROWS = 256


def _dmod_epilogue(acc_ref, x_ref, do_ref, sc_ref, sh_ref, dx_ref, dsc_ref, dsh_ref, below, below_in, below_out):
    rows_total = acc_ref.shape[0]
    step = min(EPILOGUE_ROWS, rows_total)
    dsc, dsh, dg = 0.0, 0.0, 0.0
    for r in range(rows_total // step):
        rows = slice(step * r, step * (r + 1))
        _, vjp = jax.vjp(_modulate, x_ref[rows, :], sc_ref[...], sh_ref[...])
        dx, dsc_r, dsh_r = vjp(acc_ref[rows, :])
        dx = dx + do_ref[rows, :]
        dx_ref[rows, :] = dx
        dsc, dsh = dsc + dsc_r, dsh + dsh_r
        if below is not None:
            coef = below[2]
            below_out[0][rows, :] = (coef * below_in[1][...] * dx).astype(below_out[0].dtype)
            dg = dg + jnp.sum(coef * below_in[0][rows, :] * dx, axis=0, keepdims=True)
    dsc_ref[...] += dsc
    dsh_ref[...] += dsh
    if below is not None:
        below_out[1][...] += dg


def _ffn_dh(d_gate, d_up, w8, x, d_out, scale, shift, name, after=None, below=None):
    S = d_gate.shape[1]
    tm = _pick(S, MATMUL_ROWS)
    n_below = 0 if below is None else 2

    def body(dg_ref, du_ref, wg_ref, wu_ref, x_ref, do_ref, sc_ref, sh_ref, *rest):
        below_in = rest[:n_below]
        outs = rest[len(rest) - 4 - n_below:]
        dx_ref, dsc_ref, dsh_ref = outs[:3]
        below_out, acc_ref = outs[3:3 + n_below], outs[-1]
        i, k = pl.program_id(0), pl.program_id(1)

        @pl.when(k == 0)
        def _():
            acc_ref[...] = jnp.zeros_like(acc_ref)

        acc_ref[...] += _dot_raw(dg_ref[...], wg_ref[...], "nt") + _dot_raw(du_ref[...], wu_ref[...], "nt")

        @pl.when((k == 0) & (i == 0))
        def _():
            for r in (dsc_ref, dsh_ref) + tuple(below_out[1:]):
                r[...] = jnp.zeros_like(r)

        @pl.when(k == HID_PIECES - 1)
        def _():
            _dmod_epilogue(acc_ref, x_ref, do_ref, sc_ref, sh_ref, dx_ref, dsc_ref, dsh_ref, below, below_in, below_out)

    d_spec = pl.BlockSpec((None, tm, FFN_PIECE), lambda i, k: (k, i, 0))
    row = pl.BlockSpec((tm, D_MODEL), lambda i, k: (i, 0))
    par = pl.BlockSpec((1, D_MODEL), lambda i, k: (0, 0))
    row_shape, par_shape = jax.ShapeDtypeStruct((S, D_MODEL), F32), jax.ShapeDtypeStruct((1, D_MODEL), F32)
    return pl.pallas_call(
        body, name=name, grid=(S // tm, HID_PIECES),
        in_specs=[d_spec, d_spec,
                  pl.BlockSpec((None, D_MODEL, FFN_PIECE), lambda i, k: (k, 0, 0)),
                  pl.BlockSpec((None, D_MODEL, FFN_PIECE), lambda i, k: (k + HID_PIECES, 0, 0)),
                  row, row, par, par] + [row, par][:n_below] + _after_specs(after),
        out_specs=[row, par, par] + [row, par][:n_below],
        out_shape=[row_shape, par_shape, par_shape] + [jax.ShapeDtypeStruct((S, D_MODEL), BF16), par_shape][:n_below],
        scratch_shapes=[pltpu.VMEM((tm, D_MODEL), F32)],
        compiler_params=_params(("arbitrary", "arbitrary")),
    )(d_gate, d_up, w8, w8, x, d_out, scale, shift, *(below[:2] if below is not None else ()), *_after_args(after))


def _swiglu_bwd(d_hid, hid_by_gate, hid_by_up):
    return d_hid * hid_by_gate, d_hid * hid_by_up


def _adamw_math(w_, g_, m_, v_):
    m_ = ADAM_B1 * m_ + (1.0 - ADAM_B1) * g_
    v_ = ADAM_B2 * v_ + (1.0 - ADAM_B2) * (g_ * g_)
    m_hat = m_ / (1.0 - ADAM_B1 ** ADAM_STEP)
    v_hat = v_ / (1.0 - ADAM_B2 ** ADAM_STEP)
    return -ADAM_LR * (m_hat / (jnp.sqrt(v_hat) + ADAM_EPS) + ADAM_WD * w_), m_, v_


def _adamw(w, g, m, v, name):
    R, C = w.shape
    tr = _pick(R, (256, 176, 128, 64, 32, 16, 8))

    def body(w_ref, g_ref, m_ref, v_ref, d_ref, nm_ref, nv_ref):
        d_ref[...], nm_ref[...], nv_ref[...] = _adamw_math(w_ref[...], g_ref[...], m_ref[...], v_ref[...])

    spec = pl.BlockSpec((tr, C), lambda i: (i, 0))
    return pl.pallas_call(
        body, name=name, grid=(R // tr,),
        in_specs=[spec] * 4, out_specs=[spec] * 3,
        out_shape=[jax.ShapeDtypeStruct((R, C), F32)] * 3,
        compiler_params=_params(("parallel",)),
    )(w, g, m, v)


def _sum_adamw(parts, w, m, v, name, transposed=False):
    _, R, C = parts.shape
    tr = _pick(R, (256, 176, 128, 64, 32, 16, 8))

    def body(p_ref, w_ref, m_ref, v_ref, g_ref, d_ref, nm_ref, nv_ref):
        g_ = p_ref[0].astype(F32)
        for d in range(1, N_DEV):
            g_ = g_ + p_ref[d].astype(F32)
        if transposed:
            g_ = jnp.transpose(g_)
        g_ref[...] = g_
        d_ref[...], nm_ref[...], nv_ref[...] = _adamw_math(w_ref[...], g_, m_ref[...], v_ref[...])

    spec = pl.BlockSpec((C, tr), lambda i: (0, i)) if transposed else pl.BlockSpec((tr, C), lambda i: (i, 0))
    return pl.pallas_call(
        body, name=name, grid=(R // tr,),
        in_specs=[pl.BlockSpec((N_DEV, tr, C), lambda i: (0, i, 0)), spec, spec, spec], out_specs=[spec] * 4,
        out_shape=[jax.ShapeDtypeStruct(w.shape, F32)] * 4,
        compiler_params=_params(("parallel",)),
    )(parts, w, m, v)


def _sum_devices(parts, name):
    _, R, C = parts.shape
    tr = _pick(R, (512, 256, 176, 128, 64, 32, 16, 8))

    def body(p_ref, o_ref):
        acc = p_ref[0].astype(F32)
        for d in range(1, N_DEV):
            acc = acc + p_ref[d].astype(F32)
        o_ref[...] = acc

    return pl.pallas_call(
        body, name=name, grid=(R // tr,),
        in_specs=[pl.BlockSpec((N_DEV, tr, C), lambda i: (0, i, 0))],
        out_specs=pl.BlockSpec((tr, C), lambda i: (i, 0)),
        out_shape=jax.ShapeDtypeStruct((R, C), F32),
        compiler_params=_params(("parallel",)),
    )(parts)


def _my_place():
    return lax.axis_index("x"), lax.axis_index("y"), lax.axis_index("c")


def _all_gather(blocks, name):
    n = len(blocks)

    def body(*refs):
        x_refs, out_refs = refs[:n], refs[n:2 * n]
        send_sems, recv_sems, local_sems = refs[2 * n:]
        x, y, c = _my_place()
        me, sibling = (x, y, c), (x, y, 1 - c)
        chips = [(1 - x, y), (x, 1 - y), (1 - x, 1 - y)]

        def copy(a, k, blk, to, own=False):
            slot = out_refs[a].at[4 * blk[0] + 2 * blk[1] + blk[2]]
            return pltpu.make_async_remote_copy(
                src_ref=x_refs[a] if own else slot, dst_ref=slot,
                send_sem=send_sems.at[7 * a + k], recv_sem=recv_sems.at[7 * a + k], device_id=to, device_id_type=MESH)

        mine = [pltpu.make_async_copy(x_refs[a], out_refs[a].at[4 * x + 2 * y + c], local_sems.at[a]) for a in range(n)]
        for cp in mine:
            cp.start()
        first = []
        for j, chip in enumerate(chips):
            first += [copy(a, 1 + j, me, (*chip, c), own=True) for a in range(n)]
        first += [copy(a, 0, me, sibling, own=True) for a in range(n)]
        for cp in first:
            cp.start()
        passed = []
        for j, chip in enumerate(chips):
            for a in range(n):
                copy(a, 1 + j, (*chip, c), me).wait_recv()
                passed.append(copy(a, 4 + j, (*chip, c), sibling))
                passed[-1].start()
        for a in range(n):
            copy(a, 0, sibling, me).wait_recv()
        for j, chip in enumerate(chips):
            for a in range(n):
                copy(a, 4 + j, (*chip, 1 - c), me).wait_recv()
        for cp in first + passed:
            cp.wait_send()
        for cp in mine:
            cp.wait()

    return pl.pallas_call(
        body, name=name,
        out_shape=[jax.ShapeDtypeStruct((N_DEV,) + b.shape, b.dtype) for b in blocks],
        in_specs=[pl.BlockSpec(memory_space=pl.ANY)] * n,
        out_specs=[pl.BlockSpec(memory_space=pl.ANY)] * n,
        scratch_shapes=[pltpu.SemaphoreType.DMA((7 * n,)), pltpu.SemaphoreType.DMA((7 * n,)), pltpu.SemaphoreType.DMA((n,))],
    )(*blocks)


def _all_to_all(pieces, name):
    n = len(pieces)

    def body(*refs):
        x_refs, out_refs = refs[:n], refs[n:2 * n]
        send_sems, recv_sems, local_sems = refs[2 * n:]
        x, y, c = _my_place()
        me = 4 * x + 2 * y + c
        mine = [pltpu.make_async_copy(x_refs[a].at[me], out_refs[a].at[me], local_sems.at[a]) for a in range(n)]
        for cp in mine:
            cp.start()
        copies = []
        for k in (2, 4, 6, 3, 5, 7, 1):
            px = 1 - x if k & 4 else x
            py = 1 - y if k & 2 else y
            pc = 1 - c if k & 1 else c
            peer = 4 * px + 2 * py + pc
            for a in range(n):
                copies.append(pltpu.make_async_remote_copy(
                    src_ref=x_refs[a].at[peer], dst_ref=out_refs[a].at[me],
                    send_sem=send_sems.at[7 * a + k - 1], recv_sem=recv_sems.at[7 * a + k - 1],
                    device_id=(px, py, pc), device_id_type=MESH))
        for cp in copies:
            cp.start()
        for cp in copies:
            cp.wait_recv()
        for cp in copies:
            cp.wait_send()
        for cp in mine:
            cp.wait()

    return pl.pallas_call(
        body, name=name,
        out_shape=[jax.ShapeDtypeStruct(p.shape, p.dtype) for p in pieces],
        in_specs=[pl.BlockSpec(memory_space=pl.ANY)] * n,
        out_specs=[pl.BlockSpec(memory_space=pl.ANY)] * n,
        scratch_shapes=[pltpu.SemaphoreType.DMA((7 * n,)), pltpu.SemaphoreType.DMA((7 * n,)), pltpu.SemaphoreType.DMA((n,))],
    )(*pieces)


def _peers():
    x, y, c = _my_place()
    out = []
    for k in (2, 4, 6, 3, 5, 7, 1):
        px = 1 - x if k & 4 else x
        py = 1 - y if k & 2 else y
        pc = 1 - c if k & 1 else c
        out.append((k, (px, py, pc), 4 * px + 2 * py + pc))
    return out


def _exchange_copies(x_refs, land_refs, send_sems, recv_sems, scatter):
    x, y, c = _my_place()
    me = 4 * x + 2 * y + c
    starts, arrivals = [], []
    for k, place, peer in _peers():
        for a, (x_ref, land_ref) in enumerate(zip(x_refs, land_refs)):
            sems = dict(send_sem=send_sems.at[7 * a + k - 1], recv_sem=recv_sems.at[7 * a + k - 1],
                        device_id=place, device_id_type=MESH)
            src = x_ref.at[peer] if scatter else x_ref
            starts.append(pltpu.make_async_remote_copy(src_ref=src, dst_ref=land_ref.at[me], **sems))
            arrivals.append(pltpu.make_async_remote_copy(src_ref=src, dst_ref=land_ref.at[peer], **sems))
    return starts, arrivals


def _exchange_start(arrays, scatter, name):
    n = len(arrays)
    hbm = pl.BlockSpec(memory_space=pltpu.HBM)
    sem = pl.BlockSpec(memory_space=pltpu.SEMAPHORE)
    lands = [lax.empty(a.shape if scatter else (N_DEV,) + a.shape, a.dtype) for a in arrays]

    def body(*refs):
        x_refs, land_refs = refs[:n], refs[n:2 * n]
        send_sems, recv_sems = refs[2 * n], refs[2 * n + 1]
        token = refs[-1]
        starts, _ = _exchange_copies(x_refs, land_refs, send_sems, recv_sems, scatter)
        for cp in starts:
            cp.start()
        token[...] = jnp.zeros_like(token)

    res = pl.pallas_call(
        body, name=name,
        out_shape=(pltpu.SemaphoreType.DMA((7 * n,)), pltpu.SemaphoreType.DMA((7 * n,)),
                   *[pltpu.HBM(a.shape, a.dtype) for a in arrays], *[pltpu.HBM(l.shape, l.dtype) for l in lands],
                   jax.ShapeDtypeStruct((8, 128), F32)),
        in_specs=[hbm] * (2 * n),
        out_specs=(sem, sem, *[hbm] * (2 * n), pl.BlockSpec(memory_space=pltpu.VMEM)),
        input_output_aliases={i: 2 + i for i in range(2 * n)},
        compiler_params=pltpu.CompilerParams(has_side_effects=pltpu.SideEffectType.DATAFLOW_SIDE_EFFECTING),
    )(*[pltpu.with_memory_space_constraint(a, pltpu.HBM) for a in arrays],
      *[pltpu.with_memory_space_constraint(l, pltpu.HBM) for l in lands])
    return res[0], res[1], list(res[2:2 + n]), list(res[2 + n:2 + 2 * n]), res[-1]


def _exchange_wait(handles, scatter, after, name):
    send_sems, recv_sems, arrays, lands, _ = handles
    n = len(arrays)
    hbm = pl.BlockSpec(memory_space=pltpu.HBM)
    sem = pl.BlockSpec(memory_space=pltpu.SEMAPHORE)

    def body(*refs):
        x_refs, land_refs = refs[:n], refs[n:2 * n]
        send_s, recv_s = refs[2 * n], refs[2 * n + 1]
        starts, arrivals = _exchange_copies(x_refs, land_refs, send_s, recv_s, scatter)
        for cp in arrivals:
            cp.wait_recv()
        for cp in starts:
            cp.wait_send()

    res = pl.pallas_call(
        body, name=name,
        out_shape=(*[pltpu.HBM(a.shape, a.dtype) for a in arrays], *[pltpu.HBM(l.shape, l.dtype) for l in lands]),
        in_specs=[hbm] * (2 * n) + [sem, sem, pl.BlockSpec(memory_space=pl.ANY)],
        out_specs=tuple([hbm] * (2 * n)),
        input_output_aliases={i: i for i in range(2 * n)},
        compiler_params=pltpu.CompilerParams(has_side_effects=pltpu.SideEffectType.DATAFLOW_SIDE_EFFECTING),
    )(*arrays, *lands, send_sems, recv_sems, after)
    me = 4 * lax.axis_index("x") + 2 * lax.axis_index("y") + lax.axis_index("c")
    out = []
    for src, got in zip(res[:n], res[n:]):
        zeros = (0,) * (got.ndim - 1)
        own = lax.dynamic_slice(src, (me,) + zeros, (1,) + src.shape[1:]) if scatter else src[None]
        out.append(lax.dynamic_update_slice(got, own, (me,) + zeros))
    return out


def _pad_lanes(v, at=0, width=128):
    return jnp.pad(v, ((0, 0), (at, width - at - v.shape[1])))


def _pack_weights(P):
    W = {}
    w = P["w_in"]
    W["wp"] = jnp.concatenate([w[:, :2048], w[:, 2440:2696], w[:, 2056:2440], w[:, 2696:2760], w[:, 2048:2056],
                               jnp.zeros((D_MODEL, N_IN_PACKED - N_IN), w.dtype)], axis=1).astype(BF16)
    W["conv_w"] = P["gdn_conv_w"].astype(F32)
    W["alog_p"] = _pad_lanes(P["gdn_a_log"], 64)
    W["dt_p"] = _pad_lanes(P["gdn_dt_bias"], 64)
    W["gnw"] = P["gdn_norm_w"]
    W["qnw"] = P["mla_q_norm_w"]
    W["kvnw"] = P["mla_kv_norm_w"]
    uq = P["mla_w_uq"].reshape(Q_LORA, HEADS, HEAD_DIM + ROPE)
    W["wuq"] = jnp.pad(uq, ((0, 0), (0, 0), (0, 256 - HEAD_DIM - ROPE))).reshape(Q_LORA, HEADS * 256).astype(BF16)
    ukv = P["mla_w_ukv"].reshape(KV_LORA, HEADS, 2, HEAD_DIM)
    W["wukv"] = ukv.transpose(0, 2, 1, 3).reshape(KV_LORA, 2 * HEADS * HEAD_DIM).astype(BF16)
    W["qn_w"] = P["qkn_q_nope"]
    W["qr_w"] = _pad_lanes(P["qkn_q_rope"])
    W["kn_w"] = P["qkn_k_nope"]
    W["kr_w"] = _pad_lanes(P["qkn_k_rope"])
    W["onw"] = P["mla_out_norm_w"]
    W["wout"] = P["w_out"].astype(BF16)
    return W


def _unpack_grads(G):
    g_qkv, g_z, g_ckv, g_cq, g_kab = G["wp"]
    uq = G["wuq"].reshape(Q_LORA, HEADS, 256)[:, :, :HEAD_DIM + ROPE].reshape(Q_LORA, HEADS * (HEAD_DIM + ROPE))
    ukv = G["wukv"].reshape(KV_LORA, 2, HEADS, HEAD_DIM).transpose(0, 2, 1, 3).reshape(KV_LORA, 2 * HEADS * HEAD_DIM)
    return {
        "w_in": jnp.concatenate([g_qkv, g_z, g_kab[:, ROPE:ROPE + 8], g_cq, g_ckv, g_kab[:, :ROPE]], axis=1),
        "gdn_conv_w": G["conv_w"], "gdn_a_log": G["alog_p"][:, 64:68], "gdn_dt_bias": G["dt_p"][:, 64:68],
        "gdn_norm_w": G["gnw"], "mla_q_norm_w": G["qnw"], "mla_w_uq": uq, "mla_kv_norm_w": G["kvnw"], "mla_w_ukv": ukv,
        "qkn_q_nope": G["qn_w"], "qkn_q_rope": G["qr_w"][:, :ROPE], "qkn_k_nope": G["kn_w"], "qkn_k_rope": G["kr_w"][:, :ROPE],
        "mla_out_norm_w": G["onw"], "w_out": G["wout"],
    }


def _rope_tables(positions):
    half = ROPE // 2
    inv_freq = ROPE_BASE ** (-jnp.arange(half, dtype=F32) / half)
    ang = positions.astype(F32)[:, None] * inv_freq
    cos, sin = jnp.cos(ang), jnp.sin(ang)
    zeros = jnp.zeros((positions.shape[0], 128 - ROPE), F32)
    return jnp.concatenate([cos, cos, zeros], axis=1), jnp.concatenate([-sin, sin, zeros], axis=1)


def _ffn_forward(x, scale, shift, gate_w, w8, wo4, name, target=None):
    S = x.shape[0]
    tm = _pick(S, (512, 256, 128))
    n = S // tm
    with_loss = target is not None

    def body(x_ref, sc_ref, sh_ref, g_ref, wg_ref, wu_ref, wo_ref, *rest):
        t_ref = rest[0] if with_loss else None
        h_ref, bg_ref, bu_ref, ht_ref = rest[with_loss:with_loss + 4]
        tail = rest[with_loss + 4:]
        h_scr, acc_ref = tail[-2:]
        i, p = pl.program_id(0), pl.program_id(1)

        @pl.when(p == 0)
        def _():
            h_new = _modulate(x_ref[...], sc_ref[...], sh_ref[...]).astype(BF16)
            h_scr[...] = h_new
            h_ref[...] = h_new
            acc_ref[...] = jnp.zeros_like(acc_ref)

        h = h_scr[...]
        gate = _dot_raw(h, wg_ref[...], "nn")
        up = _dot_raw(h, wu_ref[...], "nn")
        sg = _sigmoid(gate)
        act = gate * sg
        hid = act * up
        bg_ref[...] = (up * (sg * (1.0 + gate * (1.0 - sg)))).astype(BF16)
        bu_ref[...] = act.astype(BF16)
        ht_ref[...] = jnp.transpose(hid).astype(BF16)
        acc_ref[...] += _dot_raw(hid, wo_ref[...], "nn")

        if with_loss:
            dx_ref, df_ref, dg_ref, l_ref = tail[:4]

            @pl.when((p == 0) & (i == 0))
            def _():
                dg_ref[...] = jnp.zeros_like(dg_ref)
                l_ref[...] = jnp.zeros_like(l_ref)

            @pl.when(p == HID_PIECES - 1)
            def _():
                step = min(EPILOGUE_ROWS, tm)
                for r in range(tm // step):
                    rows = slice(step * r, step * (r + 1))
                    f = acc_ref[rows, :]
                    diff = x_ref[rows, :] + 0.5 * g_ref[...] * f - t_ref[rows, :]
                    dx = diff * (1.0 / D_MODEL)
                    dx_ref[rows, :] = dx
                    df_ref[rows, :] = (0.5 * g_ref[...] * dx).astype(df_ref.dtype)
                    dg_ref[...] += jnp.sum(0.5 * f * dx, axis=0, keepdims=True)
                    l_ref[...] += jnp.sum(diff * diff, axis=0, keepdims=True)

            @pl.when((p == HID_PIECES - 1) & (i == n - 1))
            def _():
                l_ref[...] = jnp.full(l_ref.shape, (0.5 / D_MODEL) * jnp.sum(l_ref[...]), F32)
        else:
            f_ref, xo_ref = tail[:2]

            @pl.when(p == HID_PIECES - 1)
            def _():
                f = acc_ref[...]
                f_ref[...] = f.astype(f_ref.dtype)
                xo_ref[...] = x_ref[...] + 0.5 * g_ref[...] * f

    row = pl.BlockSpec((tm, D_MODEL), lambda i, p: (i, 0))
    par = pl.BlockSpec((1, D_MODEL), lambda i, p: (0, 0))
    piece = pl.BlockSpec((None, tm, FFN_PIECE), lambda i, p: (p, i, 0))
    row_f32, row_bf16 = jax.ShapeDtypeStruct((S, D_MODEL), F32), jax.ShapeDtypeStruct((S, D_MODEL), BF16)
    par_f32 = jax.ShapeDtypeStruct((1, D_MODEL), F32)
    piece_shape = jax.ShapeDtypeStruct((HID_PIECES, S, FFN_PIECE), BF16)
    return pl.pallas_call(
        body, name=name, grid=(n, HID_PIECES),
        in_specs=[row, par, par, par,
                  pl.BlockSpec((None, D_MODEL, FFN_PIECE), lambda i, p: (p, 0, 0)),
                  pl.BlockSpec((None, D_MODEL, FFN_PIECE), lambda i, p: (p + HID_PIECES, 0, 0)),
                  pl.BlockSpec((None, FFN_PIECE, D_MODEL), lambda i, p: (p, 0, 0))] + [row] * with_loss,
        out_specs=[row, piece, piece, pl.BlockSpec((None, FFN_PIECE, tm), lambda i, p: (p, 0, i))]
        + ([row, row, par, par] if with_loss else [row, row]),
        out_shape=[row_bf16, piece_shape, piece_shape, jax.ShapeDtypeStruct((HID_PIECES, FFN_PIECE, S), BF16)]
        + ([row_f32, row_bf16, par_f32, par_f32] if with_loss else [row_bf16, row_f32]),
        scratch_shapes=[pltpu.VMEM((tm, D_MODEL), BF16), pltpu.VMEM((tm, D_MODEL), F32)],
        compiler_params=_params(("arbitrary", "arbitrary")),
    )(x, scale, shift, gate_w, w8, w8, wo4, *([target] if with_loss else []))


def _ffn_fwd(x, scale, shift, gate_w, w8, wo4, tag, target=None):
    res = _ffn_forward(x, scale, shift, gate_w, w8, wo4, tag + "_fwd", target)
    h, by_gate, by_up, hid_t = res[:4]
    if target is not None:
        dx_out, df, d_gate_w, loss_row = res[4:]
        return (dx_out, loss_row), (h, by_gate, by_up, hid_t, None, df, d_gate_w)
    f, x_out = res[4:]
    return x_out, (h, by_gate, by_up, hid_t, f, None, None)


def _ffn_bwd(d_out, x, scale, shift, gate_w, w8, wo4, saved, tag, grad_ready, below=None):
    h, gate, up, hid_t, f, df, d_gate_w = saved
    S = x.shape[0]
    tm = _pick(S, (512, 256, 128))
    tk = _pick(S, (512, 256, 128))
    n = S // tm
    if df is None:
        (df,), (d_gate_w,) = _rowwise_bwd(lambda f_, g_: (0.5 * g_ * f_,), [(f, tm, D_MODEL, 0)], [], [gate_w],
                                          [(d_out, tm, D_MODEL, 0)], n, tag + "_dres", row_dtypes=(BF16,))
    tb = _pick(S, MATMUL_ROWS)
    piece = pl.BlockSpec((None, tb, FFN_PIECE), lambda i, j, k: (j, i, 0))
    d_gate, d_up = _mmg(df, wo4, "nt", name=tag + "_ddown", grid=(S // tb, HID_PIECES, 1),
                        a_spec=pl.BlockSpec((tb, D_MODEL), lambda i, j, k: (i, 0)),
                        b_spec=pl.BlockSpec((None, FFN_PIECE, D_MODEL), lambda i, j, k: (j, 0, 0)),
                        out_spec=piece, out_shapes=[jax.ShapeDtypeStruct((HID_PIECES, S, FFN_PIECE), BF16)] * 2,
                        acc_shape=(tb, FFN_PIECE), extras=[gate, up], extra_specs=[piece, piece], epi=_swiglu_bwd)
    tk = _pick(S, MATMUL_ROWS)
    g_wo4 = _mmg(hid_t, df, "nn", name=tag + "_gwo", grid=(HID_PIECES, 1, S // tk),
                 a_spec=pl.BlockSpec((None, FFN_PIECE, tk), lambda i, j, k: (i, 0, k)),
                 b_spec=pl.BlockSpec((tk, D_MODEL), lambda i, j, k: (k, j)),
                 out_spec=pl.BlockSpec((None, FFN_PIECE, D_MODEL), lambda i, j, k: (i, 0, j)),
                 out_shapes=[jax.ShapeDtypeStruct((HID_PIECES, FFN_PIECE, D_MODEL), BF16)], acc_shape=(FFN_PIECE, D_MODEL))
    g_w8 = _ffn_gw8(h, d_gate, d_up, tag + "_gw8", after=grad_ready("wo4", g_wo4))
    res = _ffn_dh(d_gate, d_up, w8, x, d_out, scale, shift, tag + "_dh", after=grad_ready("w8", g_w8), below=below)
    return (res[0], res[1], res[2], d_gate_w) + tuple(res[3:])


def _dproj_dmod(d_pieces, wp, x, d_out, scale, shift, name, below=None):
    S = x.shape[0]
    tm = _pick(S, TOKEN_ROWS)
    widths = [p.shape[1] for p in d_pieces]
    starts = [sum(widths[:n]) for n in range(len(widths))]
    n_p = len(d_pieces)
    n_below = 0 if below is None else 2

    def body(*refs):
        dp_refs, (w_ref, x_ref, do_ref, sc_ref, sh_ref), rest = refs[:n_p], refs[n_p:n_p + 5], refs[n_p + 5:]
        below_in = rest[:n_below]
        dx_ref, dsc_ref, dsh_ref = rest[n_below:n_below + 3]
        below_out, acc_ref = rest[n_below + 3:n_below + 3 + n_below], rest[-1]

        @pl.when(pl.program_id(0) == 0)
        def _():
            for r in (dsc_ref, dsh_ref) + tuple(below_out[1:]):
                r[...] = jnp.zeros_like(r)

        acc = None
        for dp_ref, at, width in zip(dp_refs, starts, widths):
            part = _dot_raw(dp_ref[...], w_ref[:, at:at + width], "nt")
            acc = part if acc is None else acc + part
        acc_ref[...] = acc
        _dmod_epilogue(acc_ref, x_ref, do_ref, sc_ref, sh_ref, dx_ref, dsc_ref, dsh_ref, below, below_in, below_out)

    row = pl.BlockSpec((tm, D_MODEL), lambda i: (i, 0))
    par = pl.BlockSpec((1, D_MODEL), lambda i: (0, 0))
    row_shape, par_shape = jax.ShapeDtypeStruct((S, D_MODEL), F32), jax.ShapeDtypeStruct((1, D_MODEL), F32)
    return pl.pallas_call(
        body, name=name, grid=(S // tm,),
        in_specs=[pl.BlockSpec((tm, width), lambda i: (i, 0)) for width in widths]
        + [pl.BlockSpec(wp.shape, lambda i: (0, 0)), row, row, par, par] + [row, par][:n_below],
        out_specs=[row, par, par] + [row, par][:n_below],
        out_shape=[row_shape, par_shape, par_shape] + [jax.ShapeDtypeStruct((S, D_MODEL), BF16), par_shape][:n_below],
        scratch_shapes=[pltpu.VMEM((tm, D_MODEL), F32)],
        compiler_params=_params(("arbitrary",)),
    )(*d_pieces, wp, x, d_out, scale, shift, *(below[:2] if below is not None else ()))


def _gw_pieces(h, d_pieces, name):
    S = h.shape[0]
    tm = 512
    tk = _pick(S, MATMUL_ROWS)
    n_p = len(d_pieces)
    widths = [p.shape[1] for p in d_pieces]

    def body(h_ref, *rest):
        d_refs, o_refs = rest[:n_p], rest[n_p:]

        @pl.when(pl.program_id(1) == 0)
        def _():
            for o_ref in o_refs:
                o_ref[...] = jnp.zeros_like(o_ref)

        h_t = jnp.transpose(h_ref[...])
        for d_ref, o_ref in zip(d_refs, o_refs):
            o_ref[...] += _dot_raw(h_t, d_ref[...], "nn")

    return pl.pallas_call(
        body, name=name, grid=(D_MODEL // tm, S // tk),
        in_specs=[pl.BlockSpec((tk, tm), lambda i, k: (k, i))] + [pl.BlockSpec((tk, width), lambda i, k: (k, 0)) for width in widths],
        out_specs=[pl.BlockSpec((tm, width), lambda i, k: (i, 0)) for width in widths],
        out_shape=[jax.ShapeDtypeStruct((D_MODEL, width), F32) for width in widths],
        compiler_params=_params(("parallel", "arbitrary")),
    )(h, *d_pieces)


def _mod_proj(x, scale, shift, wp, name):
    S = x.shape[0]
    N = wp.shape[1]
    tm = _pick(S, MATMUL_ROWS)
    tn = _pick(N, (1408, 1024, 512, 256, 128))

    def body(x_ref, sc_ref, sh_ref, w_ref, h_ref, o_ref, h_scr):
        @pl.when(pl.program_id(1) == 0)
        def _():
            h_new = _modulate(x_ref[...], sc_ref[...], sh_ref[...]).astype(BF16)
            h_scr[...] = h_new
            h_ref[...] = h_new

        o_ref[...] = _dot_raw(h_scr[...], w_ref[...], "nn")

    row = pl.BlockSpec((tm, D_MODEL), lambda i, j: (i, 0))
    par = pl.BlockSpec((1, D_MODEL), lambda i, j: (0, 0))
    return pl.pallas_call(
        body, name=name, grid=(S // tm, N // tn),
        in_specs=[row, par, par, pl.BlockSpec((D_MODEL, tn), lambda i, j: (0, j))],
        out_specs=[row, pl.BlockSpec((tm, tn), lambda i, j: (i, j))],
        out_shape=[jax.ShapeDtypeStruct((S, D_MODEL), BF16), jax.ShapeDtypeStruct((S, N), F32)],
        scratch_shapes=[pltpu.VMEM((tm, D_MODEL), BF16)],
        compiler_params=_params(("parallel", "arbitrary")),
    )(x, scale, shift, wp)


def _mix_out(o_a, proj, o_b, gnw, onw, wout, x, gate_w, t, name):
    S = x.shape[0]

    def body(oa_ref, z_ref, ob_ref, gn_ref, on_ref, w_ref, x_ref, g_ref, mixed_ref, y_ref, xo_ref):
        (mixed,) = _mix_post_fn(oa_ref[...], z_ref[...], ob_ref[...], gn_ref[...], on_ref[...])
        mixed = mixed.astype(BF16)
        mixed_ref[...] = mixed
        y = _dot_raw(mixed, w_ref[...], "nn")
        y_ref[...] = y.astype(BF16)
        xo_ref[...] = x_ref[...] + g_ref[...] * y

    half, row = _row_spec(t, 512, 0), _row_spec(t, D_MODEL, 0)
    return pl.pallas_call(
        body, name=name, grid=(S // t,),
        in_specs=[half, _row_spec(t, 512, 3), half, _full_spec(gnw.shape), _full_spec(onw.shape), _full_spec(wout.shape),
                  row, _full_spec(gate_w.shape)],
        out_specs=[row, row, row],
        out_shape=[jax.ShapeDtypeStruct((S, D_MODEL), BF16)] * 2 + [jax.ShapeDtypeStruct((S, D_MODEL), F32)],
        compiler_params=_params(("parallel",)),
    )(o_a, proj, o_b, gnw, onw, wout, x, gate_w)


def _mixer_fwd(x1, scale, shift, gate_w, cos_p, sin_p, W):
    S = x1.shape[0]
    tm = _pick(S, (512, 256, 128))
    tv = _pick(S, TOKEN_ROWS)
    ta = _pick(S, (512, 256, 128))
    nc = S // CHUNK
    h2, proj = _mod_proj(x1, scale, shift, W["wp"], "mix_proj")
    qkvc, q_a, k_a, v_a, gb = _conv_fwd(proj, W["conv_w"], W["alog_p"], W["dt_p"], tv, "gdn_conv")
    kab = (proj, tv, 128, 21)
    ti = _pick(S, INTRA_ROWS)
    intra = _rowwise(_gdn_intra_fn, [(q_a, ti, 512, 0), (k_a, ti, 512, 0), (v_a, ti, 512, 0), (gb, ti, 128, 0)],
                     [], [(ti, 512, F32)] * 4 + [(ti, CHUNK, F32)] * 4 + [(ti // 8, 512, F32)] + [(ti, CHUNK, F32)] * 4,
                     S // ti, "gdn_intra")
    u, wk, qd, kd, qks, gl, invs = intra[0], intra[1], intra[2], intra[3], tuple(intra[4:8]), intra[8], tuple(intra[9:])
    o_a, s_prev = _gdn_scan_fwd(u, wk, qd, kd, qks, gl, "gdn_scan")
    mla_params = [W["qnw"], W["kvnw"], W["wuq"], W["wukv"], W["qn_w"], W["qr_w"], W["kn_w"], W["kr_w"]]
    def mla_pre_with_vt(*a):
        q_, k_, v_ = _mla_pre_fn(*a)
        return q_, k_, v_, jnp.transpose(v_)

    q_b, k_b, v_b, vt_b = _rowwise(mla_pre_with_vt,
                                   [(proj, tv, 256, 8), (proj, tv, 384, 6), kab, (cos_p, tv, 128, 0), (sin_p, tv, 128, 0)],
                                   mla_params, [(tv, 1024, BF16), (tv, 1024, BF16), (tv, 512, BF16), (512, tv, BF16, "across")],
                                   S // tv, "mla_pre")
    o_b, lse = _attn_fwd(q_b, k_b, vt_b, ta, "mla_attn")
    mixed, y, x2 = _mix_out(o_a, proj, o_b, W["gnw"], W["onw"], W["wout"], x1, gate_w, tv, "mix_out")
    saved = (h2, proj, qkvc, q_a, k_a, v_a, gb, u, wk, qd, kd, qks, gl, invs, s_prev, o_a, q_b, k_b, v_b, o_b, lse, mixed, y)
    return x2, saved


def _mixer_bwd(d_out, dy, x1, scale, shift, cos_p, sin_p, W, saved, below):
    (h2, proj, qkvc, q_a, k_a, v_a, gb, u, wk, qd, kd, qks, gl, invs, s_prev, o_a, q_b, k_b, v_b, o_b, lse, mixed, y) = saved
    S = x1.shape[0]
    tm = _pick(S, (512, 256, 128))
    tv = _pick(S, TOKEN_ROWS)
    ta = _pick(S, (512, 256, 128))
    nc = S // CHUNK
    G = {}
    G["wout"] = _mm(mixed, dy, "tn", name="mix_gwout")
    (do_a, dz, do_b), (G["gnw"], G["onw"], stats) = _rowwise_bwd(
        _mix_post_fn, [(o_a, tv, 512, 0), (proj, tv, 512, 3), (o_b, tv, 512, 0)], [], [W["gnw"], W["onw"]],
        [], S // tv, "mix_dpost", row_dtypes=(F32, BF16, F32), dout_from=(dy, W["wout"]),
        across=(lse, lambda r_vals, d_rows, lse_tile: _attn_stats(r_vals[2], d_rows[2], lse_tile)))
    dq_b, dk_b, dv_b = _attn_bwd(q_b, k_b, v_b, do_b, stats, ta, "mla_dattn")
    kab = (proj, tv, 128, 21)
    mla_params = [W["qnw"], W["kvnw"], W["wuq"], W["wukv"], W["qn_w"], W["qr_w"], W["kn_w"], W["kr_w"]]
    (d_ckv, d_cq, d_kab), mla_grads = _rowwise_bwd(
        _mla_pre_fn, [(proj, tv, 256, 8), (proj, tv, 384, 6), kab], [(cos_p, tv, 128, 0), (sin_p, tv, 128, 0)], mla_params,
        [(dq_b, tv, 1024, 0), (dk_b, tv, 1024, 0), (dv_b, tv, 512, 0)], S // tv, "mla_dpre", row_dtypes=(BF16, BF16, F32))
    for key, g in zip(("qnw", "kvnw", "wuq", "wukv", "qn_w", "qr_w", "kn_w", "kr_w"), mla_grads):
        G[key] = g
    scan_grads = _gdn_scan_bwd(u, wk, qd, kd, qks, gl, s_prev, do_a, "gdn_dscan")
    ti = _pick(S, INTRA_ROWS)
    intra_douts = [(scan_grads[i], ti, 512, 0) for i in range(4)] + [(scan_grads[4 + i], ti, CHUNK, 0) for i in range(4)]
    intra_douts.append((scan_grads[8], ti // 8, 512, 0))
    (dq_a, dk_a, dv_a, d_gb), _ = _rowwise_bwd(
        _gdn_intra_fn, [(q_a, ti, 512, 0), (k_a, ti, 512, 0), (v_a, ti, 512, 0), (gb, ti, 128, 0)],
        [(x_, ti, CHUNK, 0) for x_ in invs], [], intra_douts, S // ti, "gdn_dintra")
    d_qkv, d_kab, g_conv, G["alog_p"], G["dt_p"] = _conv_bwd(proj, qkvc, W["conv_w"], W["alog_p"], W["dt_p"],
                                                              (dq_a, dk_a, dv_a, d_gb), d_kab, tv, "gdn_dconv")
    G["conv_w"] = g_conv[:4]
    d_proj = [d_qkv, dz, d_ckv, d_cq, d_kab]
    G["wp"] = _gw_pieces(h2, d_proj, "mix_gwp")
    dx1, G["s2"], G["sh2"], d_below, dg_below = _dproj_dmod(d_proj, W["wp"], x1, d_out, scale, shift, "mix_dproj", below=below)
    return dx1, d_below, dg_below, G


def _local_step(x, target, mod, cos_p, sin_p, W1, mixer_weights, ffn2_weights, ffn_grad_ready, mixer_grads_ready):
    sh1, s1, g1, sh2, s2, g2, sh3, s3, g3 = [mod[:, D_MODEL * i:D_MODEL * (i + 1)] for i in range(N_MOD)]
    x1, saved1 = _ffn_fwd(x, s1, sh1, g1, W1["f1_w8"], W1["f1_wo4"], "ffn1")
    W = mixer_weights(x1)
    x2, saved2 = _mixer_fwd(x1, s2, sh2, g2, cos_p, sin_p, W)
    W.update(ffn2_weights(x2))
    (dx3, loss_row), saved3 = _ffn_fwd(x2, s3, sh3, g3, W["f2_w8"], W["f2_wo4"], "ffn2", target=target)
    dx2, d_s3, d_sh3, d_g3, dy, d_g2 = _ffn_bwd(dx3, x2, s3, sh3, g3, W["f2_w8"], W["f2_wo4"], saved3, "ffn2",
                                                ffn_grad_ready("f2"), below=(saved2[-1], g2, 1.0))
    dx1, df1, d_g1, G = _mixer_bwd(dx2, dy, x1, s2, sh2, cos_p, sin_p, W, saved2, below=(saved1[4], g1, 0.5))
    d_sh2, d_s2 = G.pop("sh2"), G.pop("s2")
    saved1 = saved1[:5] + (df1, d_g1 + mixer_grads_ready(G))
    dx, d_s1, d_sh1, d_g1 = _ffn_bwd(dx1, x, s1, sh1, g1, W1["f1_w8"], W1["f1_wo4"], saved1, "ffn1", ffn_grad_ready("f1"))
    d_mod = jnp.concatenate([d_sh1, d_s1, d_g1, d_sh2, d_s2, d_g2, d_sh3, d_s3, d_g3], axis=1)
    return loss_row, dx, d_mod


WEIGHT_NAMES = ("w_ada", "b_ada", "ffn1_w_in", "ffn1_w_out", "w_in", "gdn_conv_w", "gdn_a_log", "gdn_dt_bias", "gdn_norm_w",
                "mla_q_norm_w", "mla_w_uq", "mla_kv_norm_w", "mla_w_ukv", "qkn_q_nope", "qkn_q_rope", "qkn_k_nope",
                "qkn_k_rope", "mla_out_norm_w", "w_out", "ffn2_w_in", "ffn2_w_out")
FFN_SHARDED = ("ffn1_w_in", "ffn1_w_out", "ffn2_w_in", "ffn2_w_out")
TRANSPOSED_ENTRY = ("ffn1_w_in", "ffn2_w_in", "w_in", "mla_w_uq")
SHEETED = (("w_in", "col"), ("gdn_conv_w", "col"), ("mla_w_uq", "col"), ("mla_w_ukv", "col"), ("w_out", "row"))
MOD_ROWS = N_MOD * D_MODEL // 128
SMALL = {"gdn_a_log": (MOD_ROWS, 1, 64, 4), "gdn_dt_bias": (MOD_ROWS + 1, 1, 64, 4), "gdn_norm_w": (MOD_ROWS + 2, 1, 0, 128),
         "mla_q_norm_w": (MOD_ROWS + 3, 3, 0, 384), "mla_kv_norm_w": (MOD_ROWS + 6, 2, 0, 256),
         "qkn_q_nope": (MOD_ROWS + 8, 1, 0, 128), "qkn_q_rope": (MOD_ROWS + 9, 1, 0, 64), "qkn_k_nope": (MOD_ROWS + 10, 1, 0, 128),
         "qkn_k_rope": (MOD_ROWS + 11, 1, 0, 64), "mla_out_norm_w": (MOD_ROWS + 12, 1, 0, 128)}
LOSS_ROW = MOD_ROWS + 13
CONV_ROW, CONV_ROWS = 88, 4 * 1536 // 128
SHEET_ROWS = CONV_ROW + CONV_ROWS


def _to_sheet(flat, dtype, sublanes):
    n = flat.shape[-1]
    unit = sublanes * 128
    pad = (-n) % unit
    flat = jnp.pad(flat.astype(dtype), [(0, 0)] * (flat.ndim - 1) + [(0, pad)])
    return flat.reshape(flat.shape[:-1] + ((n + pad) // 128, 128))


def _small_sheet(b_like, small):
    sheet = jnp.zeros((SHEET_ROWS, 128), F32).at[:MOD_ROWS].set(b_like.reshape(MOD_ROWS, 128))
    for name, (row, rows, lane, n) in SMALL.items():
        v = small[name].reshape(1, n)
        if rows == 1:
            sheet = sheet.at[row, lane:lane + n].set(v[0])
        else:
            sheet = sheet.at[row:row + rows].set(v.reshape(rows, 128))
    return sheet


def _from_small_sheet(sheet):
    out = {"b_ada": sheet[:MOD_ROWS].reshape(1, N_MOD * D_MODEL)}
    for name, (row, rows, lane, n) in SMALL.items():
        out[name] = sheet[row, lane:lane + n].reshape(1, n) if rows == 1 else sheet[row:row + rows].reshape(1, n)
    return out


def kernel(x, c, positions, w_ada, b_ada, ffn1_w_in, ffn1_w_out, w_in, gdn_conv_w, gdn_a_log, gdn_dt_bias, gdn_norm_w, mla_q_norm_w, mla_w_uq, mla_kv_norm_w, mla_w_ukv, qkn_q_nope, qkn_q_rope, qkn_k_nope, qkn_k_rope, mla_out_norm_w, w_out, ffn2_w_in, ffn2_w_out, loss_target, m_w_ada, m_b_ada, m_ffn1_w_in, m_ffn1_w_out, m_w_in, m_gdn_conv_w, m_gdn_a_log, m_gdn_dt_bias, m_gdn_norm_w, m_mla_q_norm_w, m_mla_w_uq, m_mla_kv_norm_w, m_mla_w_ukv, m_qkn_q_nope, m_qkn_q_rope, m_qkn_k_nope, m_qkn_k_rope, m_mla_out_norm_w, m_w_out, m_ffn2_w_in, m_ffn2_w_out, v_w_ada, v_b_ada, v_ffn1_w_in, v_ffn1_w_out, v_w_in, v_gdn_conv_w, v_gdn_a_log, v_gdn_dt_bias, v_gdn_norm_w, v_mla_q_norm_w, v_mla_w_uq, v_mla_kv_norm_w, v_mla_w_ukv, v_qkn_q_nope, v_qkn_q_rope, v_qkn_k_nope, v_qkn_k_rope, v_mla_out_norm_w, v_w_out, v_ffn2_w_in, v_ffn2_w_out):
    args = locals()
    w = {n: args[n] for n in WEIGHT_NAMES}
    m = {n: args["m_" + n] for n in WEIGHT_NAMES}
    v = {n: args["v_" + n] for n in WEIGHT_NAMES}
    me = 4 * lax.axis_index("x") + 2 * lax.axis_index("y") + lax.axis_index("c")
    cols = N_MOD * D_MODEL // N_DEV
    shard = {n: w[n][0] for n in FFN_SHARDED + tuple(s[0] for s in SHEETED)}

    sc = c * _sigmoid(c)
    first = _to_sheet(jnp.concatenate([sc.reshape(-1), shard["gdn_conv_w"].reshape(-1)]), F32, 8)
    (first_all,) = _all_gather([first], "gather_c")
    sc_all = first_all[:, :D_MODEL // 128].reshape(N_DEV, D_MODEL)
    n_taps = shard["gdn_conv_w"].size
    conv_all = first_all.reshape(N_DEV, -1)[:, D_MODEL:D_MODEL + n_taps].reshape(N_DEV, 4, -1)
    b_mine = lax.dynamic_slice(b_ada, (0, me * cols), (1, cols))
    mod_cols = _mm(sc_all, w_ada[0], "nn", name="ada_mod", extra_params=[b_mine], epi=lambda acc, b_: (acc + b_,))
    (mod_all,) = _all_to_all([_to_sheet(mod_cols, F32, 8)], "scatter_mod")
    mod = mod_all.reshape(N_DEV, -1)[:, :cols].reshape(1, N_MOD * D_MODEL)

    f1_shards, mod = lax.optimization_barrier(([shard["ffn1_w_in"].astype(BF16), shard["ffn1_w_out"].astype(BF16)], mod))
    f1_w8, f1_out = _all_gather(f1_shards, "gather_w1")
    travel = [s for s in SHEETED if s[0] != "gdn_conv_w"]
    tied = lax.optimization_barrier(([shard[n].astype(BF16) for n, _ in travel], f1_w8))
    f1_w8 = tied[1]
    mixer_w = _exchange_start(tied[0], False, "gather_wm_start")
    ffn2_w = _exchange_start([shard["ffn2_w_in"].astype(BF16) + mixer_w[4][0:1, 0:1].astype(BF16),
                              shard["ffn2_w_out"].astype(BF16)], False, "gather_w2_start")
    mod = mod + ffn2_w[4][0:1, 0:1]
    W1 = dict(f1_w8=f1_w8, f1_wo4=f1_out.reshape(HID_PIECES, FFN_PIECE, D_MODEL))

    def mixer_weights(after):
        got = _exchange_wait(mixer_w, False, after, "gather_wm_wait")
        P = {n: jnp.concatenate(list(g), axis=1) if kind == "col" else g.reshape(-1, g.shape[-1])
             for (n, kind), g in zip(travel, got)}
        P["gdn_conv_w"] = jnp.concatenate(list(conv_all), axis=1)
        for n in SMALL:
            P[n] = w[n]
        return _pack_weights(P)

    def ffn2_weights(after):
        f2_w8, f2_out = _exchange_wait(ffn2_w, False, after, "gather_w2_wait")
        return dict(f2_w8=f2_w8, f2_wo4=f2_out.reshape(HID_PIECES, FFN_PIECE, D_MODEL))

    pending, small_grads = {}, {}

    def ffn_grad_ready(tag):
        def ready(which, g):
            pieces = g if which == "w8" else g.reshape((N_DEV,) + shard["ffn1_w_out"].shape)
            pending[tag + which] = _exchange_start([pieces], True, "scatter_%s_%s_start" % (tag, which))
            return pending[tag + which][4]
        return ready

    def mixer_grads_ready(G):
        g_full = _unpack_grads(G)
        small_grads.update({n: g_full[n] for n in SMALL})
        small_grads["gdn_conv_w"] = g_full["gdn_conv_w"]
        pieces = []
        for n, kind in travel:
            r, cc = shard[n].shape
            g = g_full[n].astype(BF16)
            pieces.append(jnp.stack([g[:, cc * p:cc * (p + 1)] for p in range(N_DEV)]) if kind == "col"
                          else g.reshape(N_DEV, r, cc))
        pending["mixer"] = _exchange_start(pieces, True, "scatter_mx_start")
        return pending["mixer"][4][0:1, 0:1]

    cos_p, sin_p = _rope_tables(positions[0])
    loss_row, dx, d_mod = _local_step(x[0], loss_target[0], mod, cos_p, sin_p, W1, mixer_weights, ffn2_weights,
                                      ffn_grad_ready, mixer_grads_ready)

    sheet = _small_sheet(d_mod, small_grads).at[LOSS_ROW].set(loss_row[0, :128])
    sheet = sheet.at[CONV_ROW:CONV_ROW + CONV_ROWS].set(small_grads["gdn_conv_w"].reshape(CONV_ROWS, 128))
    (sheets,) = _all_gather([sheet], "gather_small")
    summed = _sum_devices(sheets, "sum_small")
    d_mod_all = sheets[:, :MOD_ROWS].reshape(N_DEV, N_MOD * D_MODEL)
    d_mod_mine = lax.dynamic_slice(d_mod_all, (0, me * cols), (N_DEV, cols))
    grads = _from_small_sheet(summed)
    grads["w_ada"] = _mm(sc_all, d_mod_mine, "tn", name="ada_gw", hi=True)
    conv_taps = shard["gdn_conv_w"].shape[1]
    grads["gdn_conv_w"] = lax.dynamic_slice(summed[CONV_ROW:CONV_ROW + CONV_ROWS].reshape(4, -1), (0, me * conv_taps),
                                            (4, conv_taps))
    loss = summed[LOSS_ROW, 0]

    delta, new_m, new_v = {}, {}, {}
    arrived = {}
    for n, key in zip(FFN_SHARDED, ("f1w8", "f1wo4", "f2w8", "f2wo4")):
        (arrived[n],) = _exchange_wait(pending[key], True, summed, "scatter_%s_wait" % key)
    arrived.update(zip([n for n, _ in travel], _exchange_wait(pending["mixer"], True, summed, "scatter_mx_wait")))
    for n, parts in arrived.items():
        if n in TRANSPOSED_ENTRY:
            res = _sum_adamw(parts, w[n][0].T, m[n][0].T, v[n][0].T, "adamw_" + n, transposed=True)
            grads[n], delta[n], new_m[n], new_v[n] = [r.T for r in res]
        else:
            grads[n], delta[n], new_m[n], new_v[n] = _sum_adamw(parts, w[n][0], m[n][0], v[n][0], "adamw_" + n)
    for n in ("w_ada", "gdn_conv_w"):
        delta[n], new_m[n], new_v[n] = _adamw(w[n][0], grads[n], m[n][0], v[n][0], "adamw_" + n)
    small_in = [_small_sheet(t["b_ada"], t) for t in (w, grads, m, v)]
    for res, out in zip(_adamw(*small_in, "adamw_small"), (delta, new_m, new_v)):
        out.update(_from_small_sheet(res))

    def shaped(d):
        return [d[n].reshape(w[n].shape) for n in WEIGHT_NAMES]

    return (loss, dx[None], *shaped(grads), *shaped(delta), *shaped(new_m), *shaped(new_v))
```

```python
import functools

import jax
import jax.numpy as jnp
import numpy as np
from jax import lax
from jax.experimental import pallas as pl
from jax.experimental.pallas import tpu as pltpu

F32 = jnp.float32
BF16 = jnp.bfloat16

D_MODEL = 1024
D_FF = 2816
N_MOD = 9
HEADS = 4
HEAD_DIM = 128
CHUNK = 64
EPS = 1e-6
ROPE = 64
Q_LORA = 384
KV_LORA = 256
N_IN = 2760
N_IN_PACKED = 2816
ROPE_BASE = 10000.0
LOG2_E = 1.4426950408889634
N_DEV = 8

ADAM_LR = 0.001
ADAM_B1 = 0.9
ADAM_B2 = 0.999
ADAM_EPS = 1e-08
ADAM_WD = 0.01
ADAM_STEP = 10

VMEM_LIMIT_BYTES = 56 * 1024 * 1024
MATMUL_ROWS = (1024, 512, 256, 128)
MESH = pl.DeviceIdType.MESH


def _params(sem=None):
    return pltpu.CompilerParams(dimension_semantics=sem, vmem_limit_bytes=VMEM_LIMIT_BYTES)


def _pick(dim, prefs):
    for p in prefs:
        if dim % p == 0:
            return p
    return dim


_DIMS = {"nn": (((1,), (0,)), ((), ())), "nt": (((1,), (1,)), ((), ())), "tn": (((0,), (0,)), ((), ()))}


def _dot_raw(a, b, mode):
    return lax.dot_general(a.astype(BF16), b.astype(BF16), _DIMS[mode], preferred_element_type=F32)


def _dot_hi(a, b, mode="nn"):
    return lax.dot_general(a, b, _DIMS[mode], precision=lax.Precision.HIGHEST, preferred_element_type=F32)


@functools.partial(jax.custom_vjp, nondiff_argnums=(2,))
def _bdot(a, b, mode):
    return _dot_raw(a, b, mode)


def _bdot_fwd(a, b, mode):
    return _dot_raw(a, b, mode), (a, b)


def _bdot_bwd(mode, res, g):
    a, b = res
    if mode == "nn":
        return _dot_raw(g, b, "nt"), _dot_raw(a, g, "tn")
    if mode == "nt":
        return _dot_raw(g, b, "nn"), _dot_raw(g, a, "tn")
    return _dot_raw(b, g, "nt"), _dot_raw(a, g, "nn")


_bdot.defvjp(_bdot_fwd, _bdot_bwd)


def _mm(a, b, mode, *, name, out_dtypes=(F32,), epi=None, extras=(), extra_params=(), hi=False,
        tm=None, tn=None, tk=None):
    if mode == "nn":
        (M, K), (_, N) = a.shape, b.shape
    elif mode == "nt":
        (M, K), (N, _) = a.shape, b.shape
    else:
        (K, M), (_, N) = a.shape, b.shape
    tm = tm or _pick(M, (512, 1408, 256, 128) if mode == "tn" else MATMUL_ROWS + (384, 352))
    tn = tn or _pick(N, (1024, 1408, 768, 512, 384, 256, 128))
    tk = tk or _pick(K, (1024, 1408, 512, 384, 256, 128))
    a_spec = {"nn": pl.BlockSpec((tm, tk), lambda i, j, k: (i, k)), "nt": pl.BlockSpec((tm, tk), lambda i, j, k: (i, k)),
              "tn": pl.BlockSpec((tk, tm), lambda i, j, k: (k, i))}[mode]
    b_spec = {"nn": pl.BlockSpec((tk, tn), lambda i, j, k: (k, j)), "nt": pl.BlockSpec((tn, tk), lambda i, j, k: (j, k)),
              "tn": pl.BlockSpec((tk, tn), lambda i, j, k: (k, j))}[mode]
    mn_spec = pl.BlockSpec((tm, tn), lambda i, j, k: (i, j))
    return _mmg(a, b, mode, name=name, grid=(M // tm, N // tn, K // tk), a_spec=a_spec, b_spec=b_spec, out_spec=mn_spec,
                out_shapes=[jax.ShapeDtypeStruct((M, N), dt) for dt in out_dtypes], acc_shape=(tm, tn), epi=epi,
                extras=list(extras) + list(extra_params),
                extra_specs=[mn_spec] * len(extras) + [pl.BlockSpec((1, tn), lambda i, j, k: (0, j))] * len(extra_params),
                hi=hi)


def _mmg(a, b, mode, *, name, grid, a_spec, b_spec, out_spec, out_shapes, acc_shape, epi=None, extras=(),
         extra_specs=(), hi=False):
    nk = grid[2]
    n_e, n_o = len(extras), len(out_shapes)

    def body(*refs):
        a_ref, b_ref = refs[:2]
        e_refs = refs[2:2 + n_e]
        o_refs = refs[2 + n_e:2 + n_e + n_o]
        acc_ref = refs[-1]
        k = pl.program_id(2)

        @pl.when(k == 0)
        def _():
            acc_ref[...] = jnp.zeros_like(acc_ref)

        if hi:
            acc_ref[...] += _dot_hi(a_ref[...].astype(F32), b_ref[...].astype(F32), mode)
        else:
            acc_ref[...] += _dot_raw(a_ref[...], b_ref[...], mode)

        @pl.when(k == nk - 1)
        def _():
            acc = acc_ref[...]
            outs = (acc,) if epi is None else epi(acc, *[e[...].astype(F32) for e in e_refs])
            for o_ref, o in zip(o_refs, outs):
                o_ref[...] = o.astype(o_ref.dtype)

    outs = pl.pallas_call(
        body, name=name, grid=grid,
        in_specs=[a_spec, b_spec] + list(extra_specs),
        out_specs=[out_spec] * n_o,
        out_shape=list(out_shapes),
        scratch_shapes=[pltpu.VMEM(acc_shape, F32)],
        compiler_params=_params(("parallel", "parallel", "arbitrary")),
    )(a, b, *extras)
    return outs if n_o > 1 else outs[0]


def _row_spec(th, cw, ci):
    return pl.BlockSpec((th, cw), lambda i: (i, ci))


def _full_spec(shape):
    return pl.BlockSpec(shape, lambda i: (0,) * len(shape))


def _rowwise(fn, rows, params, outs, n_steps, name):
    n_r, n_p, n_o = len(rows), len(params), len(outs)

    def body(*refs):
        vals = [r[...].astype(F32) for r in refs[:n_r + n_p]]
        res = fn(*vals)
        for o_ref, o in zip(refs[n_r + n_p:], res):
            o_ref[...] = o.astype(o_ref.dtype)

    across = [len(o) == 4 for o in outs]
    res = pl.pallas_call(
        body, name=name, grid=(n_steps,),
        in_specs=[_row_spec(th, cw, ci) for (_, th, cw, ci) in rows] + [_full_spec(p.shape) for p in params],
        out_specs=[pl.BlockSpec((o[0], o[1]), lambda i: (0, i)) if ac else _row_spec(o[0], o[1], 0)
                   for o, ac in zip(outs, across)],
        out_shape=[jax.ShapeDtypeStruct((o[0], n_steps * o[1]) if ac else (n_steps * o[0], o[1]), o[2])
                   for o, ac in zip(outs, across)],
        compiler_params=_params(("parallel",)),
    )(*[r[0] for r in rows], *params)
    return res


def _rowwise_bwd(fn, rows, aux, params, douts, n_steps, name, row_dtypes=None, across=None, dout_from=None):
    n_r, n_a, n_p, n_d = len(rows), len(aux), len(params), len(douts)
    row_dtypes = row_dtypes or (F32,) * n_r

    def body(*refs):
        it = iter(refs)
        r_vals = [next(it)[...].astype(F32) for _ in range(n_r)]
        a_vals = [next(it)[...].astype(F32) for _ in range(n_a)]
        p_vals = [next(it)[...].astype(F32) for _ in range(n_p)]
        d_vals = [next(it)[...].astype(F32) for _ in range(n_d)]
        across_in = next(it) if across is not None else None
        if dout_from is not None:
            d_vals = [_dot_raw(next(it)[...], next(it)[...], "nt")]
        dr_refs = [next(it) for _ in range(n_r)]
        dp_refs = [next(it) for _ in range(n_p)]

        def f(*rp):
            return tuple(fn(*rp[:n_r], *a_vals, *rp[n_r:]))

        _, vjp = jax.vjp(f, *r_vals, *p_vals)
        grads = list(vjp(tuple(d_vals)))
        for dr_ref, g in zip(dr_refs, grads[:n_r]):
            dr_ref[...] = g.astype(dr_ref.dtype)
        if across is not None:
            next(it)[...] = across[1](r_vals, grads[:n_r], across_in[...])

        @pl.when(pl.program_id(0) == 0)
        def _():
            for dp_ref in dp_refs:
                dp_ref[...] = jnp.zeros_like(dp_ref)

        for dp_ref, g in zip(dp_refs, grads[n_r:]):
            dp_ref[...] += g

    all_rows = list(rows) + list(aux) + list(douts)
    in_specs = ([_row_spec(th, cw, ci) for (_, th, cw, ci) in list(rows) + list(aux)]
                + [_full_spec(p.shape) for p in params]
                + [_row_spec(th, cw, ci) for (_, th, cw, ci) in all_rows[n_r + n_a:]])
    across_specs = [] if across is None else [pl.BlockSpec((8, rows[0][1]), lambda i: (0, i))]
    from_specs = [] if dout_from is None else [_row_spec(rows[0][1], dout_from[0].shape[1], 0), _full_spec(dout_from[1].shape)]
    res = pl.pallas_call(
        body, name=name, grid=(n_steps,),
        in_specs=in_specs + across_specs + from_specs,
        out_specs=[_row_spec(th, cw, 0) for (_, th, cw, _) in rows] + [_full_spec(p.shape) for p in params] + across_specs,
        out_shape=[jax.ShapeDtypeStruct((n_steps * th, cw), dt) for (_, th, cw, _), dt in zip(rows, row_dtypes)]
        + [jax.ShapeDtypeStruct(p.shape, F32) for p in params]
        + [jax.ShapeDtypeStruct(across[0].shape, F32) for _ in across_specs],
        compiler_params=_params(("arbitrary",)),
    )(*[r[0] for r in list(rows) + list(aux)], *params, *[r[0] for r in all_rows[n_r + n_a:]],
      *[across[0] for _ in across_specs], *(dout_from or ()))
    return res[:n_r], res[n_r:]


def _sigmoid(x):
    return lax.logistic(x)


def _silu(x):
    return x * _sigmoid(x)


def _rms(x, w=None, n=None):
    n = n or x.shape[-1]
    y = x * lax.rsqrt(jnp.sum(x * x, axis=-1, keepdims=True) * (1.0 / n) + EPS)
    return y if w is None else y * w


def _modulate(x, scale, shift):
    return _rms(x) * (1.0 + scale) + shift


def _softplus(x):
    return jnp.maximum(x, 0.0) + jnp.log1p(jnp.exp(-jnp.abs(x)))


@jax.custom_vjp
def _rot_half64(x):
    lane = lax.broadcasted_iota(jnp.int32, x.shape, 1)
    up = pltpu.roll(x, 96, 1)
    down = pltpu.roll(x, 32, 1)
    return jnp.where(lane < 32, up, jnp.where(lane < 64, down, 0.0))


_rot_half64.defvjp(lambda x: (_rot_half64(x), None), lambda _, g: (_rot_half64(g),))


def _rope128(x, cos_p, sin_p):
    return x * cos_p + _rot_half64(x) * sin_p


def _gdn_pre_fn(qkvc, kab, alog_p, dt_p):
    a = _silu(qkvc)
    qs, ks = [], []
    for h in range(HEADS):
        qh = a[:, HEAD_DIM * h:HEAD_DIM * (h + 1)]
        kh = a[:, 512 + HEAD_DIM * h:512 + HEAD_DIM * (h + 1)]
        qs.append(qh * lax.rsqrt(jnp.sum(qh * qh, axis=-1, keepdims=True) + EPS) * (HEAD_DIM ** -0.5))
        ks.append(kh * lax.rsqrt(jnp.sum(kh * kh, axis=-1, keepdims=True) + EPS))
    lane = lax.broadcasted_iota(jnp.int32, kab.shape, 1)
    g_full = -jnp.exp(alog_p) * _softplus(kab + dt_p)
    b_full = _sigmoid(kab)
    gb = jnp.where((lane >= 64) & (lane < 68), g_full, jnp.where((lane >= 68) & (lane < 72), b_full, 0.0))
    return jnp.concatenate(qs, axis=1), jnp.concatenate(ks, axis=1), a[:, 1024:1536], gb


INTRA_ROWS = (512, 256, 128, 64)
TOKEN_ROWS = (512, 256, 128)

_BNN = (((2,), (1,)), ((0,), (0,)))
_BNT = (((2,), (2,)), ((0,), (0,)))
_BTN = (((1,), (1,)), ((0,), (0,)))


def _split_bf16(a):
    hi = a.astype(BF16)
    return hi, (a - hi.astype(F32)).astype(BF16)


def _dot3_raw(a, b, dims):
    a_hi, a_lo = _split_bf16(a)
    b_hi, b_lo = _split_bf16(b)
    dot = lambda x_, y_: lax.dot_general(x_, y_, dims, preferred_element_type=F32)
    return dot(a_hi, b_hi) + (dot(a_hi, b_lo) + dot(a_lo, b_hi))


@functools.partial(jax.custom_vjp, nondiff_argnums=(2, 3))
def _dot3(a, b, nt, exact_bwd=True):
    return _dot3_raw(a, b, _BNT if nt else _BNN)


def _dot3_fwd(a, b, nt, exact_bwd):
    return _dot3_raw(a, b, _BNT if nt else _BNN), (a, b)


def _dot3_bwd(nt, exact_bwd, res, g):
    a, b = res
    if exact_bwd:
        dot = _dot3_raw
    else:
        dot = lambda x_, y_, d_: lax.dot_general(x_.astype(BF16), y_.astype(BF16), d_, preferred_element_type=F32)
    if nt:
        return dot(g, b, _BNN), dot(jnp.swapaxes(g, 1, 2), a, _BNN)
    return dot(g, b, _BNT), dot(jnp.swapaxes(a, 1, 2), g, _BNN)


_dot3.defvjp(_dot3_fwd, _dot3_bwd)


@functools.partial(jax.custom_vjp, nondiff_argnums=(2,))
def _bdot_b(a, b, nt):
    return lax.dot_general(a.astype(BF16), b.astype(BF16), _BNT if nt else _BNN, preferred_element_type=F32)


def _bdot_b_fwd(a, b, nt):
    return _bdot_b(a, b, nt), (a, b)


def _bdot_b_bwd(nt, res, g):
    a, b = res
    dot = lambda x_, y_, d_: lax.dot_general(x_.astype(BF16), y_.astype(BF16), d_, preferred_element_type=F32)
    if nt:
        return dot(g, b, _BNN), dot(jnp.swapaxes(g, 1, 2), a, _BNN)
    return dot(g, b, _BNT), dot(jnp.swapaxes(a, 1, 2), g, _BNN)


_bdot_b.defvjp(_bdot_b_fwd, _bdot_b_bwd)


@jax.custom_vjp
def _inverse_given(a_mat, inv):
    return inv


def _inverse_given_bwd(inv, g):
    inv_t = jnp.swapaxes(inv, 1, 2)
    return -_dot3_raw(_dot3_raw(inv_t, g, _BNN), inv_t, _BNN), jnp.zeros_like(inv)


_inverse_given.defvjp(lambda a_mat, inv: (inv, inv), _inverse_given_bwd)


def _intra_batched(q, k, v, g_col, b_col, inv_known=None):
    c = CHUNK
    nb = q.shape[0]
    row = lax.broadcasted_iota(jnp.int32, (1, c, c), 1)
    col = lax.broadcasted_iota(jnp.int32, (1, c, c), 2)
    incl, strict, eye = row >= col, row > col, row == col
    tri = jnp.broadcast_to(jnp.where(incl, 1.0, 0.0).astype(F32), (nb, c, c))
    ident = jnp.where(eye, 1.0, 0.0).astype(F32)
    g_wide = _dot3(tri, jnp.broadcast_to(g_col, (nb, c, HEAD_DIM)), False)
    g_i = g_wide[:, :, :c]
    g_j = jnp.sum(jnp.where(eye, g_i, 0.0), axis=1, keepdims=True)
    decay = jnp.where(incl, jnp.exp(jnp.where(incl, g_i - g_j, 0.0)), 0.0)
    kk = _bdot_b(k, k, True)
    a_mat = jnp.where(strict, b_col * kk * decay, 0.0)
    if inv_known is None:
        x_pow = -a_mat
        inv = ident + x_pow
        for _ in range(5):
            x_pow = _dot3(x_pow, x_pow, False, False)
            inv = inv + _dot3(inv, x_pow, False, False)
    else:
        inv = _inverse_given(a_mat, inv_known)
    e_wide = jnp.exp(g_wide)
    u = _dot3(inv, v * b_col, False)
    wk = _dot3(inv, k * b_col * e_wide, False)
    qk = _bdot_b(q, k, True) * decay
    last = lax.broadcasted_iota(jnp.int32, (1, c, HEAD_DIM), 1) == c - 1
    g_last = jnp.sum(jnp.where(last, g_wide, 0.0), axis=1, keepdims=True)
    qd = q * e_wide
    kd = k * jnp.exp(g_last - g_wide)
    gl = jnp.broadcast_to(jnp.exp(g_last), (nb, 8, HEAD_DIM))
    return u, wk, qd, kd, qk, gl, inv


def _gdn_intra_fn(q, k, v, gb, *inv_known):
    t = q.shape[0]
    nch = t // CHUNK
    lane = lax.broadcasted_iota(jnp.int32, gb.shape, 1)

    def heads_first(x_):
        return jnp.concatenate([x_[:, HEAD_DIM * h:HEAD_DIM * (h + 1)].reshape(nch, CHUNK, HEAD_DIM) for h in range(HEADS)],
                               axis=0)

    def column(first_lane):
        return jnp.concatenate([jnp.sum(jnp.where(lane == first_lane + h, gb, 0.0), axis=1, keepdims=True)
                                .reshape(nch, CHUNK, 1) for h in range(HEADS)], axis=0)

    known = jnp.concatenate([x_.reshape(nch, CHUNK, CHUNK) for x_ in inv_known], axis=0) if inv_known else None
    u, wk, qd, kd, qk, gl, inv = _intra_batched(heads_first(q), heads_first(k), heads_first(v), column(64), column(68), known)

    def rows_first(x_):
        r, w_ = x_.shape[1], x_.shape[2]
        return jnp.concatenate([x_[nch * h:nch * (h + 1)].reshape(nch * r, w_) for h in range(HEADS)], axis=1)

    per_head = lambda x_: [x_[nch * h:nch * (h + 1)].reshape(t, CHUNK) for h in range(HEADS)]
    outs = (rows_first(u), rows_first(wk), rows_first(qd), rows_first(kd), *per_head(qk), rows_first(gl))
    return outs if inv_known else outs + tuple(per_head(inv))


def _scan_step(s0, u, wk, qd, kd, qk, gl):
    v_new = u - _bdot_b(wk, s0, False)
    o = _bdot_b(qd, s0, False) + _bdot_b(qk, v_new, False)
    s1 = s0 * gl[:, 0:1, :] + _bdot_b(jnp.swapaxes(kd, 1, 2), v_new, False)
    return o, s1


def _mix_post_fn(o_a, z, o_b, gnw, onw):
    parts = [_rms(o_a[:, HEAD_DIM * h:HEAD_DIM * (h + 1)], gnw) * _silu(z[:, HEAD_DIM * h:HEAD_DIM * (h + 1)])
             for h in range(HEADS)]
    parts += [_rms(o_b[:, HEAD_DIM * h:HEAD_DIM * (h + 1)], onw) for h in range(HEADS)]
    return (jnp.concatenate(parts, axis=1),)


def _mla_pre_fn(ckv, cq, kab, cos_p, sin_p, qnw, kvnw, wuq, wukv, qn_w, qr_w, kn_w, kr_w):
    scale = (HEAD_DIM + ROPE) ** -0.5 * LOG2_E
    qf = _bdot(_rms(cq, qnw), wuq, "nn")
    kvf = _bdot(_rms(ckv, kvnw), wukv, "nn")
    lane = lax.broadcasted_iota(jnp.int32, kab.shape, 1)
    kr = _rope128(_rms(jnp.where(lane < ROPE, kab, 0.0), kr_w, n=ROPE), cos_p, sin_p)
    qs, ks = [], []
    for h in range(HEADS):
        qn = _rms(qf[:, 256 * h:256 * h + 128], qn_w) * scale
        qr = _rope128(_rms(qf[:, 256 * h + 128:256 * h + 256], qr_w, n=ROPE), cos_p, sin_p) * scale
        qs += [qn, qr]
        ks += [_rms(kvf[:, 128 * h:128 * (h + 1)], kn_w), kr]
    return jnp.concatenate(qs, axis=1), jnp.concatenate(ks, axis=1), kvf[:, 512:]


def _conv_fwd(proj, conv_w, alog_p, dt_p, tm, name):
    S = proj.shape[0]
    C = 1536
    nb = tm // 8

    def body(x_ref, prev_ref, kab_ref, w_ref, al_ref, dt_ref, o_ref, q_ref, k_ref, v_ref, gb_ref, ext_ref):
        i = pl.program_id(0)
        ext_ref[0:8, :] = jnp.where(i > 0, prev_ref[...], 0.0)
        ext_ref[8:, :] = x_ref[...]
        acc = jnp.zeros((tm, C), F32)
        for k in range(4):
            acc = acc + w_ref[k:k + 1, :] * ext_ref[pl.ds(5 + k, tm), :]
        o_ref[...] = acc
        for ref, val in zip((q_ref, k_ref, v_ref, gb_ref), _gdn_pre_fn(acc, kab_ref[...], al_ref[...], dt_ref[...])):
            ref[...] = val

    third = pl.BlockSpec((tm, 512), lambda i: (i, 0))
    return pl.pallas_call(
        body, name=name, grid=(S // tm,),
        in_specs=[pl.BlockSpec((tm, C), lambda i: (i, 0)),
                  pl.BlockSpec((8, C), lambda i: (jnp.maximum(i * nb - 1, 0), 0)),
                  _row_spec(tm, 128, 21), pl.BlockSpec((4, C), lambda i: (0, 0)), _full_spec(alog_p.shape), _full_spec(dt_p.shape)],
        out_specs=[pl.BlockSpec((tm, C), lambda i: (i, 0)), third, third, third, _row_spec(tm, 128, 0)],
        out_shape=[jax.ShapeDtypeStruct((S, C), F32)] + [jax.ShapeDtypeStruct((S, 512), F32)] * 3
        + [jax.ShapeDtypeStruct((S, 128), F32)],
        scratch_shapes=[pltpu.VMEM((tm + 8, C), F32)],
        compiler_params=_params(("arbitrary",)),
    )(proj, proj, proj, conv_w, alog_p, dt_p)


def _conv_bwd(proj, qkvc, conv_w, alog_p, dt_p, douts, d_kab_add, tm, name):
    S = proj.shape[0]
    C = 1536
    nb = tm // 8
    n_steps = S // tm

    def body(x_ref, prev_ref, c_ref, kab_ref, w_ref, al_ref, dt_ref, dq_ref, dk_ref, dv_ref, dgb_ref, add_ref,
             dx_ref, dkab_ref, dw_ref, dal_ref, ddt_ref, xext_ref, dext_ref, halo_ref):
        step = pl.program_id(0)

        @pl.when(step == 0)
        def _():
            halo_ref[...] = jnp.zeros_like(halo_ref)
            for r in (dw_ref, dal_ref, ddt_ref):
                r[...] = jnp.zeros_like(r)

        _, vjp = jax.vjp(_gdn_pre_fn, c_ref[...], kab_ref[...], al_ref[...], dt_ref[...])
        d, d_kab, d_al, d_dt = vjp((dq_ref[...], dk_ref[...], dv_ref[...], dgb_ref[...]))
        dkab_ref[...] = (d_kab + add_ref[...]).astype(dkab_ref.dtype)
        xext_ref[0:8, :] = jnp.where(step < n_steps - 1, prev_ref[...], 0.0)
        xext_ref[8:, :] = x_ref[...]
        dext_ref[0:tm, :] = d
        dext_ref[tm:, :] = halo_ref[...]
        halo_ref[...] = d[0:8, :]
        acc = jnp.zeros((tm, C), F32)
        dws = []
        for k in range(4):
            acc = acc + w_ref[k:k + 1, :] * dext_ref[pl.ds(3 - k, tm), :]
            dws.append(jnp.sum(d * xext_ref[pl.ds(5 + k, tm), :], axis=0, keepdims=True))
        dx_ref[...] = acc.astype(dx_ref.dtype)
        dw_ref[...] += jnp.concatenate(dws + [jnp.zeros((4, C), F32)], axis=0)
        dal_ref[...] += d_al
        ddt_ref[...] += d_dt

    tile = lambda cols, block=0: pl.BlockSpec((tm, cols), lambda s_: (n_steps - 1 - s_, block))
    return pl.pallas_call(
        body, name=name, grid=(n_steps,),
        in_specs=[tile(C), pl.BlockSpec((8, C), lambda s_: (jnp.maximum((n_steps - 1 - s_) * nb - 1, 0), 0)),
                  tile(C), tile(128, 21), pl.BlockSpec((4, C), lambda s_: (0, 0)), _full_spec(alog_p.shape), _full_spec(dt_p.shape),
                  tile(512), tile(512), tile(512), tile(128), tile(128)],
        out_specs=[tile(C), tile(128), pl.BlockSpec((8, C), lambda s_: (0, 0)), _full_spec(alog_p.shape), _full_spec(dt_p.shape)],
        out_shape=[jax.ShapeDtypeStruct((S, C), BF16), jax.ShapeDtypeStruct((S, 128), BF16), jax.ShapeDtypeStruct((8, C), F32),
                   jax.ShapeDtypeStruct(alog_p.shape, F32), jax.ShapeDtypeStruct(dt_p.shape, F32)],
        scratch_shapes=[pltpu.VMEM((tm + 8, C), F32), pltpu.VMEM((tm + 8, C), F32), pltpu.VMEM((8, C), F32)],
        compiler_params=_params(("arbitrary",)),
    )(proj, proj, qkvc, proj, conv_w, alog_p, dt_p, *douts, d_kab_add)


SCAN_CHUNKS = (8, 4, 2, 1)


def _gdn_scan_fwd(u, wk, qd, kd, qks, gl, name):
    S = u.shape[0]
    nc = S // CHUNK
    cs = _pick(nc, SCAN_CHUNKS)
    W = HEADS * HEAD_DIM

    def body(u_ref, wk_ref, qd_ref, kd_ref, qk0, qk1, qk2, qk3, gl_ref, o_ref, sp_ref, s_ref):
        @pl.when(pl.program_id(0) == 0)
        def _():
            s_ref[...] = jnp.zeros_like(s_ref)

        state = s_ref[...]
        for c in range(cs):
            rows, gl_rows = slice(CHUNK * c, CHUNK * (c + 1)), slice(8 * c, 8 * (c + 1))
            sp_ref[c] = state
            o, state = _scan_step(state, _heads(u_ref, HEAD_DIM, rows), _heads(wk_ref, HEAD_DIM, rows),
                                  _heads(qd_ref, HEAD_DIM, rows), _heads(kd_ref, HEAD_DIM, rows),
                                  jnp.stack([r[rows, :] for r in (qk0, qk1, qk2, qk3)]), _heads(gl_ref, HEAD_DIM, gl_rows))
            for h in range(HEADS):
                o_ref[rows, HEAD_DIM * h:HEAD_DIM * (h + 1)] = o[h]
        s_ref[...] = state

    row = pl.BlockSpec((cs * CHUNK, W), lambda n: (n, 0))
    qk_spec = pl.BlockSpec((cs * CHUNK, CHUNK), lambda n: (n, 0))
    return pl.pallas_call(
        body, name=name, grid=(nc // cs,),
        in_specs=[row, row, row, row, qk_spec, qk_spec, qk_spec, qk_spec, pl.BlockSpec((cs * 8, W), lambda n: (n, 0))],
        out_specs=[row, pl.BlockSpec((cs, HEADS, HEAD_DIM, HEAD_DIM), lambda n: (n, 0, 0, 0))],
        out_shape=[jax.ShapeDtypeStruct((S, W), F32), jax.ShapeDtypeStruct((nc, HEADS, HEAD_DIM, HEAD_DIM), F32)],
        scratch_shapes=[pltpu.VMEM((HEADS, HEAD_DIM, HEAD_DIM), F32)],
        compiler_params=_params(("arbitrary",)),
    )(u, wk, qd, kd, *qks, gl)


def _gdn_scan_bwd(u, wk, qd, kd, qks, gl, s_prev, d_o, name):
    S = u.shape[0]
    nc = S // CHUNK
    cs = _pick(nc, SCAN_CHUNKS)
    nb = nc // cs
    W = HEADS * HEAD_DIM

    def body(u_ref, wk_ref, qd_ref, kd_ref, qk0, qk1, qk2, qk3, gl_ref, sp_ref, do_ref,
             du_ref, dwk_ref, dqd_ref, dkd_ref, dqk0, dqk1, dqk2, dqk3, dgl_ref, ds_ref):
        @pl.when(pl.program_id(0) == 0)
        def _():
            ds_ref[...] = jnp.zeros_like(ds_ref)

        d_state = ds_ref[...]
        for c in reversed(range(cs)):
            rows, gl_rows = slice(CHUNK * c, CHUNK * (c + 1)), slice(8 * c, 8 * (c + 1))
            _, vjp = jax.vjp(_scan_step, sp_ref[c], _heads(u_ref, HEAD_DIM, rows), _heads(wk_ref, HEAD_DIM, rows),
                             _heads(qd_ref, HEAD_DIM, rows), _heads(kd_ref, HEAD_DIM, rows),
                             jnp.stack([r[rows, :] for r in (qk0, qk1, qk2, qk3)]), _heads(gl_ref, HEAD_DIM, gl_rows))
            d_state, du, dwk, dqd, dkd, dqk, dgl = vjp((_heads(do_ref, HEAD_DIM, rows), d_state))
            for h, dqk_ref in enumerate((dqk0, dqk1, dqk2, dqk3)):
                sl = slice(HEAD_DIM * h, HEAD_DIM * (h + 1))
                du_ref[rows, sl] = du[h]
                dwk_ref[rows, sl] = dwk[h]
                dqd_ref[rows, sl] = dqd[h]
                dkd_ref[rows, sl] = dkd[h]
                dqk_ref[rows, :] = dqk[h]
                dgl_ref[gl_rows, sl] = dgl[h]
        ds_ref[...] = d_state

    rev = lambda n: (nb - 1 - n, 0)
    row = pl.BlockSpec((cs * CHUNK, W), rev)
    qk_spec = pl.BlockSpec((cs * CHUNK, CHUNK), rev)
    gl_spec = pl.BlockSpec((cs * 8, W), rev)
    qk_shape = jax.ShapeDtypeStruct((S, CHUNK), F32)
    row_shape = jax.ShapeDtypeStruct((S, W), F32)
    return pl.pallas_call(
        body, name=name, grid=(nb,),
        in_specs=[row, row, row, row, qk_spec, qk_spec, qk_spec, qk_spec, gl_spec,
                  pl.BlockSpec((cs, HEADS, HEAD_DIM, HEAD_DIM), lambda n: (nb - 1 - n, 0, 0, 0)), row],
        out_specs=[row, row, row, row, qk_spec, qk_spec, qk_spec, qk_spec, gl_spec],
        out_shape=[row_shape] * 4 + [qk_shape] * 4 + [jax.ShapeDtypeStruct((nc * 8, W), F32)],
        scratch_shapes=[pltpu.VMEM((HEADS, HEAD_DIM, HEAD_DIM), F32)],
        compiler_params=_params(("arbitrary",)),
    )(u, wk, qd, kd, *qks, gl, s_prev, d_o)


NEG = -1e30


def _chunk_mask(i, j, t, transposed=False):
    q_axis, k_axis = (1, 0) if transposed else (0, 1)
    r = (i * t + lax.broadcasted_iota(jnp.int32, (t, t), q_axis)) // CHUNK
    c = (j * t + lax.broadcasted_iota(jnp.int32, (t, t), k_axis)) // CHUNK
    return c <= r


def _tile_pairs(n, by_key):
    pairs = [(i, j) for j in range(n) for i in range(j, n)] if by_key else [(i, j) for i in range(n) for j in range(i + 1)]
    return jnp.asarray(np.array([p[0] for p in pairs], np.int32)), jnp.asarray(np.array([p[1] for p in pairs], np.int32))


def _heads(ref, width, rows=slice(None)):
    return jnp.stack([ref[rows, width * h:width * (h + 1)] for h in range(HEADS)])


def _bmm(a, b, dims):
    return lax.dot_general(a.astype(BF16), b.astype(BF16), dims, preferred_element_type=F32)


def _attn_fwd(q, k, v_t, t, name):
    S = q.shape[0]
    n = S // t
    qi, kj = _tile_pairs(n, by_key=False)

    def body(qi_ref, kj_ref, q_ref, k_ref, vt_ref, o_ref, lse_ref, m_ref, l_ref, acc_ref):
        i, j = qi_ref[pl.program_id(0)], kj_ref[pl.program_id(0)]

        @pl.when(j == 0)
        def _():
            m_ref[...] = jnp.full_like(m_ref, NEG)
            l_ref[...] = jnp.zeros_like(l_ref)
            acc_ref[...] = jnp.zeros_like(acc_ref)

        def update(masked):
            s_t = _bmm(_heads(k_ref, 256), _heads(q_ref, 256), _BNT)
            if masked:
                s_t = jnp.where(_chunk_mask(i, j, t, transposed=True)[None], s_t, NEG)
            m_old = m_ref[...]
            m_new = jnp.maximum(m_old, jnp.max(s_t, axis=1, keepdims=True))
            p_t = jnp.exp2(s_t - m_new)
            alpha = jnp.exp2(m_old - m_new)
            l_ref[...] = alpha * l_ref[...] + jnp.sum(p_t, axis=1, keepdims=True)
            v_heads = jnp.stack([vt_ref[HEAD_DIM * h:HEAD_DIM * (h + 1), :] for h in range(HEADS)])
            acc_ref[...] = alpha * acc_ref[...] + _bmm(v_heads, p_t, _BNN)
            m_ref[...] = m_new

        @pl.when(j < i)
        def _():
            update(False)

        @pl.when(j == i)
        def _():
            update(True)
            for h in range(HEADS):
                sl = slice(HEAD_DIM * h, HEAD_DIM * (h + 1))
                o_ref[:, sl] = jnp.transpose(acc_ref[h] / l_ref[h])
                lse_ref[h:h + 1, :] = m_ref[h] + jnp.log(l_ref[h]) * LOG2_E
            lse_ref[HEADS:, :] = jnp.zeros((8 - HEADS, t), F32)

    row = lambda p, qi_, kj_: (qi_[p], 0)
    return pl.pallas_call(
        body, name=name,
        grid_spec=pltpu.PrefetchScalarGridSpec(
            num_scalar_prefetch=2, grid=(qi.shape[0],),
            in_specs=[pl.BlockSpec((t, HEADS * 256), row), pl.BlockSpec((t, HEADS * 256), lambda p, qi_, kj_: (kj_[p], 0)),
                      pl.BlockSpec((HEADS * HEAD_DIM, t), lambda p, qi_, kj_: (0, kj_[p]))],
            out_specs=[pl.BlockSpec((t, HEADS * HEAD_DIM), row), pl.BlockSpec((8, t), lambda p, qi_, kj_: (0, qi_[p]))],
            scratch_shapes=[pltpu.VMEM((HEADS, 1, t), F32), pltpu.VMEM((HEADS, 1, t), F32),
                            pltpu.VMEM((HEADS, HEAD_DIM, t), F32)]),
        out_shape=[jax.ShapeDtypeStruct((S, HEADS * HEAD_DIM), F32), jax.ShapeDtypeStruct((8, S), F32)],
        compiler_params=_params(("arbitrary",)),
    )(qi, kj, q, k, v_t)


def _attn_stats(o, d_o, lse):
    lane = lax.broadcasted_iota(jnp.int32, (o.shape[0], HEAD_DIM), 1)
    stats = jnp.zeros((o.shape[0], HEAD_DIM), F32)
    for h in range(HEADS):
        sl = slice(HEAD_DIM * h, HEAD_DIM * (h + 1))
        delta = jnp.sum(d_o[:, sl] * o[:, sl], axis=1, keepdims=True)
        stats = stats + jnp.where(lane == HEADS + h, delta, 0.0)
    return lse + jnp.transpose(stats)[0:8, :]


BWD_GROUP = 2


def _attn_bwd(q, k, v, d_o, stats, t, name):
    S = q.shape[0]
    n = S // t
    groups = HEADS // BWD_GROUP
    gq, gv = BWD_GROUP * 256, BWD_GROUP * HEAD_DIM
    qi, kj = _tile_pairs(n, by_key=True)
    n_pairs = qi.shape[0]
    st = stats.reshape(2, groups, BWD_GROUP, S).transpose(1, 0, 2, 3).reshape(groups, 2 * BWD_GROUP, S)
    st = jnp.pad(st, ((0, 0), (0, 8 - 2 * BWD_GROUP), (0, 0)))

    def heads(ref, width, rows=slice(None)):
        return jnp.stack([ref[rows, width * h:width * (h + 1)] for h in range(BWD_GROUP)])

    def body(qi_ref, kj_ref, q_ref, k_ref, v_ref, do_ref, st_ref, dq_hbm, dk_ref, dv_ref, dq_acc, sem):
        g, p = pl.program_id(0), pl.program_id(1)
        i, j = qi_ref[p], kj_ref[p]

        @pl.when(i == j)
        def _():
            dk_ref[...] = jnp.zeros_like(dk_ref)
            dv_ref[...] = jnp.zeros_like(dv_ref)

        def update(masked):
            qh, kh = heads(q_ref, 256), heads(k_ref, 256)
            d_out = heads(do_ref, HEAD_DIM)
            stv = st_ref[...]
            lse_row = jnp.stack([stv[h:h + 1, :] for h in range(BWD_GROUP)])
            delta_row = jnp.stack([stv[BWD_GROUP + h:BWD_GROUP + h + 1, :] for h in range(BWD_GROUP)])
            s_t = _bmm(kh, qh, _BNT)
            p_t = jnp.exp2(s_t - lse_row)
            if masked:
                p_t = jnp.where(_chunk_mask(i, j, t, transposed=True)[None], p_t, 0.0)
            dv = _bmm(p_t, d_out, _BNN)
            dp_t = _bmm(heads(v_ref, HEAD_DIM), d_out, _BNT)
            ds_t = p_t * (dp_t - delta_row)
            dk = _bmm(ds_t, qh, _BNN)
            dq = _bmm(ds_t, kh, _BTN)
            rows = pl.ds(pl.multiple_of(i * t, t), t)
            for h in range(BWD_GROUP):
                dk_ref[:, 256 * h:256 * (h + 1)] += dk[h]
                dv_ref[:, HEAD_DIM * h:HEAD_DIM * (h + 1)] += dv[h]

            @pl.when(j == 0)
            def _():
                for h in range(BWD_GROUP):
                    dq_acc[rows, 256 * h:256 * (h + 1)] = dq[h]

            @pl.when(j > 0)
            def _():
                for h in range(BWD_GROUP):
                    dq_acc[rows, 256 * h:256 * (h + 1)] += dq[h]

        def dq_copy(block, gg):
            rows = pl.ds(pl.multiple_of(block * t, t), t)
            return pltpu.make_async_copy(dq_acc.at[rows, :], dq_hbm.at[rows, gq * gg:gq * (gg + 1)], sem.at[block])

        @pl.when(i == j)
        def _():
            update(True)
            rows = pl.ds(pl.multiple_of(i * t, t), t)
            dq_acc[rows, :] *= 1.0 / LOG2_E
            for gg in range(groups):
                @pl.when(g == gg)
                def _():
                    dq_copy(i, gg).start()

        @pl.when(i > j)
        def _():
            update(False)

        @pl.when(i == n - 1)
        def _():
            dk_ref[...] *= 1.0 / LOG2_E

        @pl.when(p == n_pairs - 1)
        def _():
            for gg in range(groups):
                @pl.when(g == gg)
                def _():
                    for block in range(n):
                        dq_copy(block, gg).wait()

    q_blk = lambda g, p, qi_, kj_: (qi_[p], g)
    k_blk = lambda g, p, qi_, kj_: (kj_[p], g)
    return pl.pallas_call(
        body, name=name,
        grid_spec=pltpu.PrefetchScalarGridSpec(
            num_scalar_prefetch=2, grid=(groups, n_pairs),
            in_specs=[pl.BlockSpec((t, gq), q_blk), pl.BlockSpec((t, gq), k_blk), pl.BlockSpec((t, gv), k_blk),
                      pl.BlockSpec((t, gv), q_blk), pl.BlockSpec((None, 8, t), lambda g, p, qi_, kj_: (g, 0, qi_[p]))],
            out_specs=[pl.BlockSpec(memory_space=pl.ANY), pl.BlockSpec((t, gq), k_blk), pl.BlockSpec((t, gv), k_blk)],
            scratch_shapes=[pltpu.VMEM((S, gq), F32), pltpu.SemaphoreType.DMA((n,))]),
        out_shape=[jax.ShapeDtypeStruct((S, HEADS * 256), F32), jax.ShapeDtypeStruct((S, HEADS * 256), F32),
                   jax.ShapeDtypeStruct((S, HEADS * HEAD_DIM), F32)],
        compiler_params=_params(("arbitrary", "arbitrary")),
    )(qi, kj, q, k, v, d_o, st)


FFN_PIECE = 2 * D_FF // N_DEV
HID_PIECES = D_FF // FFN_PIECE


def _after_specs(after):
    return [] if after is None else [pl.BlockSpec(memory_space=pl.ANY)]


def _after_args(after):
    return [] if after is None else [after]


def _ffn_gw8(h, d_gate, d_up, name, after=None):
    S = h.shape[0]
    tm = 512
    tk = _pick(S, MATMUL_ROWS)
    nk = S // tk

    def body(h_ref, dg_ref, du_ref, *rest):
        o_ref, acc_ref = rest[-2:]
        k = pl.program_id(1)

        @pl.when(k == 0)
        def _():
            acc_ref[...] = jnp.zeros_like(acc_ref)

        h_t = jnp.transpose(h_ref[...])
        for p in range(HID_PIECES):
            acc_ref[p] += _dot_raw(h_t, dg_ref[p], "nn")
            acc_ref[HID_PIECES + p] += _dot_raw(h_t, du_ref[p], "nn")

        @pl.when(k == nk - 1)
        def _():
            o_ref[...] = acc_ref[...].astype(o_ref.dtype)

    d_spec = pl.BlockSpec((HID_PIECES, tk, FFN_PIECE), lambda i, k: (0, k, 0))
    return pl.pallas_call(
        body, name=name, grid=(D_MODEL // tm, nk),
        in_specs=[pl.BlockSpec((tk, tm), lambda i, k: (k, i)), d_spec, d_spec] + _after_specs(after),
        out_specs=pl.BlockSpec((2 * HID_PIECES, tm, FFN_PIECE), lambda i, k: (0, i, 0)),
        out_shape=jax.ShapeDtypeStruct((2 * HID_PIECES, D_MODEL, FFN_PIECE), BF16),
        scratch_shapes=[pltpu.VMEM((2 * HID_PIECES, tm, FFN_PIECE), F32)],
        compiler_params=_params(("parallel", "arbitrary")),
    )(h, d_gate, d_up, *_after_args(after))


EPILOGUE_ROWS = 256


def _dmod_epilogue(acc_ref, x_ref, do_ref, sc_ref, sh_ref, dx_ref, dsc_ref, dsh_ref, below, below_in, below_out):
    rows_total = acc_ref.shape[0]
    step = min(EPILOGUE_ROWS, rows_total)
    dsc, dsh, dg = 0.0, 0.0, 0.0
    for r in range(rows_total // step):
        rows = slice(step * r, step * (r + 1))
        _, vjp = jax.vjp(_modulate, x_ref[rows, :], sc_ref[...], sh_ref[...])
        dx, dsc_r, dsh_r = vjp(acc_ref[rows, :])
        dx = dx + do_ref[rows, :]
        dx_ref[rows, :] = dx
        dsc, dsh = dsc + dsc_r, dsh + dsh_r
        if below is not None:
            coef = below[2]
            below_out[0][rows, :] = (coef * below_in[1][...] * dx).astype(below_out[0].dtype)
            dg = dg + jnp.sum(coef * below_in[0][rows, :] * dx, axis=0, keepdims=True)
    dsc_ref[...] += dsc
    dsh_ref[...] += dsh
    if below is not None:
        below_out[1][...] += dg


def _ffn_dh(d_gate, d_up, w8, x, d_out, scale, shift, name, after=None, below=None):
    S = d_gate.shape[1]
    tm = _pick(S, MATMUL_ROWS)
    n_below = 0 if below is None else 2

    def body(dg_ref, du_ref, wg_ref, wu_ref, x_ref, do_ref, sc_ref, sh_ref, *rest):
        below_in = rest[:n_below]
        outs = rest[len(rest) - 4 - n_below:]
        dx_ref, dsc_ref, dsh_ref = outs[:3]
        below_out, acc_ref = outs[3:3 + n_below], outs[-1]
        i, k = pl.program_id(0), pl.program_id(1)

        @pl.when(k == 0)
        def _():
            acc_ref[...] = jnp.zeros_like(acc_ref)

        acc_ref[...] += _dot_raw(dg_ref[...], wg_ref[...], "nt") + _dot_raw(du_ref[...], wu_ref[...], "nt")

        @pl.when((k == 0) & (i == 0))
        def _():
            for r in (dsc_ref, dsh_ref) + tuple(below_out[1:]):
                r[...] = jnp.zeros_like(r)

        @pl.when(k == HID_PIECES - 1)
        def _():
            _dmod_epilogue(acc_ref, x_ref, do_ref, sc_ref, sh_ref, dx_ref, dsc_ref, dsh_ref, below, below_in, below_out)

    d_spec = pl.BlockSpec((None, tm, FFN_PIECE), lambda i, k: (k, i, 0))
    row = pl.BlockSpec((tm, D_MODEL), lambda i, k: (i, 0))
    par = pl.BlockSpec((1, D_MODEL), lambda i, k: (0, 0))
    row_shape, par_shape = jax.ShapeDtypeStruct((S, D_MODEL), F32), jax.ShapeDtypeStruct((1, D_MODEL), F32)
    return pl.pallas_call(
        body, name=name, grid=(S // tm, HID_PIECES),
        in_specs=[d_spec, d_spec,
                  pl.BlockSpec((None, D_MODEL, FFN_PIECE), lambda i, k: (k, 0, 0)),
                  pl.BlockSpec((None, D_MODEL, FFN_PIECE), lambda i, k: (k + HID_PIECES, 0, 0)),
                  row, row, par, par] + [row, par][:n_below] + _after_specs(after),
        out_specs=[row, par, par] + [row, par][:n_below],
        out_shape=[row_shape, par_shape, par_shape] + [jax.ShapeDtypeStruct((S, D_MODEL), BF16), par_shape][:n_below],
        scratch_shapes=[pltpu.VMEM((tm, D_MODEL), F32)],
        compiler_params=_params(("arbitrary", "arbitrary")),
    )(d_gate, d_up, w8, w8, x, d_out, scale, shift, *(below[:2] if below is not None else ()), *_after_args(after))


def _swiglu_bwd(d_hid, hid_by_gate, hid_by_up):
    return d_hid * hid_by_gate, d_hid * hid_by_up


def _adamw_math(w_, g_, m_, v_):
    m_ = ADAM_B1 * m_ + (1.0 - ADAM_B1) * g_
    v_ = ADAM_B2 * v_ + (1.0 - ADAM_B2) * (g_ * g_)
    m_hat = m_ / (1.0 - ADAM_B1 ** ADAM_STEP)
    v_hat = v_ / (1.0 - ADAM_B2 ** ADAM_STEP)
    return -ADAM_LR * (m_hat / (jnp.sqrt(v_hat) + ADAM_EPS) + ADAM_WD * w_), m_, v_


def _adamw(w, g, m, v, name):
    R, C = w.shape
    tr = _pick(R, (256, 176, 128, 64, 32, 16, 8))

    def body(w_ref, g_ref, m_ref, v_ref, d_ref, nm_ref, nv_ref):
        d_ref[...], nm_ref[...], nv_ref[...] = _adamw_math(w_ref[...], g_ref[...], m_ref[...], v_ref[...])

    spec = pl.BlockSpec((tr, C), lambda i: (i, 0))
    return pl.pallas_call(
        body, name=name, grid=(R // tr,),
        in_specs=[spec] * 4, out_specs=[spec] * 3,
        out_shape=[jax.ShapeDtypeStruct((R, C), F32)] * 3,
        compiler_params=_params(("parallel",)),
    )(w, g, m, v)


def _sum_adamw(parts, w, m, v, name, transposed=False):
    _, R, C = parts.shape
    tr = _pick(R, (256, 176, 128, 64, 32, 16, 8))

    def body(p_ref, w_ref, m_ref, v_ref, g_ref, d_ref, nm_ref, nv_ref):
        g_ = p_ref[0].astype(F32)
        for d in range(1, N_DEV):
            g_ = g_ + p_ref[d].astype(F32)
        if transposed:
            g_ = jnp.transpose(g_)
        g_ref[...] = g_
        d_ref[...], nm_ref[...], nv_ref[...] = _adamw_math(w_ref[...], g_, m_ref[...], v_ref[...])

    spec = pl.BlockSpec((C, tr), lambda i: (0, i)) if transposed else pl.BlockSpec((tr, C), lambda i: (i, 0))
    return pl.pallas_call(
        body, name=name, grid=(R // tr,),
        in_specs=[pl.BlockSpec((N_DEV, tr, C), lambda i: (0, i, 0)), spec, spec, spec], out_specs=[spec] * 4,
        out_shape=[jax.ShapeDtypeStruct(w.shape, F32)] * 4,
        compiler_params=_params(("parallel",)),
    )(parts, w, m, v)


def _sum_devices(parts, name):
    _, R, C = parts.shape
    tr = _pick(R, (512, 256, 176, 128, 64, 32, 16, 8))

    def body(p_ref, o_ref):
        acc = p_ref[0].astype(F32)
        for d in range(1, N_DEV):
            acc = acc + p_ref[d].astype(F32)
        o_ref[...] = acc

    return pl.pallas_call(
        body, name=name, grid=(R // tr,),
        in_specs=[pl.BlockSpec((N_DEV, tr, C), lambda i: (0, i, 0))],
        out_specs=pl.BlockSpec((tr, C), lambda i: (i, 0)),
        out_shape=jax.ShapeDtypeStruct((R, C), F32),
        compiler_params=_params(("parallel",)),
    )(parts)


def _my_place():
    return lax.axis_index("x"), lax.axis_index("y"), lax.axis_index("c")


def _all_gather(blocks, name):
    n = len(blocks)

    def body(*refs):
        x_refs, out_refs = refs[:n], refs[n:2 * n]
        send_sems, recv_sems, local_sems = refs[2 * n:]
        x, y, c = _my_place()
        me, sibling = (x, y, c), (x, y, 1 - c)
        chips = [(1 - x, y), (x, 1 - y), (1 - x, 1 - y)]

        def copy(a, k, blk, to, own=False):
            slot = out_refs[a].at[4 * blk[0] + 2 * blk[1] + blk[2]]
            return pltpu.make_async_remote_copy(
                src_ref=x_refs[a] if own else slot, dst_ref=slot,
                send_sem=send_sems.at[7 * a + k], recv_sem=recv_sems.at[7 * a + k], device_id=to, device_id_type=MESH)

        mine = [pltpu.make_async_copy(x_refs[a], out_refs[a].at[4 * x + 2 * y + c], local_sems.at[a]) for a in range(n)]
        for cp in mine:
            cp.start()
        first = []
        for j, chip in enumerate(chips):
            first += [copy(a, 1 + j, me, (*chip, c), own=True) for a in range(n)]
        first += [copy(a, 0, me, sibling, own=True) for a in range(n)]
        for cp in first:
            cp.start()
        passed = []
        for j, chip in enumerate(chips):
            for a in range(n):
                copy(a, 1 + j, (*chip, c), me).wait_recv()
                passed.append(copy(a, 4 + j, (*chip, c), sibling))
                passed[-1].start()
        for a in range(n):
            copy(a, 0, sibling, me).wait_recv()
        for j, chip in enumerate(chips):
            for a in range(n):
                copy(a, 4 + j, (*chip, 1 - c), me).wait_recv()
        for cp in first + passed:
            cp.wait_send()
        for cp in mine:
            cp.wait()

    return pl.pallas_call(
        body, name=name,
        out_shape=[jax.ShapeDtypeStruct((N_DEV,) + b.shape, b.dtype) for b in blocks],
        in_specs=[pl.BlockSpec(memory_space=pl.ANY)] * n,
        out_specs=[pl.BlockSpec(memory_space=pl.ANY)] * n,
        scratch_shapes=[pltpu.SemaphoreType.DMA((7 * n,)), pltpu.SemaphoreType.DMA((7 * n,)), pltpu.SemaphoreType.DMA((n,))],
    )(*blocks)


def _all_to_all(pieces, name):
    n = len(pieces)

    def body(*refs):
        x_refs, out_refs = refs[:n], refs[n:2 * n]
        send_sems, recv_sems, local_sems = refs[2 * n:]
        x, y, c = _my_place()
        me = 4 * x + 2 * y + c
        mine = [pltpu.make_async_copy(x_refs[a].at[me], out_refs[a].at[me], local_sems.at[a]) for a in range(n)]
        for cp in mine:
            cp.start()
        copies = []
        for k in (2, 4, 6, 3, 5, 7, 1):
            px = 1 - x if k & 4 else x
            py = 1 - y if k & 2 else y
            pc = 1 - c if k & 1 else c
            peer = 4 * px + 2 * py + pc
            for a in range(n):
                copies.append(pltpu.make_async_remote_copy(
                    src_ref=x_refs[a].at[peer], dst_ref=out_refs[a].at[me],
                    send_sem=send_sems.at[7 * a + k - 1], recv_sem=recv_sems.at[7 * a + k - 1],
                    device_id=(px, py, pc), device_id_type=MESH))
        for cp in copies:
            cp.start()
        for cp in copies:
            cp.wait_recv()
        for cp in copies:
            cp.wait_send()
        for cp in mine:
            cp.wait()

    return pl.pallas_call(
        body, name=name,
        out_shape=[jax.ShapeDtypeStruct(p.shape, p.dtype) for p in pieces],
        in_specs=[pl.BlockSpec(memory_space=pl.ANY)] * n,
        out_specs=[pl.BlockSpec(memory_space=pl.ANY)] * n,
        scratch_shapes=[pltpu.SemaphoreType.DMA((7 * n,)), pltpu.SemaphoreType.DMA((7 * n,)), pltpu.SemaphoreType.DMA((n,))],
    )(*pieces)


def _peers():
    x, y, c = _my_place()
    out = []
    for k in (2, 4, 6, 3, 5, 7, 1):
        px = 1 - x if k & 4 else x
        py = 1 - y if k & 2 else y
        pc = 1 - c if k & 1 else c
        out.append((k, (px, py, pc), 4 * px + 2 * py + pc))
    return out


def _exchange_copies(x_refs, land_refs, send_sems, recv_sems, scatter):
    x, y, c = _my_place()
    me = 4 * x + 2 * y + c
    starts, arrivals = [], []
    for k, place, peer in _peers():
        for a, (x_ref, land_ref) in enumerate(zip(x_refs, land_refs)):
            sems = dict(send_sem=send_sems.at[7 * a + k - 1], recv_sem=recv_sems.at[7 * a + k - 1],
                        device_id=place, device_id_type=MESH)
            src = x_ref.at[peer] if scatter else x_ref
            starts.append(pltpu.make_async_remote_copy(src_ref=src, dst_ref=land_ref.at[me], **sems))
            arrivals.append(pltpu.make_async_remote_copy(src_ref=src, dst_ref=land_ref.at[peer], **sems))
    return starts, arrivals


def _exchange_start(arrays, scatter, name):
    n = len(arrays)
    hbm = pl.BlockSpec(memory_space=pltpu.HBM)
    sem = pl.BlockSpec(memory_space=pltpu.SEMAPHORE)
    lands = [lax.empty(a.shape if scatter else (N_DEV,) + a.shape, a.dtype) for a in arrays]

    def body(*refs):
        x_refs, land_refs = refs[:n], refs[n:2 * n]
        send_sems, recv_sems = refs[2 * n], refs[2 * n + 1]
        token = refs[-1]
        starts, _ = _exchange_copies(x_refs, land_refs, send_sems, recv_sems, scatter)
        for cp in starts:
            cp.start()
        token[...] = jnp.zeros_like(token)

    res = pl.pallas_call(
        body, name=name,
        out_shape=(pltpu.SemaphoreType.DMA((7 * n,)), pltpu.SemaphoreType.DMA((7 * n,)),
                   *[pltpu.HBM(a.shape, a.dtype) for a in arrays], *[pltpu.HBM(l.shape, l.dtype) for l in lands],
                   jax.ShapeDtypeStruct((8, 128), F32)),
        in_specs=[hbm] * (2 * n),
        out_specs=(sem, sem, *[hbm] * (2 * n), pl.BlockSpec(memory_space=pltpu.VMEM)),
        input_output_aliases={i: 2 + i for i in range(2 * n)},
        compiler_params=pltpu.CompilerParams(has_side_effects=pltpu.SideEffectType.DATAFLOW_SIDE_EFFECTING),
    )(*[pltpu.with_memory_space_constraint(a, pltpu.HBM) for a in arrays],
      *[pltpu.with_memory_space_constraint(l, pltpu.HBM) for l in lands])
    return res[0], res[1], list(res[2:2 + n]), list(res[2 + n:2 + 2 * n]), res[-1]


def _exchange_wait(handles, scatter, after, name):
    send_sems, recv_sems, arrays, lands, _ = handles
    n = len(arrays)
    hbm = pl.BlockSpec(memory_space=pltpu.HBM)
    sem = pl.BlockSpec(memory_space=pltpu.SEMAPHORE)

    def body(*refs):
        x_refs, land_refs = refs[:n], refs[n:2 * n]
        send_s, recv_s = refs[2 * n], refs[2 * n + 1]
        starts, arrivals = _exchange_copies(x_refs, land_refs, send_s, recv_s, scatter)
        for cp in arrivals:
            cp.wait_recv()
        for cp in starts:
            cp.wait_send()

    res = pl.pallas_call(
        body, name=name,
        out_shape=(*[pltpu.HBM(a.shape, a.dtype) for a in arrays], *[pltpu.HBM(l.shape, l.dtype) for l in lands]),
        in_specs=[hbm] * (2 * n) + [sem, sem, pl.BlockSpec(memory_space=pl.ANY)],
        out_specs=tuple([hbm] * (2 * n)),
        input_output_aliases={i: i for i in range(2 * n)},
        compiler_params=pltpu.CompilerParams(has_side_effects=pltpu.SideEffectType.DATAFLOW_SIDE_EFFECTING),
    )(*arrays, *lands, send_sems, recv_sems, after)
    me = 4 * lax.axis_index("x") + 2 * lax.axis_index("y") + lax.axis_index("c")
    out = []
    for src, got in zip(res[:n], res[n:]):
        zeros = (0,) * (got.ndim - 1)
        own = lax.dynamic_slice(src, (me,) + zeros, (1,) + src.shape[1:]) if scatter else src[None]
        out.append(lax.dynamic_update_slice(got, own, (me,) + zeros))
    return out


def _pad_lanes(v, at=0, width=128):
    return jnp.pad(v, ((0, 0), (at, width - at - v.shape[1])))


def _pack_weights(P):
    W = {}
    w = P["w_in"]
    W["wp"] = jnp.concatenate([w[:, :2048], w[:, 2440:2696], w[:, 2056:2440], w[:, 2696:2760], w[:, 2048:2056],
                               jnp.zeros((D_MODEL, N_IN_PACKED - N_IN), w.dtype)], axis=1).astype(BF16)
    W["conv_w"] = P["gdn_conv_w"].astype(F32)
    W["alog_p"] = _pad_lanes(P["gdn_a_log"], 64)
    W["dt_p"] = _pad_lanes(P["gdn_dt_bias"], 64)
    W["gnw"] = P["gdn_norm_w"]
    W["qnw"] = P["mla_q_norm_w"]
    W["kvnw"] = P["mla_kv_norm_w"]
    uq = P["mla_w_uq"].reshape(Q_LORA, HEADS, HEAD_DIM + ROPE)
    W["wuq"] = jnp.pad(uq, ((0, 0), (0, 0), (0, 256 - HEAD_DIM - ROPE))).reshape(Q_LORA, HEADS * 256).astype(BF16)
    ukv = P["mla_w_ukv"].reshape(KV_LORA, HEADS, 2, HEAD_DIM)
    W["wukv"] = ukv.transpose(0, 2, 1, 3).reshape(KV_LORA, 2 * HEADS * HEAD_DIM).astype(BF16)
    W["qn_w"] = P["qkn_q_nope"]
    W["qr_w"] = _pad_lanes(P["qkn_q_rope"])
    W["kn_w"] = P["qkn_k_nope"]
    W["kr_w"] = _pad_lanes(P["qkn_k_rope"])
    W["onw"] = P["mla_out_norm_w"]
    W["wout"] = P["w_out"].astype(BF16)
    return W


def _unpack_grads(G):
    g_qkv, g_z, g_ckv, g_cq, g_kab = G["wp"]
    uq = G["wuq"].reshape(Q_LORA, HEADS, 256)[:, :, :HEAD_DIM + ROPE].reshape(Q_LORA, HEADS * (HEAD_DIM + ROPE))
    ukv = G["wukv"].reshape(KV_LORA, 2, HEADS, HEAD_DIM).transpose(0, 2, 1, 3).reshape(KV_LORA, 2 * HEADS * HEAD_DIM)
    return {
        "w_in": jnp.concatenate([g_qkv, g_z, g_kab[:, ROPE:ROPE + 8], g_cq, g_ckv, g_kab[:, :ROPE]], axis=1),
        "gdn_conv_w": G["conv_w"], "gdn_a_log": G["alog_p"][:, 64:68], "gdn_dt_bias": G["dt_p"][:, 64:68],
        "gdn_norm_w": G["gnw"], "mla_q_norm_w": G["qnw"], "mla_w_uq": uq, "mla_kv_norm_w": G["kvnw"], "mla_w_ukv": ukv,
        "qkn_q_nope": G["qn_w"], "qkn_q_rope": G["qr_w"][:, :ROPE], "qkn_k_nope": G["kn_w"], "qkn_k_rope": G["kr_w"][:, :ROPE],
        "mla_out_norm_w": G["onw"], "w_out": G["wout"],
    }


def _rope_tables(positions):
    half = ROPE // 2
    inv_freq = ROPE_BASE ** (-jnp.arange(half, dtype=F32) / half)
    ang = positions.astype(F32)[:, None] * inv_freq
    cos, sin = jnp.cos(ang), jnp.sin(ang)
    zeros = jnp.zeros((positions.shape[0], 128 - ROPE), F32)
    return jnp.concatenate([cos, cos, zeros], axis=1), jnp.concatenate([-sin, sin, zeros], axis=1)


def _ffn_forward(x, scale, shift, gate_w, w8, wo4, name, target=None):
    S = x.shape[0]
    tm = _pick(S, (512, 256, 128))
    n = S // tm
    with_loss = target is not None

    def body(x_ref, sc_ref, sh_ref, g_ref, wg_ref, wu_ref, wo_ref, *rest):
        t_ref = rest[0] if with_loss else None
        h_ref, bg_ref, bu_ref, ht_ref = rest[with_loss:with_loss + 4]
        tail = rest[with_loss + 4:]
        h_scr, acc_ref = tail[-2:]
        i, p = pl.program_id(0), pl.program_id(1)

        @pl.when(p == 0)
        def _():
            h_new = _modulate(x_ref[...], sc_ref[...], sh_ref[...]).astype(BF16)
            h_scr[...] = h_new
            h_ref[...] = h_new
            acc_ref[...] = jnp.zeros_like(acc_ref)

        h = h_scr[...]
        gate = _dot_raw(h, wg_ref[...], "nn")
        up = _dot_raw(h, wu_ref[...], "nn")
        sg = _sigmoid(gate)
        act = gate * sg
        hid = (act * up).astype(BF16)
        bg_ref[...] = (up * (sg * (1.0 + gate * (1.0 - sg)))).astype(BF16)
        bu_ref[...] = act.astype(BF16)
        ht_ref[...] = jnp.transpose(hid)
        acc_ref[...] += _dot_raw(hid, wo_ref[...], "nn")

        if with_loss:
            dx_ref, df_ref, dg_ref, l_ref = tail[:4]

            @pl.when((p == 0) & (i == 0))
            def _():
                dg_ref[...] = jnp.zeros_like(dg_ref)
                l_ref[...] = jnp.zeros_like(l_ref)

            @pl.when(p == HID_PIECES - 1)
            def _():
                step = min(EPILOGUE_ROWS, tm)
                for r in range(tm // step):
                    rows = slice(step * r, step * (r + 1))
                    f = acc_ref[rows, :]
                    diff = x_ref[rows, :] + 0.5 * g_ref[...] * f - t_ref[rows, :]
                    dx = diff * (1.0 / D_MODEL)
                    dx_ref[rows, :] = dx
                    df_ref[rows, :] = (0.5 * g_ref[...] * dx).astype(df_ref.dtype)
                    dg_ref[...] += jnp.sum(0.5 * f * dx, axis=0, keepdims=True)
                    l_ref[...] += jnp.sum(diff * diff, axis=0, keepdims=True)

            @pl.when((p == HID_PIECES - 1) & (i == n - 1))
            def _():
                l_ref[...] = jnp.full(l_ref.shape, (0.5 / D_MODEL) * jnp.sum(l_ref[...]), F32)
        else:
            f_ref, xo_ref = tail[:2]

            @pl.when(p == HID_PIECES - 1)
            def _():
                f = acc_ref[...]
                f_ref[...] = f.astype(f_ref.dtype)
                xo_ref[...] = x_ref[...] + 0.5 * g_ref[...] * f

    row = pl.BlockSpec((tm, D_MODEL), lambda i, p: (i, 0))
    par = pl.BlockSpec((1, D_MODEL), lambda i, p: (0, 0))
    piece = pl.BlockSpec((None, tm, FFN_PIECE), lambda i, p: (p, i, 0))
    row_f32, row_bf16 = jax.ShapeDtypeStruct((S, D_MODEL), F32), jax.ShapeDtypeStruct((S, D_MODEL), BF16)
    par_f32 = jax.ShapeDtypeStruct((1, D_MODEL), F32)
    piece_shape = jax.ShapeDtypeStruct((HID_PIECES, S, FFN_PIECE), BF16)
    return pl.pallas_call(
        body, name=name, grid=(n, HID_PIECES),
        in_specs=[row, par, par, par,
                  pl.BlockSpec((None, D_MODEL, FFN_PIECE), lambda i, p: (p, 0, 0)),
                  pl.BlockSpec((None, D_MODEL, FFN_PIECE), lambda i, p: (p + HID_PIECES, 0, 0)),
                  pl.BlockSpec((None, FFN_PIECE, D_MODEL), lambda i, p: (p, 0, 0))] + [row] * with_loss,
        out_specs=[row, piece, piece, pl.BlockSpec((None, FFN_PIECE, tm), lambda i, p: (p, 0, i))]
        + ([row, row, par, par] if with_loss else [row, row]),
        out_shape=[row_bf16, piece_shape, piece_shape, jax.ShapeDtypeStruct((HID_PIECES, FFN_PIECE, S), BF16)]
        + ([row_f32, row_bf16, par_f32, par_f32] if with_loss else [row_bf16, row_f32]),
        scratch_shapes=[pltpu.VMEM((tm, D_MODEL), BF16), pltpu.VMEM((tm, D_MODEL), F32)],
        compiler_params=_params(("arbitrary", "arbitrary")),
    )(x, scale, shift, gate_w, w8, w8, wo4, *([target] if with_loss else []))


def _ffn_fwd(x, scale, shift, gate_w, w8, wo4, tag, target=None):
    res = _ffn_forward(x, scale, shift, gate_w, w8, wo4, tag + "_fwd", target)
    h, by_gate, by_up, hid_t = res[:4]
    if target is not None:
        dx_out, df, d_gate_w, loss_row = res[4:]
        return (dx_out, loss_row), (h, by_gate, by_up, hid_t, None, df, d_gate_w)
    f, x_out = res[4:]
    return x_out, (h, by_gate, by_up, hid_t, f, None, None)


def _ffn_bwd(d_out, x, scale, shift, gate_w, w8, wo4, saved, tag, grad_ready, below=None):
    h, gate, up, hid_t, f, df, d_gate_w = saved
    S = x.shape[0]
    tm = _pick(S, (512, 256, 128))
    tk = _pick(S, (512, 256, 128))
    n = S // tm
    if df is None:
        (df,), (d_gate_w,) = _rowwise_bwd(lambda f_, g_: (0.5 * g_ * f_,), [(f, tm, D_MODEL, 0)], [], [gate_w],
                                          [(d_out, tm, D_MODEL, 0)], n, tag + "_dres", row_dtypes=(BF16,))
    tb = _pick(S, MATMUL_ROWS)
    piece = pl.BlockSpec((None, tb, FFN_PIECE), lambda i, j, k: (j, i, 0))
    d_gate, d_up = _mmg(df, wo4, "nt", name=tag + "_ddown", grid=(S // tb, HID_PIECES, 1),
                        a_spec=pl.BlockSpec((tb, D_MODEL), lambda i, j, k: (i, 0)),
                        b_spec=pl.BlockSpec((None, FFN_PIECE, D_MODEL), lambda i, j, k: (j, 0, 0)),
                        out_spec=piece, out_shapes=[jax.ShapeDtypeStruct((HID_PIECES, S, FFN_PIECE), BF16)] * 2,
                        acc_shape=(tb, FFN_PIECE), extras=[gate, up], extra_specs=[piece, piece], epi=_swiglu_bwd)
    tk = _pick(S, MATMUL_ROWS)
    g_wo4 = _mmg(hid_t, df, "nn", name=tag + "_gwo", grid=(HID_PIECES, 1, S // tk),
                 a_spec=pl.BlockSpec((None, FFN_PIECE, tk), lambda i, j, k: (i, 0, k)),
                 b_spec=pl.BlockSpec((tk, D_MODEL), lambda i, j, k: (k, j)),
                 out_spec=pl.BlockSpec((None, FFN_PIECE, D_MODEL), lambda i, j, k: (i, 0, j)),
                 out_shapes=[jax.ShapeDtypeStruct((HID_PIECES, FFN_PIECE, D_MODEL), BF16)], acc_shape=(FFN_PIECE, D_MODEL))
    g_w8 = _ffn_gw8(h, d_gate, d_up, tag + "_gw8", after=grad_ready("wo4", g_wo4))
    res = _ffn_dh(d_gate, d_up, w8, x, d_out, scale, shift, tag + "_dh", after=grad_ready("w8", g_w8), below=below)
    return (res[0], res[1], res[2], d_gate_w) + tuple(res[3:])


def _dproj_dmod(d_pieces, wp, x, d_out, scale, shift, name, below=None):
    S = x.shape[0]
    tm = _pick(S, TOKEN_ROWS)
    widths = [p.shape[1] for p in d_pieces]
    starts = [sum(widths[:n]) for n in range(len(widths))]
    n_p = len(d_pieces)
    n_below = 0 if below is None else 2

    def body(*refs):
        dp_refs, (w_ref, x_ref, do_ref, sc_ref, sh_ref), rest = refs[:n_p], refs[n_p:n_p + 5], refs[n_p + 5:]
        below_in = rest[:n_below]
        dx_ref, dsc_ref, dsh_ref = rest[n_below:n_below + 3]
        below_out, acc_ref = rest[n_below + 3:n_below + 3 + n_below], rest[-1]

        @pl.when(pl.program_id(0) == 0)
        def _():
            for r in (dsc_ref, dsh_ref) + tuple(below_out[1:]):
                r[...] = jnp.zeros_like(r)

        acc = None
        for dp_ref, at, width in zip(dp_refs, starts, widths):
            part = _dot_raw(dp_ref[...], w_ref[:, at:at + width], "nt")
            acc = part if acc is None else acc + part
        acc_ref[...] = acc
        _dmod_epilogue(acc_ref, x_ref, do_ref, sc_ref, sh_ref, dx_ref, dsc_ref, dsh_ref, below, below_in, below_out)

    row = pl.BlockSpec((tm, D_MODEL), lambda i: (i, 0))
    par = pl.BlockSpec((1, D_MODEL), lambda i: (0, 0))
    row_shape, par_shape = jax.ShapeDtypeStruct((S, D_MODEL), F32), jax.ShapeDtypeStruct((1, D_MODEL), F32)
    return pl.pallas_call(
        body, name=name, grid=(S // tm,),
        in_specs=[pl.BlockSpec((tm, width), lambda i: (i, 0)) for width in widths]
        + [pl.BlockSpec(wp.shape, lambda i: (0, 0)), row, row, par, par] + [row, par][:n_below],
        out_specs=[row, par, par] + [row, par][:n_below],
        out_shape=[row_shape, par_shape, par_shape] + [jax.ShapeDtypeStruct((S, D_MODEL), BF16), par_shape][:n_below],
        scratch_shapes=[pltpu.VMEM((tm, D_MODEL), F32)],
        compiler_params=_params(("arbitrary",)),
    )(*d_pieces, wp, x, d_out, scale, shift, *(below[:2] if below is not None else ()))


def _gw_pieces(h, d_pieces, name):
    S = h.shape[0]
    tm = 512
    tk = _pick(S, MATMUL_ROWS)
    n_p = len(d_pieces)
    widths = [p.shape[1] for p in d_pieces]

    def body(h_ref, *rest):
        d_refs, o_refs = rest[:n_p], rest[n_p:]

        @pl.when(pl.program_id(1) == 0)
        def _():
            for o_ref in o_refs:
                o_ref[...] = jnp.zeros_like(o_ref)

        h_t = jnp.transpose(h_ref[...])
        for d_ref, o_ref in zip(d_refs, o_refs):
            o_ref[...] += _dot_raw(h_t, d_ref[...], "nn")

    return pl.pallas_call(
        body, name=name, grid=(D_MODEL // tm, S // tk),
        in_specs=[pl.BlockSpec((tk, tm), lambda i, k: (k, i))] + [pl.BlockSpec((tk, width), lambda i, k: (k, 0)) for width in widths],
        out_specs=[pl.BlockSpec((tm, width), lambda i, k: (i, 0)) for width in widths],
        out_shape=[jax.ShapeDtypeStruct((D_MODEL, width), F32) for width in widths],
        compiler_params=_params(("parallel", "arbitrary")),
    )(h, *d_pieces)


def _mod_proj(x, scale, shift, wp, name):
    S = x.shape[0]
    N = wp.shape[1]
    tm = _pick(S, MATMUL_ROWS)
    tn = _pick(N, (1408, 1024, 512, 256, 128))

    def body(x_ref, sc_ref, sh_ref, w_ref, h_ref, o_ref, h_scr):
        @pl.when(pl.program_id(1) == 0)
        def _():
            h_new = _modulate(x_ref[...], sc_ref[...], sh_ref[...]).astype(BF16)
            h_scr[...] = h_new
            h_ref[...] = h_new

        o_ref[...] = _dot_raw(h_scr[...], w_ref[...], "nn")

    row = pl.BlockSpec((tm, D_MODEL), lambda i, j: (i, 0))
    par = pl.BlockSpec((1, D_MODEL), lambda i, j: (0, 0))
    return pl.pallas_call(
        body, name=name, grid=(S // tm, N // tn),
        in_specs=[row, par, par, pl.BlockSpec((D_MODEL, tn), lambda i, j: (0, j))],
        out_specs=[row, pl.BlockSpec((tm, tn), lambda i, j: (i, j))],
        out_shape=[jax.ShapeDtypeStruct((S, D_MODEL), BF16), jax.ShapeDtypeStruct((S, N), F32)],
        scratch_shapes=[pltpu.VMEM((tm, D_MODEL), BF16)],
        compiler_params=_params(("parallel", "arbitrary")),
    )(x, scale, shift, wp)


def _mix_out(o_a, proj, o_b, gnw, onw, wout, x, gate_w, t, name):
    S = x.shape[0]

    def body(oa_ref, z_ref, ob_ref, gn_ref, on_ref, w_ref, x_ref, g_ref, mixed_ref, y_ref, xo_ref):
        (mixed,) = _mix_post_fn(oa_ref[...], z_ref[...], ob_ref[...], gn_ref[...], on_ref[...])
        mixed = mixed.astype(BF16)
        mixed_ref[...] = mixed
        y = _dot_raw(mixed, w_ref[...], "nn")
        y_ref[...] = y.astype(BF16)
        xo_ref[...] = x_ref[...] + g_ref[...] * y

    half, row = _row_spec(t, 512, 0), _row_spec(t, D_MODEL, 0)
    return pl.pallas_call(
        body, name=name, grid=(S // t,),
        in_specs=[half, _row_spec(t, 512, 3), half, _full_spec(gnw.shape), _full_spec(onw.shape), _full_spec(wout.shape),
                  row, _full_spec(gate_w.shape)],
        out_specs=[row, row, row],
        out_shape=[jax.ShapeDtypeStruct((S, D_MODEL), BF16)] * 2 + [jax.ShapeDtypeStruct((S, D_MODEL), F32)],
        compiler_params=_params(("parallel",)),
    )(o_a, proj, o_b, gnw, onw, wout, x, gate_w)


def _mixer_fwd(x1, scale, shift, gate_w, cos_p, sin_p, W):
    S = x1.shape[0]
    tm = _pick(S, (512, 256, 128))
    tv = _pick(S, TOKEN_ROWS)
    ta = _pick(S, (512, 256, 128))
    nc = S // CHUNK
    h2, proj = _mod_proj(x1, scale, shift, W["wp"], "mix_proj")
    qkvc, q_a, k_a, v_a, gb = _conv_fwd(proj, W["conv_w"], W["alog_p"], W["dt_p"], tv, "gdn_conv")
    kab = (proj, tv, 128, 21)
    ti = _pick(S, INTRA_ROWS)
    intra = _rowwise(_gdn_intra_fn, [(q_a, ti, 512, 0), (k_a, ti, 512, 0), (v_a, ti, 512, 0), (gb, ti, 128, 0)],
                     [], [(ti, 512, F32)] * 4 + [(ti, CHUNK, F32)] * 4 + [(ti // 8, 512, F32)] + [(ti, CHUNK, F32)] * 4,
                     S // ti, "gdn_intra")
    u, wk, qd, kd, qks, gl, invs = intra[0], intra[1], intra[2], intra[3], tuple(intra[4:8]), intra[8], tuple(intra[9:])
    o_a, s_prev = _gdn_scan_fwd(u, wk, qd, kd, qks, gl, "gdn_scan")
    mla_params = [W["qnw"], W["kvnw"], W["wuq"], W["wukv"], W["qn_w"], W["qr_w"], W["kn_w"], W["kr_w"]]
    def mla_pre_with_vt(*a):
        q_, k_, v_ = _mla_pre_fn(*a)
        return q_, k_, v_, jnp.transpose(v_)

    q_b, k_b, v_b, vt_b = _rowwise(mla_pre_with_vt,
                                   [(proj, tv, 256, 8), (proj, tv, 384, 6), kab, (cos_p, tv, 128, 0), (sin_p, tv, 128, 0)],
                                   mla_params, [(tv, 1024, BF16), (tv, 1024, BF16), (tv, 512, BF16), (512, tv, BF16, "across")],
                                   S // tv, "mla_pre")
    o_b, lse = _attn_fwd(q_b, k_b, vt_b, ta, "mla_attn")
    mixed, y, x2 = _mix_out(o_a, proj, o_b, W["gnw"], W["onw"], W["wout"], x1, gate_w, tv, "mix_out")
    saved = (h2, proj, qkvc, q_a, k_a, v_a, gb, u, wk, qd, kd, qks, gl, invs, s_prev, o_a, q_b, k_b, v_b, o_b, lse, mixed, y)
    return x2, saved


def _mixer_bwd(d_out, dy, x1, scale, shift, cos_p, sin_p, W, saved, below):
    (h2, proj, qkvc, q_a, k_a, v_a, gb, u, wk, qd, kd, qks, gl, invs, s_prev, o_a, q_b, k_b, v_b, o_b, lse, mixed, y) = saved
    S = x1.shape[0]
    tm = _pick(S, (512, 256, 128))
    tv = _pick(S, TOKEN_ROWS)
    ta = _pick(S, (512, 256, 128))
    nc = S // CHUNK
    G = {}
    G["wout"] = _mm(mixed, dy, "tn", name="mix_gwout")
    (do_a, dz, do_b), (G["gnw"], G["onw"], stats) = _rowwise_bwd(
        _mix_post_fn, [(o_a, tv, 512, 0), (proj, tv, 512, 3), (o_b, tv, 512, 0)], [], [W["gnw"], W["onw"]],
        [], S // tv, "mix_dpost", row_dtypes=(F32, BF16, F32), dout_from=(dy, W["wout"]),
        across=(lse, lambda r_vals, d_rows, lse_tile: _attn_stats(r_vals[2], d_rows[2], lse_tile)))
    dq_b, dk_b, dv_b = _attn_bwd(q_b, k_b, v_b, do_b, stats, ta, "mla_dattn")
    kab = (proj, tv, 128, 21)
    mla_params = [W["qnw"], W["kvnw"], W["wuq"], W["wukv"], W["qn_w"], W["qr_w"], W["kn_w"], W["kr_w"]]
    (d_ckv, d_cq, d_kab), mla_grads = _rowwise_bwd(
        _mla_pre_fn, [(proj, tv, 256, 8), (proj, tv, 384, 6), kab], [(cos_p, tv, 128, 0), (sin_p, tv, 128, 0)], mla_params,
        [(dq_b, tv, 1024, 0), (dk_b, tv, 1024, 0), (dv_b, tv, 512, 0)], S // tv, "mla_dpre", row_dtypes=(BF16, BF16, F32))
    for key, g in zip(("qnw", "kvnw", "wuq", "wukv", "qn_w", "qr_w", "kn_w", "kr_w"), mla_grads):
        G[key] = g
    scan_grads = _gdn_scan_bwd(u, wk, qd, kd, qks, gl, s_prev, do_a, "gdn_dscan")
    ti = _pick(S, INTRA_ROWS)
    intra_douts = [(scan_grads[i], ti, 512, 0) for i in range(4)] + [(scan_grads[4 + i], ti, CHUNK, 0) for i in range(4)]
    intra_douts.append((scan_grads[8], ti // 8, 512, 0))
    (dq_a, dk_a, dv_a, d_gb), _ = _rowwise_bwd(
        _gdn_intra_fn, [(q_a, ti, 512, 0), (k_a, ti, 512, 0), (v_a, ti, 512, 0), (gb, ti, 128, 0)],
        [(x_, ti, CHUNK, 0) for x_ in invs], [], intra_douts, S // ti, "gdn_dintra")
    d_qkv, d_kab, g_conv, G["alog_p"], G["dt_p"] = _conv_bwd(proj, qkvc, W["conv_w"], W["alog_p"], W["dt_p"],
                                                              (dq_a, dk_a, dv_a, d_gb), d_kab, tv, "gdn_dconv")
    G["conv_w"] = g_conv[:4]
    d_proj = [d_qkv, dz, d_ckv, d_cq, d_kab]
    G["wp"] = _gw_pieces(h2, d_proj, "mix_gwp")
    dx1, G["s2"], G["sh2"], d_below, dg_below = _dproj_dmod(d_proj, W["wp"], x1, d_out, scale, shift, "mix_dproj", below=below)
    return dx1, d_below, dg_below, G


def _local_step(x, target, mod, cos_p, sin_p, W1, mixer_weights, ffn2_weights, ffn_grad_ready, mixer_grads_ready):
    sh1, s1, g1, sh2, s2, g2, sh3, s3, g3 = [mod[:, D_MODEL * i:D_MODEL * (i + 1)] for i in range(N_MOD)]
    x1, saved1 = _ffn_fwd(x, s1, sh1, g1, W1["f1_w8"], W1["f1_wo4"], "ffn1")
    W = mixer_weights(x1)
    x2, saved2 = _mixer_fwd(x1, s2, sh2, g2, cos_p, sin_p, W)
    W.update(ffn2_weights(x2))
    (dx3, loss_row), saved3 = _ffn_fwd(x2, s3, sh3, g3, W["f2_w8"], W["f2_wo4"], "ffn2", target=target)
    dx2, d_s3, d_sh3, d_g3, dy, d_g2 = _ffn_bwd(dx3, x2, s3, sh3, g3, W["f2_w8"], W["f2_wo4"], saved3, "ffn2",
                                                ffn_grad_ready("f2"), below=(saved2[-1], g2, 1.0))
    dx1, df1, d_g1, G = _mixer_bwd(dx2, dy, x1, s2, sh2, cos_p, sin_p, W, saved2, below=(saved1[4], g1, 0.5))
    d_sh2, d_s2 = G.pop("sh2"), G.pop("s2")
    saved1 = saved1[:5] + (df1, d_g1 + mixer_grads_ready(G))
    dx, d_s1, d_sh1, d_g1 = _ffn_bwd(dx1, x, s1, sh1, g1, W1["f1_w8"], W1["f1_wo4"], saved1, "ffn1", ffn_grad_ready("f1"))
    d_mod = jnp.concatenate([d_sh1, d_s1, d_g1, d_sh2, d_s2, d_g2, d_sh3, d_s3, d_g3], axis=1)
    return loss_row, dx, d_mod


WEIGHT_NAMES = ("w_ada", "b_ada", "ffn1_w_in", "ffn1_w_out", "w_in", "gdn_conv_w", "gdn_a_log", "gdn_dt_bias", "gdn_norm_w",
                "mla_q_norm_w", "mla_w_uq", "mla_kv_norm_w", "mla_w_ukv", "qkn_q_nope", "qkn_q_rope", "qkn_k_nope",
                "qkn_k_rope", "mla_out_norm_w", "w_out", "ffn2_w_in", "ffn2_w_out")
FFN_SHARDED = ("ffn1_w_in", "ffn1_w_out", "ffn2_w_in", "ffn2_w_out")
TRANSPOSED_ENTRY = ("ffn1_w_in", "ffn2_w_in", "w_in", "mla_w_uq")
SHEETED = (("w_in", "col"), ("gdn_conv_w", "col"), ("mla_w_uq", "col"), ("mla_w_ukv", "col"), ("w_out", "row"))
MOD_ROWS = N_MOD * D_MODEL // 128
SMALL = {"gdn_a_log": (MOD_ROWS, 1, 64, 4), "gdn_dt_bias": (MOD_ROWS + 1, 1, 64, 4), "gdn_norm_w": (MOD_ROWS + 2, 1, 0, 128),
         "mla_q_norm_w": (MOD_ROWS + 3, 3, 0, 384), "mla_kv_norm_w": (MOD_ROWS + 6, 2, 0, 256),
         "qkn_q_nope": (MOD_ROWS + 8, 1, 0, 128), "qkn_q_rope": (MOD_ROWS + 9, 1, 0, 64), "qkn_k_nope": (MOD_ROWS + 10, 1, 0, 128),
         "qkn_k_rope": (MOD_ROWS + 11, 1, 0, 64), "mla_out_norm_w": (MOD_ROWS + 12, 1, 0, 128)}
LOSS_ROW = MOD_ROWS + 13
CONV_ROW, CONV_ROWS = 88, 4 * 1536 // 128
SHEET_ROWS = CONV_ROW + CONV_ROWS


def _to_sheet(flat, dtype, sublanes):
    n = flat.shape[-1]
    unit = sublanes * 128
    pad = (-n) % unit
    flat = jnp.pad(flat.astype(dtype), [(0, 0)] * (flat.ndim - 1) + [(0, pad)])
    return flat.reshape(flat.shape[:-1] + ((n + pad) // 128, 128))


def _small_sheet(b_like, small):
    sheet = jnp.zeros((SHEET_ROWS, 128), F32).at[:MOD_ROWS].set(b_like.reshape(MOD_ROWS, 128))
    for name, (row, rows, lane, n) in SMALL.items():
        v = small[name].reshape(1, n)
        if rows == 1:
            sheet = sheet.at[row, lane:lane + n].set(v[0])
        else:
            sheet = sheet.at[row:row + rows].set(v.reshape(rows, 128))
    return sheet


def _from_small_sheet(sheet):
    out = {"b_ada": sheet[:MOD_ROWS].reshape(1, N_MOD * D_MODEL)}
    for name, (row, rows, lane, n) in SMALL.items():
        out[name] = sheet[row, lane:lane + n].reshape(1, n) if rows == 1 else sheet[row:row + rows].reshape(1, n)
    return out


def kernel(x, c, positions, w_ada, b_ada, ffn1_w_in, ffn1_w_out, w_in, gdn_conv_w, gdn_a_log, gdn_dt_bias, gdn_norm_w, mla_q_norm_w, mla_w_uq, mla_kv_norm_w, mla_w_ukv, qkn_q_nope, qkn_q_rope, qkn_k_nope, qkn_k_rope, mla_out_norm_w, w_out, ffn2_w_in, ffn2_w_out, loss_target, m_w_ada, m_b_ada, m_ffn1_w_in, m_ffn1_w_out, m_w_in, m_gdn_conv_w, m_gdn_a_log, m_gdn_dt_bias, m_gdn_norm_w, m_mla_q_norm_w, m_mla_w_uq, m_mla_kv_norm_w, m_mla_w_ukv, m_qkn_q_nope, m_qkn_q_rope, m_qkn_k_nope, m_qkn_k_rope, m_mla_out_norm_w, m_w_out, m_ffn2_w_in, m_ffn2_w_out, v_w_ada, v_b_ada, v_ffn1_w_in, v_ffn1_w_out, v_w_in, v_gdn_conv_w, v_gdn_a_log, v_gdn_dt_bias, v_gdn_norm_w, v_mla_q_norm_w, v_mla_w_uq, v_mla_kv_norm_w, v_mla_w_ukv, v_qkn_q_nope, v_qkn_q_rope, v_qkn_k_nope, v_qkn_k_rope, v_mla_out_norm_w, v_w_out, v_ffn2_w_in, v_ffn2_w_out):
    args = locals()
    w = {n: args[n] for n in WEIGHT_NAMES}
    m = {n: args["m_" + n] for n in WEIGHT_NAMES}
    v = {n: args["v_" + n] for n in WEIGHT_NAMES}
    me = 4 * lax.axis_index("x") + 2 * lax.axis_index("y") + lax.axis_index("c")
    cols = N_MOD * D_MODEL // N_DEV
    shard = {n: w[n][0] for n in FFN_SHARDED + tuple(s[0] for s in SHEETED)}

    sc = c * _sigmoid(c)
    first = _to_sheet(jnp.concatenate([sc.reshape(-1), shard["gdn_conv_w"].reshape(-1)]), F32, 8)
    (first_all,) = _all_gather([first], "gather_c")
    sc_all = first_all[:, :D_MODEL // 128].reshape(N_DEV, D_MODEL)
    n_taps = shard["gdn_conv_w"].size
    conv_all = first_all.reshape(N_DEV, -1)[:, D_MODEL:D_MODEL + n_taps].reshape(N_DEV, 4, -1)
    b_mine = lax.dynamic_slice(b_ada, (0, me * cols), (1, cols))
    mod_cols = _mm(sc_all, w_ada[0], "nn", name="ada_mod", extra_params=[b_mine], epi=lambda acc, b_: (acc + b_,))
    (mod_all,) = _all_to_all([_to_sheet(mod_cols, F32, 8)], "scatter_mod")
    mod = mod_all.reshape(N_DEV, -1)[:, :cols].reshape(1, N_MOD * D_MODEL)

    f1_shards, mod = lax.optimization_barrier(([shard["ffn1_w_in"].astype(BF16), shard["ffn1_w_out"].astype(BF16)], mod))
    f1_w8, f1_out = _all_gather(f1_shards, "gather_w1")
    travel = [s for s in SHEETED if s[0] != "gdn_conv_w"]
    tied = lax.optimization_barrier(([shard[n].astype(BF16) for n, _ in travel], f1_w8))
    f1_w8 = tied[1]
    mixer_w = _exchange_start(tied[0], False, "gather_wm_start")
    ffn2_w = _exchange_start([shard["ffn2_w_in"].astype(BF16) + mixer_w[4][0:1, 0:1].astype(BF16),
                              shard["ffn2_w_out"].astype(BF16)], False, "gather_w2_start")
    mod = mod + ffn2_w[4][0:1, 0:1]
    W1 = dict(f1_w8=f1_w8, f1_wo4=f1_out.reshape(HID_PIECES, FFN_PIECE, D_MODEL))

    def mixer_weights(after):
        got = _exchange_wait(mixer_w, False, after, "gather_wm_wait")
        P = {n: jnp.concatenate(list(g), axis=1) if kind == "col" else g.reshape(-1, g.shape[-1])
             for (n, kind), g in zip(travel, got)}
        P["gdn_conv_w"] = jnp.concatenate(list(conv_all), axis=1)
        for n in SMALL:
            P[n] = w[n]
        return _pack_weights(P)

    def ffn2_weights(after):
        f2_w8, f2_out = _exchange_wait(ffn2_w, False, after, "gather_w2_wait")
        return dict(f2_w8=f2_w8, f2_wo4=f2_out.reshape(HID_PIECES, FFN_PIECE, D_MODEL))

    pending, small_grads = {}, {}

    def ffn_grad_ready(tag):
        def ready(which, g):
            pieces = g if which == "w8" else g.reshape((N_DEV,) + shard["ffn1_w_out"].shape)
            pending[tag + which] = _exchange_start([pieces], True, "scatter_%s_%s_start" % (tag, which))
            return pending[tag + which][4]
        return ready

    def mixer_grads_ready(G):
        g_full = _unpack_grads(G)
        small_grads.update({n: g_full[n] for n in SMALL})
        small_grads["gdn_conv_w"] = g_full["gdn_conv_w"]
        pieces = []
        for n, kind in travel:
            r, cc = shard[n].shape
            g = g_full[n].astype(BF16)
            pieces.append(jnp.stack([g[:, cc * p:cc * (p + 1)] for p in range(N_DEV)]) if kind == "col"
                          else g.reshape(N_DEV, r, cc))
        pending["mixer"] = _exchange_start(pieces, True, "scatter_mx_start")
        return pending["mixer"][4][0:1, 0:1]

    cos_p, sin_p = _rope_tables(positions[0])
    loss_row, dx, d_mod = _local_step(x[0], loss_target[0], mod, cos_p, sin_p, W1, mixer_weights, ffn2_weights,
                                      ffn_grad_ready, mixer_grads_ready)

    sheet = _small_sheet(d_mod, small_grads).at[LOSS_ROW].set(loss_row[0, :128])
    sheet = sheet.at[CONV_ROW:CONV_ROW + CONV_ROWS].set(small_grads["gdn_conv_w"].reshape(CONV_ROWS, 128))
    (sheets,) = _all_gather([sheet], "gather_small")
    summed = _sum_devices(sheets, "sum_small")
    d_mod_all = sheets[:, :MOD_ROWS].reshape(N_DEV, N_MOD * D_MODEL)
    d_mod_mine = lax.dynamic_slice(d_mod_all, (0, me * cols), (N_DEV, cols))
    grads = _from_small_sheet(summed)
    grads["w_ada"] = _mm(sc_all, d_mod_mine, "tn", name="ada_gw", hi=True)
    conv_taps = shard["gdn_conv_w"].shape[1]
    grads["gdn_conv_w"] = lax.dynamic_slice(summed[CONV_ROW:CONV_ROW + CONV_ROWS].reshape(4, -1), (0, me * conv_taps),
                                            (4, conv_taps))
    loss = summed[LOSS_ROW, 0]

    delta, new_m, new_v = {}, {}, {}
    arrived = {}
    for n, key in zip(FFN_SHARDED, ("f1w8", "f1wo4", "f2w8", "f2wo4")):
        (arrived[n],) = _exchange_wait(pending[key], True, summed, "scatter_%s_wait" % key)
    arrived.update(zip([n for n, _ in travel], _exchange_wait(pending["mixer"], True, summed, "scatter_mx_wait")))
    for n, parts in arrived.items():
        if n in TRANSPOSED_ENTRY:
            res = _sum_adamw(parts, w[n][0].T, m[n][0].T, v[n][0].T, "adamw_" + n, transposed=True)
            grads[n], delta[n], new_m[n], new_v[n] = [r.T for r in res]
        else:
            grads[n], delta[n], new_m[n], new_v[n] = _sum_adamw(parts, w[n][0], m[n][0], v[n][0], "adamw_" + n)
    for n in ("w_ada", "gdn_conv_w"):
        delta[n], new_m[n], new_v[n] = _adamw(w[n][0], grads[n], m[n][0], v[n][0], "adamw_" + n)
    small_in = [_small_sheet(t["b_ada"], t) for t in (w, grads, m, v)]
    for res, out in zip(_adamw(*small_in, "adamw_small"), (delta, new_m, new_v)):
        out.update(_from_small_sheet(res))

    def shaped(d):
        return [d[n].reshape(w[n].shape) for n in WEIGHT_NAMES]

    return (loss, dx[None], *shaped(grads), *shaped(delta), *shaped(new_m), *shaped(new_v))
```

```python
import functools

import jax
import jax.numpy as jnp
import numpy as np
from jax import lax
from jax.experimental import pallas as pl
from jax.experimental.pallas import tpu as pltpu

F32 = jnp.float32
BF16 = jnp.bfloat16

D_MODEL = 1024
D_FF = 2816
N_MOD = 9
HEADS = 4
HEAD_DIM = 128
CHUNK = 64
EPS = 1e-6
ROPE = 64
Q_LORA = 384
KV_LORA = 256
N_IN = 2760
N_IN_PACKED = 2816
ROPE_BASE = 10000.0
LOG2_E = 1.4426950408889634
N_DEV = 8

ADAM_LR = 0.001
ADAM_B1 = 0.9
ADAM_B2 = 0.999
ADAM_EPS = 1e-08
ADAM_WD = 0.01
ADAM_STEP = 10

VMEM_LIMIT_BYTES = 56 * 1024 * 1024
MATMUL_ROWS = (1024, 512, 256, 128)
MESH = pl.DeviceIdType.MESH


def _params(sem=None):
    return pltpu.CompilerParams(dimension_semantics=sem, vmem_limit_bytes=VMEM_LIMIT_BYTES)


def _pick(dim, prefs):
    for p in prefs:
        if dim % p == 0:
            return p
    return dim


_DIMS = {"nn": (((1,), (0,)), ((), ())), "nt": (((1,), (1,)), ((), ())), "tn": (((0,), (0,)), ((), ()))}


def _dot_raw(a, b, mode):
    return lax.dot_general(a.astype(BF16), b.astype(BF16), _DIMS[mode], preferred_element_type=F32)


def _dot_hi(a, b, mode="nn"):
    return lax.dot_general(a, b, _DIMS[mode], precision=lax.Precision.HIGHEST, preferred_element_type=F32)


@functools.partial(jax.custom_vjp, nondiff_argnums=(2,))
def _bdot(a, b, mode):
    return _dot_raw(a, b, mode)


def _bdot_fwd(a, b, mode):
    return _dot_raw(a, b, mode), (a, b)


def _bdot_bwd(mode, res, g):
    a, b = res
    if mode == "nn":
        return _dot_raw(g, b, "nt"), _dot_raw(a, g, "tn")
    if mode == "nt":
        return _dot_raw(g, b, "nn"), _dot_raw(g, a, "tn")
    return _dot_raw(b, g, "nt"), _dot_raw(a, g, "nn")


_bdot.defvjp(_bdot_fwd, _bdot_bwd)


def _mm(a, b, mode, *, name, out_dtypes=(F32,), epi=None, extras=(), extra_params=(), hi=False,
        tm=None, tn=None, tk=None):
    if mode == "nn":
        (M, K), (_, N) = a.shape, b.shape
    elif mode == "nt":
        (M, K), (N, _) = a.shape, b.shape
    else:
        (K, M), (_, N) = a.shape, b.shape
    tm = tm or _pick(M, (512, 1408, 256, 128) if mode == "tn" else MATMUL_ROWS + (384, 352))
    tn = tn or _pick(N, (1024, 1408, 768, 512, 384, 256, 128))
    tk = tk or _pick(K, (1024, 1408, 512, 384, 256, 128))
    a_spec = {"nn": pl.BlockSpec((tm, tk), lambda i, j, k: (i, k)), "nt": pl.BlockSpec((tm, tk), lambda i, j, k: (i, k)),
              "tn": pl.BlockSpec((tk, tm), lambda i, j, k: (k, i))}[mode]
    b_spec = {"nn": pl.BlockSpec((tk, tn), lambda i, j, k: (k, j)), "nt": pl.BlockSpec((tn, tk), lambda i, j, k: (j, k)),
              "tn": pl.BlockSpec((tk, tn), lambda i, j, k: (k, j))}[mode]
    mn_spec = pl.BlockSpec((tm, tn), lambda i, j, k: (i, j))
    return _mmg(a, b, mode, name=name, grid=(M // tm, N // tn, K // tk), a_spec=a_spec, b_spec=b_spec, out_spec=mn_spec,
                out_shapes=[jax.ShapeDtypeStruct((M, N), dt) for dt in out_dtypes], acc_shape=(tm, tn), epi=epi,
                extras=list(extras) + list(extra_params),
                extra_specs=[mn_spec] * len(extras) + [pl.BlockSpec((1, tn), lambda i, j, k: (0, j))] * len(extra_params),
                hi=hi)


def _mmg(a, b, mode, *, name, grid, a_spec, b_spec, out_spec, out_shapes, acc_shape, epi=None, extras=(),
         extra_specs=(), hi=False):
    nk = grid[2]
    n_e, n_o = len(extras), len(out_shapes)

    def body(*refs):
        a_ref, b_ref = refs[:2]
        e_refs = refs[2:2 + n_e]
        o_refs = refs[2 + n_e:2 + n_e + n_o]
        acc_ref = refs[-1]
        k = pl.program_id(2)

        @pl.when(k == 0)
        def _():
            acc_ref[...] = jnp.zeros_like(acc_ref)

        if hi:
            acc_ref[...] += _dot_hi(a_ref[...].astype(F32), b_ref[...].astype(F32), mode)
        else:
            acc_ref[...] += _dot_raw(a_ref[...], b_ref[...], mode)

        @pl.when(k == nk - 1)
        def _():
            acc = acc_ref[...]
            outs = (acc,) if epi is None else epi(acc, *[e[...].astype(F32) for e in e_refs])
            for o_ref, o in zip(o_refs, outs):
                o_ref[...] = o.astype(o_ref.dtype)

    outs = pl.pallas_call(
        body, name=name, grid=grid,
        in_specs=[a_spec, b_spec] + list(extra_specs),
        out_specs=[out_spec] * n_o,
        out_shape=list(out_shapes),
        scratch_shapes=[pltpu.VMEM(acc_shape, F32)],
        compiler_params=_params(("parallel", "parallel", "arbitrary")),
    )(a, b, *extras)
    return outs if n_o > 1 else outs[0]


def _row_spec(th, cw, ci):
    return pl.BlockSpec((th, cw), lambda i: (i, ci))


def _full_spec(shape):
    return pl.BlockSpec(shape, lambda i: (0,) * len(shape))


def _rowwise(fn, rows, params, outs, n_steps, name):
    n_r, n_p, n_o = len(rows), len(params), len(outs)

    def body(*refs):
        vals = [r[...].astype(F32) for r in refs[:n_r + n_p]]
        res = fn(*vals)
        for o_ref, o in zip(refs[n_r + n_p:], res):
            o_ref[...] = o.astype(o_ref.dtype)

    across = [len(o) == 4 for o in outs]
    res = pl.pallas_call(
        body, name=name, grid=(n_steps,),
        in_specs=[_row_spec(th, cw, ci) for (_, th, cw, ci) in rows] + [_full_spec(p.shape) for p in params],
        out_specs=[pl.BlockSpec((o[0], o[1]), lambda i: (0, i)) if ac else _row_spec(o[0], o[1], 0)
                   for o, ac in zip(outs, across)],
        out_shape=[jax.ShapeDtypeStruct((o[0], n_steps * o[1]) if ac else (n_steps * o[0], o[1]), o[2])
                   for o, ac in zip(outs, across)],
        compiler_params=_params(("parallel",)),
    )(*[r[0] for r in rows], *params)
    return res


def _rowwise_bwd(fn, rows, aux, params, douts, n_steps, name, row_dtypes=None, across=None, dout_from=None):
    n_r, n_a, n_p, n_d = len(rows), len(aux), len(params), len(douts)
    row_dtypes = row_dtypes or (F32,) * n_r

    def body(*refs):
        it = iter(refs)
        r_vals = [next(it)[...].astype(F32) for _ in range(n_r)]
        a_vals = [next(it)[...].astype(F32) for _ in range(n_a)]
        p_vals = [next(it)[...].astype(F32) for _ in range(n_p)]
        d_vals = [next(it)[...].astype(F32) for _ in range(n_d)]
        across_in = next(it) if across is not None else None
        if dout_from is not None:
            d_vals = [_dot_raw(next(it)[...], next(it)[...], "nt")]
        dr_refs = [next(it) for _ in range(n_r)]
        dp_refs = [next(it) for _ in range(n_p)]

        def f(*rp):
            return tuple(fn(*rp[:n_r], *a_vals, *rp[n_r:]))

        _, vjp = jax.vjp(f, *r_vals, *p_vals)
        grads = list(vjp(tuple(d_vals)))
        for dr_ref, g in zip(dr_refs, grads[:n_r]):
            dr_ref[...] = g.astype(dr_ref.dtype)
        if across is not None:
            next(it)[...] = across[1](r_vals, grads[:n_r], across_in[...])

        @pl.when(pl.program_id(0) == 0)
        def _():
            for dp_ref in dp_refs:
                dp_ref[...] = jnp.zeros_like(dp_ref)

        for dp_ref, g in zip(dp_refs, grads[n_r:]):
            dp_ref[...] += g

    all_rows = list(rows) + list(aux) + list(douts)
    in_specs = ([_row_spec(th, cw, ci) for (_, th, cw, ci) in list(rows) + list(aux)]
                + [_full_spec(p.shape) for p in params]
                + [_row_spec(th, cw, ci) for (_, th, cw, ci) in all_rows[n_r + n_a:]])
    across_specs = [] if across is None else [pl.BlockSpec((8, rows[0][1]), lambda i: (0, i))]
    from_specs = [] if dout_from is None else [_row_spec(rows[0][1], dout_from[0].shape[1], 0), _full_spec(dout_from[1].shape)]
    res = pl.pallas_call(
        body, name=name, grid=(n_steps,),
        in_specs=in_specs + across_specs + from_specs,
        out_specs=[_row_spec(th, cw, 0) for (_, th, cw, _) in rows] + [_full_spec(p.shape) for p in params] + across_specs,
        out_shape=[jax.ShapeDtypeStruct((n_steps * th, cw), dt) for (_, th, cw, _), dt in zip(rows, row_dtypes)]
        + [jax.ShapeDtypeStruct(p.shape, F32) for p in params]
        + [jax.ShapeDtypeStruct(across[0].shape, F32) for _ in across_specs],
        compiler_params=_params(("arbitrary",)),
    )(*[r[0] for r in list(rows) + list(aux)], *params, *[r[0] for r in all_rows[n_r + n_a:]],
      *[across[0] for _ in across_specs], *(dout_from or ()))
    return res[:n_r], res[n_r:]


def _sigmoid(x):
    return lax.logistic(x)


def _silu(x):
    return x * _sigmoid(x)


def _rms(x, w=None, n=None):
    n = n or x.shape[-1]
    y = x * lax.rsqrt(jnp.sum(x * x, axis=-1, keepdims=True) * (1.0 / n) + EPS)
    return y if w is None else y * w


def _modulate(x, scale, shift):
    return _rms(x) * (1.0 + scale) + shift


def _softplus(x):
    return jnp.maximum(x, 0.0) + jnp.log1p(jnp.exp(-jnp.abs(x)))


@jax.custom_vjp
def _rot_half64(x):
    lane = lax.broadcasted_iota(jnp.int32, x.shape, 1)
    up = pltpu.roll(x, 96, 1)
    down = pltpu.roll(x, 32, 1)
    return jnp.where(lane < 32, up, jnp.where(lane < 64, down, 0.0))


_rot_half64.defvjp(lambda x: (_rot_half64(x), None), lambda _, g: (_rot_half64(g),))


def _rope128(x, cos_p, sin_p):
    return x * cos_p + _rot_half64(x) * sin_p


def _gdn_pre_fn(qkvc, kab, alog_p, dt_p):
    a = _silu(qkvc)
    qs, ks = [], []
    for h in range(HEADS):
        qh = a[:, HEAD_DIM * h:HEAD_DIM * (h + 1)]
        kh = a[:, 512 + HEAD_DIM * h:512 + HEAD_DIM * (h + 1)]
        qs.append(qh * lax.rsqrt(jnp.sum(qh * qh, axis=-1, keepdims=True) + EPS) * (HEAD_DIM ** -0.5))
        ks.append(kh * lax.rsqrt(jnp.sum(kh * kh, axis=-1, keepdims=True) + EPS))
    lane = lax.broadcasted_iota(jnp.int32, kab.shape, 1)
    g_full = -jnp.exp(alog_p) * _softplus(kab + dt_p)
    b_full = _sigmoid(kab)
    gb = jnp.where((lane >= 64) & (lane < 68), g_full, jnp.where((lane >= 68) & (lane < 72), b_full, 0.0))
    return jnp.concatenate(qs, axis=1), jnp.concatenate(ks, axis=1), a[:, 1024:1536], gb


INTRA_ROWS = (512, 256, 128, 64)
TOKEN_ROWS = (512, 256, 128)

_BNN = (((2,), (1,)), ((0,), (0,)))
_BNT = (((2,), (2,)), ((0,), (0,)))
_BTN = (((1,), (1,)), ((0,), (0,)))


def _split_bf16(a):
    hi = a.astype(BF16)
    return hi, (a - hi.astype(F32)).astype(BF16)


def _dot3_raw(a, b, dims):
    a_hi, a_lo = _split_bf16(a)
    b_hi, b_lo = _split_bf16(b)
    dot = lambda x_, y_: lax.dot_general(x_, y_, dims, preferred_element_type=F32)
    return dot(a_hi, b_hi) + (dot(a_hi, b_lo) + dot(a_lo, b_hi))


@functools.partial(jax.custom_vjp, nondiff_argnums=(2, 3))
def _dot3(a, b, nt, exact_bwd=True):
    return _dot3_raw(a, b, _BNT if nt else _BNN)


def _dot3_fwd(a, b, nt, exact_bwd):
    return _dot3_raw(a, b, _BNT if nt else _BNN), (a, b)


def _dot3_bwd(nt, exact_bwd, res, g):
    a, b = res
    if exact_bwd:
        dot = _dot3_raw
    else:
        dot = lambda x_, y_, d_: lax.dot_general(x_.astype(BF16), y_.astype(BF16), d_, preferred_element_type=F32)
    if nt:
        return dot(g, b, _BNN), dot(jnp.swapaxes(g, 1, 2), a, _BNN)
    return dot(g, b, _BNT), dot(jnp.swapaxes(a, 1, 2), g, _BNN)


_dot3.defvjp(_dot3_fwd, _dot3_bwd)


@functools.partial(jax.custom_vjp, nondiff_argnums=(2,))
def _bdot_b(a, b, nt):
    return lax.dot_general(a.astype(BF16), b.astype(BF16), _BNT if nt else _BNN, preferred_element_type=F32)


def _bdot_b_fwd(a, b, nt):
    return _bdot_b(a, b, nt), (a, b)


def _bdot_b_bwd(nt, res, g):
    a, b = res
    dot = lambda x_, y_, d_: lax.dot_general(x_.astype(BF16), y_.astype(BF16), d_, preferred_element_type=F32)
    if nt:
        return dot(g, b, _BNN), dot(jnp.swapaxes(g, 1, 2), a, _BNN)
    return dot(g, b, _BNT), dot(jnp.swapaxes(a, 1, 2), g, _BNN)


_bdot_b.defvjp(_bdot_b_fwd, _bdot_b_bwd)


@jax.custom_vjp
def _inverse_given(a_mat, inv):
    return inv


def _inverse_given_bwd(inv, g):
    inv_t = jnp.swapaxes(inv, 1, 2)
    return -_dot3_raw(_dot3_raw(inv_t, g, _BNN), inv_t, _BNN), jnp.zeros_like(inv)


_inverse_given.defvjp(lambda a_mat, inv: (inv, inv), _inverse_given_bwd)


def _intra_batched(q, k, v, g_col, b_col, inv_known=None):
    c = CHUNK
    nb = q.shape[0]
    row = lax.broadcasted_iota(jnp.int32, (1, c, c), 1)
    col = lax.broadcasted_iota(jnp.int32, (1, c, c), 2)
    incl, strict, eye = row >= col, row > col, row == col
    tri = jnp.broadcast_to(jnp.where(incl, 1.0, 0.0).astype(F32), (nb, c, c))
    ident = jnp.where(eye, 1.0, 0.0).astype(F32)
    g_wide = _dot3(tri, jnp.broadcast_to(g_col, (nb, c, HEAD_DIM)), False)
    g_i = g_wide[:, :, :c]
    g_j = jnp.sum(jnp.where(eye, g_i, 0.0), axis=1, keepdims=True)
    decay = jnp.where(incl, jnp.exp(jnp.where(incl, g_i - g_j, 0.0)), 0.0)
    kk = _bdot_b(k, k, True)
    a_mat = jnp.where(strict, b_col * kk * decay, 0.0)
    if inv_known is None:
        x_pow = -a_mat
        inv = ident + x_pow
        for _ in range(5):
            x_pow = _dot3(x_pow, x_pow, False, False)
            inv = inv + _dot3(inv, x_pow, False, False)
    else:
        inv = _inverse_given(a_mat, inv_known)
    e_wide = jnp.exp(g_wide)
    u = _dot3(inv, v * b_col, False)
    wk = _dot3(inv, k * b_col * e_wide, False)
    qk = _bdot_b(q, k, True) * decay
    last = lax.broadcasted_iota(jnp.int32, (1, c, HEAD_DIM), 1) == c - 1
    g_last = jnp.sum(jnp.where(last, g_wide, 0.0), axis=1, keepdims=True)
    qd = q * e_wide
    kd = k * jnp.exp(g_last - g_wide)
    gl = jnp.broadcast_to(jnp.exp(g_last), (nb, 8, HEAD_DIM))
    return u, wk, qd, kd, qk, gl, inv


def _gdn_intra_fn(q, k, v, gb, *inv_known):
    t = q.shape[0]
    nch = t // CHUNK
    lane = lax.broadcasted_iota(jnp.int32, gb.shape, 1)

    def heads_first(x_):
        return jnp.concatenate([x_[:, HEAD_DIM * h:HEAD_DIM * (h + 1)].reshape(nch, CHUNK, HEAD_DIM) for h in range(HEADS)],
                               axis=0)

    def column(first_lane):
        return jnp.concatenate([jnp.sum(jnp.where(lane == first_lane + h, gb, 0.0), axis=1, keepdims=True)
                                .reshape(nch, CHUNK, 1) for h in range(HEADS)], axis=0)

    known = jnp.concatenate([x_.reshape(nch, CHUNK, CHUNK) for x_ in inv_known], axis=0) if inv_known else None
    u, wk, qd, kd, qk, gl, inv = _intra_batched(heads_first(q), heads_first(k), heads_first(v), column(64), column(68), known)

    def rows_first(x_):
        r, w_ = x_.shape[1], x_.shape[2]
        return jnp.concatenate([x_[nch * h:nch * (h + 1)].reshape(nch * r, w_) for h in range(HEADS)], axis=1)

    per_head = lambda x_: [x_[nch * h:nch * (h + 1)].reshape(t, CHUNK) for h in range(HEADS)]
    outs = (rows_first(u), rows_first(wk), rows_first(qd), rows_first(kd), *per_head(qk), rows_first(gl))
    return outs if inv_known else outs + tuple(per_head(inv))


def _scan_step(s0, u, wk, qd, kd, qk, gl):
    v_new = u - _bdot_b(wk, s0, False)
    o = _bdot_b(qd, s0, False) + _bdot_b(qk, v_new, False)
    s1 = s0 * gl[:, 0:1, :] + _bdot_b(jnp.swapaxes(kd, 1, 2), v_new, False)
    return o, s1


def _mix_post_fn(o_a, z, o_b, gnw, onw):
    parts = [_rms(o_a[:, HEAD_DIM * h:HEAD_DIM * (h + 1)], gnw) * _silu(z[:, HEAD_DIM * h:HEAD_DIM * (h + 1)])
             for h in range(HEADS)]
    parts += [_rms(o_b[:, HEAD_DIM * h:HEAD_DIM * (h + 1)], onw) for h in range(HEADS)]
    return (jnp.concatenate(parts, axis=1),)


def _mla_pre_fn(ckv, cq, kab, cos_p, sin_p, qnw, kvnw, wuq, wukv, qn_w, qr_w, kn_w, kr_w):
    scale = (HEAD_DIM + ROPE) ** -0.5 * LOG2_E
    qf = _bdot(_rms(cq, qnw), wuq, "nn")
    kvf = _bdot(_rms(ckv, kvnw), wukv, "nn")
    lane = lax.broadcasted_iota(jnp.int32, kab.shape, 1)
    kr = _rope128(_rms(jnp.where(lane < ROPE, kab, 0.0), kr_w, n=ROPE), cos_p, sin_p)
    qs, ks = [], []
    for h in range(HEADS):
        qn = _rms(qf[:, 256 * h:256 * h + 128], qn_w) * scale
        qr = _rope128(_rms(qf[:, 256 * h + 128:256 * h + 256], qr_w, n=ROPE), cos_p, sin_p) * scale
        qs += [qn, qr]
        ks += [_rms(kvf[:, 128 * h:128 * (h + 1)], kn_w), kr]
    return jnp.concatenate(qs, axis=1), jnp.concatenate(ks, axis=1), kvf[:, 512:]


def _conv_fwd(proj, conv_w, alog_p, dt_p, tm, name):
    S = proj.shape[0]
    C = 1536
    nb = tm // 8

    def body(x_ref, prev_ref, kab_ref, w_ref, al_ref, dt_ref, o_ref, q_ref, k_ref, v_ref, gb_ref, ext_ref):
        i = pl.program_id(0)
        ext_ref[0:8, :] = jnp.where(i > 0, prev_ref[...], 0.0)
        ext_ref[8:, :] = x_ref[...]
        acc = jnp.zeros((tm, C), F32)
        for k in range(4):
            acc = acc + w_ref[k:k + 1, :] * ext_ref[pl.ds(5 + k, tm), :]
        o_ref[...] = acc
        for ref, val in zip((q_ref, k_ref, v_ref, gb_ref), _gdn_pre_fn(acc, kab_ref[...], al_ref[...], dt_ref[...])):
            ref[...] = val

    third = pl.BlockSpec((tm, 512), lambda i: (i, 0))
    return pl.pallas_call(
        body, name=name, grid=(S // tm,),
        in_specs=[pl.BlockSpec((tm, C), lambda i: (i, 0)),
                  pl.BlockSpec((8, C), lambda i: (jnp.maximum(i * nb - 1, 0), 0)),
                  _row_spec(tm, 128, 21), pl.BlockSpec((4, C), lambda i: (0, 0)), _full_spec(alog_p.shape), _full_spec(dt_p.shape)],
        out_specs=[pl.BlockSpec((tm, C), lambda i: (i, 0)), third, third, third, _row_spec(tm, 128, 0)],
        out_shape=[jax.ShapeDtypeStruct((S, C), F32)] + [jax.ShapeDtypeStruct((S, 512), F32)] * 3
        + [jax.ShapeDtypeStruct((S, 128), F32)],
        scratch_shapes=[pltpu.VMEM((tm + 8, C), F32)],
        compiler_params=_params(("arbitrary",)),
    )(proj, proj, proj, conv_w, alog_p, dt_p)


def _conv_bwd(proj, qkvc, conv_w, alog_p, dt_p, douts, d_kab_add, tm, name):
    S = proj.shape[0]
    C = 1536
    nb = tm // 8
    n_steps = S // tm

    def body(x_ref, prev_ref, c_ref, kab_ref, w_ref, al_ref, dt_ref, dq_ref, dk_ref, dv_ref, dgb_ref, add_ref,
             dx_ref, dkab_ref, dw_ref, dal_ref, ddt_ref, xext_ref, dext_ref, halo_ref):
        step = pl.program_id(0)

        @pl.when(step == 0)
        def _():
            halo_ref[...] = jnp.zeros_like(halo_ref)
            for r in (dw_ref, dal_ref, ddt_ref):
                r[...] = jnp.zeros_like(r)

        _, vjp = jax.vjp(_gdn_pre_fn, c_ref[...], kab_ref[...], al_ref[...], dt_ref[...])
        d, d_kab, d_al, d_dt = vjp((dq_ref[...], dk_ref[...], dv_ref[...], dgb_ref[...]))
        dkab_ref[...] = (d_kab + add_ref[...]).astype(dkab_ref.dtype)
        xext_ref[0:8, :] = jnp.where(step < n_steps - 1, prev_ref[...], 0.0)
        xext_ref[8:, :] = x_ref[...]
        dext_ref[0:tm, :] = d
        dext_ref[tm:, :] = halo_ref[...]
        halo_ref[...] = d[0:8, :]
        acc = jnp.zeros((tm, C), F32)
        dws = []
        for k in range(4):
            acc = acc + w_ref[k:k + 1, :] * dext_ref[pl.ds(3 - k, tm), :]
            dws.append(jnp.sum(d * xext_ref[pl.ds(5 + k, tm), :], axis=0, keepdims=True))
        dx_ref[...] = acc.astype(dx_ref.dtype)
        dw_ref[...] += jnp.concatenate(dws + [jnp.zeros((4, C), F32)], axis=0)
        dal_ref[...] += d_al
        ddt_ref[...] += d_dt

    tile = lambda cols, block=0: pl.BlockSpec((tm, cols), lambda s_: (n_steps - 1 - s_, block))
    return pl.pallas_call(
        body, name=name, grid=(n_steps,),
        in_specs=[tile(C), pl.BlockSpec((8, C), lambda s_: (jnp.maximum((n_steps - 1 - s_) * nb - 1, 0), 0)),
                  tile(C), tile(128, 21), pl.BlockSpec((4, C), lambda s_: (0, 0)), _full_spec(alog_p.shape), _full_spec(dt_p.shape),
                  tile(512), tile(512), tile(512), tile(128), tile(128)],
        out_specs=[tile(C), tile(128), pl.BlockSpec((8, C), lambda s_: (0, 0)), _full_spec(alog_p.shape), _full_spec(dt_p.shape)],
        out_shape=[jax.ShapeDtypeStruct((S, C), BF16), jax.ShapeDtypeStruct((S, 128), BF16), jax.ShapeDtypeStruct((8, C), F32),
                   jax.ShapeDtypeStruct(alog_p.shape, F32), jax.ShapeDtypeStruct(dt_p.shape, F32)],
        scratch_shapes=[pltpu.VMEM((tm + 8, C), F32), pltpu.VMEM((tm + 8, C), F32), pltpu.VMEM((8, C), F32)],
        compiler_params=_params(("arbitrary",)),
    )(proj, proj, qkvc, proj, conv_w, alog_p, dt_p, *douts, d_kab_add)


SCAN_CHUNKS = (8, 4, 2, 1)


def _gdn_scan_fwd(u, wk, qd, kd, qks, gl, name):
    S = u.shape[0]
    nc = S // CHUNK
    cs = _pick(nc, SCAN_CHUNKS)
    W = HEADS * HEAD_DIM

    def body(u_ref, wk_ref, qd_ref, kd_ref, qk0, qk1, qk2, qk3, gl_ref, o_ref, sp_ref, s_ref):
        @pl.when(pl.program_id(0) == 0)
        def _():
            s_ref[...] = jnp.zeros_like(s_ref)

        state = s_ref[...]
        for c in range(cs):
            rows, gl_rows = slice(CHUNK * c, CHUNK * (c + 1)), slice(8 * c, 8 * (c + 1))
            sp_ref[c] = state
            o, state = _scan_step(state, _heads(u_ref, HEAD_DIM, rows), _heads(wk_ref, HEAD_DIM, rows),
                                  _heads(qd_ref, HEAD_DIM, rows), _heads(kd_ref, HEAD_DIM, rows),
                                  jnp.stack([r[rows, :] for r in (qk0, qk1, qk2, qk3)]), _heads(gl_ref, HEAD_DIM, gl_rows))
            for h in range(HEADS):
                o_ref[rows, HEAD_DIM * h:HEAD_DIM * (h + 1)] = o[h]
        s_ref[...] = state

    row = pl.BlockSpec((cs * CHUNK, W), lambda n: (n, 0))
    qk_spec = pl.BlockSpec((cs * CHUNK, CHUNK), lambda n: (n, 0))
    return pl.pallas_call(
        body, name=name, grid=(nc // cs,),
        in_specs=[row, row, row, row, qk_spec, qk_spec, qk_spec, qk_spec, pl.BlockSpec((cs * 8, W), lambda n: (n, 0))],
        out_specs=[row, pl.BlockSpec((cs, HEADS, HEAD_DIM, HEAD_DIM), lambda n: (n, 0, 0, 0))],
        out_shape=[jax.ShapeDtypeStruct((S, W), F32), jax.ShapeDtypeStruct((nc, HEADS, HEAD_DIM, HEAD_DIM), F32)],
        scratch_shapes=[pltpu.VMEM((HEADS, HEAD_DIM, HEAD_DIM), F32)],
        compiler_params=_params(("arbitrary",)),
    )(u, wk, qd, kd, *qks, gl)


def _gdn_scan_bwd(u, wk, qd, kd, qks, gl, s_prev, d_o, name):
    S = u.shape[0]
    nc = S // CHUNK
    cs = _pick(nc, SCAN_CHUNKS)
    nb = nc // cs
    W = HEADS * HEAD_DIM

    def body(u_ref, wk_ref, qd_ref, kd_ref, qk0, qk1, qk2, qk3, gl_ref, sp_ref, do_ref,
             du_ref, dwk_ref, dqd_ref, dkd_ref, dqk0, dqk1, dqk2, dqk3, dgl_ref, ds_ref):
        @pl.when(pl.program_id(0) == 0)
        def _():
            ds_ref[...] = jnp.zeros_like(ds_ref)

        d_state = ds_ref[...]
        for c in reversed(range(cs)):
            rows, gl_rows = slice(CHUNK * c, CHUNK * (c + 1)), slice(8 * c, 8 * (c + 1))
            _, vjp = jax.vjp(_scan_step, sp_ref[c], _heads(u_ref, HEAD_DIM, rows), _heads(wk_ref, HEAD_DIM, rows),
                             _heads(qd_ref, HEAD_DIM, rows), _heads(kd_ref, HEAD_DIM, rows),
                             jnp.stack([r[rows, :] for r in (qk0, qk1, qk2, qk3)]), _heads(gl_ref, HEAD_DIM, gl_rows))
            d_state, du, dwk, dqd, dkd, dqk, dgl = vjp((_heads(do_ref, HEAD_DIM, rows), d_state))
            for h, dqk_ref in enumerate((dqk0, dqk1, dqk2, dqk3)):
                sl = slice(HEAD_DIM * h, HEAD_DIM * (h + 1))
                du_ref[rows, sl] = du[h]
                dwk_ref[rows, sl] = dwk[h]
                dqd_ref[rows, sl] = dqd[h]
                dkd_ref[rows, sl] = dkd[h]
                dqk_ref[rows, :] = dqk[h]
                dgl_ref[gl_rows, sl] = dgl[h]
        ds_ref[...] = d_state

    rev = lambda n: (nb - 1 - n, 0)
    row = pl.BlockSpec((cs * CHUNK, W), rev)
    qk_spec = pl.BlockSpec((cs * CHUNK, CHUNK), rev)
    gl_spec = pl.BlockSpec((cs * 8, W), rev)
    qk_shape = jax.ShapeDtypeStruct((S, CHUNK), F32)
    row_shape = jax.ShapeDtypeStruct((S, W), F32)
    return pl.pallas_call(
        body, name=name, grid=(nb,),
        in_specs=[row, row, row, row, qk_spec, qk_spec, qk_spec, qk_spec, gl_spec,
                  pl.BlockSpec((cs, HEADS, HEAD_DIM, HEAD_DIM), lambda n: (nb - 1 - n, 0, 0, 0)), row],
        out_specs=[row, row, row, row, qk_spec, qk_spec, qk_spec, qk_spec, gl_spec],
        out_shape=[row_shape] * 4 + [qk_shape] * 4 + [jax.ShapeDtypeStruct((nc * 8, W), F32)],
        scratch_shapes=[pltpu.VMEM((HEADS, HEAD_DIM, HEAD_DIM), F32)],
        compiler_params=_params(("arbitrary",)),
    )(u, wk, qd, kd, *qks, gl, s_prev, d_o)


NEG = -1e30


def _chunk_mask(i, j, t, transposed=False):
    q_axis, k_axis = (1, 0) if transposed else (0, 1)
    r = (i * t + lax.broadcasted_iota(jnp.int32, (t, t), q_axis)) // CHUNK
    c = (j * t + lax.broadcasted_iota(jnp.int32, (t, t), k_axis)) // CHUNK
    return c <= r


def _tile_pairs(n, by_key):
    pairs = [(i, j) for j in range(n) for i in range(j, n)] if by_key else [(i, j) for i in range(n) for j in range(i + 1)]
    return jnp.asarray(np.array([p[0] for p in pairs], np.int32)), jnp.asarray(np.array([p[1] for p in pairs], np.int32))


def _heads(ref, width, rows=slice(None)):
    return jnp.stack([ref[rows, width * h:width * (h + 1)] for h in range(HEADS)])


def _bmm(a, b, dims):
    return lax.dot_general(a.astype(BF16), b.astype(BF16), dims, preferred_element_type=F32)


def _attn_fwd(q, k, v_t, t, name):
    S = q.shape[0]
    n = S // t
    qi, kj = _tile_pairs(n, by_key=False)

    def body(qi_ref, kj_ref, q_ref, k_ref, vt_ref, o_ref, lse_ref, m_ref, l_ref, acc_ref):
        i, j = qi_ref[pl.program_id(0)], kj_ref[pl.program_id(0)]

        @pl.when(j == 0)
        def _():
            m_ref[...] = jnp.full_like(m_ref, NEG)
            l_ref[...] = jnp.zeros_like(l_ref)
            acc_ref[...] = jnp.zeros_like(acc_ref)

        def update(masked):
            s_t = _bmm(_heads(k_ref, 256), _heads(q_ref, 256), _BNT)
            if masked:
                s_t = jnp.where(_chunk_mask(i, j, t, transposed=True)[None], s_t, NEG)
            m_old = m_ref[...]
            m_new = jnp.maximum(m_old, jnp.max(s_t, axis=1, keepdims=True))
            p_t = jnp.exp2(s_t - m_new)
            alpha = jnp.exp2(m_old - m_new)
            l_ref[...] = alpha * l_ref[...] + jnp.sum(p_t, axis=1, keepdims=True)
            v_heads = jnp.stack([vt_ref[HEAD_DIM * h:HEAD_DIM * (h + 1), :] for h in range(HEADS)])
            acc_ref[...] = alpha * acc_ref[...] + _bmm(v_heads, p_t, _BNN)
            m_ref[...] = m_new

        @pl.when(j < i)
        def _():
            update(False)

        @pl.when(j == i)
        def _():
            update(True)
            for h in range(HEADS):
                sl = slice(HEAD_DIM * h, HEAD_DIM * (h + 1))
                o_ref[:, sl] = jnp.transpose(acc_ref[h] / l_ref[h])
                lse_ref[h:h + 1, :] = m_ref[h] + jnp.log(l_ref[h]) * LOG2_E
            lse_ref[HEADS:, :] = jnp.zeros((8 - HEADS, t), F32)

    row = lambda p, qi_, kj_: (qi_[p], 0)
    return pl.pallas_call(
        body, name=name,
        grid_spec=pltpu.PrefetchScalarGridSpec(
            num_scalar_prefetch=2, grid=(qi.shape[0],),
            in_specs=[pl.BlockSpec((t, HEADS * 256), row), pl.BlockSpec((t, HEADS * 256), lambda p, qi_, kj_: (kj_[p], 0)),
                      pl.BlockSpec((HEADS * HEAD_DIM, t), lambda p, qi_, kj_: (0, kj_[p]))],
            out_specs=[pl.BlockSpec((t, HEADS * HEAD_DIM), row), pl.BlockSpec((8, t), lambda p, qi_, kj_: (0, qi_[p]))],
            scratch_shapes=[pltpu.VMEM((HEADS, 1, t), F32), pltpu.VMEM((HEADS, 1, t), F32),
                            pltpu.VMEM((HEADS, HEAD_DIM, t), F32)]),
        out_shape=[jax.ShapeDtypeStruct((S, HEADS * HEAD_DIM), F32), jax.ShapeDtypeStruct((8, S), F32)],
        compiler_params=_params(("arbitrary",)),
    )(qi, kj, q, k, v_t)


def _attn_stats(o, d_o, lse):
    lane = lax.broadcasted_iota(jnp.int32, (o.shape[0], HEAD_DIM), 1)
    stats = jnp.zeros((o.shape[0], HEAD_DIM), F32)
    for h in range(HEADS):
        sl = slice(HEAD_DIM * h, HEAD_DIM * (h + 1))
        delta = jnp.sum(d_o[:, sl] * o[:, sl], axis=1, keepdims=True)
        stats = stats + jnp.where(lane == HEADS + h, delta, 0.0)
    return lse + jnp.transpose(stats)[0:8, :]


BWD_GROUP = 2


def _attn_bwd(q, k, v, d_o, stats, t, name):
    S = q.shape[0]
    n = S // t
    groups = HEADS // BWD_GROUP
    gq, gv = BWD_GROUP * 256, BWD_GROUP * HEAD_DIM
    qi, kj = _tile_pairs(n, by_key=True)
    n_pairs = qi.shape[0]
    st = stats.reshape(2, groups, BWD_GROUP, S).transpose(1, 0, 2, 3).reshape(groups, 2 * BWD_GROUP, S)
    st = jnp.pad(st, ((0, 0), (0, 8 - 2 * BWD_GROUP), (0, 0)))

    def heads(ref, width, rows=slice(None)):
        return jnp.stack([ref[rows, width * h:width * (h + 1)] for h in range(BWD_GROUP)])

    def body(qi_ref, kj_ref, q_ref, k_ref, v_ref, do_ref, st_ref, dq_hbm, dk_ref, dv_ref, dq_acc, sem):
        g, p = pl.program_id(0), pl.program_id(1)
        i, j = qi_ref[p], kj_ref[p]

        @pl.when(i == j)
        def _():
            dk_ref[...] = jnp.zeros_like(dk_ref)
            dv_ref[...] = jnp.zeros_like(dv_ref)

        def update(masked):
            qh, kh = heads(q_ref, 256), heads(k_ref, 256)
            d_out = heads(do_ref, HEAD_DIM)
            stv = st_ref[...]
            lse_row = jnp.stack([stv[h:h + 1, :] for h in range(BWD_GROUP)])
            delta_row = jnp.stack([stv[BWD_GROUP + h:BWD_GROUP + h + 1, :] for h in range(BWD_GROUP)])
            s_t = _bmm(kh, qh, _BNT)
            p_t = jnp.exp2(s_t - lse_row)
            if masked:
                p_t = jnp.where(_chunk_mask(i, j, t, transposed=True)[None], p_t, 0.0)
            dv = _bmm(p_t, d_out, _BNN)
            dp_t = _bmm(heads(v_ref, HEAD_DIM), d_out, _BNT)
            ds_t = p_t * (dp_t - delta_row)
            dk = _bmm(ds_t, qh, _BNN)
            dq = _bmm(ds_t, kh, _BTN)
            rows = pl.ds(pl.multiple_of(i * t, t), t)
            for h in range(BWD_GROUP):
                dk_ref[:, 256 * h:256 * (h + 1)] += dk[h]
                dv_ref[:, HEAD_DIM * h:HEAD_DIM * (h + 1)] += dv[h]

            @pl.when(j == 0)
            def _():
                for h in range(BWD_GROUP):
                    dq_acc[rows, 256 * h:256 * (h + 1)] = dq[h]

            @pl.when(j > 0)
            def _():
                for h in range(BWD_GROUP):
                    dq_acc[rows, 256 * h:256 * (h + 1)] += dq[h]

        def dq_copy(block, gg):
            rows = pl.ds(pl.multiple_of(block * t, t), t)
            return pltpu.make_async_copy(dq_acc.at[rows, :], dq_hbm.at[rows, gq * gg:gq * (gg + 1)], sem.at[block])

        @pl.when(i == j)
        def _():
            update(True)
            rows = pl.ds(pl.multiple_of(i * t, t), t)
            dq_acc[rows, :] *= 1.0 / LOG2_E
            for gg in range(groups):
                @pl.when(g == gg)
                def _():
                    dq_copy(i, gg).start()

        @pl.when(i > j)
        def _():
            update(False)

        @pl.when(i == n - 1)
        def _():
            dk_ref[...] *= 1.0 / LOG2_E

        @pl.when(p == n_pairs - 1)
        def _():
            for gg in range(groups):
                @pl.when(g == gg)
                def _():
                    for block in range(n):
                        dq_copy(block, gg).wait()

    q_blk = lambda g, p, qi_, kj_: (qi_[p], g)
    k_blk = lambda g, p, qi_, kj_: (kj_[p], g)
    return pl.pallas_call(
        body, name=name,
        grid_spec=pltpu.PrefetchScalarGridSpec(
            num_scalar_prefetch=2, grid=(groups, n_pairs),
            in_specs=[pl.BlockSpec((t, gq), q_blk), pl.BlockSpec((t, gq), k_blk), pl.BlockSpec((t, gv), k_blk),
                      pl.BlockSpec((t, gv), q_blk), pl.BlockSpec((None, 8, t), lambda g, p, qi_, kj_: (g, 0, qi_[p]))],
            out_specs=[pl.BlockSpec(memory_space=pl.ANY), pl.BlockSpec((t, gq), k_blk), pl.BlockSpec((t, gv), k_blk)],
            scratch_shapes=[pltpu.VMEM((S, gq), F32), pltpu.SemaphoreType.DMA((n,))]),
        out_shape=[jax.ShapeDtypeStruct((S, HEADS * 256), F32), jax.ShapeDtypeStruct((S, HEADS * 256), F32),
                   jax.ShapeDtypeStruct((S, HEADS * HEAD_DIM), F32)],
        compiler_params=_params(("arbitrary", "arbitrary")),
    )(qi, kj, q, k, v, d_o, st)


FFN_PIECE = 2 * D_FF // N_DEV
HID_PIECES = D_FF // FFN_PIECE


def _after_specs(after):
    return [] if after is None else [pl.BlockSpec(memory_space=pl.ANY)]


def _after_args(after):
    return [] if after is None else [after]


def _ffn_gw8(h, d_gate, d_up, name, after=None):
    S = h.shape[0]
    tm = 512
    tk = _pick(S, MATMUL_ROWS)
    nk = S // tk

    def body(h_ref, dg_ref, du_ref, *rest):
        o_ref, acc_ref = rest[-2:]
        k = pl.program_id(1)

        @pl.when(k == 0)
        def _():
            acc_ref[...] = jnp.zeros_like(acc_ref)

        h_t = jnp.transpose(h_ref[...])
        for p in range(HID_PIECES):
            acc_ref[p] += _dot_raw(h_t, dg_ref[p], "nn")
            acc_ref[HID_PIECES + p] += _dot_raw(h_t, du_ref[p], "nn")

        @pl.when(k == nk - 1)
        def _():
            o_ref[...] = acc_ref[...].astype(o_ref.dtype)

    d_spec = pl.BlockSpec((HID_PIECES, tk, FFN_PIECE), lambda i, k: (0, k, 0))
    return pl.pallas_call(
        body, name=name, grid=(D_MODEL // tm, nk),
        in_specs=[pl.BlockSpec((tk, tm), lambda i, k: (k, i)), d_spec, d_spec] + _after_specs(after),
        out_specs=pl.BlockSpec((2 * HID_PIECES, tm, FFN_PIECE), lambda i, k: (0, i, 0)),
        out_shape=jax.ShapeDtypeStruct((2 * HID_PIECES, D_MODEL, FFN_PIECE), BF16),
        scratch_shapes=[pltpu.VMEM((2 * HID_PIECES, tm, FFN_PIECE), F32)],
        compiler_params=_params(("parallel", "arbitrary")),
    )(h, d_gate, d_up, *_after_args(after))


EPILOGUE_ROWS = 256


def _dmod_epilogue(acc_ref, x_ref, do_ref, sc_ref, sh_ref, dx_ref, dsc_ref, dsh_ref, below, below_in, below_out):
    rows_total = acc_ref.shape[0]
    step = min(EPILOGUE_ROWS, rows_total)
    dsc, dsh, dg = 0.0, 0.0, 0.0
    for r in range(rows_total // step):
        rows = slice(step * r, step * (r + 1))
        _, vjp = jax.vjp(_modulate, x_ref[rows, :], sc_ref[...], sh_ref[...])
        dx, dsc_r, dsh_r = vjp(acc_ref[rows, :])
        dx = dx + do_ref[rows, :]
        dx_ref[rows, :] = dx
        dsc, dsh = dsc + dsc_r, dsh + dsh_r
        if below is not None:
            coef = below[2]
            below_out[0][rows, :] = (coef * below_in[1][...] * dx).astype(below_out[0].dtype)
            dg = dg + jnp.sum(coef * below_in[0][rows, :] * dx, axis=0, keepdims=True)
    dsc_ref[...] += dsc
    dsh_ref[...] += dsh
    if below is not None:
        below_out[1][...] += dg


def _ffn_dh(d_gate, d_up, w8, x, d_out, scale, shift, name, after=None, below=None):
    S = d_gate.shape[1]
    tm = _pick(S, MATMUL_ROWS)
    n_below = 0 if below is None else 2

    def body(dg_ref, du_ref, wg_ref, wu_ref, x_ref, do_ref, sc_ref, sh_ref, *rest):
        below_in = rest[:n_below]
        outs = rest[len(rest) - 4 - n_below:]
        dx_ref, dsc_ref, dsh_ref = outs[:3]
        below_out, acc_ref = outs[3:3 + n_below], outs[-1]
        i, k = pl.program_id(0), pl.program_id(1)

        @pl.when(k == 0)
        def _():
            acc_ref[...] = jnp.zeros_like(acc_ref)

        acc_ref[...] += _dot_raw(dg_ref[...], wg_ref[...], "nt") + _dot_raw(du_ref[...], wu_ref[...], "nt")

        @pl.when((k == 0) & (i == 0))
        def _():
            for r in (dsc_ref, dsh_ref) + tuple(below_out[1:]):
                r[...] = jnp.zeros_like(r)

        @pl.when(k == HID_PIECES - 1)
        def _():
            _dmod_epilogue(acc_ref, x_ref, do_ref, sc_ref, sh_ref, dx_ref, dsc_ref, dsh_ref, below, below_in, below_out)

    d_spec = pl.BlockSpec((None, tm, FFN_PIECE), lambda i, k: (k, i, 0))
    row = pl.BlockSpec((tm, D_MODEL), lambda i, k: (i, 0))
    par = pl.BlockSpec((1, D_MODEL), lambda i, k: (0, 0))
    row_shape, par_shape = jax.ShapeDtypeStruct((S, D_MODEL), F32), jax.ShapeDtypeStruct((1, D_MODEL), F32)
    return pl.pallas_call(
        body, name=name, grid=(S // tm, HID_PIECES),
        in_specs=[d_spec, d_spec,
                  pl.BlockSpec((None, D_MODEL, FFN_PIECE), lambda i, k: (k, 0, 0)),
                  pl.BlockSpec((None, D_MODEL, FFN_PIECE), lambda i, k: (k + HID_PIECES, 0, 0)),
                  row, row, par, par] + [row, par][:n_below] + _after_specs(after),
        out_specs=[row, par, par] + [row, par][:n_below],
        out_shape=[row_shape, par_shape, par_shape] + [jax.ShapeDtypeStruct((S, D_MODEL), BF16), par_shape][:n_below],
        scratch_shapes=[pltpu.VMEM((tm, D_MODEL), F32)],
        compiler_params=_params(("arbitrary", "arbitrary")),
    )(d_gate, d_up, w8, w8, x, d_out, scale, shift, *(below[:2] if below is not None else ()), *_after_args(after))


def _swiglu_bwd(d_hid, hid_by_gate, hid_by_up):
    return d_hid * hid_by_gate, d_hid * hid_by_up


def _adamw_math(w_, g_, m_, v_):
    m_ = ADAM_B1 * m_ + (1.0 - ADAM_B1) * g_
    v_ = ADAM_B2 * v_ + (1.0 - ADAM_B2) * (g_ * g_)
    m_hat = m_ / (1.0 - ADAM_B1 ** ADAM_STEP)
    v_hat = v_ / (1.0 - ADAM_B2 ** ADAM_STEP)
    return -ADAM_LR * (m_hat / (jnp.sqrt(v_hat) + ADAM_EPS) + ADAM_WD * w_), m_, v_


def _adamw(w, g, m, v, name):
    R, C = w.shape
    tr = _pick(R, (256, 176, 128, 64, 32, 16, 8))

    def body(w_ref, g_ref, m_ref, v_ref, d_ref, nm_ref, nv_ref):
        d_ref[...], nm_ref[...], nv_ref[...] = _adamw_math(w_ref[...], g_ref[...], m_ref[...], v_ref[...])

    spec = pl.BlockSpec((tr, C), lambda i: (i, 0))
    return pl.pallas_call(
        body, name=name, grid=(R // tr,),
        in_specs=[spec] * 4, out_specs=[spec] * 3,
        out_shape=[jax.ShapeDtypeStruct((R, C), F32)] * 3,
        compiler_params=_params(("parallel",)),
    )(w, g, m, v)


def _sum_adamw(parts, w, m, v, name, transposed=False):
    _, R, C = parts.shape
    tr = _pick(R, (256, 176, 128, 64, 32, 16, 8))

    def body(p_ref, w_ref, m_ref, v_ref, g_ref, d_ref, nm_ref, nv_ref):
        g_ = p_ref[0].astype(F32)
        for d in range(1, N_DEV):
            g_ = g_ + p_ref[d].astype(F32)
        if transposed:
            g_ = jnp.transpose(g_)
        g_ref[...] = g_
        d_ref[...], nm_ref[...], nv_ref[...] = _adamw_math(w_ref[...], g_, m_ref[...], v_ref[...])

    spec = pl.BlockSpec((C, tr), lambda i: (0, i)) if transposed else pl.BlockSpec((tr, C), lambda i: (i, 0))
    return pl.pallas_call(
        body, name=name, grid=(R // tr,),
        in_specs=[pl.BlockSpec((N_DEV, tr, C), lambda i: (0, i, 0)), spec, spec, spec], out_specs=[spec] * 4,
        out_shape=[jax.ShapeDtypeStruct(w.shape, F32)] * 4,
        compiler_params=_params(("parallel",)),
    )(parts, w, m, v)


def _sum_devices(parts, name):
    _, R, C = parts.shape
    tr = _pick(R, (512, 256, 176, 128, 64, 32, 16, 8))

    def body(p_ref, o_ref):
        acc = p_ref[0].astype(F32)
        for d in range(1, N_DEV):
            acc = acc + p_ref[d].astype(F32)
        o_ref[...] = acc

    return pl.pallas_call(
        body, name=name, grid=(R // tr,),
        in_specs=[pl.BlockSpec((N_DEV, tr, C), lambda i: (0, i, 0))],
        out_specs=pl.BlockSpec((tr, C), lambda i: (i, 0)),
        out_shape=jax.ShapeDtypeStruct((R, C), F32),
        compiler_params=_params(("parallel",)),
    )(parts)


def _my_place():
    return lax.axis_index("x"), lax.axis_index("y"), lax.axis_index("c")


def _all_gather(blocks, name):
    n = len(blocks)

    def body(*refs):
        x_refs, out_refs = refs[:n], refs[n:2 * n]
        send_sems, recv_sems, local_sems = refs[2 * n:]
        x, y, c = _my_place()
        me, sibling = (x, y, c), (x, y, 1 - c)
        chips = [(1 - x, y), (x, 1 - y), (1 - x, 1 - y)]

        def copy(a, k, blk, to, own=False):
            slot = out_refs[a].at[4 * blk[0] + 2 * blk[1] + blk[2]]
            return pltpu.make_async_remote_copy(
                src_ref=x_refs[a] if own else slot, dst_ref=slot,
                send_sem=send_sems.at[7 * a + k], recv_sem=recv_sems.at[7 * a + k], device_id=to, device_id_type=MESH)

        mine = [pltpu.make_async_copy(x_refs[a], out_refs[a].at[4 * x + 2 * y + c], local_sems.at[a]) for a in range(n)]
        for cp in mine:
            cp.start()
        first = []
        for j, chip in enumerate(chips):
            first += [copy(a, 1 + j, me, (*chip, c), own=True) for a in range(n)]
        first += [copy(a, 0, me, sibling, own=True) for a in range(n)]
        for cp in first:
            cp.start()
        passed = []
        for j, chip in enumerate(chips):
            for a in range(n):
                copy(a, 1 + j, (*chip, c), me).wait_recv()
                passed.append(copy(a, 4 + j, (*chip, c), sibling))
                passed[-1].start()
        for a in range(n):
            copy(a, 0, sibling, me).wait_recv()
        for j, chip in enumerate(chips):
            for a in range(n):
                copy(a, 4 + j, (*chip, 1 - c), me).wait_recv()
        for cp in first + passed:
            cp.wait_send()
        for cp in mine:
            cp.wait()

    return pl.pallas_call(
        body, name=name,
        out_shape=[jax.ShapeDtypeStruct((N_DEV,) + b.shape, b.dtype) for b in blocks],
        in_specs=[pl.BlockSpec(memory_space=pl.ANY)] * n,
        out_specs=[pl.BlockSpec(memory_space=pl.ANY)] * n,
        scratch_shapes=[pltpu.SemaphoreType.DMA((7 * n,)), pltpu.SemaphoreType.DMA((7 * n,)), pltpu.SemaphoreType.DMA((n,))],
    )(*blocks)


def _all_to_all(pieces, name):
    n = len(pieces)

    def body(*refs):
        x_refs, out_refs = refs[:n], refs[n:2 * n]
        send_sems, recv_sems, local_sems = refs[2 * n:]
        x, y, c = _my_place()
        me = 4 * x + 2 * y + c
        mine = [pltpu.make_async_copy(x_refs[a].at[me], out_refs[a].at[me], local_sems.at[a]) for a in range(n)]
        for cp in mine:
            cp.start()
        copies = []
        for k in (2, 4, 6, 3, 5, 7, 1):
            px = 1 - x if k & 4 else x
            py = 1 - y if k & 2 else y
            pc = 1 - c if k & 1 else c
            peer = 4 * px + 2 * py + pc
            for a in range(n):
                copies.append(pltpu.make_async_remote_copy(
                    src_ref=x_refs[a].at[peer], dst_ref=out_refs[a].at[me],
                    send_sem=send_sems.at[7 * a + k - 1], recv_sem=recv_sems.at[7 * a + k - 1],
                    device_id=(px, py, pc), device_id_type=MESH))
        for cp in copies:
            cp.start()
        for cp in copies:
            cp.wait_recv()
        for cp in copies:
            cp.wait_send()
        for cp in mine:
            cp.wait()

    return pl.pallas_call(
        body, name=name,
        out_shape=[jax.ShapeDtypeStruct(p.shape, p.dtype) for p in pieces],
        in_specs=[pl.BlockSpec(memory_space=pl.ANY)] * n,
        out_specs=[pl.BlockSpec(memory_space=pl.ANY)] * n,
        scratch_shapes=[pltpu.SemaphoreType.DMA((7 * n,)), pltpu.SemaphoreType.DMA((7 * n,)), pltpu.SemaphoreType.DMA((n,))],
    )(*pieces)


def _peers():
    x, y, c = _my_place()
    out = []
    for k in (2, 4, 6, 3, 5, 7, 1):
        px = 1 - x if k & 4 else x
        py = 1 - y if k & 2 else y
        pc = 1 - c if k & 1 else c
        out.append((k, (px, py, pc), 4 * px + 2 * py + pc))
    return out


def _exchange_copies(x_refs, land_refs, send_sems, recv_sems, scatter):
    x, y, c = _my_place()
    me = 4 * x + 2 * y + c
    starts, arrivals = [], []
    for k, place, peer in _peers():
        for a, (x_ref, land_ref) in enumerate(zip(x_refs, land_refs)):
            sems = dict(send_sem=send_sems.at[7 * a + k - 1], recv_sem=recv_sems.at[7 * a + k - 1],
                        device_id=place, device_id_type=MESH)
            src = x_ref.at[peer] if scatter else x_ref
            starts.append(pltpu.make_async_remote_copy(src_ref=src, dst_ref=land_ref.at[me], **sems))
            arrivals.append(pltpu.make_async_remote_copy(src_ref=src, dst_ref=land_ref.at[peer], **sems))
    return starts, arrivals


def _exchange_start(arrays, scatter, name):
    n = len(arrays)
    hbm = pl.BlockSpec(memory_space=pltpu.HBM)
    sem = pl.BlockSpec(memory_space=pltpu.SEMAPHORE)
    lands = [lax.empty(a.shape if scatter else (N_DEV,) + a.shape, a.dtype) for a in arrays]

    def body(*refs):
        x_refs, land_refs = refs[:n], refs[n:2 * n]
        send_sems, recv_sems = refs[2 * n], refs[2 * n + 1]
        token = refs[-1]
        starts, _ = _exchange_copies(x_refs, land_refs, send_sems, recv_sems, scatter)
        for cp in starts:
            cp.start()
        token[...] = jnp.zeros_like(token)

    res = pl.pallas_call(
        body, name=name,
        out_shape=(pltpu.SemaphoreType.DMA((7 * n,)), pltpu.SemaphoreType.DMA((7 * n,)),
                   *[pltpu.HBM(a.shape, a.dtype) for a in arrays], *[pltpu.HBM(l.shape, l.dtype) for l in lands],
                   jax.ShapeDtypeStruct((8, 128), F32)),
        in_specs=[hbm] * (2 * n),
        out_specs=(sem, sem, *[hbm] * (2 * n), pl.BlockSpec(memory_space=pltpu.VMEM)),
        input_output_aliases={i: 2 + i for i in range(2 * n)},
        compiler_params=pltpu.CompilerParams(has_side_effects=pltpu.SideEffectType.DATAFLOW_SIDE_EFFECTING),
    )(*[pltpu.with_memory_space_constraint(a, pltpu.HBM) for a in arrays],
      *[pltpu.with_memory_space_constraint(l, pltpu.HBM) for l in lands])
    return res[0], res[1], list(res[2:2 + n]), list(res[2 + n:2 + 2 * n]), res[-1]


def _exchange_wait(handles, scatter, after, name):
    send_sems, recv_sems, arrays, lands, _ = handles
    n = len(arrays)
    hbm = pl.BlockSpec(memory_space=pltpu.HBM)
    sem = pl.BlockSpec(memory_space=pltpu.SEMAPHORE)

    def body(*refs):
        x_refs, land_refs = refs[:n], refs[n:2 * n]
        send_s, recv_s = refs[2 * n], refs[2 * n + 1]
        starts, arrivals = _exchange_copies(x_refs, land_refs, send_s, recv_s, scatter)
        for cp in arrivals:
            cp.wait_recv()
        for cp in starts:
            cp.wait_send()

    res = pl.pallas_call(
        body, name=name,
        out_shape=(*[pltpu.HBM(a.shape, a.dtype) for a in arrays], *[pltpu.HBM(l.shape, l.dtype) for l in lands]),
        in_specs=[hbm] * (2 * n) + [sem, sem, pl.BlockSpec(memory_space=pl.ANY)],
        out_specs=tuple([hbm] * (2 * n)),
        input_output_aliases={i: i for i in range(2 * n)},
        compiler_params=pltpu.CompilerParams(has_side_effects=pltpu.SideEffectType.DATAFLOW_SIDE_EFFECTING),
    )(*arrays, *lands, send_sems, recv_sems, after)
    me = 4 * lax.axis_index("x") + 2 * lax.axis_index("y") + lax.axis_index("c")
    out = []
    for src, got in zip(res[:n], res[n:]):
        zeros = (0,) * (got.ndim - 1)
        own = lax.dynamic_slice(src, (me,) + zeros, (1,) + src.shape[1:]) if scatter else src[None]
        out.append(lax.dynamic_update_slice(got, own, (me,) + zeros))
    return out


def _pad_lanes(v, at=0, width=128):
    return jnp.pad(v, ((0, 0), (at, width - at - v.shape[1])))


def _pack_weights(P):
    W = {}
    w = P["w_in"]
    W["wp"] = jnp.concatenate([w[:, :2048], w[:, 2440:2696], w[:, 2056:2440], w[:, 2696:2760], w[:, 2048:2056],
                               jnp.zeros((D_MODEL, N_IN_PACKED - N_IN), w.dtype)], axis=1).astype(BF16)
    W["conv_w"] = P["gdn_conv_w"].astype(F32)
    W["alog_p"] = _pad_lanes(P["gdn_a_log"], 64)
    W["dt_p"] = _pad_lanes(P["gdn_dt_bias"], 64)
    W["gnw"] = P["gdn_norm_w"]
    W["qnw"] = P["mla_q_norm_w"]
    W["kvnw"] = P["mla_kv_norm_w"]
    uq = P["mla_w_uq"].reshape(Q_LORA, HEADS, HEAD_DIM + ROPE)
    W["wuq"] = jnp.pad(uq, ((0, 0), (0, 0), (0, 256 - HEAD_DIM - ROPE))).reshape(Q_LORA, HEADS * 256).astype(BF16)
    ukv = P["mla_w_ukv"].reshape(KV_LORA, HEADS, 2, HEAD_DIM)
    W["wukv"] = ukv.transpose(0, 2, 1, 3).reshape(KV_LORA, 2 * HEADS * HEAD_DIM).astype(BF16)
    W["qn_w"] = P["qkn_q_nope"]
    W["qr_w"] = _pad_lanes(P["qkn_q_rope"])
    W["kn_w"] = P["qkn_k_nope"]
    W["kr_w"] = _pad_lanes(P["qkn_k_rope"])
    W["onw"] = P["mla_out_norm_w"]
    W["wout"] = P["w_out"].astype(BF16)
    return W


def _unpack_grads(G):
    g_qkv, g_z, g_ckv, g_cq, g_kab = G["wp"]
    uq = G["wuq"].reshape(Q_LORA, HEADS, 256)[:, :, :HEAD_DIM + ROPE].reshape(Q_LORA, HEADS * (HEAD_DIM + ROPE))
    ukv = G["wukv"].reshape(KV_LORA, 2, HEADS, HEAD_DIM).transpose(0, 2, 1, 3).reshape(KV_LORA, 2 * HEADS * HEAD_DIM)
    return {
        "w_in": jnp.concatenate([g_qkv, g_z, g_kab[:, ROPE:ROPE + 8], g_cq, g_ckv, g_kab[:, :ROPE]], axis=1),
        "gdn_conv_w": G["conv_w"], "gdn_a_log": G["alog_p"][:, 64:68], "gdn_dt_bias": G["dt_p"][:, 64:68],
        "gdn_norm_w": G["gnw"], "mla_q_norm_w": G["qnw"], "mla_w_uq": uq, "mla_kv_norm_w": G["kvnw"], "mla_w_ukv": ukv,
        "qkn_q_nope": G["qn_w"], "qkn_q_rope": G["qr_w"][:, :ROPE], "qkn_k_nope": G["kn_w"], "qkn_k_rope": G["kr_w"][:, :ROPE],
        "mla_out_norm_w": G["onw"], "w_out": G["wout"],
    }


def _rope_tables(positions):
    half = ROPE // 2
    inv_freq = ROPE_BASE ** (-jnp.arange(half, dtype=F32) / half)
    ang = positions.astype(F32)[:, None] * inv_freq
    cos, sin = jnp.cos(ang), jnp.sin(ang)
    zeros = jnp.zeros((positions.shape[0], 128 - ROPE), F32)
    return jnp.concatenate([cos, cos, zeros], axis=1), jnp.concatenate([-sin, sin, zeros], axis=1)


def _ffn_forward(x, scale, shift, gate_w, w8, wo4, name, target=None):
    S = x.shape[0]
    tm = _pick(S, (512, 256, 128))
    n = S // tm
    with_loss = target is not None

    def body(x_ref, sc_ref, sh_ref, g_ref, wg_ref, wu_ref, wo_ref, *rest):
        t_ref = rest[0] if with_loss else None
        h_ref, bg_ref, bu_ref, ht_ref = rest[with_loss:with_loss + 4]
        tail = rest[with_loss + 4:]
        h_scr, acc_ref = tail[-2:]
        i, p = pl.program_id(0), pl.program_id(1)

        @pl.when(p == 0)
        def _():
            h_new = _modulate(x_ref[...], sc_ref[...], sh_ref[...]).astype(BF16)
            h_scr[...] = h_new
            h_ref[...] = h_new
            acc_ref[...] = jnp.zeros_like(acc_ref)

        h = h_scr[...]
        gate = _dot_raw(h, wg_ref[...], "nn")
        up = _dot_raw(h, wu_ref[...], "nn")
        sg = _sigmoid(gate)
        act = gate * sg
        hid = act * up
        bg_ref[...] = (up * (sg * (1.0 + gate * (1.0 - sg)))).astype(BF16)
        bu_ref[...] = act.astype(BF16)
        ht_ref[...] = jnp.transpose(hid).astype(BF16)
        acc_ref[...] += _dot_raw(hid, wo_ref[...], "nn")

        if with_loss:
            dx_ref, df_ref, dg_ref, l_ref = tail[:4]

            @pl.when((p == 0) & (i == 0))
            def _():
                dg_ref[...] = jnp.zeros_like(dg_ref)
                l_ref[...] = jnp.zeros_like(l_ref)

            @pl.when(p == HID_PIECES - 1)
            def _():
                step = min(EPILOGUE_ROWS, tm)
                for r in range(tm // step):
                    rows = slice(step * r, step * (r + 1))
                    f = acc_ref[rows, :]
                    diff = x_ref[rows, :] + 0.5 * g_ref[...] * f - t_ref[rows, :]
                    dx = diff * (1.0 / D_MODEL)
                    dx_ref[rows, :] = dx
                    df_ref[rows, :] = (0.5 * g_ref[...] * dx).astype(df_ref.dtype)
                    dg_ref[...] += jnp.sum(0.5 * f * dx, axis=0, keepdims=True)
                    l_ref[...] += jnp.sum(diff * diff, axis=0, keepdims=True)

            @pl.when((p == HID_PIECES - 1) & (i == n - 1))
            def _():
                l_ref[...] = jnp.full(l_ref.shape, (0.5 / D_MODEL) * jnp.sum(l_ref[...]), F32)
        else:
            f_ref, xo_ref = tail[:2]

            @pl.when(p == HID_PIECES - 1)
            def _():
                f = acc_ref[...]
                f_ref[...] = f.astype(f_ref.dtype)
                xo_ref[...] = x_ref[...] + 0.5 * g_ref[...] * f

    row = pl.BlockSpec((tm, D_MODEL), lambda i, p: (i, 0))
    par = pl.BlockSpec((1, D_MODEL), lambda i, p: (0, 0))
    piece = pl.BlockSpec((None, tm, FFN_PIECE), lambda i, p: (p, i, 0))
    row_f32, row_bf16 = jax.ShapeDtypeStruct((S, D_MODEL), F32), jax.ShapeDtypeStruct((S, D_MODEL), BF16)
    par_f32 = jax.ShapeDtypeStruct((1, D_MODEL), F32)
    piece_shape = jax.ShapeDtypeStruct((HID_PIECES, S, FFN_PIECE), BF16)
    return pl.pallas_call(
        body, name=name, grid=(n, HID_PIECES),
        in_specs=[row, par, par, par,
                  pl.BlockSpec((None, D_MODEL, FFN_PIECE), lambda i, p: (p, 0, 0)),
                  pl.BlockSpec((None, D_MODEL, FFN_PIECE), lambda i, p: (p + HID_PIECES, 0, 0)),
                  pl.BlockSpec((None, FFN_PIECE, D_MODEL), lambda i, p: (p, 0, 0))] + [row] * with_loss,
        out_specs=[row, piece, piece, pl.BlockSpec((None, FFN_PIECE, tm), lambda i, p: (p, 0, i))]
        + ([row, row, par, par] if with_loss else [row, row]),
        out_shape=[row_bf16, piece_shape, piece_shape, jax.ShapeDtypeStruct((HID_PIECES, FFN_PIECE, S), BF16)]
        + ([row_f32, row_bf16, par_f32, par_f32] if with_loss else [row_bf16, row_f32]),
        scratch_shapes=[pltpu.VMEM((tm, D_MODEL), BF16), pltpu.VMEM((tm, D_MODEL), F32)],
        compiler_params=_params(("arbitrary", "arbitrary")),
    )(x, scale, shift, gate_w, w8, w8, wo4, *([target] if with_loss else []))


def _ffn_fwd(x, scale, shift, gate_w, w8, wo4, tag, target=None):
    res = _ffn_forward(x, scale, shift, gate_w, w8, wo4, tag + "_fwd", target)
    h, by_gate, by_up, hid_t = res[:4]
    if target is not None:
        dx_out, df, d_gate_w, loss_row = res[4:]
        return (dx_out, loss_row), (h, by_gate, by_up, hid_t, None, df, d_gate_w)
    f, x_out = res[4:]
    return x_out, (h, by_gate, by_up, hid_t, f, None, None)


def _ffn_bwd(d_out, x, scale, shift, gate_w, w8, wo4, saved, tag, grad_ready, below=None):
    h, gate, up, hid_t, f, df, d_gate_w = saved
    S = x.shape[0]
    tm = _pick(S, (512, 256, 128))
    tk = _pick(S, (512, 256, 128))
    n = S // tm
    if df is None:
        (df,), (d_gate_w,) = _rowwise_bwd(lambda f_, g_: (0.5 * g_ * f_,), [(f, tm, D_MODEL, 0)], [], [gate_w],
                                          [(d_out, tm, D_MODEL, 0)], n, tag + "_dres", row_dtypes=(BF16,))
    tb = _pick(S, MATMUL_ROWS)
    piece = pl.BlockSpec((None, tb, FFN_PIECE), lambda i, j, k: (j, i, 0))
    d_gate, d_up = _mmg(df, wo4, "nt", name=tag + "_ddown", grid=(S // tb, HID_PIECES, 1),
                        a_spec=pl.BlockSpec((tb, D_MODEL), lambda i, j, k: (i, 0)),
                        b_spec=pl.BlockSpec((None, FFN_PIECE, D_MODEL), lambda i, j, k: (j, 0, 0)),
                        out_spec=piece, out_shapes=[jax.ShapeDtypeStruct((HID_PIECES, S, FFN_PIECE), BF16)] * 2,
                        acc_shape=(tb, FFN_PIECE), extras=[gate, up], extra_specs=[piece, piece], epi=_swiglu_bwd)
    tk = _pick(S, MATMUL_ROWS)
    g_wo4 = _mmg(hid_t, df, "nn", name=tag + "_gwo", grid=(HID_PIECES, 1, S // tk),
                 a_spec=pl.BlockSpec((None, FFN_PIECE, tk), lambda i, j, k: (i, 0, k)),
                 b_spec=pl.BlockSpec((tk, D_MODEL), lambda i, j, k: (k, j)),
                 out_spec=pl.BlockSpec((None, FFN_PIECE, D_MODEL), lambda i, j, k: (i, 0, j)),
                 out_shapes=[jax.ShapeDtypeStruct((HID_PIECES, FFN_PIECE, D_MODEL), BF16)], acc_shape=(FFN_PIECE, D_MODEL))
    g_w8 = _ffn_gw8(h, d_gate, d_up, tag + "_gw8", after=grad_ready("wo4", g_wo4))
    res = _ffn_dh(d_gate, d_up, w8, x, d_out, scale, shift, tag + "_dh", after=grad_ready("w8", g_w8), below=below)
    return (res[0], res[1], res[2], d_gate_w) + tuple(res[3:])


def _dproj_dmod(d_pieces, wp, x, d_out, scale, shift, name, below=None):
    S = x.shape[0]
    tm = _pick(S, TOKEN_ROWS)
    widths = [p.shape[1] for p in d_pieces]
    starts = [sum(widths[:n]) for n in range(len(widths))]
    n_p = len(d_pieces)
    n_below = 0 if below is None else 2

    def body(*refs):
        dp_refs, (w_ref, x_ref, do_ref, sc_ref, sh_ref), rest = refs[:n_p], refs[n_p:n_p + 5], refs[n_p + 5:]
        below_in = rest[:n_below]
        dx_ref, dsc_ref, dsh_ref = rest[n_below:n_below + 3]
        below_out, acc_ref = rest[n_below + 3:n_below + 3 + n_below], rest[-1]

        @pl.when(pl.program_id(0) == 0)
        def _():
            for r in (dsc_ref, dsh_ref) + tuple(below_out[1:]):
                r[...] = jnp.zeros_like(r)

        acc = None
        for dp_ref, at, width in zip(dp_refs, starts, widths):
            part = _dot_raw(dp_ref[...], w_ref[:, at:at + width], "nt")
            acc = part if acc is None else acc + part
        acc_ref[...] = acc
        _dmod_epilogue(acc_ref, x_ref, do_ref, sc_ref, sh_ref, dx_ref, dsc_ref, dsh_ref, below, below_in, below_out)

    row = pl.BlockSpec((tm, D_MODEL), lambda i: (i, 0))
    par = pl.BlockSpec((1, D_MODEL), lambda i: (0, 0))
    row_shape, par_shape = jax.ShapeDtypeStruct((S, D_MODEL), F32), jax.ShapeDtypeStruct((1, D_MODEL), F32)
    return pl.pallas_call(
        body, name=name, grid=(S // tm,),
        in_specs=[pl.BlockSpec((tm, width), lambda i: (i, 0)) for width in widths]
        + [pl.BlockSpec(wp.shape, lambda i: (0, 0)), row, row, par, par] + [row, par][:n_below],
        out_specs=[row, par, par] + [row, par][:n_below],
        out_shape=[row_shape, par_shape, par_shape] + [jax.ShapeDtypeStruct((S, D_MODEL), BF16), par_shape][:n_below],
        scratch_shapes=[pltpu.VMEM((tm, D_MODEL), F32)],
        compiler_params=_params(("arbitrary",)),
    )(*d_pieces, wp, x, d_out, scale, shift, *(below[:2] if below is not None else ()))


def _gw_pieces(h, d_pieces, name):
    S = h.shape[0]
    tm = D_MODEL
    tk = _pick(S, MATMUL_ROWS)
    n_p = len(d_pieces)
    widths = [p.shape[1] for p in d_pieces]

    def body(h_ref, *rest):
        d_refs, o_refs = rest[:n_p], rest[n_p:]

        @pl.when(pl.program_id(1) == 0)
        def _():
            for o_ref in o_refs:
                o_ref[...] = jnp.zeros_like(o_ref)

        h_t = jnp.transpose(h_ref[...])
        for d_ref, o_ref in zip(d_refs, o_refs):
            o_ref[...] += _dot_raw(h_t, d_ref[...], "nn")

    return pl.pallas_call(
        body, name=name, grid=(D_MODEL // tm, S // tk),
        in_specs=[pl.BlockSpec((tk, tm), lambda i, k: (k, i))] + [pl.BlockSpec((tk, width), lambda i, k: (k, 0)) for width in widths],
        out_specs=[pl.BlockSpec((tm, width), lambda i, k: (i, 0)) for width in widths],
        out_shape=[jax.ShapeDtypeStruct((D_MODEL, width), F32) for width in widths],
        compiler_params=_params(("parallel", "arbitrary")),
    )(h, *d_pieces)


def _mod_proj(x, scale, shift, wp, name):
    S = x.shape[0]
    N = wp.shape[1]
    tm = _pick(S, MATMUL_ROWS)
    tn = _pick(N, (1408, 1024, 512, 256, 128))

    def body(x_ref, sc_ref, sh_ref, w_ref, h_ref, o_ref, h_scr):
        @pl.when(pl.program_id(1) == 0)
        def _():
            h_new = _modulate(x_ref[...], sc_ref[...], sh_ref[...]).astype(BF16)
            h_scr[...] = h_new
            h_ref[...] = h_new

        o_ref[...] = _dot_raw(h_scr[...], w_ref[...], "nn")

    row = pl.BlockSpec((tm, D_MODEL), lambda i, j: (i, 0))
    par = pl.BlockSpec((1, D_MODEL), lambda i, j: (0, 0))
    return pl.pallas_call(
        body, name=name, grid=(S // tm, N // tn),
        in_specs=[row, par, par, pl.BlockSpec((D_MODEL, tn), lambda i, j: (0, j))],
        out_specs=[row, pl.BlockSpec((tm, tn), lambda i, j: (i, j))],
        out_shape=[jax.ShapeDtypeStruct((S, D_MODEL), BF16), jax.ShapeDtypeStruct((S, N), F32)],
        scratch_shapes=[pltpu.VMEM((tm, D_MODEL), BF16)],
        compiler_params=_params(("parallel", "arbitrary")),
    )(x, scale, shift, wp)


def _mix_out(o_a, proj, o_b, gnw, onw, wout, x, gate_w, t, name):
    S = x.shape[0]

    def body(oa_ref, z_ref, ob_ref, gn_ref, on_ref, w_ref, x_ref, g_ref, mixed_ref, y_ref, xo_ref):
        (mixed,) = _mix_post_fn(oa_ref[...], z_ref[...], ob_ref[...], gn_ref[...], on_ref[...])
        mixed = mixed.astype(BF16)
        mixed_ref[...] = mixed
        y = _dot_raw(mixed, w_ref[...], "nn")
        y_ref[...] = y.astype(BF16)
        xo_ref[...] = x_ref[...] + g_ref[...] * y

    half, row = _row_spec(t, 512, 0), _row_spec(t, D_MODEL, 0)
    return pl.pallas_call(
        body, name=name, grid=(S // t,),
        in_specs=[half, _row_spec(t, 512, 3), half, _full_spec(gnw.shape), _full_spec(onw.shape), _full_spec(wout.shape),
                  row, _full_spec(gate_w.shape)],
        out_specs=[row, row, row],
        out_shape=[jax.ShapeDtypeStruct((S, D_MODEL), BF16)] * 2 + [jax.ShapeDtypeStruct((S, D_MODEL), F32)],
        compiler_params=_params(("parallel",)),
    )(o_a, proj, o_b, gnw, onw, wout, x, gate_w)


def _mixer_fwd(x1, scale, shift, gate_w, cos_p, sin_p, W):
    S = x1.shape[0]
    tm = _pick(S, (512, 256, 128))
    tv = _pick(S, TOKEN_ROWS)
    ta = _pick(S, (512, 256, 128))
    nc = S // CHUNK
    h2, proj = _mod_proj(x1, scale, shift, W["wp"], "mix_proj")
    qkvc, q_a, k_a, v_a, gb = _conv_fwd(proj, W["conv_w"], W["alog_p"], W["dt_p"], tv, "gdn_conv")
    kab = (proj, tv, 128, 21)
    ti = _pick(S, INTRA_ROWS)
    intra = _rowwise(_gdn_intra_fn, [(q_a, ti, 512, 0), (k_a, ti, 512, 0), (v_a, ti, 512, 0), (gb, ti, 128, 0)],
                     [], [(ti, 512, F32)] * 4 + [(ti, CHUNK, F32)] * 4 + [(ti // 8, 512, F32)] + [(ti, CHUNK, F32)] * 4,
                     S // ti, "gdn_intra")
    u, wk, qd, kd, qks, gl, invs = intra[0], intra[1], intra[2], intra[3], tuple(intra[4:8]), intra[8], tuple(intra[9:])
    o_a, s_prev = _gdn_scan_fwd(u, wk, qd, kd, qks, gl, "gdn_scan")
    mla_params = [W["qnw"], W["kvnw"], W["wuq"], W["wukv"], W["qn_w"], W["qr_w"], W["kn_w"], W["kr_w"]]
    def mla_pre_with_vt(*a):
        q_, k_, v_ = _mla_pre_fn(*a)
        return q_, k_, v_, jnp.transpose(v_)

    q_b, k_b, v_b, vt_b = _rowwise(mla_pre_with_vt,
                                   [(proj, tv, 256, 8), (proj, tv, 384, 6), kab, (cos_p, tv, 128, 0), (sin_p, tv, 128, 0)],
                                   mla_params, [(tv, 1024, BF16), (tv, 1024, BF16), (tv, 512, BF16), (512, tv, BF16, "across")],
                                   S // tv, "mla_pre")
    o_b, lse = _attn_fwd(q_b, k_b, vt_b, ta, "mla_attn")
    mixed, y, x2 = _mix_out(o_a, proj, o_b, W["gnw"], W["onw"], W["wout"], x1, gate_w, tv, "mix_out")
    saved = (h2, proj, qkvc, q_a, k_a, v_a, gb, u, wk, qd, kd, qks, gl, invs, s_prev, o_a, q_b, k_b, v_b, o_b, lse, mixed, y)
    return x2, saved


def _mixer_bwd(d_out, dy, x1, scale, shift, cos_p, sin_p, W, saved, below):
    (h2, proj, qkvc, q_a, k_a, v_a, gb, u, wk, qd, kd, qks, gl, invs, s_prev, o_a, q_b, k_b, v_b, o_b, lse, mixed, y) = saved
    S = x1.shape[0]
    tm = _pick(S, (512, 256, 128))
    tv = _pick(S, TOKEN_ROWS)
    ta = _pick(S, (512, 256, 128))
    nc = S // CHUNK
    G = {}
    G["wout"] = _mm(mixed, dy, "tn", name="mix_gwout")
    (do_a, dz, do_b), (G["gnw"], G["onw"], stats) = _rowwise_bwd(
        _mix_post_fn, [(o_a, tv, 512, 0), (proj, tv, 512, 3), (o_b, tv, 512, 0)], [], [W["gnw"], W["onw"]],
        [], S // tv, "mix_dpost", row_dtypes=(F32, BF16, F32), dout_from=(dy, W["wout"]),
        across=(lse, lambda r_vals, d_rows, lse_tile: _attn_stats(r_vals[2], d_rows[2], lse_tile)))
    dq_b, dk_b, dv_b = _attn_bwd(q_b, k_b, v_b, do_b, stats, ta, "mla_dattn")
    kab = (proj, tv, 128, 21)
    mla_params = [W["qnw"], W["kvnw"], W["wuq"], W["wukv"], W["qn_w"], W["qr_w"], W["kn_w"], W["kr_w"]]
    (d_ckv, d_cq, d_kab), mla_grads = _rowwise_bwd(
        _mla_pre_fn, [(proj, tv, 256, 8), (proj, tv, 384, 6), kab], [(cos_p, tv, 128, 0), (sin_p, tv, 128, 0)], mla_params,
        [(dq_b, tv, 1024, 0), (dk_b, tv, 1024, 0), (dv_b, tv, 512, 0)], S // tv, "mla_dpre", row_dtypes=(BF16, BF16, F32))
    for key, g in zip(("qnw", "kvnw", "wuq", "wukv", "qn_w", "qr_w", "kn_w", "kr_w"), mla_grads):
        G[key] = g
    scan_grads = _gdn_scan_bwd(u, wk, qd, kd, qks, gl, s_prev, do_a, "gdn_dscan")
    ti = _pick(S, INTRA_ROWS)
    intra_douts = [(scan_grads[i], ti, 512, 0) for i in range(4)] + [(scan_grads[4 + i], ti, CHUNK, 0) for i in range(4)]
    intra_douts.append((scan_grads[8], ti // 8, 512, 0))
    (dq_a, dk_a, dv_a, d_gb), _ = _rowwise_bwd(
        _gdn_intra_fn, [(q_a, ti, 512, 0), (k_a, ti, 512, 0), (v_a, ti, 512, 0), (gb, ti, 128, 0)],
        [(x_, ti, CHUNK, 0) for x_ in invs], [], intra_douts, S // ti, "gdn_dintra")
    d_qkv, d_kab, g_conv, G["alog_p"], G["dt_p"] = _conv_bwd(proj, qkvc, W["conv_w"], W["alog_p"], W["dt_p"],
                                                              (dq_a, dk_a, dv_a, d_gb), d_kab, tv, "gdn_dconv")
    G["conv_w"] = g_conv[:4]
    d_proj = [d_qkv, dz, d_ckv, d_cq, d_kab]
    G["wp"] = _gw_pieces(h2, d_proj, "mix_gwp")
    dx1, G["s2"], G["sh2"], d_below, dg_below = _dproj_dmod(d_proj, W["wp"], x1, d_out, scale, shift, "mix_dproj", below=below)
    return dx1, d_below, dg_below, G


def _local_step(x, target, mod, cos_p, sin_p, W1, mixer_weights, ffn2_weights, ffn_grad_ready, mixer_grads_ready):
    sh1, s1, g1, sh2, s2, g2, sh3, s3, g3 = [mod[:, D_MODEL * i:D_MODEL * (i + 1)] for i in range(N_MOD)]
    x1, saved1 = _ffn_fwd(x, s1, sh1, g1, W1["f1_w8"], W1["f1_wo4"], "ffn1")
    W = mixer_weights(x1)
    x2, saved2 = _mixer_fwd(x1, s2, sh2, g2, cos_p, sin_p, W)
    W.update(ffn2_weights(x2))
    (dx3, loss_row), saved3 = _ffn_fwd(x2, s3, sh3, g3, W["f2_w8"], W["f2_wo4"], "ffn2", target=target)
    dx2, d_s3, d_sh3, d_g3, dy, d_g2 = _ffn_bwd(dx3, x2, s3, sh3, g3, W["f2_w8"], W["f2_wo4"], saved3, "ffn2",
                                                ffn_grad_ready("f2"), below=(saved2[-1], g2, 1.0))
    dx1, df1, d_g1, G = _mixer_bwd(dx2, dy, x1, s2, sh2, cos_p, sin_p, W, saved2, below=(saved1[4], g1, 0.5))
    d_sh2, d_s2 = G.pop("sh2"), G.pop("s2")
    saved1 = saved1[:5] + (df1, d_g1 + mixer_grads_ready(G))
    dx, d_s1, d_sh1, d_g1 = _ffn_bwd(dx1, x, s1, sh1, g1, W1["f1_w8"], W1["f1_wo4"], saved1, "ffn1", ffn_grad_ready("f1"))
    d_mod = jnp.concatenate([d_sh1, d_s1, d_g1, d_sh2, d_s2, d_g2, d_sh3, d_s3, d_g3], axis=1)
    return loss_row, dx, d_mod


WEIGHT_NAMES = ("w_ada", "b_ada", "ffn1_w_in", "ffn1_w_out", "w_in", "gdn_conv_w", "gdn_a_log", "gdn_dt_bias", "gdn_norm_w",
                "mla_q_norm_w", "mla_w_uq", "mla_kv_norm_w", "mla_w_ukv", "qkn_q_nope", "qkn_q_rope", "qkn_k_nope",
                "qkn_k_rope", "mla_out_norm_w", "w_out", "ffn2_w_in", "ffn2_w_out")
FFN_SHARDED = ("ffn1_w_in", "ffn1_w_out", "ffn2_w_in", "ffn2_w_out")
TRANSPOSED_ENTRY = ("ffn1_w_in", "ffn2_w_in", "w_in", "mla_w_uq")
SHEETED = (("w_in", "col"), ("gdn_conv_w", "col"), ("mla_w_uq", "col"), ("mla_w_ukv", "col"), ("w_out", "row"))
MOD_ROWS = N_MOD * D_MODEL // 128
SMALL = {"gdn_a_log": (MOD_ROWS, 1, 64, 4), "gdn_dt_bias": (MOD_ROWS + 1, 1, 64, 4), "gdn_norm_w": (MOD_ROWS + 2, 1, 0, 128),
         "mla_q_norm_w": (MOD_ROWS + 3, 3, 0, 384), "mla_kv_norm_w": (MOD_ROWS + 6, 2, 0, 256),
         "qkn_q_nope": (MOD_ROWS + 8, 1, 0, 128), "qkn_q_rope": (MOD_ROWS + 9, 1, 0, 64), "qkn_k_nope": (MOD_ROWS + 10, 1, 0, 128),
         "qkn_k_rope": (MOD_ROWS + 11, 1, 0, 64), "mla_out_norm_w": (MOD_ROWS + 12, 1, 0, 128)}
LOSS_ROW = MOD_ROWS + 13
CONV_ROW, CONV_ROWS = 88, 4 * 1536 // 128
SHEET_ROWS = CONV_ROW + CONV_ROWS


def _to_sheet(flat, dtype, sublanes):
    n = flat.shape[-1]
    unit = sublanes * 128
    pad = (-n) % unit
    flat = jnp.pad(flat.astype(dtype), [(0, 0)] * (flat.ndim - 1) + [(0, pad)])
    return flat.reshape(flat.shape[:-1] + ((n + pad) // 128, 128))


def _small_sheet(b_like, small):
    sheet = jnp.zeros((SHEET_ROWS, 128), F32).at[:MOD_ROWS].set(b_like.reshape(MOD_ROWS, 128))
    for name, (row, rows, lane, n) in SMALL.items():
        v = small[name].reshape(1, n)
        if rows == 1:
            sheet = sheet.at[row, lane:lane + n].set(v[0])
        else:
            sheet = sheet.at[row:row + rows].set(v.reshape(rows, 128))
    return sheet


def _from_small_sheet(sheet):
    out = {"b_ada": sheet[:MOD_ROWS].reshape(1, N_MOD * D_MODEL)}
    for name, (row, rows, lane, n) in SMALL.items():
        out[name] = sheet[row, lane:lane + n].reshape(1, n) if rows == 1 else sheet[row:row + rows].reshape(1, n)
    return out


def kernel(x, c, positions, w_ada, b_ada, ffn1_w_in, ffn1_w_out, w_in, gdn_conv_w, gdn_a_log, gdn_dt_bias, gdn_norm_w, mla_q_norm_w, mla_w_uq, mla_kv_norm_w, mla_w_ukv, qkn_q_nope, qkn_q_rope, qkn_k_nope, qkn_k_rope, mla_out_norm_w, w_out, ffn2_w_in, ffn2_w_out, loss_target, m_w_ada, m_b_ada, m_ffn1_w_in, m_ffn1_w_out, m_w_in, m_gdn_conv_w, m_gdn_a_log, m_gdn_dt_bias, m_gdn_norm_w, m_mla_q_norm_w, m_mla_w_uq, m_mla_kv_norm_w, m_mla_w_ukv, m_qkn_q_nope, m_qkn_q_rope, m_qkn_k_nope, m_qkn_k_rope, m_mla_out_norm_w, m_w_out, m_ffn2_w_in, m_ffn2_w_out, v_w_ada, v_b_ada, v_ffn1_w_in, v_ffn1_w_out, v_w_in, v_gdn_conv_w, v_gdn_a_log, v_gdn_dt_bias, v_gdn_norm_w, v_mla_q_norm_w, v_mla_w_uq, v_mla_kv_norm_w, v_mla_w_ukv, v_qkn_q_nope, v_qkn_q_rope, v_qkn_k_nope, v_qkn_k_rope, v_mla_out_norm_w, v_w_out, v_ffn2_w_in, v_ffn2_w_out):
    args = locals()
    w = {n: args[n] for n in WEIGHT_NAMES}
    m = {n: args["m_" + n] for n in WEIGHT_NAMES}
    v = {n: args["v_" + n] for n in WEIGHT_NAMES}
    me = 4 * lax.axis_index("x") + 2 * lax.axis_index("y") + lax.axis_index("c")
    cols = N_MOD * D_MODEL // N_DEV
    shard = {n: w[n][0] for n in FFN_SHARDED + tuple(s[0] for s in SHEETED)}

    sc = c * _sigmoid(c)
    first = _to_sheet(jnp.concatenate([sc.reshape(-1), shard["gdn_conv_w"].reshape(-1)]), F32, 8)
    (first_all,) = _all_gather([first], "gather_c")
    sc_all = first_all[:, :D_MODEL // 128].reshape(N_DEV, D_MODEL)
    n_taps = shard["gdn_conv_w"].size
    conv_all = first_all.reshape(N_DEV, -1)[:, D_MODEL:D_MODEL + n_taps].reshape(N_DEV, 4, -1)
    b_mine = lax.dynamic_slice(b_ada, (0, me * cols), (1, cols))
    mod_cols = _mm(sc_all, w_ada[0], "nn", name="ada_mod", extra_params=[b_mine], epi=lambda acc, b_: (acc + b_,))
    (mod_all,) = _all_to_all([_to_sheet(mod_cols, F32, 8)], "scatter_mod")
    mod = mod_all.reshape(N_DEV, -1)[:, :cols].reshape(1, N_MOD * D_MODEL)

    f1_shards, mod = lax.optimization_barrier(([shard["ffn1_w_in"].astype(BF16), shard["ffn1_w_out"].astype(BF16)], mod))
    f1_w8, f1_out = _all_gather(f1_shards, "gather_w1")
    travel = [s for s in SHEETED if s[0] != "gdn_conv_w"]
    tied = lax.optimization_barrier(([shard[n].astype(BF16) for n, _ in travel], f1_w8))
    f1_w8 = tied[1]
    mixer_w = _exchange_start(tied[0], False, "gather_wm_start")
    ffn2_w = _exchange_start([shard["ffn2_w_in"].astype(BF16) + mixer_w[4][0:1, 0:1].astype(BF16),
                              shard["ffn2_w_out"].astype(BF16)], False, "gather_w2_start")
    mod = mod + ffn2_w[4][0:1, 0:1]
    W1 = dict(f1_w8=f1_w8, f1_wo4=f1_out.reshape(HID_PIECES, FFN_PIECE, D_MODEL))

    def mixer_weights(after):
        got = _exchange_wait(mixer_w, False, after, "gather_wm_wait")
        P = {n: jnp.concatenate(list(g), axis=1) if kind == "col" else g.reshape(-1, g.shape[-1])
             for (n, kind), g in zip(travel, got)}
        P["gdn_conv_w"] = jnp.concatenate(list(conv_all), axis=1)
        for n in SMALL:
            P[n] = w[n]
        return _pack_weights(P)

    def ffn2_weights(after):
        f2_w8, f2_out = _exchange_wait(ffn2_w, False, after, "gather_w2_wait")
        return dict(f2_w8=f2_w8, f2_wo4=f2_out.reshape(HID_PIECES, FFN_PIECE, D_MODEL))

    pending, small_grads = {}, {}

    def ffn_grad_ready(tag):
        def ready(which, g):
            pieces = g if which == "w8" else g.reshape((N_DEV,) + shard["ffn1_w_out"].shape)
            pending[tag + which] = _exchange_start([pieces], True, "scatter_%s_%s_start" % (tag, which))
            return pending[tag + which][4]
        return ready

    def mixer_grads_ready(G):
        g_full = _unpack_grads(G)
        small_grads.update({n: g_full[n] for n in SMALL})
        small_grads["gdn_conv_w"] = g_full["gdn_conv_w"]
        pieces = []
        for n, kind in travel:
            r, cc = shard[n].shape
            g = g_full[n].astype(BF16)
            pieces.append(jnp.stack([g[:, cc * p:cc * (p + 1)] for p in range(N_DEV)]) if kind == "col"
                          else g.reshape(N_DEV, r, cc))
        pending["mixer"] = _exchange_start(pieces, True, "scatter_mx_start")
        return pending["mixer"][4][0:1, 0:1]

    cos_p, sin_p = _rope_tables(positions[0])
    loss_row, dx, d_mod = _local_step(x[0], loss_target[0], mod, cos_p, sin_p, W1, mixer_weights, ffn2_weights,
                                      ffn_grad_ready, mixer_grads_ready)

    sheet = _small_sheet(d_mod, small_grads).at[LOSS_ROW].set(loss_row[0, :128])
    sheet = sheet.at[CONV_ROW:CONV_ROW + CONV_ROWS].set(small_grads["gdn_conv_w"].reshape(CONV_ROWS, 128))
    (sheets,) = _all_gather([sheet], "gather_small")
    summed = _sum_devices(sheets, "sum_small")
    d_mod_all = sheets[:, :MOD_ROWS].reshape(N_DEV, N_MOD * D_MODEL)
    d_mod_mine = lax.dynamic_slice(d_mod_all, (0, me * cols), (N_DEV, cols))
    grads = _from_small_sheet(summed)
    grads["w_ada"] = _mm(sc_all, d_mod_mine, "tn", name="ada_gw", hi=True)
    conv_taps = shard["gdn_conv_w"].shape[1]
    grads["gdn_conv_w"] = lax.dynamic_slice(summed[CONV_ROW:CONV_ROW + CONV_ROWS].reshape(4, -1), (0, me * conv_taps),
                                            (4, conv_taps))
    loss = summed[LOSS_ROW, 0]

    delta, new_m, new_v = {}, {}, {}
    arrived = {}
    for n, key in zip(FFN_SHARDED, ("f1w8", "f1wo4", "f2w8", "f2wo4")):
        (arrived[n],) = _exchange_wait(pending[key], True, summed, "scatter_%s_wait" % key)
    arrived.update(zip([n for n, _ in travel], _exchange_wait(pending["mixer"], True, summed, "scatter_mx_wait")))
    for n, parts in arrived.items():
        if n in TRANSPOSED_ENTRY:
            res = _sum_adamw(parts, w[n][0].T, m[n][0].T, v[n][0].T, "adamw_" + n, transposed=True)
            grads[n], delta[n], new_m[n], new_v[n] = [r.T for r in res]
        else:
            grads[n], delta[n], new_m[n], new_v[n] = _sum_adamw(parts, w[n][0], m[n][0], v[n][0], "adamw_" + n)
    for n in ("w_ada", "gdn_conv_w"):
        delta[n], new_m[n], new_v[n] = _adamw(w[n][0], grads[n], m[n][0], v[n][0], "adamw_" + n)
    small_in = [_small_sheet(t["b_ada"], t) for t in (w, grads, m, v)]
    for res, out in zip(_adamw(*small_in, "adamw_small"), (delta, new_m, new_v)):
        out.update(_from_small_sheet(res))

    def shaped(d):
        return [d[n].reshape(w[n].shape) for n in WEIGHT_NAMES]

    return (loss, dx[None], *shaped(grads), *shaped(delta), *shaped(new_m), *shaped(new_v))
```
